```python
import jax, jax.numpy as jnp
from jax import lax
import numpy as np

D_MODEL = 1024
BATCH = 8
SEQ = 4096
DEPTH = 2

GRID_W = 64
CTX_LEN = 256
D_A = D_MODEL
SGU_GROUPS = 8
SGU_GROUP_DIM = D_A // SGU_GROUPS
SGU_CHUNK = 128
D_B = D_MODEL
HGRN_HEAD_DIM = 128
HGRN_HEADS = D_B // HGRN_HEAD_DIM
HGRN_CHUNK = 64
D_FF = 2816
CONV_W = 3
N_MOD = 6
RMS_EPS = 1e-6
LN_EPS = 1e-5
IN_SPLITS = (D_B, 2 * D_B, 3 * D_B, 4 * D_B, 4 * D_B + D_A, 4 * D_B + 2 * D_A, 5 * D_B + 2 * D_A, 5 * D_B + 2 * D_A + D_MODEL)
D_IN = 5 * D_B + 2 * D_A + 2 * D_MODEL

kernel_name = 'hybrid_sgu_hgrn2_convffn_prefix_dit'


def rms_norm(x, w):
    xf = x.astype(jnp.float32)
    y = xf * lax.rsqrt(jnp.mean(xf * xf, axis=-1, keepdims=True) + RMS_EPS)
    return (y * w.astype(jnp.float32)).astype(x.dtype)


def layer_norm(x, w, b):
    xf = x.astype(jnp.float32)
    mu = jnp.mean(xf, axis=-1, keepdims=True)
    var = jnp.mean(jnp.square(xf - mu), axis=-1, keepdims=True)
    y = (xf - mu) * lax.rsqrt(var + LN_EPS)
    return (y * w.astype(jnp.float32) + b.astype(jnp.float32)).astype(x.dtype)


def modulate(h, shift, scale):
    return h * (1 + scale) + shift


def to_heads(t):
    bsz, length, _ = t.shape
    return t.reshape(bsz, length, HGRN_HEADS, HGRN_HEAD_DIM).transpose(0, 2, 1, 3)


def hgrn_forget(f_logit, lb):
    z = f_logit.astype(jnp.float32)
    f = lb + (1 - lb) * jax.nn.sigmoid(z)
    return to_heads((1 - lb) * jax.nn.sigmoid(-z)), to_heads(jnp.log(f))


def gla_chunked(q, k, v, g, s0):
    bsz, heads, length, _ = q.shape
    n = length // HGRN_CHUNK
    split = lambda t: t.reshape(bsz, heads, n, HGRN_CHUNK, t.shape[-1])
    q, k, v, g = split(q), split(k), split(v), split(g)
    b = jnp.cumsum(g, axis=3)
    b_last = b[:, :, :, -1:, :]
    ref = b[:, :, :, HGRN_CHUNK // 2 - 1:HGRN_CHUNK // 2, :]
    scores = jnp.einsum('bhntk,bhnsk->bhnts', q * jnp.exp(b - ref), k * jnp.exp(ref - b))
    lower = jnp.tril(jnp.ones((HGRN_CHUNK, HGRN_CHUNK), dtype=bool))
    o_intra = jnp.einsum('bhnts,bhnsv->bhntv', jnp.where(lower, scores, 0.0), v)
    q_inter = q * jnp.exp(b)
    kv = jnp.einsum('bhnsk,bhnsv->bhnkv', k * jnp.exp(b_last - b), v)
    decay = jnp.exp(b_last[:, :, :, 0, :])

    def step(state, xs):
        q_n, kv_n, d_n = xs
        o_n = jnp.einsum('bhtk,bhkv->bhtv', q_n, state)
        return d_n[..., None] * state + kv_n, o_n

    move = lambda t: jnp.moveaxis(t, 2, 0)
    s_final, o_inter = lax.scan(step, s0, (move(q_inter), move(kv), move(decay)))
    o = o_intra + jnp.moveaxis(o_inter, 0, 2)
    return o.reshape(bsz, heads, length, -1), s_final


def hgrn_bidir(q, f_fwd, f_bwd, i, lb_fwd, lb_bwd, s0_fwd, s0_bwd):
    qh = to_heads(jax.nn.silu(q.astype(jnp.float32)))
    ih = to_heads(i.astype(jnp.float32))
    k_f, g_f = hgrn_forget(f_fwd, lb_fwd)
    k_b, g_b = hgrn_forget(f_bwd, lb_bwd)
    o_f, s_f = gla_chunked(qh, k_f, ih, g_f, s0_fwd)
    rev = lambda t: jnp.flip(t, axis=2)
    o_b, s_b = gla_chunked(rev(qh), rev(k_b), rev(ih), rev(g_b), s0_bwd)
    return o_f + rev(o_b), s_f, s_b


def hgrn_readout(o, og, norm_w):
    o = o * lax.rsqrt(jnp.mean(o * o, axis=-1, keepdims=True) + RMS_EPS) * norm_w.astype(jnp.float32)
    bsz, _, length, _ = o.shape
    o = o.transpose(0, 2, 1, 3).reshape(bsz, length, D_B).astype(og.dtype)
    return o * jax.nn.silu(og)


def sgu(u, v, ln_w, ln_b, w_s, b_s):
    bsz, length, _ = v.shape
    vn = layer_norm(v, ln_w, ln_b).reshape(bsz, length // SGU_CHUNK, SGU_CHUNK, SGU_GROUPS, SGU_GROUP_DIM)
    mixed = jnp.einsum('gts,bnsgd->bntgd', w_s, vn) + b_s.T[:, :, None]
    return u * mixed.reshape(bsz, length, D_A)


def token_mixer_out(parts, o_b, sgu_ln_w, sgu_ln_b, sgu_w, sgu_b, hgrn_norm_w, w_a, w_b, w_o):
    u, v, og, gate_a, gate_b = parts
    y_a = sgu(jax.nn.gelu(u), jax.nn.gelu(v), sgu_ln_w, sgu_ln_b, sgu_w, sgu_b)
    y_b = hgrn_readout(o_b, og, hgrn_norm_w)
    merged = jax.nn.sigmoid(gate_a) * (y_a @ w_a) + jax.nn.sigmoid(gate_b) * (y_b @ w_b)
    return merged @ w_o


def dwconv_grid(a, conv_w, conv_b):
    bsz, length, ch = a.shape
    rows = length // GRID_W
    y = lax.conv_general_dilated(a.reshape(bsz, rows, GRID_W, ch), conv_w[:, :, None, :].astype(a.dtype),
                                 window_strides=(1, 1), padding='SAME',
                                 dimension_numbers=('NHWC', 'HWIO', 'NHWC'), feature_group_count=ch)
    return y.reshape(bsz, length, ch) + conv_b


def dwconv_seq(a, conv_w, conv_b):
    y = lax.conv_general_dilated(a, conv_w[CONV_W // 2][:, None, :].astype(a.dtype),
                                 window_strides=(1,), padding='SAME',
                                 dimension_numbers=('NWC', 'WIO', 'NWC'), feature_group_count=a.shape[-1])
    return y + conv_b


def conv_ffn(h, w_up, conv_w, conv_b, w_down, on_grid):
    a, v = jnp.split(h @ w_up, 2, axis=-1)
    a = dwconv_grid(a, conv_w, conv_b) if on_grid else dwconv_seq(a, conv_w, conv_b)
    return (jax.nn.gelu(a) * v) @ w_down


def _fwd_setup_inputs(seed: int = 0) -> dict:
    key = jax.random.key(seed)
    ks = jax.random.split(key, 24)
    nrm = lambda k, shape, s: s * jax.random.normal(k, shape, jnp.float32)
    gain = lambda k, shape: 1.0 + nrm(k, shape, 0.02)
    return {
        'x': nrm(ks[0], (BATCH, SEQ, D_MODEL), 1.0),
        'c': nrm(ks[1], (BATCH, D_MODEL), 1.0),
        'ctx': nrm(ks[2], (BATCH, CTX_LEN, D_MODEL), 1.0),
        'c_ctx': nrm(ks[3], (D_MODEL,), 1.0),
        'ada_w': nrm(ks[4], (DEPTH, D_MODEL, N_MOD * D_MODEL), 0.5 * D_MODEL ** -0.5),
        'ada_b': nrm(ks[5], (DEPTH, N_MOD * D_MODEL), 0.02),
        'norm1_w': gain(ks[6], (DEPTH, D_MODEL)),
        'w_in': nrm(ks[7], (DEPTH, D_MODEL, D_IN), D_MODEL ** -0.5),
        'sgu_ln_w': gain(ks[8], (DEPTH, D_A)),
        'sgu_ln_b': nrm(ks[9], (DEPTH, D_A), 0.02),
        'sgu_w': nrm(ks[10], (DEPTH, SGU_GROUPS, SGU_CHUNK, SGU_CHUNK), SGU_CHUNK ** -0.5),
        'sgu_b': gain(ks[11], (DEPTH, SGU_GROUPS, SGU_CHUNK)),
        'hgrn_lower_bounds': nrm(ks[12], (DEPTH, 2 * D_B), 0.1),
        'hgrn_norm_w': gain(ks[13], (DEPTH, HGRN_HEAD_DIM)),
        'w_branch_a': nrm(ks[14], (DEPTH, D_A, D_MODEL), D_A ** -0.5),
        'w_branch_b': nrm(ks[15], (DEPTH, D_B, D_MODEL), D_B ** -0.5),
        'w_out': nrm(ks[16], (DEPTH, D_MODEL, D_MODEL), D_MODEL ** -0.5),
        'norm2_w': gain(ks[17], (DEPTH, D_MODEL)),
        'ffn_w_up': nrm(ks[18], (DEPTH, D_MODEL, 2 * D_FF), D_MODEL ** -0.5),
        'ffn_conv_w': nrm(ks[19], (DEPTH, CONV_W, CONV_W, D_FF), 1.0 / CONV_W),
        'ffn_conv_b': nrm(ks[20], (DEPTH, D_FF), 0.02),
        'ffn_w_down': nrm(ks[21], (DEPTH, D_FF, D_MODEL), D_FF ** -0.5),
        'final_norm_w': gain(ks[22], (D_MODEL,)),
    }


def _fwd_reference(x, c, ctx, c_ctx, ada_w, ada_b, norm1_w, w_in, sgu_ln_w, sgu_ln_b, sgu_w, sgu_b,
              hgrn_lower_bounds, hgrn_norm_w, w_branch_a, w_branch_b, w_out, norm2_w,
              ffn_w_up, ffn_conv_w, ffn_conv_b, ffn_w_down, final_norm_w):
    lb = jax.nn.softmax(hgrn_lower_bounds.astype(jnp.float32), axis=0)
    lb = jnp.cumsum(lb, axis=0) - lb[0]
    zero_state = jnp.zeros((ctx.shape[0], HGRN_HEADS, HGRN_HEAD_DIM, HGRN_HEAD_DIM), jnp.float32)
    for l in range(DEPTH):
        last = l == DEPTH - 1
        mod_x = (jax.nn.silu(c) @ ada_w[l] + ada_b[l])[:, None, :]
        mod_c = jax.nn.silu(c_ctx) @ ada_w[l] + ada_b[l]
        sh1, sc1, g1, sh2, sc2, g2 = jnp.split(mod_x, N_MOD, axis=-1)
        csh1, csc1, cg1, csh2, csc2, cg2 = jnp.split(mod_c, N_MOD, axis=-1)
        lb_f, lb_b = lb[l, :D_B], lb[l, D_B:]

        h_c = modulate(rms_norm(ctx, norm1_w[l]), csh1, csc1)
        n_cols = IN_SPLITS[3] if last else D_IN
        parts_c = jnp.split(h_c @ w_in[l, :, :n_cols], IN_SPLITS[:3] if last else IN_SPLITS, axis=-1)
        o_c, s_f, s_b = hgrn_bidir(*parts_c[:4], lb_f, lb_b, zero_state, zero_state)

        h_x = modulate(rms_norm(x, norm1_w[l]), sh1, sc1)
        parts_x = jnp.split(h_x @ w_in[l], IN_SPLITS, axis=-1)
        o_x, _, _ = hgrn_bidir(*parts_x[:4], lb_f, lb_b, s_f, s_b)
        x = x + g1 * token_mixer_out(parts_x[4:], o_x, sgu_ln_w[l], sgu_ln_b[l], sgu_w[l], sgu_b[l],
                                     hgrn_norm_w[l], w_branch_a[l], w_branch_b[l], w_out[l])
        h2 = modulate(rms_norm(x, norm2_w[l]), sh2, sc2)
        x = x + g2 * conv_ffn(h2, ffn_w_up[l], ffn_conv_w[l], ffn_conv_b[l], ffn_w_down[l], True)

        if not last:
            ctx = ctx + cg1 * token_mixer_out(parts_c[4:], o_c, sgu_ln_w[l], sgu_ln_b[l], sgu_w[l], sgu_b[l],
                                              hgrn_norm_w[l], w_branch_a[l], w_branch_b[l], w_out[l])
            hc2 = modulate(rms_norm(ctx, norm2_w[l]), csh2, csc2)
            ctx = ctx + cg2 * conv_ffn(hc2, ffn_w_up[l], ffn_conv_w[l], ffn_conv_b[l], ffn_w_down[l], False)
    return rms_norm(x, final_norm_w)


import jax as _jax
import jax.numpy as _jnp

TWIN_FORMAT = 'train_step'
FWD_PARAMS = ['x', 'c', 'ctx', 'c_ctx', 'ada_w', 'ada_b', 'norm1_w', 'w_in', 'sgu_ln_w', 'sgu_ln_b', 'sgu_w', 'sgu_b', 'hgrn_lower_bounds', 'hgrn_norm_w', 'w_branch_a', 'w_branch_b', 'w_out', 'norm2_w', 'ffn_w_up', 'ffn_conv_w', 'ffn_conv_b', 'ffn_w_down', 'final_norm_w']
TWIN_WEIGHTS = ['c_ctx', 'ada_w', 'ada_b', 'norm1_w', 'w_in', 'sgu_ln_w', 'sgu_ln_b', 'sgu_w', 'sgu_b', 'hgrn_lower_bounds', 'hgrn_norm_w', 'w_branch_a', 'w_branch_b', 'w_out', 'norm2_w', 'ffn_w_up', 'ffn_conv_w', 'ffn_conv_b', 'ffn_w_down', 'final_norm_w']
TWIN_DIFF_INPUT = 'x'
TWIN_INPUTS = ['x', 'c', 'ctx', 'c_ctx', 'ada_w', 'ada_b', 'norm1_w', 'w_in', 'sgu_ln_w', 'sgu_ln_b', 'sgu_w', 'sgu_b', 'hgrn_lower_bounds', 'hgrn_norm_w', 'w_branch_a', 'w_branch_b', 'w_out', 'norm2_w', 'ffn_w_up', 'ffn_conv_w', 'ffn_conv_b', 'ffn_w_down', 'final_norm_w', 'loss_target', 'm_c_ctx', 'm_ada_w', 'm_ada_b', 'm_norm1_w', 'm_w_in', 'm_sgu_ln_w', 'm_sgu_ln_b', 'm_sgu_w', 'm_sgu_b', 'm_hgrn_lower_bounds', 'm_hgrn_norm_w', 'm_w_branch_a', 'm_w_branch_b', 'm_w_out', 'm_norm2_w', 'm_ffn_w_up', 'm_ffn_conv_w', 'm_ffn_conv_b', 'm_ffn_w_down', 'm_final_norm_w', 'v_c_ctx', 'v_ada_w', 'v_ada_b', 'v_norm1_w', 'v_w_in', 'v_sgu_ln_w', 'v_sgu_ln_b', 'v_sgu_w', 'v_sgu_b', 'v_hgrn_lower_bounds', 'v_hgrn_norm_w', 'v_w_branch_a', 'v_w_branch_b', 'v_w_out', 'v_norm2_w', 'v_ffn_w_up', 'v_ffn_conv_w', 'v_ffn_conv_b', 'v_ffn_w_down', 'v_final_norm_w']
TWIN_OUTPUTS = ['loss', 'grad_x', 'grad_c_ctx', 'grad_ada_w', 'grad_ada_b', 'grad_norm1_w', 'grad_w_in', 'grad_sgu_ln_w', 'grad_sgu_ln_b', 'grad_sgu_w', 'grad_sgu_b', 'grad_hgrn_lower_bounds', 'grad_hgrn_norm_w', 'grad_w_branch_a', 'grad_w_branch_b', 'grad_w_out', 'grad_norm2_w', 'grad_ffn_w_up', 'grad_ffn_conv_w', 'grad_ffn_conv_b', 'grad_ffn_w_down', 'grad_final_norm_w', 'delta_c_ctx', 'delta_ada_w', 'delta_ada_b', 'delta_norm1_w', 'delta_w_in', 'delta_sgu_ln_w', 'delta_sgu_ln_b', 'delta_sgu_w', 'delta_sgu_b', 'delta_hgrn_lower_bounds', 'delta_hgrn_norm_w', 'delta_w_branch_a', 'delta_w_branch_b', 'delta_w_out', 'delta_norm2_w', 'delta_ffn_w_up', 'delta_ffn_conv_w', 'delta_ffn_conv_b', 'delta_ffn_w_down', 'delta_final_norm_w', 'new_m_c_ctx', 'new_m_ada_w', 'new_m_ada_b', 'new_m_norm1_w', 'new_m_w_in', 'new_m_sgu_ln_w', 'new_m_sgu_ln_b', 'new_m_sgu_w', 'new_m_sgu_b', 'new_m_hgrn_lower_bounds', 'new_m_hgrn_norm_w', 'new_m_w_branch_a', 'new_m_w_branch_b', 'new_m_w_out', 'new_m_norm2_w', 'new_m_ffn_w_up', 'new_m_ffn_conv_w', 'new_m_ffn_conv_b', 'new_m_ffn_w_down', 'new_m_final_norm_w', 'new_v_c_ctx', 'new_v_ada_w', 'new_v_ada_b', 'new_v_norm1_w', 'new_v_w_in', 'new_v_sgu_ln_w', 'new_v_sgu_ln_b', 'new_v_sgu_w', 'new_v_sgu_b', 'new_v_hgrn_lower_bounds', 'new_v_hgrn_norm_w', 'new_v_w_branch_a', 'new_v_w_branch_b', 'new_v_w_out', 'new_v_norm2_w', 'new_v_ffn_w_up', 'new_v_ffn_conv_w', 'new_v_ffn_conv_b', 'new_v_ffn_w_down', 'new_v_final_norm_w']
TWIN_LEAF_KINDS = {'loss': 'loss', 'grad_x': 'grad_x', 'grad_c_ctx': 'grad_w', 'grad_ada_w': 'grad_w', 'grad_ada_b': 'grad_w', 'grad_norm1_w': 'grad_w', 'grad_w_in': 'grad_w', 'grad_sgu_ln_w': 'grad_w', 'grad_sgu_ln_b': 'grad_w', 'grad_sgu_w': 'grad_w', 'grad_sgu_b': 'grad_w', 'grad_hgrn_lower_bounds': 'grad_w', 'grad_hgrn_norm_w': 'grad_w', 'grad_w_branch_a': 'grad_w', 'grad_w_branch_b': 'grad_w', 'grad_w_out': 'grad_w', 'grad_norm2_w': 'grad_w', 'grad_ffn_w_up': 'grad_w', 'grad_ffn_conv_w': 'grad_w', 'grad_ffn_conv_b': 'grad_w', 'grad_ffn_w_down': 'grad_w', 'grad_final_norm_w': 'grad_w', 'delta_c_ctx': 'delta_w', 'delta_ada_w': 'delta_w', 'delta_ada_b': 'delta_w', 'delta_norm1_w': 'delta_w', 'delta_w_in': 'delta_w', 'delta_sgu_ln_w': 'delta_w', 'delta_sgu_ln_b': 'delta_w', 'delta_sgu_w': 'delta_w', 'delta_sgu_b': 'delta_w', 'delta_hgrn_lower_bounds': 'delta_w', 'delta_hgrn_norm_w': 'delta_w', 'delta_w_branch_a': 'delta_w', 'delta_w_branch_b': 'delta_w', 'delta_w_out': 'delta_w', 'delta_norm2_w': 'delta_w', 'delta_ffn_w_up': 'delta_w', 'delta_ffn_conv_w': 'delta_w', 'delta_ffn_conv_b': 'delta_w', 'delta_ffn_w_down': 'delta_w', 'delta_final_norm_w': 'delta_w', 'new_m_c_ctx': 'new_m', 'new_m_ada_w': 'new_m', 'new_m_ada_b': 'new_m', 'new_m_norm1_w': 'new_m', 'new_m_w_in': 'new_m', 'new_m_sgu_ln_w': 'new_m', 'new_m_sgu_ln_b': 'new_m', 'new_m_sgu_w': 'new_m', 'new_m_sgu_b': 'new_m', 'new_m_hgrn_lower_bounds': 'new_m', 'new_m_hgrn_norm_w': 'new_m', 'new_m_w_branch_a': 'new_m', 'new_m_w_branch_b': 'new_m', 'new_m_w_out': 'new_m', 'new_m_norm2_w': 'new_m', 'new_m_ffn_w_up': 'new_m', 'new_m_ffn_conv_w': 'new_m', 'new_m_ffn_conv_b': 'new_m', 'new_m_ffn_w_down': 'new_m', 'new_m_final_norm_w': 'new_m', 'new_v_c_ctx': 'new_v', 'new_v_ada_w': 'new_v', 'new_v_ada_b': 'new_v', 'new_v_norm1_w': 'new_v', 'new_v_w_in': 'new_v', 'new_v_sgu_ln_w': 'new_v', 'new_v_sgu_ln_b': 'new_v', 'new_v_sgu_w': 'new_v', 'new_v_sgu_b': 'new_v', 'new_v_hgrn_lower_bounds': 'new_v', 'new_v_hgrn_norm_w': 'new_v', 'new_v_w_branch_a': 'new_v', 'new_v_w_branch_b': 'new_v', 'new_v_w_out': 'new_v', 'new_v_norm2_w': 'new_v', 'new_v_ffn_w_up': 'new_v', 'new_v_ffn_conv_w': 'new_v', 'new_v_ffn_conv_b': 'new_v', 'new_v_ffn_w_down': 'new_v', 'new_v_final_norm_w': 'new_v'}


def _forward(args):
    return _fwd_reference(*[args[k] for k in FWD_PARAMS])


def _output_shape():
    def fwd():
        inp = _fwd_setup_inputs(0)
        return _fwd_reference(*[inp[k] for k in FWD_PARAMS])
    out = _jax.eval_shape(fwd)
    return out.shape, out.dtype

N_MICROBATCH = 1
ADAM_LR = 0.001
ADAM_B1 = 0.9
ADAM_B2 = 0.999
ADAM_EPS = 1e-08
ADAM_WD = 0.01
ADAM_STEP = 10
PER_EXAMPLE_BATCH_AXIS = {'x': 0, 'c': 0, 'ctx': 0, 'loss_target': 0}
SHARED_INPUTS = []
_WEIGHT_DTYPES = {'c_ctx': _jnp.float32, 'ada_w': _jnp.float32, 'ada_b': _jnp.float32, 'norm1_w': _jnp.float32, 'w_in': _jnp.float32, 'sgu_ln_w': _jnp.float32, 'sgu_ln_b': _jnp.float32, 'sgu_w': _jnp.float32, 'sgu_b': _jnp.float32, 'hgrn_lower_bounds': _jnp.float32, 'hgrn_norm_w': _jnp.float32, 'w_branch_a': _jnp.float32, 'w_branch_b': _jnp.float32, 'w_out': _jnp.float32, 'norm2_w': _jnp.float32, 'ffn_w_up': _jnp.float32, 'ffn_conv_w': _jnp.float32, 'ffn_conv_b': _jnp.float32, 'ffn_w_down': _jnp.float32, 'final_norm_w': _jnp.float32}
MOMENT_SCALE = {'c_ctx': 7.875975e-04, 'ada_w': 5.424564e-02, 'ada_b': 9.080758e-02, 'norm1_w': 4.675171e-02, 'w_in': 1.672667e-02, 'sgu_ln_w': 2.103698e-02, 'sgu_ln_b': 2.064221e-02, 'sgu_w': 2.096786e-02, 'sgu_b': 2.204034e-02, 'hgrn_lower_bounds': 7.980738e-04, 'hgrn_norm_w': 5.691366e-02, 'w_branch_a': 2.965287e-02, 'w_branch_b': 1.940433e-02, 'w_out': 3.557474e-02, 'norm2_w': 5.981081e-02, 'ffn_w_up': 2.580703e-02, 'ffn_conv_w': 2.624837e-02, 'ffn_conv_b': 2.313453e-02, 'ffn_w_down': 4.211277e-02, 'final_norm_w': 3.201746e+01}


def _to_microbatches(a, axis):
    t = _jnp.moveaxis(a, axis, 0)
    t = t.reshape((N_MICROBATCH, t.shape[0] // N_MICROBATCH) + t.shape[1:])
    return _jnp.moveaxis(t, 1, axis + 1)


def setup_inputs(seed: int = 0) -> dict:
    inp = _fwd_setup_inputs(seed)
    key = _jax.random.fold_in(_jax.random.key(seed), 7919)
    shape, _ = _output_shape()
    out = dict(inp)
    out["loss_target"] = _jax.random.normal(_jax.random.fold_in(key, 0), shape, _jnp.float32)
    for i, name in enumerate(TWIN_WEIGHTS):
        w = inp[name].astype(_jnp.float32)
        if MOMENT_SCALE is None:
            s = _jnp.sqrt(_jnp.mean(_jnp.square(w)) + 1e-30)
        else:
            s = MOMENT_SCALE[name]
        km, kv = _jax.random.split(_jax.random.fold_in(key, i + 1))
        out[name] = w
        out["m_" + name] = s * _jax.random.normal(km, w.shape, _jnp.float32)
        out["v_" + name] = (s * s) * _jax.random.uniform(kv, w.shape, _jnp.float32, 0.5, 1.5)
    if N_MICROBATCH > 1:
        for name, axis in PER_EXAMPLE_BATCH_AXIS.items():
            out[name] = _to_microbatches(out[name], axis)
    return {'x': out['x'], 'c': out['c'], 'ctx': out['ctx'], 'c_ctx': out['c_ctx'], 'ada_w': out['ada_w'], 'ada_b': out['ada_b'], 'norm1_w': out['norm1_w'], 'w_in': out['w_in'], 'sgu_ln_w': out['sgu_ln_w'], 'sgu_ln_b': out['sgu_ln_b'], 'sgu_w': out['sgu_w'], 'sgu_b': out['sgu_b'], 'hgrn_lower_bounds': out['hgrn_lower_bounds'], 'hgrn_norm_w': out['hgrn_norm_w'], 'w_branch_a': out['w_branch_a'], 'w_branch_b': out['w_branch_b'], 'w_out': out['w_out'], 'norm2_w': out['norm2_w'], 'ffn_w_up': out['ffn_w_up'], 'ffn_conv_w': out['ffn_conv_w'], 'ffn_conv_b': out['ffn_conv_b'], 'ffn_w_down': out['ffn_w_down'], 'final_norm_w': out['final_norm_w'], 'loss_target': out['loss_target'], 'm_c_ctx': out['m_c_ctx'], 'm_ada_w': out['m_ada_w'], 'm_ada_b': out['m_ada_b'], 'm_norm1_w': out['m_norm1_w'], 'm_w_in': out['m_w_in'], 'm_sgu_ln_w': out['m_sgu_ln_w'], 'm_sgu_ln_b': out['m_sgu_ln_b'], 'm_sgu_w': out['m_sgu_w'], 'm_sgu_b': out['m_sgu_b'], 'm_hgrn_lower_bounds': out['m_hgrn_lower_bounds'], 'm_hgrn_norm_w': out['m_hgrn_norm_w'], 'm_w_branch_a': out['m_w_branch_a'], 'm_w_branch_b': out['m_w_branch_b'], 'm_w_out': out['m_w_out'], 'm_norm2_w': out['m_norm2_w'], 'm_ffn_w_up': out['m_ffn_w_up'], 'm_ffn_conv_w': out['m_ffn_conv_w'], 'm_ffn_conv_b': out['m_ffn_conv_b'], 'm_ffn_w_down': out['m_ffn_w_down'], 'm_final_norm_w': out['m_final_norm_w'], 'v_c_ctx': out['v_c_ctx'], 'v_ada_w': out['v_ada_w'], 'v_ada_b': out['v_ada_b'], 'v_norm1_w': out['v_norm1_w'], 'v_w_in': out['v_w_in'], 'v_sgu_ln_w': out['v_sgu_ln_w'], 'v_sgu_ln_b': out['v_sgu_ln_b'], 'v_sgu_w': out['v_sgu_w'], 'v_sgu_b': out['v_sgu_b'], 'v_hgrn_lower_bounds': out['v_hgrn_lower_bounds'], 'v_hgrn_norm_w': out['v_hgrn_norm_w'], 'v_w_branch_a': out['v_w_branch_a'], 'v_w_branch_b': out['v_w_branch_b'], 'v_w_out': out['v_w_out'], 'v_norm2_w': out['v_norm2_w'], 'v_ffn_w_up': out['v_ffn_w_up'], 'v_ffn_conv_w': out['v_ffn_conv_w'], 'v_ffn_conv_b': out['v_ffn_conv_b'], 'v_ffn_w_down': out['v_ffn_w_down'], 'v_final_norm_w': out['v_final_norm_w']}


def _loss(weights, diff, rest, loss_target):
    with _jax.named_scope("forward"):
        args = {**rest, TWIN_DIFF_INPUT: diff, **{k: w.astype(_WEIGHT_DTYPES[k]) for k, w in weights.items()}}
        y = _forward(args)
    with _jax.named_scope("loss_head"):
        err = _jnp.square(y.astype(_jnp.float32) - loss_target)
        return 0.5 * _jnp.sum(_jnp.mean(err, axis=-1)) if err.ndim else 0.5 * err


def _adamw(w, g, m, v):
    m = ADAM_B1 * m + (1.0 - ADAM_B1) * g
    v = ADAM_B2 * v + (1.0 - ADAM_B2) * _jnp.square(g)
    m_hat = m / (1.0 - ADAM_B1 ** ADAM_STEP)
    v_hat = v / (1.0 - ADAM_B2 ** ADAM_STEP)
    delta = -ADAM_LR * (m_hat / (_jnp.sqrt(v_hat) + ADAM_EPS) + ADAM_WD * w)
    return delta, m, v


def reference(x, c, ctx, c_ctx, ada_w, ada_b, norm1_w, w_in, sgu_ln_w, sgu_ln_b, sgu_w, sgu_b, hgrn_lower_bounds, hgrn_norm_w, w_branch_a, w_branch_b, w_out, norm2_w, ffn_w_up, ffn_conv_w, ffn_conv_b, ffn_w_down, final_norm_w, loss_target, m_c_ctx, m_ada_w, m_ada_b, m_norm1_w, m_w_in, m_sgu_ln_w, m_sgu_ln_b, m_sgu_w, m_sgu_b, m_hgrn_lower_bounds, m_hgrn_norm_w, m_w_branch_a, m_w_branch_b, m_w_out, m_norm2_w, m_ffn_w_up, m_ffn_conv_w, m_ffn_conv_b, m_ffn_w_down, m_final_norm_w, v_c_ctx, v_ada_w, v_ada_b, v_norm1_w, v_w_in, v_sgu_ln_w, v_sgu_ln_b, v_sgu_w, v_sgu_b, v_hgrn_lower_bounds, v_hgrn_norm_w, v_w_branch_a, v_w_branch_b, v_w_out, v_norm2_w, v_ffn_w_up, v_ffn_conv_w, v_ffn_conv_b, v_ffn_w_down, v_final_norm_w):
    given = dict(x=x, c=c, ctx=ctx, c_ctx=c_ctx, ada_w=ada_w, ada_b=ada_b, norm1_w=norm1_w, w_in=w_in, sgu_ln_w=sgu_ln_w, sgu_ln_b=sgu_ln_b, sgu_w=sgu_w, sgu_b=sgu_b, hgrn_lower_bounds=hgrn_lower_bounds, hgrn_norm_w=hgrn_norm_w, w_branch_a=w_branch_a, w_branch_b=w_branch_b, w_out=w_out, norm2_w=norm2_w, ffn_w_up=ffn_w_up, ffn_conv_w=ffn_conv_w, ffn_conv_b=ffn_conv_b, ffn_w_down=ffn_w_down, final_norm_w=final_norm_w, loss_target=loss_target, m_c_ctx=m_c_ctx, m_ada_w=m_ada_w, m_ada_b=m_ada_b, m_norm1_w=m_norm1_w, m_w_in=m_w_in, m_sgu_ln_w=m_sgu_ln_w, m_sgu_ln_b=m_sgu_ln_b, m_sgu_w=m_sgu_w, m_sgu_b=m_sgu_b, m_hgrn_lower_bounds=m_hgrn_lower_bounds, m_hgrn_norm_w=m_hgrn_norm_w, m_w_branch_a=m_w_branch_a, m_w_branch_b=m_w_branch_b, m_w_out=m_w_out, m_norm2_w=m_norm2_w, m_ffn_w_up=m_ffn_w_up, m_ffn_conv_w=m_ffn_conv_w, m_ffn_conv_b=m_ffn_conv_b, m_ffn_w_down=m_ffn_w_down, m_final_norm_w=m_final_norm_w, v_c_ctx=v_c_ctx, v_ada_w=v_ada_w, v_ada_b=v_ada_b, v_norm1_w=v_norm1_w, v_w_in=v_w_in, v_sgu_ln_w=v_sgu_ln_w, v_sgu_ln_b=v_sgu_ln_b, v_sgu_w=v_sgu_w, v_sgu_b=v_sgu_b, v_hgrn_lower_bounds=v_hgrn_lower_bounds, v_hgrn_norm_w=v_hgrn_norm_w, v_w_branch_a=v_w_branch_a, v_w_branch_b=v_w_branch_b, v_w_out=v_w_out, v_norm2_w=v_norm2_w, v_ffn_w_up=v_ffn_w_up, v_ffn_conv_w=v_ffn_conv_w, v_ffn_conv_b=v_ffn_conv_b, v_ffn_w_down=v_ffn_w_down, v_final_norm_w=v_final_norm_w)
    weights = {n: given[n] for n in TWIN_WEIGHTS}
    shared = {n: given[n] for n in SHARED_INPUTS}
    per_example = {n: given[n] for n in ['x', 'c', 'ctx']}
    grad_fn = _jax.value_and_grad(_loss, argnums=(0, 1))

    def one_microbatch(ex, loss_target):
        ex = dict(ex)
        diff = ex.pop(TWIN_DIFF_INPUT)
        return grad_fn(weights, diff, {**shared, **ex}, loss_target)

    if N_MICROBATCH == 1:
        loss, (grad_w, grad_x) = one_microbatch(per_example, given["loss_target"])
    else:
        def body(carry, xs):
            loss_sum, grad_sum = carry
            l_k, (gw_k, gx_k) = one_microbatch(xs[0], xs[1])
            with _jax.named_scope("update"):
                return (loss_sum + l_k, _jax.tree.map(_jnp.add, grad_sum, gw_k)), gx_k

        init = (_jnp.zeros((), _jnp.float32), _jax.tree.map(_jnp.zeros_like, weights))
        (loss, grad_w), grad_x = _jax.lax.scan(body, init, (per_example, given["loss_target"]))
    with _jax.named_scope("update"):
        delta_w, new_m, new_v = {}, {}, {}
        for n in TWIN_WEIGHTS:
            delta_w[n], new_m[n], new_v[n] = _adamw(weights[n], grad_w[n], given["m_" + n], given["v_" + n])
    return (loss, grad_x, *[grad_w[n] for n in TWIN_WEIGHTS], *[delta_w[n] for n in TWIN_WEIGHTS],
            *[new_m[n] for n in TWIN_WEIGHTS], *[new_v[n] for n in TWIN_WEIGHTS])
```

```python
import functools

import jax
import jax.numpy as jnp
from jax import lax
from jax.experimental import pallas as pl
from jax.experimental.pallas import tpu as pltpu

F32 = jnp.float32
BF16 = jnp.bfloat16

GRID_W = 64
HG_CHUNK = 64
SGU_CHUNK = 128
HEAD = 128
TB = 256
N_MOD = 6
RMS_EPS = 1e-6
LN_EPS = 1e-5
VMEM_LIMIT = 48 * 1024 * 1024
N_CHIPS = 4
N_CORES = 2

ADAM_LR = 0.001
ADAM_B1 = 0.9
ADAM_B2 = 0.999
ADAM_EPS = 1e-08
ADAM_WD = 0.01
ADAM_STEP = 10

_GELU_C = 0.7978845608028654
_GELU_A = 0.044715


def _sigmoid(x):
    return 1.0 / (1.0 + jnp.exp(-x))


def _silu(x):
    return x * _sigmoid(x)


def _dsilu(x):
    s = _sigmoid(x)
    return s * (1.0 + x * (1.0 - s))


def _gelu(x):
    return 0.5 * x * (1.0 + jnp.tanh(_GELU_C * (x + _GELU_A * x * x * x)))


def _dgelu(x):
    t = jnp.tanh(_GELU_C * (x + _GELU_A * x * x * x))
    return 0.5 * (1.0 + t) + 0.5 * x * (1.0 - t * t) * _GELU_C * (1.0 + 3.0 * _GELU_A * x * x)


def _dot(a, b, ca, cb):
    return lax.dot_general(a, b, (((ca,), (cb,)), ((), ())), preferred_element_type=F32)


def _nn(a, b):
    return _dot(a, b, 1, 0)


def _nt(a, b):
    return _dot(a, b, 1, 1)


def _tn(a, b):
    return _dot(a, b, 0, 0)


def _params(*sem):
    return pltpu.CompilerParams(dimension_semantics=sem if sem else None, vmem_limit_bytes=VMEM_LIMIT)


def _stream_of(i, ctx_blocks):
    return (i >= ctx_blocks).astype(jnp.int32)


def _mm(a, b, mode, tm, tn, tk, out_dtype, name, add=None):
    if mode == "nn":
        (M, K), (K2, N) = a.shape, b.shape
    elif mode == "nt":
        (M, K), (N, K2) = a.shape, b.shape
    else:
        (K, M), (K2, N) = a.shape, b.shape
    assert K == K2 and M % tm == 0 and N % tn == 0 and K % tk == 0, (name, a.shape, b.shape, tm, tn, tk)
    nk = K // tk
    if mode == "tn":
        a_spec = pl.BlockSpec((tk, tm), lambda j, i, k: (k, i))
    else:
        a_spec = pl.BlockSpec((tm, tk), lambda j, i, k: (i, k))
    if mode == "nt":
        b_spec = pl.BlockSpec((tn, tk), lambda j, i, k: (j, k))
    else:
        b_spec = pl.BlockSpec((tk, tn), lambda j, i, k: (k, j))
    o_spec = pl.BlockSpec((tm, tn), lambda j, i, k: (i, j))
    ca, cb = {"nn": (1, 0), "nt": (1, 1), "tn": (0, 0)}[mode]

    def body(a_ref, b_ref, *rest):
        if add is None:
            o_ref, acc = rest
        else:
            add_ref, o_ref, acc = rest
        k = pl.program_id(2)

        @pl.when(k == 0)
        def _():
            acc[...] = jnp.zeros_like(acc)

        acc[...] += _dot(a_ref[...], b_ref[...], ca, cb)

        @pl.when(k == nk - 1)
        def _():
            r = acc[...]
            if add is not None:
                r = r + add_ref[...]
            o_ref[...] = r.astype(out_dtype)

    ins = [a, b] + ([] if add is None else [add])
    specs = [a_spec, b_spec] + ([] if add is None else [o_spec])
    return pl.pallas_call(
        body, name=name, grid=(N // tn, M // tm, nk), in_specs=specs, out_specs=o_spec,
        out_shape=jax.ShapeDtypeStruct((M, N), out_dtype),
        scratch_shapes=[pltpu.VMEM((tm, tn), F32)],
        compiler_params=_params("parallel", "parallel", "arbitrary"),
    )(*ins)


def _tile(n, pref):
    if n <= pref:
        return n
    best = None
    for t in range(128, pref + 1, 128):
        if n % t == 0:
            best = t
    assert best is not None, (n, pref)
    return best


def _rows_tile(n, pref):
    if n <= pref:
        return n
    best = None
    for t in range(16, pref + 1, 16):
        if n % t == 0:
            best = t
    assert best is not None, (n, pref)
    return best


def _mm_nn(a, b, out_dtype, name):
    M, K = a.shape
    N = b.shape[1]
    return _mm(a, b, "nn", _rows_tile(M, 512), _tile(N, 1536), _tile(K, 1536), out_dtype, name)


def _mm_nt(a, b, out_dtype, name, add=None):
    M, K = a.shape
    N = b.shape[0]
    return _mm(a, b, "nt", _rows_tile(M, 1088), _tile(N, 1024), _tile(K, 1536), out_dtype, name, add=add)


def _mm_tn(a, b, out_dtype, name):
    K, M = a.shape
    N = b.shape[1]
    tm, tn = _tile(M, 1408), _tile(N, 1408)
    if tm * tn > 1408 * 1152:
        tn = _tile(N, 1024)
    return _mm(a, b, "tn", tm, tn, _rows_tile(K, 2176), out_dtype, name)


def _mod_fwd(cv, w, b, name):
    R, D = cv.shape
    N = w.shape[1]
    tn = _tile(N, 1536)

    def body(cv_ref, w_ref, b_ref, mod_ref, sa_ref):
        sa = _silu(cv_ref[...]).astype(BF16)
        sa_ref[...] = sa
        mod_ref[...] = _nn(sa, w_ref[...]) + b_ref[...]

    return pl.pallas_call(
        body, name=name, grid=(N // tn,),
        in_specs=[pl.BlockSpec((R, D), lambda j: (0, 0)), pl.BlockSpec((D, tn), lambda j: (0, j)),
                  pl.BlockSpec((1, tn), lambda j: (0, j))],
        out_specs=[pl.BlockSpec((R, tn), lambda j: (0, j)), pl.BlockSpec((R, D), lambda j: (0, 0))],
        out_shape=[jax.ShapeDtypeStruct((R, N), F32), jax.ShapeDtypeStruct((R, D), BF16)],
        compiler_params=_params("arbitrary"),
    )(cv, w, b)


def _cvec_bwd(dmod, w, cv, name):
    R, N = dmod.shape
    D = w.shape[0]
    tk = _tile(N, 1536)
    nk = N // tk

    def body(dm_ref, w_ref, cv_ref, o_ref):
        k = pl.program_id(0)

        @pl.when(k == 0)
        def _():
            o_ref[...] = jnp.zeros_like(o_ref)

        o_ref[...] += _nt(dm_ref[...].astype(BF16), w_ref[...])

        @pl.when(k == nk - 1)
        def _():
            o_ref[...] = o_ref[...] * _dsilu(cv_ref[...])

    return pl.pallas_call(
        body, name=name, grid=(nk,),
        in_specs=[pl.BlockSpec((R, tk), lambda k: (0, k)), pl.BlockSpec((D, tk), lambda k: (0, k)),
                  pl.BlockSpec((R, D), lambda k: (0, 0))],
        out_specs=pl.BlockSpec((R, D), lambda k: (0, 0)),
        out_shape=jax.ShapeDtypeStruct((R, D), F32),
        compiler_params=_params("arbitrary"),
    )(dmod, w, cv)


def _norm_mod(x, nw, mod, which, ctx_rows, name):
    T, D = x.shape
    cb = ctx_rows // TB

    def body(x_ref, nw_ref, mod_ref, h_ref):
        xv = x_ref[...]
        r = lax.rsqrt(jnp.mean(xv * xv, axis=-1, keepdims=True) + RMS_EPS)
        y = xv * r * nw_ref[...]
        sh = mod_ref[which:which + 1, :]
        sc = mod_ref[which + 1:which + 2, :]
        h_ref[...] = (y * (1.0 + sc) + sh).astype(BF16)

    return pl.pallas_call(
        body, name=name, grid=(T // TB,),
        in_specs=[pl.BlockSpec((TB, D), lambda i: (i, 0)), pl.BlockSpec((1, D), lambda i: (0, 0)),
                  pl.BlockSpec((None, N_MOD, D), lambda i: (_stream_of(i, cb), 0, 0))],
        out_specs=pl.BlockSpec((TB, D), lambda i: (i, 0)),
        out_shape=jax.ShapeDtypeStruct((T, D), BF16),
        compiler_params=_params("parallel"),
    )(x, nw, mod)


def _norm_mod_bwd(dh, x, dres, nw, mod, which, ctx_rows, name):
    T, D = x.shape
    cb = ctx_rows // TB

    def body(dh_ref, x_ref, dres_ref, nw_ref, mod_ref, dx_ref, dm_ref, dnw_ref):
        i = pl.program_id(0)

        @pl.when(i == 0)
        def _():
            dnw_ref[...] = jnp.zeros_like(dnw_ref)

        @pl.when((i == 0) | (i == cb))
        def _():
            dm_ref[...] = jnp.zeros_like(dm_ref)

        xv = x_ref[...]
        dh = dh_ref[...]
        r = lax.rsqrt(jnp.mean(xv * xv, axis=-1, keepdims=True) + RMS_EPS)
        xh = xv * r
        nwv = nw_ref[...]
        sc = mod_ref[which + 1:which + 2, :]
        y = xh * nwv
        dm_ref[0:1, :] += jnp.sum(dh, axis=0, keepdims=True)
        dm_ref[1:2, :] += jnp.sum(dh * y, axis=0, keepdims=True)
        dy = dh * (1.0 + sc)
        dnw_ref[...] += jnp.sum(dy * xh, axis=0, keepdims=True)
        dxh = dy * nwv
        dx_ref[...] = dres_ref[...] + r * (dxh - xh * jnp.mean(dxh * xh, axis=-1, keepdims=True))

    return pl.pallas_call(
        body, name=name, grid=(T // TB,),
        in_specs=[pl.BlockSpec((TB, D), lambda i: (i, 0)), pl.BlockSpec((TB, D), lambda i: (i, 0)),
                  pl.BlockSpec((TB, D), lambda i: (i, 0)), pl.BlockSpec((1, D), lambda i: (0, 0)),
                  pl.BlockSpec((None, N_MOD, D), lambda i: (_stream_of(i, cb), 0, 0))],
        out_specs=[pl.BlockSpec((TB, D), lambda i: (i, 0)),
                   pl.BlockSpec((None, 2, D), lambda i: (_stream_of(i, cb), 0, 0)),
                   pl.BlockSpec((1, D), lambda i: (0, 0))],
        out_shape=[jax.ShapeDtypeStruct((T, D), F32), jax.ShapeDtypeStruct((2, 2, D), F32),
                   jax.ShapeDtypeStruct((1, D), F32)],
        compiler_params=_params("arbitrary"),
    )(dh, x, dres, nw, mod)


def _scan_chunk(n, rev, n_ctx, n_all):
    if not rev:
        return n
    return jnp.where(n < n_ctx, n_ctx - 1 - n, n_all - 1 + n_ctx - n)


def _cumsum_rows(x, rev):
    rows = x.shape[0]
    row = lax.broadcasted_iota(jnp.int32, (rows, 1), 0)
    s = 1
    while s < rows:
        if not rev:
            x = x + jnp.where(row >= s, pltpu.roll(x, s, 0), 0.0)
        else:
            x = x + jnp.where(row < rows - s, pltpu.roll(x, rows - s, 0), 0.0)
        s *= 2
    return x


def _lower_bound(hlb_ref, layer):
    h = hlb_ref[...]
    if layer == 0:
        return jnp.zeros_like(h[0:1, :])
    return _sigmoid(h[1:2, :] - h[0:1, :])


def _hgrn_gates(q_ref, f_ref, hlb_ref, layer, rev):
    lb = _lower_bound(hlb_ref, layer)
    z = f_ref[...]
    sig = _sigmoid(z)
    fg = lb + (1.0 - lb) * sig
    kk = (1.0 - lb) * (1.0 - sig)
    g = jnp.log(fg)
    b = _cumsum_rows(g, rev)
    bt = jnp.sum(g, axis=0, keepdims=True)
    mid = HG_CHUNK // 2
    r = b[mid:mid + 1, :] if rev else b[mid - 1:mid, :]
    qh = _silu(q_ref[...])
    return lb, sig, fg, kk, b, bt, r, qh


def _tri_mask(rev):
    t = lax.broadcasted_iota(jnp.int32, (HG_CHUNK, HG_CHUNK), 0)
    s = lax.broadcasted_iota(jnp.int32, (HG_CHUNK, HG_CHUNK), 1)
    return (s >= t) if rev else (s <= t)


def _hgrn_fwd(parts, hlb, layer, rev, ctx_rows, name, o_add=None):
    T = parts.shape[0]
    D = hlb.shape[1] // 2
    nh = D // HEAD
    n_all, n_ctx = T // HG_CHUNK, ctx_rows // HG_CHUNK
    chunk = functools.partial(_scan_chunk, rev=rev, n_ctx=n_ctx, n_all=n_all)
    fcol = 2 if rev else 1

    def body(q_ref, f_ref, i_ref, hlb_ref, *rest):
        if o_add is None:
            o_ref, st_ref, s_scr = rest
        else:
            oa_ref, o_ref, st_ref, s_scr = rest
        n = pl.program_id(0)

        @pl.when(n == 0)
        def _():
            s_scr[...] = jnp.zeros_like(s_scr)

        lb, sig, fg, kk, b, bt, r, qh = _hgrn_gates(q_ref, f_ref, hlb_ref, layer, rev)
        qr = (qh * jnp.exp(b - r)).astype(BF16)
        kr = (kk * jnp.exp(r - b)).astype(BF16)
        qe = (qh * jnp.exp(b)).astype(BF16)
        ke = (kk * jnp.exp(bt - b)).astype(BF16)
        dec = jnp.exp(bt)
        v = i_ref[...].astype(BF16)
        mask = _tri_mask(rev)
        for h in range(nh):
            sl = slice(h * HEAD, (h + 1) * HEAD)
            st = s_scr[h]
            st_ref[h] = st
            a = jnp.where(mask, _nt(qr[:, sl], kr[:, sl]), 0.0).astype(BF16)
            o = _nn(a, v[:, sl]) + _nt(qe[:, sl], st.astype(BF16))
            if o_add is not None:
                o = o + oa_ref[:, sl]
            o_ref[:, sl] = o
            s_scr[h] = st * dec[:, sl] + _tn(v[:, sl], ke[:, sl])

    cspec = lambda col: pl.BlockSpec((HG_CHUNK, D), lambda n: (chunk(n), col))
    ins = [parts, parts, parts, hlb]
    specs = [cspec(0), cspec(fcol), cspec(3), pl.BlockSpec((2, D), lambda n: (0, 1 if rev else 0))]
    if o_add is not None:
        ins.append(o_add)
        specs.append(cspec(0))
    return pl.pallas_call(
        body, name=name, grid=(n_all,), in_specs=specs,
        out_specs=[cspec(0), pl.BlockSpec((None, nh, HEAD, HEAD), lambda n: (n, 0, 0, 0))],
        out_shape=[jax.ShapeDtypeStruct((T, D), F32), jax.ShapeDtypeStruct((n_all, nh, HEAD, HEAD), F32)],
        scratch_shapes=[pltpu.VMEM((nh, HEAD, HEAD), F32)],
        compiler_params=_params("arbitrary"),
    )(*ins)


def _hgrn_bwd(parts, hlb, do, states, layer, rev, ctx_rows, name, dq_add=None, di_add=None):
    T = parts.shape[0]
    D = hlb.shape[1] // 2
    nh = D // HEAD
    n_all, n_ctx = T // HG_CHUNK, ctx_rows // HG_CHUNK
    step = lambda m: n_all - 1 - m
    chunk = lambda m: _scan_chunk(step(m), rev, n_ctx, n_all)
    fcol = 2 if rev else 1
    has_add = dq_add is not None

    def body(q_ref, f_ref, i_ref, hlb_ref, do_ref, st_ref, *rest):
        if has_add:
            dqa_ref, dia_ref, dq_ref, dz_ref, di_ref, dlb_ref, ds_scr = rest
        else:
            dq_ref, dz_ref, di_ref, dlb_ref, ds_scr = rest
        m = pl.program_id(0)

        @pl.when(m == 0)
        def _():
            ds_scr[...] = jnp.zeros_like(ds_scr)
            dlb_ref[...] = jnp.zeros_like(dlb_ref)

        lb, sig, fg, kk, b, bt, r, qh = _hgrn_gates(q_ref, f_ref, hlb_ref, layer, rev)
        e_qr = jnp.exp(b - r)
        e_kr = jnp.exp(r - b)
        e_b = jnp.exp(b)
        e_ke = jnp.exp(bt - b)
        dec = jnp.exp(bt)
        qr = (qh * e_qr).astype(BF16)
        kr = (kk * e_kr).astype(BF16)
        qe = (qh * e_b).astype(BF16)
        ke = (kk * e_ke).astype(BF16)
        vf = i_ref[...]
        v = vf.astype(BF16)
        dov = do_ref[...].astype(BF16)
        mask = _tri_mask(rev)
        dq_parts, dk_parts, dki_parts, dv_parts, dbt_parts = [], [], [], [], []
        for h in range(nh):
            sl = slice(h * HEAD, (h + 1) * HEAD)
            st = st_ref[h]
            stb = st.astype(BF16)
            dst = ds_scr[h]
            dstb = dst.astype(BF16)
            a = jnp.where(mask, _nt(qr[:, sl], kr[:, sl]), 0.0).astype(BF16)
            da = jnp.where(mask, _nt(dov[:, sl], v[:, sl]), 0.0).astype(BF16)
            dv_parts.append(_tn(a, dov[:, sl]) + _nt(ke[:, sl], dstb))
            dq_h = _nn(da, kr[:, sl]) * e_qr[:, sl] + _nn(dov[:, sl], stb) * e_b[:, sl]
            dk_inter = _nn(v[:, sl], dstb) * e_ke[:, sl]
            dk_h = _tn(da, qr[:, sl]) * e_kr[:, sl] + dk_inter
            dq_parts.append(dq_h)
            dk_parts.append(dk_h)
            dki_parts.append(dk_inter)
            dbt_parts.append(dec[:, sl] * jnp.sum(st * dst, axis=0, keepdims=True))
            ds_scr[h] = dst * dec[:, sl] + _tn(dov[:, sl], qe[:, sl])
        dq = jnp.concatenate(dq_parts, axis=1)
        dk = jnp.concatenate(dk_parts, axis=1)
        dki = jnp.concatenate(dki_parts, axis=1)
        dv = jnp.concatenate(dv_parts, axis=1)
        dbt = jnp.concatenate(dbt_parts, axis=1) + jnp.sum(kk * dki, axis=0, keepdims=True)
        db = qh * dq - kk * dk
        dg = _cumsum_rows(db, not rev) + dbt
        df = dg / fg - dk
        dz_ref[...] = (df * (1.0 - lb) * sig * (1.0 - sig)).astype(BF16)
        dlb_ref[...] += jnp.sum(df * (1.0 - sig), axis=0, keepdims=True)
        dqr = dq * _dsilu(q_ref[...])
        if has_add:
            dqr = dqr + dqa_ref[...]
            dv = dv + dia_ref[...]
        dq_ref[...] = dqr
        di_ref[...] = dv

        @pl.when(m == n_all - 1)
        def _():
            if layer == 0:
                dlb_ref[...] = jnp.zeros_like(dlb_ref)
            else:
                dlb_ref[...] = dlb_ref[...] * lb * (1.0 - lb)

    cspec = lambda col: pl.BlockSpec((HG_CHUNK, D), lambda m: (chunk(m), col))
    ins = [parts, parts, parts, hlb, do, states]
    specs = [cspec(0), cspec(fcol), cspec(3), pl.BlockSpec((2, D), lambda m: (0, 1 if rev else 0)), cspec(0),
             pl.BlockSpec((None, nh, HEAD, HEAD), lambda m: (step(m), 0, 0, 0))]
    if has_add:
        ins += [dq_add, di_add]
        specs += [cspec(0), cspec(0)]
    return pl.pallas_call(
        body, name=name, grid=(n_all,), in_specs=specs,
        out_specs=[cspec(0), cspec(0), cspec(0), pl.BlockSpec((1, D), lambda m: (0, 0))],
        out_shape=[jax.ShapeDtypeStruct((T, D), F32), jax.ShapeDtypeStruct((T, D), BF16),
                   jax.ShapeDtypeStruct((T, D), F32), jax.ShapeDtypeStruct((1, D), F32)],
        scratch_shapes=[pltpu.VMEM((nh, HEAD, HEAD), F32)],
        compiler_params=_params("arbitrary"),
    )(*ins)


def _sgu_ln(v_ref, lnw_ref, lnb_ref):
    gv = _gelu(v_ref[...])
    mu = jnp.mean(gv, axis=-1, keepdims=True)
    xc = gv - mu
    rstd = lax.rsqrt(jnp.mean(xc * xc, axis=-1, keepdims=True) + LN_EPS)
    xh = xc * rstd
    return xh, rstd, xh * lnw_ref[...] + lnb_ref[...]


def _sgu_fwd(parts, lnw, lnb, w, bt, name):
    T = parts.shape[0]
    D = lnw.shape[1]
    G = D // HEAD

    def body(u_ref, v_ref, lnw_ref, lnb_ref, w_ref, bt_ref, ya_ref):
        gu = _gelu(u_ref[...])
        _, _, vn = _sgu_ln(v_ref, lnw_ref, lnb_ref)
        vnb = vn.astype(BF16)
        for g in range(G):
            sl = slice(g * HEAD, (g + 1) * HEAD)
            mixed = _nn(w_ref[g], vnb[:, sl]) + bt_ref[:, g:g + 1]
            ya_ref[:, sl] = (gu[:, sl] * mixed).astype(BF16)

    return pl.pallas_call(
        body, name=name, grid=(T // SGU_CHUNK,),
        in_specs=[pl.BlockSpec((SGU_CHUNK, D), lambda n: (n, 4)), pl.BlockSpec((SGU_CHUNK, D), lambda n: (n, 5)),
                  pl.BlockSpec((1, D), lambda n: (0, 0)), pl.BlockSpec((1, D), lambda n: (0, 0)),
                  pl.BlockSpec((G, SGU_CHUNK, SGU_CHUNK), lambda n: (0, 0, 0)),
                  pl.BlockSpec((SGU_CHUNK, G), lambda n: (0, 0))],
        out_specs=pl.BlockSpec((SGU_CHUNK, D), lambda n: (n, 0)),
        out_shape=jax.ShapeDtypeStruct((T, D), BF16),
        compiler_params=_params("parallel"),
    )(parts, parts, lnw, lnb, w, bt)


def _sgu_bwd(parts, dya, lnw, lnb, w, bt, name):
    T = parts.shape[0]
    D = lnw.shape[1]
    G = D // HEAD

    def body(u_ref, v_ref, dya_ref, lnw_ref, lnb_ref, w_ref, bt_ref,
             du_ref, dv_ref, dw_ref, dbt_ref, dlnw_ref, dlnb_ref, dvn_scr):
        n = pl.program_id(0)

        @pl.when(n == 0)
        def _():
            dw_ref[...] = jnp.zeros_like(dw_ref)
            dbt_ref[...] = jnp.zeros_like(dbt_ref)
            dlnw_ref[...] = jnp.zeros_like(dlnw_ref)
            dlnb_ref[...] = jnp.zeros_like(dlnb_ref)

        u = u_ref[...]
        gu = _gelu(u)
        xh, rstd, vn = _sgu_ln(v_ref, lnw_ref, lnb_ref)
        vnb = vn.astype(BF16)
        dya = dya_ref[...]
        lane = lax.broadcasted_iota(jnp.int32, (SGU_CHUNK, G), 1)
        dbt = jnp.zeros((SGU_CHUNK, G), F32)
        for g in range(G):
            sl = slice(g * HEAD, (g + 1) * HEAD)
            wg = w_ref[g]
            mixed = _nn(wg, vnb[:, sl]) + bt_ref[:, g:g + 1]
            dmix = dya[:, sl] * gu[:, sl]
            du_ref[:, sl] = (dya[:, sl] * mixed * _dgelu(u[:, sl])).astype(BF16)
            dmb = dmix.astype(BF16)
            dvn_scr[:, sl] = _tn(wg, dmb)
            dw_ref[g] += _nt(dmb, vnb[:, sl])
            dbt = dbt + jnp.where(lane == g, jnp.sum(dmix, axis=1, keepdims=True), 0.0)
        dbt_ref[...] += dbt
        dvn = dvn_scr[...]
        dlnw_ref[...] += jnp.sum(dvn * xh, axis=0, keepdims=True)
        dlnb_ref[...] += jnp.sum(dvn, axis=0, keepdims=True)
        dxh = dvn * lnw_ref[...]
        dgv = rstd * (dxh - jnp.mean(dxh, axis=-1, keepdims=True) - xh * jnp.mean(dxh * xh, axis=-1, keepdims=True))
        dv_ref[...] = (dgv * _dgelu(v_ref[...])).astype(BF16)

    row = lambda col: pl.BlockSpec((SGU_CHUNK, D), lambda n: (n, col))
    vec = pl.BlockSpec((1, D), lambda n: (0, 0))
    wsp = pl.BlockSpec((G, SGU_CHUNK, SGU_CHUNK), lambda n: (0, 0, 0))
    bsp = pl.BlockSpec((SGU_CHUNK, G), lambda n: (0, 0))
    return pl.pallas_call(
        body, name=name, grid=(T // SGU_CHUNK,),
        in_specs=[row(4), row(5), row(0), vec, vec, wsp, bsp],
        out_specs=[row(0), row(0), wsp, bsp, vec, vec],
        out_shape=[jax.ShapeDtypeStruct((T, D), BF16), jax.ShapeDtypeStruct((T, D), BF16),
                   jax.ShapeDtypeStruct((G, SGU_CHUNK, SGU_CHUNK), F32), jax.ShapeDtypeStruct((SGU_CHUNK, G), F32),
                   jax.ShapeDtypeStruct((1, D), F32), jax.ShapeDtypeStruct((1, D), F32)],
        scratch_shapes=[pltpu.VMEM((SGU_CHUNK, D), F32)],
        compiler_params=_params("arbitrary"),
    )(parts, parts, dya, lnw, lnb, w, bt)


TBT = 128


def _token_out_fwd(o, parts, ya, x, mod, hnw, wa, wb, wo, ctx_rows, name):
    T, D = x.shape
    nh = D // HEAD
    cb = ctx_rows // TBT

    def body(o_ref, og_ref, ga_ref, gb_ref, ya_ref, x_ref, mod_ref, hnw_ref, wa_ref, wb_ref, wo_ref,
             yb_ref, pa_ref, pb_ref, mg_ref, tmo_ref, xm_ref):
        ov = o_ref[...]
        so = _silu(og_ref[...])
        nw = hnw_ref[...]
        for h in range(nh):
            sl = slice(h * HEAD, (h + 1) * HEAD)
            seg = ov[:, sl]
            r = lax.rsqrt(jnp.mean(seg * seg, axis=-1, keepdims=True) + RMS_EPS)
            yb_ref[:, sl] = (seg * r * nw * so[:, sl]).astype(BF16)
        pa = _nn(ya_ref[...], wa_ref[...])
        pb = _nn(yb_ref[...], wb_ref[...])
        pa_ref[...] = pa
        pb_ref[...] = pb
        mg = (_sigmoid(ga_ref[...]) * pa + _sigmoid(gb_ref[...]) * pb).astype(BF16)
        mg_ref[...] = mg
        out = _nn(mg, wo_ref[...])
        tmo_ref[...] = out
        xm_ref[...] = x_ref[...] + mod_ref[2:3, :] * out

    row = lambda col: pl.BlockSpec((TBT, D), lambda i: (i, col))
    wsp = pl.BlockSpec((D, D), lambda i: (0, 0))
    sd = lambda dt: jax.ShapeDtypeStruct((T, D), dt)
    return pl.pallas_call(
        body, name=name, grid=(T // TBT,),
        in_specs=[row(0), row(6), row(7), row(8), row(0), row(0),
                  pl.BlockSpec((None, N_MOD, D), lambda i: (_stream_of(i, cb), 0, 0)),
                  pl.BlockSpec((1, HEAD), lambda i: (0, 0)), wsp, wsp, wsp],
        out_specs=[row(0)] * 6,
        out_shape=[sd(BF16), sd(F32), sd(F32), sd(BF16), sd(F32), sd(F32)],
        compiler_params=_params("parallel"),
    )(o, parts, parts, parts, ya, x, mod, hnw, wa, wb, wo)


def _token_out_bwd(dx, tmo, pa, pb, o, parts, mod, hnw, wa, wb, wo, ctx_rows, name):
    T, D = dx.shape
    nh = D // HEAD
    cb = ctx_rows // TBT

    def body(dx_ref, tmo_ref, pa_ref, pb_ref, o_ref, og_ref, ga_ref, gb_ref, mod_ref, hnw_ref, wa_ref, wb_ref, wo_ref,
             dout_ref, dpa_ref, dpb_ref, dog_ref, dga_ref, dgb_ref, dya_ref, do_ref, dg1_ref, dhnw_ref):
        i = pl.program_id(0)

        @pl.when(i == 0)
        def _():
            dhnw_ref[...] = jnp.zeros_like(dhnw_ref)

        @pl.when((i == 0) | (i == cb))
        def _():
            dg1_ref[...] = jnp.zeros_like(dg1_ref)

        dxv = dx_ref[...]
        dg1_ref[...] += jnp.sum(dxv * tmo_ref[...], axis=0, keepdims=True)
        dout = (dxv * mod_ref[2:3, :]).astype(BF16)
        dout_ref[...] = dout
        dmg = _nt(dout, wo_ref[...])
        sa = _sigmoid(ga_ref[...])
        sb = _sigmoid(gb_ref[...])
        dpa = (dmg * sa).astype(BF16)
        dpb = (dmg * sb).astype(BF16)
        dpa_ref[...] = dpa
        dpb_ref[...] = dpb
        dga_ref[...] = (dmg * pa_ref[...] * sa * (1.0 - sa)).astype(BF16)
        dgb_ref[...] = (dmg * pb_ref[...] * sb * (1.0 - sb)).astype(BF16)
        dya_ref[...] = _nt(dpa, wa_ref[...])
        dyb = _nt(dpb, wb_ref[...])
        og = og_ref[...]
        so = _silu(og)
        dso = _dsilu(og)
        ov = o_ref[...]
        nw = hnw_ref[...]
        dnw = jnp.zeros((1, HEAD), F32)
        for h in range(nh):
            sl = slice(h * HEAD, (h + 1) * HEAD)
            seg = ov[:, sl]
            r = lax.rsqrt(jnp.mean(seg * seg, axis=-1, keepdims=True) + RMS_EPS)
            oh = seg * r
            dn = dyb[:, sl] * so[:, sl]
            dog_ref[:, sl] = (dyb[:, sl] * oh * nw * dso[:, sl]).astype(BF16)
            dnw = dnw + jnp.sum(dn * oh, axis=0, keepdims=True)
            doh = dn * nw
            do_ref[:, sl] = r * (doh - oh * jnp.mean(doh * oh, axis=-1, keepdims=True))
        dhnw_ref[...] += dnw

    row = lambda col: pl.BlockSpec((TBT, D), lambda i: (i, col))
    wsp = pl.BlockSpec((D, D), lambda i: (0, 0))
    sd = lambda dt: jax.ShapeDtypeStruct((T, D), dt)
    return pl.pallas_call(
        body, name=name, grid=(T // TBT,),
        in_specs=[row(0), row(0), row(0), row(0), row(0), row(6), row(7), row(8),
                  pl.BlockSpec((None, N_MOD, D), lambda i: (_stream_of(i, cb), 0, 0)),
                  pl.BlockSpec((1, HEAD), lambda i: (0, 0)), wsp, wsp, wsp],
        out_specs=[row(0)] * 8 + [pl.BlockSpec((None, 1, D), lambda i: (_stream_of(i, cb), 0, 0)),
                                  pl.BlockSpec((1, HEAD), lambda i: (0, 0))],
        out_shape=[sd(BF16)] * 6 + [sd(F32), sd(F32), jax.ShapeDtypeStruct((2, 1, D), F32),
                                    jax.ShapeDtypeStruct((1, HEAD), F32)],
        compiler_params=_params("arbitrary"),
    )(dx, tmo, pa, pb, o, parts, parts, parts, mod, hnw, wa, wb, wo)


def _conv_geometry(i, nb, cb):
    is_ctx = i < cb
    first = (i == 0) | (i == cb)
    last = (i == cb - 1) | (i == nb - 1)
    row = lax.broadcasted_iota(jnp.int32, (TB + 2 * GRID_W, 1), 0)
    w = row & (GRID_W - 1)
    left_ok = (w != 0) | is_ctx
    right_ok = (w != GRID_W - 1) | is_ctx
    return is_ctx, first, last, left_ok, right_ok


def _ext(p_ref, m_ref, n_ref, first, last):
    return jnp.concatenate([jnp.where(first, 0.0, p_ref[...]), m_ref[...], jnp.where(last, 0.0, n_ref[...])], axis=0)


def _shift_prev(e, ok):
    return jnp.where(ok, pltpu.roll(e, 1, 0), 0.0)


def _shift_next(e, ok):
    return jnp.where(ok, pltpu.roll(e, e.shape[0] - 1, 0), 0.0)


def _halo_specs(cbk, n64):
    r = TB // GRID_W
    prev = pl.BlockSpec((GRID_W, cbk), lambda j, i: (jnp.maximum(r * i - 1, 0), j))
    main = pl.BlockSpec((TB, cbk), lambda j, i: (i, j))
    nxt = pl.BlockSpec((GRID_W, cbk), lambda j, i: (jnp.minimum(r * i + r, n64 - 1), j))
    return [prev, main, nxt]


def _conv_cblock(dff):
    return _tile(dff, 1408)


def _conv_fwd(ua, uv, cw, cbias, ctx_rows, name):
    T, dff = ua.shape
    cbk = _conv_cblock(dff)
    nb, cb = T // TB, ctx_rows // TB

    def body(ap_ref, a_ref, an_ref, v_ref, cw_ref, cb_ref, ac_ref, act_ref):
        i = pl.program_id(1)
        is_ctx, first, last, lok, rok = _conv_geometry(i, nb, cb)
        e = _ext(ap_ref, a_ref, an_ref, first, last)
        el = _shift_prev(e, lok)
        er = _shift_next(e, rok)
        cwv = cw_ref[...]

        def comb(dr, lo):
            sl = slice(lo, lo + TB)
            return cwv[3 * dr:3 * dr + 1] * el[sl] + cwv[3 * dr + 1:3 * dr + 2] * e[sl] + cwv[3 * dr + 2:3 * dr + 3] * er[sl]

        out = comb(1, GRID_W) + jnp.where(is_ctx, 0.0, comb(0, 0) + comb(2, 2 * GRID_W))
        a_c = out + cb_ref[...]
        ac_ref[...] = a_c
        act_ref[...] = (_gelu(a_c) * v_ref[...]).astype(BF16)

    main = pl.BlockSpec((TB, cbk), lambda j, i: (i, j))
    return pl.pallas_call(
        body, name=name, grid=(dff // cbk, nb),
        in_specs=_halo_specs(cbk, T // GRID_W) + [main, pl.BlockSpec((9, cbk), lambda j, i: (0, j)),
                                                 pl.BlockSpec((1, cbk), lambda j, i: (0, j))],
        out_specs=[main, main],
        out_shape=[jax.ShapeDtypeStruct((T, dff), F32), jax.ShapeDtypeStruct((T, dff), BF16)],
        compiler_params=_params("parallel", "parallel"),
    )(ua, ua, ua, uv, cw, cbias)


def _conv_bwd(ua, uv, ac, dact, cw, ctx_rows, name):
    T, dff = ua.shape
    cbk = _conv_cblock(dff)
    nb, cb = T // TB, ctx_rows // TB

    def body(ap_ref, a_ref, an_ref, vp_ref, v_ref, vn_ref, cp_ref, c_ref, cn_ref, dp_ref, d_ref, dn_ref, cw_ref,
             da_ref, dv_ref, dcw_ref, dcb_ref):
        i = pl.program_id(1)

        @pl.when(i == 0)
        def _():
            dcw_ref[...] = jnp.zeros_like(dcw_ref)
            dcb_ref[...] = jnp.zeros_like(dcb_ref)

        is_ctx, first, last, lok, rok = _conv_geometry(i, nb, cb)
        ace = _ext(cp_ref, c_ref, cn_ref, first, last)
        g = _ext(dp_ref, d_ref, dn_ref, first, last) * _ext(vp_ref, v_ref, vn_ref, first, last) * _dgelu(ace)
        dv_ref[...] = (d_ref[...] * _gelu(c_ref[...])).astype(BF16)
        gm = _shift_prev(g, lok)
        gp = _shift_next(g, rok)
        cwv = cw_ref[...]

        def comb(dr, lo):
            sl = slice(lo, lo + TB)
            return cwv[3 * dr:3 * dr + 1] * gp[sl] + cwv[3 * dr + 1:3 * dr + 2] * g[sl] + cwv[3 * dr + 2:3 * dr + 3] * gm[sl]

        da = comb(1, GRID_W) + jnp.where(is_ctx, 0.0, comb(0, 2 * GRID_W) + comb(2, 0))
        da_ref[...] = da.astype(BF16)
        e = _ext(ap_ref, a_ref, an_ref, first, last)
        taps = [_shift_prev(e, lok), e, _shift_next(e, rok)]
        gmain = g[GRID_W:GRID_W + TB]
        dcb_ref[...] += jnp.sum(gmain, axis=0, keepdims=True)
        vert = jnp.where(is_ctx, 0.0, 1.0)
        for dr in range(3):
            sl = slice(dr * GRID_W, dr * GRID_W + TB)
            for dw in range(3):
                s = jnp.sum(gmain * taps[dw][sl], axis=0, keepdims=True)
                if dr != 1:
                    s = s * vert
                k = 3 * dr + dw
                dcw_ref[k:k + 1, :] += s

    main = pl.BlockSpec((TB, cbk), lambda j, i: (i, j))
    halo = _halo_specs(cbk, T // GRID_W)
    acc9 = pl.BlockSpec((9, cbk), lambda j, i: (0, j))
    acc1 = pl.BlockSpec((1, cbk), lambda j, i: (0, j))
    return pl.pallas_call(
        body, name=name, grid=(dff // cbk, nb),
        in_specs=halo + halo + halo + halo + [acc9],
        out_specs=[main, main, acc9, acc1],
        out_shape=[jax.ShapeDtypeStruct((T, dff), BF16), jax.ShapeDtypeStruct((T, dff), BF16),
                   jax.ShapeDtypeStruct((9, dff), F32), jax.ShapeDtypeStruct((1, dff), F32)],
        compiler_params=_params("parallel", "arbitrary"),
    )(ua, ua, ua, uv, uv, uv, ac, ac, ac, dact, dact, dact, cw)


def _ffn_out_fwd(act, xm, mod, wd, ctx_rows, name):
    T, D = xm.shape
    dff = act.shape[1]
    cb = ctx_rows // TB

    def body(act_ref, x_ref, mod_ref, w_ref, xo_ref, fo_ref):
        out = _nn(act_ref[...], w_ref[...])
        fo_ref[...] = out
        xo_ref[...] = x_ref[...] + mod_ref[5:6, :] * out

    row = pl.BlockSpec((TB, D), lambda i: (i, 0))
    return pl.pallas_call(
        body, name=name, grid=(T // TB,),
        in_specs=[pl.BlockSpec((TB, dff), lambda i: (i, 0)), row,
                  pl.BlockSpec((None, N_MOD, D), lambda i: (_stream_of(i, cb), 0, 0)),
                  pl.BlockSpec((dff, D), lambda i: (0, 0))],
        out_specs=[row, row],
        out_shape=[jax.ShapeDtypeStruct((T, D), F32), jax.ShapeDtypeStruct((T, D), F32)],
        compiler_params=_params("parallel"),
    )(act, xm, mod, wd)


def _ffn_out_bwd(dx, fo, mod, wd, ctx_rows, name):
    T, D = dx.shape
    dff = wd.shape[0]
    cb = ctx_rows // TB

    def body(dx_ref, fo_ref, mod_ref, w_ref, dout_ref, dact_ref, dg2_ref):
        i = pl.program_id(0)

        @pl.when((i == 0) | (i == cb))
        def _():
            dg2_ref[...] = jnp.zeros_like(dg2_ref)

        dxv = dx_ref[...]
        dg2_ref[...] += jnp.sum(dxv * fo_ref[...], axis=0, keepdims=True)
        dout = (dxv * mod_ref[5:6, :]).astype(BF16)
        dout_ref[...] = dout
        dact_ref[...] = _nt(dout, w_ref[...])

    row = pl.BlockSpec((TB, D), lambda i: (i, 0))
    return pl.pallas_call(
        body, name=name, grid=(T // TB,),
        in_specs=[row, row, pl.BlockSpec((None, N_MOD, D), lambda i: (_stream_of(i, cb), 0, 0)),
                  pl.BlockSpec((dff, D), lambda i: (0, 0))],
        out_specs=[row, pl.BlockSpec((TB, dff), lambda i: (i, 0)),
                   pl.BlockSpec((None, 1, D), lambda i: (_stream_of(i, cb), 0, 0))],
        out_shape=[jax.ShapeDtypeStruct((T, D), BF16), jax.ShapeDtypeStruct((T, dff), F32),
                   jax.ShapeDtypeStruct((2, 1, D), F32)],
        compiler_params=_params("arbitrary"),
    )(dx, fo, mod, wd)


def _loss_bwd(x, target, fw, ctx_rows, name):
    T, D = x.shape
    cb = ctx_rows // TB

    def body(x_ref, t_ref, fw_ref, dx_ref, loss_ref, dfw_ref):
        i = pl.program_id(0)

        @pl.when(i == 0)
        def _():
            loss_ref[...] = jnp.zeros_like(loss_ref)
            dfw_ref[...] = jnp.zeros_like(dfw_ref)

        @pl.when(i < cb)
        def _():
            dx_ref[...] = jnp.zeros_like(dx_ref)

        @pl.when(i >= cb)
        def _():
            xv = x_ref[...]
            r = lax.rsqrt(jnp.mean(xv * xv, axis=-1, keepdims=True) + RMS_EPS)
            xh = xv * r
            fwv = fw_ref[...]
            err = xh * fwv - t_ref[...]
            loss_ref[...] += (0.5 / D) * jnp.sum(err * err)
            dy = err * (1.0 / D)
            dfw_ref[...] += jnp.sum(dy * xh, axis=0, keepdims=True)
            dxh = dy * fwv
            dx_ref[...] = r * (dxh - xh * jnp.mean(dxh * xh, axis=-1, keepdims=True))

    row = pl.BlockSpec((TB, D), lambda i: (i, 0))
    return pl.pallas_call(
        body, name=name, grid=(T // TB,),
        in_specs=[row, pl.BlockSpec((TB, D), lambda i: (jnp.maximum(i - cb, 0), 0)), pl.BlockSpec((1, D), lambda i: (0, 0))],
        out_specs=[row, pl.BlockSpec((1, 128), lambda i: (0, 0)), pl.BlockSpec((1, D), lambda i: (0, 0))],
        out_shape=[jax.ShapeDtypeStruct((T, D), F32), jax.ShapeDtypeStruct((1, 128), F32),
                   jax.ShapeDtypeStruct((1, D), F32)],
        compiler_params=_params("arbitrary"),
    )(x, target, fw)


def _adamw(w, g, m, v, name):
    R, C = w.shape
    rb = _rows_tile(R, max(16, (1 << 19) // C // 16 * 16))
    bc1 = 1.0 - ADAM_B1 ** ADAM_STEP
    bc2 = 1.0 - ADAM_B2 ** ADAM_STEP

    def body(w_ref, g_ref, m_ref, v_ref, d_ref, nm_ref, nv_ref):
        gv = g_ref[...]
        nm = ADAM_B1 * m_ref[...] + (1.0 - ADAM_B1) * gv
        nv = ADAM_B2 * v_ref[...] + (1.0 - ADAM_B2) * (gv * gv)
        nm_ref[...] = nm
        nv_ref[...] = nv
        d_ref[...] = -ADAM_LR * ((nm / bc1) / (jnp.sqrt(nv / bc2) + ADAM_EPS) + ADAM_WD * w_ref[...])

    blk = pl.BlockSpec((rb, C), lambda i: (i, 0))
    sd = jax.ShapeDtypeStruct((R, C), F32)
    return pl.pallas_call(
        body, name=name, grid=(R // rb,), in_specs=[blk] * 4, out_specs=[blk] * 3, out_shape=[sd] * 3,
        compiler_params=_params("parallel"),
    )(w, g, m, v)


def _local_step(xs, cv, target, W, ctx_rows):
    T, D = xs.shape
    depth = len(W["w_in"])
    saved = []
    X = xs
    for l in range(depth):
        s = {}
        mod_all, sa = _mod_fwd(cv, W["ada_w"][l], W["ada_b"][l][None, :], f"mod_fwd_{l}")
        mod = mod_all[:2].reshape(2, N_MOD, D)
        h1 = _norm_mod(X, W["norm1_w"][l][None, :], mod, 0, ctx_rows, f"norm1_{l}")
        parts = _mm_nn(h1, W["w_in"][l], F32, f"in_proj_{l}")
        o_f, st_f = _hgrn_fwd(parts, W["hlb"], l, False, ctx_rows, f"hgrn_fwd_f_{l}")
        o, st_b = _hgrn_fwd(parts, W["hlb"], l, True, ctx_rows, f"hgrn_fwd_b_{l}", o_add=o_f)
        ya = _sgu_fwd(parts, W["sgu_ln_w"][l][None, :], W["sgu_ln_b"][l][None, :], W["sgu_w"][l], W["sgu_bt"][l],
                      f"sgu_fwd_{l}")
        yb, pa, pb, mg, tmo, xm = _token_out_fwd(o, parts, ya, X, mod, W["hnw"][l][None, :], W["w_a"][l], W["w_b"][l],
                                                 W["w_o"][l], ctx_rows, f"token_out_fwd_{l}")
        h2 = _norm_mod(xm, W["norm2_w"][l][None, :], mod, 3, ctx_rows, f"norm2_{l}")
        ua = _mm_nn(h2, W["w_upa"][l], F32, f"up_a_{l}")
        uv = _mm_nn(h2, W["w_upv"][l], F32, f"up_v_{l}")
        ac, act = _conv_fwd(ua, uv, W["conv_w"][l], W["conv_b"][l][None, :], ctx_rows, f"conv_fwd_{l}")
        xo, fo = _ffn_out_fwd(act, xm, mod, W["w_down"][l], ctx_rows, f"ffn_out_fwd_{l}")
        s.update(X=X, mod=mod, mod_all=mod_all, sa=sa, h1=h1, parts=parts, o=o, st_f=st_f, st_b=st_b, ya=ya, yb=yb, pa=pa,
                 pb=pb, mg=mg, tmo=tmo, xm=xm, h2=h2, ua=ua, uv=uv, ac=ac, act=act, fo=fo)
        saved.append(s)
        X = xo

    dX, loss_row, dfw = _loss_bwd(X, target, W["final_norm_w"][None, :], ctx_rows, "loss_bwd")
    G = {k: [None] * depth for k in ("ada_w", "ada_b", "norm1_w", "w_in", "sgu_ln_w", "sgu_ln_b", "sgu_w", "sgu_b", "hlb1",
                                     "hnw", "w_a", "w_b", "w_o", "norm2_w", "w_upa", "w_upv", "conv_w", "conv_b", "w_down")}
    dcv = jnp.zeros_like(cv)
    for l in reversed(range(depth)):
        s = saved[l]
        mod = s["mod"]
        dout2, dact, dg2 = _ffn_out_bwd(dX, s["fo"], mod, W["w_down"][l], ctx_rows, f"ffn_out_bwd_{l}")
        G["w_down"][l] = _mm_tn(s["act"], dout2, F32, f"dw_down_{l}")
        da, dv, dcw, dcb = _conv_bwd(s["ua"], s["uv"], s["ac"], dact, W["conv_w"][l], ctx_rows, f"conv_bwd_{l}")
        G["conv_w"][l], G["conv_b"][l] = dcw, dcb[0]
        G["w_upa"][l] = _mm_tn(s["h2"], da, F32, f"dw_upa_{l}")
        G["w_upv"][l] = _mm_tn(s["h2"], dv, F32, f"dw_upv_{l}")
        dh2 = _mm_nt(da, W["w_upa"][l], F32, f"dh2_a_{l}")
        dh2 = _mm_nt(dv, W["w_upv"][l], F32, f"dh2_v_{l}", add=dh2)
        dxm, dm2, dnw2 = _norm_mod_bwd(dh2, s["xm"], dX, W["norm2_w"][l][None, :], mod, 3, ctx_rows, f"norm2_bwd_{l}")
        G["norm2_w"][l] = dnw2[0]
        (dout1, dpa, dpb, dog, dga, dgb, dya, do, dg1, dhnw) = _token_out_bwd(
            dxm, s["tmo"], s["pa"], s["pb"], s["o"], s["parts"], mod, W["hnw"][l][None, :], W["w_a"][l], W["w_b"][l],
            W["w_o"][l], ctx_rows, f"token_out_bwd_{l}")
        G["hnw"][l] = dhnw[0]
        G["w_o"][l] = _mm_tn(s["mg"], dout1, F32, f"dw_o_{l}")
        G["w_a"][l] = _mm_tn(s["ya"], dpa, F32, f"dw_a_{l}")
        G["w_b"][l] = _mm_tn(s["yb"], dpb, F32, f"dw_b_{l}")
        du, dvs, dsw, dsbt, dlnw, dlnb = _sgu_bwd(s["parts"], dya, W["sgu_ln_w"][l][None, :], W["sgu_ln_b"][l][None, :],
                                                  W["sgu_w"][l], W["sgu_bt"][l], f"sgu_bwd_{l}")
        G["sgu_w"][l], G["sgu_b"][l], G["sgu_ln_w"][l], G["sgu_ln_b"][l] = dsw, dsbt.T, dlnw[0], dlnb[0]
        dq_f, dz_f, di_f, dlb_f = _hgrn_bwd(s["parts"], W["hlb"], do, s["st_f"], l, False, ctx_rows, f"hgrn_bwd_f_{l}")
        dq, dz_b, di, dlb_b = _hgrn_bwd(s["parts"], W["hlb"], do, s["st_b"], l, True, ctx_rows, f"hgrn_bwd_b_{l}",
                                        dq_add=dq_f, di_add=di_f)
        G["hlb1"][l] = jnp.concatenate([dlb_f[0], dlb_b[0]])
        dparts = jnp.concatenate([dq.astype(BF16), dz_f, dz_b, di.astype(BF16), du, dvs, dog, dga, dgb], axis=1)
        G["w_in"][l] = _mm_tn(s["h1"], dparts, F32, f"dw_in_{l}")
        dh1 = _mm_nt(dparts, W["w_in"][l], F32, f"dh1_{l}")
        dX, dm1, dnw1 = _norm_mod_bwd(dh1, s["X"], dxm, W["norm1_w"][l][None, :], mod, 0, ctx_rows, f"norm1_bwd_{l}")
        G["norm1_w"][l] = dnw1[0]
        dmod = jnp.concatenate([dm1, dg1, dm2, dg2], axis=1).reshape(2, N_MOD * D)
        dmod16 = jnp.concatenate([dmod, jnp.zeros((cv.shape[0] - 2, N_MOD * D), F32)], axis=0)
        G["ada_b"][l] = dmod[0] + dmod[1]
        G["ada_w"][l] = _mm_tn(s["sa"], dmod16.astype(BF16), F32, f"dw_ada_{l}")
        dcv = dcv + _cvec_bwd(dmod16, W["ada_w"][l], cv, f"dcvec_{l}")
    G["c_ctx"] = dcv[0]
    G["final_norm_w"] = dfw[0]
    return loss_row[0, 0], dX, G


def _exchange(send, group, gather, name):
    P = N_CHIPS if group == "chips" else N_CORES
    out_shape = (P,) + (send.shape if gather else send.shape[1:])

    def body(send_ref, recv_ref, send_sems, recv_sems, local_sem):
        x, y, c = lax.axis_index("x"), lax.axis_index("y"), lax.axis_index("c")
        if group == "chips":
            me = 2 * x + y
            peers = [((1 - x, y, c), 2 * (1 - x) + y), ((x, 1 - y, c), 2 * x + 1 - y), ((1 - x, 1 - y, c), 2 * (1 - x) + 1 - y)]
        else:
            me = c
            peers = [((x, y, 1 - c), 1 - c)]
        src = (lambda p: send_ref) if gather else (lambda p: send_ref.at[p])
        own = pltpu.make_async_copy(src(me), recv_ref.at[me], local_sem)
        own.start()
        copies = []
        for k, (dev, idx) in enumerate(peers):
            cp = pltpu.make_async_remote_copy(src_ref=src(idx), dst_ref=recv_ref.at[me], send_sem=send_sems.at[k],
                                              recv_sem=recv_sems.at[k], device_id=dev, device_id_type=pl.DeviceIdType.MESH)
            cp.start()
            copies.append(cp)
        for cp in copies:
            cp.wait()
        own.wait()

    return pl.pallas_call(
        body, name=name, in_specs=[pl.BlockSpec(memory_space=pltpu.HBM)],
        out_specs=pl.BlockSpec(memory_space=pltpu.HBM), out_shape=jax.ShapeDtypeStruct(out_shape, send.dtype),
        scratch_shapes=[pltpu.SemaphoreType.DMA((P - 1,)), pltpu.SemaphoreType.DMA((P - 1,)), pltpu.SemaphoreType.DMA],
    )(send)


SUM_COLS = 1024


def _sum_slots(buf, out_dtype, name):
    P, S, R, C = buf.shape
    rb = _rows_tile(R, 256)

    def body(b_ref, o_ref):
        acc = b_ref[0].astype(F32)
        for p in range(1, P):
            acc = acc + b_ref[p].astype(F32)
        o_ref[...] = acc.astype(out_dtype)

    return pl.pallas_call(
        body, name=name, grid=(S, R // rb),
        in_specs=[pl.BlockSpec((P, None, rb, C), lambda s, i: (0, s, i, 0))],
        out_specs=pl.BlockSpec((None, rb, C), lambda s, i: (s, i, 0)),
        out_shape=jax.ShapeDtypeStruct((S, R, C), out_dtype),
        compiler_params=_params("parallel", "parallel"),
    )(buf)


_BIG = ("ada_w", "w_in", "w_branch_a", "w_branch_b", "w_out", "ffn_w_up", "ffn_conv_w", "ffn_w_down")
_BIG_AXIS = {"ada_w": 2, "w_in": 2, "w_branch_a": 1, "w_branch_b": 1, "w_out": 1, "ffn_w_up": 2, "ffn_conv_w": 3,
             "ffn_w_down": 1}
_SMALL = ("c_ctx", "ada_b", "norm1_w", "sgu_ln_w", "sgu_ln_b", "sgu_w", "sgu_b", "hgrn_lower_bounds", "hgrn_norm_w",
          "norm2_w", "ffn_conv_b", "final_norm_w")
_ORDER = ("c_ctx", "ada_w", "ada_b", "norm1_w", "w_in", "sgu_ln_w", "sgu_ln_b", "sgu_w", "sgu_b", "hgrn_lower_bounds",
          "hgrn_norm_w", "w_branch_a", "w_branch_b", "w_out", "norm2_w", "ffn_w_up", "ffn_conv_w", "ffn_conv_b",
          "ffn_w_down", "final_norm_w")


def _pad_to(v, n):
    return jnp.concatenate([v, jnp.zeros((n - v.shape[0],), v.dtype)]) if v.shape[0] < n else v


def _round_up(n, m):
    return (n + m - 1) // m * m


def _gather_weights(w):
    pieces = []
    for k in _BIG:
        if k == "ffn_conv_w":
            pieces.append(lax.bitcast_convert_type(w[k], BF16).reshape(-1))
        else:
            pieces.append(w[k].astype(BF16).reshape(-1))
    flat = jnp.concatenate(pieces)
    n = _round_up(flat.shape[0], 8 * 128)
    got = _exchange(_pad_to(flat, n).reshape(n // 128, 128), "chips", True, "gather_weights").reshape(N_CHIPS, n)
    full, off = {}, 0
    for k in _BIG:
        shp = w[k].shape
        if k == "ffn_conv_w":
            cnt = 2 * w[k].size
            sh = lax.bitcast_convert_type(got[:, off:off + cnt].reshape((N_CHIPS,) + shp + (2,)), F32)
        else:
            cnt = w[k].size
            sh = got[:, off:off + cnt].reshape((N_CHIPS,) + shp)
        off += cnt
        ax = _BIG_AXIS[k]
        full[k] = jnp.concatenate([sh[j] for j in range(N_CHIPS)], axis=ax)
    return full


def kernel(x, c, ctx, c_ctx, ada_w, ada_b, norm1_w, w_in, sgu_ln_w, sgu_ln_b, sgu_w, sgu_b, hgrn_lower_bounds, hgrn_norm_w, w_branch_a, w_branch_b, w_out, norm2_w, ffn_w_up, ffn_conv_w, ffn_conv_b, ffn_w_down, final_norm_w, loss_target, m_c_ctx, m_ada_w, m_ada_b, m_norm1_w, m_w_in, m_sgu_ln_w, m_sgu_ln_b, m_sgu_w, m_sgu_b, m_hgrn_lower_bounds, m_hgrn_norm_w, m_w_branch_a, m_w_branch_b, m_w_out, m_norm2_w, m_ffn_w_up, m_ffn_conv_w, m_ffn_conv_b, m_ffn_w_down, m_final_norm_w, v_c_ctx, v_ada_w, v_ada_b, v_norm1_w, v_w_in, v_sgu_ln_w, v_sgu_ln_b, v_sgu_w, v_sgu_b, v_hgrn_lower_bounds, v_hgrn_norm_w, v_w_branch_a, v_w_branch_b, v_w_out, v_norm2_w, v_ffn_w_up, v_ffn_conv_w, v_ffn_conv_b, v_ffn_w_down, v_final_norm_w):
    w = dict(c_ctx=c_ctx, ada_w=ada_w, ada_b=ada_b, norm1_w=norm1_w, w_in=w_in, sgu_ln_w=sgu_ln_w, sgu_ln_b=sgu_ln_b,
             sgu_w=sgu_w, sgu_b=sgu_b, hgrn_lower_bounds=hgrn_lower_bounds, hgrn_norm_w=hgrn_norm_w, w_branch_a=w_branch_a,
             w_branch_b=w_branch_b, w_out=w_out, norm2_w=norm2_w, ffn_w_up=ffn_w_up, ffn_conv_w=ffn_conv_w,
             ffn_conv_b=ffn_conv_b, ffn_w_down=ffn_w_down, final_norm_w=final_norm_w)
    mom = dict(zip(_ORDER, (m_c_ctx, m_ada_w, m_ada_b, m_norm1_w, m_w_in, m_sgu_ln_w, m_sgu_ln_b, m_sgu_w, m_sgu_b,
                            m_hgrn_lower_bounds, m_hgrn_norm_w, m_w_branch_a, m_w_branch_b, m_w_out, m_norm2_w, m_ffn_w_up,
                            m_ffn_conv_w, m_ffn_conv_b, m_ffn_w_down, m_final_norm_w)))
    var = dict(zip(_ORDER, (v_c_ctx, v_ada_w, v_ada_b, v_norm1_w, v_w_in, v_sgu_ln_w, v_sgu_ln_b, v_sgu_w, v_sgu_b,
                            v_hgrn_lower_bounds, v_hgrn_norm_w, v_w_branch_a, v_w_branch_b, v_w_out, v_norm2_w, v_ffn_w_up,
                            v_ffn_conv_w, v_ffn_conv_b, v_ffn_w_down, v_final_norm_w)))
    depth, D = norm1_w.shape
    dff = ffn_conv_b.shape[1]
    ctx_rows, seq = ctx.shape[1], x.shape[1]

    full = _gather_weights(w)
    W = dict(
        ada_w=[full["ada_w"][l] for l in range(depth)], ada_b=ada_b, norm1_w=norm1_w,
        w_in=[full["w_in"][l] for l in range(depth)], sgu_ln_w=sgu_ln_w, sgu_ln_b=sgu_ln_b,
        sgu_w=sgu_w.astype(BF16), sgu_bt=jnp.swapaxes(sgu_b, 1, 2), hlb=hgrn_lower_bounds, hnw=hgrn_norm_w,
        w_a=[full["w_branch_a"][l] for l in range(depth)], w_b=[full["w_branch_b"][l] for l in range(depth)],
        w_o=[full["w_out"][l] for l in range(depth)], norm2_w=norm2_w,
        w_upa=[full["ffn_w_up"][l][:, :dff] for l in range(depth)],
        w_upv=[full["ffn_w_up"][l][:, dff:] for l in range(depth)],
        conv_w=[full["ffn_conv_w"][l].reshape(9, dff) for l in range(depth)], conv_b=ffn_conv_b,
        w_down=[full["ffn_w_down"][l] for l in range(depth)], final_norm_w=final_norm_w)

    xs = jnp.concatenate([ctx[0], x[0]], axis=0)
    cv = jnp.concatenate([c_ctx[None, :], c, jnp.zeros((14, D), F32)], axis=0)
    loss_local, dxs, G = _local_step(xs, cv, loss_target[0], W, ctx_rows)
    loss = lax.psum(loss_local, ("x", "y", "c"))
    grad_x = dxs[ctx_rows:][None]

    dh = G["hlb1"][depth - 1]
    g_full = dict(
        c_ctx=G["c_ctx"], ada_w=jnp.stack(G["ada_w"]), ada_b=jnp.stack(G["ada_b"]), norm1_w=jnp.stack(G["norm1_w"]),
        w_in=jnp.stack(G["w_in"]), sgu_ln_w=jnp.stack(G["sgu_ln_w"]), sgu_ln_b=jnp.stack(G["sgu_ln_b"]),
        sgu_w=jnp.stack(G["sgu_w"]), sgu_b=jnp.stack(G["sgu_b"]), hgrn_lower_bounds=jnp.stack([-dh, dh]),
        hgrn_norm_w=jnp.stack(G["hnw"]), w_branch_a=jnp.stack(G["w_a"]), w_branch_b=jnp.stack(G["w_b"]),
        w_out=jnp.stack(G["w_o"]), norm2_w=jnp.stack(G["norm2_w"]),
        ffn_w_up=jnp.stack([jnp.concatenate([G["w_upa"][l], G["w_upv"][l]], axis=1) for l in range(depth)]),
        ffn_conv_w=jnp.stack([G["conv_w"][l].reshape(3, 3, dff) for l in range(depth)]),
        ffn_conv_b=jnp.stack(G["conv_b"]), ffn_w_down=jnp.stack(G["w_down"]), final_norm_w=G["final_norm_w"])

    small = jnp.concatenate([g_full[k].reshape(-1) for k in _SMALL])
    rows = []
    for j in range(N_CHIPS):
        segs = [jnp.split(g_full[k], N_CHIPS, axis=_BIG_AXIS[k])[j].reshape(-1) for k in _BIG]
        rows.append(jnp.concatenate(segs + [small]))
    n_flat = rows[0].shape[0]
    half = _round_up(n_flat, 2 * 256 * SUM_COLS) // 2
    hr = half // SUM_COLS
    flat = jnp.stack([_pad_to(r, 2 * half) for r in rows])
    by_half = jnp.swapaxes(flat.reshape(N_CHIPS, N_CORES, hr, SUM_COLS), 0, 1)
    pair = _exchange(by_half, "pair", False, "reduce_pair")
    pair_sum = _sum_slots(pair, BF16, "sum_pair")
    chips = _exchange(pair_sum, "chips", False, "reduce_chips")
    mine = _sum_slots(chips[:, None], F32, "sum_chips")
    both = _exchange(mine, "pair", True, "gather_pair")
    red = both.reshape(-1)[:n_flat]

    grads, off = {}, 0
    for k in _BIG:
        n = w[k].size
        grads[k] = red[off:off + n].reshape(w[k].shape)
        off += n
    for k in _SMALL:
        n = w[k].size
        grads[k] = red[off:off + n].reshape(w[k].shape)
        off += n

    delta, new_m, new_v = {}, {}, {}
    packed = tuple(k for k in _ORDER if k in _SMALL or k == "ffn_conv_w")
    for k in _ORDER:
        if k in packed:
            continue
        shp = w[k].shape
        two = (-1, shp[-1])
        d, nm, nv = _adamw(w[k].reshape(two), grads[k].reshape(two), mom[k].reshape(two), var[k].reshape(two), f"adamw_{k}")
        delta[k], new_m[k], new_v[k] = d.reshape(shp), nm.reshape(shp), nv.reshape(shp)
    n_packed = sum(w[k].size for k in packed)
    n_pad = _round_up(n_packed, 16 * SUM_COLS)
    pack = lambda t: _pad_to(jnp.concatenate([t[k].reshape(-1) for k in packed]), n_pad).reshape(-1, SUM_COLS)
    d, nm, nv = _adamw(pack(w), pack(grads), pack(mom), pack(var), "adamw_packed")
    off = 0
    for k in packed:
        n = w[k].size
        for src, dst in ((d, delta), (nm, new_m), (nv, new_v)):
            dst[k] = src.reshape(-1)[off:off + n].reshape(w[k].shape)
        off += n

    return (loss, grad_x, *[grads[k] for k in _ORDER], *[delta[k] for k in _ORDER], *[new_m[k] for k in _ORDER],
            *[new_v[k] for k in _ORDER])
```

```python
import functools

import jax
import jax.numpy as jnp
from jax import lax
from jax.experimental import pallas as pl
from jax.experimental.pallas import tpu as pltpu

F32 = jnp.float32
BF16 = jnp.bfloat16

GRID_W = 64
HG_CHUNK = 64
SGU_CHUNK = 128
HEAD = 128
TB = 256
N_MOD = 6
RMS_EPS = 1e-6
LN_EPS = 1e-5
VMEM_LIMIT = 48 * 1024 * 1024
N_CHIPS = 4
N_CORES = 2

ADAM_LR = 0.001
ADAM_B1 = 0.9
ADAM_B2 = 0.999
ADAM_EPS = 1e-08
ADAM_WD = 0.01
ADAM_STEP = 10

_GELU_C = 0.7978845608028654
_GELU_A = 0.044715


def _sigmoid(x):
    return 1.0 / (1.0 + jnp.exp(-x))


def _silu(x):
    return x * _sigmoid(x)


def _dsilu(x):
    s = _sigmoid(x)
    return s * (1.0 + x * (1.0 - s))


def _gelu(x):
    return 0.5 * x * (1.0 + jnp.tanh(_GELU_C * (x + _GELU_A * x * x * x)))


def _dgelu(x):
    t = jnp.tanh(_GELU_C * (x + _GELU_A * x * x * x))
    return 0.5 * (1.0 + t) + 0.5 * x * (1.0 - t * t) * _GELU_C * (1.0 + 3.0 * _GELU_A * x * x)


def _dot(a, b, ca, cb):
    return lax.dot_general(a, b, (((ca,), (cb,)), ((), ())), preferred_element_type=F32)


def _nn(a, b):
    return _dot(a, b, 1, 0)


def _nt(a, b):
    return _dot(a, b, 1, 1)


def _tn(a, b):
    return _dot(a, b, 0, 0)


def _params(*sem):
    return pltpu.CompilerParams(dimension_semantics=sem if sem else None, vmem_limit_bytes=VMEM_LIMIT)


def _stream_of(i, ctx_blocks):
    return (i >= ctx_blocks).astype(jnp.int32)


def _mm(a, b, mode, tm, tn, tk, out_dtype, name, add=None, b_chips=None, out_chips=False):
    if b_chips is None:
        bshape = b.shape
    else:
        bshape = (b.shape[2], N_CHIPS * b.shape[3])
    if mode == "nn":
        (M, K), (K2, N) = a.shape, bshape
    elif mode == "nt":
        (M, K), (N, K2) = a.shape, bshape
    else:
        (K, M), (K2, N) = a.shape, bshape
    assert K == K2 and M % tm == 0 and N % tn == 0 and K % tk == 0, (name, a.shape, b.shape, tm, tn, tk)
    nk = K // tk
    if mode == "tn":
        a_spec = pl.BlockSpec((tk, tm), lambda j, i, k: (k, i))
    else:
        a_spec = pl.BlockSpec((tm, tk), lambda j, i, k: (i, k))
    if b_chips is None:
        if mode == "nt":
            b_spec = pl.BlockSpec((tn, tk), lambda j, i, k: (j, k))
        else:
            b_spec = pl.BlockSpec((tk, tn), lambda j, i, k: (k, j))
    else:
        layer, cols = b_chips, b.shape[3]
        if mode == "nn":
            per = cols // tn
            assert cols % tn == 0
            b_spec = pl.BlockSpec((None, None, tk, tn), lambda j, i, k: (j // per, layer, k, j % per))
        else:
            per = cols // tk
            assert mode == "nt" and cols % tk == 0
            b_spec = pl.BlockSpec((None, None, tn, tk), lambda j, i, k: (k // per, layer, j, k % per))
    if out_chips:
        per_o = (N // N_CHIPS) // tn
        assert (N // N_CHIPS) % tn == 0 and add is None
        o_spec = pl.BlockSpec((None, tm, tn), lambda j, i, k: (j // per_o, i, j % per_o))
        o_shape = (N_CHIPS, M, N // N_CHIPS)
    else:
        o_spec = pl.BlockSpec((tm, tn), lambda j, i, k: (i, j))
        o_shape = (M, N)
    ca, cb = {"nn": (1, 0), "nt": (1, 1), "tn": (0, 0)}[mode]

    def body(a_ref, b_ref, *rest):
        if add is None:
            o_ref, acc = rest
        else:
            add_ref, o_ref, acc = rest
        k = pl.program_id(2)

        @pl.when(k == 0)
        def _():
            acc[...] = jnp.zeros_like(acc)

        acc[...] += _dot(a_ref[...], b_ref[...], ca, cb)

        @pl.when(k == nk - 1)
        def _():
            r = acc[...]
            if add is not None:
                r = r + add_ref[...]
            o_ref[...] = r.astype(out_dtype)

    ins = [a, b] + ([] if add is None else [add])
    specs = [a_spec, b_spec] + ([] if add is None else [o_spec])
    return pl.pallas_call(
        body, name=name, grid=(N // tn, M // tm, nk), in_specs=specs, out_specs=o_spec,
        out_shape=jax.ShapeDtypeStruct(o_shape, out_dtype),
        scratch_shapes=[pltpu.VMEM((tm, tn), F32)],
        compiler_params=_params("parallel", "parallel", "arbitrary"),
    )(*ins)


def _tile(n, pref):
    if n <= pref:
        return n
    best = None
    for t in range(128, pref + 1, 128):
        if n % t == 0:
            best = t
    assert best is not None, (n, pref)
    return best


def _rows_tile(n, pref):
    if n <= pref:
        return n
    best = None
    for t in range(16, pref + 1, 16):
        if n % t == 0:
            best = t
    assert best is not None, (n, pref)
    return best


def _mm_nn_w(a, wg, layer, out_dtype, name):
    M, K = a.shape
    return _mm(a, wg, "nn", _rows_tile(M, 512), _tile(wg.shape[3], 1536), _tile(K, 1536), out_dtype, name, b_chips=layer)


def _mm_nt_w(a, wg, layer, out_dtype, name):
    M, K = a.shape
    return _mm(a, wg, "nt", _rows_tile(M, 1088), _tile(wg.shape[2], 1024), _tile(wg.shape[3], 1536), out_dtype, name,
               b_chips=layer)


def _mm_tn(a, b, out_dtype, name, out_chips=False):
    K, M = a.shape
    N = b.shape[1]
    ncol = N // N_CHIPS if out_chips else N
    tm, tn = _tile(M, 1408), _tile(ncol, 1408)
    if tm * tn > 1408 * 1152:
        tn = _tile(ncol, 1152)
    return _mm(a, b, "tn", tm, tn, _rows_tile(K, 2176), out_dtype, name, out_chips=out_chips)


def _mod_fwd(cv, wg, layer, b, name):
    R, D = cv.shape
    tn = wg.shape[3]
    N = N_CHIPS * tn

    def body(cv_ref, w_ref, b_ref, mod_ref, sa_ref):
        sa = _silu(cv_ref[...]).astype(BF16)
        sa_ref[...] = sa
        mod_ref[...] = _nn(sa, w_ref[...]) + b_ref[...]

    return pl.pallas_call(
        body, name=name, grid=(N_CHIPS,),
        in_specs=[pl.BlockSpec((R, D), lambda j: (0, 0)), pl.BlockSpec((None, None, D, tn), lambda j: (j, layer, 0, 0)),
                  pl.BlockSpec((1, tn), lambda j: (0, j))],
        out_specs=[pl.BlockSpec((R, tn), lambda j: (0, j)), pl.BlockSpec((R, D), lambda j: (0, 0))],
        out_shape=[jax.ShapeDtypeStruct((R, N), F32), jax.ShapeDtypeStruct((R, D), BF16)],
        compiler_params=_params("arbitrary"),
    )(cv, wg, b)


def _cvec_bwd(dmod, wg, layer, cv, name):
    R, N = dmod.shape
    D = wg.shape[2]
    tk = wg.shape[3]
    nk = N_CHIPS

    def body(dm_ref, w_ref, cv_ref, o_ref):
        k = pl.program_id(0)

        @pl.when(k == 0)
        def _():
            o_ref[...] = jnp.zeros_like(o_ref)

        o_ref[...] += _nt(dm_ref[...].astype(BF16), w_ref[...])

        @pl.when(k == nk - 1)
        def _():
            o_ref[...] = o_ref[...] * _dsilu(cv_ref[...])

    return pl.pallas_call(
        body, name=name, grid=(nk,),
        in_specs=[pl.BlockSpec((R, tk), lambda k: (0, k)), pl.BlockSpec((None, None, D, tk), lambda k: (k, layer, 0, 0)),
                  pl.BlockSpec((R, D), lambda k: (0, 0))],
        out_specs=pl.BlockSpec((R, D), lambda k: (0, 0)),
        out_shape=jax.ShapeDtypeStruct((R, D), F32),
        compiler_params=_params("arbitrary"),
    )(dmod, wg, cv)


def _norm_mod(x, nw, mod, which, ctx_rows, name):
    T, D = x.shape
    cb = ctx_rows // TB

    def body(x_ref, nw_ref, mod_ref, h_ref):
        xv = x_ref[...]
        r = lax.rsqrt(jnp.mean(xv * xv, axis=-1, keepdims=True) + RMS_EPS)
        y = xv * r * nw_ref[...]
        sh = mod_ref[which:which + 1, :]
        sc = mod_ref[which + 1:which + 2, :]
        h_ref[...] = (y * (1.0 + sc) + sh).astype(BF16)

    return pl.pallas_call(
        body, name=name, grid=(T // TB,),
        in_specs=[pl.BlockSpec((TB, D), lambda i: (i, 0)), pl.BlockSpec((1, D), lambda i: (0, 0)),
                  pl.BlockSpec((None, N_MOD, D), lambda i: (_stream_of(i, cb), 0, 0))],
        out_specs=pl.BlockSpec((TB, D), lambda i: (i, 0)),
        out_shape=jax.ShapeDtypeStruct((T, D), BF16),
        compiler_params=_params("parallel"),
    )(x, nw, mod)


def _norm_mod_bwd(dh, x, dres, nw, mod, which, ctx_rows, name):
    T, D = x.shape
    cb = ctx_rows // TB

    def body(dh_ref, x_ref, dres_ref, nw_ref, mod_ref, dx_ref, dm_ref, dnw_ref):
        i = pl.program_id(0)

        @pl.when(i == 0)
        def _():
            dnw_ref[...] = jnp.zeros_like(dnw_ref)

        @pl.when((i == 0) | (i == cb))
        def _():
            dm_ref[...] = jnp.zeros_like(dm_ref)

        xv = x_ref[...]
        dh = dh_ref[...]
        r = lax.rsqrt(jnp.mean(xv * xv, axis=-1, keepdims=True) + RMS_EPS)
        xh = xv * r
        nwv = nw_ref[...]
        sc = mod_ref[which + 1:which + 2, :]
        y = xh * nwv
        dm_ref[0:1, :] += jnp.sum(dh, axis=0, keepdims=True)
        dm_ref[1:2, :] += jnp.sum(dh * y, axis=0, keepdims=True)
        dy = dh * (1.0 + sc)
        dnw_ref[...] += jnp.sum(dy * xh, axis=0, keepdims=True)
        dxh = dy * nwv
        dx_ref[...] = dres_ref[...] + r * (dxh - xh * jnp.mean(dxh * xh, axis=-1, keepdims=True))

    return pl.pallas_call(
        body, name=name, grid=(T // TB,),
        in_specs=[pl.BlockSpec((TB, D), lambda i: (i, 0)), pl.BlockSpec((TB, D), lambda i: (i, 0)),
                  pl.BlockSpec((TB, D), lambda i: (i, 0)), pl.BlockSpec((1, D), lambda i: (0, 0)),
                  pl.BlockSpec((None, N_MOD, D), lambda i: (_stream_of(i, cb), 0, 0))],
        out_specs=[pl.BlockSpec((TB, D), lambda i: (i, 0)),
                   pl.BlockSpec((None, 2, D), lambda i: (_stream_of(i, cb), 0, 0)),
                   pl.BlockSpec((1, D), lambda i: (0, 0))],
        out_shape=[jax.ShapeDtypeStruct((T, D), F32), jax.ShapeDtypeStruct((2, 2, D), F32),
                   jax.ShapeDtypeStruct((1, D), F32)],
        compiler_params=_params("arbitrary"),
    )(dh, x, dres, nw, mod)


def _scan_chunk(n, rev, n_ctx, n_all):
    if not rev:
        return n
    return jnp.where(n < n_ctx, n_ctx - 1 - n, n_all - 1 + n_ctx - n)


def _cumsum_rows(x, rev):
    rows = x.shape[0]
    row = lax.broadcasted_iota(jnp.int32, (rows, 1), 0)
    s = 1
    while s < rows:
        if not rev:
            x = x + jnp.where(row >= s, pltpu.roll(x, s, 0), 0.0)
        else:
            x = x + jnp.where(row < rows - s, pltpu.roll(x, rows - s, 0), 0.0)
        s *= 2
    return x


def _lower_bound(hlb_ref, layer):
    h = hlb_ref[...]
    if layer == 0:
        return jnp.zeros_like(h[0:1, :])
    return _sigmoid(h[1:2, :] - h[0:1, :])


def _hgrn_gates(q_ref, f_ref, hlb_ref, layer, rev):
    lb = _lower_bound(hlb_ref, layer)
    z = f_ref[...]
    sig = _sigmoid(z)
    fg = lb + (1.0 - lb) * sig
    kk = (1.0 - lb) * (1.0 - sig)
    g = jnp.log(fg)
    b = _cumsum_rows(g, rev)
    bt = jnp.sum(g, axis=0, keepdims=True)
    mid = HG_CHUNK // 2
    r = b[mid:mid + 1, :] if rev else b[mid - 1:mid, :]
    qh = _silu(q_ref[...])
    return lb, sig, fg, kk, b, bt, r, qh


def _tri_mask(rev):
    t = lax.broadcasted_iota(jnp.int32, (HG_CHUNK, HG_CHUNK), 0)
    s = lax.broadcasted_iota(jnp.int32, (HG_CHUNK, HG_CHUNK), 1)
    return (s >= t) if rev else (s <= t)


def _hgrn_fwd(parts, hlb, layer, rev, ctx_rows, name, o_add=None):
    T = parts.shape[0]
    D = hlb.shape[1] // 2
    nh = D // HEAD
    n_all, n_ctx = T // HG_CHUNK, ctx_rows // HG_CHUNK
    chunk = functools.partial(_scan_chunk, rev=rev, n_ctx=n_ctx, n_all=n_all)
    fcol = 2 if rev else 1

    def body(q_ref, f_ref, i_ref, hlb_ref, *rest):
        if o_add is None:
            o_ref, st_ref, s_scr = rest
        else:
            oa_ref, o_ref, st_ref, s_scr = rest
        n = pl.program_id(0)

        @pl.when(n == 0)
        def _():
            s_scr[...] = jnp.zeros_like(s_scr)

        lb, sig, fg, kk, b, bt, r, qh = _hgrn_gates(q_ref, f_ref, hlb_ref, layer, rev)
        qr = (qh * jnp.exp(b - r)).astype(BF16)
        kr = (kk * jnp.exp(r - b)).astype(BF16)
        qe = (qh * jnp.exp(b)).astype(BF16)
        ke = (kk * jnp.exp(bt - b)).astype(BF16)
        dec = jnp.exp(bt)
        v = i_ref[...].astype(BF16)
        mask = _tri_mask(rev)
        for h in range(nh):
            sl = slice(h * HEAD, (h + 1) * HEAD)
            st = s_scr[h]
            st_ref[h] = st
            a = jnp.where(mask, _nt(qr[:, sl], kr[:, sl]), 0.0).astype(BF16)
            o = _nn(a, v[:, sl]) + _nt(qe[:, sl], st.astype(BF16))
            if o_add is not None:
                o = o + oa_ref[:, sl]
            o_ref[:, sl] = o
            s_scr[h] = st * dec[:, sl] + _tn(v[:, sl], ke[:, sl])

    cspec = lambda col: pl.BlockSpec((HG_CHUNK, D), lambda n: (chunk(n), col))
    ins = [parts, parts, parts, hlb]
    specs = [cspec(0), cspec(fcol), cspec(3), pl.BlockSpec((2, D), lambda n: (0, 1 if rev else 0))]
    if o_add is not None:
        ins.append(o_add)
        specs.append(cspec(0))
    return pl.pallas_call(
        body, name=name, grid=(n_all,), in_specs=specs,
        out_specs=[cspec(0), pl.BlockSpec((None, nh, HEAD, HEAD), lambda n: (n, 0, 0, 0))],
        out_shape=[jax.ShapeDtypeStruct((T, D), F32), jax.ShapeDtypeStruct((n_all, nh, HEAD, HEAD), F32)],
        scratch_shapes=[pltpu.VMEM((nh, HEAD, HEAD), F32)],
        compiler_params=_params("arbitrary"),
    )(*ins)


def _hgrn_bwd(parts, hlb, do, states, layer, rev, ctx_rows, name, dq_add=None, di_add=None):
    T = parts.shape[0]
    D = hlb.shape[1] // 2
    nh = D // HEAD
    n_all, n_ctx = T // HG_CHUNK, ctx_rows // HG_CHUNK
    step = lambda m: n_all - 1 - m
    chunk = lambda m: _scan_chunk(step(m), rev, n_ctx, n_all)
    fcol = 2 if rev else 1
    has_add = dq_add is not None

    def body(q_ref, f_ref, i_ref, hlb_ref, do_ref, st_ref, *rest):
        if has_add:
            dqa_ref, dia_ref, dq_ref, dz_ref, di_ref, dlb_ref, ds_scr = rest
        else:
            dq_ref, dz_ref, di_ref, dlb_ref, ds_scr = rest
        m = pl.program_id(0)

        @pl.when(m == 0)
        def _():
            ds_scr[...] = jnp.zeros_like(ds_scr)
            dlb_ref[...] = jnp.zeros_like(dlb_ref)

        lb, sig, fg, kk, b, bt, r, qh = _hgrn_gates(q_ref, f_ref, hlb_ref, layer, rev)
        e_qr = jnp.exp(b - r)
        e_kr = jnp.exp(r - b)
        e_b = jnp.exp(b)
        e_ke = jnp.exp(bt - b)
        dec = jnp.exp(bt)
        qr = (qh * e_qr).astype(BF16)
        kr = (kk * e_kr).astype(BF16)
        qe = (qh * e_b).astype(BF16)
        ke = (kk * e_ke).astype(BF16)
        vf = i_ref[...]
        v = vf.astype(BF16)
        dov = do_ref[...].astype(BF16)
        mask = _tri_mask(rev)
        dq_parts, dk_parts, dki_parts, dv_parts, dbt_parts = [], [], [], [], []
        for h in range(nh):
            sl = slice(h * HEAD, (h + 1) * HEAD)
            st = st_ref[h]
            stb = st.astype(BF16)
            dst = ds_scr[h]
            dstb = dst.astype(BF16)
            a = jnp.where(mask, _nt(qr[:, sl], kr[:, sl]), 0.0).astype(BF16)
            da = jnp.where(mask, _nt(dov[:, sl], v[:, sl]), 0.0).astype(BF16)
            dv_parts.append(_tn(a, dov[:, sl]) + _nt(ke[:, sl], dstb))
            dq_h = _nn(da, kr[:, sl]) * e_qr[:, sl] + _nn(dov[:, sl], stb) * e_b[:, sl]
            dk_inter = _nn(v[:, sl], dstb) * e_ke[:, sl]
            dk_h = _tn(da, qr[:, sl]) * e_kr[:, sl] + dk_inter
            dq_parts.append(dq_h)
            dk_parts.append(dk_h)
            dki_parts.append(dk_inter)
            dbt_parts.append(dec[:, sl] * jnp.sum(st * dst, axis=0, keepdims=True))
            ds_scr[h] = dst * dec[:, sl] + _tn(dov[:, sl], qe[:, sl])
        dq = jnp.concatenate(dq_parts, axis=1)
        dk = jnp.concatenate(dk_parts, axis=1)
        dki = jnp.concatenate(dki_parts, axis=1)
        dv = jnp.concatenate(dv_parts, axis=1)
        dbt = jnp.concatenate(dbt_parts, axis=1) + jnp.sum(kk * dki, axis=0, keepdims=True)
        db = qh * dq - kk * dk
        dg = _cumsum_rows(db, not rev) + dbt
        df = dg / fg - dk
        dz_ref[...] = (df * (1.0 - lb) * sig * (1.0 - sig)).astype(BF16)
        dlb_ref[...] += jnp.sum(df * (1.0 - sig), axis=0, keepdims=True)
        dqr = dq * _dsilu(q_ref[...])
        if has_add:
            dqr = dqr + dqa_ref[...]
            dv = dv + dia_ref[...]
        dq_ref[...] = dqr
        di_ref[...] = dv

        @pl.when(m == n_all - 1)
        def _():
            if layer == 0:
                dlb_ref[...] = jnp.zeros_like(dlb_ref)
            else:
                dlb_ref[...] = dlb_ref[...] * lb * (1.0 - lb)

    cspec = lambda col: pl.BlockSpec((HG_CHUNK, D), lambda m: (chunk(m), col))
    ins = [parts, parts, parts, hlb, do, states]
    specs = [cspec(0), cspec(fcol), cspec(3), pl.BlockSpec((2, D), lambda m: (0, 1 if rev else 0)), cspec(0),
             pl.BlockSpec((None, nh, HEAD, HEAD), lambda m: (step(m), 0, 0, 0))]
    if has_add:
        ins += [dq_add, di_add]
        specs += [cspec(0), cspec(0)]
    return pl.pallas_call(
        body, name=name, grid=(n_all,), in_specs=specs,
        out_specs=[cspec(0), cspec(0), cspec(0), pl.BlockSpec((1, D), lambda m: (0, 0))],
        out_shape=[jax.ShapeDtypeStruct((T, D), F32), jax.ShapeDtypeStruct((T, D), BF16),
                   jax.ShapeDtypeStruct((T, D), F32), jax.ShapeDtypeStruct((1, D), F32)],
        scratch_shapes=[pltpu.VMEM((nh, HEAD, HEAD), F32)],
        compiler_params=_params("arbitrary"),
    )(*ins)


def _sgu_ln(v_ref, lnw_ref, lnb_ref):
    gv = _gelu(v_ref[...])
    mu = jnp.mean(gv, axis=-1, keepdims=True)
    xc = gv - mu
    rstd = lax.rsqrt(jnp.mean(xc * xc, axis=-1, keepdims=True) + LN_EPS)
    xh = xc * rstd
    return xh, rstd, xh * lnw_ref[...] + lnb_ref[...]


def _sgu_fwd(parts, lnw, lnb, w, bt, name):
    T = parts.shape[0]
    D = lnw.shape[1]
    G = D // HEAD

    def body(u_ref, v_ref, lnw_ref, lnb_ref, w_ref, bt_ref, ya_ref):
        gu = _gelu(u_ref[...])
        _, _, vn = _sgu_ln(v_ref, lnw_ref, lnb_ref)
        vnb = vn.astype(BF16)
        for g in range(G):
            sl = slice(g * HEAD, (g + 1) * HEAD)
            mixed = _nn(w_ref[g], vnb[:, sl]) + bt_ref[:, g:g + 1]
            ya_ref[:, sl] = (gu[:, sl] * mixed).astype(BF16)

    return pl.pallas_call(
        body, name=name, grid=(T // SGU_CHUNK,),
        in_specs=[pl.BlockSpec((SGU_CHUNK, D), lambda n: (n, 4)), pl.BlockSpec((SGU_CHUNK, D), lambda n: (n, 5)),
                  pl.BlockSpec((1, D), lambda n: (0, 0)), pl.BlockSpec((1, D), lambda n: (0, 0)),
                  pl.BlockSpec((G, SGU_CHUNK, SGU_CHUNK), lambda n: (0, 0, 0)),
                  pl.BlockSpec((SGU_CHUNK, G), lambda n: (0, 0))],
        out_specs=pl.BlockSpec((SGU_CHUNK, D), lambda n: (n, 0)),
        out_shape=jax.ShapeDtypeStruct((T, D), BF16),
        compiler_params=_params("parallel"),
    )(parts, parts, lnw, lnb, w, bt)


def _sgu_bwd(parts, dya, lnw, lnb, w, bt, name):
    T = parts.shape[0]
    D = lnw.shape[1]
    G = D // HEAD

    def body(u_ref, v_ref, dya_ref, lnw_ref, lnb_ref, w_ref, bt_ref,
             du_ref, dv_ref, dw_ref, dbt_ref, dlnw_ref, dlnb_ref, dvn_scr):
        n = pl.program_id(0)

        @pl.when(n == 0)
        def _():
            dw_ref[...] = jnp.zeros_like(dw_ref)
            dbt_ref[...] = jnp.zeros_like(dbt_ref)
            dlnw_ref[...] = jnp.zeros_like(dlnw_ref)
            dlnb_ref[...] = jnp.zeros_like(dlnb_ref)

        u = u_ref[...]
        gu = _gelu(u)
        xh, rstd, vn = _sgu_ln(v_ref, lnw_ref, lnb_ref)
        vnb = vn.astype(BF16)
        dya = dya_ref[...]
        lane = lax.broadcasted_iota(jnp.int32, (SGU_CHUNK, G), 1)
        dbt = jnp.zeros((SGU_CHUNK, G), F32)
        for g in range(G):
            sl = slice(g * HEAD, (g + 1) * HEAD)
            wg = w_ref[g]
            mixed = _nn(wg, vnb[:, sl]) + bt_ref[:, g:g + 1]
            dmix = dya[:, sl] * gu[:, sl]
            du_ref[:, sl] = (dya[:, sl] * mixed * _dgelu(u[:, sl])).astype(BF16)
            dmb = dmix.astype(BF16)
            dvn_scr[:, sl] = _tn(wg, dmb)
            dw_ref[g] += _nt(dmb, vnb[:, sl])
            dbt = dbt + jnp.where(lane == g, jnp.sum(dmix, axis=1, keepdims=True), 0.0)
        dbt_ref[...] += dbt
        dvn = dvn_scr[...]
        dlnw_ref[...] += jnp.sum(dvn * xh, axis=0, keepdims=True)
        dlnb_ref[...] += jnp.sum(dvn, axis=0, keepdims=True)
        dxh = dvn * lnw_ref[...]
        dgv = rstd * (dxh - jnp.mean(dxh, axis=-1, keepdims=True) - xh * jnp.mean(dxh * xh, axis=-1, keepdims=True))
        dv_ref[...] = (dgv * _dgelu(v_ref[...])).astype(BF16)

    row = lambda col: pl.BlockSpec((SGU_CHUNK, D), lambda n: (n, col))
    vec = pl.BlockSpec((1, D), lambda n: (0, 0))
    wsp = pl.BlockSpec((G, SGU_CHUNK, SGU_CHUNK), lambda n: (0, 0, 0))
    bsp = pl.BlockSpec((SGU_CHUNK, G), lambda n: (0, 0))
    return pl.pallas_call(
        body, name=name, grid=(T // SGU_CHUNK,),
        in_specs=[row(4), row(5), row(0), vec, vec, wsp, bsp],
        out_specs=[row(0), row(0), wsp, bsp, vec, vec],
        out_shape=[jax.ShapeDtypeStruct((T, D), BF16), jax.ShapeDtypeStruct((T, D), BF16),
                   jax.ShapeDtypeStruct((G, SGU_CHUNK, SGU_CHUNK), F32), jax.ShapeDtypeStruct((SGU_CHUNK, G), F32),
                   jax.ShapeDtypeStruct((1, D), F32), jax.ShapeDtypeStruct((1, D), F32)],
        scratch_shapes=[pltpu.VMEM((SGU_CHUNK, D), F32)],
        compiler_params=_params("arbitrary"),
    )(parts, parts, dya, lnw, lnb, w, bt)


TBT = 128


def _rows_weight_spec(wg, layer):
    return pl.BlockSpec((N_CHIPS, None) + wg.shape[2:], lambda i: (0, layer, 0, 0))


def _full(w_ref):
    return w_ref[...].reshape(w_ref.shape[0] * w_ref.shape[1], w_ref.shape[2])


def _token_out_fwd(o, parts, ya, x, mod, hnw, wa, wb, wo, layer, ctx_rows, name):
    T, D = x.shape
    nh = D // HEAD
    cb = ctx_rows // TBT

    def body(o_ref, og_ref, ga_ref, gb_ref, ya_ref, x_ref, mod_ref, hnw_ref, wa_ref, wb_ref, wo_ref,
             yb_ref, pa_ref, pb_ref, mg_ref, tmo_ref, xm_ref):
        ov = o_ref[...]
        so = _silu(og_ref[...])
        nw = hnw_ref[...]
        for h in range(nh):
            sl = slice(h * HEAD, (h + 1) * HEAD)
            seg = ov[:, sl]
            r = lax.rsqrt(jnp.mean(seg * seg, axis=-1, keepdims=True) + RMS_EPS)
            yb_ref[:, sl] = (seg * r * nw * so[:, sl]).astype(BF16)
        pa = _nn(ya_ref[...], _full(wa_ref))
        pb = _nn(yb_ref[...], _full(wb_ref))
        pa_ref[...] = pa
        pb_ref[...] = pb
        mg = (_sigmoid(ga_ref[...]) * pa + _sigmoid(gb_ref[...]) * pb).astype(BF16)
        mg_ref[...] = mg
        out = _nn(mg, _full(wo_ref))
        tmo_ref[...] = out
        xm_ref[...] = x_ref[...] + mod_ref[2:3, :] * out

    row = lambda col: pl.BlockSpec((TBT, D), lambda i: (i, col))
    wsp = _rows_weight_spec(wa, layer)
    sd = lambda dt: jax.ShapeDtypeStruct((T, D), dt)
    return pl.pallas_call(
        body, name=name, grid=(T // TBT,),
        in_specs=[row(0), row(6), row(7), row(8), row(0), row(0),
                  pl.BlockSpec((None, N_MOD, D), lambda i: (_stream_of(i, cb), 0, 0)),
                  pl.BlockSpec((1, HEAD), lambda i: (0, 0)), wsp, wsp, wsp],
        out_specs=[row(0)] * 6,
        out_shape=[sd(BF16), sd(F32), sd(F32), sd(BF16), sd(F32), sd(F32)],
        compiler_params=_params("parallel"),
    )(o, parts, parts, parts, ya, x, mod, hnw, wa, wb, wo)


def _token_out_bwd(dx, tmo, pa, pb, o, parts, mod, hnw, wa, wb, wo, layer, ctx_rows, name):
    T, D = dx.shape
    nh = D // HEAD
    cb = ctx_rows // TBT

    def body(dx_ref, tmo_ref, pa_ref, pb_ref, o_ref, og_ref, ga_ref, gb_ref, mod_ref, hnw_ref, wa_ref, wb_ref, wo_ref,
             dout_ref, dpa_ref, dpb_ref, dog_ref, dga_ref, dgb_ref, dya_ref, do_ref, dg1_ref, dhnw_ref):
        i = pl.program_id(0)

        @pl.when(i == 0)
        def _():
            dhnw_ref[...] = jnp.zeros_like(dhnw_ref)

        @pl.when((i == 0) | (i == cb))
        def _():
            dg1_ref[...] = jnp.zeros_like(dg1_ref)

        dxv = dx_ref[...]
        dg1_ref[...] += jnp.sum(dxv * tmo_ref[...], axis=0, keepdims=True)
        dout = (dxv * mod_ref[2:3, :]).astype(BF16)
        dout_ref[...] = dout
        dmg = _nt(dout, _full(wo_ref))
        sa = _sigmoid(ga_ref[...])
        sb = _sigmoid(gb_ref[...])
        dpa = (dmg * sa).astype(BF16)
        dpb = (dmg * sb).astype(BF16)
        dpa_ref[...] = dpa
        dpb_ref[...] = dpb
        dga_ref[...] = (dmg * pa_ref[...] * sa * (1.0 - sa)).astype(BF16)
        dgb_ref[...] = (dmg * pb_ref[...] * sb * (1.0 - sb)).astype(BF16)
        dya_ref[...] = _nt(dpa, _full(wa_ref))
        dyb = _nt(dpb, _full(wb_ref))
        og = og_ref[...]
        so = _silu(og)
        dso = _dsilu(og)
        ov = o_ref[...]
        nw = hnw_ref[...]
        dnw = jnp.zeros((1, HEAD), F32)
        for h in range(nh):
            sl = slice(h * HEAD, (h + 1) * HEAD)
            seg = ov[:, sl]
            r = lax.rsqrt(jnp.mean(seg * seg, axis=-1, keepdims=True) + RMS_EPS)
            oh = seg * r
            dn = dyb[:, sl] * so[:, sl]
            dog_ref[:, sl] = (dyb[:, sl] * oh * nw * dso[:, sl]).astype(BF16)
            dnw = dnw + jnp.sum(dn * oh, axis=0, keepdims=True)
            doh = dn * nw
            do_ref[:, sl] = r * (doh - oh * jnp.mean(doh * oh, axis=-1, keepdims=True))
        dhnw_ref[...] += dnw

    row = lambda col: pl.BlockSpec((TBT, D), lambda i: (i, col))
    wsp = _rows_weight_spec(wa, layer)
    sd = lambda dt: jax.ShapeDtypeStruct((T, D), dt)
    return pl.pallas_call(
        body, name=name, grid=(T // TBT,),
        in_specs=[row(0), row(0), row(0), row(0), row(0), row(6), row(7), row(8),
                  pl.BlockSpec((None, N_MOD, D), lambda i: (_stream_of(i, cb), 0, 0)),
                  pl.BlockSpec((1, HEAD), lambda i: (0, 0)), wsp, wsp, wsp],
        out_specs=[row(0)] * 8 + [pl.BlockSpec((None, 1, D), lambda i: (_stream_of(i, cb), 0, 0)),
                                  pl.BlockSpec((1, HEAD), lambda i: (0, 0))],
        out_shape=[sd(BF16)] * 6 + [sd(F32), sd(F32), jax.ShapeDtypeStruct((2, 1, D), F32),
                                    jax.ShapeDtypeStruct((1, HEAD), F32)],
        compiler_params=_params("arbitrary"),
    )(dx, tmo, pa, pb, o, parts, parts, parts, mod, hnw, wa, wb, wo)


def _conv_geometry(i, nb, cb):
    is_ctx = i < cb
    first = (i == 0) | (i == cb)
    last = (i == cb - 1) | (i == nb - 1)
    row = lax.broadcasted_iota(jnp.int32, (TB + 2 * GRID_W, 1), 0)
    w = row & (GRID_W - 1)
    left_ok = (w != 0) | is_ctx
    right_ok = (w != GRID_W - 1) | is_ctx
    return is_ctx, first, last, left_ok, right_ok


def _ext(p_ref, m_ref, n_ref, first, last):
    return jnp.concatenate([jnp.where(first, 0.0, p_ref[...]), m_ref[...], jnp.where(last, 0.0, n_ref[...])], axis=0)


def _shift_prev(e, ok):
    return jnp.where(ok, pltpu.roll(e, 1, 0), 0.0)


def _shift_next(e, ok):
    return jnp.where(ok, pltpu.roll(e, e.shape[0] - 1, 0), 0.0)


def _halo_specs(cbk, n64, coff=0):
    r = TB // GRID_W
    prev = pl.BlockSpec((GRID_W, cbk), lambda j, i: (jnp.maximum(r * i - 1, 0), j + coff))
    main = pl.BlockSpec((TB, cbk), lambda j, i: (i, j + coff))
    nxt = pl.BlockSpec((GRID_W, cbk), lambda j, i: (jnp.minimum(r * i + r, n64 - 1), j + coff))
    return [prev, main, nxt]


def _conv_cblock(dff):
    return _tile(dff, 1408)


def _conv_fwd(up, cw, cbias, ctx_rows, name):
    T, dff = up.shape[0], up.shape[1] // 2
    cbk = _conv_cblock(dff)
    nb, cb = T // TB, ctx_rows // TB
    nvb = dff // cbk

    def body(ap_ref, a_ref, an_ref, v_ref, cw_ref, cb_ref, ac_ref, act_ref):
        i = pl.program_id(1)
        is_ctx, first, last, lok, rok = _conv_geometry(i, nb, cb)
        e = _ext(ap_ref, a_ref, an_ref, first, last)
        el = _shift_prev(e, lok)
        er = _shift_next(e, rok)
        cwv = cw_ref[...]

        def comb(dr, lo):
            sl = slice(lo, lo + TB)
            return cwv[3 * dr:3 * dr + 1] * el[sl] + cwv[3 * dr + 1:3 * dr + 2] * e[sl] + cwv[3 * dr + 2:3 * dr + 3] * er[sl]

        out = comb(1, GRID_W) + jnp.where(is_ctx, 0.0, comb(0, 0) + comb(2, 2 * GRID_W))
        a_c = out + cb_ref[...]
        ac_ref[...] = a_c
        act_ref[...] = (_gelu(a_c) * v_ref[...]).astype(BF16)

    main = pl.BlockSpec((TB, cbk), lambda j, i: (i, j))
    return pl.pallas_call(
        body, name=name, grid=(dff // cbk, nb),
        in_specs=_halo_specs(cbk, T // GRID_W) + [pl.BlockSpec((TB, cbk), lambda j, i: (i, j + nvb)),
                                                 pl.BlockSpec((9, cbk), lambda j, i: (0, j)),
                                                 pl.BlockSpec((1, cbk), lambda j, i: (0, j))],
        out_specs=[main, main],
        out_shape=[jax.ShapeDtypeStruct((T, dff), F32), jax.ShapeDtypeStruct((T, dff), BF16)],
        compiler_params=_params("parallel", "parallel"),
    )(up, up, up, up, cw, cbias)


def _conv_bwd(up, ac, dact, cw, ctx_rows, name):
    T, dff = up.shape[0], up.shape[1] // 2
    cbk = _conv_cblock(dff)
    nb, cb = T // TB, ctx_rows // TB
    nvb = dff // cbk

    def body(ap_ref, a_ref, an_ref, vp_ref, v_ref, vn_ref, cp_ref, c_ref, cn_ref, dp_ref, d_ref, dn_ref, cw_ref,
             da_ref, dv_ref, dcw_ref, dcb_ref):
        i = pl.program_id(1)

        @pl.when(i == 0)
        def _():
            dcw_ref[...] = jnp.zeros_like(dcw_ref)
            dcb_ref[...] = jnp.zeros_like(dcb_ref)

        is_ctx, first, last, lok, rok = _conv_geometry(i, nb, cb)
        ace = _ext(cp_ref, c_ref, cn_ref, first, last)
        g = _ext(dp_ref, d_ref, dn_ref, first, last) * _ext(vp_ref, v_ref, vn_ref, first, last) * _dgelu(ace)
        dv_ref[...] = (d_ref[...] * _gelu(c_ref[...])).astype(BF16)
        gm = _shift_prev(g, lok)
        gp = _shift_next(g, rok)
        cwv = cw_ref[...]

        def comb(dr, lo):
            sl = slice(lo, lo + TB)
            return cwv[3 * dr:3 * dr + 1] * gp[sl] + cwv[3 * dr + 1:3 * dr + 2] * g[sl] + cwv[3 * dr + 2:3 * dr + 3] * gm[sl]

        da = comb(1, GRID_W) + jnp.where(is_ctx, 0.0, comb(0, 2 * GRID_W) + comb(2, 0))
        da_ref[...] = da.astype(BF16)
        e = _ext(ap_ref, a_ref, an_ref, first, last)
        taps = [_shift_prev(e, lok), e, _shift_next(e, rok)]
        gmain = g[GRID_W:GRID_W + TB]
        dcb_ref[...] += jnp.sum(gmain, axis=0, keepdims=True)
        vert = jnp.where(is_ctx, 0.0, 1.0)
        for dr in range(3):
            sl = slice(dr * GRID_W, dr * GRID_W + TB)
            for dw in range(3):
                s = jnp.sum(gmain * taps[dw][sl], axis=0, keepdims=True)
                if dr != 1:
                    s = s * vert
                k = 3 * dr + dw
                dcw_ref[k:k + 1, :] += s

    main = pl.BlockSpec((TB, cbk), lambda j, i: (i, j))
    halo = _halo_specs(cbk, T // GRID_W)
    acc9 = pl.BlockSpec((9, cbk), lambda j, i: (0, j))
    acc1 = pl.BlockSpec((1, cbk), lambda j, i: (0, j))
    return pl.pallas_call(
        body, name=name, grid=(dff // cbk, nb),
        in_specs=halo + _halo_specs(cbk, T // GRID_W, nvb) + halo + halo + [acc9],
        out_specs=[main, main, acc9, acc1],
        out_shape=[jax.ShapeDtypeStruct((T, dff), BF16), jax.ShapeDtypeStruct((T, dff), BF16),
                   jax.ShapeDtypeStruct((9, dff), F32), jax.ShapeDtypeStruct((1, dff), F32)],
        compiler_params=_params("parallel", "arbitrary"),
    )(up, up, up, up, up, up, ac, ac, ac, dact, dact, dact, cw)


def _ffn_out_fwd(act, xm, mod, wd, layer, ctx_rows, name):
    T, D = xm.shape
    dff = act.shape[1]
    cb = ctx_rows // TB

    def body(act_ref, x_ref, mod_ref, w_ref, xo_ref, fo_ref):
        out = _nn(act_ref[...], _full(w_ref))
        fo_ref[...] = out
        xo_ref[...] = x_ref[...] + mod_ref[5:6, :] * out

    row = pl.BlockSpec((TB, D), lambda i: (i, 0))
    return pl.pallas_call(
        body, name=name, grid=(T // TB,),
        in_specs=[pl.BlockSpec((TB, dff), lambda i: (i, 0)), row,
                  pl.BlockSpec((None, N_MOD, D), lambda i: (_stream_of(i, cb), 0, 0)),
                  _rows_weight_spec(wd, layer)],
        out_specs=[row, row],
        out_shape=[jax.ShapeDtypeStruct((T, D), F32), jax.ShapeDtypeStruct((T, D), F32)],
        compiler_params=_params("parallel"),
    )(act, xm, mod, wd)


def _ffn_out_bwd(dx, fo, mod, wd, layer, ctx_rows, name):
    T, D = dx.shape
    dff = N_CHIPS * wd.shape[2]
    cb = ctx_rows // TB

    def body(dx_ref, fo_ref, mod_ref, w_ref, dout_ref, dact_ref, dg2_ref):
        i = pl.program_id(0)

        @pl.when((i == 0) | (i == cb))
        def _():
            dg2_ref[...] = jnp.zeros_like(dg2_ref)

        dxv = dx_ref[...]
        dg2_ref[...] += jnp.sum(dxv * fo_ref[...], axis=0, keepdims=True)
        dout = (dxv * mod_ref[5:6, :]).astype(BF16)
        dout_ref[...] = dout
        dact_ref[...] = _nt(dout, _full(w_ref))

    row = pl.BlockSpec((TB, D), lambda i: (i, 0))
    return pl.pallas_call(
        body, name=name, grid=(T // TB,),
        in_specs=[row, row, pl.BlockSpec((None, N_MOD, D), lambda i: (_stream_of(i, cb), 0, 0)),
                  _rows_weight_spec(wd, layer)],
        out_specs=[row, pl.BlockSpec((TB, dff), lambda i: (i, 0)),
                   pl.BlockSpec((None, 1, D), lambda i: (_stream_of(i, cb), 0, 0))],
        out_shape=[jax.ShapeDtypeStruct((T, D), BF16), jax.ShapeDtypeStruct((T, dff), F32),
                   jax.ShapeDtypeStruct((2, 1, D), F32)],
        compiler_params=_params("arbitrary"),
    )(dx, fo, mod, wd)


def _loss_bwd(x, target, fw, ctx_rows, name):
    T, D = x.shape
    cb = ctx_rows // TB

    def body(x_ref, t_ref, fw_ref, dx_ref, loss_ref, dfw_ref):
        i = pl.program_id(0)

        @pl.when(i == 0)
        def _():
            loss_ref[...] = jnp.zeros_like(loss_ref)
            dfw_ref[...] = jnp.zeros_like(dfw_ref)

        @pl.when(i < cb)
        def _():
            dx_ref[...] = jnp.zeros_like(dx_ref)

        @pl.when(i >= cb)
        def _():
            xv = x_ref[...]
            r = lax.rsqrt(jnp.mean(xv * xv, axis=-1, keepdims=True) + RMS_EPS)
            xh = xv * r
            fwv = fw_ref[...]
            err = xh * fwv - t_ref[...]
            loss_ref[...] += (0.5 / D) * jnp.sum(err * err)
            dy = err * (1.0 / D)
            dfw_ref[...] += jnp.sum(dy * xh, axis=0, keepdims=True)
            dxh = dy * fwv
            dx_ref[...] = r * (dxh - xh * jnp.mean(dxh * xh, axis=-1, keepdims=True))

    row = pl.BlockSpec((TB, D), lambda i: (i, 0))
    return pl.pallas_call(
        body, name=name, grid=(T // TB,),
        in_specs=[row, pl.BlockSpec((TB, D), lambda i: (jnp.maximum(i - cb, 0), 0)), pl.BlockSpec((1, D), lambda i: (0, 0))],
        out_specs=[row, pl.BlockSpec((1, 128), lambda i: (0, 0)), pl.BlockSpec((1, D), lambda i: (0, 0))],
        out_shape=[jax.ShapeDtypeStruct((T, D), F32), jax.ShapeDtypeStruct((1, 128), F32),
                   jax.ShapeDtypeStruct((1, D), F32)],
        compiler_params=_params("arbitrary"),
    )(x, target, fw)


def _adamw(w, gs, m, v, name):
    L, R, C = w.shape
    assert len(gs) == L
    rb = _rows_tile(R, max(16, (1 << 18) // C // 16 * 16))
    bc1 = 1.0 - ADAM_B1 ** ADAM_STEP
    bc2 = 1.0 - ADAM_B2 ** ADAM_STEP

    def body(w_ref, m_ref, v_ref, *rest):
        g_refs, (g_ref, d_ref, nm_ref, nv_ref) = rest[:L], rest[L:]
        layer = pl.program_id(0)
        for li in range(L):
            @pl.when(layer == li)
            def _():
                gv = g_refs[li][...]
                g_ref[...] = gv
                nm = ADAM_B1 * m_ref[...] + (1.0 - ADAM_B1) * gv
                nv = ADAM_B2 * v_ref[...] + (1.0 - ADAM_B2) * (gv * gv)
                nm_ref[...] = nm
                nv_ref[...] = nv
                d_ref[...] = -ADAM_LR * ((nm / bc1) / (jnp.sqrt(nv / bc2) + ADAM_EPS) + ADAM_WD * w_ref[...])

    blk = pl.BlockSpec((None, rb, C), lambda l, i: (l, i, 0))
    gblk = pl.BlockSpec((rb, C), lambda l, i: (i, 0))
    sd = jax.ShapeDtypeStruct((L, R, C), F32)
    return pl.pallas_call(
        body, name=name, grid=(L, R // rb), in_specs=[blk] * 3 + [gblk] * L, out_specs=[blk] * 4, out_shape=[sd] * 4,
        compiler_params=_params("parallel", "parallel"),
    )(w, m, v, *gs)


def _local_step(xs, cv, target, W, ctx_rows):
    T, D = xs.shape
    depth = W["w_in"].shape[1]
    saved = []
    X = xs
    for l in range(depth):
        s = {}
        mod_all, sa = _mod_fwd(cv, W["ada_w"], l, W["ada_b"][l][None, :], f"mod_fwd_{l}")
        mod = mod_all[:2].reshape(2, N_MOD, D)
        h1 = _norm_mod(X, W["norm1_w"][l][None, :], mod, 0, ctx_rows, f"norm1_{l}")
        parts = _mm_nn_w(h1, W["w_in"], l, F32, f"in_proj_{l}")
        o_f, st_f = _hgrn_fwd(parts, W["hlb"], l, False, ctx_rows, f"hgrn_fwd_f_{l}")
        o, st_b = _hgrn_fwd(parts, W["hlb"], l, True, ctx_rows, f"hgrn_fwd_b_{l}", o_add=o_f)
        ya = _sgu_fwd(parts, W["sgu_ln_w"][l][None, :], W["sgu_ln_b"][l][None, :], W["sgu_w"][l], W["sgu_bt"][l],
                      f"sgu_fwd_{l}")
        yb, pa, pb, mg, tmo, xm = _token_out_fwd(o, parts, ya, X, mod, W["hnw"][l][None, :], W["w_a"], W["w_b"], W["w_o"], l,
                                                 ctx_rows, f"token_out_fwd_{l}")
        h2 = _norm_mod(xm, W["norm2_w"][l][None, :], mod, 3, ctx_rows, f"norm2_{l}")
        up = _mm_nn_w(h2, W["w_up"], l, F32, f"up_proj_{l}")
        ac, act = _conv_fwd(up, W["conv_w"][l], W["conv_b"][l][None, :], ctx_rows, f"conv_fwd_{l}")
        xo, fo = _ffn_out_fwd(act, xm, mod, W["w_down"], l, ctx_rows, f"ffn_out_fwd_{l}")
        s.update(X=X, mod=mod, mod_all=mod_all, sa=sa, h1=h1, parts=parts, o=o, st_f=st_f, st_b=st_b, ya=ya, yb=yb, pa=pa,
                 pb=pb, mg=mg, tmo=tmo, xm=xm, h2=h2, up=up, ac=ac, act=act, fo=fo)
        saved.append(s)
        X = xo

    dX, loss_row, dfw = _loss_bwd(X, target, W["final_norm_w"][None, :], ctx_rows, "loss_bwd")
    G = {k: [None] * depth for k in ("ada_w", "ada_b", "norm1_w", "w_in", "sgu_ln_w", "sgu_ln_b", "sgu_w", "sgu_b", "hlb1",
                                     "hnw", "w_a", "w_b", "w_o", "norm2_w", "w_up", "conv_w", "conv_b", "w_down")}
    dcv = jnp.zeros_like(cv)
    for l in reversed(range(depth)):
        s = saved[l]
        mod = s["mod"]
        dout2, dact, dg2 = _ffn_out_bwd(dX, s["fo"], mod, W["w_down"], l, ctx_rows, f"ffn_out_bwd_{l}")
        G["w_down"][l] = _mm_tn(s["act"], dout2, F32, f"dw_down_{l}")
        da, dv, dcw, dcb = _conv_bwd(s["up"], s["ac"], dact, W["conv_w"][l], ctx_rows, f"conv_bwd_{l}")
        G["conv_w"][l], G["conv_b"][l] = dcw, dcb[0]
        dup = jnp.concatenate([da, dv], axis=1)
        G["w_up"][l] = _mm_tn(s["h2"], dup, F32, f"dw_up_{l}", out_chips=True)
        dh2 = _mm_nt_w(dup, W["w_up"], l, F32, f"dh2_{l}")
        dxm, dm2, dnw2 = _norm_mod_bwd(dh2, s["xm"], dX, W["norm2_w"][l][None, :], mod, 3, ctx_rows, f"norm2_bwd_{l}")
        G["norm2_w"][l] = dnw2[0]
        (dout1, dpa, dpb, dog, dga, dgb, dya, do, dg1, dhnw) = _token_out_bwd(
            dxm, s["tmo"], s["pa"], s["pb"], s["o"], s["parts"], mod, W["hnw"][l][None, :], W["w_a"], W["w_b"], W["w_o"], l,
            ctx_rows, f"token_out_bwd_{l}")
        G["hnw"][l] = dhnw[0]
        G["w_o"][l] = _mm_tn(s["mg"], dout1, F32, f"dw_o_{l}")
        G["w_a"][l] = _mm_tn(s["ya"], dpa, F32, f"dw_a_{l}")
        G["w_b"][l] = _mm_tn(s["yb"], dpb, F32, f"dw_b_{l}")
        du, dvs, dsw, dsbt, dlnw, dlnb = _sgu_bwd(s["parts"], dya, W["sgu_ln_w"][l][None, :], W["sgu_ln_b"][l][None, :],
                                                  W["sgu_w"][l], W["sgu_bt"][l], f"sgu_bwd_{l}")
        G["sgu_w"][l], G["sgu_b"][l], G["sgu_ln_w"][l], G["sgu_ln_b"][l] = dsw, dsbt.T, dlnw[0], dlnb[0]
        dq_f, dz_f, di_f, dlb_f = _hgrn_bwd(s["parts"], W["hlb"], do, s["st_f"], l, False, ctx_rows, f"hgrn_bwd_f_{l}")
        dq, dz_b, di, dlb_b = _hgrn_bwd(s["parts"], W["hlb"], do, s["st_b"], l, True, ctx_rows, f"hgrn_bwd_b_{l}",
                                        dq_add=dq_f, di_add=di_f)
        G["hlb1"][l] = jnp.concatenate([dlb_f[0], dlb_b[0]])
        dparts = jnp.concatenate([dq.astype(BF16), dz_f, dz_b, di.astype(BF16), du, dvs, dog, dga, dgb], axis=1)
        G["w_in"][l] = _mm_tn(s["h1"], dparts, F32, f"dw_in_{l}", out_chips=True)
        dh1 = _mm_nt_w(dparts, W["w_in"], l, F32, f"dh1_{l}")
        dX, dm1, dnw1 = _norm_mod_bwd(dh1, s["X"], dxm, W["norm1_w"][l][None, :], mod, 0, ctx_rows, f"norm1_bwd_{l}")
        G["norm1_w"][l] = dnw1[0]
        dmod = jnp.concatenate([dm1, dg1, dm2, dg2], axis=1).reshape(2, N_MOD * D)
        dmod16 = jnp.concatenate([dmod, jnp.zeros((cv.shape[0] - 2, N_MOD * D), F32)], axis=0)
        G["ada_b"][l] = dmod[0] + dmod[1]
        G["ada_w"][l] = _mm_tn(s["sa"], dmod16.astype(BF16), F32, f"dw_ada_{l}", out_chips=True)
        dcv = dcv + _cvec_bwd(dmod16, W["ada_w"], l, cv, f"dcvec_{l}")
    G["c_ctx"] = dcv[0]
    G["final_norm_w"] = dfw[0]
    return loss_row[0, 0], dX, G


def _chip_peers(x, y, c):
    return [((1 - x, y, c), 2 * (1 - x) + y), ((x, 1 - y, c), 2 * x + 1 - y), ((1 - x, 1 - y, c), 2 * (1 - x) + 1 - y)]


def _rdma_call(ins, out_shapes, plan, n_remote, n_local, name, aliases=None):
    n_in, n_out = len(ins), len(out_shapes)

    def body(*refs):
        in_refs, out_refs = refs[:n_in], refs[n_in:n_in + n_out]
        send_sems, recv_sems, local_sems = refs[n_in + n_out:]
        x, y, c = lax.axis_index("x"), lax.axis_index("y"), lax.axis_index("c")
        remote, local = plan(in_refs, out_refs, x, y, c)
        assert len(remote) == n_remote and len(local) == n_local, (name, len(remote), len(local))
        copies = [pltpu.make_async_copy(s, d, local_sems.at[i]) for i, (s, d) in enumerate(local)]
        copies += [pltpu.make_async_remote_copy(src_ref=s, dst_ref=d, send_sem=send_sems.at[k], recv_sem=recv_sems.at[k],
                                                device_id=dev, device_id_type=pl.DeviceIdType.MESH)
                   for k, (s, d, dev) in enumerate(remote)]
        for cp in copies:
            cp.start()
        for cp in copies:
            cp.wait()

    hbm = pl.BlockSpec(memory_space=pltpu.HBM)
    return pl.pallas_call(
        body, name=name, in_specs=[hbm] * n_in, out_specs=[hbm] * n_out, out_shape=out_shapes,
        scratch_shapes=[pltpu.SemaphoreType.DMA((n_remote,)), pltpu.SemaphoreType.DMA((n_remote,)),
                        pltpu.SemaphoreType.DMA((max(n_local, 1),))],
        input_output_aliases=aliases or {},
    )(*ins)


def _gather_chips(shards, name):
    def plan(ins, outs, x, y, c):
        me = 2 * x + y
        remote = [(s, o.at[me], dev) for dev, _ in _chip_peers(x, y, c) for s, o in zip(ins, outs)]
        return remote, [(s, o.at[me]) for s, o in zip(ins, outs)]

    shapes = [jax.ShapeDtypeStruct((N_CHIPS,) + s.shape, s.dtype) for s in shards]
    return _rdma_call(shards, shapes, plan, (N_CHIPS - 1) * len(shards), len(shards), name)


def _reduce_pair(parts, name):
    def plan(ins, outs, x, y, c):
        return [(a.at[j, 1 - c], o.at[j], (x, y, 1 - c)) for a, o in zip(ins, outs) for j in range(N_CHIPS)], []

    shapes = [jax.ShapeDtypeStruct((N_CHIPS,) + a.shape[2:], a.dtype) for a in parts]
    return _rdma_call(parts, shapes, plan, N_CHIPS * len(parts), 0, name)


def _reduce_chips(parts, name):
    def plan(ins, outs, x, y, c):
        me = 2 * x + y
        remote = [(a.at[idx], o.at[me], dev) for dev, idx in _chip_peers(x, y, c) for a, o in zip(ins, outs)]
        return remote, [(a.at[me], o.at[me]) for a, o in zip(ins, outs)]

    shapes = [jax.ShapeDtypeStruct(a.shape, a.dtype) for a in parts]
    return _rdma_call(parts, shapes, plan, (N_CHIPS - 1) * len(parts), len(parts), name)


def _gather_pair(halves, name):
    def plan(ins, outs, x, y, c):
        return [(o.at[c], o.at[c], (x, y, 1 - c)) for o in outs], []

    shapes = [jax.ShapeDtypeStruct(a.shape, a.dtype) for a in halves]
    return _rdma_call(halves, shapes, plan, len(halves), 0, name, aliases={i: i for i in range(len(halves))})


def _sum_block_rows(r, C):
    return _rows_tile(r, max(16, (1 << 18) // C // 16 * 16))


def _sum_pair(a, recv, cidx, name):
    nch, _, r, C = a.shape
    rb = _sum_block_rows(r, C)

    def body(c_ref, a_ref, r_ref, o_ref):
        o_ref[...] = (a_ref[...] + r_ref[...]).astype(BF16)

    blk = pl.BlockSpec((None, rb, C), lambda j, i, c: (j, i, 0))
    return pl.pallas_call(
        body, name=name,
        grid_spec=pltpu.PrefetchScalarGridSpec(
            num_scalar_prefetch=1, grid=(nch, r // rb),
            in_specs=[pl.BlockSpec((None, None, rb, C), lambda j, i, c: (j, c[0], i, 0)), blk], out_specs=blk),
        out_shape=jax.ShapeDtypeStruct((nch, r, C), BF16),
        compiler_params=_params("parallel", "parallel"),
    )(cidx, a, recv)


def _sum_chips(recv, cidx, name):
    nch, r, C = recv.shape
    rb = _sum_block_rows(r, C)

    def body(c_ref, r_ref, o_ref):
        acc = r_ref[0].astype(F32)
        for q in range(1, nch):
            acc = acc + r_ref[q].astype(F32)
        o_ref[...] = acc

    return pl.pallas_call(
        body, name=name,
        grid_spec=pltpu.PrefetchScalarGridSpec(
            num_scalar_prefetch=1, grid=(r // rb,),
            in_specs=[pl.BlockSpec((nch, rb, C), lambda i, c: (0, i, 0))],
            out_specs=pl.BlockSpec((None, rb, C), lambda i, c: (c[0], i, 0))),
        out_shape=jax.ShapeDtypeStruct((N_CORES, r, C), F32),
        compiler_params=_params("parallel"),
    )(cidx, recv)


PACK_COLS = 1024
_SHARDED = ("ada_w", "w_in", "w_branch_a", "w_branch_b", "w_out", "ffn_w_up", "ffn_w_down")
_SMALL = ("c_ctx", "ada_b", "norm1_w", "sgu_ln_w", "sgu_ln_b", "sgu_w", "sgu_b", "hgrn_lower_bounds", "hgrn_norm_w",
          "norm2_w", "ffn_conv_b", "final_norm_w")
_ORDER = ("c_ctx", "ada_w", "ada_b", "norm1_w", "w_in", "sgu_ln_w", "sgu_ln_b", "sgu_w", "sgu_b", "hgrn_lower_bounds",
          "hgrn_norm_w", "w_branch_a", "w_branch_b", "w_out", "norm2_w", "ffn_w_up", "ffn_conv_w", "ffn_conv_b",
          "ffn_w_down", "final_norm_w")


def _pad_to(v, n):
    return jnp.concatenate([v, jnp.zeros((n - v.shape[0],), v.dtype)]) if v.shape[0] < n else v


def _round_up(n, m):
    return (n + m - 1) // m * m


def _pack(arrays, n_pad):
    flat = jnp.concatenate([a.reshape(-1) for a in arrays])
    return _pad_to(flat, n_pad)


def _unpack(flat, like):
    out, off = [], 0
    for a in like:
        out.append(flat[off:off + a.size].reshape(a.shape))
        off += a.size
    return out


def kernel(x, c, ctx, c_ctx, ada_w, ada_b, norm1_w, w_in, sgu_ln_w, sgu_ln_b, sgu_w, sgu_b, hgrn_lower_bounds, hgrn_norm_w, w_branch_a, w_branch_b, w_out, norm2_w, ffn_w_up, ffn_conv_w, ffn_conv_b, ffn_w_down, final_norm_w, loss_target, m_c_ctx, m_ada_w, m_ada_b, m_norm1_w, m_w_in, m_sgu_ln_w, m_sgu_ln_b, m_sgu_w, m_sgu_b, m_hgrn_lower_bounds, m_hgrn_norm_w, m_w_branch_a, m_w_branch_b, m_w_out, m_norm2_w, m_ffn_w_up, m_ffn_conv_w, m_ffn_conv_b, m_ffn_w_down, m_final_norm_w, v_c_ctx, v_ada_w, v_ada_b, v_norm1_w, v_w_in, v_sgu_ln_w, v_sgu_ln_b, v_sgu_w, v_sgu_b, v_hgrn_lower_bounds, v_hgrn_norm_w, v_w_branch_a, v_w_branch_b, v_w_out, v_norm2_w, v_ffn_w_up, v_ffn_conv_w, v_ffn_conv_b, v_ffn_w_down, v_final_norm_w):
    w = dict(c_ctx=c_ctx, ada_w=ada_w, ada_b=ada_b, norm1_w=norm1_w, w_in=w_in, sgu_ln_w=sgu_ln_w, sgu_ln_b=sgu_ln_b,
             sgu_w=sgu_w, sgu_b=sgu_b, hgrn_lower_bounds=hgrn_lower_bounds, hgrn_norm_w=hgrn_norm_w, w_branch_a=w_branch_a,
             w_branch_b=w_branch_b, w_out=w_out, norm2_w=norm2_w, ffn_w_up=ffn_w_up, ffn_conv_w=ffn_conv_w,
             ffn_conv_b=ffn_conv_b, ffn_w_down=ffn_w_down, final_norm_w=final_norm_w)
    mom = dict(zip(_ORDER, (m_c_ctx, m_ada_w, m_ada_b, m_norm1_w, m_w_in, m_sgu_ln_w, m_sgu_ln_b, m_sgu_w, m_sgu_b,
                            m_hgrn_lower_bounds, m_hgrn_norm_w, m_w_branch_a, m_w_branch_b, m_w_out, m_norm2_w, m_ffn_w_up,
                            m_ffn_conv_w, m_ffn_conv_b, m_ffn_w_down, m_final_norm_w)))
    var = dict(zip(_ORDER, (v_c_ctx, v_ada_w, v_ada_b, v_norm1_w, v_w_in, v_sgu_ln_w, v_sgu_ln_b, v_sgu_w, v_sgu_b,
                            v_hgrn_lower_bounds, v_hgrn_norm_w, v_w_branch_a, v_w_branch_b, v_w_out, v_norm2_w, v_ffn_w_up,
                            v_ffn_conv_w, v_ffn_conv_b, v_ffn_w_down, v_final_norm_w)))
    depth, D = norm1_w.shape
    dff = ffn_conv_b.shape[1]
    ctx_rows, seq = ctx.shape[1], x.shape[1]

    assert depth == 2, "the lower-bound softmax is written for two layers"
    chip = 2 * lax.axis_index("x") + lax.axis_index("y")
    cidx = lax.axis_index("c").astype(jnp.int32).reshape(1)

    got = _gather_chips([w[k].astype(BF16) for k in _SHARDED] + [ffn_conv_w], "gather_weights")
    Wg = dict(zip(_SHARDED, got[:-1]))
    conv_full = jnp.transpose(got[-1], (1, 2, 3, 0, 4)).reshape(depth, 9, dff)
    W = dict(ada_w=Wg["ada_w"], ada_b=ada_b, norm1_w=norm1_w, w_in=Wg["w_in"], sgu_ln_w=sgu_ln_w, sgu_ln_b=sgu_ln_b,
             sgu_w=sgu_w.astype(BF16), sgu_bt=jnp.swapaxes(sgu_b, 1, 2), hlb=hgrn_lower_bounds, hnw=hgrn_norm_w,
             w_a=Wg["w_branch_a"], w_b=Wg["w_branch_b"], w_o=Wg["w_out"], norm2_w=norm2_w, w_up=Wg["ffn_w_up"],
             conv_w=[conv_full[l] for l in range(depth)], conv_b=ffn_conv_b, w_down=Wg["ffn_w_down"],
             final_norm_w=final_norm_w)

    xs = jnp.concatenate([ctx[0], x[0]], axis=0)
    cv = jnp.concatenate([c_ctx[None, :], c, jnp.zeros((14, D), F32)], axis=0)
    loss_local, dxs, G = _local_step(xs, cv, loss_target[0], W, ctx_rows)
    loss = lax.psum(loss_local, ("x", "y", "c"))
    grad_x = dxs[ctx_rows:][None]

    local = {"ada_w": G["ada_w"], "w_in": G["w_in"], "w_branch_a": G["w_a"], "w_branch_b": G["w_b"], "w_out": G["w_o"],
             "ffn_w_up": G["w_up"], "ffn_w_down": G["w_down"]}
    parts = []
    for k in _SHARDED:
        for l in range(depth):
            g = local[k][l]
            cols = g.shape[-1]
            parts.append(g.reshape(N_CHIPS, N_CORES, g.size // (N_CHIPS * N_CORES * cols), cols))
    dh = G["hlb1"][depth - 1]
    small_like = [w[k] for k in _SMALL] + [jnp.zeros((depth, 9, dff), F32)]
    small = [G["c_ctx"], jnp.stack(G["ada_b"]), jnp.stack(G["norm1_w"]), jnp.stack(G["sgu_ln_w"]), jnp.stack(G["sgu_ln_b"]),
             jnp.stack(G["sgu_w"]), jnp.stack(G["sgu_b"]), jnp.stack([-dh, dh]), jnp.stack(G["hnw"]), jnp.stack(G["norm2_w"]),
             jnp.stack(G["conv_b"]), G["final_norm_w"], jnp.stack(G["conv_w"])]
    n_small = sum(a.size for a in small)
    n_small_pad = _round_up(n_small, N_CORES * 16 * PACK_COLS)
    small_rows = n_small_pad // (N_CORES * PACK_COLS)
    parts.append(jnp.broadcast_to(_pack(small, n_small_pad).reshape(1, N_CORES, small_rows, PACK_COLS),
                                  (N_CHIPS, N_CORES, small_rows, PACK_COLS)))

    other = _reduce_pair(parts, "reduce_pair")
    pair_sums = [_sum_pair(a, o, cidx, f"sum_pair_{i}") for i, (a, o) in enumerate(zip(parts, other))]
    from_chips = _reduce_chips(pair_sums, "reduce_chips")
    halves = [_sum_chips(r, cidx, f"sum_chips_{i}") for i, r in enumerate(from_chips)]
    reduced = _gather_pair(halves, "gather_pair")

    g_small = _unpack(reduced[-1].reshape(-1), small_like)
    grads = dict(zip(_SMALL, g_small[:-1]))
    g_conv = lax.dynamic_slice_in_dim(g_small[-1].reshape(depth, 3, 3, dff), chip * (dff // N_CHIPS), dff // N_CHIPS, axis=3)

    delta, new_m, new_v = {}, {}, {}
    for i, k in enumerate(_SHARDED):
        shp = w[k].shape
        gs = [reduced[depth * i + l].reshape(shp[1:]) for l in range(depth)]
        grads[k], delta[k], new_m[k], new_v[k] = _adamw(w[k], gs, mom[k], var[k], f"adamw_{k}")
    packed = _SMALL + ("ffn_conv_w",)
    n_pad = _round_up(sum(w[k].size for k in packed), 16 * PACK_COLS)
    pack = lambda t: _pack([t[k] for k in packed], n_pad).reshape(1, -1, PACK_COLS)
    grads["ffn_conv_w"] = g_conv
    _, d, nm, nv = _adamw(pack(w), [pack(grads)[0]], pack(mom), pack(var), "adamw_packed")
    like = [w[k] for k in packed]
    for src, dst in ((d, delta), (nm, new_m), (nv, new_v)):
        dst.update(zip(packed, _unpack(src.reshape(-1), like)))

    return (loss, grad_x, *[grads[k] for k in _ORDER], *[delta[k] for k in _ORDER], *[new_m[k] for k in _ORDER],
            *[new_v[k] for k in _ORDER])
```

```python
import functools

import jax
import jax.numpy as jnp
from jax import lax
from jax.experimental import pallas as pl
from jax.experimental.pallas import tpu as pltpu

F32 = jnp.float32
BF16 = jnp.bfloat16

GRID_W = 64
HG_CHUNK = 64
SGU_CHUNK = 128
HEAD = 128
TB = 256
N_MOD = 6
RMS_EPS = 1e-6
LN_EPS = 1e-5
VMEM_LIMIT = 48 * 1024 * 1024
N_CHIPS = 4
N_CORES = 2

ADAM_LR = 0.001
ADAM_B1 = 0.9
ADAM_B2 = 0.999
ADAM_EPS = 1e-08
ADAM_WD = 0.01
ADAM_STEP = 10

_GELU_C = 0.7978845608028654
_GELU_A = 0.044715


def _sigmoid(x):
    return 1.0 / (1.0 + jnp.exp(-x))


def _silu(x):
    return x * _sigmoid(x)


def _dsilu(x):
    s = _sigmoid(x)
    return s * (1.0 + x * (1.0 - s))


def _gelu(x):
    return 0.5 * x * (1.0 + jnp.tanh(_GELU_C * (x + _GELU_A * x * x * x)))


def _dgelu(x):
    t = jnp.tanh(_GELU_C * (x + _GELU_A * x * x * x))
    return 0.5 * (1.0 + t) + 0.5 * x * (1.0 - t * t) * _GELU_C * (1.0 + 3.0 * _GELU_A * x * x)


def _dot(a, b, ca, cb):
    return lax.dot_general(a, b, (((ca,), (cb,)), ((), ())), preferred_element_type=F32)


def _nn(a, b):
    return _dot(a, b, 1, 0)


def _nt(a, b):
    return _dot(a, b, 1, 1)


def _tn(a, b):
    return _dot(a, b, 0, 0)


def _params(*sem):
    return pltpu.CompilerParams(dimension_semantics=sem if sem else None, vmem_limit_bytes=VMEM_LIMIT)


def _stream_of(i, ctx_blocks):
    return (i >= ctx_blocks).astype(jnp.int32)


def _mm(a, b, mode, tm, tn, tk, out_dtype, name, add=None, b_chips=False, out_chips=False):
    if not b_chips:
        bshape = b.shape
    else:
        bshape = (b.shape[1], N_CHIPS * b.shape[2])
    if mode == "nn":
        (M, K), (K2, N) = a.shape, bshape
    elif mode == "nt":
        (M, K), (N, K2) = a.shape, bshape
    else:
        (K, M), (K2, N) = a.shape, bshape
    assert K == K2 and M % tm == 0 and N % tn == 0 and K % tk == 0, (name, a.shape, b.shape, tm, tn, tk)
    nk = K // tk
    if mode == "tn":
        a_spec = pl.BlockSpec((tk, tm), lambda j, i, k: (k, i))
    else:
        a_spec = pl.BlockSpec((tm, tk), lambda j, i, k: (i, k))
    if not b_chips:
        if mode == "nt":
            b_spec = pl.BlockSpec((tn, tk), lambda j, i, k: (j, k))
        else:
            b_spec = pl.BlockSpec((tk, tn), lambda j, i, k: (k, j))
    else:
        cols = b.shape[2]
        if mode == "nn":
            per = cols // tn
            assert cols % tn == 0
            b_spec = pl.BlockSpec((None, tk, tn), lambda j, i, k: (j // per, k, j % per))
        else:
            per = cols // tk
            assert mode == "nt" and cols % tk == 0
            b_spec = pl.BlockSpec((None, tn, tk), lambda j, i, k: (k // per, j, k % per))
    if out_chips:
        per_o = (N // N_CHIPS) // tn
        assert (N // N_CHIPS) % tn == 0 and add is None
        o_spec = pl.BlockSpec((None, tm, tn), lambda j, i, k: (j // per_o, i, j % per_o))
        o_shape = (N_CHIPS, M, N // N_CHIPS)
    else:
        o_spec = pl.BlockSpec((tm, tn), lambda j, i, k: (i, j))
        o_shape = (M, N)
    ca, cb = {"nn": (1, 0), "nt": (1, 1), "tn": (0, 0)}[mode]

    def body(a_ref, b_ref, *rest):
        if add is None:
            o_ref, acc = rest
        else:
            add_ref, o_ref, acc = rest
        k = pl.program_id(2)

        @pl.when(k == 0)
        def _():
            acc[...] = jnp.zeros_like(acc)

        acc[...] += _dot(a_ref[...], b_ref[...], ca, cb)

        @pl.when(k == nk - 1)
        def _():
            r = acc[...]
            if add is not None:
                r = r + add_ref[...]
            o_ref[...] = r.astype(out_dtype)

    ins = [a, b] + ([] if add is None else [add])
    specs = [a_spec, b_spec] + ([] if add is None else [o_spec])
    return pl.pallas_call(
        body, name=name, grid=(N // tn, M // tm, nk), in_specs=specs, out_specs=o_spec,
        out_shape=jax.ShapeDtypeStruct(o_shape, out_dtype),
        scratch_shapes=[pltpu.VMEM((tm, tn), F32)],
        compiler_params=_params("parallel", "parallel", "arbitrary"),
    )(*ins)


def _tile(n, pref):
    if n <= pref:
        return n
    best = None
    for t in range(128, pref + 1, 128):
        if n % t == 0:
            best = t
    assert best is not None, (n, pref)
    return best


def _rows_tile(n, pref):
    if n <= pref:
        return n
    best = None
    for t in range(16, pref + 1, 16):
        if n % t == 0:
            best = t
    assert best is not None, (n, pref)
    return best


def _mm_nn_w(a, wg, out_dtype, name):
    M, K = a.shape
    return _mm(a, wg, "nn", _rows_tile(M, 512), _tile(wg.shape[2], 1536), _tile(K, 1536), out_dtype, name, b_chips=True)


def _mm_nt_w(a, wg, out_dtype, name):
    M, K = a.shape
    return _mm(a, wg, "nt", _rows_tile(M, 1088), _tile(wg.shape[1], 1024), _tile(wg.shape[2], 1536), out_dtype, name,
               b_chips=True)


def _mm_tn(a, b, out_dtype, name, out_chips=False):
    K, M = a.shape
    N = b.shape[1]
    ncol = N // N_CHIPS if out_chips else N
    tm, tn = _tile(M, 1408), _tile(ncol, 1408)
    if tm * tn > 1408 * 1152:
        tn = _tile(ncol, 1152)
    return _mm(a, b, "tn", tm, tn, _rows_tile(K, 2176), out_dtype, name, out_chips=out_chips)


def _mod_fwd(cv, wg, b, name):
    R, D = cv.shape
    tn = wg.shape[2]
    N = N_CHIPS * tn

    def body(cv_ref, w_ref, b_ref, mod_ref, sa_ref):
        sa = _silu(cv_ref[...]).astype(BF16)
        sa_ref[...] = sa
        mod_ref[...] = _nn(sa, w_ref[...]) + b_ref[...]

    return pl.pallas_call(
        body, name=name, grid=(N_CHIPS,),
        in_specs=[pl.BlockSpec((R, D), lambda j: (0, 0)), pl.BlockSpec((None, D, tn), lambda j: (j, 0, 0)),
                  pl.BlockSpec((1, tn), lambda j: (0, j))],
        out_specs=[pl.BlockSpec((R, tn), lambda j: (0, j)), pl.BlockSpec((R, D), lambda j: (0, 0))],
        out_shape=[jax.ShapeDtypeStruct((R, N), F32), jax.ShapeDtypeStruct((R, D), BF16)],
        compiler_params=_params("arbitrary"),
    )(cv, wg, b)


def _cvec_bwd(dmod, wg, cv, name):
    R, N = dmod.shape
    D = wg.shape[1]
    tk = wg.shape[2]
    nk = N_CHIPS

    def body(dm_ref, w_ref, cv_ref, o_ref):
        k = pl.program_id(0)

        @pl.when(k == 0)
        def _():
            o_ref[...] = jnp.zeros_like(o_ref)

        o_ref[...] += _nt(dm_ref[...].astype(BF16), w_ref[...])

        @pl.when(k == nk - 1)
        def _():
            o_ref[...] = o_ref[...] * _dsilu(cv_ref[...])

    return pl.pallas_call(
        body, name=name, grid=(nk,),
        in_specs=[pl.BlockSpec((R, tk), lambda k: (0, k)), pl.BlockSpec((None, D, tk), lambda k: (k, 0, 0)),
                  pl.BlockSpec((R, D), lambda k: (0, 0))],
        out_specs=pl.BlockSpec((R, D), lambda k: (0, 0)),
        out_shape=jax.ShapeDtypeStruct((R, D), F32),
        compiler_params=_params("arbitrary"),
    )(dmod, wg, cv)


def _norm_mod(x, nw, mod, which, ctx_rows, name):
    T, D = x.shape
    cb = ctx_rows // TB

    def body(x_ref, nw_ref, mod_ref, h_ref):
        xv = x_ref[...]
        r = lax.rsqrt(jnp.mean(xv * xv, axis=-1, keepdims=True) + RMS_EPS)
        y = xv * r * nw_ref[...]
        sh = mod_ref[which:which + 1, :]
        sc = mod_ref[which + 1:which + 2, :]
        h_ref[...] = (y * (1.0 + sc) + sh).astype(BF16)

    return pl.pallas_call(
        body, name=name, grid=(T // TB,),
        in_specs=[pl.BlockSpec((TB, D), lambda i: (i, 0)), pl.BlockSpec((1, D), lambda i: (0, 0)),
                  pl.BlockSpec((None, N_MOD, D), lambda i: (_stream_of(i, cb), 0, 0))],
        out_specs=pl.BlockSpec((TB, D), lambda i: (i, 0)),
        out_shape=jax.ShapeDtypeStruct((T, D), BF16),
        compiler_params=_params("parallel"),
    )(x, nw, mod)


def _norm_mod_bwd(dh, x, dres, nw, mod, which, ctx_rows, name):
    T, D = x.shape
    cb = ctx_rows // TB

    def body(dh_ref, x_ref, dres_ref, nw_ref, mod_ref, dx_ref, dm_ref, dnw_ref):
        i = pl.program_id(0)

        @pl.when(i == 0)
        def _():
            dnw_ref[...] = jnp.zeros_like(dnw_ref)

        @pl.when((i == 0) | (i == cb))
        def _():
            dm_ref[...] = jnp.zeros_like(dm_ref)

        xv = x_ref[...]
        dh = dh_ref[...]
        r = lax.rsqrt(jnp.mean(xv * xv, axis=-1, keepdims=True) + RMS_EPS)
        xh = xv * r
        nwv = nw_ref[...]
        sc = mod_ref[which + 1:which + 2, :]
        y = xh * nwv
        dm_ref[0:1, :] += jnp.sum(dh, axis=0, keepdims=True)
        dm_ref[1:2, :] += jnp.sum(dh * y, axis=0, keepdims=True)
        dy = dh * (1.0 + sc)
        dnw_ref[...] += jnp.sum(dy * xh, axis=0, keepdims=True)
        dxh = dy * nwv
        dx_ref[...] = dres_ref[...] + r * (dxh - xh * jnp.mean(dxh * xh, axis=-1, keepdims=True))

    return pl.pallas_call(
        body, name=name, grid=(T // TB,),
        in_specs=[pl.BlockSpec((TB, D), lambda i: (i, 0)), pl.BlockSpec((TB, D), lambda i: (i, 0)),
                  pl.BlockSpec((TB, D), lambda i: (i, 0)), pl.BlockSpec((1, D), lambda i: (0, 0)),
                  pl.BlockSpec((None, N_MOD, D), lambda i: (_stream_of(i, cb), 0, 0))],
        out_specs=[pl.BlockSpec((TB, D), lambda i: (i, 0)),
                   pl.BlockSpec((None, 2, D), lambda i: (_stream_of(i, cb), 0, 0)),
                   pl.BlockSpec((1, D), lambda i: (0, 0))],
        out_shape=[jax.ShapeDtypeStruct((T, D), F32), jax.ShapeDtypeStruct((2, 2, D), F32),
                   jax.ShapeDtypeStruct((1, D), F32)],
        compiler_params=_params("arbitrary"),
    )(dh, x, dres, nw, mod)


def _scan_chunk(n, rev, n_ctx, n_all):
    if not rev:
        return n
    return jnp.where(n < n_ctx, n_ctx - 1 - n, n_all - 1 + n_ctx - n)


def _cumsum_rows(x, rev):
    rows = x.shape[0]
    row = lax.broadcasted_iota(jnp.int32, (rows, 1), 0)
    s = 1
    while s < rows:
        if not rev:
            x = x + jnp.where(row >= s, pltpu.roll(x, s, 0), 0.0)
        else:
            x = x + jnp.where(row < rows - s, pltpu.roll(x, rows - s, 0), 0.0)
        s *= 2
    return x


def _lower_bound(hlb_ref, layer):
    h = hlb_ref[...]
    if layer == 0:
        return jnp.zeros_like(h[0:1, :])
    return _sigmoid(h[1:2, :] - h[0:1, :])


def _hgrn_gates(q_ref, f_ref, hlb_ref, layer, rev):
    lb = _lower_bound(hlb_ref, layer)
    z = f_ref[...]
    sig = _sigmoid(z)
    fg = lb + (1.0 - lb) * sig
    kk = (1.0 - lb) * (1.0 - sig)
    g = jnp.log(fg)
    b = _cumsum_rows(g, rev)
    bt = jnp.sum(g, axis=0, keepdims=True)
    mid = HG_CHUNK // 2
    r = b[mid:mid + 1, :] if rev else b[mid - 1:mid, :]
    qh = _silu(q_ref[...])
    return lb, sig, fg, kk, b, bt, r, qh


def _tri_mask(rev):
    t = lax.broadcasted_iota(jnp.int32, (HG_CHUNK, HG_CHUNK), 0)
    s = lax.broadcasted_iota(jnp.int32, (HG_CHUNK, HG_CHUNK), 1)
    return (s >= t) if rev else (s <= t)


def _hgrn_fwd(parts, hlb, layer, rev, ctx_rows, name, o_add=None):
    T = parts.shape[0]
    D = hlb.shape[1] // 2
    nh = D // HEAD
    n_all, n_ctx = T // HG_CHUNK, ctx_rows // HG_CHUNK
    chunk = functools.partial(_scan_chunk, rev=rev, n_ctx=n_ctx, n_all=n_all)
    fcol = 2 if rev else 1

    def body(q_ref, f_ref, i_ref, hlb_ref, *rest):
        if o_add is None:
            o_ref, st_ref, s_scr = rest
        else:
            oa_ref, o_ref, st_ref, s_scr = rest
        n = pl.program_id(0)

        @pl.when(n == 0)
        def _():
            s_scr[...] = jnp.zeros_like(s_scr)

        lb, sig, fg, kk, b, bt, r, qh = _hgrn_gates(q_ref, f_ref, hlb_ref, layer, rev)
        qr = (qh * jnp.exp(b - r)).astype(BF16)
        kr = (kk * jnp.exp(r - b)).astype(BF16)
        qe = (qh * jnp.exp(b)).astype(BF16)
        ke = (kk * jnp.exp(bt - b)).astype(BF16)
        dec = jnp.exp(bt)
        v = i_ref[...].astype(BF16)
        mask = _tri_mask(rev)
        for h in range(nh):
            sl = slice(h * HEAD, (h + 1) * HEAD)
            st = s_scr[h]
            st_ref[h] = st
            a = jnp.where(mask, _nt(qr[:, sl], kr[:, sl]), 0.0).astype(BF16)
            o = _nn(a, v[:, sl]) + _nt(qe[:, sl], st.astype(BF16))
            if o_add is not None:
                o = o + oa_ref[:, sl]
            o_ref[:, sl] = o
            s_scr[h] = st * dec[:, sl] + _tn(v[:, sl], ke[:, sl])

    cspec = lambda col: pl.BlockSpec((HG_CHUNK, D), lambda n: (chunk(n), col))
    ins = [parts, parts, parts, hlb]
    specs = [cspec(0), cspec(fcol), cspec(3), pl.BlockSpec((2, D), lambda n: (0, 1 if rev else 0))]
    if o_add is not None:
        ins.append(o_add)
        specs.append(cspec(0))
    return pl.pallas_call(
        body, name=name, grid=(n_all,), in_specs=specs,
        out_specs=[cspec(0), pl.BlockSpec((None, nh, HEAD, HEAD), lambda n: (n, 0, 0, 0))],
        out_shape=[jax.ShapeDtypeStruct((T, D), F32), jax.ShapeDtypeStruct((n_all, nh, HEAD, HEAD), F32)],
        scratch_shapes=[pltpu.VMEM((nh, HEAD, HEAD), F32)],
        compiler_params=_params("arbitrary"),
    )(*ins)


def _hgrn_bwd(parts, hlb, do, states, layer, rev, ctx_rows, name, dq_add=None, di_add=None):
    T = parts.shape[0]
    D = hlb.shape[1] // 2
    nh = D // HEAD
    n_all, n_ctx = T // HG_CHUNK, ctx_rows // HG_CHUNK
    step = lambda m: n_all - 1 - m
    chunk = lambda m: _scan_chunk(step(m), rev, n_ctx, n_all)
    fcol = 2 if rev else 1
    has_add = dq_add is not None

    def body(q_ref, f_ref, i_ref, hlb_ref, do_ref, st_ref, *rest):
        if has_add:
            dqa_ref, dia_ref, dq_ref, dz_ref, di_ref, dlb_ref, ds_scr = rest
        else:
            dq_ref, dz_ref, di_ref, dlb_ref, ds_scr = rest
        m = pl.program_id(0)

        @pl.when(m == 0)
        def _():
            ds_scr[...] = jnp.zeros_like(ds_scr)
            dlb_ref[...] = jnp.zeros_like(dlb_ref)

        lb, sig, fg, kk, b, bt, r, qh = _hgrn_gates(q_ref, f_ref, hlb_ref, layer, rev)
        e_qr = jnp.exp(b - r)
        e_kr = jnp.exp(r - b)
        e_b = jnp.exp(b)
        e_ke = jnp.exp(bt - b)
        dec = jnp.exp(bt)
        qr = (qh * e_qr).astype(BF16)
        kr = (kk * e_kr).astype(BF16)
        qe = (qh * e_b).astype(BF16)
        ke = (kk * e_ke).astype(BF16)
        vf = i_ref[...]
        v = vf.astype(BF16)
        dov = do_ref[...].astype(BF16)
        mask = _tri_mask(rev)
        dq_parts, dk_parts, dki_parts, dv_parts, dbt_parts = [], [], [], [], []
        for h in range(nh):
            sl = slice(h * HEAD, (h + 1) * HEAD)
            st = st_ref[h]
            stb = st.astype(BF16)
            dst = ds_scr[h]
            dstb = dst.astype(BF16)
            a = jnp.where(mask, _nt(qr[:, sl], kr[:, sl]), 0.0).astype(BF16)
            da = jnp.where(mask, _nt(dov[:, sl], v[:, sl]), 0.0).astype(BF16)
            dv_parts.append(_tn(a, dov[:, sl]) + _nt(ke[:, sl], dstb))
            dq_h = _nn(da, kr[:, sl]) * e_qr[:, sl] + _nn(dov[:, sl], stb) * e_b[:, sl]
            dk_inter = _nn(v[:, sl], dstb) * e_ke[:, sl]
            dk_h = _tn(da, qr[:, sl]) * e_kr[:, sl] + dk_inter
            dq_parts.append(dq_h)
            dk_parts.append(dk_h)
            dki_parts.append(dk_inter)
            dbt_parts.append(dec[:, sl] * jnp.sum(st * dst, axis=0, keepdims=True))
            ds_scr[h] = dst * dec[:, sl] + _tn(dov[:, sl], qe[:, sl])
        dq = jnp.concatenate(dq_parts, axis=1)
        dk = jnp.concatenate(dk_parts, axis=1)
        dki = jnp.concatenate(dki_parts, axis=1)
        dv = jnp.concatenate(dv_parts, axis=1)
        dbt = jnp.concatenate(dbt_parts, axis=1) + jnp.sum(kk * dki, axis=0, keepdims=True)
        db = qh * dq - kk * dk
        dg = _cumsum_rows(db, not rev) + dbt
        df = dg / fg - dk
        dz_ref[...] = (df * (1.0 - lb) * sig * (1.0 - sig)).astype(BF16)
        dlb_ref[...] += jnp.sum(df * (1.0 - sig), axis=0, keepdims=True)
        dqr = dq * _dsilu(q_ref[...])
        if has_add:
            dqr = dqr + dqa_ref[...]
            dv = dv + dia_ref[...]
        dq_ref[...] = dqr
        di_ref[...] = dv

        @pl.when(m == n_all - 1)
        def _():
            if layer == 0:
                dlb_ref[...] = jnp.zeros_like(dlb_ref)
            else:
                dlb_ref[...] = dlb_ref[...] * lb * (1.0 - lb)

    cspec = lambda col: pl.BlockSpec((HG_CHUNK, D), lambda m: (chunk(m), col))
    ins = [parts, parts, parts, hlb, do, states]
    specs = [cspec(0), cspec(fcol), cspec(3), pl.BlockSpec((2, D), lambda m: (0, 1 if rev else 0)), cspec(0),
             pl.BlockSpec((None, nh, HEAD, HEAD), lambda m: (step(m), 0, 0, 0))]
    if has_add:
        ins += [dq_add, di_add]
        specs += [cspec(0), cspec(0)]
    return pl.pallas_call(
        body, name=name, grid=(n_all,), in_specs=specs,
        out_specs=[cspec(0), cspec(0), cspec(0), pl.BlockSpec((1, D), lambda m: (0, 0))],
        out_shape=[jax.ShapeDtypeStruct((T, D), F32), jax.ShapeDtypeStruct((T, D), BF16),
                   jax.ShapeDtypeStruct((T, D), F32), jax.ShapeDtypeStruct((1, D), F32)],
        scratch_shapes=[pltpu.VMEM((nh, HEAD, HEAD), F32)],
        compiler_params=_params("arbitrary"),
    )(*ins)


def _sgu_ln(v_ref, lnw_ref, lnb_ref):
    gv = _gelu(v_ref[...])
    mu = jnp.mean(gv, axis=-1, keepdims=True)
    xc = gv - mu
    rstd = lax.rsqrt(jnp.mean(xc * xc, axis=-1, keepdims=True) + LN_EPS)
    xh = xc * rstd
    return xh, rstd, xh * lnw_ref[...] + lnb_ref[...]


def _sgu_fwd(parts, lnw, lnb, w, bt, name):
    T = parts.shape[0]
    D = lnw.shape[1]
    G = D // HEAD

    def body(u_ref, v_ref, lnw_ref, lnb_ref, w_ref, bt_ref, ya_ref):
        gu = _gelu(u_ref[...])
        _, _, vn = _sgu_ln(v_ref, lnw_ref, lnb_ref)
        vnb = vn.astype(BF16)
        for g in range(G):
            sl = slice(g * HEAD, (g + 1) * HEAD)
            mixed = _nn(w_ref[g], vnb[:, sl]) + bt_ref[:, g:g + 1]
            ya_ref[:, sl] = (gu[:, sl] * mixed).astype(BF16)

    return pl.pallas_call(
        body, name=name, grid=(T // SGU_CHUNK,),
        in_specs=[pl.BlockSpec((SGU_CHUNK, D), lambda n: (n, 4)), pl.BlockSpec((SGU_CHUNK, D), lambda n: (n, 5)),
                  pl.BlockSpec((1, D), lambda n: (0, 0)), pl.BlockSpec((1, D), lambda n: (0, 0)),
                  pl.BlockSpec((G, SGU_CHUNK, SGU_CHUNK), lambda n: (0, 0, 0)),
                  pl.BlockSpec((SGU_CHUNK, G), lambda n: (0, 0))],
        out_specs=pl.BlockSpec((SGU_CHUNK, D), lambda n: (n, 0)),
        out_shape=jax.ShapeDtypeStruct((T, D), BF16),
        compiler_params=_params("parallel"),
    )(parts, parts, lnw, lnb, w, bt)


def _sgu_bwd(parts, dya, lnw, lnb, w, bt, name):
    T = parts.shape[0]
    D = lnw.shape[1]
    G = D // HEAD

    def body(u_ref, v_ref, dya_ref, lnw_ref, lnb_ref, w_ref, bt_ref,
             du_ref, dv_ref, dw_ref, dbt_ref, dlnw_ref, dlnb_ref, dvn_scr):
        n = pl.program_id(0)

        @pl.when(n == 0)
        def _():
            dw_ref[...] = jnp.zeros_like(dw_ref)
            dbt_ref[...] = jnp.zeros_like(dbt_ref)
            dlnw_ref[...] = jnp.zeros_like(dlnw_ref)
            dlnb_ref[...] = jnp.zeros_like(dlnb_ref)

        u = u_ref[...]
        gu = _gelu(u)
        xh, rstd, vn = _sgu_ln(v_ref, lnw_ref, lnb_ref)
        vnb = vn.astype(BF16)
        dya = dya_ref[...]
        lane = lax.broadcasted_iota(jnp.int32, (SGU_CHUNK, G), 1)
        dbt = jnp.zeros((SGU_CHUNK, G), F32)
        for g in range(G):
            sl = slice(g * HEAD, (g + 1) * HEAD)
            wg = w_ref[g]
            mixed = _nn(wg, vnb[:, sl]) + bt_ref[:, g:g + 1]
            dmix = dya[:, sl] * gu[:, sl]
            du_ref[:, sl] = (dya[:, sl] * mixed * _dgelu(u[:, sl])).astype(BF16)
            dmb = dmix.astype(BF16)
            dvn_scr[:, sl] = _tn(wg, dmb)
            dw_ref[g] += _nt(dmb, vnb[:, sl])
            dbt = dbt + jnp.where(lane == g, jnp.sum(dmix, axis=1, keepdims=True), 0.0)
        dbt_ref[...] += dbt
        dvn = dvn_scr[...]
        dlnw_ref[...] += jnp.sum(dvn * xh, axis=0, keepdims=True)
        dlnb_ref[...] += jnp.sum(dvn, axis=0, keepdims=True)
        dxh = dvn * lnw_ref[...]
        dgv = rstd * (dxh - jnp.mean(dxh, axis=-1, keepdims=True) - xh * jnp.mean(dxh * xh, axis=-1, keepdims=True))
        dv_ref[...] = (dgv * _dgelu(v_ref[...])).astype(BF16)

    row = lambda col: pl.BlockSpec((SGU_CHUNK, D), lambda n: (n, col))
    vec = pl.BlockSpec((1, D), lambda n: (0, 0))
    wsp = pl.BlockSpec((G, SGU_CHUNK, SGU_CHUNK), lambda n: (0, 0, 0))
    bsp = pl.BlockSpec((SGU_CHUNK, G), lambda n: (0, 0))
    return pl.pallas_call(
        body, name=name, grid=(T // SGU_CHUNK,),
        in_specs=[row(4), row(5), row(0), vec, vec, wsp, bsp],
        out_specs=[row(0), row(0), wsp, bsp, vec, vec],
        out_shape=[jax.ShapeDtypeStruct((T, D), BF16), jax.ShapeDtypeStruct((T, D), BF16),
                   jax.ShapeDtypeStruct((G, SGU_CHUNK, SGU_CHUNK), F32), jax.ShapeDtypeStruct((SGU_CHUNK, G), F32),
                   jax.ShapeDtypeStruct((1, D), F32), jax.ShapeDtypeStruct((1, D), F32)],
        scratch_shapes=[pltpu.VMEM((SGU_CHUNK, D), F32)],
        compiler_params=_params("arbitrary"),
    )(parts, parts, dya, lnw, lnb, w, bt)


TBT = 128


def _rows_weight_spec(wg):
    return pl.BlockSpec(wg.shape, lambda i: (0, 0, 0))


def _full(w_ref):
    return w_ref[...].reshape(w_ref.shape[0] * w_ref.shape[1], w_ref.shape[2])


def _token_out_fwd(o, parts, ya, x, mod, hnw, wa, wb, wo, ctx_rows, name):
    T, D = x.shape
    nh = D // HEAD
    cb = ctx_rows // TBT

    def body(o_ref, og_ref, ga_ref, gb_ref, ya_ref, x_ref, mod_ref, hnw_ref, wa_ref, wb_ref, wo_ref,
             yb_ref, pa_ref, pb_ref, mg_ref, tmo_ref, xm_ref):
        ov = o_ref[...]
        so = _silu(og_ref[...])
        nw = hnw_ref[...]
        for h in range(nh):
            sl = slice(h * HEAD, (h + 1) * HEAD)
            seg = ov[:, sl]
            r = lax.rsqrt(jnp.mean(seg * seg, axis=-1, keepdims=True) + RMS_EPS)
            yb_ref[:, sl] = (seg * r * nw * so[:, sl]).astype(BF16)
        pa = _nn(ya_ref[...], _full(wa_ref))
        pb = _nn(yb_ref[...], _full(wb_ref))
        pa_ref[...] = pa
        pb_ref[...] = pb
        mg = (_sigmoid(ga_ref[...]) * pa + _sigmoid(gb_ref[...]) * pb).astype(BF16)
        mg_ref[...] = mg
        out = _nn(mg, _full(wo_ref))
        tmo_ref[...] = out
        xm_ref[...] = x_ref[...] + mod_ref[2:3, :] * out

    row = lambda col: pl.BlockSpec((TBT, D), lambda i: (i, col))
    wsp = _rows_weight_spec(wa)
    sd = lambda dt: jax.ShapeDtypeStruct((T, D), dt)
    return pl.pallas_call(
        body, name=name, grid=(T // TBT,),
        in_specs=[row(0), row(6), row(7), row(8), row(0), row(0),
                  pl.BlockSpec((None, N_MOD, D), lambda i: (_stream_of(i, cb), 0, 0)),
                  pl.BlockSpec((1, HEAD), lambda i: (0, 0)), wsp, wsp, wsp],
        out_specs=[row(0)] * 6,
        out_shape=[sd(BF16), sd(F32), sd(F32), sd(BF16), sd(F32), sd(F32)],
        compiler_params=_params("parallel"),
    )(o, parts, parts, parts, ya, x, mod, hnw, wa, wb, wo)


def _token_out_bwd(dx, tmo, pa, pb, o, parts, mod, hnw, wa, wb, wo, ctx_rows, name):
    T, D = dx.shape
    nh = D // HEAD
    cb = ctx_rows // TBT

    def body(dx_ref, tmo_ref, pa_ref, pb_ref, o_ref, og_ref, ga_ref, gb_ref, mod_ref, hnw_ref, wa_ref, wb_ref, wo_ref,
             dout_ref, dpa_ref, dpb_ref, dog_ref, dga_ref, dgb_ref, dya_ref, do_ref, dg1_ref, dhnw_ref):
        i = pl.program_id(0)

        @pl.when(i == 0)
        def _():
            dhnw_ref[...] = jnp.zeros_like(dhnw_ref)

        @pl.when((i == 0) | (i == cb))
        def _():
            dg1_ref[...] = jnp.zeros_like(dg1_ref)

        dxv = dx_ref[...]
        dg1_ref[...] += jnp.sum(dxv * tmo_ref[...], axis=0, keepdims=True)
        dout = (dxv * mod_ref[2:3, :]).astype(BF16)
        dout_ref[...] = dout
        dmg = _nt(dout, _full(wo_ref))
        sa = _sigmoid(ga_ref[...])
        sb = _sigmoid(gb_ref[...])
        dpa = (dmg * sa).astype(BF16)
        dpb = (dmg * sb).astype(BF16)
        dpa_ref[...] = dpa
        dpb_ref[...] = dpb
        dga_ref[...] = (dmg * pa_ref[...] * sa * (1.0 - sa)).astype(BF16)
        dgb_ref[...] = (dmg * pb_ref[...] * sb * (1.0 - sb)).astype(BF16)
        dya_ref[...] = _nt(dpa, _full(wa_ref))
        dyb = _nt(dpb, _full(wb_ref))
        og = og_ref[...]
        so = _silu(og)
        dso = _dsilu(og)
        ov = o_ref[...]
        nw = hnw_ref[...]
        dnw = jnp.zeros((1, HEAD), F32)
        for h in range(nh):
            sl = slice(h * HEAD, (h + 1) * HEAD)
            seg = ov[:, sl]
            r = lax.rsqrt(jnp.mean(seg * seg, axis=-1, keepdims=True) + RMS_EPS)
            oh = seg * r
            dn = dyb[:, sl] * so[:, sl]
            dog_ref[:, sl] = (dyb[:, sl] * oh * nw * dso[:, sl]).astype(BF16)
            dnw = dnw + jnp.sum(dn * oh, axis=0, keepdims=True)
            doh = dn * nw
            do_ref[:, sl] = r * (doh - oh * jnp.mean(doh * oh, axis=-1, keepdims=True))
        dhnw_ref[...] += dnw

    row = lambda col: pl.BlockSpec((TBT, D), lambda i: (i, col))
    wsp = _rows_weight_spec(wa)
    sd = lambda dt: jax.ShapeDtypeStruct((T, D), dt)
    return pl.pallas_call(
        body, name=name, grid=(T // TBT,),
        in_specs=[row(0), row(0), row(0), row(0), row(0), row(6), row(7), row(8),
                  pl.BlockSpec((None, N_MOD, D), lambda i: (_stream_of(i, cb), 0, 0)),
                  pl.BlockSpec((1, HEAD), lambda i: (0, 0)), wsp, wsp, wsp],
        out_specs=[row(0)] * 8 + [pl.BlockSpec((None, 1, D), lambda i: (_stream_of(i, cb), 0, 0)),
                                  pl.BlockSpec((1, HEAD), lambda i: (0, 0))],
        out_shape=[sd(BF16)] * 6 + [sd(F32), sd(F32), jax.ShapeDtypeStruct((2, 1, D), F32),
                                    jax.ShapeDtypeStruct((1, HEAD), F32)],
        compiler_params=_params("arbitrary"),
    )(dx, tmo, pa, pb, o, parts, parts, parts, mod, hnw, wa, wb, wo)


def _conv_geometry(i, nb, cb):
    is_ctx = i < cb
    first = (i == 0) | (i == cb)
    last = (i == cb - 1) | (i == nb - 1)
    row = lax.broadcasted_iota(jnp.int32, (TB + 2 * GRID_W, 1), 0)
    w = row & (GRID_W - 1)
    left_ok = (w != 0) | is_ctx
    right_ok = (w != GRID_W - 1) | is_ctx
    return is_ctx, first, last, left_ok, right_ok


def _ext(p_ref, m_ref, n_ref, first, last):
    return jnp.concatenate([jnp.where(first, 0.0, p_ref[...]), m_ref[...], jnp.where(last, 0.0, n_ref[...])], axis=0)


def _shift_prev(e, ok):
    return jnp.where(ok, pltpu.roll(e, 1, 0), 0.0)


def _shift_next(e, ok):
    return jnp.where(ok, pltpu.roll(e, e.shape[0] - 1, 0), 0.0)


def _halo_specs(cbk, n64, coff=0):
    r = TB // GRID_W
    prev = pl.BlockSpec((GRID_W, cbk), lambda j, i: (jnp.maximum(r * i - 1, 0), j + coff))
    main = pl.BlockSpec((TB, cbk), lambda j, i: (i, j + coff))
    nxt = pl.BlockSpec((GRID_W, cbk), lambda j, i: (jnp.minimum(r * i + r, n64 - 1), j + coff))
    return [prev, main, nxt]


def _conv_cblock(dff):
    return _tile(dff, 1408)


def _conv_fwd(up, cw, cbias, ctx_rows, name):
    T, dff = up.shape[0], up.shape[1] // 2
    cbk = _conv_cblock(dff)
    nb, cb = T // TB, ctx_rows // TB
    nvb = dff // cbk

    def body(ap_ref, a_ref, an_ref, v_ref, cw_ref, cb_ref, ac_ref, act_ref):
        i = pl.program_id(1)
        is_ctx, first, last, lok, rok = _conv_geometry(i, nb, cb)
        e = _ext(ap_ref, a_ref, an_ref, first, last)
        el = _shift_prev(e, lok)
        er = _shift_next(e, rok)
        cwv = cw_ref[...]

        def comb(dr, lo):
            sl = slice(lo, lo + TB)
            return cwv[3 * dr:3 * dr + 1] * el[sl] + cwv[3 * dr + 1:3 * dr + 2] * e[sl] + cwv[3 * dr + 2:3 * dr + 3] * er[sl]

        out = comb(1, GRID_W) + jnp.where(is_ctx, 0.0, comb(0, 0) + comb(2, 2 * GRID_W))
        a_c = out + cb_ref[...]
        ac_ref[...] = a_c
        act_ref[...] = (_gelu(a_c) * v_ref[...]).astype(BF16)

    main = pl.BlockSpec((TB, cbk), lambda j, i: (i, j))
    return pl.pallas_call(
        body, name=name, grid=(dff // cbk, nb),
        in_specs=_halo_specs(cbk, T // GRID_W) + [pl.BlockSpec((TB, cbk), lambda j, i: (i, j + nvb)),
                                                 pl.BlockSpec((9, cbk), lambda j, i: (0, j)),
                                                 pl.BlockSpec((1, cbk), lambda j, i: (0, j))],
        out_specs=[main, main],
        out_shape=[jax.ShapeDtypeStruct((T, dff), F32), jax.ShapeDtypeStruct((T, dff), BF16)],
        compiler_params=_params("parallel", "parallel"),
    )(up, up, up, up, cw, cbias)


def _conv_bwd(up, ac, dact, cw, ctx_rows, name):
    T, dff = up.shape[0], up.shape[1] // 2
    cbk = _conv_cblock(dff)
    nb, cb = T // TB, ctx_rows // TB
    nvb = dff // cbk

    def body(ap_ref, a_ref, an_ref, vp_ref, v_ref, vn_ref, cp_ref, c_ref, cn_ref, dp_ref, d_ref, dn_ref, cw_ref,
             da_ref, dv_ref, dcw_ref, dcb_ref):
        i = pl.program_id(1)

        @pl.when(i == 0)
        def _():
            dcw_ref[...] = jnp.zeros_like(dcw_ref)
            dcb_ref[...] = jnp.zeros_like(dcb_ref)

        is_ctx, first, last, lok, rok = _conv_geometry(i, nb, cb)
        ace = _ext(cp_ref, c_ref, cn_ref, first, last)
        g = _ext(dp_ref, d_ref, dn_ref, first, last) * _ext(vp_ref, v_ref, vn_ref, first, last) * _dgelu(ace)
        dv_ref[...] = (d_ref[...] * _gelu(c_ref[...])).astype(BF16)
        gm = _shift_prev(g, lok)
        gp = _shift_next(g, rok)
        cwv = cw_ref[...]

        def comb(dr, lo):
            sl = slice(lo, lo + TB)
            return cwv[3 * dr:3 * dr + 1] * gp[sl] + cwv[3 * dr + 1:3 * dr + 2] * g[sl] + cwv[3 * dr + 2:3 * dr + 3] * gm[sl]

        da = comb(1, GRID_W) + jnp.where(is_ctx, 0.0, comb(0, 2 * GRID_W) + comb(2, 0))
        da_ref[...] = da.astype(BF16)
        e = _ext(ap_ref, a_ref, an_ref, first, last)
        taps = [_shift_prev(e, lok), e, _shift_next(e, rok)]
        gmain = g[GRID_W:GRID_W + TB]
        dcb_ref[...] += jnp.sum(gmain, axis=0, keepdims=True)
        vert = jnp.where(is_ctx, 0.0, 1.0)
        for dr in range(3):
            sl = slice(dr * GRID_W, dr * GRID_W + TB)
            for dw in range(3):
                s = jnp.sum(gmain * taps[dw][sl], axis=0, keepdims=True)
                if dr != 1:
                    s = s * vert
                k = 3 * dr + dw
                dcw_ref[k:k + 1, :] += s

    main = pl.BlockSpec((TB, cbk), lambda j, i: (i, j))
    halo = _halo_specs(cbk, T // GRID_W)
    acc9 = pl.BlockSpec((9, cbk), lambda j, i: (0, j))
    acc1 = pl.BlockSpec((1, cbk), lambda j, i: (0, j))
    return pl.pallas_call(
        body, name=name, grid=(dff // cbk, nb),
        in_specs=halo + _halo_specs(cbk, T // GRID_W, nvb) + halo + halo + [acc9],
        out_specs=[main, main, acc9, acc1],
        out_shape=[jax.ShapeDtypeStruct((T, dff), BF16), jax.ShapeDtypeStruct((T, dff), BF16),
                   jax.ShapeDtypeStruct((9, dff), F32), jax.ShapeDtypeStruct((1, dff), F32)],
        compiler_params=_params("parallel", "arbitrary"),
    )(up, up, up, up, up, up, ac, ac, ac, dact, dact, dact, cw)


def _ffn_out_fwd(act, xm, mod, wd, ctx_rows, name):
    T, D = xm.shape
    dff = act.shape[1]
    cb = ctx_rows // TB

    def body(act_ref, x_ref, mod_ref, w_ref, xo_ref, fo_ref):
        out = _nn(act_ref[...], _full(w_ref))
        fo_ref[...] = out
        xo_ref[...] = x_ref[...] + mod_ref[5:6, :] * out

    row = pl.BlockSpec((TB, D), lambda i: (i, 0))
    return pl.pallas_call(
        body, name=name, grid=(T // TB,),
        in_specs=[pl.BlockSpec((TB, dff), lambda i: (i, 0)), row,
                  pl.BlockSpec((None, N_MOD, D), lambda i: (_stream_of(i, cb), 0, 0)),
                  _rows_weight_spec(wd)],
        out_specs=[row, row],
        out_shape=[jax.ShapeDtypeStruct((T, D), F32), jax.ShapeDtypeStruct((T, D), F32)],
        compiler_params=_params("parallel"),
    )(act, xm, mod, wd)


def _ffn_out_bwd(dx, fo, mod, wd, ctx_rows, name):
    T, D = dx.shape
    dff = N_CHIPS * wd.shape[1]
    cb = ctx_rows // TB

    def body(dx_ref, fo_ref, mod_ref, w_ref, dout_ref, dact_ref, dg2_ref):
        i = pl.program_id(0)

        @pl.when((i == 0) | (i == cb))
        def _():
            dg2_ref[...] = jnp.zeros_like(dg2_ref)

        dxv = dx_ref[...]
        dg2_ref[...] += jnp.sum(dxv * fo_ref[...], axis=0, keepdims=True)
        dout = (dxv * mod_ref[5:6, :]).astype(BF16)
        dout_ref[...] = dout
        dact_ref[...] = _nt(dout, _full(w_ref))

    row = pl.BlockSpec((TB, D), lambda i: (i, 0))
    return pl.pallas_call(
        body, name=name, grid=(T // TB,),
        in_specs=[row, row, pl.BlockSpec((None, N_MOD, D), lambda i: (_stream_of(i, cb), 0, 0)),
                  _rows_weight_spec(wd)],
        out_specs=[row, pl.BlockSpec((TB, dff), lambda i: (i, 0)),
                   pl.BlockSpec((None, 1, D), lambda i: (_stream_of(i, cb), 0, 0))],
        out_shape=[jax.ShapeDtypeStruct((T, D), BF16), jax.ShapeDtypeStruct((T, dff), F32),
                   jax.ShapeDtypeStruct((2, 1, D), F32)],
        compiler_params=_params("arbitrary"),
    )(dx, fo, mod, wd)


def _loss_bwd(x, target, fw, ctx_rows, name):
    T, D = x.shape
    cb = ctx_rows // TB

    def body(x_ref, t_ref, fw_ref, dx_ref, loss_ref, dfw_ref):
        i = pl.program_id(0)

        @pl.when(i == 0)
        def _():
            loss_ref[...] = jnp.zeros_like(loss_ref)
            dfw_ref[...] = jnp.zeros_like(dfw_ref)

        @pl.when(i < cb)
        def _():
            dx_ref[...] = jnp.zeros_like(dx_ref)

        @pl.when(i >= cb)
        def _():
            xv = x_ref[...]
            r = lax.rsqrt(jnp.mean(xv * xv, axis=-1, keepdims=True) + RMS_EPS)
            xh = xv * r
            fwv = fw_ref[...]
            err = xh * fwv - t_ref[...]
            loss_ref[...] += (0.5 / D) * jnp.sum(err * err)
            dy = err * (1.0 / D)
            dfw_ref[...] += jnp.sum(dy * xh, axis=0, keepdims=True)
            dxh = dy * fwv
            dx_ref[...] = r * (dxh - xh * jnp.mean(dxh * xh, axis=-1, keepdims=True))

    row = pl.BlockSpec((TB, D), lambda i: (i, 0))
    return pl.pallas_call(
        body, name=name, grid=(T // TB,),
        in_specs=[row, pl.BlockSpec((TB, D), lambda i: (jnp.maximum(i - cb, 0), 0)), pl.BlockSpec((1, D), lambda i: (0, 0))],
        out_specs=[row, pl.BlockSpec((1, 128), lambda i: (0, 0)), pl.BlockSpec((1, D), lambda i: (0, 0))],
        out_shape=[jax.ShapeDtypeStruct((T, D), F32), jax.ShapeDtypeStruct((1, 128), F32),
                   jax.ShapeDtypeStruct((1, D), F32)],
        compiler_params=_params("arbitrary"),
    )(x, target, fw)


def _adamw(w, gs, m, v, name):
    L, R, C = w.shape
    assert len(gs) == L
    rb = _rows_tile(R, max(16, (1 << 18) // C // 16 * 16))
    bc1 = 1.0 - ADAM_B1 ** ADAM_STEP
    bc2 = 1.0 - ADAM_B2 ** ADAM_STEP

    def body(w_ref, m_ref, v_ref, *rest):
        g_refs, (g_ref, d_ref, nm_ref, nv_ref) = rest[:L], rest[L:]
        layer = pl.program_id(0)
        for li in range(L):
            @pl.when(layer == li)
            def _():
                gv = g_refs[li][...]
                g_ref[...] = gv
                nm = ADAM_B1 * m_ref[...] + (1.0 - ADAM_B1) * gv
                nv = ADAM_B2 * v_ref[...] + (1.0 - ADAM_B2) * (gv * gv)
                nm_ref[...] = nm
                nv_ref[...] = nv
                d_ref[...] = -ADAM_LR * ((nm / bc1) / (jnp.sqrt(nv / bc2) + ADAM_EPS) + ADAM_WD * w_ref[...])

    blk = pl.BlockSpec((None, rb, C), lambda l, i: (l, i, 0))
    gblk = pl.BlockSpec((rb, C), lambda l, i: (i, 0))
    sd = jax.ShapeDtypeStruct((L, R, C), F32)
    return pl.pallas_call(
        body, name=name, grid=(L, R // rb), in_specs=[blk] * 3 + [gblk] * L, out_specs=[blk] * 4, out_shape=[sd] * 4,
        compiler_params=_params("parallel", "parallel"),
    )(w, m, v, *gs)


def _local_step(xs, cv, target, W, layer_weights, on_layer_grads, ctx_rows):
    T, D = xs.shape
    depth = W["norm1_w"].shape[0]
    saved = []
    X = xs
    for l in range(depth):
        s = {}
        Wl = layer_weights(l, X)
        mod_all, sa = _mod_fwd(cv, Wl["ada_w"], W["ada_b"][l][None, :], f"mod_fwd_{l}")
        mod = mod_all[:2].reshape(2, N_MOD, D)
        h1 = _norm_mod(X, W["norm1_w"][l][None, :], mod, 0, ctx_rows, f"norm1_{l}")
        parts = _mm_nn_w(h1, Wl["w_in"], F32, f"in_proj_{l}")
        o_f, st_f = _hgrn_fwd(parts, W["hlb"], l, False, ctx_rows, f"hgrn_fwd_f_{l}")
        o, st_b = _hgrn_fwd(parts, W["hlb"], l, True, ctx_rows, f"hgrn_fwd_b_{l}", o_add=o_f)
        ya = _sgu_fwd(parts, W["sgu_ln_w"][l][None, :], W["sgu_ln_b"][l][None, :], W["sgu_w"][l], W["sgu_bt"][l],
                      f"sgu_fwd_{l}")
        yb, pa, pb, mg, tmo, xm = _token_out_fwd(o, parts, ya, X, mod, W["hnw"][l][None, :], Wl["w_a"], Wl["w_b"], Wl["w_o"],
                                                 ctx_rows, f"token_out_fwd_{l}")
        h2 = _norm_mod(xm, W["norm2_w"][l][None, :], mod, 3, ctx_rows, f"norm2_{l}")
        up = _mm_nn_w(h2, Wl["w_up"], F32, f"up_proj_{l}")
        ac, act = _conv_fwd(up, Wl["conv_w"], W["conv_b"][l][None, :], ctx_rows, f"conv_fwd_{l}")
        xo, fo = _ffn_out_fwd(act, xm, mod, Wl["w_down"], ctx_rows, f"ffn_out_fwd_{l}")
        s.update(X=X, Wl=Wl, mod=mod, mod_all=mod_all, sa=sa, h1=h1, parts=parts, o=o, st_f=st_f, st_b=st_b, ya=ya, yb=yb,
                 pa=pa, pb=pb, mg=mg, tmo=tmo, xm=xm, h2=h2, up=up, ac=ac, act=act, fo=fo)
        saved.append(s)
        X = xo

    dX, loss_row, dfw = _loss_bwd(X, target, W["final_norm_w"][None, :], ctx_rows, "loss_bwd")
    G = {k: [None] * depth for k in ("ada_b", "norm1_w", "sgu_ln_w", "sgu_ln_b", "sgu_w", "sgu_b", "hlb1", "hnw", "norm2_w",
                                     "conv_w", "conv_b", "dmod")}
    dcv = jnp.zeros_like(cv)
    for l in reversed(range(depth)):
        s = saved[l]
        mod, Wl = s["mod"], s["Wl"]
        big = {}
        dout2, dact, dg2 = _ffn_out_bwd(dX, s["fo"], mod, Wl["w_down"], ctx_rows, f"ffn_out_bwd_{l}")
        big["w_down"] = _mm_tn(s["act"], dout2, F32, f"dw_down_{l}")
        da, dv, dcw, dcb = _conv_bwd(s["up"], s["ac"], dact, Wl["conv_w"], ctx_rows, f"conv_bwd_{l}")
        G["conv_w"][l], G["conv_b"][l] = dcw, dcb[0]
        dup = jnp.concatenate([da, dv], axis=1)
        big["w_up"] = _mm_tn(s["h2"], dup, F32, f"dw_up_{l}", out_chips=True)
        dh2 = _mm_nt_w(dup, Wl["w_up"], F32, f"dh2_{l}")
        dxm, dm2, dnw2 = _norm_mod_bwd(dh2, s["xm"], dX, W["norm2_w"][l][None, :], mod, 3, ctx_rows, f"norm2_bwd_{l}")
        G["norm2_w"][l] = dnw2[0]
        (dout1, dpa, dpb, dog, dga, dgb, dya, do, dg1, dhnw) = _token_out_bwd(
            dxm, s["tmo"], s["pa"], s["pb"], s["o"], s["parts"], mod, W["hnw"][l][None, :], Wl["w_a"], Wl["w_b"], Wl["w_o"],
            ctx_rows, f"token_out_bwd_{l}")
        G["hnw"][l] = dhnw[0]
        big["w_o"] = _mm_tn(s["mg"], dout1, F32, f"dw_o_{l}")
        big["w_a"] = _mm_tn(s["ya"], dpa, F32, f"dw_a_{l}")
        big["w_b"] = _mm_tn(s["yb"], dpb, F32, f"dw_b_{l}")
        du, dvs, dsw, dsbt, dlnw, dlnb = _sgu_bwd(s["parts"], dya, W["sgu_ln_w"][l][None, :], W["sgu_ln_b"][l][None, :],
                                                  W["sgu_w"][l], W["sgu_bt"][l], f"sgu_bwd_{l}")
        G["sgu_w"][l], G["sgu_b"][l], G["sgu_ln_w"][l], G["sgu_ln_b"][l] = dsw, dsbt.T, dlnw[0], dlnb[0]
        dq_f, dz_f, di_f, dlb_f = _hgrn_bwd(s["parts"], W["hlb"], do, s["st_f"], l, False, ctx_rows, f"hgrn_bwd_f_{l}")
        dq, dz_b, di, dlb_b = _hgrn_bwd(s["parts"], W["hlb"], do, s["st_b"], l, True, ctx_rows, f"hgrn_bwd_b_{l}",
                                        dq_add=dq_f, di_add=di_f)
        G["hlb1"][l] = jnp.concatenate([dlb_f[0], dlb_b[0]])
        dparts = jnp.concatenate([dq.astype(BF16), dz_f, dz_b, di.astype(BF16), du, dvs, dog, dga, dgb], axis=1)
        big["w_in"] = _mm_tn(s["h1"], dparts, F32, f"dw_in_{l}", out_chips=True)
        on_layer_grads(l, big)
        dh1 = _mm_nt_w(dparts, Wl["w_in"], F32, f"dh1_{l}")
        dX, dm1, dnw1 = _norm_mod_bwd(dh1, s["X"], dxm, W["norm1_w"][l][None, :], mod, 0, ctx_rows, f"norm1_bwd_{l}")
        G["norm1_w"][l] = dnw1[0]
        dmod = jnp.concatenate([dm1, dg1, dm2, dg2], axis=1).reshape(2, N_MOD * D)
        dmod16 = jnp.concatenate([dmod, jnp.zeros((cv.shape[0] - 2, N_MOD * D), F32)], axis=0)
        G["ada_b"][l] = dmod[0] + dmod[1]
        G["dmod"][l] = dmod
        dcv = dcv + _cvec_bwd(dmod16, Wl["ada_w"], cv, f"dcvec_{l}")
    G["c_ctx"] = dcv[0]
    G["final_norm_w"] = dfw[0]
    return loss_row[0, 0], dX, G, saved[0]["sa"]


def _chip_peers(x, y, c):
    return [((1 - x, y, c), 2 * (1 - x) + y), ((x, 1 - y, c), 2 * x + 1 - y), ((1 - x, 1 - y, c), 2 * (1 - x) + 1 - y)]


def _rdma_call(ins, out_shapes, plan, n_remote, n_local, name, aliases=None):
    n_in, n_out = len(ins), len(out_shapes)

    def body(*refs):
        in_refs, out_refs = refs[:n_in], refs[n_in:n_in + n_out]
        send_sems, recv_sems, local_sems = refs[n_in + n_out:]
        x, y, c = lax.axis_index("x"), lax.axis_index("y"), lax.axis_index("c")
        remote, local = plan(in_refs, out_refs, x, y, c)
        assert len(remote) == n_remote and len(local) == n_local, (name, len(remote), len(local))
        copies = [pltpu.make_async_copy(s, d, local_sems.at[i]) for i, (s, d) in enumerate(local)]
        copies += [pltpu.make_async_remote_copy(src_ref=s, dst_ref=d, send_sem=send_sems.at[k], recv_sem=recv_sems.at[k],
                                                device_id=dev, device_id_type=pl.DeviceIdType.MESH)
                   for k, (s, d, dev) in enumerate(remote)]
        for cp in copies:
            cp.start()
        for cp in copies:
            cp.wait()

    hbm = pl.BlockSpec(memory_space=pltpu.HBM)
    return pl.pallas_call(
        body, name=name, in_specs=[hbm] * n_in, out_specs=[hbm] * n_out, out_shape=out_shapes,
        scratch_shapes=[pltpu.SemaphoreType.DMA((n_remote,)), pltpu.SemaphoreType.DMA((n_remote,)),
                        pltpu.SemaphoreType.DMA((max(n_local, 1),))],
        input_output_aliases=aliases or {},
    )(*ins)


def _gather_weights(shards, name):
    n = len(shards)
    n_far = (N_CHIPS - 1) * n

    def body(*refs):
        ins, outs = refs[:n], refs[n:2 * n]
        far_send, far_recv, near_send, near_recv, local_sems = refs[2 * n:]
        x, y, c = lax.axis_index("x"), lax.axis_index("y"), lax.axis_index("c")
        me = 2 * x + y
        half = lambda ref: pl.ds(c * (ref.shape[0] // 2), ref.shape[0] // 2)
        own = [pltpu.make_async_copy(s, o.at[me], local_sems.at[t]) for t, (s, o) in enumerate(zip(ins, outs))]
        for cp in own:
            cp.start()
        far, near = [], []
        for p, (dev, idx) in enumerate(_chip_peers(x, y, c)):
            for t, (s, o) in enumerate(zip(ins, outs)):
                k = p * n + t
                far.append(pltpu.make_async_remote_copy(
                    src_ref=s.at[half(s)], dst_ref=o.at[me, half(s)], send_sem=far_send.at[k], recv_sem=far_recv.at[k],
                    device_id=dev, device_id_type=pl.DeviceIdType.MESH))
                near.append(pltpu.make_async_remote_copy(
                    src_ref=o.at[idx, half(s)], dst_ref=o.at[idx, half(s)], send_sem=near_send.at[k],
                    recv_sem=near_recv.at[k], device_id=(x, y, 1 - c), device_id_type=pl.DeviceIdType.MESH))
        for cp in far:
            cp.start()
        for k in range(n_far):
            far[k].wait_recv()
            near[k].start()
        for k in range(n_far):
            near[k].wait_recv()
        for cp in far + near:
            cp.wait_send()
        for cp in own:
            cp.wait()

    hbm = pl.BlockSpec(memory_space=pltpu.HBM)
    sems = pltpu.SemaphoreType.DMA((n_far,))
    return pl.pallas_call(
        body, name=name, in_specs=[hbm] * n, out_specs=[hbm] * n,
        out_shape=[jax.ShapeDtypeStruct((N_CHIPS,) + s.shape, s.dtype) for s in shards],
        scratch_shapes=[sems, sems, sems, sems, pltpu.SemaphoreType.DMA((n,))],
    )(*shards)


def _gather_all(v, name):
    def plan(ins, outs, x, y, c):
        (s,), (o,) = ins, outs
        me = 4 * x + 2 * y + c
        flip = lambda a, f: 1 - a if f else a
        remote = [(s, o.at[me], (flip(x, m & 4), flip(y, m & 2), flip(c, m & 1))) for m in range(1, 8)]
        return remote, [(s, o.at[me])]

    return _rdma_call([v], [jax.ShapeDtypeStruct((8,) + v.shape, v.dtype)], plan, 7, 1, name)[0]


def _reduce_pair(parts, name):
    def plan(ins, outs, x, y, c):
        return [(a.at[j, 1 - c], o.at[j], (x, y, 1 - c)) for a, o in zip(ins, outs) for j in range(N_CHIPS)], []

    shapes = [jax.ShapeDtypeStruct((N_CHIPS,) + a.shape[2:], a.dtype) for a in parts]
    return _rdma_call(parts, shapes, plan, N_CHIPS * len(parts), 0, name)


def _reduce_chips(parts, name):
    def plan(ins, outs, x, y, c):
        me = 2 * x + y
        return [(a.at[idx], o.at[me], dev) for dev, idx in _chip_peers(x, y, c) for a, o in zip(ins, outs)], []

    shapes = [jax.ShapeDtypeStruct(a.shape, a.dtype) for a in parts]
    return _rdma_call(parts, shapes, plan, (N_CHIPS - 1) * len(parts), 0, name)


def _gather_pair(halves, name):
    def plan(ins, outs, x, y, c):
        return [(o.at[c], o.at[c], (x, y, 1 - c)) for o in outs], []

    shapes = [jax.ShapeDtypeStruct(a.shape, a.dtype) for a in halves]
    return _rdma_call(halves, shapes, plan, len(halves), 0, name, aliases={i: i for i in range(len(halves))})


def _sum_block_rows(r, C):
    return _rows_tile(r, max(16, (1 << 18) // C // 16 * 16))


def _sum_pair(a, recv, cidx, name):
    nch, _, r, C = a.shape
    rb = _sum_block_rows(r, C)

    def body(c_ref, a_ref, r_ref, o_ref):
        o_ref[...] = (a_ref[...] + r_ref[...]).astype(BF16)

    blk = pl.BlockSpec((None, rb, C), lambda j, i, c: (j, i, 0))
    return pl.pallas_call(
        body, name=name,
        grid_spec=pltpu.PrefetchScalarGridSpec(
            num_scalar_prefetch=1, grid=(nch, r // rb),
            in_specs=[pl.BlockSpec((None, None, rb, C), lambda j, i, c: (j, c[0], i, 0)), blk], out_specs=blk),
        out_shape=jax.ShapeDtypeStruct((nch, r, C), BF16),
        compiler_params=_params("parallel", "parallel"),
    )(cidx, a, recv)


def _sum_chips(mine, recv, ids, name):
    nch, r, C = recv.shape
    rb = _sum_block_rows(r, C)

    def body(ids_ref, m_ref, *rest):
        r_refs, o_ref = rest[:nch], rest[nch]
        chip = ids_ref[1]
        own = m_ref[...].astype(F32)
        acc = jnp.where(chip == 0, own, r_refs[0][...].astype(F32))
        for q in range(1, nch):
            acc = acc + jnp.where(chip == q, own, r_refs[q][...].astype(F32))
        o_ref[...] = acc

    def slot(q):
        return pl.BlockSpec((None, rb, C), lambda i, ids: (jnp.where(ids[1] == q, (q + 1) % nch, q), i, 0))

    return pl.pallas_call(
        body, name=name,
        grid_spec=pltpu.PrefetchScalarGridSpec(
            num_scalar_prefetch=1, grid=(r // rb,),
            in_specs=[pl.BlockSpec((None, rb, C), lambda i, ids: (ids[1], i, 0))] + [slot(q) for q in range(nch)],
            out_specs=pl.BlockSpec((None, rb, C), lambda i, ids: (ids[0], i, 0))),
        out_shape=jax.ShapeDtypeStruct((N_CORES, r, C), F32),
        compiler_params=_params("parallel"),
    )(ids, mine, *([recv] * nch))


PACK_COLS = 1024
_SHARDED = ("ada_w", "w_in", "w_branch_a", "w_branch_b", "w_out", "ffn_w_up", "ffn_w_down")
_LAYER_KEYS = ("ada_w", "w_in", "w_a", "w_b", "w_o", "w_up", "w_down")
_SMALL = ("c_ctx", "ada_b", "norm1_w", "sgu_ln_w", "sgu_ln_b", "sgu_w", "sgu_b", "hgrn_lower_bounds", "hgrn_norm_w",
          "norm2_w", "ffn_conv_b", "final_norm_w")
_ORDER = ("c_ctx", "ada_w", "ada_b", "norm1_w", "w_in", "sgu_ln_w", "sgu_ln_b", "sgu_w", "sgu_b", "hgrn_lower_bounds",
          "hgrn_norm_w", "w_branch_a", "w_branch_b", "w_out", "norm2_w", "ffn_w_up", "ffn_conv_w", "ffn_conv_b",
          "ffn_w_down", "final_norm_w")


def _pad_to(v, n):
    return jnp.concatenate([v, jnp.zeros((n - v.shape[0],), v.dtype)]) if v.shape[0] < n else v


def _round_up(n, m):
    return (n + m - 1) // m * m


def _pack(arrays, n_pad):
    flat = jnp.concatenate([a.reshape(-1) for a in arrays])
    return _pad_to(flat, n_pad)


def _unpack(flat, like):
    out, off = [], 0
    for a in like:
        out.append(flat[off:off + a.size].reshape(a.shape))
        off += a.size
    return out


def kernel(x, c, ctx, c_ctx, ada_w, ada_b, norm1_w, w_in, sgu_ln_w, sgu_ln_b, sgu_w, sgu_b, hgrn_lower_bounds, hgrn_norm_w, w_branch_a, w_branch_b, w_out, norm2_w, ffn_w_up, ffn_conv_w, ffn_conv_b, ffn_w_down, final_norm_w, loss_target, m_c_ctx, m_ada_w, m_ada_b, m_norm1_w, m_w_in, m_sgu_ln_w, m_sgu_ln_b, m_sgu_w, m_sgu_b, m_hgrn_lower_bounds, m_hgrn_norm_w, m_w_branch_a, m_w_branch_b, m_w_out, m_norm2_w, m_ffn_w_up, m_ffn_conv_w, m_ffn_conv_b, m_ffn_w_down, m_final_norm_w, v_c_ctx, v_ada_w, v_ada_b, v_norm1_w, v_w_in, v_sgu_ln_w, v_sgu_ln_b, v_sgu_w, v_sgu_b, v_hgrn_lower_bounds, v_hgrn_norm_w, v_w_branch_a, v_w_branch_b, v_w_out, v_norm2_w, v_ffn_w_up, v_ffn_conv_w, v_ffn_conv_b, v_ffn_w_down, v_final_norm_w):
    w = dict(c_ctx=c_ctx, ada_w=ada_w, ada_b=ada_b, norm1_w=norm1_w, w_in=w_in, sgu_ln_w=sgu_ln_w, sgu_ln_b=sgu_ln_b,
             sgu_w=sgu_w, sgu_b=sgu_b, hgrn_lower_bounds=hgrn_lower_bounds, hgrn_norm_w=hgrn_norm_w, w_branch_a=w_branch_a,
             w_branch_b=w_branch_b, w_out=w_out, norm2_w=norm2_w, ffn_w_up=ffn_w_up, ffn_conv_w=ffn_conv_w,
             ffn_conv_b=ffn_conv_b, ffn_w_down=ffn_w_down, final_norm_w=final_norm_w)
    mom = dict(zip(_ORDER, (m_c_ctx, m_ada_w, m_ada_b, m_norm1_w, m_w_in, m_sgu_ln_w, m_sgu_ln_b, m_sgu_w, m_sgu_b,
                            m_hgrn_lower_bounds, m_hgrn_norm_w, m_w_branch_a, m_w_branch_b, m_w_out, m_norm2_w, m_ffn_w_up,
                            m_ffn_conv_w, m_ffn_conv_b, m_ffn_w_down, m_final_norm_w)))
    var = dict(zip(_ORDER, (v_c_ctx, v_ada_w, v_ada_b, v_norm1_w, v_w_in, v_sgu_ln_w, v_sgu_ln_b, v_sgu_w, v_sgu_b,
                            v_hgrn_lower_bounds, v_hgrn_norm_w, v_w_branch_a, v_w_branch_b, v_w_out, v_norm2_w, v_ffn_w_up,
                            v_ffn_conv_w, v_ffn_conv_b, v_ffn_w_down, v_final_norm_w)))
    depth, D = norm1_w.shape
    dff = ffn_conv_b.shape[1]
    ctx_rows, seq = ctx.shape[1], x.shape[1]

    assert depth == 2, "the lower-bound softmax is written for two layers"
    core = lax.axis_index("c")
    chip = 2 * lax.axis_index("x") + lax.axis_index("y")
    ids = jnp.stack([core, chip]).astype(jnp.int32)

    conv_full = []

    def layer_weights(l, after):
        shards = [w[k][l].astype(BF16) for k in _SHARDED] + ([ffn_conv_w] if l == 0 else [])
        got = _gather_weights(shards, f"gather_weights_{l}")
        if l == 0:
            conv_full.append(jnp.transpose(got[-1], (1, 2, 3, 0, 4)).reshape(depth, 9, dff))
        return dict(zip(_LAYER_KEYS, got), conv_w=conv_full[0][l])

    W = dict(ada_b=ada_b, norm1_w=norm1_w, sgu_ln_w=sgu_ln_w, sgu_ln_b=sgu_ln_b, sgu_w=sgu_w.astype(BF16),
             sgu_bt=jnp.swapaxes(sgu_b, 1, 2), hlb=hgrn_lower_bounds, hnw=hgrn_norm_w, norm2_w=norm2_w, conv_b=ffn_conv_b,
             final_norm_w=final_norm_w)
    big = {}
    xs = jnp.concatenate([ctx[0], x[0]], axis=0)
    cv = jnp.concatenate([c_ctx[None, :], c, jnp.zeros((14, D), F32)], axis=0)
    loss_local, dxs, G, sa = _local_step(xs, cv, loss_target[0], W, layer_weights, big.__setitem__, ctx_rows)
    loss = lax.psum(loss_local, ("x", "y", "c"))
    grad_x = dxs[ctx_rows:][None]

    pad8 = lambda a: jnp.pad(a, ((0, 8 - a.shape[0]), (0, 0)))
    fact = jnp.concatenate([pad8(sa[1:2].astype(F32))] + [pad8(G["dmod"][l][1].reshape(N_MOD, D)) for l in range(depth)]
                           + [pad8(G["dmod"][l][0].reshape(N_MOD, D)) for l in range(depth)], axis=0)
    facts = _gather_all(fact, "gather_mod_factors")
    lhs = jnp.concatenate([facts[:, 0].astype(BF16), jnp.broadcast_to(sa[0:1], (8, D))], axis=0)
    ada_cols = N_MOD * D // N_CHIPS
    g_ada = []
    for l in range(depth):
        lo_x, lo_c = 8 * (1 + l), 8 * (1 + depth + l)
        rhs = jnp.concatenate([facts[:, lo_x:lo_x + N_MOD].reshape(8, N_MOD * D),
                               facts[:, lo_c:lo_c + N_MOD].reshape(8, N_MOD * D)], axis=0)
        rhs = lax.dynamic_slice_in_dim(rhs, chip * ada_cols, ada_cols, axis=1).astype(BF16)
        g_ada.append(_mm_tn(lhs, rhs, F32, f"dw_ada_{l}"))

    parts = []
    for l in range(depth):
        for k in _LAYER_KEYS[1:]:
            g = big[l][k]
            cols = g.shape[-1]
            parts.append(g.reshape(N_CHIPS, N_CORES, g.size // (N_CHIPS * N_CORES * cols), cols))
    dh = G["hlb1"][depth - 1]
    small_like = [w[k] for k in _SMALL] + [jnp.zeros((depth, 9, dff), F32)]
    small = [G["c_ctx"], jnp.stack(G["ada_b"]), jnp.stack(G["norm1_w"]), jnp.stack(G["sgu_ln_w"]), jnp.stack(G["sgu_ln_b"]),
             jnp.stack(G["sgu_w"]), jnp.stack(G["sgu_b"]), jnp.stack([-dh, dh]), jnp.stack(G["hnw"]), jnp.stack(G["norm2_w"]),
             jnp.stack(G["conv_b"]), G["final_norm_w"], jnp.stack(G["conv_w"])]
    n_small = sum(a.size for a in small)
    n_small_pad = _round_up(n_small, N_CORES * 16 * PACK_COLS)
    small_rows = n_small_pad // (N_CORES * PACK_COLS)
    parts.append(jnp.broadcast_to(_pack(small, n_small_pad).reshape(1, N_CORES, small_rows, PACK_COLS),
                                  (N_CHIPS, N_CORES, small_rows, PACK_COLS)))

    other = _reduce_pair(parts, "reduce_pair")
    pair_sums = [_sum_pair(a, o, ids, f"sum_pair_{i}") for i, (a, o) in enumerate(zip(parts, other))]
    from_chips = _reduce_chips(pair_sums, "reduce_chips")
    halves = [_sum_chips(p, r, ids, f"sum_chips_{i}") for i, (p, r) in enumerate(zip(pair_sums, from_chips))]
    reduced = _gather_pair(halves, "gather_pair")

    g_small = _unpack(reduced[-1].reshape(-1), small_like)
    grads = dict(zip(_SMALL, g_small[:-1]))
    g_conv = lax.dynamic_slice_in_dim(g_small[-1].reshape(depth, 3, 3, dff), chip * (dff // N_CHIPS), dff // N_CHIPS, axis=3)

    delta, new_m, new_v = {}, {}, {}
    n_red = len(_LAYER_KEYS) - 1
    for i, k in enumerate(_SHARDED):
        shp = w[k].shape
        gs = g_ada if i == 0 else [reduced[n_red * l + i - 1].reshape(shp[1:]) for l in range(depth)]
        grads[k], delta[k], new_m[k], new_v[k] = _adamw(w[k], gs, mom[k], var[k], f"adamw_{k}")
    packed = _SMALL + ("ffn_conv_w",)
    n_pad = _round_up(sum(w[k].size for k in packed), 16 * PACK_COLS)
    pack = lambda t: _pack([t[k] for k in packed], n_pad).reshape(1, -1, PACK_COLS)
    grads["ffn_conv_w"] = g_conv
    _, d, nm, nv = _adamw(pack(w), [pack(grads)[0]], pack(mom), pack(var), "adamw_packed")
    like = [w[k] for k in packed]
    for src, dst in ((d, delta), (nm, new_m), (nv, new_v)):
        dst.update(zip(packed, _unpack(src.reshape(-1), like)))

    return (loss, grad_x, *[grads[k] for k in _ORDER], *[delta[k] for k in _ORDER], *[new_m[k] for k in _ORDER],
            *[new_v[k] for k in _ORDER])
```

```python
import functools

import jax
import jax.numpy as jnp
from jax import lax
from jax.experimental import pallas as pl
from jax.experimental.pallas import tpu as pltpu

F32 = jnp.float32
BF16 = jnp.bfloat16

GRID_W = 64
HG_CHUNK = 64
SGU_CHUNK = 128
HEAD = 128
TB = 256
N_MOD = 6
RMS_EPS = 1e-6
LN_EPS = 1e-5
VMEM_LIMIT = 48 * 1024 * 1024
N_CHIPS = 4
N_CORES = 2

ADAM_LR = 0.001
ADAM_B1 = 0.9
ADAM_B2 = 0.999
ADAM_EPS = 1e-08
ADAM_WD = 0.01
ADAM_STEP = 10

_GELU_C = 0.7978845608028654
_GELU_A = 0.044715


def _sigmoid(x):
    return 1.0 / (1.0 + jnp.exp(-x))


def _silu(x):
    return x * _sigmoid(x)


def _dsilu(x):
    s = _sigmoid(x)
    return s * (1.0 + x * (1.0 - s))


def _gelu(x):
    return 0.5 * x * (1.0 + jnp.tanh(_GELU_C * (x + _GELU_A * x * x * x)))


def _dgelu(x):
    t = jnp.tanh(_GELU_C * (x + _GELU_A * x * x * x))
    return 0.5 * (1.0 + t) + 0.5 * x * (1.0 - t * t) * _GELU_C * (1.0 + 3.0 * _GELU_A * x * x)


def _dot(a, b, ca, cb):
    return lax.dot_general(a, b, (((ca,), (cb,)), ((), ())), preferred_element_type=F32)


def _nn(a, b):
    return _dot(a, b, 1, 0)


def _nt(a, b):
    return _dot(a, b, 1, 1)


def _tn(a, b):
    return _dot(a, b, 0, 0)


def _params(*sem):
    return pltpu.CompilerParams(dimension_semantics=sem if sem else None, vmem_limit_bytes=VMEM_LIMIT)


def _stream_of(i, ctx_blocks):
    return (i >= ctx_blocks).astype(jnp.int32)


def _mm(a, b, mode, tm, tn, tk, out_dtype, name, add=None, b_chips=False, out_chips=False):
    if not b_chips:
        bshape = b.shape
    else:
        bshape = (b.shape[1], N_CHIPS * b.shape[2])
    if mode == "nn":
        (M, K), (K2, N) = a.shape, bshape
    elif mode == "nt":
        (M, K), (N, K2) = a.shape, bshape
    else:
        (K, M), (K2, N) = a.shape, bshape
    assert K == K2 and M % tm == 0 and N % tn == 0 and K % tk == 0, (name, a.shape, b.shape, tm, tn, tk)
    nk = K // tk
    if mode == "tn":
        a_spec = pl.BlockSpec((tk, tm), lambda j, i, k: (k, i))
    else:
        a_spec = pl.BlockSpec((tm, tk), lambda j, i, k: (i, k))
    if not b_chips:
        if mode == "nt":
            b_spec = pl.BlockSpec((tn, tk), lambda j, i, k: (j, k))
        else:
            b_spec = pl.BlockSpec((tk, tn), lambda j, i, k: (k, j))
    else:
        cols = b.shape[2]
        if mode == "nn":
            per = cols // tn
            assert cols % tn == 0
            b_spec = pl.BlockSpec((None, tk, tn), lambda j, i, k: (j // per, k, j % per))
        else:
            per = cols // tk
            assert mode == "nt" and cols % tk == 0
            b_spec = pl.BlockSpec((None, tn, tk), lambda j, i, k: (k // per, j, k % per))
    if out_chips:
        per_o = (N // N_CHIPS) // tn
        assert (N // N_CHIPS) % tn == 0 and add is None
        o_spec = pl.BlockSpec((None, tm, tn), lambda j, i, k: (j // per_o, i, j % per_o))
        o_shape = (N_CHIPS, M, N // N_CHIPS)
    else:
        o_spec = pl.BlockSpec((tm, tn), lambda j, i, k: (i, j))
        o_shape = (M, N)
    ca, cb = {"nn": (1, 0), "nt": (1, 1), "tn": (0, 0)}[mode]

    def body(a_ref, b_ref, *rest):
        if add is None:
            o_ref, acc = rest
        else:
            add_ref, o_ref, acc = rest
        k = pl.program_id(2)

        @pl.when(k == 0)
        def _():
            acc[...] = jnp.zeros_like(acc)

        acc[...] += _dot(a_ref[...], b_ref[...], ca, cb)

        @pl.when(k == nk - 1)
        def _():
            r = acc[...]
            if add is not None:
                r = r + add_ref[...]
            o_ref[...] = r.astype(out_dtype)

    ins = [a, b] + ([] if add is None else [add])
    specs = [a_spec, b_spec] + ([] if add is None else [o_spec])
    return pl.pallas_call(
        body, name=name, grid=(N // tn, M // tm, nk), in_specs=specs, out_specs=o_spec,
        out_shape=jax.ShapeDtypeStruct(o_shape, out_dtype),
        scratch_shapes=[pltpu.VMEM((tm, tn), F32)],
        compiler_params=_params("parallel", "parallel", "arbitrary"),
    )(*ins)


def _tile(n, pref):
    if n <= pref:
        return n
    best = None
    for t in range(128, pref + 1, 128):
        if n % t == 0:
            best = t
    assert best is not None, (n, pref)
    return best


def _rows_tile(n, pref):
    if n <= pref:
        return n
    best = None
    for t in range(16, pref + 1, 16):
        if n % t == 0:
            best = t
    assert best is not None, (n, pref)
    return best


def _mm_nn_w(a, wg, out_dtype, name):
    M, K = a.shape
    return _mm(a, wg, "nn", _rows_tile(M, 512), _tile(wg.shape[2], 1536), _tile(K, 1536), out_dtype, name, b_chips=True)


def _mm_nt_w(a, wg, out_dtype, name):
    M, K = a.shape
    return _mm(a, wg, "nt", _rows_tile(M, 1088), _tile(wg.shape[1], 1024), _tile(wg.shape[2], 1536), out_dtype, name,
               b_chips=True)


def _mm_tn(a, b, out_dtype, name, out_chips=False):
    K, M = a.shape
    N = b.shape[1]
    ncol = N // N_CHIPS if out_chips else N
    tm, tn = _tile(M, 1408), _tile(ncol, 1408)
    if tm * tn > 1408 * 1152:
        tn = _tile(ncol, 1152)
    return _mm(a, b, "tn", tm, tn, _rows_tile(K, 2176), out_dtype, name, out_chips=out_chips)


def _mod_fwd(cv, wg, b, name):
    R, D = cv.shape
    tn = wg.shape[2]
    N = N_CHIPS * tn

    def body(cv_ref, w_ref, b_ref, mod_ref, sa_ref):
        sa = _silu(cv_ref[...]).astype(BF16)
        sa_ref[...] = sa
        mod_ref[...] = _nn(sa, w_ref[...]) + b_ref[...]

    return pl.pallas_call(
        body, name=name, grid=(N_CHIPS,),
        in_specs=[pl.BlockSpec((R, D), lambda j: (0, 0)), pl.BlockSpec((None, D, tn), lambda j: (j, 0, 0)),
                  pl.BlockSpec((1, tn), lambda j: (0, j))],
        out_specs=[pl.BlockSpec((R, tn), lambda j: (0, j)), pl.BlockSpec((R, D), lambda j: (0, 0))],
        out_shape=[jax.ShapeDtypeStruct((R, N), F32), jax.ShapeDtypeStruct((R, D), BF16)],
        compiler_params=_params("arbitrary"),
    )(cv, wg, b)


def _cvec_bwd(dmod, wg, cv, name):
    R, N = dmod.shape
    D = wg.shape[1]
    tk = wg.shape[2]
    nk = N_CHIPS

    def body(dm_ref, w_ref, cv_ref, o_ref):
        k = pl.program_id(0)

        @pl.when(k == 0)
        def _():
            o_ref[...] = jnp.zeros_like(o_ref)

        o_ref[...] += _nt(dm_ref[...].astype(BF16), w_ref[...])

        @pl.when(k == nk - 1)
        def _():
            o_ref[...] = o_ref[...] * _dsilu(cv_ref[...])

    return pl.pallas_call(
        body, name=name, grid=(nk,),
        in_specs=[pl.BlockSpec((R, tk), lambda k: (0, k)), pl.BlockSpec((None, D, tk), lambda k: (k, 0, 0)),
                  pl.BlockSpec((R, D), lambda k: (0, 0))],
        out_specs=pl.BlockSpec((R, D), lambda k: (0, 0)),
        out_shape=jax.ShapeDtypeStruct((R, D), F32),
        compiler_params=_params("arbitrary"),
    )(dmod, wg, cv)


def _norm_mod(x, nw, mod, which, ctx_rows, name):
    T, D = x.shape
    cb = ctx_rows // TB

    def body(x_ref, nw_ref, mod_ref, h_ref):
        xv = x_ref[...]
        r = lax.rsqrt(jnp.mean(xv * xv, axis=-1, keepdims=True) + RMS_EPS)
        y = xv * r * nw_ref[...]
        sh = mod_ref[which:which + 1, :]
        sc = mod_ref[which + 1:which + 2, :]
        h_ref[...] = (y * (1.0 + sc) + sh).astype(BF16)

    return pl.pallas_call(
        body, name=name, grid=(T // TB,),
        in_specs=[pl.BlockSpec((TB, D), lambda i: (i, 0)), pl.BlockSpec((1, D), lambda i: (0, 0)),
                  pl.BlockSpec((None, N_MOD, D), lambda i: (_stream_of(i, cb), 0, 0))],
        out_specs=pl.BlockSpec((TB, D), lambda i: (i, 0)),
        out_shape=jax.ShapeDtypeStruct((T, D), BF16),
        compiler_params=_params("parallel"),
    )(x, nw, mod)


def _norm_mod_bwd(dh, x, dres, nw, mod, which, ctx_rows, name):
    T, D = x.shape
    cb = ctx_rows // TB

    def body(dh_ref, x_ref, dres_ref, nw_ref, mod_ref, dx_ref, dm_ref, dnw_ref):
        i = pl.program_id(0)

        @pl.when(i == 0)
        def _():
            dnw_ref[...] = jnp.zeros_like(dnw_ref)

        @pl.when((i == 0) | (i == cb))
        def _():
            dm_ref[...] = jnp.zeros_like(dm_ref)

        xv = x_ref[...]
        dh = dh_ref[...]
        r = lax.rsqrt(jnp.mean(xv * xv, axis=-1, keepdims=True) + RMS_EPS)
        xh = xv * r
        nwv = nw_ref[...]
        sc = mod_ref[which + 1:which + 2, :]
        y = xh * nwv
        dm_ref[0:1, :] += jnp.sum(dh, axis=0, keepdims=True)
        dm_ref[1:2, :] += jnp.sum(dh * y, axis=0, keepdims=True)
        dy = dh * (1.0 + sc)
        dnw_ref[...] += jnp.sum(dy * xh, axis=0, keepdims=True)
        dxh = dy * nwv
        dx_ref[...] = dres_ref[...] + r * (dxh - xh * jnp.mean(dxh * xh, axis=-1, keepdims=True))

    return pl.pallas_call(
        body, name=name, grid=(T // TB,),
        in_specs=[pl.BlockSpec((TB, D), lambda i: (i, 0)), pl.BlockSpec((TB, D), lambda i: (i, 0)),
                  pl.BlockSpec((TB, D), lambda i: (i, 0)), pl.BlockSpec((1, D), lambda i: (0, 0)),
                  pl.BlockSpec((None, N_MOD, D), lambda i: (_stream_of(i, cb), 0, 0))],
        out_specs=[pl.BlockSpec((TB, D), lambda i: (i, 0)),
                   pl.BlockSpec((None, 2, D), lambda i: (_stream_of(i, cb), 0, 0)),
                   pl.BlockSpec((1, D), lambda i: (0, 0))],
        out_shape=[jax.ShapeDtypeStruct((T, D), F32), jax.ShapeDtypeStruct((2, 2, D), F32),
                   jax.ShapeDtypeStruct((1, D), F32)],
        compiler_params=_params("arbitrary"),
    )(dh, x, dres, nw, mod)


def _scan_chunk(n, rev, n_ctx, n_all):
    if not rev:
        return n
    return jnp.where(n < n_ctx, n_ctx - 1 - n, n_all - 1 + n_ctx - n)


def _cumsum_rows(x, rev):
    rows = x.shape[0]
    row = lax.broadcasted_iota(jnp.int32, (rows, 1), 0)
    s = 1
    while s < rows:
        if not rev:
            x = x + jnp.where(row >= s, pltpu.roll(x, s, 0), 0.0)
        else:
            x = x + jnp.where(row < rows - s, pltpu.roll(x, rows - s, 0), 0.0)
        s *= 2
    return x


def _lower_bound(hlb_ref, layer):
    h = hlb_ref[...]
    if layer == 0:
        return jnp.zeros_like(h[0:1, :])
    return _sigmoid(h[1:2, :] - h[0:1, :])


def _hgrn_gates(q_ref, f_ref, hlb_ref, layer, rev):
    lb = _lower_bound(hlb_ref, layer)
    z = f_ref[...]
    sig = _sigmoid(z)
    fg = lb + (1.0 - lb) * sig
    kk = (1.0 - lb) * (1.0 - sig)
    g = jnp.log(fg)
    b = _cumsum_rows(g, rev)
    bt = jnp.sum(g, axis=0, keepdims=True)
    mid = HG_CHUNK // 2
    r = b[mid:mid + 1, :] if rev else b[mid - 1:mid, :]
    qh = _silu(q_ref[...])
    return lb, sig, fg, kk, b, bt, r, qh


def _tri_mask(rev):
    t = lax.broadcasted_iota(jnp.int32, (HG_CHUNK, HG_CHUNK), 0)
    s = lax.broadcasted_iota(jnp.int32, (HG_CHUNK, HG_CHUNK), 1)
    return (s >= t) if rev else (s <= t)


def _hgrn_fwd(parts, hlb, layer, rev, ctx_rows, name, o_add=None):
    T = parts.shape[0]
    D = hlb.shape[1] // 2
    nh = D // HEAD
    n_all, n_ctx = T // HG_CHUNK, ctx_rows // HG_CHUNK
    chunk = functools.partial(_scan_chunk, rev=rev, n_ctx=n_ctx, n_all=n_all)
    fcol = 2 if rev else 1

    def body(q_ref, f_ref, i_ref, hlb_ref, *rest):
        if o_add is None:
            o_ref, st_ref, s_scr = rest
        else:
            oa_ref, o_ref, st_ref, s_scr = rest
        n = pl.program_id(0)

        @pl.when(n == 0)
        def _():
            s_scr[...] = jnp.zeros_like(s_scr)

        lb, sig, fg, kk, b, bt, r, qh = _hgrn_gates(q_ref, f_ref, hlb_ref, layer, rev)
        qr = (qh * jnp.exp(b - r)).astype(BF16)
        kr = (kk * jnp.exp(r - b)).astype(BF16)
        qe = (qh * jnp.exp(b)).astype(BF16)
        ke = (kk * jnp.exp(bt - b)).astype(BF16)
        dec = jnp.exp(bt)
        v = i_ref[...].astype(BF16)
        mask = _tri_mask(rev)
        for h in range(nh):
            sl = slice(h * HEAD, (h + 1) * HEAD)
            st = s_scr[h]
            st_ref[h] = st
            a = jnp.where(mask, _nt(qr[:, sl], kr[:, sl]), 0.0).astype(BF16)
            o = _nn(a, v[:, sl]) + _nt(qe[:, sl], st.astype(BF16))
            if o_add is not None:
                o = o + oa_ref[:, sl]
            o_ref[:, sl] = o
            s_scr[h] = st * dec[:, sl] + _tn(v[:, sl], ke[:, sl])

    cspec = lambda col: pl.BlockSpec((HG_CHUNK, D), lambda n: (chunk(n), col))
    ins = [parts, parts, parts, hlb]
    specs = [cspec(0), cspec(fcol), cspec(3), pl.BlockSpec((2, D), lambda n: (0, 1 if rev else 0))]
    if o_add is not None:
        ins.append(o_add)
        specs.append(cspec(0))
    return pl.pallas_call(
        body, name=name, grid=(n_all,), in_specs=specs,
        out_specs=[cspec(0), pl.BlockSpec((None, nh, HEAD, HEAD), lambda n: (n, 0, 0, 0))],
        out_shape=[jax.ShapeDtypeStruct((T, D), F32), jax.ShapeDtypeStruct((n_all, nh, HEAD, HEAD), F32)],
        scratch_shapes=[pltpu.VMEM((nh, HEAD, HEAD), F32)],
        compiler_params=_params("arbitrary"),
    )(*ins)


def _hgrn_bwd(parts, hlb, do, states, layer, rev, ctx_rows, name, dq_add=None, di_add=None):
    T = parts.shape[0]
    D = hlb.shape[1] // 2
    nh = D // HEAD
    n_all, n_ctx = T // HG_CHUNK, ctx_rows // HG_CHUNK
    step = lambda m: n_all - 1 - m
    chunk = lambda m: _scan_chunk(step(m), rev, n_ctx, n_all)
    fcol = 2 if rev else 1
    has_add = dq_add is not None

    def body(q_ref, f_ref, i_ref, hlb_ref, do_ref, st_ref, *rest):
        if has_add:
            dqa_ref, dia_ref, dq_ref, dz_ref, di_ref, dlb_ref, ds_scr = rest
        else:
            dq_ref, dz_ref, di_ref, dlb_ref, ds_scr = rest
        m = pl.program_id(0)

        @pl.when(m == 0)
        def _():
            ds_scr[...] = jnp.zeros_like(ds_scr)
            dlb_ref[...] = jnp.zeros_like(dlb_ref)

        lb, sig, fg, kk, b, bt, r, qh = _hgrn_gates(q_ref, f_ref, hlb_ref, layer, rev)
        e_qr = jnp.exp(b - r)
        e_kr = jnp.exp(r - b)
        e_b = jnp.exp(b)
        e_ke = jnp.exp(bt - b)
        dec = jnp.exp(bt)
        qr = (qh * e_qr).astype(BF16)
        kr = (kk * e_kr).astype(BF16)
        qe = (qh * e_b).astype(BF16)
        ke = (kk * e_ke).astype(BF16)
        vf = i_ref[...]
        v = vf.astype(BF16)
        dov = do_ref[...].astype(BF16)
        mask = _tri_mask(rev)
        dq_parts, dk_parts, dki_parts, dv_parts, dbt_parts = [], [], [], [], []
        for h in range(nh):
            sl = slice(h * HEAD, (h + 1) * HEAD)
            st = st_ref[h]
            stb = st.astype(BF16)
            dst = ds_scr[h]
            dstb = dst.astype(BF16)
            a = jnp.where(mask, _nt(qr[:, sl], kr[:, sl]), 0.0).astype(BF16)
            da = jnp.where(mask, _nt(dov[:, sl], v[:, sl]), 0.0).astype(BF16)
            dv_parts.append(_tn(a, dov[:, sl]) + _nt(ke[:, sl], dstb))
            dq_h = _nn(da, kr[:, sl]) * e_qr[:, sl] + _nn(dov[:, sl], stb) * e_b[:, sl]
            dk_inter = _nn(v[:, sl], dstb) * e_ke[:, sl]
            dk_h = _tn(da, qr[:, sl]) * e_kr[:, sl] + dk_inter
            dq_parts.append(dq_h)
            dk_parts.append(dk_h)
            dki_parts.append(dk_inter)
            dbt_parts.append(dec[:, sl] * jnp.sum(st * dst, axis=0, keepdims=True))
            ds_scr[h] = dst * dec[:, sl] + _tn(dov[:, sl], qe[:, sl])
        dq = jnp.concatenate(dq_parts, axis=1)
        dk = jnp.concatenate(dk_parts, axis=1)
        dki = jnp.concatenate(dki_parts, axis=1)
        dv = jnp.concatenate(dv_parts, axis=1)
        dbt = jnp.concatenate(dbt_parts, axis=1) + jnp.sum(kk * dki, axis=0, keepdims=True)
        db = qh * dq - kk * dk
        dg = _cumsum_rows(db, not rev) + dbt
        df = dg / fg - dk
        dz_ref[...] = (df * (1.0 - lb) * sig * (1.0 - sig)).astype(BF16)
        dlb_ref[...] += jnp.sum(df * (1.0 - sig), axis=0, keepdims=True)
        dqr = dq * _dsilu(q_ref[...])
        if has_add:
            dqr = dqr + dqa_ref[...]
            dv = dv + dia_ref[...]
        dq_ref[...] = dqr
        di_ref[...] = dv

        @pl.when(m == n_all - 1)
        def _():
            if layer == 0:
                dlb_ref[...] = jnp.zeros_like(dlb_ref)
            else:
                dlb_ref[...] = dlb_ref[...] * lb * (1.0 - lb)

    cspec = lambda col: pl.BlockSpec((HG_CHUNK, D), lambda m: (chunk(m), col))
    ins = [parts, parts, parts, hlb, do, states]
    specs = [cspec(0), cspec(fcol), cspec(3), pl.BlockSpec((2, D), lambda m: (0, 1 if rev else 0)), cspec(0),
             pl.BlockSpec((None, nh, HEAD, HEAD), lambda m: (step(m), 0, 0, 0))]
    if has_add:
        ins += [dq_add, di_add]
        specs += [cspec(0), cspec(0)]
    return pl.pallas_call(
        body, name=name, grid=(n_all,), in_specs=specs,
        out_specs=[cspec(0), cspec(0), cspec(0), pl.BlockSpec((1, D), lambda m: (0, 0))],
        out_shape=[jax.ShapeDtypeStruct((T, D), F32), jax.ShapeDtypeStruct((T, D), BF16),
                   jax.ShapeDtypeStruct((T, D), F32), jax.ShapeDtypeStruct((1, D), F32)],
        scratch_shapes=[pltpu.VMEM((nh, HEAD, HEAD), F32)],
        compiler_params=_params("arbitrary"),
    )(*ins)


def _sgu_ln(v_ref, lnw_ref, lnb_ref):
    gv = _gelu(v_ref[...])
    mu = jnp.mean(gv, axis=-1, keepdims=True)
    xc = gv - mu
    rstd = lax.rsqrt(jnp.mean(xc * xc, axis=-1, keepdims=True) + LN_EPS)
    xh = xc * rstd
    return xh, rstd, xh * lnw_ref[...] + lnb_ref[...]


def _sgu_fwd(parts, lnw, lnb, w, bt, name):
    T = parts.shape[0]
    D = lnw.shape[1]
    G = D // HEAD

    def body(u_ref, v_ref, lnw_ref, lnb_ref, w_ref, bt_ref, ya_ref):
        gu = _gelu(u_ref[...])
        _, _, vn = _sgu_ln(v_ref, lnw_ref, lnb_ref)
        vnb = vn.astype(BF16)
        for g in range(G):
            sl = slice(g * HEAD, (g + 1) * HEAD)
            mixed = _nn(w_ref[g], vnb[:, sl]) + bt_ref[:, g:g + 1]
            ya_ref[:, sl] = (gu[:, sl] * mixed).astype(BF16)

    return pl.pallas_call(
        body, name=name, grid=(T // SGU_CHUNK,),
        in_specs=[pl.BlockSpec((SGU_CHUNK, D), lambda n: (n, 4)), pl.BlockSpec((SGU_CHUNK, D), lambda n: (n, 5)),
                  pl.BlockSpec((1, D), lambda n: (0, 0)), pl.BlockSpec((1, D), lambda n: (0, 0)),
                  pl.BlockSpec((G, SGU_CHUNK, SGU_CHUNK), lambda n: (0, 0, 0)),
                  pl.BlockSpec((SGU_CHUNK, G), lambda n: (0, 0))],
        out_specs=pl.BlockSpec((SGU_CHUNK, D), lambda n: (n, 0)),
        out_shape=jax.ShapeDtypeStruct((T, D), BF16),
        compiler_params=_params("parallel"),
    )(parts, parts, lnw, lnb, w, bt)


def _sgu_bwd(parts, dya, lnw, lnb, w, bt, name):
    T = parts.shape[0]
    D = lnw.shape[1]
    G = D // HEAD

    def body(u_ref, v_ref, dya_ref, lnw_ref, lnb_ref, w_ref, bt_ref,
             du_ref, dv_ref, dw_ref, dbt_ref, dlnw_ref, dlnb_ref, dvn_scr):
        n = pl.program_id(0)

        @pl.when(n == 0)
        def _():
            dw_ref[...] = jnp.zeros_like(dw_ref)
            dbt_ref[...] = jnp.zeros_like(dbt_ref)
            dlnw_ref[...] = jnp.zeros_like(dlnw_ref)
            dlnb_ref[...] = jnp.zeros_like(dlnb_ref)

        u = u_ref[...]
        gu = _gelu(u)
        xh, rstd, vn = _sgu_ln(v_ref, lnw_ref, lnb_ref)
        vnb = vn.astype(BF16)
        dya = dya_ref[...]
        lane = lax.broadcasted_iota(jnp.int32, (SGU_CHUNK, G), 1)
        dbt = jnp.zeros((SGU_CHUNK, G), F32)
        for g in range(G):
            sl = slice(g * HEAD, (g + 1) * HEAD)
            wg = w_ref[g]
            mixed = _nn(wg, vnb[:, sl]) + bt_ref[:, g:g + 1]
            dmix = dya[:, sl] * gu[:, sl]
            du_ref[:, sl] = (dya[:, sl] * mixed * _dgelu(u[:, sl])).astype(BF16)
            dmb = dmix.astype(BF16)
            dvn_scr[:, sl] = _tn(wg, dmb)
            dw_ref[g] += _nt(dmb, vnb[:, sl])
            dbt = dbt + jnp.where(lane == g, jnp.sum(dmix, axis=1, keepdims=True), 0.0)
        dbt_ref[...] += dbt
        dvn = dvn_scr[...]
        dlnw_ref[...] += jnp.sum(dvn * xh, axis=0, keepdims=True)
        dlnb_ref[...] += jnp.sum(dvn, axis=0, keepdims=True)
        dxh = dvn * lnw_ref[...]
        dgv = rstd * (dxh - jnp.mean(dxh, axis=-1, keepdims=True) - xh * jnp.mean(dxh * xh, axis=-1, keepdims=True))
        dv_ref[...] = (dgv * _dgelu(v_ref[...])).astype(BF16)

    row = lambda col: pl.BlockSpec((SGU_CHUNK, D), lambda n: (n, col))
    vec = pl.BlockSpec((1, D), lambda n: (0, 0))
    wsp = pl.BlockSpec((G, SGU_CHUNK, SGU_CHUNK), lambda n: (0, 0, 0))
    bsp = pl.BlockSpec((SGU_CHUNK, G), lambda n: (0, 0))
    return pl.pallas_call(
        body, name=name, grid=(T // SGU_CHUNK,),
        in_specs=[row(4), row(5), row(0), vec, vec, wsp, bsp],
        out_specs=[row(0), row(0), wsp, bsp, vec, vec],
        out_shape=[jax.ShapeDtypeStruct((T, D), BF16), jax.ShapeDtypeStruct((T, D), BF16),
                   jax.ShapeDtypeStruct((G, SGU_CHUNK, SGU_CHUNK), F32), jax.ShapeDtypeStruct((SGU_CHUNK, G), F32),
                   jax.ShapeDtypeStruct((1, D), F32), jax.ShapeDtypeStruct((1, D), F32)],
        scratch_shapes=[pltpu.VMEM((SGU_CHUNK, D), F32)],
        compiler_params=_params("arbitrary"),
    )(parts, parts, dya, lnw, lnb, w, bt)


TBT = 128


def _rows_weight_spec(wg):
    return pl.BlockSpec(wg.shape, lambda i: (0, 0, 0))


def _full(w_ref):
    return w_ref[...].reshape(w_ref.shape[0] * w_ref.shape[1], w_ref.shape[2])


def _token_out_fwd(o, parts, ya, x, mod, hnw, wa, wb, wo, ctx_rows, name):
    T, D = x.shape
    nh = D // HEAD
    cb = ctx_rows // TBT

    def body(o_ref, og_ref, ga_ref, gb_ref, ya_ref, x_ref, mod_ref, hnw_ref, wa_ref, wb_ref, wo_ref,
             yb_ref, pa_ref, pb_ref, mg_ref, tmo_ref, xm_ref):
        ov = o_ref[...]
        so = _silu(og_ref[...])
        nw = hnw_ref[...]
        for h in range(nh):
            sl = slice(h * HEAD, (h + 1) * HEAD)
            seg = ov[:, sl]
            r = lax.rsqrt(jnp.mean(seg * seg, axis=-1, keepdims=True) + RMS_EPS)
            yb_ref[:, sl] = (seg * r * nw * so[:, sl]).astype(BF16)
        pa = _nn(ya_ref[...], _full(wa_ref))
        pb = _nn(yb_ref[...], _full(wb_ref))
        pa_ref[...] = pa
        pb_ref[...] = pb
        mg = (_sigmoid(ga_ref[...]) * pa + _sigmoid(gb_ref[...]) * pb).astype(BF16)
        mg_ref[...] = mg
        out = _nn(mg, _full(wo_ref))
        tmo_ref[...] = out
        xm_ref[...] = x_ref[...] + mod_ref[2:3, :] * out

    row = lambda col: pl.BlockSpec((TBT, D), lambda i: (i, col))
    wsp = _rows_weight_spec(wa)
    sd = lambda dt: jax.ShapeDtypeStruct((T, D), dt)
    return pl.pallas_call(
        body, name=name, grid=(T // TBT,),
        in_specs=[row(0), row(6), row(7), row(8), row(0), row(0),
                  pl.BlockSpec((None, N_MOD, D), lambda i: (_stream_of(i, cb), 0, 0)),
                  pl.BlockSpec((1, HEAD), lambda i: (0, 0)), wsp, wsp, wsp],
        out_specs=[row(0)] * 6,
        out_shape=[sd(BF16), sd(F32), sd(F32), sd(BF16), sd(F32), sd(F32)],
        compiler_params=_params("parallel"),
    )(o, parts, parts, parts, ya, x, mod, hnw, wa, wb, wo)


def _token_out_bwd(dx, tmo, pa, pb, o, parts, mod, hnw, wa, wb, wo, ctx_rows, name):
    T, D = dx.shape
    nh = D // HEAD
    cb = ctx_rows // TBT

    def body(dx_ref, tmo_ref, pa_ref, pb_ref, o_ref, og_ref, ga_ref, gb_ref, mod_ref, hnw_ref, wa_ref, wb_ref, wo_ref,
             dout_ref, dpa_ref, dpb_ref, dog_ref, dga_ref, dgb_ref, dya_ref, do_ref, dg1_ref, dhnw_ref):
        i = pl.program_id(0)

        @pl.when(i == 0)
        def _():
            dhnw_ref[...] = jnp.zeros_like(dhnw_ref)

        @pl.when((i == 0) | (i == cb))
        def _():
            dg1_ref[...] = jnp.zeros_like(dg1_ref)

        dxv = dx_ref[...]
        dg1_ref[...] += jnp.sum(dxv * tmo_ref[...], axis=0, keepdims=True)
        dout = (dxv * mod_ref[2:3, :]).astype(BF16)
        dout_ref[...] = dout
        dmg = _nt(dout, _full(wo_ref))
        sa = _sigmoid(ga_ref[...])
        sb = _sigmoid(gb_ref[...])
        dpa = (dmg * sa).astype(BF16)
        dpb = (dmg * sb).astype(BF16)
        dpa_ref[...] = dpa
        dpb_ref[...] = dpb
        dga_ref[...] = (dmg * pa_ref[...] * sa * (1.0 - sa)).astype(BF16)
        dgb_ref[...] = (dmg * pb_ref[...] * sb * (1.0 - sb)).astype(BF16)
        dya_ref[...] = _nt(dpa, _full(wa_ref))
        dyb = _nt(dpb, _full(wb_ref))
        og = og_ref[...]
        so = _silu(og)
        dso = _dsilu(og)
        ov = o_ref[...]
        nw = hnw_ref[...]
        dnw = jnp.zeros((1, HEAD), F32)
        for h in range(nh):
            sl = slice(h * HEAD, (h + 1) * HEAD)
            seg = ov[:, sl]
            r = lax.rsqrt(jnp.mean(seg * seg, axis=-1, keepdims=True) + RMS_EPS)
            oh = seg * r
            dn = dyb[:, sl] * so[:, sl]
            dog_ref[:, sl] = (dyb[:, sl] * oh * nw * dso[:, sl]).astype(BF16)
            dnw = dnw + jnp.sum(dn * oh, axis=0, keepdims=True)
            doh = dn * nw
            do_ref[:, sl] = r * (doh - oh * jnp.mean(doh * oh, axis=-1, keepdims=True))
        dhnw_ref[...] += dnw

    row = lambda col: pl.BlockSpec((TBT, D), lambda i: (i, col))
    wsp = _rows_weight_spec(wa)
    sd = lambda dt: jax.ShapeDtypeStruct((T, D), dt)
    return pl.pallas_call(
        body, name=name, grid=(T // TBT,),
        in_specs=[row(0), row(0), row(0), row(0), row(0), row(6), row(7), row(8),
                  pl.BlockSpec((None, N_MOD, D), lambda i: (_stream_of(i, cb), 0, 0)),
                  pl.BlockSpec((1, HEAD), lambda i: (0, 0)), wsp, wsp, wsp],
        out_specs=[row(0)] * 8 + [pl.BlockSpec((None, 1, D), lambda i: (_stream_of(i, cb), 0, 0)),
                                  pl.BlockSpec((1, HEAD), lambda i: (0, 0))],
        out_shape=[sd(BF16)] * 6 + [sd(F32), sd(F32), jax.ShapeDtypeStruct((2, 1, D), F32),
                                    jax.ShapeDtypeStruct((1, HEAD), F32)],
        compiler_params=_params("arbitrary"),
    )(dx, tmo, pa, pb, o, parts, parts, parts, mod, hnw, wa, wb, wo)


def _conv_geometry(i, nb, cb):
    is_ctx = i < cb
    first = (i == 0) | (i == cb)
    last = (i == cb - 1) | (i == nb - 1)
    row = lax.broadcasted_iota(jnp.int32, (TB + 2 * GRID_W, 1), 0)
    w = row & (GRID_W - 1)
    left_ok = (w != 0) | is_ctx
    right_ok = (w != GRID_W - 1) | is_ctx
    return is_ctx, first, last, left_ok, right_ok


def _ext(p_ref, m_ref, n_ref, first, last):
    return jnp.concatenate([jnp.where(first, 0.0, p_ref[...]), m_ref[...], jnp.where(last, 0.0, n_ref[...])], axis=0)


def _shift_prev(e, ok):
    return jnp.where(ok, pltpu.roll(e, 1, 0), 0.0)


def _shift_next(e, ok):
    return jnp.where(ok, pltpu.roll(e, e.shape[0] - 1, 0), 0.0)


def _halo_specs(cbk, n64, coff=0):
    r = TB // GRID_W
    prev = pl.BlockSpec((GRID_W, cbk), lambda j, i: (jnp.maximum(r * i - 1, 0), j + coff))
    main = pl.BlockSpec((TB, cbk), lambda j, i: (i, j + coff))
    nxt = pl.BlockSpec((GRID_W, cbk), lambda j, i: (jnp.minimum(r * i + r, n64 - 1), j + coff))
    return [prev, main, nxt]


def _conv_cblock(dff):
    return _tile(dff, 1408)


def _conv_fwd(up, cw, cbias, ctx_rows, name):
    T, dff = up.shape[0], up.shape[1] // 2
    cbk = _conv_cblock(dff)
    nb, cb = T // TB, ctx_rows // TB
    nvb = dff // cbk

    def body(ap_ref, a_ref, an_ref, v_ref, cw_ref, cb_ref, ac_ref, act_ref):
        i = pl.program_id(1)
        is_ctx, first, last, lok, rok = _conv_geometry(i, nb, cb)
        e = _ext(ap_ref, a_ref, an_ref, first, last)
        el = _shift_prev(e, lok)
        er = _shift_next(e, rok)
        cwv = cw_ref[...]

        def comb(dr, lo):
            sl = slice(lo, lo + TB)
            return cwv[3 * dr:3 * dr + 1] * el[sl] + cwv[3 * dr + 1:3 * dr + 2] * e[sl] + cwv[3 * dr + 2:3 * dr + 3] * er[sl]

        out = comb(1, GRID_W) + jnp.where(is_ctx, 0.0, comb(0, 0) + comb(2, 2 * GRID_W))
        a_c = out + cb_ref[...]
        ac_ref[...] = a_c
        act_ref[...] = (_gelu(a_c) * v_ref[...]).astype(BF16)

    main = pl.BlockSpec((TB, cbk), lambda j, i: (i, j))
    return pl.pallas_call(
        body, name=name, grid=(dff // cbk, nb),
        in_specs=_halo_specs(cbk, T // GRID_W) + [pl.BlockSpec((TB, cbk), lambda j, i: (i, j + nvb)),
                                                 pl.BlockSpec((9, cbk), lambda j, i: (0, j)),
                                                 pl.BlockSpec((1, cbk), lambda j, i: (0, j))],
        out_specs=[main, main],
        out_shape=[jax.ShapeDtypeStruct((T, dff), F32), jax.ShapeDtypeStruct((T, dff), BF16)],
        compiler_params=_params("parallel", "parallel"),
    )(up, up, up, up, cw, cbias)


def _conv_bwd(up, ac, dact, cw, ctx_rows, name):
    T, dff = up.shape[0], up.shape[1] // 2
    cbk = _conv_cblock(dff)
    nb, cb = T // TB, ctx_rows // TB
    nvb = dff // cbk

    def body(ap_ref, a_ref, an_ref, vp_ref, v_ref, vn_ref, cp_ref, c_ref, cn_ref, dp_ref, d_ref, dn_ref, cw_ref,
             da_ref, dv_ref, dcw_ref, dcb_ref):
        i = pl.program_id(1)

        @pl.when(i == 0)
        def _():
            dcw_ref[...] = jnp.zeros_like(dcw_ref)
            dcb_ref[...] = jnp.zeros_like(dcb_ref)

        is_ctx, first, last, lok, rok = _conv_geometry(i, nb, cb)
        ace = _ext(cp_ref, c_ref, cn_ref, first, last)
        g = _ext(dp_ref, d_ref, dn_ref, first, last) * _ext(vp_ref, v_ref, vn_ref, first, last) * _dgelu(ace)
        dv_ref[...] = (d_ref[...] * _gelu(c_ref[...])).astype(BF16)
        gm = _shift_prev(g, lok)
        gp = _shift_next(g, rok)
        cwv = cw_ref[...]

        def comb(dr, lo):
            sl = slice(lo, lo + TB)
            return cwv[3 * dr:3 * dr + 1] * gp[sl] + cwv[3 * dr + 1:3 * dr + 2] * g[sl] + cwv[3 * dr + 2:3 * dr + 3] * gm[sl]

        da = comb(1, GRID_W) + jnp.where(is_ctx, 0.0, comb(0, 2 * GRID_W) + comb(2, 0))
        da_ref[...] = da.astype(BF16)
        e = _ext(ap_ref, a_ref, an_ref, first, last)
        taps = [_shift_prev(e, lok), e, _shift_next(e, rok)]
        gmain = g[GRID_W:GRID_W + TB]
        dcb_ref[...] += jnp.sum(gmain, axis=0, keepdims=True)
        vert = jnp.where(is_ctx, 0.0, 1.0)
        for dr in range(3):
            sl = slice(dr * GRID_W, dr * GRID_W + TB)
            for dw in range(3):
                s = jnp.sum(gmain * taps[dw][sl], axis=0, keepdims=True)
                if dr != 1:
                    s = s * vert
                k = 3 * dr + dw
                dcw_ref[k:k + 1, :] += s

    main = pl.BlockSpec((TB, cbk), lambda j, i: (i, j))
    halo = _halo_specs(cbk, T // GRID_W)
    acc9 = pl.BlockSpec((9, cbk), lambda j, i: (0, j))
    acc1 = pl.BlockSpec((1, cbk), lambda j, i: (0, j))
    return pl.pallas_call(
        body, name=name, grid=(dff // cbk, nb),
        in_specs=halo + _halo_specs(cbk, T // GRID_W, nvb) + halo + halo + [acc9],
        out_specs=[main, main, acc9, acc1],
        out_shape=[jax.ShapeDtypeStruct((T, dff), BF16), jax.ShapeDtypeStruct((T, dff), BF16),
                   jax.ShapeDtypeStruct((9, dff), F32), jax.ShapeDtypeStruct((1, dff), F32)],
        compiler_params=_params("parallel", "arbitrary"),
    )(up, up, up, up, up, up, ac, ac, ac, dact, dact, dact, cw)


def _ffn_out_fwd(act, xm, mod, wd, ctx_rows, name):
    T, D = xm.shape
    dff = act.shape[1]
    cb = ctx_rows // TB

    def body(act_ref, x_ref, mod_ref, w_ref, xo_ref, fo_ref):
        out = _nn(act_ref[...], _full(w_ref))
        fo_ref[...] = out
        xo_ref[...] = x_ref[...] + mod_ref[5:6, :] * out

    row = pl.BlockSpec((TB, D), lambda i: (i, 0))
    return pl.pallas_call(
        body, name=name, grid=(T // TB,),
        in_specs=[pl.BlockSpec((TB, dff), lambda i: (i, 0)), row,
                  pl.BlockSpec((None, N_MOD, D), lambda i: (_stream_of(i, cb), 0, 0)),
                  _rows_weight_spec(wd)],
        out_specs=[row, row],
        out_shape=[jax.ShapeDtypeStruct((T, D), F32), jax.ShapeDtypeStruct((T, D), F32)],
        compiler_params=_params("parallel"),
    )(act, xm, mod, wd)


def _ffn_out_bwd(dx, fo, mod, wd, ctx_rows, name):
    T, D = dx.shape
    dff = N_CHIPS * wd.shape[1]
    cb = ctx_rows // TB

    def body(dx_ref, fo_ref, mod_ref, w_ref, dout_ref, dact_ref, dg2_ref):
        i = pl.program_id(0)

        @pl.when((i == 0) | (i == cb))
        def _():
            dg2_ref[...] = jnp.zeros_like(dg2_ref)

        dxv = dx_ref[...]
        dg2_ref[...] += jnp.sum(dxv * fo_ref[...], axis=0, keepdims=True)
        dout = (dxv * mod_ref[5:6, :]).astype(BF16)
        dout_ref[...] = dout
        dact_ref[...] = _nt(dout, _full(w_ref))

    row = pl.BlockSpec((TB, D), lambda i: (i, 0))
    return pl.pallas_call(
        body, name=name, grid=(T // TB,),
        in_specs=[row, row, pl.BlockSpec((None, N_MOD, D), lambda i: (_stream_of(i, cb), 0, 0)),
                  _rows_weight_spec(wd)],
        out_specs=[row, pl.BlockSpec((TB, dff), lambda i: (i, 0)),
                   pl.BlockSpec((None, 1, D), lambda i: (_stream_of(i, cb), 0, 0))],
        out_shape=[jax.ShapeDtypeStruct((T, D), BF16), jax.ShapeDtypeStruct((T, dff), F32),
                   jax.ShapeDtypeStruct((2, 1, D), F32)],
        compiler_params=_params("arbitrary"),
    )(dx, fo, mod, wd)


def _loss_bwd(x, target, fw, ctx_rows, name):
    T, D = x.shape
    cb = ctx_rows // TB

    def body(x_ref, t_ref, fw_ref, dx_ref, loss_ref, dfw_ref):
        i = pl.program_id(0)

        @pl.when(i == 0)
        def _():
            loss_ref[...] = jnp.zeros_like(loss_ref)
            dfw_ref[...] = jnp.zeros_like(dfw_ref)

        @pl.when(i < cb)
        def _():
            dx_ref[...] = jnp.zeros_like(dx_ref)

        @pl.when(i >= cb)
        def _():
            xv = x_ref[...]
            r = lax.rsqrt(jnp.mean(xv * xv, axis=-1, keepdims=True) + RMS_EPS)
            xh = xv * r
            fwv = fw_ref[...]
            err = xh * fwv - t_ref[...]
            loss_ref[...] += (0.5 / D) * jnp.sum(err * err)
            dy = err * (1.0 / D)
            dfw_ref[...] += jnp.sum(dy * xh, axis=0, keepdims=True)
            dxh = dy * fwv
            dx_ref[...] = r * (dxh - xh * jnp.mean(dxh * xh, axis=-1, keepdims=True))

    row = pl.BlockSpec((TB, D), lambda i: (i, 0))
    return pl.pallas_call(
        body, name=name, grid=(T // TB,),
        in_specs=[row, pl.BlockSpec((TB, D), lambda i: (jnp.maximum(i - cb, 0), 0)), pl.BlockSpec((1, D), lambda i: (0, 0))],
        out_specs=[row, pl.BlockSpec((1, 128), lambda i: (0, 0)), pl.BlockSpec((1, D), lambda i: (0, 0))],
        out_shape=[jax.ShapeDtypeStruct((T, D), F32), jax.ShapeDtypeStruct((1, 128), F32),
                   jax.ShapeDtypeStruct((1, D), F32)],
        compiler_params=_params("arbitrary"),
    )(x, target, fw)


def _adamw(w, gs, m, v, name):
    L, R, C = w.shape
    assert len(gs) == L
    rb = _rows_tile(R, max(16, (1 << 18) // C // 16 * 16))
    bc1 = 1.0 - ADAM_B1 ** ADAM_STEP
    bc2 = 1.0 - ADAM_B2 ** ADAM_STEP

    def body(w_ref, m_ref, v_ref, *rest):
        g_refs, (g_ref, d_ref, nm_ref, nv_ref) = rest[:L], rest[L:]
        layer = pl.program_id(0)
        for li in range(L):
            @pl.when(layer == li)
            def _():
                gv = g_refs[li][...]
                g_ref[...] = gv
                nm = ADAM_B1 * m_ref[...] + (1.0 - ADAM_B1) * gv
                nv = ADAM_B2 * v_ref[...] + (1.0 - ADAM_B2) * (gv * gv)
                nm_ref[...] = nm
                nv_ref[...] = nv
                d_ref[...] = -ADAM_LR * ((nm / bc1) / (jnp.sqrt(nv / bc2) + ADAM_EPS) + ADAM_WD * w_ref[...])

    blk = pl.BlockSpec((None, rb, C), lambda l, i: (l, i, 0))
    gblk = pl.BlockSpec((rb, C), lambda l, i: (i, 0))
    sd = jax.ShapeDtypeStruct((L, R, C), F32)
    return pl.pallas_call(
        body, name=name, grid=(L, R // rb), in_specs=[blk] * 3 + [gblk] * L, out_specs=[blk] * 4, out_shape=[sd] * 4,
        compiler_params=_params("parallel", "parallel"),
    )(w, m, v, *gs)


def _local_step(xs, cv, target, W, layer_weights, on_layer_grads, ctx_rows):
    T, D = xs.shape
    depth = W["norm1_w"].shape[0]
    saved = []
    X = xs
    for l in range(depth):
        s = {}
        Wl = layer_weights(l, X)
        mod_all, sa = _mod_fwd(cv, Wl["ada_w"], W["ada_b"][l][None, :] + Wl["token"], f"mod_fwd_{l}")
        mod = mod_all[:2].reshape(2, N_MOD, D)
        h1 = _norm_mod(X, W["norm1_w"][l][None, :], mod, 0, ctx_rows, f"norm1_{l}")
        parts = _mm_nn_w(h1, Wl["w_in"], F32, f"in_proj_{l}")
        o_f, st_f = _hgrn_fwd(parts, W["hlb"], l, False, ctx_rows, f"hgrn_fwd_f_{l}")
        o, st_b = _hgrn_fwd(parts, W["hlb"], l, True, ctx_rows, f"hgrn_fwd_b_{l}", o_add=o_f)
        ya = _sgu_fwd(parts, W["sgu_ln_w"][l][None, :], W["sgu_ln_b"][l][None, :], W["sgu_w"][l], W["sgu_bt"][l],
                      f"sgu_fwd_{l}")
        Wl.update(Wl.pop("late")(ya))
        yb, pa, pb, mg, tmo, xm = _token_out_fwd(o, parts, ya, X, mod, W["hnw"][l][None, :] + Wl["late_token"], Wl["w_a"],
                                                 Wl["w_b"], Wl["w_o"], ctx_rows, f"token_out_fwd_{l}")
        h2 = _norm_mod(xm, W["norm2_w"][l][None, :], mod, 3, ctx_rows, f"norm2_{l}")
        up = _mm_nn_w(h2, Wl["w_up"], F32, f"up_proj_{l}")
        ac, act = _conv_fwd(up, Wl["conv_w"], W["conv_b"][l][None, :], ctx_rows, f"conv_fwd_{l}")
        xo, fo = _ffn_out_fwd(act, xm, mod, Wl["w_down"], ctx_rows, f"ffn_out_fwd_{l}")
        s.update(X=X, Wl=Wl, mod=mod, mod_all=mod_all, sa=sa, h1=h1, parts=parts, o=o, st_f=st_f, st_b=st_b, ya=ya, yb=yb,
                 pa=pa, pb=pb, mg=mg, tmo=tmo, xm=xm, h2=h2, up=up, ac=ac, act=act, fo=fo)
        saved.append(s)
        X = xo

    dX, loss_row, dfw = _loss_bwd(X, target, W["final_norm_w"][None, :], ctx_rows, "loss_bwd")
    G = {k: [None] * depth for k in ("ada_b", "norm1_w", "sgu_ln_w", "sgu_ln_b", "sgu_w", "sgu_b", "hlb1", "hnw", "norm2_w",
                                     "conv_w", "conv_b", "dmod")}
    dcv = jnp.zeros_like(cv)
    for l in reversed(range(depth)):
        s = saved[l]
        mod, Wl = s["mod"], s["Wl"]
        big = {}
        dout2, dact, dg2 = _ffn_out_bwd(dX, s["fo"], mod, Wl["w_down"], ctx_rows, f"ffn_out_bwd_{l}")
        big["w_down"] = _mm_tn(s["act"], dout2, F32, f"dw_down_{l}")
        da, dv, dcw, dcb = _conv_bwd(s["up"], s["ac"], dact, Wl["conv_w"], ctx_rows, f"conv_bwd_{l}")
        G["conv_w"][l], G["conv_b"][l] = dcw, dcb[0]
        dup = jnp.concatenate([da, dv], axis=1)
        big["w_up"] = _mm_tn(s["h2"], dup, F32, f"dw_up_{l}", out_chips=True)
        dh2 = _mm_nt_w(dup, Wl["w_up"], F32, f"dh2_{l}")
        dxm, dm2, dnw2 = _norm_mod_bwd(dh2, s["xm"], dX, W["norm2_w"][l][None, :], mod, 3, ctx_rows, f"norm2_bwd_{l}")
        G["norm2_w"][l] = dnw2[0]
        (dout1, dpa, dpb, dog, dga, dgb, dya, do, dg1, dhnw) = _token_out_bwd(
            dxm, s["tmo"], s["pa"], s["pb"], s["o"], s["parts"], mod, W["hnw"][l][None, :], Wl["w_a"], Wl["w_b"], Wl["w_o"],
            ctx_rows, f"token_out_bwd_{l}")
        G["hnw"][l] = dhnw[0]
        big["w_o"] = _mm_tn(s["mg"], dout1, F32, f"dw_o_{l}")
        big["w_a"] = _mm_tn(s["ya"], dpa, F32, f"dw_a_{l}")
        big["w_b"] = _mm_tn(s["yb"], dpb, F32, f"dw_b_{l}")
        tok = on_layer_grads(l, "early", big)
        du, dvs, dsw, dsbt, dlnw, dlnb = _sgu_bwd(s["parts"], dya, W["sgu_ln_w"][l][None, :], W["sgu_ln_b"][l][None, :] + tok,
                                                  W["sgu_w"][l], W["sgu_bt"][l], f"sgu_bwd_{l}")
        G["sgu_w"][l], G["sgu_b"][l], G["sgu_ln_w"][l], G["sgu_ln_b"][l] = dsw, dsbt.T, dlnw[0], dlnb[0]
        dq_f, dz_f, di_f, dlb_f = _hgrn_bwd(s["parts"], W["hlb"], do, s["st_f"], l, False, ctx_rows, f"hgrn_bwd_f_{l}")
        dq, dz_b, di, dlb_b = _hgrn_bwd(s["parts"], W["hlb"], do, s["st_b"], l, True, ctx_rows, f"hgrn_bwd_b_{l}",
                                        dq_add=dq_f, di_add=di_f)
        G["hlb1"][l] = jnp.concatenate([dlb_f[0], dlb_b[0]])
        dparts = jnp.concatenate([dq.astype(BF16), dz_f, dz_b, di.astype(BF16), du, dvs, dog, dga, dgb], axis=1)
        tok = on_layer_grads(l, "late", {"w_in": _mm_tn(s["h1"], dparts, F32, f"dw_in_{l}", out_chips=True)})
        dh1 = _mm_nt_w(dparts, Wl["w_in"], F32, f"dh1_{l}")
        dX, dm1, dnw1 = _norm_mod_bwd(dh1, s["X"], dxm, W["norm1_w"][l][None, :] + tok, mod, 0, ctx_rows, f"norm1_bwd_{l}")
        G["norm1_w"][l] = dnw1[0]
        dmod = jnp.concatenate([dm1, dg1, dm2, dg2], axis=1).reshape(2, N_MOD * D)
        dmod16 = jnp.concatenate([dmod, jnp.zeros((cv.shape[0] - 2, N_MOD * D), F32)], axis=0)
        G["ada_b"][l] = dmod[0] + dmod[1]
        G["dmod"][l] = dmod
        dcv = dcv + _cvec_bwd(dmod16, Wl["ada_w"], cv, f"dcvec_{l}")
    G["c_ctx"] = dcv[0]
    G["final_norm_w"] = dfw[0]
    return loss_row[0, 0], dX, G, saved[0]["sa"]


def _chip_peers(x, y, c):
    return [((1 - x, y, c), 2 * (1 - x) + y), ((x, 1 - y, c), 2 * x + 1 - y), ((1 - x, 1 - y, c), 2 * (1 - x) + 1 - y)]


def _rdma_call(ins, out_shapes, plan, n_remote, n_local, name, aliases=None):
    n_in, n_out = len(ins), len(out_shapes)

    def body(*refs):
        in_refs, out_refs = refs[:n_in], refs[n_in:n_in + n_out]
        send_sems, recv_sems, local_sems = refs[n_in + n_out:]
        x, y, c = lax.axis_index("x"), lax.axis_index("y"), lax.axis_index("c")
        remote, local = plan(in_refs, out_refs, x, y, c)
        assert len(remote) == n_remote and len(local) == n_local, (name, len(remote), len(local))
        copies = [pltpu.make_async_copy(s, d, local_sems.at[i]) for i, (s, d) in enumerate(local)]
        copies += [pltpu.make_async_remote_copy(src_ref=s, dst_ref=d, send_sem=send_sems.at[k], recv_sem=recv_sems.at[k],
                                                device_id=dev, device_id_type=pl.DeviceIdType.MESH)
                   for k, (s, d, dev) in enumerate(remote)]
        for cp in copies:
            cp.start()
        for cp in copies:
            cp.wait()

    hbm = pl.BlockSpec(memory_space=pltpu.HBM)
    return pl.pallas_call(
        body, name=name, in_specs=[hbm] * n_in, out_specs=[hbm] * n_out, out_shape=out_shapes,
        scratch_shapes=[pltpu.SemaphoreType.DMA((n_remote,)), pltpu.SemaphoreType.DMA((n_remote,)),
                        pltpu.SemaphoreType.DMA((max(n_local, 1),))],
        input_output_aliases=aliases or {},
    )(*ins)


def _gather_weights(shards, name):
    n = len(shards)
    n_far = (N_CHIPS - 1) * n

    def body(*refs):
        ins, outs = refs[:n], refs[n:2 * n]
        far_send, far_recv, near_send, near_recv, local_sems = refs[2 * n:]
        x, y, c = lax.axis_index("x"), lax.axis_index("y"), lax.axis_index("c")
        me = 2 * x + y
        half = lambda ref: pl.ds(c * (ref.shape[0] // 2), ref.shape[0] // 2)
        own = [pltpu.make_async_copy(s, o.at[me], local_sems.at[t]) for t, (s, o) in enumerate(zip(ins, outs))]
        for cp in own:
            cp.start()
        far, near = [], []
        for p, (dev, idx) in enumerate(_chip_peers(x, y, c)):
            for t, (s, o) in enumerate(zip(ins, outs)):
                k = p * n + t
                far.append(pltpu.make_async_remote_copy(
                    src_ref=s.at[half(s)], dst_ref=o.at[me, half(s)], send_sem=far_send.at[k], recv_sem=far_recv.at[k],
                    device_id=dev, device_id_type=pl.DeviceIdType.MESH))
                near.append(pltpu.make_async_remote_copy(
                    src_ref=o.at[idx, half(s)], dst_ref=o.at[idx, half(s)], send_sem=near_send.at[k],
                    recv_sem=near_recv.at[k], device_id=(x, y, 1 - c), device_id_type=pl.DeviceIdType.MESH))
        for cp in far:
            cp.start()
        for k in range(n_far):
            far[k].wait_recv()
            near[k].start()
        for k in range(n_far):
            near[k].wait_recv()
        for cp in far + near:
            cp.wait_send()
        for cp in own:
            cp.wait()

    hbm = pl.BlockSpec(memory_space=pltpu.HBM)
    sems = pltpu.SemaphoreType.DMA((n_far,))
    return pl.pallas_call(
        body, name=name, in_specs=[hbm] * n, out_specs=[hbm] * n,
        out_shape=[jax.ShapeDtypeStruct((N_CHIPS,) + s.shape, s.dtype) for s in shards],
        scratch_shapes=[sems, sems, sems, sems, pltpu.SemaphoreType.DMA((n,))],
    )(*shards)


def _gather_all(v, name):
    def plan(ins, outs, x, y, c):
        (s,), (o,) = ins, outs
        me = 4 * x + 2 * y + c
        flip = lambda a, f: 1 - a if f else a
        remote = [(s, o.at[me], (flip(x, m & 4), flip(y, m & 2), flip(c, m & 1))) for m in range(1, 8)]
        return remote, [(s, o.at[me])]

    return _rdma_call([v], [jax.ShapeDtypeStruct((8,) + v.shape, v.dtype)], plan, 7, 1, name)[0]


def _reduce_pair(parts, name):
    def plan(ins, outs, x, y, c):
        return [(a.at[j, 1 - c], o.at[j], (x, y, 1 - c)) for a, o in zip(ins, outs) for j in range(N_CHIPS)], []

    shapes = [jax.ShapeDtypeStruct((N_CHIPS,) + a.shape[2:], a.dtype) for a in parts]
    return _rdma_call(parts, shapes, plan, N_CHIPS * len(parts), 0, name)


def _reduce_chips(parts, name):
    def plan(ins, outs, x, y, c):
        me = 2 * x + y
        return [(a.at[idx], o.at[me], dev) for dev, idx in _chip_peers(x, y, c) for a, o in zip(ins, outs)], []

    shapes = [jax.ShapeDtypeStruct(a.shape, a.dtype) for a in parts]
    return _rdma_call(parts, shapes, plan, (N_CHIPS - 1) * len(parts), 0, name)


def _gather_pair(halves, name):
    def plan(ins, outs, x, y, c):
        return [(o.at[c], o.at[c], (x, y, 1 - c)) for o in outs], []

    shapes = [jax.ShapeDtypeStruct(a.shape, a.dtype) for a in halves]
    return _rdma_call(halves, shapes, plan, len(halves), 0, name, aliases={i: i for i in range(len(halves))})


def _split_start(ins, lands, plan, n_remote, name):
    n_buf = len(ins) + len(lands)

    def body(*refs):
        in_refs, land_refs = refs[:len(ins)], refs[len(ins):n_buf]
        send_sems, recv_sems, token = refs[n_buf], refs[n_buf + 1], refs[-1]
        x, y, c = lax.axis_index("x"), lax.axis_index("y"), lax.axis_index("c")
        remote = plan(in_refs, land_refs, x, y, c)
        assert len(remote) == n_remote, (name, len(remote))
        for k, (s, d, dev) in enumerate(remote):
            pltpu.make_async_remote_copy(src_ref=s, dst_ref=d, send_sem=send_sems.at[k], recv_sem=recv_sems.at[k],
                                         device_id=dev, device_id_type=pl.DeviceIdType.MESH).start()
        token[...] = jnp.zeros_like(token)

    hbm = pl.BlockSpec(memory_space=pltpu.HBM)
    sem = pl.BlockSpec(memory_space=pltpu.SEMAPHORE)
    bufs = list(ins) + list(lands)
    out = pl.pallas_call(
        body, name=name, in_specs=[hbm] * n_buf,
        out_specs=(sem, sem) + (hbm,) * n_buf + (pl.BlockSpec(memory_space=pltpu.VMEM),),
        out_shape=(pltpu.SemaphoreType.DMA((n_remote,)), pltpu.SemaphoreType.DMA((n_remote,)))
        + tuple(pltpu.HBM(a.shape, a.dtype) for a in bufs) + (jax.ShapeDtypeStruct((8, 128), F32),),
        input_output_aliases={i: 2 + i for i in range(n_buf)},
        compiler_params=pltpu.CompilerParams(has_side_effects=pltpu.SideEffectType.DATAFLOW_SIDE_EFFECTING),
    )(*[pltpu.with_memory_space_constraint(a, pltpu.HBM) for a in bufs])
    return dict(send=out[0], recv=out[1], ins=list(out[2:2 + len(ins)]), lands=list(out[2 + len(ins):2 + n_buf]),
                token=out[-1][0, 0], plan=plan, n_remote=n_remote)


def _split_wait(st, after, name):
    n_in, n_buf = len(st["ins"]), len(st["ins"]) + len(st["lands"])
    plan, n_remote = st["plan"], st["n_remote"]

    def body(*refs):
        in_refs, land_refs = refs[:n_in], refs[n_in:n_buf]
        send_sems, recv_sems = refs[n_buf], refs[n_buf + 1]
        x, y, c = lax.axis_index("x"), lax.axis_index("y"), lax.axis_index("c")
        for k, (s, d, dev) in enumerate(plan(in_refs, land_refs, x, y, c)):
            cp = pltpu.make_async_remote_copy(src_ref=s, dst_ref=d, send_sem=send_sems.at[k], recv_sem=recv_sems.at[k],
                                              device_id=dev, device_id_type=pl.DeviceIdType.MESH)
            cp.wait_send()
            cp.wait_recv()

    hbm = pl.BlockSpec(memory_space=pltpu.HBM)
    sem = pl.BlockSpec(memory_space=pltpu.SEMAPHORE)
    bufs = st["ins"] + st["lands"]
    out = pl.pallas_call(
        body, name=name, in_specs=[hbm] * n_buf + [sem, sem, pl.BlockSpec(memory_space=pl.ANY)],
        out_specs=[hbm] * n_buf, out_shape=[pltpu.HBM(a.shape, a.dtype) for a in bufs],
        input_output_aliases={i: i for i in range(n_buf)},
        compiler_params=pltpu.CompilerParams(has_side_effects=pltpu.SideEffectType.DATAFLOW_SIDE_EFFECTING),
    )(*bufs, st["send"], st["recv"], after)
    return list(out[:n_in]), list(out[n_in:])


def _plan_chips(ins, lands, x, y, c):
    me = 2 * x + y
    return [(a.at[idx], o.at[me], dev) for dev, idx in _chip_peers(x, y, c) for a, o in zip(ins, lands)]


def _plan_gather_half(ins, lands, x, y, c):
    me = 2 * x + y
    half = lambda ref: pl.ds(c * (ref.shape[0] // 2), ref.shape[0] // 2)
    return [(s.at[half(s)], o.at[me, half(s)], dev) for dev, _ in _chip_peers(x, y, c) for s, o in zip(ins, lands)]


def _pair_forward(shards, lands, name):
    n = len(shards)

    def plan(ins, outs, x, y, c):
        half = lambda ref: pl.ds(c * (ref.shape[0] // 2), ref.shape[0] // 2)
        remote = [(o.at[idx, half(s)], o.at[idx, half(s)], (x, y, 1 - c))
                  for _, idx in _chip_peers(x, y, c) for s, o in zip(ins[:n], outs)]
        return remote, [(s, o.at[2 * x + y]) for s, o in zip(ins[:n], outs)]

    shapes = [jax.ShapeDtypeStruct(a.shape, a.dtype) for a in lands]
    return _rdma_call(list(shards) + list(lands), shapes, plan, (N_CHIPS - 1) * n, n, name,
                      aliases={n + i: i for i in range(n)})


def _sum_block_rows(r, C):
    return _rows_tile(r, max(16, (1 << 18) // C // 16 * 16))


def _sum_pair(a, recv, cidx, name):
    nch, _, r, C = a.shape
    rb = _sum_block_rows(r, C)

    def body(c_ref, a_ref, r_ref, o_ref):
        o_ref[...] = (a_ref[...] + r_ref[...]).astype(BF16)

    blk = pl.BlockSpec((None, rb, C), lambda j, i, c: (j, i, 0))
    return pl.pallas_call(
        body, name=name,
        grid_spec=pltpu.PrefetchScalarGridSpec(
            num_scalar_prefetch=1, grid=(nch, r // rb),
            in_specs=[pl.BlockSpec((None, None, rb, C), lambda j, i, c: (j, c[0], i, 0)), blk], out_specs=blk),
        out_shape=jax.ShapeDtypeStruct((nch, r, C), BF16),
        compiler_params=_params("parallel", "parallel"),
    )(cidx, a, recv)


def _sum_chips(mine, recv, ids, name):
    nch, r, C = recv.shape
    rb = _sum_block_rows(r, C)

    def body(ids_ref, m_ref, *rest):
        r_refs, o_ref = rest[:nch], rest[nch]
        chip = ids_ref[1]
        own = m_ref[...].astype(F32)
        acc = jnp.where(chip == 0, own, r_refs[0][...].astype(F32))
        for q in range(1, nch):
            acc = acc + jnp.where(chip == q, own, r_refs[q][...].astype(F32))
        o_ref[...] = acc

    def slot(q):
        return pl.BlockSpec((None, rb, C), lambda i, ids: (jnp.where(ids[1] == q, (q + 1) % nch, q), i, 0))

    return pl.pallas_call(
        body, name=name,
        grid_spec=pltpu.PrefetchScalarGridSpec(
            num_scalar_prefetch=1, grid=(r // rb,),
            in_specs=[pl.BlockSpec((None, rb, C), lambda i, ids: (ids[1], i, 0))] + [slot(q) for q in range(nch)],
            out_specs=pl.BlockSpec((None, rb, C), lambda i, ids: (ids[0], i, 0))),
        out_shape=jax.ShapeDtypeStruct((N_CORES, r, C), F32),
        compiler_params=_params("parallel"),
    )(ids, mine, *([recv] * nch))


PACK_COLS = 1024
_SHARDED = ("ada_w", "w_in", "w_branch_a", "w_branch_b", "w_out", "ffn_w_up", "ffn_w_down")
_LAYER_KEYS = ("ada_w", "w_in", "w_a", "w_b", "w_o", "w_up", "w_down")
_SMALL = ("c_ctx", "ada_b", "norm1_w", "sgu_ln_w", "sgu_ln_b", "sgu_w", "sgu_b", "hgrn_lower_bounds", "hgrn_norm_w",
          "norm2_w", "ffn_conv_b", "final_norm_w")
_ORDER = ("c_ctx", "ada_w", "ada_b", "norm1_w", "w_in", "sgu_ln_w", "sgu_ln_b", "sgu_w", "sgu_b", "hgrn_lower_bounds",
          "hgrn_norm_w", "w_branch_a", "w_branch_b", "w_out", "norm2_w", "ffn_w_up", "ffn_conv_w", "ffn_conv_b",
          "ffn_w_down", "final_norm_w")


def _pad_to(v, n):
    return jnp.concatenate([v, jnp.zeros((n - v.shape[0],), v.dtype)]) if v.shape[0] < n else v


def _round_up(n, m):
    return (n + m - 1) // m * m


def _pack(arrays, n_pad):
    flat = jnp.concatenate([a.reshape(-1) for a in arrays])
    return _pad_to(flat, n_pad)


def _unpack(flat, like):
    out, off = [], 0
    for a in like:
        out.append(flat[off:off + a.size].reshape(a.shape))
        off += a.size
    return out


def kernel(x, c, ctx, c_ctx, ada_w, ada_b, norm1_w, w_in, sgu_ln_w, sgu_ln_b, sgu_w, sgu_b, hgrn_lower_bounds, hgrn_norm_w, w_branch_a, w_branch_b, w_out, norm2_w, ffn_w_up, ffn_conv_w, ffn_conv_b, ffn_w_down, final_norm_w, loss_target, m_c_ctx, m_ada_w, m_ada_b, m_norm1_w, m_w_in, m_sgu_ln_w, m_sgu_ln_b, m_sgu_w, m_sgu_b, m_hgrn_lower_bounds, m_hgrn_norm_w, m_w_branch_a, m_w_branch_b, m_w_out, m_norm2_w, m_ffn_w_up, m_ffn_conv_w, m_ffn_conv_b, m_ffn_w_down, m_final_norm_w, v_c_ctx, v_ada_w, v_ada_b, v_norm1_w, v_w_in, v_sgu_ln_w, v_sgu_ln_b, v_sgu_w, v_sgu_b, v_hgrn_lower_bounds, v_hgrn_norm_w, v_w_branch_a, v_w_branch_b, v_w_out, v_norm2_w, v_ffn_w_up, v_ffn_conv_w, v_ffn_conv_b, v_ffn_w_down, v_final_norm_w):
    w = dict(c_ctx=c_ctx, ada_w=ada_w, ada_b=ada_b, norm1_w=norm1_w, w_in=w_in, sgu_ln_w=sgu_ln_w, sgu_ln_b=sgu_ln_b,
             sgu_w=sgu_w, sgu_b=sgu_b, hgrn_lower_bounds=hgrn_lower_bounds, hgrn_norm_w=hgrn_norm_w, w_branch_a=w_branch_a,
             w_branch_b=w_branch_b, w_out=w_out, norm2_w=norm2_w, ffn_w_up=ffn_w_up, ffn_conv_w=ffn_conv_w,
             ffn_conv_b=ffn_conv_b, ffn_w_down=ffn_w_down, final_norm_w=final_norm_w)
    mom = dict(zip(_ORDER, (m_c_ctx, m_ada_w, m_ada_b, m_norm1_w, m_w_in, m_sgu_ln_w, m_sgu_ln_b, m_sgu_w, m_sgu_b,
                            m_hgrn_lower_bounds, m_hgrn_norm_w, m_w_branch_a, m_w_branch_b, m_w_out, m_norm2_w, m_ffn_w_up,
                            m_ffn_conv_w, m_ffn_conv_b, m_ffn_w_down, m_final_norm_w)))
    var = dict(zip(_ORDER, (v_c_ctx, v_ada_w, v_ada_b, v_norm1_w, v_w_in, v_sgu_ln_w, v_sgu_ln_b, v_sgu_w, v_sgu_b,
                            v_hgrn_lower_bounds, v_hgrn_norm_w, v_w_branch_a, v_w_branch_b, v_w_out, v_norm2_w, v_ffn_w_up,
                            v_ffn_conv_w, v_ffn_conv_b, v_ffn_w_down, v_final_norm_w)))
    depth, D = norm1_w.shape
    dff = ffn_conv_b.shape[1]
    ctx_rows, seq = ctx.shape[1], x.shape[1]

    assert depth == 2, "the lower-bound softmax is written for two layers"
    core = lax.axis_index("c")
    chip = 2 * lax.axis_index("x") + lax.axis_index("y")
    ids = jnp.stack([core, chip]).astype(jnp.int32)

    first, rest = _LAYER_KEYS[:2], _LAYER_KEYS[2:]
    shard = lambda l, k: w[_SHARDED[_LAYER_KEYS.index(k)]][l].astype(BF16)
    started, conv_full = {}, []

    def start_gather(l, keys, tag):
        shards = [shard(l, k) for k in keys]
        lands = [lax.empty((N_CHIPS,) + s.shape, s.dtype) for s in shards]
        started[tag] = _split_start(shards, lands, _plan_gather_half, (N_CHIPS - 1) * len(keys), f"gather_start_{tag}")
        return started[tag]["token"]

    def finish_gather(keys, tag, after):
        shards, lands = _split_wait(started[tag], after, f"gather_wait_{tag}")
        return dict(zip(keys, _pair_forward(shards, lands, f"gather_forward_{tag}")))

    def layer_weights(l, after):
        if l == 0:
            got = _gather_weights([shard(0, k) for k in first] + [ffn_conv_w], "gather_weights_first")
            conv_full.append(jnp.transpose(got[-1], (1, 2, 3, 0, 4)).reshape(depth, 9, dff))
            out = dict(zip(first, got), token=start_gather(0, rest, "rest_0"))
        else:
            out = dict(finish_gather(first, f"first_{l}", after), token=0.0)

        def late(after_late):
            more = finish_gather(rest, f"rest_{l}", after_late)
            more["late_token"] = 0.0
            if l + 1 < depth:
                more["late_token"] = start_gather(l + 1, first, f"first_{l + 1}") + start_gather(l + 1, rest, f"rest_{l + 1}")
            return more

        return dict(out, conv_w=conv_full[0][l], late=late)

    pending, pair_sums_of = [], {}

    def pair_reduce(tag, gs):
        parts = [g.reshape(N_CHIPS, N_CORES, g.size // (N_CHIPS * N_CORES * g.shape[-1]), g.shape[-1]) for g in gs]
        other = _reduce_pair(parts, f"reduce_pair_{tag}")
        return [_sum_pair(a, o, ids, f"sum_pair_{tag}_{i}") for i, (a, o) in enumerate(zip(parts, other))]

    def on_layer_grads(l, stage, gs):
        keys = [k for k in gs if k != "w_in"] if stage == "early" else ["w_in"]
        if l == 0 and stage == "late":
            pair_sums_of["last"] = (keys, gs)
            return 0.0
        tag = f"{stage}_{l}"
        sums = pair_reduce(tag, [gs[k] for k in keys])
        lands = [lax.empty(s.shape, s.dtype) for s in sums]
        st = _split_start(sums, lands, _plan_chips, (N_CHIPS - 1) * len(sums), f"reduce_chips_start_{tag}")
        pending.append((tag, l, keys, st))
        return st["token"]

    W = dict(ada_b=ada_b, norm1_w=norm1_w, sgu_ln_w=sgu_ln_w, sgu_ln_b=sgu_ln_b, sgu_w=sgu_w.astype(BF16),
             sgu_bt=jnp.swapaxes(sgu_b, 1, 2), hlb=hgrn_lower_bounds, hnw=hgrn_norm_w, norm2_w=norm2_w, conv_b=ffn_conv_b,
             final_norm_w=final_norm_w)
    xs = jnp.concatenate([ctx[0], x[0]], axis=0)
    cv = jnp.concatenate([c_ctx[None, :], c, jnp.zeros((14, D), F32)], axis=0)
    loss_local, dxs, G, sa = _local_step(xs, cv, loss_target[0], W, layer_weights, on_layer_grads, ctx_rows)
    loss = lax.psum(loss_local, ("x", "y", "c"))
    grad_x = dxs[ctx_rows:][None]

    pad8 = lambda a: jnp.pad(a, ((0, 8 - a.shape[0]), (0, 0)))
    fact = jnp.concatenate([pad8(sa[1:2].astype(F32))] + [pad8(G["dmod"][l][1].reshape(N_MOD, D)) for l in range(depth)]
                           + [pad8(G["dmod"][l][0].reshape(N_MOD, D)) for l in range(depth)], axis=0)
    facts = _gather_all(fact, "gather_mod_factors")
    lhs = jnp.concatenate([facts[:, 0].astype(BF16), jnp.broadcast_to(sa[0:1], (8, D))], axis=0)
    ada_cols = N_MOD * D // N_CHIPS
    g_ada = []
    for l in range(depth):
        lo_x, lo_c = 8 * (1 + l), 8 * (1 + depth + l)
        rhs = jnp.concatenate([facts[:, lo_x:lo_x + N_MOD].reshape(8, N_MOD * D),
                               facts[:, lo_c:lo_c + N_MOD].reshape(8, N_MOD * D)], axis=0)
        rhs = lax.dynamic_slice_in_dim(rhs, chip * ada_cols, ada_cols, axis=1).astype(BF16)
        g_ada.append(_mm_tn(lhs, rhs, F32, f"dw_ada_{l}"))

    dh = G["hlb1"][depth - 1]
    small_like = [w[k] for k in _SMALL] + [jnp.zeros((depth, 9, dff), F32)]
    small = [G["c_ctx"], jnp.stack(G["ada_b"]), jnp.stack(G["norm1_w"]), jnp.stack(G["sgu_ln_w"]), jnp.stack(G["sgu_ln_b"]),
             jnp.stack(G["sgu_w"]), jnp.stack(G["sgu_b"]), jnp.stack([-dh, dh]), jnp.stack(G["hnw"]), jnp.stack(G["norm2_w"]),
             jnp.stack(G["conv_b"]), G["final_norm_w"], jnp.stack(G["conv_w"])]
    n_small = sum(a.size for a in small)
    n_small_pad = _round_up(n_small, N_CORES * 16 * PACK_COLS)
    small_rows = n_small_pad // (N_CORES * PACK_COLS)
    small_rep = jnp.broadcast_to(_pack(small, n_small_pad).reshape(1, N_CORES, small_rows, PACK_COLS),
                                 (N_CHIPS, N_CORES, small_rows, PACK_COLS))
    last_keys, last_gs = pair_sums_of["last"]
    last_sums = pair_reduce("last", [last_gs[k] for k in last_keys] + [small_rep])
    last_recv = _reduce_chips(last_sums, "reduce_chips_last")

    halves, where = [], {}
    for tag, l, keys, st in pending:
        sums, recv = _split_wait(st, dxs, f"reduce_chips_wait_{tag}")
        for i, k in enumerate(keys):
            where[(l, k)] = len(halves)
            halves.append(_sum_chips(sums[i], recv[i], ids, f"sum_chips_{tag}_{i}"))
    for i, k in enumerate(list(last_keys) + ["small"]):
        where[(0, k)] = len(halves)
        halves.append(_sum_chips(last_sums[i], last_recv[i], ids, f"sum_chips_last_{i}"))
    reduced = _gather_pair(halves, "gather_pair")

    g_small = _unpack(reduced[where[(0, "small")]].reshape(-1), small_like)
    grads = dict(zip(_SMALL, g_small[:-1]))
    g_conv = lax.dynamic_slice_in_dim(g_small[-1].reshape(depth, 3, 3, dff), chip * (dff // N_CHIPS), dff // N_CHIPS, axis=3)

    delta, new_m, new_v = {}, {}, {}
    for i, k in enumerate(_SHARDED):
        shp = w[k].shape
        gs = g_ada if i == 0 else [reduced[where[(l, _LAYER_KEYS[i])]].reshape(shp[1:]) for l in range(depth)]
        grads[k], delta[k], new_m[k], new_v[k] = _adamw(w[k], gs, mom[k], var[k], f"adamw_{k}")
    packed = _SMALL + ("ffn_conv_w",)
    n_pad = _round_up(sum(w[k].size for k in packed), 16 * PACK_COLS)
    pack = lambda t: _pack([t[k] for k in packed], n_pad).reshape(1, -1, PACK_COLS)
    grads["ffn_conv_w"] = g_conv
    _, d, nm, nv = _adamw(pack(w), [pack(grads)[0]], pack(mom), pack(var), "adamw_packed")
    like = [w[k] for k in packed]
    for src, dst in ((d, delta), (nm, new_m), (nv, new_v)):
        dst.update(zip(packed, _unpack(src.reshape(-1), like)))

    return (loss, grad_x, *[grads[k] for k in _ORDER], *[delta[k] for k in _ORDER], *[new_m[k] for k in _ORDER],
            *[new_v[k] for k in _ORDER])
```

```python
import functools

import jax
import jax.numpy as jnp
from jax import lax
from jax.experimental import pallas as pl
from jax.experimental.pallas import tpu as pltpu

F32 = jnp.float32
BF16 = jnp.bfloat16

GRID_W = 64
HG_CHUNK = 64
SGU_CHUNK = 128
HEAD = 128
TB = 256
N_MOD = 6
RMS_EPS = 1e-6
LN_EPS = 1e-5
VMEM_LIMIT = 48 * 1024 * 1024
N_CHIPS = 4
N_CORES = 2

ADAM_LR = 0.001
ADAM_B1 = 0.9
ADAM_B2 = 0.999
ADAM_EPS = 1e-08
ADAM_WD = 0.01
ADAM_STEP = 10

_GELU_C = 0.7978845608028654
_GELU_A = 0.044715


def _sigmoid(x):
    return 1.0 / (1.0 + jnp.exp(-x))


def _silu(x):
    return x * _sigmoid(x)


def _dsilu(x):
    s = _sigmoid(x)
    return s * (1.0 + x * (1.0 - s))


def _gelu(x):
    return 0.5 * x * (1.0 + jnp.tanh(_GELU_C * (x + _GELU_A * x * x * x)))


def _dgelu(x):
    t = jnp.tanh(_GELU_C * (x + _GELU_A * x * x * x))
    return 0.5 * (1.0 + t) + 0.5 * x * (1.0 - t * t) * _GELU_C * (1.0 + 3.0 * _GELU_A * x * x)


def _dot(a, b, ca, cb):
    return lax.dot_general(a, b, (((ca,), (cb,)), ((), ())), preferred_element_type=F32)


def _nn(a, b):
    return _dot(a, b, 1, 0)


def _nt(a, b):
    return _dot(a, b, 1, 1)


def _tn(a, b):
    return _dot(a, b, 0, 0)


def _params(*sem):
    return pltpu.CompilerParams(dimension_semantics=sem if sem else None, vmem_limit_bytes=VMEM_LIMIT)


def _stream_of(i, ctx_blocks):
    return (i >= ctx_blocks).astype(jnp.int32)


def _mm(a, b, mode, tm, tn, tk, out_dtype, name, add=None, b_chips=False, out_chips=False):
    if not b_chips:
        bshape = b.shape
    else:
        bshape = (b.shape[1], N_CHIPS * b.shape[2])
    if mode == "nn":
        (M, K), (K2, N) = a.shape, bshape
    elif mode == "nt":
        (M, K), (N, K2) = a.shape, bshape
    else:
        (K, M), (K2, N) = a.shape, bshape
    assert K == K2 and M % tm == 0 and N % tn == 0 and K % tk == 0, (name, a.shape, b.shape, tm, tn, tk)
    nk = K // tk
    if mode == "tn":
        a_spec = pl.BlockSpec((tk, tm), lambda j, i, k: (k, i))
    else:
        a_spec = pl.BlockSpec((tm, tk), lambda j, i, k: (i, k))
    if not b_chips:
        if mode == "nt":
            b_spec = pl.BlockSpec((tn, tk), lambda j, i, k: (j, k))
        else:
            b_spec = pl.BlockSpec((tk, tn), lambda j, i, k: (k, j))
    else:
        cols = b.shape[2]
        if mode == "nn":
            per = cols // tn
            assert cols % tn == 0
            b_spec = pl.BlockSpec((None, tk, tn), lambda j, i, k: (j // per, k, j % per))
        else:
            per = cols // tk
            assert mode == "nt" and cols % tk == 0
            b_spec = pl.BlockSpec((None, tn, tk), lambda j, i, k: (k // per, j, k % per))
    if out_chips:
        per_o = (N // N_CHIPS) // tn
        assert (N // N_CHIPS) % tn == 0 and add is None
        o_spec = pl.BlockSpec((None, tm, tn), lambda j, i, k: (j // per_o, i, j % per_o))
        o_shape = (N_CHIPS, M, N // N_CHIPS)
    else:
        o_spec = pl.BlockSpec((tm, tn), lambda j, i, k: (i, j))
        o_shape = (M, N)
    ca, cb = {"nn": (1, 0), "nt": (1, 1), "tn": (0, 0)}[mode]

    def body(a_ref, b_ref, *rest):
        if add is None:
            o_ref, acc = rest
        else:
            add_ref, o_ref, acc = rest
        k = pl.program_id(2)

        @pl.when(k == 0)
        def _():
            acc[...] = jnp.zeros_like(acc)

        acc[...] += _dot(a_ref[...], b_ref[...], ca, cb)

        @pl.when(k == nk - 1)
        def _():
            r = acc[...]
            if add is not None:
                r = r + add_ref[...]
            o_ref[...] = r.astype(out_dtype)

    ins = [a, b] + ([] if add is None else [add])
    specs = [a_spec, b_spec] + ([] if add is None else [o_spec])
    return pl.pallas_call(
        body, name=name, grid=(N // tn, M // tm, nk), in_specs=specs, out_specs=o_spec,
        out_shape=jax.ShapeDtypeStruct(o_shape, out_dtype),
        scratch_shapes=[pltpu.VMEM((tm, tn), F32)],
        compiler_params=_params("parallel", "parallel", "arbitrary"),
    )(*ins)


def _tile(n, pref):
    if n <= pref:
        return n
    best = None
    for t in range(128, pref + 1, 128):
        if n % t == 0:
            best = t
    assert best is not None, (n, pref)
    return best


def _rows_tile(n, pref):
    if n <= pref:
        return n
    best = None
    for t in range(16, pref + 1, 16):
        if n % t == 0:
            best = t
    assert best is not None, (n, pref)
    return best


def _mm_nn_w(a, wg, out_dtype, name):
    M, K = a.shape
    return _mm(a, wg, "nn", _rows_tile(M, 512), _tile(wg.shape[2], 1536), _tile(K, 1536), out_dtype, name, b_chips=True)


def _mm_nt_w(a, wg, out_dtype, name):
    M, K = a.shape
    return _mm(a, wg, "nt", _rows_tile(M, 1088), _tile(wg.shape[1], 1024), _tile(wg.shape[2], 1536), out_dtype, name,
               b_chips=True)


def _mm_tn(a, b, out_dtype, name, out_chips=False):
    K, M = a.shape
    N = b.shape[1]
    ncol = N // N_CHIPS if out_chips else N
    tm, tn = _tile(M, 1408), _tile(ncol, 1408)
    if tm * tn > 1408 * 1152:
        tn = _tile(ncol, 1152)
    return _mm(a, b, "tn", tm, tn, _rows_tile(K, 2176), out_dtype, name, out_chips=out_chips)


def _mod_fwd(cv, wg, b, name):
    R, D = cv.shape
    tn = wg.shape[2]
    N = N_CHIPS * tn

    def body(cv_ref, w_ref, b_ref, mod_ref, sa_ref):
        sa = _silu(cv_ref[...]).astype(BF16)
        sa_ref[...] = sa
        mod_ref[...] = _nn(sa, w_ref[...]) + b_ref[...]

    return pl.pallas_call(
        body, name=name, grid=(N_CHIPS,),
        in_specs=[pl.BlockSpec((R, D), lambda j: (0, 0)), pl.BlockSpec((None, D, tn), lambda j: (j, 0, 0)),
                  pl.BlockSpec((1, tn), lambda j: (0, j))],
        out_specs=[pl.BlockSpec((R, tn), lambda j: (0, j)), pl.BlockSpec((R, D), lambda j: (0, 0))],
        out_shape=[jax.ShapeDtypeStruct((R, N), F32), jax.ShapeDtypeStruct((R, D), BF16)],
        compiler_params=_params("arbitrary"),
    )(cv, wg, b)


def _cvec_bwd(dmod, wg, cv, name):
    R, N = dmod.shape
    D = wg.shape[1]
    tk = wg.shape[2]
    nk = N_CHIPS

    def body(dm_ref, w_ref, cv_ref, o_ref):
        k = pl.program_id(0)

        @pl.when(k == 0)
        def _():
            o_ref[...] = jnp.zeros_like(o_ref)

        o_ref[...] += _nt(dm_ref[...].astype(BF16), w_ref[...])

        @pl.when(k == nk - 1)
        def _():
            o_ref[...] = o_ref[...] * _dsilu(cv_ref[...])

    return pl.pallas_call(
        body, name=name, grid=(nk,),
        in_specs=[pl.BlockSpec((R, tk), lambda k: (0, k)), pl.BlockSpec((None, D, tk), lambda k: (k, 0, 0)),
                  pl.BlockSpec((R, D), lambda k: (0, 0))],
        out_specs=pl.BlockSpec((R, D), lambda k: (0, 0)),
        out_shape=jax.ShapeDtypeStruct((R, D), F32),
        compiler_params=_params("arbitrary"),
    )(dmod, wg, cv)


def _norm_mod(x, nw, mod, which, ctx_rows, name):
    T, D = x.shape
    cb = ctx_rows // TB

    def body(x_ref, nw_ref, mod_ref, h_ref):
        xv = x_ref[...]
        r = lax.rsqrt(jnp.mean(xv * xv, axis=-1, keepdims=True) + RMS_EPS)
        y = xv * r * nw_ref[...]
        sh = mod_ref[which:which + 1, :]
        sc = mod_ref[which + 1:which + 2, :]
        h_ref[...] = (y * (1.0 + sc) + sh).astype(BF16)

    return pl.pallas_call(
        body, name=name, grid=(T // TB,),
        in_specs=[pl.BlockSpec((TB, D), lambda i: (i, 0)), pl.BlockSpec((1, D), lambda i: (0, 0)),
                  pl.BlockSpec((None, N_MOD, D), lambda i: (_stream_of(i, cb), 0, 0))],
        out_specs=pl.BlockSpec((TB, D), lambda i: (i, 0)),
        out_shape=jax.ShapeDtypeStruct((T, D), BF16),
        compiler_params=_params("parallel"),
    )(x, nw, mod)


def _norm_mod_bwd(dh, x, dres, nw, mod, which, ctx_rows, name):
    T, D = x.shape
    cb = ctx_rows // TB

    def body(dh_ref, x_ref, dres_ref, nw_ref, mod_ref, dx_ref, dm_ref, dnw_ref):
        i = pl.program_id(0)

        @pl.when(i == 0)
        def _():
            dnw_ref[...] = jnp.zeros_like(dnw_ref)

        @pl.when((i == 0) | (i == cb))
        def _():
            dm_ref[...] = jnp.zeros_like(dm_ref)

        xv = x_ref[...]
        dh = dh_ref[...]
        r = lax.rsqrt(jnp.mean(xv * xv, axis=-1, keepdims=True) + RMS_EPS)
        xh = xv * r
        nwv = nw_ref[...]
        sc = mod_ref[which + 1:which + 2, :]
        y = xh * nwv
        dm_ref[0:1, :] += jnp.sum(dh, axis=0, keepdims=True)
        dm_ref[1:2, :] += jnp.sum(dh * y, axis=0, keepdims=True)
        dy = dh * (1.0 + sc)
        dnw_ref[...] += jnp.sum(dy * xh, axis=0, keepdims=True)
        dxh = dy * nwv
        dx_ref[...] = dres_ref[...] + r * (dxh - xh * jnp.mean(dxh * xh, axis=-1, keepdims=True))

    return pl.pallas_call(
        body, name=name, grid=(T // TB,),
        in_specs=[pl.BlockSpec((TB, D), lambda i: (i, 0)), pl.BlockSpec((TB, D), lambda i: (i, 0)),
                  pl.BlockSpec((TB, D), lambda i: (i, 0)), pl.BlockSpec((1, D), lambda i: (0, 0)),
                  pl.BlockSpec((None, N_MOD, D), lambda i: (_stream_of(i, cb), 0, 0))],
        out_specs=[pl.BlockSpec((TB, D), lambda i: (i, 0)),
                   pl.BlockSpec((None, 2, D), lambda i: (_stream_of(i, cb), 0, 0)),
                   pl.BlockSpec((1, D), lambda i: (0, 0))],
        out_shape=[jax.ShapeDtypeStruct((T, D), F32), jax.ShapeDtypeStruct((2, 2, D), F32),
                   jax.ShapeDtypeStruct((1, D), F32)],
        compiler_params=_params("arbitrary"),
    )(dh, x, dres, nw, mod)


def _scan_chunk(n, rev, n_ctx, n_all):
    if not rev:
        return n
    return jnp.where(n < n_ctx, n_ctx - 1 - n, n_all - 1 + n_ctx - n)


def _cumsum_rows(x, rev):
    rows = x.shape[0]
    row = lax.broadcasted_iota(jnp.int32, (rows, 1), 0)
    s = 1
    while s < rows:
        if not rev:
            x = x + jnp.where(row >= s, pltpu.roll(x, s, 0), 0.0)
        else:
            x = x + jnp.where(row < rows - s, pltpu.roll(x, rows - s, 0), 0.0)
        s *= 2
    return x


def _lower_bound(hlb_ref, layer):
    h = hlb_ref[...]
    if layer == 0:
        return jnp.zeros_like(h[0:1, :])
    return _sigmoid(h[1:2, :] - h[0:1, :])


def _hgrn_gates(q_ref, f_ref, hlb_ref, layer, rev):
    lb = _lower_bound(hlb_ref, layer)
    z = f_ref[...]
    sig = _sigmoid(z)
    fg = lb + (1.0 - lb) * sig
    kk = (1.0 - lb) * (1.0 - sig)
    g = jnp.log(fg)
    b = _cumsum_rows(g, rev)
    bt = jnp.sum(g, axis=0, keepdims=True)
    mid = HG_CHUNK // 2
    r = b[mid:mid + 1, :] if rev else b[mid - 1:mid, :]
    qh = _silu(q_ref[...])
    return lb, sig, fg, kk, b, bt, r, qh


def _tri_mask(rev):
    t = lax.broadcasted_iota(jnp.int32, (HG_CHUNK, HG_CHUNK), 0)
    s = lax.broadcasted_iota(jnp.int32, (HG_CHUNK, HG_CHUNK), 1)
    return (s >= t) if rev else (s <= t)


def _hgrn_fwd(parts, hlb, layer, rev, ctx_rows, name, o_add=None):
    T = parts.shape[0]
    D = hlb.shape[1] // 2
    nh = D // HEAD
    n_all, n_ctx = T // HG_CHUNK, ctx_rows // HG_CHUNK
    chunk = functools.partial(_scan_chunk, rev=rev, n_ctx=n_ctx, n_all=n_all)
    fcol = 2 if rev else 1

    def body(q_ref, f_ref, i_ref, hlb_ref, *rest):
        if o_add is None:
            o_ref, st_ref, s_scr = rest
        else:
            oa_ref, o_ref, st_ref, s_scr = rest
        n = pl.program_id(0)

        @pl.when(n == 0)
        def _():
            s_scr[...] = jnp.zeros_like(s_scr)

        lb, sig, fg, kk, b, bt, r, qh = _hgrn_gates(q_ref, f_ref, hlb_ref, layer, rev)
        qr = (qh * jnp.exp(b - r)).astype(BF16)
        kr = (kk * jnp.exp(r - b)).astype(BF16)
        qe = (qh * jnp.exp(b)).astype(BF16)
        ke = (kk * jnp.exp(bt - b)).astype(BF16)
        dec = jnp.exp(bt)
        v = i_ref[...].astype(BF16)
        mask = _tri_mask(rev)
        for h in range(nh):
            sl = slice(h * HEAD, (h + 1) * HEAD)
            st = s_scr[h]
            st_ref[h] = st
            a = jnp.where(mask, _nt(qr[:, sl], kr[:, sl]), 0.0).astype(BF16)
            o = _nn(a, v[:, sl]) + _nt(qe[:, sl], st.astype(BF16))
            if o_add is not None:
                o = o + oa_ref[:, sl]
            o_ref[:, sl] = o
            s_scr[h] = st * dec[:, sl] + _tn(v[:, sl], ke[:, sl])

    cspec = lambda col: pl.BlockSpec((HG_CHUNK, D), lambda n: (chunk(n), col))
    ins = [parts, parts, parts, hlb]
    specs = [cspec(0), cspec(fcol), cspec(3), pl.BlockSpec((2, D), lambda n: (0, 1 if rev else 0))]
    if o_add is not None:
        ins.append(o_add)
        specs.append(cspec(0))
    return pl.pallas_call(
        body, name=name, grid=(n_all,), in_specs=specs,
        out_specs=[cspec(0), pl.BlockSpec((None, nh, HEAD, HEAD), lambda n: (n, 0, 0, 0))],
        out_shape=[jax.ShapeDtypeStruct((T, D), F32), jax.ShapeDtypeStruct((n_all, nh, HEAD, HEAD), F32)],
        scratch_shapes=[pltpu.VMEM((nh, HEAD, HEAD), F32)],
        compiler_params=_params("arbitrary"),
    )(*ins)


def _hgrn_bwd(parts, hlb, do, states, layer, rev, ctx_rows, name, dq_add=None, di_add=None):
    T = parts.shape[0]
    D = hlb.shape[1] // 2
    nh = D // HEAD
    n_all, n_ctx = T // HG_CHUNK, ctx_rows // HG_CHUNK
    step = lambda m: n_all - 1 - m
    chunk = lambda m: _scan_chunk(step(m), rev, n_ctx, n_all)
    fcol = 2 if rev else 1
    has_add = dq_add is not None

    def body(q_ref, f_ref, i_ref, hlb_ref, do_ref, st_ref, *rest):
        if has_add:
            dqa_ref, dia_ref, dq_ref, dz_ref, di_ref, dlb_ref, ds_scr = rest
        else:
            dq_ref, dz_ref, di_ref, dlb_ref, ds_scr = rest
        m = pl.program_id(0)

        @pl.when(m == 0)
        def _():
            ds_scr[...] = jnp.zeros_like(ds_scr)
            dlb_ref[...] = jnp.zeros_like(dlb_ref)

        lb, sig, fg, kk, b, bt, r, qh = _hgrn_gates(q_ref, f_ref, hlb_ref, layer, rev)
        e_qr = jnp.exp(b - r)
        e_kr = jnp.exp(r - b)
        e_b = jnp.exp(b)
        e_ke = jnp.exp(bt - b)
        dec = jnp.exp(bt)
        qr = (qh * e_qr).astype(BF16)
        kr = (kk * e_kr).astype(BF16)
        qe = (qh * e_b).astype(BF16)
        ke = (kk * e_ke).astype(BF16)
        vf = i_ref[...]
        v = vf.astype(BF16)
        dov = do_ref[...].astype(BF16)
        mask = _tri_mask(rev)
        dq_parts, dk_parts, dki_parts, dv_parts, dbt_parts = [], [], [], [], []
        for h in range(nh):
            sl = slice(h * HEAD, (h + 1) * HEAD)
            st = st_ref[h]
            stb = st.astype(BF16)
            dst = ds_scr[h]
            dstb = dst.astype(BF16)
            a = jnp.where(mask, _nt(qr[:, sl], kr[:, sl]), 0.0).astype(BF16)
            da = jnp.where(mask, _nt(dov[:, sl], v[:, sl]), 0.0).astype(BF16)
            dv_parts.append(_tn(a, dov[:, sl]) + _nt(ke[:, sl], dstb))
            dq_h = _nn(da, kr[:, sl]) * e_qr[:, sl] + _nn(dov[:, sl], stb) * e_b[:, sl]
            dk_inter = _nn(v[:, sl], dstb) * e_ke[:, sl]
            dk_h = _tn(da, qr[:, sl]) * e_kr[:, sl] + dk_inter
            dq_parts.append(dq_h)
            dk_parts.append(dk_h)
            dki_parts.append(dk_inter)
            dbt_parts.append(dec[:, sl] * jnp.sum(st * dst, axis=0, keepdims=True))
            ds_scr[h] = dst * dec[:, sl] + _tn(dov[:, sl], qe[:, sl])
        dq = jnp.concatenate(dq_parts, axis=1)
        dk = jnp.concatenate(dk_parts, axis=1)
        dki = jnp.concatenate(dki_parts, axis=1)
        dv = jnp.concatenate(dv_parts, axis=1)
        dbt = jnp.concatenate(dbt_parts, axis=1) + jnp.sum(kk * dki, axis=0, keepdims=True)
        db = qh * dq - kk * dk
        dg = _cumsum_rows(db, not rev) + dbt
        df = dg / fg - dk
        dz_ref[...] = (df * (1.0 - lb) * sig * (1.0 - sig)).astype(BF16)
        dlb_ref[...] += jnp.sum(df * (1.0 - sig), axis=0, keepdims=True)
        dqr = dq * _dsilu(q_ref[...])
        if has_add:
            dqr = dqr + dqa_ref[...]
            dv = dv + dia_ref[...]
        dq_ref[...] = dqr
        di_ref[...] = dv

        @pl.when(m == n_all - 1)
        def _():
            if layer == 0:
                dlb_ref[...] = jnp.zeros_like(dlb_ref)
            else:
                dlb_ref[...] = dlb_ref[...] * lb * (1.0 - lb)

    cspec = lambda col: pl.BlockSpec((HG_CHUNK, D), lambda m: (chunk(m), col))
    ins = [parts, parts, parts, hlb, do, states]
    specs = [cspec(0), cspec(fcol), cspec(3), pl.BlockSpec((2, D), lambda m: (0, 1 if rev else 0)), cspec(0),
             pl.BlockSpec((None, nh, HEAD, HEAD), lambda m: (step(m), 0, 0, 0))]
    if has_add:
        ins += [dq_add, di_add]
        specs += [cspec(0), cspec(0)]
    return pl.pallas_call(
        body, name=name, grid=(n_all,), in_specs=specs,
        out_specs=[cspec(0), cspec(0), cspec(0), pl.BlockSpec((1, D), lambda m: (0, 0))],
        out_shape=[jax.ShapeDtypeStruct((T, D), F32), jax.ShapeDtypeStruct((T, D), BF16),
                   jax.ShapeDtypeStruct((T, D), F32), jax.ShapeDtypeStruct((1, D), F32)],
        scratch_shapes=[pltpu.VMEM((nh, HEAD, HEAD), F32)],
        compiler_params=_params("arbitrary"),
    )(*ins)


def _sgu_ln(v_ref, lnw_ref, lnb_ref):
    gv = _gelu(v_ref[...])
    mu = jnp.mean(gv, axis=-1, keepdims=True)
    xc = gv - mu
    rstd = lax.rsqrt(jnp.mean(xc * xc, axis=-1, keepdims=True) + LN_EPS)
    xh = xc * rstd
    return xh, rstd, xh * lnw_ref[...] + lnb_ref[...]


def _sgu_fwd(parts, lnw, lnb, w, bt, name):
    T = parts.shape[0]
    D = lnw.shape[1]
    G = D // HEAD

    def body(u_ref, v_ref, lnw_ref, lnb_ref, w_ref, bt_ref, ya_ref):
        gu = _gelu(u_ref[...])
        _, _, vn = _sgu_ln(v_ref, lnw_ref, lnb_ref)
        vnb = vn.astype(BF16)
        for g in range(G):
            sl = slice(g * HEAD, (g + 1) * HEAD)
            mixed = _nn(w_ref[g], vnb[:, sl]) + bt_ref[:, g:g + 1]
            ya_ref[:, sl] = (gu[:, sl] * mixed).astype(BF16)

    return pl.pallas_call(
        body, name=name, grid=(T // SGU_CHUNK,),
        in_specs=[pl.BlockSpec((SGU_CHUNK, D), lambda n: (n, 4)), pl.BlockSpec((SGU_CHUNK, D), lambda n: (n, 5)),
                  pl.BlockSpec((1, D), lambda n: (0, 0)), pl.BlockSpec((1, D), lambda n: (0, 0)),
                  pl.BlockSpec((G, SGU_CHUNK, SGU_CHUNK), lambda n: (0, 0, 0)),
                  pl.BlockSpec((SGU_CHUNK, G), lambda n: (0, 0))],
        out_specs=pl.BlockSpec((SGU_CHUNK, D), lambda n: (n, 0)),
        out_shape=jax.ShapeDtypeStruct((T, D), BF16),
        compiler_params=_params("parallel"),
    )(parts, parts, lnw, lnb, w, bt)


def _sgu_bwd(parts, dya, lnw, lnb, w, bt, name):
    T = parts.shape[0]
    D = lnw.shape[1]
    G = D // HEAD

    def body(u_ref, v_ref, dya_ref, lnw_ref, lnb_ref, w_ref, bt_ref,
             du_ref, dv_ref, dw_ref, dbt_ref, dlnw_ref, dlnb_ref, dvn_scr):
        n = pl.program_id(0)

        @pl.when(n == 0)
        def _():
            dw_ref[...] = jnp.zeros_like(dw_ref)
            dbt_ref[...] = jnp.zeros_like(dbt_ref)
            dlnw_ref[...] = jnp.zeros_like(dlnw_ref)
            dlnb_ref[...] = jnp.zeros_like(dlnb_ref)

        u = u_ref[...]
        gu = _gelu(u)
        xh, rstd, vn = _sgu_ln(v_ref, lnw_ref, lnb_ref)
        vnb = vn.astype(BF16)
        dya = dya_ref[...]
        lane = lax.broadcasted_iota(jnp.int32, (SGU_CHUNK, G), 1)
        dbt = jnp.zeros((SGU_CHUNK, G), F32)
        for g in range(G):
            sl = slice(g * HEAD, (g + 1) * HEAD)
            wg = w_ref[g]
            mixed = _nn(wg, vnb[:, sl]) + bt_ref[:, g:g + 1]
            dmix = dya[:, sl] * gu[:, sl]
            du_ref[:, sl] = (dya[:, sl] * mixed * _dgelu(u[:, sl])).astype(BF16)
            dmb = dmix.astype(BF16)
            dvn_scr[:, sl] = _tn(wg, dmb)
            dw_ref[g] += _nt(dmb, vnb[:, sl])
            dbt = dbt + jnp.where(lane == g, jnp.sum(dmix, axis=1, keepdims=True), 0.0)
        dbt_ref[...] += dbt
        dvn = dvn_scr[...]
        dlnw_ref[...] += jnp.sum(dvn * xh, axis=0, keepdims=True)
        dlnb_ref[...] += jnp.sum(dvn, axis=0, keepdims=True)
        dxh = dvn * lnw_ref[...]
        dgv = rstd * (dxh - jnp.mean(dxh, axis=-1, keepdims=True) - xh * jnp.mean(dxh * xh, axis=-1, keepdims=True))
        dv_ref[...] = (dgv * _dgelu(v_ref[...])).astype(BF16)

    row = lambda col: pl.BlockSpec((SGU_CHUNK, D), lambda n: (n, col))
    vec = pl.BlockSpec((1, D), lambda n: (0, 0))
    wsp = pl.BlockSpec((G, SGU_CHUNK, SGU_CHUNK), lambda n: (0, 0, 0))
    bsp = pl.BlockSpec((SGU_CHUNK, G), lambda n: (0, 0))
    return pl.pallas_call(
        body, name=name, grid=(T // SGU_CHUNK,),
        in_specs=[row(4), row(5), row(0), vec, vec, wsp, bsp],
        out_specs=[row(0), row(0), wsp, bsp, vec, vec],
        out_shape=[jax.ShapeDtypeStruct((T, D), BF16), jax.ShapeDtypeStruct((T, D), BF16),
                   jax.ShapeDtypeStruct((G, SGU_CHUNK, SGU_CHUNK), F32), jax.ShapeDtypeStruct((SGU_CHUNK, G), F32),
                   jax.ShapeDtypeStruct((1, D), F32), jax.ShapeDtypeStruct((1, D), F32)],
        scratch_shapes=[pltpu.VMEM((SGU_CHUNK, D), F32)],
        compiler_params=_params("arbitrary"),
    )(parts, parts, dya, lnw, lnb, w, bt)


TBT = 128


def _rows_weight_spec(wg):
    return pl.BlockSpec(wg.shape, lambda i: (0, 0, 0))


def _full(w_ref):
    return w_ref[...].reshape(w_ref.shape[0] * w_ref.shape[1], w_ref.shape[2])


def _token_out_fwd(o, parts, ya, x, mod, hnw, wa, wb, wo, ctx_rows, name):
    T, D = x.shape
    nh = D // HEAD
    cb = ctx_rows // TBT

    def body(o_ref, og_ref, ga_ref, gb_ref, ya_ref, x_ref, mod_ref, hnw_ref, wa_ref, wb_ref, wo_ref,
             yb_ref, pa_ref, pb_ref, mg_ref, tmo_ref, xm_ref):
        ov = o_ref[...]
        so = _silu(og_ref[...])
        nw = hnw_ref[...]
        for h in range(nh):
            sl = slice(h * HEAD, (h + 1) * HEAD)
            seg = ov[:, sl]
            r = lax.rsqrt(jnp.mean(seg * seg, axis=-1, keepdims=True) + RMS_EPS)
            yb_ref[:, sl] = (seg * r * nw * so[:, sl]).astype(BF16)
        pa = _nn(ya_ref[...], _full(wa_ref))
        pb = _nn(yb_ref[...], _full(wb_ref))
        pa_ref[...] = pa
        pb_ref[...] = pb
        mg = (_sigmoid(ga_ref[...]) * pa + _sigmoid(gb_ref[...]) * pb).astype(BF16)
        mg_ref[...] = mg
        out = _nn(mg, _full(wo_ref))
        tmo_ref[...] = out
        xm_ref[...] = x_ref[...] + mod_ref[2:3, :] * out

    row = lambda col: pl.BlockSpec((TBT, D), lambda i: (i, col))
    wsp = _rows_weight_spec(wa)
    sd = lambda dt: jax.ShapeDtypeStruct((T, D), dt)
    return pl.pallas_call(
        body, name=name, grid=(T // TBT,),
        in_specs=[row(0), row(6), row(7), row(8), row(0), row(0),
                  pl.BlockSpec((None, N_MOD, D), lambda i: (_stream_of(i, cb), 0, 0)),
                  pl.BlockSpec((1, HEAD), lambda i: (0, 0)), wsp, wsp, wsp],
        out_specs=[row(0)] * 6,
        out_shape=[sd(BF16), sd(F32), sd(F32), sd(BF16), sd(F32), sd(F32)],
        compiler_params=_params("parallel"),
    )(o, parts, parts, parts, ya, x, mod, hnw, wa, wb, wo)


def _token_out_bwd(dx, tmo, pa, pb, o, parts, mod, hnw, wa, wb, wo, ctx_rows, name):
    T, D = dx.shape
    nh = D // HEAD
    cb = ctx_rows // TBT

    def body(dx_ref, tmo_ref, pa_ref, pb_ref, o_ref, og_ref, ga_ref, gb_ref, mod_ref, hnw_ref, wa_ref, wb_ref, wo_ref,
             dout_ref, dpa_ref, dpb_ref, dog_ref, dga_ref, dgb_ref, dya_ref, do_ref, dg1_ref, dhnw_ref):
        i = pl.program_id(0)

        @pl.when(i == 0)
        def _():
            dhnw_ref[...] = jnp.zeros_like(dhnw_ref)

        @pl.when((i == 0) | (i == cb))
        def _():
            dg1_ref[...] = jnp.zeros_like(dg1_ref)

        dxv = dx_ref[...]
        dg1_ref[...] += jnp.sum(dxv * tmo_ref[...], axis=0, keepdims=True)
        dout = (dxv * mod_ref[2:3, :]).astype(BF16)
        dout_ref[...] = dout
        dmg = _nt(dout, _full(wo_ref))
        sa = _sigmoid(ga_ref[...])
        sb = _sigmoid(gb_ref[...])
        dpa = (dmg * sa).astype(BF16)
        dpb = (dmg * sb).astype(BF16)
        dpa_ref[...] = dpa
        dpb_ref[...] = dpb
        dga_ref[...] = (dmg * pa_ref[...] * sa * (1.0 - sa)).astype(BF16)
        dgb_ref[...] = (dmg * pb_ref[...] * sb * (1.0 - sb)).astype(BF16)
        dya_ref[...] = _nt(dpa, _full(wa_ref))
        dyb = _nt(dpb, _full(wb_ref))
        og = og_ref[...]
        so = _silu(og)
        dso = _dsilu(og)
        ov = o_ref[...]
        nw = hnw_ref[...]
        dnw = jnp.zeros((1, HEAD), F32)
        for h in range(nh):
            sl = slice(h * HEAD, (h + 1) * HEAD)
            seg = ov[:, sl]
            r = lax.rsqrt(jnp.mean(seg * seg, axis=-1, keepdims=True) + RMS_EPS)
            oh = seg * r
            dn = dyb[:, sl] * so[:, sl]
            dog_ref[:, sl] = (dyb[:, sl] * oh * nw * dso[:, sl]).astype(BF16)
            dnw = dnw + jnp.sum(dn * oh, axis=0, keepdims=True)
            doh = dn * nw
            do_ref[:, sl] = r * (doh - oh * jnp.mean(doh * oh, axis=-1, keepdims=True))
        dhnw_ref[...] += dnw

    row = lambda col: pl.BlockSpec((TBT, D), lambda i: (i, col))
    wsp = _rows_weight_spec(wa)
    sd = lambda dt: jax.ShapeDtypeStruct((T, D), dt)
    return pl.pallas_call(
        body, name=name, grid=(T // TBT,),
        in_specs=[row(0), row(0), row(0), row(0), row(0), row(6), row(7), row(8),
                  pl.BlockSpec((None, N_MOD, D), lambda i: (_stream_of(i, cb), 0, 0)),
                  pl.BlockSpec((1, HEAD), lambda i: (0, 0)), wsp, wsp, wsp],
        out_specs=[row(0)] * 8 + [pl.BlockSpec((None, 1, D), lambda i: (_stream_of(i, cb), 0, 0)),
                                  pl.BlockSpec((1, HEAD), lambda i: (0, 0))],
        out_shape=[sd(BF16)] * 6 + [sd(F32), sd(F32), jax.ShapeDtypeStruct((2, 1, D), F32),
                                    jax.ShapeDtypeStruct((1, HEAD), F32)],
        compiler_params=_params("arbitrary"),
    )(dx, tmo, pa, pb, o, parts, parts, parts, mod, hnw, wa, wb, wo)


def _conv_geometry(i, nb, cb):
    is_ctx = i < cb
    first = (i == 0) | (i == cb)
    last = (i == cb - 1) | (i == nb - 1)
    row = lax.broadcasted_iota(jnp.int32, (TB + 2 * GRID_W, 1), 0)
    w = row & (GRID_W - 1)
    left_ok = (w != 0) | is_ctx
    right_ok = (w != GRID_W - 1) | is_ctx
    return is_ctx, first, last, left_ok, right_ok


def _ext(p_ref, m_ref, n_ref, first, last):
    return jnp.concatenate([jnp.where(first, 0.0, p_ref[...]), m_ref[...], jnp.where(last, 0.0, n_ref[...])], axis=0)


def _shift_prev(e, ok):
    return jnp.where(ok, pltpu.roll(e, 1, 0), 0.0)


def _shift_next(e, ok):
    return jnp.where(ok, pltpu.roll(e, e.shape[0] - 1, 0), 0.0)


def _halo_specs(cbk, n64, coff=0):
    r = TB // GRID_W
    prev = pl.BlockSpec((GRID_W, cbk), lambda j, i: (jnp.maximum(r * i - 1, 0), j + coff))
    main = pl.BlockSpec((TB, cbk), lambda j, i: (i, j + coff))
    nxt = pl.BlockSpec((GRID_W, cbk), lambda j, i: (jnp.minimum(r * i + r, n64 - 1), j + coff))
    return [prev, main, nxt]


def _conv_cblock(dff):
    return _tile(dff, 1408)


def _conv_fwd(up, cw, cbias, ctx_rows, name):
    T, dff = up.shape[0], up.shape[1] // 2
    cbk = _conv_cblock(dff)
    nb, cb = T // TB, ctx_rows // TB
    nvb = dff // cbk

    def body(ap_ref, a_ref, an_ref, v_ref, cw_ref, cb_ref, ac_ref, act_ref):
        i = pl.program_id(1)
        is_ctx, first, last, lok, rok = _conv_geometry(i, nb, cb)
        e = _ext(ap_ref, a_ref, an_ref, first, last)
        el = _shift_prev(e, lok)
        er = _shift_next(e, rok)
        cwv = cw_ref[...]

        def comb(dr, lo):
            sl = slice(lo, lo + TB)
            return cwv[3 * dr:3 * dr + 1] * el[sl] + cwv[3 * dr + 1:3 * dr + 2] * e[sl] + cwv[3 * dr + 2:3 * dr + 3] * er[sl]

        out = comb(1, GRID_W) + jnp.where(is_ctx, 0.0, comb(0, 0) + comb(2, 2 * GRID_W))
        a_c = out + cb_ref[...]
        ac_ref[...] = a_c
        act_ref[...] = (_gelu(a_c) * v_ref[...]).astype(BF16)

    main = pl.BlockSpec((TB, cbk), lambda j, i: (i, j))
    return pl.pallas_call(
        body, name=name, grid=(dff // cbk, nb),
        in_specs=_halo_specs(cbk, T // GRID_W) + [pl.BlockSpec((TB, cbk), lambda j, i: (i, j + nvb)),
                                                 pl.BlockSpec((9, cbk), lambda j, i: (0, j)),
                                                 pl.BlockSpec((1, cbk), lambda j, i: (0, j))],
        out_specs=[main, main],
        out_shape=[jax.ShapeDtypeStruct((T, dff), F32), jax.ShapeDtypeStruct((T, dff), BF16)],
        compiler_params=_params("parallel", "parallel"),
    )(up, up, up, up, cw, cbias)


def _conv_bwd(up, ac, dact, cw, ctx_rows, name):
    T, dff = up.shape[0], up.shape[1] // 2
    cbk = _conv_cblock(dff)
    nb, cb = T // TB, ctx_rows // TB
    nvb = dff // cbk

    def body(ap_ref, a_ref, an_ref, vp_ref, v_ref, vn_ref, cp_ref, c_ref, cn_ref, dp_ref, d_ref, dn_ref, cw_ref,
             da_ref, dv_ref, dcw_ref, dcb_ref):
        i = pl.program_id(1)

        @pl.when(i == 0)
        def _():
            dcw_ref[...] = jnp.zeros_like(dcw_ref)
            dcb_ref[...] = jnp.zeros_like(dcb_ref)

        is_ctx, first, last, lok, rok = _conv_geometry(i, nb, cb)
        ace = _ext(cp_ref, c_ref, cn_ref, first, last)
        g = _ext(dp_ref, d_ref, dn_ref, first, last) * _ext(vp_ref, v_ref, vn_ref, first, last) * _dgelu(ace)
        dv_ref[...] = (d_ref[...] * _gelu(c_ref[...])).astype(BF16)
        gm = _shift_prev(g, lok)
        gp = _shift_next(g, rok)
        cwv = cw_ref[...]

        def comb(dr, lo):
            sl = slice(lo, lo + TB)
            return cwv[3 * dr:3 * dr + 1] * gp[sl] + cwv[3 * dr + 1:3 * dr + 2] * g[sl] + cwv[3 * dr + 2:3 * dr + 3] * gm[sl]

        da = comb(1, GRID_W) + jnp.where(is_ctx, 0.0, comb(0, 2 * GRID_W) + comb(2, 0))
        da_ref[...] = da.astype(BF16)
        e = _ext(ap_ref, a_ref, an_ref, first, last)
        taps = [_shift_prev(e, lok), e, _shift_next(e, rok)]
        gmain = g[GRID_W:GRID_W + TB]
        dcb_ref[...] += jnp.sum(gmain, axis=0, keepdims=True)
        vert = jnp.where(is_ctx, 0.0, 1.0)
        for dr in range(3):
            sl = slice(dr * GRID_W, dr * GRID_W + TB)
            for dw in range(3):
                s = jnp.sum(gmain * taps[dw][sl], axis=0, keepdims=True)
                if dr != 1:
                    s = s * vert
                k = 3 * dr + dw
                dcw_ref[k:k + 1, :] += s

    main = pl.BlockSpec((TB, cbk), lambda j, i: (i, j))
    halo = _halo_specs(cbk, T // GRID_W)
    acc9 = pl.BlockSpec((9, cbk), lambda j, i: (0, j))
    acc1 = pl.BlockSpec((1, cbk), lambda j, i: (0, j))
    return pl.pallas_call(
        body, name=name, grid=(dff // cbk, nb),
        in_specs=halo + _halo_specs(cbk, T // GRID_W, nvb) + halo + halo + [acc9],
        out_specs=[main, main, acc9, acc1],
        out_shape=[jax.ShapeDtypeStruct((T, dff), BF16), jax.ShapeDtypeStruct((T, dff), BF16),
                   jax.ShapeDtypeStruct((9, dff), F32), jax.ShapeDtypeStruct((1, dff), F32)],
        compiler_params=_params("parallel", "arbitrary"),
    )(up, up, up, up, up, up, ac, ac, ac, dact, dact, dact, cw)


def _ffn_out_fwd(act, xm, mod, wd, ctx_rows, name):
    T, D = xm.shape
    dff = act.shape[1]
    cb = ctx_rows // TB

    def body(act_ref, x_ref, mod_ref, w_ref, xo_ref, fo_ref):
        out = _nn(act_ref[...], _full(w_ref))
        fo_ref[...] = out
        xo_ref[...] = x_ref[...] + mod_ref[5:6, :] * out

    row = pl.BlockSpec((TB, D), lambda i: (i, 0))
    return pl.pallas_call(
        body, name=name, grid=(T // TB,),
        in_specs=[pl.BlockSpec((TB, dff), lambda i: (i, 0)), row,
                  pl.BlockSpec((None, N_MOD, D), lambda i: (_stream_of(i, cb), 0, 0)),
                  _rows_weight_spec(wd)],
        out_specs=[row, row],
        out_shape=[jax.ShapeDtypeStruct((T, D), F32), jax.ShapeDtypeStruct((T, D), F32)],
        compiler_params=_params("parallel"),
    )(act, xm, mod, wd)


def _ffn_out_bwd(dx, fo, mod, wd, ctx_rows, name):
    T, D = dx.shape
    dff = N_CHIPS * wd.shape[1]
    cb = ctx_rows // TB

    def body(dx_ref, fo_ref, mod_ref, w_ref, dout_ref, dact_ref, dg2_ref):
        i = pl.program_id(0)

        @pl.when((i == 0) | (i == cb))
        def _():
            dg2_ref[...] = jnp.zeros_like(dg2_ref)

        dxv = dx_ref[...]
        dg2_ref[...] += jnp.sum(dxv * fo_ref[...], axis=0, keepdims=True)
        dout = (dxv * mod_ref[5:6, :]).astype(BF16)
        dout_ref[...] = dout
        dact_ref[...] = _nt(dout, _full(w_ref))

    row = pl.BlockSpec((TB, D), lambda i: (i, 0))
    return pl.pallas_call(
        body, name=name, grid=(T // TB,),
        in_specs=[row, row, pl.BlockSpec((None, N_MOD, D), lambda i: (_stream_of(i, cb), 0, 0)),
                  _rows_weight_spec(wd)],
        out_specs=[row, pl.BlockSpec((TB, dff), lambda i: (i, 0)),
                   pl.BlockSpec((None, 1, D), lambda i: (_stream_of(i, cb), 0, 0))],
        out_shape=[jax.ShapeDtypeStruct((T, D), BF16), jax.ShapeDtypeStruct((T, dff), F32),
                   jax.ShapeDtypeStruct((2, 1, D), F32)],
        compiler_params=_params("arbitrary"),
    )(dx, fo, mod, wd)


def _loss_bwd(x, target, fw, ctx_rows, name):
    T, D = x.shape
    cb = ctx_rows // TB

    def body(x_ref, t_ref, fw_ref, dx_ref, loss_ref, dfw_ref):
        i = pl.program_id(0)

        @pl.when(i == 0)
        def _():
            loss_ref[...] = jnp.zeros_like(loss_ref)
            dfw_ref[...] = jnp.zeros_like(dfw_ref)

        @pl.when(i < cb)
        def _():
            dx_ref[...] = jnp.zeros_like(dx_ref)

        @pl.when(i >= cb)
        def _():
            xv = x_ref[...]
            r = lax.rsqrt(jnp.mean(xv * xv, axis=-1, keepdims=True) + RMS_EPS)
            xh = xv * r
            fwv = fw_ref[...]
            err = xh * fwv - t_ref[...]
            loss_ref[...] += (0.5 / D) * jnp.sum(err * err)
            dy = err * (1.0 / D)
            dfw_ref[...] += jnp.sum(dy * xh, axis=0, keepdims=True)
            dxh = dy * fwv
            dx_ref[...] = r * (dxh - xh * jnp.mean(dxh * xh, axis=-1, keepdims=True))

    row = pl.BlockSpec((TB, D), lambda i: (i, 0))
    return pl.pallas_call(
        body, name=name, grid=(T // TB,),
        in_specs=[row, pl.BlockSpec((TB, D), lambda i: (jnp.maximum(i - cb, 0), 0)), pl.BlockSpec((1, D), lambda i: (0, 0))],
        out_specs=[row, pl.BlockSpec((1, 128), lambda i: (0, 0)), pl.BlockSpec((1, D), lambda i: (0, 0))],
        out_shape=[jax.ShapeDtypeStruct((T, D), F32), jax.ShapeDtypeStruct((1, 128), F32),
                   jax.ShapeDtypeStruct((1, D), F32)],
        compiler_params=_params("arbitrary"),
    )(x, target, fw)


def _adamw(w, gs, m, v, name):
    L, R, C = w.shape
    assert len(gs) == L
    rb = _rows_tile(R, max(16, (1 << 18) // C // 16 * 16))
    bc1 = 1.0 - ADAM_B1 ** ADAM_STEP
    bc2 = 1.0 - ADAM_B2 ** ADAM_STEP

    def body(w_ref, m_ref, v_ref, *rest):
        g_refs, (g_ref, d_ref, nm_ref, nv_ref) = rest[:L], rest[L:]
        layer = pl.program_id(0)
        for li in range(L):
            @pl.when(layer == li)
            def _():
                gv = g_refs[li][...]
                g_ref[...] = gv
                nm = ADAM_B1 * m_ref[...] + (1.0 - ADAM_B1) * gv
                nv = ADAM_B2 * v_ref[...] + (1.0 - ADAM_B2) * (gv * gv)
                nm_ref[...] = nm
                nv_ref[...] = nv
                d_ref[...] = -ADAM_LR * ((nm / bc1) / (jnp.sqrt(nv / bc2) + ADAM_EPS) + ADAM_WD * w_ref[...])

    blk = pl.BlockSpec((None, rb, C), lambda l, i: (l, i, 0))
    gblk = pl.BlockSpec((rb, C), lambda l, i: (i, 0))
    sd = jax.ShapeDtypeStruct((L, R, C), F32)
    return pl.pallas_call(
        body, name=name, grid=(L, R // rb), in_specs=[blk] * 3 + [gblk] * L, out_specs=[blk] * 4, out_shape=[sd] * 4,
        compiler_params=_params("parallel", "parallel"),
    )(w, m, v, *gs)


def _local_step(xs, cv, target, W, layer_weights, on_layer_grads, ctx_rows):
    T, D = xs.shape
    depth = W["norm1_w"].shape[0]
    saved = []
    X = xs
    for l in range(depth):
        s = {}
        Wl = layer_weights(l, X)
        mod_all, sa = _mod_fwd(cv, Wl["ada_w"], W["ada_b"][l][None, :] + Wl["token"], f"mod_fwd_{l}")
        mod = mod_all[:2].reshape(2, N_MOD, D)
        h1 = _norm_mod(X, W["norm1_w"][l][None, :], mod, 0, ctx_rows, f"norm1_{l}")
        parts = _mm_nn_w(h1, Wl["w_in"], F32, f"in_proj_{l}")
        o_f, st_f = _hgrn_fwd(parts, W["hlb"], l, False, ctx_rows, f"hgrn_fwd_f_{l}")
        o, st_b = _hgrn_fwd(parts, W["hlb"], l, True, ctx_rows, f"hgrn_fwd_b_{l}", o_add=o_f)
        ya = _sgu_fwd(parts, W["sgu_ln_w"][l][None, :], W["sgu_ln_b"][l][None, :], W["sgu_w"][l], W["sgu_bt"][l],
                      f"sgu_fwd_{l}")
        Wl.update(Wl.pop("late")(ya))
        yb, pa, pb, mg, tmo, xm = _token_out_fwd(o, parts, ya, X, mod, W["hnw"][l][None, :] + Wl["late_token"], Wl["w_a"],
                                                 Wl["w_b"], Wl["w_o"], ctx_rows, f"token_out_fwd_{l}")
        h2 = _norm_mod(xm, W["norm2_w"][l][None, :], mod, 3, ctx_rows, f"norm2_{l}")
        up = _mm_nn_w(h2, Wl["w_up"], F32, f"up_proj_{l}")
        ac, act = _conv_fwd(up, Wl["conv_w"], W["conv_b"][l][None, :], ctx_rows, f"conv_fwd_{l}")
        xo, fo = _ffn_out_fwd(act, xm, mod, Wl["w_down"], ctx_rows, f"ffn_out_fwd_{l}")
        s.update(X=X, Wl=Wl, mod=mod, mod_all=mod_all, sa=sa, h1=h1, parts=parts, o=o, st_f=st_f, st_b=st_b, ya=ya, yb=yb,
                 pa=pa, pb=pb, mg=mg, tmo=tmo, xm=xm, h2=h2, up=up, ac=ac, act=act, fo=fo)
        saved.append(s)
        X = xo

    dX, loss_row, dfw = _loss_bwd(X, target, W["final_norm_w"][None, :], ctx_rows, "loss_bwd")
    G = {k: [None] * depth for k in ("ada_b", "norm1_w", "sgu_ln_w", "sgu_ln_b", "sgu_w", "sgu_b", "hlb1", "hnw", "norm2_w",
                                     "conv_w", "conv_b", "dmod")}
    dcv = jnp.zeros_like(cv)
    for l in reversed(range(depth)):
        s = saved[l]
        mod, Wl = s["mod"], s["Wl"]
        big = {}
        dout2, dact, dg2 = _ffn_out_bwd(dX, s["fo"], mod, Wl["w_down"], ctx_rows, f"ffn_out_bwd_{l}")
        big["w_down"] = _mm_tn(s["act"], dout2, F32, f"dw_down_{l}")
        da, dv, dcw, dcb = _conv_bwd(s["up"], s["ac"], dact, Wl["conv_w"], ctx_rows, f"conv_bwd_{l}")
        G["conv_w"][l], G["conv_b"][l] = dcw, dcb[0]
        dup = jnp.concatenate([da, dv], axis=1)
        big["w_up"] = _mm_tn(s["h2"], dup, F32, f"dw_up_{l}", out_chips=True)
        dh2 = _mm_nt_w(dup, Wl["w_up"], F32, f"dh2_{l}")
        dxm, dm2, dnw2 = _norm_mod_bwd(dh2, s["xm"], dX, W["norm2_w"][l][None, :], mod, 3, ctx_rows, f"norm2_bwd_{l}")
        G["norm2_w"][l] = dnw2[0]
        (dout1, dpa, dpb, dog, dga, dgb, dya, do, dg1, dhnw) = _token_out_bwd(
            dxm, s["tmo"], s["pa"], s["pb"], s["o"], s["parts"], mod, W["hnw"][l][None, :], Wl["w_a"], Wl["w_b"], Wl["w_o"],
            ctx_rows, f"token_out_bwd_{l}")
        G["hnw"][l] = dhnw[0]
        big["w_o"] = _mm_tn(s["mg"], dout1, F32, f"dw_o_{l}")
        big["w_a"] = _mm_tn(s["ya"], dpa, F32, f"dw_a_{l}")
        big["w_b"] = _mm_tn(s["yb"], dpb, F32, f"dw_b_{l}")
        tok = on_layer_grads(l, "early", big)
        du, dvs, dsw, dsbt, dlnw, dlnb = _sgu_bwd(s["parts"], dya, W["sgu_ln_w"][l][None, :], W["sgu_ln_b"][l][None, :] + tok,
                                                  W["sgu_w"][l], W["sgu_bt"][l], f"sgu_bwd_{l}")
        G["sgu_w"][l], G["sgu_b"][l], G["sgu_ln_w"][l], G["sgu_ln_b"][l] = dsw, dsbt.T, dlnw[0], dlnb[0]
        dq_f, dz_f, di_f, dlb_f = _hgrn_bwd(s["parts"], W["hlb"], do, s["st_f"], l, False, ctx_rows, f"hgrn_bwd_f_{l}")
        dq, dz_b, di, dlb_b = _hgrn_bwd(s["parts"], W["hlb"], do, s["st_b"], l, True, ctx_rows, f"hgrn_bwd_b_{l}",
                                        dq_add=dq_f, di_add=di_f)
        G["hlb1"][l] = jnp.concatenate([dlb_f[0], dlb_b[0]])
        dparts = jnp.concatenate([dq.astype(BF16), dz_f, dz_b, di.astype(BF16), du, dvs, dog, dga, dgb], axis=1)
        tok = on_layer_grads(l, "late", {"w_in": _mm_tn(s["h1"], dparts, F32, f"dw_in_{l}", out_chips=True)})
        dh1 = _mm_nt_w(dparts, Wl["w_in"], F32, f"dh1_{l}")
        dX, dm1, dnw1 = _norm_mod_bwd(dh1, s["X"], dxm, W["norm1_w"][l][None, :] + tok, mod, 0, ctx_rows, f"norm1_bwd_{l}")
        G["norm1_w"][l] = dnw1[0]
        dmod = jnp.concatenate([dm1, dg1, dm2, dg2], axis=1).reshape(2, N_MOD * D)
        dmod16 = jnp.concatenate([dmod, jnp.zeros((cv.shape[0] - 2, N_MOD * D), F32)], axis=0)
        G["ada_b"][l] = dmod[0] + dmod[1]
        G["dmod"][l] = dmod
        dcv = dcv + _cvec_bwd(dmod16, Wl["ada_w"], cv, f"dcvec_{l}")
    G["c_ctx"] = dcv[0]
    G["final_norm_w"] = dfw[0]
    return loss_row[0, 0], dX, G, saved[0]["sa"]


def _chip_peers(x, y, c):
    return [((1 - x, y, c), 2 * (1 - x) + y), ((x, 1 - y, c), 2 * x + 1 - y), ((1 - x, 1 - y, c), 2 * (1 - x) + 1 - y)]


def _rdma_call(ins, out_shapes, plan, n_remote, n_local, name, aliases=None):
    n_in, n_out = len(ins), len(out_shapes)

    def body(*refs):
        in_refs, out_refs = refs[:n_in], refs[n_in:n_in + n_out]
        send_sems, recv_sems, local_sems = refs[n_in + n_out:]
        x, y, c = lax.axis_index("x"), lax.axis_index("y"), lax.axis_index("c")
        remote, local = plan(in_refs, out_refs, x, y, c)
        assert len(remote) == n_remote and len(local) == n_local, (name, len(remote), len(local))
        copies = [pltpu.make_async_copy(s, d, local_sems.at[i]) for i, (s, d) in enumerate(local)]
        copies += [pltpu.make_async_remote_copy(src_ref=s, dst_ref=d, send_sem=send_sems.at[k], recv_sem=recv_sems.at[k],
                                                device_id=dev, device_id_type=pl.DeviceIdType.MESH)
                   for k, (s, d, dev) in enumerate(remote)]
        for cp in copies:
            cp.start()
        for cp in copies:
            cp.wait()

    hbm = pl.BlockSpec(memory_space=pltpu.HBM)
    return pl.pallas_call(
        body, name=name, in_specs=[hbm] * n_in, out_specs=[hbm] * n_out, out_shape=out_shapes,
        scratch_shapes=[pltpu.SemaphoreType.DMA((n_remote,)), pltpu.SemaphoreType.DMA((n_remote,)),
                        pltpu.SemaphoreType.DMA((max(n_local, 1),))],
        input_output_aliases=aliases or {},
    )(*ins)


DMA_PIECE_BYTES = 1 << 18
DMA_MAX_PIECES = 8


def _row_pieces(shape, dtype):
    rows = shape[0]
    row_bytes = jnp.dtype(dtype).itemsize
    for d in shape[1:]:
        row_bytes *= d
    n = 1
    while n < DMA_MAX_PIECES and rows % (2 * n * 16) == 0 and rows * row_bytes // (2 * n) >= DMA_PIECE_BYTES:
        n *= 2
    return [(i * (rows // n), rows // n) for i in range(n)]


def _half_pieces(o, c):
    r2 = o.shape[1] // 2
    return [pl.ds(c * r2 + st, sz) for st, sz in _row_pieces((r2,) + o.shape[2:], o.dtype)]


def _n_half_pieces(arrays):
    return sum(len(_row_pieces((a.shape[1] // 2,) + a.shape[2:], a.dtype)) for a in arrays)


def _plan_gather_far(lands, x, y, c):
    me = 2 * x + y
    return [(o.at[me, rows], o.at[me, rows], dev) for dev, _ in _chip_peers(x, y, c) for o in lands
            for rows in _half_pieces(o, c)]


def _plan_gather_near(lands, x, y, c):
    return [(o.at[idx, rows], o.at[idx, rows], (x, y, 1 - c)) for _, idx in _chip_peers(x, y, c) for o in lands
            for rows in _half_pieces(o, c)]


def _gather_weights(lands, name):
    n = len(lands)
    n_far = (N_CHIPS - 1) * _n_half_pieces(lands)

    def body(*refs):
        outs = refs[n:2 * n]
        far_send, far_recv, near_send, near_recv = refs[2 * n:]
        x, y, c = lax.axis_index("x"), lax.axis_index("y"), lax.axis_index("c")
        mk = lambda plan, send, recv: [
            pltpu.make_async_remote_copy(src_ref=s, dst_ref=d, send_sem=send.at[k], recv_sem=recv.at[k], device_id=dev,
                                         device_id_type=pl.DeviceIdType.MESH)
            for k, (s, d, dev) in enumerate(plan(outs, x, y, c))]
        far, near = mk(_plan_gather_far, far_send, far_recv), mk(_plan_gather_near, near_send, near_recv)
        assert len(far) == n_far and len(near) == n_far
        for cp in far:
            cp.start()
        for k in range(n_far):
            far[k].wait_recv()
            near[k].start()
        for k in range(n_far):
            near[k].wait_recv()
        for cp in far + near:
            cp.wait_send()

    hbm = pl.BlockSpec(memory_space=pltpu.HBM)
    sems = pltpu.SemaphoreType.DMA((n_far,))
    return pl.pallas_call(
        body, name=name, in_specs=[hbm] * n, out_specs=[hbm] * n,
        out_shape=[jax.ShapeDtypeStruct(a.shape, a.dtype) for a in lands],
        scratch_shapes=[sems, sems, sems, sems], input_output_aliases={i: i for i in range(n)},
    )(*lands)


def _gather_all(v, name):
    def plan(ins, outs, x, y, c):
        (s,), (o,) = ins, outs
        me = 4 * x + 2 * y + c
        flip = lambda a, f: 1 - a if f else a
        remote = [(s, o.at[me], (flip(x, m & 4), flip(y, m & 2), flip(c, m & 1))) for m in range(1, 8)]
        return remote, [(s, o.at[me])]

    return _rdma_call([v], [jax.ShapeDtypeStruct((8,) + v.shape, v.dtype)], plan, 7, 1, name)[0]


def _reduce_pair(parts, name):
    def plan(ins, outs, x, y, c):
        return [(a.at[j, 1 - c, pl.ds(st, sz)], o.at[j, pl.ds(st, sz)], (x, y, 1 - c)) for a, o in zip(ins, outs)
                for j in range(N_CHIPS) for st, sz in _row_pieces(a.shape[2:], a.dtype)], []

    shapes = [jax.ShapeDtypeStruct((N_CHIPS,) + a.shape[2:], a.dtype) for a in parts]
    n_remote = N_CHIPS * sum(len(_row_pieces(a.shape[2:], a.dtype)) for a in parts)
    return _rdma_call(parts, shapes, plan, n_remote, 0, name)


def _plan_chips(ins, lands, x, y, c):
    me = 2 * x + y
    return [(a.at[idx, pl.ds(st, sz)], o.at[me, pl.ds(st, sz)], dev) for dev, idx in _chip_peers(x, y, c)
            for a, o in zip(ins, lands) for st, sz in _row_pieces(a.shape[1:], a.dtype)]


def _n_chips_copies(parts):
    return (N_CHIPS - 1) * sum(len(_row_pieces(a.shape[1:], a.dtype)) for a in parts)


def _reduce_chips(parts, name):
    shapes = [jax.ShapeDtypeStruct(a.shape, a.dtype) for a in parts]
    return _rdma_call(parts, shapes, lambda ins, outs, x, y, c: (_plan_chips(ins, outs, x, y, c), []),
                      _n_chips_copies(parts), 0, name)


def _gather_pair(halves, name):
    def plan(ins, outs, x, y, c):
        return [(o.at[c, pl.ds(st, sz)], o.at[c, pl.ds(st, sz)], (x, y, 1 - c)) for o in outs
                for st, sz in _row_pieces(o.shape[1:], o.dtype)], []

    shapes = [jax.ShapeDtypeStruct(a.shape, a.dtype) for a in halves]
    n_remote = sum(len(_row_pieces(a.shape[1:], a.dtype)) for a in halves)
    return _rdma_call(halves, shapes, plan, n_remote, 0, name, aliases={i: i for i in range(len(halves))})


def _split_start(ins, lands, plan, n_remote, name):
    n_buf = len(ins) + len(lands)

    def body(*refs):
        in_refs, land_refs = refs[:len(ins)], refs[len(ins):n_buf]
        send_sems, recv_sems, token = refs[n_buf], refs[n_buf + 1], refs[-1]
        x, y, c = lax.axis_index("x"), lax.axis_index("y"), lax.axis_index("c")
        remote = plan(in_refs, land_refs, x, y, c)
        assert len(remote) == n_remote, (name, len(remote))
        for k, (s, d, dev) in enumerate(remote):
            pltpu.make_async_remote_copy(src_ref=s, dst_ref=d, send_sem=send_sems.at[k], recv_sem=recv_sems.at[k],
                                         device_id=dev, device_id_type=pl.DeviceIdType.MESH).start()
        token[...] = jnp.zeros_like(token)

    hbm = pl.BlockSpec(memory_space=pltpu.HBM)
    sem = pl.BlockSpec(memory_space=pltpu.SEMAPHORE)
    bufs = list(ins) + list(lands)
    out = pl.pallas_call(
        body, name=name, in_specs=[hbm] * n_buf,
        out_specs=(sem, sem) + (hbm,) * n_buf + (pl.BlockSpec(memory_space=pltpu.VMEM),),
        out_shape=(pltpu.SemaphoreType.DMA((n_remote,)), pltpu.SemaphoreType.DMA((n_remote,)))
        + tuple(pltpu.HBM(a.shape, a.dtype) for a in bufs) + (jax.ShapeDtypeStruct((8, 128), F32),),
        input_output_aliases={i: 2 + i for i in range(n_buf)},
        compiler_params=pltpu.CompilerParams(has_side_effects=pltpu.SideEffectType.DATAFLOW_SIDE_EFFECTING),
    )(*[pltpu.with_memory_space_constraint(a, pltpu.HBM) for a in bufs])
    return dict(send=out[0], recv=out[1], ins=list(out[2:2 + len(ins)]), lands=list(out[2 + len(ins):2 + n_buf]),
                token=out[-1][0, 0], plan=plan, n_remote=n_remote)


def _split_wait(st, after, name):
    n_in, n_buf = len(st["ins"]), len(st["ins"]) + len(st["lands"])
    plan, n_remote = st["plan"], st["n_remote"]

    def body(*refs):
        in_refs, land_refs = refs[:n_in], refs[n_in:n_buf]
        send_sems, recv_sems = refs[n_buf], refs[n_buf + 1]
        x, y, c = lax.axis_index("x"), lax.axis_index("y"), lax.axis_index("c")
        for k, (s, d, dev) in enumerate(plan(in_refs, land_refs, x, y, c)):
            cp = pltpu.make_async_remote_copy(src_ref=s, dst_ref=d, send_sem=send_sems.at[k], recv_sem=recv_sems.at[k],
                                              device_id=dev, device_id_type=pl.DeviceIdType.MESH)
            cp.wait_send()
            cp.wait_recv()

    hbm = pl.BlockSpec(memory_space=pltpu.HBM)
    sem = pl.BlockSpec(memory_space=pltpu.SEMAPHORE)
    bufs = st["ins"] + st["lands"]
    out = pl.pallas_call(
        body, name=name, in_specs=[hbm] * n_buf + [sem, sem, pl.BlockSpec(memory_space=pl.ANY)],
        out_specs=[hbm] * n_buf, out_shape=[pltpu.HBM(a.shape, a.dtype) for a in bufs],
        input_output_aliases={i: i for i in range(n_buf)},
        compiler_params=pltpu.CompilerParams(has_side_effects=pltpu.SideEffectType.DATAFLOW_SIDE_EFFECTING),
    )(*bufs, st["send"], st["recv"], after)
    return list(out[:n_in]), list(out[n_in:])


def _pair_forward(lands, name):
    shapes = [jax.ShapeDtypeStruct(a.shape, a.dtype) for a in lands]
    return _rdma_call(lands, shapes, lambda ins, outs, x, y, c: (_plan_gather_near(outs, x, y, c), []),
                      (N_CHIPS - 1) * _n_half_pieces(lands), 0, name, aliases={i: i for i in range(len(lands))})


def _sum_block_rows(r, C):
    return _rows_tile(r, max(16, (1 << 18) // C // 16 * 16))


def _sum_pair(a, recv, cidx, name):
    nch, _, r, C = a.shape
    rb = _sum_block_rows(r, C)

    def body(c_ref, a_ref, r_ref, o_ref):
        o_ref[...] = (a_ref[...] + r_ref[...]).astype(BF16)

    blk = pl.BlockSpec((None, rb, C), lambda j, i, c: (j, i, 0))
    return pl.pallas_call(
        body, name=name,
        grid_spec=pltpu.PrefetchScalarGridSpec(
            num_scalar_prefetch=1, grid=(nch, r // rb),
            in_specs=[pl.BlockSpec((None, None, rb, C), lambda j, i, c: (j, c[0], i, 0)), blk], out_specs=blk),
        out_shape=jax.ShapeDtypeStruct((nch, r, C), BF16),
        compiler_params=_params("parallel", "parallel"),
    )(cidx, a, recv)


def _sum_chips(mine, recv, ids, name):
    nch, r, C = recv.shape
    rb = _sum_block_rows(r, C)

    def body(ids_ref, m_ref, *rest):
        r_refs, o_ref = rest[:nch], rest[nch]
        chip = ids_ref[1]
        own = m_ref[...].astype(F32)
        acc = jnp.where(chip == 0, own, r_refs[0][...].astype(F32))
        for q in range(1, nch):
            acc = acc + jnp.where(chip == q, own, r_refs[q][...].astype(F32))
        o_ref[...] = acc

    def slot(q):
        return pl.BlockSpec((None, rb, C), lambda i, ids: (jnp.where(ids[1] == q, (q + 1) % nch, q), i, 0))

    return pl.pallas_call(
        body, name=name,
        grid_spec=pltpu.PrefetchScalarGridSpec(
            num_scalar_prefetch=1, grid=(r // rb,),
            in_specs=[pl.BlockSpec((None, rb, C), lambda i, ids: (ids[1], i, 0))] + [slot(q) for q in range(nch)],
            out_specs=pl.BlockSpec((None, rb, C), lambda i, ids: (ids[0], i, 0))),
        out_shape=jax.ShapeDtypeStruct((N_CORES, r, C), F32),
        compiler_params=_params("parallel"),
    )(ids, mine, *([recv] * nch))


PACK_COLS = 1024
_SHARDED = ("ada_w", "w_in", "w_branch_a", "w_branch_b", "w_out", "ffn_w_up", "ffn_w_down")
_LAYER_KEYS = ("ada_w", "w_in", "w_a", "w_b", "w_o", "w_up", "w_down")
_SMALL = ("c_ctx", "ada_b", "norm1_w", "sgu_ln_w", "sgu_ln_b", "sgu_w", "sgu_b", "hgrn_lower_bounds", "hgrn_norm_w",
          "norm2_w", "ffn_conv_b", "final_norm_w")
_ORDER = ("c_ctx", "ada_w", "ada_b", "norm1_w", "w_in", "sgu_ln_w", "sgu_ln_b", "sgu_w", "sgu_b", "hgrn_lower_bounds",
          "hgrn_norm_w", "w_branch_a", "w_branch_b", "w_out", "norm2_w", "ffn_w_up", "ffn_conv_w", "ffn_conv_b",
          "ffn_w_down", "final_norm_w")


def _pad_to(v, n):
    return jnp.concatenate([v, jnp.zeros((n - v.shape[0],), v.dtype)]) if v.shape[0] < n else v


def _round_up(n, m):
    return (n + m - 1) // m * m


def _pack(arrays, n_pad):
    flat = jnp.concatenate([a.reshape(-1) for a in arrays])
    return _pad_to(flat, n_pad)


def _unpack(flat, like):
    out, off = [], 0
    for a in like:
        out.append(flat[off:off + a.size].reshape(a.shape))
        off += a.size
    return out


def kernel(x, c, ctx, c_ctx, ada_w, ada_b, norm1_w, w_in, sgu_ln_w, sgu_ln_b, sgu_w, sgu_b, hgrn_lower_bounds, hgrn_norm_w, w_branch_a, w_branch_b, w_out, norm2_w, ffn_w_up, ffn_conv_w, ffn_conv_b, ffn_w_down, final_norm_w, loss_target, m_c_ctx, m_ada_w, m_ada_b, m_norm1_w, m_w_in, m_sgu_ln_w, m_sgu_ln_b, m_sgu_w, m_sgu_b, m_hgrn_lower_bounds, m_hgrn_norm_w, m_w_branch_a, m_w_branch_b, m_w_out, m_norm2_w, m_ffn_w_up, m_ffn_conv_w, m_ffn_conv_b, m_ffn_w_down, m_final_norm_w, v_c_ctx, v_ada_w, v_ada_b, v_norm1_w, v_w_in, v_sgu_ln_w, v_sgu_ln_b, v_sgu_w, v_sgu_b, v_hgrn_lower_bounds, v_hgrn_norm_w, v_w_branch_a, v_w_branch_b, v_w_out, v_norm2_w, v_ffn_w_up, v_ffn_conv_w, v_ffn_conv_b, v_ffn_w_down, v_final_norm_w):
    w = dict(c_ctx=c_ctx, ada_w=ada_w, ada_b=ada_b, norm1_w=norm1_w, w_in=w_in, sgu_ln_w=sgu_ln_w, sgu_ln_b=sgu_ln_b,
             sgu_w=sgu_w, sgu_b=sgu_b, hgrn_lower_bounds=hgrn_lower_bounds, hgrn_norm_w=hgrn_norm_w, w_branch_a=w_branch_a,
             w_branch_b=w_branch_b, w_out=w_out, norm2_w=norm2_w, ffn_w_up=ffn_w_up, ffn_conv_w=ffn_conv_w,
             ffn_conv_b=ffn_conv_b, ffn_w_down=ffn_w_down, final_norm_w=final_norm_w)
    mom = dict(zip(_ORDER, (m_c_ctx, m_ada_w, m_ada_b, m_norm1_w, m_w_in, m_sgu_ln_w, m_sgu_ln_b, m_sgu_w, m_sgu_b,
                            m_hgrn_lower_bounds, m_hgrn_norm_w, m_w_branch_a, m_w_branch_b, m_w_out, m_norm2_w, m_ffn_w_up,
                            m_ffn_conv_w, m_ffn_conv_b, m_ffn_w_down, m_final_norm_w)))
    var = dict(zip(_ORDER, (v_c_ctx, v_ada_w, v_ada_b, v_norm1_w, v_w_in, v_sgu_ln_w, v_sgu_ln_b, v_sgu_w, v_sgu_b,
                            v_hgrn_lower_bounds, v_hgrn_norm_w, v_w_branch_a, v_w_branch_b, v_w_out, v_norm2_w, v_ffn_w_up,
                            v_ffn_conv_w, v_ffn_conv_b, v_ffn_w_down, v_final_norm_w)))
    depth, D = norm1_w.shape
    dff = ffn_conv_b.shape[1]
    ctx_rows, seq = ctx.shape[1], x.shape[1]

    assert depth == 2, "the lower-bound softmax is written for two layers"
    core = lax.axis_index("c")
    chip = 2 * lax.axis_index("x") + lax.axis_index("y")
    ids = jnp.stack([core, chip]).astype(jnp.int32)

    first, rest = _LAYER_KEYS[:2], _LAYER_KEYS[2:]
    shard = lambda l, k: w[_SHARDED[_LAYER_KEYS.index(k)]][l].astype(BF16)
    started, conv_full = {}, []

    def landing(s):
        return lax.dynamic_update_slice(lax.empty((N_CHIPS,) + s.shape, s.dtype), s[None], (chip,) + (0,) * s.ndim)

    def start_gather(l, keys, tag):
        lands = [landing(shard(l, k)) for k in keys]
        started[tag] = _split_start([], lands, lambda ins, lds, x, y, c: _plan_gather_far(lds, x, y, c),
                                    (N_CHIPS - 1) * _n_half_pieces(lands), f"gather_start_{tag}")
        return started[tag]["token"]

    def finish_gather(keys, tag, after):
        _, lands = _split_wait(started[tag], after, f"gather_wait_{tag}")
        return dict(zip(keys, _pair_forward(lands, f"gather_forward_{tag}")))

    def layer_weights(l, after):
        if l == 0:
            got = _gather_weights([landing(shard(0, k)) for k in first] + [landing(ffn_conv_w)], "gather_weights_first")
            conv_full.append(jnp.transpose(got[-1], (1, 2, 3, 0, 4)).reshape(depth, 9, dff))
            out = dict(zip(first, got), token=start_gather(0, rest, "rest_0"))
        else:
            out = dict(finish_gather(first, f"first_{l}", after), token=0.0)

        def late(after_late):
            more = finish_gather(rest, f"rest_{l}", after_late)
            more["late_token"] = 0.0
            if l + 1 < depth:
                more["late_token"] = start_gather(l + 1, first, f"first_{l + 1}") + start_gather(l + 1, rest, f"rest_{l + 1}")
            return more

        return dict(out, conv_w=conv_full[0][l], late=late)

    pending, pair_sums_of = [], {}

    def pair_reduce(tag, gs):
        parts = [g.reshape(N_CHIPS, N_CORES, g.size // (N_CHIPS * N_CORES * g.shape[-1]), g.shape[-1]) for g in gs]
        other = _reduce_pair(parts, f"reduce_pair_{tag}")
        return [_sum_pair(a, o, ids, f"sum_pair_{tag}_{i}") for i, (a, o) in enumerate(zip(parts, other))]

    def on_layer_grads(l, stage, gs):
        keys = [k for k in gs if k != "w_in"] if stage == "early" else ["w_in"]
        if l == 0 and stage == "late":
            pair_sums_of["last"] = (keys, gs)
            return 0.0
        tag = f"{stage}_{l}"
        sums = pair_reduce(tag, [gs[k] for k in keys])
        lands = [lax.empty(s.shape, s.dtype) for s in sums]
        st = _split_start(sums, lands, _plan_chips, _n_chips_copies(sums), f"reduce_chips_start_{tag}")
        pending.append((tag, l, keys, st))
        return st["token"]

    W = dict(ada_b=ada_b, norm1_w=norm1_w, sgu_ln_w=sgu_ln_w, sgu_ln_b=sgu_ln_b, sgu_w=sgu_w.astype(BF16),
             sgu_bt=jnp.swapaxes(sgu_b, 1, 2), hlb=hgrn_lower_bounds, hnw=hgrn_norm_w, norm2_w=norm2_w, conv_b=ffn_conv_b,
             final_norm_w=final_norm_w)
    xs = jnp.concatenate([ctx[0], x[0]], axis=0)
    cv = jnp.concatenate([c_ctx[None, :], c, jnp.zeros((14, D), F32)], axis=0)
    loss_local, dxs, G, sa = _local_step(xs, cv, loss_target[0], W, layer_weights, on_layer_grads, ctx_rows)
    loss = lax.psum(loss_local, ("x", "y", "c"))
    grad_x = dxs[ctx_rows:][None]

    pad8 = lambda a: jnp.pad(a, ((0, 8 - a.shape[0]), (0, 0)))
    fact = jnp.concatenate([pad8(sa[1:2].astype(F32))] + [pad8(G["dmod"][l][1].reshape(N_MOD, D)) for l in range(depth)]
                           + [pad8(G["dmod"][l][0].reshape(N_MOD, D)) for l in range(depth)], axis=0)
    facts = _gather_all(fact, "gather_mod_factors")
    lhs = jnp.concatenate([facts[:, 0].astype(BF16), jnp.broadcast_to(sa[0:1], (8, D))], axis=0)
    ada_cols = N_MOD * D // N_CHIPS
    g_ada = []
    for l in range(depth):
        lo_x, lo_c = 8 * (1 + l), 8 * (1 + depth + l)
        rhs = jnp.concatenate([facts[:, lo_x:lo_x + N_MOD].reshape(8, N_MOD * D),
                               facts[:, lo_c:lo_c + N_MOD].reshape(8, N_MOD * D)], axis=0)
        rhs = lax.dynamic_slice_in_dim(rhs, chip * ada_cols, ada_cols, axis=1).astype(BF16)
        g_ada.append(_mm_tn(lhs, rhs, F32, f"dw_ada_{l}"))

    dh = G["hlb1"][depth - 1]
    small_like = [w[k] for k in _SMALL] + [jnp.zeros((depth, 9, dff), F32)]
    small = [G["c_ctx"], jnp.stack(G["ada_b"]), jnp.stack(G["norm1_w"]), jnp.stack(G["sgu_ln_w"]), jnp.stack(G["sgu_ln_b"]),
             jnp.stack(G["sgu_w"]), jnp.stack(G["sgu_b"]), jnp.stack([-dh, dh]), jnp.stack(G["hnw"]), jnp.stack(G["norm2_w"]),
             jnp.stack(G["conv_b"]), G["final_norm_w"], jnp.stack(G["conv_w"])]
    n_small = sum(a.size for a in small)
    n_small_pad = _round_up(n_small, N_CORES * 16 * PACK_COLS)
    small_rows = n_small_pad // (N_CORES * PACK_COLS)
    small_rep = jnp.broadcast_to(_pack(small, n_small_pad).reshape(1, N_CORES, small_rows, PACK_COLS),
                                 (N_CHIPS, N_CORES, small_rows, PACK_COLS))
    last_keys, last_gs = pair_sums_of["last"]
    last_sums = pair_reduce("last", [last_gs[k] for k in last_keys] + [small_rep])
    last_recv = _reduce_chips(last_sums, "reduce_chips_last")

    halves, where = [], {}
    for tag, l, keys, st in pending:
        sums, recv = _split_wait(st, dxs, f"reduce_chips_wait_{tag}")
        for i, k in enumerate(keys):
            where[(l, k)] = len(halves)
            halves.append(_sum_chips(sums[i], recv[i], ids, f"sum_chips_{tag}_{i}"))
    for i, k in enumerate(list(last_keys) + ["small"]):
        where[(0, k)] = len(halves)
        halves.append(_sum_chips(last_sums[i], last_recv[i], ids, f"sum_chips_last_{i}"))
    reduced = _gather_pair(halves, "gather_pair")

    g_small = _unpack(reduced[where[(0, "small")]].reshape(-1), small_like)
    grads = dict(zip(_SMALL, g_small[:-1]))
    g_conv = lax.dynamic_slice_in_dim(g_small[-1].reshape(depth, 3, 3, dff), chip * (dff // N_CHIPS), dff // N_CHIPS, axis=3)

    delta, new_m, new_v = {}, {}, {}
    for i, k in enumerate(_SHARDED):
        shp = w[k].shape
        gs = g_ada if i == 0 else [reduced[where[(l, _LAYER_KEYS[i])]].reshape(shp[1:]) for l in range(depth)]
        grads[k], delta[k], new_m[k], new_v[k] = _adamw(w[k], gs, mom[k], var[k], f"adamw_{k}")
    packed = _SMALL + ("ffn_conv_w",)
    n_pad = _round_up(sum(w[k].size for k in packed), 16 * PACK_COLS)
    pack = lambda t: _pack([t[k] for k in packed], n_pad).reshape(1, -1, PACK_COLS)
    grads["ffn_conv_w"] = g_conv
    _, d, nm, nv = _adamw(pack(w), [pack(grads)[0]], pack(mom), pack(var), "adamw_packed")
    like = [w[k] for k in packed]
    for src, dst in ((d, delta), (nm, new_m), (nv, new_v)):
        dst.update(zip(packed, _unpack(src.reshape(-1), like)))

    return (loss, grad_x, *[grads[k] for k in _ORDER], *[delta[k] for k in _ORDER], *[new_m[k] for k in _ORDER],
            *[new_v[k] for k in _ORDER])
```

```python
import functools

import jax
import jax.numpy as jnp
from jax import lax
from jax.experimental import pallas as pl
from jax.experimental.pallas import tpu as pltpu

F32 = jnp.float32
BF16 = jnp.bfloat16

GRID_W = 64
HG_CHUNK = 64
SGU_CHUNK = 128
HEAD = 128
TB = 256
N_MOD = 6
RMS_EPS = 1e-6
LN_EPS = 1e-5
VMEM_LIMIT = 48 * 1024 * 1024
N_CHIPS = 4
N_CORES = 2

ADAM_LR = 0.001
ADAM_B1 = 0.9
ADAM_B2 = 0.999
ADAM_EPS = 1e-08
ADAM_WD = 0.01
ADAM_STEP = 10

_GELU_C = 0.7978845608028654
_GELU_A = 0.044715


def _sigmoid(x):
    return 1.0 / (1.0 + jnp.exp(-x))


def _silu(x):
    return x * _sigmoid(x)


def _dsilu(x):
    s = _sigmoid(x)
    return s * (1.0 + x * (1.0 - s))


def _gelu(x):
    return 0.5 * x * (1.0 + jnp.tanh(_GELU_C * (x + _GELU_A * x * x * x)))


def _dgelu(x):
    t = jnp.tanh(_GELU_C * (x + _GELU_A * x * x * x))
    return 0.5 * (1.0 + t) + 0.5 * x * (1.0 - t * t) * _GELU_C * (1.0 + 3.0 * _GELU_A * x * x)


def _dot(a, b, ca, cb):
    return lax.dot_general(a, b, (((ca,), (cb,)), ((), ())), preferred_element_type=F32)


def _nn(a, b):
    return _dot(a, b, 1, 0)


def _nt(a, b):
    return _dot(a, b, 1, 1)


def _tn(a, b):
    return _dot(a, b, 0, 0)


def _params(*sem, vmem=VMEM_LIMIT):
    return pltpu.CompilerParams(dimension_semantics=sem if sem else None, vmem_limit_bytes=vmem)


def _stream_of(i, ctx_blocks):
    return (i >= ctx_blocks).astype(jnp.int32)


def _mm(a, b, mode, tm, tn, tk, out_dtype, name, add=None, b_chips=False, out_chips=False):
    if not b_chips:
        bshape = b.shape
    else:
        bshape = (b.shape[1], N_CHIPS * b.shape[2])
    if mode == "nn":
        (M, K), (K2, N) = a.shape, bshape
    elif mode == "nt":
        (M, K), (N, K2) = a.shape, bshape
    else:
        (K, M), (K2, N) = a.shape, bshape
    assert K == K2 and M % tm == 0 and N % tn == 0 and K % tk == 0, (name, a.shape, b.shape, tm, tn, tk)
    nk = K // tk
    if mode == "tn":
        a_spec = pl.BlockSpec((tk, tm), lambda j, i, k: (k, i))
    else:
        a_spec = pl.BlockSpec((tm, tk), lambda j, i, k: (i, k))
    if not b_chips:
        if mode == "nt":
            b_spec = pl.BlockSpec((tn, tk), lambda j, i, k: (j, k))
        else:
            b_spec = pl.BlockSpec((tk, tn), lambda j, i, k: (k, j))
    else:
        cols = b.shape[2]
        if mode == "nn":
            per = cols // tn
            assert cols % tn == 0
            b_spec = pl.BlockSpec((None, tk, tn), lambda j, i, k: (j // per, k, j % per))
        else:
            per = cols // tk
            assert mode == "nt" and cols % tk == 0
            b_spec = pl.BlockSpec((None, tn, tk), lambda j, i, k: (k // per, j, k % per))
    if out_chips:
        per_o = (N // N_CHIPS) // tn
        assert (N // N_CHIPS) % tn == 0 and add is None
        o_spec = pl.BlockSpec((None, tm, tn), lambda j, i, k: (j // per_o, i, j % per_o))
        o_shape = (N_CHIPS, M, N // N_CHIPS)
    else:
        o_spec = pl.BlockSpec((tm, tn), lambda j, i, k: (i, j))
        o_shape = (M, N)
    ca, cb = {"nn": (1, 0), "nt": (1, 1), "tn": (0, 0)}[mode]

    def body(a_ref, b_ref, *rest):
        if add is None:
            o_ref, acc = rest
        else:
            add_ref, o_ref, acc = rest
        k = pl.program_id(2)

        @pl.when(k == 0)
        def _():
            acc[...] = jnp.zeros_like(acc)

        acc[...] += _dot(a_ref[...], b_ref[...], ca, cb)

        @pl.when(k == nk - 1)
        def _():
            r = acc[...]
            if add is not None:
                r = r + add_ref[...]
            o_ref[...] = r.astype(out_dtype)

    ins = [a, b] + ([] if add is None else [add])
    specs = [a_spec, b_spec] + ([] if add is None else [o_spec])
    return pl.pallas_call(
        body, name=name, grid=(N // tn, M // tm, nk), in_specs=specs, out_specs=o_spec,
        out_shape=jax.ShapeDtypeStruct(o_shape, out_dtype),
        scratch_shapes=[pltpu.VMEM((tm, tn), F32)],
        compiler_params=_params("parallel", "parallel", "arbitrary"),
    )(*ins)


def _tile(n, pref):
    if n <= pref:
        return n
    best = None
    for t in range(128, pref + 1, 128):
        if n % t == 0:
            best = t
    assert best is not None, (n, pref)
    return best


def _rows_tile(n, pref):
    if n <= pref:
        return n
    best = None
    for t in range(16, pref + 1, 16):
        if n % t == 0:
            best = t
    assert best is not None, (n, pref)
    return best


def _mm_nn_w(a, wg, out_dtype, name):
    M, K = a.shape
    return _mm(a, wg, "nn", _rows_tile(M, 1088), _tile(wg.shape[2], 1536), _tile(K, 1536), out_dtype, name, b_chips=True)


def _mm_nt_w(a, wg, out_dtype, name):
    M, K = a.shape
    return _mm(a, wg, "nt", _rows_tile(M, 1088), _tile(wg.shape[1], 1024), _tile(wg.shape[2], 1536), out_dtype, name,
               b_chips=True)


def _mm_tn(a, b, out_dtype, name, out_chips=False):
    K, M = a.shape
    N = b.shape[1]
    ncol = N // N_CHIPS if out_chips else N
    tm, tn = _tile(M, 1408), _tile(ncol, 1408)
    if tm * tn > 1408 * 1152:
        tn = _tile(ncol, 1152)
    return _mm(a, b, "tn", tm, tn, _rows_tile(K, 2176), out_dtype, name, out_chips=out_chips)


def _mod_fwd(cv, wg, b, name):
    R, D = cv.shape
    tn = wg.shape[2]
    N = N_CHIPS * tn

    def body(cv_ref, w_ref, b_ref, mod_ref, sa_ref):
        sa = _silu(cv_ref[...]).astype(BF16)
        sa_ref[...] = sa
        mod_ref[...] = _nn(sa, w_ref[...]) + b_ref[...]

    return pl.pallas_call(
        body, name=name, grid=(N_CHIPS,),
        in_specs=[pl.BlockSpec((R, D), lambda j: (0, 0)), pl.BlockSpec((None, D, tn), lambda j: (j, 0, 0)),
                  pl.BlockSpec((1, tn), lambda j: (0, j))],
        out_specs=[pl.BlockSpec((R, tn), lambda j: (0, j)), pl.BlockSpec((R, D), lambda j: (0, 0))],
        out_shape=[jax.ShapeDtypeStruct((R, N), F32), jax.ShapeDtypeStruct((R, D), BF16)],
        compiler_params=_params("arbitrary"),
    )(cv, wg, b)


def _cvec_bwd(dmod, wg, cv, name):
    R, N = dmod.shape
    D = wg.shape[1]
    tk = wg.shape[2]
    nk = N_CHIPS

    def body(dm_ref, w_ref, cv_ref, o_ref):
        k = pl.program_id(0)

        @pl.when(k == 0)
        def _():
            o_ref[...] = jnp.zeros_like(o_ref)

        o_ref[...] += _nt(dm_ref[...].astype(BF16), w_ref[...])

        @pl.when(k == nk - 1)
        def _():
            o_ref[...] = o_ref[...] * _dsilu(cv_ref[...])

    return pl.pallas_call(
        body, name=name, grid=(nk,),
        in_specs=[pl.BlockSpec((R, tk), lambda k: (0, k)), pl.BlockSpec((None, D, tk), lambda k: (k, 0, 0)),
                  pl.BlockSpec((R, D), lambda k: (0, 0))],
        out_specs=pl.BlockSpec((R, D), lambda k: (0, 0)),
        out_shape=jax.ShapeDtypeStruct((R, D), F32),
        compiler_params=_params("arbitrary"),
    )(dmod, wg, cv)


def _norm_mod(x, nw, mod, which, ctx_rows, name):
    T, D = x.shape
    cb = ctx_rows // TB

    def body(x_ref, nw_ref, mod_ref, h_ref):
        xv = x_ref[...]
        r = lax.rsqrt(jnp.mean(xv * xv, axis=-1, keepdims=True) + RMS_EPS)
        y = xv * r * nw_ref[...]
        sh = mod_ref[which:which + 1, :]
        sc = mod_ref[which + 1:which + 2, :]
        h_ref[...] = (y * (1.0 + sc) + sh).astype(BF16)

    return pl.pallas_call(
        body, name=name, grid=(T // TB,),
        in_specs=[pl.BlockSpec((TB, D), lambda i: (i, 0)), pl.BlockSpec((1, D), lambda i: (0, 0)),
                  pl.BlockSpec((None, N_MOD, D), lambda i: (_stream_of(i, cb), 0, 0))],
        out_specs=pl.BlockSpec((TB, D), lambda i: (i, 0)),
        out_shape=jax.ShapeDtypeStruct((T, D), BF16),
        compiler_params=_params("parallel"),
    )(x, nw, mod)


def _norm_mod_bwd(dh, x, dres, nw, mod, which, ctx_rows, name):
    T, D = x.shape
    cb = ctx_rows // TB

    def body(dh_ref, x_ref, dres_ref, nw_ref, mod_ref, dx_ref, dm_ref, dnw_ref):
        i = pl.program_id(0)

        @pl.when(i == 0)
        def _():
            dnw_ref[...] = jnp.zeros_like(dnw_ref)

        @pl.when((i == 0) | (i == cb))
        def _():
            dm_ref[...] = jnp.zeros_like(dm_ref)

        xv = x_ref[...]
        dh = dh_ref[...]
        r = lax.rsqrt(jnp.mean(xv * xv, axis=-1, keepdims=True) + RMS_EPS)
        xh = xv * r
        nwv = nw_ref[...]
        sc = mod_ref[which + 1:which + 2, :]
        y = xh * nwv
        dm_ref[0:1, :] += jnp.sum(dh, axis=0, keepdims=True)
        dm_ref[1:2, :] += jnp.sum(dh * y, axis=0, keepdims=True)
        dy = dh * (1.0 + sc)
        dnw_ref[...] += jnp.sum(dy * xh, axis=0, keepdims=True)
        dxh = dy * nwv
        dx_ref[...] = dres_ref[...] + r * (dxh - xh * jnp.mean(dxh * xh, axis=-1, keepdims=True))

    return pl.pallas_call(
        body, name=name, grid=(T // TB,),
        in_specs=[pl.BlockSpec((TB, D), lambda i: (i, 0)), pl.BlockSpec((TB, D), lambda i: (i, 0)),
                  pl.BlockSpec((TB, D), lambda i: (i, 0)), pl.BlockSpec((1, D), lambda i: (0, 0)),
                  pl.BlockSpec((None, N_MOD, D), lambda i: (_stream_of(i, cb), 0, 0))],
        out_specs=[pl.BlockSpec((TB, D), lambda i: (i, 0)),
                   pl.BlockSpec((None, 2, D), lambda i: (_stream_of(i, cb), 0, 0)),
                   pl.BlockSpec((1, D), lambda i: (0, 0))],
        out_shape=[jax.ShapeDtypeStruct((T, D), F32), jax.ShapeDtypeStruct((2, 2, D), F32),
                   jax.ShapeDtypeStruct((1, D), F32)],
        compiler_params=_params("arbitrary"),
    )(dh, x, dres, nw, mod)


def _scan_chunk(n, rev, n_ctx, n_all):
    if not rev:
        return n
    return jnp.where(n < n_ctx, n_ctx - 1 - n, n_all - 1 + n_ctx - n)


def _cumsum_rows(x, rev):
    rows = x.shape[0]
    row = lax.broadcasted_iota(jnp.int32, (rows, 1), 0)
    s = 1
    while s < rows:
        if not rev:
            x = x + jnp.where(row >= s, pltpu.roll(x, s, 0), 0.0)
        else:
            x = x + jnp.where(row < rows - s, pltpu.roll(x, rows - s, 0), 0.0)
        s *= 2
    return x


def _lower_bound(hlb_ref, layer):
    h = hlb_ref[...]
    if layer == 0:
        return jnp.zeros_like(h[0:1, :])
    return _sigmoid(h[1:2, :] - h[0:1, :])


def _hgrn_gates(q_ref, f_ref, hlb_ref, layer, rev):
    lb = _lower_bound(hlb_ref, layer)
    z = f_ref[...]
    sig = _sigmoid(z)
    fg = lb + (1.0 - lb) * sig
    kk = (1.0 - lb) * (1.0 - sig)
    g = jnp.log(fg)
    b = _cumsum_rows(g, rev)
    bt = jnp.sum(g, axis=0, keepdims=True)
    mid = HG_CHUNK // 2
    r = b[mid:mid + 1, :] if rev else b[mid - 1:mid, :]
    qh = _silu(q_ref[...])
    return lb, sig, fg, kk, b, bt, r, qh


def _tri_mask(rev):
    t = lax.broadcasted_iota(jnp.int32, (HG_CHUNK, HG_CHUNK), 0)
    s = lax.broadcasted_iota(jnp.int32, (HG_CHUNK, HG_CHUNK), 1)
    return (s >= t) if rev else (s <= t)


def _hgrn_fwd(parts, hlb, layer, rev, ctx_rows, name, o_add=None):
    T = parts.shape[0]
    D = hlb.shape[1] // 2
    nh = D // HEAD
    n_all, n_ctx = T // HG_CHUNK, ctx_rows // HG_CHUNK
    chunk = functools.partial(_scan_chunk, rev=rev, n_ctx=n_ctx, n_all=n_all)
    fcol = 2 if rev else 1

    def body(q_ref, f_ref, i_ref, hlb_ref, *rest):
        if o_add is None:
            o_ref, st_ref, s_scr = rest
        else:
            oa_ref, o_ref, st_ref, s_scr = rest
        n = pl.program_id(0)

        @pl.when(n == 0)
        def _():
            s_scr[...] = jnp.zeros_like(s_scr)

        lb, sig, fg, kk, b, bt, r, qh = _hgrn_gates(q_ref, f_ref, hlb_ref, layer, rev)
        qr = (qh * jnp.exp(b - r)).astype(BF16)
        kr = (kk * jnp.exp(r - b)).astype(BF16)
        qe = (qh * jnp.exp(b)).astype(BF16)
        ke = (kk * jnp.exp(bt - b)).astype(BF16)
        dec = jnp.exp(bt)
        v = i_ref[...].astype(BF16)
        mask = _tri_mask(rev)
        for h in range(nh):
            sl = slice(h * HEAD, (h + 1) * HEAD)
            st = s_scr[h]
            st_ref[h] = st
            a = jnp.where(mask, _nt(qr[:, sl], kr[:, sl]), 0.0).astype(BF16)
            o = _nn(a, v[:, sl]) + _nt(qe[:, sl], st.astype(BF16))
            if o_add is not None:
                o = o + oa_ref[:, sl]
            o_ref[:, sl] = o
            s_scr[h] = st * dec[:, sl] + _tn(v[:, sl], ke[:, sl])

    cspec = lambda col: pl.BlockSpec((HG_CHUNK, D), lambda n: (chunk(n), col))
    ins = [parts, parts, parts, hlb]
    specs = [cspec(0), cspec(fcol), cspec(3), pl.BlockSpec((2, D), lambda n: (0, 1 if rev else 0))]
    if o_add is not None:
        ins.append(o_add)
        specs.append(cspec(0))
    return pl.pallas_call(
        body, name=name, grid=(n_all,), in_specs=specs,
        out_specs=[cspec(0), pl.BlockSpec((None, nh, HEAD, HEAD), lambda n: (n, 0, 0, 0))],
        out_shape=[jax.ShapeDtypeStruct((T, D), F32), jax.ShapeDtypeStruct((n_all, nh, HEAD, HEAD), F32)],
        scratch_shapes=[pltpu.VMEM((nh, HEAD, HEAD), F32)],
        compiler_params=_params("arbitrary"),
    )(*ins)


def _hgrn_bwd(parts, hlb, do, states, layer, rev, ctx_rows, name, other=None, dparts=None):
    T = parts.shape[0]
    D = hlb.shape[1] // 2
    nh = D // HEAD
    n_all, n_ctx = T // HG_CHUNK, ctx_rows // HG_CHUNK
    step = lambda m: n_all - 1 - m
    chunk = lambda m: _scan_chunk(step(m), rev, n_ctx, n_all)
    fcol = 2 if rev else 1
    has_add = other is not None
    assert not has_add or rev

    def body(q_ref, f_ref, i_ref, hlb_ref, do_ref, st_ref, *rest):
        if has_add:
            dqa_ref, dza_ref, dia_ref, _, out_ref, dlb_ref, ds_scr = rest
            dq_ref, dz_ref, di_ref = out_ref.at[:, 0:D], out_ref.at[:, 2 * D:3 * D], out_ref.at[:, 3 * D:4 * D]
            out_ref[:, D:2 * D] = dza_ref[...]
        else:
            dq_ref, dz_ref, di_ref, dlb_ref, ds_scr = rest
        m = pl.program_id(0)

        @pl.when(m == 0)
        def _():
            ds_scr[...] = jnp.zeros_like(ds_scr)
            dlb_ref[...] = jnp.zeros_like(dlb_ref)

        lb, sig, fg, kk, b, bt, r, qh = _hgrn_gates(q_ref, f_ref, hlb_ref, layer, rev)
        e_qr = jnp.exp(b - r)
        e_kr = jnp.exp(r - b)
        e_b = jnp.exp(b)
        e_ke = jnp.exp(bt - b)
        dec = jnp.exp(bt)
        qr = (qh * e_qr).astype(BF16)
        kr = (kk * e_kr).astype(BF16)
        qe = (qh * e_b).astype(BF16)
        ke = (kk * e_ke).astype(BF16)
        vf = i_ref[...]
        v = vf.astype(BF16)
        dov = do_ref[...].astype(BF16)
        mask = _tri_mask(rev)
        dq_parts, dk_parts, dki_parts, dv_parts, dbt_parts = [], [], [], [], []
        for h in range(nh):
            sl = slice(h * HEAD, (h + 1) * HEAD)
            st = st_ref[h]
            stb = st.astype(BF16)
            dst = ds_scr[h]
            dstb = dst.astype(BF16)
            a = jnp.where(mask, _nt(qr[:, sl], kr[:, sl]), 0.0).astype(BF16)
            da = jnp.where(mask, _nt(dov[:, sl], v[:, sl]), 0.0).astype(BF16)
            dv_parts.append(_tn(a, dov[:, sl]) + _nt(ke[:, sl], dstb))
            dq_h = _nn(da, kr[:, sl]) * e_qr[:, sl] + _nn(dov[:, sl], stb) * e_b[:, sl]
            dk_inter = _nn(v[:, sl], dstb) * e_ke[:, sl]
            dk_h = _tn(da, qr[:, sl]) * e_kr[:, sl] + dk_inter
            dq_parts.append(dq_h)
            dk_parts.append(dk_h)
            dki_parts.append(dk_inter)
            dbt_parts.append(dec[:, sl] * jnp.sum(st * dst, axis=0, keepdims=True))
            ds_scr[h] = dst * dec[:, sl] + _tn(dov[:, sl], qe[:, sl])
        dq = jnp.concatenate(dq_parts, axis=1)
        dk = jnp.concatenate(dk_parts, axis=1)
        dki = jnp.concatenate(dki_parts, axis=1)
        dv = jnp.concatenate(dv_parts, axis=1)
        dbt = jnp.concatenate(dbt_parts, axis=1) + jnp.sum(kk * dki, axis=0, keepdims=True)
        db = qh * dq - kk * dk
        dg = _cumsum_rows(db, not rev) + dbt
        df = dg / fg - dk
        dz_ref[...] = (df * (1.0 - lb) * sig * (1.0 - sig)).astype(BF16)
        dlb_ref[...] += jnp.sum(df * (1.0 - sig), axis=0, keepdims=True)
        dqr = dq * _dsilu(q_ref[...])
        if has_add:
            dqr = dqr + dqa_ref[...]
            dv = dv + dia_ref[...]
        dq_ref[...] = dqr.astype(dq_ref.dtype)
        di_ref[...] = dv.astype(di_ref.dtype)

        @pl.when(m == n_all - 1)
        def _():
            if layer == 0:
                dlb_ref[...] = jnp.zeros_like(dlb_ref)
            else:
                dlb_ref[...] = dlb_ref[...] * lb * (1.0 - lb)

    cspec = lambda col: pl.BlockSpec((HG_CHUNK, D), lambda m: (chunk(m), col))
    ins = [parts, parts, parts, hlb, do, states]
    specs = [cspec(0), cspec(fcol), cspec(3), pl.BlockSpec((2, D), lambda m: (0, 1 if rev else 0)), cspec(0),
             pl.BlockSpec((None, nh, HEAD, HEAD), lambda m: (step(m), 0, 0, 0))]
    dlb_spec = pl.BlockSpec((1, D), lambda m: (0, 0))
    dlb_shape = jax.ShapeDtypeStruct((1, D), F32)
    if has_add:
        return pl.pallas_call(
            body, name=name, grid=(n_all,),
            in_specs=specs + [cspec(0), cspec(0), cspec(0), pl.BlockSpec(memory_space=pl.ANY)],
            out_specs=[pl.BlockSpec((HG_CHUNK, 4 * D), lambda m: (chunk(m), 0)), dlb_spec],
            out_shape=[jax.ShapeDtypeStruct(dparts.shape, dparts.dtype), dlb_shape],
            scratch_shapes=[pltpu.VMEM((nh, HEAD, HEAD), F32)], input_output_aliases={len(ins) + 3: 0},
            compiler_params=_params("arbitrary"),
        )(*ins, *other, dparts)
    return pl.pallas_call(
        body, name=name, grid=(n_all,), in_specs=specs,
        out_specs=[cspec(0), cspec(0), cspec(0), dlb_spec],
        out_shape=[jax.ShapeDtypeStruct((T, D), F32), jax.ShapeDtypeStruct((T, D), BF16),
                   jax.ShapeDtypeStruct((T, D), F32), dlb_shape],
        scratch_shapes=[pltpu.VMEM((nh, HEAD, HEAD), F32)],
        compiler_params=_params("arbitrary"),
    )(*ins)


def _sgu_ln(v_ref, lnw_ref, lnb_ref):
    gv = _gelu(v_ref[...])
    mu = jnp.mean(gv, axis=-1, keepdims=True)
    xc = gv - mu
    rstd = lax.rsqrt(jnp.mean(xc * xc, axis=-1, keepdims=True) + LN_EPS)
    xh = xc * rstd
    return xh, rstd, xh * lnw_ref[...] + lnb_ref[...]


def _sgu_fwd(parts, lnw, lnb, w, bt, name):
    T = parts.shape[0]
    D = lnw.shape[1]
    G = D // HEAD

    def body(u_ref, v_ref, lnw_ref, lnb_ref, w_ref, bt_ref, ya_ref):
        gu = _gelu(u_ref[...])
        _, _, vn = _sgu_ln(v_ref, lnw_ref, lnb_ref)
        vnb = vn.astype(BF16)
        for g in range(G):
            sl = slice(g * HEAD, (g + 1) * HEAD)
            mixed = _nn(w_ref[g], vnb[:, sl]) + bt_ref[:, g:g + 1]
            ya_ref[:, sl] = (gu[:, sl] * mixed).astype(BF16)

    return pl.pallas_call(
        body, name=name, grid=(T // SGU_CHUNK,),
        in_specs=[pl.BlockSpec((SGU_CHUNK, D), lambda n: (n, 4)), pl.BlockSpec((SGU_CHUNK, D), lambda n: (n, 5)),
                  pl.BlockSpec((1, D), lambda n: (0, 0)), pl.BlockSpec((1, D), lambda n: (0, 0)),
                  pl.BlockSpec((G, SGU_CHUNK, SGU_CHUNK), lambda n: (0, 0, 0)),
                  pl.BlockSpec((SGU_CHUNK, G), lambda n: (0, 0))],
        out_specs=pl.BlockSpec((SGU_CHUNK, D), lambda n: (n, 0)),
        out_shape=jax.ShapeDtypeStruct((T, D), BF16),
        compiler_params=_params("parallel"),
    )(parts, parts, lnw, lnb, w, bt)


def _sgu_bwd(parts, dya, lnw, lnb, w, bt, dparts, name):
    T = parts.shape[0]
    D = lnw.shape[1]
    G = D // HEAD

    def body(u_ref, v_ref, dya_ref, lnw_ref, lnb_ref, w_ref, bt_ref, dparts_in,
             duv_ref, dw_ref, dbt_ref, dlnw_ref, dlnb_ref, dvn_scr):
        du_ref = duv_ref.at[:, 0:D]
        dv_ref = duv_ref.at[:, D:2 * D]
        n = pl.program_id(0)

        @pl.when(n == 0)
        def _():
            dw_ref[...] = jnp.zeros_like(dw_ref)
            dbt_ref[...] = jnp.zeros_like(dbt_ref)
            dlnw_ref[...] = jnp.zeros_like(dlnw_ref)
            dlnb_ref[...] = jnp.zeros_like(dlnb_ref)

        u = u_ref[...]
        gu = _gelu(u)
        xh, rstd, vn = _sgu_ln(v_ref, lnw_ref, lnb_ref)
        vnb = vn.astype(BF16)
        dya = dya_ref[...]
        lane = lax.broadcasted_iota(jnp.int32, (SGU_CHUNK, G), 1)
        dbt = jnp.zeros((SGU_CHUNK, G), F32)
        for g in range(G):
            sl = slice(g * HEAD, (g + 1) * HEAD)
            wg = w_ref[g]
            mixed = _nn(wg, vnb[:, sl]) + bt_ref[:, g:g + 1]
            dmix = dya[:, sl] * gu[:, sl]
            du_ref[:, sl] = (dya[:, sl] * mixed * _dgelu(u[:, sl])).astype(BF16)
            dmb = dmix.astype(BF16)
            dvn_scr[:, sl] = _tn(wg, dmb)
            dw_ref[g] += _nt(dmb, vnb[:, sl])
            dbt = dbt + jnp.where(lane == g, jnp.sum(dmix, axis=1, keepdims=True), 0.0)
        dbt_ref[...] += dbt
        dvn = dvn_scr[...]
        dlnw_ref[...] += jnp.sum(dvn * xh, axis=0, keepdims=True)
        dlnb_ref[...] += jnp.sum(dvn, axis=0, keepdims=True)
        dxh = dvn * lnw_ref[...]
        dgv = rstd * (dxh - jnp.mean(dxh, axis=-1, keepdims=True) - xh * jnp.mean(dxh * xh, axis=-1, keepdims=True))
        dv_ref[...] = (dgv * _dgelu(v_ref[...])).astype(BF16)

    row = lambda col: pl.BlockSpec((SGU_CHUNK, D), lambda n: (n, col))
    vec = pl.BlockSpec((1, D), lambda n: (0, 0))
    wsp = pl.BlockSpec((G, SGU_CHUNK, SGU_CHUNK), lambda n: (0, 0, 0))
    bsp = pl.BlockSpec((SGU_CHUNK, G), lambda n: (0, 0))
    return pl.pallas_call(
        body, name=name, grid=(T // SGU_CHUNK,),
        in_specs=[row(4), row(5), row(0), vec, vec, wsp, bsp, pl.BlockSpec(memory_space=pl.ANY)],
        out_specs=[pl.BlockSpec((SGU_CHUNK, 2 * D), lambda n: (n, 2)), wsp, bsp, vec, vec],
        out_shape=[jax.ShapeDtypeStruct(dparts.shape, dparts.dtype),
                   jax.ShapeDtypeStruct((G, SGU_CHUNK, SGU_CHUNK), F32), jax.ShapeDtypeStruct((SGU_CHUNK, G), F32),
                   jax.ShapeDtypeStruct((1, D), F32), jax.ShapeDtypeStruct((1, D), F32)],
        scratch_shapes=[pltpu.VMEM((SGU_CHUNK, D), F32)], input_output_aliases={7: 0},
        compiler_params=_params("arbitrary"),
    )(parts, parts, dya, lnw, lnb, w, bt, dparts)


TBT = 256
VMEM_LIMIT_TOKEN_OUT = 58 * 1024 * 1024


def _rows_weight_spec(wg):
    return pl.BlockSpec(wg.shape, lambda i: (0, 0, 0))


def _full(w_ref):
    return w_ref[...].reshape(w_ref.shape[0] * w_ref.shape[1], w_ref.shape[2])


def _token_out_fwd(o, parts, ya, x, mod, hnw, wa, wb, wo, ctx_rows, name):
    T, D = x.shape
    nh = D // HEAD
    cb = ctx_rows // TBT

    def body(o_ref, og_ref, ga_ref, gb_ref, ya_ref, x_ref, mod_ref, hnw_ref, wa_ref, wb_ref, wo_ref,
             yb_ref, pa_ref, pb_ref, mg_ref, tmo_ref, xm_ref):
        ov = o_ref[...]
        so = _silu(og_ref[...])
        nw = hnw_ref[...]
        for h in range(nh):
            sl = slice(h * HEAD, (h + 1) * HEAD)
            seg = ov[:, sl]
            r = lax.rsqrt(jnp.mean(seg * seg, axis=-1, keepdims=True) + RMS_EPS)
            yb_ref[:, sl] = (seg * r * nw * so[:, sl]).astype(BF16)
        pa = _nn(ya_ref[...], _full(wa_ref))
        pb = _nn(yb_ref[...], _full(wb_ref))
        pa_ref[...] = pa
        pb_ref[...] = pb
        mg = (_sigmoid(ga_ref[...]) * pa + _sigmoid(gb_ref[...]) * pb).astype(BF16)
        mg_ref[...] = mg
        out = _nn(mg, _full(wo_ref))
        tmo_ref[...] = out
        xm_ref[...] = x_ref[...] + mod_ref[2:3, :] * out

    row = lambda col: pl.BlockSpec((TBT, D), lambda i: (i, col))
    wsp = _rows_weight_spec(wa)
    sd = lambda dt: jax.ShapeDtypeStruct((T, D), dt)
    return pl.pallas_call(
        body, name=name, grid=(T // TBT,),
        in_specs=[row(0), row(6), row(7), row(8), row(0), row(0),
                  pl.BlockSpec((None, N_MOD, D), lambda i: (_stream_of(i, cb), 0, 0)),
                  pl.BlockSpec((1, HEAD), lambda i: (0, 0)), wsp, wsp, wsp],
        out_specs=[row(0)] * 6,
        out_shape=[sd(BF16), sd(F32), sd(F32), sd(BF16), sd(F32), sd(F32)],
        compiler_params=_params("parallel", vmem=VMEM_LIMIT_TOKEN_OUT),
    )(o, parts, parts, parts, ya, x, mod, hnw, wa, wb, wo)


def _token_out_bwd(dx, tmo, pa, pb, o, parts, mod, hnw, wa, wb, wo, ctx_rows, name):
    T, D = dx.shape
    nh = D // HEAD
    cb = ctx_rows // TBT

    def body(dx_ref, tmo_ref, pa_ref, pb_ref, o_ref, og_ref, ga_ref, gb_ref, mod_ref, hnw_ref, wa_ref, wb_ref, wo_ref,
             dout_ref, dpa_ref, dpb_ref, dgate_ref, dya_ref, do_ref, dg1_ref, dhnw_ref):
        i = pl.program_id(0)

        @pl.when(i == 0)
        def _():
            dhnw_ref[...] = jnp.zeros_like(dhnw_ref)

        @pl.when((i == 0) | (i == cb))
        def _():
            dg1_ref[...] = jnp.zeros_like(dg1_ref)

        dxv = dx_ref[...]
        dg1_ref[...] += jnp.sum(dxv * tmo_ref[...], axis=0, keepdims=True)
        dout = (dxv * mod_ref[2:3, :]).astype(BF16)
        dout_ref[...] = dout
        dmg = _nt(dout, _full(wo_ref))
        sa = _sigmoid(ga_ref[...])
        sb = _sigmoid(gb_ref[...])
        dpa = (dmg * sa).astype(BF16)
        dpb = (dmg * sb).astype(BF16)
        dpa_ref[...] = dpa
        dpb_ref[...] = dpb
        dgate_ref[:, D:2 * D] = (dmg * pa_ref[...] * sa * (1.0 - sa)).astype(BF16)
        dgate_ref[:, 2 * D:3 * D] = (dmg * pb_ref[...] * sb * (1.0 - sb)).astype(BF16)
        dya_ref[...] = _nt(dpa, _full(wa_ref))
        dyb = _nt(dpb, _full(wb_ref))
        og = og_ref[...]
        so = _silu(og)
        dso = _dsilu(og)
        ov = o_ref[...]
        nw = hnw_ref[...]
        dnw = jnp.zeros((1, HEAD), F32)
        for h in range(nh):
            sl = slice(h * HEAD, (h + 1) * HEAD)
            seg = ov[:, sl]
            r = lax.rsqrt(jnp.mean(seg * seg, axis=-1, keepdims=True) + RMS_EPS)
            oh = seg * r
            dn = dyb[:, sl] * so[:, sl]
            dgate_ref[:, sl] = (dyb[:, sl] * oh * nw * dso[:, sl]).astype(BF16)
            dnw = dnw + jnp.sum(dn * oh, axis=0, keepdims=True)
            doh = dn * nw
            do_ref[:, sl] = r * (doh - oh * jnp.mean(doh * oh, axis=-1, keepdims=True))
        dhnw_ref[...] += dnw

    row = lambda col: pl.BlockSpec((TBT, D), lambda i: (i, col))
    wsp = _rows_weight_spec(wa)
    sd = lambda dt: jax.ShapeDtypeStruct((T, D), dt)
    return pl.pallas_call(
        body, name=name, grid=(T // TBT,),
        in_specs=[row(0), row(0), row(0), row(0), row(0), row(6), row(7), row(8),
                  pl.BlockSpec((None, N_MOD, D), lambda i: (_stream_of(i, cb), 0, 0)),
                  pl.BlockSpec((1, HEAD), lambda i: (0, 0)), wsp, wsp, wsp],
        out_specs=[row(0)] * 3 + [pl.BlockSpec((TBT, 3 * D), lambda i: (i, 2)), row(0), row(0),
                                  pl.BlockSpec((None, 1, D), lambda i: (_stream_of(i, cb), 0, 0)),
                                  pl.BlockSpec((1, HEAD), lambda i: (0, 0))],
        out_shape=[sd(BF16)] * 3 + [jax.ShapeDtypeStruct((T, 9 * D), BF16), sd(F32), sd(F32),
                                    jax.ShapeDtypeStruct((2, 1, D), F32), jax.ShapeDtypeStruct((1, HEAD), F32)],
        compiler_params=_params("arbitrary", vmem=VMEM_LIMIT_TOKEN_OUT),
    )(dx, tmo, pa, pb, o, parts, parts, parts, mod, hnw, wa, wb, wo)


def _conv_geometry(i, nb, cb):
    is_ctx = i < cb
    first = (i == 0) | (i == cb)
    last = (i == cb - 1) | (i == nb - 1)
    row = lax.broadcasted_iota(jnp.int32, (TB + 2 * GRID_W, 1), 0)
    w = row & (GRID_W - 1)
    left_ok = (w != 0) | is_ctx
    right_ok = (w != GRID_W - 1) | is_ctx
    return is_ctx, first, last, left_ok, right_ok


def _ext(p_ref, m_ref, n_ref, first, last):
    return jnp.concatenate([jnp.where(first, 0.0, p_ref[...]), m_ref[...], jnp.where(last, 0.0, n_ref[...])], axis=0)


def _shift_prev(e, ok):
    return jnp.where(ok, pltpu.roll(e, 1, 0), 0.0)


def _shift_next(e, ok):
    return jnp.where(ok, pltpu.roll(e, e.shape[0] - 1, 0), 0.0)


def _halo_specs(cbk, n64, coff=0):
    r = TB // GRID_W
    prev = pl.BlockSpec((GRID_W, cbk), lambda j, i: (jnp.maximum(r * i - 1, 0), j + coff))
    main = pl.BlockSpec((TB, cbk), lambda j, i: (i, j + coff))
    nxt = pl.BlockSpec((GRID_W, cbk), lambda j, i: (jnp.minimum(r * i + r, n64 - 1), j + coff))
    return [prev, main, nxt]


def _conv_cblock(dff):
    return _tile(dff, 1408)


def _conv_fwd(up, cw, cbias, ctx_rows, name):
    T, dff = up.shape[0], up.shape[1] // 2
    cbk = _conv_cblock(dff)
    nb, cb = T // TB, ctx_rows // TB
    nvb = dff // cbk

    def body(ap_ref, a_ref, an_ref, v_ref, cw_ref, cb_ref, ac_ref, act_ref):
        i = pl.program_id(1)
        is_ctx, first, last, lok, rok = _conv_geometry(i, nb, cb)
        e = _ext(ap_ref, a_ref, an_ref, first, last)
        el = _shift_prev(e, lok)
        er = _shift_next(e, rok)
        cwv = cw_ref[...]

        def comb(dr, lo):
            sl = slice(lo, lo + TB)
            return cwv[3 * dr:3 * dr + 1] * el[sl] + cwv[3 * dr + 1:3 * dr + 2] * e[sl] + cwv[3 * dr + 2:3 * dr + 3] * er[sl]

        out = comb(1, GRID_W) + jnp.where(is_ctx, 0.0, comb(0, 0) + comb(2, 2 * GRID_W))
        a_c = out + cb_ref[...]
        ac_ref[...] = a_c
        act_ref[...] = (_gelu(a_c) * v_ref[...]).astype(BF16)

    main = pl.BlockSpec((TB, cbk), lambda j, i: (i, j))
    return pl.pallas_call(
        body, name=name, grid=(dff // cbk, nb),
        in_specs=_halo_specs(cbk, T // GRID_W) + [pl.BlockSpec((TB, cbk), lambda j, i: (i, j + nvb)),
                                                 pl.BlockSpec((9, cbk), lambda j, i: (0, j)),
                                                 pl.BlockSpec((1, cbk), lambda j, i: (0, j))],
        out_specs=[main, main],
        out_shape=[jax.ShapeDtypeStruct((T, dff), F32), jax.ShapeDtypeStruct((T, dff), BF16)],
        compiler_params=_params("parallel", "parallel"),
    )(up, up, up, up, cw, cbias)


def _conv_bwd(up, ac, dact, cw, ctx_rows, name):
    T, dff = up.shape[0], up.shape[1] // 2
    cbk = _conv_cblock(dff)
    nb, cb = T // TB, ctx_rows // TB
    nvb = dff // cbk

    def body(ap_ref, a_ref, an_ref, vp_ref, v_ref, vn_ref, cp_ref, c_ref, cn_ref, dp_ref, d_ref, dn_ref, cw_ref,
             da_ref, dv_ref, dcw_ref, dcb_ref):
        i = pl.program_id(1)

        @pl.when(i == 0)
        def _():
            dcw_ref[...] = jnp.zeros_like(dcw_ref)
            dcb_ref[...] = jnp.zeros_like(dcb_ref)

        is_ctx, first, last, lok, rok = _conv_geometry(i, nb, cb)
        ace = _ext(cp_ref, c_ref, cn_ref, first, last)
        g = _ext(dp_ref, d_ref, dn_ref, first, last) * _ext(vp_ref, v_ref, vn_ref, first, last) * _dgelu(ace)
        dv_ref[...] = (d_ref[...] * _gelu(c_ref[...])).astype(BF16)
        gm = _shift_prev(g, lok)
        gp = _shift_next(g, rok)
        cwv = cw_ref[...]

        def comb(dr, lo):
            sl = slice(lo, lo + TB)
            return cwv[3 * dr:3 * dr + 1] * gp[sl] + cwv[3 * dr + 1:3 * dr + 2] * g[sl] + cwv[3 * dr + 2:3 * dr + 3] * gm[sl]

        da = comb(1, GRID_W) + jnp.where(is_ctx, 0.0, comb(0, 2 * GRID_W) + comb(2, 0))
        da_ref[...] = da.astype(BF16)
        e = _ext(ap_ref, a_ref, an_ref, first, last)
        taps = [_shift_prev(e, lok), e, _shift_next(e, rok)]
        gmain = g[GRID_W:GRID_W + TB]
        dcb_ref[...] += jnp.sum(gmain, axis=0, keepdims=True)
        vert = jnp.where(is_ctx, 0.0, 1.0)
        for dr in range(3):
            sl = slice(dr * GRID_W, dr * GRID_W + TB)
            for dw in range(3):
                s = jnp.sum(gmain * taps[dw][sl], axis=0, keepdims=True)
                if dr != 1:
                    s = s * vert
                k = 3 * dr + dw
                dcw_ref[k:k + 1, :] += s

    main = pl.BlockSpec((TB, cbk), lambda j, i: (i, j))
    halo = _halo_specs(cbk, T // GRID_W)
    acc9 = pl.BlockSpec((9, cbk), lambda j, i: (0, j))
    acc1 = pl.BlockSpec((1, cbk), lambda j, i: (0, j))
    return pl.pallas_call(
        body, name=name, grid=(dff // cbk, nb),
        in_specs=halo + _halo_specs(cbk, T // GRID_W, nvb) + halo + halo + [acc9],
        out_specs=[main, main, acc9, acc1],
        out_shape=[jax.ShapeDtypeStruct((T, dff), BF16), jax.ShapeDtypeStruct((T, dff), BF16),
                   jax.ShapeDtypeStruct((9, dff), F32), jax.ShapeDtypeStruct((1, dff), F32)],
        compiler_params=_params("parallel", "arbitrary"),
    )(up, up, up, up, up, up, ac, ac, ac, dact, dact, dact, cw)


def _ffn_out_fwd(act, xm, mod, wd, ctx_rows, name):
    T, D = xm.shape
    dff = act.shape[1]
    cb = ctx_rows // TB

    def body(act_ref, x_ref, mod_ref, w_ref, xo_ref, fo_ref):
        out = _nn(act_ref[...], _full(w_ref))
        fo_ref[...] = out
        xo_ref[...] = x_ref[...] + mod_ref[5:6, :] * out

    row = pl.BlockSpec((TB, D), lambda i: (i, 0))
    return pl.pallas_call(
        body, name=name, grid=(T // TB,),
        in_specs=[pl.BlockSpec((TB, dff), lambda i: (i, 0)), row,
                  pl.BlockSpec((None, N_MOD, D), lambda i: (_stream_of(i, cb), 0, 0)),
                  _rows_weight_spec(wd)],
        out_specs=[row, row],
        out_shape=[jax.ShapeDtypeStruct((T, D), F32), jax.ShapeDtypeStruct((T, D), F32)],
        compiler_params=_params("parallel"),
    )(act, xm, mod, wd)


def _ffn_out_bwd(dx, fo, mod, wd, ctx_rows, name):
    T, D = dx.shape
    dff = N_CHIPS * wd.shape[1]
    cb = ctx_rows // TB

    def body(dx_ref, fo_ref, mod_ref, w_ref, dout_ref, dact_ref, dg2_ref):
        i = pl.program_id(0)

        @pl.when((i == 0) | (i == cb))
        def _():
            dg2_ref[...] = jnp.zeros_like(dg2_ref)

        dxv = dx_ref[...]
        dg2_ref[...] += jnp.sum(dxv * fo_ref[...], axis=0, keepdims=True)
        dout = (dxv * mod_ref[5:6, :]).astype(BF16)
        dout_ref[...] = dout
        dact_ref[...] = _nt(dout, _full(w_ref))

    row = pl.BlockSpec((TB, D), lambda i: (i, 0))
    return pl.pallas_call(
        body, name=name, grid=(T // TB,),
        in_specs=[row, row, pl.BlockSpec((None, N_MOD, D), lambda i: (_stream_of(i, cb), 0, 0)),
                  _rows_weight_spec(wd)],
        out_specs=[row, pl.BlockSpec((TB, dff), lambda i: (i, 0)),
                   pl.BlockSpec((None, 1, D), lambda i: (_stream_of(i, cb), 0, 0))],
        out_shape=[jax.ShapeDtypeStruct((T, D), BF16), jax.ShapeDtypeStruct((T, dff), F32),
                   jax.ShapeDtypeStruct((2, 1, D), F32)],
        compiler_params=_params("arbitrary"),
    )(dx, fo, mod, wd)


def _loss_bwd(x, target, fw, ctx_rows, name):
    T, D = x.shape
    cb = ctx_rows // TB

    def body(x_ref, t_ref, fw_ref, dx_ref, loss_ref, dfw_ref):
        i = pl.program_id(0)

        @pl.when(i == 0)
        def _():
            loss_ref[...] = jnp.zeros_like(loss_ref)
            dfw_ref[...] = jnp.zeros_like(dfw_ref)

        @pl.when(i < cb)
        def _():
            dx_ref[...] = jnp.zeros_like(dx_ref)

        @pl.when(i >= cb)
        def _():
            xv = x_ref[...]
            r = lax.rsqrt(jnp.mean(xv * xv, axis=-1, keepdims=True) + RMS_EPS)
            xh = xv * r
            fwv = fw_ref[...]
            err = xh * fwv - t_ref[...]
            loss_ref[...] += (0.5 / D) * jnp.sum(err * err)
            dy = err * (1.0 / D)
            dfw_ref[...] += jnp.sum(dy * xh, axis=0, keepdims=True)
            dxh = dy * fwv
            dx_ref[...] = r * (dxh - xh * jnp.mean(dxh * xh, axis=-1, keepdims=True))

    row = pl.BlockSpec((TB, D), lambda i: (i, 0))
    return pl.pallas_call(
        body, name=name, grid=(T // TB,),
        in_specs=[row, pl.BlockSpec((TB, D), lambda i: (jnp.maximum(i - cb, 0), 0)), pl.BlockSpec((1, D), lambda i: (0, 0))],
        out_specs=[row, pl.BlockSpec((1, 128), lambda i: (0, 0)), pl.BlockSpec((1, D), lambda i: (0, 0))],
        out_shape=[jax.ShapeDtypeStruct((T, D), F32), jax.ShapeDtypeStruct((1, 128), F32),
                   jax.ShapeDtypeStruct((1, D), F32)],
        compiler_params=_params("arbitrary"),
    )(x, target, fw)


def _adamw(w, gs, m, v, name):
    L, R, C = w.shape
    assert len(gs) == L
    rb = _rows_tile(R, max(16, (1 << 18) // C // 16 * 16))
    bc1 = 1.0 - ADAM_B1 ** ADAM_STEP
    bc2 = 1.0 - ADAM_B2 ** ADAM_STEP

    def body(w_ref, m_ref, v_ref, *rest):
        g_refs, (g_ref, d_ref, nm_ref, nv_ref) = rest[:L], rest[L:]
        layer = pl.program_id(0)
        for li in range(L):
            @pl.when(layer == li)
            def _():
                gv = g_refs[li][...]
                g_ref[...] = gv
                nm = ADAM_B1 * m_ref[...] + (1.0 - ADAM_B1) * gv
                nv = ADAM_B2 * v_ref[...] + (1.0 - ADAM_B2) * (gv * gv)
                nm_ref[...] = nm
                nv_ref[...] = nv
                d_ref[...] = -ADAM_LR * ((nm / bc1) / (jnp.sqrt(nv / bc2) + ADAM_EPS) + ADAM_WD * w_ref[...])

    blk = pl.BlockSpec((None, rb, C), lambda l, i: (l, i, 0))
    gblk = pl.BlockSpec((rb, C), lambda l, i: (i, 0))
    sd = jax.ShapeDtypeStruct((L, R, C), F32)
    return pl.pallas_call(
        body, name=name, grid=(L, R // rb), in_specs=[blk] * 3 + [gblk] * L, out_specs=[blk] * 4, out_shape=[sd] * 4,
        compiler_params=_params("parallel", "parallel"),
    )(w, m, v, *gs)


def _local_step(xs, cv, target, W, layer_weights, on_layer_grads, ctx_rows):
    T, D = xs.shape
    depth = W["norm1_w"].shape[0]
    saved = []
    X = xs
    for l in range(depth):
        s = {}
        Wl = layer_weights(l, X)
        mod_all, sa = _mod_fwd(cv, Wl["ada_w"], W["ada_b"][l][None, :] + Wl["token"], f"mod_fwd_{l}")
        mod = mod_all[:2].reshape(2, N_MOD, D)
        h1 = _norm_mod(X, W["norm1_w"][l][None, :], mod, 0, ctx_rows, f"norm1_{l}")
        parts = _mm_nn_w(h1, Wl["w_in"], F32, f"in_proj_{l}")
        o_f, st_f = _hgrn_fwd(parts, W["hlb"], l, False, ctx_rows, f"hgrn_fwd_f_{l}")
        o, st_b = _hgrn_fwd(parts, W["hlb"], l, True, ctx_rows, f"hgrn_fwd_b_{l}", o_add=o_f)
        ya = _sgu_fwd(parts, W["sgu_ln_w"][l][None, :], W["sgu_ln_b"][l][None, :], W["sgu_w"][l], W["sgu_bt"][l],
                      f"sgu_fwd_{l}")
        Wl.update(Wl.pop("late")(ya))
        yb, pa, pb, mg, tmo, xm = _token_out_fwd(o, parts, ya, X, mod, W["hnw"][l][None, :] + Wl["late_token"], Wl["w_a"],
                                                 Wl["w_b"], Wl["w_o"], ctx_rows, f"token_out_fwd_{l}")
        h2 = _norm_mod(xm, W["norm2_w"][l][None, :], mod, 3, ctx_rows, f"norm2_{l}")
        up = _mm_nn_w(h2, Wl["w_up"], F32, f"up_proj_{l}")
        ac, act = _conv_fwd(up, Wl["conv_w"], W["conv_b"][l][None, :], ctx_rows, f"conv_fwd_{l}")
        xo, fo = _ffn_out_fwd(act, xm, mod, Wl["w_down"], ctx_rows, f"ffn_out_fwd_{l}")
        s.update(X=X, Wl=Wl, mod=mod, mod_all=mod_all, sa=sa, h1=h1, parts=parts, o=o, st_f=st_f, st_b=st_b, ya=ya, yb=yb,
                 pa=pa, pb=pb, mg=mg, tmo=tmo, xm=xm, h2=h2, up=up, ac=ac, act=act, fo=fo)
        saved.append(s)
        X = xo

    dX, loss_row, dfw = _loss_bwd(X, target, W["final_norm_w"][None, :], ctx_rows, "loss_bwd")
    G = {k: [None] * depth for k in ("ada_b", "norm1_w", "sgu_ln_w", "sgu_ln_b", "sgu_w", "sgu_b", "hlb1", "hnw", "norm2_w",
                                     "conv_w", "conv_b", "dmod")}
    dcv = jnp.zeros_like(cv)
    for l in reversed(range(depth)):
        s = saved[l]
        mod, Wl = s["mod"], s["Wl"]
        big = {}
        dout2, dact, dg2 = _ffn_out_bwd(dX, s["fo"], mod, Wl["w_down"], ctx_rows, f"ffn_out_bwd_{l}")
        big["w_down"] = _mm_tn(s["act"], dout2, F32, f"dw_down_{l}")
        da, dv, dcw, dcb = _conv_bwd(s["up"], s["ac"], dact, Wl["conv_w"], ctx_rows, f"conv_bwd_{l}")
        G["conv_w"][l], G["conv_b"][l] = dcw, dcb[0]
        dup = jnp.concatenate([da, dv], axis=1)
        big["w_up"] = _mm_tn(s["h2"], dup, F32, f"dw_up_{l}", out_chips=True)
        dh2 = _mm_nt_w(dup, Wl["w_up"], F32, f"dh2_{l}")
        dxm, dm2, dnw2 = _norm_mod_bwd(dh2, s["xm"], dX, W["norm2_w"][l][None, :], mod, 3, ctx_rows, f"norm2_bwd_{l}")
        G["norm2_w"][l] = dnw2[0]
        (dout1, dpa, dpb, dparts, dya, do, dg1, dhnw) = _token_out_bwd(
            dxm, s["tmo"], s["pa"], s["pb"], s["o"], s["parts"], mod, W["hnw"][l][None, :], Wl["w_a"], Wl["w_b"], Wl["w_o"],
            ctx_rows, f"token_out_bwd_{l}")
        G["hnw"][l] = dhnw[0]
        big["w_o"] = _mm_tn(s["mg"], dout1, F32, f"dw_o_{l}")
        big["w_a"] = _mm_tn(s["ya"], dpa, F32, f"dw_a_{l}")
        big["w_b"] = _mm_tn(s["yb"], dpb, F32, f"dw_b_{l}")
        tok = on_layer_grads(l, "early", big)
        dparts, dsw, dsbt, dlnw, dlnb = _sgu_bwd(s["parts"], dya, W["sgu_ln_w"][l][None, :], W["sgu_ln_b"][l][None, :] + tok,
                                                 W["sgu_w"][l], W["sgu_bt"][l], dparts, f"sgu_bwd_{l}")
        G["sgu_w"][l], G["sgu_b"][l], G["sgu_ln_w"][l], G["sgu_ln_b"][l] = dsw, dsbt.T, dlnw[0], dlnb[0]
        dq_f, dz_f, di_f, dlb_f = _hgrn_bwd(s["parts"], W["hlb"], do, s["st_f"], l, False, ctx_rows, f"hgrn_bwd_f_{l}")
        dparts, dlb_b = _hgrn_bwd(s["parts"], W["hlb"], do, s["st_b"], l, True, ctx_rows, f"hgrn_bwd_b_{l}",
                                  other=(dq_f, dz_f, di_f), dparts=dparts)
        G["hlb1"][l] = jnp.concatenate([dlb_f[0], dlb_b[0]])
        tok = on_layer_grads(l, "late", {"w_in": _mm_tn(s["h1"], dparts, F32, f"dw_in_{l}", out_chips=True)})
        dh1 = _mm_nt_w(dparts, Wl["w_in"], F32, f"dh1_{l}")
        dX, dm1, dnw1 = _norm_mod_bwd(dh1, s["X"], dxm, W["norm1_w"][l][None, :] + tok, mod, 0, ctx_rows, f"norm1_bwd_{l}")
        G["norm1_w"][l] = dnw1[0]
        dmod = jnp.concatenate([dm1, dg1, dm2, dg2], axis=1).reshape(2, N_MOD * D)
        dmod16 = jnp.concatenate([dmod, jnp.zeros((cv.shape[0] - 2, N_MOD * D), F32)], axis=0)
        G["ada_b"][l] = dmod[0] + dmod[1]
        G["dmod"][l] = dmod
        dcv = dcv + _cvec_bwd(dmod16, Wl["ada_w"], cv, f"dcvec_{l}")
    G["c_ctx"] = dcv[0]
    G["final_norm_w"] = dfw[0]
    return loss_row[0, 0], dX, G, saved[0]["sa"]


def _chip_peers(x, y, c):
    return [((1 - x, y, c), 2 * (1 - x) + y), ((x, 1 - y, c), 2 * x + 1 - y), ((1 - x, 1 - y, c), 2 * (1 - x) + 1 - y)]


def _rdma_call(ins, out_shapes, plan, n_remote, n_local, name, aliases=None):
    n_in, n_out = len(ins), len(out_shapes)

    def body(*refs):
        in_refs, out_refs = refs[:n_in], refs[n_in:n_in + n_out]
        send_sems, recv_sems, local_sems = refs[n_in + n_out:]
        x, y, c = lax.axis_index("x"), lax.axis_index("y"), lax.axis_index("c")
        remote, local = plan(in_refs, out_refs, x, y, c)
        assert len(remote) == n_remote and len(local) == n_local, (name, len(remote), len(local))
        copies = [pltpu.make_async_copy(s, d, local_sems.at[i]) for i, (s, d) in enumerate(local)]
        copies += [pltpu.make_async_remote_copy(src_ref=s, dst_ref=d, send_sem=send_sems.at[k], recv_sem=recv_sems.at[k],
                                                device_id=dev, device_id_type=pl.DeviceIdType.MESH)
                   for k, (s, d, dev) in enumerate(remote)]
        for cp in copies:
            cp.start()
        for cp in copies:
            cp.wait()

    hbm = pl.BlockSpec(memory_space=pltpu.HBM)
    return pl.pallas_call(
        body, name=name, in_specs=[hbm] * n_in, out_specs=[hbm] * n_out, out_shape=out_shapes,
        scratch_shapes=[pltpu.SemaphoreType.DMA((n_remote,)), pltpu.SemaphoreType.DMA((n_remote,)),
                        pltpu.SemaphoreType.DMA((max(n_local, 1),))],
        input_output_aliases=aliases or {},
    )(*ins)


DMA_PIECE_BYTES = 1 << 18
DMA_MAX_PIECES = 8


def _row_pieces(shape, dtype):
    rows = shape[0]
    row_bytes = jnp.dtype(dtype).itemsize
    for d in shape[1:]:
        row_bytes *= d
    n = 1
    while n < DMA_MAX_PIECES and rows % (2 * n * 16) == 0 and rows * row_bytes // (2 * n) >= DMA_PIECE_BYTES:
        n *= 2
    return [(i * (rows // n), rows // n) for i in range(n)]


def _half_pieces(o, c):
    r2 = o.shape[1] // 2
    return [pl.ds(c * r2 + st, sz) for st, sz in _row_pieces((r2,) + o.shape[2:], o.dtype)]


def _n_half_pieces(arrays):
    return sum(len(_row_pieces((a.shape[1] // 2,) + a.shape[2:], a.dtype)) for a in arrays)


def _plan_gather_far(lands, x, y, c):
    me = 2 * x + y
    return [(o.at[me, rows], o.at[me, rows], dev) for dev, _ in _chip_peers(x, y, c) for o in lands
            for rows in _half_pieces(o, c)]


def _plan_gather_near(lands, x, y, c):
    return [(o.at[idx, rows], o.at[idx, rows], (x, y, 1 - c)) for _, idx in _chip_peers(x, y, c) for o in lands
            for rows in _half_pieces(o, c)]


def _gather_weights(lands, name):
    n = len(lands)
    n_far = (N_CHIPS - 1) * _n_half_pieces(lands)

    def body(*refs):
        outs = refs[n:2 * n]
        far_send, far_recv, near_send, near_recv = refs[2 * n:]
        x, y, c = lax.axis_index("x"), lax.axis_index("y"), lax.axis_index("c")
        mk = lambda plan, send, recv: [
            pltpu.make_async_remote_copy(src_ref=s, dst_ref=d, send_sem=send.at[k], recv_sem=recv.at[k], device_id=dev,
                                         device_id_type=pl.DeviceIdType.MESH)
            for k, (s, d, dev) in enumerate(plan(outs, x, y, c))]
        far, near = mk(_plan_gather_far, far_send, far_recv), mk(_plan_gather_near, near_send, near_recv)
        assert len(far) == n_far and len(near) == n_far
        for cp in far:
            cp.start()
        for k in range(n_far):
            far[k].wait_recv()
            near[k].start()
        for k in range(n_far):
            near[k].wait_recv()
        for cp in far + near:
            cp.wait_send()

    hbm = pl.BlockSpec(memory_space=pltpu.HBM)
    sems = pltpu.SemaphoreType.DMA((n_far,))
    return pl.pallas_call(
        body, name=name, in_specs=[hbm] * n, out_specs=[hbm] * n,
        out_shape=[jax.ShapeDtypeStruct(a.shape, a.dtype) for a in lands],
        scratch_shapes=[sems, sems, sems, sems], input_output_aliases={i: i for i in range(n)},
    )(*lands)


def _gather_all(v, name):
    def plan(ins, outs, x, y, c):
        (s,), (o,) = ins, outs
        me = 4 * x + 2 * y + c
        flip = lambda a, f: 1 - a if f else a
        remote = [(s, o.at[me], (flip(x, m & 4), flip(y, m & 2), flip(c, m & 1))) for m in range(1, 8)]
        return remote, [(s, o.at[me])]

    return _rdma_call([v], [jax.ShapeDtypeStruct((8,) + v.shape, v.dtype)], plan, 7, 1, name)[0]


def _reduce_pair(parts, name):
    def plan(ins, outs, x, y, c):
        return [(a.at[j, 1 - c, pl.ds(st, sz)], o.at[j, pl.ds(st, sz)], (x, y, 1 - c)) for a, o in zip(ins, outs)
                for j in range(N_CHIPS) for st, sz in _row_pieces(a.shape[2:], a.dtype)], []

    shapes = [jax.ShapeDtypeStruct((N_CHIPS,) + a.shape[2:], a.dtype) for a in parts]
    n_remote = N_CHIPS * sum(len(_row_pieces(a.shape[2:], a.dtype)) for a in parts)
    return _rdma_call(parts, shapes, plan, n_remote, 0, name)


def _plan_chips(ins, lands, x, y, c):
    me = 2 * x + y
    return [(a.at[idx, pl.ds(st, sz)], o.at[me, pl.ds(st, sz)], dev) for dev, idx in _chip_peers(x, y, c)
            for a, o in zip(ins, lands) for st, sz in _row_pieces(a.shape[1:], a.dtype)]


def _n_chips_copies(parts):
    return (N_CHIPS - 1) * sum(len(_row_pieces(a.shape[1:], a.dtype)) for a in parts)


def _reduce_chips(parts, name):
    shapes = [jax.ShapeDtypeStruct(a.shape, a.dtype) for a in parts]
    return _rdma_call(parts, shapes, lambda ins, outs, x, y, c: (_plan_chips(ins, outs, x, y, c), []),
                      _n_chips_copies(parts), 0, name)


def _gather_pair(halves, name):
    def plan(ins, outs, x, y, c):
        return [(o.at[c, pl.ds(st, sz)], o.at[c, pl.ds(st, sz)], (x, y, 1 - c)) for o in outs
                for st, sz in _row_pieces(o.shape[1:], o.dtype)], []

    shapes = [jax.ShapeDtypeStruct(a.shape, a.dtype) for a in halves]
    n_remote = sum(len(_row_pieces(a.shape[1:], a.dtype)) for a in halves)
    return _rdma_call(halves, shapes, plan, n_remote, 0, name, aliases={i: i for i in range(len(halves))})


def _split_start(ins, lands, plan, n_remote, name):
    n_buf = len(ins) + len(lands)

    def body(*refs):
        in_refs, land_refs = refs[:len(ins)], refs[len(ins):n_buf]
        send_sems, recv_sems, token = refs[n_buf], refs[n_buf + 1], refs[-1]
        x, y, c = lax.axis_index("x"), lax.axis_index("y"), lax.axis_index("c")
        remote = plan(in_refs, land_refs, x, y, c)
        assert len(remote) == n_remote, (name, len(remote))
        for k, (s, d, dev) in enumerate(remote):
            pltpu.make_async_remote_copy(src_ref=s, dst_ref=d, send_sem=send_sems.at[k], recv_sem=recv_sems.at[k],
                                         device_id=dev, device_id_type=pl.DeviceIdType.MESH).start()
        token[...] = jnp.zeros_like(token)

    hbm = pl.BlockSpec(memory_space=pltpu.HBM)
    sem = pl.BlockSpec(memory_space=pltpu.SEMAPHORE)
    bufs = list(ins) + list(lands)
    out = pl.pallas_call(
        body, name=name, in_specs=[hbm] * n_buf,
        out_specs=(sem, sem) + (hbm,) * n_buf + (pl.BlockSpec(memory_space=pltpu.VMEM),),
        out_shape=(pltpu.SemaphoreType.DMA((n_remote,)), pltpu.SemaphoreType.DMA((n_remote,)))
        + tuple(pltpu.HBM(a.shape, a.dtype) for a in bufs) + (jax.ShapeDtypeStruct((8, 128), F32),),
        input_output_aliases={i: 2 + i for i in range(n_buf)},
        compiler_params=pltpu.CompilerParams(has_side_effects=pltpu.SideEffectType.DATAFLOW_SIDE_EFFECTING),
    )(*[pltpu.with_memory_space_constraint(a, pltpu.HBM) for a in bufs])
    return dict(send=out[0], recv=out[1], ins=list(out[2:2 + len(ins)]), lands=list(out[2 + len(ins):2 + n_buf]),
                token=out[-1][0, 0], plan=plan, n_remote=n_remote)


def _split_wait(st, after, name):
    n_in, n_buf = len(st["ins"]), len(st["ins"]) + len(st["lands"])
    plan, n_remote = st["plan"], st["n_remote"]

    def body(*refs):
        in_refs, land_refs = refs[:n_in], refs[n_in:n_buf]
        send_sems, recv_sems = refs[n_buf], refs[n_buf + 1]
        x, y, c = lax.axis_index("x"), lax.axis_index("y"), lax.axis_index("c")
        for k, (s, d, dev) in enumerate(plan(in_refs, land_refs, x, y, c)):
            cp = pltpu.make_async_remote_copy(src_ref=s, dst_ref=d, send_sem=send_sems.at[k], recv_sem=recv_sems.at[k],
                                              device_id=dev, device_id_type=pl.DeviceIdType.MESH)
            cp.wait_send()
            cp.wait_recv()

    hbm = pl.BlockSpec(memory_space=pltpu.HBM)
    sem = pl.BlockSpec(memory_space=pltpu.SEMAPHORE)
    bufs = st["ins"] + st["lands"]
    out = pl.pallas_call(
        body, name=name, in_specs=[hbm] * n_buf + [sem, sem, pl.BlockSpec(memory_space=pl.ANY)],
        out_specs=[hbm] * n_buf, out_shape=[pltpu.HBM(a.shape, a.dtype) for a in bufs],
        input_output_aliases={i: i for i in range(n_buf)},
        compiler_params=pltpu.CompilerParams(has_side_effects=pltpu.SideEffectType.DATAFLOW_SIDE_EFFECTING),
    )(*bufs, st["send"], st["recv"], after)
    return list(out[:n_in]), list(out[n_in:])


def _pair_forward(lands, name):
    shapes = [jax.ShapeDtypeStruct(a.shape, a.dtype) for a in lands]
    return _rdma_call(lands, shapes, lambda ins, outs, x, y, c: (_plan_gather_near(outs, x, y, c), []),
                      (N_CHIPS - 1) * _n_half_pieces(lands), 0, name, aliases={i: i for i in range(len(lands))})


def _sum_block_rows(r, C):
    return _rows_tile(r, max(16, (1 << 18) // C // 16 * 16))


def _sum_pair(a, recv, cidx, name):
    nch, _, r, C = a.shape
    rb = _sum_block_rows(r, C)

    def body(c_ref, a_ref, r_ref, o_ref):
        o_ref[...] = (a_ref[...] + r_ref[...]).astype(BF16)

    blk = pl.BlockSpec((None, rb, C), lambda j, i, c: (j, i, 0))
    return pl.pallas_call(
        body, name=name,
        grid_spec=pltpu.PrefetchScalarGridSpec(
            num_scalar_prefetch=1, grid=(nch, r // rb),
            in_specs=[pl.BlockSpec((None, None, rb, C), lambda j, i, c: (j, c[0], i, 0)), blk], out_specs=blk),
        out_shape=jax.ShapeDtypeStruct((nch, r, C), BF16),
        compiler_params=_params("parallel", "parallel"),
    )(cidx, a, recv)


def _sum_chips(mine, recv, ids, name):
    nch, r, C = recv.shape
    rb = _sum_block_rows(r, C)

    def body(ids_ref, m_ref, *rest):
        r_refs, o_ref = rest[:nch], rest[nch]
        chip = ids_ref[1]
        own = m_ref[...].astype(F32)
        acc = jnp.where(chip == 0, own, r_refs[0][...].astype(F32))
        for q in range(1, nch):
            acc = acc + jnp.where(chip == q, own, r_refs[q][...].astype(F32))
        o_ref[...] = acc

    def slot(q):
        return pl.BlockSpec((None, rb, C), lambda i, ids: (jnp.where(ids[1] == q, (q + 1) % nch, q), i, 0))

    return pl.pallas_call(
        body, name=name,
        grid_spec=pltpu.PrefetchScalarGridSpec(
            num_scalar_prefetch=1, grid=(r // rb,),
            in_specs=[pl.BlockSpec((None, rb, C), lambda i, ids: (ids[1], i, 0))] + [slot(q) for q in range(nch)],
            out_specs=pl.BlockSpec((None, rb, C), lambda i, ids: (ids[0], i, 0))),
        out_shape=jax.ShapeDtypeStruct((N_CORES, r, C), F32),
        compiler_params=_params("parallel"),
    )(ids, mine, *([recv] * nch))


PACK_COLS = 1024
_SHARDED = ("ada_w", "w_in", "w_branch_a", "w_branch_b", "w_out", "ffn_w_up", "ffn_w_down")
_LAYER_KEYS = ("ada_w", "w_in", "w_a", "w_b", "w_o", "w_up", "w_down")
_SMALL = ("c_ctx", "ada_b", "norm1_w", "sgu_ln_w", "sgu_ln_b", "sgu_w", "sgu_b", "hgrn_lower_bounds", "hgrn_norm_w",
          "norm2_w", "ffn_conv_b", "final_norm_w")
_ORDER = ("c_ctx", "ada_w", "ada_b", "norm1_w", "w_in", "sgu_ln_w", "sgu_ln_b", "sgu_w", "sgu_b", "hgrn_lower_bounds",
          "hgrn_norm_w", "w_branch_a", "w_branch_b", "w_out", "norm2_w", "ffn_w_up", "ffn_conv_w", "ffn_conv_b",
          "ffn_w_down", "final_norm_w")


def _pad_to(v, n):
    return jnp.concatenate([v, jnp.zeros((n - v.shape[0],), v.dtype)]) if v.shape[0] < n else v


def _round_up(n, m):
    return (n + m - 1) // m * m


def _pack(arrays, n_pad):
    flat = jnp.concatenate([a.reshape(-1) for a in arrays])
    return _pad_to(flat, n_pad)


def _unpack(flat, like):
    out, off = [], 0
    for a in like:
        out.append(flat[off:off + a.size].reshape(a.shape))
        off += a.size
    return out


def kernel(x, c, ctx, c_ctx, ada_w, ada_b, norm1_w, w_in, sgu_ln_w, sgu_ln_b, sgu_w, sgu_b, hgrn_lower_bounds, hgrn_norm_w, w_branch_a, w_branch_b, w_out, norm2_w, ffn_w_up, ffn_conv_w, ffn_conv_b, ffn_w_down, final_norm_w, loss_target, m_c_ctx, m_ada_w, m_ada_b, m_norm1_w, m_w_in, m_sgu_ln_w, m_sgu_ln_b, m_sgu_w, m_sgu_b, m_hgrn_lower_bounds, m_hgrn_norm_w, m_w_branch_a, m_w_branch_b, m_w_out, m_norm2_w, m_ffn_w_up, m_ffn_conv_w, m_ffn_conv_b, m_ffn_w_down, m_final_norm_w, v_c_ctx, v_ada_w, v_ada_b, v_norm1_w, v_w_in, v_sgu_ln_w, v_sgu_ln_b, v_sgu_w, v_sgu_b, v_hgrn_lower_bounds, v_hgrn_norm_w, v_w_branch_a, v_w_branch_b, v_w_out, v_norm2_w, v_ffn_w_up, v_ffn_conv_w, v_ffn_conv_b, v_ffn_w_down, v_final_norm_w):
    w = dict(c_ctx=c_ctx, ada_w=ada_w, ada_b=ada_b, norm1_w=norm1_w, w_in=w_in, sgu_ln_w=sgu_ln_w, sgu_ln_b=sgu_ln_b,
             sgu_w=sgu_w, sgu_b=sgu_b, hgrn_lower_bounds=hgrn_lower_bounds, hgrn_norm_w=hgrn_norm_w, w_branch_a=w_branch_a,
             w_branch_b=w_branch_b, w_out=w_out, norm2_w=norm2_w, ffn_w_up=ffn_w_up, ffn_conv_w=ffn_conv_w,
             ffn_conv_b=ffn_conv_b, ffn_w_down=ffn_w_down, final_norm_w=final_norm_w)
    mom = dict(zip(_ORDER, (m_c_ctx, m_ada_w, m_ada_b, m_norm1_w, m_w_in, m_sgu_ln_w, m_sgu_ln_b, m_sgu_w, m_sgu_b,
                            m_hgrn_lower_bounds, m_hgrn_norm_w, m_w_branch_a, m_w_branch_b, m_w_out, m_norm2_w, m_ffn_w_up,
                            m_ffn_conv_w, m_ffn_conv_b, m_ffn_w_down, m_final_norm_w)))
    var = dict(zip(_ORDER, (v_c_ctx, v_ada_w, v_ada_b, v_norm1_w, v_w_in, v_sgu_ln_w, v_sgu_ln_b, v_sgu_w, v_sgu_b,
                            v_hgrn_lower_bounds, v_hgrn_norm_w, v_w_branch_a, v_w_branch_b, v_w_out, v_norm2_w, v_ffn_w_up,
                            v_ffn_conv_w, v_ffn_conv_b, v_ffn_w_down, v_final_norm_w)))
    depth, D = norm1_w.shape
    dff = ffn_conv_b.shape[1]
    ctx_rows, seq = ctx.shape[1], x.shape[1]

    assert depth == 2, "the lower-bound softmax is written for two layers"
    core = lax.axis_index("c")
    chip = 2 * lax.axis_index("x") + lax.axis_index("y")
    ids = jnp.stack([core, chip]).astype(jnp.int32)

    first, rest = _LAYER_KEYS[:2], _LAYER_KEYS[2:]
    shard = lambda l, k: w[_SHARDED[_LAYER_KEYS.index(k)]][l].astype(BF16)
    started, conv_full = {}, []

    def landing(s):
        return lax.dynamic_update_slice(lax.empty((N_CHIPS,) + s.shape, s.dtype), s[None], (chip,) + (0,) * s.ndim)

    def start_gather(l, keys, tag):
        lands = [landing(shard(l, k)) for k in keys]
        started[tag] = _split_start([], lands, lambda ins, lds, x, y, c: _plan_gather_far(lds, x, y, c),
                                    (N_CHIPS - 1) * _n_half_pieces(lands), f"gather_start_{tag}")
        return started[tag]["token"]

    def finish_gather(keys, tag, after):
        _, lands = _split_wait(started[tag], after, f"gather_wait_{tag}")
        return dict(zip(keys, _pair_forward(lands, f"gather_forward_{tag}")))

    def layer_weights(l, after):
        if l == 0:
            got = _gather_weights([landing(shard(0, k)) for k in first] + [landing(ffn_conv_w)], "gather_weights_first")
            conv_full.append(jnp.transpose(got[-1], (1, 2, 3, 0, 4)).reshape(depth, 9, dff))
            out = dict(zip(first, got), token=start_gather(0, rest, "rest_0"))
        else:
            out = dict(finish_gather(first, f"first_{l}", after), token=0.0)

        def late(after_late):
            more = finish_gather(rest, f"rest_{l}", after_late)
            more["late_token"] = 0.0
            if l + 1 < depth:
                more["late_token"] = start_gather(l + 1, first, f"first_{l + 1}") + start_gather(l + 1, rest, f"rest_{l + 1}")
            return more

        return dict(out, conv_w=conv_full[0][l], late=late)

    pending, pair_sums_of = [], {}

    def pair_reduce(tag, gs):
        parts = [g.reshape(N_CHIPS, N_CORES, g.size // (N_CHIPS * N_CORES * g.shape[-1]), g.shape[-1]) for g in gs]
        other = _reduce_pair(parts, f"reduce_pair_{tag}")
        return [_sum_pair(a, o, ids, f"sum_pair_{tag}_{i}") for i, (a, o) in enumerate(zip(parts, other))]

    def on_layer_grads(l, stage, gs):
        keys = [k for k in gs if k != "w_in"] if stage == "early" else ["w_in"]
        if l == 0 and stage == "late":
            pair_sums_of["last"] = (keys, gs)
            return 0.0
        tag = f"{stage}_{l}"
        sums = pair_reduce(tag, [gs[k] for k in keys])
        lands = [lax.empty(s.shape, s.dtype) for s in sums]
        st = _split_start(sums, lands, _plan_chips, _n_chips_copies(sums), f"reduce_chips_start_{tag}")
        pending.append((tag, l, keys, st))
        return st["token"]

    W = dict(ada_b=ada_b, norm1_w=norm1_w, sgu_ln_w=sgu_ln_w, sgu_ln_b=sgu_ln_b, sgu_w=sgu_w.astype(BF16),
             sgu_bt=jnp.swapaxes(sgu_b, 1, 2), hlb=hgrn_lower_bounds, hnw=hgrn_norm_w, norm2_w=norm2_w, conv_b=ffn_conv_b,
             final_norm_w=final_norm_w)
    xs = jnp.concatenate([ctx[0], x[0]], axis=0)
    cv = jnp.concatenate([c_ctx[None, :], c, jnp.zeros((14, D), F32)], axis=0)
    loss_local, dxs, G, sa = _local_step(xs, cv, loss_target[0], W, layer_weights, on_layer_grads, ctx_rows)
    loss = lax.psum(loss_local, ("x", "y", "c"))
    grad_x = dxs[ctx_rows:][None]

    pad8 = lambda a: jnp.pad(a, ((0, 8 - a.shape[0]), (0, 0)))
    fact = jnp.concatenate([pad8(sa[1:2].astype(F32))] + [pad8(G["dmod"][l][1].reshape(N_MOD, D)) for l in range(depth)]
                           + [pad8(G["dmod"][l][0].reshape(N_MOD, D)) for l in range(depth)], axis=0)
    facts = _gather_all(fact, "gather_mod_factors")
    lhs = jnp.concatenate([facts[:, 0].astype(BF16), jnp.broadcast_to(sa[0:1], (8, D))], axis=0)
    ada_cols = N_MOD * D // N_CHIPS
    g_ada = []
    for l in range(depth):
        lo_x, lo_c = 8 * (1 + l), 8 * (1 + depth + l)
        rhs = jnp.concatenate([facts[:, lo_x:lo_x + N_MOD].reshape(8, N_MOD * D),
                               facts[:, lo_c:lo_c + N_MOD].reshape(8, N_MOD * D)], axis=0)
        rhs = lax.dynamic_slice_in_dim(rhs, chip * ada_cols, ada_cols, axis=1).astype(BF16)
        g_ada.append(_mm_tn(lhs, rhs, F32, f"dw_ada_{l}"))

    dh = G["hlb1"][depth - 1]
    small_like = [w[k] for k in _SMALL] + [jnp.zeros((depth, 9, dff), F32)]
    small = [G["c_ctx"], jnp.stack(G["ada_b"]), jnp.stack(G["norm1_w"]), jnp.stack(G["sgu_ln_w"]), jnp.stack(G["sgu_ln_b"]),
             jnp.stack(G["sgu_w"]), jnp.stack(G["sgu_b"]), jnp.stack([-dh, dh]), jnp.stack(G["hnw"]), jnp.stack(G["norm2_w"]),
             jnp.stack(G["conv_b"]), G["final_norm_w"], jnp.stack(G["conv_w"])]
    n_small = sum(a.size for a in small)
    n_small_pad = _round_up(n_small, N_CORES * 16 * PACK_COLS)
    small_rows = n_small_pad // (N_CORES * PACK_COLS)
    small_rep = jnp.broadcast_to(_pack(small, n_small_pad).reshape(1, N_CORES, small_rows, PACK_COLS),
                                 (N_CHIPS, N_CORES, small_rows, PACK_COLS))
    last_keys, last_gs = pair_sums_of["last"]
    last_sums = pair_reduce("last", [last_gs[k] for k in last_keys] + [small_rep])
    last_recv = _reduce_chips(last_sums, "reduce_chips_last")

    halves, where = [], {}
    for tag, l, keys, st in pending:
        sums, recv = _split_wait(st, dxs, f"reduce_chips_wait_{tag}")
        for i, k in enumerate(keys):
            where[(l, k)] = len(halves)
            halves.append(_sum_chips(sums[i], recv[i], ids, f"sum_chips_{tag}_{i}"))
    for i, k in enumerate(list(last_keys) + ["small"]):
        where[(0, k)] = len(halves)
        halves.append(_sum_chips(last_sums[i], last_recv[i], ids, f"sum_chips_last_{i}"))
    reduced = _gather_pair(halves, "gather_pair")

    g_small = _unpack(reduced[where[(0, "small")]].reshape(-1), small_like)
    grads = dict(zip(_SMALL, g_small[:-1]))
    g_conv = lax.dynamic_slice_in_dim(g_small[-1].reshape(depth, 3, 3, dff), chip * (dff // N_CHIPS), dff // N_CHIPS, axis=3)

    delta, new_m, new_v = {}, {}, {}
    for i, k in enumerate(_SHARDED):
        shp = w[k].shape
        gs = g_ada if i == 0 else [reduced[where[(l, _LAYER_KEYS[i])]].reshape(shp[1:]) for l in range(depth)]
        grads[k], delta[k], new_m[k], new_v[k] = _adamw(w[k], gs, mom[k], var[k], f"adamw_{k}")
    packed = _SMALL + ("ffn_conv_w",)
    n_pad = _round_up(sum(w[k].size for k in packed), 16 * PACK_COLS)
    pack = lambda t: _pack([t[k] for k in packed], n_pad).reshape(1, -1, PACK_COLS)
    grads["ffn_conv_w"] = g_conv
    _, d, nm, nv = _adamw(pack(w), [pack(grads)[0]], pack(mom), pack(var), "adamw_packed")
    like = [w[k] for k in packed]
    for src, dst in ((d, delta), (nm, new_m), (nv, new_v)):
        dst.update(zip(packed, _unpack(src.reshape(-1), like)))

    return (loss, grad_x, *[grads[k] for k in _ORDER], *[delta[k] for k in _ORDER], *[new_m[k] for k in _ORDER],
            *[new_v[k] for k in _ORDER])
```

```python
import functools

import jax
import jax.numpy as jnp
from jax import lax
from jax.experimental import pallas as pl
from jax.experimental.pallas import tpu as pltpu

F32 = jnp.float32
BF16 = jnp.bfloat16

GRID_W = 64
HG_CHUNK = 64
SGU_CHUNK = 128
HEAD = 128
TB = 256
N_MOD = 6
RMS_EPS = 1e-6
LN_EPS = 1e-5
VMEM_LIMIT = 48 * 1024 * 1024
N_CHIPS = 4
N_CORES = 2

ADAM_LR = 0.001
ADAM_B1 = 0.9
ADAM_B2 = 0.999
ADAM_EPS = 1e-08
ADAM_WD = 0.01
ADAM_STEP = 10

_GELU_C = 0.7978845608028654
_GELU_A = 0.044715


def _sigmoid(x):
    return 0.5 * jnp.tanh(0.5 * x) + 0.5


def _silu(x):
    return x * _sigmoid(x)


def _silu_both(x):
    s = _sigmoid(x)
    return x * s, s * (1.0 + x * (1.0 - s))


def _dsilu(x):
    return _silu_both(x)[1]


def _gelu_both(x):
    x2 = x * x
    t = jnp.tanh(_GELU_C * (x + _GELU_A * x2 * x))
    h = 0.5 * (1.0 + t)
    return x * h, h + 0.5 * x * (1.0 - t * t) * (_GELU_C + 3.0 * _GELU_C * _GELU_A * x2)


def _gelu(x):
    return 0.5 * x * (1.0 + jnp.tanh(_GELU_C * (x + _GELU_A * x * x * x)))


def _dgelu(x):
    return _gelu_both(x)[1]


def _dot(a, b, ca, cb):
    return lax.dot_general(a, b, (((ca,), (cb,)), ((), ())), preferred_element_type=F32)


def _nn(a, b):
    return _dot(a, b, 1, 0)


def _nt(a, b):
    return _dot(a, b, 1, 1)


def _tn(a, b):
    return _dot(a, b, 0, 0)


def _params(*sem, vmem=VMEM_LIMIT):
    return pltpu.CompilerParams(dimension_semantics=sem if sem else None, vmem_limit_bytes=vmem)


def _stream_of(i, ctx_blocks):
    return (i >= ctx_blocks).astype(jnp.int32)


def _mm(a, b, mode, tm, tn, tk, out_dtype, name, add=None, b_chips=False, out_chips=False):
    if not b_chips:
        bshape = b.shape
    else:
        bshape = (b.shape[1], N_CHIPS * b.shape[2])
    if mode == "nn":
        (M, K), (K2, N) = a.shape, bshape
    elif mode == "nt":
        (M, K), (N, K2) = a.shape, bshape
    else:
        (K, M), (K2, N) = a.shape, bshape
    assert K == K2 and M % tm == 0 and N % tn == 0 and K % tk == 0, (name, a.shape, b.shape, tm, tn, tk)
    nk = K // tk
    if mode == "tn":
        a_spec = pl.BlockSpec((tk, tm), lambda j, i, k: (k, i))
    else:
        a_spec = pl.BlockSpec((tm, tk), lambda j, i, k: (i, k))
    if not b_chips:
        if mode == "nt":
            b_spec = pl.BlockSpec((tn, tk), lambda j, i, k: (j, k))
        else:
            b_spec = pl.BlockSpec((tk, tn), lambda j, i, k: (k, j))
    else:
        cols = b.shape[2]
        if mode == "nn":
            per = cols // tn
            assert cols % tn == 0
            b_spec = pl.BlockSpec((None, tk, tn), lambda j, i, k: (j // per, k, j % per))
        else:
            per = cols // tk
            assert mode == "nt" and cols % tk == 0
            b_spec = pl.BlockSpec((None, tn, tk), lambda j, i, k: (k // per, j, k % per))
    if out_chips:
        per_o = (N // N_CHIPS) // tn
        assert (N // N_CHIPS) % tn == 0 and add is None
        o_spec = pl.BlockSpec((None, tm, tn), lambda j, i, k: (j // per_o, i, j % per_o))
        o_shape = (N_CHIPS, M, N // N_CHIPS)
    else:
        o_spec = pl.BlockSpec((tm, tn), lambda j, i, k: (i, j))
        o_shape = (M, N)
    ca, cb = {"nn": (1, 0), "nt": (1, 1), "tn": (0, 0)}[mode]

    def body(a_ref, b_ref, *rest):
        if add is None:
            o_ref, acc = rest
        else:
            add_ref, o_ref, acc = rest
        k = pl.program_id(2)

        @pl.when(k == 0)
        def _():
            acc[...] = jnp.zeros_like(acc)

        acc[...] += _dot(a_ref[...], b_ref[...], ca, cb)

        @pl.when(k == nk - 1)
        def _():
            r = acc[...]
            if add is not None:
                r = r + add_ref[...]
            o_ref[...] = r.astype(out_dtype)

    ins = [a, b] + ([] if add is None else [add])
    specs = [a_spec, b_spec] + ([] if add is None else [o_spec])
    return pl.pallas_call(
        body, name=name, grid=(N // tn, M // tm, nk), in_specs=specs, out_specs=o_spec,
        out_shape=jax.ShapeDtypeStruct(o_shape, out_dtype),
        scratch_shapes=[pltpu.VMEM((tm, tn), F32)],
        compiler_params=_params("parallel", "parallel", "arbitrary"),
    )(*ins)


def _tile(n, pref):
    if n <= pref:
        return n
    best = None
    for t in range(128, pref + 1, 128):
        if n % t == 0:
            best = t
    assert best is not None, (n, pref)
    return best


def _rows_tile(n, pref):
    if n <= pref:
        return n
    best = None
    for t in range(16, pref + 1, 16):
        if n % t == 0:
            best = t
    assert best is not None, (n, pref)
    return best


def _mm_nn_w(a, wg, out_dtype, name):
    M, K = a.shape
    return _mm(a, wg, "nn", _rows_tile(M, 1088), _tile(wg.shape[2], 1536), _tile(K, 1536), out_dtype, name, b_chips=True)


def _mm_nt_w(a, wg, out_dtype, name):
    M, K = a.shape
    return _mm(a, wg, "nt", _rows_tile(M, 1088), _tile(wg.shape[1], 1024), _tile(wg.shape[2], 1536), out_dtype, name,
               b_chips=True)


def _mm_tn(a, b, out_dtype, name, out_chips=False):
    K, M = a.shape
    N = b.shape[1]
    ncol = N // N_CHIPS if out_chips else N
    tm, tn = _tile(M, 1408), _tile(ncol, 1408)
    if tm * tn > 1408 * 1152:
        tn = _tile(ncol, 1152)
    return _mm(a, b, "tn", tm, tn, _rows_tile(K, 2176), out_dtype, name, out_chips=out_chips)


def _mod_fwd(cv, wg, b, name):
    R, D = cv.shape
    tn = wg.shape[2]
    N = N_CHIPS * tn

    def body(cv_ref, w_ref, b_ref, mod_ref, sa_ref):
        sa = _silu(cv_ref[...]).astype(BF16)
        sa_ref[...] = sa
        mod_ref[...] = _nn(sa, w_ref[...]) + b_ref[...]

    return pl.pallas_call(
        body, name=name, grid=(N_CHIPS,),
        in_specs=[pl.BlockSpec((R, D), lambda j: (0, 0)), pl.BlockSpec((None, D, tn), lambda j: (j, 0, 0)),
                  pl.BlockSpec((1, tn), lambda j: (0, j))],
        out_specs=[pl.BlockSpec((R, tn), lambda j: (0, j)), pl.BlockSpec((R, D), lambda j: (0, 0))],
        out_shape=[jax.ShapeDtypeStruct((R, N), F32), jax.ShapeDtypeStruct((R, D), BF16)],
        compiler_params=_params("arbitrary"),
    )(cv, wg, b)


def _cvec_bwd(dmod, wg, cv, name):
    R, N = dmod.shape
    D = wg.shape[1]
    tk = wg.shape[2]
    nk = N_CHIPS

    def body(dm_ref, w_ref, cv_ref, o_ref):
        k = pl.program_id(0)

        @pl.when(k == 0)
        def _():
            o_ref[...] = jnp.zeros_like(o_ref)

        o_ref[...] += _nt(dm_ref[...].astype(BF16), w_ref[...])

        @pl.when(k == nk - 1)
        def _():
            o_ref[...] = o_ref[...] * _dsilu(cv_ref[...])

    return pl.pallas_call(
        body, name=name, grid=(nk,),
        in_specs=[pl.BlockSpec((R, tk), lambda k: (0, k)), pl.BlockSpec((None, D, tk), lambda k: (k, 0, 0)),
                  pl.BlockSpec((R, D), lambda k: (0, 0))],
        out_specs=pl.BlockSpec((R, D), lambda k: (0, 0)),
        out_shape=jax.ShapeDtypeStruct((R, D), F32),
        compiler_params=_params("arbitrary"),
    )(dmod, wg, cv)


def _norm_mod(x, nw, mod, which, ctx_rows, name):
    T, D = x.shape
    cb = ctx_rows // TB

    def body(x_ref, nw_ref, mod_ref, h_ref):
        xv = x_ref[...]
        r = lax.rsqrt(jnp.mean(xv * xv, axis=-1, keepdims=True) + RMS_EPS)
        y = xv * r * nw_ref[...]
        sh = mod_ref[which:which + 1, :]
        sc = mod_ref[which + 1:which + 2, :]
        h_ref[...] = (y * (1.0 + sc) + sh).astype(BF16)

    return pl.pallas_call(
        body, name=name, grid=(T // TB,),
        in_specs=[pl.BlockSpec((TB, D), lambda i: (i, 0)), pl.BlockSpec((1, D), lambda i: (0, 0)),
                  pl.BlockSpec((None, N_MOD, D), lambda i: (_stream_of(i, cb), 0, 0))],
        out_specs=pl.BlockSpec((TB, D), lambda i: (i, 0)),
        out_shape=jax.ShapeDtypeStruct((T, D), BF16),
        compiler_params=_params("parallel"),
    )(x, nw, mod)


def _norm_mod_bwd(dh, x, dres, nw, mod, which, ctx_rows, name):
    T, D = x.shape
    cb = ctx_rows // TB

    def body(dh_ref, x_ref, dres_ref, nw_ref, mod_ref, dx_ref, dm_ref, dnw_ref):
        i = pl.program_id(0)

        @pl.when(i == 0)
        def _():
            dnw_ref[...] = jnp.zeros_like(dnw_ref)

        @pl.when((i == 0) | (i == cb))
        def _():
            dm_ref[...] = jnp.zeros_like(dm_ref)

        xv = x_ref[...]
        dh = dh_ref[...]
        r = lax.rsqrt(jnp.mean(xv * xv, axis=-1, keepdims=True) + RMS_EPS)
        xh = xv * r
        nwv = nw_ref[...]
        sc = mod_ref[which + 1:which + 2, :]
        y = xh * nwv
        dm_ref[0:1, :] += jnp.sum(dh, axis=0, keepdims=True)
        dm_ref[1:2, :] += jnp.sum(dh * y, axis=0, keepdims=True)
        dy = dh * (1.0 + sc)
        dnw_ref[...] += jnp.sum(dy * xh, axis=0, keepdims=True)
        dxh = dy * nwv
        dx_ref[...] = dres_ref[...] + r * (dxh - xh * jnp.mean(dxh * xh, axis=-1, keepdims=True))

    return pl.pallas_call(
        body, name=name, grid=(T // TB,),
        in_specs=[pl.BlockSpec((TB, D), lambda i: (i, 0)), pl.BlockSpec((TB, D), lambda i: (i, 0)),
                  pl.BlockSpec((TB, D), lambda i: (i, 0)), pl.BlockSpec((1, D), lambda i: (0, 0)),
                  pl.BlockSpec((None, N_MOD, D), lambda i: (_stream_of(i, cb), 0, 0))],
        out_specs=[pl.BlockSpec((TB, D), lambda i: (i, 0)),
                   pl.BlockSpec((None, 2, D), lambda i: (_stream_of(i, cb), 0, 0)),
                   pl.BlockSpec((1, D), lambda i: (0, 0))],
        out_shape=[jax.ShapeDtypeStruct((T, D), F32), jax.ShapeDtypeStruct((2, 2, D), F32),
                   jax.ShapeDtypeStruct((1, D), F32)],
        compiler_params=_params("arbitrary"),
    )(dh, x, dres, nw, mod)


def _scan_chunk(n, rev, n_ctx, n_all):
    if not rev:
        return n
    return jnp.where(n < n_ctx, n_ctx - 1 - n, n_all - 1 + n_ctx - n)


def _cumsum_rows(x, rev):
    rows = x.shape[0]
    row = lax.broadcasted_iota(jnp.int32, (rows, 1), 0)
    s = 1
    while s < rows:
        if not rev:
            x = x + jnp.where(row >= s, pltpu.roll(x, s, 0), 0.0)
        else:
            x = x + jnp.where(row < rows - s, pltpu.roll(x, rows - s, 0), 0.0)
        s *= 2
    return x


def _lower_bound(hlb_ref, layer):
    h = hlb_ref[...]
    if layer == 0:
        return jnp.zeros_like(h[0:1, :])
    return _sigmoid(h[1:2, :] - h[0:1, :])


def _hgrn_gates(q_ref, f_ref, hlb_ref, layer, rev):
    lb = _lower_bound(hlb_ref, layer)
    z = f_ref[...]
    sig = 1.0 / (1.0 + jnp.exp(-z))
    fg = lb + (1.0 - lb) * sig
    kk = (1.0 - lb) * (1.0 - sig)
    g = jnp.log(fg)
    b = _cumsum_rows(g, rev)
    bt = jnp.sum(g, axis=0, keepdims=True)
    mid = HG_CHUNK // 2
    r = b[mid:mid + 1, :] if rev else b[mid - 1:mid, :]
    qh = _silu(q_ref[...])
    return lb, sig, fg, kk, b, bt, r, qh


def _tri_mask(rev):
    t = lax.broadcasted_iota(jnp.int32, (HG_CHUNK, HG_CHUNK), 0)
    s = lax.broadcasted_iota(jnp.int32, (HG_CHUNK, HG_CHUNK), 1)
    return (s >= t) if rev else (s <= t)


def _hgrn_fwd(parts, hlb, layer, rev, ctx_rows, name, o_add=None):
    T = parts.shape[0]
    D = hlb.shape[1] // 2
    nh = D // HEAD
    n_all, n_ctx = T // HG_CHUNK, ctx_rows // HG_CHUNK
    chunk = functools.partial(_scan_chunk, rev=rev, n_ctx=n_ctx, n_all=n_all)
    fcol = 2 if rev else 1

    def body(q_ref, f_ref, i_ref, hlb_ref, *rest):
        if o_add is None:
            o_ref, st_ref, s_scr = rest
        else:
            oa_ref, o_ref, st_ref, s_scr = rest
        n = pl.program_id(0)

        @pl.when(n == 0)
        def _():
            s_scr[...] = jnp.zeros_like(s_scr)

        lb, sig, fg, kk, b, bt, r, qh = _hgrn_gates(q_ref, f_ref, hlb_ref, layer, rev)
        qr = (qh * jnp.exp(b - r)).astype(BF16)
        kr = (kk * jnp.exp(r - b)).astype(BF16)
        qe = (qh * jnp.exp(b)).astype(BF16)
        ke = (kk * jnp.exp(bt - b)).astype(BF16)
        dec = jnp.exp(bt)
        v = i_ref[...].astype(BF16)
        mask = _tri_mask(rev)
        hs = [slice(h * HEAD, (h + 1) * HEAD) for h in range(nh)]
        st = [s_scr[h] for h in range(nh)]
        a_raw = [_nt(qr[:, sl], kr[:, sl]) for sl in hs]
        o_int = [_nt(qe[:, sl], st[h].astype(BF16)) for h, sl in enumerate(hs)]
        kv = [_tn(v[:, sl], ke[:, sl]) for sl in hs]
        for h, sl in enumerate(hs):
            st_ref[h] = st[h]
            o = _nn(jnp.where(mask, a_raw[h], 0.0).astype(BF16), v[:, sl]) + o_int[h]
            if o_add is not None:
                o = o + oa_ref[:, sl]
            o_ref[:, sl] = o
            s_scr[h] = st[h] * dec[:, sl] + kv[h]

    cspec = lambda col: pl.BlockSpec((HG_CHUNK, D), lambda n: (chunk(n), col))
    ins = [parts, parts, parts, hlb]
    specs = [cspec(0), cspec(fcol), cspec(3), pl.BlockSpec((2, D), lambda n: (0, 1 if rev else 0))]
    if o_add is not None:
        ins.append(o_add)
        specs.append(cspec(0))
    return pl.pallas_call(
        body, name=name, grid=(n_all,), in_specs=specs,
        out_specs=[cspec(0), pl.BlockSpec((None, nh, HEAD, HEAD), lambda n: (n, 0, 0, 0))],
        out_shape=[jax.ShapeDtypeStruct((T, D), F32), jax.ShapeDtypeStruct((n_all, nh, HEAD, HEAD), F32)],
        scratch_shapes=[pltpu.VMEM((nh, HEAD, HEAD), F32)],
        compiler_params=_params("arbitrary"),
    )(*ins)


def _hgrn_bwd(parts, hlb, do, states, layer, rev, ctx_rows, name, other=None, dparts=None):
    T = parts.shape[0]
    D = hlb.shape[1] // 2
    nh = D // HEAD
    n_all, n_ctx = T // HG_CHUNK, ctx_rows // HG_CHUNK
    step = lambda m: n_all - 1 - m
    chunk = lambda m: _scan_chunk(step(m), rev, n_ctx, n_all)
    fcol = 2 if rev else 1
    has_add = other is not None
    assert not has_add or rev

    def body(q_ref, f_ref, i_ref, hlb_ref, do_ref, st_ref, *rest):
        if has_add:
            dqa_ref, dza_ref, dia_ref, _, out_ref, dlb_ref, ds_scr = rest
            dq_ref, dz_ref, di_ref = out_ref.at[:, 0:D], out_ref.at[:, 2 * D:3 * D], out_ref.at[:, 3 * D:4 * D]
            out_ref[:, D:2 * D] = dza_ref[...]
        else:
            dq_ref, dz_ref, di_ref, dlb_ref, ds_scr = rest
        m = pl.program_id(0)

        @pl.when(m == 0)
        def _():
            ds_scr[...] = jnp.zeros_like(ds_scr)
            dlb_ref[...] = jnp.zeros_like(dlb_ref)

        lb, sig, fg, kk, b, bt, r, qh = _hgrn_gates(q_ref, f_ref, hlb_ref, layer, rev)
        e_qr = jnp.exp(b - r)
        e_kr = jnp.exp(r - b)
        e_b = jnp.exp(b)
        e_ke = jnp.exp(bt - b)
        dec = jnp.exp(bt)
        qr = (qh * e_qr).astype(BF16)
        kr = (kk * e_kr).astype(BF16)
        qe = (qh * e_b).astype(BF16)
        ke = (kk * e_ke).astype(BF16)
        vf = i_ref[...]
        v = vf.astype(BF16)
        dov = do_ref[...].astype(BF16)
        mask = _tri_mask(rev)
        hs = [slice(h * HEAD, (h + 1) * HEAD) for h in range(nh)]
        st = [st_ref[h] for h in range(nh)]
        dst = [ds_scr[h] for h in range(nh)]
        stb = [t.astype(BF16) for t in st]
        dstb = [t.astype(BF16) for t in dst]
        a_raw = [_nt(qr[:, sl], kr[:, sl]) for sl in hs]
        da_raw = [_nt(dov[:, sl], v[:, sl]) for sl in hs]
        dq_int = [_nn(dov[:, sl], stb[h]) for h, sl in enumerate(hs)]
        dk_int = [_nn(v[:, sl], dstb[h]) for h, sl in enumerate(hs)]
        dv_int = [_nt(ke[:, sl], dstb[h]) for h, sl in enumerate(hs)]
        ds_new = [_tn(dov[:, sl], qe[:, sl]) for sl in hs]
        a = [jnp.where(mask, t, 0.0).astype(BF16) for t in a_raw]
        da = [jnp.where(mask, t, 0.0).astype(BF16) for t in da_raw]
        dv_parts = [_tn(a[h], dov[:, sl]) + dv_int[h] for h, sl in enumerate(hs)]
        dq_parts = [_nn(da[h], kr[:, sl]) * e_qr[:, sl] + dq_int[h] * e_b[:, sl] for h, sl in enumerate(hs)]
        dki_parts = [dk_int[h] * e_ke[:, sl] for h, sl in enumerate(hs)]
        dk_parts = [_tn(da[h], qr[:, sl]) * e_kr[:, sl] + dki_parts[h] for h, sl in enumerate(hs)]
        dbt_parts = [dec[:, sl] * jnp.sum(st[h] * dst[h], axis=0, keepdims=True) for h, sl in enumerate(hs)]
        for h, sl in enumerate(hs):
            ds_scr[h] = dst[h] * dec[:, sl] + ds_new[h]
        dq = jnp.concatenate(dq_parts, axis=1)
        dk = jnp.concatenate(dk_parts, axis=1)
        dki = jnp.concatenate(dki_parts, axis=1)
        dv = jnp.concatenate(dv_parts, axis=1)
        dbt = jnp.concatenate(dbt_parts, axis=1) + jnp.sum(kk * dki, axis=0, keepdims=True)
        db = qh * dq - kk * dk
        dg = _cumsum_rows(db, not rev) + dbt
        df = dg / fg - dk
        dz_ref[...] = (df * (1.0 - lb) * sig * (1.0 - sig)).astype(BF16)
        dlb_ref[...] += jnp.sum(df * (1.0 - sig), axis=0, keepdims=True)
        dqr = dq * _dsilu(q_ref[...])
        if has_add:
            dqr = dqr + dqa_ref[...]
            dv = dv + dia_ref[...]
        dq_ref[...] = dqr.astype(dq_ref.dtype)
        di_ref[...] = dv.astype(di_ref.dtype)

        @pl.when(m == n_all - 1)
        def _():
            if layer == 0:
                dlb_ref[...] = jnp.zeros_like(dlb_ref)
            else:
                dlb_ref[...] = dlb_ref[...] * lb * (1.0 - lb)

    cspec = lambda col: pl.BlockSpec((HG_CHUNK, D), lambda m: (chunk(m), col))
    ins = [parts, parts, parts, hlb, do, states]
    specs = [cspec(0), cspec(fcol), cspec(3), pl.BlockSpec((2, D), lambda m: (0, 1 if rev else 0)), cspec(0),
             pl.BlockSpec((None, nh, HEAD, HEAD), lambda m: (step(m), 0, 0, 0))]
    dlb_spec = pl.BlockSpec((1, D), lambda m: (0, 0))
    dlb_shape = jax.ShapeDtypeStruct((1, D), F32)
    if has_add:
        return pl.pallas_call(
            body, name=name, grid=(n_all,),
            in_specs=specs + [cspec(0), cspec(0), cspec(0), pl.BlockSpec(memory_space=pl.ANY)],
            out_specs=[pl.BlockSpec((HG_CHUNK, 4 * D), lambda m: (chunk(m), 0)), dlb_spec],
            out_shape=[jax.ShapeDtypeStruct(dparts.shape, dparts.dtype), dlb_shape],
            scratch_shapes=[pltpu.VMEM((nh, HEAD, HEAD), F32)], input_output_aliases={len(ins) + 3: 0},
            compiler_params=_params("arbitrary"),
        )(*ins, *other, dparts)
    return pl.pallas_call(
        body, name=name, grid=(n_all,), in_specs=specs,
        out_specs=[cspec(0), cspec(0), cspec(0), dlb_spec],
        out_shape=[jax.ShapeDtypeStruct((T, D), F32), jax.ShapeDtypeStruct((T, D), BF16),
                   jax.ShapeDtypeStruct((T, D), F32), dlb_shape],
        scratch_shapes=[pltpu.VMEM((nh, HEAD, HEAD), F32)],
        compiler_params=_params("arbitrary"),
    )(*ins)


def _sgu_ln(gv, lnw_ref, lnb_ref):
    mu = jnp.mean(gv, axis=-1, keepdims=True)
    xc = gv - mu
    rstd = lax.rsqrt(jnp.mean(xc * xc, axis=-1, keepdims=True) + LN_EPS)
    xh = xc * rstd
    return xh, rstd, xh * lnw_ref[...] + lnb_ref[...]


def _sgu_fwd(parts, lnw, lnb, w, bt, name):
    T = parts.shape[0]
    D = lnw.shape[1]
    G = D // HEAD

    def body(u_ref, v_ref, lnw_ref, lnb_ref, w_ref, bt_ref, ya_ref):
        gu = _gelu(u_ref[...])
        _, _, vn = _sgu_ln(_gelu(v_ref[...]), lnw_ref, lnb_ref)
        vnb = vn.astype(BF16)
        for g in range(G):
            sl = slice(g * HEAD, (g + 1) * HEAD)
            mixed = _nn(w_ref[g], vnb[:, sl]) + bt_ref[:, g:g + 1]
            ya_ref[:, sl] = (gu[:, sl] * mixed).astype(BF16)

    return pl.pallas_call(
        body, name=name, grid=(T // SGU_CHUNK,),
        in_specs=[pl.BlockSpec((SGU_CHUNK, D), lambda n: (n, 4)), pl.BlockSpec((SGU_CHUNK, D), lambda n: (n, 5)),
                  pl.BlockSpec((1, D), lambda n: (0, 0)), pl.BlockSpec((1, D), lambda n: (0, 0)),
                  pl.BlockSpec((G, SGU_CHUNK, SGU_CHUNK), lambda n: (0, 0, 0)),
                  pl.BlockSpec((SGU_CHUNK, G), lambda n: (0, 0))],
        out_specs=pl.BlockSpec((SGU_CHUNK, D), lambda n: (n, 0)),
        out_shape=jax.ShapeDtypeStruct((T, D), BF16),
        compiler_params=_params("parallel"),
    )(parts, parts, lnw, lnb, w, bt)


def _sgu_bwd(parts, dya, lnw, lnb, w, bt, dparts, name):
    T = parts.shape[0]
    D = lnw.shape[1]
    G = D // HEAD

    def body(u_ref, v_ref, dya_ref, lnw_ref, lnb_ref, w_ref, bt_ref, dparts_in,
             duv_ref, dw_ref, dbt_ref, dlnw_ref, dlnb_ref, dvn_scr):
        du_ref = duv_ref.at[:, 0:D]
        dv_ref = duv_ref.at[:, D:2 * D]
        n = pl.program_id(0)

        @pl.when(n == 0)
        def _():
            dw_ref[...] = jnp.zeros_like(dw_ref)
            dbt_ref[...] = jnp.zeros_like(dbt_ref)
            dlnw_ref[...] = jnp.zeros_like(dlnw_ref)
            dlnb_ref[...] = jnp.zeros_like(dlnb_ref)

        gu, dgu = _gelu_both(u_ref[...])
        gv, dgv_dv = _gelu_both(v_ref[...])
        xh, rstd, vn = _sgu_ln(gv, lnw_ref, lnb_ref)
        vnb = vn.astype(BF16)
        dya = dya_ref[...]
        lane = lax.broadcasted_iota(jnp.int32, (SGU_CHUNK, G), 1)
        dbt = jnp.zeros((SGU_CHUNK, G), F32)
        for g in range(G):
            sl = slice(g * HEAD, (g + 1) * HEAD)
            wg = w_ref[g]
            mixed = _nn(wg, vnb[:, sl]) + bt_ref[:, g:g + 1]
            dmix = dya[:, sl] * gu[:, sl]
            du_ref[:, sl] = (dya[:, sl] * mixed * dgu[:, sl]).astype(BF16)
            dmb = dmix.astype(BF16)
            dvn_scr[:, sl] = _tn(wg, dmb)
            dw_ref[g] += _nt(dmb, vnb[:, sl])
            dbt = dbt + jnp.where(lane == g, jnp.sum(dmix, axis=1, keepdims=True), 0.0)
        dbt_ref[...] += dbt
        dvn = dvn_scr[...]
        dlnw_ref[...] += jnp.sum(dvn * xh, axis=0, keepdims=True)
        dlnb_ref[...] += jnp.sum(dvn, axis=0, keepdims=True)
        dxh = dvn * lnw_ref[...]
        dgv = rstd * (dxh - jnp.mean(dxh, axis=-1, keepdims=True) - xh * jnp.mean(dxh * xh, axis=-1, keepdims=True))
        dv_ref[...] = (dgv * dgv_dv).astype(BF16)

    row = lambda col: pl.BlockSpec((SGU_CHUNK, D), lambda n: (n, col))
    vec = pl.BlockSpec((1, D), lambda n: (0, 0))
    wsp = pl.BlockSpec((G, SGU_CHUNK, SGU_CHUNK), lambda n: (0, 0, 0))
    bsp = pl.BlockSpec((SGU_CHUNK, G), lambda n: (0, 0))
    return pl.pallas_call(
        body, name=name, grid=(T // SGU_CHUNK,),
        in_specs=[row(4), row(5), row(0), vec, vec, wsp, bsp, pl.BlockSpec(memory_space=pl.ANY)],
        out_specs=[pl.BlockSpec((SGU_CHUNK, 2 * D), lambda n: (n, 2)), wsp, bsp, vec, vec],
        out_shape=[jax.ShapeDtypeStruct(dparts.shape, dparts.dtype),
                   jax.ShapeDtypeStruct((G, SGU_CHUNK, SGU_CHUNK), F32), jax.ShapeDtypeStruct((SGU_CHUNK, G), F32),
                   jax.ShapeDtypeStruct((1, D), F32), jax.ShapeDtypeStruct((1, D), F32)],
        scratch_shapes=[pltpu.VMEM((SGU_CHUNK, D), F32)], input_output_aliases={7: 0},
        compiler_params=_params("arbitrary"),
    )(parts, parts, dya, lnw, lnb, w, bt, dparts)


TBT = 256
VMEM_LIMIT_TOKEN_OUT = 58 * 1024 * 1024


def _rows_weight_spec(wg):
    return pl.BlockSpec(wg.shape, lambda i: (0, 0, 0))


def _full(w_ref):
    return w_ref[...].reshape(w_ref.shape[0] * w_ref.shape[1], w_ref.shape[2])


def _token_out_fwd(o, parts, ya, x, mod, hnw, wa, wb, wo, ctx_rows, name):
    T, D = x.shape
    nh = D // HEAD
    cb = ctx_rows // TBT

    def body(o_ref, og_ref, ga_ref, gb_ref, ya_ref, x_ref, mod_ref, hnw_ref, wa_ref, wb_ref, wo_ref,
             yb_ref, pa_ref, pb_ref, mg_ref, tmo_ref, xm_ref):
        ov = o_ref[...]
        so = _silu(og_ref[...])
        nw = hnw_ref[...]
        for h in range(nh):
            sl = slice(h * HEAD, (h + 1) * HEAD)
            seg = ov[:, sl]
            r = lax.rsqrt(jnp.mean(seg * seg, axis=-1, keepdims=True) + RMS_EPS)
            yb_ref[:, sl] = (seg * r * nw * so[:, sl]).astype(BF16)
        pa = _nn(ya_ref[...], _full(wa_ref))
        pb = _nn(yb_ref[...], _full(wb_ref))
        pa_ref[...] = pa
        pb_ref[...] = pb
        mg = (_sigmoid(ga_ref[...]) * pa + _sigmoid(gb_ref[...]) * pb).astype(BF16)
        mg_ref[...] = mg
        out = _nn(mg, _full(wo_ref))
        tmo_ref[...] = out
        xm_ref[...] = x_ref[...] + mod_ref[2:3, :] * out

    row = lambda col: pl.BlockSpec((TBT, D), lambda i: (i, col))
    wsp = _rows_weight_spec(wa)
    sd = lambda dt: jax.ShapeDtypeStruct((T, D), dt)
    return pl.pallas_call(
        body, name=name, grid=(T // TBT,),
        in_specs=[row(0), row(6), row(7), row(8), row(0), row(0),
                  pl.BlockSpec((None, N_MOD, D), lambda i: (_stream_of(i, cb), 0, 0)),
                  pl.BlockSpec((1, HEAD), lambda i: (0, 0)), wsp, wsp, wsp],
        out_specs=[row(0)] * 6,
        out_shape=[sd(BF16), sd(F32), sd(F32), sd(BF16), sd(F32), sd(F32)],
        compiler_params=_params("parallel", vmem=VMEM_LIMIT_TOKEN_OUT),
    )(o, parts, parts, parts, ya, x, mod, hnw, wa, wb, wo)


def _token_out_bwd(dx, tmo, pa, pb, o, parts, mod, hnw, wa, wb, wo, ctx_rows, name):
    T, D = dx.shape
    nh = D // HEAD
    cb = ctx_rows // TBT

    def body(dx_ref, tmo_ref, pa_ref, pb_ref, o_ref, og_ref, ga_ref, gb_ref, mod_ref, hnw_ref, wa_ref, wb_ref, wo_ref,
             dout_ref, dpa_ref, dpb_ref, dgate_ref, dya_ref, do_ref, dg1_ref, dhnw_ref):
        i = pl.program_id(0)

        @pl.when(i == 0)
        def _():
            dhnw_ref[...] = jnp.zeros_like(dhnw_ref)

        @pl.when((i == 0) | (i == cb))
        def _():
            dg1_ref[...] = jnp.zeros_like(dg1_ref)

        dxv = dx_ref[...]
        dg1_ref[...] += jnp.sum(dxv * tmo_ref[...], axis=0, keepdims=True)
        dout = (dxv * mod_ref[2:3, :]).astype(BF16)
        dout_ref[...] = dout
        dmg = _nt(dout, _full(wo_ref))
        sa = _sigmoid(ga_ref[...])
        sb = _sigmoid(gb_ref[...])
        dpa = (dmg * sa).astype(BF16)
        dpb = (dmg * sb).astype(BF16)
        dpa_ref[...] = dpa
        dpb_ref[...] = dpb
        dgate_ref[:, D:2 * D] = (dmg * pa_ref[...] * sa * (1.0 - sa)).astype(BF16)
        dgate_ref[:, 2 * D:3 * D] = (dmg * pb_ref[...] * sb * (1.0 - sb)).astype(BF16)
        dya_ref[...] = _nt(dpa, _full(wa_ref))
        dyb = _nt(dpb, _full(wb_ref))
        so, dso = _silu_both(og_ref[...])
        ov = o_ref[...]
        nw = hnw_ref[...]
        dnw = jnp.zeros((1, HEAD), F32)
        for h in range(nh):
            sl = slice(h * HEAD, (h + 1) * HEAD)
            seg = ov[:, sl]
            r = lax.rsqrt(jnp.mean(seg * seg, axis=-1, keepdims=True) + RMS_EPS)
            oh = seg * r
            dn = dyb[:, sl] * so[:, sl]
            dgate_ref[:, sl] = (dyb[:, sl] * oh * nw * dso[:, sl]).astype(BF16)
            dnw = dnw + jnp.sum(dn * oh, axis=0, keepdims=True)
            doh = dn * nw
            do_ref[:, sl] = r * (doh - oh * jnp.mean(doh * oh, axis=-1, keepdims=True))
        dhnw_ref[...] += dnw

    row = lambda col: pl.BlockSpec((TBT, D), lambda i: (i, col))
    wsp = _rows_weight_spec(wa)
    sd = lambda dt: jax.ShapeDtypeStruct((T, D), dt)
    return pl.pallas_call(
        body, name=name, grid=(T // TBT,),
        in_specs=[row(0), row(0), row(0), row(0), row(0), row(6), row(7), row(8),
                  pl.BlockSpec((None, N_MOD, D), lambda i: (_stream_of(i, cb), 0, 0)),
                  pl.BlockSpec((1, HEAD), lambda i: (0, 0)), wsp, wsp, wsp],
        out_specs=[row(0)] * 3 + [pl.BlockSpec((TBT, 3 * D), lambda i: (i, 2)), row(0), row(0),
                                  pl.BlockSpec((None, 1, D), lambda i: (_stream_of(i, cb), 0, 0)),
                                  pl.BlockSpec((1, HEAD), lambda i: (0, 0))],
        out_shape=[sd(BF16)] * 3 + [jax.ShapeDtypeStruct((T, 9 * D), BF16), sd(F32), sd(F32),
                                    jax.ShapeDtypeStruct((2, 1, D), F32), jax.ShapeDtypeStruct((1, HEAD), F32)],
        compiler_params=_params("arbitrary", vmem=VMEM_LIMIT_TOKEN_OUT),
    )(dx, tmo, pa, pb, o, parts, parts, parts, mod, hnw, wa, wb, wo)


def _conv_geometry(i, nb, cb):
    is_ctx = i < cb
    first = (i == 0) | (i == cb)
    last = (i == cb - 1) | (i == nb - 1)
    row = lax.broadcasted_iota(jnp.int32, (TB + 2 * GRID_W, 1), 0)
    w = row & (GRID_W - 1)
    left_ok = (w != 0) | is_ctx
    right_ok = (w != GRID_W - 1) | is_ctx
    return is_ctx, first, last, left_ok, right_ok


def _ext(p_ref, m_ref, n_ref, first, last):
    return jnp.concatenate([jnp.where(first, 0.0, p_ref[...]), m_ref[...], jnp.where(last, 0.0, n_ref[...])], axis=0)


def _shift_prev(e, ok):
    return jnp.where(ok, pltpu.roll(e, 1, 0), 0.0)


def _shift_next(e, ok):
    return jnp.where(ok, pltpu.roll(e, e.shape[0] - 1, 0), 0.0)


def _halo_specs(cbk, n64, coff=0):
    r = TB // GRID_W
    prev = pl.BlockSpec((GRID_W, cbk), lambda j, i: (jnp.maximum(r * i - 1, 0), j + coff))
    main = pl.BlockSpec((TB, cbk), lambda j, i: (i, j + coff))
    nxt = pl.BlockSpec((GRID_W, cbk), lambda j, i: (jnp.minimum(r * i + r, n64 - 1), j + coff))
    return [prev, main, nxt]


def _conv_cblock(dff):
    return _tile(dff, 1408)


def _conv_fwd(up, cw, cbias, ctx_rows, name):
    T, dff = up.shape[0], up.shape[1] // 2
    cbk = _conv_cblock(dff)
    nb, cb = T // TB, ctx_rows // TB
    nvb = dff // cbk

    def body(ap_ref, a_ref, an_ref, v_ref, cw_ref, cb_ref, ac_ref, act_ref):
        i = pl.program_id(1)
        is_ctx, first, last, lok, rok = _conv_geometry(i, nb, cb)
        e = _ext(ap_ref, a_ref, an_ref, first, last)
        el = _shift_prev(e, lok)
        er = _shift_next(e, rok)
        cwv = cw_ref[...]

        def comb(dr, lo):
            sl = slice(lo, lo + TB)
            return cwv[3 * dr:3 * dr + 1] * el[sl] + cwv[3 * dr + 1:3 * dr + 2] * e[sl] + cwv[3 * dr + 2:3 * dr + 3] * er[sl]

        out = comb(1, GRID_W) + jnp.where(is_ctx, 0.0, comb(0, 0) + comb(2, 2 * GRID_W))
        a_c = out + cb_ref[...]
        ac_ref[...] = a_c
        act_ref[...] = (_gelu(a_c) * v_ref[...]).astype(BF16)

    main = pl.BlockSpec((TB, cbk), lambda j, i: (i, j))
    return pl.pallas_call(
        body, name=name, grid=(dff // cbk, nb),
        in_specs=_halo_specs(cbk, T // GRID_W) + [pl.BlockSpec((TB, cbk), lambda j, i: (i, j + nvb)),
                                                 pl.BlockSpec((9, cbk), lambda j, i: (0, j)),
                                                 pl.BlockSpec((1, cbk), lambda j, i: (0, j))],
        out_specs=[main, main],
        out_shape=[jax.ShapeDtypeStruct((T, dff), F32), jax.ShapeDtypeStruct((T, dff), BF16)],
        compiler_params=_params("parallel", "parallel"),
    )(up, up, up, up, cw, cbias)


def _conv_bwd(up, ac, dact, cw, ctx_rows, name):
    T, dff = up.shape[0], up.shape[1] // 2
    cbk = _conv_cblock(dff)
    nb, cb = T // TB, ctx_rows // TB
    nvb = dff // cbk

    def body(ap_ref, a_ref, an_ref, vp_ref, v_ref, vn_ref, cp_ref, c_ref, cn_ref, dp_ref, d_ref, dn_ref, cw_ref,
             da_ref, dv_ref, dcw_ref, dcb_ref):
        i = pl.program_id(1)

        @pl.when(i == 0)
        def _():
            dcw_ref[...] = jnp.zeros_like(dcw_ref)
            dcb_ref[...] = jnp.zeros_like(dcb_ref)

        is_ctx, first, last, lok, rok = _conv_geometry(i, nb, cb)
        gl, dgl = _gelu_both(_ext(cp_ref, c_ref, cn_ref, first, last))
        g = _ext(dp_ref, d_ref, dn_ref, first, last) * _ext(vp_ref, v_ref, vn_ref, first, last) * dgl
        dv_ref[...] = (d_ref[...] * gl[GRID_W:GRID_W + TB]).astype(BF16)
        gm = _shift_prev(g, lok)
        gp = _shift_next(g, rok)
        cwv = cw_ref[...]

        def comb(dr, lo):
            sl = slice(lo, lo + TB)
            return cwv[3 * dr:3 * dr + 1] * gp[sl] + cwv[3 * dr + 1:3 * dr + 2] * g[sl] + cwv[3 * dr + 2:3 * dr + 3] * gm[sl]

        da = comb(1, GRID_W) + jnp.where(is_ctx, 0.0, comb(0, 2 * GRID_W) + comb(2, 0))
        da_ref[...] = da.astype(BF16)
        e = _ext(ap_ref, a_ref, an_ref, first, last)
        taps = [_shift_prev(e, lok), e, _shift_next(e, rok)]
        gmain = g[GRID_W:GRID_W + TB]
        dcb_ref[...] += jnp.sum(gmain, axis=0, keepdims=True)
        vert = jnp.where(is_ctx, 0.0, 1.0)
        for dr in range(3):
            sl = slice(dr * GRID_W, dr * GRID_W + TB)
            for dw in range(3):
                s = jnp.sum(gmain * taps[dw][sl], axis=0, keepdims=True)
                if dr != 1:
                    s = s * vert
                k = 3 * dr + dw
                dcw_ref[k:k + 1, :] += s

    main = pl.BlockSpec((TB, cbk), lambda j, i: (i, j))
    halo = _halo_specs(cbk, T // GRID_W)
    acc9 = pl.BlockSpec((9, cbk), lambda j, i: (0, j))
    acc1 = pl.BlockSpec((1, cbk), lambda j, i: (0, j))
    return pl.pallas_call(
        body, name=name, grid=(dff // cbk, nb),
        in_specs=halo + _halo_specs(cbk, T // GRID_W, nvb) + halo + halo + [acc9],
        out_specs=[main, main, acc9, acc1],
        out_shape=[jax.ShapeDtypeStruct((T, dff), BF16), jax.ShapeDtypeStruct((T, dff), BF16),
                   jax.ShapeDtypeStruct((9, dff), F32), jax.ShapeDtypeStruct((1, dff), F32)],
        compiler_params=_params("parallel", "arbitrary"),
    )(up, up, up, up, up, up, ac, ac, ac, dact, dact, dact, cw)


def _ffn_out_fwd(act, xm, mod, wd, ctx_rows, name):
    T, D = xm.shape
    dff = act.shape[1]
    cb = ctx_rows // TB

    def body(act_ref, x_ref, mod_ref, w_ref, xo_ref, fo_ref):
        out = _nn(act_ref[...], _full(w_ref))
        fo_ref[...] = out
        xo_ref[...] = x_ref[...] + mod_ref[5:6, :] * out

    row = pl.BlockSpec((TB, D), lambda i: (i, 0))
    return pl.pallas_call(
        body, name=name, grid=(T // TB,),
        in_specs=[pl.BlockSpec((TB, dff), lambda i: (i, 0)), row,
                  pl.BlockSpec((None, N_MOD, D), lambda i: (_stream_of(i, cb), 0, 0)),
                  _rows_weight_spec(wd)],
        out_specs=[row, row],
        out_shape=[jax.ShapeDtypeStruct((T, D), F32), jax.ShapeDtypeStruct((T, D), F32)],
        compiler_params=_params("parallel"),
    )(act, xm, mod, wd)


def _ffn_out_bwd(dx, fo, mod, wd, ctx_rows, name):
    T, D = dx.shape
    dff = N_CHIPS * wd.shape[1]
    cb = ctx_rows // TB

    def body(dx_ref, fo_ref, mod_ref, w_ref, dout_ref, dact_ref, dg2_ref):
        i = pl.program_id(0)

        @pl.when((i == 0) | (i == cb))
        def _():
            dg2_ref[...] = jnp.zeros_like(dg2_ref)

        dxv = dx_ref[...]
        dg2_ref[...] += jnp.sum(dxv * fo_ref[...], axis=0, keepdims=True)
        dout = (dxv * mod_ref[5:6, :]).astype(BF16)
        dout_ref[...] = dout
        dact_ref[...] = _nt(dout, _full(w_ref))

    row = pl.BlockSpec((TB, D), lambda i: (i, 0))
    return pl.pallas_call(
        body, name=name, grid=(T // TB,),
        in_specs=[row, row, pl.BlockSpec((None, N_MOD, D), lambda i: (_stream_of(i, cb), 0, 0)),
                  _rows_weight_spec(wd)],
        out_specs=[row, pl.BlockSpec((TB, dff), lambda i: (i, 0)),
                   pl.BlockSpec((None, 1, D), lambda i: (_stream_of(i, cb), 0, 0))],
        out_shape=[jax.ShapeDtypeStruct((T, D), BF16), jax.ShapeDtypeStruct((T, dff), F32),
                   jax.ShapeDtypeStruct((2, 1, D), F32)],
        compiler_params=_params("arbitrary"),
    )(dx, fo, mod, wd)


def _loss_bwd(x, target, fw, ctx_rows, name):
    T, D = x.shape
    cb = ctx_rows // TB

    def body(x_ref, t_ref, fw_ref, dx_ref, loss_ref, dfw_ref):
        i = pl.program_id(0)

        @pl.when(i == 0)
        def _():
            loss_ref[...] = jnp.zeros_like(loss_ref)
            dfw_ref[...] = jnp.zeros_like(dfw_ref)

        @pl.when(i < cb)
        def _():
            dx_ref[...] = jnp.zeros_like(dx_ref)

        @pl.when(i >= cb)
        def _():
            xv = x_ref[...]
            r = lax.rsqrt(jnp.mean(xv * xv, axis=-1, keepdims=True) + RMS_EPS)
            xh = xv * r
            fwv = fw_ref[...]
            err = xh * fwv - t_ref[...]
            loss_ref[...] += (0.5 / D) * jnp.sum(err * err)
            dy = err * (1.0 / D)
            dfw_ref[...] += jnp.sum(dy * xh, axis=0, keepdims=True)
            dxh = dy * fwv
            dx_ref[...] = r * (dxh - xh * jnp.mean(dxh * xh, axis=-1, keepdims=True))

    row = pl.BlockSpec((TB, D), lambda i: (i, 0))
    return pl.pallas_call(
        body, name=name, grid=(T // TB,),
        in_specs=[row, pl.BlockSpec((TB, D), lambda i: (jnp.maximum(i - cb, 0), 0)), pl.BlockSpec((1, D), lambda i: (0, 0))],
        out_specs=[row, pl.BlockSpec((1, 128), lambda i: (0, 0)), pl.BlockSpec((1, D), lambda i: (0, 0))],
        out_shape=[jax.ShapeDtypeStruct((T, D), F32), jax.ShapeDtypeStruct((1, 128), F32),
                   jax.ShapeDtypeStruct((1, D), F32)],
        compiler_params=_params("arbitrary"),
    )(x, target, fw)


def _adamw(w, gs, m, v, name):
    L, R, C = w.shape
    assert len(gs) == L
    rb = _rows_tile(R, max(16, (1 << 18) // C // 16 * 16))
    bc1 = 1.0 - ADAM_B1 ** ADAM_STEP
    bc2 = 1.0 - ADAM_B2 ** ADAM_STEP

    def body(w_ref, m_ref, v_ref, *rest):
        g_refs, (g_ref, d_ref, nm_ref, nv_ref) = rest[:L], rest[L:]
        layer = pl.program_id(0)
        for li in range(L):
            @pl.when(layer == li)
            def _():
                gv = g_refs[li][...]
                g_ref[...] = gv
                nm = ADAM_B1 * m_ref[...] + (1.0 - ADAM_B1) * gv
                nv = ADAM_B2 * v_ref[...] + (1.0 - ADAM_B2) * (gv * gv)
                nm_ref[...] = nm
                nv_ref[...] = nv
                d_ref[...] = -ADAM_LR * ((nm / bc1) / (jnp.sqrt(nv / bc2) + ADAM_EPS) + ADAM_WD * w_ref[...])

    blk = pl.BlockSpec((None, rb, C), lambda l, i: (l, i, 0))
    gblk = pl.BlockSpec((rb, C), lambda l, i: (i, 0))
    sd = jax.ShapeDtypeStruct((L, R, C), F32)
    return pl.pallas_call(
        body, name=name, grid=(L, R // rb), in_specs=[blk] * 3 + [gblk] * L, out_specs=[blk] * 4, out_shape=[sd] * 4,
        compiler_params=_params("parallel", "parallel"),
    )(w, m, v, *gs)


def _local_step(xs, cv, target, W, layer_weights, on_layer_grads, ctx_rows):
    T, D = xs.shape
    depth = W["norm1_w"].shape[0]
    saved = []
    X = xs
    for l in range(depth):
        s = {}
        Wl = layer_weights(l, X)
        mod_all, sa = _mod_fwd(cv, Wl["ada_w"], W["ada_b"][l][None, :] + Wl["token"], f"mod_fwd_{l}")
        mod = mod_all[:2].reshape(2, N_MOD, D)
        h1 = _norm_mod(X, W["norm1_w"][l][None, :], mod, 0, ctx_rows, f"norm1_{l}")
        parts = _mm_nn_w(h1, Wl["w_in"], F32, f"in_proj_{l}")
        o_f, st_f = _hgrn_fwd(parts, W["hlb"], l, False, ctx_rows, f"hgrn_fwd_f_{l}")
        o, st_b = _hgrn_fwd(parts, W["hlb"], l, True, ctx_rows, f"hgrn_fwd_b_{l}", o_add=o_f)
        ya = _sgu_fwd(parts, W["sgu_ln_w"][l][None, :], W["sgu_ln_b"][l][None, :], W["sgu_w"][l], W["sgu_bt"][l],
                      f"sgu_fwd_{l}")
        Wl.update(Wl.pop("late")(ya))
        yb, pa, pb, mg, tmo, xm = _token_out_fwd(o, parts, ya, X, mod, W["hnw"][l][None, :] + Wl["late_token"], Wl["w_a"],
                                                 Wl["w_b"], Wl["w_o"], ctx_rows, f"token_out_fwd_{l}")
        h2 = _norm_mod(xm, W["norm2_w"][l][None, :], mod, 3, ctx_rows, f"norm2_{l}")
        up = _mm_nn_w(h2, Wl["w_up"], F32, f"up_proj_{l}")
        ac, act = _conv_fwd(up, Wl["conv_w"], W["conv_b"][l][None, :], ctx_rows, f"conv_fwd_{l}")
        xo, fo = _ffn_out_fwd(act, xm, mod, Wl["w_down"], ctx_rows, f"ffn_out_fwd_{l}")
        s.update(X=X, Wl=Wl, mod=mod, mod_all=mod_all, sa=sa, h1=h1, parts=parts, o=o, st_f=st_f, st_b=st_b, ya=ya, yb=yb,
                 pa=pa, pb=pb, mg=mg, tmo=tmo, xm=xm, h2=h2, up=up, ac=ac, act=act, fo=fo)
        saved.append(s)
        X = xo

    dX, loss_row, dfw = _loss_bwd(X, target, W["final_norm_w"][None, :], ctx_rows, "loss_bwd")
    G = {k: [None] * depth for k in ("ada_b", "norm1_w", "sgu_ln_w", "sgu_ln_b", "sgu_w", "sgu_b", "hlb1", "hnw", "norm2_w",
                                     "conv_w", "conv_b", "dmod")}
    dcv = jnp.zeros_like(cv)
    for l in reversed(range(depth)):
        s = saved[l]
        mod, Wl = s["mod"], s["Wl"]
        big = {}
        dout2, dact, dg2 = _ffn_out_bwd(dX, s["fo"], mod, Wl["w_down"], ctx_rows, f"ffn_out_bwd_{l}")
        big["w_down"] = _mm_tn(s["act"], dout2, F32, f"dw_down_{l}")
        da, dv, dcw, dcb = _conv_bwd(s["up"], s["ac"], dact, Wl["conv_w"], ctx_rows, f"conv_bwd_{l}")
        G["conv_w"][l], G["conv_b"][l] = dcw, dcb[0]
        dup = jnp.concatenate([da, dv], axis=1)
        big["w_up"] = _mm_tn(s["h2"], dup, F32, f"dw_up_{l}", out_chips=True)
        dh2 = _mm_nt_w(dup, Wl["w_up"], F32, f"dh2_{l}")
        dxm, dm2, dnw2 = _norm_mod_bwd(dh2, s["xm"], dX, W["norm2_w"][l][None, :], mod, 3, ctx_rows, f"norm2_bwd_{l}")
        G["norm2_w"][l] = dnw2[0]
        (dout1, dpa, dpb, dparts, dya, do, dg1, dhnw) = _token_out_bwd(
            dxm, s["tmo"], s["pa"], s["pb"], s["o"], s["parts"], mod, W["hnw"][l][None, :], Wl["w_a"], Wl["w_b"], Wl["w_o"],
            ctx_rows, f"token_out_bwd_{l}")
        G["hnw"][l] = dhnw[0]
        big["w_o"] = _mm_tn(s["mg"], dout1, F32, f"dw_o_{l}")
        big["w_a"] = _mm_tn(s["ya"], dpa, F32, f"dw_a_{l}")
        big["w_b"] = _mm_tn(s["yb"], dpb, F32, f"dw_b_{l}")
        tok = on_layer_grads(l, "early", big)
        dparts, dsw, dsbt, dlnw, dlnb = _sgu_bwd(s["parts"], dya, W["sgu_ln_w"][l][None, :], W["sgu_ln_b"][l][None, :] + tok,
                                                 W["sgu_w"][l], W["sgu_bt"][l], dparts, f"sgu_bwd_{l}")
        G["sgu_w"][l], G["sgu_b"][l], G["sgu_ln_w"][l], G["sgu_ln_b"][l] = dsw, dsbt.T, dlnw[0], dlnb[0]
        dq_f, dz_f, di_f, dlb_f = _hgrn_bwd(s["parts"], W["hlb"], do, s["st_f"], l, False, ctx_rows, f"hgrn_bwd_f_{l}")
        dparts, dlb_b = _hgrn_bwd(s["parts"], W["hlb"], do, s["st_b"], l, True, ctx_rows, f"hgrn_bwd_b_{l}",
                                  other=(dq_f, dz_f, di_f), dparts=dparts)
        G["hlb1"][l] = jnp.concatenate([dlb_f[0], dlb_b[0]])
        tok = on_layer_grads(l, "late", {"w_in": _mm_tn(s["h1"], dparts, F32, f"dw_in_{l}", out_chips=True)})
        dh1 = _mm_nt_w(dparts, Wl["w_in"], F32, f"dh1_{l}")
        dX, dm1, dnw1 = _norm_mod_bwd(dh1, s["X"], dxm, W["norm1_w"][l][None, :] + tok, mod, 0, ctx_rows, f"norm1_bwd_{l}")
        G["norm1_w"][l] = dnw1[0]
        dmod = jnp.concatenate([dm1, dg1, dm2, dg2], axis=1).reshape(2, N_MOD * D)
        dmod16 = jnp.concatenate([dmod, jnp.zeros((cv.shape[0] - 2, N_MOD * D), F32)], axis=0)
        G["ada_b"][l] = dmod[0] + dmod[1]
        G["dmod"][l] = dmod
        dcv = dcv + _cvec_bwd(dmod16, Wl["ada_w"], cv, f"dcvec_{l}")
    G["c_ctx"] = dcv[0]
    G["final_norm_w"] = dfw[0]
    return loss_row[0, 0], dX, G, saved[0]["sa"]


def _chip_peers(x, y, c):
    return [((1 - x, y, c), 2 * (1 - x) + y), ((x, 1 - y, c), 2 * x + 1 - y), ((1 - x, 1 - y, c), 2 * (1 - x) + 1 - y)]


def _rdma_call(ins, out_shapes, plan, n_remote, n_local, name, aliases=None):
    n_in, n_out = len(ins), len(out_shapes)

    def body(*refs):
        in_refs, out_refs = refs[:n_in], refs[n_in:n_in + n_out]
        send_sems, recv_sems, local_sems = refs[n_in + n_out:]
        x, y, c = lax.axis_index("x"), lax.axis_index("y"), lax.axis_index("c")
        remote, local = plan(in_refs, out_refs, x, y, c)
        assert len(remote) == n_remote and len(local) == n_local, (name, len(remote), len(local))
        copies = [pltpu.make_async_copy(s, d, local_sems.at[i]) for i, (s, d) in enumerate(local)]
        copies += [pltpu.make_async_remote_copy(src_ref=s, dst_ref=d, send_sem=send_sems.at[k], recv_sem=recv_sems.at[k],
                                                device_id=dev, device_id_type=pl.DeviceIdType.MESH)
                   for k, (s, d, dev) in enumerate(remote)]
        for cp in copies:
            cp.start()
        for cp in copies:
            cp.wait()

    hbm = pl.BlockSpec(memory_space=pltpu.HBM)
    return pl.pallas_call(
        body, name=name, in_specs=[hbm] * n_in, out_specs=[hbm] * n_out, out_shape=out_shapes,
        scratch_shapes=[pltpu.SemaphoreType.DMA((n_remote,)), pltpu.SemaphoreType.DMA((n_remote,)),
                        pltpu.SemaphoreType.DMA((max(n_local, 1),))],
        input_output_aliases=aliases or {},
    )(*ins)


DMA_PIECE_BYTES = 1 << 18
DMA_MAX_PIECES = 8


def _row_pieces(shape, dtype):
    rows = shape[0]
    row_bytes = jnp.dtype(dtype).itemsize
    for d in shape[1:]:
        row_bytes *= d
    n = 1
    while n < DMA_MAX_PIECES and rows % (2 * n * 16) == 0 and rows * row_bytes // (2 * n) >= DMA_PIECE_BYTES:
        n *= 2
    return [(i * (rows // n), rows // n) for i in range(n)]


def _half_pieces(o, c):
    r2 = o.shape[1] // 2
    return [pl.ds(c * r2 + st, sz) for st, sz in _row_pieces((r2,) + o.shape[2:], o.dtype)]


def _n_half_pieces(arrays):
    return sum(len(_row_pieces((a.shape[1] // 2,) + a.shape[2:], a.dtype)) for a in arrays)


def _plan_gather_far(lands, x, y, c):
    me = 2 * x + y
    return [(o.at[me, rows], o.at[me, rows], dev) for dev, _ in _chip_peers(x, y, c) for o in lands
            for rows in _half_pieces(o, c)]


def _plan_gather_near(lands, x, y, c):
    return [(o.at[idx, rows], o.at[idx, rows], (x, y, 1 - c)) for _, idx in _chip_peers(x, y, c) for o in lands
            for rows in _half_pieces(o, c)]


def _gather_weights(lands, name):
    n = len(lands)
    n_far = (N_CHIPS - 1) * _n_half_pieces(lands)

    def body(*refs):
        outs = refs[n:2 * n]
        far_send, far_recv, near_send, near_recv = refs[2 * n:]
        x, y, c = lax.axis_index("x"), lax.axis_index("y"), lax.axis_index("c")
        mk = lambda plan, send, recv: [
            pltpu.make_async_remote_copy(src_ref=s, dst_ref=d, send_sem=send.at[k], recv_sem=recv.at[k], device_id=dev,
                                         device_id_type=pl.DeviceIdType.MESH)
            for k, (s, d, dev) in enumerate(plan(outs, x, y, c))]
        far, near = mk(_plan_gather_far, far_send, far_recv), mk(_plan_gather_near, near_send, near_recv)
        assert len(far) == n_far and len(near) == n_far
        for cp in far:
            cp.start()
        for k in range(n_far):
            far[k].wait_recv()
            near[k].start()
        for k in range(n_far):
            near[k].wait_recv()
        for cp in far + near:
            cp.wait_send()

    hbm = pl.BlockSpec(memory_space=pltpu.HBM)
    sems = pltpu.SemaphoreType.DMA((n_far,))
    return pl.pallas_call(
        body, name=name, in_specs=[hbm] * n, out_specs=[hbm] * n,
        out_shape=[jax.ShapeDtypeStruct(a.shape, a.dtype) for a in lands],
        scratch_shapes=[sems, sems, sems, sems], input_output_aliases={i: i for i in range(n)},
    )(*lands)


def _gather_all(v, name):
    def plan(ins, outs, x, y, c):
        (s,), (o,) = ins, outs
        me = 4 * x + 2 * y + c
        flip = lambda a, f: 1 - a if f else a
        remote = [(s, o.at[me], (flip(x, m & 4), flip(y, m & 2), flip(c, m & 1))) for m in range(1, 8)]
        return remote, [(s, o.at[me])]

    return _rdma_call([v], [jax.ShapeDtypeStruct((8,) + v.shape, v.dtype)], plan, 7, 1, name)[0]


def _reduce_pair(parts, name):
    def plan(ins, outs, x, y, c):
        return [(a.at[j, 1 - c, pl.ds(st, sz)], o.at[j, pl.ds(st, sz)], (x, y, 1 - c)) for a, o in zip(ins, outs)
                for j in range(N_CHIPS) for st, sz in _row_pieces(a.shape[2:], a.dtype)], []

    shapes = [jax.ShapeDtypeStruct((N_CHIPS,) + a.shape[2:], a.dtype) for a in parts]
    n_remote = N_CHIPS * sum(len(_row_pieces(a.shape[2:], a.dtype)) for a in parts)
    return _rdma_call(parts, shapes, plan, n_remote, 0, name)


def _plan_chips(ins, lands, x, y, c):
    me = 2 * x + y
    return [(a.at[idx, pl.ds(st, sz)], o.at[me, pl.ds(st, sz)], dev) for dev, idx in _chip_peers(x, y, c)
            for a, o in zip(ins, lands) for st, sz in _row_pieces(a.shape[1:], a.dtype)]


def _n_chips_copies(parts):
    return (N_CHIPS - 1) * sum(len(_row_pieces(a.shape[1:], a.dtype)) for a in parts)


def _reduce_chips(parts, name):
    shapes = [jax.ShapeDtypeStruct(a.shape, a.dtype) for a in parts]
    return _rdma_call(parts, shapes, lambda ins, outs, x, y, c: (_plan_chips(ins, outs, x, y, c), []),
                      _n_chips_copies(parts), 0, name)


def _gather_pair(halves, name):
    def plan(ins, outs, x, y, c):
        return [(o.at[c, pl.ds(st, sz)], o.at[c, pl.ds(st, sz)], (x, y, 1 - c)) for o in outs
                for st, sz in _row_pieces(o.shape[1:], o.dtype)], []

    shapes = [jax.ShapeDtypeStruct(a.shape, a.dtype) for a in halves]
    n_remote = sum(len(_row_pieces(a.shape[1:], a.dtype)) for a in halves)
    return _rdma_call(halves, shapes, plan, n_remote, 0, name, aliases={i: i for i in range(len(halves))})


def _split_start(ins, lands, plan, n_remote, name):
    n_buf = len(ins) + len(lands)

    def body(*refs):
        in_refs, land_refs = refs[:len(ins)], refs[len(ins):n_buf]
        send_sems, recv_sems, token = refs[n_buf], refs[n_buf + 1], refs[-1]
        x, y, c = lax.axis_index("x"), lax.axis_index("y"), lax.axis_index("c")
        remote = plan(in_refs, land_refs, x, y, c)
        assert len(remote) == n_remote, (name, len(remote))
        for k, (s, d, dev) in enumerate(remote):
            pltpu.make_async_remote_copy(src_ref=s, dst_ref=d, send_sem=send_sems.at[k], recv_sem=recv_sems.at[k],
                                         device_id=dev, device_id_type=pl.DeviceIdType.MESH).start()
        token[...] = jnp.zeros_like(token)

    hbm = pl.BlockSpec(memory_space=pltpu.HBM)
    sem = pl.BlockSpec(memory_space=pltpu.SEMAPHORE)
    bufs = list(ins) + list(lands)
    out = pl.pallas_call(
        body, name=name, in_specs=[hbm] * n_buf,
        out_specs=(sem, sem) + (hbm,) * n_buf + (pl.BlockSpec(memory_space=pltpu.VMEM),),
        out_shape=(pltpu.SemaphoreType.DMA((n_remote,)), pltpu.SemaphoreType.DMA((n_remote,)))
        + tuple(pltpu.HBM(a.shape, a.dtype) for a in bufs) + (jax.ShapeDtypeStruct((8, 128), F32),),
        input_output_aliases={i: 2 + i for i in range(n_buf)},
        compiler_params=pltpu.CompilerParams(has_side_effects=pltpu.SideEffectType.DATAFLOW_SIDE_EFFECTING),
    )(*[pltpu.with_memory_space_constraint(a, pltpu.HBM) for a in bufs])
    return dict(send=out[0], recv=out[1], ins=list(out[2:2 + len(ins)]), lands=list(out[2 + len(ins):2 + n_buf]),
                token=out[-1][0, 0], plan=plan, n_remote=n_remote)


def _split_wait(st, after, name):
    n_in, n_buf = len(st["ins"]), len(st["ins"]) + len(st["lands"])
    plan, n_remote = st["plan"], st["n_remote"]

    def body(*refs):
        in_refs, land_refs = refs[:n_in], refs[n_in:n_buf]
        send_sems, recv_sems = refs[n_buf], refs[n_buf + 1]
        x, y, c = lax.axis_index("x"), lax.axis_index("y"), lax.axis_index("c")
        for k, (s, d, dev) in enumerate(plan(in_refs, land_refs, x, y, c)):
            cp = pltpu.make_async_remote_copy(src_ref=s, dst_ref=d, send_sem=send_sems.at[k], recv_sem=recv_sems.at[k],
                                              device_id=dev, device_id_type=pl.DeviceIdType.MESH)
            cp.wait_send()
            cp.wait_recv()

    hbm = pl.BlockSpec(memory_space=pltpu.HBM)
    sem = pl.BlockSpec(memory_space=pltpu.SEMAPHORE)
    bufs = st["ins"] + st["lands"]
    out = pl.pallas_call(
        body, name=name, in_specs=[hbm] * n_buf + [sem, sem, pl.BlockSpec(memory_space=pl.ANY)],
        out_specs=[hbm] * n_buf, out_shape=[pltpu.HBM(a.shape, a.dtype) for a in bufs],
        input_output_aliases={i: i for i in range(n_buf)},
        compiler_params=pltpu.CompilerParams(has_side_effects=pltpu.SideEffectType.DATAFLOW_SIDE_EFFECTING),
    )(*bufs, st["send"], st["recv"], after)
    return list(out[:n_in]), list(out[n_in:])


def _pair_forward(lands, name):
    shapes = [jax.ShapeDtypeStruct(a.shape, a.dtype) for a in lands]
    return _rdma_call(lands, shapes, lambda ins, outs, x, y, c: (_plan_gather_near(outs, x, y, c), []),
                      (N_CHIPS - 1) * _n_half_pieces(lands), 0, name, aliases={i: i for i in range(len(lands))})


def _sum_block_rows(r, C):
    return _rows_tile(r, max(16, (1 << 18) // C // 16 * 16))


def _sum_pair(a, recv, cidx, name):
    nch, _, r, C = a.shape
    rb = _sum_block_rows(r, C)

    def body(c_ref, a_ref, r_ref, o_ref):
        o_ref[...] = (a_ref[...] + r_ref[...]).astype(BF16)

    blk = pl.BlockSpec((None, rb, C), lambda j, i, c: (j, i, 0))
    return pl.pallas_call(
        body, name=name,
        grid_spec=pltpu.PrefetchScalarGridSpec(
            num_scalar_prefetch=1, grid=(nch, r // rb),
            in_specs=[pl.BlockSpec((None, None, rb, C), lambda j, i, c: (j, c[0], i, 0)), blk], out_specs=blk),
        out_shape=jax.ShapeDtypeStruct((nch, r, C), BF16),
        compiler_params=_params("parallel", "parallel"),
    )(cidx, a, recv)


def _sum_chips(mine, recv, ids, name):
    nch, r, C = recv.shape
    rb = _sum_block_rows(r, C)

    def body(ids_ref, m_ref, *rest):
        r_refs, o_ref = rest[:nch], rest[nch]
        chip = ids_ref[1]
        own = m_ref[...].astype(F32)
        acc = jnp.where(chip == 0, own, r_refs[0][...].astype(F32))
        for q in range(1, nch):
            acc = acc + jnp.where(chip == q, own, r_refs[q][...].astype(F32))
        o_ref[...] = acc

    def slot(q):
        return pl.BlockSpec((None, rb, C), lambda i, ids: (jnp.where(ids[1] == q, (q + 1) % nch, q), i, 0))

    return pl.pallas_call(
        body, name=name,
        grid_spec=pltpu.PrefetchScalarGridSpec(
            num_scalar_prefetch=1, grid=(r // rb,),
            in_specs=[pl.BlockSpec((None, rb, C), lambda i, ids: (ids[1], i, 0))] + [slot(q) for q in range(nch)],
            out_specs=pl.BlockSpec((None, rb, C), lambda i, ids: (ids[0], i, 0))),
        out_shape=jax.ShapeDtypeStruct((N_CORES, r, C), F32),
        compiler_params=_params("parallel"),
    )(ids, mine, *([recv] * nch))


PACK_COLS = 1024
_SHARDED = ("ada_w", "w_in", "w_branch_a", "w_branch_b", "w_out", "ffn_w_up", "ffn_w_down")
_LAYER_KEYS = ("ada_w", "w_in", "w_a", "w_b", "w_o", "w_up", "w_down")
_SMALL = ("c_ctx", "ada_b", "norm1_w", "sgu_ln_w", "sgu_ln_b", "sgu_w", "sgu_b", "hgrn_lower_bounds", "hgrn_norm_w",
          "norm2_w", "ffn_conv_b", "final_norm_w")
_ORDER = ("c_ctx", "ada_w", "ada_b", "norm1_w", "w_in", "sgu_ln_w", "sgu_ln_b", "sgu_w", "sgu_b", "hgrn_lower_bounds",
          "hgrn_norm_w", "w_branch_a", "w_branch_b", "w_out", "norm2_w", "ffn_w_up", "ffn_conv_w", "ffn_conv_b",
          "ffn_w_down", "final_norm_w")


def _pad_to(v, n):
    return jnp.concatenate([v, jnp.zeros((n - v.shape[0],), v.dtype)]) if v.shape[0] < n else v


def _round_up(n, m):
    return (n + m - 1) // m * m


def _pack(arrays, n_pad):
    flat = jnp.concatenate([a.reshape(-1) for a in arrays])
    return _pad_to(flat, n_pad)


def _unpack(flat, like):
    out, off = [], 0
    for a in like:
        out.append(flat[off:off + a.size].reshape(a.shape))
        off += a.size
    return out


def kernel(x, c, ctx, c_ctx, ada_w, ada_b, norm1_w, w_in, sgu_ln_w, sgu_ln_b, sgu_w, sgu_b, hgrn_lower_bounds, hgrn_norm_w, w_branch_a, w_branch_b, w_out, norm2_w, ffn_w_up, ffn_conv_w, ffn_conv_b, ffn_w_down, final_norm_w, loss_target, m_c_ctx, m_ada_w, m_ada_b, m_norm1_w, m_w_in, m_sgu_ln_w, m_sgu_ln_b, m_sgu_w, m_sgu_b, m_hgrn_lower_bounds, m_hgrn_norm_w, m_w_branch_a, m_w_branch_b, m_w_out, m_norm2_w, m_ffn_w_up, m_ffn_conv_w, m_ffn_conv_b, m_ffn_w_down, m_final_norm_w, v_c_ctx, v_ada_w, v_ada_b, v_norm1_w, v_w_in, v_sgu_ln_w, v_sgu_ln_b, v_sgu_w, v_sgu_b, v_hgrn_lower_bounds, v_hgrn_norm_w, v_w_branch_a, v_w_branch_b, v_w_out, v_norm2_w, v_ffn_w_up, v_ffn_conv_w, v_ffn_conv_b, v_ffn_w_down, v_final_norm_w):
    w = dict(c_ctx=c_ctx, ada_w=ada_w, ada_b=ada_b, norm1_w=norm1_w, w_in=w_in, sgu_ln_w=sgu_ln_w, sgu_ln_b=sgu_ln_b,
             sgu_w=sgu_w, sgu_b=sgu_b, hgrn_lower_bounds=hgrn_lower_bounds, hgrn_norm_w=hgrn_norm_w, w_branch_a=w_branch_a,
             w_branch_b=w_branch_b, w_out=w_out, norm2_w=norm2_w, ffn_w_up=ffn_w_up, ffn_conv_w=ffn_conv_w,
             ffn_conv_b=ffn_conv_b, ffn_w_down=ffn_w_down, final_norm_w=final_norm_w)
    mom = dict(zip(_ORDER, (m_c_ctx, m_ada_w, m_ada_b, m_norm1_w, m_w_in, m_sgu_ln_w, m_sgu_ln_b, m_sgu_w, m_sgu_b,
                            m_hgrn_lower_bounds, m_hgrn_norm_w, m_w_branch_a, m_w_branch_b, m_w_out, m_norm2_w, m_ffn_w_up,
                            m_ffn_conv_w, m_ffn_conv_b, m_ffn_w_down, m_final_norm_w)))
    var = dict(zip(_ORDER, (v_c_ctx, v_ada_w, v_ada_b, v_norm1_w, v_w_in, v_sgu_ln_w, v_sgu_ln_b, v_sgu_w, v_sgu_b,
                            v_hgrn_lower_bounds, v_hgrn_norm_w, v_w_branch_a, v_w_branch_b, v_w_out, v_norm2_w, v_ffn_w_up,
                            v_ffn_conv_w, v_ffn_conv_b, v_ffn_w_down, v_final_norm_w)))
    depth, D = norm1_w.shape
    dff = ffn_conv_b.shape[1]
    ctx_rows, seq = ctx.shape[1], x.shape[1]

    assert depth == 2, "the lower-bound softmax is written for two layers"
    core = lax.axis_index("c")
    chip = 2 * lax.axis_index("x") + lax.axis_index("y")
    ids = jnp.stack([core, chip]).astype(jnp.int32)

    first, rest = _LAYER_KEYS[:2], _LAYER_KEYS[2:]
    shard = lambda l, k: w[_SHARDED[_LAYER_KEYS.index(k)]][l].astype(BF16)
    started, conv_full = {}, []

    def landing(s):
        return lax.dynamic_update_slice(lax.empty((N_CHIPS,) + s.shape, s.dtype), s[None], (chip,) + (0,) * s.ndim)

    def start_gather(l, keys, tag):
        lands = [landing(shard(l, k)) for k in keys]
        started[tag] = _split_start([], lands, lambda ins, lds, x, y, c: _plan_gather_far(lds, x, y, c),
                                    (N_CHIPS - 1) * _n_half_pieces(lands), f"gather_start_{tag}")
        return started[tag]["token"]

    def finish_gather(keys, tag, after):
        _, lands = _split_wait(started[tag], after, f"gather_wait_{tag}")
        return dict(zip(keys, _pair_forward(lands, f"gather_forward_{tag}")))

    def layer_weights(l, after):
        if l == 0:
            got = _gather_weights([landing(shard(0, k)) for k in first] + [landing(ffn_conv_w)], "gather_weights_first")
            conv_full.append(jnp.transpose(got[-1], (1, 2, 3, 0, 4)).reshape(depth, 9, dff))
            out = dict(zip(first, got), token=start_gather(0, rest, "rest_0"))
        else:
            out = dict(finish_gather(first, f"first_{l}", after), token=0.0)

        def late(after_late):
            more = finish_gather(rest, f"rest_{l}", after_late)
            more["late_token"] = 0.0
            if l + 1 < depth:
                more["late_token"] = start_gather(l + 1, first, f"first_{l + 1}") + start_gather(l + 1, rest, f"rest_{l + 1}")
            return more

        return dict(out, conv_w=conv_full[0][l], late=late)

    pending, pair_sums_of = [], {}

    def pair_reduce(tag, gs):
        parts = [g.reshape(N_CHIPS, N_CORES, g.size // (N_CHIPS * N_CORES * g.shape[-1]), g.shape[-1]) for g in gs]
        other = _reduce_pair(parts, f"reduce_pair_{tag}")
        return [_sum_pair(a, o, ids, f"sum_pair_{tag}_{i}") for i, (a, o) in enumerate(zip(parts, other))]

    def on_layer_grads(l, stage, gs):
        keys = [k for k in gs if k != "w_in"] if stage == "early" else ["w_in"]
        if l == 0 and stage == "late":
            pair_sums_of["last"] = (keys, gs)
            return 0.0
        tag = f"{stage}_{l}"
        sums = pair_reduce(tag, [gs[k] for k in keys])
        lands = [lax.empty(s.shape, s.dtype) for s in sums]
        st = _split_start(sums, lands, _plan_chips, _n_chips_copies(sums), f"reduce_chips_start_{tag}")
        pending.append((tag, l, keys, st))
        return st["token"]

    W = dict(ada_b=ada_b, norm1_w=norm1_w, sgu_ln_w=sgu_ln_w, sgu_ln_b=sgu_ln_b, sgu_w=sgu_w.astype(BF16),
             sgu_bt=jnp.swapaxes(sgu_b, 1, 2), hlb=hgrn_lower_bounds, hnw=hgrn_norm_w, norm2_w=norm2_w, conv_b=ffn_conv_b,
             final_norm_w=final_norm_w)
    xs = jnp.concatenate([ctx[0], x[0]], axis=0)
    cv = jnp.concatenate([c_ctx[None, :], c, jnp.zeros((14, D), F32)], axis=0)
    loss_local, dxs, G, sa = _local_step(xs, cv, loss_target[0], W, layer_weights, on_layer_grads, ctx_rows)
    loss = lax.psum(loss_local, ("x", "y", "c"))
    grad_x = dxs[ctx_rows:][None]

    pad8 = lambda a: jnp.pad(a, ((0, 8 - a.shape[0]), (0, 0)))
    fact = jnp.concatenate([pad8(sa[1:2].astype(F32))] + [pad8(G["dmod"][l][1].reshape(N_MOD, D)) for l in range(depth)]
                           + [pad8(G["dmod"][l][0].reshape(N_MOD, D)) for l in range(depth)], axis=0)
    facts = _gather_all(fact, "gather_mod_factors")
    lhs = jnp.concatenate([facts[:, 0].astype(BF16), jnp.broadcast_to(sa[0:1], (8, D))], axis=0)
    ada_cols = N_MOD * D // N_CHIPS
    g_ada = []
    for l in range(depth):
        lo_x, lo_c = 8 * (1 + l), 8 * (1 + depth + l)
        rhs = jnp.concatenate([facts[:, lo_x:lo_x + N_MOD].reshape(8, N_MOD * D),
                               facts[:, lo_c:lo_c + N_MOD].reshape(8, N_MOD * D)], axis=0)
        rhs = lax.dynamic_slice_in_dim(rhs, chip * ada_cols, ada_cols, axis=1).astype(BF16)
        g_ada.append(_mm_tn(lhs, rhs, F32, f"dw_ada_{l}"))

    dh = G["hlb1"][depth - 1]
    small_like = [w[k] for k in _SMALL] + [jnp.zeros((depth, 9, dff), F32)]
    small = [G["c_ctx"], jnp.stack(G["ada_b"]), jnp.stack(G["norm1_w"]), jnp.stack(G["sgu_ln_w"]), jnp.stack(G["sgu_ln_b"]),
             jnp.stack(G["sgu_w"]), jnp.stack(G["sgu_b"]), jnp.stack([-dh, dh]), jnp.stack(G["hnw"]), jnp.stack(G["norm2_w"]),
             jnp.stack(G["conv_b"]), G["final_norm_w"], jnp.stack(G["conv_w"])]
    n_small = sum(a.size for a in small)
    n_small_pad = _round_up(n_small, N_CORES * 16 * PACK_COLS)
    small_rows = n_small_pad // (N_CORES * PACK_COLS)
    small_rep = jnp.broadcast_to(_pack(small, n_small_pad).reshape(1, N_CORES, small_rows, PACK_COLS),
                                 (N_CHIPS, N_CORES, small_rows, PACK_COLS))
    last_keys, last_gs = pair_sums_of["last"]
    last_sums = pair_reduce("last", [last_gs[k] for k in last_keys] + [small_rep])
    last_recv = _reduce_chips(last_sums, "reduce_chips_last")

    halves, where = [], {}
    for tag, l, keys, st in pending:
        sums, recv = _split_wait(st, dxs, f"reduce_chips_wait_{tag}")
        for i, k in enumerate(keys):
            where[(l, k)] = len(halves)
            halves.append(_sum_chips(sums[i], recv[i], ids, f"sum_chips_{tag}_{i}"))
    for i, k in enumerate(list(last_keys) + ["small"]):
        where[(0, k)] = len(halves)
        halves.append(_sum_chips(last_sums[i], last_recv[i], ids, f"sum_chips_last_{i}"))
    reduced = _gather_pair(halves, "gather_pair")

    g_small = _unpack(reduced[where[(0, "small")]].reshape(-1), small_like)
    grads = dict(zip(_SMALL, g_small[:-1]))
    g_conv = lax.dynamic_slice_in_dim(g_small[-1].reshape(depth, 3, 3, dff), chip * (dff // N_CHIPS), dff // N_CHIPS, axis=3)

    delta, new_m, new_v = {}, {}, {}
    for i, k in enumerate(_SHARDED):
        shp = w[k].shape
        gs = g_ada if i == 0 else [reduced[where[(l, _LAYER_KEYS[i])]].reshape(shp[1:]) for l in range(depth)]
        grads[k], delta[k], new_m[k], new_v[k] = _adamw(w[k], gs, mom[k], var[k], f"adamw_{k}")
    packed = _SMALL + ("ffn_conv_w",)
    n_pad = _round_up(sum(w[k].size for k in packed), 16 * PACK_COLS)
    pack = lambda t: _pack([t[k] for k in packed], n_pad).reshape(1, -1, PACK_COLS)
    grads["ffn_conv_w"] = g_conv
    _, d, nm, nv = _adamw(pack(w), [pack(grads)[0]], pack(mom), pack(var), "adamw_packed")
    like = [w[k] for k in packed]
    for src, dst in ((d, delta), (nm, new_m), (nv, new_v)):
        dst.update(zip(packed, _unpack(src.reshape(-1), like)))

    return (loss, grad_x, *[grads[k] for k in _ORDER], *[delta[k] for k in _ORDER], *[new_m[k] for k in _ORDER],
            *[new_v[k] for k in _ORDER])
```

```python
import functools

import jax
import jax.numpy as jnp
from jax import lax
from jax.experimental import pallas as pl
from jax.experimental.pallas import tpu as pltpu

F32 = jnp.float32
BF16 = jnp.bfloat16

GRID_W = 64
HG_CHUNK = 64
SGU_CHUNK = 128
HEAD = 128
TB = 256
N_MOD = 6
RMS_EPS = 1e-6
LN_EPS = 1e-5
VMEM_LIMIT = 48 * 1024 * 1024
N_CHIPS = 4
N_CORES = 2

ADAM_LR = 0.001
ADAM_B1 = 0.9
ADAM_B2 = 0.999
ADAM_EPS = 1e-08
ADAM_WD = 0.01
ADAM_STEP = 10

_GELU_C = 0.7978845608028654
_GELU_A = 0.044715


def _sigmoid(x):
    return 0.5 * jnp.tanh(0.5 * x) + 0.5


def _silu(x):
    return x * _sigmoid(x)


def _silu_both(x):
    s = _sigmoid(x)
    return x * s, s * (1.0 + x * (1.0 - s))


def _dsilu(x):
    return _silu_both(x)[1]


def _gelu_both(x):
    x2 = x * x
    t = jnp.tanh(_GELU_C * (x + _GELU_A * x2 * x))
    h = 0.5 * (1.0 + t)
    return x * h, h + 0.5 * x * (1.0 - t * t) * (_GELU_C + 3.0 * _GELU_C * _GELU_A * x2)


def _gelu(x):
    return 0.5 * x * (1.0 + jnp.tanh(_GELU_C * (x + _GELU_A * x * x * x)))


def _dgelu(x):
    return _gelu_both(x)[1]


def _dot(a, b, ca, cb):
    return lax.dot_general(a, b, (((ca,), (cb,)), ((), ())), preferred_element_type=F32)


def _nn(a, b):
    return _dot(a, b, 1, 0)


def _nt(a, b):
    return _dot(a, b, 1, 1)


def _tn(a, b):
    return _dot(a, b, 0, 0)


def _params(*sem, vmem=VMEM_LIMIT):
    return pltpu.CompilerParams(dimension_semantics=sem if sem else None, vmem_limit_bytes=vmem)


def _stream_of(i, ctx_blocks):
    return (i >= ctx_blocks).astype(jnp.int32)


def _mm(a, b, mode, tm, tn, tk, out_dtype, name, add=None, b_chips=False, out_chips=False):
    if not b_chips:
        bshape = b.shape
    else:
        bshape = (b.shape[1], N_CHIPS * b.shape[2])
    if mode == "nn":
        (M, K), (K2, N) = a.shape, bshape
    elif mode == "nt":
        (M, K), (N, K2) = a.shape, bshape
    else:
        (K, M), (K2, N) = a.shape, bshape
    assert K == K2 and M % tm == 0 and N % tn == 0 and K % tk == 0, (name, a.shape, b.shape, tm, tn, tk)
    nk = K // tk
    if mode == "tn":
        a_spec = pl.BlockSpec((tk, tm), lambda j, i, k: (k, i))
    else:
        a_spec = pl.BlockSpec((tm, tk), lambda j, i, k: (i, k))
    if not b_chips:
        if mode == "nt":
            b_spec = pl.BlockSpec((tn, tk), lambda j, i, k: (j, k))
        else:
            b_spec = pl.BlockSpec((tk, tn), lambda j, i, k: (k, j))
    else:
        cols = b.shape[2]
        if mode == "nn":
            per = cols // tn
            assert cols % tn == 0
            b_spec = pl.BlockSpec((None, tk, tn), lambda j, i, k: (j // per, k, j % per))
        else:
            per = cols // tk
            assert mode == "nt" and cols % tk == 0
            b_spec = pl.BlockSpec((None, tn, tk), lambda j, i, k: (k // per, j, k % per))
    if out_chips:
        per_o = (N // N_CHIPS) // tn
        assert (N // N_CHIPS) % tn == 0 and add is None
        o_spec = pl.BlockSpec((None, tm, tn), lambda j, i, k: (j // per_o, i, j % per_o))
        o_shape = (N_CHIPS, M, N // N_CHIPS)
    else:
        o_spec = pl.BlockSpec((tm, tn), lambda j, i, k: (i, j))
        o_shape = (M, N)
    ca, cb = {"nn": (1, 0), "nt": (1, 1), "tn": (0, 0)}[mode]

    def body(a_ref, b_ref, *rest):
        if add is None:
            o_ref, acc = rest
        else:
            add_ref, o_ref, acc = rest
        k = pl.program_id(2)

        @pl.when(k == 0)
        def _():
            acc[...] = jnp.zeros_like(acc)

        acc[...] += _dot(a_ref[...], b_ref[...], ca, cb)

        @pl.when(k == nk - 1)
        def _():
            r = acc[...]
            if add is not None:
                r = r + add_ref[...]
            o_ref[...] = r.astype(out_dtype)

    ins = [a, b] + ([] if add is None else [add])
    specs = [a_spec, b_spec] + ([] if add is None else [o_spec])
    return pl.pallas_call(
        body, name=name, grid=(N // tn, M // tm, nk), in_specs=specs, out_specs=o_spec,
        out_shape=jax.ShapeDtypeStruct(o_shape, out_dtype),
        scratch_shapes=[pltpu.VMEM((tm, tn), F32)],
        compiler_params=_params("parallel", "parallel", "arbitrary"),
    )(*ins)


def _tile(n, pref):
    if n <= pref:
        return n
    best = None
    for t in range(128, pref + 1, 128):
        if n % t == 0:
            best = t
    assert best is not None, (n, pref)
    return best


def _rows_tile(n, pref):
    if n <= pref:
        return n
    best = None
    for t in range(16, pref + 1, 16):
        if n % t == 0:
            best = t
    assert best is not None, (n, pref)
    return best


def _mm_nn_w(a, wg, out_dtype, name):
    M, K = a.shape
    return _mm(a, wg, "nn", _rows_tile(M, 1088), _tile(wg.shape[2], 1536), _tile(K, 1536), out_dtype, name, b_chips=True)


def _mm_nt_w(a, wg, out_dtype, name):
    M, K = a.shape
    return _mm(a, wg, "nt", _rows_tile(M, 1088), _tile(wg.shape[1], 1024), _tile(wg.shape[2], 1536), out_dtype, name,
               b_chips=True)


def _mm_tn(a, b, out_dtype, name, out_chips=False):
    K, M = a.shape
    N = b.shape[1]
    ncol = N // N_CHIPS if out_chips else N
    tm, tn = _tile(M, 1408), _tile(ncol, 1408)
    if tm * tn > 1408 * 1152:
        tn = _tile(ncol, 1152)
    return _mm(a, b, "tn", tm, tn, _rows_tile(K, 2176), out_dtype, name, out_chips=out_chips)


def _mod_fwd(cv, wg, b, name):
    R, D = cv.shape
    tn = wg.shape[2]
    N = N_CHIPS * tn

    def body(cv_ref, w_ref, b_ref, mod_ref, sa_ref):
        sa = _silu(cv_ref[...]).astype(BF16)
        sa_ref[...] = sa
        mod_ref[...] = _nn(sa, w_ref[...]) + b_ref[...]

    return pl.pallas_call(
        body, name=name, grid=(N_CHIPS,),
        in_specs=[pl.BlockSpec((R, D), lambda j: (0, 0)), pl.BlockSpec((None, D, tn), lambda j: (j, 0, 0)),
                  pl.BlockSpec((1, tn), lambda j: (0, j))],
        out_specs=[pl.BlockSpec((R, tn), lambda j: (0, j)), pl.BlockSpec((R, D), lambda j: (0, 0))],
        out_shape=[jax.ShapeDtypeStruct((R, N), F32), jax.ShapeDtypeStruct((R, D), BF16)],
        compiler_params=_params("arbitrary"),
    )(cv, wg, b)


def _cvec_bwd(dmod, wg, cv, name):
    R, N = dmod.shape
    D = wg.shape[1]
    tk = wg.shape[2]
    nk = N_CHIPS

    def body(dm_ref, w_ref, cv_ref, o_ref):
        k = pl.program_id(0)

        @pl.when(k == 0)
        def _():
            o_ref[...] = jnp.zeros_like(o_ref)

        o_ref[...] += _nt(dm_ref[...].astype(BF16), w_ref[...])

        @pl.when(k == nk - 1)
        def _():
            o_ref[...] = o_ref[...] * _dsilu(cv_ref[...])

    return pl.pallas_call(
        body, name=name, grid=(nk,),
        in_specs=[pl.BlockSpec((R, tk), lambda k: (0, k)), pl.BlockSpec((None, D, tk), lambda k: (k, 0, 0)),
                  pl.BlockSpec((R, D), lambda k: (0, 0))],
        out_specs=pl.BlockSpec((R, D), lambda k: (0, 0)),
        out_shape=jax.ShapeDtypeStruct((R, D), F32),
        compiler_params=_params("arbitrary"),
    )(dmod, wg, cv)


def _norm_mod(x, nw, mod, which, ctx_rows, name):
    T, D = x.shape
    cb = ctx_rows // TB

    def body(x_ref, nw_ref, mod_ref, h_ref):
        xv = x_ref[...]
        r = lax.rsqrt(jnp.mean(xv * xv, axis=-1, keepdims=True) + RMS_EPS)
        y = xv * r * nw_ref[...]
        sh = mod_ref[which:which + 1, :]
        sc = mod_ref[which + 1:which + 2, :]
        h_ref[...] = (y * (1.0 + sc) + sh).astype(BF16)

    return pl.pallas_call(
        body, name=name, grid=(T // TB,),
        in_specs=[pl.BlockSpec((TB, D), lambda i: (i, 0)), pl.BlockSpec((1, D), lambda i: (0, 0)),
                  pl.BlockSpec((None, N_MOD, D), lambda i: (_stream_of(i, cb), 0, 0))],
        out_specs=pl.BlockSpec((TB, D), lambda i: (i, 0)),
        out_shape=jax.ShapeDtypeStruct((T, D), BF16),
        compiler_params=_params("parallel"),
    )(x, nw, mod)


def _norm_mod_bwd(dh, x, dres, nw, mod, which, ctx_rows, name):
    T, D = x.shape
    cb = ctx_rows // TB

    def body(dh_ref, x_ref, dres_ref, nw_ref, mod_ref, dx_ref, dm_ref, dnw_ref):
        i = pl.program_id(0)

        @pl.when(i == 0)
        def _():
            dnw_ref[...] = jnp.zeros_like(dnw_ref)

        @pl.when((i == 0) | (i == cb))
        def _():
            dm_ref[...] = jnp.zeros_like(dm_ref)

        xv = x_ref[...]
        dh = dh_ref[...]
        r = lax.rsqrt(jnp.mean(xv * xv, axis=-1, keepdims=True) + RMS_EPS)
        xh = xv * r
        nwv = nw_ref[...]
        sc = mod_ref[which + 1:which + 2, :]
        y = xh * nwv
        dm_ref[0:1, :] += jnp.sum(dh, axis=0, keepdims=True)
        dm_ref[1:2, :] += jnp.sum(dh * y, axis=0, keepdims=True)
        dy = dh * (1.0 + sc)
        dnw_ref[...] += jnp.sum(dy * xh, axis=0, keepdims=True)
        dxh = dy * nwv
        dx_ref[...] = dres_ref[...] + r * (dxh - xh * jnp.mean(dxh * xh, axis=-1, keepdims=True))

    return pl.pallas_call(
        body, name=name, grid=(T // TB,),
        in_specs=[pl.BlockSpec((TB, D), lambda i: (i, 0)), pl.BlockSpec((TB, D), lambda i: (i, 0)),
                  pl.BlockSpec((TB, D), lambda i: (i, 0)), pl.BlockSpec((1, D), lambda i: (0, 0)),
                  pl.BlockSpec((None, N_MOD, D), lambda i: (_stream_of(i, cb), 0, 0))],
        out_specs=[pl.BlockSpec((TB, D), lambda i: (i, 0)),
                   pl.BlockSpec((None, 2, D), lambda i: (_stream_of(i, cb), 0, 0)),
                   pl.BlockSpec((1, D), lambda i: (0, 0))],
        out_shape=[jax.ShapeDtypeStruct((T, D), F32), jax.ShapeDtypeStruct((2, 2, D), F32),
                   jax.ShapeDtypeStruct((1, D), F32)],
        compiler_params=_params("arbitrary"),
    )(dh, x, dres, nw, mod)


def _scan_chunk(n, rev, n_ctx, n_all):
    if not rev:
        return n
    return jnp.where(n < n_ctx, n_ctx - 1 - n, n_all - 1 + n_ctx - n)


def _cumsum_rows(x, rev):
    rows = x.shape[0]
    row = lax.broadcasted_iota(jnp.int32, (rows, 1), 0)
    s = 1
    while s < rows:
        if not rev:
            x = x + jnp.where(row >= s, pltpu.roll(x, s, 0), 0.0)
        else:
            x = x + jnp.where(row < rows - s, pltpu.roll(x, rows - s, 0), 0.0)
        s *= 2
    return x


def _lower_bound(hlb_ref, layer):
    h = hlb_ref[...]
    if layer == 0:
        return jnp.zeros_like(h[0:1, :])
    return _sigmoid(h[1:2, :] - h[0:1, :])


def _hgrn_gates(q_ref, f_ref, hlb_ref, layer, rev):
    lb = _lower_bound(hlb_ref, layer)
    z = f_ref[...]
    sig = 1.0 / (1.0 + jnp.exp(-z))
    fg = lb + (1.0 - lb) * sig
    kk = (1.0 - lb) * (1.0 - sig)
    g = jnp.log(fg)
    b = _cumsum_rows(g, rev)
    bt = jnp.sum(g, axis=0, keepdims=True)
    mid = HG_CHUNK // 2
    r = b[mid:mid + 1, :] if rev else b[mid - 1:mid, :]
    qh = _silu(q_ref[...])
    return lb, sig, fg, kk, b, bt, r, qh


def _tri_mask(rev):
    t = lax.broadcasted_iota(jnp.int32, (HG_CHUNK, HG_CHUNK), 0)
    s = lax.broadcasted_iota(jnp.int32, (HG_CHUNK, HG_CHUNK), 1)
    return (s >= t) if rev else (s <= t)


def _hgrn_fwd(parts, hlb, layer, rev, ctx_rows, name, o_add=None):
    T = parts.shape[0]
    D = hlb.shape[1] // 2
    nh = D // HEAD
    n_all, n_ctx = T // HG_CHUNK, ctx_rows // HG_CHUNK
    chunk = functools.partial(_scan_chunk, rev=rev, n_ctx=n_ctx, n_all=n_all)
    fcol = 2 if rev else 1

    def body(q_ref, f_ref, i_ref, hlb_ref, *rest):
        if o_add is None:
            o_ref, st_ref, s_scr = rest
        else:
            oa_ref, o_ref, st_ref, s_scr = rest
        n = pl.program_id(0)

        @pl.when(n == 0)
        def _():
            s_scr[...] = jnp.zeros_like(s_scr)

        lb, sig, fg, kk, b, bt, r, qh = _hgrn_gates(q_ref, f_ref, hlb_ref, layer, rev)
        qr = (qh * jnp.exp(b - r)).astype(BF16)
        kr = (kk * jnp.exp(r - b)).astype(BF16)
        qe = (qh * jnp.exp(b)).astype(BF16)
        ke = (kk * jnp.exp(bt - b)).astype(BF16)
        dec = jnp.exp(bt)
        v = i_ref[...].astype(BF16)
        mask = _tri_mask(rev)
        hs = [slice(h * HEAD, (h + 1) * HEAD) for h in range(nh)]
        st = [s_scr[h] for h in range(nh)]
        a_raw = [_nt(qr[:, sl], kr[:, sl]) for sl in hs]
        o_int = [_nt(qe[:, sl], st[h].astype(BF16)) for h, sl in enumerate(hs)]
        kv = [_tn(v[:, sl], ke[:, sl]) for sl in hs]
        for h, sl in enumerate(hs):
            st_ref[h] = st[h]
            o = _nn(jnp.where(mask, a_raw[h], 0.0).astype(BF16), v[:, sl]) + o_int[h]
            if o_add is not None:
                o = o + oa_ref[:, sl]
            o_ref[:, sl] = o
            s_scr[h] = st[h] * dec[:, sl] + kv[h]

    cspec = lambda col: pl.BlockSpec((HG_CHUNK, D), lambda n: (chunk(n), col))
    ins = [parts, parts, parts, hlb]
    specs = [cspec(0), cspec(fcol), cspec(3), pl.BlockSpec((2, D), lambda n: (0, 1 if rev else 0))]
    if o_add is not None:
        ins.append(o_add)
        specs.append(cspec(0))
    return pl.pallas_call(
        body, name=name, grid=(n_all,), in_specs=specs,
        out_specs=[cspec(0), pl.BlockSpec((None, nh, HEAD, HEAD), lambda n: (n, 0, 0, 0))],
        out_shape=[jax.ShapeDtypeStruct((T, D), F32), jax.ShapeDtypeStruct((n_all, nh, HEAD, HEAD), F32)],
        scratch_shapes=[pltpu.VMEM((nh, HEAD, HEAD), F32)],
        compiler_params=_params("arbitrary"),
    )(*ins)


def _hgrn_bwd(parts, hlb, do, states, layer, rev, ctx_rows, name, other=None, dparts=None):
    T = parts.shape[0]
    D = hlb.shape[1] // 2
    nh = D // HEAD
    n_all, n_ctx = T // HG_CHUNK, ctx_rows // HG_CHUNK
    step = lambda m: n_all - 1 - m
    chunk = lambda m: _scan_chunk(step(m), rev, n_ctx, n_all)
    fcol = 2 if rev else 1
    has_add = other is not None
    assert not has_add or rev

    def body(q_ref, f_ref, i_ref, hlb_ref, do_ref, st_ref, *rest):
        if has_add:
            dqa_ref, dza_ref, dia_ref, _, out_ref, dlb_ref, ds_scr = rest
            dq_ref, dz_ref, di_ref = out_ref.at[:, 0:D], out_ref.at[:, 2 * D:3 * D], out_ref.at[:, 3 * D:4 * D]
            out_ref[:, D:2 * D] = dza_ref[...]
        else:
            dq_ref, dz_ref, di_ref, dlb_ref, ds_scr = rest
        m = pl.program_id(0)

        @pl.when(m == 0)
        def _():
            ds_scr[...] = jnp.zeros_like(ds_scr)
            dlb_ref[...] = jnp.zeros_like(dlb_ref)

        lb, sig, fg, kk, b, bt, r, qh = _hgrn_gates(q_ref, f_ref, hlb_ref, layer, rev)
        e_qr = jnp.exp(b - r)
        e_kr = jnp.exp(r - b)
        e_b = jnp.exp(b)
        e_ke = jnp.exp(bt - b)
        dec = jnp.exp(bt)
        qr = (qh * e_qr).astype(BF16)
        kr = (kk * e_kr).astype(BF16)
        qe = (qh * e_b).astype(BF16)
        ke = (kk * e_ke).astype(BF16)
        vf = i_ref[...]
        v = vf.astype(BF16)
        dov = do_ref[...].astype(BF16)
        mask = _tri_mask(rev)
        hs = [slice(h * HEAD, (h + 1) * HEAD) for h in range(nh)]
        st = [st_ref[h] for h in range(nh)]
        dst = [ds_scr[h] for h in range(nh)]
        stb = [t.astype(BF16) for t in st]
        dstb = [t.astype(BF16) for t in dst]
        a_raw = [_nt(qr[:, sl], kr[:, sl]) for sl in hs]
        da_raw = [_nt(dov[:, sl], v[:, sl]) for sl in hs]
        dq_int = [_nn(dov[:, sl], stb[h]) for h, sl in enumerate(hs)]
        dk_int = [_nn(v[:, sl], dstb[h]) for h, sl in enumerate(hs)]
        dv_int = [_nt(ke[:, sl], dstb[h]) for h, sl in enumerate(hs)]
        ds_new = [_tn(dov[:, sl], qe[:, sl]) for sl in hs]
        a = [jnp.where(mask, t, 0.0).astype(BF16) for t in a_raw]
        da = [jnp.where(mask, t, 0.0).astype(BF16) for t in da_raw]
        dv_parts = [_tn(a[h], dov[:, sl]) + dv_int[h] for h, sl in enumerate(hs)]
        dq_parts = [_nn(da[h], kr[:, sl]) * e_qr[:, sl] + dq_int[h] * e_b[:, sl] for h, sl in enumerate(hs)]
        dki_parts = [dk_int[h] * e_ke[:, sl] for h, sl in enumerate(hs)]
        dk_parts = [_tn(da[h], qr[:, sl]) * e_kr[:, sl] + dki_parts[h] for h, sl in enumerate(hs)]
        dbt_parts = [dec[:, sl] * jnp.sum(st[h] * dst[h], axis=0, keepdims=True) for h, sl in enumerate(hs)]
        for h, sl in enumerate(hs):
            ds_scr[h] = dst[h] * dec[:, sl] + ds_new[h]
        dq = jnp.concatenate(dq_parts, axis=1)
        dk = jnp.concatenate(dk_parts, axis=1)
        dki = jnp.concatenate(dki_parts, axis=1)
        dv = jnp.concatenate(dv_parts, axis=1)
        dbt = jnp.concatenate(dbt_parts, axis=1) + jnp.sum(kk * dki, axis=0, keepdims=True)
        db = qh * dq - kk * dk
        dg = _cumsum_rows(db, not rev) + dbt
        df = dg / fg - dk
        dz_ref[...] = (df * (1.0 - lb) * sig * (1.0 - sig)).astype(BF16)
        dlb_ref[...] += jnp.sum(df * (1.0 - sig), axis=0, keepdims=True)
        dqr = dq * _dsilu(q_ref[...])
        if has_add:
            dqr = dqr + dqa_ref[...]
            dv = dv + dia_ref[...]
        dq_ref[...] = dqr.astype(dq_ref.dtype)
        di_ref[...] = dv.astype(di_ref.dtype)

        @pl.when(m == n_all - 1)
        def _():
            if layer == 0:
                dlb_ref[...] = jnp.zeros_like(dlb_ref)
            else:
                dlb_ref[...] = dlb_ref[...] * lb * (1.0 - lb)

    cspec = lambda col: pl.BlockSpec((HG_CHUNK, D), lambda m: (chunk(m), col))
    ins = [parts, parts, parts, hlb, do, states]
    specs = [cspec(0), cspec(fcol), cspec(3), pl.BlockSpec((2, D), lambda m: (0, 1 if rev else 0)), cspec(0),
             pl.BlockSpec((None, nh, HEAD, HEAD), lambda m: (step(m), 0, 0, 0))]
    dlb_spec = pl.BlockSpec((1, D), lambda m: (0, 0))
    dlb_shape = jax.ShapeDtypeStruct((1, D), F32)
    if has_add:
        return pl.pallas_call(
            body, name=name, grid=(n_all,),
            in_specs=specs + [cspec(0), cspec(0), cspec(0), pl.BlockSpec(memory_space=pl.ANY)],
            out_specs=[pl.BlockSpec((HG_CHUNK, 4 * D), lambda m: (chunk(m), 0)), dlb_spec],
            out_shape=[jax.ShapeDtypeStruct(dparts.shape, dparts.dtype), dlb_shape],
            scratch_shapes=[pltpu.VMEM((nh, HEAD, HEAD), F32)], input_output_aliases={len(ins) + 3: 0},
            compiler_params=_params("arbitrary"),
        )(*ins, *other, dparts)
    return pl.pallas_call(
        body, name=name, grid=(n_all,), in_specs=specs,
        out_specs=[cspec(0), cspec(0), cspec(0), dlb_spec],
        out_shape=[jax.ShapeDtypeStruct((T, D), F32), jax.ShapeDtypeStruct((T, D), BF16),
                   jax.ShapeDtypeStruct((T, D), F32), dlb_shape],
        scratch_shapes=[pltpu.VMEM((nh, HEAD, HEAD), F32)],
        compiler_params=_params("arbitrary"),
    )(*ins)


def _sgu_ln(gv, lnw_ref, lnb_ref):
    mu = jnp.mean(gv, axis=-1, keepdims=True)
    xc = gv - mu
    rstd = lax.rsqrt(jnp.mean(xc * xc, axis=-1, keepdims=True) + LN_EPS)
    xh = xc * rstd
    return xh, rstd, xh * lnw_ref[...] + lnb_ref[...]


def _sgu_fwd(parts, lnw, lnb, w, bt, name):
    T = parts.shape[0]
    D = lnw.shape[1]
    G = D // HEAD

    def body(u_ref, v_ref, lnw_ref, lnb_ref, w_ref, bt_ref, ya_ref):
        gu = _gelu(u_ref[...])
        _, _, vn = _sgu_ln(_gelu(v_ref[...]), lnw_ref, lnb_ref)
        vnb = vn.astype(BF16)
        for g in range(G):
            sl = slice(g * HEAD, (g + 1) * HEAD)
            mixed = _nn(w_ref[g], vnb[:, sl]) + bt_ref[:, g:g + 1]
            ya_ref[:, sl] = (gu[:, sl] * mixed).astype(BF16)

    return pl.pallas_call(
        body, name=name, grid=(T // SGU_CHUNK,),
        in_specs=[pl.BlockSpec((SGU_CHUNK, D), lambda n: (n, 4)), pl.BlockSpec((SGU_CHUNK, D), lambda n: (n, 5)),
                  pl.BlockSpec((1, D), lambda n: (0, 0)), pl.BlockSpec((1, D), lambda n: (0, 0)),
                  pl.BlockSpec((G, SGU_CHUNK, SGU_CHUNK), lambda n: (0, 0, 0)),
                  pl.BlockSpec((SGU_CHUNK, G), lambda n: (0, 0))],
        out_specs=pl.BlockSpec((SGU_CHUNK, D), lambda n: (n, 0)),
        out_shape=jax.ShapeDtypeStruct((T, D), BF16),
        compiler_params=_params("parallel"),
    )(parts, parts, lnw, lnb, w, bt)


def _sgu_bwd(parts, dya, lnw, lnb, w, bt, dparts, name):
    T = parts.shape[0]
    D = lnw.shape[1]
    G = D // HEAD

    def body(u_ref, v_ref, dya_ref, lnw_ref, lnb_ref, w_ref, bt_ref, dparts_in,
             duv_ref, dw_ref, dbt_ref, dlnw_ref, dlnb_ref, dvn_scr):
        du_ref = duv_ref.at[:, 0:D]
        dv_ref = duv_ref.at[:, D:2 * D]
        n = pl.program_id(0)

        @pl.when(n == 0)
        def _():
            dw_ref[...] = jnp.zeros_like(dw_ref)
            dbt_ref[...] = jnp.zeros_like(dbt_ref)
            dlnw_ref[...] = jnp.zeros_like(dlnw_ref)
            dlnb_ref[...] = jnp.zeros_like(dlnb_ref)

        gu, dgu = _gelu_both(u_ref[...])
        gv, dgv_dv = _gelu_both(v_ref[...])
        xh, rstd, vn = _sgu_ln(gv, lnw_ref, lnb_ref)
        vnb = vn.astype(BF16)
        dya = dya_ref[...]
        lane = lax.broadcasted_iota(jnp.int32, (SGU_CHUNK, G), 1)
        dbt = jnp.zeros((SGU_CHUNK, G), F32)
        for g in range(G):
            sl = slice(g * HEAD, (g + 1) * HEAD)
            wg = w_ref[g]
            mixed = _nn(wg, vnb[:, sl]) + bt_ref[:, g:g + 1]
            dmix = dya[:, sl] * gu[:, sl]
            du_ref[:, sl] = (dya[:, sl] * mixed * dgu[:, sl]).astype(BF16)
            dmb = dmix.astype(BF16)
            dvn_scr[:, sl] = _tn(wg, dmb)
            dw_ref[g] += _nt(dmb, vnb[:, sl])
            dbt = dbt + jnp.where(lane == g, jnp.sum(dmix, axis=1, keepdims=True), 0.0)
        dbt_ref[...] += dbt
        dvn = dvn_scr[...]
        dlnw_ref[...] += jnp.sum(dvn * xh, axis=0, keepdims=True)
        dlnb_ref[...] += jnp.sum(dvn, axis=0, keepdims=True)
        dxh = dvn * lnw_ref[...]
        dgv = rstd * (dxh - jnp.mean(dxh, axis=-1, keepdims=True) - xh * jnp.mean(dxh * xh, axis=-1, keepdims=True))
        dv_ref[...] = (dgv * dgv_dv).astype(BF16)

    row = lambda col: pl.BlockSpec((SGU_CHUNK, D), lambda n: (n, col))
    vec = pl.BlockSpec((1, D), lambda n: (0, 0))
    wsp = pl.BlockSpec((G, SGU_CHUNK, SGU_CHUNK), lambda n: (0, 0, 0))
    bsp = pl.BlockSpec((SGU_CHUNK, G), lambda n: (0, 0))
    return pl.pallas_call(
        body, name=name, grid=(T // SGU_CHUNK,),
        in_specs=[row(4), row(5), row(0), vec, vec, wsp, bsp, pl.BlockSpec(memory_space=pl.ANY)],
        out_specs=[pl.BlockSpec((SGU_CHUNK, 2 * D), lambda n: (n, 2)), wsp, bsp, vec, vec],
        out_shape=[jax.ShapeDtypeStruct(dparts.shape, dparts.dtype),
                   jax.ShapeDtypeStruct((G, SGU_CHUNK, SGU_CHUNK), F32), jax.ShapeDtypeStruct((SGU_CHUNK, G), F32),
                   jax.ShapeDtypeStruct((1, D), F32), jax.ShapeDtypeStruct((1, D), F32)],
        scratch_shapes=[pltpu.VMEM((SGU_CHUNK, D), F32)], input_output_aliases={7: 0},
        compiler_params=_params("arbitrary"),
    )(parts, parts, dya, lnw, lnb, w, bt, dparts)


TBT = 256
VMEM_LIMIT_TOKEN_OUT = 58 * 1024 * 1024


def _rows_weight_spec(wg):
    return pl.BlockSpec(wg.shape, lambda i: (0, 0, 0))


def _full(w_ref):
    return w_ref[...].reshape(w_ref.shape[0] * w_ref.shape[1], w_ref.shape[2])


def _token_out_fwd(o, parts, ya, x, mod, hnw, wa, wb, wo, ctx_rows, name):
    T, D = x.shape
    nh = D // HEAD
    cb = ctx_rows // TBT

    def body(o_ref, og_ref, ga_ref, gb_ref, ya_ref, x_ref, mod_ref, hnw_ref, wa_ref, wb_ref, wo_ref,
             yb_ref, pa_ref, pb_ref, mg_ref, tmo_ref, xm_ref):
        ov = o_ref[...]
        so = _silu(og_ref[...])
        nw = hnw_ref[...]
        for h in range(nh):
            sl = slice(h * HEAD, (h + 1) * HEAD)
            seg = ov[:, sl]
            r = lax.rsqrt(jnp.mean(seg * seg, axis=-1, keepdims=True) + RMS_EPS)
            yb_ref[:, sl] = (seg * r * nw * so[:, sl]).astype(BF16)
        pa = _nn(ya_ref[...], _full(wa_ref))
        pb = _nn(yb_ref[...], _full(wb_ref))
        pa_ref[...] = pa
        pb_ref[...] = pb
        mg = (_sigmoid(ga_ref[...]) * pa + _sigmoid(gb_ref[...]) * pb).astype(BF16)
        mg_ref[...] = mg
        out = _nn(mg, _full(wo_ref))
        tmo_ref[...] = out
        xm_ref[...] = x_ref[...] + mod_ref[2:3, :] * out

    row = lambda col: pl.BlockSpec((TBT, D), lambda i: (i, col))
    wsp = _rows_weight_spec(wa)
    sd = lambda dt: jax.ShapeDtypeStruct((T, D), dt)
    return pl.pallas_call(
        body, name=name, grid=(T // TBT,),
        in_specs=[row(0), row(6), row(7), row(8), row(0), row(0),
                  pl.BlockSpec((None, N_MOD, D), lambda i: (_stream_of(i, cb), 0, 0)),
                  pl.BlockSpec((1, HEAD), lambda i: (0, 0)), wsp, wsp, wsp],
        out_specs=[row(0)] * 6,
        out_shape=[sd(BF16), sd(F32), sd(F32), sd(BF16), sd(F32), sd(F32)],
        compiler_params=_params("parallel", vmem=VMEM_LIMIT_TOKEN_OUT),
    )(o, parts, parts, parts, ya, x, mod, hnw, wa, wb, wo)


def _token_out_bwd(dx, tmo, pa, pb, o, parts, mod, hnw, wa, wb, wo, ctx_rows, name):
    T, D = dx.shape
    nh = D // HEAD
    cb = ctx_rows // TBT

    def body(dx_ref, tmo_ref, pa_ref, pb_ref, o_ref, og_ref, ga_ref, gb_ref, mod_ref, hnw_ref, wa_ref, wb_ref, wo_ref,
             dout_ref, dpa_ref, dpb_ref, dgate_ref, dya_ref, do_ref, dg1_ref, dhnw_ref):
        i = pl.program_id(0)

        @pl.when(i == 0)
        def _():
            dhnw_ref[...] = jnp.zeros_like(dhnw_ref)

        @pl.when((i == 0) | (i == cb))
        def _():
            dg1_ref[...] = jnp.zeros_like(dg1_ref)

        dxv = dx_ref[...]
        dg1_ref[...] += jnp.sum(dxv * tmo_ref[...], axis=0, keepdims=True)
        dout = (dxv * mod_ref[2:3, :]).astype(BF16)
        dout_ref[...] = dout
        dmg = _nt(dout, _full(wo_ref))
        sa = _sigmoid(ga_ref[...])
        sb = _sigmoid(gb_ref[...])
        dpa = (dmg * sa).astype(BF16)
        dpb = (dmg * sb).astype(BF16)
        dpa_ref[...] = dpa
        dpb_ref[...] = dpb
        dgate_ref[:, D:2 * D] = (dmg * pa_ref[...] * sa * (1.0 - sa)).astype(BF16)
        dgate_ref[:, 2 * D:3 * D] = (dmg * pb_ref[...] * sb * (1.0 - sb)).astype(BF16)
        dya_ref[...] = _nt(dpa, _full(wa_ref))
        dyb = _nt(dpb, _full(wb_ref))
        so, dso = _silu_both(og_ref[...])
        ov = o_ref[...]
        nw = hnw_ref[...]
        dnw = jnp.zeros((1, HEAD), F32)
        for h in range(nh):
            sl = slice(h * HEAD, (h + 1) * HEAD)
            seg = ov[:, sl]
            r = lax.rsqrt(jnp.mean(seg * seg, axis=-1, keepdims=True) + RMS_EPS)
            oh = seg * r
            dn = dyb[:, sl] * so[:, sl]
            dgate_ref[:, sl] = (dyb[:, sl] * oh * nw * dso[:, sl]).astype(BF16)
            dnw = dnw + jnp.sum(dn * oh, axis=0, keepdims=True)
            doh = dn * nw
            do_ref[:, sl] = r * (doh - oh * jnp.mean(doh * oh, axis=-1, keepdims=True))
        dhnw_ref[...] += dnw

    row = lambda col: pl.BlockSpec((TBT, D), lambda i: (i, col))
    wsp = _rows_weight_spec(wa)
    sd = lambda dt: jax.ShapeDtypeStruct((T, D), dt)
    return pl.pallas_call(
        body, name=name, grid=(T // TBT,),
        in_specs=[row(0), row(0), row(0), row(0), row(0), row(6), row(7), row(8),
                  pl.BlockSpec((None, N_MOD, D), lambda i: (_stream_of(i, cb), 0, 0)),
                  pl.BlockSpec((1, HEAD), lambda i: (0, 0)), wsp, wsp, wsp],
        out_specs=[row(0)] * 3 + [pl.BlockSpec((TBT, 3 * D), lambda i: (i, 2)), row(0), row(0),
                                  pl.BlockSpec((None, 1, D), lambda i: (_stream_of(i, cb), 0, 0)),
                                  pl.BlockSpec((1, HEAD), lambda i: (0, 0))],
        out_shape=[sd(BF16)] * 3 + [jax.ShapeDtypeStruct((T, 9 * D), BF16), sd(F32), sd(F32),
                                    jax.ShapeDtypeStruct((2, 1, D), F32), jax.ShapeDtypeStruct((1, HEAD), F32)],
        compiler_params=_params("arbitrary", vmem=VMEM_LIMIT_TOKEN_OUT),
    )(dx, tmo, pa, pb, o, parts, parts, parts, mod, hnw, wa, wb, wo)


def _conv_geometry(i, nb, cb):
    is_ctx = i < cb
    first = (i == 0) | (i == cb)
    last = (i == cb - 1) | (i == nb - 1)
    row = lax.broadcasted_iota(jnp.int32, (TB + 2 * GRID_W, 1), 0)
    w = row & (GRID_W - 1)
    left_ok = (w != 0) | is_ctx
    right_ok = (w != GRID_W - 1) | is_ctx
    return is_ctx, first, last, left_ok, right_ok


def _ext(p_ref, m_ref, n_ref, first, last):
    return jnp.concatenate([jnp.where(first, 0.0, p_ref[...]), m_ref[...], jnp.where(last, 0.0, n_ref[...])], axis=0)


def _shift_prev(e, ok):
    return jnp.where(ok, pltpu.roll(e, 1, 0), 0.0)


def _shift_next(e, ok):
    return jnp.where(ok, pltpu.roll(e, e.shape[0] - 1, 0), 0.0)


def _halo_specs(cbk, n64, coff=0):
    r = TB // GRID_W
    prev = pl.BlockSpec((GRID_W, cbk), lambda j, i: (jnp.maximum(r * i - 1, 0), j + coff))
    main = pl.BlockSpec((TB, cbk), lambda j, i: (i, j + coff))
    nxt = pl.BlockSpec((GRID_W, cbk), lambda j, i: (jnp.minimum(r * i + r, n64 - 1), j + coff))
    return [prev, main, nxt]


def _conv_cblock(dff):
    return _tile(dff, 1408)


def _conv_fwd(up, cw, cbias, ctx_rows, name):
    T, dff = up.shape[0], up.shape[1] // 2
    cbk = _conv_cblock(dff)
    nb, cb = T // TB, ctx_rows // TB
    nvb = dff // cbk

    def body(ap_ref, a_ref, an_ref, v_ref, cw_ref, cb_ref, ac_ref, act_ref):
        i = pl.program_id(1)
        is_ctx, first, last, lok, rok = _conv_geometry(i, nb, cb)
        e = _ext(ap_ref, a_ref, an_ref, first, last)
        el = _shift_prev(e, lok)
        er = _shift_next(e, rok)
        cwv = cw_ref[...]

        def comb(dr, lo):
            sl = slice(lo, lo + TB)
            return cwv[3 * dr:3 * dr + 1] * el[sl] + cwv[3 * dr + 1:3 * dr + 2] * e[sl] + cwv[3 * dr + 2:3 * dr + 3] * er[sl]

        out = comb(1, GRID_W) + jnp.where(is_ctx, 0.0, comb(0, 0) + comb(2, 2 * GRID_W))
        a_c = out + cb_ref[...]
        ac_ref[...] = a_c
        act_ref[...] = (_gelu(a_c) * v_ref[...]).astype(BF16)

    main = pl.BlockSpec((TB, cbk), lambda j, i: (i, j))
    return pl.pallas_call(
        body, name=name, grid=(dff // cbk, nb),
        in_specs=_halo_specs(cbk, T // GRID_W) + [pl.BlockSpec((TB, cbk), lambda j, i: (i, j + nvb)),
                                                 pl.BlockSpec((9, cbk), lambda j, i: (0, j)),
                                                 pl.BlockSpec((1, cbk), lambda j, i: (0, j))],
        out_specs=[main, main],
        out_shape=[jax.ShapeDtypeStruct((T, dff), F32), jax.ShapeDtypeStruct((T, dff), BF16)],
        compiler_params=_params("parallel", "parallel"),
    )(up, up, up, up, cw, cbias)


def _conv_bwd(up, ac, dact, cw, ctx_rows, name):
    T, dff = up.shape[0], up.shape[1] // 2
    cbk = _conv_cblock(dff)
    nb, cb = T // TB, ctx_rows // TB
    nvb = dff // cbk

    def body(ap_ref, a_ref, an_ref, vp_ref, v_ref, vn_ref, cp_ref, c_ref, cn_ref, dp_ref, d_ref, dn_ref, cw_ref,
             da_ref, dv_ref, dcw_ref, dcb_ref):
        i = pl.program_id(1)

        @pl.when(i == 0)
        def _():
            dcw_ref[...] = jnp.zeros_like(dcw_ref)
            dcb_ref[...] = jnp.zeros_like(dcb_ref)

        is_ctx, first, last, lok, rok = _conv_geometry(i, nb, cb)
        gl, dgl = _gelu_both(_ext(cp_ref, c_ref, cn_ref, first, last))
        g = _ext(dp_ref, d_ref, dn_ref, first, last) * _ext(vp_ref, v_ref, vn_ref, first, last) * dgl
        dv_ref[...] = (d_ref[...] * gl[GRID_W:GRID_W + TB]).astype(BF16)
        gm = _shift_prev(g, lok)
        gp = _shift_next(g, rok)
        cwv = cw_ref[...]

        def comb(dr, lo):
            sl = slice(lo, lo + TB)
            return cwv[3 * dr:3 * dr + 1] * gp[sl] + cwv[3 * dr + 1:3 * dr + 2] * g[sl] + cwv[3 * dr + 2:3 * dr + 3] * gm[sl]

        da = comb(1, GRID_W) + jnp.where(is_ctx, 0.0, comb(0, 2 * GRID_W) + comb(2, 0))
        da_ref[...] = da.astype(BF16)
        e = _ext(ap_ref, a_ref, an_ref, first, last)
        taps = [_shift_prev(e, lok), e, _shift_next(e, rok)]
        gmain = g[GRID_W:GRID_W + TB]
        dcb_ref[...] += jnp.sum(gmain, axis=0, keepdims=True)
        vert = jnp.where(is_ctx, 0.0, 1.0)
        for dr in range(3):
            sl = slice(dr * GRID_W, dr * GRID_W + TB)
            for dw in range(3):
                s = jnp.sum(gmain * taps[dw][sl], axis=0, keepdims=True)
                if dr != 1:
                    s = s * vert
                k = 3 * dr + dw
                dcw_ref[k:k + 1, :] += s

    main = pl.BlockSpec((TB, cbk), lambda j, i: (i, j))
    halo = _halo_specs(cbk, T // GRID_W)
    acc9 = pl.BlockSpec((9, cbk), lambda j, i: (0, j))
    acc1 = pl.BlockSpec((1, cbk), lambda j, i: (0, j))
    return pl.pallas_call(
        body, name=name, grid=(dff // cbk, nb),
        in_specs=halo + _halo_specs(cbk, T // GRID_W, nvb) + halo + halo + [acc9],
        out_specs=[main, main, acc9, acc1],
        out_shape=[jax.ShapeDtypeStruct((T, dff), BF16), jax.ShapeDtypeStruct((T, dff), BF16),
                   jax.ShapeDtypeStruct((9, dff), F32), jax.ShapeDtypeStruct((1, dff), F32)],
        compiler_params=_params("parallel", "arbitrary"),
    )(up, up, up, up, up, up, ac, ac, ac, dact, dact, dact, cw)


def _ffn_out_fwd(act, xm, mod, wd, ctx_rows, name):
    T, D = xm.shape
    dff = act.shape[1]
    cb = ctx_rows // TB

    def body(act_ref, x_ref, mod_ref, w_ref, xo_ref, fo_ref):
        out = _nn(act_ref[...], _full(w_ref))
        fo_ref[...] = out
        xo_ref[...] = x_ref[...] + mod_ref[5:6, :] * out

    row = pl.BlockSpec((TB, D), lambda i: (i, 0))
    return pl.pallas_call(
        body, name=name, grid=(T // TB,),
        in_specs=[pl.BlockSpec((TB, dff), lambda i: (i, 0)), row,
                  pl.BlockSpec((None, N_MOD, D), lambda i: (_stream_of(i, cb), 0, 0)),
                  _rows_weight_spec(wd)],
        out_specs=[row, row],
        out_shape=[jax.ShapeDtypeStruct((T, D), F32), jax.ShapeDtypeStruct((T, D), F32)],
        compiler_params=_params("parallel"),
    )(act, xm, mod, wd)


def _ffn_out_bwd(dx, fo, mod, wd, ctx_rows, name):
    T, D = dx.shape
    dff = N_CHIPS * wd.shape[1]
    cb = ctx_rows // TB

    def body(dx_ref, fo_ref, mod_ref, w_ref, dout_ref, dact_ref, dg2_ref):
        i = pl.program_id(0)

        @pl.when((i == 0) | (i == cb))
        def _():
            dg2_ref[...] = jnp.zeros_like(dg2_ref)

        dxv = dx_ref[...]
        dg2_ref[...] += jnp.sum(dxv * fo_ref[...], axis=0, keepdims=True)
        dout = (dxv * mod_ref[5:6, :]).astype(BF16)
        dout_ref[...] = dout
        dact_ref[...] = _nt(dout, _full(w_ref))

    row = pl.BlockSpec((TB, D), lambda i: (i, 0))
    return pl.pallas_call(
        body, name=name, grid=(T // TB,),
        in_specs=[row, row, pl.BlockSpec((None, N_MOD, D), lambda i: (_stream_of(i, cb), 0, 0)),
                  _rows_weight_spec(wd)],
        out_specs=[row, pl.BlockSpec((TB, dff), lambda i: (i, 0)),
                   pl.BlockSpec((None, 1, D), lambda i: (_stream_of(i, cb), 0, 0))],
        out_shape=[jax.ShapeDtypeStruct((T, D), BF16), jax.ShapeDtypeStruct((T, dff), F32),
                   jax.ShapeDtypeStruct((2, 1, D), F32)],
        compiler_params=_params("arbitrary"),
    )(dx, fo, mod, wd)


def _loss_bwd(x, target, fw, ctx_rows, name):
    T, D = x.shape
    cb = ctx_rows // TB

    def body(x_ref, t_ref, fw_ref, dx_ref, loss_ref, dfw_ref):
        i = pl.program_id(0)

        @pl.when(i == 0)
        def _():
            loss_ref[...] = jnp.zeros_like(loss_ref)
            dfw_ref[...] = jnp.zeros_like(dfw_ref)

        @pl.when(i < cb)
        def _():
            dx_ref[...] = jnp.zeros_like(dx_ref)

        @pl.when(i >= cb)
        def _():
            xv = x_ref[...]
            r = lax.rsqrt(jnp.mean(xv * xv, axis=-1, keepdims=True) + RMS_EPS)
            xh = xv * r
            fwv = fw_ref[...]
            err = xh * fwv - t_ref[...]
            loss_ref[...] += (0.5 / D) * jnp.sum(err * err)
            dy = err * (1.0 / D)
            dfw_ref[...] += jnp.sum(dy * xh, axis=0, keepdims=True)
            dxh = dy * fwv
            dx_ref[...] = r * (dxh - xh * jnp.mean(dxh * xh, axis=-1, keepdims=True))

    row = pl.BlockSpec((TB, D), lambda i: (i, 0))
    return pl.pallas_call(
        body, name=name, grid=(T // TB,),
        in_specs=[row, pl.BlockSpec((TB, D), lambda i: (jnp.maximum(i - cb, 0), 0)), pl.BlockSpec((1, D), lambda i: (0, 0))],
        out_specs=[row, pl.BlockSpec((1, 128), lambda i: (0, 0)), pl.BlockSpec((1, D), lambda i: (0, 0))],
        out_shape=[jax.ShapeDtypeStruct((T, D), F32), jax.ShapeDtypeStruct((1, 128), F32),
                   jax.ShapeDtypeStruct((1, D), F32)],
        compiler_params=_params("arbitrary"),
    )(x, target, fw)


def _adamw(w, gs, m, v, name):
    L, R, C = w.shape
    assert len(gs) == L
    rb = _rows_tile(R, max(16, (1 << 18) // C // 16 * 16))
    bc1 = 1.0 - ADAM_B1 ** ADAM_STEP
    bc2 = 1.0 - ADAM_B2 ** ADAM_STEP

    def body(w_ref, m_ref, v_ref, *rest):
        g_refs, (g_ref, d_ref, nm_ref, nv_ref) = rest[:L], rest[L:]
        layer = pl.program_id(0)
        for li in range(L):
            @pl.when(layer == li)
            def _():
                gv = g_refs[li][...]
                g_ref[...] = gv
                nm = ADAM_B1 * m_ref[...] + (1.0 - ADAM_B1) * gv
                nv = ADAM_B2 * v_ref[...] + (1.0 - ADAM_B2) * (gv * gv)
                nm_ref[...] = nm
                nv_ref[...] = nv
                d_ref[...] = -ADAM_LR * ((nm / bc1) / (jnp.sqrt(nv / bc2) + ADAM_EPS) + ADAM_WD * w_ref[...])

    blk = pl.BlockSpec((None, rb, C), lambda l, i: (l, i, 0))
    gblk = pl.BlockSpec((rb, C), lambda l, i: (i, 0))
    sd = jax.ShapeDtypeStruct((L, R, C), F32)
    return pl.pallas_call(
        body, name=name, grid=(L, R // rb), in_specs=[blk] * 3 + [gblk] * L, out_specs=[blk] * 4, out_shape=[sd] * 4,
        compiler_params=_params("parallel", "parallel"),
    )(w, m, v, *gs)


def _local_step(xs, cv, target, W, layer_weights, on_layer_grads, ctx_rows):
    T, D = xs.shape
    depth = W["norm1_w"].shape[0]
    saved = []
    X = xs
    for l in range(depth):
        s = {}
        Wl = layer_weights(l, X)
        mod_all, sa = _mod_fwd(cv, Wl["ada_w"], W["ada_b"][l][None, :] + Wl["token"], f"mod_fwd_{l}")
        mod = mod_all[:2].reshape(2, N_MOD, D)
        h1 = _norm_mod(X, W["norm1_w"][l][None, :], mod, 0, ctx_rows, f"norm1_{l}")
        parts = _mm_nn_w(h1, Wl["w_in"], F32, f"in_proj_{l}")
        o_f, st_f = _hgrn_fwd(parts, W["hlb"], l, False, ctx_rows, f"hgrn_fwd_f_{l}")
        o, st_b = _hgrn_fwd(parts, W["hlb"], l, True, ctx_rows, f"hgrn_fwd_b_{l}", o_add=o_f)
        ya = _sgu_fwd(parts, W["sgu_ln_w"][l][None, :], W["sgu_ln_b"][l][None, :], W["sgu_w"][l], W["sgu_bt"][l],
                      f"sgu_fwd_{l}")
        Wl.update(Wl.pop("late")(ya))
        yb, pa, pb, mg, tmo, xm = _token_out_fwd(o, parts, ya, X, mod, W["hnw"][l][None, :] + Wl["late_token"], Wl["w_a"],
                                                 Wl["w_b"], Wl["w_o"], ctx_rows, f"token_out_fwd_{l}")
        h2 = _norm_mod(xm, W["norm2_w"][l][None, :], mod, 3, ctx_rows, f"norm2_{l}")
        up = _mm_nn_w(h2, Wl["w_up"], F32, f"up_proj_{l}")
        ac, act = _conv_fwd(up, Wl["conv_w"], W["conv_b"][l][None, :], ctx_rows, f"conv_fwd_{l}")
        xo, fo = _ffn_out_fwd(act, xm, mod, Wl["w_down"], ctx_rows, f"ffn_out_fwd_{l}")
        s.update(X=X, Wl=Wl, mod=mod, mod_all=mod_all, sa=sa, h1=h1, parts=parts, o=o, st_f=st_f, st_b=st_b, ya=ya, yb=yb,
                 pa=pa, pb=pb, mg=mg, tmo=tmo, xm=xm, h2=h2, up=up, ac=ac, act=act, fo=fo)
        saved.append(s)
        X = xo

    dX, loss_row, dfw = _loss_bwd(X, target, W["final_norm_w"][None, :], ctx_rows, "loss_bwd")
    G = {k: [None] * depth for k in ("ada_b", "norm1_w", "sgu_ln_w", "sgu_ln_b", "sgu_w", "sgu_b", "hlb1", "hnw", "norm2_w",
                                     "conv_w", "conv_b", "dmod")}
    dcv = jnp.zeros_like(cv)
    for l in reversed(range(depth)):
        s = saved[l]
        mod, Wl = s["mod"], s["Wl"]
        big = {}
        dout2, dact, dg2 = _ffn_out_bwd(dX, s["fo"], mod, Wl["w_down"], ctx_rows, f"ffn_out_bwd_{l}")
        big["w_down"] = _mm_tn(s["act"], dout2, F32, f"dw_down_{l}")
        da, dv, dcw, dcb = _conv_bwd(s["up"], s["ac"], dact, Wl["conv_w"], ctx_rows, f"conv_bwd_{l}")
        G["conv_w"][l], G["conv_b"][l] = dcw, dcb[0]
        dup = jnp.concatenate([da, dv], axis=1)
        big["w_up"] = _mm_tn(s["h2"], dup, F32, f"dw_up_{l}", out_chips=True)
        dh2 = _mm_nt_w(dup, Wl["w_up"], F32, f"dh2_{l}")
        dxm, dm2, dnw2 = _norm_mod_bwd(dh2, s["xm"], dX, W["norm2_w"][l][None, :], mod, 3, ctx_rows, f"norm2_bwd_{l}")
        G["norm2_w"][l] = dnw2[0]
        (dout1, dpa, dpb, dparts, dya, do, dg1, dhnw) = _token_out_bwd(
            dxm, s["tmo"], s["pa"], s["pb"], s["o"], s["parts"], mod, W["hnw"][l][None, :], Wl["w_a"], Wl["w_b"], Wl["w_o"],
            ctx_rows, f"token_out_bwd_{l}")
        G["hnw"][l] = dhnw[0]
        big["w_o"] = _mm_tn(s["mg"], dout1, F32, f"dw_o_{l}")
        big["w_a"] = _mm_tn(s["ya"], dpa, F32, f"dw_a_{l}")
        big["w_b"] = _mm_tn(s["yb"], dpb, F32, f"dw_b_{l}")
        tok = on_layer_grads(l, "early", big)
        dparts, dsw, dsbt, dlnw, dlnb = _sgu_bwd(s["parts"], dya, W["sgu_ln_w"][l][None, :], W["sgu_ln_b"][l][None, :] + tok,
                                                 W["sgu_w"][l], W["sgu_bt"][l], dparts, f"sgu_bwd_{l}")
        G["sgu_w"][l], G["sgu_b"][l], G["sgu_ln_w"][l], G["sgu_ln_b"][l] = dsw, dsbt.T, dlnw[0], dlnb[0]
        dq_f, dz_f, di_f, dlb_f = _hgrn_bwd(s["parts"], W["hlb"], do, s["st_f"], l, False, ctx_rows, f"hgrn_bwd_f_{l}")
        dparts, dlb_b = _hgrn_bwd(s["parts"], W["hlb"], do, s["st_b"], l, True, ctx_rows, f"hgrn_bwd_b_{l}",
                                  other=(dq_f, dz_f, di_f), dparts=dparts)
        G["hlb1"][l] = jnp.concatenate([dlb_f[0], dlb_b[0]])
        tok = on_layer_grads(l, "late", {"w_in": _mm_tn(s["h1"], dparts, F32, f"dw_in_{l}", out_chips=True)})
        dh1 = _mm_nt_w(dparts, Wl["w_in"], F32, f"dh1_{l}")
        tok = tok + on_layer_grads(l, "end", {"after": dh1})
        dX, dm1, dnw1 = _norm_mod_bwd(dh1, s["X"], dxm, W["norm1_w"][l][None, :] + tok, mod, 0, ctx_rows, f"norm1_bwd_{l}")
        G["norm1_w"][l] = dnw1[0]
        dmod = jnp.concatenate([dm1, dg1, dm2, dg2], axis=1).reshape(2, N_MOD * D)
        dmod16 = jnp.concatenate([dmod, jnp.zeros((cv.shape[0] - 2, N_MOD * D), F32)], axis=0)
        G["ada_b"][l] = dmod[0] + dmod[1]
        G["dmod"][l] = dmod
        dcv = dcv + _cvec_bwd(dmod16, Wl["ada_w"], cv, f"dcvec_{l}")
    G["c_ctx"] = dcv[0]
    G["final_norm_w"] = dfw[0]
    return loss_row[0, 0], dX, G, saved[0]["sa"]


def _chip_peers(x, y, c):
    return [((1 - x, y, c), 2 * (1 - x) + y), ((x, 1 - y, c), 2 * x + 1 - y), ((1 - x, 1 - y, c), 2 * (1 - x) + 1 - y)]


def _rdma_call(ins, out_shapes, plan, n_remote, n_local, name, aliases=None):
    n_in, n_out = len(ins), len(out_shapes)

    def body(*refs):
        in_refs, out_refs = refs[:n_in], refs[n_in:n_in + n_out]
        send_sems, recv_sems, local_sems = refs[n_in + n_out:]
        x, y, c = lax.axis_index("x"), lax.axis_index("y"), lax.axis_index("c")
        remote, local = plan(in_refs, out_refs, x, y, c)
        assert len(remote) == n_remote and len(local) == n_local, (name, len(remote), len(local))
        copies = [pltpu.make_async_copy(s, d, local_sems.at[i]) for i, (s, d) in enumerate(local)]
        copies += [pltpu.make_async_remote_copy(src_ref=s, dst_ref=d, send_sem=send_sems.at[k], recv_sem=recv_sems.at[k],
                                                device_id=dev, device_id_type=pl.DeviceIdType.MESH)
                   for k, (s, d, dev) in enumerate(remote)]
        for cp in copies:
            cp.start()
        for cp in copies:
            cp.wait()

    hbm = pl.BlockSpec(memory_space=pltpu.HBM)
    return pl.pallas_call(
        body, name=name, in_specs=[hbm] * n_in, out_specs=[hbm] * n_out, out_shape=out_shapes,
        scratch_shapes=[pltpu.SemaphoreType.DMA((n_remote,)), pltpu.SemaphoreType.DMA((n_remote,)),
                        pltpu.SemaphoreType.DMA((max(n_local, 1),))],
        input_output_aliases=aliases or {},
    )(*ins)


DMA_PIECE_BYTES = 1 << 18
DMA_MAX_PIECES = 8


def _row_pieces(shape, dtype):
    rows = shape[0]
    row_bytes = jnp.dtype(dtype).itemsize
    for d in shape[1:]:
        row_bytes *= d
    n = 1
    while n < DMA_MAX_PIECES and rows % (2 * n * 16) == 0 and rows * row_bytes // (2 * n) >= DMA_PIECE_BYTES:
        n *= 2
    return [(i * (rows // n), rows // n) for i in range(n)]


def _half_pieces(o, c):
    r2 = o.shape[1] // 2
    return [pl.ds(c * r2 + st, sz) for st, sz in _row_pieces((r2,) + o.shape[2:], o.dtype)]


def _n_half_pieces(arrays):
    return sum(len(_row_pieces((a.shape[1] // 2,) + a.shape[2:], a.dtype)) for a in arrays)


def _plan_gather_far(lands, x, y, c):
    me = 2 * x + y
    return [(o.at[me, rows], o.at[me, rows], dev) for dev, _ in _chip_peers(x, y, c) for o in lands
            for rows in _half_pieces(o, c)]


def _plan_gather_near(lands, x, y, c):
    return [(o.at[idx, rows], o.at[idx, rows], (x, y, 1 - c)) for _, idx in _chip_peers(x, y, c) for o in lands
            for rows in _half_pieces(o, c)]


def _gather_weights(lands, name):
    n = len(lands)
    n_far = (N_CHIPS - 1) * _n_half_pieces(lands)

    def body(*refs):
        outs = refs[n:2 * n]
        far_send, far_recv, near_send, near_recv = refs[2 * n:]
        x, y, c = lax.axis_index("x"), lax.axis_index("y"), lax.axis_index("c")
        mk = lambda plan, send, recv: [
            pltpu.make_async_remote_copy(src_ref=s, dst_ref=d, send_sem=send.at[k], recv_sem=recv.at[k], device_id=dev,
                                         device_id_type=pl.DeviceIdType.MESH)
            for k, (s, d, dev) in enumerate(plan(outs, x, y, c))]
        far, near = mk(_plan_gather_far, far_send, far_recv), mk(_plan_gather_near, near_send, near_recv)
        assert len(far) == n_far and len(near) == n_far
        for cp in far:
            cp.start()
        for k in range(n_far):
            far[k].wait_recv()
            near[k].start()
        for k in range(n_far):
            near[k].wait_recv()
        for cp in far + near:
            cp.wait_send()

    hbm = pl.BlockSpec(memory_space=pltpu.HBM)
    sems = pltpu.SemaphoreType.DMA((n_far,))
    return pl.pallas_call(
        body, name=name, in_specs=[hbm] * n, out_specs=[hbm] * n,
        out_shape=[jax.ShapeDtypeStruct(a.shape, a.dtype) for a in lands],
        scratch_shapes=[sems, sems, sems, sems], input_output_aliases={i: i for i in range(n)},
    )(*lands)


def _gather_all(v, name):
    def plan(ins, outs, x, y, c):
        (s,), (o,) = ins, outs
        me = 4 * x + 2 * y + c
        flip = lambda a, f: 1 - a if f else a
        remote = [(s, o.at[me], (flip(x, m & 4), flip(y, m & 2), flip(c, m & 1))) for m in range(1, 8)]
        return remote, [(s, o.at[me])]

    return _rdma_call([v], [jax.ShapeDtypeStruct((8,) + v.shape, v.dtype)], plan, 7, 1, name)[0]


def _plan_pair(ins, lands, x, y, c):
    return [(a.at[j, 1 - c, pl.ds(st, sz)], o.at[j, pl.ds(st, sz)], (x, y, 1 - c)) for a, o in zip(ins, lands)
            for j in range(N_CHIPS) for st, sz in _row_pieces(a.shape[2:], a.dtype)]


def _n_pair_copies(parts):
    return N_CHIPS * sum(len(_row_pieces(a.shape[2:], a.dtype)) for a in parts)


def _reduce_pair(parts, name):
    shapes = [jax.ShapeDtypeStruct((N_CHIPS,) + a.shape[2:], a.dtype) for a in parts]
    return _rdma_call(parts, shapes, lambda ins, outs, x, y, c: (_plan_pair(ins, outs, x, y, c), []),
                      _n_pair_copies(parts), 0, name)


def _plan_chips(ins, lands, x, y, c):
    me = 2 * x + y
    return [(a.at[idx, pl.ds(st, sz)], o.at[me, pl.ds(st, sz)], dev) for dev, idx in _chip_peers(x, y, c)
            for a, o in zip(ins, lands) for st, sz in _row_pieces(a.shape[1:], a.dtype)]


def _n_chips_copies(parts):
    return (N_CHIPS - 1) * sum(len(_row_pieces(a.shape[1:], a.dtype)) for a in parts)


def _reduce_chips(parts, name):
    shapes = [jax.ShapeDtypeStruct(a.shape, a.dtype) for a in parts]
    return _rdma_call(parts, shapes, lambda ins, outs, x, y, c: (_plan_chips(ins, outs, x, y, c), []),
                      _n_chips_copies(parts), 0, name)


def _gather_pair(halves, name):
    def plan(ins, outs, x, y, c):
        return [(o.at[c, pl.ds(st, sz)], o.at[c, pl.ds(st, sz)], (x, y, 1 - c)) for o in outs
                for st, sz in _row_pieces(o.shape[1:], o.dtype)], []

    shapes = [jax.ShapeDtypeStruct(a.shape, a.dtype) for a in halves]
    n_remote = sum(len(_row_pieces(a.shape[1:], a.dtype)) for a in halves)
    return _rdma_call(halves, shapes, plan, n_remote, 0, name, aliases={i: i for i in range(len(halves))})


def _split_start(ins, lands, plan, n_remote, name):
    n_buf = len(ins) + len(lands)

    def body(*refs):
        in_refs, land_refs = refs[:len(ins)], refs[len(ins):n_buf]
        send_sems, recv_sems, token = refs[n_buf], refs[n_buf + 1], refs[-1]
        x, y, c = lax.axis_index("x"), lax.axis_index("y"), lax.axis_index("c")
        remote = plan(in_refs, land_refs, x, y, c)
        assert len(remote) == n_remote, (name, len(remote))
        for k, (s, d, dev) in enumerate(remote):
            pltpu.make_async_remote_copy(src_ref=s, dst_ref=d, send_sem=send_sems.at[k], recv_sem=recv_sems.at[k],
                                         device_id=dev, device_id_type=pl.DeviceIdType.MESH).start()
        token[...] = jnp.zeros_like(token)

    hbm = pl.BlockSpec(memory_space=pltpu.HBM)
    sem = pl.BlockSpec(memory_space=pltpu.SEMAPHORE)
    bufs = list(ins) + list(lands)
    out = pl.pallas_call(
        body, name=name, in_specs=[hbm] * n_buf,
        out_specs=(sem, sem) + (hbm,) * n_buf + (pl.BlockSpec(memory_space=pltpu.VMEM),),
        out_shape=(pltpu.SemaphoreType.DMA((n_remote,)), pltpu.SemaphoreType.DMA((n_remote,)))
        + tuple(pltpu.HBM(a.shape, a.dtype) for a in bufs) + (jax.ShapeDtypeStruct((8, 128), F32),),
        input_output_aliases={i: 2 + i for i in range(n_buf)},
        compiler_params=pltpu.CompilerParams(has_side_effects=pltpu.SideEffectType.DATAFLOW_SIDE_EFFECTING),
    )(*[pltpu.with_memory_space_constraint(a, pltpu.HBM) for a in bufs])
    return dict(send=out[0], recv=out[1], ins=list(out[2:2 + len(ins)]), lands=list(out[2 + len(ins):2 + n_buf]),
                token=out[-1][0, 0], plan=plan, n_remote=n_remote)


def _split_wait(st, after, name):
    n_in, n_buf = len(st["ins"]), len(st["ins"]) + len(st["lands"])
    plan, n_remote = st["plan"], st["n_remote"]

    def body(*refs):
        in_refs, land_refs = refs[:n_in], refs[n_in:n_buf]
        send_sems, recv_sems = refs[n_buf], refs[n_buf + 1]
        x, y, c = lax.axis_index("x"), lax.axis_index("y"), lax.axis_index("c")
        for k, (s, d, dev) in enumerate(plan(in_refs, land_refs, x, y, c)):
            cp = pltpu.make_async_remote_copy(src_ref=s, dst_ref=d, send_sem=send_sems.at[k], recv_sem=recv_sems.at[k],
                                              device_id=dev, device_id_type=pl.DeviceIdType.MESH)
            cp.wait_send()
            cp.wait_recv()

    hbm = pl.BlockSpec(memory_space=pltpu.HBM)
    sem = pl.BlockSpec(memory_space=pltpu.SEMAPHORE)
    bufs = st["ins"] + st["lands"]
    out = pl.pallas_call(
        body, name=name, in_specs=[hbm] * n_buf + [sem, sem, pl.BlockSpec(memory_space=pl.ANY)],
        out_specs=[hbm] * n_buf, out_shape=[pltpu.HBM(a.shape, a.dtype) for a in bufs],
        input_output_aliases={i: i for i in range(n_buf)},
        compiler_params=pltpu.CompilerParams(has_side_effects=pltpu.SideEffectType.DATAFLOW_SIDE_EFFECTING),
    )(*bufs, st["send"], st["recv"], after)
    return list(out[:n_in]), list(out[n_in:])


def _pair_forward(lands, name):
    shapes = [jax.ShapeDtypeStruct(a.shape, a.dtype) for a in lands]
    return _rdma_call(lands, shapes, lambda ins, outs, x, y, c: (_plan_gather_near(outs, x, y, c), []),
                      (N_CHIPS - 1) * _n_half_pieces(lands), 0, name, aliases={i: i for i in range(len(lands))})


def _sum_block_rows(r, C):
    return _rows_tile(r, max(16, (1 << 18) // C // 16 * 16))


def _sum_pair(a, recv, cidx, name):
    nch, _, r, C = a.shape
    rb = _sum_block_rows(r, C)

    def body(c_ref, a_ref, r_ref, o_ref):
        o_ref[...] = (a_ref[...] + r_ref[...]).astype(BF16)

    blk = pl.BlockSpec((None, rb, C), lambda j, i, c: (j, i, 0))
    return pl.pallas_call(
        body, name=name,
        grid_spec=pltpu.PrefetchScalarGridSpec(
            num_scalar_prefetch=1, grid=(nch, r // rb),
            in_specs=[pl.BlockSpec((None, None, rb, C), lambda j, i, c: (j, c[0], i, 0)), blk], out_specs=blk),
        out_shape=jax.ShapeDtypeStruct((nch, r, C), BF16),
        compiler_params=_params("parallel", "parallel"),
    )(cidx, a, recv)


def _sum_chips(mine, recv, ids, name):
    nch, r, C = recv.shape
    rb = _sum_block_rows(r, C)

    def body(ids_ref, m_ref, *rest):
        r_refs, o_ref = rest[:nch], rest[nch]
        chip = ids_ref[1]
        own = m_ref[...].astype(F32)
        acc = jnp.where(chip == 0, own, r_refs[0][...].astype(F32))
        for q in range(1, nch):
            acc = acc + jnp.where(chip == q, own, r_refs[q][...].astype(F32))
        o_ref[...] = acc

    def slot(q):
        return pl.BlockSpec((None, rb, C), lambda i, ids: (jnp.where(ids[1] == q, (q + 1) % nch, q), i, 0))

    return pl.pallas_call(
        body, name=name,
        grid_spec=pltpu.PrefetchScalarGridSpec(
            num_scalar_prefetch=1, grid=(r // rb,),
            in_specs=[pl.BlockSpec((None, rb, C), lambda i, ids: (ids[1], i, 0))] + [slot(q) for q in range(nch)],
            out_specs=pl.BlockSpec((None, rb, C), lambda i, ids: (ids[0], i, 0))),
        out_shape=jax.ShapeDtypeStruct((N_CORES, r, C), F32),
        compiler_params=_params("parallel"),
    )(ids, mine, *([recv] * nch))


PACK_COLS = 1024
_SHARDED = ("ada_w", "w_in", "w_branch_a", "w_branch_b", "w_out", "ffn_w_up", "ffn_w_down")
_LAYER_KEYS = ("ada_w", "w_in", "w_a", "w_b", "w_o", "w_up", "w_down")
_SMALL = ("c_ctx", "ada_b", "norm1_w", "sgu_ln_w", "sgu_ln_b", "sgu_w", "sgu_b", "hgrn_lower_bounds", "hgrn_norm_w",
          "norm2_w", "ffn_conv_b", "final_norm_w")
_ORDER = ("c_ctx", "ada_w", "ada_b", "norm1_w", "w_in", "sgu_ln_w", "sgu_ln_b", "sgu_w", "sgu_b", "hgrn_lower_bounds",
          "hgrn_norm_w", "w_branch_a", "w_branch_b", "w_out", "norm2_w", "ffn_w_up", "ffn_conv_w", "ffn_conv_b",
          "ffn_w_down", "final_norm_w")


def _pad_to(v, n):
    return jnp.concatenate([v, jnp.zeros((n - v.shape[0],), v.dtype)]) if v.shape[0] < n else v


def _round_up(n, m):
    return (n + m - 1) // m * m


def _pack(arrays, n_pad):
    flat = jnp.concatenate([a.reshape(-1) for a in arrays])
    return _pad_to(flat, n_pad)


def _unpack(flat, like):
    out, off = [], 0
    for a in like:
        out.append(flat[off:off + a.size].reshape(a.shape))
        off += a.size
    return out


def kernel(x, c, ctx, c_ctx, ada_w, ada_b, norm1_w, w_in, sgu_ln_w, sgu_ln_b, sgu_w, sgu_b, hgrn_lower_bounds, hgrn_norm_w, w_branch_a, w_branch_b, w_out, norm2_w, ffn_w_up, ffn_conv_w, ffn_conv_b, ffn_w_down, final_norm_w, loss_target, m_c_ctx, m_ada_w, m_ada_b, m_norm1_w, m_w_in, m_sgu_ln_w, m_sgu_ln_b, m_sgu_w, m_sgu_b, m_hgrn_lower_bounds, m_hgrn_norm_w, m_w_branch_a, m_w_branch_b, m_w_out, m_norm2_w, m_ffn_w_up, m_ffn_conv_w, m_ffn_conv_b, m_ffn_w_down, m_final_norm_w, v_c_ctx, v_ada_w, v_ada_b, v_norm1_w, v_w_in, v_sgu_ln_w, v_sgu_ln_b, v_sgu_w, v_sgu_b, v_hgrn_lower_bounds, v_hgrn_norm_w, v_w_branch_a, v_w_branch_b, v_w_out, v_norm2_w, v_ffn_w_up, v_ffn_conv_w, v_ffn_conv_b, v_ffn_w_down, v_final_norm_w):
    w = dict(c_ctx=c_ctx, ada_w=ada_w, ada_b=ada_b, norm1_w=norm1_w, w_in=w_in, sgu_ln_w=sgu_ln_w, sgu_ln_b=sgu_ln_b,
             sgu_w=sgu_w, sgu_b=sgu_b, hgrn_lower_bounds=hgrn_lower_bounds, hgrn_norm_w=hgrn_norm_w, w_branch_a=w_branch_a,
             w_branch_b=w_branch_b, w_out=w_out, norm2_w=norm2_w, ffn_w_up=ffn_w_up, ffn_conv_w=ffn_conv_w,
             ffn_conv_b=ffn_conv_b, ffn_w_down=ffn_w_down, final_norm_w=final_norm_w)
    mom = dict(zip(_ORDER, (m_c_ctx, m_ada_w, m_ada_b, m_norm1_w, m_w_in, m_sgu_ln_w, m_sgu_ln_b, m_sgu_w, m_sgu_b,
                            m_hgrn_lower_bounds, m_hgrn_norm_w, m_w_branch_a, m_w_branch_b, m_w_out, m_norm2_w, m_ffn_w_up,
                            m_ffn_conv_w, m_ffn_conv_b, m_ffn_w_down, m_final_norm_w)))
    var = dict(zip(_ORDER, (v_c_ctx, v_ada_w, v_ada_b, v_norm1_w, v_w_in, v_sgu_ln_w, v_sgu_ln_b, v_sgu_w, v_sgu_b,
                            v_hgrn_lower_bounds, v_hgrn_norm_w, v_w_branch_a, v_w_branch_b, v_w_out, v_norm2_w, v_ffn_w_up,
                            v_ffn_conv_w, v_ffn_conv_b, v_ffn_w_down, v_final_norm_w)))
    depth, D = norm1_w.shape
    dff = ffn_conv_b.shape[1]
    ctx_rows, seq = ctx.shape[1], x.shape[1]

    assert depth == 2, "the lower-bound softmax is written for two layers"
    core = lax.axis_index("c")
    chip = 2 * lax.axis_index("x") + lax.axis_index("y")
    ids = jnp.stack([core, chip]).astype(jnp.int32)

    first, rest = _LAYER_KEYS[:2], _LAYER_KEYS[2:]
    shard = lambda l, k: w[_SHARDED[_LAYER_KEYS.index(k)]][l].astype(BF16)
    started, conv_full = {}, []

    def landing(s):
        return lax.dynamic_update_slice(lax.empty((N_CHIPS,) + s.shape, s.dtype), s[None], (chip,) + (0,) * s.ndim)

    def start_gather(l, keys, tag):
        lands = [landing(shard(l, k)) for k in keys]
        started[tag] = _split_start([], lands, lambda ins, lds, x, y, c: _plan_gather_far(lds, x, y, c),
                                    (N_CHIPS - 1) * _n_half_pieces(lands), f"gather_start_{tag}")
        return started[tag]["token"]

    def finish_gather(keys, tag, after):
        _, lands = _split_wait(started[tag], after, f"gather_wait_{tag}")
        return dict(zip(keys, _pair_forward(lands, f"gather_forward_{tag}")))

    def layer_weights(l, after):
        if l == 0:
            got = _gather_weights([landing(shard(0, k)) for k in first] + [landing(ffn_conv_w)], "gather_weights_first")
            conv_full.append(jnp.transpose(got[-1], (1, 2, 3, 0, 4)).reshape(depth, 9, dff))
            out = dict(zip(first, got), token=start_gather(0, rest, "rest_0"))
        else:
            out = dict(finish_gather(first, f"first_{l}", after), token=0.0)

        def late(after_late):
            more = finish_gather(rest, f"rest_{l}", after_late)
            more["late_token"] = 0.0
            if l + 1 < depth:
                more["late_token"] = start_gather(l + 1, first, f"first_{l + 1}") + start_gather(l + 1, rest, f"rest_{l + 1}")
            return more

        return dict(out, conv_w=conv_full[0][l], late=late)

    groups, order = {}, []

    def as_parts(gs):
        return [g.reshape(N_CHIPS, N_CORES, g.size // (N_CHIPS * N_CORES * g.shape[-1]), g.shape[-1]) for g in gs]

    def pair_start(tag, l, keys, gs):
        parts = as_parts(gs)
        lands = [lax.empty((N_CHIPS,) + p.shape[2:], p.dtype) for p in parts]
        groups[tag] = dict(l=l, keys=keys, pair=_split_start(parts, lands, _plan_pair, _n_pair_copies(parts),
                                                             f"reduce_pair_start_{tag}"))
        order.append(tag)
        return groups[tag]["pair"]["token"]

    def chips_start(tag, after):
        parts, other = _split_wait(groups[tag]["pair"], after, f"reduce_pair_wait_{tag}")
        sums = [_sum_pair(a, o, ids, f"sum_pair_{tag}_{i}") for i, (a, o) in enumerate(zip(parts, other))]
        lands = [lax.empty(s.shape, s.dtype) for s in sums]
        groups[tag]["chips"] = _split_start(sums, lands, _plan_chips, _n_chips_copies(sums), f"reduce_chips_start_{tag}")
        return groups[tag]["chips"]["token"]

    def chips_finish(tag, after):
        sums, recv = _split_wait(groups[tag]["chips"], after, f"reduce_chips_wait_{tag}")
        return {(groups[tag]["l"], k): _sum_chips(sums[i], recv[i], ids, f"sum_chips_{tag}_{i}")
                for i, k in enumerate(groups[tag]["keys"])}

    def on_layer_grads(l, stage, gs):
        if stage == "early":
            return pair_start(f"early_{l}", l, list(gs), list(gs.values()))
        if stage == "late":
            return pair_start(f"late_{l}", l, ["w_in"], [gs["w_in"]]) + chips_start(f"early_{l}", gs["w_in"])
        return chips_start(f"late_{l}", gs["after"])

    W = dict(ada_b=ada_b, norm1_w=norm1_w, sgu_ln_w=sgu_ln_w, sgu_ln_b=sgu_ln_b, sgu_w=sgu_w.astype(BF16),
             sgu_bt=jnp.swapaxes(sgu_b, 1, 2), hlb=hgrn_lower_bounds, hnw=hgrn_norm_w, norm2_w=norm2_w, conv_b=ffn_conv_b,
             final_norm_w=final_norm_w)
    xs = jnp.concatenate([ctx[0], x[0]], axis=0)
    cv = jnp.concatenate([c_ctx[None, :], c, jnp.zeros((14, D), F32)], axis=0)
    loss_local, dxs, G, sa = _local_step(xs, cv, loss_target[0], W, layer_weights, on_layer_grads, ctx_rows)
    loss = lax.psum(loss_local, ("x", "y", "c"))
    grad_x = dxs[ctx_rows:][None]

    pad8 = lambda a: jnp.pad(a, ((0, 8 - a.shape[0]), (0, 0)))
    fact = jnp.concatenate([pad8(sa[1:2].astype(F32))] + [pad8(G["dmod"][l][1].reshape(N_MOD, D)) for l in range(depth)]
                           + [pad8(G["dmod"][l][0].reshape(N_MOD, D)) for l in range(depth)], axis=0)
    facts = _gather_all(fact, "gather_mod_factors")
    lhs = jnp.concatenate([facts[:, 0].astype(BF16), jnp.broadcast_to(sa[0:1], (8, D))], axis=0)
    ada_cols = N_MOD * D // N_CHIPS
    g_ada = []
    for l in range(depth):
        lo_x, lo_c = 8 * (1 + l), 8 * (1 + depth + l)
        rhs = jnp.concatenate([facts[:, lo_x:lo_x + N_MOD].reshape(8, N_MOD * D),
                               facts[:, lo_c:lo_c + N_MOD].reshape(8, N_MOD * D)], axis=0)
        rhs = lax.dynamic_slice_in_dim(rhs, chip * ada_cols, ada_cols, axis=1).astype(BF16)
        g_ada.append(_mm_tn(lhs, rhs, F32, f"dw_ada_{l}"))

    dh = G["hlb1"][depth - 1]
    small_like = [w[k] for k in _SMALL] + [jnp.zeros((depth, 9, dff), F32)]
    small = [G["c_ctx"], jnp.stack(G["ada_b"]), jnp.stack(G["norm1_w"]), jnp.stack(G["sgu_ln_w"]), jnp.stack(G["sgu_ln_b"]),
             jnp.stack(G["sgu_w"]), jnp.stack(G["sgu_b"]), jnp.stack([-dh, dh]), jnp.stack(G["hnw"]), jnp.stack(G["norm2_w"]),
             jnp.stack(G["conv_b"]), G["final_norm_w"], jnp.stack(G["conv_w"])]
    n_small = sum(a.size for a in small)
    n_small_pad = _round_up(n_small, N_CORES * 16 * PACK_COLS)
    small_rows = n_small_pad // (N_CORES * PACK_COLS)
    small_rep = jnp.broadcast_to(_pack(small, n_small_pad).reshape(1, N_CORES, small_rows, PACK_COLS),
                                 (N_CHIPS, N_CORES, small_rows, PACK_COLS))
    small_parts = as_parts([small_rep])
    small_sums = [_sum_pair(small_parts[0], _reduce_pair(small_parts, "reduce_pair_small")[0], ids, "sum_pair_small")]
    small_half = _sum_chips(small_sums[0], _reduce_chips(small_sums, "reduce_chips_small")[0], ids, "sum_chips_small")

    def gather_halves(halves, name):
        return dict(zip(halves, _gather_pair(list(halves.values()), name)))

    last = order[-1]
    halves = {("small",): small_half}
    for tag in order[:-1]:
        halves.update(chips_finish(tag, dxs))
    reduced = gather_halves(halves, "gather_pair")

    g_small = _unpack(reduced[("small",)].reshape(-1), small_like)
    grads = dict(zip(_SMALL, g_small[:-1]))
    grads["ffn_conv_w"] = lax.dynamic_slice_in_dim(g_small[-1].reshape(depth, 3, 3, dff), chip * (dff // N_CHIPS),
                                                   dff // N_CHIPS, axis=3)
    delta, new_m, new_v = {}, {}, {}

    def adamw_sharded(i):
        k = _SHARDED[i]
        gs = g_ada if i == 0 else [reduced[(l, _LAYER_KEYS[i])].reshape(w[k].shape[1:]) for l in range(depth)]
        grads[k], delta[k], new_m[k], new_v[k] = _adamw(w[k], gs, mom[k], var[k], f"adamw_{k}")

    last_keys = groups[last]["keys"]
    for i in range(len(_SHARDED)):
        if _LAYER_KEYS[i] not in last_keys:
            adamw_sharded(i)
    packed = _SMALL + ("ffn_conv_w",)
    n_pad = _round_up(sum(w[k].size for k in packed), 16 * PACK_COLS)
    pack = lambda t: _pack([t[k] for k in packed], n_pad).reshape(1, -1, PACK_COLS)
    _, d, nm, nv = _adamw(pack(w), [pack(grads)[0]], pack(mom), pack(var), "adamw_packed")
    like = [w[k] for k in packed]
    for src, dst in ((d, delta), (nm, new_m), (nv, new_v)):
        dst.update(zip(packed, _unpack(src.reshape(-1), like)))
    reduced.update(gather_halves(chips_finish(last, nv), "gather_pair_last"))
    for i in range(len(_SHARDED)):
        if _LAYER_KEYS[i] in last_keys:
            adamw_sharded(i)

    return (loss, grad_x, *[grads[k] for k in _ORDER], *[delta[k] for k in _ORDER], *[new_m[k] for k in _ORDER],
            *[new_v[k] for k in _ORDER])
```

```python
import functools

import jax
import jax.numpy as jnp
from jax import lax
from jax.experimental import pallas as pl
from jax.experimental.pallas import tpu as pltpu

F32 = jnp.float32
BF16 = jnp.bfloat16

GRID_W = 64
HG_CHUNK = 64
SGU_CHUNK = 128
HEAD = 128
TB = 256
N_MOD = 6
RMS_EPS = 1e-6
LN_EPS = 1e-5
VMEM_LIMIT = 48 * 1024 * 1024
N_CHIPS = 4
N_CORES = 2

ADAM_LR = 0.001
ADAM_B1 = 0.9
ADAM_B2 = 0.999
ADAM_EPS = 1e-08
ADAM_WD = 0.01
ADAM_STEP = 10

_GELU_C = 0.7978845608028654
_GELU_A = 0.044715


def _sigmoid(x):
    return 0.5 * jnp.tanh(0.5 * x) + 0.5


def _silu(x):
    return x * _sigmoid(x)


def _silu_both(x):
    s = _sigmoid(x)
    return x * s, s * (1.0 + x * (1.0 - s))


def _dsilu(x):
    return _silu_both(x)[1]


def _gelu_both(x):
    x2 = x * x
    t = jnp.tanh(_GELU_C * (x + _GELU_A * x2 * x))
    h = 0.5 * (1.0 + t)
    return x * h, h + 0.5 * x * (1.0 - t * t) * (_GELU_C + 3.0 * _GELU_C * _GELU_A * x2)


def _gelu(x):
    return 0.5 * x * (1.0 + jnp.tanh(_GELU_C * (x + _GELU_A * x * x * x)))


def _dgelu(x):
    return _gelu_both(x)[1]


def _dot(a, b, ca, cb):
    return lax.dot_general(a, b, (((ca,), (cb,)), ((), ())), preferred_element_type=F32)


def _nn(a, b):
    return _dot(a, b, 1, 0)


def _nt(a, b):
    return _dot(a, b, 1, 1)


def _tn(a, b):
    return _dot(a, b, 0, 0)


def _params(*sem, vmem=VMEM_LIMIT):
    return pltpu.CompilerParams(dimension_semantics=sem if sem else None, vmem_limit_bytes=vmem)


def _stream_of(i, ctx_blocks):
    return (i >= ctx_blocks).astype(jnp.int32)


def _mm(a, b, mode, tm, tn, tk, out_dtype, name, add=None, b_chips=False, out_chips=False):
    if not b_chips:
        bshape = b.shape
    else:
        bshape = (b.shape[1], N_CHIPS * b.shape[2])
    if mode == "nn":
        (M, K), (K2, N) = a.shape, bshape
    elif mode == "nt":
        (M, K), (N, K2) = a.shape, bshape
    else:
        (K, M), (K2, N) = a.shape, bshape
    assert K == K2 and M % tm == 0 and N % tn == 0 and K % tk == 0, (name, a.shape, b.shape, tm, tn, tk)
    nk = K // tk
    if mode == "tn":
        a_spec = pl.BlockSpec((tk, tm), lambda j, i, k: (k, i))
    else:
        a_spec = pl.BlockSpec((tm, tk), lambda j, i, k: (i, k))
    if not b_chips:
        if mode == "nt":
            b_spec = pl.BlockSpec((tn, tk), lambda j, i, k: (j, k))
        else:
            b_spec = pl.BlockSpec((tk, tn), lambda j, i, k: (k, j))
    else:
        cols = b.shape[2]
        if mode == "nn":
            per = cols // tn
            assert cols % tn == 0
            b_spec = pl.BlockSpec((None, tk, tn), lambda j, i, k: (j // per, k, j % per))
        else:
            per = cols // tk
            assert mode == "nt" and cols % tk == 0
            b_spec = pl.BlockSpec((None, tn, tk), lambda j, i, k: (k // per, j, k % per))
    if out_chips:
        per_o = (N // N_CHIPS) // tn
        assert (N // N_CHIPS) % tn == 0 and add is None
        o_spec = pl.BlockSpec((None, tm, tn), lambda j, i, k: (j // per_o, i, j % per_o))
        o_shape = (N_CHIPS, M, N // N_CHIPS)
    else:
        o_spec = pl.BlockSpec((tm, tn), lambda j, i, k: (i, j))
        o_shape = (M, N)
    ca, cb = {"nn": (1, 0), "nt": (1, 1), "tn": (0, 0)}[mode]

    def body(a_ref, b_ref, *rest):
        if add is None:
            o_ref, acc = rest
        else:
            add_ref, o_ref, acc = rest
        k = pl.program_id(2)

        @pl.when(k == 0)
        def _():
            acc[...] = jnp.zeros_like(acc)

        acc[...] += _dot(a_ref[...], b_ref[...], ca, cb)

        @pl.when(k == nk - 1)
        def _():
            r = acc[...]
            if add is not None:
                r = r + add_ref[...]
            o_ref[...] = r.astype(out_dtype)

    ins = [a, b] + ([] if add is None else [add])
    specs = [a_spec, b_spec] + ([] if add is None else [o_spec])
    return pl.pallas_call(
        body, name=name, grid=(N // tn, M // tm, nk), in_specs=specs, out_specs=o_spec,
        out_shape=jax.ShapeDtypeStruct(o_shape, out_dtype),
        scratch_shapes=[pltpu.VMEM((tm, tn), F32)],
        compiler_params=_params("parallel", "parallel", "arbitrary"),
    )(*ins)


def _tile(n, pref):
    if n <= pref:
        return n
    best = None
    for t in range(128, pref + 1, 128):
        if n % t == 0:
            best = t
    assert best is not None, (n, pref)
    return best


def _rows_tile(n, pref):
    if n <= pref:
        return n
    best = None
    for t in range(16, pref + 1, 16):
        if n % t == 0:
            best = t
    assert best is not None, (n, pref)
    return best


def _mm_nn_w(a, wg, out_dtype, name):
    M, K = a.shape
    return _mm(a, wg, "nn", _rows_tile(M, 1088), _tile(wg.shape[2], 1536), _tile(K, 1536), out_dtype, name, b_chips=True)


def _mm_nt_w(a, wg, out_dtype, name):
    M, K = a.shape
    return _mm(a, wg, "nt", _rows_tile(M, 1088), _tile(wg.shape[1], 1024), _tile(wg.shape[2], 1536), out_dtype, name,
               b_chips=True)


def _mm_tn(a, b, out_dtype, name, out_chips=False):
    K, M = a.shape
    N = b.shape[1]
    ncol = N // N_CHIPS if out_chips else N
    tm, tn = _tile(M, 1408), _tile(ncol, 1408)
    if tm * tn > 1408 * 1152:
        tn = _tile(ncol, 1152)
    return _mm(a, b, "tn", tm, tn, _rows_tile(K, 2176), out_dtype, name, out_chips=out_chips)


def _mod_fwd(cv, wg, b, name):
    R, D = cv.shape
    tn = wg.shape[2]
    N = N_CHIPS * tn

    def body(cv_ref, w_ref, b_ref, mod_ref, sa_ref):
        sa = _silu(cv_ref[...]).astype(BF16)
        sa_ref[...] = sa
        mod_ref[...] = _nn(sa, w_ref[...]) + b_ref[...]

    return pl.pallas_call(
        body, name=name, grid=(N_CHIPS,),
        in_specs=[pl.BlockSpec((R, D), lambda j: (0, 0)), pl.BlockSpec((None, D, tn), lambda j: (j, 0, 0)),
                  pl.BlockSpec((1, tn), lambda j: (0, j))],
        out_specs=[pl.BlockSpec((R, tn), lambda j: (0, j)), pl.BlockSpec((R, D), lambda j: (0, 0))],
        out_shape=[jax.ShapeDtypeStruct((R, N), F32), jax.ShapeDtypeStruct((R, D), BF16)],
        compiler_params=_params("arbitrary"),
    )(cv, wg, b)


def _cvec_bwd(dmod, wg, cv, name):
    R, N = dmod.shape
    D = wg.shape[1]
    tk = wg.shape[2]
    nk = N_CHIPS

    def body(dm_ref, w_ref, cv_ref, o_ref):
        k = pl.program_id(0)

        @pl.when(k == 0)
        def _():
            o_ref[...] = jnp.zeros_like(o_ref)

        o_ref[...] += _nt(dm_ref[...].astype(BF16), w_ref[...])

        @pl.when(k == nk - 1)
        def _():
            o_ref[...] = o_ref[...] * _dsilu(cv_ref[...])

    return pl.pallas_call(
        body, name=name, grid=(nk,),
        in_specs=[pl.BlockSpec((R, tk), lambda k: (0, k)), pl.BlockSpec((None, D, tk), lambda k: (k, 0, 0)),
                  pl.BlockSpec((R, D), lambda k: (0, 0))],
        out_specs=pl.BlockSpec((R, D), lambda k: (0, 0)),
        out_shape=jax.ShapeDtypeStruct((R, D), F32),
        compiler_params=_params("arbitrary"),
    )(dmod, wg, cv)


def _norm_mod(x, nw, mod, which, ctx_rows, name):
    T, D = x.shape
    cb = ctx_rows // TB

    def body(x_ref, nw_ref, mod_ref, h_ref):
        xv = x_ref[...]
        r = lax.rsqrt(jnp.mean(xv * xv, axis=-1, keepdims=True) + RMS_EPS)
        y = xv * r * nw_ref[...]
        sh = mod_ref[which:which + 1, :]
        sc = mod_ref[which + 1:which + 2, :]
        h_ref[...] = (y * (1.0 + sc) + sh).astype(BF16)

    return pl.pallas_call(
        body, name=name, grid=(T // TB,),
        in_specs=[pl.BlockSpec((TB, D), lambda i: (i, 0)), pl.BlockSpec((1, D), lambda i: (0, 0)),
                  pl.BlockSpec((None, N_MOD, D), lambda i: (_stream_of(i, cb), 0, 0))],
        out_specs=pl.BlockSpec((TB, D), lambda i: (i, 0)),
        out_shape=jax.ShapeDtypeStruct((T, D), BF16),
        compiler_params=_params("parallel"),
    )(x, nw, mod)


def _norm_mod_bwd(dh, x, dres, nw, mod, which, ctx_rows, name):
    T, D = x.shape
    cb = ctx_rows // TB

    def body(dh_ref, x_ref, dres_ref, nw_ref, mod_ref, dx_ref, dm_ref, dnw_ref):
        i = pl.program_id(0)

        @pl.when(i == 0)
        def _():
            dnw_ref[...] = jnp.zeros_like(dnw_ref)

        @pl.when((i == 0) | (i == cb))
        def _():
            dm_ref[...] = jnp.zeros_like(dm_ref)

        xv = x_ref[...]
        dh = dh_ref[...]
        r = lax.rsqrt(jnp.mean(xv * xv, axis=-1, keepdims=True) + RMS_EPS)
        xh = xv * r
        nwv = nw_ref[...]
        sc = mod_ref[which + 1:which + 2, :]
        y = xh * nwv
        dm_ref[0:1, :] += jnp.sum(dh, axis=0, keepdims=True)
        dm_ref[1:2, :] += jnp.sum(dh * y, axis=0, keepdims=True)
        dy = dh * (1.0 + sc)
        dnw_ref[...] += jnp.sum(dy * xh, axis=0, keepdims=True)
        dxh = dy * nwv
        dx_ref[...] = dres_ref[...] + r * (dxh - xh * jnp.mean(dxh * xh, axis=-1, keepdims=True))

    return pl.pallas_call(
        body, name=name, grid=(T // TB,),
        in_specs=[pl.BlockSpec((TB, D), lambda i: (i, 0)), pl.BlockSpec((TB, D), lambda i: (i, 0)),
                  pl.BlockSpec((TB, D), lambda i: (i, 0)), pl.BlockSpec((1, D), lambda i: (0, 0)),
                  pl.BlockSpec((None, N_MOD, D), lambda i: (_stream_of(i, cb), 0, 0))],
        out_specs=[pl.BlockSpec((TB, D), lambda i: (i, 0)),
                   pl.BlockSpec((None, 2, D), lambda i: (_stream_of(i, cb), 0, 0)),
                   pl.BlockSpec((1, D), lambda i: (0, 0))],
        out_shape=[jax.ShapeDtypeStruct((T, D), F32), jax.ShapeDtypeStruct((2, 2, D), F32),
                   jax.ShapeDtypeStruct((1, D), F32)],
        compiler_params=_params("arbitrary"),
    )(dh, x, dres, nw, mod)


def _scan_chunk(n, rev, n_ctx, n_all):
    if not rev:
        return n
    return jnp.where(n < n_ctx, n_ctx - 1 - n, n_all - 1 + n_ctx - n)


def _cumsum_rows(x, rev):
    rows = x.shape[0]
    row = lax.broadcasted_iota(jnp.int32, (rows, 1), 0)
    s = 1
    while s < rows:
        if not rev:
            x = x + jnp.where(row >= s, pltpu.roll(x, s, 0), 0.0)
        else:
            x = x + jnp.where(row < rows - s, pltpu.roll(x, rows - s, 0), 0.0)
        s *= 2
    return x


def _lower_bound(hlb_ref, layer):
    h = hlb_ref[...]
    if layer == 0:
        return jnp.zeros_like(h[0:1, :])
    return _sigmoid(h[1:2, :] - h[0:1, :])


def _hgrn_gates(q_ref, f_ref, hlb_ref, layer, rev):
    lb = _lower_bound(hlb_ref, layer)
    z = f_ref[...]
    sig = 1.0 / (1.0 + jnp.exp(-z))
    fg = lb + (1.0 - lb) * sig
    kk = (1.0 - lb) * (1.0 - sig)
    g = jnp.log(fg)
    b = _cumsum_rows(g, rev)
    bt = jnp.sum(g, axis=0, keepdims=True)
    mid = HG_CHUNK // 2
    r = b[mid:mid + 1, :] if rev else b[mid - 1:mid, :]
    qh = _silu(q_ref[...])
    return lb, sig, fg, kk, b, bt, r, qh


def _tri_mask(rev):
    t = lax.broadcasted_iota(jnp.int32, (HG_CHUNK, HG_CHUNK), 0)
    s = lax.broadcasted_iota(jnp.int32, (HG_CHUNK, HG_CHUNK), 1)
    return (s >= t) if rev else (s <= t)


def _hgrn_fwd(parts, hlb, layer, rev, ctx_rows, name, o_add=None):
    T = parts.shape[0]
    D = hlb.shape[1] // 2
    nh = D // HEAD
    n_all, n_ctx = T // HG_CHUNK, ctx_rows // HG_CHUNK
    chunk = functools.partial(_scan_chunk, rev=rev, n_ctx=n_ctx, n_all=n_all)
    fcol = 2 if rev else 1

    def body(q_ref, f_ref, i_ref, hlb_ref, *rest):
        if o_add is None:
            o_ref, st_ref, s_scr = rest
        else:
            oa_ref, o_ref, st_ref, s_scr = rest
        n = pl.program_id(0)

        @pl.when(n == 0)
        def _():
            s_scr[...] = jnp.zeros_like(s_scr)

        lb, sig, fg, kk, b, bt, r, qh = _hgrn_gates(q_ref, f_ref, hlb_ref, layer, rev)
        qr = (qh * jnp.exp(b - r)).astype(BF16)
        kr = (kk * jnp.exp(r - b)).astype(BF16)
        qe = (qh * jnp.exp(b)).astype(BF16)
        ke = (kk * jnp.exp(bt - b)).astype(BF16)
        dec = jnp.exp(bt)
        v = i_ref[...].astype(BF16)
        mask = _tri_mask(rev)
        hs = [slice(h * HEAD, (h + 1) * HEAD) for h in range(nh)]
        st = [s_scr[h] for h in range(nh)]
        a_raw = [_nt(qr[:, sl], kr[:, sl]) for sl in hs]
        o_int = [_nt(qe[:, sl], st[h].astype(BF16)) for h, sl in enumerate(hs)]
        kv = [_tn(v[:, sl], ke[:, sl]) for sl in hs]
        for h, sl in enumerate(hs):
            st_ref[h] = st[h]
            o = _nn(jnp.where(mask, a_raw[h], 0.0).astype(BF16), v[:, sl]) + o_int[h]
            if o_add is not None:
                o = o + oa_ref[:, sl]
            o_ref[:, sl] = o
            s_scr[h] = st[h] * dec[:, sl] + kv[h]

    cspec = lambda col: pl.BlockSpec((HG_CHUNK, D), lambda n: (chunk(n), col))
    ins = [parts, parts, parts, hlb]
    specs = [cspec(0), cspec(fcol), cspec(3), pl.BlockSpec((2, D), lambda n: (0, 1 if rev else 0))]
    if o_add is not None:
        ins.append(o_add)
        specs.append(cspec(0))
    return pl.pallas_call(
        body, name=name, grid=(n_all,), in_specs=specs,
        out_specs=[cspec(0), pl.BlockSpec((None, nh, HEAD, HEAD), lambda n: (n, 0, 0, 0))],
        out_shape=[jax.ShapeDtypeStruct((T, D), F32), jax.ShapeDtypeStruct((n_all, nh, HEAD, HEAD), F32)],
        scratch_shapes=[pltpu.VMEM((nh, HEAD, HEAD), F32)],
        compiler_params=_params("arbitrary"),
    )(*ins)


def _hgrn_bwd(parts, hlb, do, states, layer, rev, ctx_rows, name, other=None, dparts=None):
    T = parts.shape[0]
    D = hlb.shape[1] // 2
    nh = D // HEAD
    n_all, n_ctx = T // HG_CHUNK, ctx_rows // HG_CHUNK
    step = lambda m: n_all - 1 - m
    chunk = lambda m: _scan_chunk(step(m), rev, n_ctx, n_all)
    fcol = 2 if rev else 1
    has_add = other is not None
    assert not has_add or rev

    def body(q_ref, f_ref, i_ref, hlb_ref, do_ref, st_ref, *rest):
        if has_add:
            dqa_ref, dza_ref, dia_ref, _, out_ref, dlb_ref, ds_scr = rest
            dq_ref, dz_ref, di_ref = out_ref.at[:, 0:D], out_ref.at[:, 2 * D:3 * D], out_ref.at[:, 3 * D:4 * D]
            out_ref[:, D:2 * D] = dza_ref[...]
        else:
            dq_ref, dz_ref, di_ref, dlb_ref, ds_scr = rest
        m = pl.program_id(0)

        @pl.when(m == 0)
        def _():
            ds_scr[...] = jnp.zeros_like(ds_scr)
            dlb_ref[...] = jnp.zeros_like(dlb_ref)

        lb, sig, fg, kk, b, bt, r, qh = _hgrn_gates(q_ref, f_ref, hlb_ref, layer, rev)
        e_qr = jnp.exp(b - r)
        e_kr = jnp.exp(r - b)
        e_b = jnp.exp(b)
        e_ke = jnp.exp(bt - b)
        dec = jnp.exp(bt)
        qr = (qh * e_qr).astype(BF16)
        kr = (kk * e_kr).astype(BF16)
        qe = (qh * e_b).astype(BF16)
        ke = (kk * e_ke).astype(BF16)
        vf = i_ref[...]
        v = vf.astype(BF16)
        dov = do_ref[...].astype(BF16)
        mask = _tri_mask(rev)
        hs = [slice(h * HEAD, (h + 1) * HEAD) for h in range(nh)]
        st = [st_ref[h] for h in range(nh)]
        dst = [ds_scr[h] for h in range(nh)]
        stb = [t.astype(BF16) for t in st]
        dstb = [t.astype(BF16) for t in dst]
        a_raw = [_nt(qr[:, sl], kr[:, sl]) for sl in hs]
        da_raw = [_nt(dov[:, sl], v[:, sl]) for sl in hs]
        dq_int = [_nn(dov[:, sl], stb[h]) for h, sl in enumerate(hs)]
        dk_int = [_nn(v[:, sl], dstb[h]) for h, sl in enumerate(hs)]
        dv_int = [_nt(ke[:, sl], dstb[h]) for h, sl in enumerate(hs)]
        ds_new = [_tn(dov[:, sl], qe[:, sl]) for sl in hs]
        a = [jnp.where(mask, t, 0.0).astype(BF16) for t in a_raw]
        da = [jnp.where(mask, t, 0.0).astype(BF16) for t in da_raw]
        dv_parts = [_tn(a[h], dov[:, sl]) + dv_int[h] for h, sl in enumerate(hs)]
        dq_parts = [_nn(da[h], kr[:, sl]) * e_qr[:, sl] + dq_int[h] * e_b[:, sl] for h, sl in enumerate(hs)]
        dki_parts = [dk_int[h] * e_ke[:, sl] for h, sl in enumerate(hs)]
        dk_parts = [_tn(da[h], qr[:, sl]) * e_kr[:, sl] + dki_parts[h] for h, sl in enumerate(hs)]
        dbt_parts = [dec[:, sl] * jnp.sum(st[h] * dst[h], axis=0, keepdims=True) for h, sl in enumerate(hs)]
        for h, sl in enumerate(hs):
            ds_scr[h] = dst[h] * dec[:, sl] + ds_new[h]
        dq = jnp.concatenate(dq_parts, axis=1)
        dk = jnp.concatenate(dk_parts, axis=1)
        dki = jnp.concatenate(dki_parts, axis=1)
        dv = jnp.concatenate(dv_parts, axis=1)
        dbt = jnp.concatenate(dbt_parts, axis=1) + jnp.sum(kk * dki, axis=0, keepdims=True)
        db = qh * dq - kk * dk
        dg = _cumsum_rows(db, not rev) + dbt
        df = dg / fg - dk
        dz_ref[...] = (df * (1.0 - lb) * sig * (1.0 - sig)).astype(BF16)
        dlb_ref[...] += jnp.sum(df * (1.0 - sig), axis=0, keepdims=True)
        dqr = dq * _dsilu(q_ref[...])
        if has_add:
            dqr = dqr + dqa_ref[...]
            dv = dv + dia_ref[...]
        dq_ref[...] = dqr.astype(dq_ref.dtype)
        di_ref[...] = dv.astype(di_ref.dtype)

        @pl.when(m == n_all - 1)
        def _():
            if layer == 0:
                dlb_ref[...] = jnp.zeros_like(dlb_ref)
            else:
                dlb_ref[...] = dlb_ref[...] * lb * (1.0 - lb)

    cspec = lambda col: pl.BlockSpec((HG_CHUNK, D), lambda m: (chunk(m), col))
    ins = [parts, parts, parts, hlb, do, states]
    specs = [cspec(0), cspec(fcol), cspec(3), pl.BlockSpec((2, D), lambda m: (0, 1 if rev else 0)), cspec(0),
             pl.BlockSpec((None, nh, HEAD, HEAD), lambda m: (step(m), 0, 0, 0))]
    dlb_spec = pl.BlockSpec((1, D), lambda m: (0, 0))
    dlb_shape = jax.ShapeDtypeStruct((1, D), F32)
    if has_add:
        return pl.pallas_call(
            body, name=name, grid=(n_all,),
            in_specs=specs + [cspec(0), cspec(0), cspec(0), pl.BlockSpec(memory_space=pl.ANY)],
            out_specs=[pl.BlockSpec((HG_CHUNK, 4 * D), lambda m: (chunk(m), 0)), dlb_spec],
            out_shape=[jax.ShapeDtypeStruct(dparts.shape, dparts.dtype), dlb_shape],
            scratch_shapes=[pltpu.VMEM((nh, HEAD, HEAD), F32)], input_output_aliases={len(ins) + 3: 0},
            compiler_params=_params("arbitrary"),
        )(*ins, *other, dparts)
    return pl.pallas_call(
        body, name=name, grid=(n_all,), in_specs=specs,
        out_specs=[cspec(0), cspec(0), cspec(0), dlb_spec],
        out_shape=[jax.ShapeDtypeStruct((T, D), F32), jax.ShapeDtypeStruct((T, D), BF16),
                   jax.ShapeDtypeStruct((T, D), F32), dlb_shape],
        scratch_shapes=[pltpu.VMEM((nh, HEAD, HEAD), F32)],
        compiler_params=_params("arbitrary"),
    )(*ins)


def _sgu_ln(gv, lnw_ref, lnb_ref):
    mu = jnp.mean(gv, axis=-1, keepdims=True)
    xc = gv - mu
    rstd = lax.rsqrt(jnp.mean(xc * xc, axis=-1, keepdims=True) + LN_EPS)
    xh = xc * rstd
    return xh, rstd, xh * lnw_ref[...] + lnb_ref[...]


def _sgu_fwd(parts, lnw, lnb, w, bt, name):
    T = parts.shape[0]
    D = lnw.shape[1]
    G = D // HEAD

    def body(u_ref, v_ref, lnw_ref, lnb_ref, w_ref, bt_ref, ya_ref):
        gu = _gelu(u_ref[...])
        _, _, vn = _sgu_ln(_gelu(v_ref[...]), lnw_ref, lnb_ref)
        vnb = vn.astype(BF16)
        for g in range(G):
            sl = slice(g * HEAD, (g + 1) * HEAD)
            mixed = _nn(w_ref[g], vnb[:, sl]) + bt_ref[:, g:g + 1]
            ya_ref[:, sl] = (gu[:, sl] * mixed).astype(BF16)

    return pl.pallas_call(
        body, name=name, grid=(T // SGU_CHUNK,),
        in_specs=[pl.BlockSpec((SGU_CHUNK, D), lambda n: (n, 4)), pl.BlockSpec((SGU_CHUNK, D), lambda n: (n, 5)),
                  pl.BlockSpec((1, D), lambda n: (0, 0)), pl.BlockSpec((1, D), lambda n: (0, 0)),
                  pl.BlockSpec((G, SGU_CHUNK, SGU_CHUNK), lambda n: (0, 0, 0)),
                  pl.BlockSpec((SGU_CHUNK, G), lambda n: (0, 0))],
        out_specs=pl.BlockSpec((SGU_CHUNK, D), lambda n: (n, 0)),
        out_shape=jax.ShapeDtypeStruct((T, D), BF16),
        compiler_params=_params("parallel"),
    )(parts, parts, lnw, lnb, w, bt)


def _sgu_bwd(parts, dya, lnw, lnb, w, bt, dparts, name):
    T = parts.shape[0]
    D = lnw.shape[1]
    G = D // HEAD

    def body(u_ref, v_ref, dya_ref, lnw_ref, lnb_ref, w_ref, bt_ref, dparts_in,
             duv_ref, dw_ref, dbt_ref, dlnw_ref, dlnb_ref, dvn_scr):
        du_ref = duv_ref.at[:, 0:D]
        dv_ref = duv_ref.at[:, D:2 * D]
        n = pl.program_id(0)

        @pl.when(n == 0)
        def _():
            dw_ref[...] = jnp.zeros_like(dw_ref)
            dbt_ref[...] = jnp.zeros_like(dbt_ref)
            dlnw_ref[...] = jnp.zeros_like(dlnw_ref)
            dlnb_ref[...] = jnp.zeros_like(dlnb_ref)

        gu, dgu = _gelu_both(u_ref[...])
        gv, dgv_dv = _gelu_both(v_ref[...])
        xh, rstd, vn = _sgu_ln(gv, lnw_ref, lnb_ref)
        vnb = vn.astype(BF16)
        dya = dya_ref[...]
        lane = lax.broadcasted_iota(jnp.int32, (SGU_CHUNK, G), 1)
        dbt = jnp.zeros((SGU_CHUNK, G), F32)
        for g in range(G):
            sl = slice(g * HEAD, (g + 1) * HEAD)
            wg = w_ref[g]
            mixed = _nn(wg, vnb[:, sl]) + bt_ref[:, g:g + 1]
            dmix = dya[:, sl] * gu[:, sl]
            du_ref[:, sl] = (dya[:, sl] * mixed * dgu[:, sl]).astype(BF16)
            dmb = dmix.astype(BF16)
            dvn_scr[:, sl] = _tn(wg, dmb)
            dw_ref[g] += _nt(dmb, vnb[:, sl])
            dbt = dbt + jnp.where(lane == g, jnp.sum(dmix, axis=1, keepdims=True), 0.0)
        dbt_ref[...] += dbt
        dvn = dvn_scr[...]
        dlnw_ref[...] += jnp.sum(dvn * xh, axis=0, keepdims=True)
        dlnb_ref[...] += jnp.sum(dvn, axis=0, keepdims=True)
        dxh = dvn * lnw_ref[...]
        dgv = rstd * (dxh - jnp.mean(dxh, axis=-1, keepdims=True) - xh * jnp.mean(dxh * xh, axis=-1, keepdims=True))
        dv_ref[...] = (dgv * dgv_dv).astype(BF16)

    row = lambda col: pl.BlockSpec((SGU_CHUNK, D), lambda n: (n, col))
    vec = pl.BlockSpec((1, D), lambda n: (0, 0))
    wsp = pl.BlockSpec((G, SGU_CHUNK, SGU_CHUNK), lambda n: (0, 0, 0))
    bsp = pl.BlockSpec((SGU_CHUNK, G), lambda n: (0, 0))
    return pl.pallas_call(
        body, name=name, grid=(T // SGU_CHUNK,),
        in_specs=[row(4), row(5), row(0), vec, vec, wsp, bsp, pl.BlockSpec(memory_space=pl.ANY)],
        out_specs=[pl.BlockSpec((SGU_CHUNK, 2 * D), lambda n: (n, 2)), wsp, bsp, vec, vec],
        out_shape=[jax.ShapeDtypeStruct(dparts.shape, dparts.dtype),
                   jax.ShapeDtypeStruct((G, SGU_CHUNK, SGU_CHUNK), F32), jax.ShapeDtypeStruct((SGU_CHUNK, G), F32),
                   jax.ShapeDtypeStruct((1, D), F32), jax.ShapeDtypeStruct((1, D), F32)],
        scratch_shapes=[pltpu.VMEM((SGU_CHUNK, D), F32)], input_output_aliases={7: 0},
        compiler_params=_params("arbitrary"),
    )(parts, parts, dya, lnw, lnb, w, bt, dparts)


TBT = 256
VMEM_LIMIT_TOKEN_OUT = 58 * 1024 * 1024


def _rows_weight_spec(wg):
    return pl.BlockSpec(wg.shape, lambda i: (0, 0, 0))


def _full(w_ref):
    return w_ref[...].reshape(w_ref.shape[0] * w_ref.shape[1], w_ref.shape[2])


def _token_out_fwd(o, parts, ya, x, mod, hnw, wa, wb, wo, ctx_rows, name):
    T, D = x.shape
    nh = D // HEAD
    cb = ctx_rows // TBT

    def body(o_ref, og_ref, ga_ref, gb_ref, ya_ref, x_ref, mod_ref, hnw_ref, wa_ref, wb_ref, wo_ref,
             yb_ref, pa_ref, pb_ref, mg_ref, tmo_ref, xm_ref):
        ov = o_ref[...]
        so = _silu(og_ref[...])
        nw = hnw_ref[...]
        for h in range(nh):
            sl = slice(h * HEAD, (h + 1) * HEAD)
            seg = ov[:, sl]
            r = lax.rsqrt(jnp.mean(seg * seg, axis=-1, keepdims=True) + RMS_EPS)
            yb_ref[:, sl] = (seg * r * nw * so[:, sl]).astype(BF16)
        pa = _nn(ya_ref[...], _full(wa_ref))
        pb = _nn(yb_ref[...], _full(wb_ref))
        pa_ref[...] = pa
        pb_ref[...] = pb
        mg = (_sigmoid(ga_ref[...]) * pa + _sigmoid(gb_ref[...]) * pb).astype(BF16)
        mg_ref[...] = mg
        out = _nn(mg, _full(wo_ref))
        tmo_ref[...] = out
        xm_ref[...] = x_ref[...] + mod_ref[2:3, :] * out

    row = lambda col: pl.BlockSpec((TBT, D), lambda i: (i, col))
    wsp = _rows_weight_spec(wa)
    sd = lambda dt: jax.ShapeDtypeStruct((T, D), dt)
    return pl.pallas_call(
        body, name=name, grid=(T // TBT,),
        in_specs=[row(0), row(6), row(7), row(8), row(0), row(0),
                  pl.BlockSpec((None, N_MOD, D), lambda i: (_stream_of(i, cb), 0, 0)),
                  pl.BlockSpec((1, HEAD), lambda i: (0, 0)), wsp, wsp, wsp],
        out_specs=[row(0)] * 6,
        out_shape=[sd(BF16), sd(F32), sd(F32), sd(BF16), sd(F32), sd(F32)],
        compiler_params=_params("parallel", vmem=VMEM_LIMIT_TOKEN_OUT),
    )(o, parts, parts, parts, ya, x, mod, hnw, wa, wb, wo)


def _token_out_bwd(dx, tmo, pa, pb, o, parts, mod, hnw, wa, wb, wo, ctx_rows, name):
    T, D = dx.shape
    nh = D // HEAD
    cb = ctx_rows // TBT

    def body(dx_ref, tmo_ref, pa_ref, pb_ref, o_ref, og_ref, ga_ref, gb_ref, mod_ref, hnw_ref, wa_ref, wb_ref, wo_ref,
             dout_ref, dpa_ref, dpb_ref, dgate_ref, dya_ref, do_ref, dg1_ref, dhnw_ref):
        i = pl.program_id(0)

        @pl.when(i == 0)
        def _():
            dhnw_ref[...] = jnp.zeros_like(dhnw_ref)

        @pl.when((i == 0) | (i == cb))
        def _():
            dg1_ref[...] = jnp.zeros_like(dg1_ref)

        dxv = dx_ref[...]
        dg1_ref[...] += jnp.sum(dxv * tmo_ref[...], axis=0, keepdims=True)
        dout = (dxv * mod_ref[2:3, :]).astype(BF16)
        dout_ref[...] = dout
        dmg = _nt(dout, _full(wo_ref))
        sa = _sigmoid(ga_ref[...])
        sb = _sigmoid(gb_ref[...])
        dpa = (dmg * sa).astype(BF16)
        dpb = (dmg * sb).astype(BF16)
        dpa_ref[...] = dpa
        dpb_ref[...] = dpb
        dgate_ref[:, D:2 * D] = (dmg * pa_ref[...] * sa * (1.0 - sa)).astype(BF16)
        dgate_ref[:, 2 * D:3 * D] = (dmg * pb_ref[...] * sb * (1.0 - sb)).astype(BF16)
        dya_ref[...] = _nt(dpa, _full(wa_ref))
        dyb = _nt(dpb, _full(wb_ref))
        so, dso = _silu_both(og_ref[...])
        ov = o_ref[...]
        nw = hnw_ref[...]
        dnw = jnp.zeros((1, HEAD), F32)
        for h in range(nh):
            sl = slice(h * HEAD, (h + 1) * HEAD)
            seg = ov[:, sl]
            r = lax.rsqrt(jnp.mean(seg * seg, axis=-1, keepdims=True) + RMS_EPS)
            oh = seg * r
            dn = dyb[:, sl] * so[:, sl]
            dgate_ref[:, sl] = (dyb[:, sl] * oh * nw * dso[:, sl]).astype(BF16)
            dnw = dnw + jnp.sum(dn * oh, axis=0, keepdims=True)
            doh = dn * nw
            do_ref[:, sl] = r * (doh - oh * jnp.mean(doh * oh, axis=-1, keepdims=True))
        dhnw_ref[...] += dnw

    row = lambda col: pl.BlockSpec((TBT, D), lambda i: (i, col))
    wsp = _rows_weight_spec(wa)
    sd = lambda dt: jax.ShapeDtypeStruct((T, D), dt)
    return pl.pallas_call(
        body, name=name, grid=(T // TBT,),
        in_specs=[row(0), row(0), row(0), row(0), row(0), row(6), row(7), row(8),
                  pl.BlockSpec((None, N_MOD, D), lambda i: (_stream_of(i, cb), 0, 0)),
                  pl.BlockSpec((1, HEAD), lambda i: (0, 0)), wsp, wsp, wsp],
        out_specs=[row(0)] * 3 + [pl.BlockSpec((TBT, 3 * D), lambda i: (i, 2)), row(0), row(0),
                                  pl.BlockSpec((None, 1, D), lambda i: (_stream_of(i, cb), 0, 0)),
                                  pl.BlockSpec((1, HEAD), lambda i: (0, 0))],
        out_shape=[sd(BF16)] * 3 + [jax.ShapeDtypeStruct((T, 9 * D), BF16), sd(F32), sd(F32),
                                    jax.ShapeDtypeStruct((2, 1, D), F32), jax.ShapeDtypeStruct((1, HEAD), F32)],
        compiler_params=_params("arbitrary", vmem=VMEM_LIMIT_TOKEN_OUT),
    )(dx, tmo, pa, pb, o, parts, parts, parts, mod, hnw, wa, wb, wo)


def _conv_geometry(i, nb, cb):
    is_ctx = i < cb
    first = (i == 0) | (i == cb)
    last = (i == cb - 1) | (i == nb - 1)
    row = lax.broadcasted_iota(jnp.int32, (TB + 2 * GRID_W, 1), 0)
    w = row & (GRID_W - 1)
    left_ok = (w != 0) | is_ctx
    right_ok = (w != GRID_W - 1) | is_ctx
    return is_ctx, first, last, left_ok, right_ok


def _ext(p_ref, m_ref, n_ref, first, last):
    return jnp.concatenate([jnp.where(first, 0.0, p_ref[...]), m_ref[...], jnp.where(last, 0.0, n_ref[...])], axis=0)


def _shift_prev(e, ok):
    return jnp.where(ok, pltpu.roll(e, 1, 0), 0.0)


def _shift_next(e, ok):
    return jnp.where(ok, pltpu.roll(e, e.shape[0] - 1, 0), 0.0)


def _halo_specs(cbk, n64, coff=0):
    r = TB // GRID_W
    prev = pl.BlockSpec((GRID_W, cbk), lambda j, i: (jnp.maximum(r * i - 1, 0), j + coff))
    main = pl.BlockSpec((TB, cbk), lambda j, i: (i, j + coff))
    nxt = pl.BlockSpec((GRID_W, cbk), lambda j, i: (jnp.minimum(r * i + r, n64 - 1), j + coff))
    return [prev, main, nxt]


def _conv_cblock(dff):
    return _tile(dff, 1408)


def _conv_fwd(up, cw, cbias, ctx_rows, name):
    T, dff = up.shape[0], up.shape[1] // 2
    cbk = _conv_cblock(dff)
    nb, cb = T // TB, ctx_rows // TB
    nvb = dff // cbk

    def body(ap_ref, a_ref, an_ref, v_ref, cw_ref, cb_ref, ac_ref, act_ref):
        i = pl.program_id(1)
        is_ctx, first, last, lok, rok = _conv_geometry(i, nb, cb)
        e = _ext(ap_ref, a_ref, an_ref, first, last)
        el = _shift_prev(e, lok)
        er = _shift_next(e, rok)
        cwv = cw_ref[...]

        def comb(dr, lo):
            sl = slice(lo, lo + TB)
            return cwv[3 * dr:3 * dr + 1] * el[sl] + cwv[3 * dr + 1:3 * dr + 2] * e[sl] + cwv[3 * dr + 2:3 * dr + 3] * er[sl]

        out = comb(1, GRID_W) + jnp.where(is_ctx, 0.0, comb(0, 0) + comb(2, 2 * GRID_W))
        a_c = out + cb_ref[...]
        ac_ref[...] = a_c
        act_ref[...] = (_gelu(a_c) * v_ref[...]).astype(BF16)

    main = pl.BlockSpec((TB, cbk), lambda j, i: (i, j))
    return pl.pallas_call(
        body, name=name, grid=(dff // cbk, nb),
        in_specs=_halo_specs(cbk, T // GRID_W) + [pl.BlockSpec((TB, cbk), lambda j, i: (i, j + nvb)),
                                                 pl.BlockSpec((9, cbk), lambda j, i: (0, j)),
                                                 pl.BlockSpec((1, cbk), lambda j, i: (0, j))],
        out_specs=[main, main],
        out_shape=[jax.ShapeDtypeStruct((T, dff), F32), jax.ShapeDtypeStruct((T, dff), BF16)],
        compiler_params=_params("parallel", "parallel"),
    )(up, up, up, up, cw, cbias)


def _conv_bwd(up, ac, dact, cw, ctx_rows, name):
    T, dff = up.shape[0], up.shape[1] // 2
    cbk = _conv_cblock(dff)
    nb, cb = T // TB, ctx_rows // TB
    nvb = dff // cbk

    def body(ap_ref, a_ref, an_ref, vp_ref, v_ref, vn_ref, cp_ref, c_ref, cn_ref, dp_ref, d_ref, dn_ref, cw_ref,
             da_ref, dv_ref, dcw_ref, dcb_ref):
        i = pl.program_id(1)

        @pl.when(i == 0)
        def _():
            dcw_ref[...] = jnp.zeros_like(dcw_ref)
            dcb_ref[...] = jnp.zeros_like(dcb_ref)

        is_ctx, first, last, lok, rok = _conv_geometry(i, nb, cb)
        gl, dgl = _gelu_both(_ext(cp_ref, c_ref, cn_ref, first, last))
        g = _ext(dp_ref, d_ref, dn_ref, first, last) * _ext(vp_ref, v_ref, vn_ref, first, last) * dgl
        dv_ref[...] = (d_ref[...] * gl[GRID_W:GRID_W + TB]).astype(BF16)
        gm = _shift_prev(g, lok)
        gp = _shift_next(g, rok)
        cwv = cw_ref[...]

        def comb(dr, lo):
            sl = slice(lo, lo + TB)
            return cwv[3 * dr:3 * dr + 1] * gp[sl] + cwv[3 * dr + 1:3 * dr + 2] * g[sl] + cwv[3 * dr + 2:3 * dr + 3] * gm[sl]

        da = comb(1, GRID_W) + jnp.where(is_ctx, 0.0, comb(0, 2 * GRID_W) + comb(2, 0))
        da_ref[...] = da.astype(BF16)
        e = _ext(ap_ref, a_ref, an_ref, first, last)
        taps = [_shift_prev(e, lok), e, _shift_next(e, rok)]
        gmain = g[GRID_W:GRID_W + TB]
        dcb_ref[...] += jnp.sum(gmain, axis=0, keepdims=True)
        vert = jnp.where(is_ctx, 0.0, 1.0)
        for dr in range(3):
            sl = slice(dr * GRID_W, dr * GRID_W + TB)
            for dw in range(3):
                s = jnp.sum(gmain * taps[dw][sl], axis=0, keepdims=True)
                if dr != 1:
                    s = s * vert
                k = 3 * dr + dw
                dcw_ref[k:k + 1, :] += s

    main = pl.BlockSpec((TB, cbk), lambda j, i: (i, j))
    halo = _halo_specs(cbk, T // GRID_W)
    acc9 = pl.BlockSpec((9, cbk), lambda j, i: (0, j))
    acc1 = pl.BlockSpec((1, cbk), lambda j, i: (0, j))
    return pl.pallas_call(
        body, name=name, grid=(dff // cbk, nb),
        in_specs=halo + _halo_specs(cbk, T // GRID_W, nvb) + halo + halo + [acc9],
        out_specs=[main, main, acc9, acc1],
        out_shape=[jax.ShapeDtypeStruct((T, dff), BF16), jax.ShapeDtypeStruct((T, dff), BF16),
                   jax.ShapeDtypeStruct((9, dff), F32), jax.ShapeDtypeStruct((1, dff), F32)],
        compiler_params=_params("parallel", "arbitrary"),
    )(up, up, up, up, up, up, ac, ac, ac, dact, dact, dact, cw)


def _ffn_out_fwd(act, xm, mod, wd, ctx_rows, name):
    T, D = xm.shape
    dff = act.shape[1]
    cb = ctx_rows // TB

    def body(act_ref, x_ref, mod_ref, w_ref, xo_ref, fo_ref):
        out = _nn(act_ref[...], _full(w_ref))
        fo_ref[...] = out
        xo_ref[...] = x_ref[...] + mod_ref[5:6, :] * out

    row = pl.BlockSpec((TB, D), lambda i: (i, 0))
    return pl.pallas_call(
        body, name=name, grid=(T // TB,),
        in_specs=[pl.BlockSpec((TB, dff), lambda i: (i, 0)), row,
                  pl.BlockSpec((None, N_MOD, D), lambda i: (_stream_of(i, cb), 0, 0)),
                  _rows_weight_spec(wd)],
        out_specs=[row, row],
        out_shape=[jax.ShapeDtypeStruct((T, D), F32), jax.ShapeDtypeStruct((T, D), F32)],
        compiler_params=_params("parallel"),
    )(act, xm, mod, wd)


def _ffn_out_bwd(dx, fo, mod, wd, ctx_rows, name):
    T, D = dx.shape
    dff = N_CHIPS * wd.shape[1]
    cb = ctx_rows // TB

    def body(dx_ref, fo_ref, mod_ref, w_ref, dout_ref, dact_ref, dg2_ref):
        i = pl.program_id(0)

        @pl.when((i == 0) | (i == cb))
        def _():
            dg2_ref[...] = jnp.zeros_like(dg2_ref)

        dxv = dx_ref[...]
        dg2_ref[...] += jnp.sum(dxv * fo_ref[...], axis=0, keepdims=True)
        dout = (dxv * mod_ref[5:6, :]).astype(BF16)
        dout_ref[...] = dout
        dact_ref[...] = _nt(dout, _full(w_ref))

    row = pl.BlockSpec((TB, D), lambda i: (i, 0))
    return pl.pallas_call(
        body, name=name, grid=(T // TB,),
        in_specs=[row, row, pl.BlockSpec((None, N_MOD, D), lambda i: (_stream_of(i, cb), 0, 0)),
                  _rows_weight_spec(wd)],
        out_specs=[row, pl.BlockSpec((TB, dff), lambda i: (i, 0)),
                   pl.BlockSpec((None, 1, D), lambda i: (_stream_of(i, cb), 0, 0))],
        out_shape=[jax.ShapeDtypeStruct((T, D), BF16), jax.ShapeDtypeStruct((T, dff), F32),
                   jax.ShapeDtypeStruct((2, 1, D), F32)],
        compiler_params=_params("arbitrary"),
    )(dx, fo, mod, wd)


def _loss_bwd(x, target, fw, ctx_rows, name):
    T, D = x.shape
    cb = ctx_rows // TB

    def body(x_ref, t_ref, fw_ref, dx_ref, loss_ref, dfw_ref):
        i = pl.program_id(0)

        @pl.when(i == 0)
        def _():
            loss_ref[...] = jnp.zeros_like(loss_ref)
            dfw_ref[...] = jnp.zeros_like(dfw_ref)

        @pl.when(i < cb)
        def _():
            dx_ref[...] = jnp.zeros_like(dx_ref)

        @pl.when(i >= cb)
        def _():
            xv = x_ref[...]
            r = lax.rsqrt(jnp.mean(xv * xv, axis=-1, keepdims=True) + RMS_EPS)
            xh = xv * r
            fwv = fw_ref[...]
            err = xh * fwv - t_ref[...]
            loss_ref[...] += (0.5 / D) * jnp.sum(err * err)
            dy = err * (1.0 / D)
            dfw_ref[...] += jnp.sum(dy * xh, axis=0, keepdims=True)
            dxh = dy * fwv
            dx_ref[...] = r * (dxh - xh * jnp.mean(dxh * xh, axis=-1, keepdims=True))

    row = pl.BlockSpec((TB, D), lambda i: (i, 0))
    return pl.pallas_call(
        body, name=name, grid=(T // TB,),
        in_specs=[row, pl.BlockSpec((TB, D), lambda i: (jnp.maximum(i - cb, 0), 0)), pl.BlockSpec((1, D), lambda i: (0, 0))],
        out_specs=[row, pl.BlockSpec((1, 128), lambda i: (0, 0)), pl.BlockSpec((1, D), lambda i: (0, 0))],
        out_shape=[jax.ShapeDtypeStruct((T, D), F32), jax.ShapeDtypeStruct((1, 128), F32),
                   jax.ShapeDtypeStruct((1, D), F32)],
        compiler_params=_params("arbitrary"),
    )(x, target, fw)


def _adamw(w, gs, m, v, name):
    L, R, C = w.shape
    assert len(gs) == L
    rb = _rows_tile(R, max(16, (1 << 18) // C // 16 * 16))
    bc1 = 1.0 - ADAM_B1 ** ADAM_STEP
    bc2 = 1.0 - ADAM_B2 ** ADAM_STEP

    def body(w_ref, m_ref, v_ref, *rest):
        g_refs, (g_ref, d_ref, nm_ref, nv_ref) = rest[:L], rest[L:]
        layer = pl.program_id(0)
        for li in range(L):
            @pl.when(layer == li)
            def _():
                gv = g_refs[li][...]
                g_ref[...] = gv
                nm = ADAM_B1 * m_ref[...] + (1.0 - ADAM_B1) * gv
                nv = ADAM_B2 * v_ref[...] + (1.0 - ADAM_B2) * (gv * gv)
                nm_ref[...] = nm
                nv_ref[...] = nv
                d_ref[...] = -ADAM_LR * ((nm / bc1) / (jnp.sqrt(nv / bc2) + ADAM_EPS) + ADAM_WD * w_ref[...])

    blk = pl.BlockSpec((None, rb, C), lambda l, i: (l, i, 0))
    gblk = pl.BlockSpec((rb, C), lambda l, i: (i, 0))
    sd = jax.ShapeDtypeStruct((L, R, C), F32)
    return pl.pallas_call(
        body, name=name, grid=(L, R // rb), in_specs=[blk] * 3 + [gblk] * L, out_specs=[blk] * 4, out_shape=[sd] * 4,
        compiler_params=_params("parallel", "parallel"),
    )(w, m, v, *gs)


def _local_step(xs, cv, target, W, layer_weights, on_layer_grads, ctx_rows):
    T, D = xs.shape
    depth = W["norm1_w"].shape[0]
    saved = []
    X = xs
    for l in range(depth):
        s = {}
        Wl = layer_weights(l, X)
        mod_all, sa = _mod_fwd(cv, Wl["ada_w"], W["ada_b"][l][None, :] + Wl["token"], f"mod_fwd_{l}")
        mod = mod_all[:2].reshape(2, N_MOD, D)
        h1 = _norm_mod(X, W["norm1_w"][l][None, :], mod, 0, ctx_rows, f"norm1_{l}")
        parts = _mm_nn_w(h1, Wl["w_in"], F32, f"in_proj_{l}")
        o_f, st_f = _hgrn_fwd(parts, W["hlb"], l, False, ctx_rows, f"hgrn_fwd_f_{l}")
        o, st_b = _hgrn_fwd(parts, W["hlb"], l, True, ctx_rows, f"hgrn_fwd_b_{l}", o_add=o_f)
        ya = _sgu_fwd(parts, W["sgu_ln_w"][l][None, :], W["sgu_ln_b"][l][None, :], W["sgu_w"][l], W["sgu_bt"][l],
                      f"sgu_fwd_{l}")
        Wl.update(Wl.pop("late")(ya))
        yb, pa, pb, mg, tmo, xm = _token_out_fwd(o, parts, ya, X, mod, W["hnw"][l][None, :] + Wl["late_token"], Wl["w_a"],
                                                 Wl["w_b"], Wl["w_o"], ctx_rows, f"token_out_fwd_{l}")
        h2 = _norm_mod(xm, W["norm2_w"][l][None, :], mod, 3, ctx_rows, f"norm2_{l}")
        up = _mm_nn_w(h2, Wl["w_up"], F32, f"up_proj_{l}")
        ac, act = _conv_fwd(up, Wl["conv_w"], W["conv_b"][l][None, :], ctx_rows, f"conv_fwd_{l}")
        xo, fo = _ffn_out_fwd(act, xm, mod, Wl["w_down"], ctx_rows, f"ffn_out_fwd_{l}")
        s.update(X=X, Wl=Wl, mod=mod, mod_all=mod_all, sa=sa, h1=h1, parts=parts, o=o, st_f=st_f, st_b=st_b, ya=ya, yb=yb,
                 pa=pa, pb=pb, mg=mg, tmo=tmo, xm=xm, h2=h2, up=up, ac=ac, act=act, fo=fo)
        saved.append(s)
        X = xo

    dX, loss_row, dfw = _loss_bwd(X, target, W["final_norm_w"][None, :], ctx_rows, "loss_bwd")
    G = {k: [None] * depth for k in ("ada_b", "norm1_w", "sgu_ln_w", "sgu_ln_b", "sgu_w", "sgu_b", "hlb1", "hnw", "norm2_w",
                                     "conv_w", "conv_b", "dmod")}
    dcv = jnp.zeros_like(cv)
    for l in reversed(range(depth)):
        s = saved[l]
        mod, Wl = s["mod"], s["Wl"]
        big = {}
        dout2, dact, dg2 = _ffn_out_bwd(dX, s["fo"], mod, Wl["w_down"], ctx_rows, f"ffn_out_bwd_{l}")
        big["w_down"] = _mm_tn(s["act"], dout2, F32, f"dw_down_{l}")
        da, dv, dcw, dcb = _conv_bwd(s["up"], s["ac"], dact, Wl["conv_w"], ctx_rows, f"conv_bwd_{l}")
        G["conv_w"][l], G["conv_b"][l] = dcw, dcb[0]
        dup = jnp.concatenate([da, dv], axis=1)
        big["w_up"] = _mm_tn(s["h2"], dup, F32, f"dw_up_{l}", out_chips=True)
        dh2 = _mm_nt_w(dup, Wl["w_up"], F32, f"dh2_{l}")
        dxm, dm2, dnw2 = _norm_mod_bwd(dh2, s["xm"], dX, W["norm2_w"][l][None, :], mod, 3, ctx_rows, f"norm2_bwd_{l}")
        G["norm2_w"][l] = dnw2[0]
        (dout1, dpa, dpb, dparts, dya, do, dg1, dhnw) = _token_out_bwd(
            dxm, s["tmo"], s["pa"], s["pb"], s["o"], s["parts"], mod, W["hnw"][l][None, :], Wl["w_a"], Wl["w_b"], Wl["w_o"],
            ctx_rows, f"token_out_bwd_{l}")
        G["hnw"][l] = dhnw[0]
        big["w_o"] = _mm_tn(s["mg"], dout1, F32, f"dw_o_{l}")
        big["w_a"] = _mm_tn(s["ya"], dpa, F32, f"dw_a_{l}")
        big["w_b"] = _mm_tn(s["yb"], dpb, F32, f"dw_b_{l}")
        tok = on_layer_grads(l, "early", big)
        dparts, dsw, dsbt, dlnw, dlnb = _sgu_bwd(s["parts"], dya, W["sgu_ln_w"][l][None, :], W["sgu_ln_b"][l][None, :] + tok,
                                                 W["sgu_w"][l], W["sgu_bt"][l], dparts, f"sgu_bwd_{l}")
        G["sgu_w"][l], G["sgu_b"][l], G["sgu_ln_w"][l], G["sgu_ln_b"][l] = dsw, dsbt.T, dlnw[0], dlnb[0]
        dq_f, dz_f, di_f, dlb_f = _hgrn_bwd(s["parts"], W["hlb"], do, s["st_f"], l, False, ctx_rows, f"hgrn_bwd_f_{l}")
        dparts, dlb_b = _hgrn_bwd(s["parts"], W["hlb"], do, s["st_b"], l, True, ctx_rows, f"hgrn_bwd_b_{l}",
                                  other=(dq_f, dz_f, di_f), dparts=dparts)
        G["hlb1"][l] = jnp.concatenate([dlb_f[0], dlb_b[0]])
        tok = on_layer_grads(l, "late", {"w_in": _mm_tn(s["h1"], dparts, F32, f"dw_in_{l}", out_chips=True)})
        dh1 = _mm_nt_w(dparts, Wl["w_in"], F32, f"dh1_{l}")
        tok = tok + on_layer_grads(l, "end", {"after": dh1})
        dX, dm1, dnw1 = _norm_mod_bwd(dh1, s["X"], dxm, W["norm1_w"][l][None, :] + tok, mod, 0, ctx_rows, f"norm1_bwd_{l}")
        G["norm1_w"][l] = dnw1[0]
        dmod = jnp.concatenate([dm1, dg1, dm2, dg2], axis=1).reshape(2, N_MOD * D)
        dmod16 = jnp.concatenate([dmod, jnp.zeros((cv.shape[0] - 2, N_MOD * D), F32)], axis=0)
        G["ada_b"][l] = dmod[0] + dmod[1]
        G["dmod"][l] = dmod
        dcv = dcv + _cvec_bwd(dmod16, Wl["ada_w"], cv, f"dcvec_{l}")
    G["c_ctx"] = dcv[0]
    G["final_norm_w"] = dfw[0]
    return loss_row[0, 0], dX, G, saved[0]["sa"]


def _chip_peers(x, y, c):
    return [((1 - x, y, c), 2 * (1 - x) + y), ((x, 1 - y, c), 2 * x + 1 - y), ((1 - x, 1 - y, c), 2 * (1 - x) + 1 - y)]


def _rdma_call(ins, out_shapes, plan, n_remote, n_local, name, aliases=None):
    n_in, n_out = len(ins), len(out_shapes)

    def body(*refs):
        in_refs, out_refs = refs[:n_in], refs[n_in:n_in + n_out]
        send_sems, recv_sems, local_sems = refs[n_in + n_out:]
        x, y, c = lax.axis_index("x"), lax.axis_index("y"), lax.axis_index("c")
        remote, local = plan(in_refs, out_refs, x, y, c)
        assert len(remote) == n_remote and len(local) == n_local, (name, len(remote), len(local))
        copies = [pltpu.make_async_copy(s, d, local_sems.at[i]) for i, (s, d) in enumerate(local)]
        copies += [pltpu.make_async_remote_copy(src_ref=s, dst_ref=d, send_sem=send_sems.at[k], recv_sem=recv_sems.at[k],
                                                device_id=dev, device_id_type=pl.DeviceIdType.MESH)
                   for k, (s, d, dev) in enumerate(remote)]
        for cp in copies:
            cp.start()
        for cp in copies:
            cp.wait()

    hbm = pl.BlockSpec(memory_space=pltpu.HBM)
    return pl.pallas_call(
        body, name=name, in_specs=[hbm] * n_in, out_specs=[hbm] * n_out, out_shape=out_shapes,
        scratch_shapes=[pltpu.SemaphoreType.DMA((n_remote,)), pltpu.SemaphoreType.DMA((n_remote,)),
                        pltpu.SemaphoreType.DMA((max(n_local, 1),))],
        input_output_aliases=aliases or {},
    )(*ins)


DMA_PIECE_BYTES = 1 << 18
DMA_MAX_PIECES = 8


def _row_pieces(shape, dtype):
    rows = shape[0]
    row_bytes = jnp.dtype(dtype).itemsize
    for d in shape[1:]:
        row_bytes *= d
    n = 1
    while n < DMA_MAX_PIECES and rows % (2 * n * 16) == 0 and rows * row_bytes // (2 * n) >= DMA_PIECE_BYTES:
        n *= 2
    return [(i * (rows // n), rows // n) for i in range(n)]


def _half_pieces(o, c):
    r2 = o.shape[1] // 2
    return [pl.ds(c * r2 + st, sz) for st, sz in _row_pieces((r2,) + o.shape[2:], o.dtype)]


def _n_half_pieces(arrays):
    return sum(len(_row_pieces((a.shape[1] // 2,) + a.shape[2:], a.dtype)) for a in arrays)


def _plan_gather_far(lands, x, y, c):
    me = 2 * x + y
    return [(o.at[me, rows], o.at[me, rows], dev) for dev, _ in _chip_peers(x, y, c) for o in lands
            for rows in _half_pieces(o, c)]


def _plan_gather_near(lands, x, y, c):
    return [(o.at[idx, rows], o.at[idx, rows], (x, y, 1 - c)) for _, idx in _chip_peers(x, y, c) for o in lands
            for rows in _half_pieces(o, c)]


def _gather_weights(lands, name):
    n = len(lands)
    n_far = (N_CHIPS - 1) * _n_half_pieces(lands)

    def body(*refs):
        outs = refs[n:2 * n]
        far_send, far_recv, near_send, near_recv = refs[2 * n:]
        x, y, c = lax.axis_index("x"), lax.axis_index("y"), lax.axis_index("c")
        mk = lambda plan, send, recv: [
            pltpu.make_async_remote_copy(src_ref=s, dst_ref=d, send_sem=send.at[k], recv_sem=recv.at[k], device_id=dev,
                                         device_id_type=pl.DeviceIdType.MESH)
            for k, (s, d, dev) in enumerate(plan(outs, x, y, c))]
        far, near = mk(_plan_gather_far, far_send, far_recv), mk(_plan_gather_near, near_send, near_recv)
        assert len(far) == n_far and len(near) == n_far
        for cp in far:
            cp.start()
        for k in range(n_far):
            far[k].wait_recv()
            near[k].start()
        for k in range(n_far):
            near[k].wait_recv()
        for cp in far + near:
            cp.wait_send()

    hbm = pl.BlockSpec(memory_space=pltpu.HBM)
    sems = pltpu.SemaphoreType.DMA((n_far,))
    return pl.pallas_call(
        body, name=name, in_specs=[hbm] * n, out_specs=[hbm] * n,
        out_shape=[jax.ShapeDtypeStruct(a.shape, a.dtype) for a in lands],
        scratch_shapes=[sems, sems, sems, sems], input_output_aliases={i: i for i in range(n)},
    )(*lands)


def _gather_all(v, name):
    def plan(ins, outs, x, y, c):
        (s,), (o,) = ins, outs
        me = 4 * x + 2 * y + c
        flip = lambda a, f: 1 - a if f else a
        remote = [(s, o.at[me], (flip(x, m & 4), flip(y, m & 2), flip(c, m & 1))) for m in range(1, 8)]
        return remote, [(s, o.at[me])]

    return _rdma_call([v], [jax.ShapeDtypeStruct((8,) + v.shape, v.dtype)], plan, 7, 1, name)[0]


def _plan_pair(ins, lands, x, y, c):
    return [(a.at[j, 1 - c, pl.ds(st, sz)], o.at[j, pl.ds(st, sz)], (x, y, 1 - c)) for a, o in zip(ins, lands)
            for j in range(N_CHIPS) for st, sz in _row_pieces(a.shape[2:], a.dtype)]


def _n_pair_copies(parts):
    return N_CHIPS * sum(len(_row_pieces(a.shape[2:], a.dtype)) for a in parts)


def _reduce_pair(parts, name):
    shapes = [jax.ShapeDtypeStruct((N_CHIPS,) + a.shape[2:], a.dtype) for a in parts]
    return _rdma_call(parts, shapes, lambda ins, outs, x, y, c: (_plan_pair(ins, outs, x, y, c), []),
                      _n_pair_copies(parts), 0, name)


def _plan_chips(ins, lands, x, y, c):
    me = 2 * x + y
    return [(a.at[idx, pl.ds(st, sz)], o.at[me, pl.ds(st, sz)], dev) for dev, idx in _chip_peers(x, y, c)
            for a, o in zip(ins, lands) for st, sz in _row_pieces(a.shape[1:], a.dtype)]


def _n_chips_copies(parts):
    return (N_CHIPS - 1) * sum(len(_row_pieces(a.shape[1:], a.dtype)) for a in parts)


def _reduce_chips(parts, name):
    shapes = [jax.ShapeDtypeStruct(a.shape, a.dtype) for a in parts]
    return _rdma_call(parts, shapes, lambda ins, outs, x, y, c: (_plan_chips(ins, outs, x, y, c), []),
                      _n_chips_copies(parts), 0, name)


def _gather_pair(halves, name):
    def plan(ins, outs, x, y, c):
        return [(o.at[c, pl.ds(st, sz)], o.at[c, pl.ds(st, sz)], (x, y, 1 - c)) for o in outs
                for st, sz in _row_pieces(o.shape[1:], o.dtype)], []

    shapes = [jax.ShapeDtypeStruct(a.shape, a.dtype) for a in halves]
    n_remote = sum(len(_row_pieces(a.shape[1:], a.dtype)) for a in halves)
    return _rdma_call(halves, shapes, plan, n_remote, 0, name, aliases={i: i for i in range(len(halves))})


def _split_start(ins, lands, plan, n_remote, name):
    n_buf = len(ins) + len(lands)

    def body(*refs):
        in_refs, land_refs = refs[:len(ins)], refs[len(ins):n_buf]
        send_sems, recv_sems, token = refs[n_buf], refs[n_buf + 1], refs[-1]
        x, y, c = lax.axis_index("x"), lax.axis_index("y"), lax.axis_index("c")
        remote = plan(in_refs, land_refs, x, y, c)
        assert len(remote) == n_remote, (name, len(remote))
        for k, (s, d, dev) in enumerate(remote):
            pltpu.make_async_remote_copy(src_ref=s, dst_ref=d, send_sem=send_sems.at[k], recv_sem=recv_sems.at[k],
                                         device_id=dev, device_id_type=pl.DeviceIdType.MESH).start()
        token[...] = jnp.zeros_like(token)

    hbm = pl.BlockSpec(memory_space=pltpu.HBM)
    sem = pl.BlockSpec(memory_space=pltpu.SEMAPHORE)
    bufs = list(ins) + list(lands)
    out = pl.pallas_call(
        body, name=name, in_specs=[hbm] * n_buf,
        out_specs=(sem, sem) + (hbm,) * n_buf + (pl.BlockSpec(memory_space=pltpu.VMEM),),
        out_shape=(pltpu.SemaphoreType.DMA((n_remote,)), pltpu.SemaphoreType.DMA((n_remote,)))
        + tuple(pltpu.HBM(a.shape, a.dtype) for a in bufs) + (jax.ShapeDtypeStruct((8, 128), F32),),
        input_output_aliases={i: 2 + i for i in range(n_buf)},
        compiler_params=pltpu.CompilerParams(has_side_effects=pltpu.SideEffectType.DATAFLOW_SIDE_EFFECTING),
    )(*[pltpu.with_memory_space_constraint(a, pltpu.HBM) for a in bufs])
    return dict(send=out[0], recv=out[1], ins=list(out[2:2 + len(ins)]), lands=list(out[2 + len(ins):2 + n_buf]),
                token=out[-1][0, 0], token_array=out[-1], plan=plan, n_remote=n_remote)


def _split_wait(st, after, name):
    n_in, n_buf = len(st["ins"]), len(st["ins"]) + len(st["lands"])
    plan, n_remote = st["plan"], st["n_remote"]

    def body(*refs):
        in_refs, land_refs = refs[:n_in], refs[n_in:n_buf]
        send_sems, recv_sems = refs[n_buf], refs[n_buf + 1]
        x, y, c = lax.axis_index("x"), lax.axis_index("y"), lax.axis_index("c")
        for k, (s, d, dev) in enumerate(plan(in_refs, land_refs, x, y, c)):
            cp = pltpu.make_async_remote_copy(src_ref=s, dst_ref=d, send_sem=send_sems.at[k], recv_sem=recv_sems.at[k],
                                              device_id=dev, device_id_type=pl.DeviceIdType.MESH)
            cp.wait_send()
            cp.wait_recv()

    hbm = pl.BlockSpec(memory_space=pltpu.HBM)
    sem = pl.BlockSpec(memory_space=pltpu.SEMAPHORE)
    bufs = st["ins"] + st["lands"]
    out = pl.pallas_call(
        body, name=name, in_specs=[hbm] * n_buf + [sem, sem, pl.BlockSpec(memory_space=pl.ANY)],
        out_specs=[hbm] * n_buf, out_shape=[pltpu.HBM(a.shape, a.dtype) for a in bufs],
        input_output_aliases={i: i for i in range(n_buf)},
        compiler_params=pltpu.CompilerParams(has_side_effects=pltpu.SideEffectType.DATAFLOW_SIDE_EFFECTING),
    )(*bufs, st["send"], st["recv"], after)
    return list(out[:n_in]), list(out[n_in:])


def _pair_forward(lands, name):
    shapes = [jax.ShapeDtypeStruct(a.shape, a.dtype) for a in lands]
    return _rdma_call(lands, shapes, lambda ins, outs, x, y, c: (_plan_gather_near(outs, x, y, c), []),
                      (N_CHIPS - 1) * _n_half_pieces(lands), 0, name, aliases={i: i for i in range(len(lands))})


def _sum_block_rows(r, C):
    return _rows_tile(r, max(16, (1 << 18) // C // 16 * 16))


def _sum_pair(a, recv, cidx, name):
    nch, _, r, C = a.shape
    rb = _sum_block_rows(r, C)

    def body(c_ref, a_ref, r_ref, o_ref):
        o_ref[...] = (a_ref[...] + r_ref[...]).astype(BF16)

    blk = pl.BlockSpec((None, rb, C), lambda j, i, c: (j, i, 0))
    return pl.pallas_call(
        body, name=name,
        grid_spec=pltpu.PrefetchScalarGridSpec(
            num_scalar_prefetch=1, grid=(nch, r // rb),
            in_specs=[pl.BlockSpec((None, None, rb, C), lambda j, i, c: (j, c[0], i, 0)), blk], out_specs=blk),
        out_shape=jax.ShapeDtypeStruct((nch, r, C), BF16),
        compiler_params=_params("parallel", "parallel"),
    )(cidx, a, recv)


def _sum_chips(mine, recv, ids, name):
    nch, r, C = recv.shape
    rb = _sum_block_rows(r, C)

    def body(ids_ref, m_ref, *rest):
        r_refs, o_ref = rest[:nch], rest[nch]
        chip = ids_ref[1]
        own = m_ref[...].astype(F32)
        acc = jnp.where(chip == 0, own, r_refs[0][...].astype(F32))
        for q in range(1, nch):
            acc = acc + jnp.where(chip == q, own, r_refs[q][...].astype(F32))
        o_ref[...] = acc

    def slot(q):
        return pl.BlockSpec((None, rb, C), lambda i, ids: (jnp.where(ids[1] == q, (q + 1) % nch, q), i, 0))

    return pl.pallas_call(
        body, name=name,
        grid_spec=pltpu.PrefetchScalarGridSpec(
            num_scalar_prefetch=1, grid=(r // rb,),
            in_specs=[pl.BlockSpec((None, rb, C), lambda i, ids: (ids[1], i, 0))] + [slot(q) for q in range(nch)],
            out_specs=pl.BlockSpec((None, rb, C), lambda i, ids: (ids[0], i, 0))),
        out_shape=jax.ShapeDtypeStruct((N_CORES, r, C), F32),
        compiler_params=_params("parallel"),
    )(ids, mine, *([recv] * nch))


PACK_COLS = 1024
_SHARDED = ("ada_w", "w_in", "w_branch_a", "w_branch_b", "w_out", "ffn_w_up", "ffn_w_down")
_LAYER_KEYS = ("ada_w", "w_in", "w_a", "w_b", "w_o", "w_up", "w_down")
_SMALL = ("c_ctx", "ada_b", "norm1_w", "sgu_ln_w", "sgu_ln_b", "sgu_w", "sgu_b", "hgrn_lower_bounds", "hgrn_norm_w",
          "norm2_w", "ffn_conv_b", "final_norm_w")
_ORDER = ("c_ctx", "ada_w", "ada_b", "norm1_w", "w_in", "sgu_ln_w", "sgu_ln_b", "sgu_w", "sgu_b", "hgrn_lower_bounds",
          "hgrn_norm_w", "w_branch_a", "w_branch_b", "w_out", "norm2_w", "ffn_w_up", "ffn_conv_w", "ffn_conv_b",
          "ffn_w_down", "final_norm_w")


def _pad_to(v, n):
    return jnp.concatenate([v, jnp.zeros((n - v.shape[0],), v.dtype)]) if v.shape[0] < n else v


def _round_up(n, m):
    return (n + m - 1) // m * m


def _pack(arrays, n_pad):
    flat = jnp.concatenate([a.reshape(-1) for a in arrays])
    return _pad_to(flat, n_pad)


def _unpack(flat, like):
    out, off = [], 0
    for a in like:
        out.append(flat[off:off + a.size].reshape(a.shape))
        off += a.size
    return out


def kernel(x, c, ctx, c_ctx, ada_w, ada_b, norm1_w, w_in, sgu_ln_w, sgu_ln_b, sgu_w, sgu_b, hgrn_lower_bounds, hgrn_norm_w, w_branch_a, w_branch_b, w_out, norm2_w, ffn_w_up, ffn_conv_w, ffn_conv_b, ffn_w_down, final_norm_w, loss_target, m_c_ctx, m_ada_w, m_ada_b, m_norm1_w, m_w_in, m_sgu_ln_w, m_sgu_ln_b, m_sgu_w, m_sgu_b, m_hgrn_lower_bounds, m_hgrn_norm_w, m_w_branch_a, m_w_branch_b, m_w_out, m_norm2_w, m_ffn_w_up, m_ffn_conv_w, m_ffn_conv_b, m_ffn_w_down, m_final_norm_w, v_c_ctx, v_ada_w, v_ada_b, v_norm1_w, v_w_in, v_sgu_ln_w, v_sgu_ln_b, v_sgu_w, v_sgu_b, v_hgrn_lower_bounds, v_hgrn_norm_w, v_w_branch_a, v_w_branch_b, v_w_out, v_norm2_w, v_ffn_w_up, v_ffn_conv_w, v_ffn_conv_b, v_ffn_w_down, v_final_norm_w):
    w = dict(c_ctx=c_ctx, ada_w=ada_w, ada_b=ada_b, norm1_w=norm1_w, w_in=w_in, sgu_ln_w=sgu_ln_w, sgu_ln_b=sgu_ln_b,
             sgu_w=sgu_w, sgu_b=sgu_b, hgrn_lower_bounds=hgrn_lower_bounds, hgrn_norm_w=hgrn_norm_w, w_branch_a=w_branch_a,
             w_branch_b=w_branch_b, w_out=w_out, norm2_w=norm2_w, ffn_w_up=ffn_w_up, ffn_conv_w=ffn_conv_w,
             ffn_conv_b=ffn_conv_b, ffn_w_down=ffn_w_down, final_norm_w=final_norm_w)
    mom = dict(zip(_ORDER, (m_c_ctx, m_ada_w, m_ada_b, m_norm1_w, m_w_in, m_sgu_ln_w, m_sgu_ln_b, m_sgu_w, m_sgu_b,
                            m_hgrn_lower_bounds, m_hgrn_norm_w, m_w_branch_a, m_w_branch_b, m_w_out, m_norm2_w, m_ffn_w_up,
                            m_ffn_conv_w, m_ffn_conv_b, m_ffn_w_down, m_final_norm_w)))
    var = dict(zip(_ORDER, (v_c_ctx, v_ada_w, v_ada_b, v_norm1_w, v_w_in, v_sgu_ln_w, v_sgu_ln_b, v_sgu_w, v_sgu_b,
                            v_hgrn_lower_bounds, v_hgrn_norm_w, v_w_branch_a, v_w_branch_b, v_w_out, v_norm2_w, v_ffn_w_up,
                            v_ffn_conv_w, v_ffn_conv_b, v_ffn_w_down, v_final_norm_w)))
    depth, D = norm1_w.shape
    dff = ffn_conv_b.shape[1]
    ctx_rows, seq = ctx.shape[1], x.shape[1]

    assert depth == 2, "the lower-bound softmax is written for two layers"
    core = lax.axis_index("c")
    chip = 2 * lax.axis_index("x") + lax.axis_index("y")
    ids = jnp.stack([core, chip]).astype(jnp.int32)

    first, rest = _LAYER_KEYS[:2], _LAYER_KEYS[2:]
    shard = lambda l, k: w[_SHARDED[_LAYER_KEYS.index(k)]][l].astype(BF16)
    started, conv_full = {}, []

    def landing(s):
        return lax.dynamic_update_slice(lax.empty((N_CHIPS,) + s.shape, s.dtype), s[None], (chip,) + (0,) * s.ndim)

    def start_gather(l, keys, tag):
        lands = [landing(shard(l, k)) for k in keys]
        started[tag] = _split_start([], lands, lambda ins, lds, x, y, c: _plan_gather_far(lds, x, y, c),
                                    (N_CHIPS - 1) * _n_half_pieces(lands), f"gather_start_{tag}")
        return started[tag]["token"]

    def finish_gather(keys, tag, after):
        _, lands = _split_wait(started[tag], after, f"gather_wait_{tag}")
        return dict(zip(keys, _pair_forward(lands, f"gather_forward_{tag}")))

    def layer_weights(l, after):
        if l == 0:
            got = _gather_weights([landing(shard(0, k)) for k in first] + [landing(ffn_conv_w)], "gather_weights_first")
            conv_full.append(jnp.transpose(got[-1], (1, 2, 3, 0, 4)).reshape(depth, 9, dff))
            out = dict(zip(first, got), token=start_gather(0, rest, "rest_0"))
        else:
            out = dict(finish_gather(first, f"first_{l}", after), token=0.0)

        def late(after_late):
            more = finish_gather(rest, f"rest_{l}", after_late)
            more["late_token"] = 0.0
            if l + 1 < depth:
                more["late_token"] = start_gather(l + 1, first, f"first_{l + 1}") + start_gather(l + 1, rest, f"rest_{l + 1}")
            return more

        return dict(out, conv_w=conv_full[0][l], late=late)

    groups, order = {}, []

    def as_parts(gs):
        return [g.reshape(N_CHIPS, N_CORES, g.size // (N_CHIPS * N_CORES * g.shape[-1]), g.shape[-1]) for g in gs]

    def pair_start(tag, l, keys, gs):
        parts = as_parts(gs)
        lands = [lax.empty((N_CHIPS,) + p.shape[2:], p.dtype) for p in parts]
        groups[tag] = dict(l=l, keys=keys, pair=_split_start(parts, lands, _plan_pair, _n_pair_copies(parts),
                                                             f"reduce_pair_start_{tag}"))
        order.append(tag)
        return groups[tag]["pair"]["token"]

    def chips_start(tag, after):
        parts, other = _split_wait(groups[tag]["pair"], after, f"reduce_pair_wait_{tag}")
        sums = [_sum_pair(a, o, ids, f"sum_pair_{tag}_{i}") for i, (a, o) in enumerate(zip(parts, other))]
        lands = [lax.empty(s.shape, s.dtype) for s in sums]
        groups[tag]["chips"] = _split_start(sums, lands, _plan_chips, _n_chips_copies(sums), f"reduce_chips_start_{tag}")
        return groups[tag]["chips"]["token"]

    def chips_finish(tag, after):
        sums, recv = _split_wait(groups[tag]["chips"], after, f"reduce_chips_wait_{tag}")
        return {(groups[tag]["l"], k): _sum_chips(sums[i], recv[i], ids, f"sum_chips_{tag}_{i}")
                for i, k in enumerate(groups[tag]["keys"])}

    def on_layer_grads(l, stage, gs):
        if stage == "early":
            return pair_start(f"early_{l}", l, list(gs), list(gs.values()))
        if stage == "late":
            return pair_start(f"late_{l}", l, ["w_in"], [gs["w_in"]]) + chips_start(f"early_{l}", gs["w_in"])
        return chips_start(f"late_{l}", gs["after"])

    W = dict(ada_b=ada_b, norm1_w=norm1_w, sgu_ln_w=sgu_ln_w, sgu_ln_b=sgu_ln_b, sgu_w=sgu_w.astype(BF16),
             sgu_bt=jnp.swapaxes(sgu_b, 1, 2), hlb=hgrn_lower_bounds, hnw=hgrn_norm_w, norm2_w=norm2_w, conv_b=ffn_conv_b,
             final_norm_w=final_norm_w)
    xs = jnp.concatenate([ctx[0], x[0]], axis=0)
    cv = jnp.concatenate([c_ctx[None, :], c, jnp.zeros((14, D), F32)], axis=0)
    loss_local, dxs, G, sa = _local_step(xs, cv, loss_target[0], W, layer_weights, on_layer_grads, ctx_rows)
    loss = lax.psum(loss_local, ("x", "y", "c"))
    grad_x = dxs[ctx_rows:][None]

    pad8 = lambda a: jnp.pad(a, ((0, 8 - a.shape[0]), (0, 0)))
    fact = jnp.concatenate([pad8(sa[1:2].astype(F32))] + [pad8(G["dmod"][l][1].reshape(N_MOD, D)) for l in range(depth)]
                           + [pad8(G["dmod"][l][0].reshape(N_MOD, D)) for l in range(depth)], axis=0)
    facts = _gather_all(fact, "gather_mod_factors")
    lhs = jnp.concatenate([facts[:, 0].astype(BF16), jnp.broadcast_to(sa[0:1], (8, D))], axis=0)
    ada_cols = N_MOD * D // N_CHIPS
    g_ada = []
    for l in range(depth):
        lo_x, lo_c = 8 * (1 + l), 8 * (1 + depth + l)
        rhs = jnp.concatenate([facts[:, lo_x:lo_x + N_MOD].reshape(8, N_MOD * D),
                               facts[:, lo_c:lo_c + N_MOD].reshape(8, N_MOD * D)], axis=0)
        rhs = lax.dynamic_slice_in_dim(rhs, chip * ada_cols, ada_cols, axis=1).astype(BF16)
        g_ada.append(_mm_tn(lhs, rhs, F32, f"dw_ada_{l}"))

    dh = G["hlb1"][depth - 1]
    small_like = [w[k] for k in _SMALL] + [jnp.zeros((depth, 9, dff), F32)]
    small = [G["c_ctx"], jnp.stack(G["ada_b"]), jnp.stack(G["norm1_w"]), jnp.stack(G["sgu_ln_w"]), jnp.stack(G["sgu_ln_b"]),
             jnp.stack(G["sgu_w"]), jnp.stack(G["sgu_b"]), jnp.stack([-dh, dh]), jnp.stack(G["hnw"]), jnp.stack(G["norm2_w"]),
             jnp.stack(G["conv_b"]), G["final_norm_w"], jnp.stack(G["conv_w"])]
    n_small = sum(a.size for a in small)
    n_small_pad = _round_up(n_small, N_CORES * 16 * PACK_COLS)
    small_rows = n_small_pad // (N_CORES * PACK_COLS)
    small_rep = jnp.broadcast_to(_pack(small, n_small_pad).reshape(1, N_CORES, small_rows, PACK_COLS),
                                 (N_CHIPS, N_CORES, small_rows, PACK_COLS))
    small_parts = as_parts([small_rep])
    small_sums = [_sum_pair(small_parts[0], _reduce_pair(small_parts, "reduce_pair_small")[0], ids, "sum_pair_small")]
    groups["small"] = dict(l=None, keys=["small"], chips=_split_start(
        small_sums, [lax.empty(small_sums[0].shape, small_sums[0].dtype)], _plan_chips, _n_chips_copies(small_sums),
        "reduce_chips_start_small"))

    def gather_halves(halves, name):
        return dict(zip(halves, _gather_pair(list(halves.values()), name)))

    last = order[-1]
    halves = {}
    for tag in order[:-1]:
        halves.update(chips_finish(tag, groups["small"]["chips"]["token_array"]))
    reduced = gather_halves(halves, "gather_pair")
    grads, delta, new_m, new_v = {}, {}, {}, {}

    def adamw_sharded(i):
        k = _SHARDED[i]
        gs = g_ada if i == 0 else [reduced[(l, _LAYER_KEYS[i])].reshape(w[k].shape[1:]) for l in range(depth)]
        grads[k], delta[k], new_m[k], new_v[k] = _adamw(w[k], gs, mom[k], var[k], f"adamw_{k}")

    last_keys = groups[last]["keys"]
    for i in range(len(_SHARDED)):
        if _LAYER_KEYS[i] not in last_keys:
            adamw_sharded(i)
    halves = chips_finish(last, new_v[_SHARDED[-1]])
    halves.update(chips_finish("small", new_v[_SHARDED[-1]]))
    reduced.update(gather_halves(halves, "gather_pair_last"))
    for i in range(len(_SHARDED)):
        if _LAYER_KEYS[i] in last_keys:
            adamw_sharded(i)

    g_small = _unpack(reduced[(None, "small")].reshape(-1), small_like)
    grads.update(zip(_SMALL, g_small[:-1]))
    grads["ffn_conv_w"] = lax.dynamic_slice_in_dim(g_small[-1].reshape(depth, 3, 3, dff), chip * (dff // N_CHIPS),
                                                   dff // N_CHIPS, axis=3)
    packed = _SMALL + ("ffn_conv_w",)
    n_pad = _round_up(sum(w[k].size for k in packed), 16 * PACK_COLS)
    pack = lambda t: _pack([t[k] for k in packed], n_pad).reshape(1, -1, PACK_COLS)
    _, d, nm, nv = _adamw(pack(w), [pack(grads)[0]], pack(mom), pack(var), "adamw_packed")
    like = [w[k] for k in packed]
    for src, dst in ((d, delta), (nm, new_m), (nv, new_v)):
        dst.update(zip(packed, _unpack(src.reshape(-1), like)))

    return (loss, grad_x, *[grads[k] for k in _ORDER], *[delta[k] for k in _ORDER], *[new_m[k] for k in _ORDER],
            *[new_v[k] for k in _ORDER])
```

```python
import functools

import jax
import jax.numpy as jnp
from jax import lax
from jax.experimental import pallas as pl
from jax.experimental.pallas import tpu as pltpu

F32 = jnp.float32
BF16 = jnp.bfloat16

GRID_W = 64
HG_CHUNK = 64
SGU_CHUNK = 128
HEAD = 128
TB = 256
N_MOD = 6
RMS_EPS = 1e-6
LN_EPS = 1e-5
VMEM_LIMIT = 48 * 1024 * 1024
N_CHIPS = 4
N_CORES = 2

ADAM_LR = 0.001
ADAM_B1 = 0.9
ADAM_B2 = 0.999
ADAM_EPS = 1e-08
ADAM_WD = 0.01
ADAM_STEP = 10

_GELU_C = 0.7978845608028654
_GELU_A = 0.044715


def _sigmoid(x):
    return 0.5 * jnp.tanh(0.5 * x) + 0.5


def _silu(x):
    return x * _sigmoid(x)


def _silu_both(x):
    s = _sigmoid(x)
    return x * s, s * (1.0 + x * (1.0 - s))


def _dsilu(x):
    return _silu_both(x)[1]


def _gelu_both(x):
    x2 = x * x
    t = jnp.tanh(_GELU_C * (x + _GELU_A * x2 * x))
    h = 0.5 * (1.0 + t)
    return x * h, h + 0.5 * x * (1.0 - t * t) * (_GELU_C + 3.0 * _GELU_C * _GELU_A * x2)


def _gelu(x):
    return 0.5 * x * (1.0 + jnp.tanh(_GELU_C * (x + _GELU_A * x * x * x)))


def _dgelu(x):
    return _gelu_both(x)[1]


def _dot(a, b, ca, cb):
    return lax.dot_general(a, b, (((ca,), (cb,)), ((), ())), preferred_element_type=F32)


def _nn(a, b):
    return _dot(a, b, 1, 0)


def _nt(a, b):
    return _dot(a, b, 1, 1)


def _tn(a, b):
    return _dot(a, b, 0, 0)


def _params(*sem, vmem=VMEM_LIMIT):
    return pltpu.CompilerParams(dimension_semantics=sem if sem else None, vmem_limit_bytes=vmem)


def _stream_of(i, ctx_blocks):
    return (i >= ctx_blocks).astype(jnp.int32)


def _mm(a, b, mode, tm, tn, tk, out_dtype, name, add=None, b_chips=False, out_chips=False):
    if not b_chips:
        bshape = b.shape
    else:
        bshape = (b.shape[1], N_CHIPS * b.shape[2])
    if mode == "nn":
        (M, K), (K2, N) = a.shape, bshape
    elif mode == "nt":
        (M, K), (N, K2) = a.shape, bshape
    else:
        (K, M), (K2, N) = a.shape, bshape
    assert K == K2 and M % tm == 0 and N % tn == 0 and K % tk == 0, (name, a.shape, b.shape, tm, tn, tk)
    nk = K // tk
    if mode == "tn":
        a_spec = pl.BlockSpec((tk, tm), lambda j, i, k: (k, i))
    else:
        a_spec = pl.BlockSpec((tm, tk), lambda j, i, k: (i, k))
    if not b_chips:
        if mode == "nt":
            b_spec = pl.BlockSpec((tn, tk), lambda j, i, k: (j, k))
        else:
            b_spec = pl.BlockSpec((tk, tn), lambda j, i, k: (k, j))
    else:
        cols = b.shape[2]
        if mode == "nn":
            per = cols // tn
            assert cols % tn == 0
            b_spec = pl.BlockSpec((None, tk, tn), lambda j, i, k: (j // per, k, j % per))
        else:
            per = cols // tk
            assert mode == "nt" and cols % tk == 0
            b_spec = pl.BlockSpec((None, tn, tk), lambda j, i, k: (k // per, j, k % per))
    if out_chips:
        per_o = (N // N_CHIPS) // tn
        assert (N // N_CHIPS) % tn == 0 and add is None
        o_spec = pl.BlockSpec((None, tm, tn), lambda j, i, k: (j // per_o, i, j % per_o))
        o_shape = (N_CHIPS, M, N // N_CHIPS)
    else:
        o_spec = pl.BlockSpec((tm, tn), lambda j, i, k: (i, j))
        o_shape = (M, N)
    ca, cb = {"nn": (1, 0), "nt": (1, 1), "tn": (0, 0)}[mode]

    def body(a_ref, b_ref, *rest):
        if add is None:
            o_ref, acc = rest
        else:
            add_ref, o_ref, acc = rest
        k = pl.program_id(2)

        @pl.when(k == 0)
        def _():
            acc[...] = jnp.zeros_like(acc)

        acc[...] += _dot(a_ref[...], b_ref[...], ca, cb)

        @pl.when(k == nk - 1)
        def _():
            r = acc[...]
            if add is not None:
                r = r + add_ref[...]
            o_ref[...] = r.astype(out_dtype)

    ins = [a, b] + ([] if add is None else [add])
    specs = [a_spec, b_spec] + ([] if add is None else [o_spec])
    return pl.pallas_call(
        body, name=name, grid=(N // tn, M // tm, nk), in_specs=specs, out_specs=o_spec,
        out_shape=jax.ShapeDtypeStruct(o_shape, out_dtype),
        scratch_shapes=[pltpu.VMEM((tm, tn), F32)],
        compiler_params=_params("parallel", "parallel", "arbitrary"),
    )(*ins)


def _tile(n, pref):
    if n <= pref:
        return n
    best = None
    for t in range(128, pref + 1, 128):
        if n % t == 0:
            best = t
    assert best is not None, (n, pref)
    return best


def _rows_tile(n, pref):
    if n <= pref:
        return n
    best = None
    for t in range(16, pref + 1, 16):
        if n % t == 0:
            best = t
    assert best is not None, (n, pref)
    return best


def _mm_nn_w(a, wg, out_dtype, name):
    M, K = a.shape
    return _mm(a, wg, "nn", _rows_tile(M, 1088), _tile(wg.shape[2], 1536), _tile(K, 1536), out_dtype, name, b_chips=True)


def _mm_nt_w(a, wg, out_dtype, name):
    M, K = a.shape
    return _mm(a, wg, "nt", _rows_tile(M, 1088), _tile(wg.shape[1], 1024), _tile(wg.shape[2], 1536), out_dtype, name,
               b_chips=True)


def _mm_tn(a, b, out_dtype, name, out_chips=False):
    K, M = a.shape
    N = b.shape[1]
    ncol = N // N_CHIPS if out_chips else N
    tm, tn = _tile(M, 1408), _tile(ncol, 1408)
    if tm * tn > 1408 * 1152:
        tn = _tile(ncol, 1152)
    return _mm(a, b, "tn", tm, tn, _rows_tile(K, 2176), out_dtype, name, out_chips=out_chips)


def _mod_fwd(cv, wg, b, name):
    R, D = cv.shape
    tn = wg.shape[2]
    N = N_CHIPS * tn

    def body(cv_ref, w_ref, b_ref, mod_ref, sa_ref):
        sa = _silu(cv_ref[...]).astype(BF16)
        sa_ref[...] = sa
        mod_ref[...] = _nn(sa, w_ref[...]) + b_ref[...]

    return pl.pallas_call(
        body, name=name, grid=(N_CHIPS,),
        in_specs=[pl.BlockSpec((R, D), lambda j: (0, 0)), pl.BlockSpec((None, D, tn), lambda j: (j, 0, 0)),
                  pl.BlockSpec((1, tn), lambda j: (0, j))],
        out_specs=[pl.BlockSpec((R, tn), lambda j: (0, j)), pl.BlockSpec((R, D), lambda j: (0, 0))],
        out_shape=[jax.ShapeDtypeStruct((R, N), F32), jax.ShapeDtypeStruct((R, D), BF16)],
        compiler_params=_params("arbitrary"),
    )(cv, wg, b)


def _cvec_bwd(dmod, wg, cv, name):
    R, N = dmod.shape
    D = wg.shape[1]
    tk = wg.shape[2]
    nk = N_CHIPS

    def body(dm_ref, w_ref, cv_ref, o_ref):
        k = pl.program_id(0)

        @pl.when(k == 0)
        def _():
            o_ref[...] = jnp.zeros_like(o_ref)

        o_ref[...] += _nt(dm_ref[...].astype(BF16), w_ref[...])

        @pl.when(k == nk - 1)
        def _():
            o_ref[...] = o_ref[...] * _dsilu(cv_ref[...])

    return pl.pallas_call(
        body, name=name, grid=(nk,),
        in_specs=[pl.BlockSpec((R, tk), lambda k: (0, k)), pl.BlockSpec((None, D, tk), lambda k: (k, 0, 0)),
                  pl.BlockSpec((R, D), lambda k: (0, 0))],
        out_specs=pl.BlockSpec((R, D), lambda k: (0, 0)),
        out_shape=jax.ShapeDtypeStruct((R, D), F32),
        compiler_params=_params("arbitrary"),
    )(dmod, wg, cv)


def _norm_mod(x, nw, mod, which, ctx_rows, name):
    T, D = x.shape
    cb = ctx_rows // TB

    def body(x_ref, nw_ref, mod_ref, h_ref):
        xv = x_ref[...]
        r = lax.rsqrt(jnp.mean(xv * xv, axis=-1, keepdims=True) + RMS_EPS)
        y = xv * r * nw_ref[...]
        sh = mod_ref[which:which + 1, :]
        sc = mod_ref[which + 1:which + 2, :]
        h_ref[...] = (y * (1.0 + sc) + sh).astype(BF16)

    return pl.pallas_call(
        body, name=name, grid=(T // TB,),
        in_specs=[pl.BlockSpec((TB, D), lambda i: (i, 0)), pl.BlockSpec((1, D), lambda i: (0, 0)),
                  pl.BlockSpec((None, N_MOD, D), lambda i: (_stream_of(i, cb), 0, 0))],
        out_specs=pl.BlockSpec((TB, D), lambda i: (i, 0)),
        out_shape=jax.ShapeDtypeStruct((T, D), BF16),
        compiler_params=_params("parallel"),
    )(x, nw, mod)


def _norm_mod_bwd(dh, x, dres, nw, mod, which, ctx_rows, name):
    T, D = x.shape
    cb = ctx_rows // TB

    def body(dh_ref, x_ref, dres_ref, nw_ref, mod_ref, dx_ref, dm_ref, dnw_ref):
        i = pl.program_id(0)

        @pl.when(i == 0)
        def _():
            dnw_ref[...] = jnp.zeros_like(dnw_ref)

        @pl.when((i == 0) | (i == cb))
        def _():
            dm_ref[...] = jnp.zeros_like(dm_ref)

        xv = x_ref[...]
        dh = dh_ref[...]
        r = lax.rsqrt(jnp.mean(xv * xv, axis=-1, keepdims=True) + RMS_EPS)
        xh = xv * r
        nwv = nw_ref[...]
        sc = mod_ref[which + 1:which + 2, :]
        y = xh * nwv
        dm_ref[0:1, :] += jnp.sum(dh, axis=0, keepdims=True)
        dm_ref[1:2, :] += jnp.sum(dh * y, axis=0, keepdims=True)
        dy = dh * (1.0 + sc)
        dnw_ref[...] += jnp.sum(dy * xh, axis=0, keepdims=True)
        dxh = dy * nwv
        dx_ref[...] = dres_ref[...] + r * (dxh - xh * jnp.mean(dxh * xh, axis=-1, keepdims=True))

    return pl.pallas_call(
        body, name=name, grid=(T // TB,),
        in_specs=[pl.BlockSpec((TB, D), lambda i: (i, 0)), pl.BlockSpec((TB, D), lambda i: (i, 0)),
                  pl.BlockSpec((TB, D), lambda i: (i, 0)), pl.BlockSpec((1, D), lambda i: (0, 0)),
                  pl.BlockSpec((None, N_MOD, D), lambda i: (_stream_of(i, cb), 0, 0))],
        out_specs=[pl.BlockSpec((TB, D), lambda i: (i, 0)),
                   pl.BlockSpec((None, 2, D), lambda i: (_stream_of(i, cb), 0, 0)),
                   pl.BlockSpec((1, D), lambda i: (0, 0))],
        out_shape=[jax.ShapeDtypeStruct((T, D), F32), jax.ShapeDtypeStruct((2, 2, D), F32),
                   jax.ShapeDtypeStruct((1, D), F32)],
        compiler_params=_params("arbitrary"),
    )(dh, x, dres, nw, mod)


def _scan_chunk(n, rev, n_ctx, n_all):
    if not rev:
        return n
    return jnp.where(n < n_ctx, n_ctx - 1 - n, n_all - 1 + n_ctx - n)


def _cumsum_rows(x, rev):
    rows = x.shape[0]
    row = lax.broadcasted_iota(jnp.int32, (rows, 1), 0)
    s = 1
    while s < rows:
        if not rev:
            x = x + jnp.where(row >= s, pltpu.roll(x, s, 0), 0.0)
        else:
            x = x + jnp.where(row < rows - s, pltpu.roll(x, rows - s, 0), 0.0)
        s *= 2
    return x


def _lower_bound(hlb_ref, layer):
    h = hlb_ref[...]
    if layer == 0:
        return jnp.zeros_like(h[0:1, :])
    return _sigmoid(h[1:2, :] - h[0:1, :])


HG_STEP = 4


def _step_rows(j, rev, backward):
    sub = j if rev == backward else HG_STEP - 1 - j
    return slice(sub * HG_CHUNK, (sub + 1) * HG_CHUNK)


def _hgrn_gates(q_ref, f_ref, hlb_ref, layer, rev, rows):
    lb = _lower_bound(hlb_ref, layer)
    z = f_ref[rows, :]
    sig = 1.0 / (1.0 + jnp.exp(-z))
    fg = lb + (1.0 - lb) * sig
    kk = (1.0 - lb) * (1.0 - sig)
    g = jnp.log(fg)
    b = _cumsum_rows(g, rev)
    bt = jnp.sum(g, axis=0, keepdims=True)
    mid = HG_CHUNK // 2
    r = b[mid:mid + 1, :] if rev else b[mid - 1:mid, :]
    qh = _silu(q_ref[rows, :])
    return lb, sig, fg, kk, b, bt, r, qh


def _tri_mask(rev):
    t = lax.broadcasted_iota(jnp.int32, (HG_CHUNK, HG_CHUNK), 0)
    s = lax.broadcasted_iota(jnp.int32, (HG_CHUNK, HG_CHUNK), 1)
    return (s >= t) if rev else (s <= t)


def _hgrn_fwd(parts, hlb, layer, rev, ctx_rows, name, o_add=None):
    T = parts.shape[0]
    D = hlb.shape[1] // 2
    nh = D // HEAD
    n_all, n_ctx = T // HG_CHUNK, ctx_rows // HG_CHUNK
    assert n_all % HG_STEP == 0 and n_ctx % HG_STEP == 0
    n_steps = n_all // HG_STEP
    block = functools.partial(_scan_chunk, rev=rev, n_ctx=n_ctx // HG_STEP, n_all=n_steps)
    fcol = 2 if rev else 1

    def body(q_ref, f_ref, i_ref, hlb_ref, *rest):
        if o_add is None:
            o_ref, st_ref, s_scr = rest
        else:
            oa_ref, o_ref, st_ref, s_scr = rest
        n = pl.program_id(0)

        @pl.when(n == 0)
        def _():
            s_scr[...] = jnp.zeros_like(s_scr)

        mask = _tri_mask(rev)
        hs = [slice(h * HEAD, (h + 1) * HEAD) for h in range(nh)]
        for j in range(HG_STEP):
            rows = _step_rows(j, rev, False)
            lb, sig, fg, kk, b, bt, r, qh = _hgrn_gates(q_ref, f_ref, hlb_ref, layer, rev, rows)
            qr = (qh * jnp.exp(b - r)).astype(BF16)
            kr = (kk * jnp.exp(r - b)).astype(BF16)
            qe = (qh * jnp.exp(b)).astype(BF16)
            ke = (kk * jnp.exp(bt - b)).astype(BF16)
            dec = jnp.exp(bt)
            v = i_ref[rows, :].astype(BF16)
            st = [s_scr[h] for h in range(nh)]
            a_raw = [_nt(qr[:, sl], kr[:, sl]) for sl in hs]
            o_int = [_nt(qe[:, sl], st[h].astype(BF16)) for h, sl in enumerate(hs)]
            kv = [_tn(v[:, sl], ke[:, sl]) for sl in hs]
            for h, sl in enumerate(hs):
                st_ref[j, h] = st[h]
                o = _nn(jnp.where(mask, a_raw[h], 0.0).astype(BF16), v[:, sl]) + o_int[h]
                if o_add is not None:
                    o = o + oa_ref[rows, sl]
                o_ref[rows, sl] = o
                s_scr[h] = st[h] * dec[:, sl] + kv[h]

    cspec = lambda col: pl.BlockSpec((HG_STEP * HG_CHUNK, D), lambda n: (block(n), col))
    ins = [parts, parts, parts, hlb]
    specs = [cspec(0), cspec(fcol), cspec(3), pl.BlockSpec((2, D), lambda n: (0, 1 if rev else 0))]
    if o_add is not None:
        ins.append(o_add)
        specs.append(cspec(0))
    return pl.pallas_call(
        body, name=name, grid=(n_steps,), in_specs=specs,
        out_specs=[cspec(0), pl.BlockSpec((HG_STEP, nh, HEAD, HEAD), lambda n: (n, 0, 0, 0))],
        out_shape=[jax.ShapeDtypeStruct((T, D), F32), jax.ShapeDtypeStruct((n_all, nh, HEAD, HEAD), F32)],
        scratch_shapes=[pltpu.VMEM((nh, HEAD, HEAD), F32)],
        compiler_params=_params("arbitrary"),
    )(*ins)


def _hgrn_bwd(parts, hlb, do, states, layer, rev, ctx_rows, name, other=None, dparts=None):
    T = parts.shape[0]
    D = hlb.shape[1] // 2
    nh = D // HEAD
    n_all, n_ctx = T // HG_CHUNK, ctx_rows // HG_CHUNK
    assert n_all % HG_STEP == 0 and n_ctx % HG_STEP == 0
    n_steps = n_all // HG_STEP
    step = lambda m: n_steps - 1 - m
    block = lambda m: _scan_chunk(step(m), rev, n_ctx // HG_STEP, n_steps)
    fcol = 2 if rev else 1
    has_add = other is not None
    assert not has_add or rev

    def body(q_ref, f_ref, i_ref, hlb_ref, do_ref, st_ref, *rest):
        if has_add:
            dqa_ref, dza_ref, dia_ref, _, out_ref, dlb_ref, ds_scr = rest
            dq_ref, dz_ref, di_ref = out_ref.at[:, 0:D], out_ref.at[:, 2 * D:3 * D], out_ref.at[:, 3 * D:4 * D]
            out_ref[:, D:2 * D] = dza_ref[...]
        else:
            dq_ref, dz_ref, di_ref, dlb_ref, ds_scr = rest
        m = pl.program_id(0)

        @pl.when(m == 0)
        def _():
            ds_scr[...] = jnp.zeros_like(ds_scr)
            dlb_ref[...] = jnp.zeros_like(dlb_ref)

        mask = _tri_mask(rev)
        hs = [slice(h * HEAD, (h + 1) * HEAD) for h in range(nh)]
        for j in range(HG_STEP):
            rows = _step_rows(j, rev, True)
            slot = HG_STEP - 1 - j
            lb, sig, fg, kk, b, bt, r, qh = _hgrn_gates(q_ref, f_ref, hlb_ref, layer, rev, rows)
            e_qr = jnp.exp(b - r)
            e_kr = jnp.exp(r - b)
            e_b = jnp.exp(b)
            e_ke = jnp.exp(bt - b)
            dec = jnp.exp(bt)
            qr = (qh * e_qr).astype(BF16)
            kr = (kk * e_kr).astype(BF16)
            qe = (qh * e_b).astype(BF16)
            ke = (kk * e_ke).astype(BF16)
            v = i_ref[rows, :].astype(BF16)
            dov = do_ref[rows, :].astype(BF16)
            st = [st_ref[slot, h] for h in range(nh)]
            dst = [ds_scr[h] for h in range(nh)]
            stb = [t.astype(BF16) for t in st]
            dstb = [t.astype(BF16) for t in dst]
            a_raw = [_nt(qr[:, sl], kr[:, sl]) for sl in hs]
            da_raw = [_nt(dov[:, sl], v[:, sl]) for sl in hs]
            dq_int = [_nn(dov[:, sl], stb[h]) for h, sl in enumerate(hs)]
            dk_int = [_nn(v[:, sl], dstb[h]) for h, sl in enumerate(hs)]
            dv_int = [_nt(ke[:, sl], dstb[h]) for h, sl in enumerate(hs)]
            ds_new = [_tn(dov[:, sl], qe[:, sl]) for sl in hs]
            a = [jnp.where(mask, t, 0.0).astype(BF16) for t in a_raw]
            da = [jnp.where(mask, t, 0.0).astype(BF16) for t in da_raw]
            dv_parts = [_tn(a[h], dov[:, sl]) + dv_int[h] for h, sl in enumerate(hs)]
            dq_parts = [_nn(da[h], kr[:, sl]) * e_qr[:, sl] + dq_int[h] * e_b[:, sl] for h, sl in enumerate(hs)]
            dki_parts = [dk_int[h] * e_ke[:, sl] for h, sl in enumerate(hs)]
            dk_parts = [_tn(da[h], qr[:, sl]) * e_kr[:, sl] + dki_parts[h] for h, sl in enumerate(hs)]
            dbt_parts = [dec[:, sl] * jnp.sum(st[h] * dst[h], axis=0, keepdims=True) for h, sl in enumerate(hs)]
            for h, sl in enumerate(hs):
                ds_scr[h] = dst[h] * dec[:, sl] + ds_new[h]
            dq = jnp.concatenate(dq_parts, axis=1)
            dk = jnp.concatenate(dk_parts, axis=1)
            dki = jnp.concatenate(dki_parts, axis=1)
            dv = jnp.concatenate(dv_parts, axis=1)
            dbt = jnp.concatenate(dbt_parts, axis=1) + jnp.sum(kk * dki, axis=0, keepdims=True)
            db = qh * dq - kk * dk
            dg = _cumsum_rows(db, not rev) + dbt
            df = dg / fg - dk
            dz_ref[rows, :] = (df * (1.0 - lb) * sig * (1.0 - sig)).astype(BF16)
            dlb_ref[...] += jnp.sum(df * (1.0 - sig), axis=0, keepdims=True)
            dqr = dq * _dsilu(q_ref[rows, :])
            if has_add:
                dqr = dqr + dqa_ref[rows, :]
                dv = dv + dia_ref[rows, :]
            dq_ref[rows, :] = dqr.astype(dq_ref.dtype)
            di_ref[rows, :] = dv.astype(di_ref.dtype)

        @pl.when(m == n_steps - 1)
        def _():
            lb = _lower_bound(hlb_ref, layer)
            if layer == 0:
                dlb_ref[...] = jnp.zeros_like(dlb_ref)
            else:
                dlb_ref[...] = dlb_ref[...] * lb * (1.0 - lb)

    cspec = lambda col: pl.BlockSpec((HG_STEP * HG_CHUNK, D), lambda m: (block(m), col))
    ins = [parts, parts, parts, hlb, do, states]
    specs = [cspec(0), cspec(fcol), cspec(3), pl.BlockSpec((2, D), lambda m: (0, 1 if rev else 0)), cspec(0),
             pl.BlockSpec((HG_STEP, nh, HEAD, HEAD), lambda m: (step(m), 0, 0, 0))]
    dlb_spec = pl.BlockSpec((1, D), lambda m: (0, 0))
    dlb_shape = jax.ShapeDtypeStruct((1, D), F32)
    if has_add:
        return pl.pallas_call(
            body, name=name, grid=(n_steps,),
            in_specs=specs + [cspec(0), cspec(0), cspec(0), pl.BlockSpec(memory_space=pl.ANY)],
            out_specs=[pl.BlockSpec((HG_STEP * HG_CHUNK, 4 * D), lambda m: (block(m), 0)), dlb_spec],
            out_shape=[jax.ShapeDtypeStruct(dparts.shape, dparts.dtype), dlb_shape],
            scratch_shapes=[pltpu.VMEM((nh, HEAD, HEAD), F32)], input_output_aliases={len(ins) + 3: 0},
            compiler_params=_params("arbitrary"),
        )(*ins, *other, dparts)
    return pl.pallas_call(
        body, name=name, grid=(n_steps,), in_specs=specs,
        out_specs=[cspec(0), cspec(0), cspec(0), dlb_spec],
        out_shape=[jax.ShapeDtypeStruct((T, D), F32), jax.ShapeDtypeStruct((T, D), BF16),
                   jax.ShapeDtypeStruct((T, D), F32), dlb_shape],
        scratch_shapes=[pltpu.VMEM((nh, HEAD, HEAD), F32)],
        compiler_params=_params("arbitrary"),
    )(*ins)


def _sgu_ln(gv, lnw_ref, lnb_ref):
    mu = jnp.mean(gv, axis=-1, keepdims=True)
    xc = gv - mu
    rstd = lax.rsqrt(jnp.mean(xc * xc, axis=-1, keepdims=True) + LN_EPS)
    xh = xc * rstd
    return xh, rstd, xh * lnw_ref[...] + lnb_ref[...]


def _sgu_fwd(parts, lnw, lnb, w, bt, name):
    T = parts.shape[0]
    D = lnw.shape[1]
    G = D // HEAD

    def body(u_ref, v_ref, lnw_ref, lnb_ref, w_ref, bt_ref, ya_ref):
        gu = _gelu(u_ref[...])
        _, _, vn = _sgu_ln(_gelu(v_ref[...]), lnw_ref, lnb_ref)
        vnb = vn.astype(BF16)
        for g in range(G):
            sl = slice(g * HEAD, (g + 1) * HEAD)
            mixed = _nn(w_ref[g], vnb[:, sl]) + bt_ref[:, g:g + 1]
            ya_ref[:, sl] = (gu[:, sl] * mixed).astype(BF16)

    return pl.pallas_call(
        body, name=name, grid=(T // SGU_CHUNK,),
        in_specs=[pl.BlockSpec((SGU_CHUNK, D), lambda n: (n, 4)), pl.BlockSpec((SGU_CHUNK, D), lambda n: (n, 5)),
                  pl.BlockSpec((1, D), lambda n: (0, 0)), pl.BlockSpec((1, D), lambda n: (0, 0)),
                  pl.BlockSpec((G, SGU_CHUNK, SGU_CHUNK), lambda n: (0, 0, 0)),
                  pl.BlockSpec((SGU_CHUNK, G), lambda n: (0, 0))],
        out_specs=pl.BlockSpec((SGU_CHUNK, D), lambda n: (n, 0)),
        out_shape=jax.ShapeDtypeStruct((T, D), BF16),
        compiler_params=_params("parallel"),
    )(parts, parts, lnw, lnb, w, bt)


def _sgu_bwd(parts, dya, lnw, lnb, w, bt, dparts, name):
    T = parts.shape[0]
    D = lnw.shape[1]
    G = D // HEAD

    def body(u_ref, v_ref, dya_ref, lnw_ref, lnb_ref, w_ref, bt_ref, dparts_in,
             duv_ref, dw_ref, dbt_ref, dlnw_ref, dlnb_ref, dvn_scr):
        du_ref = duv_ref.at[:, 0:D]
        dv_ref = duv_ref.at[:, D:2 * D]
        n = pl.program_id(0)

        @pl.when(n == 0)
        def _():
            dw_ref[...] = jnp.zeros_like(dw_ref)
            dbt_ref[...] = jnp.zeros_like(dbt_ref)
            dlnw_ref[...] = jnp.zeros_like(dlnw_ref)
            dlnb_ref[...] = jnp.zeros_like(dlnb_ref)

        gu, dgu = _gelu_both(u_ref[...])
        gv, dgv_dv = _gelu_both(v_ref[...])
        xh, rstd, vn = _sgu_ln(gv, lnw_ref, lnb_ref)
        vnb = vn.astype(BF16)
        dya = dya_ref[...]
        lane = lax.broadcasted_iota(jnp.int32, (SGU_CHUNK, G), 1)
        dbt = jnp.zeros((SGU_CHUNK, G), F32)
        for g in range(G):
            sl = slice(g * HEAD, (g + 1) * HEAD)
            wg = w_ref[g]
            mixed = _nn(wg, vnb[:, sl]) + bt_ref[:, g:g + 1]
            dmix = dya[:, sl] * gu[:, sl]
            du_ref[:, sl] = (dya[:, sl] * mixed * dgu[:, sl]).astype(BF16)
            dmb = dmix.astype(BF16)
            dvn_scr[:, sl] = _tn(wg, dmb)
            dw_ref[g] += _nt(dmb, vnb[:, sl])
            dbt = dbt + jnp.where(lane == g, jnp.sum(dmix, axis=1, keepdims=True), 0.0)
        dbt_ref[...] += dbt
        dvn = dvn_scr[...]
        dlnw_ref[...] += jnp.sum(dvn * xh, axis=0, keepdims=True)
        dlnb_ref[...] += jnp.sum(dvn, axis=0, keepdims=True)
        dxh = dvn * lnw_ref[...]
        dgv = rstd * (dxh - jnp.mean(dxh, axis=-1, keepdims=True) - xh * jnp.mean(dxh * xh, axis=-1, keepdims=True))
        dv_ref[...] = (dgv * dgv_dv).astype(BF16)

    row = lambda col: pl.BlockSpec((SGU_CHUNK, D), lambda n: (n, col))
    vec = pl.BlockSpec((1, D), lambda n: (0, 0))
    wsp = pl.BlockSpec((G, SGU_CHUNK, SGU_CHUNK), lambda n: (0, 0, 0))
    bsp = pl.BlockSpec((SGU_CHUNK, G), lambda n: (0, 0))
    return pl.pallas_call(
        body, name=name, grid=(T // SGU_CHUNK,),
        in_specs=[row(4), row(5), row(0), vec, vec, wsp, bsp, pl.BlockSpec(memory_space=pl.ANY)],
        out_specs=[pl.BlockSpec((SGU_CHUNK, 2 * D), lambda n: (n, 2)), wsp, bsp, vec, vec],
        out_shape=[jax.ShapeDtypeStruct(dparts.shape, dparts.dtype),
                   jax.ShapeDtypeStruct((G, SGU_CHUNK, SGU_CHUNK), F32), jax.ShapeDtypeStruct((SGU_CHUNK, G), F32),
                   jax.ShapeDtypeStruct((1, D), F32), jax.ShapeDtypeStruct((1, D), F32)],
        scratch_shapes=[pltpu.VMEM((SGU_CHUNK, D), F32)], input_output_aliases={7: 0},
        compiler_params=_params("arbitrary"),
    )(parts, parts, dya, lnw, lnb, w, bt, dparts)


TBT = 256
VMEM_LIMIT_TOKEN_OUT = 58 * 1024 * 1024


def _rows_weight_spec(wg):
    return pl.BlockSpec(wg.shape, lambda i: (0, 0, 0))


def _full(w_ref):
    return w_ref[...].reshape(w_ref.shape[0] * w_ref.shape[1], w_ref.shape[2])


def _token_out_fwd(o, parts, ya, x, mod, hnw, wa, wb, wo, ctx_rows, name):
    T, D = x.shape
    nh = D // HEAD
    cb = ctx_rows // TBT

    def body(o_ref, og_ref, ga_ref, gb_ref, ya_ref, x_ref, mod_ref, hnw_ref, wa_ref, wb_ref, wo_ref,
             yb_ref, pa_ref, pb_ref, mg_ref, tmo_ref, xm_ref):
        ov = o_ref[...]
        so = _silu(og_ref[...])
        nw = hnw_ref[...]
        for h in range(nh):
            sl = slice(h * HEAD, (h + 1) * HEAD)
            seg = ov[:, sl]
            r = lax.rsqrt(jnp.mean(seg * seg, axis=-1, keepdims=True) + RMS_EPS)
            yb_ref[:, sl] = (seg * r * nw * so[:, sl]).astype(BF16)
        pa = _nn(ya_ref[...], _full(wa_ref))
        pb = _nn(yb_ref[...], _full(wb_ref))
        pa_ref[...] = pa
        pb_ref[...] = pb
        mg = (_sigmoid(ga_ref[...]) * pa + _sigmoid(gb_ref[...]) * pb).astype(BF16)
        mg_ref[...] = mg
        out = _nn(mg, _full(wo_ref))
        tmo_ref[...] = out
        xm_ref[...] = x_ref[...] + mod_ref[2:3, :] * out

    row = lambda col: pl.BlockSpec((TBT, D), lambda i: (i, col))
    wsp = _rows_weight_spec(wa)
    sd = lambda dt: jax.ShapeDtypeStruct((T, D), dt)
    return pl.pallas_call(
        body, name=name, grid=(T // TBT,),
        in_specs=[row(0), row(6), row(7), row(8), row(0), row(0),
                  pl.BlockSpec((None, N_MOD, D), lambda i: (_stream_of(i, cb), 0, 0)),
                  pl.BlockSpec((1, HEAD), lambda i: (0, 0)), wsp, wsp, wsp],
        out_specs=[row(0)] * 6,
        out_shape=[sd(BF16), sd(F32), sd(F32), sd(BF16), sd(F32), sd(F32)],
        compiler_params=_params("parallel", vmem=VMEM_LIMIT_TOKEN_OUT),
    )(o, parts, parts, parts, ya, x, mod, hnw, wa, wb, wo)


def _token_out_bwd(dx, tmo, pa, pb, o, parts, mod, hnw, wa, wb, wo, ctx_rows, name):
    T, D = dx.shape
    nh = D // HEAD
    cb = ctx_rows // TBT

    def body(dx_ref, tmo_ref, pa_ref, pb_ref, o_ref, og_ref, ga_ref, gb_ref, mod_ref, hnw_ref, wa_ref, wb_ref, wo_ref,
             dout_ref, dpa_ref, dpb_ref, dgate_ref, dya_ref, do_ref, dg1_ref, dhnw_ref):
        i = pl.program_id(0)

        @pl.when(i == 0)
        def _():
            dhnw_ref[...] = jnp.zeros_like(dhnw_ref)

        @pl.when((i == 0) | (i == cb))
        def _():
            dg1_ref[...] = jnp.zeros_like(dg1_ref)

        dxv = dx_ref[...]
        dg1_ref[...] += jnp.sum(dxv * tmo_ref[...], axis=0, keepdims=True)
        dout = (dxv * mod_ref[2:3, :]).astype(BF16)
        dout_ref[...] = dout
        dmg = _nt(dout, _full(wo_ref))
        sa = _sigmoid(ga_ref[...])
        sb = _sigmoid(gb_ref[...])
        dpa = (dmg * sa).astype(BF16)
        dpb = (dmg * sb).astype(BF16)
        dpa_ref[...] = dpa
        dpb_ref[...] = dpb
        dgate_ref[:, D:2 * D] = (dmg * pa_ref[...] * sa * (1.0 - sa)).astype(BF16)
        dgate_ref[:, 2 * D:3 * D] = (dmg * pb_ref[...] * sb * (1.0 - sb)).astype(BF16)
        dya_ref[...] = _nt(dpa, _full(wa_ref))
        dyb = _nt(dpb, _full(wb_ref))
        so, dso = _silu_both(og_ref[...])
        ov = o_ref[...]
        nw = hnw_ref[...]
        dnw = jnp.zeros((1, HEAD), F32)
        for h in range(nh):
            sl = slice(h * HEAD, (h + 1) * HEAD)
            seg = ov[:, sl]
            r = lax.rsqrt(jnp.mean(seg * seg, axis=-1, keepdims=True) + RMS_EPS)
            oh = seg * r
            dn = dyb[:, sl] * so[:, sl]
            dgate_ref[:, sl] = (dyb[:, sl] * oh * nw * dso[:, sl]).astype(BF16)
            dnw = dnw + jnp.sum(dn * oh, axis=0, keepdims=True)
            doh = dn * nw
            do_ref[:, sl] = r * (doh - oh * jnp.mean(doh * oh, axis=-1, keepdims=True))
        dhnw_ref[...] += dnw

    row = lambda col: pl.BlockSpec((TBT, D), lambda i: (i, col))
    wsp = _rows_weight_spec(wa)
    sd = lambda dt: jax.ShapeDtypeStruct((T, D), dt)
    return pl.pallas_call(
        body, name=name, grid=(T // TBT,),
        in_specs=[row(0), row(0), row(0), row(0), row(0), row(6), row(7), row(8),
                  pl.BlockSpec((None, N_MOD, D), lambda i: (_stream_of(i, cb), 0, 0)),
                  pl.BlockSpec((1, HEAD), lambda i: (0, 0)), wsp, wsp, wsp],
        out_specs=[row(0)] * 3 + [pl.BlockSpec((TBT, 3 * D), lambda i: (i, 2)), row(0), row(0),
                                  pl.BlockSpec((None, 1, D), lambda i: (_stream_of(i, cb), 0, 0)),
                                  pl.BlockSpec((1, HEAD), lambda i: (0, 0))],
        out_shape=[sd(BF16)] * 3 + [jax.ShapeDtypeStruct((T, 9 * D), BF16), sd(F32), sd(F32),
                                    jax.ShapeDtypeStruct((2, 1, D), F32), jax.ShapeDtypeStruct((1, HEAD), F32)],
        compiler_params=_params("arbitrary", vmem=VMEM_LIMIT_TOKEN_OUT),
    )(dx, tmo, pa, pb, o, parts, parts, parts, mod, hnw, wa, wb, wo)


def _conv_geometry(i, nb, cb):
    is_ctx = i < cb
    first = (i == 0) | (i == cb)
    last = (i == cb - 1) | (i == nb - 1)
    row = lax.broadcasted_iota(jnp.int32, (TB + 2 * GRID_W, 1), 0)
    w = row & (GRID_W - 1)
    left_ok = (w != 0) | is_ctx
    right_ok = (w != GRID_W - 1) | is_ctx
    return is_ctx, first, last, left_ok, right_ok


def _ext(p_ref, m_ref, n_ref, first, last):
    return jnp.concatenate([jnp.where(first, 0.0, p_ref[...]), m_ref[...], jnp.where(last, 0.0, n_ref[...])], axis=0)


def _shift_prev(e, ok):
    return jnp.where(ok, pltpu.roll(e, 1, 0), 0.0)


def _shift_next(e, ok):
    return jnp.where(ok, pltpu.roll(e, e.shape[0] - 1, 0), 0.0)


def _halo_specs(cbk, n64, coff=0):
    r = TB // GRID_W
    prev = pl.BlockSpec((GRID_W, cbk), lambda j, i: (jnp.maximum(r * i - 1, 0), j + coff))
    main = pl.BlockSpec((TB, cbk), lambda j, i: (i, j + coff))
    nxt = pl.BlockSpec((GRID_W, cbk), lambda j, i: (jnp.minimum(r * i + r, n64 - 1), j + coff))
    return [prev, main, nxt]


def _conv_cblock(dff):
    return _tile(dff, 1408)


def _conv_fwd(up, cw, cbias, ctx_rows, name):
    T, dff = up.shape[0], up.shape[1] // 2
    cbk = _conv_cblock(dff)
    nb, cb = T // TB, ctx_rows // TB
    nvb = dff // cbk

    def body(ap_ref, a_ref, an_ref, v_ref, cw_ref, cb_ref, ac_ref, act_ref):
        i = pl.program_id(1)
        is_ctx, first, last, lok, rok = _conv_geometry(i, nb, cb)
        e = _ext(ap_ref, a_ref, an_ref, first, last)
        el = _shift_prev(e, lok)
        er = _shift_next(e, rok)
        cwv = cw_ref[...]

        def comb(dr, lo):
            sl = slice(lo, lo + TB)
            return cwv[3 * dr:3 * dr + 1] * el[sl] + cwv[3 * dr + 1:3 * dr + 2] * e[sl] + cwv[3 * dr + 2:3 * dr + 3] * er[sl]

        out = comb(1, GRID_W) + jnp.where(is_ctx, 0.0, comb(0, 0) + comb(2, 2 * GRID_W))
        a_c = out + cb_ref[...]
        ac_ref[...] = a_c
        act_ref[...] = (_gelu(a_c) * v_ref[...]).astype(BF16)

    main = pl.BlockSpec((TB, cbk), lambda j, i: (i, j))
    return pl.pallas_call(
        body, name=name, grid=(dff // cbk, nb),
        in_specs=_halo_specs(cbk, T // GRID_W) + [pl.BlockSpec((TB, cbk), lambda j, i: (i, j + nvb)),
                                                 pl.BlockSpec((9, cbk), lambda j, i: (0, j)),
                                                 pl.BlockSpec((1, cbk), lambda j, i: (0, j))],
        out_specs=[main, main],
        out_shape=[jax.ShapeDtypeStruct((T, dff), F32), jax.ShapeDtypeStruct((T, dff), BF16)],
        compiler_params=_params("parallel", "parallel"),
    )(up, up, up, up, cw, cbias)


def _conv_bwd(up, ac, dact, cw, ctx_rows, name):
    T, dff = up.shape[0], up.shape[1] // 2
    cbk = _conv_cblock(dff)
    nb, cb = T // TB, ctx_rows // TB
    nvb = dff // cbk

    def body(ap_ref, a_ref, an_ref, vp_ref, v_ref, vn_ref, cp_ref, c_ref, cn_ref, dp_ref, d_ref, dn_ref, cw_ref,
             da_ref, dv_ref, dcw_ref, dcb_ref):
        i = pl.program_id(1)

        @pl.when(i == 0)
        def _():
            dcw_ref[...] = jnp.zeros_like(dcw_ref)
            dcb_ref[...] = jnp.zeros_like(dcb_ref)

        is_ctx, first, last, lok, rok = _conv_geometry(i, nb, cb)
        gl, dgl = _gelu_both(_ext(cp_ref, c_ref, cn_ref, first, last))
        g = _ext(dp_ref, d_ref, dn_ref, first, last) * _ext(vp_ref, v_ref, vn_ref, first, last) * dgl
        dv_ref[...] = (d_ref[...] * gl[GRID_W:GRID_W + TB]).astype(BF16)
        gm = _shift_prev(g, lok)
        gp = _shift_next(g, rok)
        cwv = cw_ref[...]

        def comb(dr, lo):
            sl = slice(lo, lo + TB)
            return cwv[3 * dr:3 * dr + 1] * gp[sl] + cwv[3 * dr + 1:3 * dr + 2] * g[sl] + cwv[3 * dr + 2:3 * dr + 3] * gm[sl]

        da = comb(1, GRID_W) + jnp.where(is_ctx, 0.0, comb(0, 2 * GRID_W) + comb(2, 0))
        da_ref[...] = da.astype(BF16)
        e = _ext(ap_ref, a_ref, an_ref, first, last)
        taps = [_shift_prev(e, lok), e, _shift_next(e, rok)]
        gmain = g[GRID_W:GRID_W + TB]
        dcb_ref[...] += jnp.sum(gmain, axis=0, keepdims=True)
        vert = jnp.where(is_ctx, 0.0, 1.0)
        for dr in range(3):
            sl = slice(dr * GRID_W, dr * GRID_W + TB)
            for dw in range(3):
                s = jnp.sum(gmain * taps[dw][sl], axis=0, keepdims=True)
                if dr != 1:
                    s = s * vert
                k = 3 * dr + dw
                dcw_ref[k:k + 1, :] += s

    main = pl.BlockSpec((TB, cbk), lambda j, i: (i, j))
    halo = _halo_specs(cbk, T // GRID_W)
    acc9 = pl.BlockSpec((9, cbk), lambda j, i: (0, j))
    acc1 = pl.BlockSpec((1, cbk), lambda j, i: (0, j))
    return pl.pallas_call(
        body, name=name, grid=(dff // cbk, nb),
        in_specs=halo + _halo_specs(cbk, T // GRID_W, nvb) + halo + halo + [acc9],
        out_specs=[main, main, acc9, acc1],
        out_shape=[jax.ShapeDtypeStruct((T, dff), BF16), jax.ShapeDtypeStruct((T, dff), BF16),
                   jax.ShapeDtypeStruct((9, dff), F32), jax.ShapeDtypeStruct((1, dff), F32)],
        compiler_params=_params("parallel", "arbitrary"),
    )(up, up, up, up, up, up, ac, ac, ac, dact, dact, dact, cw)


def _ffn_out_fwd(act, xm, mod, wd, ctx_rows, name):
    T, D = xm.shape
    dff = act.shape[1]
    cb = ctx_rows // TB

    def body(act_ref, x_ref, mod_ref, w_ref, xo_ref, fo_ref):
        out = _nn(act_ref[...], _full(w_ref))
        fo_ref[...] = out
        xo_ref[...] = x_ref[...] + mod_ref[5:6, :] * out

    row = pl.BlockSpec((TB, D), lambda i: (i, 0))
    return pl.pallas_call(
        body, name=name, grid=(T // TB,),
        in_specs=[pl.BlockSpec((TB, dff), lambda i: (i, 0)), row,
                  pl.BlockSpec((None, N_MOD, D), lambda i: (_stream_of(i, cb), 0, 0)),
                  _rows_weight_spec(wd)],
        out_specs=[row, row],
        out_shape=[jax.ShapeDtypeStruct((T, D), F32), jax.ShapeDtypeStruct((T, D), F32)],
        compiler_params=_params("parallel"),
    )(act, xm, mod, wd)


def _ffn_out_bwd(dx, fo, mod, wd, ctx_rows, name):
    T, D = dx.shape
    dff = N_CHIPS * wd.shape[1]
    cb = ctx_rows // TB

    def body(dx_ref, fo_ref, mod_ref, w_ref, dout_ref, dact_ref, dg2_ref):
        i = pl.program_id(0)

        @pl.when((i == 0) | (i == cb))
        def _():
            dg2_ref[...] = jnp.zeros_like(dg2_ref)

        dxv = dx_ref[...]
        dg2_ref[...] += jnp.sum(dxv * fo_ref[...], axis=0, keepdims=True)
        dout = (dxv * mod_ref[5:6, :]).astype(BF16)
        dout_ref[...] = dout
        dact_ref[...] = _nt(dout, _full(w_ref))

    row = pl.BlockSpec((TB, D), lambda i: (i, 0))
    return pl.pallas_call(
        body, name=name, grid=(T // TB,),
        in_specs=[row, row, pl.BlockSpec((None, N_MOD, D), lambda i: (_stream_of(i, cb), 0, 0)),
                  _rows_weight_spec(wd)],
        out_specs=[row, pl.BlockSpec((TB, dff), lambda i: (i, 0)),
                   pl.BlockSpec((None, 1, D), lambda i: (_stream_of(i, cb), 0, 0))],
        out_shape=[jax.ShapeDtypeStruct((T, D), BF16), jax.ShapeDtypeStruct((T, dff), F32),
                   jax.ShapeDtypeStruct((2, 1, D), F32)],
        compiler_params=_params("arbitrary"),
    )(dx, fo, mod, wd)


def _loss_bwd(x, target, fw, ctx_rows, name):
    T, D = x.shape
    cb = ctx_rows // TB

    def body(x_ref, t_ref, fw_ref, dx_ref, loss_ref, dfw_ref):
        i = pl.program_id(0)

        @pl.when(i == 0)
        def _():
            loss_ref[...] = jnp.zeros_like(loss_ref)
            dfw_ref[...] = jnp.zeros_like(dfw_ref)

        @pl.when(i < cb)
        def _():
            dx_ref[...] = jnp.zeros_like(dx_ref)

        @pl.when(i >= cb)
        def _():
            xv = x_ref[...]
            r = lax.rsqrt(jnp.mean(xv * xv, axis=-1, keepdims=True) + RMS_EPS)
            xh = xv * r
            fwv = fw_ref[...]
            err = xh * fwv - t_ref[...]
            loss_ref[...] += (0.5 / D) * jnp.sum(err * err)
            dy = err * (1.0 / D)
            dfw_ref[...] += jnp.sum(dy * xh, axis=0, keepdims=True)
            dxh = dy * fwv
            dx_ref[...] = r * (dxh - xh * jnp.mean(dxh * xh, axis=-1, keepdims=True))

    row = pl.BlockSpec((TB, D), lambda i: (i, 0))
    return pl.pallas_call(
        body, name=name, grid=(T // TB,),
        in_specs=[row, pl.BlockSpec((TB, D), lambda i: (jnp.maximum(i - cb, 0), 0)), pl.BlockSpec((1, D), lambda i: (0, 0))],
        out_specs=[row, pl.BlockSpec((1, 128), lambda i: (0, 0)), pl.BlockSpec((1, D), lambda i: (0, 0))],
        out_shape=[jax.ShapeDtypeStruct((T, D), F32), jax.ShapeDtypeStruct((1, 128), F32),
                   jax.ShapeDtypeStruct((1, D), F32)],
        compiler_params=_params("arbitrary"),
    )(x, target, fw)


def _adamw(w, gs, m, v, name):
    L, R, C = w.shape
    assert len(gs) == L
    rb = _rows_tile(R, max(16, (1 << 18) // C // 16 * 16))
    bc1 = 1.0 - ADAM_B1 ** ADAM_STEP
    bc2 = 1.0 - ADAM_B2 ** ADAM_STEP

    def body(w_ref, m_ref, v_ref, *rest):
        g_refs, (g_ref, d_ref, nm_ref, nv_ref) = rest[:L], rest[L:]
        layer = pl.program_id(0)
        for li in range(L):
            @pl.when(layer == li)
            def _():
                gv = g_refs[li][...]
                g_ref[...] = gv
                nm = ADAM_B1 * m_ref[...] + (1.0 - ADAM_B1) * gv
                nv = ADAM_B2 * v_ref[...] + (1.0 - ADAM_B2) * (gv * gv)
                nm_ref[...] = nm
                nv_ref[...] = nv
                d_ref[...] = -ADAM_LR * ((nm / bc1) / (jnp.sqrt(nv / bc2) + ADAM_EPS) + ADAM_WD * w_ref[...])

    blk = pl.BlockSpec((None, rb, C), lambda l, i: (l, i, 0))
    gblk = pl.BlockSpec((rb, C), lambda l, i: (i, 0))
    sd = jax.ShapeDtypeStruct((L, R, C), F32)
    return pl.pallas_call(
        body, name=name, grid=(L, R // rb), in_specs=[blk] * 3 + [gblk] * L, out_specs=[blk] * 4, out_shape=[sd] * 4,
        compiler_params=_params("parallel", "parallel"),
    )(w, m, v, *gs)


def _local_step(xs, cv, target, W, layer_weights, on_layer_grads, ctx_rows):
    T, D = xs.shape
    depth = W["norm1_w"].shape[0]
    saved = []
    X = xs
    for l in range(depth):
        s = {}
        Wl = layer_weights(l, X)
        mod_all, sa = _mod_fwd(cv, Wl["ada_w"], W["ada_b"][l][None, :] + Wl["token"], f"mod_fwd_{l}")
        mod = mod_all[:2].reshape(2, N_MOD, D)
        h1 = _norm_mod(X, W["norm1_w"][l][None, :], mod, 0, ctx_rows, f"norm1_{l}")
        parts = _mm_nn_w(h1, Wl["w_in"], F32, f"in_proj_{l}")
        o_f, st_f = _hgrn_fwd(parts, W["hlb"], l, False, ctx_rows, f"hgrn_fwd_f_{l}")
        o, st_b = _hgrn_fwd(parts, W["hlb"], l, True, ctx_rows, f"hgrn_fwd_b_{l}", o_add=o_f)
        ya = _sgu_fwd(parts, W["sgu_ln_w"][l][None, :], W["sgu_ln_b"][l][None, :], W["sgu_w"][l], W["sgu_bt"][l],
                      f"sgu_fwd_{l}")
        Wl.update(Wl.pop("late")(ya))
        yb, pa, pb, mg, tmo, xm = _token_out_fwd(o, parts, ya, X, mod, W["hnw"][l][None, :] + Wl["late_token"], Wl["w_a"],
                                                 Wl["w_b"], Wl["w_o"], ctx_rows, f"token_out_fwd_{l}")
        h2 = _norm_mod(xm, W["norm2_w"][l][None, :], mod, 3, ctx_rows, f"norm2_{l}")
        up = _mm_nn_w(h2, Wl["w_up"], F32, f"up_proj_{l}")
        ac, act = _conv_fwd(up, Wl["conv_w"], W["conv_b"][l][None, :], ctx_rows, f"conv_fwd_{l}")
        xo, fo = _ffn_out_fwd(act, xm, mod, Wl["w_down"], ctx_rows, f"ffn_out_fwd_{l}")
        s.update(X=X, Wl=Wl, mod=mod, mod_all=mod_all, sa=sa, h1=h1, parts=parts, o=o, st_f=st_f, st_b=st_b, ya=ya, yb=yb,
                 pa=pa, pb=pb, mg=mg, tmo=tmo, xm=xm, h2=h2, up=up, ac=ac, act=act, fo=fo)
        saved.append(s)
        X = xo

    dX, loss_row, dfw = _loss_bwd(X, target, W["final_norm_w"][None, :], ctx_rows, "loss_bwd")
    G = {k: [None] * depth for k in ("ada_b", "norm1_w", "sgu_ln_w", "sgu_ln_b", "sgu_w", "sgu_b", "hlb1", "hnw", "norm2_w",
                                     "conv_w", "conv_b", "dmod")}
    dcv = jnp.zeros_like(cv)
    for l in reversed(range(depth)):
        s = saved[l]
        mod, Wl = s["mod"], s["Wl"]
        big = {}
        dout2, dact, dg2 = _ffn_out_bwd(dX, s["fo"], mod, Wl["w_down"], ctx_rows, f"ffn_out_bwd_{l}")
        big["w_down"] = _mm_tn(s["act"], dout2, F32, f"dw_down_{l}")
        da, dv, dcw, dcb = _conv_bwd(s["up"], s["ac"], dact, Wl["conv_w"], ctx_rows, f"conv_bwd_{l}")
        G["conv_w"][l], G["conv_b"][l] = dcw, dcb[0]
        dup = jnp.concatenate([da, dv], axis=1)
        big["w_up"] = _mm_tn(s["h2"], dup, F32, f"dw_up_{l}", out_chips=True)
        dh2 = _mm_nt_w(dup, Wl["w_up"], F32, f"dh2_{l}")
        dxm, dm2, dnw2 = _norm_mod_bwd(dh2, s["xm"], dX, W["norm2_w"][l][None, :], mod, 3, ctx_rows, f"norm2_bwd_{l}")
        G["norm2_w"][l] = dnw2[0]
        (dout1, dpa, dpb, dparts, dya, do, dg1, dhnw) = _token_out_bwd(
            dxm, s["tmo"], s["pa"], s["pb"], s["o"], s["parts"], mod, W["hnw"][l][None, :], Wl["w_a"], Wl["w_b"], Wl["w_o"],
            ctx_rows, f"token_out_bwd_{l}")
        G["hnw"][l] = dhnw[0]
        big["w_o"] = _mm_tn(s["mg"], dout1, F32, f"dw_o_{l}")
        big["w_a"] = _mm_tn(s["ya"], dpa, F32, f"dw_a_{l}")
        big["w_b"] = _mm_tn(s["yb"], dpb, F32, f"dw_b_{l}")
        tok = on_layer_grads(l, "early", big)
        dparts, dsw, dsbt, dlnw, dlnb = _sgu_bwd(s["parts"], dya, W["sgu_ln_w"][l][None, :], W["sgu_ln_b"][l][None, :] + tok,
                                                 W["sgu_w"][l], W["sgu_bt"][l], dparts, f"sgu_bwd_{l}")
        G["sgu_w"][l], G["sgu_b"][l], G["sgu_ln_w"][l], G["sgu_ln_b"][l] = dsw, dsbt.T, dlnw[0], dlnb[0]
        dq_f, dz_f, di_f, dlb_f = _hgrn_bwd(s["parts"], W["hlb"], do, s["st_f"], l, False, ctx_rows, f"hgrn_bwd_f_{l}")
        dparts, dlb_b = _hgrn_bwd(s["parts"], W["hlb"], do, s["st_b"], l, True, ctx_rows, f"hgrn_bwd_b_{l}",
                                  other=(dq_f, dz_f, di_f), dparts=dparts)
        G["hlb1"][l] = jnp.concatenate([dlb_f[0], dlb_b[0]])
        tok = on_layer_grads(l, "late", {"w_in": _mm_tn(s["h1"], dparts, F32, f"dw_in_{l}", out_chips=True)})
        dh1 = _mm_nt_w(dparts, Wl["w_in"], F32, f"dh1_{l}")
        tok = tok + on_layer_grads(l, "end", {"after": dh1})
        dX, dm1, dnw1 = _norm_mod_bwd(dh1, s["X"], dxm, W["norm1_w"][l][None, :] + tok, mod, 0, ctx_rows, f"norm1_bwd_{l}")
        G["norm1_w"][l] = dnw1[0]
        dmod = jnp.concatenate([dm1, dg1, dm2, dg2], axis=1).reshape(2, N_MOD * D)
        dmod16 = jnp.concatenate([dmod, jnp.zeros((cv.shape[0] - 2, N_MOD * D), F32)], axis=0)
        G["ada_b"][l] = dmod[0] + dmod[1]
        G["dmod"][l] = dmod
        dcv = dcv + _cvec_bwd(dmod16, Wl["ada_w"], cv, f"dcvec_{l}")
    G["c_ctx"] = dcv[0]
    G["final_norm_w"] = dfw[0]
    return loss_row[0, 0], dX, G, saved[0]["sa"]


def _chip_peers(x, y, c):
    return [((1 - x, y, c), 2 * (1 - x) + y), ((x, 1 - y, c), 2 * x + 1 - y), ((1 - x, 1 - y, c), 2 * (1 - x) + 1 - y)]


def _rdma_call(ins, out_shapes, plan, n_remote, n_local, name, aliases=None):
    n_in, n_out = len(ins), len(out_shapes)

    def body(*refs):
        in_refs, out_refs = refs[:n_in], refs[n_in:n_in + n_out]
        send_sems, recv_sems, local_sems = refs[n_in + n_out:]
        x, y, c = lax.axis_index("x"), lax.axis_index("y"), lax.axis_index("c")
        remote, local = plan(in_refs, out_refs, x, y, c)
        assert len(remote) == n_remote and len(local) == n_local, (name, len(remote), len(local))
        copies = [pltpu.make_async_copy(s, d, local_sems.at[i]) for i, (s, d) in enumerate(local)]
        copies += [pltpu.make_async_remote_copy(src_ref=s, dst_ref=d, send_sem=send_sems.at[k], recv_sem=recv_sems.at[k],
                                                device_id=dev, device_id_type=pl.DeviceIdType.MESH)
                   for k, (s, d, dev) in enumerate(remote)]
        for cp in copies:
            cp.start()
        for cp in copies:
            cp.wait()

    hbm = pl.BlockSpec(memory_space=pltpu.HBM)
    return pl.pallas_call(
        body, name=name, in_specs=[hbm] * n_in, out_specs=[hbm] * n_out, out_shape=out_shapes,
        scratch_shapes=[pltpu.SemaphoreType.DMA((n_remote,)), pltpu.SemaphoreType.DMA((n_remote,)),
                        pltpu.SemaphoreType.DMA((max(n_local, 1),))],
        input_output_aliases=aliases or {},
    )(*ins)


DMA_PIECE_BYTES = 1 << 18
DMA_MAX_PIECES = 8


def _row_pieces(shape, dtype):
    rows = shape[0]
    row_bytes = jnp.dtype(dtype).itemsize
    for d in shape[1:]:
        row_bytes *= d
    n = 1
    while n < DMA_MAX_PIECES and rows % (2 * n * 16) == 0 and rows * row_bytes // (2 * n) >= DMA_PIECE_BYTES:
        n *= 2
    return [(i * (rows // n), rows // n) for i in range(n)]


def _half_pieces(o, c):
    r2 = o.shape[1] // 2
    return [pl.ds(c * r2 + st, sz) for st, sz in _row_pieces((r2,) + o.shape[2:], o.dtype)]


def _n_half_pieces(arrays):
    return sum(len(_row_pieces((a.shape[1] // 2,) + a.shape[2:], a.dtype)) for a in arrays)


def _plan_gather_far(lands, x, y, c):
    me = 2 * x + y
    return [(o.at[me, rows], o.at[me, rows], dev) for dev, _ in _chip_peers(x, y, c) for o in lands
            for rows in _half_pieces(o, c)]


def _plan_gather_near(lands, x, y, c):
    return [(o.at[idx, rows], o.at[idx, rows], (x, y, 1 - c)) for _, idx in _chip_peers(x, y, c) for o in lands
            for rows in _half_pieces(o, c)]


def _gather_weights(lands, name):
    n = len(lands)
    n_far = (N_CHIPS - 1) * _n_half_pieces(lands)

    def body(*refs):
        outs = refs[n:2 * n]
        far_send, far_recv, near_send, near_recv = refs[2 * n:]
        x, y, c = lax.axis_index("x"), lax.axis_index("y"), lax.axis_index("c")
        mk = lambda plan, send, recv: [
            pltpu.make_async_remote_copy(src_ref=s, dst_ref=d, send_sem=send.at[k], recv_sem=recv.at[k], device_id=dev,
                                         device_id_type=pl.DeviceIdType.MESH)
            for k, (s, d, dev) in enumerate(plan(outs, x, y, c))]
        far, near = mk(_plan_gather_far, far_send, far_recv), mk(_plan_gather_near, near_send, near_recv)
        assert len(far) == n_far and len(near) == n_far
        for cp in far:
            cp.start()
        for k in range(n_far):
            far[k].wait_recv()
            near[k].start()
        for k in range(n_far):
            near[k].wait_recv()
        for cp in far + near:
            cp.wait_send()

    hbm = pl.BlockSpec(memory_space=pltpu.HBM)
    sems = pltpu.SemaphoreType.DMA((n_far,))
    return pl.pallas_call(
        body, name=name, in_specs=[hbm] * n, out_specs=[hbm] * n,
        out_shape=[jax.ShapeDtypeStruct(a.shape, a.dtype) for a in lands],
        scratch_shapes=[sems, sems, sems, sems], input_output_aliases={i: i for i in range(n)},
    )(*lands)


def _gather_all(v, name):
    def plan(ins, outs, x, y, c):
        (s,), (o,) = ins, outs
        me = 4 * x + 2 * y + c
        flip = lambda a, f: 1 - a if f else a
        remote = [(s, o.at[me], (flip(x, m & 4), flip(y, m & 2), flip(c, m & 1))) for m in range(1, 8)]
        return remote, [(s, o.at[me])]

    return _rdma_call([v], [jax.ShapeDtypeStruct((8,) + v.shape, v.dtype)], plan, 7, 1, name)[0]


def _plan_pair(ins, lands, x, y, c):
    return [(a.at[j, 1 - c, pl.ds(st, sz)], o.at[j, pl.ds(st, sz)], (x, y, 1 - c)) for a, o in zip(ins, lands)
            for j in range(N_CHIPS) for st, sz in _row_pieces(a.shape[2:], a.dtype)]


def _n_pair_copies(parts):
    return N_CHIPS * sum(len(_row_pieces(a.shape[2:], a.dtype)) for a in parts)


def _reduce_pair(parts, name):
    shapes = [jax.ShapeDtypeStruct((N_CHIPS,) + a.shape[2:], a.dtype) for a in parts]
    return _rdma_call(parts, shapes, lambda ins, outs, x, y, c: (_plan_pair(ins, outs, x, y, c), []),
                      _n_pair_copies(parts), 0, name)


def _plan_chips(ins, lands, x, y, c):
    me = 2 * x + y
    return [(a.at[idx, pl.ds(st, sz)], o.at[me, pl.ds(st, sz)], dev) for dev, idx in _chip_peers(x, y, c)
            for a, o in zip(ins, lands) for st, sz in _row_pieces(a.shape[1:], a.dtype)]


def _n_chips_copies(parts):
    return (N_CHIPS - 1) * sum(len(_row_pieces(a.shape[1:], a.dtype)) for a in parts)


def _reduce_chips(parts, name):
    shapes = [jax.ShapeDtypeStruct(a.shape, a.dtype) for a in parts]
    return _rdma_call(parts, shapes, lambda ins, outs, x, y, c: (_plan_chips(ins, outs, x, y, c), []),
                      _n_chips_copies(parts), 0, name)


def _gather_pair(halves, name):
    def plan(ins, outs, x, y, c):
        return [(o.at[c, pl.ds(st, sz)], o.at[c, pl.ds(st, sz)], (x, y, 1 - c)) for o in outs
                for st, sz in _row_pieces(o.shape[1:], o.dtype)], []

    shapes = [jax.ShapeDtypeStruct(a.shape, a.dtype) for a in halves]
    n_remote = sum(len(_row_pieces(a.shape[1:], a.dtype)) for a in halves)
    return _rdma_call(halves, shapes, plan, n_remote, 0, name, aliases={i: i for i in range(len(halves))})


def _split_start(ins, lands, plan, n_remote, name):
    n_buf = len(ins) + len(lands)

    def body(*refs):
        in_refs, land_refs = refs[:len(ins)], refs[len(ins):n_buf]
        send_sems, recv_sems, token = refs[n_buf], refs[n_buf + 1], refs[-1]
        x, y, c = lax.axis_index("x"), lax.axis_index("y"), lax.axis_index("c")
        remote = plan(in_refs, land_refs, x, y, c)
        assert len(remote) == n_remote, (name, len(remote))
        for k, (s, d, dev) in enumerate(remote):
            pltpu.make_async_remote_copy(src_ref=s, dst_ref=d, send_sem=send_sems.at[k], recv_sem=recv_sems.at[k],
                                         device_id=dev, device_id_type=pl.DeviceIdType.MESH).start()
        token[...] = jnp.zeros_like(token)

    hbm = pl.BlockSpec(memory_space=pltpu.HBM)
    sem = pl.BlockSpec(memory_space=pltpu.SEMAPHORE)
    bufs = list(ins) + list(lands)
    out = pl.pallas_call(
        body, name=name, in_specs=[hbm] * n_buf,
        out_specs=(sem, sem) + (hbm,) * n_buf + (pl.BlockSpec(memory_space=pltpu.VMEM),),
        out_shape=(pltpu.SemaphoreType.DMA((n_remote,)), pltpu.SemaphoreType.DMA((n_remote,)))
        + tuple(pltpu.HBM(a.shape, a.dtype) for a in bufs) + (jax.ShapeDtypeStruct((8, 128), F32),),
        input_output_aliases={i: 2 + i for i in range(n_buf)},
        compiler_params=pltpu.CompilerParams(has_side_effects=pltpu.SideEffectType.DATAFLOW_SIDE_EFFECTING),
    )(*[pltpu.with_memory_space_constraint(a, pltpu.HBM) for a in bufs])
    return dict(send=out[0], recv=out[1], ins=list(out[2:2 + len(ins)]), lands=list(out[2 + len(ins):2 + n_buf]),
                token=out[-1][0, 0], token_array=out[-1], plan=plan, n_remote=n_remote)


def _split_wait(st, after, name):
    n_in, n_buf = len(st["ins"]), len(st["ins"]) + len(st["lands"])
    plan, n_remote = st["plan"], st["n_remote"]

    def body(*refs):
        in_refs, land_refs = refs[:n_in], refs[n_in:n_buf]
        send_sems, recv_sems = refs[n_buf], refs[n_buf + 1]
        x, y, c = lax.axis_index("x"), lax.axis_index("y"), lax.axis_index("c")
        for k, (s, d, dev) in enumerate(plan(in_refs, land_refs, x, y, c)):
            cp = pltpu.make_async_remote_copy(src_ref=s, dst_ref=d, send_sem=send_sems.at[k], recv_sem=recv_sems.at[k],
                                              device_id=dev, device_id_type=pl.DeviceIdType.MESH)
            cp.wait_send()
            cp.wait_recv()

    hbm = pl.BlockSpec(memory_space=pltpu.HBM)
    sem = pl.BlockSpec(memory_space=pltpu.SEMAPHORE)
    bufs = st["ins"] + st["lands"]
    out = pl.pallas_call(
        body, name=name, in_specs=[hbm] * n_buf + [sem, sem, pl.BlockSpec(memory_space=pl.ANY)],
        out_specs=[hbm] * n_buf, out_shape=[pltpu.HBM(a.shape, a.dtype) for a in bufs],
        input_output_aliases={i: i for i in range(n_buf)},
        compiler_params=pltpu.CompilerParams(has_side_effects=pltpu.SideEffectType.DATAFLOW_SIDE_EFFECTING),
    )(*bufs, st["send"], st["recv"], after)
    return list(out[:n_in]), list(out[n_in:])


def _pair_forward(lands, name):
    shapes = [jax.ShapeDtypeStruct(a.shape, a.dtype) for a in lands]
    return _rdma_call(lands, shapes, lambda ins, outs, x, y, c: (_plan_gather_near(outs, x, y, c), []),
                      (N_CHIPS - 1) * _n_half_pieces(lands), 0, name, aliases={i: i for i in range(len(lands))})


def _sum_block_rows(r, C):
    return _rows_tile(r, max(16, (1 << 18) // C // 16 * 16))


def _sum_pair(a, recv, cidx, name):
    nch, _, r, C = a.shape
    rb = _sum_block_rows(r, C)

    def body(c_ref, a_ref, r_ref, o_ref):
        o_ref[...] = (a_ref[...] + r_ref[...]).astype(BF16)

    blk = pl.BlockSpec((None, rb, C), lambda j, i, c: (j, i, 0))
    return pl.pallas_call(
        body, name=name,
        grid_spec=pltpu.PrefetchScalarGridSpec(
            num_scalar_prefetch=1, grid=(nch, r // rb),
            in_specs=[pl.BlockSpec((None, None, rb, C), lambda j, i, c: (j, c[0], i, 0)), blk], out_specs=blk),
        out_shape=jax.ShapeDtypeStruct((nch, r, C), BF16),
        compiler_params=_params("parallel", "parallel"),
    )(cidx, a, recv)


def _sum_chips(mine, recv, ids, name):
    nch, r, C = recv.shape
    rb = _sum_block_rows(r, C)

    def body(ids_ref, m_ref, *rest):
        r_refs, o_ref = rest[:nch], rest[nch]
        chip = ids_ref[1]
        own = m_ref[...].astype(F32)
        acc = jnp.where(chip == 0, own, r_refs[0][...].astype(F32))
        for q in range(1, nch):
            acc = acc + jnp.where(chip == q, own, r_refs[q][...].astype(F32))
        o_ref[...] = acc

    def slot(q):
        return pl.BlockSpec((None, rb, C), lambda i, ids: (jnp.where(ids[1] == q, (q + 1) % nch, q), i, 0))

    return pl.pallas_call(
        body, name=name,
        grid_spec=pltpu.PrefetchScalarGridSpec(
            num_scalar_prefetch=1, grid=(r // rb,),
            in_specs=[pl.BlockSpec((None, rb, C), lambda i, ids: (ids[1], i, 0))] + [slot(q) for q in range(nch)],
            out_specs=pl.BlockSpec((None, rb, C), lambda i, ids: (ids[0], i, 0))),
        out_shape=jax.ShapeDtypeStruct((N_CORES, r, C), F32),
        compiler_params=_params("parallel"),
    )(ids, mine, *([recv] * nch))


PACK_COLS = 1024
_SHARDED = ("ada_w", "w_in", "w_branch_a", "w_branch_b", "w_out", "ffn_w_up", "ffn_w_down")
_LAYER_KEYS = ("ada_w", "w_in", "w_a", "w_b", "w_o", "w_up", "w_down")
_SMALL = ("c_ctx", "ada_b", "norm1_w", "sgu_ln_w", "sgu_ln_b", "sgu_w", "sgu_b", "hgrn_lower_bounds", "hgrn_norm_w",
          "norm2_w", "ffn_conv_b", "final_norm_w")
_ORDER = ("c_ctx", "ada_w", "ada_b", "norm1_w", "w_in", "sgu_ln_w", "sgu_ln_b", "sgu_w", "sgu_b", "hgrn_lower_bounds",
          "hgrn_norm_w", "w_branch_a", "w_branch_b", "w_out", "norm2_w", "ffn_w_up", "ffn_conv_w", "ffn_conv_b",
          "ffn_w_down", "final_norm_w")


def _pad_to(v, n):
    return jnp.concatenate([v, jnp.zeros((n - v.shape[0],), v.dtype)]) if v.shape[0] < n else v


def _round_up(n, m):
    return (n + m - 1) // m * m


def _pack(arrays, n_pad):
    flat = jnp.concatenate([a.reshape(-1) for a in arrays])
    return _pad_to(flat, n_pad)


def _unpack(flat, like):
    out, off = [], 0
    for a in like:
        out.append(flat[off:off + a.size].reshape(a.shape))
        off += a.size
    return out


def kernel(x, c, ctx, c_ctx, ada_w, ada_b, norm1_w, w_in, sgu_ln_w, sgu_ln_b, sgu_w, sgu_b, hgrn_lower_bounds, hgrn_norm_w, w_branch_a, w_branch_b, w_out, norm2_w, ffn_w_up, ffn_conv_w, ffn_conv_b, ffn_w_down, final_norm_w, loss_target, m_c_ctx, m_ada_w, m_ada_b, m_norm1_w, m_w_in, m_sgu_ln_w, m_sgu_ln_b, m_sgu_w, m_sgu_b, m_hgrn_lower_bounds, m_hgrn_norm_w, m_w_branch_a, m_w_branch_b, m_w_out, m_norm2_w, m_ffn_w_up, m_ffn_conv_w, m_ffn_conv_b, m_ffn_w_down, m_final_norm_w, v_c_ctx, v_ada_w, v_ada_b, v_norm1_w, v_w_in, v_sgu_ln_w, v_sgu_ln_b, v_sgu_w, v_sgu_b, v_hgrn_lower_bounds, v_hgrn_norm_w, v_w_branch_a, v_w_branch_b, v_w_out, v_norm2_w, v_ffn_w_up, v_ffn_conv_w, v_ffn_conv_b, v_ffn_w_down, v_final_norm_w):
    w = dict(c_ctx=c_ctx, ada_w=ada_w, ada_b=ada_b, norm1_w=norm1_w, w_in=w_in, sgu_ln_w=sgu_ln_w, sgu_ln_b=sgu_ln_b,
             sgu_w=sgu_w, sgu_b=sgu_b, hgrn_lower_bounds=hgrn_lower_bounds, hgrn_norm_w=hgrn_norm_w, w_branch_a=w_branch_a,
             w_branch_b=w_branch_b, w_out=w_out, norm2_w=norm2_w, ffn_w_up=ffn_w_up, ffn_conv_w=ffn_conv_w,
             ffn_conv_b=ffn_conv_b, ffn_w_down=ffn_w_down, final_norm_w=final_norm_w)
    mom = dict(zip(_ORDER, (m_c_ctx, m_ada_w, m_ada_b, m_norm1_w, m_w_in, m_sgu_ln_w, m_sgu_ln_b, m_sgu_w, m_sgu_b,
                            m_hgrn_lower_bounds, m_hgrn_norm_w, m_w_branch_a, m_w_branch_b, m_w_out, m_norm2_w, m_ffn_w_up,
                            m_ffn_conv_w, m_ffn_conv_b, m_ffn_w_down, m_final_norm_w)))
    var = dict(zip(_ORDER, (v_c_ctx, v_ada_w, v_ada_b, v_norm1_w, v_w_in, v_sgu_ln_w, v_sgu_ln_b, v_sgu_w, v_sgu_b,
                            v_hgrn_lower_bounds, v_hgrn_norm_w, v_w_branch_a, v_w_branch_b, v_w_out, v_norm2_w, v_ffn_w_up,
                            v_ffn_conv_w, v_ffn_conv_b, v_ffn_w_down, v_final_norm_w)))
    depth, D = norm1_w.shape
    dff = ffn_conv_b.shape[1]
    ctx_rows, seq = ctx.shape[1], x.shape[1]

    assert depth == 2, "the lower-bound softmax is written for two layers"
    core = lax.axis_index("c")
    chip = 2 * lax.axis_index("x") + lax.axis_index("y")
    ids = jnp.stack([core, chip]).astype(jnp.int32)

    first, rest = _LAYER_KEYS[:2], _LAYER_KEYS[2:]
    shard = lambda l, k: w[_SHARDED[_LAYER_KEYS.index(k)]][l].astype(BF16)
    started, conv_full = {}, []

    def landing(s):
        return lax.dynamic_update_slice(lax.empty((N_CHIPS,) + s.shape, s.dtype), s[None], (chip,) + (0,) * s.ndim)

    def start_gather(l, keys, tag):
        lands = [landing(shard(l, k)) for k in keys]
        started[tag] = _split_start([], lands, lambda ins, lds, x, y, c: _plan_gather_far(lds, x, y, c),
                                    (N_CHIPS - 1) * _n_half_pieces(lands), f"gather_start_{tag}")
        return started[tag]["token"]

    def finish_gather(keys, tag, after):
        _, lands = _split_wait(started[tag], after, f"gather_wait_{tag}")
        return dict(zip(keys, _pair_forward(lands, f"gather_forward_{tag}")))

    def layer_weights(l, after):
        if l == 0:
            got = _gather_weights([landing(shard(0, k)) for k in first] + [landing(ffn_conv_w)], "gather_weights_first")
            conv_full.append(jnp.transpose(got[-1], (1, 2, 3, 0, 4)).reshape(depth, 9, dff))
            out = dict(zip(first, got), token=start_gather(0, rest, "rest_0"))
        else:
            out = dict(finish_gather(first, f"first_{l}", after), token=0.0)

        def late(after_late):
            more = finish_gather(rest, f"rest_{l}", after_late)
            more["late_token"] = 0.0
            if l + 1 < depth:
                more["late_token"] = start_gather(l + 1, first, f"first_{l + 1}") + start_gather(l + 1, rest, f"rest_{l + 1}")
            return more

        return dict(out, conv_w=conv_full[0][l], late=late)

    groups, order = {}, []

    def as_parts(gs):
        return [g.reshape(N_CHIPS, N_CORES, g.size // (N_CHIPS * N_CORES * g.shape[-1]), g.shape[-1]) for g in gs]

    def pair_start(tag, l, keys, gs):
        parts = as_parts(gs)
        lands = [lax.empty((N_CHIPS,) + p.shape[2:], p.dtype) for p in parts]
        groups[tag] = dict(l=l, keys=keys, pair=_split_start(parts, lands, _plan_pair, _n_pair_copies(parts),
                                                             f"reduce_pair_start_{tag}"))
        order.append(tag)
        return groups[tag]["pair"]["token"]

    def chips_start(tag, after):
        parts, other = _split_wait(groups[tag]["pair"], after, f"reduce_pair_wait_{tag}")
        sums = [_sum_pair(a, o, ids, f"sum_pair_{tag}_{i}") for i, (a, o) in enumerate(zip(parts, other))]
        lands = [lax.empty(s.shape, s.dtype) for s in sums]
        groups[tag]["chips"] = _split_start(sums, lands, _plan_chips, _n_chips_copies(sums), f"reduce_chips_start_{tag}")
        return groups[tag]["chips"]["token"]

    def chips_finish(tag, after):
        sums, recv = _split_wait(groups[tag]["chips"], after, f"reduce_chips_wait_{tag}")
        return {(groups[tag]["l"], k): _sum_chips(sums[i], recv[i], ids, f"sum_chips_{tag}_{i}")
                for i, k in enumerate(groups[tag]["keys"])}

    def on_layer_grads(l, stage, gs):
        if stage == "early":
            return pair_start(f"early_{l}", l, list(gs), list(gs.values()))
        if stage == "late":
            return pair_start(f"late_{l}", l, ["w_in"], [gs["w_in"]]) + chips_start(f"early_{l}", gs["w_in"])
        return chips_start(f"late_{l}", gs["after"])

    W = dict(ada_b=ada_b, norm1_w=norm1_w, sgu_ln_w=sgu_ln_w, sgu_ln_b=sgu_ln_b, sgu_w=sgu_w.astype(BF16),
             sgu_bt=jnp.swapaxes(sgu_b, 1, 2), hlb=hgrn_lower_bounds, hnw=hgrn_norm_w, norm2_w=norm2_w, conv_b=ffn_conv_b,
             final_norm_w=final_norm_w)
    xs = jnp.concatenate([ctx[0], x[0]], axis=0)
    cv = jnp.concatenate([c_ctx[None, :], c, jnp.zeros((14, D), F32)], axis=0)
    loss_local, dxs, G, sa = _local_step(xs, cv, loss_target[0], W, layer_weights, on_layer_grads, ctx_rows)
    loss = lax.psum(loss_local, ("x", "y", "c"))
    grad_x = dxs[ctx_rows:][None]

    pad8 = lambda a: jnp.pad(a, ((0, 8 - a.shape[0]), (0, 0)))
    fact = jnp.concatenate([pad8(sa[1:2].astype(F32))] + [pad8(G["dmod"][l][1].reshape(N_MOD, D)) for l in range(depth)]
                           + [pad8(G["dmod"][l][0].reshape(N_MOD, D)) for l in range(depth)], axis=0)
    facts = _gather_all(fact, "gather_mod_factors")
    lhs = jnp.concatenate([facts[:, 0].astype(BF16), jnp.broadcast_to(sa[0:1], (8, D))], axis=0)
    ada_cols = N_MOD * D // N_CHIPS
    g_ada = []
    for l in range(depth):
        lo_x, lo_c = 8 * (1 + l), 8 * (1 + depth + l)
        rhs = jnp.concatenate([facts[:, lo_x:lo_x + N_MOD].reshape(8, N_MOD * D),
                               facts[:, lo_c:lo_c + N_MOD].reshape(8, N_MOD * D)], axis=0)
        rhs = lax.dynamic_slice_in_dim(rhs, chip * ada_cols, ada_cols, axis=1).astype(BF16)
        g_ada.append(_mm_tn(lhs, rhs, F32, f"dw_ada_{l}"))

    dh = G["hlb1"][depth - 1]
    small_like = [w[k] for k in _SMALL] + [jnp.zeros((depth, 9, dff), F32)]
    small = [G["c_ctx"], jnp.stack(G["ada_b"]), jnp.stack(G["norm1_w"]), jnp.stack(G["sgu_ln_w"]), jnp.stack(G["sgu_ln_b"]),
             jnp.stack(G["sgu_w"]), jnp.stack(G["sgu_b"]), jnp.stack([-dh, dh]), jnp.stack(G["hnw"]), jnp.stack(G["norm2_w"]),
             jnp.stack(G["conv_b"]), G["final_norm_w"], jnp.stack(G["conv_w"])]
    n_small = sum(a.size for a in small)
    n_small_pad = _round_up(n_small, N_CORES * 16 * PACK_COLS)
    small_rows = n_small_pad // (N_CORES * PACK_COLS)
    small_rep = jnp.broadcast_to(_pack(small, n_small_pad).reshape(1, N_CORES, small_rows, PACK_COLS),
                                 (N_CHIPS, N_CORES, small_rows, PACK_COLS))
    small_parts = as_parts([small_rep])
    small_sums = [_sum_pair(small_parts[0], _reduce_pair(small_parts, "reduce_pair_small")[0], ids, "sum_pair_small")]
    groups["small"] = dict(l=None, keys=["small"], chips=_split_start(
        small_sums, [lax.empty(small_sums[0].shape, small_sums[0].dtype)], _plan_chips, _n_chips_copies(small_sums),
        "reduce_chips_start_small"))

    def gather_halves(halves, name):
        return dict(zip(halves, _gather_pair(list(halves.values()), name)))

    last = order[-1]
    halves = {}
    for tag in order[:-1]:
        halves.update(chips_finish(tag, groups["small"]["chips"]["token_array"]))
    reduced = gather_halves(halves, "gather_pair")
    grads, delta, new_m, new_v = {}, {}, {}, {}

    def adamw_sharded(i):
        k = _SHARDED[i]
        gs = g_ada if i == 0 else [reduced[(l, _LAYER_KEYS[i])].reshape(w[k].shape[1:]) for l in range(depth)]
        grads[k], delta[k], new_m[k], new_v[k] = _adamw(w[k], gs, mom[k], var[k], f"adamw_{k}")

    last_keys = groups[last]["keys"]
    for i in range(len(_SHARDED)):
        if _LAYER_KEYS[i] not in last_keys:
            adamw_sharded(i)
    halves = chips_finish(last, new_v[_SHARDED[-1]])
    halves.update(chips_finish("small", new_v[_SHARDED[-1]]))
    reduced.update(gather_halves(halves, "gather_pair_last"))
    for i in range(len(_SHARDED)):
        if _LAYER_KEYS[i] in last_keys:
            adamw_sharded(i)

    g_small = _unpack(reduced[(None, "small")].reshape(-1), small_like)
    grads.update(zip(_SMALL, g_small[:-1]))
    grads["ffn_conv_w"] = lax.dynamic_slice_in_dim(g_small[-1].reshape(depth, 3, 3, dff), chip * (dff // N_CHIPS),
                                                   dff // N_CHIPS, axis=3)
    packed = _SMALL + ("ffn_conv_w",)
    n_pad = _round_up(sum(w[k].size for k in packed), 16 * PACK_COLS)
    pack = lambda t: _pack([t[k] for k in packed], n_pad).reshape(1, -1, PACK_COLS)
    _, d, nm, nv = _adamw(pack(w), [pack(grads)[0]], pack(mom), pack(var), "adamw_packed")
    like = [w[k] for k in packed]
    for src, dst in ((d, delta), (nm, new_m), (nv, new_v)):
        dst.update(zip(packed, _unpack(src.reshape(-1), like)))

    return (loss, grad_x, *[grads[k] for k in _ORDER], *[delta[k] for k in _ORDER], *[new_m[k] for k in _ORDER],
            *[new_v[k] for k in _ORDER])
```

```python
import functools

import jax
import jax.numpy as jnp
from jax import lax
from jax.experimental import pallas as pl
from jax.experimental.pallas import tpu as pltpu

F32 = jnp.float32
BF16 = jnp.bfloat16

GRID_W = 64
HG_CHUNK = 64
SGU_CHUNK = 128
HEAD = 128
TB = 256
N_MOD = 6
RMS_EPS = 1e-6
LN_EPS = 1e-5
VMEM_LIMIT = 48 * 1024 * 1024
N_CHIPS = 4
N_CORES = 2

ADAM_LR = 0.001
ADAM_B1 = 0.9
ADAM_B2 = 0.999
ADAM_EPS = 1e-08
ADAM_WD = 0.01
ADAM_STEP = 10

_GELU_C = 0.7978845608028654
_GELU_A = 0.044715


def _sigmoid(x):
    return 0.5 * jnp.tanh(0.5 * x) + 0.5


def _silu(x):
    return x * _sigmoid(x)


def _silu_both(x):
    s = _sigmoid(x)
    return x * s, s * (1.0 + x * (1.0 - s))


def _dsilu(x):
    return _silu_both(x)[1]


def _gelu_both(x):
    x2 = x * x
    t = jnp.tanh(_GELU_C * (x + _GELU_A * x2 * x))
    h = 0.5 * (1.0 + t)
    return x * h, h + 0.5 * x * (1.0 - t * t) * (_GELU_C + 3.0 * _GELU_C * _GELU_A * x2)


def _gelu(x):
    return 0.5 * x * (1.0 + jnp.tanh(_GELU_C * (x + _GELU_A * x * x * x)))


def _dgelu(x):
    return _gelu_both(x)[1]


def _dot(a, b, ca, cb):
    return lax.dot_general(a, b, (((ca,), (cb,)), ((), ())), preferred_element_type=F32)


def _nn(a, b):
    return _dot(a, b, 1, 0)


def _nt(a, b):
    return _dot(a, b, 1, 1)


def _tn(a, b):
    return _dot(a, b, 0, 0)


def _params(*sem, vmem=VMEM_LIMIT):
    return pltpu.CompilerParams(dimension_semantics=sem if sem else None, vmem_limit_bytes=vmem)


def _stream_of(i, ctx_blocks):
    return (i >= ctx_blocks).astype(jnp.int32)


def _mm(a, b, mode, tm, tn, tk, out_dtype, name, add=None, b_chips=False, out_chips=False):
    if not b_chips:
        bshape = b.shape
    else:
        bshape = (b.shape[1], N_CHIPS * b.shape[2])
    if mode == "nn":
        (M, K), (K2, N) = a.shape, bshape
    elif mode == "nt":
        (M, K), (N, K2) = a.shape, bshape
    else:
        (K, M), (K2, N) = a.shape, bshape
    assert K == K2 and M % tm == 0 and N % tn == 0 and K % tk == 0, (name, a.shape, b.shape, tm, tn, tk)
    nk = K // tk
    if mode == "tn":
        a_spec = pl.BlockSpec((tk, tm), lambda j, i, k: (k, i))
    else:
        a_spec = pl.BlockSpec((tm, tk), lambda j, i, k: (i, k))
    if not b_chips:
        if mode == "nt":
            b_spec = pl.BlockSpec((tn, tk), lambda j, i, k: (j, k))
        else:
            b_spec = pl.BlockSpec((tk, tn), lambda j, i, k: (k, j))
    else:
        cols = b.shape[2]
        if mode == "nn":
            per = cols // tn
            assert cols % tn == 0
            b_spec = pl.BlockSpec((None, tk, tn), lambda j, i, k: (j // per, k, j % per))
        else:
            per = cols // tk
            assert mode == "nt" and cols % tk == 0
            b_spec = pl.BlockSpec((None, tn, tk), lambda j, i, k: (k // per, j, k % per))
    if out_chips:
        per_o = (N // N_CHIPS) // tn
        assert (N // N_CHIPS) % tn == 0 and add is None
        o_spec = pl.BlockSpec((None, tm, tn), lambda j, i, k: (j // per_o, i, j % per_o))
        o_shape = (N_CHIPS, M, N // N_CHIPS)
    else:
        o_spec = pl.BlockSpec((tm, tn), lambda j, i, k: (i, j))
        o_shape = (M, N)
    ca, cb = {"nn": (1, 0), "nt": (1, 1), "tn": (0, 0)}[mode]

    in_place = add is None and (nk == 1 or out_dtype == F32)

    def body(a_ref, b_ref, *rest):
        if in_place:
            (o_ref,) = rest
        elif add is None:
            o_ref, acc = rest
        else:
            add_ref, o_ref, acc = rest
        k = pl.program_id(2)
        part = _dot(a_ref[...], b_ref[...], ca, cb)
        if in_place:
            if nk == 1:
                o_ref[...] = part.astype(out_dtype)
            else:
                @pl.when(k == 0)
                def _():
                    o_ref[...] = part

                @pl.when(k > 0)
                def _():
                    o_ref[...] += part
            return

        @pl.when(k == 0)
        def _():
            acc[...] = jnp.zeros_like(acc)

        acc[...] += part

        @pl.when(k == nk - 1)
        def _():
            r = acc[...]
            if add is not None:
                r = r + add_ref[...]
            o_ref[...] = r.astype(out_dtype)

    ins = [a, b] + ([] if add is None else [add])
    specs = [a_spec, b_spec] + ([] if add is None else [o_spec])
    return pl.pallas_call(
        body, name=name, grid=(N // tn, M // tm, nk), in_specs=specs, out_specs=o_spec,
        out_shape=jax.ShapeDtypeStruct(o_shape, out_dtype),
        scratch_shapes=[] if in_place else [pltpu.VMEM((tm, tn), F32)],
        compiler_params=_params("parallel", "parallel", "arbitrary"),
    )(*ins)


def _tile(n, pref):
    if n <= pref:
        return n
    best = None
    for t in range(128, pref + 1, 128):
        if n % t == 0:
            best = t
    assert best is not None, (n, pref)
    return best


def _rows_tile(n, pref):
    if n <= pref:
        return n
    best = None
    for t in range(16, pref + 1, 16):
        if n % t == 0:
            best = t
    assert best is not None, (n, pref)
    return best


def _mm_nn_w(a, wg, out_dtype, name):
    M, K = a.shape
    return _mm(a, wg, "nn", _rows_tile(M, 2176), _tile(wg.shape[2], 1536), _tile(K, 1536), out_dtype, name, b_chips=True)


def _mm_nt_w(a, wg, out_dtype, name):
    M, K = a.shape
    return _mm(a, wg, "nt", _rows_tile(M, 2176), _tile(wg.shape[1], 1024), _tile(wg.shape[2], 1536), out_dtype, name,
               b_chips=True)


def _mm_tn(a, b, out_dtype, name, out_chips=False):
    K, M = a.shape
    N = b.shape[1]
    ncol = N // N_CHIPS if out_chips else N
    tm, tn = _tile(M, 1408), _tile(ncol, 1408)
    if tm * tn > 1408 * 1152:
        tn = _tile(ncol, 1152)
    return _mm(a, b, "tn", tm, tn, _rows_tile(K, 2176), out_dtype, name, out_chips=out_chips)


def _mod_fwd(cv, wg, b, name):
    R, D = cv.shape
    tn = wg.shape[2]
    N = N_CHIPS * tn

    def body(cv_ref, w_ref, b_ref, mod_ref, sa_ref):
        sa = _silu(cv_ref[...]).astype(BF16)
        sa_ref[...] = sa
        mod_ref[...] = _nn(sa, w_ref[...]) + b_ref[...]

    return pl.pallas_call(
        body, name=name, grid=(N_CHIPS,),
        in_specs=[pl.BlockSpec((R, D), lambda j: (0, 0)), pl.BlockSpec((None, D, tn), lambda j: (j, 0, 0)),
                  pl.BlockSpec((1, tn), lambda j: (0, j))],
        out_specs=[pl.BlockSpec((R, tn), lambda j: (0, j)), pl.BlockSpec((R, D), lambda j: (0, 0))],
        out_shape=[jax.ShapeDtypeStruct((R, N), F32), jax.ShapeDtypeStruct((R, D), BF16)],
        compiler_params=_params("arbitrary"),
    )(cv, wg, b)


def _cvec_bwd(dmod, wg, cv, name):
    R, N = dmod.shape
    D = wg.shape[1]
    tk = wg.shape[2]
    nk = N_CHIPS

    def body(dm_ref, w_ref, cv_ref, o_ref):
        k = pl.program_id(0)

        @pl.when(k == 0)
        def _():
            o_ref[...] = jnp.zeros_like(o_ref)

        o_ref[...] += _nt(dm_ref[...].astype(BF16), w_ref[...])

        @pl.when(k == nk - 1)
        def _():
            o_ref[...] = o_ref[...] * _dsilu(cv_ref[...])

    return pl.pallas_call(
        body, name=name, grid=(nk,),
        in_specs=[pl.BlockSpec((R, tk), lambda k: (0, k)), pl.BlockSpec((None, D, tk), lambda k: (k, 0, 0)),
                  pl.BlockSpec((R, D), lambda k: (0, 0))],
        out_specs=pl.BlockSpec((R, D), lambda k: (0, 0)),
        out_shape=jax.ShapeDtypeStruct((R, D), F32),
        compiler_params=_params("arbitrary"),
    )(dmod, wg, cv)


def _norm_mod(x, nw, mod, which, ctx_rows, name):
    T, D = x.shape
    cb = ctx_rows // TB

    def body(x_ref, nw_ref, mod_ref, h_ref):
        xv = x_ref[...]
        r = lax.rsqrt(jnp.mean(xv * xv, axis=-1, keepdims=True) + RMS_EPS)
        y = xv * r * nw_ref[...]
        sh = mod_ref[which:which + 1, :]
        sc = mod_ref[which + 1:which + 2, :]
        h_ref[...] = (y * (1.0 + sc) + sh).astype(BF16)

    return pl.pallas_call(
        body, name=name, grid=(T // TB,),
        in_specs=[pl.BlockSpec((TB, D), lambda i: (i, 0)), pl.BlockSpec((1, D), lambda i: (0, 0)),
                  pl.BlockSpec((None, N_MOD, D), lambda i: (_stream_of(i, cb), 0, 0))],
        out_specs=pl.BlockSpec((TB, D), lambda i: (i, 0)),
        out_shape=jax.ShapeDtypeStruct((T, D), BF16),
        compiler_params=_params("parallel"),
    )(x, nw, mod)


def _norm_mod_bwd(dh, x, dres, nw, mod, which, ctx_rows, name):
    T, D = x.shape
    cb = ctx_rows // TB

    def body(dh_ref, x_ref, dres_ref, nw_ref, mod_ref, dx_ref, dm_ref, dnw_ref):
        i = pl.program_id(0)

        @pl.when(i == 0)
        def _():
            dnw_ref[...] = jnp.zeros_like(dnw_ref)

        @pl.when((i == 0) | (i == cb))
        def _():
            dm_ref[...] = jnp.zeros_like(dm_ref)

        xv = x_ref[...]
        dh = dh_ref[...]
        r = lax.rsqrt(jnp.mean(xv * xv, axis=-1, keepdims=True) + RMS_EPS)
        xh = xv * r
        nwv = nw_ref[...]
        sc = mod_ref[which + 1:which + 2, :]
        y = xh * nwv
        dm_ref[0:1, :] += jnp.sum(dh, axis=0, keepdims=True)
        dm_ref[1:2, :] += jnp.sum(dh * y, axis=0, keepdims=True)
        dy = dh * (1.0 + sc)
        dnw_ref[...] += jnp.sum(dy * xh, axis=0, keepdims=True)
        dxh = dy * nwv
        dx_ref[...] = dres_ref[...] + r * (dxh - xh * jnp.mean(dxh * xh, axis=-1, keepdims=True))

    return pl.pallas_call(
        body, name=name, grid=(T // TB,),
        in_specs=[pl.BlockSpec((TB, D), lambda i: (i, 0)), pl.BlockSpec((TB, D), lambda i: (i, 0)),
                  pl.BlockSpec((TB, D), lambda i: (i, 0)), pl.BlockSpec((1, D), lambda i: (0, 0)),
                  pl.BlockSpec((None, N_MOD, D), lambda i: (_stream_of(i, cb), 0, 0))],
        out_specs=[pl.BlockSpec((TB, D), lambda i: (i, 0)),
                   pl.BlockSpec((None, 2, D), lambda i: (_stream_of(i, cb), 0, 0)),
                   pl.BlockSpec((1, D), lambda i: (0, 0))],
        out_shape=[jax.ShapeDtypeStruct((T, D), F32), jax.ShapeDtypeStruct((2, 2, D), F32),
                   jax.ShapeDtypeStruct((1, D), F32)],
        compiler_params=_params("arbitrary"),
    )(dh, x, dres, nw, mod)


def _scan_chunk(n, rev, n_ctx, n_all):
    if not rev:
        return n
    return jnp.where(n < n_ctx, n_ctx - 1 - n, n_all - 1 + n_ctx - n)


def _cumsum_rows(x, rev):
    rows = x.shape[0]
    row = lax.broadcasted_iota(jnp.int32, (rows, 1), 0)
    s = 1
    while s < rows:
        if not rev:
            x = x + jnp.where(row >= s, pltpu.roll(x, s, 0), 0.0)
        else:
            x = x + jnp.where(row < rows - s, pltpu.roll(x, rows - s, 0), 0.0)
        s *= 2
    return x


def _lower_bound(hlb_ref, layer):
    h = hlb_ref[...]
    if layer == 0:
        return jnp.zeros_like(h[0:1, :])
    return _sigmoid(h[1:2, :] - h[0:1, :])


HG_STEP = 4


def _step_rows(j, rev, backward):
    sub = j if rev == backward else HG_STEP - 1 - j
    return slice(sub * HG_CHUNK, (sub + 1) * HG_CHUNK)


def _hgrn_gates(q_ref, f_ref, hlb_ref, layer, rev, rows):
    lb = _lower_bound(hlb_ref, layer)
    z = f_ref[rows, :]
    sig = 1.0 / (1.0 + jnp.exp(-z))
    fg = lb + (1.0 - lb) * sig
    kk = (1.0 - lb) * (1.0 - sig)
    g = jnp.log(fg)
    b = _cumsum_rows(g, rev)
    bt = jnp.sum(g, axis=0, keepdims=True)
    mid = HG_CHUNK // 2
    r = b[mid:mid + 1, :] if rev else b[mid - 1:mid, :]
    qh = _silu(q_ref[rows, :])
    return lb, sig, fg, kk, b, bt, r, qh


def _tri_mask(rev):
    t = lax.broadcasted_iota(jnp.int32, (HG_CHUNK, HG_CHUNK), 0)
    s = lax.broadcasted_iota(jnp.int32, (HG_CHUNK, HG_CHUNK), 1)
    return (s >= t) if rev else (s <= t)


def _hgrn_fwd(parts, hlb, layer, rev, ctx_rows, name, o_add=None):
    T = parts.shape[0]
    D = hlb.shape[1] // 2
    nh = D // HEAD
    n_all, n_ctx = T // HG_CHUNK, ctx_rows // HG_CHUNK
    assert n_all % HG_STEP == 0 and n_ctx % HG_STEP == 0
    n_steps = n_all // HG_STEP
    block = functools.partial(_scan_chunk, rev=rev, n_ctx=n_ctx // HG_STEP, n_all=n_steps)
    fcol = 2 if rev else 1

    def body(q_ref, f_ref, i_ref, hlb_ref, *rest):
        if o_add is None:
            o_ref, st_ref, s_scr = rest
        else:
            oa_ref, o_ref, st_ref, s_scr = rest
        n = pl.program_id(0)

        @pl.when(n == 0)
        def _():
            s_scr[...] = jnp.zeros_like(s_scr)

        mask = _tri_mask(rev)
        hs = [slice(h * HEAD, (h + 1) * HEAD) for h in range(nh)]
        for j in range(HG_STEP):
            rows = _step_rows(j, rev, False)
            lb, sig, fg, kk, b, bt, r, qh = _hgrn_gates(q_ref, f_ref, hlb_ref, layer, rev, rows)
            qr = (qh * jnp.exp(b - r)).astype(BF16)
            kr = (kk * jnp.exp(r - b)).astype(BF16)
            qe = (qh * jnp.exp(b)).astype(BF16)
            ke = (kk * jnp.exp(bt - b)).astype(BF16)
            dec = jnp.exp(bt)
            v = i_ref[rows, :].astype(BF16)
            st = [s_scr[h] for h in range(nh)]
            a_raw = [_nt(qr[:, sl], kr[:, sl]) for sl in hs]
            o_int = [_nt(qe[:, sl], st[h].astype(BF16)) for h, sl in enumerate(hs)]
            kv = [_tn(v[:, sl], ke[:, sl]) for sl in hs]
            for h, sl in enumerate(hs):
                st_ref[j, h] = st[h]
                o = _nn(jnp.where(mask, a_raw[h], 0.0).astype(BF16), v[:, sl]) + o_int[h]
                if o_add is not None:
                    o = o + oa_ref[rows, sl]
                o_ref[rows, sl] = o
                s_scr[h] = st[h] * dec[:, sl] + kv[h]

    cspec = lambda col: pl.BlockSpec((HG_STEP * HG_CHUNK, D), lambda n: (block(n), col))
    ins = [parts, parts, parts, hlb]
    specs = [cspec(0), cspec(fcol), cspec(3), pl.BlockSpec((2, D), lambda n: (0, 1 if rev else 0))]
    if o_add is not None:
        ins.append(o_add)
        specs.append(cspec(0))
    return pl.pallas_call(
        body, name=name, grid=(n_steps,), in_specs=specs,
        out_specs=[cspec(0), pl.BlockSpec((HG_STEP, nh, HEAD, HEAD), lambda n: (n, 0, 0, 0))],
        out_shape=[jax.ShapeDtypeStruct((T, D), F32), jax.ShapeDtypeStruct((n_all, nh, HEAD, HEAD), F32)],
        scratch_shapes=[pltpu.VMEM((nh, HEAD, HEAD), F32)],
        compiler_params=_params("arbitrary"),
    )(*ins)


def _hgrn_bwd(parts, hlb, do, states, layer, rev, ctx_rows, name, other=None, dparts=None):
    T = parts.shape[0]
    D = hlb.shape[1] // 2
    nh = D // HEAD
    n_all, n_ctx = T // HG_CHUNK, ctx_rows // HG_CHUNK
    assert n_all % HG_STEP == 0 and n_ctx % HG_STEP == 0
    n_steps = n_all // HG_STEP
    step = lambda m: n_steps - 1 - m
    block = lambda m: _scan_chunk(step(m), rev, n_ctx // HG_STEP, n_steps)
    fcol = 2 if rev else 1
    has_add = other is not None
    assert not has_add or rev

    def body(q_ref, f_ref, i_ref, hlb_ref, do_ref, st_ref, *rest):
        if has_add:
            dqa_ref, dza_ref, dia_ref, _, out_ref, dlb_ref, ds_scr = rest
            dq_ref, dz_ref, di_ref = out_ref.at[:, 0:D], out_ref.at[:, 2 * D:3 * D], out_ref.at[:, 3 * D:4 * D]
            out_ref[:, D:2 * D] = dza_ref[...]
        else:
            dq_ref, dz_ref, di_ref, dlb_ref, ds_scr = rest
        m = pl.program_id(0)

        @pl.when(m == 0)
        def _():
            ds_scr[...] = jnp.zeros_like(ds_scr)
            dlb_ref[...] = jnp.zeros_like(dlb_ref)

        mask = _tri_mask(rev)
        hs = [slice(h * HEAD, (h + 1) * HEAD) for h in range(nh)]
        for j in range(HG_STEP):
            rows = _step_rows(j, rev, True)
            slot = HG_STEP - 1 - j
            lb, sig, fg, kk, b, bt, r, qh = _hgrn_gates(q_ref, f_ref, hlb_ref, layer, rev, rows)
            e_qr = jnp.exp(b - r)
            e_kr = jnp.exp(r - b)
            e_b = jnp.exp(b)
            e_ke = jnp.exp(bt - b)
            dec = jnp.exp(bt)
            qr = (qh * e_qr).astype(BF16)
            kr = (kk * e_kr).astype(BF16)
            qe = (qh * e_b).astype(BF16)
            ke = (kk * e_ke).astype(BF16)
            v = i_ref[rows, :].astype(BF16)
            dov = do_ref[rows, :].astype(BF16)
            st = [st_ref[slot, h] for h in range(nh)]
            dst = [ds_scr[h] for h in range(nh)]
            stb = [t.astype(BF16) for t in st]
            dstb = [t.astype(BF16) for t in dst]
            a_raw = [_nt(qr[:, sl], kr[:, sl]) for sl in hs]
            da_raw = [_nt(dov[:, sl], v[:, sl]) for sl in hs]
            dq_int = [_nn(dov[:, sl], stb[h]) for h, sl in enumerate(hs)]
            dk_int = [_nn(v[:, sl], dstb[h]) for h, sl in enumerate(hs)]
            dv_int = [_nt(ke[:, sl], dstb[h]) for h, sl in enumerate(hs)]
            ds_new = [_tn(dov[:, sl], qe[:, sl]) for sl in hs]
            a = [jnp.where(mask, t, 0.0).astype(BF16) for t in a_raw]
            da = [jnp.where(mask, t, 0.0).astype(BF16) for t in da_raw]
            dv_parts = [_tn(a[h], dov[:, sl]) + dv_int[h] for h, sl in enumerate(hs)]
            dq_parts = [_nn(da[h], kr[:, sl]) * e_qr[:, sl] + dq_int[h] * e_b[:, sl] for h, sl in enumerate(hs)]
            dki_parts = [dk_int[h] * e_ke[:, sl] for h, sl in enumerate(hs)]
            dk_parts = [_tn(da[h], qr[:, sl]) * e_kr[:, sl] + dki_parts[h] for h, sl in enumerate(hs)]
            dbt_parts = [dec[:, sl] * jnp.sum(st[h] * dst[h], axis=0, keepdims=True) for h, sl in enumerate(hs)]
            for h, sl in enumerate(hs):
                ds_scr[h] = dst[h] * dec[:, sl] + ds_new[h]
            dq = jnp.concatenate(dq_parts, axis=1)
            dk = jnp.concatenate(dk_parts, axis=1)
            dki = jnp.concatenate(dki_parts, axis=1)
            dv = jnp.concatenate(dv_parts, axis=1)
            dbt = jnp.concatenate(dbt_parts, axis=1) + jnp.sum(kk * dki, axis=0, keepdims=True)
            db = qh * dq - kk * dk
            dg = _cumsum_rows(db, not rev) + dbt
            df = dg / fg - dk
            dz_ref[rows, :] = (df * (1.0 - lb) * sig * (1.0 - sig)).astype(BF16)
            dlb_ref[...] += jnp.sum(df * (1.0 - sig), axis=0, keepdims=True)
            dqr = dq * _dsilu(q_ref[rows, :])
            if has_add:
                dqr = dqr + dqa_ref[rows, :]
                dv = dv + dia_ref[rows, :]
            dq_ref[rows, :] = dqr.astype(dq_ref.dtype)
            di_ref[rows, :] = dv.astype(di_ref.dtype)

        @pl.when(m == n_steps - 1)
        def _():
            lb = _lower_bound(hlb_ref, layer)
            if layer == 0:
                dlb_ref[...] = jnp.zeros_like(dlb_ref)
            else:
                dlb_ref[...] = dlb_ref[...] * lb * (1.0 - lb)

    cspec = lambda col: pl.BlockSpec((HG_STEP * HG_CHUNK, D), lambda m: (block(m), col))
    ins = [parts, parts, parts, hlb, do, states]
    specs = [cspec(0), cspec(fcol), cspec(3), pl.BlockSpec((2, D), lambda m: (0, 1 if rev else 0)), cspec(0),
             pl.BlockSpec((HG_STEP, nh, HEAD, HEAD), lambda m: (step(m), 0, 0, 0))]
    dlb_spec = pl.BlockSpec((1, D), lambda m: (0, 0))
    dlb_shape = jax.ShapeDtypeStruct((1, D), F32)
    if has_add:
        return pl.pallas_call(
            body, name=name, grid=(n_steps,),
            in_specs=specs + [cspec(0), cspec(0), cspec(0), pl.BlockSpec(memory_space=pl.ANY)],
            out_specs=[pl.BlockSpec((HG_STEP * HG_CHUNK, 4 * D), lambda m: (block(m), 0)), dlb_spec],
            out_shape=[jax.ShapeDtypeStruct(dparts.shape, dparts.dtype), dlb_shape],
            scratch_shapes=[pltpu.VMEM((nh, HEAD, HEAD), F32)], input_output_aliases={len(ins) + 3: 0},
            compiler_params=_params("arbitrary"),
        )(*ins, *other, dparts)
    return pl.pallas_call(
        body, name=name, grid=(n_steps,), in_specs=specs,
        out_specs=[cspec(0), cspec(0), cspec(0), dlb_spec],
        out_shape=[jax.ShapeDtypeStruct((T, D), F32), jax.ShapeDtypeStruct((T, D), BF16),
                   jax.ShapeDtypeStruct((T, D), F32), dlb_shape],
        scratch_shapes=[pltpu.VMEM((nh, HEAD, HEAD), F32)],
        compiler_params=_params("arbitrary"),
    )(*ins)


def _sgu_ln(gv, lnw_ref, lnb_ref):
    mu = jnp.mean(gv, axis=-1, keepdims=True)
    xc = gv - mu
    rstd = lax.rsqrt(jnp.mean(xc * xc, axis=-1, keepdims=True) + LN_EPS)
    xh = xc * rstd
    return xh, rstd, xh * lnw_ref[...] + lnb_ref[...]


def _sgu_fwd(parts, lnw, lnb, w, bt, name):
    T = parts.shape[0]
    D = lnw.shape[1]
    G = D // HEAD

    def body(u_ref, v_ref, lnw_ref, lnb_ref, w_ref, bt_ref, ya_ref):
        gu = _gelu(u_ref[...])
        _, _, vn = _sgu_ln(_gelu(v_ref[...]), lnw_ref, lnb_ref)
        vnb = vn.astype(BF16)
        for g in range(G):
            sl = slice(g * HEAD, (g + 1) * HEAD)
            mixed = _nn(w_ref[g], vnb[:, sl]) + bt_ref[:, g:g + 1]
            ya_ref[:, sl] = (gu[:, sl] * mixed).astype(BF16)

    return pl.pallas_call(
        body, name=name, grid=(T // SGU_CHUNK,),
        in_specs=[pl.BlockSpec((SGU_CHUNK, D), lambda n: (n, 4)), pl.BlockSpec((SGU_CHUNK, D), lambda n: (n, 5)),
                  pl.BlockSpec((1, D), lambda n: (0, 0)), pl.BlockSpec((1, D), lambda n: (0, 0)),
                  pl.BlockSpec((G, SGU_CHUNK, SGU_CHUNK), lambda n: (0, 0, 0)),
                  pl.BlockSpec((SGU_CHUNK, G), lambda n: (0, 0))],
        out_specs=pl.BlockSpec((SGU_CHUNK, D), lambda n: (n, 0)),
        out_shape=jax.ShapeDtypeStruct((T, D), BF16),
        compiler_params=_params("parallel"),
    )(parts, parts, lnw, lnb, w, bt)


def _sgu_bwd(parts, dya, lnw, lnb, w, bt, dparts, name):
    T = parts.shape[0]
    D = lnw.shape[1]
    G = D // HEAD

    def body(u_ref, v_ref, dya_ref, lnw_ref, lnb_ref, w_ref, bt_ref, dparts_in,
             duv_ref, dw_ref, dbt_ref, dlnw_ref, dlnb_ref, dvn_scr):
        du_ref = duv_ref.at[:, 0:D]
        dv_ref = duv_ref.at[:, D:2 * D]
        n = pl.program_id(0)

        @pl.when(n == 0)
        def _():
            dw_ref[...] = jnp.zeros_like(dw_ref)
            dbt_ref[...] = jnp.zeros_like(dbt_ref)
            dlnw_ref[...] = jnp.zeros_like(dlnw_ref)
            dlnb_ref[...] = jnp.zeros_like(dlnb_ref)

        gu, dgu = _gelu_both(u_ref[...])
        gv, dgv_dv = _gelu_both(v_ref[...])
        xh, rstd, vn = _sgu_ln(gv, lnw_ref, lnb_ref)
        vnb = vn.astype(BF16)
        dya = dya_ref[...]
        lane = lax.broadcasted_iota(jnp.int32, (SGU_CHUNK, G), 1)
        dbt = jnp.zeros((SGU_CHUNK, G), F32)
        for g in range(G):
            sl = slice(g * HEAD, (g + 1) * HEAD)
            wg = w_ref[g]
            mixed = _nn(wg, vnb[:, sl]) + bt_ref[:, g:g + 1]
            dmix = dya[:, sl] * gu[:, sl]
            du_ref[:, sl] = (dya[:, sl] * mixed * dgu[:, sl]).astype(BF16)
            dmb = dmix.astype(BF16)
            dvn_scr[:, sl] = _tn(wg, dmb)
            dw_ref[g] += _nt(dmb, vnb[:, sl])
            dbt = dbt + jnp.where(lane == g, jnp.sum(dmix, axis=1, keepdims=True), 0.0)
        dbt_ref[...] += dbt
        dvn = dvn_scr[...]
        dlnw_ref[...] += jnp.sum(dvn * xh, axis=0, keepdims=True)
        dlnb_ref[...] += jnp.sum(dvn, axis=0, keepdims=True)
        dxh = dvn * lnw_ref[...]
        dgv = rstd * (dxh - jnp.mean(dxh, axis=-1, keepdims=True) - xh * jnp.mean(dxh * xh, axis=-1, keepdims=True))
        dv_ref[...] = (dgv * dgv_dv).astype(BF16)

    row = lambda col: pl.BlockSpec((SGU_CHUNK, D), lambda n: (n, col))
    vec = pl.BlockSpec((1, D), lambda n: (0, 0))
    wsp = pl.BlockSpec((G, SGU_CHUNK, SGU_CHUNK), lambda n: (0, 0, 0))
    bsp = pl.BlockSpec((SGU_CHUNK, G), lambda n: (0, 0))
    return pl.pallas_call(
        body, name=name, grid=(T // SGU_CHUNK,),
        in_specs=[row(4), row(5), row(0), vec, vec, wsp, bsp, pl.BlockSpec(memory_space=pl.ANY)],
        out_specs=[pl.BlockSpec((SGU_CHUNK, 2 * D), lambda n: (n, 2)), wsp, bsp, vec, vec],
        out_shape=[jax.ShapeDtypeStruct(dparts.shape, dparts.dtype),
                   jax.ShapeDtypeStruct((G, SGU_CHUNK, SGU_CHUNK), F32), jax.ShapeDtypeStruct((SGU_CHUNK, G), F32),
                   jax.ShapeDtypeStruct((1, D), F32), jax.ShapeDtypeStruct((1, D), F32)],
        scratch_shapes=[pltpu.VMEM((SGU_CHUNK, D), F32)], input_output_aliases={7: 0},
        compiler_params=_params("arbitrary"),
    )(parts, parts, dya, lnw, lnb, w, bt, dparts)


TBT = 256
VMEM_LIMIT_TOKEN_OUT = 58 * 1024 * 1024


def _rows_weight_spec(wg):
    return pl.BlockSpec(wg.shape, lambda i: (0, 0, 0))


def _full(w_ref):
    return w_ref[...].reshape(w_ref.shape[0] * w_ref.shape[1], w_ref.shape[2])


def _token_out_fwd(o, parts, ya, x, mod, hnw, wa, wb, wo, ctx_rows, name):
    T, D = x.shape
    nh = D // HEAD
    cb = ctx_rows // TBT

    def body(o_ref, og_ref, ga_ref, gb_ref, ya_ref, x_ref, mod_ref, hnw_ref, wa_ref, wb_ref, wo_ref,
             yb_ref, pa_ref, pb_ref, mg_ref, tmo_ref, xm_ref):
        ov = o_ref[...]
        so = _silu(og_ref[...])
        nw = hnw_ref[...]
        for h in range(nh):
            sl = slice(h * HEAD, (h + 1) * HEAD)
            seg = ov[:, sl]
            r = lax.rsqrt(jnp.mean(seg * seg, axis=-1, keepdims=True) + RMS_EPS)
            yb_ref[:, sl] = (seg * r * nw * so[:, sl]).astype(BF16)
        pa = _nn(ya_ref[...], _full(wa_ref))
        pb = _nn(yb_ref[...], _full(wb_ref))
        pa_ref[...] = pa
        pb_ref[...] = pb
        mg = (_sigmoid(ga_ref[...]) * pa + _sigmoid(gb_ref[...]) * pb).astype(BF16)
        mg_ref[...] = mg
        out = _nn(mg, _full(wo_ref))
        tmo_ref[...] = out
        xm_ref[...] = x_ref[...] + mod_ref[2:3, :] * out

    row = lambda col: pl.BlockSpec((TBT, D), lambda i: (i, col))
    wsp = _rows_weight_spec(wa)
    sd = lambda dt: jax.ShapeDtypeStruct((T, D), dt)
    return pl.pallas_call(
        body, name=name, grid=(T // TBT,),
        in_specs=[row(0), row(6), row(7), row(8), row(0), row(0),
                  pl.BlockSpec((None, N_MOD, D), lambda i: (_stream_of(i, cb), 0, 0)),
                  pl.BlockSpec((1, HEAD), lambda i: (0, 0)), wsp, wsp, wsp],
        out_specs=[row(0)] * 6,
        out_shape=[sd(BF16), sd(F32), sd(F32), sd(BF16), sd(F32), sd(F32)],
        compiler_params=_params("parallel", vmem=VMEM_LIMIT_TOKEN_OUT),
    )(o, parts, parts, parts, ya, x, mod, hnw, wa, wb, wo)


def _token_out_bwd(dx, tmo, pa, pb, o, parts, mod, hnw, wa, wb, wo, ctx_rows, name):
    T, D = dx.shape
    nh = D // HEAD
    cb = ctx_rows // TBT

    def body(dx_ref, tmo_ref, pa_ref, pb_ref, o_ref, og_ref, ga_ref, gb_ref, mod_ref, hnw_ref, wa_ref, wb_ref, wo_ref,
             dout_ref, dpa_ref, dpb_ref, dgate_ref, dya_ref, do_ref, dg1_ref, dhnw_ref):
        i = pl.program_id(0)

        @pl.when(i == 0)
        def _():
            dhnw_ref[...] = jnp.zeros_like(dhnw_ref)

        @pl.when((i == 0) | (i == cb))
        def _():
            dg1_ref[...] = jnp.zeros_like(dg1_ref)

        dxv = dx_ref[...]
        dg1_ref[...] += jnp.sum(dxv * tmo_ref[...], axis=0, keepdims=True)
        dout = (dxv * mod_ref[2:3, :]).astype(BF16)
        dout_ref[...] = dout
        dmg = _nt(dout, _full(wo_ref))
        sa = _sigmoid(ga_ref[...])
        sb = _sigmoid(gb_ref[...])
        dpa = (dmg * sa).astype(BF16)
        dpb = (dmg * sb).astype(BF16)
        dpa_ref[...] = dpa
        dpb_ref[...] = dpb
        dgate_ref[:, D:2 * D] = (dmg * pa_ref[...] * sa * (1.0 - sa)).astype(BF16)
        dgate_ref[:, 2 * D:3 * D] = (dmg * pb_ref[...] * sb * (1.0 - sb)).astype(BF16)
        dya_ref[...] = _nt(dpa, _full(wa_ref))
        dyb = _nt(dpb, _full(wb_ref))
        so, dso = _silu_both(og_ref[...])
        ov = o_ref[...]
        nw = hnw_ref[...]
        dnw = jnp.zeros((1, HEAD), F32)
        for h in range(nh):
            sl = slice(h * HEAD, (h + 1) * HEAD)
            seg = ov[:, sl]
            r = lax.rsqrt(jnp.mean(seg * seg, axis=-1, keepdims=True) + RMS_EPS)
            oh = seg * r
            dn = dyb[:, sl] * so[:, sl]
            dgate_ref[:, sl] = (dyb[:, sl] * oh * nw * dso[:, sl]).astype(BF16)
            dnw = dnw + jnp.sum(dn * oh, axis=0, keepdims=True)
            doh = dn * nw
            do_ref[:, sl] = r * (doh - oh * jnp.mean(doh * oh, axis=-1, keepdims=True))
        dhnw_ref[...] += dnw

    row = lambda col: pl.BlockSpec((TBT, D), lambda i: (i, col))
    wsp = _rows_weight_spec(wa)
    sd = lambda dt: jax.ShapeDtypeStruct((T, D), dt)
    return pl.pallas_call(
        body, name=name, grid=(T // TBT,),
        in_specs=[row(0), row(0), row(0), row(0), row(0), row(6), row(7), row(8),
                  pl.BlockSpec((None, N_MOD, D), lambda i: (_stream_of(i, cb), 0, 0)),
                  pl.BlockSpec((1, HEAD), lambda i: (0, 0)), wsp, wsp, wsp],
        out_specs=[row(0)] * 3 + [pl.BlockSpec((TBT, 3 * D), lambda i: (i, 2)), row(0), row(0),
                                  pl.BlockSpec((None, 1, D), lambda i: (_stream_of(i, cb), 0, 0)),
                                  pl.BlockSpec((1, HEAD), lambda i: (0, 0))],
        out_shape=[sd(BF16)] * 3 + [jax.ShapeDtypeStruct((T, 9 * D), BF16), sd(F32), sd(F32),
                                    jax.ShapeDtypeStruct((2, 1, D), F32), jax.ShapeDtypeStruct((1, HEAD), F32)],
        compiler_params=_params("arbitrary", vmem=VMEM_LIMIT_TOKEN_OUT),
    )(dx, tmo, pa, pb, o, parts, parts, parts, mod, hnw, wa, wb, wo)


def _conv_geometry(i, nb, cb):
    is_ctx = i < cb
    first = (i == 0) | (i == cb)
    last = (i == cb - 1) | (i == nb - 1)
    row = lax.broadcasted_iota(jnp.int32, (TB + 2 * GRID_W, 1), 0)
    w = row & (GRID_W - 1)
    left_ok = (w != 0) | is_ctx
    right_ok = (w != GRID_W - 1) | is_ctx
    return is_ctx, first, last, left_ok, right_ok


def _ext(p_ref, m_ref, n_ref, first, last):
    return jnp.concatenate([jnp.where(first, 0.0, p_ref[...]), m_ref[...], jnp.where(last, 0.0, n_ref[...])], axis=0)


def _shift_prev(e, ok):
    return jnp.where(ok, pltpu.roll(e, 1, 0), 0.0)


def _shift_next(e, ok):
    return jnp.where(ok, pltpu.roll(e, e.shape[0] - 1, 0), 0.0)


def _halo_specs(cbk, n64, coff=0):
    r = TB // GRID_W
    prev = pl.BlockSpec((GRID_W, cbk), lambda j, i: (jnp.maximum(r * i - 1, 0), j + coff))
    main = pl.BlockSpec((TB, cbk), lambda j, i: (i, j + coff))
    nxt = pl.BlockSpec((GRID_W, cbk), lambda j, i: (jnp.minimum(r * i + r, n64 - 1), j + coff))
    return [prev, main, nxt]


def _conv_cblock(dff):
    return _tile(dff, 1408)


def _conv_fwd(up, cw, cbias, ctx_rows, name):
    T, dff = up.shape[0], up.shape[1] // 2
    cbk = _conv_cblock(dff)
    nb, cb = T // TB, ctx_rows // TB
    nvb = dff // cbk

    def body(ap_ref, a_ref, an_ref, v_ref, cw_ref, cb_ref, ac_ref, act_ref):
        i = pl.program_id(1)
        is_ctx, first, last, lok, rok = _conv_geometry(i, nb, cb)
        e = _ext(ap_ref, a_ref, an_ref, first, last)
        el = _shift_prev(e, lok)
        er = _shift_next(e, rok)
        cwv = cw_ref[...]

        def comb(dr, lo):
            sl = slice(lo, lo + TB)
            return cwv[3 * dr:3 * dr + 1] * el[sl] + cwv[3 * dr + 1:3 * dr + 2] * e[sl] + cwv[3 * dr + 2:3 * dr + 3] * er[sl]

        out = comb(1, GRID_W) + jnp.where(is_ctx, 0.0, comb(0, 0) + comb(2, 2 * GRID_W))
        a_c = out + cb_ref[...]
        ac_ref[...] = a_c
        act_ref[...] = (_gelu(a_c) * v_ref[...]).astype(BF16)

    main = pl.BlockSpec((TB, cbk), lambda j, i: (i, j))
    return pl.pallas_call(
        body, name=name, grid=(dff // cbk, nb),
        in_specs=_halo_specs(cbk, T // GRID_W) + [pl.BlockSpec((TB, cbk), lambda j, i: (i, j + nvb)),
                                                 pl.BlockSpec((9, cbk), lambda j, i: (0, j)),
                                                 pl.BlockSpec((1, cbk), lambda j, i: (0, j))],
        out_specs=[main, main],
        out_shape=[jax.ShapeDtypeStruct((T, dff), F32), jax.ShapeDtypeStruct((T, dff), BF16)],
        compiler_params=_params("parallel", "parallel"),
    )(up, up, up, up, cw, cbias)


def _conv_bwd(up, ac, dact, cw, ctx_rows, name):
    T, dff = up.shape[0], up.shape[1] // 2
    cbk = _conv_cblock(dff)
    nb, cb = T // TB, ctx_rows // TB
    nvb = dff // cbk

    def body(ap_ref, a_ref, an_ref, vp_ref, v_ref, vn_ref, cp_ref, c_ref, cn_ref, dp_ref, d_ref, dn_ref, cw_ref,
             da_ref, dv_ref, dcw_ref, dcb_ref):
        i = pl.program_id(1)

        @pl.when(i == 0)
        def _():
            dcw_ref[...] = jnp.zeros_like(dcw_ref)
            dcb_ref[...] = jnp.zeros_like(dcb_ref)

        is_ctx, first, last, lok, rok = _conv_geometry(i, nb, cb)
        gl, dgl = _gelu_both(_ext(cp_ref, c_ref, cn_ref, first, last))
        g = _ext(dp_ref, d_ref, dn_ref, first, last) * _ext(vp_ref, v_ref, vn_ref, first, last) * dgl
        dv_ref[...] = (d_ref[...] * gl[GRID_W:GRID_W + TB]).astype(BF16)
        gm = _shift_prev(g, lok)
        gp = _shift_next(g, rok)
        cwv = cw_ref[...]

        def comb(dr, lo):
            sl = slice(lo, lo + TB)
            return cwv[3 * dr:3 * dr + 1] * gp[sl] + cwv[3 * dr + 1:3 * dr + 2] * g[sl] + cwv[3 * dr + 2:3 * dr + 3] * gm[sl]

        da = comb(1, GRID_W) + jnp.where(is_ctx, 0.0, comb(0, 2 * GRID_W) + comb(2, 0))
        da_ref[...] = da.astype(BF16)
        e = _ext(ap_ref, a_ref, an_ref, first, last)
        taps = [_shift_prev(e, lok), e, _shift_next(e, rok)]
        gmain = g[GRID_W:GRID_W + TB]
        dcb_ref[...] += jnp.sum(gmain, axis=0, keepdims=True)
        vert = jnp.where(is_ctx, 0.0, 1.0)
        for dr in range(3):
            sl = slice(dr * GRID_W, dr * GRID_W + TB)
            for dw in range(3):
                s = jnp.sum(gmain * taps[dw][sl], axis=0, keepdims=True)
                if dr != 1:
                    s = s * vert
                k = 3 * dr + dw
                dcw_ref[k:k + 1, :] += s

    main = pl.BlockSpec((TB, cbk), lambda j, i: (i, j))
    halo = _halo_specs(cbk, T // GRID_W)
    acc9 = pl.BlockSpec((9, cbk), lambda j, i: (0, j))
    acc1 = pl.BlockSpec((1, cbk), lambda j, i: (0, j))
    return pl.pallas_call(
        body, name=name, grid=(dff // cbk, nb),
        in_specs=halo + _halo_specs(cbk, T // GRID_W, nvb) + halo + halo + [acc9],
        out_specs=[main, main, acc9, acc1],
        out_shape=[jax.ShapeDtypeStruct((T, dff), BF16), jax.ShapeDtypeStruct((T, dff), BF16),
                   jax.ShapeDtypeStruct((9, dff), F32), jax.ShapeDtypeStruct((1, dff), F32)],
        compiler_params=_params("parallel", "arbitrary"),
    )(up, up, up, up, up, up, ac, ac, ac, dact, dact, dact, cw)


def _ffn_out_fwd(act, xm, mod, wd, ctx_rows, name):
    T, D = xm.shape
    dff = act.shape[1]
    cb = ctx_rows // TB

    def body(act_ref, x_ref, mod_ref, w_ref, xo_ref, fo_ref):
        out = _nn(act_ref[...], _full(w_ref))
        fo_ref[...] = out
        xo_ref[...] = x_ref[...] + mod_ref[5:6, :] * out

    row = pl.BlockSpec((TB, D), lambda i: (i, 0))
    return pl.pallas_call(
        body, name=name, grid=(T // TB,),
        in_specs=[pl.BlockSpec((TB, dff), lambda i: (i, 0)), row,
                  pl.BlockSpec((None, N_MOD, D), lambda i: (_stream_of(i, cb), 0, 0)),
                  _rows_weight_spec(wd)],
        out_specs=[row, row],
        out_shape=[jax.ShapeDtypeStruct((T, D), F32), jax.ShapeDtypeStruct((T, D), F32)],
        compiler_params=_params("parallel"),
    )(act, xm, mod, wd)


def _ffn_out_bwd(dx, fo, mod, wd, ctx_rows, name):
    T, D = dx.shape
    dff = N_CHIPS * wd.shape[1]
    cb = ctx_rows // TB

    def body(dx_ref, fo_ref, mod_ref, w_ref, dout_ref, dact_ref, dg2_ref):
        i = pl.program_id(0)

        @pl.when((i == 0) | (i == cb))
        def _():
            dg2_ref[...] = jnp.zeros_like(dg2_ref)

        dxv = dx_ref[...]
        dg2_ref[...] += jnp.sum(dxv * fo_ref[...], axis=0, keepdims=True)
        dout = (dxv * mod_ref[5:6, :]).astype(BF16)
        dout_ref[...] = dout
        dact_ref[...] = _nt(dout, _full(w_ref))

    row = pl.BlockSpec((TB, D), lambda i: (i, 0))
    return pl.pallas_call(
        body, name=name, grid=(T // TB,),
        in_specs=[row, row, pl.BlockSpec((None, N_MOD, D), lambda i: (_stream_of(i, cb), 0, 0)),
                  _rows_weight_spec(wd)],
        out_specs=[row, pl.BlockSpec((TB, dff), lambda i: (i, 0)),
                   pl.BlockSpec((None, 1, D), lambda i: (_stream_of(i, cb), 0, 0))],
        out_shape=[jax.ShapeDtypeStruct((T, D), BF16), jax.ShapeDtypeStruct((T, dff), F32),
                   jax.ShapeDtypeStruct((2, 1, D), F32)],
        compiler_params=_params("arbitrary"),
    )(dx, fo, mod, wd)


def _loss_bwd(x, target, fw, ctx_rows, name):
    T, D = x.shape
    cb = ctx_rows // TB

    def body(x_ref, t_ref, fw_ref, dx_ref, loss_ref, dfw_ref):
        i = pl.program_id(0)

        @pl.when(i == 0)
        def _():
            loss_ref[...] = jnp.zeros_like(loss_ref)
            dfw_ref[...] = jnp.zeros_like(dfw_ref)

        @pl.when(i < cb)
        def _():
            dx_ref[...] = jnp.zeros_like(dx_ref)

        @pl.when(i >= cb)
        def _():
            xv = x_ref[...]
            r = lax.rsqrt(jnp.mean(xv * xv, axis=-1, keepdims=True) + RMS_EPS)
            xh = xv * r
            fwv = fw_ref[...]
            err = xh * fwv - t_ref[...]
            loss_ref[...] += (0.5 / D) * jnp.sum(err * err)
            dy = err * (1.0 / D)
            dfw_ref[...] += jnp.sum(dy * xh, axis=0, keepdims=True)
            dxh = dy * fwv
            dx_ref[...] = r * (dxh - xh * jnp.mean(dxh * xh, axis=-1, keepdims=True))

    row = pl.BlockSpec((TB, D), lambda i: (i, 0))
    return pl.pallas_call(
        body, name=name, grid=(T // TB,),
        in_specs=[row, pl.BlockSpec((TB, D), lambda i: (jnp.maximum(i - cb, 0), 0)), pl.BlockSpec((1, D), lambda i: (0, 0))],
        out_specs=[row, pl.BlockSpec((1, 128), lambda i: (0, 0)), pl.BlockSpec((1, D), lambda i: (0, 0))],
        out_shape=[jax.ShapeDtypeStruct((T, D), F32), jax.ShapeDtypeStruct((1, 128), F32),
                   jax.ShapeDtypeStruct((1, D), F32)],
        compiler_params=_params("arbitrary"),
    )(x, target, fw)


def _adamw(w, gs, m, v, name):
    L, R, C = w.shape
    assert len(gs) == L
    rb = _rows_tile(R, max(16, (1 << 18) // C // 16 * 16))
    bc1 = 1.0 - ADAM_B1 ** ADAM_STEP
    bc2 = 1.0 - ADAM_B2 ** ADAM_STEP

    def body(w_ref, m_ref, v_ref, *rest):
        g_refs, (g_ref, d_ref, nm_ref, nv_ref) = rest[:L], rest[L:]
        layer = pl.program_id(0)
        for li in range(L):
            @pl.when(layer == li)
            def _():
                gv = g_refs[li][...]
                g_ref[...] = gv
                nm = ADAM_B1 * m_ref[...] + (1.0 - ADAM_B1) * gv
                nv = ADAM_B2 * v_ref[...] + (1.0 - ADAM_B2) * (gv * gv)
                nm_ref[...] = nm
                nv_ref[...] = nv
                d_ref[...] = -ADAM_LR * ((nm / bc1) / (jnp.sqrt(nv / bc2) + ADAM_EPS) + ADAM_WD * w_ref[...])

    blk = pl.BlockSpec((None, rb, C), lambda l, i: (l, i, 0))
    gblk = pl.BlockSpec((rb, C), lambda l, i: (i, 0))
    sd = jax.ShapeDtypeStruct((L, R, C), F32)
    return pl.pallas_call(
        body, name=name, grid=(L, R // rb), in_specs=[blk] * 3 + [gblk] * L, out_specs=[blk] * 4, out_shape=[sd] * 4,
        compiler_params=_params("parallel", "parallel"),
    )(w, m, v, *gs)


def _local_step(xs, cv, target, W, layer_weights, on_layer_grads, ctx_rows):
    T, D = xs.shape
    depth = W["norm1_w"].shape[0]
    saved = []
    X = xs
    for l in range(depth):
        s = {}
        Wl = layer_weights(l, X)
        mod_all, sa = _mod_fwd(cv, Wl["ada_w"], W["ada_b"][l][None, :] + Wl["token"], f"mod_fwd_{l}")
        mod = mod_all[:2].reshape(2, N_MOD, D)
        h1 = _norm_mod(X, W["norm1_w"][l][None, :], mod, 0, ctx_rows, f"norm1_{l}")
        parts = _mm_nn_w(h1, Wl["w_in"], F32, f"in_proj_{l}")
        o_f, st_f = _hgrn_fwd(parts, W["hlb"], l, False, ctx_rows, f"hgrn_fwd_f_{l}")
        o, st_b = _hgrn_fwd(parts, W["hlb"], l, True, ctx_rows, f"hgrn_fwd_b_{l}", o_add=o_f)
        ya = _sgu_fwd(parts, W["sgu_ln_w"][l][None, :], W["sgu_ln_b"][l][None, :], W["sgu_w"][l], W["sgu_bt"][l],
                      f"sgu_fwd_{l}")
        Wl.update(Wl.pop("late")(ya))
        yb, pa, pb, mg, tmo, xm = _token_out_fwd(o, parts, ya, X, mod, W["hnw"][l][None, :] + Wl["late_token"], Wl["w_a"],
                                                 Wl["w_b"], Wl["w_o"], ctx_rows, f"token_out_fwd_{l}")
        h2 = _norm_mod(xm, W["norm2_w"][l][None, :], mod, 3, ctx_rows, f"norm2_{l}")
        up = _mm_nn_w(h2, Wl["w_up"], F32, f"up_proj_{l}")
        ac, act = _conv_fwd(up, Wl["conv_w"], W["conv_b"][l][None, :], ctx_rows, f"conv_fwd_{l}")
        xo, fo = _ffn_out_fwd(act, xm, mod, Wl["w_down"], ctx_rows, f"ffn_out_fwd_{l}")
        s.update(X=X, Wl=Wl, mod=mod, mod_all=mod_all, sa=sa, h1=h1, parts=parts, o=o, st_f=st_f, st_b=st_b, ya=ya, yb=yb,
                 pa=pa, pb=pb, mg=mg, tmo=tmo, xm=xm, h2=h2, up=up, ac=ac, act=act, fo=fo)
        saved.append(s)
        X = xo

    dX, loss_row, dfw = _loss_bwd(X, target, W["final_norm_w"][None, :], ctx_rows, "loss_bwd")
    G = {k: [None] * depth for k in ("ada_b", "norm1_w", "sgu_ln_w", "sgu_ln_b", "sgu_w", "sgu_b", "hlb1", "hnw", "norm2_w",
                                     "conv_w", "conv_b", "dmod")}
    dcv = jnp.zeros_like(cv)
    for l in reversed(range(depth)):
        s = saved[l]
        mod, Wl = s["mod"], s["Wl"]
        big = {}
        dout2, dact, dg2 = _ffn_out_bwd(dX, s["fo"], mod, Wl["w_down"], ctx_rows, f"ffn_out_bwd_{l}")
        big["w_down"] = _mm_tn(s["act"], dout2, F32, f"dw_down_{l}")
        da, dv, dcw, dcb = _conv_bwd(s["up"], s["ac"], dact, Wl["conv_w"], ctx_rows, f"conv_bwd_{l}")
        G["conv_w"][l], G["conv_b"][l] = dcw, dcb[0]
        dup = jnp.concatenate([da, dv], axis=1)
        big["w_up"] = _mm_tn(s["h2"], dup, F32, f"dw_up_{l}", out_chips=True)
        dh2 = _mm_nt_w(dup, Wl["w_up"], F32, f"dh2_{l}")
        dxm, dm2, dnw2 = _norm_mod_bwd(dh2, s["xm"], dX, W["norm2_w"][l][None, :], mod, 3, ctx_rows, f"norm2_bwd_{l}")
        G["norm2_w"][l] = dnw2[0]
        (dout1, dpa, dpb, dparts, dya, do, dg1, dhnw) = _token_out_bwd(
            dxm, s["tmo"], s["pa"], s["pb"], s["o"], s["parts"], mod, W["hnw"][l][None, :], Wl["w_a"], Wl["w_b"], Wl["w_o"],
            ctx_rows, f"token_out_bwd_{l}")
        G["hnw"][l] = dhnw[0]
        big["w_o"] = _mm_tn(s["mg"], dout1, F32, f"dw_o_{l}")
        big["w_a"] = _mm_tn(s["ya"], dpa, F32, f"dw_a_{l}")
        big["w_b"] = _mm_tn(s["yb"], dpb, F32, f"dw_b_{l}")
        tok = on_layer_grads(l, "early", big)
        dparts, dsw, dsbt, dlnw, dlnb = _sgu_bwd(s["parts"], dya, W["sgu_ln_w"][l][None, :], W["sgu_ln_b"][l][None, :] + tok,
                                                 W["sgu_w"][l], W["sgu_bt"][l], dparts, f"sgu_bwd_{l}")
        G["sgu_w"][l], G["sgu_b"][l], G["sgu_ln_w"][l], G["sgu_ln_b"][l] = dsw, dsbt.T, dlnw[0], dlnb[0]
        dq_f, dz_f, di_f, dlb_f = _hgrn_bwd(s["parts"], W["hlb"], do, s["st_f"], l, False, ctx_rows, f"hgrn_bwd_f_{l}")
        dparts, dlb_b = _hgrn_bwd(s["parts"], W["hlb"], do, s["st_b"], l, True, ctx_rows, f"hgrn_bwd_b_{l}",
                                  other=(dq_f, dz_f, di_f), dparts=dparts)
        G["hlb1"][l] = jnp.concatenate([dlb_f[0], dlb_b[0]])
        tok = on_layer_grads(l, "late", {"w_in": _mm_tn(s["h1"], dparts, F32, f"dw_in_{l}", out_chips=True)})
        dh1 = _mm_nt_w(dparts, Wl["w_in"], F32, f"dh1_{l}")
        tok = tok + on_layer_grads(l, "end", {"after": dh1})
        dX, dm1, dnw1 = _norm_mod_bwd(dh1, s["X"], dxm, W["norm1_w"][l][None, :] + tok, mod, 0, ctx_rows, f"norm1_bwd_{l}")
        G["norm1_w"][l] = dnw1[0]
        dmod = jnp.concatenate([dm1, dg1, dm2, dg2], axis=1).reshape(2, N_MOD * D)
        dmod16 = jnp.concatenate([dmod, jnp.zeros((cv.shape[0] - 2, N_MOD * D), F32)], axis=0)
        G["ada_b"][l] = dmod[0] + dmod[1]
        G["dmod"][l] = dmod
        dcv = dcv + _cvec_bwd(dmod16, Wl["ada_w"], cv, f"dcvec_{l}")
    G["c_ctx"] = dcv[0]
    G["final_norm_w"] = dfw[0]
    return loss_row[0, 0], dX, G, saved[0]["sa"]


def _chip_peers(x, y, c):
    return [((1 - x, y, c), 2 * (1 - x) + y), ((x, 1 - y, c), 2 * x + 1 - y), ((1 - x, 1 - y, c), 2 * (1 - x) + 1 - y)]


def _rdma_call(ins, out_shapes, plan, n_remote, n_local, name, aliases=None):
    n_in, n_out = len(ins), len(out_shapes)

    def body(*refs):
        in_refs, out_refs = refs[:n_in], refs[n_in:n_in + n_out]
        send_sems, recv_sems, local_sems = refs[n_in + n_out:]
        x, y, c = lax.axis_index("x"), lax.axis_index("y"), lax.axis_index("c")
        remote, local = plan(in_refs, out_refs, x, y, c)
        assert len(remote) == n_remote and len(local) == n_local, (name, len(remote), len(local))
        copies = [pltpu.make_async_copy(s, d, local_sems.at[i]) for i, (s, d) in enumerate(local)]
        copies += [pltpu.make_async_remote_copy(src_ref=s, dst_ref=d, send_sem=send_sems.at[k], recv_sem=recv_sems.at[k],
                                                device_id=dev, device_id_type=pl.DeviceIdType.MESH)
                   for k, (s, d, dev) in enumerate(remote)]
        for cp in copies:
            cp.start()
        for cp in copies:
            cp.wait()

    hbm = pl.BlockSpec(memory_space=pltpu.HBM)
    return pl.pallas_call(
        body, name=name, in_specs=[hbm] * n_in, out_specs=[hbm] * n_out, out_shape=out_shapes,
        scratch_shapes=[pltpu.SemaphoreType.DMA((n_remote,)), pltpu.SemaphoreType.DMA((n_remote,)),
                        pltpu.SemaphoreType.DMA((max(n_local, 1),))],
        input_output_aliases=aliases or {},
    )(*ins)


DMA_PIECE_BYTES = 1 << 18
DMA_MAX_PIECES = 8


def _row_pieces(shape, dtype):
    rows = shape[0]
    row_bytes = jnp.dtype(dtype).itemsize
    for d in shape[1:]:
        row_bytes *= d
    n = 1
    while n < DMA_MAX_PIECES and rows % (2 * n * 16) == 0 and rows * row_bytes // (2 * n) >= DMA_PIECE_BYTES:
        n *= 2
    return [(i * (rows // n), rows // n) for i in range(n)]


def _half_pieces(o, c):
    r2 = o.shape[1] // 2
    return [pl.ds(c * r2 + st, sz) for st, sz in _row_pieces((r2,) + o.shape[2:], o.dtype)]


def _n_half_pieces(arrays):
    return sum(len(_row_pieces((a.shape[1] // 2,) + a.shape[2:], a.dtype)) for a in arrays)


def _plan_gather_far(lands, x, y, c):
    me = 2 * x + y
    return [(o.at[me, rows], o.at[me, rows], dev) for dev, _ in _chip_peers(x, y, c) for o in lands
            for rows in _half_pieces(o, c)]


def _plan_gather_near(lands, x, y, c):
    return [(o.at[idx, rows], o.at[idx, rows], (x, y, 1 - c)) for _, idx in _chip_peers(x, y, c) for o in lands
            for rows in _half_pieces(o, c)]


def _gather_weights(lands, name):
    n = len(lands)
    n_far = (N_CHIPS - 1) * _n_half_pieces(lands)

    def body(*refs):
        outs = refs[n:2 * n]
        far_send, far_recv, near_send, near_recv = refs[2 * n:]
        x, y, c = lax.axis_index("x"), lax.axis_index("y"), lax.axis_index("c")
        mk = lambda plan, send, recv: [
            pltpu.make_async_remote_copy(src_ref=s, dst_ref=d, send_sem=send.at[k], recv_sem=recv.at[k], device_id=dev,
                                         device_id_type=pl.DeviceIdType.MESH)
            for k, (s, d, dev) in enumerate(plan(outs, x, y, c))]
        far, near = mk(_plan_gather_far, far_send, far_recv), mk(_plan_gather_near, near_send, near_recv)
        assert len(far) == n_far and len(near) == n_far
        for cp in far:
            cp.start()
        for k in range(n_far):
            far[k].wait_recv()
            near[k].start()
        for k in range(n_far):
            near[k].wait_recv()
        for cp in far + near:
            cp.wait_send()

    hbm = pl.BlockSpec(memory_space=pltpu.HBM)
    sems = pltpu.SemaphoreType.DMA((n_far,))
    return pl.pallas_call(
        body, name=name, in_specs=[hbm] * n, out_specs=[hbm] * n,
        out_shape=[jax.ShapeDtypeStruct(a.shape, a.dtype) for a in lands],
        scratch_shapes=[sems, sems, sems, sems], input_output_aliases={i: i for i in range(n)},
    )(*lands)


def _gather_all(v, name):
    def plan(ins, outs, x, y, c):
        (s,), (o,) = ins, outs
        me = 4 * x + 2 * y + c
        flip = lambda a, f: 1 - a if f else a
        remote = [(s, o.at[me], (flip(x, m & 4), flip(y, m & 2), flip(c, m & 1))) for m in range(1, 8)]
        return remote, [(s, o.at[me])]

    return _rdma_call([v], [jax.ShapeDtypeStruct((8,) + v.shape, v.dtype)], plan, 7, 1, name)[0]


def _plan_pair(ins, lands, x, y, c):
    return [(a.at[j, 1 - c, pl.ds(st, sz)], o.at[j, pl.ds(st, sz)], (x, y, 1 - c)) for a, o in zip(ins, lands)
            for j in range(N_CHIPS) for st, sz in _row_pieces(a.shape[2:], a.dtype)]


def _n_pair_copies(parts):
    return N_CHIPS * sum(len(_row_pieces(a.shape[2:], a.dtype)) for a in parts)


def _reduce_pair(parts, name):
    shapes = [jax.ShapeDtypeStruct((N_CHIPS,) + a.shape[2:], a.dtype) for a in parts]
    return _rdma_call(parts, shapes, lambda ins, outs, x, y, c: (_plan_pair(ins, outs, x, y, c), []),
                      _n_pair_copies(parts), 0, name)


def _plan_chips(ins, lands, x, y, c):
    me = 2 * x + y
    return [(a.at[idx, pl.ds(st, sz)], o.at[me, pl.ds(st, sz)], dev) for dev, idx in _chip_peers(x, y, c)
            for a, o in zip(ins, lands) for st, sz in _row_pieces(a.shape[1:], a.dtype)]


def _n_chips_copies(parts):
    return (N_CHIPS - 1) * sum(len(_row_pieces(a.shape[1:], a.dtype)) for a in parts)


def _reduce_chips(parts, name):
    shapes = [jax.ShapeDtypeStruct(a.shape, a.dtype) for a in parts]
    return _rdma_call(parts, shapes, lambda ins, outs, x, y, c: (_plan_chips(ins, outs, x, y, c), []),
                      _n_chips_copies(parts), 0, name)


def _gather_pair(halves, name):
    def plan(ins, outs, x, y, c):
        return [(o.at[c, pl.ds(st, sz)], o.at[c, pl.ds(st, sz)], (x, y, 1 - c)) for o in outs
                for st, sz in _row_pieces(o.shape[1:], o.dtype)], []

    shapes = [jax.ShapeDtypeStruct(a.shape, a.dtype) for a in halves]
    n_remote = sum(len(_row_pieces(a.shape[1:], a.dtype)) for a in halves)
    return _rdma_call(halves, shapes, plan, n_remote, 0, name, aliases={i: i for i in range(len(halves))})


def _split_start(ins, lands, plan, n_remote, name):
    n_buf = len(ins) + len(lands)

    def body(*refs):
        in_refs, land_refs = refs[:len(ins)], refs[len(ins):n_buf]
        send_sems, recv_sems, token = refs[n_buf], refs[n_buf + 1], refs[-1]
        x, y, c = lax.axis_index("x"), lax.axis_index("y"), lax.axis_index("c")
        remote = plan(in_refs, land_refs, x, y, c)
        assert len(remote) == n_remote, (name, len(remote))
        for k, (s, d, dev) in enumerate(remote):
            pltpu.make_async_remote_copy(src_ref=s, dst_ref=d, send_sem=send_sems.at[k], recv_sem=recv_sems.at[k],
                                         device_id=dev, device_id_type=pl.DeviceIdType.MESH).start()
        token[...] = jnp.zeros_like(token)

    hbm = pl.BlockSpec(memory_space=pltpu.HBM)
    sem = pl.BlockSpec(memory_space=pltpu.SEMAPHORE)
    bufs = list(ins) + list(lands)
    out = pl.pallas_call(
        body, name=name, in_specs=[hbm] * n_buf,
        out_specs=(sem, sem) + (hbm,) * n_buf + (pl.BlockSpec(memory_space=pltpu.VMEM),),
        out_shape=(pltpu.SemaphoreType.DMA((n_remote,)), pltpu.SemaphoreType.DMA((n_remote,)))
        + tuple(pltpu.HBM(a.shape, a.dtype) for a in bufs) + (jax.ShapeDtypeStruct((8, 128), F32),),
        input_output_aliases={i: 2 + i for i in range(n_buf)},
        compiler_params=pltpu.CompilerParams(has_side_effects=pltpu.SideEffectType.DATAFLOW_SIDE_EFFECTING),
    )(*[pltpu.with_memory_space_constraint(a, pltpu.HBM) for a in bufs])
    return dict(send=out[0], recv=out[1], ins=list(out[2:2 + len(ins)]), lands=list(out[2 + len(ins):2 + n_buf]),
                token=out[-1][0, 0], token_array=out[-1], plan=plan, n_remote=n_remote)


def _split_wait(st, after, name):
    n_in, n_buf = len(st["ins"]), len(st["ins"]) + len(st["lands"])
    plan, n_remote = st["plan"], st["n_remote"]

    def body(*refs):
        in_refs, land_refs = refs[:n_in], refs[n_in:n_buf]
        send_sems, recv_sems = refs[n_buf], refs[n_buf + 1]
        x, y, c = lax.axis_index("x"), lax.axis_index("y"), lax.axis_index("c")
        for k, (s, d, dev) in enumerate(plan(in_refs, land_refs, x, y, c)):
            cp = pltpu.make_async_remote_copy(src_ref=s, dst_ref=d, send_sem=send_sems.at[k], recv_sem=recv_sems.at[k],
                                              device_id=dev, device_id_type=pl.DeviceIdType.MESH)
            cp.wait_send()
            cp.wait_recv()

    hbm = pl.BlockSpec(memory_space=pltpu.HBM)
    sem = pl.BlockSpec(memory_space=pltpu.SEMAPHORE)
    bufs = st["ins"] + st["lands"]
    out = pl.pallas_call(
        body, name=name, in_specs=[hbm] * n_buf + [sem, sem, pl.BlockSpec(memory_space=pl.ANY)],
        out_specs=[hbm] * n_buf, out_shape=[pltpu.HBM(a.shape, a.dtype) for a in bufs],
        input_output_aliases={i: i for i in range(n_buf)},
        compiler_params=pltpu.CompilerParams(has_side_effects=pltpu.SideEffectType.DATAFLOW_SIDE_EFFECTING),
    )(*bufs, st["send"], st["recv"], after)
    return list(out[:n_in]), list(out[n_in:])


def _pair_forward(lands, name):
    shapes = [jax.ShapeDtypeStruct(a.shape, a.dtype) for a in lands]
    return _rdma_call(lands, shapes, lambda ins, outs, x, y, c: (_plan_gather_near(outs, x, y, c), []),
                      (N_CHIPS - 1) * _n_half_pieces(lands), 0, name, aliases={i: i for i in range(len(lands))})


def _sum_block_rows(r, C):
    return _rows_tile(r, max(16, (1 << 18) // C // 16 * 16))


def _sum_pair(a, recv, cidx, name):
    nch, _, r, C = a.shape
    rb = _sum_block_rows(r, C)

    def body(c_ref, a_ref, r_ref, o_ref):
        o_ref[...] = (a_ref[...] + r_ref[...]).astype(BF16)

    blk = pl.BlockSpec((None, rb, C), lambda j, i, c: (j, i, 0))
    return pl.pallas_call(
        body, name=name,
        grid_spec=pltpu.PrefetchScalarGridSpec(
            num_scalar_prefetch=1, grid=(nch, r // rb),
            in_specs=[pl.BlockSpec((None, None, rb, C), lambda j, i, c: (j, c[0], i, 0)), blk], out_specs=blk),
        out_shape=jax.ShapeDtypeStruct((nch, r, C), BF16),
        compiler_params=_params("parallel", "parallel"),
    )(cidx, a, recv)


def _sum_chips(mine, recv, ids, name):
    nch, r, C = recv.shape
    rb = _sum_block_rows(r, C)

    def body(ids_ref, m_ref, *rest):
        r_refs, o_ref = rest[:nch], rest[nch]
        chip = ids_ref[1]
        own = m_ref[...].astype(F32)
        acc = jnp.where(chip == 0, own, r_refs[0][...].astype(F32))
        for q in range(1, nch):
            acc = acc + jnp.where(chip == q, own, r_refs[q][...].astype(F32))
        o_ref[...] = acc

    def slot(q):
        return pl.BlockSpec((None, rb, C), lambda i, ids: (jnp.where(ids[1] == q, (q + 1) % nch, q), i, 0))

    return pl.pallas_call(
        body, name=name,
        grid_spec=pltpu.PrefetchScalarGridSpec(
            num_scalar_prefetch=1, grid=(r // rb,),
            in_specs=[pl.BlockSpec((None, rb, C), lambda i, ids: (ids[1], i, 0))] + [slot(q) for q in range(nch)],
            out_specs=pl.BlockSpec((None, rb, C), lambda i, ids: (ids[0], i, 0))),
        out_shape=jax.ShapeDtypeStruct((N_CORES, r, C), F32),
        compiler_params=_params("parallel"),
    )(ids, mine, *([recv] * nch))


PACK_COLS = 1024
_SHARDED = ("ada_w", "w_in", "w_branch_a", "w_branch_b", "w_out", "ffn_w_up", "ffn_w_down")
_LAYER_KEYS = ("ada_w", "w_in", "w_a", "w_b", "w_o", "w_up", "w_down")
_SMALL = ("c_ctx", "ada_b", "norm1_w", "sgu_ln_w", "sgu_ln_b", "sgu_w", "sgu_b", "hgrn_lower_bounds", "hgrn_norm_w",
          "norm2_w", "ffn_conv_b", "final_norm_w")
_ORDER = ("c_ctx", "ada_w", "ada_b", "norm1_w", "w_in", "sgu_ln_w", "sgu_ln_b", "sgu_w", "sgu_b", "hgrn_lower_bounds",
          "hgrn_norm_w", "w_branch_a", "w_branch_b", "w_out", "norm2_w", "ffn_w_up", "ffn_conv_w", "ffn_conv_b",
          "ffn_w_down", "final_norm_w")


def _pad_to(v, n):
    return jnp.concatenate([v, jnp.zeros((n - v.shape[0],), v.dtype)]) if v.shape[0] < n else v


def _round_up(n, m):
    return (n + m - 1) // m * m


def _pack(arrays, n_pad):
    flat = jnp.concatenate([a.reshape(-1) for a in arrays])
    return _pad_to(flat, n_pad)


def _unpack(flat, like):
    out, off = [], 0
    for a in like:
        out.append(flat[off:off + a.size].reshape(a.shape))
        off += a.size
    return out


def kernel(x, c, ctx, c_ctx, ada_w, ada_b, norm1_w, w_in, sgu_ln_w, sgu_ln_b, sgu_w, sgu_b, hgrn_lower_bounds, hgrn_norm_w, w_branch_a, w_branch_b, w_out, norm2_w, ffn_w_up, ffn_conv_w, ffn_conv_b, ffn_w_down, final_norm_w, loss_target, m_c_ctx, m_ada_w, m_ada_b, m_norm1_w, m_w_in, m_sgu_ln_w, m_sgu_ln_b, m_sgu_w, m_sgu_b, m_hgrn_lower_bounds, m_hgrn_norm_w, m_w_branch_a, m_w_branch_b, m_w_out, m_norm2_w, m_ffn_w_up, m_ffn_conv_w, m_ffn_conv_b, m_ffn_w_down, m_final_norm_w, v_c_ctx, v_ada_w, v_ada_b, v_norm1_w, v_w_in, v_sgu_ln_w, v_sgu_ln_b, v_sgu_w, v_sgu_b, v_hgrn_lower_bounds, v_hgrn_norm_w, v_w_branch_a, v_w_branch_b, v_w_out, v_norm2_w, v_ffn_w_up, v_ffn_conv_w, v_ffn_conv_b, v_ffn_w_down, v_final_norm_w):
    w = dict(c_ctx=c_ctx, ada_w=ada_w, ada_b=ada_b, norm1_w=norm1_w, w_in=w_in, sgu_ln_w=sgu_ln_w, sgu_ln_b=sgu_ln_b,
             sgu_w=sgu_w, sgu_b=sgu_b, hgrn_lower_bounds=hgrn_lower_bounds, hgrn_norm_w=hgrn_norm_w, w_branch_a=w_branch_a,
             w_branch_b=w_branch_b, w_out=w_out, norm2_w=norm2_w, ffn_w_up=ffn_w_up, ffn_conv_w=ffn_conv_w,
             ffn_conv_b=ffn_conv_b, ffn_w_down=ffn_w_down, final_norm_w=final_norm_w)
    mom = dict(zip(_ORDER, (m_c_ctx, m_ada_w, m_ada_b, m_norm1_w, m_w_in, m_sgu_ln_w, m_sgu_ln_b, m_sgu_w, m_sgu_b,
                            m_hgrn_lower_bounds, m_hgrn_norm_w, m_w_branch_a, m_w_branch_b, m_w_out, m_norm2_w, m_ffn_w_up,
                            m_ffn_conv_w, m_ffn_conv_b, m_ffn_w_down, m_final_norm_w)))
    var = dict(zip(_ORDER, (v_c_ctx, v_ada_w, v_ada_b, v_norm1_w, v_w_in, v_sgu_ln_w, v_sgu_ln_b, v_sgu_w, v_sgu_b,
                            v_hgrn_lower_bounds, v_hgrn_norm_w, v_w_branch_a, v_w_branch_b, v_w_out, v_norm2_w, v_ffn_w_up,
                            v_ffn_conv_w, v_ffn_conv_b, v_ffn_w_down, v_final_norm_w)))
    depth, D = norm1_w.shape
    dff = ffn_conv_b.shape[1]
    ctx_rows, seq = ctx.shape[1], x.shape[1]

    assert depth == 2, "the lower-bound softmax is written for two layers"
    core = lax.axis_index("c")
    chip = 2 * lax.axis_index("x") + lax.axis_index("y")
    ids = jnp.stack([core, chip]).astype(jnp.int32)

    first, rest = _LAYER_KEYS[:2], _LAYER_KEYS[2:]
    shard = lambda l, k: w[_SHARDED[_LAYER_KEYS.index(k)]][l].astype(BF16)
    started, conv_full = {}, []

    def landing(s):
        return lax.dynamic_update_slice(lax.empty((N_CHIPS,) + s.shape, s.dtype), s[None], (chip,) + (0,) * s.ndim)

    def start_gather(l, keys, tag):
        lands = [landing(shard(l, k)) for k in keys]
        started[tag] = _split_start([], lands, lambda ins, lds, x, y, c: _plan_gather_far(lds, x, y, c),
                                    (N_CHIPS - 1) * _n_half_pieces(lands), f"gather_start_{tag}")
        return started[tag]["token"]

    def finish_gather(keys, tag, after):
        _, lands = _split_wait(started[tag], after, f"gather_wait_{tag}")
        return dict(zip(keys, _pair_forward(lands, f"gather_forward_{tag}")))

    def layer_weights(l, after):
        if l == 0:
            got = _gather_weights([landing(shard(0, k)) for k in first] + [landing(ffn_conv_w)], "gather_weights_first")
            conv_full.append(jnp.transpose(got[-1], (1, 2, 3, 0, 4)).reshape(depth, 9, dff))
            out = dict(zip(first, got), token=start_gather(0, rest, "rest_0"))
        else:
            out = dict(finish_gather(first, f"first_{l}", after), token=0.0)

        def late(after_late):
            more = finish_gather(rest, f"rest_{l}", after_late)
            more["late_token"] = 0.0
            if l + 1 < depth:
                more["late_token"] = start_gather(l + 1, first, f"first_{l + 1}") + start_gather(l + 1, rest, f"rest_{l + 1}")
            return more

        return dict(out, conv_w=conv_full[0][l], late=late)

    groups, order = {}, []

    def as_parts(gs):
        return [g.reshape(N_CHIPS, N_CORES, g.size // (N_CHIPS * N_CORES * g.shape[-1]), g.shape[-1]) for g in gs]

    def pair_start(tag, l, keys, gs):
        parts = as_parts(gs)
        lands = [lax.empty((N_CHIPS,) + p.shape[2:], p.dtype) for p in parts]
        groups[tag] = dict(l=l, keys=keys, pair=_split_start(parts, lands, _plan_pair, _n_pair_copies(parts),
                                                             f"reduce_pair_start_{tag}"))
        order.append(tag)
        return groups[tag]["pair"]["token"]

    def chips_start(tag, after):
        parts, other = _split_wait(groups[tag]["pair"], after, f"reduce_pair_wait_{tag}")
        sums = [_sum_pair(a, o, ids, f"sum_pair_{tag}_{i}") for i, (a, o) in enumerate(zip(parts, other))]
        lands = [lax.empty(s.shape, s.dtype) for s in sums]
        groups[tag]["chips"] = _split_start(sums, lands, _plan_chips, _n_chips_copies(sums), f"reduce_chips_start_{tag}")
        return groups[tag]["chips"]["token"]

    def chips_finish(tag, after):
        sums, recv = _split_wait(groups[tag]["chips"], after, f"reduce_chips_wait_{tag}")
        return {(groups[tag]["l"], k): _sum_chips(sums[i], recv[i], ids, f"sum_chips_{tag}_{i}")
                for i, k in enumerate(groups[tag]["keys"])}

    def on_layer_grads(l, stage, gs):
        if stage == "early":
            return pair_start(f"early_{l}", l, list(gs), list(gs.values()))
        if stage == "late":
            return pair_start(f"late_{l}", l, ["w_in"], [gs["w_in"]]) + chips_start(f"early_{l}", gs["w_in"])
        return chips_start(f"late_{l}", gs["after"])

    W = dict(ada_b=ada_b, norm1_w=norm1_w, sgu_ln_w=sgu_ln_w, sgu_ln_b=sgu_ln_b, sgu_w=sgu_w.astype(BF16),
             sgu_bt=jnp.swapaxes(sgu_b, 1, 2), hlb=hgrn_lower_bounds, hnw=hgrn_norm_w, norm2_w=norm2_w, conv_b=ffn_conv_b,
             final_norm_w=final_norm_w)
    xs = jnp.concatenate([ctx[0], x[0]], axis=0)
    cv = jnp.concatenate([c_ctx[None, :], c, jnp.zeros((14, D), F32)], axis=0)
    loss_local, dxs, G, sa = _local_step(xs, cv, loss_target[0], W, layer_weights, on_layer_grads, ctx_rows)
    loss = lax.psum(loss_local, ("x", "y", "c"))
    grad_x = dxs[ctx_rows:][None]

    pad8 = lambda a: jnp.pad(a, ((0, 8 - a.shape[0]), (0, 0)))
    fact = jnp.concatenate([pad8(sa[1:2].astype(F32))] + [pad8(G["dmod"][l][1].reshape(N_MOD, D)) for l in range(depth)]
                           + [pad8(G["dmod"][l][0].reshape(N_MOD, D)) for l in range(depth)], axis=0)
    facts = _gather_all(fact, "gather_mod_factors")
    lhs = jnp.concatenate([facts[:, 0].astype(BF16), jnp.broadcast_to(sa[0:1], (8, D))], axis=0)
    ada_cols = N_MOD * D // N_CHIPS
    g_ada = []
    for l in range(depth):
        lo_x, lo_c = 8 * (1 + l), 8 * (1 + depth + l)
        rhs = jnp.concatenate([facts[:, lo_x:lo_x + N_MOD].reshape(8, N_MOD * D),
                               facts[:, lo_c:lo_c + N_MOD].reshape(8, N_MOD * D)], axis=0)
        rhs = lax.dynamic_slice_in_dim(rhs, chip * ada_cols, ada_cols, axis=1).astype(BF16)
        g_ada.append(_mm_tn(lhs, rhs, F32, f"dw_ada_{l}"))

    dh = G["hlb1"][depth - 1]
    small_like = [w[k] for k in _SMALL] + [jnp.zeros((depth, 9, dff), F32)]
    small = [G["c_ctx"], jnp.stack(G["ada_b"]), jnp.stack(G["norm1_w"]), jnp.stack(G["sgu_ln_w"]), jnp.stack(G["sgu_ln_b"]),
             jnp.stack(G["sgu_w"]), jnp.stack(G["sgu_b"]), jnp.stack([-dh, dh]), jnp.stack(G["hnw"]), jnp.stack(G["norm2_w"]),
             jnp.stack(G["conv_b"]), G["final_norm_w"], jnp.stack(G["conv_w"])]
    n_small = sum(a.size for a in small)
    n_small_pad = _round_up(n_small, N_CORES * 16 * PACK_COLS)
    small_rows = n_small_pad // (N_CORES * PACK_COLS)
    small_rep = jnp.broadcast_to(_pack(small, n_small_pad).reshape(1, N_CORES, small_rows, PACK_COLS),
                                 (N_CHIPS, N_CORES, small_rows, PACK_COLS))
    small_parts = as_parts([small_rep])
    small_sums = [_sum_pair(small_parts[0], _reduce_pair(small_parts, "reduce_pair_small")[0], ids, "sum_pair_small")]
    groups["small"] = dict(l=None, keys=["small"], chips=_split_start(
        small_sums, [lax.empty(small_sums[0].shape, small_sums[0].dtype)], _plan_chips, _n_chips_copies(small_sums),
        "reduce_chips_start_small"))

    def gather_halves(halves, name):
        return dict(zip(halves, _gather_pair(list(halves.values()), name)))

    last = order[-1]
    halves = {}
    for tag in order[:-1]:
        halves.update(chips_finish(tag, groups["small"]["chips"]["token_array"]))
    reduced = gather_halves(halves, "gather_pair")
    grads, delta, new_m, new_v = {}, {}, {}, {}

    def adamw_sharded(i):
        k = _SHARDED[i]
        gs = g_ada if i == 0 else [reduced[(l, _LAYER_KEYS[i])].reshape(w[k].shape[1:]) for l in range(depth)]
        grads[k], delta[k], new_m[k], new_v[k] = _adamw(w[k], gs, mom[k], var[k], f"adamw_{k}")

    last_keys = groups[last]["keys"]
    for i in range(len(_SHARDED)):
        if _LAYER_KEYS[i] not in last_keys:
            adamw_sharded(i)
    halves = chips_finish(last, new_v[_SHARDED[-1]])
    halves.update(chips_finish("small", new_v[_SHARDED[-1]]))
    reduced.update(gather_halves(halves, "gather_pair_last"))
    for i in range(len(_SHARDED)):
        if _LAYER_KEYS[i] in last_keys:
            adamw_sharded(i)

    g_small = _unpack(reduced[(None, "small")].reshape(-1), small_like)
    grads.update(zip(_SMALL, g_small[:-1]))
    grads["ffn_conv_w"] = lax.dynamic_slice_in_dim(g_small[-1].reshape(depth, 3, 3, dff), chip * (dff // N_CHIPS),
                                                   dff // N_CHIPS, axis=3)
    packed = _SMALL + ("ffn_conv_w",)
    n_pad = _round_up(sum(w[k].size for k in packed), 16 * PACK_COLS)
    pack = lambda t: _pack([t[k] for k in packed], n_pad).reshape(1, -1, PACK_COLS)
    _, d, nm, nv = _adamw(pack(w), [pack(grads)[0]], pack(mom), pack(var), "adamw_packed")
    like = [w[k] for k in packed]
    for src, dst in ((d, delta), (nm, new_m), (nv, new_v)):
        dst.update(zip(packed, _unpack(src.reshape(-1), like)))

    return (loss, grad_x, *[grads[k] for k in _ORDER], *[delta[k] for k in _ORDER], *[new_m[k] for k in _ORDER],
            *[new_v[k] for k in _ORDER])
```

```python
import functools

import jax
import jax.numpy as jnp
from jax import lax
from jax.experimental import pallas as pl
from jax.experimental.pallas import tpu as pltpu

F32 = jnp.float32
BF16 = jnp.bfloat16

GRID_W = 64
HG_CHUNK = 64
SGU_CHUNK = 128
HEAD = 128
TB = 256
N_MOD = 6
RMS_EPS = 1e-6
LN_EPS = 1e-5
VMEM_LIMIT = 48 * 1024 * 1024
N_CHIPS = 4
N_CORES = 2

ADAM_LR = 0.001
ADAM_B1 = 0.9
ADAM_B2 = 0.999
ADAM_EPS = 1e-08
ADAM_WD = 0.01
ADAM_STEP = 10

_GELU_C = 0.7978845608028654
_GELU_A = 0.044715


def _sigmoid(x):
    return 0.5 * jnp.tanh(0.5 * x) + 0.5


def _silu(x):
    return x * _sigmoid(x)


def _silu_both(x):
    s = _sigmoid(x)
    return x * s, s * (1.0 + x * (1.0 - s))


def _dsilu(x):
    return _silu_both(x)[1]


def _gelu_both(x):
    x2 = x * x
    t = jnp.tanh(_GELU_C * (x + _GELU_A * x2 * x))
    h = 0.5 * (1.0 + t)
    return x * h, h + 0.5 * x * (1.0 - t * t) * (_GELU_C + 3.0 * _GELU_C * _GELU_A * x2)


def _gelu(x):
    return 0.5 * x * (1.0 + jnp.tanh(_GELU_C * (x + _GELU_A * x * x * x)))


def _dgelu(x):
    return _gelu_both(x)[1]


def _dot(a, b, ca, cb):
    return lax.dot_general(a, b, (((ca,), (cb,)), ((), ())), preferred_element_type=F32)


def _nn(a, b):
    return _dot(a, b, 1, 0)


def _nt(a, b):
    return _dot(a, b, 1, 1)


def _tn(a, b):
    return _dot(a, b, 0, 0)


def _params(*sem, vmem=VMEM_LIMIT):
    return pltpu.CompilerParams(dimension_semantics=sem if sem else None, vmem_limit_bytes=vmem)


def _stream_of(i, ctx_blocks):
    return (i >= ctx_blocks).astype(jnp.int32)


def _mm(a, b, mode, tm, tn, tk, out_dtype, name, b_chips=False, out_chips=False):
    a_pair, b_pair = isinstance(a, tuple), isinstance(b, tuple)
    assert (not a_pair or mode == "nt") and (not b_pair or (mode == "tn" and not b_chips))
    ashape = (a[0].shape[0], 2 * a[0].shape[1]) if a_pair else a.shape
    if b_pair:
        bshape = (b[0].shape[0], 2 * b[0].shape[1])
    elif not b_chips:
        bshape = b.shape
    else:
        bshape = (b.shape[1], N_CHIPS * b.shape[2])
    if mode == "nn":
        (M, K), (K2, N) = ashape, bshape
    elif mode == "nt":
        (M, K), (N, K2) = ashape, bshape
    else:
        (K, M), (K2, N) = ashape, bshape
    assert K == K2 and M % tm == 0 and N % tn == 0 and K % tk == 0, (name, ashape, bshape, tm, tn, tk)
    nk = K // tk
    if a_pair:
        n1 = a[0].shape[1] // tk
        assert a[0].shape[1] % tk == 0
        a_specs = [pl.BlockSpec((tm, tk), lambda j, i, k: (i, jnp.minimum(k, n1 - 1))),
                   pl.BlockSpec((tm, tk), lambda j, i, k: (i, jnp.maximum(k - n1, 0)))]
    elif mode == "tn":
        a_specs = [pl.BlockSpec((tk, tm), lambda j, i, k: (k, i))]
    else:
        a_specs = [pl.BlockSpec((tm, tk), lambda j, i, k: (i, k))]
    if b_pair:
        n1 = b[0].shape[1] // tn
        assert b[0].shape[1] % tn == 0
        b_specs = [pl.BlockSpec((tk, tn), lambda j, i, k: (k, jnp.minimum(j, n1 - 1))),
                   pl.BlockSpec((tk, tn), lambda j, i, k: (k, jnp.maximum(j - n1, 0)))]
    elif not b_chips:
        if mode == "nt":
            b_spec = pl.BlockSpec((tn, tk), lambda j, i, k: (j, k))
        else:
            b_spec = pl.BlockSpec((tk, tn), lambda j, i, k: (k, j))
    else:
        cols = b.shape[2]
        if mode == "nn":
            per = cols // tn
            assert cols % tn == 0
            b_spec = pl.BlockSpec((None, tk, tn), lambda j, i, k: (j // per, k, j % per))
        else:
            per = cols // tk
            assert mode == "nt" and cols % tk == 0
            b_spec = pl.BlockSpec((None, tn, tk), lambda j, i, k: (k // per, j, k % per))
    if not b_pair:
        b_specs = [b_spec]
    if out_chips:
        per_o = (N // N_CHIPS) // tn
        assert (N // N_CHIPS) % tn == 0
        o_spec = pl.BlockSpec((None, tm, tn), lambda j, i, k: (j // per_o, i, j % per_o))
        o_shape = (N_CHIPS, M, N // N_CHIPS)
    else:
        o_spec = pl.BlockSpec((tm, tn), lambda j, i, k: (i, j))
        o_shape = (M, N)
    ca, cb = {"nn": (1, 0), "nt": (1, 1), "tn": (0, 0)}[mode]

    in_place = nk == 1 or out_dtype == F32
    na, nb = len(a_specs), len(b_specs)

    def body(*refs):
        a_refs, b_refs, rest = refs[:na], refs[na:na + nb], refs[na + nb:]
        if in_place:
            (o_ref,) = rest
        else:
            o_ref, acc = rest
        k = pl.program_id(2)
        av = a_refs[0][...] if not a_pair else jnp.where(k < n1, a_refs[0][...], a_refs[1][...])
        bv = b_refs[0][...] if not b_pair else jnp.where(pl.program_id(0) < n1, b_refs[0][...], b_refs[1][...])
        part = _dot(av, bv, ca, cb)
        if in_place:
            if nk == 1:
                o_ref[...] = part.astype(out_dtype)
            else:
                @pl.when(k == 0)
                def _():
                    o_ref[...] = part

                @pl.when(k > 0)
                def _():
                    o_ref[...] += part
            return

        @pl.when(k == 0)
        def _():
            acc[...] = jnp.zeros_like(acc)

        acc[...] += part

        @pl.when(k == nk - 1)
        def _():
            o_ref[...] = acc[...].astype(out_dtype)

    ins = (list(a) if a_pair else [a]) + (list(b) if b_pair else [b])
    return pl.pallas_call(
        body, name=name, grid=(N // tn, M // tm, nk), in_specs=a_specs + b_specs, out_specs=o_spec,
        out_shape=jax.ShapeDtypeStruct(o_shape, out_dtype),
        scratch_shapes=[] if in_place else [pltpu.VMEM((tm, tn), F32)],
        compiler_params=_params("parallel", "parallel", "arbitrary"),
    )(*ins)


def _tile(n, pref):
    if n <= pref:
        return n
    best = None
    for t in range(128, pref + 1, 128):
        if n % t == 0:
            best = t
    assert best is not None, (n, pref)
    return best


def _rows_tile(n, pref):
    if n <= pref:
        return n
    best = None
    for t in range(16, pref + 1, 16):
        if n % t == 0:
            best = t
    assert best is not None, (n, pref)
    return best


def _mm_nn_w(a, wg, out_dtype, name):
    M, K = a.shape
    return _mm(a, wg, "nn", _rows_tile(M, 2176), _tile(wg.shape[2], 1536), _tile(K, 1536), out_dtype, name, b_chips=True)


def _mm_nt_w(a, wg, out_dtype, name):
    M = a[0].shape[0] if isinstance(a, tuple) else a.shape[0]
    return _mm(a, wg, "nt", _rows_tile(M, 1088), _tile(wg.shape[1], 1024), _tile(wg.shape[2], 1536), out_dtype, name,
               b_chips=True)


def _mm_tn(a, b, out_dtype, name, out_chips=False):
    K, M = a.shape
    N = 2 * b[0].shape[1] if isinstance(b, tuple) else b.shape[1]
    ncol = N // N_CHIPS if out_chips else N
    tm, tn = _tile(M, 1408), _tile(ncol, 1408)
    if tm * tn > 1408 * 1152:
        tn = _tile(ncol, 1152)
    tk = _rows_tile(K, 1088 if isinstance(b, tuple) else 2176)
    return _mm(a, b, "tn", tm, tn, tk, out_dtype, name, out_chips=out_chips)


def _mod_fwd(cv, wg, b, name):
    R, D = cv.shape
    tn = wg.shape[2]
    N = N_CHIPS * tn

    def body(cv_ref, w_ref, b_ref, mod_ref, sa_ref):
        sa = _silu(cv_ref[...]).astype(BF16)
        sa_ref[...] = sa
        mod_ref[...] = _nn(sa, w_ref[...]) + b_ref[...]

    return pl.pallas_call(
        body, name=name, grid=(N_CHIPS,),
        in_specs=[pl.BlockSpec((R, D), lambda j: (0, 0)), pl.BlockSpec((None, D, tn), lambda j: (j, 0, 0)),
                  pl.BlockSpec((1, tn), lambda j: (0, j))],
        out_specs=[pl.BlockSpec((R, tn), lambda j: (0, j)), pl.BlockSpec((R, D), lambda j: (0, 0))],
        out_shape=[jax.ShapeDtypeStruct((R, N), F32), jax.ShapeDtypeStruct((R, D), BF16)],
        compiler_params=_params("arbitrary"),
    )(cv, wg, b)


def _cvec_bwd(dmod, wg, cv, name):
    R, N = dmod.shape
    D = wg.shape[1]
    tk = wg.shape[2]
    nk = N_CHIPS

    def body(dm_ref, w_ref, cv_ref, o_ref):
        k = pl.program_id(0)

        @pl.when(k == 0)
        def _():
            o_ref[...] = jnp.zeros_like(o_ref)

        o_ref[...] += _nt(dm_ref[...].astype(BF16), w_ref[...])

        @pl.when(k == nk - 1)
        def _():
            o_ref[...] = o_ref[...] * _dsilu(cv_ref[...])

    return pl.pallas_call(
        body, name=name, grid=(nk,),
        in_specs=[pl.BlockSpec((R, tk), lambda k: (0, k)), pl.BlockSpec((None, D, tk), lambda k: (k, 0, 0)),
                  pl.BlockSpec((R, D), lambda k: (0, 0))],
        out_specs=pl.BlockSpec((R, D), lambda k: (0, 0)),
        out_shape=jax.ShapeDtypeStruct((R, D), F32),
        compiler_params=_params("arbitrary"),
    )(dmod, wg, cv)


def _norm_mod(x, nw, mod, which, ctx_rows, name):
    T, D = x.shape
    cb = ctx_rows // TB

    def body(x_ref, nw_ref, mod_ref, h_ref):
        xv = x_ref[...]
        r = lax.rsqrt(jnp.mean(xv * xv, axis=-1, keepdims=True) + RMS_EPS)
        y = xv * r * nw_ref[...]
        sh = mod_ref[which:which + 1, :]
        sc = mod_ref[which + 1:which + 2, :]
        h_ref[...] = (y * (1.0 + sc) + sh).astype(BF16)

    return pl.pallas_call(
        body, name=name, grid=(T // TB,),
        in_specs=[pl.BlockSpec((TB, D), lambda i: (i, 0)), pl.BlockSpec((1, D), lambda i: (0, 0)),
                  pl.BlockSpec((None, N_MOD, D), lambda i: (_stream_of(i, cb), 0, 0))],
        out_specs=pl.BlockSpec((TB, D), lambda i: (i, 0)),
        out_shape=jax.ShapeDtypeStruct((T, D), BF16),
        compiler_params=_params("parallel"),
    )(x, nw, mod)


def _norm_mod_bwd(dh, x, dres, nw, mod, which, ctx_rows, name):
    T, D = x.shape
    cb = ctx_rows // TB

    def body(dh_ref, x_ref, dres_ref, nw_ref, mod_ref, dx_ref, dm_ref, dnw_ref):
        i = pl.program_id(0)

        @pl.when(i == 0)
        def _():
            dnw_ref[...] = jnp.zeros_like(dnw_ref)

        @pl.when((i == 0) | (i == cb))
        def _():
            dm_ref[...] = jnp.zeros_like(dm_ref)

        xv = x_ref[...]
        dh = dh_ref[...]
        r = lax.rsqrt(jnp.mean(xv * xv, axis=-1, keepdims=True) + RMS_EPS)
        xh = xv * r
        nwv = nw_ref[...]
        sc = mod_ref[which + 1:which + 2, :]
        y = xh * nwv
        dm_ref[0:1, :] += jnp.sum(dh, axis=0, keepdims=True)
        dm_ref[1:2, :] += jnp.sum(dh * y, axis=0, keepdims=True)
        dy = dh * (1.0 + sc)
        dnw_ref[...] += jnp.sum(dy * xh, axis=0, keepdims=True)
        dxh = dy * nwv
        dx_ref[...] = dres_ref[...] + r * (dxh - xh * jnp.mean(dxh * xh, axis=-1, keepdims=True))

    return pl.pallas_call(
        body, name=name, grid=(T // TB,),
        in_specs=[pl.BlockSpec((TB, D), lambda i: (i, 0)), pl.BlockSpec((TB, D), lambda i: (i, 0)),
                  pl.BlockSpec((TB, D), lambda i: (i, 0)), pl.BlockSpec((1, D), lambda i: (0, 0)),
                  pl.BlockSpec((None, N_MOD, D), lambda i: (_stream_of(i, cb), 0, 0))],
        out_specs=[pl.BlockSpec((TB, D), lambda i: (i, 0)),
                   pl.BlockSpec((None, 2, D), lambda i: (_stream_of(i, cb), 0, 0)),
                   pl.BlockSpec((1, D), lambda i: (0, 0))],
        out_shape=[jax.ShapeDtypeStruct((T, D), F32), jax.ShapeDtypeStruct((2, 2, D), F32),
                   jax.ShapeDtypeStruct((1, D), F32)],
        compiler_params=_params("arbitrary"),
    )(dh, x, dres, nw, mod)


def _scan_chunk(n, rev, n_ctx, n_all):
    if not rev:
        return n
    return jnp.where(n < n_ctx, n_ctx - 1 - n, n_all - 1 + n_ctx - n)


def _cumsum_rows(x, rev):
    rows = x.shape[0]
    row = lax.broadcasted_iota(jnp.int32, (rows, 1), 0)
    s = 1
    while s < rows:
        if not rev:
            x = x + jnp.where(row >= s, pltpu.roll(x, s, 0), 0.0)
        else:
            x = x + jnp.where(row < rows - s, pltpu.roll(x, rows - s, 0), 0.0)
        s *= 2
    return x


def _lower_bound(hlb_ref, layer):
    h = hlb_ref[...]
    if layer == 0:
        return jnp.zeros_like(h[0:1, :])
    return _sigmoid(h[1:2, :] - h[0:1, :])


HG_STEP = 4


def _step_rows(j, rev, backward):
    sub = j if rev == backward else HG_STEP - 1 - j
    return slice(sub * HG_CHUNK, (sub + 1) * HG_CHUNK)


def _hgrn_gates(q_ref, f_ref, hlb_ref, layer, rev, rows):
    lb = _lower_bound(hlb_ref, layer)
    z = f_ref[rows, :]
    sig = 1.0 / (1.0 + jnp.exp(-z))
    fg = lb + (1.0 - lb) * sig
    kk = (1.0 - lb) * (1.0 - sig)
    g = jnp.log(fg)
    b = _cumsum_rows(g, rev)
    bt = jnp.sum(g, axis=0, keepdims=True)
    mid = HG_CHUNK // 2
    r = b[mid:mid + 1, :] if rev else b[mid - 1:mid, :]
    qh = _silu(q_ref[rows, :])
    return lb, sig, fg, kk, b, bt, r, qh


def _tri_mask(rev):
    t = lax.broadcasted_iota(jnp.int32, (HG_CHUNK, HG_CHUNK), 0)
    s = lax.broadcasted_iota(jnp.int32, (HG_CHUNK, HG_CHUNK), 1)
    return (s >= t) if rev else (s <= t)


def _hgrn_fwd(parts, hlb, layer, rev, ctx_rows, name, o_add=None):
    T = parts.shape[0]
    D = hlb.shape[1] // 2
    nh = D // HEAD
    n_all, n_ctx = T // HG_CHUNK, ctx_rows // HG_CHUNK
    assert n_all % HG_STEP == 0 and n_ctx % HG_STEP == 0
    n_steps = n_all // HG_STEP
    block = functools.partial(_scan_chunk, rev=rev, n_ctx=n_ctx // HG_STEP, n_all=n_steps)
    fcol = 2 if rev else 1

    def body(q_ref, f_ref, i_ref, hlb_ref, *rest):
        if o_add is None:
            o_ref, st_ref, s_scr = rest
        else:
            oa_ref, o_ref, st_ref, s_scr = rest
        n = pl.program_id(0)

        @pl.when(n == 0)
        def _():
            s_scr[...] = jnp.zeros_like(s_scr)

        mask = _tri_mask(rev)
        hs = [slice(h * HEAD, (h + 1) * HEAD) for h in range(nh)]
        for j in range(HG_STEP):
            rows = _step_rows(j, rev, False)
            lb, sig, fg, kk, b, bt, r, qh = _hgrn_gates(q_ref, f_ref, hlb_ref, layer, rev, rows)
            qr = (qh * jnp.exp(b - r)).astype(BF16)
            kr = (kk * jnp.exp(r - b)).astype(BF16)
            qe = (qh * jnp.exp(b)).astype(BF16)
            ke = (kk * jnp.exp(bt - b)).astype(BF16)
            dec = jnp.exp(bt)
            v = i_ref[rows, :].astype(BF16)
            st = [s_scr[h] for h in range(nh)]
            a_raw = [_nt(qr[:, sl], kr[:, sl]) for sl in hs]
            o_int = [_nt(qe[:, sl], st[h].astype(BF16)) for h, sl in enumerate(hs)]
            kv = [_tn(v[:, sl], ke[:, sl]) for sl in hs]
            for h, sl in enumerate(hs):
                st_ref[j, h] = st[h]
                o = _nn(jnp.where(mask, a_raw[h], 0.0).astype(BF16), v[:, sl]) + o_int[h]
                if o_add is not None:
                    o = o + oa_ref[rows, sl]
                o_ref[rows, sl] = o
                s_scr[h] = st[h] * dec[:, sl] + kv[h]

    cspec = lambda col: pl.BlockSpec((HG_STEP * HG_CHUNK, D), lambda n: (block(n), col))
    ins = [parts, parts, parts, hlb]
    specs = [cspec(0), cspec(fcol), cspec(3), pl.BlockSpec((2, D), lambda n: (0, 1 if rev else 0))]
    if o_add is not None:
        ins.append(o_add)
        specs.append(cspec(0))
    return pl.pallas_call(
        body, name=name, grid=(n_steps,), in_specs=specs,
        out_specs=[cspec(0), pl.BlockSpec((HG_STEP, nh, HEAD, HEAD), lambda n: (n, 0, 0, 0))],
        out_shape=[jax.ShapeDtypeStruct((T, D), F32), jax.ShapeDtypeStruct((n_all, nh, HEAD, HEAD), F32)],
        scratch_shapes=[pltpu.VMEM((nh, HEAD, HEAD), F32)],
        compiler_params=_params("arbitrary"),
    )(*ins)


def _hgrn_bwd(parts, hlb, do, states, layer, rev, ctx_rows, name, other=None, dparts=None):
    T = parts.shape[0]
    D = hlb.shape[1] // 2
    nh = D // HEAD
    n_all, n_ctx = T // HG_CHUNK, ctx_rows // HG_CHUNK
    assert n_all % HG_STEP == 0 and n_ctx % HG_STEP == 0
    n_steps = n_all // HG_STEP
    step = lambda m: n_steps - 1 - m
    block = lambda m: _scan_chunk(step(m), rev, n_ctx // HG_STEP, n_steps)
    fcol = 2 if rev else 1
    has_add = other is not None
    assert not has_add or rev

    def body(q_ref, f_ref, i_ref, hlb_ref, do_ref, st_ref, *rest):
        if has_add:
            dqa_ref, dza_ref, dia_ref, _, out_ref, dlb_ref, ds_scr = rest
            dq_ref, dz_ref, di_ref = out_ref.at[:, 0:D], out_ref.at[:, 2 * D:3 * D], out_ref.at[:, 3 * D:4 * D]
            out_ref[:, D:2 * D] = dza_ref[...]
        else:
            dq_ref, dz_ref, di_ref, dlb_ref, ds_scr = rest
        m = pl.program_id(0)

        @pl.when(m == 0)
        def _():
            ds_scr[...] = jnp.zeros_like(ds_scr)
            dlb_ref[...] = jnp.zeros_like(dlb_ref)

        mask = _tri_mask(rev)
        hs = [slice(h * HEAD, (h + 1) * HEAD) for h in range(nh)]
        for j in range(HG_STEP):
            rows = _step_rows(j, rev, True)
            slot = HG_STEP - 1 - j
            lb, sig, fg, kk, b, bt, r, qh = _hgrn_gates(q_ref, f_ref, hlb_ref, layer, rev, rows)
            e_qr = jnp.exp(b - r)
            e_kr = jnp.exp(r - b)
            e_b = jnp.exp(b)
            e_ke = jnp.exp(bt - b)
            dec = jnp.exp(bt)
            qr = (qh * e_qr).astype(BF16)
            kr = (kk * e_kr).astype(BF16)
            qe = (qh * e_b).astype(BF16)
            ke = (kk * e_ke).astype(BF16)
            v = i_ref[rows, :].astype(BF16)
            dov = do_ref[rows, :].astype(BF16)
            st = [st_ref[slot, h] for h in range(nh)]
            dst = [ds_scr[h] for h in range(nh)]
            stb = [t.astype(BF16) for t in st]
            dstb = [t.astype(BF16) for t in dst]
            a_raw = [_nt(qr[:, sl], kr[:, sl]) for sl in hs]
            da_raw = [_nt(dov[:, sl], v[:, sl]) for sl in hs]
            dq_int = [_nn(dov[:, sl], stb[h]) for h, sl in enumerate(hs)]
            dk_int = [_nn(v[:, sl], dstb[h]) for h, sl in enumerate(hs)]
            dv_int = [_nt(ke[:, sl], dstb[h]) for h, sl in enumerate(hs)]
            ds_new = [_tn(dov[:, sl], qe[:, sl]) for sl in hs]
            a = [jnp.where(mask, t, 0.0).astype(BF16) for t in a_raw]
            da = [jnp.where(mask, t, 0.0).astype(BF16) for t in da_raw]
            dv_parts = [_tn(a[h], dov[:, sl]) + dv_int[h] for h, sl in enumerate(hs)]
            dq_parts = [_nn(da[h], kr[:, sl]) * e_qr[:, sl] + dq_int[h] * e_b[:, sl] for h, sl in enumerate(hs)]
            dki_parts = [dk_int[h] * e_ke[:, sl] for h, sl in enumerate(hs)]
            dk_parts = [_tn(da[h], qr[:, sl]) * e_kr[:, sl] + dki_parts[h] for h, sl in enumerate(hs)]
            dbt_parts = [dec[:, sl] * jnp.sum(st[h] * dst[h], axis=0, keepdims=True) for h, sl in enumerate(hs)]
            for h, sl in enumerate(hs):
                ds_scr[h] = dst[h] * dec[:, sl] + ds_new[h]
            dq = jnp.concatenate(dq_parts, axis=1)
            dk = jnp.concatenate(dk_parts, axis=1)
            dki = jnp.concatenate(dki_parts, axis=1)
            dv = jnp.concatenate(dv_parts, axis=1)
            dbt = jnp.concatenate(dbt_parts, axis=1) + jnp.sum(kk * dki, axis=0, keepdims=True)
            db = qh * dq - kk * dk
            dg = _cumsum_rows(db, not rev) + dbt
            df = dg / fg - dk
            dz_ref[rows, :] = (df * (1.0 - lb) * sig * (1.0 - sig)).astype(BF16)
            dlb_ref[...] += jnp.sum(df * (1.0 - sig), axis=0, keepdims=True)
            dqr = dq * _dsilu(q_ref[rows, :])
            if has_add:
                dqr = dqr + dqa_ref[rows, :]
                dv = dv + dia_ref[rows, :]
            dq_ref[rows, :] = dqr.astype(dq_ref.dtype)
            di_ref[rows, :] = dv.astype(di_ref.dtype)

        @pl.when(m == n_steps - 1)
        def _():
            lb = _lower_bound(hlb_ref, layer)
            if layer == 0:
                dlb_ref[...] = jnp.zeros_like(dlb_ref)
            else:
                dlb_ref[...] = dlb_ref[...] * lb * (1.0 - lb)

    cspec = lambda col: pl.BlockSpec((HG_STEP * HG_CHUNK, D), lambda m: (block(m), col))
    ins = [parts, parts, parts, hlb, do, states]
    specs = [cspec(0), cspec(fcol), cspec(3), pl.BlockSpec((2, D), lambda m: (0, 1 if rev else 0)), cspec(0),
             pl.BlockSpec((HG_STEP, nh, HEAD, HEAD), lambda m: (step(m), 0, 0, 0))]
    dlb_spec = pl.BlockSpec((1, D), lambda m: (0, 0))
    dlb_shape = jax.ShapeDtypeStruct((1, D), F32)
    if has_add:
        return pl.pallas_call(
            body, name=name, grid=(n_steps,),
            in_specs=specs + [cspec(0), cspec(0), cspec(0), pl.BlockSpec(memory_space=pl.ANY)],
            out_specs=[pl.BlockSpec((HG_STEP * HG_CHUNK, 4 * D), lambda m: (block(m), 0)), dlb_spec],
            out_shape=[jax.ShapeDtypeStruct(dparts.shape, dparts.dtype), dlb_shape],
            scratch_shapes=[pltpu.VMEM((nh, HEAD, HEAD), F32)], input_output_aliases={len(ins) + 3: 0},
            compiler_params=_params("arbitrary"),
        )(*ins, *other, dparts)
    return pl.pallas_call(
        body, name=name, grid=(n_steps,), in_specs=specs,
        out_specs=[cspec(0), cspec(0), cspec(0), dlb_spec],
        out_shape=[jax.ShapeDtypeStruct((T, D), F32), jax.ShapeDtypeStruct((T, D), BF16),
                   jax.ShapeDtypeStruct((T, D), F32), dlb_shape],
        scratch_shapes=[pltpu.VMEM((nh, HEAD, HEAD), F32)],
        compiler_params=_params("arbitrary"),
    )(*ins)


def _sgu_ln(gv, lnw_ref, lnb_ref):
    mu = jnp.mean(gv, axis=-1, keepdims=True)
    xc = gv - mu
    rstd = lax.rsqrt(jnp.mean(xc * xc, axis=-1, keepdims=True) + LN_EPS)
    xh = xc * rstd
    return xh, rstd, xh * lnw_ref[...] + lnb_ref[...]


def _sgu_fwd(parts, lnw, lnb, w, bt, name):
    T = parts.shape[0]
    D = lnw.shape[1]
    G = D // HEAD

    def body(u_ref, v_ref, lnw_ref, lnb_ref, w_ref, bt_ref, ya_ref):
        gu = _gelu(u_ref[...])
        _, _, vn = _sgu_ln(_gelu(v_ref[...]), lnw_ref, lnb_ref)
        vnb = vn.astype(BF16)
        for g in range(G):
            sl = slice(g * HEAD, (g + 1) * HEAD)
            mixed = _nn(w_ref[g], vnb[:, sl]) + bt_ref[:, g:g + 1]
            ya_ref[:, sl] = (gu[:, sl] * mixed).astype(BF16)

    return pl.pallas_call(
        body, name=name, grid=(T // SGU_CHUNK,),
        in_specs=[pl.BlockSpec((SGU_CHUNK, D), lambda n: (n, 4)), pl.BlockSpec((SGU_CHUNK, D), lambda n: (n, 5)),
                  pl.BlockSpec((1, D), lambda n: (0, 0)), pl.BlockSpec((1, D), lambda n: (0, 0)),
                  pl.BlockSpec((G, SGU_CHUNK, SGU_CHUNK), lambda n: (0, 0, 0)),
                  pl.BlockSpec((SGU_CHUNK, G), lambda n: (0, 0))],
        out_specs=pl.BlockSpec((SGU_CHUNK, D), lambda n: (n, 0)),
        out_shape=jax.ShapeDtypeStruct((T, D), BF16),
        compiler_params=_params("parallel"),
    )(parts, parts, lnw, lnb, w, bt)


def _sgu_bwd(parts, dya, lnw, lnb, w, bt, dparts, name):
    T = parts.shape[0]
    D = lnw.shape[1]
    G = D // HEAD

    def body(u_ref, v_ref, dya_ref, lnw_ref, lnb_ref, w_ref, bt_ref, dparts_in,
             duv_ref, dw_ref, dbt_ref, dlnw_ref, dlnb_ref, dvn_scr):
        du_ref = duv_ref.at[:, 0:D]
        dv_ref = duv_ref.at[:, D:2 * D]
        n = pl.program_id(0)

        @pl.when(n == 0)
        def _():
            dw_ref[...] = jnp.zeros_like(dw_ref)
            dbt_ref[...] = jnp.zeros_like(dbt_ref)
            dlnw_ref[...] = jnp.zeros_like(dlnw_ref)
            dlnb_ref[...] = jnp.zeros_like(dlnb_ref)

        gu, dgu = _gelu_both(u_ref[...])
        gv, dgv_dv = _gelu_both(v_ref[...])
        xh, rstd, vn = _sgu_ln(gv, lnw_ref, lnb_ref)
        vnb = vn.astype(BF16)
        dya = dya_ref[...]
        lane = lax.broadcasted_iota(jnp.int32, (SGU_CHUNK, G), 1)
        dbt = jnp.zeros((SGU_CHUNK, G), F32)
        for g in range(G):
            sl = slice(g * HEAD, (g + 1) * HEAD)
            wg = w_ref[g]
            mixed = _nn(wg, vnb[:, sl]) + bt_ref[:, g:g + 1]
            dmix = dya[:, sl] * gu[:, sl]
            du_ref[:, sl] = (dya[:, sl] * mixed * dgu[:, sl]).astype(BF16)
            dmb = dmix.astype(BF16)
            dvn_scr[:, sl] = _tn(wg, dmb)
            dw_ref[g] += _nt(dmb, vnb[:, sl])
            dbt = dbt + jnp.where(lane == g, jnp.sum(dmix, axis=1, keepdims=True), 0.0)
        dbt_ref[...] += dbt
        dvn = dvn_scr[...]
        dlnw_ref[...] += jnp.sum(dvn * xh, axis=0, keepdims=True)
        dlnb_ref[...] += jnp.sum(dvn, axis=0, keepdims=True)
        dxh = dvn * lnw_ref[...]
        dgv = rstd * (dxh - jnp.mean(dxh, axis=-1, keepdims=True) - xh * jnp.mean(dxh * xh, axis=-1, keepdims=True))
        dv_ref[...] = (dgv * dgv_dv).astype(BF16)

    row = lambda col: pl.BlockSpec((SGU_CHUNK, D), lambda n: (n, col))
    vec = pl.BlockSpec((1, D), lambda n: (0, 0))
    wsp = pl.BlockSpec((G, SGU_CHUNK, SGU_CHUNK), lambda n: (0, 0, 0))
    bsp = pl.BlockSpec((SGU_CHUNK, G), lambda n: (0, 0))
    return pl.pallas_call(
        body, name=name, grid=(T // SGU_CHUNK,),
        in_specs=[row(4), row(5), row(0), vec, vec, wsp, bsp, pl.BlockSpec(memory_space=pl.ANY)],
        out_specs=[pl.BlockSpec((SGU_CHUNK, 2 * D), lambda n: (n, 2)), wsp, bsp, vec, vec],
        out_shape=[jax.ShapeDtypeStruct(dparts.shape, dparts.dtype),
                   jax.ShapeDtypeStruct((G, SGU_CHUNK, SGU_CHUNK), F32), jax.ShapeDtypeStruct((SGU_CHUNK, G), F32),
                   jax.ShapeDtypeStruct((1, D), F32), jax.ShapeDtypeStruct((1, D), F32)],
        scratch_shapes=[pltpu.VMEM((SGU_CHUNK, D), F32)], input_output_aliases={7: 0},
        compiler_params=_params("arbitrary"),
    )(parts, parts, dya, lnw, lnb, w, bt, dparts)


TBT = 256
VMEM_LIMIT_TOKEN_OUT = 58 * 1024 * 1024


def _rows_weight_spec(wg):
    return pl.BlockSpec(wg.shape, lambda i: (0, 0, 0))


def _full(w_ref):
    return w_ref[...].reshape(w_ref.shape[0] * w_ref.shape[1], w_ref.shape[2])


def _token_out_fwd(o, parts, ya, x, mod, hnw, wa, wb, wo, ctx_rows, name):
    T, D = x.shape
    nh = D // HEAD
    cb = ctx_rows // TBT

    def body(o_ref, og_ref, ga_ref, gb_ref, ya_ref, x_ref, mod_ref, hnw_ref, wa_ref, wb_ref, wo_ref,
             yb_ref, pa_ref, pb_ref, mg_ref, tmo_ref, xm_ref):
        ov = o_ref[...]
        so = _silu(og_ref[...])
        nw = hnw_ref[...]
        for h in range(nh):
            sl = slice(h * HEAD, (h + 1) * HEAD)
            seg = ov[:, sl]
            r = lax.rsqrt(jnp.mean(seg * seg, axis=-1, keepdims=True) + RMS_EPS)
            yb_ref[:, sl] = (seg * r * nw * so[:, sl]).astype(BF16)
        pa = _nn(ya_ref[...], _full(wa_ref))
        pb = _nn(yb_ref[...], _full(wb_ref))
        pa_ref[...] = pa
        pb_ref[...] = pb
        mg = (_sigmoid(ga_ref[...]) * pa + _sigmoid(gb_ref[...]) * pb).astype(BF16)
        mg_ref[...] = mg
        out = _nn(mg, _full(wo_ref))
        tmo_ref[...] = out
        xm_ref[...] = x_ref[...] + mod_ref[2:3, :] * out

    row = lambda col: pl.BlockSpec((TBT, D), lambda i: (i, col))
    wsp = _rows_weight_spec(wa)
    sd = lambda dt: jax.ShapeDtypeStruct((T, D), dt)
    return pl.pallas_call(
        body, name=name, grid=(T // TBT,),
        in_specs=[row(0), row(6), row(7), row(8), row(0), row(0),
                  pl.BlockSpec((None, N_MOD, D), lambda i: (_stream_of(i, cb), 0, 0)),
                  pl.BlockSpec((1, HEAD), lambda i: (0, 0)), wsp, wsp, wsp],
        out_specs=[row(0)] * 6,
        out_shape=[sd(BF16), sd(F32), sd(F32), sd(BF16), sd(F32), sd(F32)],
        compiler_params=_params("parallel", vmem=VMEM_LIMIT_TOKEN_OUT),
    )(o, parts, parts, parts, ya, x, mod, hnw, wa, wb, wo)


def _token_out_bwd(dx, tmo, pa, pb, o, parts, mod, hnw, wa, wb, wo, ctx_rows, name):
    T, D = dx.shape
    nh = D // HEAD
    cb = ctx_rows // TBT

    def body(dx_ref, tmo_ref, pa_ref, pb_ref, o_ref, og_ref, ga_ref, gb_ref, mod_ref, hnw_ref, wa_ref, wb_ref, wo_ref,
             dout_ref, dpa_ref, dpb_ref, dgate_ref, dya_ref, do_ref, dg1_ref, dhnw_ref):
        i = pl.program_id(0)

        @pl.when(i == 0)
        def _():
            dhnw_ref[...] = jnp.zeros_like(dhnw_ref)

        @pl.when((i == 0) | (i == cb))
        def _():
            dg1_ref[...] = jnp.zeros_like(dg1_ref)

        dxv = dx_ref[...]
        dg1_ref[...] += jnp.sum(dxv * tmo_ref[...], axis=0, keepdims=True)
        dout = (dxv * mod_ref[2:3, :]).astype(BF16)
        dout_ref[...] = dout
        dmg = _nt(dout, _full(wo_ref))
        sa = _sigmoid(ga_ref[...])
        sb = _sigmoid(gb_ref[...])
        dpa = (dmg * sa).astype(BF16)
        dpb = (dmg * sb).astype(BF16)
        dpa_ref[...] = dpa
        dpb_ref[...] = dpb
        dgate_ref[:, D:2 * D] = (dmg * pa_ref[...] * sa * (1.0 - sa)).astype(BF16)
        dgate_ref[:, 2 * D:3 * D] = (dmg * pb_ref[...] * sb * (1.0 - sb)).astype(BF16)
        dya_ref[...] = _nt(dpa, _full(wa_ref))
        dyb = _nt(dpb, _full(wb_ref))
        so, dso = _silu_both(og_ref[...])
        ov = o_ref[...]
        nw = hnw_ref[...]
        dnw = jnp.zeros((1, HEAD), F32)
        for h in range(nh):
            sl = slice(h * HEAD, (h + 1) * HEAD)
            seg = ov[:, sl]
            r = lax.rsqrt(jnp.mean(seg * seg, axis=-1, keepdims=True) + RMS_EPS)
            oh = seg * r
            dn = dyb[:, sl] * so[:, sl]
            dgate_ref[:, sl] = (dyb[:, sl] * oh * nw * dso[:, sl]).astype(BF16)
            dnw = dnw + jnp.sum(dn * oh, axis=0, keepdims=True)
            doh = dn * nw
            do_ref[:, sl] = r * (doh - oh * jnp.mean(doh * oh, axis=-1, keepdims=True))
        dhnw_ref[...] += dnw

    row = lambda col: pl.BlockSpec((TBT, D), lambda i: (i, col))
    wsp = _rows_weight_spec(wa)
    sd = lambda dt: jax.ShapeDtypeStruct((T, D), dt)
    return pl.pallas_call(
        body, name=name, grid=(T // TBT,),
        in_specs=[row(0), row(0), row(0), row(0), row(0), row(6), row(7), row(8),
                  pl.BlockSpec((None, N_MOD, D), lambda i: (_stream_of(i, cb), 0, 0)),
                  pl.BlockSpec((1, HEAD), lambda i: (0, 0)), wsp, wsp, wsp],
        out_specs=[row(0)] * 3 + [pl.BlockSpec((TBT, 3 * D), lambda i: (i, 2)), row(0), row(0),
                                  pl.BlockSpec((None, 1, D), lambda i: (_stream_of(i, cb), 0, 0)),
                                  pl.BlockSpec((1, HEAD), lambda i: (0, 0))],
        out_shape=[sd(BF16)] * 3 + [jax.ShapeDtypeStruct((T, 9 * D), BF16), sd(F32), sd(F32),
                                    jax.ShapeDtypeStruct((2, 1, D), F32), jax.ShapeDtypeStruct((1, HEAD), F32)],
        compiler_params=_params("arbitrary", vmem=VMEM_LIMIT_TOKEN_OUT),
    )(dx, tmo, pa, pb, o, parts, parts, parts, mod, hnw, wa, wb, wo)


def _conv_geometry(i, nb, cb):
    is_ctx = i < cb
    first = (i == 0) | (i == cb)
    last = (i == cb - 1) | (i == nb - 1)
    row = lax.broadcasted_iota(jnp.int32, (TB + 2 * GRID_W, 1), 0)
    w = row & (GRID_W - 1)
    left_ok = (w != 0) | is_ctx
    right_ok = (w != GRID_W - 1) | is_ctx
    return is_ctx, first, last, left_ok, right_ok


def _ext(p_ref, m_ref, n_ref, first, last):
    return jnp.concatenate([jnp.where(first, 0.0, p_ref[...]), m_ref[...], jnp.where(last, 0.0, n_ref[...])], axis=0)


def _shift_prev(e, ok):
    return jnp.where(ok, pltpu.roll(e, 1, 0), 0.0)


def _shift_next(e, ok):
    return jnp.where(ok, pltpu.roll(e, e.shape[0] - 1, 0), 0.0)


def _halo_specs(cbk, n64, coff=0):
    r = TB // GRID_W
    prev = pl.BlockSpec((GRID_W, cbk), lambda j, i: (jnp.maximum(r * i - 1, 0), j + coff))
    main = pl.BlockSpec((TB, cbk), lambda j, i: (i, j + coff))
    nxt = pl.BlockSpec((GRID_W, cbk), lambda j, i: (jnp.minimum(r * i + r, n64 - 1), j + coff))
    return [prev, main, nxt]


def _conv_cblock(dff):
    return _tile(dff, 1408)


def _conv_fwd(up, cw, cbias, ctx_rows, name):
    T, dff = up.shape[0], up.shape[1] // 2
    cbk = _conv_cblock(dff)
    nb, cb = T // TB, ctx_rows // TB
    nvb = dff // cbk

    def body(ap_ref, a_ref, an_ref, v_ref, cw_ref, cb_ref, ac_ref, act_ref):
        i = pl.program_id(1)
        is_ctx, first, last, lok, rok = _conv_geometry(i, nb, cb)
        e = _ext(ap_ref, a_ref, an_ref, first, last)
        el = _shift_prev(e, lok)
        er = _shift_next(e, rok)
        cwv = cw_ref[...]

        def comb(dr, lo):
            sl = slice(lo, lo + TB)
            return cwv[3 * dr:3 * dr + 1] * el[sl] + cwv[3 * dr + 1:3 * dr + 2] * e[sl] + cwv[3 * dr + 2:3 * dr + 3] * er[sl]

        out = comb(1, GRID_W) + jnp.where(is_ctx, 0.0, comb(0, 0) + comb(2, 2 * GRID_W))
        a_c = out + cb_ref[...]
        ac_ref[...] = a_c
        act_ref[...] = (_gelu(a_c) * v_ref[...]).astype(BF16)

    main = pl.BlockSpec((TB, cbk), lambda j, i: (i, j))
    return pl.pallas_call(
        body, name=name, grid=(dff // cbk, nb),
        in_specs=_halo_specs(cbk, T // GRID_W) + [pl.BlockSpec((TB, cbk), lambda j, i: (i, j + nvb)),
                                                 pl.BlockSpec((9, cbk), lambda j, i: (0, j)),
                                                 pl.BlockSpec((1, cbk), lambda j, i: (0, j))],
        out_specs=[main, main],
        out_shape=[jax.ShapeDtypeStruct((T, dff), F32), jax.ShapeDtypeStruct((T, dff), BF16)],
        compiler_params=_params("parallel", "parallel"),
    )(up, up, up, up, cw, cbias)


def _conv_bwd(up, ac, dact, cw, ctx_rows, name):
    T, dff = up.shape[0], up.shape[1] // 2
    cbk = _conv_cblock(dff)
    nb, cb = T // TB, ctx_rows // TB
    nvb = dff // cbk

    def body(ap_ref, a_ref, an_ref, vp_ref, v_ref, vn_ref, cp_ref, c_ref, cn_ref, dp_ref, d_ref, dn_ref, cw_ref,
             da_ref, dv_ref, dcw_ref, dcb_ref):
        i = pl.program_id(1)

        @pl.when(i == 0)
        def _():
            dcw_ref[...] = jnp.zeros_like(dcw_ref)
            dcb_ref[...] = jnp.zeros_like(dcb_ref)

        is_ctx, first, last, lok, rok = _conv_geometry(i, nb, cb)
        gl, dgl = _gelu_both(_ext(cp_ref, c_ref, cn_ref, first, last))
        g = _ext(dp_ref, d_ref, dn_ref, first, last) * _ext(vp_ref, v_ref, vn_ref, first, last) * dgl
        dv_ref[...] = (d_ref[...] * gl[GRID_W:GRID_W + TB]).astype(BF16)
        gm = _shift_prev(g, lok)
        gp = _shift_next(g, rok)
        cwv = cw_ref[...]

        def comb(dr, lo):
            sl = slice(lo, lo + TB)
            return cwv[3 * dr:3 * dr + 1] * gp[sl] + cwv[3 * dr + 1:3 * dr + 2] * g[sl] + cwv[3 * dr + 2:3 * dr + 3] * gm[sl]

        da = comb(1, GRID_W) + jnp.where(is_ctx, 0.0, comb(0, 2 * GRID_W) + comb(2, 0))
        da_ref[...] = da.astype(BF16)
        e = _ext(ap_ref, a_ref, an_ref, first, last)
        taps = [_shift_prev(e, lok), e, _shift_next(e, rok)]
        gmain = g[GRID_W:GRID_W + TB]
        dcb_ref[...] += jnp.sum(gmain, axis=0, keepdims=True)
        vert = jnp.where(is_ctx, 0.0, 1.0)
        for dr in range(3):
            sl = slice(dr * GRID_W, dr * GRID_W + TB)
            for dw in range(3):
                s = jnp.sum(gmain * taps[dw][sl], axis=0, keepdims=True)
                if dr != 1:
                    s = s * vert
                k = 3 * dr + dw
                dcw_ref[k:k + 1, :] += s

    main = pl.BlockSpec((TB, cbk), lambda j, i: (i, j))
    halo = _halo_specs(cbk, T // GRID_W)
    acc9 = pl.BlockSpec((9, cbk), lambda j, i: (0, j))
    acc1 = pl.BlockSpec((1, cbk), lambda j, i: (0, j))
    return pl.pallas_call(
        body, name=name, grid=(dff // cbk, nb),
        in_specs=halo + _halo_specs(cbk, T // GRID_W, nvb) + halo + halo + [acc9],
        out_specs=[main, main, acc9, acc1],
        out_shape=[jax.ShapeDtypeStruct((T, dff), BF16), jax.ShapeDtypeStruct((T, dff), BF16),
                   jax.ShapeDtypeStruct((9, dff), F32), jax.ShapeDtypeStruct((1, dff), F32)],
        compiler_params=_params("parallel", "arbitrary"),
    )(up, up, up, up, up, up, ac, ac, ac, dact, dact, dact, cw)


def _ffn_out_fwd(act, xm, mod, wd, ctx_rows, name):
    T, D = xm.shape
    dff = act.shape[1]
    cb = ctx_rows // TB

    def body(act_ref, x_ref, mod_ref, w_ref, xo_ref, fo_ref):
        out = _nn(act_ref[...], _full(w_ref))
        fo_ref[...] = out
        xo_ref[...] = x_ref[...] + mod_ref[5:6, :] * out

    row = pl.BlockSpec((TB, D), lambda i: (i, 0))
    return pl.pallas_call(
        body, name=name, grid=(T // TB,),
        in_specs=[pl.BlockSpec((TB, dff), lambda i: (i, 0)), row,
                  pl.BlockSpec((None, N_MOD, D), lambda i: (_stream_of(i, cb), 0, 0)),
                  _rows_weight_spec(wd)],
        out_specs=[row, row],
        out_shape=[jax.ShapeDtypeStruct((T, D), F32), jax.ShapeDtypeStruct((T, D), F32)],
        compiler_params=_params("parallel"),
    )(act, xm, mod, wd)


def _ffn_out_bwd(dx, fo, mod, wd, ctx_rows, name):
    T, D = dx.shape
    dff = N_CHIPS * wd.shape[1]
    cb = ctx_rows // TB

    def body(dx_ref, fo_ref, mod_ref, w_ref, dout_ref, dact_ref, dg2_ref):
        i = pl.program_id(0)

        @pl.when((i == 0) | (i == cb))
        def _():
            dg2_ref[...] = jnp.zeros_like(dg2_ref)

        dxv = dx_ref[...]
        dg2_ref[...] += jnp.sum(dxv * fo_ref[...], axis=0, keepdims=True)
        dout = (dxv * mod_ref[5:6, :]).astype(BF16)
        dout_ref[...] = dout
        dact_ref[...] = _nt(dout, _full(w_ref))

    row = pl.BlockSpec((TB, D), lambda i: (i, 0))
    return pl.pallas_call(
        body, name=name, grid=(T // TB,),
        in_specs=[row, row, pl.BlockSpec((None, N_MOD, D), lambda i: (_stream_of(i, cb), 0, 0)),
                  _rows_weight_spec(wd)],
        out_specs=[row, pl.BlockSpec((TB, dff), lambda i: (i, 0)),
                   pl.BlockSpec((None, 1, D), lambda i: (_stream_of(i, cb), 0, 0))],
        out_shape=[jax.ShapeDtypeStruct((T, D), BF16), jax.ShapeDtypeStruct((T, dff), F32),
                   jax.ShapeDtypeStruct((2, 1, D), F32)],
        compiler_params=_params("arbitrary"),
    )(dx, fo, mod, wd)


def _loss_bwd(x, target, fw, ctx_rows, name):
    T, D = x.shape
    cb = ctx_rows // TB

    def body(x_ref, t_ref, fw_ref, dx_ref, loss_ref, dfw_ref):
        i = pl.program_id(0)

        @pl.when(i == 0)
        def _():
            loss_ref[...] = jnp.zeros_like(loss_ref)
            dfw_ref[...] = jnp.zeros_like(dfw_ref)

        @pl.when(i < cb)
        def _():
            dx_ref[...] = jnp.zeros_like(dx_ref)

        @pl.when(i >= cb)
        def _():
            xv = x_ref[...]
            r = lax.rsqrt(jnp.mean(xv * xv, axis=-1, keepdims=True) + RMS_EPS)
            xh = xv * r
            fwv = fw_ref[...]
            err = xh * fwv - t_ref[...]
            loss_ref[...] += (0.5 / D) * jnp.sum(err * err)
            dy = err * (1.0 / D)
            dfw_ref[...] += jnp.sum(dy * xh, axis=0, keepdims=True)
            dxh = dy * fwv
            dx_ref[...] = r * (dxh - xh * jnp.mean(dxh * xh, axis=-1, keepdims=True))

    row = pl.BlockSpec((TB, D), lambda i: (i, 0))
    return pl.pallas_call(
        body, name=name, grid=(T // TB,),
        in_specs=[row, pl.BlockSpec((TB, D), lambda i: (jnp.maximum(i - cb, 0), 0)), pl.BlockSpec((1, D), lambda i: (0, 0))],
        out_specs=[row, pl.BlockSpec((1, 128), lambda i: (0, 0)), pl.BlockSpec((1, D), lambda i: (0, 0))],
        out_shape=[jax.ShapeDtypeStruct((T, D), F32), jax.ShapeDtypeStruct((1, 128), F32),
                   jax.ShapeDtypeStruct((1, D), F32)],
        compiler_params=_params("arbitrary"),
    )(x, target, fw)


def _adamw(w, gs, m, v, name):
    L, R, C = w.shape
    assert len(gs) == L
    rb = _rows_tile(R, max(16, (1 << 18) // C // 16 * 16))
    bc1 = 1.0 - ADAM_B1 ** ADAM_STEP
    bc2 = 1.0 - ADAM_B2 ** ADAM_STEP

    def body(w_ref, m_ref, v_ref, *rest):
        g_refs, (g_ref, d_ref, nm_ref, nv_ref) = rest[:L], rest[L:]
        layer = pl.program_id(0)
        for li in range(L):
            @pl.when(layer == li)
            def _():
                gv = g_refs[li][...]
                g_ref[...] = gv
                nm = ADAM_B1 * m_ref[...] + (1.0 - ADAM_B1) * gv
                nv = ADAM_B2 * v_ref[...] + (1.0 - ADAM_B2) * (gv * gv)
                nm_ref[...] = nm
                nv_ref[...] = nv
                d_ref[...] = -ADAM_LR * ((nm / bc1) / (jnp.sqrt(nv / bc2) + ADAM_EPS) + ADAM_WD * w_ref[...])

    blk = pl.BlockSpec((None, rb, C), lambda l, i: (l, i, 0))
    gblk = pl.BlockSpec((rb, C), lambda l, i: (i, 0))
    sd = jax.ShapeDtypeStruct((L, R, C), F32)
    return pl.pallas_call(
        body, name=name, grid=(L, R // rb), in_specs=[blk] * 3 + [gblk] * L, out_specs=[blk] * 4, out_shape=[sd] * 4,
        compiler_params=_params("parallel", "parallel"),
    )(w, m, v, *gs)


def _local_step(xs, cv, target, W, layer_weights, on_layer_grads, ctx_rows):
    T, D = xs.shape
    depth = W["norm1_w"].shape[0]
    saved = []
    X = xs
    for l in range(depth):
        s = {}
        Wl = layer_weights(l, X)
        mod_all, sa = _mod_fwd(cv, Wl["ada_w"], W["ada_b"][l][None, :] + Wl["token"], f"mod_fwd_{l}")
        mod = mod_all[:2].reshape(2, N_MOD, D)
        h1 = _norm_mod(X, W["norm1_w"][l][None, :], mod, 0, ctx_rows, f"norm1_{l}")
        parts = _mm_nn_w(h1, Wl["w_in"], F32, f"in_proj_{l}")
        o_f, st_f = _hgrn_fwd(parts, W["hlb"], l, False, ctx_rows, f"hgrn_fwd_f_{l}")
        o, st_b = _hgrn_fwd(parts, W["hlb"], l, True, ctx_rows, f"hgrn_fwd_b_{l}", o_add=o_f)
        ya = _sgu_fwd(parts, W["sgu_ln_w"][l][None, :], W["sgu_ln_b"][l][None, :], W["sgu_w"][l], W["sgu_bt"][l],
                      f"sgu_fwd_{l}")
        Wl.update(Wl.pop("late")(ya))
        yb, pa, pb, mg, tmo, xm = _token_out_fwd(o, parts, ya, X, mod, W["hnw"][l][None, :] + Wl["late_token"], Wl["w_a"],
                                                 Wl["w_b"], Wl["w_o"], ctx_rows, f"token_out_fwd_{l}")
        h2 = _norm_mod(xm, W["norm2_w"][l][None, :], mod, 3, ctx_rows, f"norm2_{l}")
        up = _mm_nn_w(h2, Wl["w_up"], F32, f"up_proj_{l}")
        ac, act = _conv_fwd(up, Wl["conv_w"], W["conv_b"][l][None, :], ctx_rows, f"conv_fwd_{l}")
        xo, fo = _ffn_out_fwd(act, xm, mod, Wl["w_down"], ctx_rows, f"ffn_out_fwd_{l}")
        s.update(X=X, Wl=Wl, mod=mod, mod_all=mod_all, sa=sa, h1=h1, parts=parts, o=o, st_f=st_f, st_b=st_b, ya=ya, yb=yb,
                 pa=pa, pb=pb, mg=mg, tmo=tmo, xm=xm, h2=h2, up=up, ac=ac, act=act, fo=fo)
        saved.append(s)
        X = xo

    dX, loss_row, dfw = _loss_bwd(X, target, W["final_norm_w"][None, :], ctx_rows, "loss_bwd")
    G = {k: [None] * depth for k in ("ada_b", "norm1_w", "sgu_ln_w", "sgu_ln_b", "sgu_w", "sgu_b", "hlb1", "hnw", "norm2_w",
                                     "conv_w", "conv_b", "dmod")}
    dcv = jnp.zeros_like(cv)
    for l in reversed(range(depth)):
        s = saved[l]
        mod, Wl = s["mod"], s["Wl"]
        big = {}
        dout2, dact, dg2 = _ffn_out_bwd(dX, s["fo"], mod, Wl["w_down"], ctx_rows, f"ffn_out_bwd_{l}")
        big["w_down"] = _mm_tn(s["act"], dout2, F32, f"dw_down_{l}")
        da, dv, dcw, dcb = _conv_bwd(s["up"], s["ac"], dact, Wl["conv_w"], ctx_rows, f"conv_bwd_{l}")
        G["conv_w"][l], G["conv_b"][l] = dcw, dcb[0]
        big["w_up"] = _mm_tn(s["h2"], (da, dv), F32, f"dw_up_{l}", out_chips=True)
        dh2 = _mm_nt_w((da, dv), Wl["w_up"], F32, f"dh2_{l}")
        dxm, dm2, dnw2 = _norm_mod_bwd(dh2, s["xm"], dX, W["norm2_w"][l][None, :], mod, 3, ctx_rows, f"norm2_bwd_{l}")
        G["norm2_w"][l] = dnw2[0]
        (dout1, dpa, dpb, dparts, dya, do, dg1, dhnw) = _token_out_bwd(
            dxm, s["tmo"], s["pa"], s["pb"], s["o"], s["parts"], mod, W["hnw"][l][None, :], Wl["w_a"], Wl["w_b"], Wl["w_o"],
            ctx_rows, f"token_out_bwd_{l}")
        G["hnw"][l] = dhnw[0]
        big["w_o"] = _mm_tn(s["mg"], dout1, F32, f"dw_o_{l}")
        big["w_a"] = _mm_tn(s["ya"], dpa, F32, f"dw_a_{l}")
        big["w_b"] = _mm_tn(s["yb"], dpb, F32, f"dw_b_{l}")
        tok = on_layer_grads(l, "early", big)
        dparts, dsw, dsbt, dlnw, dlnb = _sgu_bwd(s["parts"], dya, W["sgu_ln_w"][l][None, :], W["sgu_ln_b"][l][None, :] + tok,
                                                 W["sgu_w"][l], W["sgu_bt"][l], dparts, f"sgu_bwd_{l}")
        G["sgu_w"][l], G["sgu_b"][l], G["sgu_ln_w"][l], G["sgu_ln_b"][l] = dsw, dsbt.T, dlnw[0], dlnb[0]
        dq_f, dz_f, di_f, dlb_f = _hgrn_bwd(s["parts"], W["hlb"], do, s["st_f"], l, False, ctx_rows, f"hgrn_bwd_f_{l}")
        dparts, dlb_b = _hgrn_bwd(s["parts"], W["hlb"], do, s["st_b"], l, True, ctx_rows, f"hgrn_bwd_b_{l}",
                                  other=(dq_f, dz_f, di_f), dparts=dparts)
        G["hlb1"][l] = jnp.concatenate([dlb_f[0], dlb_b[0]])
        tok = on_layer_grads(l, "late", {"w_in": _mm_tn(s["h1"], dparts, F32, f"dw_in_{l}", out_chips=True)})
        dh1 = _mm_nt_w(dparts, Wl["w_in"], F32, f"dh1_{l}")
        tok = tok + on_layer_grads(l, "end", {"after": dh1})
        dX, dm1, dnw1 = _norm_mod_bwd(dh1, s["X"], dxm, W["norm1_w"][l][None, :] + tok, mod, 0, ctx_rows, f"norm1_bwd_{l}")
        G["norm1_w"][l] = dnw1[0]
        dmod = jnp.concatenate([dm1, dg1, dm2, dg2], axis=1).reshape(2, N_MOD * D)
        dmod16 = jnp.concatenate([dmod, jnp.zeros((cv.shape[0] - 2, N_MOD * D), F32)], axis=0)
        G["ada_b"][l] = dmod[0] + dmod[1]
        G["dmod"][l] = dmod
        dcv = dcv + _cvec_bwd(dmod16, Wl["ada_w"], cv, f"dcvec_{l}")
    G["c_ctx"] = dcv[0]
    G["final_norm_w"] = dfw[0]
    return loss_row[0, 0], dX, G, saved[0]["sa"]


def _chip_peers(x, y, c):
    return [((1 - x, y, c), 2 * (1 - x) + y), ((x, 1 - y, c), 2 * x + 1 - y), ((1 - x, 1 - y, c), 2 * (1 - x) + 1 - y)]


def _rdma_call(ins, out_shapes, plan, n_remote, n_local, name, aliases=None):
    n_in, n_out = len(ins), len(out_shapes)

    def body(*refs):
        in_refs, out_refs = refs[:n_in], refs[n_in:n_in + n_out]
        send_sems, recv_sems, local_sems = refs[n_in + n_out:]
        x, y, c = lax.axis_index("x"), lax.axis_index("y"), lax.axis_index("c")
        remote, local = plan(in_refs, out_refs, x, y, c)
        assert len(remote) == n_remote and len(local) == n_local, (name, len(remote), len(local))
        copies = [pltpu.make_async_copy(s, d, local_sems.at[i]) for i, (s, d) in enumerate(local)]
        copies += [pltpu.make_async_remote_copy(src_ref=s, dst_ref=d, send_sem=send_sems.at[k], recv_sem=recv_sems.at[k],
                                                device_id=dev, device_id_type=pl.DeviceIdType.MESH)
                   for k, (s, d, dev) in enumerate(remote)]
        for cp in copies:
            cp.start()
        for cp in copies:
            cp.wait()

    hbm = pl.BlockSpec(memory_space=pltpu.HBM)
    return pl.pallas_call(
        body, name=name, in_specs=[hbm] * n_in, out_specs=[hbm] * n_out, out_shape=out_shapes,
        scratch_shapes=[pltpu.SemaphoreType.DMA((n_remote,)), pltpu.SemaphoreType.DMA((n_remote,)),
                        pltpu.SemaphoreType.DMA((max(n_local, 1),))],
        input_output_aliases=aliases or {},
    )(*ins)


DMA_PIECE_BYTES = 1 << 18
DMA_MAX_PIECES = 8


def _row_pieces(shape, dtype):
    rows = shape[0]
    row_bytes = jnp.dtype(dtype).itemsize
    for d in shape[1:]:
        row_bytes *= d
    n = 1
    while n < DMA_MAX_PIECES and rows % (2 * n * 16) == 0 and rows * row_bytes // (2 * n) >= DMA_PIECE_BYTES:
        n *= 2
    return [(i * (rows // n), rows // n) for i in range(n)]


def _half_pieces(o, c):
    r2 = o.shape[1] // 2
    return [pl.ds(c * r2 + st, sz) for st, sz in _row_pieces((r2,) + o.shape[2:], o.dtype)]


def _n_half_pieces(arrays):
    return sum(len(_row_pieces((a.shape[1] // 2,) + a.shape[2:], a.dtype)) for a in arrays)


def _plan_gather_far(lands, x, y, c):
    me = 2 * x + y
    return [(o.at[me, rows], o.at[me, rows], dev) for dev, _ in _chip_peers(x, y, c) for o in lands
            for rows in _half_pieces(o, c)]


def _plan_gather_near(lands, x, y, c):
    return [(o.at[idx, rows], o.at[idx, rows], (x, y, 1 - c)) for _, idx in _chip_peers(x, y, c) for o in lands
            for rows in _half_pieces(o, c)]


def _gather_weights(lands, name):
    n = len(lands)
    n_far = (N_CHIPS - 1) * _n_half_pieces(lands)

    def body(*refs):
        outs = refs[n:2 * n]
        far_send, far_recv, near_send, near_recv = refs[2 * n:]
        x, y, c = lax.axis_index("x"), lax.axis_index("y"), lax.axis_index("c")
        mk = lambda plan, send, recv: [
            pltpu.make_async_remote_copy(src_ref=s, dst_ref=d, send_sem=send.at[k], recv_sem=recv.at[k], device_id=dev,
                                         device_id_type=pl.DeviceIdType.MESH)
            for k, (s, d, dev) in enumerate(plan(outs, x, y, c))]
        far, near = mk(_plan_gather_far, far_send, far_recv), mk(_plan_gather_near, near_send, near_recv)
        assert len(far) == n_far and len(near) == n_far
        for cp in far:
            cp.start()
        for k in range(n_far):
            far[k].wait_recv()
            near[k].start()
        for k in range(n_far):
            near[k].wait_recv()
        for cp in far + near:
            cp.wait_send()

    hbm = pl.BlockSpec(memory_space=pltpu.HBM)
    sems = pltpu.SemaphoreType.DMA((n_far,))
    return pl.pallas_call(
        body, name=name, in_specs=[hbm] * n, out_specs=[hbm] * n,
        out_shape=[jax.ShapeDtypeStruct(a.shape, a.dtype) for a in lands],
        scratch_shapes=[sems, sems, sems, sems], input_output_aliases={i: i for i in range(n)},
    )(*lands)


def _gather_all(v, name):
    def plan(ins, outs, x, y, c):
        (s,), (o,) = ins, outs
        me = 4 * x + 2 * y + c
        flip = lambda a, f: 1 - a if f else a
        remote = [(s, o.at[me], (flip(x, m & 4), flip(y, m & 2), flip(c, m & 1))) for m in range(1, 8)]
        return remote, [(s, o.at[me])]

    return _rdma_call([v], [jax.ShapeDtypeStruct((8,) + v.shape, v.dtype)], plan, 7, 1, name)[0]


def _plan_pair(ins, lands, x, y, c):
    return [(a.at[j, 1 - c, pl.ds(st, sz)], o.at[j, pl.ds(st, sz)], (x, y, 1 - c)) for a, o in zip(ins, lands)
            for j in range(N_CHIPS) for st, sz in _row_pieces(a.shape[2:], a.dtype)]


def _n_pair_copies(parts):
    return N_CHIPS * sum(len(_row_pieces(a.shape[2:], a.dtype)) for a in parts)


def _reduce_pair(parts, name):
    shapes = [jax.ShapeDtypeStruct((N_CHIPS,) + a.shape[2:], a.dtype) for a in parts]
    return _rdma_call(parts, shapes, lambda ins, outs, x, y, c: (_plan_pair(ins, outs, x, y, c), []),
                      _n_pair_copies(parts), 0, name)


def _plan_chips(ins, lands, x, y, c):
    me = 2 * x + y
    return [(a.at[idx, pl.ds(st, sz)], o.at[me, pl.ds(st, sz)], dev) for dev, idx in _chip_peers(x, y, c)
            for a, o in zip(ins, lands) for st, sz in _row_pieces(a.shape[1:], a.dtype)]


def _n_chips_copies(parts):
    return (N_CHIPS - 1) * sum(len(_row_pieces(a.shape[1:], a.dtype)) for a in parts)


def _reduce_chips(parts, name):
    shapes = [jax.ShapeDtypeStruct(a.shape, a.dtype) for a in parts]
    return _rdma_call(parts, shapes, lambda ins, outs, x, y, c: (_plan_chips(ins, outs, x, y, c), []),
                      _n_chips_copies(parts), 0, name)


def _gather_pair(halves, name):
    def plan(ins, outs, x, y, c):
        return [(o.at[c, pl.ds(st, sz)], o.at[c, pl.ds(st, sz)], (x, y, 1 - c)) for o in outs
                for st, sz in _row_pieces(o.shape[1:], o.dtype)], []

    shapes = [jax.ShapeDtypeStruct(a.shape, a.dtype) for a in halves]
    n_remote = sum(len(_row_pieces(a.shape[1:], a.dtype)) for a in halves)
    return _rdma_call(halves, shapes, plan, n_remote, 0, name, aliases={i: i for i in range(len(halves))})


def _split_start(ins, lands, plan, n_remote, name):
    n_buf = len(ins) + len(lands)

    def body(*refs):
        in_refs, land_refs = refs[:len(ins)], refs[len(ins):n_buf]
        send_sems, recv_sems, token = refs[n_buf], refs[n_buf + 1], refs[-1]
        x, y, c = lax.axis_index("x"), lax.axis_index("y"), lax.axis_index("c")
        remote = plan(in_refs, land_refs, x, y, c)
        assert len(remote) == n_remote, (name, len(remote))
        for k, (s, d, dev) in enumerate(remote):
            pltpu.make_async_remote_copy(src_ref=s, dst_ref=d, send_sem=send_sems.at[k], recv_sem=recv_sems.at[k],
                                         device_id=dev, device_id_type=pl.DeviceIdType.MESH).start()
        token[...] = jnp.zeros_like(token)

    hbm = pl.BlockSpec(memory_space=pltpu.HBM)
    sem = pl.BlockSpec(memory_space=pltpu.SEMAPHORE)
    bufs = list(ins) + list(lands)
    out = pl.pallas_call(
        body, name=name, in_specs=[hbm] * n_buf,
        out_specs=(sem, sem) + (hbm,) * n_buf + (pl.BlockSpec(memory_space=pltpu.VMEM),),
        out_shape=(pltpu.SemaphoreType.DMA((n_remote,)), pltpu.SemaphoreType.DMA((n_remote,)))
        + tuple(pltpu.HBM(a.shape, a.dtype) for a in bufs) + (jax.ShapeDtypeStruct((8, 128), F32),),
        input_output_aliases={i: 2 + i for i in range(n_buf)},
        compiler_params=pltpu.CompilerParams(has_side_effects=pltpu.SideEffectType.DATAFLOW_SIDE_EFFECTING),
    )(*[pltpu.with_memory_space_constraint(a, pltpu.HBM) for a in bufs])
    return dict(send=out[0], recv=out[1], ins=list(out[2:2 + len(ins)]), lands=list(out[2 + len(ins):2 + n_buf]),
                token=out[-1][0, 0], token_array=out[-1], plan=plan, n_remote=n_remote)


def _split_wait(st, after, name):
    n_in, n_buf = len(st["ins"]), len(st["ins"]) + len(st["lands"])
    plan, n_remote = st["plan"], st["n_remote"]

    def body(*refs):
        in_refs, land_refs = refs[:n_in], refs[n_in:n_buf]
        send_sems, recv_sems = refs[n_buf], refs[n_buf + 1]
        x, y, c = lax.axis_index("x"), lax.axis_index("y"), lax.axis_index("c")
        for k, (s, d, dev) in enumerate(plan(in_refs, land_refs, x, y, c)):
            cp = pltpu.make_async_remote_copy(src_ref=s, dst_ref=d, send_sem=send_sems.at[k], recv_sem=recv_sems.at[k],
                                              device_id=dev, device_id_type=pl.DeviceIdType.MESH)
            cp.wait_send()
            cp.wait_recv()

    hbm = pl.BlockSpec(memory_space=pltpu.HBM)
    sem = pl.BlockSpec(memory_space=pltpu.SEMAPHORE)
    bufs = st["ins"] + st["lands"]
    out = pl.pallas_call(
        body, name=name, in_specs=[hbm] * n_buf + [sem, sem, pl.BlockSpec(memory_space=pl.ANY)],
        out_specs=[hbm] * n_buf, out_shape=[pltpu.HBM(a.shape, a.dtype) for a in bufs],
        input_output_aliases={i: i for i in range(n_buf)},
        compiler_params=pltpu.CompilerParams(has_side_effects=pltpu.SideEffectType.DATAFLOW_SIDE_EFFECTING),
    )(*bufs, st["send"], st["recv"], after)
    return list(out[:n_in]), list(out[n_in:])


def _pair_forward(lands, name):
    shapes = [jax.ShapeDtypeStruct(a.shape, a.dtype) for a in lands]
    return _rdma_call(lands, shapes, lambda ins, outs, x, y, c: (_plan_gather_near(outs, x, y, c), []),
                      (N_CHIPS - 1) * _n_half_pieces(lands), 0, name, aliases={i: i for i in range(len(lands))})


def _sum_block_rows(r, C):
    return _rows_tile(r, max(16, (1 << 18) // C // 16 * 16))


def _sum_pair(a, recv, cidx, name):
    nch, _, r, C = a.shape
    rb = _sum_block_rows(r, C)

    def body(c_ref, a_ref, r_ref, o_ref):
        o_ref[...] = (a_ref[...] + r_ref[...]).astype(BF16)

    blk = pl.BlockSpec((None, rb, C), lambda j, i, c: (j, i, 0))
    return pl.pallas_call(
        body, name=name,
        grid_spec=pltpu.PrefetchScalarGridSpec(
            num_scalar_prefetch=1, grid=(nch, r // rb),
            in_specs=[pl.BlockSpec((None, None, rb, C), lambda j, i, c: (j, c[0], i, 0)), blk], out_specs=blk),
        out_shape=jax.ShapeDtypeStruct((nch, r, C), BF16),
        compiler_params=_params("parallel", "parallel"),
    )(cidx, a, recv)


def _sum_chips(mine, recv, ids, name):
    nch, r, C = recv.shape
    rb = _sum_block_rows(r, C)

    def body(ids_ref, m_ref, *rest):
        r_refs, o_ref = rest[:nch], rest[nch]
        chip = ids_ref[1]
        own = m_ref[...].astype(F32)
        acc = jnp.where(chip == 0, own, r_refs[0][...].astype(F32))
        for q in range(1, nch):
            acc = acc + jnp.where(chip == q, own, r_refs[q][...].astype(F32))
        o_ref[...] = acc

    def slot(q):
        return pl.BlockSpec((None, rb, C), lambda i, ids: (jnp.where(ids[1] == q, (q + 1) % nch, q), i, 0))

    return pl.pallas_call(
        body, name=name,
        grid_spec=pltpu.PrefetchScalarGridSpec(
            num_scalar_prefetch=1, grid=(r // rb,),
            in_specs=[pl.BlockSpec((None, rb, C), lambda i, ids: (ids[1], i, 0))] + [slot(q) for q in range(nch)],
            out_specs=pl.BlockSpec((None, rb, C), lambda i, ids: (ids[0], i, 0))),
        out_shape=jax.ShapeDtypeStruct((N_CORES, r, C), F32),
        compiler_params=_params("parallel"),
    )(ids, mine, *([recv] * nch))


PACK_COLS = 1024
_SHARDED = ("ada_w", "w_in", "w_branch_a", "w_branch_b", "w_out", "ffn_w_up", "ffn_w_down")
_LAYER_KEYS = ("ada_w", "w_in", "w_a", "w_b", "w_o", "w_up", "w_down")
_SMALL = ("c_ctx", "ada_b", "norm1_w", "sgu_ln_w", "sgu_ln_b", "sgu_w", "sgu_b", "hgrn_lower_bounds", "hgrn_norm_w",
          "norm2_w", "ffn_conv_b", "final_norm_w")
_ORDER = ("c_ctx", "ada_w", "ada_b", "norm1_w", "w_in", "sgu_ln_w", "sgu_ln_b", "sgu_w", "sgu_b", "hgrn_lower_bounds",
          "hgrn_norm_w", "w_branch_a", "w_branch_b", "w_out", "norm2_w", "ffn_w_up", "ffn_conv_w", "ffn_conv_b",
          "ffn_w_down", "final_norm_w")


def _pad_to(v, n):
    return jnp.concatenate([v, jnp.zeros((n - v.shape[0],), v.dtype)]) if v.shape[0] < n else v


def _round_up(n, m):
    return (n + m - 1) // m * m


def _pack(arrays, n_pad):
    flat = jnp.concatenate([a.reshape(-1) for a in arrays])
    return _pad_to(flat, n_pad)


def _unpack(flat, like):
    out, off = [], 0
    for a in like:
        out.append(flat[off:off + a.size].reshape(a.shape))
        off += a.size
    return out


def kernel(x, c, ctx, c_ctx, ada_w, ada_b, norm1_w, w_in, sgu_ln_w, sgu_ln_b, sgu_w, sgu_b, hgrn_lower_bounds, hgrn_norm_w, w_branch_a, w_branch_b, w_out, norm2_w, ffn_w_up, ffn_conv_w, ffn_conv_b, ffn_w_down, final_norm_w, loss_target, m_c_ctx, m_ada_w, m_ada_b, m_norm1_w, m_w_in, m_sgu_ln_w, m_sgu_ln_b, m_sgu_w, m_sgu_b, m_hgrn_lower_bounds, m_hgrn_norm_w, m_w_branch_a, m_w_branch_b, m_w_out, m_norm2_w, m_ffn_w_up, m_ffn_conv_w, m_ffn_conv_b, m_ffn_w_down, m_final_norm_w, v_c_ctx, v_ada_w, v_ada_b, v_norm1_w, v_w_in, v_sgu_ln_w, v_sgu_ln_b, v_sgu_w, v_sgu_b, v_hgrn_lower_bounds, v_hgrn_norm_w, v_w_branch_a, v_w_branch_b, v_w_out, v_norm2_w, v_ffn_w_up, v_ffn_conv_w, v_ffn_conv_b, v_ffn_w_down, v_final_norm_w):
    w = dict(c_ctx=c_ctx, ada_w=ada_w, ada_b=ada_b, norm1_w=norm1_w, w_in=w_in, sgu_ln_w=sgu_ln_w, sgu_ln_b=sgu_ln_b,
             sgu_w=sgu_w, sgu_b=sgu_b, hgrn_lower_bounds=hgrn_lower_bounds, hgrn_norm_w=hgrn_norm_w, w_branch_a=w_branch_a,
             w_branch_b=w_branch_b, w_out=w_out, norm2_w=norm2_w, ffn_w_up=ffn_w_up, ffn_conv_w=ffn_conv_w,
             ffn_conv_b=ffn_conv_b, ffn_w_down=ffn_w_down, final_norm_w=final_norm_w)
    mom = dict(zip(_ORDER, (m_c_ctx, m_ada_w, m_ada_b, m_norm1_w, m_w_in, m_sgu_ln_w, m_sgu_ln_b, m_sgu_w, m_sgu_b,
                            m_hgrn_lower_bounds, m_hgrn_norm_w, m_w_branch_a, m_w_branch_b, m_w_out, m_norm2_w, m_ffn_w_up,
                            m_ffn_conv_w, m_ffn_conv_b, m_ffn_w_down, m_final_norm_w)))
    var = dict(zip(_ORDER, (v_c_ctx, v_ada_w, v_ada_b, v_norm1_w, v_w_in, v_sgu_ln_w, v_sgu_ln_b, v_sgu_w, v_sgu_b,
                            v_hgrn_lower_bounds, v_hgrn_norm_w, v_w_branch_a, v_w_branch_b, v_w_out, v_norm2_w, v_ffn_w_up,
                            v_ffn_conv_w, v_ffn_conv_b, v_ffn_w_down, v_final_norm_w)))
    depth, D = norm1_w.shape
    dff = ffn_conv_b.shape[1]
    ctx_rows, seq = ctx.shape[1], x.shape[1]

    assert depth == 2, "the lower-bound softmax is written for two layers"
    core = lax.axis_index("c")
    chip = 2 * lax.axis_index("x") + lax.axis_index("y")
    ids = jnp.stack([core, chip]).astype(jnp.int32)

    first, rest = _LAYER_KEYS[:2], _LAYER_KEYS[2:]
    shard = lambda l, k: w[_SHARDED[_LAYER_KEYS.index(k)]][l].astype(BF16)
    started, conv_full = {}, []

    def landing(s):
        return lax.dynamic_update_slice(lax.empty((N_CHIPS,) + s.shape, s.dtype), s[None], (chip,) + (0,) * s.ndim)

    def start_gather(l, keys, tag):
        lands = [landing(shard(l, k)) for k in keys]
        started[tag] = _split_start([], lands, lambda ins, lds, x, y, c: _plan_gather_far(lds, x, y, c),
                                    (N_CHIPS - 1) * _n_half_pieces(lands), f"gather_start_{tag}")
        return started[tag]["token"]

    def finish_gather(keys, tag, after):
        _, lands = _split_wait(started[tag], after, f"gather_wait_{tag}")
        return dict(zip(keys, _pair_forward(lands, f"gather_forward_{tag}")))

    def layer_weights(l, after):
        if l == 0:
            got = _gather_weights([landing(shard(0, k)) for k in first] + [landing(ffn_conv_w)], "gather_weights_first")
            conv_full.append(jnp.transpose(got[-1], (1, 2, 3, 0, 4)).reshape(depth, 9, dff))
            out = dict(zip(first, got), token=start_gather(0, rest, "rest_0"))
        else:
            out = dict(finish_gather(first, f"first_{l}", after), token=0.0)

        def late(after_late):
            more = finish_gather(rest, f"rest_{l}", after_late)
            more["late_token"] = 0.0
            if l + 1 < depth:
                more["late_token"] = start_gather(l + 1, first, f"first_{l + 1}") + start_gather(l + 1, rest, f"rest_{l + 1}")
            return more

        return dict(out, conv_w=conv_full[0][l], late=late)

    groups, order = {}, []

    def as_parts(gs):
        return [g.reshape(N_CHIPS, N_CORES, g.size // (N_CHIPS * N_CORES * g.shape[-1]), g.shape[-1]) for g in gs]

    def pair_start(tag, l, keys, gs):
        parts = as_parts(gs)
        lands = [lax.empty((N_CHIPS,) + p.shape[2:], p.dtype) for p in parts]
        groups[tag] = dict(l=l, keys=keys, pair=_split_start(parts, lands, _plan_pair, _n_pair_copies(parts),
                                                             f"reduce_pair_start_{tag}"))
        order.append(tag)
        return groups[tag]["pair"]["token"]

    def chips_start(tag, after):
        parts, other = _split_wait(groups[tag]["pair"], after, f"reduce_pair_wait_{tag}")
        sums = [_sum_pair(a, o, ids, f"sum_pair_{tag}_{i}") for i, (a, o) in enumerate(zip(parts, other))]
        lands = [lax.empty(s.shape, s.dtype) for s in sums]
        groups[tag]["chips"] = _split_start(sums, lands, _plan_chips, _n_chips_copies(sums), f"reduce_chips_start_{tag}")
        return groups[tag]["chips"]["token"]

    def chips_finish(tag, after):
        sums, recv = _split_wait(groups[tag]["chips"], after, f"reduce_chips_wait_{tag}")
        return {(groups[tag]["l"], k): _sum_chips(sums[i], recv[i], ids, f"sum_chips_{tag}_{i}")
                for i, k in enumerate(groups[tag]["keys"])}

    def on_layer_grads(l, stage, gs):
        if stage == "early":
            return pair_start(f"early_{l}", l, list(gs), list(gs.values()))
        if stage == "late":
            return pair_start(f"late_{l}", l, ["w_in"], [gs["w_in"]]) + chips_start(f"early_{l}", gs["w_in"])
        return chips_start(f"late_{l}", gs["after"])

    W = dict(ada_b=ada_b, norm1_w=norm1_w, sgu_ln_w=sgu_ln_w, sgu_ln_b=sgu_ln_b, sgu_w=sgu_w.astype(BF16),
             sgu_bt=jnp.swapaxes(sgu_b, 1, 2), hlb=hgrn_lower_bounds, hnw=hgrn_norm_w, norm2_w=norm2_w, conv_b=ffn_conv_b,
             final_norm_w=final_norm_w)
    xs = jnp.concatenate([ctx[0], x[0]], axis=0)
    cv = jnp.concatenate([c_ctx[None, :], c, jnp.zeros((14, D), F32)], axis=0)
    loss_local, dxs, G, sa = _local_step(xs, cv, loss_target[0], W, layer_weights, on_layer_grads, ctx_rows)
    loss = lax.psum(loss_local, ("x", "y", "c"))
    grad_x = dxs[ctx_rows:][None]

    pad8 = lambda a: jnp.pad(a, ((0, 8 - a.shape[0]), (0, 0)))
    fact = jnp.concatenate([pad8(sa[1:2].astype(F32))] + [pad8(G["dmod"][l][1].reshape(N_MOD, D)) for l in range(depth)]
                           + [pad8(G["dmod"][l][0].reshape(N_MOD, D)) for l in range(depth)], axis=0)
    facts = _gather_all(fact, "gather_mod_factors")
    lhs = jnp.concatenate([facts[:, 0].astype(BF16), jnp.broadcast_to(sa[0:1], (8, D))], axis=0)
    ada_cols = N_MOD * D // N_CHIPS
    g_ada = []
    for l in range(depth):
        lo_x, lo_c = 8 * (1 + l), 8 * (1 + depth + l)
        rhs = jnp.concatenate([facts[:, lo_x:lo_x + N_MOD].reshape(8, N_MOD * D),
                               facts[:, lo_c:lo_c + N_MOD].reshape(8, N_MOD * D)], axis=0)
        rhs = lax.dynamic_slice_in_dim(rhs, chip * ada_cols, ada_cols, axis=1).astype(BF16)
        g_ada.append(_mm_tn(lhs, rhs, F32, f"dw_ada_{l}"))

    dh = G["hlb1"][depth - 1]
    small_like = [w[k] for k in _SMALL] + [jnp.zeros((depth, 9, dff), F32)]
    small = [G["c_ctx"], jnp.stack(G["ada_b"]), jnp.stack(G["norm1_w"]), jnp.stack(G["sgu_ln_w"]), jnp.stack(G["sgu_ln_b"]),
             jnp.stack(G["sgu_w"]), jnp.stack(G["sgu_b"]), jnp.stack([-dh, dh]), jnp.stack(G["hnw"]), jnp.stack(G["norm2_w"]),
             jnp.stack(G["conv_b"]), G["final_norm_w"], jnp.stack(G["conv_w"])]
    n_small = sum(a.size for a in small)
    n_small_pad = _round_up(n_small, N_CORES * 16 * PACK_COLS)
    small_rows = n_small_pad // (N_CORES * PACK_COLS)
    small_rep = jnp.broadcast_to(_pack(small, n_small_pad).reshape(1, N_CORES, small_rows, PACK_COLS),
                                 (N_CHIPS, N_CORES, small_rows, PACK_COLS))
    small_parts = as_parts([small_rep])
    small_sums = [_sum_pair(small_parts[0], _reduce_pair(small_parts, "reduce_pair_small")[0], ids, "sum_pair_small")]
    groups["small"] = dict(l=None, keys=["small"], chips=_split_start(
        small_sums, [lax.empty(small_sums[0].shape, small_sums[0].dtype)], _plan_chips, _n_chips_copies(small_sums),
        "reduce_chips_start_small"))

    def gather_halves(halves, name):
        return dict(zip(halves, _gather_pair(list(halves.values()), name)))

    last = order[-1]
    halves = {}
    for tag in order[:-1]:
        halves.update(chips_finish(tag, groups["small"]["chips"]["token_array"]))
    reduced = gather_halves(halves, "gather_pair")
    grads, delta, new_m, new_v = {}, {}, {}, {}

    def adamw_sharded(i):
        k = _SHARDED[i]
        gs = g_ada if i == 0 else [reduced[(l, _LAYER_KEYS[i])].reshape(w[k].shape[1:]) for l in range(depth)]
        grads[k], delta[k], new_m[k], new_v[k] = _adamw(w[k], gs, mom[k], var[k], f"adamw_{k}")

    last_keys = groups[last]["keys"]
    for i in range(len(_SHARDED)):
        if _LAYER_KEYS[i] not in last_keys:
            adamw_sharded(i)
    halves = chips_finish(last, new_v[_SHARDED[-1]])
    halves.update(chips_finish("small", new_v[_SHARDED[-1]]))
    reduced.update(gather_halves(halves, "gather_pair_last"))
    for i in range(len(_SHARDED)):
        if _LAYER_KEYS[i] in last_keys:
            adamw_sharded(i)

    g_small = _unpack(reduced[(None, "small")].reshape(-1), small_like)
    grads.update(zip(_SMALL, g_small[:-1]))
    grads["ffn_conv_w"] = lax.dynamic_slice_in_dim(g_small[-1].reshape(depth, 3, 3, dff), chip * (dff // N_CHIPS),
                                                   dff // N_CHIPS, axis=3)
    packed = _SMALL + ("ffn_conv_w",)
    n_pad = _round_up(sum(w[k].size for k in packed), 16 * PACK_COLS)
    pack = lambda t: _pack([t[k] for k in packed], n_pad).reshape(1, -1, PACK_COLS)
    _, d, nm, nv = _adamw(pack(w), [pack(grads)[0]], pack(mom), pack(var), "adamw_packed")
    like = [w[k] for k in packed]
    for src, dst in ((d, delta), (nm, new_m), (nv, new_v)):
        dst.update(zip(packed, _unpack(src.reshape(-1), like)))

    return (loss, grad_x, *[grads[k] for k in _ORDER], *[delta[k] for k in _ORDER], *[new_m[k] for k in _ORDER],
            *[new_v[k] for k in _ORDER])
```

```python
import functools

import jax
import jax.numpy as jnp
from jax import lax
from jax.experimental import pallas as pl
from jax.experimental.pallas import tpu as pltpu

F32 = jnp.float32
BF16 = jnp.bfloat16

GRID_W = 64
HG_CHUNK = 64
SGU_CHUNK = 128
HEAD = 128
TB = 256
N_MOD = 6
RMS_EPS = 1e-6
LN_EPS = 1e-5
VMEM_LIMIT = 48 * 1024 * 1024
N_CHIPS = 4
N_CORES = 2

ADAM_LR = 0.001
ADAM_B1 = 0.9
ADAM_B2 = 0.999
ADAM_EPS = 1e-08
ADAM_WD = 0.01
ADAM_STEP = 10

_GELU_C = 0.7978845608028654
_GELU_A = 0.044715


def _sigmoid(x):
    return 0.5 * jnp.tanh(0.5 * x) + 0.5


def _silu(x):
    return x * _sigmoid(x)


def _silu_both(x):
    s = _sigmoid(x)
    return x * s, s * (1.0 + x * (1.0 - s))


def _dsilu(x):
    return _silu_both(x)[1]


def _gelu_both(x):
    x2 = x * x
    t = jnp.tanh(_GELU_C * (x + _GELU_A * x2 * x))
    h = 0.5 * (1.0 + t)
    return x * h, h + 0.5 * x * (1.0 - t * t) * (_GELU_C + 3.0 * _GELU_C * _GELU_A * x2)


def _gelu(x):
    return 0.5 * x * (1.0 + jnp.tanh(_GELU_C * (x + _GELU_A * x * x * x)))


def _dgelu(x):
    return _gelu_both(x)[1]


def _dot(a, b, ca, cb):
    return lax.dot_general(a, b, (((ca,), (cb,)), ((), ())), preferred_element_type=F32)


def _nn(a, b):
    return _dot(a, b, 1, 0)


def _nt(a, b):
    return _dot(a, b, 1, 1)


def _tn(a, b):
    return _dot(a, b, 0, 0)


def _params(*sem, vmem=VMEM_LIMIT):
    return pltpu.CompilerParams(dimension_semantics=sem if sem else None, vmem_limit_bytes=vmem)


def _stream_of(i, ctx_blocks):
    return (i >= ctx_blocks).astype(jnp.int32)


def _mm(a, b, mode, tm, tn, tk, out_dtype, name, b_chips=False, out_chips=False):
    a_pair, b_pair = isinstance(a, tuple), isinstance(b, tuple)
    assert (not a_pair or mode == "nt") and (not b_pair or (mode == "tn" and not b_chips))
    ashape = (a[0].shape[0], 2 * a[0].shape[1]) if a_pair else a.shape
    if b_pair:
        bshape = (b[0].shape[0], 2 * b[0].shape[1])
    elif not b_chips:
        bshape = b.shape
    else:
        bshape = (b.shape[1], N_CHIPS * b.shape[2])
    if mode == "nn":
        (M, K), (K2, N) = ashape, bshape
    elif mode == "nt":
        (M, K), (N, K2) = ashape, bshape
    else:
        (K, M), (K2, N) = ashape, bshape
    assert K == K2 and M % tm == 0 and N % tn == 0 and K % tk == 0, (name, ashape, bshape, tm, tn, tk)
    nk = K // tk
    if a_pair:
        n1 = a[0].shape[1] // tk
        assert a[0].shape[1] % tk == 0
        a_specs = [pl.BlockSpec((tm, tk), lambda j, i, k: (i, jnp.minimum(k, n1 - 1))),
                   pl.BlockSpec((tm, tk), lambda j, i, k: (i, jnp.maximum(k - n1, 0)))]
    elif mode == "tn":
        a_specs = [pl.BlockSpec((tk, tm), lambda j, i, k: (k, i))]
    else:
        a_specs = [pl.BlockSpec((tm, tk), lambda j, i, k: (i, k))]
    if b_pair:
        n1 = b[0].shape[1] // tn
        assert b[0].shape[1] % tn == 0
        b_specs = [pl.BlockSpec((tk, tn), lambda j, i, k: (k, jnp.minimum(j, n1 - 1))),
                   pl.BlockSpec((tk, tn), lambda j, i, k: (k, jnp.maximum(j - n1, 0)))]
    elif not b_chips:
        if mode == "nt":
            b_spec = pl.BlockSpec((tn, tk), lambda j, i, k: (j, k))
        else:
            b_spec = pl.BlockSpec((tk, tn), lambda j, i, k: (k, j))
    else:
        cols = b.shape[2]
        if mode == "nn":
            per = cols // tn
            assert cols % tn == 0
            b_spec = pl.BlockSpec((None, tk, tn), lambda j, i, k: (j // per, k, j % per))
        else:
            per = cols // tk
            assert mode == "nt" and cols % tk == 0
            b_spec = pl.BlockSpec((None, tn, tk), lambda j, i, k: (k // per, j, k % per))
    if not b_pair:
        b_specs = [b_spec]
    if out_chips:
        per_o = (N // N_CHIPS) // tn
        assert (N // N_CHIPS) % tn == 0
        o_spec = pl.BlockSpec((None, tm, tn), lambda j, i, k: (j // per_o, i, j % per_o))
        o_shape = (N_CHIPS, M, N // N_CHIPS)
    else:
        o_spec = pl.BlockSpec((tm, tn), lambda j, i, k: (i, j))
        o_shape = (M, N)
    ca, cb = {"nn": (1, 0), "nt": (1, 1), "tn": (0, 0)}[mode]

    in_place = nk == 1
    na, nb = len(a_specs), len(b_specs)

    def body(*refs):
        a_refs, b_refs, rest = refs[:na], refs[na:na + nb], refs[na + nb:]
        if in_place:
            (o_ref,) = rest
        else:
            o_ref, acc = rest
        k = pl.program_id(2)
        av = a_refs[0][...] if not a_pair else jnp.where(k < n1, a_refs[0][...], a_refs[1][...])
        bv = b_refs[0][...] if not b_pair else jnp.where(pl.program_id(0) < n1, b_refs[0][...], b_refs[1][...])
        part = _dot(av, bv, ca, cb)
        if in_place:
            o_ref[...] = part.astype(out_dtype)
            return

        @pl.when(k == 0)
        def _():
            acc[...] = jnp.zeros_like(acc)

        acc[...] += part

        @pl.when(k == nk - 1)
        def _():
            o_ref[...] = acc[...].astype(out_dtype)

    ins = (list(a) if a_pair else [a]) + (list(b) if b_pair else [b])
    return pl.pallas_call(
        body, name=name, grid=(N // tn, M // tm, nk), in_specs=a_specs + b_specs, out_specs=o_spec,
        out_shape=jax.ShapeDtypeStruct(o_shape, out_dtype),
        scratch_shapes=[] if in_place else [pltpu.VMEM((tm, tn), F32)],
        compiler_params=_params("parallel", "parallel", "arbitrary"),
    )(*ins)


def _tile(n, pref):
    if n <= pref:
        return n
    best = None
    for t in range(128, pref + 1, 128):
        if n % t == 0:
            best = t
    assert best is not None, (n, pref)
    return best


def _rows_tile(n, pref):
    if n <= pref:
        return n
    best = None
    for t in range(16, pref + 1, 16):
        if n % t == 0:
            best = t
    assert best is not None, (n, pref)
    return best


def _mm_nn_w(a, wg, out_dtype, name):
    M, K = a.shape
    return _mm(a, wg, "nn", _rows_tile(M, 2176), _tile(wg.shape[2], 1536), _tile(K, 1536), out_dtype, name, b_chips=True)


def _mm_nt_w(a, wg, out_dtype, name):
    M = a[0].shape[0] if isinstance(a, tuple) else a.shape[0]
    return _mm(a, wg, "nt", _rows_tile(M, 1088), _tile(wg.shape[1], 1024), _tile(wg.shape[2], 1536), out_dtype, name,
               b_chips=True)


def _mm_tn(a, b, out_dtype, name, out_chips=False):
    K, M = a.shape
    N = 2 * b[0].shape[1] if isinstance(b, tuple) else b.shape[1]
    ncol = N // N_CHIPS if out_chips else N
    tm, tn = _tile(M, 1408), _tile(ncol, 1408)
    if tm * tn > 1408 * 1152:
        tn = _tile(ncol, 1152)
    tk = _rows_tile(K, 1088 if isinstance(b, tuple) else 2176)
    return _mm(a, b, "tn", tm, tn, tk, out_dtype, name, out_chips=out_chips)


def _mod_fwd(cv, wg, b, name):
    R, D = cv.shape
    tn = wg.shape[2]
    N = N_CHIPS * tn

    def body(cv_ref, w_ref, b_ref, mod_ref, sa_ref):
        sa = _silu(cv_ref[...]).astype(BF16)
        sa_ref[...] = sa
        mod_ref[...] = _nn(sa, w_ref[...]) + b_ref[...]

    return pl.pallas_call(
        body, name=name, grid=(N_CHIPS,),
        in_specs=[pl.BlockSpec((R, D), lambda j: (0, 0)), pl.BlockSpec((None, D, tn), lambda j: (j, 0, 0)),
                  pl.BlockSpec((1, tn), lambda j: (0, j))],
        out_specs=[pl.BlockSpec((R, tn), lambda j: (0, j)), pl.BlockSpec((R, D), lambda j: (0, 0))],
        out_shape=[jax.ShapeDtypeStruct((R, N), F32), jax.ShapeDtypeStruct((R, D), BF16)],
        compiler_params=_params("arbitrary"),
    )(cv, wg, b)


def _cvec_bwd(dmod, wg, cv, name):
    R, N = dmod.shape
    D = wg.shape[1]
    tk = wg.shape[2]
    nk = N_CHIPS

    def body(dm_ref, w_ref, cv_ref, o_ref):
        k = pl.program_id(0)

        @pl.when(k == 0)
        def _():
            o_ref[...] = jnp.zeros_like(o_ref)

        o_ref[...] += _nt(dm_ref[...].astype(BF16), w_ref[...])

        @pl.when(k == nk - 1)
        def _():
            o_ref[...] = o_ref[...] * _dsilu(cv_ref[...])

    return pl.pallas_call(
        body, name=name, grid=(nk,),
        in_specs=[pl.BlockSpec((R, tk), lambda k: (0, k)), pl.BlockSpec((None, D, tk), lambda k: (k, 0, 0)),
                  pl.BlockSpec((R, D), lambda k: (0, 0))],
        out_specs=pl.BlockSpec((R, D), lambda k: (0, 0)),
        out_shape=jax.ShapeDtypeStruct((R, D), F32),
        compiler_params=_params("arbitrary"),
    )(dmod, wg, cv)


def _norm_mod(x, nw, mod, which, ctx_rows, name):
    T, D = x.shape
    cb = ctx_rows // TB

    def body(x_ref, nw_ref, mod_ref, h_ref):
        xv = x_ref[...]
        r = lax.rsqrt(jnp.mean(xv * xv, axis=-1, keepdims=True) + RMS_EPS)
        y = xv * r * nw_ref[...]
        sh = mod_ref[which:which + 1, :]
        sc = mod_ref[which + 1:which + 2, :]
        h_ref[...] = (y * (1.0 + sc) + sh).astype(BF16)

    return pl.pallas_call(
        body, name=name, grid=(T // TB,),
        in_specs=[pl.BlockSpec((TB, D), lambda i: (i, 0)), pl.BlockSpec((1, D), lambda i: (0, 0)),
                  pl.BlockSpec((None, N_MOD, D), lambda i: (_stream_of(i, cb), 0, 0))],
        out_specs=pl.BlockSpec((TB, D), lambda i: (i, 0)),
        out_shape=jax.ShapeDtypeStruct((T, D), BF16),
        compiler_params=_params("parallel"),
    )(x, nw, mod)


def _norm_mod_bwd(dh, x, dres, nw, mod, which, ctx_rows, name):
    T, D = x.shape
    cb = ctx_rows // TB

    def body(dh_ref, x_ref, dres_ref, nw_ref, mod_ref, dx_ref, dm_ref, dnw_ref):
        i = pl.program_id(0)

        @pl.when(i == 0)
        def _():
            dnw_ref[...] = jnp.zeros_like(dnw_ref)

        @pl.when((i == 0) | (i == cb))
        def _():
            dm_ref[...] = jnp.zeros_like(dm_ref)

        xv = x_ref[...]
        dh = dh_ref[...]
        r = lax.rsqrt(jnp.mean(xv * xv, axis=-1, keepdims=True) + RMS_EPS)
        xh = xv * r
        nwv = nw_ref[...]
        sc = mod_ref[which + 1:which + 2, :]
        y = xh * nwv
        dm_ref[0:1, :] += jnp.sum(dh, axis=0, keepdims=True)
        dm_ref[1:2, :] += jnp.sum(dh * y, axis=0, keepdims=True)
        dy = dh * (1.0 + sc)
        dnw_ref[...] += jnp.sum(dy * xh, axis=0, keepdims=True)
        dxh = dy * nwv
        dx_ref[...] = dres_ref[...] + r * (dxh - xh * jnp.mean(dxh * xh, axis=-1, keepdims=True))

    return pl.pallas_call(
        body, name=name, grid=(T // TB,),
        in_specs=[pl.BlockSpec((TB, D), lambda i: (i, 0)), pl.BlockSpec((TB, D), lambda i: (i, 0)),
                  pl.BlockSpec((TB, D), lambda i: (i, 0)), pl.BlockSpec((1, D), lambda i: (0, 0)),
                  pl.BlockSpec((None, N_MOD, D), lambda i: (_stream_of(i, cb), 0, 0))],
        out_specs=[pl.BlockSpec((TB, D), lambda i: (i, 0)),
                   pl.BlockSpec((None, 2, D), lambda i: (_stream_of(i, cb), 0, 0)),
                   pl.BlockSpec((1, D), lambda i: (0, 0))],
        out_shape=[jax.ShapeDtypeStruct((T, D), F32), jax.ShapeDtypeStruct((2, 2, D), F32),
                   jax.ShapeDtypeStruct((1, D), F32)],
        compiler_params=_params("arbitrary"),
    )(dh, x, dres, nw, mod)


def _scan_chunk(n, rev, n_ctx, n_all):
    if not rev:
        return n
    return jnp.where(n < n_ctx, n_ctx - 1 - n, n_all - 1 + n_ctx - n)


def _cumsum_rows(x, rev):
    rows = x.shape[0]
    row = lax.broadcasted_iota(jnp.int32, (rows, 1), 0)
    s = 1
    while s < rows:
        if not rev:
            x = x + jnp.where(row >= s, pltpu.roll(x, s, 0), 0.0)
        else:
            x = x + jnp.where(row < rows - s, pltpu.roll(x, rows - s, 0), 0.0)
        s *= 2
    return x


def _lower_bound(hlb_ref, layer):
    h = hlb_ref[...]
    if layer == 0:
        return jnp.zeros_like(h[0:1, :])
    return _sigmoid(h[1:2, :] - h[0:1, :])


HG_STEP = 4


def _step_rows(j, rev, backward):
    sub = j if rev == backward else HG_STEP - 1 - j
    return slice(sub * HG_CHUNK, (sub + 1) * HG_CHUNK)


def _hgrn_gates(q_ref, f_ref, hlb_ref, layer, rev, rows):
    lb = _lower_bound(hlb_ref, layer)
    z = f_ref[rows, :]
    sig = 1.0 / (1.0 + jnp.exp(-z))
    fg = lb + (1.0 - lb) * sig
    kk = (1.0 - lb) * (1.0 - sig)
    g = jnp.log(fg)
    b = _cumsum_rows(g, rev)
    bt = jnp.sum(g, axis=0, keepdims=True)
    mid = HG_CHUNK // 2
    r = b[mid:mid + 1, :] if rev else b[mid - 1:mid, :]
    qh = _silu(q_ref[rows, :])
    return lb, sig, fg, kk, b, bt, r, qh


def _tri_mask(rev):
    t = lax.broadcasted_iota(jnp.int32, (HG_CHUNK, HG_CHUNK), 0)
    s = lax.broadcasted_iota(jnp.int32, (HG_CHUNK, HG_CHUNK), 1)
    return (s >= t) if rev else (s <= t)


def _hgrn_fwd(parts, hlb, layer, rev, ctx_rows, name, o_add=None):
    T = parts.shape[0]
    D = hlb.shape[1] // 2
    nh = D // HEAD
    n_all, n_ctx = T // HG_CHUNK, ctx_rows // HG_CHUNK
    assert n_all % HG_STEP == 0 and n_ctx % HG_STEP == 0
    n_steps = n_all // HG_STEP
    block = functools.partial(_scan_chunk, rev=rev, n_ctx=n_ctx // HG_STEP, n_all=n_steps)
    fcol = 2 if rev else 1

    def body(q_ref, f_ref, i_ref, hlb_ref, *rest):
        if o_add is None:
            o_ref, st_ref, s_scr = rest
        else:
            oa_ref, o_ref, st_ref, s_scr = rest
        n = pl.program_id(0)

        @pl.when(n == 0)
        def _():
            s_scr[...] = jnp.zeros_like(s_scr)

        mask = _tri_mask(rev)
        hs = [slice(h * HEAD, (h + 1) * HEAD) for h in range(nh)]
        for j in range(HG_STEP):
            rows = _step_rows(j, rev, False)
            lb, sig, fg, kk, b, bt, r, qh = _hgrn_gates(q_ref, f_ref, hlb_ref, layer, rev, rows)
            qr = (qh * jnp.exp(b - r)).astype(BF16)
            kr = (kk * jnp.exp(r - b)).astype(BF16)
            qe = (qh * jnp.exp(b)).astype(BF16)
            ke = (kk * jnp.exp(bt - b)).astype(BF16)
            dec = jnp.exp(bt)
            v = i_ref[rows, :].astype(BF16)
            st = [s_scr[h] for h in range(nh)]
            a_raw = [_nt(qr[:, sl], kr[:, sl]) for sl in hs]
            o_int = [_nt(qe[:, sl], st[h].astype(BF16)) for h, sl in enumerate(hs)]
            kv = [_tn(v[:, sl], ke[:, sl]) for sl in hs]
            for h, sl in enumerate(hs):
                st_ref[j, h] = st[h]
                o = _nn(jnp.where(mask, a_raw[h], 0.0).astype(BF16), v[:, sl]) + o_int[h]
                if o_add is not None:
                    o = o + oa_ref[rows, sl]
                o_ref[rows, sl] = o
                s_scr[h] = st[h] * dec[:, sl] + kv[h]

    cspec = lambda col: pl.BlockSpec((HG_STEP * HG_CHUNK, D), lambda n: (block(n), col))
    ins = [parts, parts, parts, hlb]
    specs = [cspec(0), cspec(fcol), cspec(3), pl.BlockSpec((2, D), lambda n: (0, 1 if rev else 0))]
    if o_add is not None:
        ins.append(o_add)
        specs.append(cspec(0))
    return pl.pallas_call(
        body, name=name, grid=(n_steps,), in_specs=specs,
        out_specs=[cspec(0), pl.BlockSpec((HG_STEP, nh, HEAD, HEAD), lambda n: (n, 0, 0, 0))],
        out_shape=[jax.ShapeDtypeStruct((T, D), F32), jax.ShapeDtypeStruct((n_all, nh, HEAD, HEAD), F32)],
        scratch_shapes=[pltpu.VMEM((nh, HEAD, HEAD), F32)],
        compiler_params=_params("arbitrary"),
    )(*ins)


def _hgrn_bwd(parts, hlb, do, states, layer, rev, ctx_rows, name, other=None, dparts=None):
    T = parts.shape[0]
    D = hlb.shape[1] // 2
    nh = D // HEAD
    n_all, n_ctx = T // HG_CHUNK, ctx_rows // HG_CHUNK
    assert n_all % HG_STEP == 0 and n_ctx % HG_STEP == 0
    n_steps = n_all // HG_STEP
    step = lambda m: n_steps - 1 - m
    block = lambda m: _scan_chunk(step(m), rev, n_ctx // HG_STEP, n_steps)
    fcol = 2 if rev else 1
    has_add = other is not None
    assert not has_add or rev

    def body(q_ref, f_ref, i_ref, hlb_ref, do_ref, st_ref, *rest):
        if has_add:
            dqa_ref, dza_ref, dia_ref, _, out_ref, dlb_ref, ds_scr = rest
            dq_ref, dz_ref, di_ref = out_ref.at[:, 0:D], out_ref.at[:, 2 * D:3 * D], out_ref.at[:, 3 * D:4 * D]
            out_ref[:, D:2 * D] = dza_ref[...]
        else:
            dq_ref, dz_ref, di_ref, dlb_ref, ds_scr = rest
        m = pl.program_id(0)

        @pl.when(m == 0)
        def _():
            ds_scr[...] = jnp.zeros_like(ds_scr)
            dlb_ref[...] = jnp.zeros_like(dlb_ref)

        mask = _tri_mask(rev)
        hs = [slice(h * HEAD, (h + 1) * HEAD) for h in range(nh)]
        for j in range(HG_STEP):
            rows = _step_rows(j, rev, True)
            slot = HG_STEP - 1 - j
            lb, sig, fg, kk, b, bt, r, qh = _hgrn_gates(q_ref, f_ref, hlb_ref, layer, rev, rows)
            e_qr = jnp.exp(b - r)
            e_kr = jnp.exp(r - b)
            e_b = jnp.exp(b)
            e_ke = jnp.exp(bt - b)
            dec = jnp.exp(bt)
            qr = (qh * e_qr).astype(BF16)
            kr = (kk * e_kr).astype(BF16)
            qe = (qh * e_b).astype(BF16)
            ke = (kk * e_ke).astype(BF16)
            v = i_ref[rows, :].astype(BF16)
            dov = do_ref[rows, :].astype(BF16)
            st = [st_ref[slot, h] for h in range(nh)]
            dst = [ds_scr[h] for h in range(nh)]
            stb = [t.astype(BF16) for t in st]
            dstb = [t.astype(BF16) for t in dst]
            a_raw = [_nt(qr[:, sl], kr[:, sl]) for sl in hs]
            da_raw = [_nt(dov[:, sl], v[:, sl]) for sl in hs]
            dq_int = [_nn(dov[:, sl], stb[h]) for h, sl in enumerate(hs)]
            dk_int = [_nn(v[:, sl], dstb[h]) for h, sl in enumerate(hs)]
            dv_int = [_nt(ke[:, sl], dstb[h]) for h, sl in enumerate(hs)]
            ds_new = [_tn(dov[:, sl], qe[:, sl]) for sl in hs]
            a = [jnp.where(mask, t, 0.0).astype(BF16) for t in a_raw]
            da = [jnp.where(mask, t, 0.0).astype(BF16) for t in da_raw]
            dv_parts = [_tn(a[h], dov[:, sl]) + dv_int[h] for h, sl in enumerate(hs)]
            dq_parts = [_nn(da[h], kr[:, sl]) * e_qr[:, sl] + dq_int[h] * e_b[:, sl] for h, sl in enumerate(hs)]
            dki_parts = [dk_int[h] * e_ke[:, sl] for h, sl in enumerate(hs)]
            dk_parts = [_tn(da[h], qr[:, sl]) * e_kr[:, sl] + dki_parts[h] for h, sl in enumerate(hs)]
            dbt_parts = [dec[:, sl] * jnp.sum(st[h] * dst[h], axis=0, keepdims=True) for h, sl in enumerate(hs)]
            for h, sl in enumerate(hs):
                ds_scr[h] = dst[h] * dec[:, sl] + ds_new[h]
            dq = jnp.concatenate(dq_parts, axis=1)
            dk = jnp.concatenate(dk_parts, axis=1)
            dki = jnp.concatenate(dki_parts, axis=1)
            dv = jnp.concatenate(dv_parts, axis=1)
            dbt = jnp.concatenate(dbt_parts, axis=1) + jnp.sum(kk * dki, axis=0, keepdims=True)
            db = qh * dq - kk * dk
            dg = _cumsum_rows(db, not rev) + dbt
            df = dg / fg - dk
            dz_ref[rows, :] = (df * (1.0 - lb) * sig * (1.0 - sig)).astype(BF16)
            dlb_ref[...] += jnp.sum(df * (1.0 - sig), axis=0, keepdims=True)
            dqr = dq * _dsilu(q_ref[rows, :])
            if has_add:
                dqr = dqr + dqa_ref[rows, :]
                dv = dv + dia_ref[rows, :]
            dq_ref[rows, :] = dqr.astype(dq_ref.dtype)
            di_ref[rows, :] = dv.astype(di_ref.dtype)

        @pl.when(m == n_steps - 1)
        def _():
            lb = _lower_bound(hlb_ref, layer)
            if layer == 0:
                dlb_ref[...] = jnp.zeros_like(dlb_ref)
            else:
                dlb_ref[...] = dlb_ref[...] * lb * (1.0 - lb)

    cspec = lambda col: pl.BlockSpec((HG_STEP * HG_CHUNK, D), lambda m: (block(m), col))
    ins = [parts, parts, parts, hlb, do, states]
    specs = [cspec(0), cspec(fcol), cspec(3), pl.BlockSpec((2, D), lambda m: (0, 1 if rev else 0)), cspec(0),
             pl.BlockSpec((HG_STEP, nh, HEAD, HEAD), lambda m: (step(m), 0, 0, 0))]
    dlb_spec = pl.BlockSpec((1, D), lambda m: (0, 0))
    dlb_shape = jax.ShapeDtypeStruct((1, D), F32)
    if has_add:
        return pl.pallas_call(
            body, name=name, grid=(n_steps,),
            in_specs=specs + [cspec(0), cspec(0), cspec(0), pl.BlockSpec(memory_space=pl.ANY)],
            out_specs=[pl.BlockSpec((HG_STEP * HG_CHUNK, 4 * D), lambda m: (block(m), 0)), dlb_spec],
            out_shape=[jax.ShapeDtypeStruct(dparts.shape, dparts.dtype), dlb_shape],
            scratch_shapes=[pltpu.VMEM((nh, HEAD, HEAD), F32)], input_output_aliases={len(ins) + 3: 0},
            compiler_params=_params("arbitrary"),
        )(*ins, *other, dparts)
    return pl.pallas_call(
        body, name=name, grid=(n_steps,), in_specs=specs,
        out_specs=[cspec(0), cspec(0), cspec(0), dlb_spec],
        out_shape=[jax.ShapeDtypeStruct((T, D), F32), jax.ShapeDtypeStruct((T, D), BF16),
                   jax.ShapeDtypeStruct((T, D), F32), dlb_shape],
        scratch_shapes=[pltpu.VMEM((nh, HEAD, HEAD), F32)],
        compiler_params=_params("arbitrary"),
    )(*ins)


def _sgu_ln(gv, lnw_ref, lnb_ref):
    mu = jnp.mean(gv, axis=-1, keepdims=True)
    xc = gv - mu
    rstd = lax.rsqrt(jnp.mean(xc * xc, axis=-1, keepdims=True) + LN_EPS)
    xh = xc * rstd
    return xh, rstd, xh * lnw_ref[...] + lnb_ref[...]


def _sgu_fwd(parts, lnw, lnb, w, bt, name):
    T = parts.shape[0]
    D = lnw.shape[1]
    G = D // HEAD

    def body(u_ref, v_ref, lnw_ref, lnb_ref, w_ref, bt_ref, ya_ref):
        gu = _gelu(u_ref[...])
        _, _, vn = _sgu_ln(_gelu(v_ref[...]), lnw_ref, lnb_ref)
        vnb = vn.astype(BF16)
        for g in range(G):
            sl = slice(g * HEAD, (g + 1) * HEAD)
            mixed = _nn(w_ref[g], vnb[:, sl]) + bt_ref[:, g:g + 1]
            ya_ref[:, sl] = (gu[:, sl] * mixed).astype(BF16)

    return pl.pallas_call(
        body, name=name, grid=(T // SGU_CHUNK,),
        in_specs=[pl.BlockSpec((SGU_CHUNK, D), lambda n: (n, 4)), pl.BlockSpec((SGU_CHUNK, D), lambda n: (n, 5)),
                  pl.BlockSpec((1, D), lambda n: (0, 0)), pl.BlockSpec((1, D), lambda n: (0, 0)),
                  pl.BlockSpec((G, SGU_CHUNK, SGU_CHUNK), lambda n: (0, 0, 0)),
                  pl.BlockSpec((SGU_CHUNK, G), lambda n: (0, 0))],
        out_specs=pl.BlockSpec((SGU_CHUNK, D), lambda n: (n, 0)),
        out_shape=jax.ShapeDtypeStruct((T, D), BF16),
        compiler_params=_params("parallel"),
    )(parts, parts, lnw, lnb, w, bt)


def _sgu_bwd(parts, dya, lnw, lnb, w, bt, dparts, name):
    T = parts.shape[0]
    D = lnw.shape[1]
    G = D // HEAD

    def body(u_ref, v_ref, dya_ref, lnw_ref, lnb_ref, w_ref, bt_ref, dparts_in,
             duv_ref, dw_ref, dbt_ref, dlnw_ref, dlnb_ref, dvn_scr):
        du_ref = duv_ref.at[:, 0:D]
        dv_ref = duv_ref.at[:, D:2 * D]
        n = pl.program_id(0)

        @pl.when(n == 0)
        def _():
            dw_ref[...] = jnp.zeros_like(dw_ref)
            dbt_ref[...] = jnp.zeros_like(dbt_ref)
            dlnw_ref[...] = jnp.zeros_like(dlnw_ref)
            dlnb_ref[...] = jnp.zeros_like(dlnb_ref)

        gu, dgu = _gelu_both(u_ref[...])
        gv, dgv_dv = _gelu_both(v_ref[...])
        xh, rstd, vn = _sgu_ln(gv, lnw_ref, lnb_ref)
        vnb = vn.astype(BF16)
        dya = dya_ref[...]
        lane = lax.broadcasted_iota(jnp.int32, (SGU_CHUNK, G), 1)
        dbt = jnp.zeros((SGU_CHUNK, G), F32)
        for g in range(G):
            sl = slice(g * HEAD, (g + 1) * HEAD)
            wg = w_ref[g]
            mixed = _nn(wg, vnb[:, sl]) + bt_ref[:, g:g + 1]
            dmix = dya[:, sl] * gu[:, sl]
            du_ref[:, sl] = (dya[:, sl] * mixed * dgu[:, sl]).astype(BF16)
            dmb = dmix.astype(BF16)
            dvn_scr[:, sl] = _tn(wg, dmb)
            dw_ref[g] += _nt(dmb, vnb[:, sl])
            dbt = dbt + jnp.where(lane == g, jnp.sum(dmix, axis=1, keepdims=True), 0.0)
        dbt_ref[...] += dbt
        dvn = dvn_scr[...]
        dlnw_ref[...] += jnp.sum(dvn * xh, axis=0, keepdims=True)
        dlnb_ref[...] += jnp.sum(dvn, axis=0, keepdims=True)
        dxh = dvn * lnw_ref[...]
        dgv = rstd * (dxh - jnp.mean(dxh, axis=-1, keepdims=True) - xh * jnp.mean(dxh * xh, axis=-1, keepdims=True))
        dv_ref[...] = (dgv * dgv_dv).astype(BF16)

    row = lambda col: pl.BlockSpec((SGU_CHUNK, D), lambda n: (n, col))
    vec = pl.BlockSpec((1, D), lambda n: (0, 0))
    wsp = pl.BlockSpec((G, SGU_CHUNK, SGU_CHUNK), lambda n: (0, 0, 0))
    bsp = pl.BlockSpec((SGU_CHUNK, G), lambda n: (0, 0))
    return pl.pallas_call(
        body, name=name, grid=(T // SGU_CHUNK,),
        in_specs=[row(4), row(5), row(0), vec, vec, wsp, bsp, pl.BlockSpec(memory_space=pl.ANY)],
        out_specs=[pl.BlockSpec((SGU_CHUNK, 2 * D), lambda n: (n, 2)), wsp, bsp, vec, vec],
        out_shape=[jax.ShapeDtypeStruct(dparts.shape, dparts.dtype),
                   jax.ShapeDtypeStruct((G, SGU_CHUNK, SGU_CHUNK), F32), jax.ShapeDtypeStruct((SGU_CHUNK, G), F32),
                   jax.ShapeDtypeStruct((1, D), F32), jax.ShapeDtypeStruct((1, D), F32)],
        scratch_shapes=[pltpu.VMEM((SGU_CHUNK, D), F32)], input_output_aliases={7: 0},
        compiler_params=_params("arbitrary"),
    )(parts, parts, dya, lnw, lnb, w, bt, dparts)


TBT = 256
VMEM_LIMIT_TOKEN_OUT = 58 * 1024 * 1024


def _rows_weight_spec(wg):
    return pl.BlockSpec(wg.shape, lambda i: (0, 0, 0))


def _full(w_ref):
    return w_ref[...].reshape(w_ref.shape[0] * w_ref.shape[1], w_ref.shape[2])


def _token_out_fwd(o, parts, ya, x, mod, hnw, wa, wb, wo, ctx_rows, name):
    T, D = x.shape
    nh = D // HEAD
    cb = ctx_rows // TBT

    def body(o_ref, og_ref, ga_ref, gb_ref, ya_ref, x_ref, mod_ref, hnw_ref, wa_ref, wb_ref, wo_ref,
             yb_ref, pa_ref, pb_ref, mg_ref, tmo_ref, xm_ref):
        ov = o_ref[...]
        so = _silu(og_ref[...])
        nw = hnw_ref[...]
        for h in range(nh):
            sl = slice(h * HEAD, (h + 1) * HEAD)
            seg = ov[:, sl]
            r = lax.rsqrt(jnp.mean(seg * seg, axis=-1, keepdims=True) + RMS_EPS)
            yb_ref[:, sl] = (seg * r * nw * so[:, sl]).astype(BF16)
        pa = _nn(ya_ref[...], _full(wa_ref))
        pb = _nn(yb_ref[...], _full(wb_ref))
        pa_ref[...] = pa
        pb_ref[...] = pb
        mg = (_sigmoid(ga_ref[...]) * pa + _sigmoid(gb_ref[...]) * pb).astype(BF16)
        mg_ref[...] = mg
        out = _nn(mg, _full(wo_ref))
        tmo_ref[...] = out
        xm_ref[...] = x_ref[...] + mod_ref[2:3, :] * out

    row = lambda col: pl.BlockSpec((TBT, D), lambda i: (i, col))
    wsp = _rows_weight_spec(wa)
    sd = lambda dt: jax.ShapeDtypeStruct((T, D), dt)
    return pl.pallas_call(
        body, name=name, grid=(T // TBT,),
        in_specs=[row(0), row(6), row(7), row(8), row(0), row(0),
                  pl.BlockSpec((None, N_MOD, D), lambda i: (_stream_of(i, cb), 0, 0)),
                  pl.BlockSpec((1, HEAD), lambda i: (0, 0)), wsp, wsp, wsp],
        out_specs=[row(0)] * 6,
        out_shape=[sd(BF16), sd(F32), sd(F32), sd(BF16), sd(F32), sd(F32)],
        compiler_params=_params("parallel", vmem=VMEM_LIMIT_TOKEN_OUT),
    )(o, parts, parts, parts, ya, x, mod, hnw, wa, wb, wo)


def _token_out_bwd(dx, tmo, pa, pb, o, parts, mod, hnw, wa, wb, wo, ctx_rows, name):
    T, D = dx.shape
    nh = D // HEAD
    cb = ctx_rows // TBT

    def body(dx_ref, tmo_ref, pa_ref, pb_ref, o_ref, og_ref, ga_ref, gb_ref, mod_ref, hnw_ref, wa_ref, wb_ref, wo_ref,
             dout_ref, dpa_ref, dpb_ref, dgate_ref, dya_ref, do_ref, dg1_ref, dhnw_ref):
        i = pl.program_id(0)

        @pl.when(i == 0)
        def _():
            dhnw_ref[...] = jnp.zeros_like(dhnw_ref)

        @pl.when((i == 0) | (i == cb))
        def _():
            dg1_ref[...] = jnp.zeros_like(dg1_ref)

        dxv = dx_ref[...]
        dg1_ref[...] += jnp.sum(dxv * tmo_ref[...], axis=0, keepdims=True)
        dout = (dxv * mod_ref[2:3, :]).astype(BF16)
        dout_ref[...] = dout
        dmg = _nt(dout, _full(wo_ref))
        sa = _sigmoid(ga_ref[...])
        sb = _sigmoid(gb_ref[...])
        dpa = (dmg * sa).astype(BF16)
        dpb = (dmg * sb).astype(BF16)
        dpa_ref[...] = dpa
        dpb_ref[...] = dpb
        dgate_ref[:, D:2 * D] = (dmg * pa_ref[...] * sa * (1.0 - sa)).astype(BF16)
        dgate_ref[:, 2 * D:3 * D] = (dmg * pb_ref[...] * sb * (1.0 - sb)).astype(BF16)
        dya_ref[...] = _nt(dpa, _full(wa_ref))
        dyb = _nt(dpb, _full(wb_ref))
        so, dso = _silu_both(og_ref[...])
        ov = o_ref[...]
        nw = hnw_ref[...]
        dnw = jnp.zeros((1, HEAD), F32)
        for h in range(nh):
            sl = slice(h * HEAD, (h + 1) * HEAD)
            seg = ov[:, sl]
            r = lax.rsqrt(jnp.mean(seg * seg, axis=-1, keepdims=True) + RMS_EPS)
            oh = seg * r
            dn = dyb[:, sl] * so[:, sl]
            dgate_ref[:, sl] = (dyb[:, sl] * oh * nw * dso[:, sl]).astype(BF16)
            dnw = dnw + jnp.sum(dn * oh, axis=0, keepdims=True)
            doh = dn * nw
            do_ref[:, sl] = r * (doh - oh * jnp.mean(doh * oh, axis=-1, keepdims=True))
        dhnw_ref[...] += dnw

    row = lambda col: pl.BlockSpec((TBT, D), lambda i: (i, col))
    wsp = _rows_weight_spec(wa)
    sd = lambda dt: jax.ShapeDtypeStruct((T, D), dt)
    return pl.pallas_call(
        body, name=name, grid=(T // TBT,),
        in_specs=[row(0), row(0), row(0), row(0), row(0), row(6), row(7), row(8),
                  pl.BlockSpec((None, N_MOD, D), lambda i: (_stream_of(i, cb), 0, 0)),
                  pl.BlockSpec((1, HEAD), lambda i: (0, 0)), wsp, wsp, wsp],
        out_specs=[row(0)] * 3 + [pl.BlockSpec((TBT, 3 * D), lambda i: (i, 2)), row(0), row(0),
                                  pl.BlockSpec((None, 1, D), lambda i: (_stream_of(i, cb), 0, 0)),
                                  pl.BlockSpec((1, HEAD), lambda i: (0, 0))],
        out_shape=[sd(BF16)] * 3 + [jax.ShapeDtypeStruct((T, 9 * D), BF16), sd(F32), sd(F32),
                                    jax.ShapeDtypeStruct((2, 1, D), F32), jax.ShapeDtypeStruct((1, HEAD), F32)],
        compiler_params=_params("arbitrary", vmem=VMEM_LIMIT_TOKEN_OUT),
    )(dx, tmo, pa, pb, o, parts, parts, parts, mod, hnw, wa, wb, wo)


def _conv_geometry(i, nb, cb):
    is_ctx = i < cb
    first = (i == 0) | (i == cb)
    last = (i == cb - 1) | (i == nb - 1)
    row = lax.broadcasted_iota(jnp.int32, (TB + 2 * GRID_W, 1), 0)
    w = row & (GRID_W - 1)
    left_ok = (w != 0) | is_ctx
    right_ok = (w != GRID_W - 1) | is_ctx
    return is_ctx, first, last, left_ok, right_ok


def _ext(p_ref, m_ref, n_ref, first, last):
    return jnp.concatenate([jnp.where(first, 0.0, p_ref[...]), m_ref[...], jnp.where(last, 0.0, n_ref[...])], axis=0)


def _shift_prev(e, ok):
    return jnp.where(ok, pltpu.roll(e, 1, 0), 0.0)


def _shift_next(e, ok):
    return jnp.where(ok, pltpu.roll(e, e.shape[0] - 1, 0), 0.0)


def _halo_specs(cbk, n64, coff=0):
    r = TB // GRID_W
    prev = pl.BlockSpec((GRID_W, cbk), lambda j, i: (jnp.maximum(r * i - 1, 0), j + coff))
    main = pl.BlockSpec((TB, cbk), lambda j, i: (i, j + coff))
    nxt = pl.BlockSpec((GRID_W, cbk), lambda j, i: (jnp.minimum(r * i + r, n64 - 1), j + coff))
    return [prev, main, nxt]


def _conv_cblock(dff):
    return _tile(dff, 1408)


def _conv_fwd(up, cw, cbias, ctx_rows, name):
    T, dff = up.shape[0], up.shape[1] // 2
    cbk = _conv_cblock(dff)
    nb, cb = T // TB, ctx_rows // TB
    nvb = dff // cbk

    def body(ap_ref, a_ref, an_ref, v_ref, cw_ref, cb_ref, ac_ref, act_ref):
        i = pl.program_id(1)
        is_ctx, first, last, lok, rok = _conv_geometry(i, nb, cb)
        e = _ext(ap_ref, a_ref, an_ref, first, last)
        el = _shift_prev(e, lok)
        er = _shift_next(e, rok)
        cwv = cw_ref[...]

        def comb(dr, lo):
            sl = slice(lo, lo + TB)
            return cwv[3 * dr:3 * dr + 1] * el[sl] + cwv[3 * dr + 1:3 * dr + 2] * e[sl] + cwv[3 * dr + 2:3 * dr + 3] * er[sl]

        out = comb(1, GRID_W) + jnp.where(is_ctx, 0.0, comb(0, 0) + comb(2, 2 * GRID_W))
        a_c = out + cb_ref[...]
        ac_ref[...] = a_c
        act_ref[...] = (_gelu(a_c) * v_ref[...]).astype(BF16)

    main = pl.BlockSpec((TB, cbk), lambda j, i: (i, j))
    return pl.pallas_call(
        body, name=name, grid=(dff // cbk, nb),
        in_specs=_halo_specs(cbk, T // GRID_W) + [pl.BlockSpec((TB, cbk), lambda j, i: (i, j + nvb)),
                                                 pl.BlockSpec((9, cbk), lambda j, i: (0, j)),
                                                 pl.BlockSpec((1, cbk), lambda j, i: (0, j))],
        out_specs=[main, main],
        out_shape=[jax.ShapeDtypeStruct((T, dff), F32), jax.ShapeDtypeStruct((T, dff), BF16)],
        compiler_params=_params("parallel", "parallel"),
    )(up, up, up, up, cw, cbias)


def _conv_bwd(up, ac, dact, cw, ctx_rows, name):
    T, dff = up.shape[0], up.shape[1] // 2
    cbk = _conv_cblock(dff)
    nb, cb = T // TB, ctx_rows // TB
    nvb = dff // cbk

    def body(ap_ref, a_ref, an_ref, vp_ref, v_ref, vn_ref, cp_ref, c_ref, cn_ref, dp_ref, d_ref, dn_ref, cw_ref,
             da_ref, dv_ref, dcw_ref, dcb_ref):
        i = pl.program_id(1)

        @pl.when(i == 0)
        def _():
            dcw_ref[...] = jnp.zeros_like(dcw_ref)
            dcb_ref[...] = jnp.zeros_like(dcb_ref)

        is_ctx, first, last, lok, rok = _conv_geometry(i, nb, cb)
        gl, dgl = _gelu_both(_ext(cp_ref, c_ref, cn_ref, first, last))
        g = _ext(dp_ref, d_ref, dn_ref, first, last) * _ext(vp_ref, v_ref, vn_ref, first, last) * dgl
        dv_ref[...] = (d_ref[...] * gl[GRID_W:GRID_W + TB]).astype(BF16)
        gm = _shift_prev(g, lok)
        gp = _shift_next(g, rok)
        cwv = cw_ref[...]

        def comb(dr, lo):
            sl = slice(lo, lo + TB)
            return cwv[3 * dr:3 * dr + 1] * gp[sl] + cwv[3 * dr + 1:3 * dr + 2] * g[sl] + cwv[3 * dr + 2:3 * dr + 3] * gm[sl]

        da = comb(1, GRID_W) + jnp.where(is_ctx, 0.0, comb(0, 2 * GRID_W) + comb(2, 0))
        da_ref[...] = da.astype(BF16)
        e = _ext(ap_ref, a_ref, an_ref, first, last)
        taps = [_shift_prev(e, lok), e, _shift_next(e, rok)]
        gmain = g[GRID_W:GRID_W + TB]
        dcb_ref[...] += jnp.sum(gmain, axis=0, keepdims=True)
        vert = jnp.where(is_ctx, 0.0, 1.0)
        for dr in range(3):
            sl = slice(dr * GRID_W, dr * GRID_W + TB)
            for dw in range(3):
                s = jnp.sum(gmain * taps[dw][sl], axis=0, keepdims=True)
                if dr != 1:
                    s = s * vert
                k = 3 * dr + dw
                dcw_ref[k:k + 1, :] += s

    main = pl.BlockSpec((TB, cbk), lambda j, i: (i, j))
    halo = _halo_specs(cbk, T // GRID_W)
    acc9 = pl.BlockSpec((9, cbk), lambda j, i: (0, j))
    acc1 = pl.BlockSpec((1, cbk), lambda j, i: (0, j))
    return pl.pallas_call(
        body, name=name, grid=(dff // cbk, nb),
        in_specs=halo + _halo_specs(cbk, T // GRID_W, nvb) + halo + halo + [acc9],
        out_specs=[main, main, acc9, acc1],
        out_shape=[jax.ShapeDtypeStruct((T, dff), BF16), jax.ShapeDtypeStruct((T, dff), BF16),
                   jax.ShapeDtypeStruct((9, dff), F32), jax.ShapeDtypeStruct((1, dff), F32)],
        compiler_params=_params("parallel", "arbitrary"),
    )(up, up, up, up, up, up, ac, ac, ac, dact, dact, dact, cw)


def _ffn_out_fwd(act, xm, mod, wd, ctx_rows, name):
    T, D = xm.shape
    dff = act.shape[1]
    cb = ctx_rows // TB

    def body(act_ref, x_ref, mod_ref, w_ref, xo_ref, fo_ref):
        out = _nn(act_ref[...], _full(w_ref))
        fo_ref[...] = out
        xo_ref[...] = x_ref[...] + mod_ref[5:6, :] * out

    row = pl.BlockSpec((TB, D), lambda i: (i, 0))
    return pl.pallas_call(
        body, name=name, grid=(T // TB,),
        in_specs=[pl.BlockSpec((TB, dff), lambda i: (i, 0)), row,
                  pl.BlockSpec((None, N_MOD, D), lambda i: (_stream_of(i, cb), 0, 0)),
                  _rows_weight_spec(wd)],
        out_specs=[row, row],
        out_shape=[jax.ShapeDtypeStruct((T, D), F32), jax.ShapeDtypeStruct((T, D), F32)],
        compiler_params=_params("parallel"),
    )(act, xm, mod, wd)


def _ffn_out_bwd(dx, fo, mod, wd, ctx_rows, name):
    T, D = dx.shape
    dff = N_CHIPS * wd.shape[1]
    cb = ctx_rows // TB

    def body(dx_ref, fo_ref, mod_ref, w_ref, dout_ref, dact_ref, dg2_ref):
        i = pl.program_id(0)

        @pl.when((i == 0) | (i == cb))
        def _():
            dg2_ref[...] = jnp.zeros_like(dg2_ref)

        dxv = dx_ref[...]
        dg2_ref[...] += jnp.sum(dxv * fo_ref[...], axis=0, keepdims=True)
        dout = (dxv * mod_ref[5:6, :]).astype(BF16)
        dout_ref[...] = dout
        dact_ref[...] = _nt(dout, _full(w_ref))

    row = pl.BlockSpec((TB, D), lambda i: (i, 0))
    return pl.pallas_call(
        body, name=name, grid=(T // TB,),
        in_specs=[row, row, pl.BlockSpec((None, N_MOD, D), lambda i: (_stream_of(i, cb), 0, 0)),
                  _rows_weight_spec(wd)],
        out_specs=[row, pl.BlockSpec((TB, dff), lambda i: (i, 0)),
                   pl.BlockSpec((None, 1, D), lambda i: (_stream_of(i, cb), 0, 0))],
        out_shape=[jax.ShapeDtypeStruct((T, D), BF16), jax.ShapeDtypeStruct((T, dff), F32),
                   jax.ShapeDtypeStruct((2, 1, D), F32)],
        compiler_params=_params("arbitrary"),
    )(dx, fo, mod, wd)


def _loss_bwd(x, target, fw, ctx_rows, name):
    T, D = x.shape
    cb = ctx_rows // TB

    def body(x_ref, t_ref, fw_ref, dx_ref, loss_ref, dfw_ref):
        i = pl.program_id(0)

        @pl.when(i == 0)
        def _():
            loss_ref[...] = jnp.zeros_like(loss_ref)
            dfw_ref[...] = jnp.zeros_like(dfw_ref)

        @pl.when(i < cb)
        def _():
            dx_ref[...] = jnp.zeros_like(dx_ref)

        @pl.when(i >= cb)
        def _():
            xv = x_ref[...]
            r = lax.rsqrt(jnp.mean(xv * xv, axis=-1, keepdims=True) + RMS_EPS)
            xh = xv * r
            fwv = fw_ref[...]
            err = xh * fwv - t_ref[...]
            loss_ref[...] += (0.5 / D) * jnp.sum(err * err)
            dy = err * (1.0 / D)
            dfw_ref[...] += jnp.sum(dy * xh, axis=0, keepdims=True)
            dxh = dy * fwv
            dx_ref[...] = r * (dxh - xh * jnp.mean(dxh * xh, axis=-1, keepdims=True))

    row = pl.BlockSpec((TB, D), lambda i: (i, 0))
    return pl.pallas_call(
        body, name=name, grid=(T // TB,),
        in_specs=[row, pl.BlockSpec((TB, D), lambda i: (jnp.maximum(i - cb, 0), 0)), pl.BlockSpec((1, D), lambda i: (0, 0))],
        out_specs=[row, pl.BlockSpec((1, 128), lambda i: (0, 0)), pl.BlockSpec((1, D), lambda i: (0, 0))],
        out_shape=[jax.ShapeDtypeStruct((T, D), F32), jax.ShapeDtypeStruct((1, 128), F32),
                   jax.ShapeDtypeStruct((1, D), F32)],
        compiler_params=_params("arbitrary"),
    )(x, target, fw)


def _adamw(w, gs, m, v, name):
    L, R, C = w.shape
    assert len(gs) == L
    rb = _rows_tile(R, max(16, (1 << 18) // C // 16 * 16))
    bc1 = 1.0 - ADAM_B1 ** ADAM_STEP
    bc2 = 1.0 - ADAM_B2 ** ADAM_STEP

    def body(w_ref, m_ref, v_ref, *rest):
        g_refs, (g_ref, d_ref, nm_ref, nv_ref) = rest[:L], rest[L:]
        layer = pl.program_id(0)
        for li in range(L):
            @pl.when(layer == li)
            def _():
                gv = g_refs[li][...]
                g_ref[...] = gv
                nm = ADAM_B1 * m_ref[...] + (1.0 - ADAM_B1) * gv
                nv = ADAM_B2 * v_ref[...] + (1.0 - ADAM_B2) * (gv * gv)
                nm_ref[...] = nm
                nv_ref[...] = nv
                d_ref[...] = -ADAM_LR * ((nm / bc1) / (jnp.sqrt(nv / bc2) + ADAM_EPS) + ADAM_WD * w_ref[...])

    blk = pl.BlockSpec((None, rb, C), lambda l, i: (l, i, 0))
    gblk = pl.BlockSpec((rb, C), lambda l, i: (i, 0))
    sd = jax.ShapeDtypeStruct((L, R, C), F32)
    return pl.pallas_call(
        body, name=name, grid=(L, R // rb), in_specs=[blk] * 3 + [gblk] * L, out_specs=[blk] * 4, out_shape=[sd] * 4,
        compiler_params=_params("parallel", "parallel"),
    )(w, m, v, *gs)


def _local_step(xs, cv, target, W, layer_weights, on_layer_grads, ctx_rows):
    T, D = xs.shape
    depth = W["norm1_w"].shape[0]
    saved = []
    X = xs
    for l in range(depth):
        s = {}
        Wl = layer_weights(l, X)
        mod_all, sa = _mod_fwd(cv, Wl["ada_w"], W["ada_b"][l][None, :] + Wl["token"], f"mod_fwd_{l}")
        mod = mod_all[:2].reshape(2, N_MOD, D)
        h1 = _norm_mod(X, W["norm1_w"][l][None, :], mod, 0, ctx_rows, f"norm1_{l}")
        parts = _mm_nn_w(h1, Wl["w_in"], F32, f"in_proj_{l}")
        o_f, st_f = _hgrn_fwd(parts, W["hlb"], l, False, ctx_rows, f"hgrn_fwd_f_{l}")
        o, st_b = _hgrn_fwd(parts, W["hlb"], l, True, ctx_rows, f"hgrn_fwd_b_{l}", o_add=o_f)
        ya = _sgu_fwd(parts, W["sgu_ln_w"][l][None, :], W["sgu_ln_b"][l][None, :], W["sgu_w"][l], W["sgu_bt"][l],
                      f"sgu_fwd_{l}")
        Wl.update(Wl.pop("late")(ya))
        yb, pa, pb, mg, tmo, xm = _token_out_fwd(o, parts, ya, X, mod, W["hnw"][l][None, :] + Wl["late_token"], Wl["w_a"],
                                                 Wl["w_b"], Wl["w_o"], ctx_rows, f"token_out_fwd_{l}")
        h2 = _norm_mod(xm, W["norm2_w"][l][None, :], mod, 3, ctx_rows, f"norm2_{l}")
        up = _mm_nn_w(h2, Wl["w_up"], F32, f"up_proj_{l}")
        ac, act = _conv_fwd(up, Wl["conv_w"], W["conv_b"][l][None, :], ctx_rows, f"conv_fwd_{l}")
        xo, fo = _ffn_out_fwd(act, xm, mod, Wl["w_down"], ctx_rows, f"ffn_out_fwd_{l}")
        s.update(X=X, Wl=Wl, mod=mod, mod_all=mod_all, sa=sa, h1=h1, parts=parts, o=o, st_f=st_f, st_b=st_b, ya=ya, yb=yb,
                 pa=pa, pb=pb, mg=mg, tmo=tmo, xm=xm, h2=h2, up=up, ac=ac, act=act, fo=fo)
        saved.append(s)
        X = xo

    dX, loss_row, dfw = _loss_bwd(X, target, W["final_norm_w"][None, :], ctx_rows, "loss_bwd")
    G = {k: [None] * depth for k in ("ada_b", "norm1_w", "sgu_ln_w", "sgu_ln_b", "sgu_w", "sgu_b", "hlb1", "hnw", "norm2_w",
                                     "conv_w", "conv_b", "dmod")}
    dcv = jnp.zeros_like(cv)
    for l in reversed(range(depth)):
        s = saved[l]
        mod, Wl = s["mod"], s["Wl"]
        big = {}
        dout2, dact, dg2 = _ffn_out_bwd(dX, s["fo"], mod, Wl["w_down"], ctx_rows, f"ffn_out_bwd_{l}")
        big["w_down"] = _mm_tn(s["act"], dout2, F32, f"dw_down_{l}")
        da, dv, dcw, dcb = _conv_bwd(s["up"], s["ac"], dact, Wl["conv_w"], ctx_rows, f"conv_bwd_{l}")
        G["conv_w"][l], G["conv_b"][l] = dcw, dcb[0]
        big["w_up"] = _mm_tn(s["h2"], (da, dv), F32, f"dw_up_{l}", out_chips=True)
        dh2 = _mm_nt_w((da, dv), Wl["w_up"], F32, f"dh2_{l}")
        dxm, dm2, dnw2 = _norm_mod_bwd(dh2, s["xm"], dX, W["norm2_w"][l][None, :], mod, 3, ctx_rows, f"norm2_bwd_{l}")
        G["norm2_w"][l] = dnw2[0]
        (dout1, dpa, dpb, dparts, dya, do, dg1, dhnw) = _token_out_bwd(
            dxm, s["tmo"], s["pa"], s["pb"], s["o"], s["parts"], mod, W["hnw"][l][None, :], Wl["w_a"], Wl["w_b"], Wl["w_o"],
            ctx_rows, f"token_out_bwd_{l}")
        G["hnw"][l] = dhnw[0]
        big["w_o"] = _mm_tn(s["mg"], dout1, F32, f"dw_o_{l}")
        big["w_a"] = _mm_tn(s["ya"], dpa, F32, f"dw_a_{l}")
        big["w_b"] = _mm_tn(s["yb"], dpb, F32, f"dw_b_{l}")
        tok = on_layer_grads(l, "early", big)
        dparts, dsw, dsbt, dlnw, dlnb = _sgu_bwd(s["parts"], dya, W["sgu_ln_w"][l][None, :], W["sgu_ln_b"][l][None, :] + tok,
                                                 W["sgu_w"][l], W["sgu_bt"][l], dparts, f"sgu_bwd_{l}")
        G["sgu_w"][l], G["sgu_b"][l], G["sgu_ln_w"][l], G["sgu_ln_b"][l] = dsw, dsbt.T, dlnw[0], dlnb[0]
        dq_f, dz_f, di_f, dlb_f = _hgrn_bwd(s["parts"], W["hlb"], do, s["st_f"], l, False, ctx_rows, f"hgrn_bwd_f_{l}")
        dparts, dlb_b = _hgrn_bwd(s["parts"], W["hlb"], do, s["st_b"], l, True, ctx_rows, f"hgrn_bwd_b_{l}",
                                  other=(dq_f, dz_f, di_f), dparts=dparts)
        G["hlb1"][l] = jnp.concatenate([dlb_f[0], dlb_b[0]])
        tok = on_layer_grads(l, "late", {"w_in": _mm_tn(s["h1"], dparts, F32, f"dw_in_{l}", out_chips=True)})
        dh1 = _mm_nt_w(dparts, Wl["w_in"], F32, f"dh1_{l}")
        tok = tok + on_layer_grads(l, "end", {"after": dh1})
        dX, dm1, dnw1 = _norm_mod_bwd(dh1, s["X"], dxm, W["norm1_w"][l][None, :] + tok, mod, 0, ctx_rows, f"norm1_bwd_{l}")
        G["norm1_w"][l] = dnw1[0]
        dmod = jnp.concatenate([dm1, dg1, dm2, dg2], axis=1).reshape(2, N_MOD * D)
        dmod16 = jnp.concatenate([dmod, jnp.zeros((cv.shape[0] - 2, N_MOD * D), F32)], axis=0)
        G["ada_b"][l] = dmod[0] + dmod[1]
        G["dmod"][l] = dmod
        dcv = dcv + _cvec_bwd(dmod16, Wl["ada_w"], cv, f"dcvec_{l}")
    G["c_ctx"] = dcv[0]
    G["final_norm_w"] = dfw[0]
    return loss_row[0, 0], dX, G, saved[0]["sa"]


def _chip_peers(x, y, c):
    return [((1 - x, y, c), 2 * (1 - x) + y), ((x, 1 - y, c), 2 * x + 1 - y), ((1 - x, 1 - y, c), 2 * (1 - x) + 1 - y)]


def _rdma_call(ins, out_shapes, plan, n_remote, n_local, name, aliases=None):
    n_in, n_out = len(ins), len(out_shapes)

    def body(*refs):
        in_refs, out_refs = refs[:n_in], refs[n_in:n_in + n_out]
        send_sems, recv_sems, local_sems = refs[n_in + n_out:]
        x, y, c = lax.axis_index("x"), lax.axis_index("y"), lax.axis_index("c")
        remote, local = plan(in_refs, out_refs, x, y, c)
        assert len(remote) == n_remote and len(local) == n_local, (name, len(remote), len(local))
        copies = [pltpu.make_async_copy(s, d, local_sems.at[i]) for i, (s, d) in enumerate(local)]
        copies += [pltpu.make_async_remote_copy(src_ref=s, dst_ref=d, send_sem=send_sems.at[k], recv_sem=recv_sems.at[k],
                                                device_id=dev, device_id_type=pl.DeviceIdType.MESH)
                   for k, (s, d, dev) in enumerate(remote)]
        for cp in copies:
            cp.start()
        for cp in copies:
            cp.wait()

    hbm = pl.BlockSpec(memory_space=pltpu.HBM)
    return pl.pallas_call(
        body, name=name, in_specs=[hbm] * n_in, out_specs=[hbm] * n_out, out_shape=out_shapes,
        scratch_shapes=[pltpu.SemaphoreType.DMA((n_remote,)), pltpu.SemaphoreType.DMA((n_remote,)),
                        pltpu.SemaphoreType.DMA((max(n_local, 1),))],
        input_output_aliases=aliases or {},
    )(*ins)


DMA_PIECE_BYTES = 1 << 18
DMA_MAX_PIECES = 8


def _row_pieces(shape, dtype):
    rows = shape[0]
    row_bytes = jnp.dtype(dtype).itemsize
    for d in shape[1:]:
        row_bytes *= d
    n = 1
    while n < DMA_MAX_PIECES and rows % (2 * n * 16) == 0 and rows * row_bytes // (2 * n) >= DMA_PIECE_BYTES:
        n *= 2
    return [(i * (rows // n), rows // n) for i in range(n)]


def _half_pieces(o, c):
    r2 = o.shape[1] // 2
    return [pl.ds(c * r2 + st, sz) for st, sz in _row_pieces((r2,) + o.shape[2:], o.dtype)]


def _n_half_pieces(arrays):
    return sum(len(_row_pieces((a.shape[1] // 2,) + a.shape[2:], a.dtype)) for a in arrays)


def _plan_gather_far(lands, x, y, c):
    me = 2 * x + y
    return [(o.at[me, rows], o.at[me, rows], dev) for dev, _ in _chip_peers(x, y, c) for o in lands
            for rows in _half_pieces(o, c)]


def _plan_gather_near(lands, x, y, c):
    return [(o.at[idx, rows], o.at[idx, rows], (x, y, 1 - c)) for _, idx in _chip_peers(x, y, c) for o in lands
            for rows in _half_pieces(o, c)]


def _gather_weights(lands, name):
    n = len(lands)
    n_far = (N_CHIPS - 1) * _n_half_pieces(lands)

    def body(*refs):
        outs = refs[n:2 * n]
        far_send, far_recv, near_send, near_recv = refs[2 * n:]
        x, y, c = lax.axis_index("x"), lax.axis_index("y"), lax.axis_index("c")
        mk = lambda plan, send, recv: [
            pltpu.make_async_remote_copy(src_ref=s, dst_ref=d, send_sem=send.at[k], recv_sem=recv.at[k], device_id=dev,
                                         device_id_type=pl.DeviceIdType.MESH)
            for k, (s, d, dev) in enumerate(plan(outs, x, y, c))]
        far, near = mk(_plan_gather_far, far_send, far_recv), mk(_plan_gather_near, near_send, near_recv)
        assert len(far) == n_far and len(near) == n_far
        for cp in far:
            cp.start()
        for k in range(n_far):
            far[k].wait_recv()
            near[k].start()
        for k in range(n_far):
            near[k].wait_recv()
        for cp in far + near:
            cp.wait_send()

    hbm = pl.BlockSpec(memory_space=pltpu.HBM)
    sems = pltpu.SemaphoreType.DMA((n_far,))
    return pl.pallas_call(
        body, name=name, in_specs=[hbm] * n, out_specs=[hbm] * n,
        out_shape=[jax.ShapeDtypeStruct(a.shape, a.dtype) for a in lands],
        scratch_shapes=[sems, sems, sems, sems], input_output_aliases={i: i for i in range(n)},
    )(*lands)


def _gather_all(v, name):
    def plan(ins, outs, x, y, c):
        (s,), (o,) = ins, outs
        me = 4 * x + 2 * y + c
        flip = lambda a, f: 1 - a if f else a
        remote = [(s, o.at[me], (flip(x, m & 4), flip(y, m & 2), flip(c, m & 1))) for m in range(1, 8)]
        return remote, [(s, o.at[me])]

    return _rdma_call([v], [jax.ShapeDtypeStruct((8,) + v.shape, v.dtype)], plan, 7, 1, name)[0]


def _plan_pair(ins, lands, x, y, c):
    return [(a.at[j, 1 - c, pl.ds(st, sz)], o.at[j, pl.ds(st, sz)], (x, y, 1 - c)) for a, o in zip(ins, lands)
            for j in range(N_CHIPS) for st, sz in _row_pieces(a.shape[2:], a.dtype)]


def _n_pair_copies(parts):
    return N_CHIPS * sum(len(_row_pieces(a.shape[2:], a.dtype)) for a in parts)


def _reduce_pair(parts, name):
    shapes = [jax.ShapeDtypeStruct((N_CHIPS,) + a.shape[2:], a.dtype) for a in parts]
    return _rdma_call(parts, shapes, lambda ins, outs, x, y, c: (_plan_pair(ins, outs, x, y, c), []),
                      _n_pair_copies(parts), 0, name)


def _plan_chips(ins, lands, x, y, c):
    me = 2 * x + y
    return [(a.at[idx, pl.ds(st, sz)], o.at[me, pl.ds(st, sz)], dev) for dev, idx in _chip_peers(x, y, c)
            for a, o in zip(ins, lands) for st, sz in _row_pieces(a.shape[1:], a.dtype)]


def _n_chips_copies(parts):
    return (N_CHIPS - 1) * sum(len(_row_pieces(a.shape[1:], a.dtype)) for a in parts)


def _reduce_chips(parts, name):
    shapes = [jax.ShapeDtypeStruct(a.shape, a.dtype) for a in parts]
    return _rdma_call(parts, shapes, lambda ins, outs, x, y, c: (_plan_chips(ins, outs, x, y, c), []),
                      _n_chips_copies(parts), 0, name)


def _gather_pair(halves, name):
    def plan(ins, outs, x, y, c):
        return [(o.at[c, pl.ds(st, sz)], o.at[c, pl.ds(st, sz)], (x, y, 1 - c)) for o in outs
                for st, sz in _row_pieces(o.shape[1:], o.dtype)], []

    shapes = [jax.ShapeDtypeStruct(a.shape, a.dtype) for a in halves]
    n_remote = sum(len(_row_pieces(a.shape[1:], a.dtype)) for a in halves)
    return _rdma_call(halves, shapes, plan, n_remote, 0, name, aliases={i: i for i in range(len(halves))})


def _split_start(ins, lands, plan, n_remote, name):
    n_buf = len(ins) + len(lands)

    def body(*refs):
        in_refs, land_refs = refs[:len(ins)], refs[len(ins):n_buf]
        send_sems, recv_sems, token = refs[n_buf], refs[n_buf + 1], refs[-1]
        x, y, c = lax.axis_index("x"), lax.axis_index("y"), lax.axis_index("c")
        remote = plan(in_refs, land_refs, x, y, c)
        assert len(remote) == n_remote, (name, len(remote))
        for k, (s, d, dev) in enumerate(remote):
            pltpu.make_async_remote_copy(src_ref=s, dst_ref=d, send_sem=send_sems.at[k], recv_sem=recv_sems.at[k],
                                         device_id=dev, device_id_type=pl.DeviceIdType.MESH).start()
        token[...] = jnp.zeros_like(token)

    hbm = pl.BlockSpec(memory_space=pltpu.HBM)
    sem = pl.BlockSpec(memory_space=pltpu.SEMAPHORE)
    bufs = list(ins) + list(lands)
    out = pl.pallas_call(
        body, name=name, in_specs=[hbm] * n_buf,
        out_specs=(sem, sem) + (hbm,) * n_buf + (pl.BlockSpec(memory_space=pltpu.VMEM),),
        out_shape=(pltpu.SemaphoreType.DMA((n_remote,)), pltpu.SemaphoreType.DMA((n_remote,)))
        + tuple(pltpu.HBM(a.shape, a.dtype) for a in bufs) + (jax.ShapeDtypeStruct((8, 128), F32),),
        input_output_aliases={i: 2 + i for i in range(n_buf)},
        compiler_params=pltpu.CompilerParams(has_side_effects=pltpu.SideEffectType.DATAFLOW_SIDE_EFFECTING),
    )(*[pltpu.with_memory_space_constraint(a, pltpu.HBM) for a in bufs])
    return dict(send=out[0], recv=out[1], ins=list(out[2:2 + len(ins)]), lands=list(out[2 + len(ins):2 + n_buf]),
                token=out[-1][0, 0], token_array=out[-1], plan=plan, n_remote=n_remote)


def _split_wait(st, after, name):
    n_in, n_buf = len(st["ins"]), len(st["ins"]) + len(st["lands"])
    plan, n_remote = st["plan"], st["n_remote"]

    def body(*refs):
        in_refs, land_refs = refs[:n_in], refs[n_in:n_buf]
        send_sems, recv_sems = refs[n_buf], refs[n_buf + 1]
        x, y, c = lax.axis_index("x"), lax.axis_index("y"), lax.axis_index("c")
        for k, (s, d, dev) in enumerate(plan(in_refs, land_refs, x, y, c)):
            cp = pltpu.make_async_remote_copy(src_ref=s, dst_ref=d, send_sem=send_sems.at[k], recv_sem=recv_sems.at[k],
                                              device_id=dev, device_id_type=pl.DeviceIdType.MESH)
            cp.wait_send()
            cp.wait_recv()

    hbm = pl.BlockSpec(memory_space=pltpu.HBM)
    sem = pl.BlockSpec(memory_space=pltpu.SEMAPHORE)
    bufs = st["ins"] + st["lands"]
    out = pl.pallas_call(
        body, name=name, in_specs=[hbm] * n_buf + [sem, sem, pl.BlockSpec(memory_space=pl.ANY)],
        out_specs=[hbm] * n_buf, out_shape=[pltpu.HBM(a.shape, a.dtype) for a in bufs],
        input_output_aliases={i: i for i in range(n_buf)},
        compiler_params=pltpu.CompilerParams(has_side_effects=pltpu.SideEffectType.DATAFLOW_SIDE_EFFECTING),
    )(*bufs, st["send"], st["recv"], after)
    return list(out[:n_in]), list(out[n_in:])


def _pair_forward(lands, name):
    shapes = [jax.ShapeDtypeStruct(a.shape, a.dtype) for a in lands]
    return _rdma_call(lands, shapes, lambda ins, outs, x, y, c: (_plan_gather_near(outs, x, y, c), []),
                      (N_CHIPS - 1) * _n_half_pieces(lands), 0, name, aliases={i: i for i in range(len(lands))})


def _sum_block_rows(r, C):
    return _rows_tile(r, max(16, (1 << 18) // C // 16 * 16))


def _sum_pair(a, recv, cidx, name):
    nch, _, r, C = a.shape
    rb = _sum_block_rows(r, C)

    def body(c_ref, a_ref, r_ref, o_ref):
        o_ref[...] = (a_ref[...] + r_ref[...]).astype(BF16)

    blk = pl.BlockSpec((None, rb, C), lambda j, i, c: (j, i, 0))
    return pl.pallas_call(
        body, name=name,
        grid_spec=pltpu.PrefetchScalarGridSpec(
            num_scalar_prefetch=1, grid=(nch, r // rb),
            in_specs=[pl.BlockSpec((None, None, rb, C), lambda j, i, c: (j, c[0], i, 0)), blk], out_specs=blk),
        out_shape=jax.ShapeDtypeStruct((nch, r, C), BF16),
        compiler_params=_params("parallel", "parallel"),
    )(cidx, a, recv)


def _sum_chips(mine, recv, ids, name):
    nch, r, C = recv.shape
    rb = _sum_block_rows(r, C)

    def body(ids_ref, m_ref, *rest):
        r_refs, o_ref = rest[:nch], rest[nch]
        chip = ids_ref[1]
        own = m_ref[...].astype(F32)
        acc = jnp.where(chip == 0, own, r_refs[0][...].astype(F32))
        for q in range(1, nch):
            acc = acc + jnp.where(chip == q, own, r_refs[q][...].astype(F32))
        o_ref[...] = acc

    def slot(q):
        return pl.BlockSpec((None, rb, C), lambda i, ids: (jnp.where(ids[1] == q, (q + 1) % nch, q), i, 0))

    return pl.pallas_call(
        body, name=name,
        grid_spec=pltpu.PrefetchScalarGridSpec(
            num_scalar_prefetch=1, grid=(r // rb,),
            in_specs=[pl.BlockSpec((None, rb, C), lambda i, ids: (ids[1], i, 0))] + [slot(q) for q in range(nch)],
            out_specs=pl.BlockSpec((None, rb, C), lambda i, ids: (ids[0], i, 0))),
        out_shape=jax.ShapeDtypeStruct((N_CORES, r, C), F32),
        compiler_params=_params("parallel"),
    )(ids, mine, *([recv] * nch))


PACK_COLS = 1024
_SHARDED = ("ada_w", "w_in", "w_branch_a", "w_branch_b", "w_out", "ffn_w_up", "ffn_w_down")
_LAYER_KEYS = ("ada_w", "w_in", "w_a", "w_b", "w_o", "w_up", "w_down")
_SMALL = ("c_ctx", "ada_b", "norm1_w", "sgu_ln_w", "sgu_ln_b", "sgu_w", "sgu_b", "hgrn_lower_bounds", "hgrn_norm_w",
          "norm2_w", "ffn_conv_b", "final_norm_w")
_ORDER = ("c_ctx", "ada_w", "ada_b", "norm1_w", "w_in", "sgu_ln_w", "sgu_ln_b", "sgu_w", "sgu_b", "hgrn_lower_bounds",
          "hgrn_norm_w", "w_branch_a", "w_branch_b", "w_out", "norm2_w", "ffn_w_up", "ffn_conv_w", "ffn_conv_b",
          "ffn_w_down", "final_norm_w")


def _pad_to(v, n):
    return jnp.concatenate([v, jnp.zeros((n - v.shape[0],), v.dtype)]) if v.shape[0] < n else v


def _round_up(n, m):
    return (n + m - 1) // m * m


def _pack(arrays, n_pad):
    flat = jnp.concatenate([a.reshape(-1) for a in arrays])
    return _pad_to(flat, n_pad)


def _unpack(flat, like):
    out, off = [], 0
    for a in like:
        out.append(flat[off:off + a.size].reshape(a.shape))
        off += a.size
    return out


def kernel(x, c, ctx, c_ctx, ada_w, ada_b, norm1_w, w_in, sgu_ln_w, sgu_ln_b, sgu_w, sgu_b, hgrn_lower_bounds, hgrn_norm_w, w_branch_a, w_branch_b, w_out, norm2_w, ffn_w_up, ffn_conv_w, ffn_conv_b, ffn_w_down, final_norm_w, loss_target, m_c_ctx, m_ada_w, m_ada_b, m_norm1_w, m_w_in, m_sgu_ln_w, m_sgu_ln_b, m_sgu_w, m_sgu_b, m_hgrn_lower_bounds, m_hgrn_norm_w, m_w_branch_a, m_w_branch_b, m_w_out, m_norm2_w, m_ffn_w_up, m_ffn_conv_w, m_ffn_conv_b, m_ffn_w_down, m_final_norm_w, v_c_ctx, v_ada_w, v_ada_b, v_norm1_w, v_w_in, v_sgu_ln_w, v_sgu_ln_b, v_sgu_w, v_sgu_b, v_hgrn_lower_bounds, v_hgrn_norm_w, v_w_branch_a, v_w_branch_b, v_w_out, v_norm2_w, v_ffn_w_up, v_ffn_conv_w, v_ffn_conv_b, v_ffn_w_down, v_final_norm_w):
    w = dict(c_ctx=c_ctx, ada_w=ada_w, ada_b=ada_b, norm1_w=norm1_w, w_in=w_in, sgu_ln_w=sgu_ln_w, sgu_ln_b=sgu_ln_b,
             sgu_w=sgu_w, sgu_b=sgu_b, hgrn_lower_bounds=hgrn_lower_bounds, hgrn_norm_w=hgrn_norm_w, w_branch_a=w_branch_a,
             w_branch_b=w_branch_b, w_out=w_out, norm2_w=norm2_w, ffn_w_up=ffn_w_up, ffn_conv_w=ffn_conv_w,
             ffn_conv_b=ffn_conv_b, ffn_w_down=ffn_w_down, final_norm_w=final_norm_w)
    mom = dict(zip(_ORDER, (m_c_ctx, m_ada_w, m_ada_b, m_norm1_w, m_w_in, m_sgu_ln_w, m_sgu_ln_b, m_sgu_w, m_sgu_b,
                            m_hgrn_lower_bounds, m_hgrn_norm_w, m_w_branch_a, m_w_branch_b, m_w_out, m_norm2_w, m_ffn_w_up,
                            m_ffn_conv_w, m_ffn_conv_b, m_ffn_w_down, m_final_norm_w)))
    var = dict(zip(_ORDER, (v_c_ctx, v_ada_w, v_ada_b, v_norm1_w, v_w_in, v_sgu_ln_w, v_sgu_ln_b, v_sgu_w, v_sgu_b,
                            v_hgrn_lower_bounds, v_hgrn_norm_w, v_w_branch_a, v_w_branch_b, v_w_out, v_norm2_w, v_ffn_w_up,
                            v_ffn_conv_w, v_ffn_conv_b, v_ffn_w_down, v_final_norm_w)))
    depth, D = norm1_w.shape
    dff = ffn_conv_b.shape[1]
    ctx_rows, seq = ctx.shape[1], x.shape[1]

    assert depth == 2, "the lower-bound softmax is written for two layers"
    core = lax.axis_index("c")
    chip = 2 * lax.axis_index("x") + lax.axis_index("y")
    ids = jnp.stack([core, chip]).astype(jnp.int32)

    first, rest = _LAYER_KEYS[:2], _LAYER_KEYS[2:]
    shard = lambda l, k: w[_SHARDED[_LAYER_KEYS.index(k)]][l].astype(BF16)
    started, conv_full = {}, []

    def landing(s):
        return lax.dynamic_update_slice(lax.empty((N_CHIPS,) + s.shape, s.dtype), s[None], (chip,) + (0,) * s.ndim)

    def start_gather(l, keys, tag):
        lands = [landing(shard(l, k)) for k in keys]
        started[tag] = _split_start([], lands, lambda ins, lds, x, y, c: _plan_gather_far(lds, x, y, c),
                                    (N_CHIPS - 1) * _n_half_pieces(lands), f"gather_start_{tag}")
        return started[tag]["token"]

    def finish_gather(keys, tag, after):
        _, lands = _split_wait(started[tag], after, f"gather_wait_{tag}")
        return dict(zip(keys, _pair_forward(lands, f"gather_forward_{tag}")))

    def layer_weights(l, after):
        if l == 0:
            got = _gather_weights([landing(shard(0, k)) for k in first] + [landing(ffn_conv_w)], "gather_weights_first")
            conv_full.append(jnp.transpose(got[-1], (1, 2, 3, 0, 4)).reshape(depth, 9, dff))
            out = dict(zip(first, got), token=start_gather(0, rest, "rest_0"))
        else:
            out = dict(finish_gather(first, f"first_{l}", after), token=0.0)

        def late(after_late):
            more = finish_gather(rest, f"rest_{l}", after_late)
            more["late_token"] = 0.0
            if l + 1 < depth:
                more["late_token"] = start_gather(l + 1, first, f"first_{l + 1}") + start_gather(l + 1, rest, f"rest_{l + 1}")
            return more

        return dict(out, conv_w=conv_full[0][l], late=late)

    groups, order = {}, []

    def as_parts(gs):
        return [g.reshape(N_CHIPS, N_CORES, g.size // (N_CHIPS * N_CORES * g.shape[-1]), g.shape[-1]) for g in gs]

    def pair_start(tag, l, keys, gs):
        parts = as_parts(gs)
        lands = [lax.empty((N_CHIPS,) + p.shape[2:], p.dtype) for p in parts]
        groups[tag] = dict(l=l, keys=keys, pair=_split_start(parts, lands, _plan_pair, _n_pair_copies(parts),
                                                             f"reduce_pair_start_{tag}"))
        order.append(tag)
        return groups[tag]["pair"]["token"]

    def chips_start(tag, after):
        parts, other = _split_wait(groups[tag]["pair"], after, f"reduce_pair_wait_{tag}")
        sums = [_sum_pair(a, o, ids, f"sum_pair_{tag}_{i}") for i, (a, o) in enumerate(zip(parts, other))]
        lands = [lax.empty(s.shape, s.dtype) for s in sums]
        groups[tag]["chips"] = _split_start(sums, lands, _plan_chips, _n_chips_copies(sums), f"reduce_chips_start_{tag}")
        return groups[tag]["chips"]["token"]

    def chips_finish(tag, after):
        sums, recv = _split_wait(groups[tag]["chips"], after, f"reduce_chips_wait_{tag}")
        return {(groups[tag]["l"], k): _sum_chips(sums[i], recv[i], ids, f"sum_chips_{tag}_{i}")
                for i, k in enumerate(groups[tag]["keys"])}

    def on_layer_grads(l, stage, gs):
        if stage == "early":
            return pair_start(f"early_{l}", l, list(gs), list(gs.values()))
        if stage == "late":
            return pair_start(f"late_{l}", l, ["w_in"], [gs["w_in"]]) + chips_start(f"early_{l}", gs["w_in"])
        return chips_start(f"late_{l}", gs["after"])

    W = dict(ada_b=ada_b, norm1_w=norm1_w, sgu_ln_w=sgu_ln_w, sgu_ln_b=sgu_ln_b, sgu_w=sgu_w.astype(BF16),
             sgu_bt=jnp.swapaxes(sgu_b, 1, 2), hlb=hgrn_lower_bounds, hnw=hgrn_norm_w, norm2_w=norm2_w, conv_b=ffn_conv_b,
             final_norm_w=final_norm_w)
    xs = jnp.concatenate([ctx[0], x[0]], axis=0)
    cv = jnp.concatenate([c_ctx[None, :], c, jnp.zeros((14, D), F32)], axis=0)
    loss_local, dxs, G, sa = _local_step(xs, cv, loss_target[0], W, layer_weights, on_layer_grads, ctx_rows)
    loss = lax.psum(loss_local, ("x", "y", "c"))
    grad_x = dxs[ctx_rows:][None]

    pad8 = lambda a: jnp.pad(a, ((0, 8 - a.shape[0]), (0, 0)))
    fact = jnp.concatenate([pad8(sa[1:2].astype(F32))] + [pad8(G["dmod"][l][1].reshape(N_MOD, D)) for l in range(depth)]
                           + [pad8(G["dmod"][l][0].reshape(N_MOD, D)) for l in range(depth)], axis=0)
    facts = _gather_all(fact, "gather_mod_factors")
    lhs = jnp.concatenate([facts[:, 0].astype(BF16), jnp.broadcast_to(sa[0:1], (8, D))], axis=0)
    ada_cols = N_MOD * D // N_CHIPS
    g_ada = []
    for l in range(depth):
        lo_x, lo_c = 8 * (1 + l), 8 * (1 + depth + l)
        rhs = jnp.concatenate([facts[:, lo_x:lo_x + N_MOD].reshape(8, N_MOD * D),
                               facts[:, lo_c:lo_c + N_MOD].reshape(8, N_MOD * D)], axis=0)
        rhs = lax.dynamic_slice_in_dim(rhs, chip * ada_cols, ada_cols, axis=1).astype(BF16)
        g_ada.append(_mm_tn(lhs, rhs, F32, f"dw_ada_{l}"))

    dh = G["hlb1"][depth - 1]
    small_like = [w[k] for k in _SMALL] + [jnp.zeros((depth, 9, dff), F32)]
    small = [G["c_ctx"], jnp.stack(G["ada_b"]), jnp.stack(G["norm1_w"]), jnp.stack(G["sgu_ln_w"]), jnp.stack(G["sgu_ln_b"]),
             jnp.stack(G["sgu_w"]), jnp.stack(G["sgu_b"]), jnp.stack([-dh, dh]), jnp.stack(G["hnw"]), jnp.stack(G["norm2_w"]),
             jnp.stack(G["conv_b"]), G["final_norm_w"], jnp.stack(G["conv_w"])]
    n_small = sum(a.size for a in small)
    n_small_pad = _round_up(n_small, N_CORES * 16 * PACK_COLS)
    small_rows = n_small_pad // (N_CORES * PACK_COLS)
    small_rep = jnp.broadcast_to(_pack(small, n_small_pad).reshape(1, N_CORES, small_rows, PACK_COLS),
                                 (N_CHIPS, N_CORES, small_rows, PACK_COLS))
    small_parts = as_parts([small_rep])
    small_sums = [_sum_pair(small_parts[0], _reduce_pair(small_parts, "reduce_pair_small")[0], ids, "sum_pair_small")]
    groups["small"] = dict(l=None, keys=["small"], chips=_split_start(
        small_sums, [lax.empty(small_sums[0].shape, small_sums[0].dtype)], _plan_chips, _n_chips_copies(small_sums),
        "reduce_chips_start_small"))

    def gather_halves(halves, name):
        return dict(zip(halves, _gather_pair(list(halves.values()), name)))

    last = order[-1]
    halves = {}
    for tag in order[:-1]:
        halves.update(chips_finish(tag, groups["small"]["chips"]["token_array"]))
    reduced = gather_halves(halves, "gather_pair")
    grads, delta, new_m, new_v = {}, {}, {}, {}

    def adamw_sharded(i):
        k = _SHARDED[i]
        gs = g_ada if i == 0 else [reduced[(l, _LAYER_KEYS[i])].reshape(w[k].shape[1:]) for l in range(depth)]
        grads[k], delta[k], new_m[k], new_v[k] = _adamw(w[k], gs, mom[k], var[k], f"adamw_{k}")

    last_keys = groups[last]["keys"]
    for i in range(len(_SHARDED)):
        if _LAYER_KEYS[i] not in last_keys:
            adamw_sharded(i)
    halves = chips_finish(last, new_v[_SHARDED[-1]])
    halves.update(chips_finish("small", new_v[_SHARDED[-1]]))
    reduced.update(gather_halves(halves, "gather_pair_last"))
    for i in range(len(_SHARDED)):
        if _LAYER_KEYS[i] in last_keys:
            adamw_sharded(i)

    g_small = _unpack(reduced[(None, "small")].reshape(-1), small_like)
    grads.update(zip(_SMALL, g_small[:-1]))
    grads["ffn_conv_w"] = lax.dynamic_slice_in_dim(g_small[-1].reshape(depth, 3, 3, dff), chip * (dff // N_CHIPS),
                                                   dff // N_CHIPS, axis=3)
    packed = _SMALL + ("ffn_conv_w",)
    n_pad = _round_up(sum(w[k].size for k in packed), 16 * PACK_COLS)
    pack = lambda t: _pack([t[k] for k in packed], n_pad).reshape(1, -1, PACK_COLS)
    _, d, nm, nv = _adamw(pack(w), [pack(grads)[0]], pack(mom), pack(var), "adamw_packed")
    like = [w[k] for k in packed]
    for src, dst in ((d, delta), (nm, new_m), (nv, new_v)):
        dst.update(zip(packed, _unpack(src.reshape(-1), like)))

    return (loss, grad_x, *[grads[k] for k in _ORDER], *[delta[k] for k in _ORDER], *[new_m[k] for k in _ORDER],
            *[new_v[k] for k in _ORDER])
```

```python
import functools

import jax
import jax.numpy as jnp
from jax import lax
from jax.experimental import pallas as pl
from jax.experimental.pallas import tpu as pltpu

F32 = jnp.float32
BF16 = jnp.bfloat16

GRID_W = 64
HG_CHUNK = 64
SGU_CHUNK = 128
HEAD = 128
TB = 256
N_MOD = 6
RMS_EPS = 1e-6
LN_EPS = 1e-5
VMEM_LIMIT = 48 * 1024 * 1024
VMEM_LIMIT_PAIR = 58 * 1024 * 1024
N_CHIPS = 4
N_CORES = 2

ADAM_LR = 0.001
ADAM_B1 = 0.9
ADAM_B2 = 0.999
ADAM_EPS = 1e-08
ADAM_WD = 0.01
ADAM_STEP = 10

_GELU_C = 0.7978845608028654
_GELU_A = 0.044715


def _sigmoid(x):
    return 0.5 * jnp.tanh(0.5 * x) + 0.5


def _silu(x):
    return x * _sigmoid(x)


def _silu_both(x):
    s = _sigmoid(x)
    return x * s, s * (1.0 + x * (1.0 - s))


def _dsilu(x):
    return _silu_both(x)[1]


def _gelu_both(x):
    x2 = x * x
    t = jnp.tanh(_GELU_C * (x + _GELU_A * x2 * x))
    h = 0.5 * (1.0 + t)
    return x * h, h + 0.5 * x * (1.0 - t * t) * (_GELU_C + 3.0 * _GELU_C * _GELU_A * x2)


def _gelu(x):
    return 0.5 * x * (1.0 + jnp.tanh(_GELU_C * (x + _GELU_A * x * x * x)))


def _dgelu(x):
    return _gelu_both(x)[1]


def _dot(a, b, ca, cb):
    return lax.dot_general(a, b, (((ca,), (cb,)), ((), ())), preferred_element_type=F32)


def _nn(a, b):
    return _dot(a, b, 1, 0)


def _nt(a, b):
    return _dot(a, b, 1, 1)


def _tn(a, b):
    return _dot(a, b, 0, 0)


def _params(*sem, vmem=VMEM_LIMIT):
    return pltpu.CompilerParams(dimension_semantics=sem if sem else None, vmem_limit_bytes=vmem)


def _stream_of(i, ctx_blocks):
    return (i >= ctx_blocks).astype(jnp.int32)


def _mm(a, b, mode, tm, tn, tk, out_dtype, name, b_chips=False, out_chips=False, vmem=VMEM_LIMIT):
    a_pair, b_pair = isinstance(a, tuple), isinstance(b, tuple)
    assert (not a_pair or mode == "nt") and (not b_pair or (mode == "tn" and not b_chips))
    ashape = (a[0].shape[0], 2 * a[0].shape[1]) if a_pair else a.shape
    if b_pair:
        bshape = (b[0].shape[0], 2 * b[0].shape[1])
    elif not b_chips:
        bshape = b.shape
    else:
        bshape = (b.shape[1], N_CHIPS * b.shape[2])
    if mode == "nn":
        (M, K), (K2, N) = ashape, bshape
    elif mode == "nt":
        (M, K), (N, K2) = ashape, bshape
    else:
        (K, M), (K2, N) = ashape, bshape
    assert K == K2 and M % tm == 0 and N % tn == 0 and K % tk == 0, (name, ashape, bshape, tm, tn, tk)
    nk = K // tk
    if a_pair:
        n1 = a[0].shape[1] // tk
        assert a[0].shape[1] % tk == 0
        a_specs = [pl.BlockSpec((tm, tk), lambda j, i, k: (i, jnp.minimum(k, n1 - 1))),
                   pl.BlockSpec((tm, tk), lambda j, i, k: (i, jnp.maximum(k - n1, 0)))]
    elif mode == "tn":
        a_specs = [pl.BlockSpec((tk, tm), lambda j, i, k: (k, i))]
    else:
        a_specs = [pl.BlockSpec((tm, tk), lambda j, i, k: (i, k))]
    if b_pair:
        n1 = b[0].shape[1] // tn
        assert b[0].shape[1] % tn == 0
        b_specs = [pl.BlockSpec((tk, tn), lambda j, i, k: (k, jnp.minimum(j, n1 - 1))),
                   pl.BlockSpec((tk, tn), lambda j, i, k: (k, jnp.maximum(j - n1, 0)))]
    elif not b_chips:
        if mode == "nt":
            b_spec = pl.BlockSpec((tn, tk), lambda j, i, k: (j, k))
        else:
            b_spec = pl.BlockSpec((tk, tn), lambda j, i, k: (k, j))
    else:
        cols = b.shape[2]
        if mode == "nn":
            per = cols // tn
            assert cols % tn == 0
            b_spec = pl.BlockSpec((None, tk, tn), lambda j, i, k: (j // per, k, j % per))
        else:
            per = cols // tk
            assert mode == "nt" and cols % tk == 0
            b_spec = pl.BlockSpec((None, tn, tk), lambda j, i, k: (k // per, j, k % per))
    if not b_pair:
        b_specs = [b_spec]
    if out_chips:
        per_o = (N // N_CHIPS) // tn
        assert (N // N_CHIPS) % tn == 0
        o_spec = pl.BlockSpec((None, tm, tn), lambda j, i, k: (j // per_o, i, j % per_o))
        o_shape = (N_CHIPS, M, N // N_CHIPS)
    else:
        o_spec = pl.BlockSpec((tm, tn), lambda j, i, k: (i, j))
        o_shape = (M, N)
    ca, cb = {"nn": (1, 0), "nt": (1, 1), "tn": (0, 0)}[mode]

    in_place = nk == 1
    na, nb = len(a_specs), len(b_specs)

    def body(*refs):
        a_refs, b_refs, rest = refs[:na], refs[na:na + nb], refs[na + nb:]
        if in_place:
            (o_ref,) = rest
        else:
            o_ref, acc = rest
        k = pl.program_id(2)

        if in_place:
            o_ref[...] = _dot(a_refs[0][...], b_refs[0][...], ca, cb).astype(out_dtype)
            return

        @pl.when(k == 0)
        def _():
            acc[...] = jnp.zeros_like(acc)

        if a_pair or b_pair:
            first = (k < n1) if a_pair else (pl.program_id(0) < n1)
            for which, cond in ((0, first), (1, jnp.logical_not(first))):
                @pl.when(cond)
                def _():
                    acc[...] += _dot(a_refs[which if a_pair else 0][...], b_refs[which if b_pair else 0][...], ca, cb)
        else:
            acc[...] += _dot(a_refs[0][...], b_refs[0][...], ca, cb)

        @pl.when(k == nk - 1)
        def _():
            o_ref[...] = acc[...].astype(out_dtype)

    ins = (list(a) if a_pair else [a]) + (list(b) if b_pair else [b])
    return pl.pallas_call(
        body, name=name, grid=(N // tn, M // tm, nk), in_specs=a_specs + b_specs, out_specs=o_spec,
        out_shape=jax.ShapeDtypeStruct(o_shape, out_dtype),
        scratch_shapes=[] if in_place else [pltpu.VMEM((tm, tn), F32)],
        compiler_params=_params("parallel", "parallel", "arbitrary", vmem=vmem),
    )(*ins)


def _tile(n, pref):
    if n <= pref:
        return n
    best = None
    for t in range(128, pref + 1, 128):
        if n % t == 0:
            best = t
    assert best is not None, (n, pref)
    return best


def _rows_tile(n, pref):
    if n <= pref:
        return n
    best = None
    for t in range(16, pref + 1, 16):
        if n % t == 0:
            best = t
    assert best is not None, (n, pref)
    return best


def _mm_nn_w(a, wg, out_dtype, name):
    M, K = a.shape
    return _mm(a, wg, "nn", _rows_tile(M, 2176), _tile(wg.shape[2], 1536), _tile(K, 1536), out_dtype, name, b_chips=True)


def _mm_nt_w(a, wg, out_dtype, name):
    M = a[0].shape[0] if isinstance(a, tuple) else a.shape[0]
    return _mm(a, wg, "nt", _rows_tile(M, 1088), _tile(wg.shape[1], 1024), _tile(wg.shape[2], 1536), out_dtype, name,
               b_chips=True)


def _mm_tn(a, b, out_dtype, name, out_chips=False):
    K, M = a.shape
    N = 2 * b[0].shape[1] if isinstance(b, tuple) else b.shape[1]
    ncol = N // N_CHIPS if out_chips else N
    tm, tn = _tile(M, 1408), _tile(ncol, 1408)
    if tm * tn > 1408 * 1152:
        tn = _tile(ncol, 1152)
    vmem = VMEM_LIMIT_PAIR if isinstance(b, tuple) else VMEM_LIMIT
    return _mm(a, b, "tn", tm, tn, _rows_tile(K, 2176), out_dtype, name, out_chips=out_chips, vmem=vmem)


def _mod_fwd(cv, wg, b, name):
    R, D = cv.shape
    tn = wg.shape[2]
    N = N_CHIPS * tn

    def body(cv_ref, w_ref, b_ref, mod_ref, sa_ref):
        sa = _silu(cv_ref[...]).astype(BF16)
        sa_ref[...] = sa
        mod_ref[...] = _nn(sa, w_ref[...]) + b_ref[...]

    return pl.pallas_call(
        body, name=name, grid=(N_CHIPS,),
        in_specs=[pl.BlockSpec((R, D), lambda j: (0, 0)), pl.BlockSpec((None, D, tn), lambda j: (j, 0, 0)),
                  pl.BlockSpec((1, tn), lambda j: (0, j))],
        out_specs=[pl.BlockSpec((R, tn), lambda j: (0, j)), pl.BlockSpec((R, D), lambda j: (0, 0))],
        out_shape=[jax.ShapeDtypeStruct((R, N), F32), jax.ShapeDtypeStruct((R, D), BF16)],
        compiler_params=_params("arbitrary"),
    )(cv, wg, b)


def _cvec_bwd(dmod, wg, cv, name):
    R, N = dmod.shape
    D = wg.shape[1]
    tk = wg.shape[2]
    nk = N_CHIPS

    def body(dm_ref, w_ref, cv_ref, o_ref):
        k = pl.program_id(0)

        @pl.when(k == 0)
        def _():
            o_ref[...] = jnp.zeros_like(o_ref)

        o_ref[...] += _nt(dm_ref[...].astype(BF16), w_ref[...])

        @pl.when(k == nk - 1)
        def _():
            o_ref[...] = o_ref[...] * _dsilu(cv_ref[...])

    return pl.pallas_call(
        body, name=name, grid=(nk,),
        in_specs=[pl.BlockSpec((R, tk), lambda k: (0, k)), pl.BlockSpec((None, D, tk), lambda k: (k, 0, 0)),
                  pl.BlockSpec((R, D), lambda k: (0, 0))],
        out_specs=pl.BlockSpec((R, D), lambda k: (0, 0)),
        out_shape=jax.ShapeDtypeStruct((R, D), F32),
        compiler_params=_params("arbitrary"),
    )(dmod, wg, cv)


def _norm_mod(x, nw, mod, which, ctx_rows, name):
    T, D = x.shape
    cb = ctx_rows // TB

    def body(x_ref, nw_ref, mod_ref, h_ref):
        xv = x_ref[...]
        r = lax.rsqrt(jnp.mean(xv * xv, axis=-1, keepdims=True) + RMS_EPS)
        y = xv * r * nw_ref[...]
        sh = mod_ref[which:which + 1, :]
        sc = mod_ref[which + 1:which + 2, :]
        h_ref[...] = (y * (1.0 + sc) + sh).astype(BF16)

    return pl.pallas_call(
        body, name=name, grid=(T // TB,),
        in_specs=[pl.BlockSpec((TB, D), lambda i: (i, 0)), pl.BlockSpec((1, D), lambda i: (0, 0)),
                  pl.BlockSpec((None, N_MOD, D), lambda i: (_stream_of(i, cb), 0, 0))],
        out_specs=pl.BlockSpec((TB, D), lambda i: (i, 0)),
        out_shape=jax.ShapeDtypeStruct((T, D), BF16),
        compiler_params=_params("parallel"),
    )(x, nw, mod)


def _norm_mod_bwd(dh, x, dres, nw, mod, which, ctx_rows, name):
    T, D = x.shape
    cb = ctx_rows // TB

    def body(dh_ref, x_ref, dres_ref, nw_ref, mod_ref, dx_ref, dm_ref, dnw_ref):
        i = pl.program_id(0)

        @pl.when(i == 0)
        def _():
            dnw_ref[...] = jnp.zeros_like(dnw_ref)

        @pl.when((i == 0) | (i == cb))
        def _():
            dm_ref[...] = jnp.zeros_like(dm_ref)

        xv = x_ref[...]
        dh = dh_ref[...]
        r = lax.rsqrt(jnp.mean(xv * xv, axis=-1, keepdims=True) + RMS_EPS)
        xh = xv * r
        nwv = nw_ref[...]
        sc = mod_ref[which + 1:which + 2, :]
        y = xh * nwv
        dm_ref[0:1, :] += jnp.sum(dh, axis=0, keepdims=True)
        dm_ref[1:2, :] += jnp.sum(dh * y, axis=0, keepdims=True)
        dy = dh * (1.0 + sc)
        dnw_ref[...] += jnp.sum(dy * xh, axis=0, keepdims=True)
        dxh = dy * nwv
        dx_ref[...] = dres_ref[...] + r * (dxh - xh * jnp.mean(dxh * xh, axis=-1, keepdims=True))

    return pl.pallas_call(
        body, name=name, grid=(T // TB,),
        in_specs=[pl.BlockSpec((TB, D), lambda i: (i, 0)), pl.BlockSpec((TB, D), lambda i: (i, 0)),
                  pl.BlockSpec((TB, D), lambda i: (i, 0)), pl.BlockSpec((1, D), lambda i: (0, 0)),
                  pl.BlockSpec((None, N_MOD, D), lambda i: (_stream_of(i, cb), 0, 0))],
        out_specs=[pl.BlockSpec((TB, D), lambda i: (i, 0)),
                   pl.BlockSpec((None, 2, D), lambda i: (_stream_of(i, cb), 0, 0)),
                   pl.BlockSpec((1, D), lambda i: (0, 0))],
        out_shape=[jax.ShapeDtypeStruct((T, D), F32), jax.ShapeDtypeStruct((2, 2, D), F32),
                   jax.ShapeDtypeStruct((1, D), F32)],
        compiler_params=_params("arbitrary"),
    )(dh, x, dres, nw, mod)


def _scan_chunk(n, rev, n_ctx, n_all):
    if not rev:
        return n
    return jnp.where(n < n_ctx, n_ctx - 1 - n, n_all - 1 + n_ctx - n)


def _cumsum_rows(x, rev):
    rows = x.shape[0]
    row = lax.broadcasted_iota(jnp.int32, (rows, 1), 0)
    s = 1
    while s < rows:
        if not rev:
            x = x + jnp.where(row >= s, pltpu.roll(x, s, 0), 0.0)
        else:
            x = x + jnp.where(row < rows - s, pltpu.roll(x, rows - s, 0), 0.0)
        s *= 2
    return x


def _lower_bound(hlb_ref, layer):
    h = hlb_ref[...]
    if layer == 0:
        return jnp.zeros_like(h[0:1, :])
    return _sigmoid(h[1:2, :] - h[0:1, :])


HG_STEP = 4


def _step_rows(j, rev, backward):
    sub = j if rev == backward else HG_STEP - 1 - j
    return slice(sub * HG_CHUNK, (sub + 1) * HG_CHUNK)


def _hgrn_gates(q_ref, f_ref, hlb_ref, layer, rev, rows):
    lb = _lower_bound(hlb_ref, layer)
    z = f_ref[rows, :]
    sig = 1.0 / (1.0 + jnp.exp(-z))
    fg = lb + (1.0 - lb) * sig
    kk = (1.0 - lb) * (1.0 - sig)
    g = jnp.log(fg)
    b = _cumsum_rows(g, rev)
    bt = jnp.sum(g, axis=0, keepdims=True)
    mid = HG_CHUNK // 2
    r = b[mid:mid + 1, :] if rev else b[mid - 1:mid, :]
    qh = _silu(q_ref[rows, :])
    return lb, sig, fg, kk, b, bt, r, qh


def _tri_mask(rev):
    t = lax.broadcasted_iota(jnp.int32, (HG_CHUNK, HG_CHUNK), 0)
    s = lax.broadcasted_iota(jnp.int32, (HG_CHUNK, HG_CHUNK), 1)
    return (s >= t) if rev else (s <= t)


def _hgrn_fwd(parts, hlb, layer, rev, ctx_rows, name, o_add=None):
    T = parts.shape[0]
    D = hlb.shape[1] // 2
    nh = D // HEAD
    n_all, n_ctx = T // HG_CHUNK, ctx_rows // HG_CHUNK
    assert n_all % HG_STEP == 0 and n_ctx % HG_STEP == 0
    n_steps = n_all // HG_STEP
    block = functools.partial(_scan_chunk, rev=rev, n_ctx=n_ctx // HG_STEP, n_all=n_steps)
    fcol = 2 if rev else 1

    def body(q_ref, f_ref, i_ref, hlb_ref, *rest):
        if o_add is None:
            o_ref, st_ref, s_scr = rest
        else:
            oa_ref, o_ref, st_ref, s_scr = rest
        n = pl.program_id(0)

        @pl.when(n == 0)
        def _():
            s_scr[...] = jnp.zeros_like(s_scr)

        mask = _tri_mask(rev)
        hs = [slice(h * HEAD, (h + 1) * HEAD) for h in range(nh)]
        for j in range(HG_STEP):
            rows = _step_rows(j, rev, False)
            lb, sig, fg, kk, b, bt, r, qh = _hgrn_gates(q_ref, f_ref, hlb_ref, layer, rev, rows)
            qr = (qh * jnp.exp(b - r)).astype(BF16)
            kr = (kk * jnp.exp(r - b)).astype(BF16)
            qe = (qh * jnp.exp(b)).astype(BF16)
            ke = (kk * jnp.exp(bt - b)).astype(BF16)
            dec = jnp.exp(bt)
            v = i_ref[rows, :].astype(BF16)
            st = [s_scr[h] for h in range(nh)]
            a_raw = [_nt(qr[:, sl], kr[:, sl]) for sl in hs]
            o_int = [_nt(qe[:, sl], st[h].astype(BF16)) for h, sl in enumerate(hs)]
            kv = [_tn(v[:, sl], ke[:, sl]) for sl in hs]
            for h, sl in enumerate(hs):
                st_ref[j, h] = st[h]
                o = _nn(jnp.where(mask, a_raw[h], 0.0).astype(BF16), v[:, sl]) + o_int[h]
                if o_add is not None:
                    o = o + oa_ref[rows, sl]
                o_ref[rows, sl] = o
                s_scr[h] = st[h] * dec[:, sl] + kv[h]

    cspec = lambda col: pl.BlockSpec((HG_STEP * HG_CHUNK, D), lambda n: (block(n), col))
    ins = [parts, parts, parts, hlb]
    specs = [cspec(0), cspec(fcol), cspec(3), pl.BlockSpec((2, D), lambda n: (0, 1 if rev else 0))]
    if o_add is not None:
        ins.append(o_add)
        specs.append(cspec(0))
    return pl.pallas_call(
        body, name=name, grid=(n_steps,), in_specs=specs,
        out_specs=[cspec(0), pl.BlockSpec((HG_STEP, nh, HEAD, HEAD), lambda n: (n, 0, 0, 0))],
        out_shape=[jax.ShapeDtypeStruct((T, D), F32), jax.ShapeDtypeStruct((n_all, nh, HEAD, HEAD), F32)],
        scratch_shapes=[pltpu.VMEM((nh, HEAD, HEAD), F32)],
        compiler_params=_params("arbitrary"),
    )(*ins)


def _hgrn_bwd(parts, hlb, do, states, layer, rev, ctx_rows, name, other=None, dparts=None):
    T = parts.shape[0]
    D = hlb.shape[1] // 2
    nh = D // HEAD
    n_all, n_ctx = T // HG_CHUNK, ctx_rows // HG_CHUNK
    assert n_all % HG_STEP == 0 and n_ctx % HG_STEP == 0
    n_steps = n_all // HG_STEP
    step = lambda m: n_steps - 1 - m
    block = lambda m: _scan_chunk(step(m), rev, n_ctx // HG_STEP, n_steps)
    fcol = 2 if rev else 1
    has_add = other is not None
    assert not has_add or rev

    def body(q_ref, f_ref, i_ref, hlb_ref, do_ref, st_ref, *rest):
        if has_add:
            dqa_ref, dza_ref, dia_ref, _, out_ref, dlb_ref, ds_scr = rest
            dq_ref, dz_ref, di_ref = out_ref.at[:, 0:D], out_ref.at[:, 2 * D:3 * D], out_ref.at[:, 3 * D:4 * D]
            out_ref[:, D:2 * D] = dza_ref[...]
        else:
            dq_ref, dz_ref, di_ref, dlb_ref, ds_scr = rest
        m = pl.program_id(0)

        @pl.when(m == 0)
        def _():
            ds_scr[...] = jnp.zeros_like(ds_scr)
            dlb_ref[...] = jnp.zeros_like(dlb_ref)

        mask = _tri_mask(rev)
        hs = [slice(h * HEAD, (h + 1) * HEAD) for h in range(nh)]
        for j in range(HG_STEP):
            rows = _step_rows(j, rev, True)
            slot = HG_STEP - 1 - j
            lb, sig, fg, kk, b, bt, r, qh = _hgrn_gates(q_ref, f_ref, hlb_ref, layer, rev, rows)
            e_qr = jnp.exp(b - r)
            e_kr = jnp.exp(r - b)
            e_b = jnp.exp(b)
            e_ke = jnp.exp(bt - b)
            dec = jnp.exp(bt)
            qr = (qh * e_qr).astype(BF16)
            kr = (kk * e_kr).astype(BF16)
            qe = (qh * e_b).astype(BF16)
            ke = (kk * e_ke).astype(BF16)
            v = i_ref[rows, :].astype(BF16)
            dov = do_ref[rows, :].astype(BF16)
            st = [st_ref[slot, h] for h in range(nh)]
            dst = [ds_scr[h] for h in range(nh)]
            stb = [t.astype(BF16) for t in st]
            dstb = [t.astype(BF16) for t in dst]
            a_raw = [_nt(qr[:, sl], kr[:, sl]) for sl in hs]
            da_raw = [_nt(dov[:, sl], v[:, sl]) for sl in hs]
            dq_int = [_nn(dov[:, sl], stb[h]) for h, sl in enumerate(hs)]
            dk_int = [_nn(v[:, sl], dstb[h]) for h, sl in enumerate(hs)]
            dv_int = [_nt(ke[:, sl], dstb[h]) for h, sl in enumerate(hs)]
            ds_new = [_tn(dov[:, sl], qe[:, sl]) for sl in hs]
            a = [jnp.where(mask, t, 0.0).astype(BF16) for t in a_raw]
            da = [jnp.where(mask, t, 0.0).astype(BF16) for t in da_raw]
            dv_parts = [_tn(a[h], dov[:, sl]) + dv_int[h] for h, sl in enumerate(hs)]
            dq_parts = [_nn(da[h], kr[:, sl]) * e_qr[:, sl] + dq_int[h] * e_b[:, sl] for h, sl in enumerate(hs)]
            dki_parts = [dk_int[h] * e_ke[:, sl] for h, sl in enumerate(hs)]
            dk_parts = [_tn(da[h], qr[:, sl]) * e_kr[:, sl] + dki_parts[h] for h, sl in enumerate(hs)]
            dbt_parts = [dec[:, sl] * jnp.sum(st[h] * dst[h], axis=0, keepdims=True) for h, sl in enumerate(hs)]
            for h, sl in enumerate(hs):
                ds_scr[h] = dst[h] * dec[:, sl] + ds_new[h]
            dq = jnp.concatenate(dq_parts, axis=1)
            dk = jnp.concatenate(dk_parts, axis=1)
            dki = jnp.concatenate(dki_parts, axis=1)
            dv = jnp.concatenate(dv_parts, axis=1)
            dbt = jnp.concatenate(dbt_parts, axis=1) + jnp.sum(kk * dki, axis=0, keepdims=True)
            db = qh * dq - kk * dk
            dg = _cumsum_rows(db, not rev) + dbt
            df = dg / fg - dk
            dz_ref[rows, :] = (df * (1.0 - lb) * sig * (1.0 - sig)).astype(BF16)
            dlb_ref[...] += jnp.sum(df * (1.0 - sig), axis=0, keepdims=True)
            dqr = dq * _dsilu(q_ref[rows, :])
            if has_add:
                dqr = dqr + dqa_ref[rows, :]
                dv = dv + dia_ref[rows, :]
            dq_ref[rows, :] = dqr.astype(dq_ref.dtype)
            di_ref[rows, :] = dv.astype(di_ref.dtype)

        @pl.when(m == n_steps - 1)
        def _():
            lb = _lower_bound(hlb_ref, layer)
            if layer == 0:
                dlb_ref[...] = jnp.zeros_like(dlb_ref)
            else:
                dlb_ref[...] = dlb_ref[...] * lb * (1.0 - lb)

    cspec = lambda col: pl.BlockSpec((HG_STEP * HG_CHUNK, D), lambda m: (block(m), col))
    ins = [parts, parts, parts, hlb, do, states]
    specs = [cspec(0), cspec(fcol), cspec(3), pl.BlockSpec((2, D), lambda m: (0, 1 if rev else 0)), cspec(0),
             pl.BlockSpec((HG_STEP, nh, HEAD, HEAD), lambda m: (step(m), 0, 0, 0))]
    dlb_spec = pl.BlockSpec((1, D), lambda m: (0, 0))
    dlb_shape = jax.ShapeDtypeStruct((1, D), F32)
    if has_add:
        return pl.pallas_call(
            body, name=name, grid=(n_steps,),
            in_specs=specs + [cspec(0), cspec(0), cspec(0), pl.BlockSpec(memory_space=pl.ANY)],
            out_specs=[pl.BlockSpec((HG_STEP * HG_CHUNK, 4 * D), lambda m: (block(m), 0)), dlb_spec],
            out_shape=[jax.ShapeDtypeStruct(dparts.shape, dparts.dtype), dlb_shape],
            scratch_shapes=[pltpu.VMEM((nh, HEAD, HEAD), F32)], input_output_aliases={len(ins) + 3: 0},
            compiler_params=_params("arbitrary"),
        )(*ins, *other, dparts)
    return pl.pallas_call(
        body, name=name, grid=(n_steps,), in_specs=specs,
        out_specs=[cspec(0), cspec(0), cspec(0), dlb_spec],
        out_shape=[jax.ShapeDtypeStruct((T, D), F32), jax.ShapeDtypeStruct((T, D), BF16),
                   jax.ShapeDtypeStruct((T, D), F32), dlb_shape],
        scratch_shapes=[pltpu.VMEM((nh, HEAD, HEAD), F32)],
        compiler_params=_params("arbitrary"),
    )(*ins)


def _sgu_ln(gv, lnw_ref, lnb_ref):
    mu = jnp.mean(gv, axis=-1, keepdims=True)
    xc = gv - mu
    rstd = lax.rsqrt(jnp.mean(xc * xc, axis=-1, keepdims=True) + LN_EPS)
    xh = xc * rstd
    return xh, rstd, xh * lnw_ref[...] + lnb_ref[...]


def _sgu_fwd(parts, lnw, lnb, w, bt, name):
    T = parts.shape[0]
    D = lnw.shape[1]
    G = D // HEAD

    def body(u_ref, v_ref, lnw_ref, lnb_ref, w_ref, bt_ref, ya_ref):
        gu = _gelu(u_ref[...])
        _, _, vn = _sgu_ln(_gelu(v_ref[...]), lnw_ref, lnb_ref)
        vnb = vn.astype(BF16)
        for g in range(G):
            sl = slice(g * HEAD, (g + 1) * HEAD)
            mixed = _nn(w_ref[g], vnb[:, sl]) + bt_ref[:, g:g + 1]
            ya_ref[:, sl] = (gu[:, sl] * mixed).astype(BF16)

    return pl.pallas_call(
        body, name=name, grid=(T // SGU_CHUNK,),
        in_specs=[pl.BlockSpec((SGU_CHUNK, D), lambda n: (n, 4)), pl.BlockSpec((SGU_CHUNK, D), lambda n: (n, 5)),
                  pl.BlockSpec((1, D), lambda n: (0, 0)), pl.BlockSpec((1, D), lambda n: (0, 0)),
                  pl.BlockSpec((G, SGU_CHUNK, SGU_CHUNK), lambda n: (0, 0, 0)),
                  pl.BlockSpec((SGU_CHUNK, G), lambda n: (0, 0))],
        out_specs=pl.BlockSpec((SGU_CHUNK, D), lambda n: (n, 0)),
        out_shape=jax.ShapeDtypeStruct((T, D), BF16),
        compiler_params=_params("parallel"),
    )(parts, parts, lnw, lnb, w, bt)


def _sgu_bwd(parts, dya, lnw, lnb, w, bt, dparts, name):
    T = parts.shape[0]
    D = lnw.shape[1]
    G = D // HEAD

    def body(u_ref, v_ref, dya_ref, lnw_ref, lnb_ref, w_ref, bt_ref, dparts_in,
             duv_ref, dw_ref, dbt_ref, dlnw_ref, dlnb_ref, dvn_scr):
        du_ref = duv_ref.at[:, 0:D]
        dv_ref = duv_ref.at[:, D:2 * D]
        n = pl.program_id(0)

        @pl.when(n == 0)
        def _():
            dw_ref[...] = jnp.zeros_like(dw_ref)
            dbt_ref[...] = jnp.zeros_like(dbt_ref)
            dlnw_ref[...] = jnp.zeros_like(dlnw_ref)
            dlnb_ref[...] = jnp.zeros_like(dlnb_ref)

        gu, dgu = _gelu_both(u_ref[...])
        gv, dgv_dv = _gelu_both(v_ref[...])
        xh, rstd, vn = _sgu_ln(gv, lnw_ref, lnb_ref)
        vnb = vn.astype(BF16)
        dya = dya_ref[...]
        lane = lax.broadcasted_iota(jnp.int32, (SGU_CHUNK, G), 1)
        dbt = jnp.zeros((SGU_CHUNK, G), F32)
        for g in range(G):
            sl = slice(g * HEAD, (g + 1) * HEAD)
            wg = w_ref[g]
            mixed = _nn(wg, vnb[:, sl]) + bt_ref[:, g:g + 1]
            dmix = dya[:, sl] * gu[:, sl]
            du_ref[:, sl] = (dya[:, sl] * mixed * dgu[:, sl]).astype(BF16)
            dmb = dmix.astype(BF16)
            dvn_scr[:, sl] = _tn(wg, dmb)
            dw_ref[g] += _nt(dmb, vnb[:, sl])
            dbt = dbt + jnp.where(lane == g, jnp.sum(dmix, axis=1, keepdims=True), 0.0)
        dbt_ref[...] += dbt
        dvn = dvn_scr[...]
        dlnw_ref[...] += jnp.sum(dvn * xh, axis=0, keepdims=True)
        dlnb_ref[...] += jnp.sum(dvn, axis=0, keepdims=True)
        dxh = dvn * lnw_ref[...]
        dgv = rstd * (dxh - jnp.mean(dxh, axis=-1, keepdims=True) - xh * jnp.mean(dxh * xh, axis=-1, keepdims=True))
        dv_ref[...] = (dgv * dgv_dv).astype(BF16)

    row = lambda col: pl.BlockSpec((SGU_CHUNK, D), lambda n: (n, col))
    vec = pl.BlockSpec((1, D), lambda n: (0, 0))
    wsp = pl.BlockSpec((G, SGU_CHUNK, SGU_CHUNK), lambda n: (0, 0, 0))
    bsp = pl.BlockSpec((SGU_CHUNK, G), lambda n: (0, 0))
    return pl.pallas_call(
        body, name=name, grid=(T // SGU_CHUNK,),
        in_specs=[row(4), row(5), row(0), vec, vec, wsp, bsp, pl.BlockSpec(memory_space=pl.ANY)],
        out_specs=[pl.BlockSpec((SGU_CHUNK, 2 * D), lambda n: (n, 2)), wsp, bsp, vec, vec],
        out_shape=[jax.ShapeDtypeStruct(dparts.shape, dparts.dtype),
                   jax.ShapeDtypeStruct((G, SGU_CHUNK, SGU_CHUNK), F32), jax.ShapeDtypeStruct((SGU_CHUNK, G), F32),
                   jax.ShapeDtypeStruct((1, D), F32), jax.ShapeDtypeStruct((1, D), F32)],
        scratch_shapes=[pltpu.VMEM((SGU_CHUNK, D), F32)], input_output_aliases={7: 0},
        compiler_params=_params("arbitrary"),
    )(parts, parts, dya, lnw, lnb, w, bt, dparts)


TBT = 256
VMEM_LIMIT_TOKEN_OUT = 58 * 1024 * 1024


def _rows_weight_spec(wg):
    return pl.BlockSpec(wg.shape, lambda i: (0, 0, 0))


def _full(w_ref):
    return w_ref[...].reshape(w_ref.shape[0] * w_ref.shape[1], w_ref.shape[2])


def _token_out_fwd(o, parts, ya, x, mod, hnw, wa, wb, wo, ctx_rows, name):
    T, D = x.shape
    nh = D // HEAD
    cb = ctx_rows // TBT

    def body(o_ref, og_ref, ga_ref, gb_ref, ya_ref, x_ref, mod_ref, hnw_ref, wa_ref, wb_ref, wo_ref,
             yb_ref, pa_ref, pb_ref, mg_ref, tmo_ref, xm_ref):
        ov = o_ref[...]
        so = _silu(og_ref[...])
        nw = hnw_ref[...]
        for h in range(nh):
            sl = slice(h * HEAD, (h + 1) * HEAD)
            seg = ov[:, sl]
            r = lax.rsqrt(jnp.mean(seg * seg, axis=-1, keepdims=True) + RMS_EPS)
            yb_ref[:, sl] = (seg * r * nw * so[:, sl]).astype(BF16)
        pa = _nn(ya_ref[...], _full(wa_ref))
        pb = _nn(yb_ref[...], _full(wb_ref))
        pa_ref[...] = pa
        pb_ref[...] = pb
        mg = (_sigmoid(ga_ref[...]) * pa + _sigmoid(gb_ref[...]) * pb).astype(BF16)
        mg_ref[...] = mg
        out = _nn(mg, _full(wo_ref))
        tmo_ref[...] = out
        xm_ref[...] = x_ref[...] + mod_ref[2:3, :] * out

    row = lambda col: pl.BlockSpec((TBT, D), lambda i: (i, col))
    wsp = _rows_weight_spec(wa)
    sd = lambda dt: jax.ShapeDtypeStruct((T, D), dt)
    return pl.pallas_call(
        body, name=name, grid=(T // TBT,),
        in_specs=[row(0), row(6), row(7), row(8), row(0), row(0),
                  pl.BlockSpec((None, N_MOD, D), lambda i: (_stream_of(i, cb), 0, 0)),
                  pl.BlockSpec((1, HEAD), lambda i: (0, 0)), wsp, wsp, wsp],
        out_specs=[row(0)] * 6,
        out_shape=[sd(BF16), sd(F32), sd(F32), sd(BF16), sd(F32), sd(F32)],
        compiler_params=_params("parallel", vmem=VMEM_LIMIT_TOKEN_OUT),
    )(o, parts, parts, parts, ya, x, mod, hnw, wa, wb, wo)


def _token_out_bwd(dx, tmo, pa, pb, o, parts, mod, hnw, wa, wb, wo, ctx_rows, name):
    T, D = dx.shape
    nh = D // HEAD
    cb = ctx_rows // TBT

    def body(dx_ref, tmo_ref, pa_ref, pb_ref, o_ref, og_ref, ga_ref, gb_ref, mod_ref, hnw_ref, wa_ref, wb_ref, wo_ref,
             dout_ref, dpa_ref, dpb_ref, dgate_ref, dya_ref, do_ref, dg1_ref, dhnw_ref):
        i = pl.program_id(0)

        @pl.when(i == 0)
        def _():
            dhnw_ref[...] = jnp.zeros_like(dhnw_ref)

        @pl.when((i == 0) | (i == cb))
        def _():
            dg1_ref[...] = jnp.zeros_like(dg1_ref)

        dxv = dx_ref[...]
        dg1_ref[...] += jnp.sum(dxv * tmo_ref[...], axis=0, keepdims=True)
        dout = (dxv * mod_ref[2:3, :]).astype(BF16)
        dout_ref[...] = dout
        dmg = _nt(dout, _full(wo_ref))
        sa = _sigmoid(ga_ref[...])
        sb = _sigmoid(gb_ref[...])
        dpa = (dmg * sa).astype(BF16)
        dpb = (dmg * sb).astype(BF16)
        dpa_ref[...] = dpa
        dpb_ref[...] = dpb
        dgate_ref[:, D:2 * D] = (dmg * pa_ref[...] * sa * (1.0 - sa)).astype(BF16)
        dgate_ref[:, 2 * D:3 * D] = (dmg * pb_ref[...] * sb * (1.0 - sb)).astype(BF16)
        dya_ref[...] = _nt(dpa, _full(wa_ref))
        dyb = _nt(dpb, _full(wb_ref))
        so, dso = _silu_both(og_ref[...])
        ov = o_ref[...]
        nw = hnw_ref[...]
        dnw = jnp.zeros((1, HEAD), F32)
        for h in range(nh):
            sl = slice(h * HEAD, (h + 1) * HEAD)
            seg = ov[:, sl]
            r = lax.rsqrt(jnp.mean(seg * seg, axis=-1, keepdims=True) + RMS_EPS)
            oh = seg * r
            dn = dyb[:, sl] * so[:, sl]
            dgate_ref[:, sl] = (dyb[:, sl] * oh * nw * dso[:, sl]).astype(BF16)
            dnw = dnw + jnp.sum(dn * oh, axis=0, keepdims=True)
            doh = dn * nw
            do_ref[:, sl] = r * (doh - oh * jnp.mean(doh * oh, axis=-1, keepdims=True))
        dhnw_ref[...] += dnw

    row = lambda col: pl.BlockSpec((TBT, D), lambda i: (i, col))
    wsp = _rows_weight_spec(wa)
    sd = lambda dt: jax.ShapeDtypeStruct((T, D), dt)
    return pl.pallas_call(
        body, name=name, grid=(T // TBT,),
        in_specs=[row(0), row(0), row(0), row(0), row(0), row(6), row(7), row(8),
                  pl.BlockSpec((None, N_MOD, D), lambda i: (_stream_of(i, cb), 0, 0)),
                  pl.BlockSpec((1, HEAD), lambda i: (0, 0)), wsp, wsp, wsp],
        out_specs=[row(0)] * 3 + [pl.BlockSpec((TBT, 3 * D), lambda i: (i, 2)), row(0), row(0),
                                  pl.BlockSpec((None, 1, D), lambda i: (_stream_of(i, cb), 0, 0)),
                                  pl.BlockSpec((1, HEAD), lambda i: (0, 0))],
        out_shape=[sd(BF16)] * 3 + [jax.ShapeDtypeStruct((T, 9 * D), BF16), sd(F32), sd(F32),
                                    jax.ShapeDtypeStruct((2, 1, D), F32), jax.ShapeDtypeStruct((1, HEAD), F32)],
        compiler_params=_params("arbitrary", vmem=VMEM_LIMIT_TOKEN_OUT),
    )(dx, tmo, pa, pb, o, parts, parts, parts, mod, hnw, wa, wb, wo)


def _conv_geometry(i, nb, cb):
    is_ctx = i < cb
    first = (i == 0) | (i == cb)
    last = (i == cb - 1) | (i == nb - 1)
    row = lax.broadcasted_iota(jnp.int32, (TB + 2 * GRID_W, 1), 0)
    w = row & (GRID_W - 1)
    left_ok = (w != 0) | is_ctx
    right_ok = (w != GRID_W - 1) | is_ctx
    return is_ctx, first, last, left_ok, right_ok


def _ext(p_ref, m_ref, n_ref, first, last):
    return jnp.concatenate([jnp.where(first, 0.0, p_ref[...]), m_ref[...], jnp.where(last, 0.0, n_ref[...])], axis=0)


def _shift_prev(e, ok):
    return jnp.where(ok, pltpu.roll(e, 1, 0), 0.0)


def _shift_next(e, ok):
    return jnp.where(ok, pltpu.roll(e, e.shape[0] - 1, 0), 0.0)


def _halo_specs(cbk, n64, coff=0):
    r = TB // GRID_W
    prev = pl.BlockSpec((GRID_W, cbk), lambda j, i: (jnp.maximum(r * i - 1, 0), j + coff))
    main = pl.BlockSpec((TB, cbk), lambda j, i: (i, j + coff))
    nxt = pl.BlockSpec((GRID_W, cbk), lambda j, i: (jnp.minimum(r * i + r, n64 - 1), j + coff))
    return [prev, main, nxt]


def _conv_cblock(dff):
    return _tile(dff, 1408)


def _conv_fwd(up, cw, cbias, ctx_rows, name):
    T, dff = up.shape[0], up.shape[1] // 2
    cbk = _conv_cblock(dff)
    nb, cb = T // TB, ctx_rows // TB
    nvb = dff // cbk

    def body(ap_ref, a_ref, an_ref, v_ref, cw_ref, cb_ref, ac_ref, act_ref):
        i = pl.program_id(1)
        is_ctx, first, last, lok, rok = _conv_geometry(i, nb, cb)
        e = _ext(ap_ref, a_ref, an_ref, first, last)
        el = _shift_prev(e, lok)
        er = _shift_next(e, rok)
        cwv = cw_ref[...]

        def comb(dr, lo):
            sl = slice(lo, lo + TB)
            return cwv[3 * dr:3 * dr + 1] * el[sl] + cwv[3 * dr + 1:3 * dr + 2] * e[sl] + cwv[3 * dr + 2:3 * dr + 3] * er[sl]

        out = comb(1, GRID_W) + jnp.where(is_ctx, 0.0, comb(0, 0) + comb(2, 2 * GRID_W))
        a_c = out + cb_ref[...]
        ac_ref[...] = a_c
        act_ref[...] = (_gelu(a_c) * v_ref[...]).astype(BF16)

    main = pl.BlockSpec((TB, cbk), lambda j, i: (i, j))
    return pl.pallas_call(
        body, name=name, grid=(dff // cbk, nb),
        in_specs=_halo_specs(cbk, T // GRID_W) + [pl.BlockSpec((TB, cbk), lambda j, i: (i, j + nvb)),
                                                 pl.BlockSpec((9, cbk), lambda j, i: (0, j)),
                                                 pl.BlockSpec((1, cbk), lambda j, i: (0, j))],
        out_specs=[main, main],
        out_shape=[jax.ShapeDtypeStruct((T, dff), F32), jax.ShapeDtypeStruct((T, dff), BF16)],
        compiler_params=_params("parallel", "parallel"),
    )(up, up, up, up, cw, cbias)


def _conv_bwd(up, ac, dact, cw, ctx_rows, name):
    T, dff = up.shape[0], up.shape[1] // 2
    cbk = _conv_cblock(dff)
    nb, cb = T // TB, ctx_rows // TB
    nvb = dff // cbk

    def body(ap_ref, a_ref, an_ref, vp_ref, v_ref, vn_ref, cp_ref, c_ref, cn_ref, dp_ref, d_ref, dn_ref, cw_ref,
             da_ref, dv_ref, dcw_ref, dcb_ref):
        i = pl.program_id(1)

        @pl.when(i == 0)
        def _():
            dcw_ref[...] = jnp.zeros_like(dcw_ref)
            dcb_ref[...] = jnp.zeros_like(dcb_ref)

        is_ctx, first, last, lok, rok = _conv_geometry(i, nb, cb)
        gl, dgl = _gelu_both(_ext(cp_ref, c_ref, cn_ref, first, last))
        g = _ext(dp_ref, d_ref, dn_ref, first, last) * _ext(vp_ref, v_ref, vn_ref, first, last) * dgl
        dv_ref[...] = (d_ref[...] * gl[GRID_W:GRID_W + TB]).astype(BF16)
        gm = _shift_prev(g, lok)
        gp = _shift_next(g, rok)
        cwv = cw_ref[...]

        def comb(dr, lo):
            sl = slice(lo, lo + TB)
            return cwv[3 * dr:3 * dr + 1] * gp[sl] + cwv[3 * dr + 1:3 * dr + 2] * g[sl] + cwv[3 * dr + 2:3 * dr + 3] * gm[sl]

        da = comb(1, GRID_W) + jnp.where(is_ctx, 0.0, comb(0, 2 * GRID_W) + comb(2, 0))
        da_ref[...] = da.astype(BF16)
        e = _ext(ap_ref, a_ref, an_ref, first, last)
        taps = [_shift_prev(e, lok), e, _shift_next(e, rok)]
        gmain = g[GRID_W:GRID_W + TB]
        dcb_ref[...] += jnp.sum(gmain, axis=0, keepdims=True)
        vert = jnp.where(is_ctx, 0.0, 1.0)
        for dr in range(3):
            sl = slice(dr * GRID_W, dr * GRID_W + TB)
            for dw in range(3):
                s = jnp.sum(gmain * taps[dw][sl], axis=0, keepdims=True)
                if dr != 1:
                    s = s * vert
                k = 3 * dr + dw
                dcw_ref[k:k + 1, :] += s

    main = pl.BlockSpec((TB, cbk), lambda j, i: (i, j))
    halo = _halo_specs(cbk, T // GRID_W)
    acc9 = pl.BlockSpec((9, cbk), lambda j, i: (0, j))
    acc1 = pl.BlockSpec((1, cbk), lambda j, i: (0, j))
    return pl.pallas_call(
        body, name=name, grid=(dff // cbk, nb),
        in_specs=halo + _halo_specs(cbk, T // GRID_W, nvb) + halo + halo + [acc9],
        out_specs=[main, main, acc9, acc1],
        out_shape=[jax.ShapeDtypeStruct((T, dff), BF16), jax.ShapeDtypeStruct((T, dff), BF16),
                   jax.ShapeDtypeStruct((9, dff), F32), jax.ShapeDtypeStruct((1, dff), F32)],
        compiler_params=_params("parallel", "arbitrary"),
    )(up, up, up, up, up, up, ac, ac, ac, dact, dact, dact, cw)


def _ffn_out_fwd(act, xm, mod, wd, ctx_rows, name):
    T, D = xm.shape
    dff = act.shape[1]
    cb = ctx_rows // TB

    def body(act_ref, x_ref, mod_ref, w_ref, xo_ref, fo_ref):
        out = _nn(act_ref[...], _full(w_ref))
        fo_ref[...] = out
        xo_ref[...] = x_ref[...] + mod_ref[5:6, :] * out

    row = pl.BlockSpec((TB, D), lambda i: (i, 0))
    return pl.pallas_call(
        body, name=name, grid=(T // TB,),
        in_specs=[pl.BlockSpec((TB, dff), lambda i: (i, 0)), row,
                  pl.BlockSpec((None, N_MOD, D), lambda i: (_stream_of(i, cb), 0, 0)),
                  _rows_weight_spec(wd)],
        out_specs=[row, row],
        out_shape=[jax.ShapeDtypeStruct((T, D), F32), jax.ShapeDtypeStruct((T, D), F32)],
        compiler_params=_params("parallel"),
    )(act, xm, mod, wd)


def _ffn_out_bwd(dx, fo, mod, wd, ctx_rows, name):
    T, D = dx.shape
    dff = N_CHIPS * wd.shape[1]
    cb = ctx_rows // TB

    def body(dx_ref, fo_ref, mod_ref, w_ref, dout_ref, dact_ref, dg2_ref):
        i = pl.program_id(0)

        @pl.when((i == 0) | (i == cb))
        def _():
            dg2_ref[...] = jnp.zeros_like(dg2_ref)

        dxv = dx_ref[...]
        dg2_ref[...] += jnp.sum(dxv * fo_ref[...], axis=0, keepdims=True)
        dout = (dxv * mod_ref[5:6, :]).astype(BF16)
        dout_ref[...] = dout
        dact_ref[...] = _nt(dout, _full(w_ref))

    row = pl.BlockSpec((TB, D), lambda i: (i, 0))
    return pl.pallas_call(
        body, name=name, grid=(T // TB,),
        in_specs=[row, row, pl.BlockSpec((None, N_MOD, D), lambda i: (_stream_of(i, cb), 0, 0)),
                  _rows_weight_spec(wd)],
        out_specs=[row, pl.BlockSpec((TB, dff), lambda i: (i, 0)),
                   pl.BlockSpec((None, 1, D), lambda i: (_stream_of(i, cb), 0, 0))],
        out_shape=[jax.ShapeDtypeStruct((T, D), BF16), jax.ShapeDtypeStruct((T, dff), F32),
                   jax.ShapeDtypeStruct((2, 1, D), F32)],
        compiler_params=_params("arbitrary"),
    )(dx, fo, mod, wd)


def _loss_bwd(x, target, fw, ctx_rows, name):
    T, D = x.shape
    cb = ctx_rows // TB

    def body(x_ref, t_ref, fw_ref, dx_ref, loss_ref, dfw_ref):
        i = pl.program_id(0)

        @pl.when(i == 0)
        def _():
            loss_ref[...] = jnp.zeros_like(loss_ref)
            dfw_ref[...] = jnp.zeros_like(dfw_ref)

        @pl.when(i < cb)
        def _():
            dx_ref[...] = jnp.zeros_like(dx_ref)

        @pl.when(i >= cb)
        def _():
            xv = x_ref[...]
            r = lax.rsqrt(jnp.mean(xv * xv, axis=-1, keepdims=True) + RMS_EPS)
            xh = xv * r
            fwv = fw_ref[...]
            err = xh * fwv - t_ref[...]
            loss_ref[...] += (0.5 / D) * jnp.sum(err * err)
            dy = err * (1.0 / D)
            dfw_ref[...] += jnp.sum(dy * xh, axis=0, keepdims=True)
            dxh = dy * fwv
            dx_ref[...] = r * (dxh - xh * jnp.mean(dxh * xh, axis=-1, keepdims=True))

    row = pl.BlockSpec((TB, D), lambda i: (i, 0))
    return pl.pallas_call(
        body, name=name, grid=(T // TB,),
        in_specs=[row, pl.BlockSpec((TB, D), lambda i: (jnp.maximum(i - cb, 0), 0)), pl.BlockSpec((1, D), lambda i: (0, 0))],
        out_specs=[row, pl.BlockSpec((1, 128), lambda i: (0, 0)), pl.BlockSpec((1, D), lambda i: (0, 0))],
        out_shape=[jax.ShapeDtypeStruct((T, D), F32), jax.ShapeDtypeStruct((1, 128), F32),
                   jax.ShapeDtypeStruct((1, D), F32)],
        compiler_params=_params("arbitrary"),
    )(x, target, fw)


def _adamw(w, gs, m, v, name):
    L, R, C = w.shape
    assert len(gs) == L
    rb = _rows_tile(R, max(16, (1 << 18) // C // 16 * 16))
    bc1 = 1.0 - ADAM_B1 ** ADAM_STEP
    bc2 = 1.0 - ADAM_B2 ** ADAM_STEP

    def body(w_ref, m_ref, v_ref, *rest):
        g_refs, (g_ref, d_ref, nm_ref, nv_ref) = rest[:L], rest[L:]
        layer = pl.program_id(0)
        for li in range(L):
            @pl.when(layer == li)
            def _():
                gv = g_refs[li][...]
                g_ref[...] = gv
                nm = ADAM_B1 * m_ref[...] + (1.0 - ADAM_B1) * gv
                nv = ADAM_B2 * v_ref[...] + (1.0 - ADAM_B2) * (gv * gv)
                nm_ref[...] = nm
                nv_ref[...] = nv
                d_ref[...] = -ADAM_LR * ((nm / bc1) / (jnp.sqrt(nv / bc2) + ADAM_EPS) + ADAM_WD * w_ref[...])

    blk = pl.BlockSpec((None, rb, C), lambda l, i: (l, i, 0))
    gblk = pl.BlockSpec((rb, C), lambda l, i: (i, 0))
    sd = jax.ShapeDtypeStruct((L, R, C), F32)
    return pl.pallas_call(
        body, name=name, grid=(L, R // rb), in_specs=[blk] * 3 + [gblk] * L, out_specs=[blk] * 4, out_shape=[sd] * 4,
        compiler_params=_params("parallel", "parallel"),
    )(w, m, v, *gs)


def _local_step(xs, cv, target, W, layer_weights, on_layer_grads, ctx_rows):
    T, D = xs.shape
    depth = W["norm1_w"].shape[0]
    saved = []
    X = xs
    for l in range(depth):
        s = {}
        Wl = layer_weights(l, X)
        mod_all, sa = _mod_fwd(cv, Wl["ada_w"], W["ada_b"][l][None, :] + Wl["token"], f"mod_fwd_{l}")
        mod = mod_all[:2].reshape(2, N_MOD, D)
        h1 = _norm_mod(X, W["norm1_w"][l][None, :], mod, 0, ctx_rows, f"norm1_{l}")
        parts = _mm_nn_w(h1, Wl["w_in"], F32, f"in_proj_{l}")
        o_f, st_f = _hgrn_fwd(parts, W["hlb"], l, False, ctx_rows, f"hgrn_fwd_f_{l}")
        o, st_b = _hgrn_fwd(parts, W["hlb"], l, True, ctx_rows, f"hgrn_fwd_b_{l}", o_add=o_f)
        ya = _sgu_fwd(parts, W["sgu_ln_w"][l][None, :], W["sgu_ln_b"][l][None, :], W["sgu_w"][l], W["sgu_bt"][l],
                      f"sgu_fwd_{l}")
        Wl.update(Wl.pop("late")(ya))
        yb, pa, pb, mg, tmo, xm = _token_out_fwd(o, parts, ya, X, mod, W["hnw"][l][None, :] + Wl["late_token"], Wl["w_a"],
                                                 Wl["w_b"], Wl["w_o"], ctx_rows, f"token_out_fwd_{l}")
        h2 = _norm_mod(xm, W["norm2_w"][l][None, :], mod, 3, ctx_rows, f"norm2_{l}")
        up = _mm_nn_w(h2, Wl["w_up"], F32, f"up_proj_{l}")
        ac, act = _conv_fwd(up, Wl["conv_w"], W["conv_b"][l][None, :], ctx_rows, f"conv_fwd_{l}")
        xo, fo = _ffn_out_fwd(act, xm, mod, Wl["w_down"], ctx_rows, f"ffn_out_fwd_{l}")
        s.update(X=X, Wl=Wl, mod=mod, mod_all=mod_all, sa=sa, h1=h1, parts=parts, o=o, st_f=st_f, st_b=st_b, ya=ya, yb=yb,
                 pa=pa, pb=pb, mg=mg, tmo=tmo, xm=xm, h2=h2, up=up, ac=ac, act=act, fo=fo)
        saved.append(s)
        X = xo

    dX, loss_row, dfw = _loss_bwd(X, target, W["final_norm_w"][None, :], ctx_rows, "loss_bwd")
    G = {k: [None] * depth for k in ("ada_b", "norm1_w", "sgu_ln_w", "sgu_ln_b", "sgu_w", "sgu_b", "hlb1", "hnw", "norm2_w",
                                     "conv_w", "conv_b", "dmod")}
    dcv = jnp.zeros_like(cv)
    for l in reversed(range(depth)):
        s = saved[l]
        mod, Wl = s["mod"], s["Wl"]
        big = {}
        dout2, dact, dg2 = _ffn_out_bwd(dX, s["fo"], mod, Wl["w_down"], ctx_rows, f"ffn_out_bwd_{l}")
        big["w_down"] = _mm_tn(s["act"], dout2, F32, f"dw_down_{l}")
        da, dv, dcw, dcb = _conv_bwd(s["up"], s["ac"], dact, Wl["conv_w"], ctx_rows, f"conv_bwd_{l}")
        G["conv_w"][l], G["conv_b"][l] = dcw, dcb[0]
        big["w_up"] = _mm_tn(s["h2"], (da, dv), F32, f"dw_up_{l}", out_chips=True)
        dh2 = _mm_nt_w((da, dv), Wl["w_up"], F32, f"dh2_{l}")
        dxm, dm2, dnw2 = _norm_mod_bwd(dh2, s["xm"], dX, W["norm2_w"][l][None, :], mod, 3, ctx_rows, f"norm2_bwd_{l}")
        G["norm2_w"][l] = dnw2[0]
        (dout1, dpa, dpb, dparts, dya, do, dg1, dhnw) = _token_out_bwd(
            dxm, s["tmo"], s["pa"], s["pb"], s["o"], s["parts"], mod, W["hnw"][l][None, :], Wl["w_a"], Wl["w_b"], Wl["w_o"],
            ctx_rows, f"token_out_bwd_{l}")
        G["hnw"][l] = dhnw[0]
        big["w_o"] = _mm_tn(s["mg"], dout1, F32, f"dw_o_{l}")
        big["w_a"] = _mm_tn(s["ya"], dpa, F32, f"dw_a_{l}")
        big["w_b"] = _mm_tn(s["yb"], dpb, F32, f"dw_b_{l}")
        tok = on_layer_grads(l, "early", big)
        dparts, dsw, dsbt, dlnw, dlnb = _sgu_bwd(s["parts"], dya, W["sgu_ln_w"][l][None, :], W["sgu_ln_b"][l][None, :] + tok,
                                                 W["sgu_w"][l], W["sgu_bt"][l], dparts, f"sgu_bwd_{l}")
        G["sgu_w"][l], G["sgu_b"][l], G["sgu_ln_w"][l], G["sgu_ln_b"][l] = dsw, dsbt.T, dlnw[0], dlnb[0]
        dq_f, dz_f, di_f, dlb_f = _hgrn_bwd(s["parts"], W["hlb"], do, s["st_f"], l, False, ctx_rows, f"hgrn_bwd_f_{l}")
        dparts, dlb_b = _hgrn_bwd(s["parts"], W["hlb"], do, s["st_b"], l, True, ctx_rows, f"hgrn_bwd_b_{l}",
                                  other=(dq_f, dz_f, di_f), dparts=dparts)
        G["hlb1"][l] = jnp.concatenate([dlb_f[0], dlb_b[0]])
        tok = on_layer_grads(l, "late", {"w_in": _mm_tn(s["h1"], dparts, F32, f"dw_in_{l}", out_chips=True)})
        dh1 = _mm_nt_w(dparts, Wl["w_in"], F32, f"dh1_{l}")
        tok = tok + on_layer_grads(l, "end", {"after": dh1})
        dX, dm1, dnw1 = _norm_mod_bwd(dh1, s["X"], dxm, W["norm1_w"][l][None, :] + tok, mod, 0, ctx_rows, f"norm1_bwd_{l}")
        G["norm1_w"][l] = dnw1[0]
        dmod = jnp.concatenate([dm1, dg1, dm2, dg2], axis=1).reshape(2, N_MOD * D)
        dmod16 = jnp.concatenate([dmod, jnp.zeros((cv.shape[0] - 2, N_MOD * D), F32)], axis=0)
        G["ada_b"][l] = dmod[0] + dmod[1]
        G["dmod"][l] = dmod
        dcv = dcv + _cvec_bwd(dmod16, Wl["ada_w"], cv, f"dcvec_{l}")
    G["c_ctx"] = dcv[0]
    G["final_norm_w"] = dfw[0]
    return loss_row[0, 0], dX, G, saved[0]["sa"]


def _chip_peers(x, y, c):
    return [((1 - x, y, c), 2 * (1 - x) + y), ((x, 1 - y, c), 2 * x + 1 - y), ((1 - x, 1 - y, c), 2 * (1 - x) + 1 - y)]


def _rdma_call(ins, out_shapes, plan, n_remote, n_local, name, aliases=None):
    n_in, n_out = len(ins), len(out_shapes)

    def body(*refs):
        in_refs, out_refs = refs[:n_in], refs[n_in:n_in + n_out]
        send_sems, recv_sems, local_sems = refs[n_in + n_out:]
        x, y, c = lax.axis_index("x"), lax.axis_index("y"), lax.axis_index("c")
        remote, local = plan(in_refs, out_refs, x, y, c)
        assert len(remote) == n_remote and len(local) == n_local, (name, len(remote), len(local))
        copies = [pltpu.make_async_copy(s, d, local_sems.at[i]) for i, (s, d) in enumerate(local)]
        copies += [pltpu.make_async_remote_copy(src_ref=s, dst_ref=d, send_sem=send_sems.at[k], recv_sem=recv_sems.at[k],
                                                device_id=dev, device_id_type=pl.DeviceIdType.MESH)
                   for k, (s, d, dev) in enumerate(remote)]
        for cp in copies:
            cp.start()
        for cp in copies:
            cp.wait()

    hbm = pl.BlockSpec(memory_space=pltpu.HBM)
    return pl.pallas_call(
        body, name=name, in_specs=[hbm] * n_in, out_specs=[hbm] * n_out, out_shape=out_shapes,
        scratch_shapes=[pltpu.SemaphoreType.DMA((n_remote,)), pltpu.SemaphoreType.DMA((n_remote,)),
                        pltpu.SemaphoreType.DMA((max(n_local, 1),))],
        input_output_aliases=aliases or {},
    )(*ins)


DMA_PIECE_BYTES = 1 << 18
DMA_MAX_PIECES = 8


def _row_pieces(shape, dtype):
    rows = shape[0]
    row_bytes = jnp.dtype(dtype).itemsize
    for d in shape[1:]:
        row_bytes *= d
    n = 1
    while n < DMA_MAX_PIECES and rows % (2 * n * 16) == 0 and rows * row_bytes // (2 * n) >= DMA_PIECE_BYTES:
        n *= 2
    return [(i * (rows // n), rows // n) for i in range(n)]


def _half_pieces(o, c):
    r2 = o.shape[1] // 2
    return [pl.ds(c * r2 + st, sz) for st, sz in _row_pieces((r2,) + o.shape[2:], o.dtype)]


def _n_half_pieces(arrays):
    return sum(len(_row_pieces((a.shape[1] // 2,) + a.shape[2:], a.dtype)) for a in arrays)


def _plan_gather_far(lands, x, y, c):
    me = 2 * x + y
    return [(o.at[me, rows], o.at[me, rows], dev) for dev, _ in _chip_peers(x, y, c) for o in lands
            for rows in _half_pieces(o, c)]


def _plan_gather_near(lands, x, y, c):
    return [(o.at[idx, rows], o.at[idx, rows], (x, y, 1 - c)) for _, idx in _chip_peers(x, y, c) for o in lands
            for rows in _half_pieces(o, c)]


def _gather_weights(lands, name):
    n = len(lands)
    n_far = (N_CHIPS - 1) * _n_half_pieces(lands)

    def body(*refs):
        outs = refs[n:2 * n]
        far_send, far_recv, near_send, near_recv = refs[2 * n:]
        x, y, c = lax.axis_index("x"), lax.axis_index("y"), lax.axis_index("c")
        mk = lambda plan, send, recv: [
            pltpu.make_async_remote_copy(src_ref=s, dst_ref=d, send_sem=send.at[k], recv_sem=recv.at[k], device_id=dev,
                                         device_id_type=pl.DeviceIdType.MESH)
            for k, (s, d, dev) in enumerate(plan(outs, x, y, c))]
        far, near = mk(_plan_gather_far, far_send, far_recv), mk(_plan_gather_near, near_send, near_recv)
        assert len(far) == n_far and len(near) == n_far
        for cp in far:
            cp.start()
        for k in range(n_far):
            far[k].wait_recv()
            near[k].start()
        for k in range(n_far):
            near[k].wait_recv()
        for cp in far + near:
            cp.wait_send()

    hbm = pl.BlockSpec(memory_space=pltpu.HBM)
    sems = pltpu.SemaphoreType.DMA((n_far,))
    return pl.pallas_call(
        body, name=name, in_specs=[hbm] * n, out_specs=[hbm] * n,
        out_shape=[jax.ShapeDtypeStruct(a.shape, a.dtype) for a in lands],
        scratch_shapes=[sems, sems, sems, sems], input_output_aliases={i: i for i in range(n)},
    )(*lands)


def _gather_all(v, name):
    def plan(ins, outs, x, y, c):
        (s,), (o,) = ins, outs
        me = 4 * x + 2 * y + c
        flip = lambda a, f: 1 - a if f else a
        remote = [(s, o.at[me], (flip(x, m & 4), flip(y, m & 2), flip(c, m & 1))) for m in range(1, 8)]
        return remote, [(s, o.at[me])]

    return _rdma_call([v], [jax.ShapeDtypeStruct((8,) + v.shape, v.dtype)], plan, 7, 1, name)[0]


def _plan_pair(ins, lands, x, y, c):
    return [(a.at[j, 1 - c, pl.ds(st, sz)], o.at[j, pl.ds(st, sz)], (x, y, 1 - c)) for a, o in zip(ins, lands)
            for j in range(N_CHIPS) for st, sz in _row_pieces(a.shape[2:], a.dtype)]


def _n_pair_copies(parts):
    return N_CHIPS * sum(len(_row_pieces(a.shape[2:], a.dtype)) for a in parts)


def _reduce_pair(parts, name):
    shapes = [jax.ShapeDtypeStruct((N_CHIPS,) + a.shape[2:], a.dtype) for a in parts]
    return _rdma_call(parts, shapes, lambda ins, outs, x, y, c: (_plan_pair(ins, outs, x, y, c), []),
                      _n_pair_copies(parts), 0, name)


def _plan_chips(ins, lands, x, y, c):
    me = 2 * x + y
    return [(a.at[idx, pl.ds(st, sz)], o.at[me, pl.ds(st, sz)], dev) for dev, idx in _chip_peers(x, y, c)
            for a, o in zip(ins, lands) for st, sz in _row_pieces(a.shape[1:], a.dtype)]


def _n_chips_copies(parts):
    return (N_CHIPS - 1) * sum(len(_row_pieces(a.shape[1:], a.dtype)) for a in parts)


def _reduce_chips(parts, name):
    shapes = [jax.ShapeDtypeStruct(a.shape, a.dtype) for a in parts]
    return _rdma_call(parts, shapes, lambda ins, outs, x, y, c: (_plan_chips(ins, outs, x, y, c), []),
                      _n_chips_copies(parts), 0, name)


def _gather_pair(halves, name):
    def plan(ins, outs, x, y, c):
        return [(o.at[c, pl.ds(st, sz)], o.at[c, pl.ds(st, sz)], (x, y, 1 - c)) for o in outs
                for st, sz in _row_pieces(o.shape[1:], o.dtype)], []

    shapes = [jax.ShapeDtypeStruct(a.shape, a.dtype) for a in halves]
    n_remote = sum(len(_row_pieces(a.shape[1:], a.dtype)) for a in halves)
    return _rdma_call(halves, shapes, plan, n_remote, 0, name, aliases={i: i for i in range(len(halves))})


def _split_start(ins, lands, plan, n_remote, name):
    n_buf = len(ins) + len(lands)

    def body(*refs):
        in_refs, land_refs = refs[:len(ins)], refs[len(ins):n_buf]
        send_sems, recv_sems, token = refs[n_buf], refs[n_buf + 1], refs[-1]
        x, y, c = lax.axis_index("x"), lax.axis_index("y"), lax.axis_index("c")
        remote = plan(in_refs, land_refs, x, y, c)
        assert len(remote) == n_remote, (name, len(remote))
        for k, (s, d, dev) in enumerate(remote):
            pltpu.make_async_remote_copy(src_ref=s, dst_ref=d, send_sem=send_sems.at[k], recv_sem=recv_sems.at[k],
                                         device_id=dev, device_id_type=pl.DeviceIdType.MESH).start()
        token[...] = jnp.zeros_like(token)

    hbm = pl.BlockSpec(memory_space=pltpu.HBM)
    sem = pl.BlockSpec(memory_space=pltpu.SEMAPHORE)
    bufs = list(ins) + list(lands)
    out = pl.pallas_call(
        body, name=name, in_specs=[hbm] * n_buf,
        out_specs=(sem, sem) + (hbm,) * n_buf + (pl.BlockSpec(memory_space=pltpu.VMEM),),
        out_shape=(pltpu.SemaphoreType.DMA((n_remote,)), pltpu.SemaphoreType.DMA((n_remote,)))
        + tuple(pltpu.HBM(a.shape, a.dtype) for a in bufs) + (jax.ShapeDtypeStruct((8, 128), F32),),
        input_output_aliases={i: 2 + i for i in range(n_buf)},
        compiler_params=pltpu.CompilerParams(has_side_effects=pltpu.SideEffectType.DATAFLOW_SIDE_EFFECTING),
    )(*[pltpu.with_memory_space_constraint(a, pltpu.HBM) for a in bufs])
    return dict(send=out[0], recv=out[1], ins=list(out[2:2 + len(ins)]), lands=list(out[2 + len(ins):2 + n_buf]),
                token=out[-1][0, 0], token_array=out[-1], plan=plan, n_remote=n_remote)


def _split_wait(st, after, name):
    n_in, n_buf = len(st["ins"]), len(st["ins"]) + len(st["lands"])
    plan, n_remote = st["plan"], st["n_remote"]

    def body(*refs):
        in_refs, land_refs = refs[:n_in], refs[n_in:n_buf]
        send_sems, recv_sems = refs[n_buf], refs[n_buf + 1]
        x, y, c = lax.axis_index("x"), lax.axis_index("y"), lax.axis_index("c")
        for k, (s, d, dev) in enumerate(plan(in_refs, land_refs, x, y, c)):
            cp = pltpu.make_async_remote_copy(src_ref=s, dst_ref=d, send_sem=send_sems.at[k], recv_sem=recv_sems.at[k],
                                              device_id=dev, device_id_type=pl.DeviceIdType.MESH)
            cp.wait_send()
            cp.wait_recv()

    hbm = pl.BlockSpec(memory_space=pltpu.HBM)
    sem = pl.BlockSpec(memory_space=pltpu.SEMAPHORE)
    bufs = st["ins"] + st["lands"]
    out = pl.pallas_call(
        body, name=name, in_specs=[hbm] * n_buf + [sem, sem, pl.BlockSpec(memory_space=pl.ANY)],
        out_specs=[hbm] * n_buf, out_shape=[pltpu.HBM(a.shape, a.dtype) for a in bufs],
        input_output_aliases={i: i for i in range(n_buf)},
        compiler_params=pltpu.CompilerParams(has_side_effects=pltpu.SideEffectType.DATAFLOW_SIDE_EFFECTING),
    )(*bufs, st["send"], st["recv"], after)
    return list(out[:n_in]), list(out[n_in:])


def _pair_forward(lands, name):
    shapes = [jax.ShapeDtypeStruct(a.shape, a.dtype) for a in lands]
    return _rdma_call(lands, shapes, lambda ins, outs, x, y, c: (_plan_gather_near(outs, x, y, c), []),
                      (N_CHIPS - 1) * _n_half_pieces(lands), 0, name, aliases={i: i for i in range(len(lands))})


def _sum_block_rows(r, C):
    return _rows_tile(r, max(16, (1 << 18) // C // 16 * 16))


def _sum_pair(a, recv, cidx, name):
    nch, _, r, C = a.shape
    rb = _sum_block_rows(r, C)

    def body(c_ref, a_ref, r_ref, o_ref):
        o_ref[...] = (a_ref[...] + r_ref[...]).astype(BF16)

    blk = pl.BlockSpec((None, rb, C), lambda j, i, c: (j, i, 0))
    return pl.pallas_call(
        body, name=name,
        grid_spec=pltpu.PrefetchScalarGridSpec(
            num_scalar_prefetch=1, grid=(nch, r // rb),
            in_specs=[pl.BlockSpec((None, None, rb, C), lambda j, i, c: (j, c[0], i, 0)), blk], out_specs=blk),
        out_shape=jax.ShapeDtypeStruct((nch, r, C), BF16),
        compiler_params=_params("parallel", "parallel"),
    )(cidx, a, recv)


def _sum_chips(mine, recv, ids, name):
    nch, r, C = recv.shape
    rb = _sum_block_rows(r, C)

    def body(ids_ref, m_ref, *rest):
        r_refs, o_ref = rest[:nch], rest[nch]
        chip = ids_ref[1]
        own = m_ref[...].astype(F32)
        acc = jnp.where(chip == 0, own, r_refs[0][...].astype(F32))
        for q in range(1, nch):
            acc = acc + jnp.where(chip == q, own, r_refs[q][...].astype(F32))
        o_ref[...] = acc

    def slot(q):
        return pl.BlockSpec((None, rb, C), lambda i, ids: (jnp.where(ids[1] == q, (q + 1) % nch, q), i, 0))

    return pl.pallas_call(
        body, name=name,
        grid_spec=pltpu.PrefetchScalarGridSpec(
            num_scalar_prefetch=1, grid=(r // rb,),
            in_specs=[pl.BlockSpec((None, rb, C), lambda i, ids: (ids[1], i, 0))] + [slot(q) for q in range(nch)],
            out_specs=pl.BlockSpec((None, rb, C), lambda i, ids: (ids[0], i, 0))),
        out_shape=jax.ShapeDtypeStruct((N_CORES, r, C), F32),
        compiler_params=_params("parallel"),
    )(ids, mine, *([recv] * nch))


PACK_COLS = 1024
_SHARDED = ("ada_w", "w_in", "w_branch_a", "w_branch_b", "w_out", "ffn_w_up", "ffn_w_down")
_LAYER_KEYS = ("ada_w", "w_in", "w_a", "w_b", "w_o", "w_up", "w_down")
_SMALL = ("c_ctx", "ada_b", "norm1_w", "sgu_ln_w", "sgu_ln_b", "sgu_w", "sgu_b", "hgrn_lower_bounds", "hgrn_norm_w",
          "norm2_w", "ffn_conv_b", "final_norm_w")
_ORDER = ("c_ctx", "ada_w", "ada_b", "norm1_w", "w_in", "sgu_ln_w", "sgu_ln_b", "sgu_w", "sgu_b", "hgrn_lower_bounds",
          "hgrn_norm_w", "w_branch_a", "w_branch_b", "w_out", "norm2_w", "ffn_w_up", "ffn_conv_w", "ffn_conv_b",
          "ffn_w_down", "final_norm_w")


def _pad_to(v, n):
    return jnp.concatenate([v, jnp.zeros((n - v.shape[0],), v.dtype)]) if v.shape[0] < n else v


def _round_up(n, m):
    return (n + m - 1) // m * m


def _pack(arrays, n_pad):
    flat = jnp.concatenate([a.reshape(-1) for a in arrays])
    return _pad_to(flat, n_pad)


def _unpack(flat, like):
    out, off = [], 0
    for a in like:
        out.append(flat[off:off + a.size].reshape(a.shape))
        off += a.size
    return out


def kernel(x, c, ctx, c_ctx, ada_w, ada_b, norm1_w, w_in, sgu_ln_w, sgu_ln_b, sgu_w, sgu_b, hgrn_lower_bounds, hgrn_norm_w, w_branch_a, w_branch_b, w_out, norm2_w, ffn_w_up, ffn_conv_w, ffn_conv_b, ffn_w_down, final_norm_w, loss_target, m_c_ctx, m_ada_w, m_ada_b, m_norm1_w, m_w_in, m_sgu_ln_w, m_sgu_ln_b, m_sgu_w, m_sgu_b, m_hgrn_lower_bounds, m_hgrn_norm_w, m_w_branch_a, m_w_branch_b, m_w_out, m_norm2_w, m_ffn_w_up, m_ffn_conv_w, m_ffn_conv_b, m_ffn_w_down, m_final_norm_w, v_c_ctx, v_ada_w, v_ada_b, v_norm1_w, v_w_in, v_sgu_ln_w, v_sgu_ln_b, v_sgu_w, v_sgu_b, v_hgrn_lower_bounds, v_hgrn_norm_w, v_w_branch_a, v_w_branch_b, v_w_out, v_norm2_w, v_ffn_w_up, v_ffn_conv_w, v_ffn_conv_b, v_ffn_w_down, v_final_norm_w):
    w = dict(c_ctx=c_ctx, ada_w=ada_w, ada_b=ada_b, norm1_w=norm1_w, w_in=w_in, sgu_ln_w=sgu_ln_w, sgu_ln_b=sgu_ln_b,
             sgu_w=sgu_w, sgu_b=sgu_b, hgrn_lower_bounds=hgrn_lower_bounds, hgrn_norm_w=hgrn_norm_w, w_branch_a=w_branch_a,
             w_branch_b=w_branch_b, w_out=w_out, norm2_w=norm2_w, ffn_w_up=ffn_w_up, ffn_conv_w=ffn_conv_w,
             ffn_conv_b=ffn_conv_b, ffn_w_down=ffn_w_down, final_norm_w=final_norm_w)
    mom = dict(zip(_ORDER, (m_c_ctx, m_ada_w, m_ada_b, m_norm1_w, m_w_in, m_sgu_ln_w, m_sgu_ln_b, m_sgu_w, m_sgu_b,
                            m_hgrn_lower_bounds, m_hgrn_norm_w, m_w_branch_a, m_w_branch_b, m_w_out, m_norm2_w, m_ffn_w_up,
                            m_ffn_conv_w, m_ffn_conv_b, m_ffn_w_down, m_final_norm_w)))
    var = dict(zip(_ORDER, (v_c_ctx, v_ada_w, v_ada_b, v_norm1_w, v_w_in, v_sgu_ln_w, v_sgu_ln_b, v_sgu_w, v_sgu_b,
                            v_hgrn_lower_bounds, v_hgrn_norm_w, v_w_branch_a, v_w_branch_b, v_w_out, v_norm2_w, v_ffn_w_up,
                            v_ffn_conv_w, v_ffn_conv_b, v_ffn_w_down, v_final_norm_w)))
    depth, D = norm1_w.shape
    dff = ffn_conv_b.shape[1]
    ctx_rows, seq = ctx.shape[1], x.shape[1]

    assert depth == 2, "the lower-bound softmax is written for two layers"
    core = lax.axis_index("c")
    chip = 2 * lax.axis_index("x") + lax.axis_index("y")
    ids = jnp.stack([core, chip]).astype(jnp.int32)

    first, rest = _LAYER_KEYS[:2], _LAYER_KEYS[2:]
    shard = lambda l, k: w[_SHARDED[_LAYER_KEYS.index(k)]][l].astype(BF16)
    started, conv_full = {}, []

    def landing(s):
        return lax.dynamic_update_slice(lax.empty((N_CHIPS,) + s.shape, s.dtype), s[None], (chip,) + (0,) * s.ndim)

    def start_gather(l, keys, tag):
        lands = [landing(shard(l, k)) for k in keys]
        started[tag] = _split_start([], lands, lambda ins, lds, x, y, c: _plan_gather_far(lds, x, y, c),
                                    (N_CHIPS - 1) * _n_half_pieces(lands), f"gather_start_{tag}")
        return started[tag]["token"]

    def finish_gather(keys, tag, after):
        _, lands = _split_wait(started[tag], after, f"gather_wait_{tag}")
        return dict(zip(keys, _pair_forward(lands, f"gather_forward_{tag}")))

    def layer_weights(l, after):
        if l == 0:
            got = _gather_weights([landing(shard(0, k)) for k in first] + [landing(ffn_conv_w)], "gather_weights_first")
            conv_full.append(jnp.transpose(got[-1], (1, 2, 3, 0, 4)).reshape(depth, 9, dff))
            out = dict(zip(first, got), token=start_gather(0, rest, "rest_0"))
        else:
            out = dict(finish_gather(first, f"first_{l}", after), token=0.0)

        def late(after_late):
            more = finish_gather(rest, f"rest_{l}", after_late)
            more["late_token"] = 0.0
            if l + 1 < depth:
                more["late_token"] = start_gather(l + 1, first, f"first_{l + 1}") + start_gather(l + 1, rest, f"rest_{l + 1}")
            return more

        return dict(out, conv_w=conv_full[0][l], late=late)

    groups, order = {}, []

    def as_parts(gs):
        return [g.reshape(N_CHIPS, N_CORES, g.size // (N_CHIPS * N_CORES * g.shape[-1]), g.shape[-1]) for g in gs]

    def pair_start(tag, l, keys, gs):
        parts = as_parts(gs)
        lands = [lax.empty((N_CHIPS,) + p.shape[2:], p.dtype) for p in parts]
        groups[tag] = dict(l=l, keys=keys, pair=_split_start(parts, lands, _plan_pair, _n_pair_copies(parts),
                                                             f"reduce_pair_start_{tag}"))
        order.append(tag)
        return groups[tag]["pair"]["token"]

    def chips_start(tag, after):
        parts, other = _split_wait(groups[tag]["pair"], after, f"reduce_pair_wait_{tag}")
        sums = [_sum_pair(a, o, ids, f"sum_pair_{tag}_{i}") for i, (a, o) in enumerate(zip(parts, other))]
        lands = [lax.empty(s.shape, s.dtype) for s in sums]
        groups[tag]["chips"] = _split_start(sums, lands, _plan_chips, _n_chips_copies(sums), f"reduce_chips_start_{tag}")
        return groups[tag]["chips"]["token"]

    def chips_finish(tag, after):
        sums, recv = _split_wait(groups[tag]["chips"], after, f"reduce_chips_wait_{tag}")
        return {(groups[tag]["l"], k): _sum_chips(sums[i], recv[i], ids, f"sum_chips_{tag}_{i}")
                for i, k in enumerate(groups[tag]["keys"])}

    def on_layer_grads(l, stage, gs):
        if stage == "early":
            return pair_start(f"early_{l}", l, list(gs), list(gs.values()))
        if stage == "late":
            return pair_start(f"late_{l}", l, ["w_in"], [gs["w_in"]]) + chips_start(f"early_{l}", gs["w_in"])
        return chips_start(f"late_{l}", gs["after"])

    W = dict(ada_b=ada_b, norm1_w=norm1_w, sgu_ln_w=sgu_ln_w, sgu_ln_b=sgu_ln_b, sgu_w=sgu_w.astype(BF16),
             sgu_bt=jnp.swapaxes(sgu_b, 1, 2), hlb=hgrn_lower_bounds, hnw=hgrn_norm_w, norm2_w=norm2_w, conv_b=ffn_conv_b,
             final_norm_w=final_norm_w)
    xs = jnp.concatenate([ctx[0], x[0]], axis=0)
    cv = jnp.concatenate([c_ctx[None, :], c, jnp.zeros((14, D), F32)], axis=0)
    loss_local, dxs, G, sa = _local_step(xs, cv, loss_target[0], W, layer_weights, on_layer_grads, ctx_rows)
    loss = lax.psum(loss_local, ("x", "y", "c"))
    grad_x = dxs[ctx_rows:][None]

    pad8 = lambda a: jnp.pad(a, ((0, 8 - a.shape[0]), (0, 0)))
    fact = jnp.concatenate([pad8(sa[1:2].astype(F32))] + [pad8(G["dmod"][l][1].reshape(N_MOD, D)) for l in range(depth)]
                           + [pad8(G["dmod"][l][0].reshape(N_MOD, D)) for l in range(depth)], axis=0)
    facts = _gather_all(fact, "gather_mod_factors")
    lhs = jnp.concatenate([facts[:, 0].astype(BF16), jnp.broadcast_to(sa[0:1], (8, D))], axis=0)
    ada_cols = N_MOD * D // N_CHIPS
    g_ada = []
    for l in range(depth):
        lo_x, lo_c = 8 * (1 + l), 8 * (1 + depth + l)
        rhs = jnp.concatenate([facts[:, lo_x:lo_x + N_MOD].reshape(8, N_MOD * D),
                               facts[:, lo_c:lo_c + N_MOD].reshape(8, N_MOD * D)], axis=0)
        rhs = lax.dynamic_slice_in_dim(rhs, chip * ada_cols, ada_cols, axis=1).astype(BF16)
        g_ada.append(_mm_tn(lhs, rhs, F32, f"dw_ada_{l}"))

    dh = G["hlb1"][depth - 1]
    small_like = [w[k] for k in _SMALL] + [jnp.zeros((depth, 9, dff), F32)]
    small = [G["c_ctx"], jnp.stack(G["ada_b"]), jnp.stack(G["norm1_w"]), jnp.stack(G["sgu_ln_w"]), jnp.stack(G["sgu_ln_b"]),
             jnp.stack(G["sgu_w"]), jnp.stack(G["sgu_b"]), jnp.stack([-dh, dh]), jnp.stack(G["hnw"]), jnp.stack(G["norm2_w"]),
             jnp.stack(G["conv_b"]), G["final_norm_w"], jnp.stack(G["conv_w"])]
    n_small = sum(a.size for a in small)
    n_small_pad = _round_up(n_small, N_CORES * 16 * PACK_COLS)
    small_rows = n_small_pad // (N_CORES * PACK_COLS)
    small_rep = jnp.broadcast_to(_pack(small, n_small_pad).reshape(1, N_CORES, small_rows, PACK_COLS),
                                 (N_CHIPS, N_CORES, small_rows, PACK_COLS))
    small_parts = as_parts([small_rep])
    small_sums = [_sum_pair(small_parts[0], _reduce_pair(small_parts, "reduce_pair_small")[0], ids, "sum_pair_small")]
    groups["small"] = dict(l=None, keys=["small"], chips=_split_start(
        small_sums, [lax.empty(small_sums[0].shape, small_sums[0].dtype)], _plan_chips, _n_chips_copies(small_sums),
        "reduce_chips_start_small"))

    def gather_halves(halves, name):
        return dict(zip(halves, _gather_pair(list(halves.values()), name)))

    last = order[-1]
    halves = {}
    for tag in order[:-1]:
        halves.update(chips_finish(tag, groups["small"]["chips"]["token_array"]))
    reduced = gather_halves(halves, "gather_pair")
    grads, delta, new_m, new_v = {}, {}, {}, {}

    def adamw_sharded(i):
        k = _SHARDED[i]
        gs = g_ada if i == 0 else [reduced[(l, _LAYER_KEYS[i])].reshape(w[k].shape[1:]) for l in range(depth)]
        grads[k], delta[k], new_m[k], new_v[k] = _adamw(w[k], gs, mom[k], var[k], f"adamw_{k}")

    last_keys = groups[last]["keys"]
    for i in range(len(_SHARDED)):
        if _LAYER_KEYS[i] not in last_keys:
            adamw_sharded(i)
    halves = chips_finish(last, new_v[_SHARDED[-1]])
    halves.update(chips_finish("small", new_v[_SHARDED[-1]]))
    reduced.update(gather_halves(halves, "gather_pair_last"))
    for i in range(len(_SHARDED)):
        if _LAYER_KEYS[i] in last_keys:
            adamw_sharded(i)

    g_small = _unpack(reduced[(None, "small")].reshape(-1), small_like)
    grads.update(zip(_SMALL, g_small[:-1]))
    grads["ffn_conv_w"] = lax.dynamic_slice_in_dim(g_small[-1].reshape(depth, 3, 3, dff), chip * (dff // N_CHIPS),
                                                   dff // N_CHIPS, axis=3)
    packed = _SMALL + ("ffn_conv_w",)
    n_pad = _round_up(sum(w[k].size for k in packed), 16 * PACK_COLS)
    pack = lambda t: _pack([t[k] for k in packed], n_pad).reshape(1, -1, PACK_COLS)
    _, d, nm, nv = _adamw(pack(w), [pack(grads)[0]], pack(mom), pack(var), "adamw_packed")
    like = [w[k] for k in packed]
    for src, dst in ((d, delta), (nm, new_m), (nv, new_v)):
        dst.update(zip(packed, _unpack(src.reshape(-1), like)))

    return (loss, grad_x, *[grads[k] for k in _ORDER], *[delta[k] for k in _ORDER], *[new_m[k] for k in _ORDER],
            *[new_v[k] for k in _ORDER])
```

```python
import functools

import jax
import jax.numpy as jnp
from jax import lax
from jax.experimental import pallas as pl
from jax.experimental.pallas import tpu as pltpu

F32 = jnp.float32
BF16 = jnp.bfloat16

GRID_W = 64
HG_CHUNK = 64
SGU_CHUNK = 128
HEAD = 128
TB = 256
N_MOD = 6
RMS_EPS = 1e-6
LN_EPS = 1e-5
VMEM_LIMIT = 48 * 1024 * 1024
VMEM_LIMIT_PAIR = 58 * 1024 * 1024
N_CHIPS = 4
N_CORES = 2

ADAM_LR = 0.001
ADAM_B1 = 0.9
ADAM_B2 = 0.999
ADAM_EPS = 1e-08
ADAM_WD = 0.01
ADAM_STEP = 10

_GELU_C = 0.7978845608028654
_GELU_A = 0.044715


def _sigmoid(x):
    return 0.5 * jnp.tanh(0.5 * x) + 0.5


def _silu(x):
    return x * _sigmoid(x)


def _silu_both(x):
    s = _sigmoid(x)
    return x * s, s * (1.0 + x * (1.0 - s))


def _dsilu(x):
    return _silu_both(x)[1]


def _gelu_both(x):
    x2 = x * x
    t = jnp.tanh(_GELU_C * (x + _GELU_A * x2 * x))
    h = 0.5 * (1.0 + t)
    return x * h, h + 0.5 * x * (1.0 - t * t) * (_GELU_C + 3.0 * _GELU_C * _GELU_A * x2)


def _gelu(x):
    return 0.5 * x * (1.0 + jnp.tanh(_GELU_C * (x + _GELU_A * x * x * x)))


def _dot(a, b, ca, cb):
    return lax.dot_general(a, b, (((ca,), (cb,)), ((), ())), preferred_element_type=F32)


def _nn(a, b):
    return _dot(a, b, 1, 0)


def _nt(a, b):
    return _dot(a, b, 1, 1)


def _tn(a, b):
    return _dot(a, b, 0, 0)


def _params(*sem, vmem=VMEM_LIMIT):
    return pltpu.CompilerParams(dimension_semantics=sem if sem else None, vmem_limit_bytes=vmem)


def _stream_of(i, ctx_blocks):
    return (i >= ctx_blocks).astype(jnp.int32)


def _mm(a, b, mode, tm, tn, tk, out_dtype, name, b_chips=False, out_chips=False, vmem=VMEM_LIMIT):
    a_pair, b_pair = isinstance(a, tuple), isinstance(b, tuple)
    assert (not a_pair or mode == "nt") and (not b_pair or (mode == "tn" and not b_chips))
    ashape = (a[0].shape[0], 2 * a[0].shape[1]) if a_pair else a.shape
    if b_pair:
        bshape = (b[0].shape[0], 2 * b[0].shape[1])
    elif not b_chips:
        bshape = b.shape
    else:
        bshape = (b.shape[1], N_CHIPS * b.shape[2])
    if mode == "nn":
        (M, K), (K2, N) = ashape, bshape
    elif mode == "nt":
        (M, K), (N, K2) = ashape, bshape
    else:
        (K, M), (K2, N) = ashape, bshape
    assert K == K2 and M % tm == 0 and N % tn == 0 and K % tk == 0, (name, ashape, bshape, tm, tn, tk)
    nk = K // tk
    if a_pair:
        n1 = a[0].shape[1] // tk
        assert a[0].shape[1] % tk == 0
        a_specs = [pl.BlockSpec((tm, tk), lambda j, i, k: (i, jnp.minimum(k, n1 - 1))),
                   pl.BlockSpec((tm, tk), lambda j, i, k: (i, jnp.maximum(k - n1, 0)))]
    elif mode == "tn":
        a_specs = [pl.BlockSpec((tk, tm), lambda j, i, k: (k, i))]
    else:
        a_specs = [pl.BlockSpec((tm, tk), lambda j, i, k: (i, k))]
    if b_pair:
        n1 = b[0].shape[1] // tn
        assert b[0].shape[1] % tn == 0
        b_specs = [pl.BlockSpec((tk, tn), lambda j, i, k: (k, jnp.minimum(j, n1 - 1))),
                   pl.BlockSpec((tk, tn), lambda j, i, k: (k, jnp.maximum(j - n1, 0)))]
    elif not b_chips:
        if mode == "nt":
            b_spec = pl.BlockSpec((tn, tk), lambda j, i, k: (j, k))
        else:
            b_spec = pl.BlockSpec((tk, tn), lambda j, i, k: (k, j))
    else:
        cols = b.shape[2]
        if mode == "nn":
            per = cols // tn
            assert cols % tn == 0
            b_spec = pl.BlockSpec((None, tk, tn), lambda j, i, k: (j // per, k, j % per))
        else:
            per = cols // tk
            assert mode == "nt" and cols % tk == 0
            b_spec = pl.BlockSpec((None, tn, tk), lambda j, i, k: (k // per, j, k % per))
    if not b_pair:
        b_specs = [b_spec]
    if out_chips:
        per_o = (N // N_CHIPS) // tn
        assert (N // N_CHIPS) % tn == 0
        o_spec = pl.BlockSpec((None, tm, tn), lambda j, i, k: (j // per_o, i, j % per_o))
        o_shape = (N_CHIPS, M, N // N_CHIPS)
    else:
        o_spec = pl.BlockSpec((tm, tn), lambda j, i, k: (i, j))
        o_shape = (M, N)
    ca, cb = {"nn": (1, 0), "nt": (1, 1), "tn": (0, 0)}[mode]

    in_place = nk == 1
    na, nb = len(a_specs), len(b_specs)

    def body(*refs):
        a_refs, b_refs, rest = refs[:na], refs[na:na + nb], refs[na + nb:]
        if in_place:
            (o_ref,) = rest
        else:
            o_ref, acc = rest
        k = pl.program_id(2)

        if in_place:
            o_ref[...] = _dot(a_refs[0][...], b_refs[0][...], ca, cb).astype(out_dtype)
            return

        @pl.when(k == 0)
        def _():
            acc[...] = jnp.zeros_like(acc)

        if a_pair or b_pair:
            first = (k < n1) if a_pair else (pl.program_id(0) < n1)
            for which, cond in ((0, first), (1, jnp.logical_not(first))):
                @pl.when(cond)
                def _():
                    acc[...] += _dot(a_refs[which if a_pair else 0][...], b_refs[which if b_pair else 0][...], ca, cb)
        else:
            acc[...] += _dot(a_refs[0][...], b_refs[0][...], ca, cb)

        @pl.when(k == nk - 1)
        def _():
            o_ref[...] = acc[...].astype(out_dtype)

    ins = (list(a) if a_pair else [a]) + (list(b) if b_pair else [b])
    return pl.pallas_call(
        body, name=name, grid=(N // tn, M // tm, nk), in_specs=a_specs + b_specs, out_specs=o_spec,
        out_shape=jax.ShapeDtypeStruct(o_shape, out_dtype),
        scratch_shapes=[] if in_place else [pltpu.VMEM((tm, tn), F32)],
        compiler_params=_params("parallel", "parallel", "arbitrary", vmem=vmem),
    )(*ins)


def _tile(n, pref):
    if n <= pref:
        return n
    best = None
    for t in range(128, pref + 1, 128):
        if n % t == 0:
            best = t
    assert best is not None, (n, pref)
    return best


def _rows_tile(n, pref):
    if n <= pref:
        return n
    best = None
    for t in range(16, pref + 1, 16):
        if n % t == 0:
            best = t
    assert best is not None, (n, pref)
    return best


def _mm_nn_w(a, wg, out_dtype, name):
    M, K = a.shape
    return _mm(a, wg, "nn", _rows_tile(M, 2176), _tile(wg.shape[2], 1536), _tile(K, 1536), out_dtype, name, b_chips=True)


def _mm_nt_w(a, wg, out_dtype, name):
    M = a[0].shape[0] if isinstance(a, tuple) else a.shape[0]
    return _mm(a, wg, "nt", _rows_tile(M, 1088), _tile(wg.shape[1], 1024), _tile(wg.shape[2], 1536), out_dtype, name,
               b_chips=True)


def _mm_tn(a, b, out_dtype, name, out_chips=False):
    K, M = a.shape
    N = 2 * b[0].shape[1] if isinstance(b, tuple) else b.shape[1]
    ncol = N // N_CHIPS if out_chips else N
    tm, tn = _tile(M, 1408), _tile(ncol, 1408)
    if tm * tn > 1408 * 1152:
        tn = _tile(ncol, 1152)
    vmem = VMEM_LIMIT_PAIR if isinstance(b, tuple) else VMEM_LIMIT
    return _mm(a, b, "tn", tm, tn, _rows_tile(K, 2176), out_dtype, name, out_chips=out_chips, vmem=vmem)


def _mod_fwd(cv, wg, b, name):
    R, D = cv.shape
    tn = wg.shape[2]
    N = N_CHIPS * tn

    def body(cv_ref, w_ref, b_ref, mod_ref, sa_ref):
        sa = _silu(cv_ref[...]).astype(BF16)
        sa_ref[...] = sa
        mod_ref[...] = _nn(sa, w_ref[...]) + b_ref[...]

    return pl.pallas_call(
        body, name=name, grid=(N_CHIPS,),
        in_specs=[pl.BlockSpec((R, D), lambda j: (0, 0)), pl.BlockSpec((None, D, tn), lambda j: (j, 0, 0)),
                  pl.BlockSpec((1, tn), lambda j: (0, j))],
        out_specs=[pl.BlockSpec((R, tn), lambda j: (0, j)), pl.BlockSpec((R, D), lambda j: (0, 0))],
        out_shape=[jax.ShapeDtypeStruct((R, N), F32), jax.ShapeDtypeStruct((R, D), BF16)],
        compiler_params=_params("arbitrary"),
    )(cv, wg, b)


def _cvec_bwd(dmod, wg, cv, name):
    R, N = dmod.shape
    D = wg.shape[1]
    tk = wg.shape[2]
    nk = N_CHIPS

    def body(dm_ref, w_ref, cv_ref, o_ref):
        k = pl.program_id(0)

        @pl.when(k == 0)
        def _():
            o_ref[...] = jnp.zeros_like(o_ref)

        o_ref[...] += _nt(dm_ref[...].astype(BF16), w_ref[...])

        @pl.when(k == nk - 1)
        def _():
            o_ref[...] = o_ref[...] * _dsilu(cv_ref[...])

    return pl.pallas_call(
        body, name=name, grid=(nk,),
        in_specs=[pl.BlockSpec((R, tk), lambda k: (0, k)), pl.BlockSpec((None, D, tk), lambda k: (k, 0, 0)),
                  pl.BlockSpec((R, D), lambda k: (0, 0))],
        out_specs=pl.BlockSpec((R, D), lambda k: (0, 0)),
        out_shape=jax.ShapeDtypeStruct((R, D), F32),
        compiler_params=_params("arbitrary"),
    )(dmod, wg, cv)


def _norm_mod(x, nw, mod, which, ctx_rows, name):
    T, D = x.shape
    cb = ctx_rows // TB

    def body(x_ref, nw_ref, mod_ref, h_ref):
        xv = x_ref[...]
        r = lax.rsqrt(jnp.mean(xv * xv, axis=-1, keepdims=True) + RMS_EPS)
        y = xv * r * nw_ref[...]
        sh = mod_ref[which:which + 1, :]
        sc = mod_ref[which + 1:which + 2, :]
        h_ref[...] = (y * (1.0 + sc) + sh).astype(BF16)

    return pl.pallas_call(
        body, name=name, grid=(T // TB,),
        in_specs=[pl.BlockSpec((TB, D), lambda i: (i, 0)), pl.BlockSpec((1, D), lambda i: (0, 0)),
                  pl.BlockSpec((None, N_MOD, D), lambda i: (_stream_of(i, cb), 0, 0))],
        out_specs=pl.BlockSpec((TB, D), lambda i: (i, 0)),
        out_shape=jax.ShapeDtypeStruct((T, D), BF16),
        compiler_params=_params("parallel"),
    )(x, nw, mod)


def _norm_mod_bwd(dh, x, dres, nw, mod, which, ctx_rows, name):
    T, D = x.shape
    cb = ctx_rows // TB

    def body(dh_ref, x_ref, dres_ref, nw_ref, mod_ref, dx_ref, dm_ref, dnw_ref):
        i = pl.program_id(0)

        @pl.when(i == 0)
        def _():
            dnw_ref[...] = jnp.zeros_like(dnw_ref)

        @pl.when((i == 0) | (i == cb))
        def _():
            dm_ref[...] = jnp.zeros_like(dm_ref)

        xv = x_ref[...]
        dh = dh_ref[...]
        r = lax.rsqrt(jnp.mean(xv * xv, axis=-1, keepdims=True) + RMS_EPS)
        xh = xv * r
        nwv = nw_ref[...]
        sc = mod_ref[which + 1:which + 2, :]
        y = xh * nwv
        dm_ref[0:1, :] += jnp.sum(dh, axis=0, keepdims=True)
        dm_ref[1:2, :] += jnp.sum(dh * y, axis=0, keepdims=True)
        dy = dh * (1.0 + sc)
        dnw_ref[...] += jnp.sum(dy * xh, axis=0, keepdims=True)
        dxh = dy * nwv
        dx_ref[...] = dres_ref[...] + r * (dxh - xh * jnp.mean(dxh * xh, axis=-1, keepdims=True))

    return pl.pallas_call(
        body, name=name, grid=(T // TB,),
        in_specs=[pl.BlockSpec((TB, D), lambda i: (i, 0)), pl.BlockSpec((TB, D), lambda i: (i, 0)),
                  pl.BlockSpec((TB, D), lambda i: (i, 0)), pl.BlockSpec((1, D), lambda i: (0, 0)),
                  pl.BlockSpec((None, N_MOD, D), lambda i: (_stream_of(i, cb), 0, 0))],
        out_specs=[pl.BlockSpec((TB, D), lambda i: (i, 0)),
                   pl.BlockSpec((None, 2, D), lambda i: (_stream_of(i, cb), 0, 0)),
                   pl.BlockSpec((1, D), lambda i: (0, 0))],
        out_shape=[jax.ShapeDtypeStruct((T, D), F32), jax.ShapeDtypeStruct((2, 2, D), F32),
                   jax.ShapeDtypeStruct((1, D), F32)],
        compiler_params=_params("arbitrary"),
    )(dh, x, dres, nw, mod)


def _scan_chunk(n, rev, n_ctx, n_all):
    if not rev:
        return n
    return jnp.where(n < n_ctx, n_ctx - 1 - n, n_all - 1 + n_ctx - n)


def _cumsum_rows(x, rev):
    rows = x.shape[0]
    row = lax.broadcasted_iota(jnp.int32, (rows, 1), 0)
    s = 1
    while s < rows:
        if not rev:
            x = x + jnp.where(row >= s, pltpu.roll(x, s, 0), 0.0)
        else:
            x = x + jnp.where(row < rows - s, pltpu.roll(x, rows - s, 0), 0.0)
        s *= 2
    return x


def _lower_bound(hlb_ref, layer):
    h = hlb_ref[...]
    if layer == 0:
        return jnp.zeros_like(h[0:1, :])
    return _sigmoid(h[1:2, :] - h[0:1, :])


HG_STEP = 4


def _step_rows(j, rev, backward):
    sub = j if rev == backward else HG_STEP - 1 - j
    return slice(sub * HG_CHUNK, (sub + 1) * HG_CHUNK)


def _hgrn_gates(q_ref, f_ref, hlb_ref, layer, rev, rows):
    lb = _lower_bound(hlb_ref, layer)
    z = f_ref[rows, :]
    sig = 1.0 / (1.0 + jnp.exp(-z))
    fg = lb + (1.0 - lb) * sig
    kk = (1.0 - lb) * (1.0 - sig)
    g = jnp.log(fg)
    b = _cumsum_rows(g, rev)
    bt = jnp.sum(g, axis=0, keepdims=True)
    mid = HG_CHUNK // 2
    r = b[mid:mid + 1, :] if rev else b[mid - 1:mid, :]
    qh = _silu(q_ref[rows, :])
    return lb, sig, g, kk, b, bt, r, qh


def _tri_mask(rev):
    t = lax.broadcasted_iota(jnp.int32, (HG_CHUNK, HG_CHUNK), 0)
    s = lax.broadcasted_iota(jnp.int32, (HG_CHUNK, HG_CHUNK), 1)
    return (s >= t) if rev else (s <= t)


def _hgrn_fwd(parts, hlb, layer, rev, ctx_rows, name, o_add=None):
    T = parts.shape[0]
    D = hlb.shape[1] // 2
    nh = D // HEAD
    n_all, n_ctx = T // HG_CHUNK, ctx_rows // HG_CHUNK
    assert n_all % HG_STEP == 0 and n_ctx % HG_STEP == 0
    n_steps = n_all // HG_STEP
    block = functools.partial(_scan_chunk, rev=rev, n_ctx=n_ctx // HG_STEP, n_all=n_steps)
    fcol = 2 if rev else 1

    def body(q_ref, f_ref, i_ref, hlb_ref, *rest):
        if o_add is None:
            o_ref, st_ref, s_scr = rest
        else:
            oa_ref, o_ref, st_ref, s_scr = rest
        n = pl.program_id(0)

        @pl.when(n == 0)
        def _():
            s_scr[...] = jnp.zeros_like(s_scr)

        mask = _tri_mask(rev)
        hs = [slice(h * HEAD, (h + 1) * HEAD) for h in range(nh)]
        for j in range(HG_STEP):
            rows = _step_rows(j, rev, False)
            lb, sig, g, kk, b, bt, r, qh = _hgrn_gates(q_ref, f_ref, hlb_ref, layer, rev, rows)
            qr = (qh * jnp.exp(b - r)).astype(BF16)
            kr = (kk * jnp.exp(r - b)).astype(BF16)
            qe = (qh * jnp.exp(b)).astype(BF16)
            ke = (kk * jnp.exp(bt - b)).astype(BF16)
            dec = jnp.exp(bt)
            v = i_ref[rows, :].astype(BF16)
            st = [s_scr[h] for h in range(nh)]
            a_raw = [_nt(qr[:, sl], kr[:, sl]) for sl in hs]
            o_int = [_nt(qe[:, sl], st[h].astype(BF16)) for h, sl in enumerate(hs)]
            kv = [_tn(v[:, sl], ke[:, sl]) for sl in hs]
            for h, sl in enumerate(hs):
                st_ref[j, h] = st[h]
                o = _nn(jnp.where(mask, a_raw[h], 0.0).astype(BF16), v[:, sl]) + o_int[h]
                if o_add is not None:
                    o = o + oa_ref[rows, sl]
                o_ref[rows, sl] = o
                s_scr[h] = st[h] * dec[:, sl] + kv[h]

    cspec = lambda col: pl.BlockSpec((HG_STEP * HG_CHUNK, D), lambda n: (block(n), col))
    ins = [parts, parts, parts, hlb]
    specs = [cspec(0), cspec(fcol), cspec(3), pl.BlockSpec((2, D), lambda n: (0, 1 if rev else 0))]
    if o_add is not None:
        ins.append(o_add)
        specs.append(cspec(0))
    return pl.pallas_call(
        body, name=name, grid=(n_steps,), in_specs=specs,
        out_specs=[cspec(0), pl.BlockSpec((HG_STEP, nh, HEAD, HEAD), lambda n: (n, 0, 0, 0))],
        out_shape=[jax.ShapeDtypeStruct((T, D), F32), jax.ShapeDtypeStruct((n_all, nh, HEAD, HEAD), F32)],
        scratch_shapes=[pltpu.VMEM((nh, HEAD, HEAD), F32)],
        compiler_params=_params("arbitrary"),
    )(*ins)


def _hgrn_bwd(parts, hlb, do, states, layer, rev, ctx_rows, name, other=None, dparts=None):
    T = parts.shape[0]
    D = hlb.shape[1] // 2
    nh = D // HEAD
    n_all, n_ctx = T // HG_CHUNK, ctx_rows // HG_CHUNK
    assert n_all % HG_STEP == 0 and n_ctx % HG_STEP == 0
    n_steps = n_all // HG_STEP
    step = lambda m: n_steps - 1 - m
    block = lambda m: _scan_chunk(step(m), rev, n_ctx // HG_STEP, n_steps)
    fcol = 2 if rev else 1
    has_add = other is not None
    assert not has_add or rev

    def body(q_ref, f_ref, i_ref, hlb_ref, do_ref, st_ref, *rest):
        if has_add:
            dqa_ref, dza_ref, dia_ref, _, out_ref, dlb_ref, ds_scr = rest
            dq_ref, dz_ref, di_ref = out_ref.at[:, 0:D], out_ref.at[:, 2 * D:3 * D], out_ref.at[:, 3 * D:4 * D]
            out_ref[:, D:2 * D] = dza_ref[...]
        else:
            dq_ref, dz_ref, di_ref, dlb_ref, ds_scr = rest
        m = pl.program_id(0)

        @pl.when(m == 0)
        def _():
            ds_scr[...] = jnp.zeros_like(ds_scr)
            dlb_ref[...] = jnp.zeros_like(dlb_ref)

        mask = _tri_mask(rev)
        hs = [slice(h * HEAD, (h + 1) * HEAD) for h in range(nh)]
        for j in range(HG_STEP):
            rows = _step_rows(j, rev, True)
            slot = HG_STEP - 1 - j
            lb, sig, g, kk, b, bt, r, qh = _hgrn_gates(q_ref, f_ref, hlb_ref, layer, rev, rows)
            e_qr = jnp.exp(b - r)
            e_kr = jnp.exp(r - b)
            e_b = jnp.exp(b)
            e_ke = jnp.exp(bt - b)
            dec = jnp.exp(bt)
            qr = (qh * e_qr).astype(BF16)
            kr = (kk * e_kr).astype(BF16)
            qe = (qh * e_b).astype(BF16)
            ke = (kk * e_ke).astype(BF16)
            v = i_ref[rows, :].astype(BF16)
            dov = do_ref[rows, :].astype(BF16)
            st = [st_ref[slot, h] for h in range(nh)]
            dst = [ds_scr[h] for h in range(nh)]
            stb = [t.astype(BF16) for t in st]
            dstb = [t.astype(BF16) for t in dst]
            a_raw = [_nt(qr[:, sl], kr[:, sl]) for sl in hs]
            da_raw = [_nt(dov[:, sl], v[:, sl]) for sl in hs]
            dq_int = [_nn(dov[:, sl], stb[h]) for h, sl in enumerate(hs)]
            dk_int = [_nn(v[:, sl], dstb[h]) for h, sl in enumerate(hs)]
            dv_int = [_nt(ke[:, sl], dstb[h]) for h, sl in enumerate(hs)]
            ds_new = [_tn(dov[:, sl], qe[:, sl]) for sl in hs]
            a = [jnp.where(mask, t, 0.0).astype(BF16) for t in a_raw]
            da = [jnp.where(mask, t, 0.0).astype(BF16) for t in da_raw]
            dv_parts = [_tn(a[h], dov[:, sl]) + dv_int[h] for h, sl in enumerate(hs)]
            dq_parts = [_nn(da[h], kr[:, sl]) * e_qr[:, sl] + dq_int[h] * e_b[:, sl] for h, sl in enumerate(hs)]
            dki_parts = [dk_int[h] * e_ke[:, sl] for h, sl in enumerate(hs)]
            dk_parts = [_tn(da[h], qr[:, sl]) * e_kr[:, sl] + dki_parts[h] for h, sl in enumerate(hs)]
            dbt_parts = [dec[:, sl] * jnp.sum(st[h] * dst[h], axis=0, keepdims=True) for h, sl in enumerate(hs)]
            for h, sl in enumerate(hs):
                ds_scr[h] = dst[h] * dec[:, sl] + ds_new[h]
            dq = jnp.concatenate(dq_parts, axis=1)
            dk = jnp.concatenate(dk_parts, axis=1)
            dki = jnp.concatenate(dki_parts, axis=1)
            dv = jnp.concatenate(dv_parts, axis=1)
            dbt = jnp.concatenate(dbt_parts, axis=1) + jnp.sum(kk * dki, axis=0, keepdims=True)
            db = qh * dq - kk * dk
            dg = _cumsum_rows(db, not rev) + dbt
            df = dg * jnp.exp(-g) - dk
            dz_ref[rows, :] = (df * (1.0 - lb) * sig * (1.0 - sig)).astype(BF16)
            dlb_ref[...] += jnp.sum(df * (1.0 - sig), axis=0, keepdims=True)
            dqr = dq * _dsilu(q_ref[rows, :])
            if has_add:
                dqr = dqr + dqa_ref[rows, :]
                dv = dv + dia_ref[rows, :]
            dq_ref[rows, :] = dqr.astype(dq_ref.dtype)
            di_ref[rows, :] = dv.astype(di_ref.dtype)

        @pl.when(m == n_steps - 1)
        def _():
            lb = _lower_bound(hlb_ref, layer)
            if layer == 0:
                dlb_ref[...] = jnp.zeros_like(dlb_ref)
            else:
                dlb_ref[...] = dlb_ref[...] * lb * (1.0 - lb)

    cspec = lambda col: pl.BlockSpec((HG_STEP * HG_CHUNK, D), lambda m: (block(m), col))
    ins = [parts, parts, parts, hlb, do, states]
    specs = [cspec(0), cspec(fcol), cspec(3), pl.BlockSpec((2, D), lambda m: (0, 1 if rev else 0)), cspec(0),
             pl.BlockSpec((HG_STEP, nh, HEAD, HEAD), lambda m: (step(m), 0, 0, 0))]
    dlb_spec = pl.BlockSpec((1, D), lambda m: (0, 0))
    dlb_shape = jax.ShapeDtypeStruct((1, D), F32)
    if has_add:
        return pl.pallas_call(
            body, name=name, grid=(n_steps,),
            in_specs=specs + [cspec(0), cspec(0), cspec(0), pl.BlockSpec(memory_space=pl.ANY)],
            out_specs=[pl.BlockSpec((HG_STEP * HG_CHUNK, 4 * D), lambda m: (block(m), 0)), dlb_spec],
            out_shape=[jax.ShapeDtypeStruct(dparts.shape, dparts.dtype), dlb_shape],
            scratch_shapes=[pltpu.VMEM((nh, HEAD, HEAD), F32)], input_output_aliases={len(ins) + 3: 0},
            compiler_params=_params("arbitrary"),
        )(*ins, *other, dparts)
    return pl.pallas_call(
        body, name=name, grid=(n_steps,), in_specs=specs,
        out_specs=[cspec(0), cspec(0), cspec(0), dlb_spec],
        out_shape=[jax.ShapeDtypeStruct((T, D), F32), jax.ShapeDtypeStruct((T, D), BF16),
                   jax.ShapeDtypeStruct((T, D), F32), dlb_shape],
        scratch_shapes=[pltpu.VMEM((nh, HEAD, HEAD), F32)],
        compiler_params=_params("arbitrary"),
    )(*ins)


def _sgu_ln(gv, lnw_ref, lnb_ref):
    mu = jnp.mean(gv, axis=-1, keepdims=True)
    xc = gv - mu
    rstd = lax.rsqrt(jnp.mean(xc * xc, axis=-1, keepdims=True) + LN_EPS)
    xh = xc * rstd
    return xh, rstd, xh * lnw_ref[...] + lnb_ref[...]


def _sgu_fwd(parts, lnw, lnb, w, bt, name):
    T = parts.shape[0]
    D = lnw.shape[1]
    G = D // HEAD

    def body(u_ref, v_ref, lnw_ref, lnb_ref, w_ref, bt_ref, ya_ref):
        gu = _gelu(u_ref[...])
        _, _, vn = _sgu_ln(_gelu(v_ref[...]), lnw_ref, lnb_ref)
        vnb = vn.astype(BF16)
        for g in range(G):
            sl = slice(g * HEAD, (g + 1) * HEAD)
            mixed = _nn(w_ref[g], vnb[:, sl]) + bt_ref[:, g:g + 1]
            ya_ref[:, sl] = (gu[:, sl] * mixed).astype(BF16)

    return pl.pallas_call(
        body, name=name, grid=(T // SGU_CHUNK,),
        in_specs=[pl.BlockSpec((SGU_CHUNK, D), lambda n: (n, 4)), pl.BlockSpec((SGU_CHUNK, D), lambda n: (n, 5)),
                  pl.BlockSpec((1, D), lambda n: (0, 0)), pl.BlockSpec((1, D), lambda n: (0, 0)),
                  pl.BlockSpec((G, SGU_CHUNK, SGU_CHUNK), lambda n: (0, 0, 0)),
                  pl.BlockSpec((SGU_CHUNK, G), lambda n: (0, 0))],
        out_specs=pl.BlockSpec((SGU_CHUNK, D), lambda n: (n, 0)),
        out_shape=jax.ShapeDtypeStruct((T, D), BF16),
        compiler_params=_params("parallel"),
    )(parts, parts, lnw, lnb, w, bt)


def _sgu_bwd(parts, dya, lnw, lnb, w, bt, dparts, name):
    T = parts.shape[0]
    D = lnw.shape[1]
    G = D // HEAD

    def body(u_ref, v_ref, dya_ref, lnw_ref, lnb_ref, w_ref, bt_ref, dparts_in,
             duv_ref, dw_ref, dbt_ref, dlnw_ref, dlnb_ref, dvn_scr):
        du_ref = duv_ref.at[:, 0:D]
        dv_ref = duv_ref.at[:, D:2 * D]
        n = pl.program_id(0)

        @pl.when(n == 0)
        def _():
            dw_ref[...] = jnp.zeros_like(dw_ref)
            dbt_ref[...] = jnp.zeros_like(dbt_ref)
            dlnw_ref[...] = jnp.zeros_like(dlnw_ref)
            dlnb_ref[...] = jnp.zeros_like(dlnb_ref)

        gu, dgu = _gelu_both(u_ref[...])
        gv, dgv_dv = _gelu_both(v_ref[...])
        xh, rstd, vn = _sgu_ln(gv, lnw_ref, lnb_ref)
        vnb = vn.astype(BF16)
        dya = dya_ref[...]
        lane = lax.broadcasted_iota(jnp.int32, (SGU_CHUNK, G), 1)
        dbt = jnp.zeros((SGU_CHUNK, G), F32)
        for g in range(G):
            sl = slice(g * HEAD, (g + 1) * HEAD)
            wg = w_ref[g]
            mixed = _nn(wg, vnb[:, sl]) + bt_ref[:, g:g + 1]
            dmix = dya[:, sl] * gu[:, sl]
            du_ref[:, sl] = (dya[:, sl] * mixed * dgu[:, sl]).astype(BF16)
            dmb = dmix.astype(BF16)
            dvn_scr[:, sl] = _tn(wg, dmb)
            dw_ref[g] += _nt(dmb, vnb[:, sl])
            dbt = dbt + jnp.where(lane == g, jnp.sum(dmix, axis=1, keepdims=True), 0.0)
        dbt_ref[...] += dbt
        dvn = dvn_scr[...]
        dlnw_ref[...] += jnp.sum(dvn * xh, axis=0, keepdims=True)
        dlnb_ref[...] += jnp.sum(dvn, axis=0, keepdims=True)
        dxh = dvn * lnw_ref[...]
        dgv = rstd * (dxh - jnp.mean(dxh, axis=-1, keepdims=True) - xh * jnp.mean(dxh * xh, axis=-1, keepdims=True))
        dv_ref[...] = (dgv * dgv_dv).astype(BF16)

    row = lambda col: pl.BlockSpec((SGU_CHUNK, D), lambda n: (n, col))
    vec = pl.BlockSpec((1, D), lambda n: (0, 0))
    wsp = pl.BlockSpec((G, SGU_CHUNK, SGU_CHUNK), lambda n: (0, 0, 0))
    bsp = pl.BlockSpec((SGU_CHUNK, G), lambda n: (0, 0))
    return pl.pallas_call(
        body, name=name, grid=(T // SGU_CHUNK,),
        in_specs=[row(4), row(5), row(0), vec, vec, wsp, bsp, pl.BlockSpec(memory_space=pl.ANY)],
        out_specs=[pl.BlockSpec((SGU_CHUNK, 2 * D), lambda n: (n, 2)), wsp, bsp, vec, vec],
        out_shape=[jax.ShapeDtypeStruct(dparts.shape, dparts.dtype),
                   jax.ShapeDtypeStruct((G, SGU_CHUNK, SGU_CHUNK), F32), jax.ShapeDtypeStruct((SGU_CHUNK, G), F32),
                   jax.ShapeDtypeStruct((1, D), F32), jax.ShapeDtypeStruct((1, D), F32)],
        scratch_shapes=[pltpu.VMEM((SGU_CHUNK, D), F32)], input_output_aliases={7: 0},
        compiler_params=_params("arbitrary"),
    )(parts, parts, dya, lnw, lnb, w, bt, dparts)


TBT = 256
VMEM_LIMIT_TOKEN_OUT = 58 * 1024 * 1024


def _rows_weight_spec(wg):
    return pl.BlockSpec(wg.shape, lambda i: (0, 0, 0))


def _full(w_ref):
    return w_ref[...].reshape(w_ref.shape[0] * w_ref.shape[1], w_ref.shape[2])


def _token_out_fwd(o, parts, ya, x, mod, hnw, nw2, wa, wb, wo, ctx_rows, name):
    T, D = x.shape
    nh = D // HEAD
    cb = ctx_rows // TBT

    def body(o_ref, og_ref, ga_ref, gb_ref, ya_ref, x_ref, mod_ref, hnw_ref, nw2_ref, wa_ref, wb_ref, wo_ref,
             yb_ref, pa_ref, pb_ref, mg_ref, tmo_ref, xm_ref, h2_ref):
        ov = o_ref[...]
        so = _silu(og_ref[...])
        nw = hnw_ref[...]
        for h in range(nh):
            sl = slice(h * HEAD, (h + 1) * HEAD)
            seg = ov[:, sl]
            r = lax.rsqrt(jnp.mean(seg * seg, axis=-1, keepdims=True) + RMS_EPS)
            yb_ref[:, sl] = (seg * r * nw * so[:, sl]).astype(BF16)
        pa = _nn(ya_ref[...], _full(wa_ref))
        pb = _nn(yb_ref[...], _full(wb_ref))
        pa_ref[...] = pa
        pb_ref[...] = pb
        mg = (_sigmoid(ga_ref[...]) * pa + _sigmoid(gb_ref[...]) * pb).astype(BF16)
        mg_ref[...] = mg
        out = _nn(mg, _full(wo_ref))
        tmo_ref[...] = out
        xm = x_ref[...] + mod_ref[2:3, :] * out
        xm_ref[...] = xm
        r = lax.rsqrt(jnp.mean(xm * xm, axis=-1, keepdims=True) + RMS_EPS)
        h2_ref[...] = (xm * r * nw2_ref[...] * (1.0 + mod_ref[4:5, :]) + mod_ref[3:4, :]).astype(BF16)

    row = lambda col: pl.BlockSpec((TBT, D), lambda i: (i, col))
    wsp = _rows_weight_spec(wa)
    sd = lambda dt: jax.ShapeDtypeStruct((T, D), dt)
    return pl.pallas_call(
        body, name=name, grid=(T // TBT,),
        in_specs=[row(0), row(6), row(7), row(8), row(0), row(0),
                  pl.BlockSpec((None, N_MOD, D), lambda i: (_stream_of(i, cb), 0, 0)),
                  pl.BlockSpec((1, HEAD), lambda i: (0, 0)), pl.BlockSpec((1, D), lambda i: (0, 0)), wsp, wsp, wsp],
        out_specs=[row(0)] * 7,
        out_shape=[sd(BF16), sd(F32), sd(F32), sd(BF16), sd(F32), sd(F32), sd(BF16)],
        compiler_params=_params("parallel", vmem=VMEM_LIMIT_TOKEN_OUT),
    )(o, parts, parts, parts, ya, x, mod, hnw, nw2, wa, wb, wo)


def _token_out_bwd(dx, tmo, pa, pb, o, parts, mod, hnw, wa, wb, wo, ctx_rows, name):
    T, D = dx.shape
    nh = D // HEAD
    cb = ctx_rows // TBT

    def body(dx_ref, tmo_ref, pa_ref, pb_ref, o_ref, og_ref, ga_ref, gb_ref, mod_ref, hnw_ref, wa_ref, wb_ref, wo_ref,
             dout_ref, dpa_ref, dpb_ref, dgate_ref, dya_ref, do_ref, dg1_ref, dhnw_ref):
        i = pl.program_id(0)

        @pl.when(i == 0)
        def _():
            dhnw_ref[...] = jnp.zeros_like(dhnw_ref)

        @pl.when((i == 0) | (i == cb))
        def _():
            dg1_ref[...] = jnp.zeros_like(dg1_ref)

        dxv = dx_ref[...]
        dg1_ref[...] += jnp.sum(dxv * tmo_ref[...], axis=0, keepdims=True)
        dout = (dxv * mod_ref[2:3, :]).astype(BF16)
        dout_ref[...] = dout
        dmg = _nt(dout, _full(wo_ref))
        sa = _sigmoid(ga_ref[...])
        sb = _sigmoid(gb_ref[...])
        dpa = (dmg * sa).astype(BF16)
        dpb = (dmg * sb).astype(BF16)
        dpa_ref[...] = dpa
        dpb_ref[...] = dpb
        dgate_ref[:, D:2 * D] = (dmg * pa_ref[...] * sa * (1.0 - sa)).astype(BF16)
        dgate_ref[:, 2 * D:3 * D] = (dmg * pb_ref[...] * sb * (1.0 - sb)).astype(BF16)
        dya_ref[...] = _nt(dpa, _full(wa_ref))
        dyb = _nt(dpb, _full(wb_ref))
        so, dso = _silu_both(og_ref[...])
        ov = o_ref[...]
        nw = hnw_ref[...]
        dnw = jnp.zeros((1, HEAD), F32)
        for h in range(nh):
            sl = slice(h * HEAD, (h + 1) * HEAD)
            seg = ov[:, sl]
            r = lax.rsqrt(jnp.mean(seg * seg, axis=-1, keepdims=True) + RMS_EPS)
            oh = seg * r
            dn = dyb[:, sl] * so[:, sl]
            dgate_ref[:, sl] = (dyb[:, sl] * oh * nw * dso[:, sl]).astype(BF16)
            dnw = dnw + jnp.sum(dn * oh, axis=0, keepdims=True)
            doh = dn * nw
            do_ref[:, sl] = r * (doh - oh * jnp.mean(doh * oh, axis=-1, keepdims=True))
        dhnw_ref[...] += dnw

    row = lambda col: pl.BlockSpec((TBT, D), lambda i: (i, col))
    wsp = _rows_weight_spec(wa)
    sd = lambda dt: jax.ShapeDtypeStruct((T, D), dt)
    return pl.pallas_call(
        body, name=name, grid=(T // TBT,),
        in_specs=[row(0), row(0), row(0), row(0), row(0), row(6), row(7), row(8),
                  pl.BlockSpec((None, N_MOD, D), lambda i: (_stream_of(i, cb), 0, 0)),
                  pl.BlockSpec((1, HEAD), lambda i: (0, 0)), wsp, wsp, wsp],
        out_specs=[row(0)] * 3 + [pl.BlockSpec((TBT, 3 * D), lambda i: (i, 2)), row(0), row(0),
                                  pl.BlockSpec((None, 1, D), lambda i: (_stream_of(i, cb), 0, 0)),
                                  pl.BlockSpec((1, HEAD), lambda i: (0, 0))],
        out_shape=[sd(BF16)] * 3 + [jax.ShapeDtypeStruct((T, 9 * D), BF16), sd(F32), sd(F32),
                                    jax.ShapeDtypeStruct((2, 1, D), F32), jax.ShapeDtypeStruct((1, HEAD), F32)],
        compiler_params=_params("arbitrary", vmem=VMEM_LIMIT_TOKEN_OUT),
    )(dx, tmo, pa, pb, o, parts, parts, parts, mod, hnw, wa, wb, wo)


def _conv_geometry(i, nb, cb):
    is_ctx = i < cb
    first = (i == 0) | (i == cb)
    last = (i == cb - 1) | (i == nb - 1)
    row = lax.broadcasted_iota(jnp.int32, (TB + 2 * GRID_W, 1), 0)
    w = row & (GRID_W - 1)
    left_ok = (w != 0) | is_ctx
    right_ok = (w != GRID_W - 1) | is_ctx
    return is_ctx, first, last, left_ok, right_ok


def _ext(p_ref, m_ref, n_ref, first, last):
    return jnp.concatenate([jnp.where(first, 0.0, p_ref[...]), m_ref[...], jnp.where(last, 0.0, n_ref[...])], axis=0)


def _shift_prev(e, ok):
    return jnp.where(ok, pltpu.roll(e, 1, 0), 0.0)


def _shift_next(e, ok):
    return jnp.where(ok, pltpu.roll(e, e.shape[0] - 1, 0), 0.0)


def _halo_specs(cbk, n64, coff=0):
    r = TB // GRID_W
    prev = pl.BlockSpec((GRID_W, cbk), lambda j, i: (jnp.maximum(r * i - 1, 0), j + coff))
    main = pl.BlockSpec((TB, cbk), lambda j, i: (i, j + coff))
    nxt = pl.BlockSpec((GRID_W, cbk), lambda j, i: (jnp.minimum(r * i + r, n64 - 1), j + coff))
    return [prev, main, nxt]


def _conv_cblock(dff):
    return _tile(dff, 1408)


def _conv_fwd(up, cw, cbias, ctx_rows, name):
    T, dff = up.shape[0], up.shape[1] // 2
    cbk = _conv_cblock(dff)
    nb, cb = T // TB, ctx_rows // TB
    nvb = dff // cbk

    def body(ap_ref, a_ref, an_ref, v_ref, cw_ref, cb_ref, ac_ref, act_ref):
        i = pl.program_id(1)
        is_ctx, first, last, lok, rok = _conv_geometry(i, nb, cb)
        e = _ext(ap_ref, a_ref, an_ref, first, last)
        el = _shift_prev(e, lok)
        er = _shift_next(e, rok)
        cwv = cw_ref[...]

        def comb(dr, lo):
            sl = slice(lo, lo + TB)
            return cwv[3 * dr:3 * dr + 1] * el[sl] + cwv[3 * dr + 1:3 * dr + 2] * e[sl] + cwv[3 * dr + 2:3 * dr + 3] * er[sl]

        out = comb(1, GRID_W) + jnp.where(is_ctx, 0.0, comb(0, 0) + comb(2, 2 * GRID_W))
        a_c = out + cb_ref[...]
        ac_ref[...] = a_c
        act_ref[...] = (_gelu(a_c) * v_ref[...]).astype(BF16)

    main = pl.BlockSpec((TB, cbk), lambda j, i: (i, j))
    return pl.pallas_call(
        body, name=name, grid=(dff // cbk, nb),
        in_specs=_halo_specs(cbk, T // GRID_W) + [pl.BlockSpec((TB, cbk), lambda j, i: (i, j + nvb)),
                                                 pl.BlockSpec((9, cbk), lambda j, i: (0, j)),
                                                 pl.BlockSpec((1, cbk), lambda j, i: (0, j))],
        out_specs=[main, main],
        out_shape=[jax.ShapeDtypeStruct((T, dff), F32), jax.ShapeDtypeStruct((T, dff), BF16)],
        compiler_params=_params("parallel", "parallel"),
    )(up, up, up, up, cw, cbias)


def _conv_bwd(up, ac, dact, cw, ctx_rows, name):
    T, dff = up.shape[0], up.shape[1] // 2
    cbk = _conv_cblock(dff)
    nb, cb = T // TB, ctx_rows // TB
    nvb = dff // cbk

    def body(ap_ref, a_ref, an_ref, vp_ref, v_ref, vn_ref, cp_ref, c_ref, cn_ref, dp_ref, d_ref, dn_ref, cw_ref,
             da_ref, dv_ref, dcw_ref, dcb_ref):
        i = pl.program_id(1)

        @pl.when(i == 0)
        def _():
            dcw_ref[...] = jnp.zeros_like(dcw_ref)
            dcb_ref[...] = jnp.zeros_like(dcb_ref)

        is_ctx, first, last, lok, rok = _conv_geometry(i, nb, cb)
        gl, dgl = _gelu_both(_ext(cp_ref, c_ref, cn_ref, first, last))
        g = _ext(dp_ref, d_ref, dn_ref, first, last) * _ext(vp_ref, v_ref, vn_ref, first, last) * dgl
        dv_ref[...] = (d_ref[...] * gl[GRID_W:GRID_W + TB]).astype(BF16)
        gm = _shift_prev(g, lok)
        gp = _shift_next(g, rok)
        cwv = cw_ref[...]

        def comb(dr, lo):
            sl = slice(lo, lo + TB)
            return cwv[3 * dr:3 * dr + 1] * gp[sl] + cwv[3 * dr + 1:3 * dr + 2] * g[sl] + cwv[3 * dr + 2:3 * dr + 3] * gm[sl]

        da = comb(1, GRID_W) + jnp.where(is_ctx, 0.0, comb(0, 2 * GRID_W) + comb(2, 0))
        da_ref[...] = da.astype(BF16)
        e = _ext(ap_ref, a_ref, an_ref, first, last)
        taps = [_shift_prev(e, lok), e, _shift_next(e, rok)]
        gmain = g[GRID_W:GRID_W + TB]
        dcb_ref[...] += jnp.sum(gmain, axis=0, keepdims=True)
        vert = jnp.where(is_ctx, 0.0, 1.0)
        for dr in range(3):
            sl = slice(dr * GRID_W, dr * GRID_W + TB)
            for dw in range(3):
                s = jnp.sum(gmain * taps[dw][sl], axis=0, keepdims=True)
                if dr != 1:
                    s = s * vert
                k = 3 * dr + dw
                dcw_ref[k:k + 1, :] += s

    main = pl.BlockSpec((TB, cbk), lambda j, i: (i, j))
    halo = _halo_specs(cbk, T // GRID_W)
    acc9 = pl.BlockSpec((9, cbk), lambda j, i: (0, j))
    acc1 = pl.BlockSpec((1, cbk), lambda j, i: (0, j))
    return pl.pallas_call(
        body, name=name, grid=(dff // cbk, nb),
        in_specs=halo + _halo_specs(cbk, T // GRID_W, nvb) + halo + halo + [acc9],
        out_specs=[main, main, acc9, acc1],
        out_shape=[jax.ShapeDtypeStruct((T, dff), BF16), jax.ShapeDtypeStruct((T, dff), BF16),
                   jax.ShapeDtypeStruct((9, dff), F32), jax.ShapeDtypeStruct((1, dff), F32)],
        compiler_params=_params("parallel", "arbitrary"),
    )(up, up, up, up, up, up, ac, ac, ac, dact, dact, dact, cw)


def _ffn_out_fwd(act, xm, mod, wd, ctx_rows, name):
    T, D = xm.shape
    dff = act.shape[1]
    cb = ctx_rows // TB

    def body(act_ref, x_ref, mod_ref, w_ref, xo_ref, fo_ref):
        out = _nn(act_ref[...], _full(w_ref))
        fo_ref[...] = out
        xo_ref[...] = x_ref[...] + mod_ref[5:6, :] * out

    row = pl.BlockSpec((TB, D), lambda i: (i, 0))
    return pl.pallas_call(
        body, name=name, grid=(T // TB,),
        in_specs=[pl.BlockSpec((TB, dff), lambda i: (i, 0)), row,
                  pl.BlockSpec((None, N_MOD, D), lambda i: (_stream_of(i, cb), 0, 0)),
                  _rows_weight_spec(wd)],
        out_specs=[row, row],
        out_shape=[jax.ShapeDtypeStruct((T, D), F32), jax.ShapeDtypeStruct((T, D), F32)],
        compiler_params=_params("parallel"),
    )(act, xm, mod, wd)


def _ffn_out_bwd(dx, fo, mod, wd, ctx_rows, name):
    T, D = dx.shape
    dff = N_CHIPS * wd.shape[1]
    cb = ctx_rows // TB

    def body(dx_ref, fo_ref, mod_ref, w_ref, dout_ref, dact_ref, dg2_ref):
        i = pl.program_id(0)

        @pl.when((i == 0) | (i == cb))
        def _():
            dg2_ref[...] = jnp.zeros_like(dg2_ref)

        dxv = dx_ref[...]
        dg2_ref[...] += jnp.sum(dxv * fo_ref[...], axis=0, keepdims=True)
        dout = (dxv * mod_ref[5:6, :]).astype(BF16)
        dout_ref[...] = dout
        dact_ref[...] = _nt(dout, _full(w_ref))

    row = pl.BlockSpec((TB, D), lambda i: (i, 0))
    return pl.pallas_call(
        body, name=name, grid=(T // TB,),
        in_specs=[row, row, pl.BlockSpec((None, N_MOD, D), lambda i: (_stream_of(i, cb), 0, 0)),
                  _rows_weight_spec(wd)],
        out_specs=[row, pl.BlockSpec((TB, dff), lambda i: (i, 0)),
                   pl.BlockSpec((None, 1, D), lambda i: (_stream_of(i, cb), 0, 0))],
        out_shape=[jax.ShapeDtypeStruct((T, D), BF16), jax.ShapeDtypeStruct((T, dff), F32),
                   jax.ShapeDtypeStruct((2, 1, D), F32)],
        compiler_params=_params("arbitrary"),
    )(dx, fo, mod, wd)


def _loss_bwd(x, target, fw, ctx_rows, name):
    T, D = x.shape
    cb = ctx_rows // TB

    def body(x_ref, t_ref, fw_ref, dx_ref, loss_ref, dfw_ref):
        i = pl.program_id(0)

        @pl.when(i == 0)
        def _():
            loss_ref[...] = jnp.zeros_like(loss_ref)
            dfw_ref[...] = jnp.zeros_like(dfw_ref)

        @pl.when(i < cb)
        def _():
            dx_ref[...] = jnp.zeros_like(dx_ref)

        @pl.when(i >= cb)
        def _():
            xv = x_ref[...]
            r = lax.rsqrt(jnp.mean(xv * xv, axis=-1, keepdims=True) + RMS_EPS)
            xh = xv * r
            fwv = fw_ref[...]
            err = xh * fwv - t_ref[...]
            loss_ref[...] += (0.5 / D) * jnp.sum(err * err)
            dy = err * (1.0 / D)
            dfw_ref[...] += jnp.sum(dy * xh, axis=0, keepdims=True)
            dxh = dy * fwv
            dx_ref[...] = r * (dxh - xh * jnp.mean(dxh * xh, axis=-1, keepdims=True))

    row = pl.BlockSpec((TB, D), lambda i: (i, 0))
    return pl.pallas_call(
        body, name=name, grid=(T // TB,),
        in_specs=[row, pl.BlockSpec((TB, D), lambda i: (jnp.maximum(i - cb, 0), 0)), pl.BlockSpec((1, D), lambda i: (0, 0))],
        out_specs=[row, pl.BlockSpec((1, 128), lambda i: (0, 0)), pl.BlockSpec((1, D), lambda i: (0, 0))],
        out_shape=[jax.ShapeDtypeStruct((T, D), F32), jax.ShapeDtypeStruct((1, 128), F32),
                   jax.ShapeDtypeStruct((1, D), F32)],
        compiler_params=_params("arbitrary"),
    )(x, target, fw)


def _adamw(w, gs, m, v, name):
    L, R, C = w.shape
    assert len(gs) == L
    rb = _rows_tile(R, max(16, (1 << 18) // C // 16 * 16))
    bc1 = 1.0 - ADAM_B1 ** ADAM_STEP
    bc2 = 1.0 - ADAM_B2 ** ADAM_STEP

    def body(w_ref, m_ref, v_ref, *rest):
        g_refs, (g_ref, d_ref, nm_ref, nv_ref) = rest[:L], rest[L:]
        layer = pl.program_id(0)
        for li in range(L):
            @pl.when(layer == li)
            def _():
                gv = g_refs[li][...]
                g_ref[...] = gv
                nm = ADAM_B1 * m_ref[...] + (1.0 - ADAM_B1) * gv
                nv = ADAM_B2 * v_ref[...] + (1.0 - ADAM_B2) * (gv * gv)
                nm_ref[...] = nm
                nv_ref[...] = nv
                d_ref[...] = -ADAM_LR * ((nm / bc1) / (jnp.sqrt(nv / bc2) + ADAM_EPS) + ADAM_WD * w_ref[...])

    blk = pl.BlockSpec((None, rb, C), lambda l, i: (l, i, 0))
    gblk = pl.BlockSpec((rb, C), lambda l, i: (i, 0))
    sd = jax.ShapeDtypeStruct((L, R, C), F32)
    return pl.pallas_call(
        body, name=name, grid=(L, R // rb), in_specs=[blk] * 3 + [gblk] * L, out_specs=[blk] * 4, out_shape=[sd] * 4,
        compiler_params=_params("parallel", "parallel"),
    )(w, m, v, *gs)


def _local_step(xs, cv, target, W, layer_weights, on_layer_grads, ctx_rows):
    T, D = xs.shape
    depth = W["norm1_w"].shape[0]
    saved = []
    X = xs
    for l in range(depth):
        s = {}
        Wl = layer_weights(l, X)
        mod_all, sa = _mod_fwd(cv, Wl["ada_w"], W["ada_b"][l][None, :] + Wl["token"], f"mod_fwd_{l}")
        mod = mod_all[:2].reshape(2, N_MOD, D)
        h1 = _norm_mod(X, W["norm1_w"][l][None, :], mod, 0, ctx_rows, f"norm1_{l}")
        parts = _mm_nn_w(h1, Wl["w_in"], F32, f"in_proj_{l}")
        o_f, st_f = _hgrn_fwd(parts, W["hlb"], l, False, ctx_rows, f"hgrn_fwd_f_{l}")
        o, st_b = _hgrn_fwd(parts, W["hlb"], l, True, ctx_rows, f"hgrn_fwd_b_{l}", o_add=o_f)
        ya = _sgu_fwd(parts, W["sgu_ln_w"][l][None, :], W["sgu_ln_b"][l][None, :], W["sgu_w"][l], W["sgu_bt"][l],
                      f"sgu_fwd_{l}")
        Wl.update(Wl.pop("late")(ya))
        yb, pa, pb, mg, tmo, xm, h2 = _token_out_fwd(o, parts, ya, X, mod, W["hnw"][l][None, :] + Wl["late_token"],
                                                     W["norm2_w"][l][None, :], Wl["w_a"], Wl["w_b"], Wl["w_o"], ctx_rows,
                                                     f"token_out_fwd_{l}")
        up = _mm_nn_w(h2, Wl["w_up"], F32, f"up_proj_{l}")
        ac, act = _conv_fwd(up, Wl["conv_w"], W["conv_b"][l][None, :], ctx_rows, f"conv_fwd_{l}")
        xo, fo = _ffn_out_fwd(act, xm, mod, Wl["w_down"], ctx_rows, f"ffn_out_fwd_{l}")
        s.update(X=X, Wl=Wl, mod=mod, mod_all=mod_all, sa=sa, h1=h1, parts=parts, o=o, st_f=st_f, st_b=st_b, ya=ya, yb=yb,
                 pa=pa, pb=pb, mg=mg, tmo=tmo, xm=xm, h2=h2, up=up, ac=ac, act=act, fo=fo)
        saved.append(s)
        X = xo

    dX, loss_row, dfw = _loss_bwd(X, target, W["final_norm_w"][None, :], ctx_rows, "loss_bwd")
    G = {k: [None] * depth for k in ("ada_b", "norm1_w", "sgu_ln_w", "sgu_ln_b", "sgu_w", "sgu_b", "hlb1", "hnw", "norm2_w",
                                     "conv_w", "conv_b", "dmod")}
    dcv = jnp.zeros_like(cv)
    for l in reversed(range(depth)):
        s = saved[l]
        mod, Wl = s["mod"], s["Wl"]
        big = {}
        dout2, dact, dg2 = _ffn_out_bwd(dX, s["fo"], mod, Wl["w_down"], ctx_rows, f"ffn_out_bwd_{l}")
        big["w_down"] = _mm_tn(s["act"], dout2, F32, f"dw_down_{l}")
        da, dv, dcw, dcb = _conv_bwd(s["up"], s["ac"], dact, Wl["conv_w"], ctx_rows, f"conv_bwd_{l}")
        G["conv_w"][l], G["conv_b"][l] = dcw, dcb[0]
        big["w_up"] = _mm_tn(s["h2"], (da, dv), F32, f"dw_up_{l}", out_chips=True)
        dh2 = _mm_nt_w((da, dv), Wl["w_up"], F32, f"dh2_{l}")
        dxm, dm2, dnw2 = _norm_mod_bwd(dh2, s["xm"], dX, W["norm2_w"][l][None, :], mod, 3, ctx_rows, f"norm2_bwd_{l}")
        G["norm2_w"][l] = dnw2[0]
        (dout1, dpa, dpb, dparts, dya, do, dg1, dhnw) = _token_out_bwd(
            dxm, s["tmo"], s["pa"], s["pb"], s["o"], s["parts"], mod, W["hnw"][l][None, :], Wl["w_a"], Wl["w_b"], Wl["w_o"],
            ctx_rows, f"token_out_bwd_{l}")
        G["hnw"][l] = dhnw[0]
        big["w_o"] = _mm_tn(s["mg"], dout1, F32, f"dw_o_{l}")
        big["w_a"] = _mm_tn(s["ya"], dpa, F32, f"dw_a_{l}")
        big["w_b"] = _mm_tn(s["yb"], dpb, F32, f"dw_b_{l}")
        tok = on_layer_grads(l, "early", big)
        dparts, dsw, dsbt, dlnw, dlnb = _sgu_bwd(s["parts"], dya, W["sgu_ln_w"][l][None, :], W["sgu_ln_b"][l][None, :] + tok,
                                                 W["sgu_w"][l], W["sgu_bt"][l], dparts, f"sgu_bwd_{l}")
        G["sgu_w"][l], G["sgu_b"][l], G["sgu_ln_w"][l], G["sgu_ln_b"][l] = dsw, dsbt.T, dlnw[0], dlnb[0]
        dq_f, dz_f, di_f, dlb_f = _hgrn_bwd(s["parts"], W["hlb"], do, s["st_f"], l, False, ctx_rows, f"hgrn_bwd_f_{l}")
        dparts, dlb_b = _hgrn_bwd(s["parts"], W["hlb"], do, s["st_b"], l, True, ctx_rows, f"hgrn_bwd_b_{l}",
                                  other=(dq_f, dz_f, di_f), dparts=dparts)
        G["hlb1"][l] = jnp.concatenate([dlb_f[0], dlb_b[0]])
        tok = on_layer_grads(l, "late", {"w_in": _mm_tn(s["h1"], dparts, F32, f"dw_in_{l}", out_chips=True)})
        dh1 = _mm_nt_w(dparts, Wl["w_in"], F32, f"dh1_{l}")
        tok = tok + on_layer_grads(l, "end", {"after": dh1})
        dX, dm1, dnw1 = _norm_mod_bwd(dh1, s["X"], dxm, W["norm1_w"][l][None, :] + tok, mod, 0, ctx_rows, f"norm1_bwd_{l}")
        G["norm1_w"][l] = dnw1[0]
        dmod = jnp.concatenate([dm1, dg1, dm2, dg2], axis=1).reshape(2, N_MOD * D)
        dmod16 = jnp.concatenate([dmod, jnp.zeros((cv.shape[0] - 2, N_MOD * D), F32)], axis=0)
        G["ada_b"][l] = dmod[0] + dmod[1]
        G["dmod"][l] = dmod
        dcv = dcv + _cvec_bwd(dmod16, Wl["ada_w"], cv, f"dcvec_{l}")
    G["c_ctx"] = dcv[0]
    G["final_norm_w"] = dfw[0]
    return loss_row[0, 0], dX, G, saved[0]["sa"]


def _chip_peers(x, y, c):
    return [((1 - x, y, c), 2 * (1 - x) + y), ((x, 1 - y, c), 2 * x + 1 - y), ((1 - x, 1 - y, c), 2 * (1 - x) + 1 - y)]


def _rdma_call(ins, out_shapes, plan, n_remote, n_local, name, aliases=None):
    n_in, n_out = len(ins), len(out_shapes)

    def body(*refs):
        in_refs, out_refs = refs[:n_in], refs[n_in:n_in + n_out]
        send_sems, recv_sems, local_sems = refs[n_in + n_out:]
        x, y, c = lax.axis_index("x"), lax.axis_index("y"), lax.axis_index("c")
        remote, local = plan(in_refs, out_refs, x, y, c)
        assert len(remote) == n_remote and len(local) == n_local, (name, len(remote), len(local))
        copies = [pltpu.make_async_copy(s, d, local_sems.at[i]) for i, (s, d) in enumerate(local)]
        copies += [pltpu.make_async_remote_copy(src_ref=s, dst_ref=d, send_sem=send_sems.at[k], recv_sem=recv_sems.at[k],
                                                device_id=dev, device_id_type=pl.DeviceIdType.MESH)
                   for k, (s, d, dev) in enumerate(remote)]
        for cp in copies:
            cp.start()
        for cp in copies:
            cp.wait()

    hbm = pl.BlockSpec(memory_space=pltpu.HBM)
    return pl.pallas_call(
        body, name=name, in_specs=[hbm] * n_in, out_specs=[hbm] * n_out, out_shape=out_shapes,
        scratch_shapes=[pltpu.SemaphoreType.DMA((n_remote,)), pltpu.SemaphoreType.DMA((n_remote,)),
                        pltpu.SemaphoreType.DMA((max(n_local, 1),))],
        input_output_aliases=aliases or {},
    )(*ins)


DMA_PIECE_BYTES = 1 << 18
DMA_MAX_PIECES = 8


def _row_pieces(shape, dtype):
    rows = shape[0]
    row_bytes = jnp.dtype(dtype).itemsize
    for d in shape[1:]:
        row_bytes *= d
    n = 1
    while n < DMA_MAX_PIECES and rows % (2 * n * 16) == 0 and rows * row_bytes // (2 * n) >= DMA_PIECE_BYTES:
        n *= 2
    return [(i * (rows // n), rows // n) for i in range(n)]


def _half_pieces(o, c):
    r2 = o.shape[1] // 2
    return [pl.ds(c * r2 + st, sz) for st, sz in _row_pieces((r2,) + o.shape[2:], o.dtype)]


def _n_half_pieces(arrays):
    return sum(len(_row_pieces((a.shape[1] // 2,) + a.shape[2:], a.dtype)) for a in arrays)


def _plan_gather_far(lands, x, y, c):
    me = 2 * x + y
    return [(o.at[me, rows], o.at[me, rows], dev) for dev, _ in _chip_peers(x, y, c) for o in lands
            for rows in _half_pieces(o, c)]


def _plan_gather_near(lands, x, y, c):
    return [(o.at[idx, rows], o.at[idx, rows], (x, y, 1 - c)) for _, idx in _chip_peers(x, y, c) for o in lands
            for rows in _half_pieces(o, c)]


def _gather_weights(lands, name):
    n = len(lands)
    n_far = (N_CHIPS - 1) * _n_half_pieces(lands)

    def body(*refs):
        outs = refs[n:2 * n]
        far_send, far_recv, near_send, near_recv = refs[2 * n:]
        x, y, c = lax.axis_index("x"), lax.axis_index("y"), lax.axis_index("c")
        mk = lambda plan, send, recv: [
            pltpu.make_async_remote_copy(src_ref=s, dst_ref=d, send_sem=send.at[k], recv_sem=recv.at[k], device_id=dev,
                                         device_id_type=pl.DeviceIdType.MESH)
            for k, (s, d, dev) in enumerate(plan(outs, x, y, c))]
        far, near = mk(_plan_gather_far, far_send, far_recv), mk(_plan_gather_near, near_send, near_recv)
        assert len(far) == n_far and len(near) == n_far
        for cp in far:
            cp.start()
        for k in range(n_far):
            far[k].wait_recv()
            near[k].start()
        for k in range(n_far):
            near[k].wait_recv()
        for cp in far + near:
            cp.wait_send()

    hbm = pl.BlockSpec(memory_space=pltpu.HBM)
    sems = pltpu.SemaphoreType.DMA((n_far,))
    return pl.pallas_call(
        body, name=name, in_specs=[hbm] * n, out_specs=[hbm] * n,
        out_shape=[jax.ShapeDtypeStruct(a.shape, a.dtype) for a in lands],
        scratch_shapes=[sems, sems, sems, sems], input_output_aliases={i: i for i in range(n)},
    )(*lands)


def _gather_all(v, name):
    def plan(ins, outs, x, y, c):
        (s,), (o,) = ins, outs
        me = 4 * x + 2 * y + c
        flip = lambda a, f: 1 - a if f else a
        remote = [(s, o.at[me], (flip(x, m & 4), flip(y, m & 2), flip(c, m & 1))) for m in range(1, 8)]
        return remote, [(s, o.at[me])]

    return _rdma_call([v], [jax.ShapeDtypeStruct((8,) + v.shape, v.dtype)], plan, 7, 1, name)[0]


def _plan_pair(ins, lands, x, y, c):
    return [(a.at[j, 1 - c, pl.ds(st, sz)], o.at[j, pl.ds(st, sz)], (x, y, 1 - c)) for a, o in zip(ins, lands)
            for j in range(N_CHIPS) for st, sz in _row_pieces(a.shape[2:], a.dtype)]


def _n_pair_copies(parts):
    return N_CHIPS * sum(len(_row_pieces(a.shape[2:], a.dtype)) for a in parts)


def _reduce_pair(parts, name):
    shapes = [jax.ShapeDtypeStruct((N_CHIPS,) + a.shape[2:], a.dtype) for a in parts]
    return _rdma_call(parts, shapes, lambda ins, outs, x, y, c: (_plan_pair(ins, outs, x, y, c), []),
                      _n_pair_copies(parts), 0, name)


def _plan_chips(ins, lands, x, y, c):
    me = 2 * x + y
    return [(a.at[idx, pl.ds(st, sz)], o.at[me, pl.ds(st, sz)], dev) for dev, idx in _chip_peers(x, y, c)
            for a, o in zip(ins, lands) for st, sz in _row_pieces(a.shape[1:], a.dtype)]


def _n_chips_copies(parts):
    return (N_CHIPS - 1) * sum(len(_row_pieces(a.shape[1:], a.dtype)) for a in parts)


def _gather_pair(halves, name):
    def plan(ins, outs, x, y, c):
        return [(o.at[c, pl.ds(st, sz)], o.at[c, pl.ds(st, sz)], (x, y, 1 - c)) for o in outs
                for st, sz in _row_pieces(o.shape[1:], o.dtype)], []

    shapes = [jax.ShapeDtypeStruct(a.shape, a.dtype) for a in halves]
    n_remote = sum(len(_row_pieces(a.shape[1:], a.dtype)) for a in halves)
    return _rdma_call(halves, shapes, plan, n_remote, 0, name, aliases={i: i for i in range(len(halves))})


def _split_start(ins, lands, plan, n_remote, name):
    n_buf = len(ins) + len(lands)

    def body(*refs):
        in_refs, land_refs = refs[:len(ins)], refs[len(ins):n_buf]
        send_sems, recv_sems, token = refs[n_buf], refs[n_buf + 1], refs[-1]
        x, y, c = lax.axis_index("x"), lax.axis_index("y"), lax.axis_index("c")
        remote = plan(in_refs, land_refs, x, y, c)
        assert len(remote) == n_remote, (name, len(remote))
        for k, (s, d, dev) in enumerate(remote):
            pltpu.make_async_remote_copy(src_ref=s, dst_ref=d, send_sem=send_sems.at[k], recv_sem=recv_sems.at[k],
                                         device_id=dev, device_id_type=pl.DeviceIdType.MESH).start()
        token[...] = jnp.zeros_like(token)

    hbm = pl.BlockSpec(memory_space=pltpu.HBM)
    sem = pl.BlockSpec(memory_space=pltpu.SEMAPHORE)
    bufs = list(ins) + list(lands)
    out = pl.pallas_call(
        body, name=name, in_specs=[hbm] * n_buf,
        out_specs=(sem, sem) + (hbm,) * n_buf + (pl.BlockSpec(memory_space=pltpu.VMEM),),
        out_shape=(pltpu.SemaphoreType.DMA((n_remote,)), pltpu.SemaphoreType.DMA((n_remote,)))
        + tuple(pltpu.HBM(a.shape, a.dtype) for a in bufs) + (jax.ShapeDtypeStruct((8, 128), F32),),
        input_output_aliases={i: 2 + i for i in range(n_buf)},
        compiler_params=pltpu.CompilerParams(has_side_effects=pltpu.SideEffectType.DATAFLOW_SIDE_EFFECTING),
    )(*[pltpu.with_memory_space_constraint(a, pltpu.HBM) for a in bufs])
    return dict(send=out[0], recv=out[1], ins=list(out[2:2 + len(ins)]), lands=list(out[2 + len(ins):2 + n_buf]),
                token=out[-1][0, 0], token_array=out[-1], plan=plan, n_remote=n_remote)


def _split_wait(st, after, name):
    n_in, n_buf = len(st["ins"]), len(st["ins"]) + len(st["lands"])
    plan, n_remote = st["plan"], st["n_remote"]

    def body(*refs):
        in_refs, land_refs = refs[:n_in], refs[n_in:n_buf]
        send_sems, recv_sems = refs[n_buf], refs[n_buf + 1]
        x, y, c = lax.axis_index("x"), lax.axis_index("y"), lax.axis_index("c")
        for k, (s, d, dev) in enumerate(plan(in_refs, land_refs, x, y, c)):
            cp = pltpu.make_async_remote_copy(src_ref=s, dst_ref=d, send_sem=send_sems.at[k], recv_sem=recv_sems.at[k],
                                              device_id=dev, device_id_type=pl.DeviceIdType.MESH)
            cp.wait_send()
            cp.wait_recv()

    hbm = pl.BlockSpec(memory_space=pltpu.HBM)
    sem = pl.BlockSpec(memory_space=pltpu.SEMAPHORE)
    bufs = st["ins"] + st["lands"]
    out = pl.pallas_call(
        body, name=name, in_specs=[hbm] * n_buf + [sem, sem, pl.BlockSpec(memory_space=pl.ANY)],
        out_specs=[hbm] * n_buf, out_shape=[pltpu.HBM(a.shape, a.dtype) for a in bufs],
        input_output_aliases={i: i for i in range(n_buf)},
        compiler_params=pltpu.CompilerParams(has_side_effects=pltpu.SideEffectType.DATAFLOW_SIDE_EFFECTING),
    )(*bufs, st["send"], st["recv"], after)
    return list(out[:n_in]), list(out[n_in:])


def _pair_forward(lands, name):
    shapes = [jax.ShapeDtypeStruct(a.shape, a.dtype) for a in lands]
    return _rdma_call(lands, shapes, lambda ins, outs, x, y, c: (_plan_gather_near(outs, x, y, c), []),
                      (N_CHIPS - 1) * _n_half_pieces(lands), 0, name, aliases={i: i for i in range(len(lands))})


def _sum_block_rows(r, C):
    return _rows_tile(r, max(16, (1 << 18) // C // 16 * 16))


def _sum_pair(a, recv, cidx, name):
    nch, _, r, C = a.shape
    rb = _sum_block_rows(r, C)

    def body(c_ref, a_ref, r_ref, o_ref):
        o_ref[...] = (a_ref[...] + r_ref[...]).astype(BF16)

    blk = pl.BlockSpec((None, rb, C), lambda j, i, c: (j, i, 0))
    return pl.pallas_call(
        body, name=name,
        grid_spec=pltpu.PrefetchScalarGridSpec(
            num_scalar_prefetch=1, grid=(nch, r // rb),
            in_specs=[pl.BlockSpec((None, None, rb, C), lambda j, i, c: (j, c[0], i, 0)), blk], out_specs=blk),
        out_shape=jax.ShapeDtypeStruct((nch, r, C), BF16),
        compiler_params=_params("parallel", "parallel"),
    )(cidx, a, recv)


def _sum_chips(mine, recv, ids, name):
    nch, r, C = recv.shape
    rb = _sum_block_rows(r, C)

    def body(ids_ref, m_ref, *rest):
        r_refs, o_ref = rest[:nch], rest[nch]
        chip = ids_ref[1]
        own = m_ref[...].astype(F32)
        acc = jnp.where(chip == 0, own, r_refs[0][...].astype(F32))
        for q in range(1, nch):
            acc = acc + jnp.where(chip == q, own, r_refs[q][...].astype(F32))
        o_ref[...] = acc

    def slot(q):
        return pl.BlockSpec((None, rb, C), lambda i, ids: (jnp.where(ids[1] == q, (q + 1) % nch, q), i, 0))

    return pl.pallas_call(
        body, name=name,
        grid_spec=pltpu.PrefetchScalarGridSpec(
            num_scalar_prefetch=1, grid=(r // rb,),
            in_specs=[pl.BlockSpec((None, rb, C), lambda i, ids: (ids[1], i, 0))] + [slot(q) for q in range(nch)],
            out_specs=pl.BlockSpec((None, rb, C), lambda i, ids: (ids[0], i, 0))),
        out_shape=jax.ShapeDtypeStruct((N_CORES, r, C), F32),
        compiler_params=_params("parallel"),
    )(ids, mine, *([recv] * nch))


PACK_COLS = 1024
_SHARDED = ("ada_w", "w_in", "w_branch_a", "w_branch_b", "w_out", "ffn_w_up", "ffn_w_down")
_LAYER_KEYS = ("ada_w", "w_in", "w_a", "w_b", "w_o", "w_up", "w_down")
_SMALL = ("c_ctx", "ada_b", "norm1_w", "sgu_ln_w", "sgu_ln_b", "sgu_w", "sgu_b", "hgrn_lower_bounds", "hgrn_norm_w",
          "norm2_w", "ffn_conv_b", "final_norm_w")
_ORDER = ("c_ctx", "ada_w", "ada_b", "norm1_w", "w_in", "sgu_ln_w", "sgu_ln_b", "sgu_w", "sgu_b", "hgrn_lower_bounds",
          "hgrn_norm_w", "w_branch_a", "w_branch_b", "w_out", "norm2_w", "ffn_w_up", "ffn_conv_w", "ffn_conv_b",
          "ffn_w_down", "final_norm_w")


def _pad_to(v, n):
    return jnp.concatenate([v, jnp.zeros((n - v.shape[0],), v.dtype)]) if v.shape[0] < n else v


def _round_up(n, m):
    return (n + m - 1) // m * m


def _pack(arrays, n_pad):
    flat = jnp.concatenate([a.reshape(-1) for a in arrays])
    return _pad_to(flat, n_pad)


def _unpack(flat, like):
    out, off = [], 0
    for a in like:
        out.append(flat[off:off + a.size].reshape(a.shape))
        off += a.size
    return out


def kernel(x, c, ctx, c_ctx, ada_w, ada_b, norm1_w, w_in, sgu_ln_w, sgu_ln_b, sgu_w, sgu_b, hgrn_lower_bounds, hgrn_norm_w, w_branch_a, w_branch_b, w_out, norm2_w, ffn_w_up, ffn_conv_w, ffn_conv_b, ffn_w_down, final_norm_w, loss_target, m_c_ctx, m_ada_w, m_ada_b, m_norm1_w, m_w_in, m_sgu_ln_w, m_sgu_ln_b, m_sgu_w, m_sgu_b, m_hgrn_lower_bounds, m_hgrn_norm_w, m_w_branch_a, m_w_branch_b, m_w_out, m_norm2_w, m_ffn_w_up, m_ffn_conv_w, m_ffn_conv_b, m_ffn_w_down, m_final_norm_w, v_c_ctx, v_ada_w, v_ada_b, v_norm1_w, v_w_in, v_sgu_ln_w, v_sgu_ln_b, v_sgu_w, v_sgu_b, v_hgrn_lower_bounds, v_hgrn_norm_w, v_w_branch_a, v_w_branch_b, v_w_out, v_norm2_w, v_ffn_w_up, v_ffn_conv_w, v_ffn_conv_b, v_ffn_w_down, v_final_norm_w):
    w = dict(c_ctx=c_ctx, ada_w=ada_w, ada_b=ada_b, norm1_w=norm1_w, w_in=w_in, sgu_ln_w=sgu_ln_w, sgu_ln_b=sgu_ln_b,
             sgu_w=sgu_w, sgu_b=sgu_b, hgrn_lower_bounds=hgrn_lower_bounds, hgrn_norm_w=hgrn_norm_w, w_branch_a=w_branch_a,
             w_branch_b=w_branch_b, w_out=w_out, norm2_w=norm2_w, ffn_w_up=ffn_w_up, ffn_conv_w=ffn_conv_w,
             ffn_conv_b=ffn_conv_b, ffn_w_down=ffn_w_down, final_norm_w=final_norm_w)
    mom = dict(zip(_ORDER, (m_c_ctx, m_ada_w, m_ada_b, m_norm1_w, m_w_in, m_sgu_ln_w, m_sgu_ln_b, m_sgu_w, m_sgu_b,
                            m_hgrn_lower_bounds, m_hgrn_norm_w, m_w_branch_a, m_w_branch_b, m_w_out, m_norm2_w, m_ffn_w_up,
                            m_ffn_conv_w, m_ffn_conv_b, m_ffn_w_down, m_final_norm_w)))
    var = dict(zip(_ORDER, (v_c_ctx, v_ada_w, v_ada_b, v_norm1_w, v_w_in, v_sgu_ln_w, v_sgu_ln_b, v_sgu_w, v_sgu_b,
                            v_hgrn_lower_bounds, v_hgrn_norm_w, v_w_branch_a, v_w_branch_b, v_w_out, v_norm2_w, v_ffn_w_up,
                            v_ffn_conv_w, v_ffn_conv_b, v_ffn_w_down, v_final_norm_w)))
    depth, D = norm1_w.shape
    dff = ffn_conv_b.shape[1]
    ctx_rows = ctx.shape[1]

    assert depth == 2, "the lower-bound softmax is written for two layers"
    core = lax.axis_index("c")
    chip = 2 * lax.axis_index("x") + lax.axis_index("y")
    ids = jnp.stack([core, chip]).astype(jnp.int32)

    first, rest = _LAYER_KEYS[:2], _LAYER_KEYS[2:]
    shard = lambda l, k: w[_SHARDED[_LAYER_KEYS.index(k)]][l].astype(BF16)
    started, conv_full = {}, []

    def landing(s):
        return lax.dynamic_update_slice(lax.empty((N_CHIPS,) + s.shape, s.dtype), s[None], (chip,) + (0,) * s.ndim)

    def start_gather(l, keys, tag):
        lands = [landing(shard(l, k)) for k in keys]
        started[tag] = _split_start([], lands, lambda ins, lds, x, y, c: _plan_gather_far(lds, x, y, c),
                                    (N_CHIPS - 1) * _n_half_pieces(lands), f"gather_start_{tag}")
        return started[tag]["token"]

    def finish_gather(keys, tag, after):
        _, lands = _split_wait(started[tag], after, f"gather_wait_{tag}")
        return dict(zip(keys, _pair_forward(lands, f"gather_forward_{tag}")))

    def layer_weights(l, after):
        if l == 0:
            got = _gather_weights([landing(shard(0, k)) for k in first] + [landing(ffn_conv_w)], "gather_weights_first")
            conv_full.append(jnp.transpose(got[-1], (1, 2, 3, 0, 4)).reshape(depth, 9, dff))
            out = dict(zip(first, got), token=start_gather(0, rest, "rest_0"))
        else:
            out = dict(finish_gather(first, f"first_{l}", after), token=0.0)

        def late(after_late):
            more = finish_gather(rest, f"rest_{l}", after_late)
            more["late_token"] = 0.0
            if l + 1 < depth:
                more["late_token"] = start_gather(l + 1, first, f"first_{l + 1}") + start_gather(l + 1, rest, f"rest_{l + 1}")
            return more

        return dict(out, conv_w=conv_full[0][l], late=late)

    groups, order = {}, []

    def as_parts(gs):
        return [g.reshape(N_CHIPS, N_CORES, g.size // (N_CHIPS * N_CORES * g.shape[-1]), g.shape[-1]) for g in gs]

    def pair_start(tag, l, keys, gs):
        parts = as_parts(gs)
        lands = [lax.empty((N_CHIPS,) + p.shape[2:], p.dtype) for p in parts]
        groups[tag] = dict(l=l, keys=keys, pair=_split_start(parts, lands, _plan_pair, _n_pair_copies(parts),
                                                             f"reduce_pair_start_{tag}"))
        order.append(tag)
        return groups[tag]["pair"]["token"]

    def chips_start(tag, after):
        parts, other = _split_wait(groups[tag]["pair"], after, f"reduce_pair_wait_{tag}")
        sums = [_sum_pair(a, o, ids, f"sum_pair_{tag}_{i}") for i, (a, o) in enumerate(zip(parts, other))]
        lands = [lax.empty(s.shape, s.dtype) for s in sums]
        groups[tag]["chips"] = _split_start(sums, lands, _plan_chips, _n_chips_copies(sums), f"reduce_chips_start_{tag}")
        return groups[tag]["chips"]["token"]

    def chips_finish(tag, after):
        sums, recv = _split_wait(groups[tag]["chips"], after, f"reduce_chips_wait_{tag}")
        return {(groups[tag]["l"], k): _sum_chips(sums[i], recv[i], ids, f"sum_chips_{tag}_{i}")
                for i, k in enumerate(groups[tag]["keys"])}

    def on_layer_grads(l, stage, gs):
        if stage == "early":
            return pair_start(f"early_{l}", l, list(gs), list(gs.values()))
        if stage == "late":
            return pair_start(f"late_{l}", l, ["w_in"], [gs["w_in"]]) + chips_start(f"early_{l}", gs["w_in"])
        return chips_start(f"late_{l}", gs["after"])

    W = dict(ada_b=ada_b, norm1_w=norm1_w, sgu_ln_w=sgu_ln_w, sgu_ln_b=sgu_ln_b, sgu_w=sgu_w.astype(BF16),
             sgu_bt=jnp.swapaxes(sgu_b, 1, 2), hlb=hgrn_lower_bounds, hnw=hgrn_norm_w, norm2_w=norm2_w, conv_b=ffn_conv_b,
             final_norm_w=final_norm_w)
    xs = jnp.concatenate([ctx[0], x[0]], axis=0)
    cv = jnp.concatenate([c_ctx[None, :], c, jnp.zeros((14, D), F32)], axis=0)
    loss_local, dxs, G, sa = _local_step(xs, cv, loss_target[0], W, layer_weights, on_layer_grads, ctx_rows)
    loss = lax.psum(loss_local, ("x", "y", "c"))
    grad_x = dxs[ctx_rows:][None]

    pad8 = lambda a: jnp.pad(a, ((0, 8 - a.shape[0]), (0, 0)))
    fact = jnp.concatenate([pad8(sa[1:2].astype(F32))] + [pad8(G["dmod"][l][1].reshape(N_MOD, D)) for l in range(depth)]
                           + [pad8(G["dmod"][l][0].reshape(N_MOD, D)) for l in range(depth)], axis=0)
    facts = _gather_all(fact, "gather_mod_factors")
    lhs = jnp.concatenate([facts[:, 0].astype(BF16), jnp.broadcast_to(sa[0:1], (8, D))], axis=0)
    ada_cols = N_MOD * D // N_CHIPS
    g_ada = []
    for l in range(depth):
        lo_x, lo_c = 8 * (1 + l), 8 * (1 + depth + l)
        rhs = jnp.concatenate([facts[:, lo_x:lo_x + N_MOD].reshape(8, N_MOD * D),
                               facts[:, lo_c:lo_c + N_MOD].reshape(8, N_MOD * D)], axis=0)
        rhs = lax.dynamic_slice_in_dim(rhs, chip * ada_cols, ada_cols, axis=1).astype(BF16)
        g_ada.append(_mm_tn(lhs, rhs, F32, f"dw_ada_{l}"))

    dh = G["hlb1"][depth - 1]
    small_like = [w[k] for k in _SMALL] + [jnp.zeros((depth, 9, dff), F32)]
    small = [G["c_ctx"], jnp.stack(G["ada_b"]), jnp.stack(G["norm1_w"]), jnp.stack(G["sgu_ln_w"]), jnp.stack(G["sgu_ln_b"]),
             jnp.stack(G["sgu_w"]), jnp.stack(G["sgu_b"]), jnp.stack([-dh, dh]), jnp.stack(G["hnw"]), jnp.stack(G["norm2_w"]),
             jnp.stack(G["conv_b"]), G["final_norm_w"], jnp.stack(G["conv_w"])]
    n_small = sum(a.size for a in small)
    n_small_pad = _round_up(n_small, N_CORES * 16 * PACK_COLS)
    small_rows = n_small_pad // (N_CORES * PACK_COLS)
    small_rep = jnp.broadcast_to(_pack(small, n_small_pad).reshape(1, N_CORES, small_rows, PACK_COLS),
                                 (N_CHIPS, N_CORES, small_rows, PACK_COLS))
    small_parts = as_parts([small_rep])
    small_sums = [_sum_pair(small_parts[0], _reduce_pair(small_parts, "reduce_pair_small")[0], ids, "sum_pair_small")]
    groups["small"] = dict(l=None, keys=["small"], chips=_split_start(
        small_sums, [lax.empty(small_sums[0].shape, small_sums[0].dtype)], _plan_chips, _n_chips_copies(small_sums),
        "reduce_chips_start_small"))

    def gather_halves(halves, name):
        return dict(zip(halves, _gather_pair(list(halves.values()), name)))

    last = order[-1]
    halves = {}
    for tag in order[:-1]:
        halves.update(chips_finish(tag, groups["small"]["chips"]["token_array"]))
    reduced = gather_halves(halves, "gather_pair")
    grads, delta, new_m, new_v = {}, {}, {}, {}

    def adamw_sharded(i):
        k = _SHARDED[i]
        gs = g_ada if i == 0 else [reduced[(l, _LAYER_KEYS[i])].reshape(w[k].shape[1:]) for l in range(depth)]
        grads[k], delta[k], new_m[k], new_v[k] = _adamw(w[k], gs, mom[k], var[k], f"adamw_{k}")

    last_keys = groups[last]["keys"]
    for i in range(len(_SHARDED)):
        if _LAYER_KEYS[i] not in last_keys:
            adamw_sharded(i)
    halves = chips_finish(last, new_v[_SHARDED[-1]])
    halves.update(chips_finish("small", new_v[_SHARDED[-1]]))
    reduced.update(gather_halves(halves, "gather_pair_last"))
    for i in range(len(_SHARDED)):
        if _LAYER_KEYS[i] in last_keys:
            adamw_sharded(i)

    g_small = _unpack(reduced[(None, "small")].reshape(-1), small_like)
    grads.update(zip(_SMALL, g_small[:-1]))
    grads["ffn_conv_w"] = lax.dynamic_slice_in_dim(g_small[-1].reshape(depth, 3, 3, dff), chip * (dff // N_CHIPS),
                                                   dff // N_CHIPS, axis=3)
    packed = _SMALL + ("ffn_conv_w",)
    n_pad = _round_up(sum(w[k].size for k in packed), 16 * PACK_COLS)
    pack = lambda t: _pack([t[k] for k in packed], n_pad).reshape(1, -1, PACK_COLS)
    _, d, nm, nv = _adamw(pack(w), [pack(grads)[0]], pack(mom), pack(var), "adamw_packed")
    like = [w[k] for k in packed]
    for src, dst in ((d, delta), (nm, new_m), (nv, new_v)):
        dst.update(zip(packed, _unpack(src.reshape(-1), like)))

    return (loss, grad_x, *[grads[k] for k in _ORDER], *[delta[k] for k in _ORDER], *[new_m[k] for k in _ORDER],
            *[new_v[k] for k in _ORDER])
```

```python
import functools

import jax
import jax.numpy as jnp
from jax import lax
from jax.experimental import pallas as pl
from jax.experimental.pallas import tpu as pltpu

F32 = jnp.float32
BF16 = jnp.bfloat16

GRID_W = 64
HG_CHUNK = 64
SGU_CHUNK = 128
HEAD = 128
TB = 256
N_MOD = 6
RMS_EPS = 1e-6
LN_EPS = 1e-5
VMEM_LIMIT = 48 * 1024 * 1024
VMEM_LIMIT_PAIR = 58 * 1024 * 1024
N_CHIPS = 4
N_CORES = 2

ADAM_LR = 0.001
ADAM_B1 = 0.9
ADAM_B2 = 0.999
ADAM_EPS = 1e-08
ADAM_WD = 0.01
ADAM_STEP = 10

_GELU_C = 0.7978845608028654
_GELU_A = 0.044715


def _sigmoid(x):
    return 0.5 * jnp.tanh(0.5 * x) + 0.5


def _silu(x):
    return x * _sigmoid(x)


def _silu_both(x):
    s = _sigmoid(x)
    return x * s, s * (1.0 + x * (1.0 - s))


def _dsilu(x):
    return _silu_both(x)[1]


def _gelu_both(x):
    x2 = x * x
    t = jnp.tanh(_GELU_C * (x + _GELU_A * x2 * x))
    h = 0.5 * (1.0 + t)
    return x * h, h + 0.5 * x * (1.0 - t * t) * (_GELU_C + 3.0 * _GELU_C * _GELU_A * x2)


def _gelu(x):
    return 0.5 * x * (1.0 + jnp.tanh(_GELU_C * (x + _GELU_A * x * x * x)))


def _dot(a, b, ca, cb):
    return lax.dot_general(a, b, (((ca,), (cb,)), ((), ())), preferred_element_type=F32)


def _nn(a, b):
    return _dot(a, b, 1, 0)


def _nt(a, b):
    return _dot(a, b, 1, 1)


def _tn(a, b):
    return _dot(a, b, 0, 0)


def _params(*sem, vmem=VMEM_LIMIT):
    return pltpu.CompilerParams(dimension_semantics=sem if sem else None, vmem_limit_bytes=vmem)


def _stream_of(i, ctx_blocks):
    return (i >= ctx_blocks).astype(jnp.int32)


def _mm(a, b, mode, tm, tn, tk, out_dtype, name, b_chips=False, out_chips=False, vmem=VMEM_LIMIT):
    a_pair, b_pair = isinstance(a, tuple), isinstance(b, tuple)
    assert (not a_pair or mode == "nt") and (not b_pair or (mode == "tn" and not b_chips))
    ashape = (a[0].shape[0], 2 * a[0].shape[1]) if a_pair else a.shape
    if b_pair:
        bshape = (b[0].shape[0], 2 * b[0].shape[1])
    elif not b_chips:
        bshape = b.shape
    else:
        bshape = (b.shape[1], N_CHIPS * b.shape[2])
    if mode == "nn":
        (M, K), (K2, N) = ashape, bshape
    elif mode == "nt":
        (M, K), (N, K2) = ashape, bshape
    else:
        (K, M), (K2, N) = ashape, bshape
    assert K == K2 and M % tm == 0 and N % tn == 0 and K % tk == 0, (name, ashape, bshape, tm, tn, tk)
    nk = K // tk
    if a_pair:
        n1 = a[0].shape[1] // tk
        assert a[0].shape[1] % tk == 0
        a_specs = [pl.BlockSpec((tm, tk), lambda j, i, k: (i, jnp.minimum(k, n1 - 1))),
                   pl.BlockSpec((tm, tk), lambda j, i, k: (i, jnp.maximum(k - n1, 0)))]
    elif mode == "tn":
        a_specs = [pl.BlockSpec((tk, tm), lambda j, i, k: (k, i))]
    else:
        a_specs = [pl.BlockSpec((tm, tk), lambda j, i, k: (i, k))]
    if b_pair:
        n1 = b[0].shape[1] // tn
        assert b[0].shape[1] % tn == 0
        b_specs = [pl.BlockSpec((tk, tn), lambda j, i, k: (k, jnp.minimum(j, n1 - 1))),
                   pl.BlockSpec((tk, tn), lambda j, i, k: (k, jnp.maximum(j - n1, 0)))]
    elif not b_chips:
        if mode == "nt":
            b_spec = pl.BlockSpec((tn, tk), lambda j, i, k: (j, k))
        else:
            b_spec = pl.BlockSpec((tk, tn), lambda j, i, k: (k, j))
    else:
        cols = b.shape[2]
        if mode == "nn":
            per = cols // tn
            assert cols % tn == 0
            b_spec = pl.BlockSpec((None, tk, tn), lambda j, i, k: (j // per, k, j % per))
        else:
            per = cols // tk
            assert mode == "nt" and cols % tk == 0
            b_spec = pl.BlockSpec((None, tn, tk), lambda j, i, k: (k // per, j, k % per))
    if not b_pair:
        b_specs = [b_spec]
    if out_chips:
        per_o = (N // N_CHIPS) // tn
        assert (N // N_CHIPS) % tn == 0
        o_spec = pl.BlockSpec((None, tm, tn), lambda j, i, k: (j // per_o, i, j % per_o))
        o_shape = (N_CHIPS, M, N // N_CHIPS)
    else:
        o_spec = pl.BlockSpec((tm, tn), lambda j, i, k: (i, j))
        o_shape = (M, N)
    ca, cb = {"nn": (1, 0), "nt": (1, 1), "tn": (0, 0)}[mode]

    in_place = nk == 1
    na, nb = len(a_specs), len(b_specs)

    def body(*refs):
        a_refs, b_refs, rest = refs[:na], refs[na:na + nb], refs[na + nb:]
        if in_place:
            (o_ref,) = rest
        else:
            o_ref, acc = rest
        k = pl.program_id(2)

        if not in_place:
            @pl.when(k == 0)
            def _():
                acc[...] = jnp.zeros_like(acc)

        def multiply(which):
            part = _dot(a_refs[which if a_pair else 0][...], b_refs[which if b_pair else 0][...], ca, cb)
            if in_place:
                o_ref[...] = part.astype(out_dtype)
            else:
                acc[...] += part

        if a_pair or b_pair:
            first = (k < n1) if a_pair else (pl.program_id(0) < n1)
            pl.when(first)(functools.partial(multiply, 0))
            pl.when(jnp.logical_not(first))(functools.partial(multiply, 1))
        else:
            multiply(0)

        if not in_place:
            @pl.when(k == nk - 1)
            def _():
                o_ref[...] = acc[...].astype(out_dtype)

    ins = (list(a) if a_pair else [a]) + (list(b) if b_pair else [b])
    return pl.pallas_call(
        body, name=name, grid=(N // tn, M // tm, nk), in_specs=a_specs + b_specs, out_specs=o_spec,
        out_shape=jax.ShapeDtypeStruct(o_shape, out_dtype),
        scratch_shapes=[] if in_place else [pltpu.VMEM((tm, tn), F32)],
        compiler_params=_params("parallel", "parallel", "arbitrary", vmem=vmem),
    )(*ins)


def _tile(n, pref):
    if n <= pref:
        return n
    best = None
    for t in range(128, pref + 1, 128):
        if n % t == 0:
            best = t
    assert best is not None, (n, pref)
    return best


def _rows_tile(n, pref):
    if n <= pref:
        return n
    best = None
    for t in range(16, pref + 1, 16):
        if n % t == 0:
            best = t
    assert best is not None, (n, pref)
    return best


def _mm_nn_w(a, wg, out_dtype, name):
    M, K = a.shape
    return _mm(a, wg, "nn", _rows_tile(M, 2176), _tile(wg.shape[2], 1536), _tile(K, 1536), out_dtype, name, b_chips=True)


def _mm_nt_w(a, wg, out_dtype, name):
    M = a[0].shape[0] if isinstance(a, tuple) else a.shape[0]
    return _mm(a, wg, "nt", _rows_tile(M, 1088), _tile(wg.shape[1], 1024), _tile(wg.shape[2], 1536), out_dtype, name,
               b_chips=True)


def _mm_tn(a, b, out_dtype, name, out_chips=False):
    K, M = a.shape
    N = 2 * b[0].shape[1] if isinstance(b, tuple) else b.shape[1]
    ncol = N // N_CHIPS if out_chips else N
    tm, tn = _tile(M, 1408), _tile(ncol, 1408)
    if tm * tn > 1408 * 1152:
        tn = _tile(ncol, 1152)
    vmem = VMEM_LIMIT_PAIR if isinstance(b, tuple) else VMEM_LIMIT
    return _mm(a, b, "tn", tm, tn, _rows_tile(K, 2176), out_dtype, name, out_chips=out_chips, vmem=vmem)


def _mod_fwd(cv, wg, b, name):
    R, D = cv.shape
    tn = wg.shape[2]
    N = N_CHIPS * tn

    def body(cv_ref, w_ref, b_ref, mod_ref, sa_ref):
        sa = _silu(cv_ref[...]).astype(BF16)
        sa_ref[...] = sa
        mod_ref[...] = _nn(sa, w_ref[...]) + b_ref[...]

    return pl.pallas_call(
        body, name=name, grid=(N_CHIPS,),
        in_specs=[pl.BlockSpec((R, D), lambda j: (0, 0)), pl.BlockSpec((None, D, tn), lambda j: (j, 0, 0)),
                  pl.BlockSpec((1, tn), lambda j: (0, j))],
        out_specs=[pl.BlockSpec((R, tn), lambda j: (0, j)), pl.BlockSpec((R, D), lambda j: (0, 0))],
        out_shape=[jax.ShapeDtypeStruct((R, N), F32), jax.ShapeDtypeStruct((R, D), BF16)],
        compiler_params=_params("arbitrary"),
    )(cv, wg, b)


def _cvec_bwd(dmod, wg, cv, name):
    R, N = dmod.shape
    D = wg.shape[1]
    tk = wg.shape[2]
    nk = N_CHIPS

    def body(dm_ref, w_ref, cv_ref, o_ref):
        k = pl.program_id(0)

        @pl.when(k == 0)
        def _():
            o_ref[...] = jnp.zeros_like(o_ref)

        o_ref[...] += _nt(dm_ref[...].astype(BF16), w_ref[...])

        @pl.when(k == nk - 1)
        def _():
            o_ref[...] = o_ref[...] * _dsilu(cv_ref[...])

    return pl.pallas_call(
        body, name=name, grid=(nk,),
        in_specs=[pl.BlockSpec((R, tk), lambda k: (0, k)), pl.BlockSpec((None, D, tk), lambda k: (k, 0, 0)),
                  pl.BlockSpec((R, D), lambda k: (0, 0))],
        out_specs=pl.BlockSpec((R, D), lambda k: (0, 0)),
        out_shape=jax.ShapeDtypeStruct((R, D), F32),
        compiler_params=_params("arbitrary"),
    )(dmod, wg, cv)


def _norm_mod(x, nw, mod, which, ctx_rows, name):
    T, D = x.shape
    cb = ctx_rows // TB

    def body(x_ref, nw_ref, mod_ref, h_ref):
        xv = x_ref[...]
        r = lax.rsqrt(jnp.mean(xv * xv, axis=-1, keepdims=True) + RMS_EPS)
        y = xv * r * nw_ref[...]
        sh = mod_ref[which:which + 1, :]
        sc = mod_ref[which + 1:which + 2, :]
        h_ref[...] = (y * (1.0 + sc) + sh).astype(BF16)

    return pl.pallas_call(
        body, name=name, grid=(T // TB,),
        in_specs=[pl.BlockSpec((TB, D), lambda i: (i, 0)), pl.BlockSpec((1, D), lambda i: (0, 0)),
                  pl.BlockSpec((None, N_MOD, D), lambda i: (_stream_of(i, cb), 0, 0))],
        out_specs=pl.BlockSpec((TB, D), lambda i: (i, 0)),
        out_shape=jax.ShapeDtypeStruct((T, D), BF16),
        compiler_params=_params("parallel"),
    )(x, nw, mod)


def _norm_mod_bwd(dh, x, dres, nw, mod, which, ctx_rows, name):
    T, D = x.shape
    cb = ctx_rows // TB

    def body(dh_ref, x_ref, dres_ref, nw_ref, mod_ref, dx_ref, dm_ref, dnw_ref):
        i = pl.program_id(0)

        @pl.when(i == 0)
        def _():
            dnw_ref[...] = jnp.zeros_like(dnw_ref)

        @pl.when((i == 0) | (i == cb))
        def _():
            dm_ref[...] = jnp.zeros_like(dm_ref)

        xv = x_ref[...]
        dh = dh_ref[...]
        r = lax.rsqrt(jnp.mean(xv * xv, axis=-1, keepdims=True) + RMS_EPS)
        xh = xv * r
        nwv = nw_ref[...]
        sc = mod_ref[which + 1:which + 2, :]
        y = xh * nwv
        dm_ref[0:1, :] += jnp.sum(dh, axis=0, keepdims=True)
        dm_ref[1:2, :] += jnp.sum(dh * y, axis=0, keepdims=True)
        dy = dh * (1.0 + sc)
        dnw_ref[...] += jnp.sum(dy * xh, axis=0, keepdims=True)
        dxh = dy * nwv
        dx_ref[...] = dres_ref[...] + r * (dxh - xh * jnp.mean(dxh * xh, axis=-1, keepdims=True))

    return pl.pallas_call(
        body, name=name, grid=(T // TB,),
        in_specs=[pl.BlockSpec((TB, D), lambda i: (i, 0)), pl.BlockSpec((TB, D), lambda i: (i, 0)),
                  pl.BlockSpec((TB, D), lambda i: (i, 0)), pl.BlockSpec((1, D), lambda i: (0, 0)),
                  pl.BlockSpec((None, N_MOD, D), lambda i: (_stream_of(i, cb), 0, 0))],
        out_specs=[pl.BlockSpec((TB, D), lambda i: (i, 0)),
                   pl.BlockSpec((None, 2, D), lambda i: (_stream_of(i, cb), 0, 0)),
                   pl.BlockSpec((1, D), lambda i: (0, 0))],
        out_shape=[jax.ShapeDtypeStruct((T, D), F32), jax.ShapeDtypeStruct((2, 2, D), F32),
                   jax.ShapeDtypeStruct((1, D), F32)],
        compiler_params=_params("arbitrary"),
    )(dh, x, dres, nw, mod)


def _scan_chunk(n, rev, n_ctx, n_all):
    if not rev:
        return n
    return jnp.where(n < n_ctx, n_ctx - 1 - n, n_all - 1 + n_ctx - n)


def _cumsum_rows(x, rev):
    rows = x.shape[0]
    row = lax.broadcasted_iota(jnp.int32, (rows, 1), 0)
    s = 1
    while s < rows:
        if not rev:
            x = x + jnp.where(row >= s, pltpu.roll(x, s, 0), 0.0)
        else:
            x = x + jnp.where(row < rows - s, pltpu.roll(x, rows - s, 0), 0.0)
        s *= 2
    return x


def _lower_bound(hlb_ref, layer):
    h = hlb_ref[...]
    if layer == 0:
        return jnp.zeros_like(h[0:1, :])
    return _sigmoid(h[1:2, :] - h[0:1, :])


HG_STEP = 4


def _step_rows(j, rev, backward):
    sub = j if rev == backward else HG_STEP - 1 - j
    return slice(sub * HG_CHUNK, (sub + 1) * HG_CHUNK)


def _hgrn_gates(q_ref, f_ref, hlb_ref, layer, rev, rows):
    lb = _lower_bound(hlb_ref, layer)
    z = f_ref[rows, :]
    sig = 1.0 / (1.0 + jnp.exp(-z))
    fg = lb + (1.0 - lb) * sig
    kk = (1.0 - lb) * (1.0 - sig)
    g = jnp.log(fg)
    b = _cumsum_rows(g, rev)
    bt = jnp.sum(g, axis=0, keepdims=True)
    mid = HG_CHUNK // 2
    r = b[mid:mid + 1, :] if rev else b[mid - 1:mid, :]
    qh = _silu(q_ref[rows, :])
    return lb, sig, fg, kk, b, bt, r, qh


def _tri_mask(rev):
    t = lax.broadcasted_iota(jnp.int32, (HG_CHUNK, HG_CHUNK), 0)
    s = lax.broadcasted_iota(jnp.int32, (HG_CHUNK, HG_CHUNK), 1)
    return (s >= t) if rev else (s <= t)


def _hgrn_fwd(parts, hlb, layer, rev, ctx_rows, name, o_add=None):
    T = parts.shape[0]
    D = hlb.shape[1] // 2
    nh = D // HEAD
    n_all, n_ctx = T // HG_CHUNK, ctx_rows // HG_CHUNK
    assert n_all % HG_STEP == 0 and n_ctx % HG_STEP == 0
    n_steps = n_all // HG_STEP
    block = functools.partial(_scan_chunk, rev=rev, n_ctx=n_ctx // HG_STEP, n_all=n_steps)
    fcol = 2 if rev else 1

    def body(q_ref, f_ref, i_ref, hlb_ref, *rest):
        if o_add is None:
            o_ref, st_ref, s_scr = rest
        else:
            oa_ref, o_ref, st_ref, s_scr = rest
        n = pl.program_id(0)

        @pl.when(n == 0)
        def _():
            s_scr[...] = jnp.zeros_like(s_scr)

        mask = _tri_mask(rev)
        hs = [slice(h * HEAD, (h + 1) * HEAD) for h in range(nh)]
        for j in range(HG_STEP):
            rows = _step_rows(j, rev, False)
            lb, sig, fg, kk, b, bt, r, qh = _hgrn_gates(q_ref, f_ref, hlb_ref, layer, rev, rows)
            qr = (qh * jnp.exp(b - r)).astype(BF16)
            kr = (kk * jnp.exp(r - b)).astype(BF16)
            qe = (qh * jnp.exp(b)).astype(BF16)
            ke = (kk * jnp.exp(bt - b)).astype(BF16)
            dec = jnp.exp(bt)
            v = i_ref[rows, :].astype(BF16)
            st = [s_scr[h] for h in range(nh)]
            a_raw = [_nt(qr[:, sl], kr[:, sl]) for sl in hs]
            o_int = [_nt(qe[:, sl], st[h].astype(BF16)) for h, sl in enumerate(hs)]
            kv = [_tn(v[:, sl], ke[:, sl]) for sl in hs]
            for h, sl in enumerate(hs):
                st_ref[j, h] = st[h]
                o = _nn(jnp.where(mask, a_raw[h], 0.0).astype(BF16), v[:, sl]) + o_int[h]
                if o_add is not None:
                    o = o + oa_ref[rows, sl]
                o_ref[rows, sl] = o
                s_scr[h] = st[h] * dec[:, sl] + kv[h]

    cspec = lambda col: pl.BlockSpec((HG_STEP * HG_CHUNK, D), lambda n: (block(n), col))
    ins = [parts, parts, parts, hlb]
    specs = [cspec(0), cspec(fcol), cspec(3), pl.BlockSpec((2, D), lambda n: (0, 1 if rev else 0))]
    if o_add is not None:
        ins.append(o_add)
        specs.append(cspec(0))
    return pl.pallas_call(
        body, name=name, grid=(n_steps,), in_specs=specs,
        out_specs=[cspec(0), pl.BlockSpec((HG_STEP, nh, HEAD, HEAD), lambda n: (n, 0, 0, 0))],
        out_shape=[jax.ShapeDtypeStruct((T, D), F32), jax.ShapeDtypeStruct((n_all, nh, HEAD, HEAD), F32)],
        scratch_shapes=[pltpu.VMEM((nh, HEAD, HEAD), F32)],
        compiler_params=_params("arbitrary"),
    )(*ins)


def _hgrn_bwd(parts, hlb, do, states, layer, rev, ctx_rows, name, other=None, dparts=None):
    T = parts.shape[0]
    D = hlb.shape[1] // 2
    nh = D // HEAD
    n_all, n_ctx = T // HG_CHUNK, ctx_rows // HG_CHUNK
    assert n_all % HG_STEP == 0 and n_ctx % HG_STEP == 0
    n_steps = n_all // HG_STEP
    step = lambda m: n_steps - 1 - m
    block = lambda m: _scan_chunk(step(m), rev, n_ctx // HG_STEP, n_steps)
    fcol = 2 if rev else 1
    has_add = other is not None
    assert not has_add or rev

    def body(q_ref, f_ref, i_ref, hlb_ref, do_ref, st_ref, *rest):
        if has_add:
            dqa_ref, dza_ref, dia_ref, _, out_ref, dlb_ref, ds_scr = rest
            dq_ref, dz_ref, di_ref = out_ref.at[:, 0:D], out_ref.at[:, 2 * D:3 * D], out_ref.at[:, 3 * D:4 * D]
            out_ref[:, D:2 * D] = dza_ref[...]
        else:
            dq_ref, dz_ref, di_ref, dlb_ref, ds_scr = rest
        m = pl.program_id(0)

        @pl.when(m == 0)
        def _():
            ds_scr[...] = jnp.zeros_like(ds_scr)
            dlb_ref[...] = jnp.zeros_like(dlb_ref)

        mask = _tri_mask(rev)
        hs = [slice(h * HEAD, (h + 1) * HEAD) for h in range(nh)]
        for j in range(HG_STEP):
            rows = _step_rows(j, rev, True)
            slot = HG_STEP - 1 - j
            lb, sig, fg, kk, b, bt, r, qh = _hgrn_gates(q_ref, f_ref, hlb_ref, layer, rev, rows)
            e_qr = jnp.exp(b - r)
            e_kr = jnp.exp(r - b)
            e_b = jnp.exp(b)
            e_ke = jnp.exp(bt - b)
            dec = jnp.exp(bt)
            qr = (qh * e_qr).astype(BF16)
            kr = (kk * e_kr).astype(BF16)
            qe = (qh * e_b).astype(BF16)
            ke = (kk * e_ke).astype(BF16)
            v = i_ref[rows, :].astype(BF16)
            dov = do_ref[rows, :].astype(BF16)
            st = [st_ref[slot, h] for h in range(nh)]
            dst = [ds_scr[h] for h in range(nh)]
            stb = [t.astype(BF16) for t in st]
            dstb = [t.astype(BF16) for t in dst]
            a_raw = [_nt(qr[:, sl], kr[:, sl]) for sl in hs]
            da_raw = [_nt(dov[:, sl], v[:, sl]) for sl in hs]
            dq_int = [_nn(dov[:, sl], stb[h]) for h, sl in enumerate(hs)]
            dk_int = [_nn(v[:, sl], dstb[h]) for h, sl in enumerate(hs)]
            dv_int = [_nt(ke[:, sl], dstb[h]) for h, sl in enumerate(hs)]
            ds_new = [_tn(dov[:, sl], qe[:, sl]) for sl in hs]
            a = [jnp.where(mask, t, 0.0).astype(BF16) for t in a_raw]
            da = [jnp.where(mask, t, 0.0).astype(BF16) for t in da_raw]
            dv_parts = [_tn(a[h], dov[:, sl]) + dv_int[h] for h, sl in enumerate(hs)]
            dq_parts = [_nn(da[h], kr[:, sl]) * e_qr[:, sl] + dq_int[h] * e_b[:, sl] for h, sl in enumerate(hs)]
            dki_parts = [dk_int[h] * e_ke[:, sl] for h, sl in enumerate(hs)]
            dk_parts = [_tn(da[h], qr[:, sl]) * e_kr[:, sl] + dki_parts[h] for h, sl in enumerate(hs)]
            dbt_parts = [dec[:, sl] * jnp.sum(st[h] * dst[h], axis=0, keepdims=True) for h, sl in enumerate(hs)]
            for h, sl in enumerate(hs):
                ds_scr[h] = dst[h] * dec[:, sl] + ds_new[h]
            dq = jnp.concatenate(dq_parts, axis=1)
            dk = jnp.concatenate(dk_parts, axis=1)
            dki = jnp.concatenate(dki_parts, axis=1)
            dv = jnp.concatenate(dv_parts, axis=1)
            dbt = jnp.concatenate(dbt_parts, axis=1) + jnp.sum(kk * dki, axis=0, keepdims=True)
            db = qh * dq - kk * dk
            dg = _cumsum_rows(db, not rev) + dbt
            df = dg / fg - dk
            dz_ref[rows, :] = (df * (1.0 - lb) * sig * (1.0 - sig)).astype(BF16)
            dlb_ref[...] += jnp.sum(df * (1.0 - sig), axis=0, keepdims=True)
            dqr = dq * _dsilu(q_ref[rows, :])
            if has_add:
                dqr = dqr + dqa_ref[rows, :]
                dv = dv + dia_ref[rows, :]
            dq_ref[rows, :] = dqr.astype(dq_ref.dtype)
            di_ref[rows, :] = dv.astype(di_ref.dtype)

        @pl.when(m == n_steps - 1)
        def _():
            lb = _lower_bound(hlb_ref, layer)
            if layer == 0:
                dlb_ref[...] = jnp.zeros_like(dlb_ref)
            else:
                dlb_ref[...] = dlb_ref[...] * lb * (1.0 - lb)

    cspec = lambda col: pl.BlockSpec((HG_STEP * HG_CHUNK, D), lambda m: (block(m), col))
    ins = [parts, parts, parts, hlb, do, states]
    specs = [cspec(0), cspec(fcol), cspec(3), pl.BlockSpec((2, D), lambda m: (0, 1 if rev else 0)), cspec(0),
             pl.BlockSpec((HG_STEP, nh, HEAD, HEAD), lambda m: (step(m), 0, 0, 0))]
    dlb_spec = pl.BlockSpec((1, D), lambda m: (0, 0))
    dlb_shape = jax.ShapeDtypeStruct((1, D), F32)
    if has_add:
        return pl.pallas_call(
            body, name=name, grid=(n_steps,),
            in_specs=specs + [cspec(0), cspec(0), cspec(0), pl.BlockSpec(memory_space=pl.ANY)],
            out_specs=[pl.BlockSpec((HG_STEP * HG_CHUNK, 4 * D), lambda m: (block(m), 0)), dlb_spec],
            out_shape=[jax.ShapeDtypeStruct(dparts.shape, dparts.dtype), dlb_shape],
            scratch_shapes=[pltpu.VMEM((nh, HEAD, HEAD), F32)], input_output_aliases={len(ins) + 3: 0},
            compiler_params=_params("arbitrary"),
        )(*ins, *other, dparts)
    return pl.pallas_call(
        body, name=name, grid=(n_steps,), in_specs=specs,
        out_specs=[cspec(0), cspec(0), cspec(0), dlb_spec],
        out_shape=[jax.ShapeDtypeStruct((T, D), F32), jax.ShapeDtypeStruct((T, D), BF16),
                   jax.ShapeDtypeStruct((T, D), F32), dlb_shape],
        scratch_shapes=[pltpu.VMEM((nh, HEAD, HEAD), F32)],
        compiler_params=_params("arbitrary"),
    )(*ins)


def _sgu_ln(gv, lnw_ref, lnb_ref):
    mu = jnp.mean(gv, axis=-1, keepdims=True)
    xc = gv - mu
    rstd = lax.rsqrt(jnp.mean(xc * xc, axis=-1, keepdims=True) + LN_EPS)
    xh = xc * rstd
    return xh, rstd, xh * lnw_ref[...] + lnb_ref[...]


def _sgu_fwd(parts, lnw, lnb, w, bt, name):
    T = parts.shape[0]
    D = lnw.shape[1]
    G = D // HEAD

    def body(u_ref, v_ref, lnw_ref, lnb_ref, w_ref, bt_ref, ya_ref):
        gu = _gelu(u_ref[...])
        _, _, vn = _sgu_ln(_gelu(v_ref[...]), lnw_ref, lnb_ref)
        vnb = vn.astype(BF16)
        for g in range(G):
            sl = slice(g * HEAD, (g + 1) * HEAD)
            mixed = _nn(w_ref[g], vnb[:, sl]) + bt_ref[:, g:g + 1]
            ya_ref[:, sl] = (gu[:, sl] * mixed).astype(BF16)

    return pl.pallas_call(
        body, name=name, grid=(T // SGU_CHUNK,),
        in_specs=[pl.BlockSpec((SGU_CHUNK, D), lambda n: (n, 4)), pl.BlockSpec((SGU_CHUNK, D), lambda n: (n, 5)),
                  pl.BlockSpec((1, D), lambda n: (0, 0)), pl.BlockSpec((1, D), lambda n: (0, 0)),
                  pl.BlockSpec((G, SGU_CHUNK, SGU_CHUNK), lambda n: (0, 0, 0)),
                  pl.BlockSpec((SGU_CHUNK, G), lambda n: (0, 0))],
        out_specs=pl.BlockSpec((SGU_CHUNK, D), lambda n: (n, 0)),
        out_shape=jax.ShapeDtypeStruct((T, D), BF16),
        compiler_params=_params("parallel"),
    )(parts, parts, lnw, lnb, w, bt)


def _sgu_bwd(parts, dya, lnw, lnb, w, bt, dparts, name):
    T = parts.shape[0]
    D = lnw.shape[1]
    G = D // HEAD

    def body(u_ref, v_ref, dya_ref, lnw_ref, lnb_ref, w_ref, bt_ref, dparts_in,
             duv_ref, dw_ref, dbt_ref, dlnw_ref, dlnb_ref, dvn_scr):
        du_ref = duv_ref.at[:, 0:D]
        dv_ref = duv_ref.at[:, D:2 * D]
        n = pl.program_id(0)

        @pl.when(n == 0)
        def _():
            dw_ref[...] = jnp.zeros_like(dw_ref)
            dbt_ref[...] = jnp.zeros_like(dbt_ref)
            dlnw_ref[...] = jnp.zeros_like(dlnw_ref)
            dlnb_ref[...] = jnp.zeros_like(dlnb_ref)

        gu, dgu = _gelu_both(u_ref[...])
        gv, dgv_dv = _gelu_both(v_ref[...])
        xh, rstd, vn = _sgu_ln(gv, lnw_ref, lnb_ref)
        vnb = vn.astype(BF16)
        dya = dya_ref[...]
        lane = lax.broadcasted_iota(jnp.int32, (SGU_CHUNK, G), 1)
        dbt = jnp.zeros((SGU_CHUNK, G), F32)
        for g in range(G):
            sl = slice(g * HEAD, (g + 1) * HEAD)
            wg = w_ref[g]
            mixed = _nn(wg, vnb[:, sl]) + bt_ref[:, g:g + 1]
            dmix = dya[:, sl] * gu[:, sl]
            du_ref[:, sl] = (dya[:, sl] * mixed * dgu[:, sl]).astype(BF16)
            dmb = dmix.astype(BF16)
            dvn_scr[:, sl] = _tn(wg, dmb)
            dw_ref[g] += _nt(dmb, vnb[:, sl])
            dbt = dbt + jnp.where(lane == g, jnp.sum(dmix, axis=1, keepdims=True), 0.0)
        dbt_ref[...] += dbt
        dvn = dvn_scr[...]
        dlnw_ref[...] += jnp.sum(dvn * xh, axis=0, keepdims=True)
        dlnb_ref[...] += jnp.sum(dvn, axis=0, keepdims=True)
        dxh = dvn * lnw_ref[...]
        dgv = rstd * (dxh - jnp.mean(dxh, axis=-1, keepdims=True) - xh * jnp.mean(dxh * xh, axis=-1, keepdims=True))
        dv_ref[...] = (dgv * dgv_dv).astype(BF16)

    row = lambda col: pl.BlockSpec((SGU_CHUNK, D), lambda n: (n, col))
    vec = pl.BlockSpec((1, D), lambda n: (0, 0))
    wsp = pl.BlockSpec((G, SGU_CHUNK, SGU_CHUNK), lambda n: (0, 0, 0))
    bsp = pl.BlockSpec((SGU_CHUNK, G), lambda n: (0, 0))
    return pl.pallas_call(
        body, name=name, grid=(T // SGU_CHUNK,),
        in_specs=[row(4), row(5), row(0), vec, vec, wsp, bsp, pl.BlockSpec(memory_space=pl.ANY)],
        out_specs=[pl.BlockSpec((SGU_CHUNK, 2 * D), lambda n: (n, 2)), wsp, bsp, vec, vec],
        out_shape=[jax.ShapeDtypeStruct(dparts.shape, dparts.dtype),
                   jax.ShapeDtypeStruct((G, SGU_CHUNK, SGU_CHUNK), F32), jax.ShapeDtypeStruct((SGU_CHUNK, G), F32),
                   jax.ShapeDtypeStruct((1, D), F32), jax.ShapeDtypeStruct((1, D), F32)],
        scratch_shapes=[pltpu.VMEM((SGU_CHUNK, D), F32)], input_output_aliases={7: 0},
        compiler_params=_params("arbitrary"),
    )(parts, parts, dya, lnw, lnb, w, bt, dparts)


TBT = 256
VMEM_LIMIT_TOKEN_OUT = 58 * 1024 * 1024


def _rows_weight_spec(wg):
    return pl.BlockSpec(wg.shape, lambda i: (0, 0, 0))


def _full(w_ref):
    return w_ref[...].reshape(w_ref.shape[0] * w_ref.shape[1], w_ref.shape[2])


def _token_out_fwd(o, parts, ya, x, mod, hnw, nw2, wa, wb, wo, ctx_rows, name):
    T, D = x.shape
    nh = D // HEAD
    cb = ctx_rows // TBT

    def body(o_ref, og_ref, ga_ref, gb_ref, ya_ref, x_ref, mod_ref, hnw_ref, nw2_ref, wa_ref, wb_ref, wo_ref,
             yb_ref, pa_ref, pb_ref, mg_ref, tmo_ref, xm_ref, h2_ref):
        ov = o_ref[...]
        so = _silu(og_ref[...])
        nw = hnw_ref[...]
        for h in range(nh):
            sl = slice(h * HEAD, (h + 1) * HEAD)
            seg = ov[:, sl]
            r = lax.rsqrt(jnp.mean(seg * seg, axis=-1, keepdims=True) + RMS_EPS)
            yb_ref[:, sl] = (seg * r * nw * so[:, sl]).astype(BF16)
        pa = _nn(ya_ref[...], _full(wa_ref))
        pb = _nn(yb_ref[...], _full(wb_ref))
        pa_ref[...] = pa
        pb_ref[...] = pb
        mg = (_sigmoid(ga_ref[...]) * pa + _sigmoid(gb_ref[...]) * pb).astype(BF16)
        mg_ref[...] = mg
        out = _nn(mg, _full(wo_ref))
        tmo_ref[...] = out
        xm = x_ref[...] + mod_ref[2:3, :] * out
        xm_ref[...] = xm
        r = lax.rsqrt(jnp.mean(xm * xm, axis=-1, keepdims=True) + RMS_EPS)
        h2_ref[...] = (xm * r * nw2_ref[...] * (1.0 + mod_ref[4:5, :]) + mod_ref[3:4, :]).astype(BF16)

    row = lambda col: pl.BlockSpec((TBT, D), lambda i: (i, col))
    wsp = _rows_weight_spec(wa)
    sd = lambda dt: jax.ShapeDtypeStruct((T, D), dt)
    return pl.pallas_call(
        body, name=name, grid=(T // TBT,),
        in_specs=[row(0), row(6), row(7), row(8), row(0), row(0),
                  pl.BlockSpec((None, N_MOD, D), lambda i: (_stream_of(i, cb), 0, 0)),
                  pl.BlockSpec((1, HEAD), lambda i: (0, 0)), pl.BlockSpec((1, D), lambda i: (0, 0)), wsp, wsp, wsp],
        out_specs=[row(0)] * 7,
        out_shape=[sd(BF16), sd(F32), sd(F32), sd(BF16), sd(F32), sd(F32), sd(BF16)],
        compiler_params=_params("parallel", vmem=VMEM_LIMIT_TOKEN_OUT),
    )(o, parts, parts, parts, ya, x, mod, hnw, nw2, wa, wb, wo)


def _token_out_bwd(dx, tmo, pa, pb, o, parts, mod, hnw, wa, wb, wo, ctx_rows, name):
    T, D = dx.shape
    nh = D // HEAD
    cb = ctx_rows // TBT

    def body(dx_ref, tmo_ref, pa_ref, pb_ref, o_ref, og_ref, ga_ref, gb_ref, mod_ref, hnw_ref, wa_ref, wb_ref, wo_ref,
             dout_ref, dpa_ref, dpb_ref, dgate_ref, dya_ref, do_ref, dg1_ref, dhnw_ref):
        i = pl.program_id(0)

        @pl.when(i == 0)
        def _():
            dhnw_ref[...] = jnp.zeros_like(dhnw_ref)

        @pl.when((i == 0) | (i == cb))
        def _():
            dg1_ref[...] = jnp.zeros_like(dg1_ref)

        dxv = dx_ref[...]
        dg1_ref[...] += jnp.sum(dxv * tmo_ref[...], axis=0, keepdims=True)
        dout = (dxv * mod_ref[2:3, :]).astype(BF16)
        dout_ref[...] = dout
        dmg = _nt(dout, _full(wo_ref))
        sa = _sigmoid(ga_ref[...])
        sb = _sigmoid(gb_ref[...])
        dpa = (dmg * sa).astype(BF16)
        dpb = (dmg * sb).astype(BF16)
        dpa_ref[...] = dpa
        dpb_ref[...] = dpb
        dgate_ref[:, D:2 * D] = (dmg * pa_ref[...] * sa * (1.0 - sa)).astype(BF16)
        dgate_ref[:, 2 * D:3 * D] = (dmg * pb_ref[...] * sb * (1.0 - sb)).astype(BF16)
        dya_ref[...] = _nt(dpa, _full(wa_ref))
        dyb = _nt(dpb, _full(wb_ref))
        so, dso = _silu_both(og_ref[...])
        ov = o_ref[...]
        nw = hnw_ref[...]
        dnw = jnp.zeros((1, HEAD), F32)
        for h in range(nh):
            sl = slice(h * HEAD, (h + 1) * HEAD)
            seg = ov[:, sl]
            r = lax.rsqrt(jnp.mean(seg * seg, axis=-1, keepdims=True) + RMS_EPS)
            oh = seg * r
            dn = dyb[:, sl] * so[:, sl]
            dgate_ref[:, sl] = (dyb[:, sl] * oh * nw * dso[:, sl]).astype(BF16)
            dnw = dnw + jnp.sum(dn * oh, axis=0, keepdims=True)
            doh = dn * nw
            do_ref[:, sl] = r * (doh - oh * jnp.mean(doh * oh, axis=-1, keepdims=True))
        dhnw_ref[...] += dnw

    row = lambda col: pl.BlockSpec((TBT, D), lambda i: (i, col))
    wsp = _rows_weight_spec(wa)
    sd = lambda dt: jax.ShapeDtypeStruct((T, D), dt)
    return pl.pallas_call(
        body, name=name, grid=(T // TBT,),
        in_specs=[row(0), row(0), row(0), row(0), row(0), row(6), row(7), row(8),
                  pl.BlockSpec((None, N_MOD, D), lambda i: (_stream_of(i, cb), 0, 0)),
                  pl.BlockSpec((1, HEAD), lambda i: (0, 0)), wsp, wsp, wsp],
        out_specs=[row(0)] * 3 + [pl.BlockSpec((TBT, 3 * D), lambda i: (i, 2)), row(0), row(0),
                                  pl.BlockSpec((None, 1, D), lambda i: (_stream_of(i, cb), 0, 0)),
                                  pl.BlockSpec((1, HEAD), lambda i: (0, 0))],
        out_shape=[sd(BF16)] * 3 + [jax.ShapeDtypeStruct((T, 9 * D), BF16), sd(F32), sd(F32),
                                    jax.ShapeDtypeStruct((2, 1, D), F32), jax.ShapeDtypeStruct((1, HEAD), F32)],
        compiler_params=_params("arbitrary", vmem=VMEM_LIMIT_TOKEN_OUT),
    )(dx, tmo, pa, pb, o, parts, parts, parts, mod, hnw, wa, wb, wo)


def _conv_geometry(i, nb, cb):
    is_ctx = i < cb
    first = (i == 0) | (i == cb)
    last = (i == cb - 1) | (i == nb - 1)
    row = lax.broadcasted_iota(jnp.int32, (TB + 2 * GRID_W, 1), 0)
    w = row & (GRID_W - 1)
    left_ok = (w != 0) | is_ctx
    right_ok = (w != GRID_W - 1) | is_ctx
    return is_ctx, first, last, left_ok, right_ok


def _ext(p_ref, m_ref, n_ref, first, last):
    return jnp.concatenate([jnp.where(first, 0.0, p_ref[...]), m_ref[...], jnp.where(last, 0.0, n_ref[...])], axis=0)


def _shift_prev(e, ok):
    return jnp.where(ok, pltpu.roll(e, 1, 0), 0.0)


def _shift_next(e, ok):
    return jnp.where(ok, pltpu.roll(e, e.shape[0] - 1, 0), 0.0)


def _halo_specs(cbk, n64, coff=0):
    r = TB // GRID_W
    prev = pl.BlockSpec((GRID_W, cbk), lambda j, i: (jnp.maximum(r * i - 1, 0), j + coff))
    main = pl.BlockSpec((TB, cbk), lambda j, i: (i, j + coff))
    nxt = pl.BlockSpec((GRID_W, cbk), lambda j, i: (jnp.minimum(r * i + r, n64 - 1), j + coff))
    return [prev, main, nxt]


def _conv_cblock(dff):
    return _tile(dff, 1408)


def _conv_fwd(up, cw, cbias, ctx_rows, name):
    T, dff = up.shape[0], up.shape[1] // 2
    cbk = _conv_cblock(dff)
    nb, cb = T // TB, ctx_rows // TB
    nvb = dff // cbk

    def body(ap_ref, a_ref, an_ref, v_ref, cw_ref, cb_ref, ac_ref, act_ref):
        i = pl.program_id(1)
        is_ctx, first, last, lok, rok = _conv_geometry(i, nb, cb)
        e = _ext(ap_ref, a_ref, an_ref, first, last)
        el = _shift_prev(e, lok)
        er = _shift_next(e, rok)
        cwv = cw_ref[...]

        def comb(dr, lo):
            sl = slice(lo, lo + TB)
            return cwv[3 * dr:3 * dr + 1] * el[sl] + cwv[3 * dr + 1:3 * dr + 2] * e[sl] + cwv[3 * dr + 2:3 * dr + 3] * er[sl]

        out = comb(1, GRID_W) + jnp.where(is_ctx, 0.0, comb(0, 0) + comb(2, 2 * GRID_W))
        a_c = out + cb_ref[...]
        ac_ref[...] = a_c
        act_ref[...] = (_gelu(a_c) * v_ref[...]).astype(BF16)

    main = pl.BlockSpec((TB, cbk), lambda j, i: (i, j))
    return pl.pallas_call(
        body, name=name, grid=(dff // cbk, nb),
        in_specs=_halo_specs(cbk, T // GRID_W) + [pl.BlockSpec((TB, cbk), lambda j, i: (i, j + nvb)),
                                                 pl.BlockSpec((9, cbk), lambda j, i: (0, j)),
                                                 pl.BlockSpec((1, cbk), lambda j, i: (0, j))],
        out_specs=[main, main],
        out_shape=[jax.ShapeDtypeStruct((T, dff), F32), jax.ShapeDtypeStruct((T, dff), BF16)],
        compiler_params=_params("parallel", "parallel"),
    )(up, up, up, up, cw, cbias)


def _conv_bwd(up, ac, dact, cw, ctx_rows, name):
    T, dff = up.shape[0], up.shape[1] // 2
    cbk = _conv_cblock(dff)
    nb, cb = T // TB, ctx_rows // TB
    nvb = dff // cbk

    def body(ap_ref, a_ref, an_ref, vp_ref, v_ref, vn_ref, cp_ref, c_ref, cn_ref, dp_ref, d_ref, dn_ref, cw_ref,
             da_ref, dv_ref, dcw_ref, dcb_ref):
        i = pl.program_id(1)

        @pl.when(i == 0)
        def _():
            dcw_ref[...] = jnp.zeros_like(dcw_ref)
            dcb_ref[...] = jnp.zeros_like(dcb_ref)

        is_ctx, first, last, lok, rok = _conv_geometry(i, nb, cb)
        gl, dgl = _gelu_both(_ext(cp_ref, c_ref, cn_ref, first, last))
        g = _ext(dp_ref, d_ref, dn_ref, first, last) * _ext(vp_ref, v_ref, vn_ref, first, last) * dgl
        dv_ref[...] = (d_ref[...] * gl[GRID_W:GRID_W + TB]).astype(BF16)
        gm = _shift_prev(g, lok)
        gp = _shift_next(g, rok)
        cwv = cw_ref[...]

        def comb(dr, lo):
            sl = slice(lo, lo + TB)
            return cwv[3 * dr:3 * dr + 1] * gp[sl] + cwv[3 * dr + 1:3 * dr + 2] * g[sl] + cwv[3 * dr + 2:3 * dr + 3] * gm[sl]

        da = comb(1, GRID_W) + jnp.where(is_ctx, 0.0, comb(0, 2 * GRID_W) + comb(2, 0))
        da_ref[...] = da.astype(BF16)
        e = _ext(ap_ref, a_ref, an_ref, first, last)
        taps = [_shift_prev(e, lok), e, _shift_next(e, rok)]
        gmain = g[GRID_W:GRID_W + TB]
        dcb_ref[...] += jnp.sum(gmain, axis=0, keepdims=True)
        vert = jnp.where(is_ctx, 0.0, 1.0)
        for dr in range(3):
            sl = slice(dr * GRID_W, dr * GRID_W + TB)
            for dw in range(3):
                s = jnp.sum(gmain * taps[dw][sl], axis=0, keepdims=True)
                if dr != 1:
                    s = s * vert
                k = 3 * dr + dw
                dcw_ref[k:k + 1, :] += s

    main = pl.BlockSpec((TB, cbk), lambda j, i: (i, j))
    halo = _halo_specs(cbk, T // GRID_W)
    acc9 = pl.BlockSpec((9, cbk), lambda j, i: (0, j))
    acc1 = pl.BlockSpec((1, cbk), lambda j, i: (0, j))
    return pl.pallas_call(
        body, name=name, grid=(dff // cbk, nb),
        in_specs=halo + _halo_specs(cbk, T // GRID_W, nvb) + halo + halo + [acc9],
        out_specs=[main, main, acc9, acc1],
        out_shape=[jax.ShapeDtypeStruct((T, dff), BF16), jax.ShapeDtypeStruct((T, dff), BF16),
                   jax.ShapeDtypeStruct((9, dff), F32), jax.ShapeDtypeStruct((1, dff), F32)],
        compiler_params=_params("parallel", "arbitrary"),
    )(up, up, up, up, up, up, ac, ac, ac, dact, dact, dact, cw)


def _ffn_out_fwd(act, xm, mod, wd, ctx_rows, name):
    T, D = xm.shape
    dff = act.shape[1]
    cb = ctx_rows // TB

    def body(act_ref, x_ref, mod_ref, w_ref, xo_ref, fo_ref):
        out = _nn(act_ref[...], _full(w_ref))
        fo_ref[...] = out
        xo_ref[...] = x_ref[...] + mod_ref[5:6, :] * out

    row = pl.BlockSpec((TB, D), lambda i: (i, 0))
    return pl.pallas_call(
        body, name=name, grid=(T // TB,),
        in_specs=[pl.BlockSpec((TB, dff), lambda i: (i, 0)), row,
                  pl.BlockSpec((None, N_MOD, D), lambda i: (_stream_of(i, cb), 0, 0)),
                  _rows_weight_spec(wd)],
        out_specs=[row, row],
        out_shape=[jax.ShapeDtypeStruct((T, D), F32), jax.ShapeDtypeStruct((T, D), F32)],
        compiler_params=_params("parallel"),
    )(act, xm, mod, wd)


def _ffn_out_bwd(dx, fo, mod, wd, ctx_rows, name):
    T, D = dx.shape
    dff = N_CHIPS * wd.shape[1]
    cb = ctx_rows // TB

    def body(dx_ref, fo_ref, mod_ref, w_ref, dout_ref, dact_ref, dg2_ref):
        i = pl.program_id(0)

        @pl.when((i == 0) | (i == cb))
        def _():
            dg2_ref[...] = jnp.zeros_like(dg2_ref)

        dxv = dx_ref[...]
        dg2_ref[...] += jnp.sum(dxv * fo_ref[...], axis=0, keepdims=True)
        dout = (dxv * mod_ref[5:6, :]).astype(BF16)
        dout_ref[...] = dout
        dact_ref[...] = _nt(dout, _full(w_ref))

    row = pl.BlockSpec((TB, D), lambda i: (i, 0))
    return pl.pallas_call(
        body, name=name, grid=(T // TB,),
        in_specs=[row, row, pl.BlockSpec((None, N_MOD, D), lambda i: (_stream_of(i, cb), 0, 0)),
                  _rows_weight_spec(wd)],
        out_specs=[row, pl.BlockSpec((TB, dff), lambda i: (i, 0)),
                   pl.BlockSpec((None, 1, D), lambda i: (_stream_of(i, cb), 0, 0))],
        out_shape=[jax.ShapeDtypeStruct((T, D), BF16), jax.ShapeDtypeStruct((T, dff), F32),
                   jax.ShapeDtypeStruct((2, 1, D), F32)],
        compiler_params=_params("arbitrary"),
    )(dx, fo, mod, wd)


def _loss_bwd(x, target, fw, ctx_rows, name):
    T, D = x.shape
    cb = ctx_rows // TB

    def body(x_ref, t_ref, fw_ref, dx_ref, loss_ref, dfw_ref):
        i = pl.program_id(0)

        @pl.when(i == 0)
        def _():
            loss_ref[...] = jnp.zeros_like(loss_ref)
            dfw_ref[...] = jnp.zeros_like(dfw_ref)

        @pl.when(i < cb)
        def _():
            dx_ref[...] = jnp.zeros_like(dx_ref)

        @pl.when(i >= cb)
        def _():
            xv = x_ref[...]
            r = lax.rsqrt(jnp.mean(xv * xv, axis=-1, keepdims=True) + RMS_EPS)
            xh = xv * r
            fwv = fw_ref[...]
            err = xh * fwv - t_ref[...]
            loss_ref[...] += (0.5 / D) * jnp.sum(err * err)
            dy = err * (1.0 / D)
            dfw_ref[...] += jnp.sum(dy * xh, axis=0, keepdims=True)
            dxh = dy * fwv
            dx_ref[...] = r * (dxh - xh * jnp.mean(dxh * xh, axis=-1, keepdims=True))

    row = pl.BlockSpec((TB, D), lambda i: (i, 0))
    return pl.pallas_call(
        body, name=name, grid=(T // TB,),
        in_specs=[row, pl.BlockSpec((TB, D), lambda i: (jnp.maximum(i - cb, 0), 0)), pl.BlockSpec((1, D), lambda i: (0, 0))],
        out_specs=[row, pl.BlockSpec((1, 128), lambda i: (0, 0)), pl.BlockSpec((1, D), lambda i: (0, 0))],
        out_shape=[jax.ShapeDtypeStruct((T, D), F32), jax.ShapeDtypeStruct((1, 128), F32),
                   jax.ShapeDtypeStruct((1, D), F32)],
        compiler_params=_params("arbitrary"),
    )(x, target, fw)


def _adamw(w, gs, m, v, name):
    L, R, C = w.shape
    assert len(gs) == L
    rb = _rows_tile(R, max(16, (1 << 18) // C // 16 * 16))
    bc1 = 1.0 - ADAM_B1 ** ADAM_STEP
    bc2 = 1.0 - ADAM_B2 ** ADAM_STEP

    def body(w_ref, m_ref, v_ref, *rest):
        g_refs, (g_ref, d_ref, nm_ref, nv_ref) = rest[:L], rest[L:]
        layer = pl.program_id(0)
        for li in range(L):
            @pl.when(layer == li)
            def _():
                gv = g_refs[li][...]
                g_ref[...] = gv
                nm = ADAM_B1 * m_ref[...] + (1.0 - ADAM_B1) * gv
                nv = ADAM_B2 * v_ref[...] + (1.0 - ADAM_B2) * (gv * gv)
                nm_ref[...] = nm
                nv_ref[...] = nv
                d_ref[...] = -ADAM_LR * ((nm / bc1) / (jnp.sqrt(nv / bc2) + ADAM_EPS) + ADAM_WD * w_ref[...])

    blk = pl.BlockSpec((None, rb, C), lambda l, i: (l, i, 0))
    gblk = pl.BlockSpec((rb, C), lambda l, i: (i, 0))
    sd = jax.ShapeDtypeStruct((L, R, C), F32)
    return pl.pallas_call(
        body, name=name, grid=(L, R // rb), in_specs=[blk] * 3 + [gblk] * L, out_specs=[blk] * 4, out_shape=[sd] * 4,
        compiler_params=_params("parallel", "parallel"),
    )(w, m, v, *gs)


def _local_step(xs, cv, target, W, layer_weights, on_layer_grads, ctx_rows):
    T, D = xs.shape
    depth = W["norm1_w"].shape[0]
    saved = []
    X = xs
    for l in range(depth):
        s = {}
        Wl = layer_weights(l, X)
        mod_all, sa = _mod_fwd(cv, Wl["ada_w"], W["ada_b"][l][None, :] + Wl["token"], f"mod_fwd_{l}")
        mod = mod_all[:2].reshape(2, N_MOD, D)
        h1 = _norm_mod(X, W["norm1_w"][l][None, :], mod, 0, ctx_rows, f"norm1_{l}")
        parts = _mm_nn_w(h1, Wl["w_in"], F32, f"in_proj_{l}")
        o_f, st_f = _hgrn_fwd(parts, W["hlb"], l, False, ctx_rows, f"hgrn_fwd_f_{l}")
        o, st_b = _hgrn_fwd(parts, W["hlb"], l, True, ctx_rows, f"hgrn_fwd_b_{l}", o_add=o_f)
        ya = _sgu_fwd(parts, W["sgu_ln_w"][l][None, :], W["sgu_ln_b"][l][None, :], W["sgu_w"][l], W["sgu_bt"][l],
                      f"sgu_fwd_{l}")
        Wl.update(Wl.pop("late")(ya))
        yb, pa, pb, mg, tmo, xm, h2 = _token_out_fwd(o, parts, ya, X, mod, W["hnw"][l][None, :] + Wl["late_token"],
                                                     W["norm2_w"][l][None, :], Wl["w_a"], Wl["w_b"], Wl["w_o"], ctx_rows,
                                                     f"token_out_fwd_{l}")
        up = _mm_nn_w(h2, Wl["w_up"], F32, f"up_proj_{l}")
        ac, act = _conv_fwd(up, Wl["conv_w"], W["conv_b"][l][None, :], ctx_rows, f"conv_fwd_{l}")
        xo, fo = _ffn_out_fwd(act, xm, mod, Wl["w_down"], ctx_rows, f"ffn_out_fwd_{l}")
        s.update(X=X, Wl=Wl, mod=mod, mod_all=mod_all, sa=sa, h1=h1, parts=parts, o=o, st_f=st_f, st_b=st_b, ya=ya, yb=yb,
                 pa=pa, pb=pb, mg=mg, tmo=tmo, xm=xm, h2=h2, up=up, ac=ac, act=act, fo=fo)
        saved.append(s)
        X = xo

    dX, loss_row, dfw = _loss_bwd(X, target, W["final_norm_w"][None, :], ctx_rows, "loss_bwd")
    G = {k: [None] * depth for k in ("ada_b", "norm1_w", "sgu_ln_w", "sgu_ln_b", "sgu_w", "sgu_b", "hlb1", "hnw", "norm2_w",
                                     "conv_w", "conv_b", "dmod")}
    dcv = jnp.zeros_like(cv)
    for l in reversed(range(depth)):
        s = saved[l]
        mod, Wl = s["mod"], s["Wl"]
        big = {}
        dout2, dact, dg2 = _ffn_out_bwd(dX, s["fo"], mod, Wl["w_down"], ctx_rows, f"ffn_out_bwd_{l}")
        big["w_down"] = _mm_tn(s["act"], dout2, F32, f"dw_down_{l}")
        da, dv, dcw, dcb = _conv_bwd(s["up"], s["ac"], dact, Wl["conv_w"], ctx_rows, f"conv_bwd_{l}")
        G["conv_w"][l], G["conv_b"][l] = dcw, dcb[0]
        big["w_up"] = _mm_tn(s["h2"], (da, dv), F32, f"dw_up_{l}", out_chips=True)
        dh2 = _mm_nt_w((da, dv), Wl["w_up"], F32, f"dh2_{l}")
        dxm, dm2, dnw2 = _norm_mod_bwd(dh2, s["xm"], dX, W["norm2_w"][l][None, :], mod, 3, ctx_rows, f"norm2_bwd_{l}")
        G["norm2_w"][l] = dnw2[0]
        (dout1, dpa, dpb, dparts, dya, do, dg1, dhnw) = _token_out_bwd(
            dxm, s["tmo"], s["pa"], s["pb"], s["o"], s["parts"], mod, W["hnw"][l][None, :], Wl["w_a"], Wl["w_b"], Wl["w_o"],
            ctx_rows, f"token_out_bwd_{l}")
        G["hnw"][l] = dhnw[0]
        big["w_o"] = _mm_tn(s["mg"], dout1, F32, f"dw_o_{l}")
        big["w_a"] = _mm_tn(s["ya"], dpa, F32, f"dw_a_{l}")
        big["w_b"] = _mm_tn(s["yb"], dpb, F32, f"dw_b_{l}")
        tok = on_layer_grads(l, "early", big)
        dparts, dsw, dsbt, dlnw, dlnb = _sgu_bwd(s["parts"], dya, W["sgu_ln_w"][l][None, :], W["sgu_ln_b"][l][None, :] + tok,
                                                 W["sgu_w"][l], W["sgu_bt"][l], dparts, f"sgu_bwd_{l}")
        G["sgu_w"][l], G["sgu_b"][l], G["sgu_ln_w"][l], G["sgu_ln_b"][l] = dsw, dsbt.T, dlnw[0], dlnb[0]
        dq_f, dz_f, di_f, dlb_f = _hgrn_bwd(s["parts"], W["hlb"], do, s["st_f"], l, False, ctx_rows, f"hgrn_bwd_f_{l}")
        dparts, dlb_b = _hgrn_bwd(s["parts"], W["hlb"], do, s["st_b"], l, True, ctx_rows, f"hgrn_bwd_b_{l}",
                                  other=(dq_f, dz_f, di_f), dparts=dparts)
        G["hlb1"][l] = jnp.concatenate([dlb_f[0], dlb_b[0]])
        tok = on_layer_grads(l, "late", {"w_in": _mm_tn(s["h1"], dparts, F32, f"dw_in_{l}", out_chips=True)})
        dh1 = _mm_nt_w(dparts, Wl["w_in"], F32, f"dh1_{l}")
        tok = tok + on_layer_grads(l, "end", {"after": dh1})
        dX, dm1, dnw1 = _norm_mod_bwd(dh1, s["X"], dxm, W["norm1_w"][l][None, :] + tok, mod, 0, ctx_rows, f"norm1_bwd_{l}")
        G["norm1_w"][l] = dnw1[0]
        dmod = jnp.concatenate([dm1, dg1, dm2, dg2], axis=1).reshape(2, N_MOD * D)
        dmod16 = jnp.concatenate([dmod, jnp.zeros((cv.shape[0] - 2, N_MOD * D), F32)], axis=0)
        G["ada_b"][l] = dmod[0] + dmod[1]
        G["dmod"][l] = dmod
        dcv = dcv + _cvec_bwd(dmod16, Wl["ada_w"], cv, f"dcvec_{l}")
    G["c_ctx"] = dcv[0]
    G["final_norm_w"] = dfw[0]
    return loss_row[0, 0], dX, G, saved[0]["sa"]


def _chip_peers(x, y, c):
    return [((1 - x, y, c), 2 * (1 - x) + y), ((x, 1 - y, c), 2 * x + 1 - y), ((1 - x, 1 - y, c), 2 * (1 - x) + 1 - y)]


def _rdma_call(ins, out_shapes, plan, n_remote, n_local, name, aliases=None):
    n_in, n_out = len(ins), len(out_shapes)

    def body(*refs):
        in_refs, out_refs = refs[:n_in], refs[n_in:n_in + n_out]
        send_sems, recv_sems, local_sems = refs[n_in + n_out:]
        x, y, c = lax.axis_index("x"), lax.axis_index("y"), lax.axis_index("c")
        remote, local = plan(in_refs, out_refs, x, y, c)
        assert len(remote) == n_remote and len(local) == n_local, (name, len(remote), len(local))
        copies = [pltpu.make_async_copy(s, d, local_sems.at[i]) for i, (s, d) in enumerate(local)]
        copies += [pltpu.make_async_remote_copy(src_ref=s, dst_ref=d, send_sem=send_sems.at[k], recv_sem=recv_sems.at[k],
                                                device_id=dev, device_id_type=pl.DeviceIdType.MESH)
                   for k, (s, d, dev) in enumerate(remote)]
        for cp in copies:
            cp.start()
        for cp in copies:
            cp.wait()

    hbm = pl.BlockSpec(memory_space=pltpu.HBM)
    return pl.pallas_call(
        body, name=name, in_specs=[hbm] * n_in, out_specs=[hbm] * n_out, out_shape=out_shapes,
        scratch_shapes=[pltpu.SemaphoreType.DMA((n_remote,)), pltpu.SemaphoreType.DMA((n_remote,)),
                        pltpu.SemaphoreType.DMA((max(n_local, 1),))],
        input_output_aliases=aliases or {},
    )(*ins)


DMA_PIECE_BYTES = 1 << 18
DMA_MAX_PIECES = 8


def _row_pieces(shape, dtype):
    rows = shape[0]
    row_bytes = jnp.dtype(dtype).itemsize
    for d in shape[1:]:
        row_bytes *= d
    n = 1
    while n < DMA_MAX_PIECES and rows % (2 * n * 16) == 0 and rows * row_bytes // (2 * n) >= DMA_PIECE_BYTES:
        n *= 2
    return [(i * (rows // n), rows // n) for i in range(n)]


def _half_pieces(o, c):
    r2 = o.shape[1] // 2
    return [pl.ds(c * r2 + st, sz) for st, sz in _row_pieces((r2,) + o.shape[2:], o.dtype)]


def _n_half_pieces(arrays):
    return sum(len(_row_pieces((a.shape[1] // 2,) + a.shape[2:], a.dtype)) for a in arrays)


def _plan_gather_far(lands, x, y, c):
    me = 2 * x + y
    return [(o.at[me, rows], o.at[me, rows], dev) for dev, _ in _chip_peers(x, y, c) for o in lands
            for rows in _half_pieces(o, c)]


def _plan_gather_near(lands, x, y, c):
    return [(o.at[idx, rows], o.at[idx, rows], (x, y, 1 - c)) for _, idx in _chip_peers(x, y, c) for o in lands
            for rows in _half_pieces(o, c)]


def _gather_weights(lands, name):
    n = len(lands)
    n_far = (N_CHIPS - 1) * _n_half_pieces(lands)

    def body(*refs):
        outs = refs[n:2 * n]
        far_send, far_recv, near_send, near_recv = refs[2 * n:]
        x, y, c = lax.axis_index("x"), lax.axis_index("y"), lax.axis_index("c")
        mk = lambda plan, send, recv: [
            pltpu.make_async_remote_copy(src_ref=s, dst_ref=d, send_sem=send.at[k], recv_sem=recv.at[k], device_id=dev,
                                         device_id_type=pl.DeviceIdType.MESH)
            for k, (s, d, dev) in enumerate(plan(outs, x, y, c))]
        far, near = mk(_plan_gather_far, far_send, far_recv), mk(_plan_gather_near, near_send, near_recv)
        assert len(far) == n_far and len(near) == n_far
        for cp in far:
            cp.start()
        for k in range(n_far):
            far[k].wait_recv()
            near[k].start()
        for k in range(n_far):
            near[k].wait_recv()
        for cp in far + near:
            cp.wait_send()

    hbm = pl.BlockSpec(memory_space=pltpu.HBM)
    sems = pltpu.SemaphoreType.DMA((n_far,))
    return pl.pallas_call(
        body, name=name, in_specs=[hbm] * n, out_specs=[hbm] * n,
        out_shape=[jax.ShapeDtypeStruct(a.shape, a.dtype) for a in lands],
        scratch_shapes=[sems, sems, sems, sems], input_output_aliases={i: i for i in range(n)},
    )(*lands)


def _gather_all(v, name):
    def plan(ins, outs, x, y, c):
        (s,), (o,) = ins, outs
        me = 4 * x + 2 * y + c
        flip = lambda a, f: 1 - a if f else a
        remote = [(s, o.at[me], (flip(x, m & 4), flip(y, m & 2), flip(c, m & 1))) for m in range(1, 8)]
        return remote, [(s, o.at[me])]

    return _rdma_call([v], [jax.ShapeDtypeStruct((8,) + v.shape, v.dtype)], plan, 7, 1, name)[0]


def _plan_pair(ins, lands, x, y, c):
    return [(a.at[j, 1 - c, pl.ds(st, sz)], o.at[j, pl.ds(st, sz)], (x, y, 1 - c)) for a, o in zip(ins, lands)
            for j in range(N_CHIPS) for st, sz in _row_pieces(a.shape[2:], a.dtype)]


def _n_pair_copies(parts):
    return N_CHIPS * sum(len(_row_pieces(a.shape[2:], a.dtype)) for a in parts)


def _reduce_pair(parts, name):
    shapes = [jax.ShapeDtypeStruct((N_CHIPS,) + a.shape[2:], a.dtype) for a in parts]
    return _rdma_call(parts, shapes, lambda ins, outs, x, y, c: (_plan_pair(ins, outs, x, y, c), []),
                      _n_pair_copies(parts), 0, name)


def _plan_chips(ins, lands, x, y, c):
    me = 2 * x + y
    return [(a.at[idx, pl.ds(st, sz)], o.at[me, pl.ds(st, sz)], dev) for dev, idx in _chip_peers(x, y, c)
            for a, o in zip(ins, lands) for st, sz in _row_pieces(a.shape[1:], a.dtype)]


def _n_chips_copies(parts):
    return (N_CHIPS - 1) * sum(len(_row_pieces(a.shape[1:], a.dtype)) for a in parts)


def _gather_pair(halves, name):
    def plan(ins, outs, x, y, c):
        return [(o.at[c, pl.ds(st, sz)], o.at[c, pl.ds(st, sz)], (x, y, 1 - c)) for o in outs
                for st, sz in _row_pieces(o.shape[1:], o.dtype)], []

    shapes = [jax.ShapeDtypeStruct(a.shape, a.dtype) for a in halves]
    n_remote = sum(len(_row_pieces(a.shape[1:], a.dtype)) for a in halves)
    return _rdma_call(halves, shapes, plan, n_remote, 0, name, aliases={i: i for i in range(len(halves))})


def _split_start(ins, lands, plan, n_remote, name):
    n_buf = len(ins) + len(lands)

    def body(*refs):
        in_refs, land_refs = refs[:len(ins)], refs[len(ins):n_buf]
        send_sems, recv_sems, token = refs[n_buf], refs[n_buf + 1], refs[-1]
        x, y, c = lax.axis_index("x"), lax.axis_index("y"), lax.axis_index("c")
        remote = plan(in_refs, land_refs, x, y, c)
        assert len(remote) == n_remote, (name, len(remote))
        for k, (s, d, dev) in enumerate(remote):
            pltpu.make_async_remote_copy(src_ref=s, dst_ref=d, send_sem=send_sems.at[k], recv_sem=recv_sems.at[k],
                                         device_id=dev, device_id_type=pl.DeviceIdType.MESH).start()
        token[...] = jnp.zeros_like(token)

    hbm = pl.BlockSpec(memory_space=pltpu.HBM)
    sem = pl.BlockSpec(memory_space=pltpu.SEMAPHORE)
    bufs = list(ins) + list(lands)
    out = pl.pallas_call(
        body, name=name, in_specs=[hbm] * n_buf,
        out_specs=(sem, sem) + (hbm,) * n_buf + (pl.BlockSpec(memory_space=pltpu.VMEM),),
        out_shape=(pltpu.SemaphoreType.DMA((n_remote,)), pltpu.SemaphoreType.DMA((n_remote,)))
        + tuple(pltpu.HBM(a.shape, a.dtype) for a in bufs) + (jax.ShapeDtypeStruct((8, 128), F32),),
        input_output_aliases={i: 2 + i for i in range(n_buf)},
        compiler_params=pltpu.CompilerParams(has_side_effects=pltpu.SideEffectType.DATAFLOW_SIDE_EFFECTING),
    )(*[pltpu.with_memory_space_constraint(a, pltpu.HBM) for a in bufs])
    return dict(send=out[0], recv=out[1], ins=list(out[2:2 + len(ins)]), lands=list(out[2 + len(ins):2 + n_buf]),
                token=out[-1][0, 0], token_array=out[-1], plan=plan, n_remote=n_remote)


def _split_wait(st, after, name):
    n_in, n_buf = len(st["ins"]), len(st["ins"]) + len(st["lands"])
    plan, n_remote = st["plan"], st["n_remote"]

    def body(*refs):
        in_refs, land_refs = refs[:n_in], refs[n_in:n_buf]
        send_sems, recv_sems = refs[n_buf], refs[n_buf + 1]
        x, y, c = lax.axis_index("x"), lax.axis_index("y"), lax.axis_index("c")
        for k, (s, d, dev) in enumerate(plan(in_refs, land_refs, x, y, c)):
            cp = pltpu.make_async_remote_copy(src_ref=s, dst_ref=d, send_sem=send_sems.at[k], recv_sem=recv_sems.at[k],
                                              device_id=dev, device_id_type=pl.DeviceIdType.MESH)
            cp.wait_send()
            cp.wait_recv()

    hbm = pl.BlockSpec(memory_space=pltpu.HBM)
    sem = pl.BlockSpec(memory_space=pltpu.SEMAPHORE)
    bufs = st["ins"] + st["lands"]
    out = pl.pallas_call(
        body, name=name, in_specs=[hbm] * n_buf + [sem, sem, pl.BlockSpec(memory_space=pl.ANY)],
        out_specs=[hbm] * n_buf, out_shape=[pltpu.HBM(a.shape, a.dtype) for a in bufs],
        input_output_aliases={i: i for i in range(n_buf)},
        compiler_params=pltpu.CompilerParams(has_side_effects=pltpu.SideEffectType.DATAFLOW_SIDE_EFFECTING),
    )(*bufs, st["send"], st["recv"], after)
    return list(out[:n_in]), list(out[n_in:])


def _pair_forward(lands, name):
    shapes = [jax.ShapeDtypeStruct(a.shape, a.dtype) for a in lands]
    return _rdma_call(lands, shapes, lambda ins, outs, x, y, c: (_plan_gather_near(outs, x, y, c), []),
                      (N_CHIPS - 1) * _n_half_pieces(lands), 0, name, aliases={i: i for i in range(len(lands))})


def _sum_block_rows(r, C):
    return _rows_tile(r, max(16, (1 << 18) // C // 16 * 16))


def _sum_pair(a, recv, cidx, name):
    nch, _, r, C = a.shape
    rb = _sum_block_rows(r, C)

    def body(c_ref, a_ref, r_ref, o_ref):
        o_ref[...] = (a_ref[...] + r_ref[...]).astype(BF16)

    blk = pl.BlockSpec((None, rb, C), lambda j, i, c: (j, i, 0))
    return pl.pallas_call(
        body, name=name,
        grid_spec=pltpu.PrefetchScalarGridSpec(
            num_scalar_prefetch=1, grid=(nch, r // rb),
            in_specs=[pl.BlockSpec((None, None, rb, C), lambda j, i, c: (j, c[0], i, 0)), blk], out_specs=blk),
        out_shape=jax.ShapeDtypeStruct((nch, r, C), BF16),
        compiler_params=_params("parallel", "parallel"),
    )(cidx, a, recv)


def _sum_chips(mine, recv, ids, name):
    nch, r, C = recv.shape
    rb = _sum_block_rows(r, C)

    def body(ids_ref, m_ref, *rest):
        r_refs, o_ref = rest[:nch], rest[nch]
        chip = ids_ref[1]
        own = m_ref[...].astype(F32)
        acc = jnp.where(chip == 0, own, r_refs[0][...].astype(F32))
        for q in range(1, nch):
            acc = acc + jnp.where(chip == q, own, r_refs[q][...].astype(F32))
        o_ref[...] = acc

    def slot(q):
        return pl.BlockSpec((None, rb, C), lambda i, ids: (jnp.where(ids[1] == q, (q + 1) % nch, q), i, 0))

    return pl.pallas_call(
        body, name=name,
        grid_spec=pltpu.PrefetchScalarGridSpec(
            num_scalar_prefetch=1, grid=(r // rb,),
            in_specs=[pl.BlockSpec((None, rb, C), lambda i, ids: (ids[1], i, 0))] + [slot(q) for q in range(nch)],
            out_specs=pl.BlockSpec((None, rb, C), lambda i, ids: (ids[0], i, 0))),
        out_shape=jax.ShapeDtypeStruct((N_CORES, r, C), F32),
        compiler_params=_params("parallel"),
    )(ids, mine, *([recv] * nch))


PACK_COLS = 1024
_SHARDED = ("ada_w", "w_in", "w_branch_a", "w_branch_b", "w_out", "ffn_w_up", "ffn_w_down")
_LAYER_KEYS = ("ada_w", "w_in", "w_a", "w_b", "w_o", "w_up", "w_down")
_SMALL = ("c_ctx", "ada_b", "norm1_w", "sgu_ln_w", "sgu_ln_b", "sgu_w", "sgu_b", "hgrn_lower_bounds", "hgrn_norm_w",
          "norm2_w", "ffn_conv_b", "final_norm_w")
_ORDER = ("c_ctx", "ada_w", "ada_b", "norm1_w", "w_in", "sgu_ln_w", "sgu_ln_b", "sgu_w", "sgu_b", "hgrn_lower_bounds",
          "hgrn_norm_w", "w_branch_a", "w_branch_b", "w_out", "norm2_w", "ffn_w_up", "ffn_conv_w", "ffn_conv_b",
          "ffn_w_down", "final_norm_w")


def _pad_to(v, n):
    return jnp.concatenate([v, jnp.zeros((n - v.shape[0],), v.dtype)]) if v.shape[0] < n else v


def _round_up(n, m):
    return (n + m - 1) // m * m


def _pack(arrays, n_pad):
    flat = jnp.concatenate([a.reshape(-1) for a in arrays])
    return _pad_to(flat, n_pad)


def _unpack(flat, like):
    out, off = [], 0
    for a in like:
        out.append(flat[off:off + a.size].reshape(a.shape))
        off += a.size
    return out


def kernel(x, c, ctx, c_ctx, ada_w, ada_b, norm1_w, w_in, sgu_ln_w, sgu_ln_b, sgu_w, sgu_b, hgrn_lower_bounds, hgrn_norm_w, w_branch_a, w_branch_b, w_out, norm2_w, ffn_w_up, ffn_conv_w, ffn_conv_b, ffn_w_down, final_norm_w, loss_target, m_c_ctx, m_ada_w, m_ada_b, m_norm1_w, m_w_in, m_sgu_ln_w, m_sgu_ln_b, m_sgu_w, m_sgu_b, m_hgrn_lower_bounds, m_hgrn_norm_w, m_w_branch_a, m_w_branch_b, m_w_out, m_norm2_w, m_ffn_w_up, m_ffn_conv_w, m_ffn_conv_b, m_ffn_w_down, m_final_norm_w, v_c_ctx, v_ada_w, v_ada_b, v_norm1_w, v_w_in, v_sgu_ln_w, v_sgu_ln_b, v_sgu_w, v_sgu_b, v_hgrn_lower_bounds, v_hgrn_norm_w, v_w_branch_a, v_w_branch_b, v_w_out, v_norm2_w, v_ffn_w_up, v_ffn_conv_w, v_ffn_conv_b, v_ffn_w_down, v_final_norm_w):
    w = dict(c_ctx=c_ctx, ada_w=ada_w, ada_b=ada_b, norm1_w=norm1_w, w_in=w_in, sgu_ln_w=sgu_ln_w, sgu_ln_b=sgu_ln_b,
             sgu_w=sgu_w, sgu_b=sgu_b, hgrn_lower_bounds=hgrn_lower_bounds, hgrn_norm_w=hgrn_norm_w, w_branch_a=w_branch_a,
             w_branch_b=w_branch_b, w_out=w_out, norm2_w=norm2_w, ffn_w_up=ffn_w_up, ffn_conv_w=ffn_conv_w,
             ffn_conv_b=ffn_conv_b, ffn_w_down=ffn_w_down, final_norm_w=final_norm_w)
    mom = dict(zip(_ORDER, (m_c_ctx, m_ada_w, m_ada_b, m_norm1_w, m_w_in, m_sgu_ln_w, m_sgu_ln_b, m_sgu_w, m_sgu_b,
                            m_hgrn_lower_bounds, m_hgrn_norm_w, m_w_branch_a, m_w_branch_b, m_w_out, m_norm2_w, m_ffn_w_up,
                            m_ffn_conv_w, m_ffn_conv_b, m_ffn_w_down, m_final_norm_w)))
    var = dict(zip(_ORDER, (v_c_ctx, v_ada_w, v_ada_b, v_norm1_w, v_w_in, v_sgu_ln_w, v_sgu_ln_b, v_sgu_w, v_sgu_b,
                            v_hgrn_lower_bounds, v_hgrn_norm_w, v_w_branch_a, v_w_branch_b, v_w_out, v_norm2_w, v_ffn_w_up,
                            v_ffn_conv_w, v_ffn_conv_b, v_ffn_w_down, v_final_norm_w)))
    depth, D = norm1_w.shape
    dff = ffn_conv_b.shape[1]
    ctx_rows = ctx.shape[1]

    assert depth == 2, "the lower-bound softmax is written for two layers"
    core = lax.axis_index("c")
    chip = 2 * lax.axis_index("x") + lax.axis_index("y")
    ids = jnp.stack([core, chip]).astype(jnp.int32)

    first, rest = _LAYER_KEYS[:2], _LAYER_KEYS[2:]
    shard = lambda l, k: w[_SHARDED[_LAYER_KEYS.index(k)]][l].astype(BF16)
    started, conv_full = {}, []

    def landing(s):
        return lax.dynamic_update_slice(lax.empty((N_CHIPS,) + s.shape, s.dtype), s[None], (chip,) + (0,) * s.ndim)

    def start_gather(l, keys, tag):
        lands = [landing(shard(l, k)) for k in keys]
        started[tag] = _split_start([], lands, lambda ins, lds, x, y, c: _plan_gather_far(lds, x, y, c),
                                    (N_CHIPS - 1) * _n_half_pieces(lands), f"gather_start_{tag}")
        return started[tag]["token"]

    def finish_gather(keys, tag, after):
        _, lands = _split_wait(started[tag], after, f"gather_wait_{tag}")
        return dict(zip(keys, _pair_forward(lands, f"gather_forward_{tag}")))

    def layer_weights(l, after):
        if l == 0:
            got = _gather_weights([landing(shard(0, k)) for k in first] + [landing(ffn_conv_w)], "gather_weights_first")
            conv_full.append(jnp.transpose(got[-1], (1, 2, 3, 0, 4)).reshape(depth, 9, dff))
            out = dict(zip(first, got), token=start_gather(0, rest, "rest_0"))
        else:
            out = dict(finish_gather(first, f"first_{l}", after), token=0.0)

        def late(after_late):
            more = finish_gather(rest, f"rest_{l}", after_late)
            more["late_token"] = 0.0
            if l + 1 < depth:
                more["late_token"] = start_gather(l + 1, first, f"first_{l + 1}") + start_gather(l + 1, rest, f"rest_{l + 1}")
            return more

        return dict(out, conv_w=conv_full[0][l], late=late)

    groups, order = {}, []

    def as_parts(gs):
        return [g.reshape(N_CHIPS, N_CORES, g.size // (N_CHIPS * N_CORES * g.shape[-1]), g.shape[-1]) for g in gs]

    def pair_start(tag, l, keys, gs):
        parts = as_parts(gs)
        lands = [lax.empty((N_CHIPS,) + p.shape[2:], p.dtype) for p in parts]
        groups[tag] = dict(l=l, keys=keys, pair=_split_start(parts, lands, _plan_pair, _n_pair_copies(parts),
                                                             f"reduce_pair_start_{tag}"))
        order.append(tag)
        return groups[tag]["pair"]["token"]

    def chips_start(tag, after):
        parts, other = _split_wait(groups[tag]["pair"], after, f"reduce_pair_wait_{tag}")
        sums = [_sum_pair(a, o, ids, f"sum_pair_{tag}_{i}") for i, (a, o) in enumerate(zip(parts, other))]
        lands = [lax.empty(s.shape, s.dtype) for s in sums]
        groups[tag]["chips"] = _split_start(sums, lands, _plan_chips, _n_chips_copies(sums), f"reduce_chips_start_{tag}")
        return groups[tag]["chips"]["token"]

    def chips_finish(tag, after):
        sums, recv = _split_wait(groups[tag]["chips"], after, f"reduce_chips_wait_{tag}")
        return {(groups[tag]["l"], k): _sum_chips(sums[i], recv[i], ids, f"sum_chips_{tag}_{i}")
                for i, k in enumerate(groups[tag]["keys"])}

    def on_layer_grads(l, stage, gs):
        if stage == "early":
            return pair_start(f"early_{l}", l, list(gs), list(gs.values()))
        if stage == "late":
            return pair_start(f"late_{l}", l, ["w_in"], [gs["w_in"]]) + chips_start(f"early_{l}", gs["w_in"])
        return chips_start(f"late_{l}", gs["after"])

    W = dict(ada_b=ada_b, norm1_w=norm1_w, sgu_ln_w=sgu_ln_w, sgu_ln_b=sgu_ln_b, sgu_w=sgu_w.astype(BF16),
             sgu_bt=jnp.swapaxes(sgu_b, 1, 2), hlb=hgrn_lower_bounds, hnw=hgrn_norm_w, norm2_w=norm2_w, conv_b=ffn_conv_b,
             final_norm_w=final_norm_w)
    xs = jnp.concatenate([ctx[0], x[0]], axis=0)
    cv = jnp.concatenate([c_ctx[None, :], c, jnp.zeros((14, D), F32)], axis=0)
    loss_local, dxs, G, sa = _local_step(xs, cv, loss_target[0], W, layer_weights, on_layer_grads, ctx_rows)
    loss = lax.psum(loss_local, ("x", "y", "c"))
    grad_x = dxs[ctx_rows:][None]

    pad8 = lambda a: jnp.pad(a, ((0, 8 - a.shape[0]), (0, 0)))
    fact = jnp.concatenate([pad8(sa[1:2].astype(F32))] + [pad8(G["dmod"][l][1].reshape(N_MOD, D)) for l in range(depth)]
                           + [pad8(G["dmod"][l][0].reshape(N_MOD, D)) for l in range(depth)], axis=0)
    facts = _gather_all(fact, "gather_mod_factors")
    lhs = jnp.concatenate([facts[:, 0].astype(BF16), jnp.broadcast_to(sa[0:1], (8, D))], axis=0)
    ada_cols = N_MOD * D // N_CHIPS
    g_ada = []
    for l in range(depth):
        lo_x, lo_c = 8 * (1 + l), 8 * (1 + depth + l)
        rhs = jnp.concatenate([facts[:, lo_x:lo_x + N_MOD].reshape(8, N_MOD * D),
                               facts[:, lo_c:lo_c + N_MOD].reshape(8, N_MOD * D)], axis=0)
        rhs = lax.dynamic_slice_in_dim(rhs, chip * ada_cols, ada_cols, axis=1).astype(BF16)
        g_ada.append(_mm_tn(lhs, rhs, F32, f"dw_ada_{l}"))

    dh = G["hlb1"][depth - 1]
    small_like = [w[k] for k in _SMALL] + [jnp.zeros((depth, 9, dff), F32)]
    small = [G["c_ctx"], jnp.stack(G["ada_b"]), jnp.stack(G["norm1_w"]), jnp.stack(G["sgu_ln_w"]), jnp.stack(G["sgu_ln_b"]),
             jnp.stack(G["sgu_w"]), jnp.stack(G["sgu_b"]), jnp.stack([-dh, dh]), jnp.stack(G["hnw"]), jnp.stack(G["norm2_w"]),
             jnp.stack(G["conv_b"]), G["final_norm_w"], jnp.stack(G["conv_w"])]
    n_small = sum(a.size for a in small)
    n_small_pad = _round_up(n_small, N_CORES * 16 * PACK_COLS)
    small_rows = n_small_pad // (N_CORES * PACK_COLS)
    small_rep = jnp.broadcast_to(_pack(small, n_small_pad).reshape(1, N_CORES, small_rows, PACK_COLS),
                                 (N_CHIPS, N_CORES, small_rows, PACK_COLS))
    small_parts = as_parts([small_rep])
    small_sums = [_sum_pair(small_parts[0], _reduce_pair(small_parts, "reduce_pair_small")[0], ids, "sum_pair_small")]
    groups["small"] = dict(l=None, keys=["small"], chips=_split_start(
        small_sums, [lax.empty(small_sums[0].shape, small_sums[0].dtype)], _plan_chips, _n_chips_copies(small_sums),
        "reduce_chips_start_small"))

    def gather_halves(halves, name):
        return dict(zip(halves, _gather_pair(list(halves.values()), name)))

    last = order[-1]
    halves = {}
    for tag in order[:-1]:
        halves.update(chips_finish(tag, groups["small"]["chips"]["token_array"]))
    reduced = gather_halves(halves, "gather_pair")
    grads, delta, new_m, new_v = {}, {}, {}, {}

    def adamw_sharded(i):
        k = _SHARDED[i]
        gs = g_ada if i == 0 else [reduced[(l, _LAYER_KEYS[i])].reshape(w[k].shape[1:]) for l in range(depth)]
        grads[k], delta[k], new_m[k], new_v[k] = _adamw(w[k], gs, mom[k], var[k], f"adamw_{k}")

    last_keys = groups[last]["keys"]
    for i in range(len(_SHARDED)):
        if _LAYER_KEYS[i] not in last_keys:
            adamw_sharded(i)
    halves = chips_finish(last, new_v[_SHARDED[-1]])
    halves.update(chips_finish("small", new_v[_SHARDED[-1]]))
    reduced.update(gather_halves(halves, "gather_pair_last"))
    for i in range(len(_SHARDED)):
        if _LAYER_KEYS[i] in last_keys:
            adamw_sharded(i)

    g_small = _unpack(reduced[(None, "small")].reshape(-1), small_like)
    grads.update(zip(_SMALL, g_small[:-1]))
    grads["ffn_conv_w"] = lax.dynamic_slice_in_dim(g_small[-1].reshape(depth, 3, 3, dff), chip * (dff // N_CHIPS),
                                                   dff // N_CHIPS, axis=3)
    packed = _SMALL + ("ffn_conv_w",)
    n_pad = _round_up(sum(w[k].size for k in packed), 16 * PACK_COLS)
    pack = lambda t: _pack([t[k] for k in packed], n_pad).reshape(1, -1, PACK_COLS)
    _, d, nm, nv = _adamw(pack(w), [pack(grads)[0]], pack(mom), pack(var), "adamw_packed")
    like = [w[k] for k in packed]
    for src, dst in ((d, delta), (nm, new_m), (nv, new_v)):
        dst.update(zip(packed, _unpack(src.reshape(-1), like)))

    return (loss, grad_x, *[grads[k] for k in _ORDER], *[delta[k] for k in _ORDER], *[new_m[k] for k in _ORDER],
            *[new_v[k] for k in _ORDER])
```

```python
import functools

import jax
import jax.numpy as jnp
from jax import lax
from jax.experimental import pallas as pl
from jax.experimental.pallas import tpu as pltpu

F32 = jnp.float32
BF16 = jnp.bfloat16

GRID_W = 64
HG_CHUNK = 64
SGU_CHUNK = 128
HEAD = 128
TB = 256
N_MOD = 6
RMS_EPS = 1e-6
LN_EPS = 1e-5
VMEM_LIMIT = 48 * 1024 * 1024
VMEM_LIMIT_PAIR = 58 * 1024 * 1024
N_CHIPS = 4
N_CORES = 2

ADAM_LR = 0.001
ADAM_B1 = 0.9
ADAM_B2 = 0.999
ADAM_EPS = 1e-08
ADAM_WD = 0.01
ADAM_STEP = 10

_GELU_C = 0.7978845608028654
_GELU_A = 0.044715


def _sigmoid(x):
    return 0.5 * jnp.tanh(0.5 * x) + 0.5


def _silu(x):
    return x * _sigmoid(x)


def _silu_both(x):
    s = _sigmoid(x)
    return x * s, s * (1.0 + x * (1.0 - s))


def _dsilu(x):
    return _silu_both(x)[1]


def _gelu_both(x):
    x2 = x * x
    t = jnp.tanh(_GELU_C * (x + _GELU_A * x2 * x))
    h = 0.5 * (1.0 + t)
    return x * h, h + 0.5 * x * (1.0 - t * t) * (_GELU_C + 3.0 * _GELU_C * _GELU_A * x2)


def _gelu(x):
    return 0.5 * x * (1.0 + jnp.tanh(_GELU_C * (x + _GELU_A * x * x * x)))


def _dot(a, b, ca, cb):
    return lax.dot_general(a, b, (((ca,), (cb,)), ((), ())), preferred_element_type=F32)


def _nn(a, b):
    return _dot(a, b, 1, 0)


def _nt(a, b):
    return _dot(a, b, 1, 1)


def _tn(a, b):
    return _dot(a, b, 0, 0)


def _params(*sem, vmem=VMEM_LIMIT):
    return pltpu.CompilerParams(dimension_semantics=sem if sem else None, vmem_limit_bytes=vmem)


def _stream_of(i, ctx_blocks):
    return (i >= ctx_blocks).astype(jnp.int32)


def _mm(a, b, mode, tm, tn, tk, out_dtype, name, b_chips=False, out_chips=False, vmem=VMEM_LIMIT):
    a_pair, b_pair = isinstance(a, tuple), isinstance(b, tuple)
    assert (not a_pair or mode == "nt") and (not b_pair or (mode == "tn" and not b_chips))
    ashape = (a[0].shape[0], 2 * a[0].shape[1]) if a_pair else a.shape
    if b_pair:
        bshape = (b[0].shape[0], 2 * b[0].shape[1])
    elif not b_chips:
        bshape = b.shape
    else:
        bshape = (b.shape[1], N_CHIPS * b.shape[2])
    if mode == "nn":
        (M, K), (K2, N) = ashape, bshape
    elif mode == "nt":
        (M, K), (N, K2) = ashape, bshape
    else:
        (K, M), (K2, N) = ashape, bshape
    assert K == K2 and M % tm == 0 and N % tn == 0 and K % tk == 0, (name, ashape, bshape, tm, tn, tk)
    nk = K // tk
    if a_pair:
        n1 = a[0].shape[1] // tk
        assert a[0].shape[1] % tk == 0
        a_specs = [pl.BlockSpec((tm, tk), lambda j, i, k: (i, jnp.minimum(k, n1 - 1))),
                   pl.BlockSpec((tm, tk), lambda j, i, k: (i, jnp.maximum(k - n1, 0)))]
    elif mode == "tn":
        a_specs = [pl.BlockSpec((tk, tm), lambda j, i, k: (k, i))]
    else:
        a_specs = [pl.BlockSpec((tm, tk), lambda j, i, k: (i, k))]
    if b_pair:
        n1 = b[0].shape[1] // tn
        assert b[0].shape[1] % tn == 0
        b_specs = [pl.BlockSpec((tk, tn), lambda j, i, k: (k, jnp.minimum(j, n1 - 1))),
                   pl.BlockSpec((tk, tn), lambda j, i, k: (k, jnp.maximum(j - n1, 0)))]
    elif not b_chips:
        if mode == "nt":
            b_spec = pl.BlockSpec((tn, tk), lambda j, i, k: (j, k))
        else:
            b_spec = pl.BlockSpec((tk, tn), lambda j, i, k: (k, j))
    else:
        cols = b.shape[2]
        if mode == "nn":
            per = cols // tn
            assert cols % tn == 0
            b_spec = pl.BlockSpec((None, tk, tn), lambda j, i, k: (j // per, k, j % per))
        else:
            per = cols // tk
            assert mode == "nt" and cols % tk == 0
            b_spec = pl.BlockSpec((None, tn, tk), lambda j, i, k: (k // per, j, k % per))
    if not b_pair:
        b_specs = [b_spec]
    if out_chips:
        per_o = (N // N_CHIPS) // tn
        assert (N // N_CHIPS) % tn == 0
        o_spec = pl.BlockSpec((None, tm, tn), lambda j, i, k: (j // per_o, i, j % per_o))
        o_shape = (N_CHIPS, M, N // N_CHIPS)
    else:
        o_spec = pl.BlockSpec((tm, tn), lambda j, i, k: (i, j))
        o_shape = (M, N)
    ca, cb = {"nn": (1, 0), "nt": (1, 1), "tn": (0, 0)}[mode]

    in_place = nk == 1
    na, nb = len(a_specs), len(b_specs)

    def body(*refs):
        a_refs, b_refs, rest = refs[:na], refs[na:na + nb], refs[na + nb:]
        if in_place:
            (o_ref,) = rest
        else:
            o_ref, acc = rest
        k = pl.program_id(2)

        if not in_place:
            @pl.when(k == 0)
            def _():
                acc[...] = jnp.zeros_like(acc)

        def multiply(which):
            part = _dot(a_refs[which if a_pair else 0][...], b_refs[which if b_pair else 0][...], ca, cb)
            if in_place:
                o_ref[...] = part.astype(out_dtype)
            else:
                acc[...] += part

        if a_pair or b_pair:
            first = (k < n1) if a_pair else (pl.program_id(0) < n1)
            pl.when(first)(functools.partial(multiply, 0))
            pl.when(jnp.logical_not(first))(functools.partial(multiply, 1))
        else:
            multiply(0)

        if not in_place:
            @pl.when(k == nk - 1)
            def _():
                o_ref[...] = acc[...].astype(out_dtype)

    ins = (list(a) if a_pair else [a]) + (list(b) if b_pair else [b])
    return pl.pallas_call(
        body, name=name, grid=(N // tn, M // tm, nk), in_specs=a_specs + b_specs, out_specs=o_spec,
        out_shape=jax.ShapeDtypeStruct(o_shape, out_dtype),
        scratch_shapes=[] if in_place else [pltpu.VMEM((tm, tn), F32)],
        compiler_params=_params("parallel", "parallel", "arbitrary", vmem=vmem),
    )(*ins)


def _tile(n, pref):
    if n <= pref:
        return n
    best = None
    for t in range(128, pref + 1, 128):
        if n % t == 0:
            best = t
    assert best is not None, (n, pref)
    return best


def _rows_tile(n, pref):
    if n <= pref:
        return n
    best = None
    for t in range(16, pref + 1, 16):
        if n % t == 0:
            best = t
    assert best is not None, (n, pref)
    return best


def _mm_nn_w(a, wg, out_dtype, name):
    M, K = a.shape
    return _mm(a, wg, "nn", _rows_tile(M, 2176), _tile(wg.shape[2], 1536), _tile(K, 1536), out_dtype, name, b_chips=True)


def _mm_nt_w(a, wg, out_dtype, name):
    M = a[0].shape[0] if isinstance(a, tuple) else a.shape[0]
    return _mm(a, wg, "nt", _rows_tile(M, 1088), _tile(wg.shape[1], 1024), _tile(wg.shape[2], 2304), out_dtype, name,
               b_chips=True)


def _mm_tn(a, b, out_dtype, name, out_chips=False):
    K, M = a.shape
    N = 2 * b[0].shape[1] if isinstance(b, tuple) else b.shape[1]
    ncol = N // N_CHIPS if out_chips else N
    tm, tn = _tile(M, 1408), _tile(ncol, 1408)
    if tm * tn > 1408 * 1152:
        tn = _tile(ncol, 1152)
    vmem = VMEM_LIMIT_PAIR if isinstance(b, tuple) else VMEM_LIMIT
    return _mm(a, b, "tn", tm, tn, _rows_tile(K, 2176), out_dtype, name, out_chips=out_chips, vmem=vmem)


def _mod_fwd(cv, wg, b, name):
    R, D = cv.shape
    tn = wg.shape[2]
    N = N_CHIPS * tn

    def body(cv_ref, w_ref, b_ref, mod_ref, sa_ref):
        sa = _silu(cv_ref[...]).astype(BF16)
        sa_ref[...] = sa
        mod_ref[...] = _nn(sa, w_ref[...]) + b_ref[...]

    return pl.pallas_call(
        body, name=name, grid=(N_CHIPS,),
        in_specs=[pl.BlockSpec((R, D), lambda j: (0, 0)), pl.BlockSpec((None, D, tn), lambda j: (j, 0, 0)),
                  pl.BlockSpec((1, tn), lambda j: (0, j))],
        out_specs=[pl.BlockSpec((R, tn), lambda j: (0, j)), pl.BlockSpec((R, D), lambda j: (0, 0))],
        out_shape=[jax.ShapeDtypeStruct((R, N), F32), jax.ShapeDtypeStruct((R, D), BF16)],
        compiler_params=_params("arbitrary"),
    )(cv, wg, b)


def _cvec_bwd(dmod, wg, cv, name):
    R, N = dmod.shape
    D = wg.shape[1]
    tk = wg.shape[2]
    nk = N_CHIPS

    def body(dm_ref, w_ref, cv_ref, o_ref):
        k = pl.program_id(0)

        @pl.when(k == 0)
        def _():
            o_ref[...] = jnp.zeros_like(o_ref)

        o_ref[...] += _nt(dm_ref[...].astype(BF16), w_ref[...])

        @pl.when(k == nk - 1)
        def _():
            o_ref[...] = o_ref[...] * _dsilu(cv_ref[...])

    return pl.pallas_call(
        body, name=name, grid=(nk,),
        in_specs=[pl.BlockSpec((R, tk), lambda k: (0, k)), pl.BlockSpec((None, D, tk), lambda k: (k, 0, 0)),
                  pl.BlockSpec((R, D), lambda k: (0, 0))],
        out_specs=pl.BlockSpec((R, D), lambda k: (0, 0)),
        out_shape=jax.ShapeDtypeStruct((R, D), F32),
        compiler_params=_params("arbitrary"),
    )(dmod, wg, cv)


def _norm_mod(x, nw, mod, which, ctx_rows, name):
    T, D = x.shape
    cb = ctx_rows // TB

    def body(x_ref, nw_ref, mod_ref, h_ref):
        xv = x_ref[...]
        r = lax.rsqrt(jnp.mean(xv * xv, axis=-1, keepdims=True) + RMS_EPS)
        y = xv * r * nw_ref[...]
        sh = mod_ref[which:which + 1, :]
        sc = mod_ref[which + 1:which + 2, :]
        h_ref[...] = (y * (1.0 + sc) + sh).astype(BF16)

    return pl.pallas_call(
        body, name=name, grid=(T // TB,),
        in_specs=[pl.BlockSpec((TB, D), lambda i: (i, 0)), pl.BlockSpec((1, D), lambda i: (0, 0)),
                  pl.BlockSpec((None, N_MOD, D), lambda i: (_stream_of(i, cb), 0, 0))],
        out_specs=pl.BlockSpec((TB, D), lambda i: (i, 0)),
        out_shape=jax.ShapeDtypeStruct((T, D), BF16),
        compiler_params=_params("parallel"),
    )(x, nw, mod)


def _norm_mod_bwd(dh, x, dres, nw, mod, which, ctx_rows, name):
    T, D = x.shape
    cb = ctx_rows // TB

    def body(dh_ref, x_ref, dres_ref, nw_ref, mod_ref, dx_ref, dm_ref, dnw_ref):
        i = pl.program_id(0)

        @pl.when(i == 0)
        def _():
            dnw_ref[...] = jnp.zeros_like(dnw_ref)

        @pl.when((i == 0) | (i == cb))
        def _():
            dm_ref[...] = jnp.zeros_like(dm_ref)

        xv = x_ref[...]
        dh = dh_ref[...]
        r = lax.rsqrt(jnp.mean(xv * xv, axis=-1, keepdims=True) + RMS_EPS)
        xh = xv * r
        nwv = nw_ref[...]
        sc = mod_ref[which + 1:which + 2, :]
        y = xh * nwv
        dm_ref[0:1, :] += jnp.sum(dh, axis=0, keepdims=True)
        dm_ref[1:2, :] += jnp.sum(dh * y, axis=0, keepdims=True)
        dy = dh * (1.0 + sc)
        dnw_ref[...] += jnp.sum(dy * xh, axis=0, keepdims=True)
        dxh = dy * nwv
        dx_ref[...] = dres_ref[...] + r * (dxh - xh * jnp.mean(dxh * xh, axis=-1, keepdims=True))

    return pl.pallas_call(
        body, name=name, grid=(T // TB,),
        in_specs=[pl.BlockSpec((TB, D), lambda i: (i, 0)), pl.BlockSpec((TB, D), lambda i: (i, 0)),
                  pl.BlockSpec((TB, D), lambda i: (i, 0)), pl.BlockSpec((1, D), lambda i: (0, 0)),
                  pl.BlockSpec((None, N_MOD, D), lambda i: (_stream_of(i, cb), 0, 0))],
        out_specs=[pl.BlockSpec((TB, D), lambda i: (i, 0)),
                   pl.BlockSpec((None, 2, D), lambda i: (_stream_of(i, cb), 0, 0)),
                   pl.BlockSpec((1, D), lambda i: (0, 0))],
        out_shape=[jax.ShapeDtypeStruct((T, D), F32), jax.ShapeDtypeStruct((2, 2, D), F32),
                   jax.ShapeDtypeStruct((1, D), F32)],
        compiler_params=_params("arbitrary"),
    )(dh, x, dres, nw, mod)


def _scan_chunk(n, rev, n_ctx, n_all):
    if not rev:
        return n
    return jnp.where(n < n_ctx, n_ctx - 1 - n, n_all - 1 + n_ctx - n)


def _cumsum_rows(x, rev):
    rows = x.shape[0]
    row = lax.broadcasted_iota(jnp.int32, (rows, 1), 0)
    s = 1
    while s < rows:
        if not rev:
            x = x + jnp.where(row >= s, pltpu.roll(x, s, 0), 0.0)
        else:
            x = x + jnp.where(row < rows - s, pltpu.roll(x, rows - s, 0), 0.0)
        s *= 2
    return x


def _lower_bound(hlb_ref, layer):
    h = hlb_ref[...]
    if layer == 0:
        return jnp.zeros_like(h[0:1, :])
    return _sigmoid(h[1:2, :] - h[0:1, :])


HG_STEP = 4


def _step_rows(j, rev, backward):
    sub = j if rev == backward else HG_STEP - 1 - j
    return slice(sub * HG_CHUNK, (sub + 1) * HG_CHUNK)


def _hgrn_gates(q_ref, f_ref, hlb_ref, layer, rev, rows):
    lb = _lower_bound(hlb_ref, layer)
    z = f_ref[rows, :]
    sig = 1.0 / (1.0 + jnp.exp(-z))
    fg = lb + (1.0 - lb) * sig
    kk = (1.0 - lb) * (1.0 - sig)
    g = jnp.log(fg)
    b = _cumsum_rows(g, rev)
    bt = jnp.sum(g, axis=0, keepdims=True)
    mid = HG_CHUNK // 2
    r = b[mid:mid + 1, :] if rev else b[mid - 1:mid, :]
    qh = _silu(q_ref[rows, :])
    return lb, sig, fg, kk, b, bt, r, qh


def _tri_mask(rev):
    t = lax.broadcasted_iota(jnp.int32, (HG_CHUNK, HG_CHUNK), 0)
    s = lax.broadcasted_iota(jnp.int32, (HG_CHUNK, HG_CHUNK), 1)
    return (s >= t) if rev else (s <= t)


def _hgrn_fwd(parts, hlb, layer, rev, ctx_rows, name, o_add=None):
    T = parts.shape[0]
    D = hlb.shape[1] // 2
    nh = D // HEAD
    n_all, n_ctx = T // HG_CHUNK, ctx_rows // HG_CHUNK
    assert n_all % HG_STEP == 0 and n_ctx % HG_STEP == 0
    n_steps = n_all // HG_STEP
    block = functools.partial(_scan_chunk, rev=rev, n_ctx=n_ctx // HG_STEP, n_all=n_steps)
    fcol = 2 if rev else 1

    def body(q_ref, f_ref, i_ref, hlb_ref, *rest):
        if o_add is None:
            o_ref, st_ref, s_scr = rest
        else:
            oa_ref, o_ref, st_ref, s_scr = rest
        n = pl.program_id(0)

        @pl.when(n == 0)
        def _():
            s_scr[...] = jnp.zeros_like(s_scr)

        mask = _tri_mask(rev)
        hs = [slice(h * HEAD, (h + 1) * HEAD) for h in range(nh)]
        for j in range(HG_STEP):
            rows = _step_rows(j, rev, False)
            lb, sig, fg, kk, b, bt, r, qh = _hgrn_gates(q_ref, f_ref, hlb_ref, layer, rev, rows)
            qr = (qh * jnp.exp(b - r)).astype(BF16)
            kr = (kk * jnp.exp(r - b)).astype(BF16)
            qe = (qh * jnp.exp(b)).astype(BF16)
            ke = (kk * jnp.exp(bt - b)).astype(BF16)
            dec = jnp.exp(bt)
            v = i_ref[rows, :].astype(BF16)
            st = [s_scr[h] for h in range(nh)]
            a_raw = [_nt(qr[:, sl], kr[:, sl]) for sl in hs]
            o_int = [_nt(qe[:, sl], st[h].astype(BF16)) for h, sl in enumerate(hs)]
            kv = [_tn(v[:, sl], ke[:, sl]) for sl in hs]
            for h, sl in enumerate(hs):
                st_ref[j, h] = st[h]
                o = _nn(jnp.where(mask, a_raw[h], 0.0).astype(BF16), v[:, sl]) + o_int[h]
                if o_add is not None:
                    o = o + oa_ref[rows, sl]
                o_ref[rows, sl] = o
                s_scr[h] = st[h] * dec[:, sl] + kv[h]

    cspec = lambda col: pl.BlockSpec((HG_STEP * HG_CHUNK, D), lambda n: (block(n), col))
    ins = [parts, parts, parts, hlb]
    specs = [cspec(0), cspec(fcol), cspec(3), pl.BlockSpec((2, D), lambda n: (0, 1 if rev else 0))]
    if o_add is not None:
        ins.append(o_add)
        specs.append(cspec(0))
    return pl.pallas_call(
        body, name=name, grid=(n_steps,), in_specs=specs,
        out_specs=[cspec(0), pl.BlockSpec((HG_STEP, nh, HEAD, HEAD), lambda n: (n, 0, 0, 0))],
        out_shape=[jax.ShapeDtypeStruct((T, D), F32), jax.ShapeDtypeStruct((n_all, nh, HEAD, HEAD), F32)],
        scratch_shapes=[pltpu.VMEM((nh, HEAD, HEAD), F32)],
        compiler_params=_params("arbitrary"),
    )(*ins)


def _hgrn_bwd(parts, hlb, do, states, layer, rev, ctx_rows, name, other=None, dparts=None):
    T = parts.shape[0]
    D = hlb.shape[1] // 2
    nh = D // HEAD
    n_all, n_ctx = T // HG_CHUNK, ctx_rows // HG_CHUNK
    assert n_all % HG_STEP == 0 and n_ctx % HG_STEP == 0
    n_steps = n_all // HG_STEP
    step = lambda m: n_steps - 1 - m
    block = lambda m: _scan_chunk(step(m), rev, n_ctx // HG_STEP, n_steps)
    fcol = 2 if rev else 1
    has_add = other is not None
    assert not has_add or rev

    def body(q_ref, f_ref, i_ref, hlb_ref, do_ref, st_ref, *rest):
        if has_add:
            dqa_ref, dza_ref, dia_ref, _, out_ref, dlb_ref, ds_scr = rest
            dq_ref, dz_ref, di_ref = out_ref.at[:, 0:D], out_ref.at[:, 2 * D:3 * D], out_ref.at[:, 3 * D:4 * D]
            out_ref[:, D:2 * D] = dza_ref[...]
        else:
            dq_ref, dz_ref, di_ref, dlb_ref, ds_scr = rest
        m = pl.program_id(0)

        @pl.when(m == 0)
        def _():
            ds_scr[...] = jnp.zeros_like(ds_scr)
            dlb_ref[...] = jnp.zeros_like(dlb_ref)

        mask = _tri_mask(rev)
        hs = [slice(h * HEAD, (h + 1) * HEAD) for h in range(nh)]
        for j in range(HG_STEP):
            rows = _step_rows(j, rev, True)
            slot = HG_STEP - 1 - j
            lb, sig, fg, kk, b, bt, r, qh = _hgrn_gates(q_ref, f_ref, hlb_ref, layer, rev, rows)
            e_qr = jnp.exp(b - r)
            e_kr = jnp.exp(r - b)
            e_b = jnp.exp(b)
            e_ke = jnp.exp(bt - b)
            dec = jnp.exp(bt)
            qr = (qh * e_qr).astype(BF16)
            kr = (kk * e_kr).astype(BF16)
            qe = (qh * e_b).astype(BF16)
            ke = (kk * e_ke).astype(BF16)
            v = i_ref[rows, :].astype(BF16)
            dov = do_ref[rows, :].astype(BF16)
            st = [st_ref[slot, h] for h in range(nh)]
            dst = [ds_scr[h] for h in range(nh)]
            stb = [t.astype(BF16) for t in st]
            dstb = [t.astype(BF16) for t in dst]
            a_raw = [_nt(qr[:, sl], kr[:, sl]) for sl in hs]
            da_raw = [_nt(dov[:, sl], v[:, sl]) for sl in hs]
            dq_int = [_nn(dov[:, sl], stb[h]) for h, sl in enumerate(hs)]
            dk_int = [_nn(v[:, sl], dstb[h]) for h, sl in enumerate(hs)]
            dv_int = [_nt(ke[:, sl], dstb[h]) for h, sl in enumerate(hs)]
            ds_new = [_tn(dov[:, sl], qe[:, sl]) for sl in hs]
            a = [jnp.where(mask, t, 0.0).astype(BF16) for t in a_raw]
            da = [jnp.where(mask, t, 0.0).astype(BF16) for t in da_raw]
            dv_parts = [_tn(a[h], dov[:, sl]) + dv_int[h] for h, sl in enumerate(hs)]
            dq_parts = [_nn(da[h], kr[:, sl]) * e_qr[:, sl] + dq_int[h] * e_b[:, sl] for h, sl in enumerate(hs)]
            dki_parts = [dk_int[h] * e_ke[:, sl] for h, sl in enumerate(hs)]
            dk_parts = [_tn(da[h], qr[:, sl]) * e_kr[:, sl] + dki_parts[h] for h, sl in enumerate(hs)]
            dbt_parts = [dec[:, sl] * jnp.sum(st[h] * dst[h], axis=0, keepdims=True) for h, sl in enumerate(hs)]
            for h, sl in enumerate(hs):
                ds_scr[h] = dst[h] * dec[:, sl] + ds_new[h]
            dq = jnp.concatenate(dq_parts, axis=1)
            dk = jnp.concatenate(dk_parts, axis=1)
            dki = jnp.concatenate(dki_parts, axis=1)
            dv = jnp.concatenate(dv_parts, axis=1)
            dbt = jnp.concatenate(dbt_parts, axis=1) + jnp.sum(kk * dki, axis=0, keepdims=True)
            db = qh * dq - kk * dk
            dg = _cumsum_rows(db, not rev) + dbt
            df = dg / fg - dk
            dz_ref[rows, :] = (df * (1.0 - lb) * sig * (1.0 - sig)).astype(BF16)
            dlb_ref[...] += jnp.sum(df * (1.0 - sig), axis=0, keepdims=True)
            dqr = dq * _dsilu(q_ref[rows, :])
            if has_add:
                dqr = dqr + dqa_ref[rows, :]
                dv = dv + dia_ref[rows, :]
            dq_ref[rows, :] = dqr.astype(dq_ref.dtype)
            di_ref[rows, :] = dv.astype(di_ref.dtype)

        @pl.when(m == n_steps - 1)
        def _():
            lb = _lower_bound(hlb_ref, layer)
            if layer == 0:
                dlb_ref[...] = jnp.zeros_like(dlb_ref)
            else:
                dlb_ref[...] = dlb_ref[...] * lb * (1.0 - lb)

    cspec = lambda col: pl.BlockSpec((HG_STEP * HG_CHUNK, D), lambda m: (block(m), col))
    ins = [parts, parts, parts, hlb, do, states]
    specs = [cspec(0), cspec(fcol), cspec(3), pl.BlockSpec((2, D), lambda m: (0, 1 if rev else 0)), cspec(0),
             pl.BlockSpec((HG_STEP, nh, HEAD, HEAD), lambda m: (step(m), 0, 0, 0))]
    dlb_spec = pl.BlockSpec((1, D), lambda m: (0, 0))
    dlb_shape = jax.ShapeDtypeStruct((1, D), F32)
    if has_add:
        return pl.pallas_call(
            body, name=name, grid=(n_steps,),
            in_specs=specs + [cspec(0), cspec(0), cspec(0), pl.BlockSpec(memory_space=pl.ANY)],
            out_specs=[pl.BlockSpec((HG_STEP * HG_CHUNK, 4 * D), lambda m: (block(m), 0)), dlb_spec],
            out_shape=[jax.ShapeDtypeStruct(dparts.shape, dparts.dtype), dlb_shape],
            scratch_shapes=[pltpu.VMEM((nh, HEAD, HEAD), F32)], input_output_aliases={len(ins) + 3: 0},
            compiler_params=_params("arbitrary"),
        )(*ins, *other, dparts)
    return pl.pallas_call(
        body, name=name, grid=(n_steps,), in_specs=specs,
        out_specs=[cspec(0), cspec(0), cspec(0), dlb_spec],
        out_shape=[jax.ShapeDtypeStruct((T, D), F32), jax.ShapeDtypeStruct((T, D), BF16),
                   jax.ShapeDtypeStruct((T, D), F32), dlb_shape],
        scratch_shapes=[pltpu.VMEM((nh, HEAD, HEAD), F32)],
        compiler_params=_params("arbitrary"),
    )(*ins)


def _sgu_ln(gv, lnw_ref, lnb_ref):
    mu = jnp.mean(gv, axis=-1, keepdims=True)
    xc = gv - mu
    rstd = lax.rsqrt(jnp.mean(xc * xc, axis=-1, keepdims=True) + LN_EPS)
    xh = xc * rstd
    return xh, rstd, xh * lnw_ref[...] + lnb_ref[...]


def _sgu_fwd(parts, lnw, lnb, w, bt, name):
    T = parts.shape[0]
    D = lnw.shape[1]
    G = D // HEAD

    def body(u_ref, v_ref, lnw_ref, lnb_ref, w_ref, bt_ref, ya_ref):
        gu = _gelu(u_ref[...])
        _, _, vn = _sgu_ln(_gelu(v_ref[...]), lnw_ref, lnb_ref)
        vnb = vn.astype(BF16)
        for g in range(G):
            sl = slice(g * HEAD, (g + 1) * HEAD)
            mixed = _nn(w_ref[g], vnb[:, sl]) + bt_ref[:, g:g + 1]
            ya_ref[:, sl] = (gu[:, sl] * mixed).astype(BF16)

    return pl.pallas_call(
        body, name=name, grid=(T // SGU_CHUNK,),
        in_specs=[pl.BlockSpec((SGU_CHUNK, D), lambda n: (n, 4)), pl.BlockSpec((SGU_CHUNK, D), lambda n: (n, 5)),
                  pl.BlockSpec((1, D), lambda n: (0, 0)), pl.BlockSpec((1, D), lambda n: (0, 0)),
                  pl.BlockSpec((G, SGU_CHUNK, SGU_CHUNK), lambda n: (0, 0, 0)),
                  pl.BlockSpec((SGU_CHUNK, G), lambda n: (0, 0))],
        out_specs=pl.BlockSpec((SGU_CHUNK, D), lambda n: (n, 0)),
        out_shape=jax.ShapeDtypeStruct((T, D), BF16),
        compiler_params=_params("parallel"),
    )(parts, parts, lnw, lnb, w, bt)


def _sgu_bwd(parts, dya, lnw, lnb, w, bt, dparts, name):
    T = parts.shape[0]
    D = lnw.shape[1]
    G = D // HEAD

    def body(u_ref, v_ref, dya_ref, lnw_ref, lnb_ref, w_ref, bt_ref, dparts_in,
             duv_ref, dw_ref, dbt_ref, dlnw_ref, dlnb_ref, dvn_scr):
        du_ref = duv_ref.at[:, 0:D]
        dv_ref = duv_ref.at[:, D:2 * D]
        n = pl.program_id(0)

        @pl.when(n == 0)
        def _():
            dw_ref[...] = jnp.zeros_like(dw_ref)
            dbt_ref[...] = jnp.zeros_like(dbt_ref)
            dlnw_ref[...] = jnp.zeros_like(dlnw_ref)
            dlnb_ref[...] = jnp.zeros_like(dlnb_ref)

        gu, dgu = _gelu_both(u_ref[...])
        gv, dgv_dv = _gelu_both(v_ref[...])
        xh, rstd, vn = _sgu_ln(gv, lnw_ref, lnb_ref)
        vnb = vn.astype(BF16)
        dya = dya_ref[...]
        lane = lax.broadcasted_iota(jnp.int32, (SGU_CHUNK, G), 1)
        dbt = jnp.zeros((SGU_CHUNK, G), F32)
        for g in range(G):
            sl = slice(g * HEAD, (g + 1) * HEAD)
            wg = w_ref[g]
            mixed = _nn(wg, vnb[:, sl]) + bt_ref[:, g:g + 1]
            dmix = dya[:, sl] * gu[:, sl]
            du_ref[:, sl] = (dya[:, sl] * mixed * dgu[:, sl]).astype(BF16)
            dmb = dmix.astype(BF16)
            dvn_scr[:, sl] = _tn(wg, dmb)
            dw_ref[g] += _nt(dmb, vnb[:, sl])
            dbt = dbt + jnp.where(lane == g, jnp.sum(dmix, axis=1, keepdims=True), 0.0)
        dbt_ref[...] += dbt
        dvn = dvn_scr[...]
        dlnw_ref[...] += jnp.sum(dvn * xh, axis=0, keepdims=True)
        dlnb_ref[...] += jnp.sum(dvn, axis=0, keepdims=True)
        dxh = dvn * lnw_ref[...]
        dgv = rstd * (dxh - jnp.mean(dxh, axis=-1, keepdims=True) - xh * jnp.mean(dxh * xh, axis=-1, keepdims=True))
        dv_ref[...] = (dgv * dgv_dv).astype(BF16)

    row = lambda col: pl.BlockSpec((SGU_CHUNK, D), lambda n: (n, col))
    vec = pl.BlockSpec((1, D), lambda n: (0, 0))
    wsp = pl.BlockSpec((G, SGU_CHUNK, SGU_CHUNK), lambda n: (0, 0, 0))
    bsp = pl.BlockSpec((SGU_CHUNK, G), lambda n: (0, 0))
    return pl.pallas_call(
        body, name=name, grid=(T // SGU_CHUNK,),
        in_specs=[row(4), row(5), row(0), vec, vec, wsp, bsp, pl.BlockSpec(memory_space=pl.ANY)],
        out_specs=[pl.BlockSpec((SGU_CHUNK, 2 * D), lambda n: (n, 2)), wsp, bsp, vec, vec],
        out_shape=[jax.ShapeDtypeStruct(dparts.shape, dparts.dtype),
                   jax.ShapeDtypeStruct((G, SGU_CHUNK, SGU_CHUNK), F32), jax.ShapeDtypeStruct((SGU_CHUNK, G), F32),
                   jax.ShapeDtypeStruct((1, D), F32), jax.ShapeDtypeStruct((1, D), F32)],
        scratch_shapes=[pltpu.VMEM((SGU_CHUNK, D), F32)], input_output_aliases={7: 0},
        compiler_params=_params("arbitrary"),
    )(parts, parts, dya, lnw, lnb, w, bt, dparts)


TBT = 256
VMEM_LIMIT_TOKEN_OUT = 58 * 1024 * 1024


def _rows_weight_spec(wg):
    return pl.BlockSpec(wg.shape, lambda i: (0, 0, 0))


def _full(w_ref):
    return w_ref[...].reshape(w_ref.shape[0] * w_ref.shape[1], w_ref.shape[2])


def _token_out_fwd(o, parts, ya, x, mod, hnw, nw2, wa, wb, wo, ctx_rows, name):
    T, D = x.shape
    nh = D // HEAD
    cb = ctx_rows // TBT

    def body(o_ref, og_ref, ga_ref, gb_ref, ya_ref, x_ref, mod_ref, hnw_ref, nw2_ref, wa_ref, wb_ref, wo_ref,
             yb_ref, pa_ref, pb_ref, mg_ref, tmo_ref, xm_ref, h2_ref):
        ov = o_ref[...]
        so = _silu(og_ref[...])
        nw = hnw_ref[...]
        for h in range(nh):
            sl = slice(h * HEAD, (h + 1) * HEAD)
            seg = ov[:, sl]
            r = lax.rsqrt(jnp.mean(seg * seg, axis=-1, keepdims=True) + RMS_EPS)
            yb_ref[:, sl] = (seg * r * nw * so[:, sl]).astype(BF16)
        pa = _nn(ya_ref[...], _full(wa_ref))
        pb = _nn(yb_ref[...], _full(wb_ref))
        pa_ref[...] = pa.astype(BF16)
        pb_ref[...] = pb.astype(BF16)
        mg = (_sigmoid(ga_ref[...]) * pa + _sigmoid(gb_ref[...]) * pb).astype(BF16)
        mg_ref[...] = mg
        out = _nn(mg, _full(wo_ref))
        tmo_ref[...] = out.astype(BF16)
        xm = x_ref[...] + mod_ref[2:3, :] * out
        xm_ref[...] = xm
        r = lax.rsqrt(jnp.mean(xm * xm, axis=-1, keepdims=True) + RMS_EPS)
        h2_ref[...] = (xm * r * nw2_ref[...] * (1.0 + mod_ref[4:5, :]) + mod_ref[3:4, :]).astype(BF16)

    row = lambda col: pl.BlockSpec((TBT, D), lambda i: (i, col))
    wsp = _rows_weight_spec(wa)
    sd = lambda dt: jax.ShapeDtypeStruct((T, D), dt)
    return pl.pallas_call(
        body, name=name, grid=(T // TBT,),
        in_specs=[row(0), row(6), row(7), row(8), row(0), row(0),
                  pl.BlockSpec((None, N_MOD, D), lambda i: (_stream_of(i, cb), 0, 0)),
                  pl.BlockSpec((1, HEAD), lambda i: (0, 0)), pl.BlockSpec((1, D), lambda i: (0, 0)), wsp, wsp, wsp],
        out_specs=[row(0)] * 7,
        out_shape=[sd(BF16), sd(BF16), sd(BF16), sd(BF16), sd(BF16), sd(F32), sd(BF16)],
        compiler_params=_params("parallel", vmem=VMEM_LIMIT_TOKEN_OUT),
    )(o, parts, parts, parts, ya, x, mod, hnw, nw2, wa, wb, wo)


def _token_out_bwd(dx, tmo, pa, pb, o, parts, mod, hnw, wa, wb, wo, ctx_rows, name):
    T, D = dx.shape
    nh = D // HEAD
    cb = ctx_rows // TBT

    def body(dx_ref, tmo_ref, pa_ref, pb_ref, o_ref, og_ref, ga_ref, gb_ref, mod_ref, hnw_ref, wa_ref, wb_ref, wo_ref,
             dout_ref, dpa_ref, dpb_ref, dgate_ref, dya_ref, do_ref, dg1_ref, dhnw_ref):
        i = pl.program_id(0)

        @pl.when(i == 0)
        def _():
            dhnw_ref[...] = jnp.zeros_like(dhnw_ref)

        @pl.when((i == 0) | (i == cb))
        def _():
            dg1_ref[...] = jnp.zeros_like(dg1_ref)

        dxv = dx_ref[...]
        dg1_ref[...] += jnp.sum(dxv * tmo_ref[...], axis=0, keepdims=True)
        dout = (dxv * mod_ref[2:3, :]).astype(BF16)
        dout_ref[...] = dout
        dmg = _nt(dout, _full(wo_ref))
        sa = _sigmoid(ga_ref[...])
        sb = _sigmoid(gb_ref[...])
        dpa = (dmg * sa).astype(BF16)
        dpb = (dmg * sb).astype(BF16)
        dpa_ref[...] = dpa
        dpb_ref[...] = dpb
        dgate_ref[:, D:2 * D] = (dmg * pa_ref[...] * sa * (1.0 - sa)).astype(BF16)
        dgate_ref[:, 2 * D:3 * D] = (dmg * pb_ref[...] * sb * (1.0 - sb)).astype(BF16)
        dya_ref[...] = _nt(dpa, _full(wa_ref))
        dyb = _nt(dpb, _full(wb_ref))
        so, dso = _silu_both(og_ref[...])
        ov = o_ref[...]
        nw = hnw_ref[...]
        dnw = jnp.zeros((1, HEAD), F32)
        for h in range(nh):
            sl = slice(h * HEAD, (h + 1) * HEAD)
            seg = ov[:, sl]
            r = lax.rsqrt(jnp.mean(seg * seg, axis=-1, keepdims=True) + RMS_EPS)
            oh = seg * r
            dn = dyb[:, sl] * so[:, sl]
            dgate_ref[:, sl] = (dyb[:, sl] * oh * nw * dso[:, sl]).astype(BF16)
            dnw = dnw + jnp.sum(dn * oh, axis=0, keepdims=True)
            doh = dn * nw
            do_ref[:, sl] = r * (doh - oh * jnp.mean(doh * oh, axis=-1, keepdims=True))
        dhnw_ref[...] += dnw

    row = lambda col: pl.BlockSpec((TBT, D), lambda i: (i, col))
    wsp = _rows_weight_spec(wa)
    sd = lambda dt: jax.ShapeDtypeStruct((T, D), dt)
    return pl.pallas_call(
        body, name=name, grid=(T // TBT,),
        in_specs=[row(0), row(0), row(0), row(0), row(0), row(6), row(7), row(8),
                  pl.BlockSpec((None, N_MOD, D), lambda i: (_stream_of(i, cb), 0, 0)),
                  pl.BlockSpec((1, HEAD), lambda i: (0, 0)), wsp, wsp, wsp],
        out_specs=[row(0)] * 3 + [pl.BlockSpec((TBT, 3 * D), lambda i: (i, 2)), row(0), row(0),
                                  pl.BlockSpec((None, 1, D), lambda i: (_stream_of(i, cb), 0, 0)),
                                  pl.BlockSpec((1, HEAD), lambda i: (0, 0))],
        out_shape=[sd(BF16)] * 3 + [jax.ShapeDtypeStruct((T, 9 * D), BF16), sd(F32), sd(F32),
                                    jax.ShapeDtypeStruct((2, 1, D), F32), jax.ShapeDtypeStruct((1, HEAD), F32)],
        compiler_params=_params("arbitrary", vmem=VMEM_LIMIT_TOKEN_OUT),
    )(dx, tmo, pa, pb, o, parts, parts, parts, mod, hnw, wa, wb, wo)


def _conv_geometry(i, nb, cb):
    is_ctx = i < cb
    first = (i == 0) | (i == cb)
    last = (i == cb - 1) | (i == nb - 1)
    row = lax.broadcasted_iota(jnp.int32, (TB + 2 * GRID_W, 1), 0)
    w = row & (GRID_W - 1)
    left_ok = (w != 0) | is_ctx
    right_ok = (w != GRID_W - 1) | is_ctx
    return is_ctx, first, last, left_ok, right_ok


def _ext(p_ref, m_ref, n_ref, first, last):
    return jnp.concatenate([jnp.where(first, 0.0, p_ref[...]), m_ref[...], jnp.where(last, 0.0, n_ref[...])], axis=0)


def _shift_prev(e, ok):
    return jnp.where(ok, pltpu.roll(e, 1, 0), 0.0)


def _shift_next(e, ok):
    return jnp.where(ok, pltpu.roll(e, e.shape[0] - 1, 0), 0.0)


def _halo_specs(cbk, n64, coff=0):
    r = TB // GRID_W
    prev = pl.BlockSpec((GRID_W, cbk), lambda j, i: (jnp.maximum(r * i - 1, 0), j + coff))
    main = pl.BlockSpec((TB, cbk), lambda j, i: (i, j + coff))
    nxt = pl.BlockSpec((GRID_W, cbk), lambda j, i: (jnp.minimum(r * i + r, n64 - 1), j + coff))
    return [prev, main, nxt]


def _conv_cblock(dff):
    return _tile(dff, 1408)


def _conv_fwd(up, cw, cbias, ctx_rows, name):
    T, dff = up.shape[0], up.shape[1] // 2
    cbk = _conv_cblock(dff)
    nb, cb = T // TB, ctx_rows // TB
    nvb = dff // cbk

    def body(ap_ref, a_ref, an_ref, v_ref, cw_ref, cb_ref, ac_ref, act_ref):
        i = pl.program_id(1)
        is_ctx, first, last, lok, rok = _conv_geometry(i, nb, cb)
        e = _ext(ap_ref, a_ref, an_ref, first, last)
        el = _shift_prev(e, lok)
        er = _shift_next(e, rok)
        cwv = cw_ref[...]

        def comb(dr, lo):
            sl = slice(lo, lo + TB)
            return cwv[3 * dr:3 * dr + 1] * el[sl] + cwv[3 * dr + 1:3 * dr + 2] * e[sl] + cwv[3 * dr + 2:3 * dr + 3] * er[sl]

        out = comb(1, GRID_W) + jnp.where(is_ctx, 0.0, comb(0, 0) + comb(2, 2 * GRID_W))
        a_c = out + cb_ref[...]
        ac_ref[...] = a_c
        act_ref[...] = (_gelu(a_c) * v_ref[...]).astype(BF16)

    main = pl.BlockSpec((TB, cbk), lambda j, i: (i, j))
    return pl.pallas_call(
        body, name=name, grid=(dff // cbk, nb),
        in_specs=_halo_specs(cbk, T // GRID_W) + [pl.BlockSpec((TB, cbk), lambda j, i: (i, j + nvb)),
                                                 pl.BlockSpec((9, cbk), lambda j, i: (0, j)),
                                                 pl.BlockSpec((1, cbk), lambda j, i: (0, j))],
        out_specs=[main, main],
        out_shape=[jax.ShapeDtypeStruct((T, dff), F32), jax.ShapeDtypeStruct((T, dff), BF16)],
        compiler_params=_params("parallel", "parallel"),
    )(up, up, up, up, cw, cbias)


def _conv_bwd(up, ac, dact, cw, ctx_rows, name):
    T, dff = up.shape[0], up.shape[1] // 2
    cbk = _conv_cblock(dff)
    nb, cb = T // TB, ctx_rows // TB
    nvb = dff // cbk

    def body(ap_ref, a_ref, an_ref, vp_ref, v_ref, vn_ref, cp_ref, c_ref, cn_ref, dp_ref, d_ref, dn_ref, cw_ref,
             da_ref, dv_ref, dcw_ref, dcb_ref):
        i = pl.program_id(1)

        @pl.when(i == 0)
        def _():
            dcw_ref[...] = jnp.zeros_like(dcw_ref)
            dcb_ref[...] = jnp.zeros_like(dcb_ref)

        is_ctx, first, last, lok, rok = _conv_geometry(i, nb, cb)
        gl, dgl = _gelu_both(_ext(cp_ref, c_ref, cn_ref, first, last))
        g = _ext(dp_ref, d_ref, dn_ref, first, last) * _ext(vp_ref, v_ref, vn_ref, first, last) * dgl
        dv_ref[...] = (d_ref[...] * gl[GRID_W:GRID_W + TB]).astype(BF16)
        gm = _shift_prev(g, lok)
        gp = _shift_next(g, rok)
        cwv = cw_ref[...]

        def comb(dr, lo):
            sl = slice(lo, lo + TB)
            return cwv[3 * dr:3 * dr + 1] * gp[sl] + cwv[3 * dr + 1:3 * dr + 2] * g[sl] + cwv[3 * dr + 2:3 * dr + 3] * gm[sl]

        da = comb(1, GRID_W) + jnp.where(is_ctx, 0.0, comb(0, 2 * GRID_W) + comb(2, 0))
        da_ref[...] = da.astype(BF16)
        e = _ext(ap_ref, a_ref, an_ref, first, last)
        taps = [_shift_prev(e, lok), e, _shift_next(e, rok)]
        gmain = g[GRID_W:GRID_W + TB]
        dcb_ref[...] += jnp.sum(gmain, axis=0, keepdims=True)
        vert = jnp.where(is_ctx, 0.0, 1.0)
        for dr in range(3):
            sl = slice(dr * GRID_W, dr * GRID_W + TB)
            for dw in range(3):
                s = jnp.sum(gmain * taps[dw][sl], axis=0, keepdims=True)
                if dr != 1:
                    s = s * vert
                k = 3 * dr + dw
                dcw_ref[k:k + 1, :] += s

    main = pl.BlockSpec((TB, cbk), lambda j, i: (i, j))
    halo = _halo_specs(cbk, T // GRID_W)
    acc9 = pl.BlockSpec((9, cbk), lambda j, i: (0, j))
    acc1 = pl.BlockSpec((1, cbk), lambda j, i: (0, j))
    return pl.pallas_call(
        body, name=name, grid=(dff // cbk, nb),
        in_specs=halo + _halo_specs(cbk, T // GRID_W, nvb) + halo + halo + [acc9],
        out_specs=[main, main, acc9, acc1],
        out_shape=[jax.ShapeDtypeStruct((T, dff), BF16), jax.ShapeDtypeStruct((T, dff), BF16),
                   jax.ShapeDtypeStruct((9, dff), F32), jax.ShapeDtypeStruct((1, dff), F32)],
        compiler_params=_params("parallel", "arbitrary"),
    )(up, up, up, up, up, up, ac, ac, ac, dact, dact, dact, cw)


def _ffn_out_fwd(act, xm, mod, wd, ctx_rows, name):
    T, D = xm.shape
    dff = act.shape[1]
    cb = ctx_rows // TB

    def body(act_ref, x_ref, mod_ref, w_ref, xo_ref, fo_ref):
        out = _nn(act_ref[...], _full(w_ref))
        fo_ref[...] = out
        xo_ref[...] = x_ref[...] + mod_ref[5:6, :] * out

    row = pl.BlockSpec((TB, D), lambda i: (i, 0))
    return pl.pallas_call(
        body, name=name, grid=(T // TB,),
        in_specs=[pl.BlockSpec((TB, dff), lambda i: (i, 0)), row,
                  pl.BlockSpec((None, N_MOD, D), lambda i: (_stream_of(i, cb), 0, 0)),
                  _rows_weight_spec(wd)],
        out_specs=[row, row],
        out_shape=[jax.ShapeDtypeStruct((T, D), F32), jax.ShapeDtypeStruct((T, D), F32)],
        compiler_params=_params("parallel"),
    )(act, xm, mod, wd)


def _ffn_out_bwd(dx, fo, mod, wd, ctx_rows, name):
    T, D = dx.shape
    dff = N_CHIPS * wd.shape[1]
    cb = ctx_rows // TB

    def body(dx_ref, fo_ref, mod_ref, w_ref, dout_ref, dact_ref, dg2_ref):
        i = pl.program_id(0)

        @pl.when((i == 0) | (i == cb))
        def _():
            dg2_ref[...] = jnp.zeros_like(dg2_ref)

        dxv = dx_ref[...]
        dg2_ref[...] += jnp.sum(dxv * fo_ref[...], axis=0, keepdims=True)
        dout = (dxv * mod_ref[5:6, :]).astype(BF16)
        dout_ref[...] = dout
        dact_ref[...] = _nt(dout, _full(w_ref))

    row = pl.BlockSpec((TB, D), lambda i: (i, 0))
    return pl.pallas_call(
        body, name=name, grid=(T // TB,),
        in_specs=[row, row, pl.BlockSpec((None, N_MOD, D), lambda i: (_stream_of(i, cb), 0, 0)),
                  _rows_weight_spec(wd)],
        out_specs=[row, pl.BlockSpec((TB, dff), lambda i: (i, 0)),
                   pl.BlockSpec((None, 1, D), lambda i: (_stream_of(i, cb), 0, 0))],
        out_shape=[jax.ShapeDtypeStruct((T, D), BF16), jax.ShapeDtypeStruct((T, dff), F32),
                   jax.ShapeDtypeStruct((2, 1, D), F32)],
        compiler_params=_params("arbitrary"),
    )(dx, fo, mod, wd)


def _loss_bwd(x, target, fw, ctx_rows, name):
    T, D = x.shape
    cb = ctx_rows // TB

    def body(x_ref, t_ref, fw_ref, dx_ref, loss_ref, dfw_ref):
        i = pl.program_id(0)

        @pl.when(i == 0)
        def _():
            loss_ref[...] = jnp.zeros_like(loss_ref)
            dfw_ref[...] = jnp.zeros_like(dfw_ref)

        @pl.when(i < cb)
        def _():
            dx_ref[...] = jnp.zeros_like(dx_ref)

        @pl.when(i >= cb)
        def _():
            xv = x_ref[...]
            r = lax.rsqrt(jnp.mean(xv * xv, axis=-1, keepdims=True) + RMS_EPS)
            xh = xv * r
            fwv = fw_ref[...]
            err = xh * fwv - t_ref[...]
            loss_ref[...] += (0.5 / D) * jnp.sum(err * err)
            dy = err * (1.0 / D)
            dfw_ref[...] += jnp.sum(dy * xh, axis=0, keepdims=True)
            dxh = dy * fwv
            dx_ref[...] = r * (dxh - xh * jnp.mean(dxh * xh, axis=-1, keepdims=True))

    row = pl.BlockSpec((TB, D), lambda i: (i, 0))
    return pl.pallas_call(
        body, name=name, grid=(T // TB,),
        in_specs=[row, pl.BlockSpec((TB, D), lambda i: (jnp.maximum(i - cb, 0), 0)), pl.BlockSpec((1, D), lambda i: (0, 0))],
        out_specs=[row, pl.BlockSpec((1, 128), lambda i: (0, 0)), pl.BlockSpec((1, D), lambda i: (0, 0))],
        out_shape=[jax.ShapeDtypeStruct((T, D), F32), jax.ShapeDtypeStruct((1, 128), F32),
                   jax.ShapeDtypeStruct((1, D), F32)],
        compiler_params=_params("arbitrary"),
    )(x, target, fw)


def _adamw(w, gs, m, v, name):
    L, R, C = w.shape
    assert len(gs) == L
    rb = _rows_tile(R, max(16, (1 << 18) // C // 16 * 16))
    bc1 = 1.0 - ADAM_B1 ** ADAM_STEP
    bc2 = 1.0 - ADAM_B2 ** ADAM_STEP

    def body(w_ref, m_ref, v_ref, *rest):
        g_refs, (g_ref, d_ref, nm_ref, nv_ref) = rest[:L], rest[L:]
        layer = pl.program_id(0)
        for li in range(L):
            @pl.when(layer == li)
            def _():
                gv = g_refs[li][...]
                g_ref[...] = gv
                nm = ADAM_B1 * m_ref[...] + (1.0 - ADAM_B1) * gv
                nv = ADAM_B2 * v_ref[...] + (1.0 - ADAM_B2) * (gv * gv)
                nm_ref[...] = nm
                nv_ref[...] = nv
                d_ref[...] = -ADAM_LR * ((nm / bc1) / (jnp.sqrt(nv / bc2) + ADAM_EPS) + ADAM_WD * w_ref[...])

    blk = pl.BlockSpec((None, rb, C), lambda l, i: (l, i, 0))
    gblk = pl.BlockSpec((rb, C), lambda l, i: (i, 0))
    sd = jax.ShapeDtypeStruct((L, R, C), F32)
    return pl.pallas_call(
        body, name=name, grid=(L, R // rb), in_specs=[blk] * 3 + [gblk] * L, out_specs=[blk] * 4, out_shape=[sd] * 4,
        compiler_params=_params("parallel", "parallel"),
    )(w, m, v, *gs)


def _local_step(xs, cv, target, W, layer_weights, on_layer_grads, ctx_rows):
    T, D = xs.shape
    depth = W["norm1_w"].shape[0]
    saved = []
    X = xs
    for l in range(depth):
        s = {}
        Wl = layer_weights(l, X)
        mod_all, sa = _mod_fwd(cv, Wl["ada_w"], W["ada_b"][l][None, :] + Wl["token"], f"mod_fwd_{l}")
        mod = mod_all[:2].reshape(2, N_MOD, D)
        h1 = _norm_mod(X, W["norm1_w"][l][None, :], mod, 0, ctx_rows, f"norm1_{l}")
        parts = _mm_nn_w(h1, Wl["w_in"], F32, f"in_proj_{l}")
        o_f, st_f = _hgrn_fwd(parts, W["hlb"], l, False, ctx_rows, f"hgrn_fwd_f_{l}")
        o, st_b = _hgrn_fwd(parts, W["hlb"], l, True, ctx_rows, f"hgrn_fwd_b_{l}", o_add=o_f)
        ya = _sgu_fwd(parts, W["sgu_ln_w"][l][None, :], W["sgu_ln_b"][l][None, :], W["sgu_w"][l], W["sgu_bt"][l],
                      f"sgu_fwd_{l}")
        Wl.update(Wl.pop("late")(ya))
        yb, pa, pb, mg, tmo, xm, h2 = _token_out_fwd(o, parts, ya, X, mod, W["hnw"][l][None, :] + Wl["late_token"],
                                                     W["norm2_w"][l][None, :], Wl["w_a"], Wl["w_b"], Wl["w_o"], ctx_rows,
                                                     f"token_out_fwd_{l}")
        up = _mm_nn_w(h2, Wl["w_up"], F32, f"up_proj_{l}")
        ac, act = _conv_fwd(up, Wl["conv_w"], W["conv_b"][l][None, :], ctx_rows, f"conv_fwd_{l}")
        xo, fo = _ffn_out_fwd(act, xm, mod, Wl["w_down"], ctx_rows, f"ffn_out_fwd_{l}")
        s.update(X=X, Wl=Wl, mod=mod, mod_all=mod_all, sa=sa, h1=h1, parts=parts, o=o, st_f=st_f, st_b=st_b, ya=ya, yb=yb,
                 pa=pa, pb=pb, mg=mg, tmo=tmo, xm=xm, h2=h2, up=up, ac=ac, act=act, fo=fo)
        saved.append(s)
        X = xo

    dX, loss_row, dfw = _loss_bwd(X, target, W["final_norm_w"][None, :], ctx_rows, "loss_bwd")
    G = {k: [None] * depth for k in ("ada_b", "norm1_w", "sgu_ln_w", "sgu_ln_b", "sgu_w", "sgu_b", "hlb1", "hnw", "norm2_w",
                                     "conv_w", "conv_b", "dmod")}
    dcv = jnp.zeros_like(cv)
    for l in reversed(range(depth)):
        s = saved[l]
        mod, Wl = s["mod"], s["Wl"]
        big = {}
        dout2, dact, dg2 = _ffn_out_bwd(dX, s["fo"], mod, Wl["w_down"], ctx_rows, f"ffn_out_bwd_{l}")
        big["w_down"] = _mm_tn(s["act"], dout2, F32, f"dw_down_{l}")
        da, dv, dcw, dcb = _conv_bwd(s["up"], s["ac"], dact, Wl["conv_w"], ctx_rows, f"conv_bwd_{l}")
        G["conv_w"][l], G["conv_b"][l] = dcw, dcb[0]
        big["w_up"] = _mm_tn(s["h2"], (da, dv), F32, f"dw_up_{l}", out_chips=True)
        dh2 = _mm_nt_w((da, dv), Wl["w_up"], F32, f"dh2_{l}")
        dxm, dm2, dnw2 = _norm_mod_bwd(dh2, s["xm"], dX, W["norm2_w"][l][None, :], mod, 3, ctx_rows, f"norm2_bwd_{l}")
        G["norm2_w"][l] = dnw2[0]
        (dout1, dpa, dpb, dparts, dya, do, dg1, dhnw) = _token_out_bwd(
            dxm, s["tmo"], s["pa"], s["pb"], s["o"], s["parts"], mod, W["hnw"][l][None, :], Wl["w_a"], Wl["w_b"], Wl["w_o"],
            ctx_rows, f"token_out_bwd_{l}")
        G["hnw"][l] = dhnw[0]
        big["w_o"] = _mm_tn(s["mg"], dout1, F32, f"dw_o_{l}")
        big["w_a"] = _mm_tn(s["ya"], dpa, F32, f"dw_a_{l}")
        big["w_b"] = _mm_tn(s["yb"], dpb, F32, f"dw_b_{l}")
        tok = on_layer_grads(l, "early", big)
        dparts, dsw, dsbt, dlnw, dlnb = _sgu_bwd(s["parts"], dya, W["sgu_ln_w"][l][None, :], W["sgu_ln_b"][l][None, :] + tok,
                                                 W["sgu_w"][l], W["sgu_bt"][l], dparts, f"sgu_bwd_{l}")
        G["sgu_w"][l], G["sgu_b"][l], G["sgu_ln_w"][l], G["sgu_ln_b"][l] = dsw, dsbt.T, dlnw[0], dlnb[0]
        dq_f, dz_f, di_f, dlb_f = _hgrn_bwd(s["parts"], W["hlb"], do, s["st_f"], l, False, ctx_rows, f"hgrn_bwd_f_{l}")
        dparts, dlb_b = _hgrn_bwd(s["parts"], W["hlb"], do, s["st_b"], l, True, ctx_rows, f"hgrn_bwd_b_{l}",
                                  other=(dq_f, dz_f, di_f), dparts=dparts)
        G["hlb1"][l] = jnp.concatenate([dlb_f[0], dlb_b[0]])
        tok = on_layer_grads(l, "late", {"w_in": _mm_tn(s["h1"], dparts, F32, f"dw_in_{l}", out_chips=True)})
        dh1 = _mm_nt_w(dparts, Wl["w_in"], F32, f"dh1_{l}")
        tok = tok + on_layer_grads(l, "end", {"after": dh1})
        dX, dm1, dnw1 = _norm_mod_bwd(dh1, s["X"], dxm, W["norm1_w"][l][None, :] + tok, mod, 0, ctx_rows, f"norm1_bwd_{l}")
        G["norm1_w"][l] = dnw1[0]
        dmod = jnp.concatenate([dm1, dg1, dm2, dg2], axis=1).reshape(2, N_MOD * D)
        dmod16 = jnp.concatenate([dmod, jnp.zeros((cv.shape[0] - 2, N_MOD * D), F32)], axis=0)
        G["ada_b"][l] = dmod[0] + dmod[1]
        G["dmod"][l] = dmod
        dcv = dcv + _cvec_bwd(dmod16, Wl["ada_w"], cv, f"dcvec_{l}")
    G["c_ctx"] = dcv[0]
    G["final_norm_w"] = dfw[0]
    return loss_row[0, 0], dX, G, saved[0]["sa"]


def _chip_peers(x, y, c):
    return [((1 - x, y, c), 2 * (1 - x) + y), ((x, 1 - y, c), 2 * x + 1 - y), ((1 - x, 1 - y, c), 2 * (1 - x) + 1 - y)]


def _rdma_call(ins, out_shapes, plan, n_remote, n_local, name, aliases=None):
    n_in, n_out = len(ins), len(out_shapes)

    def body(*refs):
        in_refs, out_refs = refs[:n_in], refs[n_in:n_in + n_out]
        send_sems, recv_sems, local_sems = refs[n_in + n_out:]
        x, y, c = lax.axis_index("x"), lax.axis_index("y"), lax.axis_index("c")
        remote, local = plan(in_refs, out_refs, x, y, c)
        assert len(remote) == n_remote and len(local) == n_local, (name, len(remote), len(local))
        copies = [pltpu.make_async_copy(s, d, local_sems.at[i]) for i, (s, d) in enumerate(local)]
        copies += [pltpu.make_async_remote_copy(src_ref=s, dst_ref=d, send_sem=send_sems.at[k], recv_sem=recv_sems.at[k],
                                                device_id=dev, device_id_type=pl.DeviceIdType.MESH)
                   for k, (s, d, dev) in enumerate(remote)]
        for cp in copies:
            cp.start()
        for cp in copies:
            cp.wait()

    hbm = pl.BlockSpec(memory_space=pltpu.HBM)
    return pl.pallas_call(
        body, name=name, in_specs=[hbm] * n_in, out_specs=[hbm] * n_out, out_shape=out_shapes,
        scratch_shapes=[pltpu.SemaphoreType.DMA((n_remote,)), pltpu.SemaphoreType.DMA((n_remote,)),
                        pltpu.SemaphoreType.DMA((max(n_local, 1),))],
        input_output_aliases=aliases or {},
    )(*ins)


DMA_PIECE_BYTES = 1 << 18
DMA_MAX_PIECES = 8


def _row_pieces(shape, dtype):
    rows = shape[0]
    row_bytes = jnp.dtype(dtype).itemsize
    for d in shape[1:]:
        row_bytes *= d
    n = 1
    while n < DMA_MAX_PIECES and rows % (2 * n * 16) == 0 and rows * row_bytes // (2 * n) >= DMA_PIECE_BYTES:
        n *= 2
    return [(i * (rows // n), rows // n) for i in range(n)]


def _half_pieces(o, c):
    r2 = o.shape[1] // 2
    return [pl.ds(c * r2 + st, sz) for st, sz in _row_pieces((r2,) + o.shape[2:], o.dtype)]


def _n_half_pieces(arrays):
    return sum(len(_row_pieces((a.shape[1] // 2,) + a.shape[2:], a.dtype)) for a in arrays)


def _plan_gather_far(lands, x, y, c):
    me = 2 * x + y
    return [(o.at[me, rows], o.at[me, rows], dev) for dev, _ in _chip_peers(x, y, c) for o in lands
            for rows in _half_pieces(o, c)]


def _plan_gather_near(lands, x, y, c):
    return [(o.at[idx, rows], o.at[idx, rows], (x, y, 1 - c)) for _, idx in _chip_peers(x, y, c) for o in lands
            for rows in _half_pieces(o, c)]


def _gather_weights(lands, name):
    n = len(lands)
    n_far = (N_CHIPS - 1) * _n_half_pieces(lands)

    def body(*refs):
        outs = refs[n:2 * n]
        far_send, far_recv, near_send, near_recv = refs[2 * n:]
        x, y, c = lax.axis_index("x"), lax.axis_index("y"), lax.axis_index("c")
        mk = lambda plan, send, recv: [
            pltpu.make_async_remote_copy(src_ref=s, dst_ref=d, send_sem=send.at[k], recv_sem=recv.at[k], device_id=dev,
                                         device_id_type=pl.DeviceIdType.MESH)
            for k, (s, d, dev) in enumerate(plan(outs, x, y, c))]
        far, near = mk(_plan_gather_far, far_send, far_recv), mk(_plan_gather_near, near_send, near_recv)
        assert len(far) == n_far and len(near) == n_far
        for cp in far:
            cp.start()
        for k in range(n_far):
            far[k].wait_recv()
            near[k].start()
        for k in range(n_far):
            near[k].wait_recv()
        for cp in far + near:
            cp.wait_send()

    hbm = pl.BlockSpec(memory_space=pltpu.HBM)
    sems = pltpu.SemaphoreType.DMA((n_far,))
    return pl.pallas_call(
        body, name=name, in_specs=[hbm] * n, out_specs=[hbm] * n,
        out_shape=[jax.ShapeDtypeStruct(a.shape, a.dtype) for a in lands],
        scratch_shapes=[sems, sems, sems, sems], input_output_aliases={i: i for i in range(n)},
    )(*lands)


def _gather_all(v, name):
    def plan(ins, outs, x, y, c):
        (s,), (o,) = ins, outs
        me = 4 * x + 2 * y + c
        flip = lambda a, f: 1 - a if f else a
        remote = [(s, o.at[me], (flip(x, m & 4), flip(y, m & 2), flip(c, m & 1))) for m in range(1, 8)]
        return remote, [(s, o.at[me])]

    return _rdma_call([v], [jax.ShapeDtypeStruct((8,) + v.shape, v.dtype)], plan, 7, 1, name)[0]


def _plan_pair(ins, lands, x, y, c):
    return [(a.at[j, 1 - c, pl.ds(st, sz)], o.at[j, pl.ds(st, sz)], (x, y, 1 - c)) for a, o in zip(ins, lands)
            for j in range(N_CHIPS) for st, sz in _row_pieces(a.shape[2:], a.dtype)]


def _n_pair_copies(parts):
    return N_CHIPS * sum(len(_row_pieces(a.shape[2:], a.dtype)) for a in parts)


def _reduce_pair(parts, name):
    shapes = [jax.ShapeDtypeStruct((N_CHIPS,) + a.shape[2:], a.dtype) for a in parts]
    return _rdma_call(parts, shapes, lambda ins, outs, x, y, c: (_plan_pair(ins, outs, x, y, c), []),
                      _n_pair_copies(parts), 0, name)


def _plan_chips(ins, lands, x, y, c):
    me = 2 * x + y
    return [(a.at[idx, pl.ds(st, sz)], o.at[me, pl.ds(st, sz)], dev) for dev, idx in _chip_peers(x, y, c)
            for a, o in zip(ins, lands) for st, sz in _row_pieces(a.shape[1:], a.dtype)]


def _n_chips_copies(parts):
    return (N_CHIPS - 1) * sum(len(_row_pieces(a.shape[1:], a.dtype)) for a in parts)


def _gather_pair(halves, name):
    def plan(ins, outs, x, y, c):
        return [(o.at[c, pl.ds(st, sz)], o.at[c, pl.ds(st, sz)], (x, y, 1 - c)) for o in outs
                for st, sz in _row_pieces(o.shape[1:], o.dtype)], []

    shapes = [jax.ShapeDtypeStruct(a.shape, a.dtype) for a in halves]
    n_remote = sum(len(_row_pieces(a.shape[1:], a.dtype)) for a in halves)
    return _rdma_call(halves, shapes, plan, n_remote, 0, name, aliases={i: i for i in range(len(halves))})


def _split_start(ins, lands, plan, n_remote, name):
    n_buf = len(ins) + len(lands)

    def body(*refs):
        in_refs, land_refs = refs[:len(ins)], refs[len(ins):n_buf]
        send_sems, recv_sems, token = refs[n_buf], refs[n_buf + 1], refs[-1]
        x, y, c = lax.axis_index("x"), lax.axis_index("y"), lax.axis_index("c")
        remote = plan(in_refs, land_refs, x, y, c)
        assert len(remote) == n_remote, (name, len(remote))
        for k, (s, d, dev) in enumerate(remote):
            pltpu.make_async_remote_copy(src_ref=s, dst_ref=d, send_sem=send_sems.at[k], recv_sem=recv_sems.at[k],
                                         device_id=dev, device_id_type=pl.DeviceIdType.MESH).start()
        token[...] = jnp.zeros_like(token)

    hbm = pl.BlockSpec(memory_space=pltpu.HBM)
    sem = pl.BlockSpec(memory_space=pltpu.SEMAPHORE)
    bufs = list(ins) + list(lands)
    out = pl.pallas_call(
        body, name=name, in_specs=[hbm] * n_buf,
        out_specs=(sem, sem) + (hbm,) * n_buf + (pl.BlockSpec(memory_space=pltpu.VMEM),),
        out_shape=(pltpu.SemaphoreType.DMA((n_remote,)), pltpu.SemaphoreType.DMA((n_remote,)))
        + tuple(pltpu.HBM(a.shape, a.dtype) for a in bufs) + (jax.ShapeDtypeStruct((8, 128), F32),),
        input_output_aliases={i: 2 + i for i in range(n_buf)},
        compiler_params=pltpu.CompilerParams(has_side_effects=pltpu.SideEffectType.DATAFLOW_SIDE_EFFECTING),
    )(*[pltpu.with_memory_space_constraint(a, pltpu.HBM) for a in bufs])
    return dict(send=out[0], recv=out[1], ins=list(out[2:2 + len(ins)]), lands=list(out[2 + len(ins):2 + n_buf]),
                token=out[-1][0, 0], token_array=out[-1], plan=plan, n_remote=n_remote)


def _split_wait(st, after, name):
    n_in, n_buf = len(st["ins"]), len(st["ins"]) + len(st["lands"])
    plan, n_remote = st["plan"], st["n_remote"]

    def body(*refs):
        in_refs, land_refs = refs[:n_in], refs[n_in:n_buf]
        send_sems, recv_sems = refs[n_buf], refs[n_buf + 1]
        x, y, c = lax.axis_index("x"), lax.axis_index("y"), lax.axis_index("c")
        for k, (s, d, dev) in enumerate(plan(in_refs, land_refs, x, y, c)):
            cp = pltpu.make_async_remote_copy(src_ref=s, dst_ref=d, send_sem=send_sems.at[k], recv_sem=recv_sems.at[k],
                                              device_id=dev, device_id_type=pl.DeviceIdType.MESH)
            cp.wait_send()
            cp.wait_recv()

    hbm = pl.BlockSpec(memory_space=pltpu.HBM)
    sem = pl.BlockSpec(memory_space=pltpu.SEMAPHORE)
    bufs = st["ins"] + st["lands"]
    out = pl.pallas_call(
        body, name=name, in_specs=[hbm] * n_buf + [sem, sem, pl.BlockSpec(memory_space=pl.ANY)],
        out_specs=[hbm] * n_buf, out_shape=[pltpu.HBM(a.shape, a.dtype) for a in bufs],
        input_output_aliases={i: i for i in range(n_buf)},
        compiler_params=pltpu.CompilerParams(has_side_effects=pltpu.SideEffectType.DATAFLOW_SIDE_EFFECTING),
    )(*bufs, st["send"], st["recv"], after)
    return list(out[:n_in]), list(out[n_in:])


def _pair_forward(lands, name):
    shapes = [jax.ShapeDtypeStruct(a.shape, a.dtype) for a in lands]
    return _rdma_call(lands, shapes, lambda ins, outs, x, y, c: (_plan_gather_near(outs, x, y, c), []),
                      (N_CHIPS - 1) * _n_half_pieces(lands), 0, name, aliases={i: i for i in range(len(lands))})


def _sum_block_rows(r, C):
    return _rows_tile(r, max(16, (1 << 18) // C // 16 * 16))


def _sum_pair(a, recv, cidx, name):
    nch, _, r, C = a.shape
    rb = _sum_block_rows(r, C)

    def body(c_ref, a_ref, r_ref, o_ref):
        o_ref[...] = (a_ref[...] + r_ref[...]).astype(BF16)

    blk = pl.BlockSpec((None, rb, C), lambda j, i, c: (j, i, 0))
    return pl.pallas_call(
        body, name=name,
        grid_spec=pltpu.PrefetchScalarGridSpec(
            num_scalar_prefetch=1, grid=(nch, r // rb),
            in_specs=[pl.BlockSpec((None, None, rb, C), lambda j, i, c: (j, c[0], i, 0)), blk], out_specs=blk),
        out_shape=jax.ShapeDtypeStruct((nch, r, C), BF16),
        compiler_params=_params("parallel", "parallel"),
    )(cidx, a, recv)


def _sum_chips(mine, recv, ids, name):
    nch, r, C = recv.shape
    rb = _sum_block_rows(r, C)

    def body(ids_ref, m_ref, *rest):
        r_refs, o_ref = rest[:nch], rest[nch]
        chip = ids_ref[1]
        own = m_ref[...].astype(F32)
        acc = jnp.where(chip == 0, own, r_refs[0][...].astype(F32))
        for q in range(1, nch):
            acc = acc + jnp.where(chip == q, own, r_refs[q][...].astype(F32))
        o_ref[...] = acc

    def slot(q):
        return pl.BlockSpec((None, rb, C), lambda i, ids: (jnp.where(ids[1] == q, (q + 1) % nch, q), i, 0))

    return pl.pallas_call(
        body, name=name,
        grid_spec=pltpu.PrefetchScalarGridSpec(
            num_scalar_prefetch=1, grid=(r // rb,),
            in_specs=[pl.BlockSpec((None, rb, C), lambda i, ids: (ids[1], i, 0))] + [slot(q) for q in range(nch)],
            out_specs=pl.BlockSpec((None, rb, C), lambda i, ids: (ids[0], i, 0))),
        out_shape=jax.ShapeDtypeStruct((N_CORES, r, C), F32),
        compiler_params=_params("parallel"),
    )(ids, mine, *([recv] * nch))


PACK_COLS = 1024
_SHARDED = ("ada_w", "w_in", "w_branch_a", "w_branch_b", "w_out", "ffn_w_up", "ffn_w_down")
_LAYER_KEYS = ("ada_w", "w_in", "w_a", "w_b", "w_o", "w_up", "w_down")
_SMALL = ("c_ctx", "ada_b", "norm1_w", "sgu_ln_w", "sgu_ln_b", "sgu_w", "sgu_b", "hgrn_lower_bounds", "hgrn_norm_w",
          "norm2_w", "ffn_conv_b", "final_norm_w")
_ORDER = ("c_ctx", "ada_w", "ada_b", "norm1_w", "w_in", "sgu_ln_w", "sgu_ln_b", "sgu_w", "sgu_b", "hgrn_lower_bounds",
          "hgrn_norm_w", "w_branch_a", "w_branch_b", "w_out", "norm2_w", "ffn_w_up", "ffn_conv_w", "ffn_conv_b",
          "ffn_w_down", "final_norm_w")


def _pad_to(v, n):
    return jnp.concatenate([v, jnp.zeros((n - v.shape[0],), v.dtype)]) if v.shape[0] < n else v


def _round_up(n, m):
    return (n + m - 1) // m * m


def _pack(arrays, n_pad):
    flat = jnp.concatenate([a.reshape(-1) for a in arrays])
    return _pad_to(flat, n_pad)


def _unpack(flat, like):
    out, off = [], 0
    for a in like:
        out.append(flat[off:off + a.size].reshape(a.shape))
        off += a.size
    return out


def kernel(x, c, ctx, c_ctx, ada_w, ada_b, norm1_w, w_in, sgu_ln_w, sgu_ln_b, sgu_w, sgu_b, hgrn_lower_bounds, hgrn_norm_w, w_branch_a, w_branch_b, w_out, norm2_w, ffn_w_up, ffn_conv_w, ffn_conv_b, ffn_w_down, final_norm_w, loss_target, m_c_ctx, m_ada_w, m_ada_b, m_norm1_w, m_w_in, m_sgu_ln_w, m_sgu_ln_b, m_sgu_w, m_sgu_b, m_hgrn_lower_bounds, m_hgrn_norm_w, m_w_branch_a, m_w_branch_b, m_w_out, m_norm2_w, m_ffn_w_up, m_ffn_conv_w, m_ffn_conv_b, m_ffn_w_down, m_final_norm_w, v_c_ctx, v_ada_w, v_ada_b, v_norm1_w, v_w_in, v_sgu_ln_w, v_sgu_ln_b, v_sgu_w, v_sgu_b, v_hgrn_lower_bounds, v_hgrn_norm_w, v_w_branch_a, v_w_branch_b, v_w_out, v_norm2_w, v_ffn_w_up, v_ffn_conv_w, v_ffn_conv_b, v_ffn_w_down, v_final_norm_w):
    w = dict(c_ctx=c_ctx, ada_w=ada_w, ada_b=ada_b, norm1_w=norm1_w, w_in=w_in, sgu_ln_w=sgu_ln_w, sgu_ln_b=sgu_ln_b,
             sgu_w=sgu_w, sgu_b=sgu_b, hgrn_lower_bounds=hgrn_lower_bounds, hgrn_norm_w=hgrn_norm_w, w_branch_a=w_branch_a,
             w_branch_b=w_branch_b, w_out=w_out, norm2_w=norm2_w, ffn_w_up=ffn_w_up, ffn_conv_w=ffn_conv_w,
             ffn_conv_b=ffn_conv_b, ffn_w_down=ffn_w_down, final_norm_w=final_norm_w)
    mom = dict(zip(_ORDER, (m_c_ctx, m_ada_w, m_ada_b, m_norm1_w, m_w_in, m_sgu_ln_w, m_sgu_ln_b, m_sgu_w, m_sgu_b,
                            m_hgrn_lower_bounds, m_hgrn_norm_w, m_w_branch_a, m_w_branch_b, m_w_out, m_norm2_w, m_ffn_w_up,
                            m_ffn_conv_w, m_ffn_conv_b, m_ffn_w_down, m_final_norm_w)))
    var = dict(zip(_ORDER, (v_c_ctx, v_ada_w, v_ada_b, v_norm1_w, v_w_in, v_sgu_ln_w, v_sgu_ln_b, v_sgu_w, v_sgu_b,
                            v_hgrn_lower_bounds, v_hgrn_norm_w, v_w_branch_a, v_w_branch_b, v_w_out, v_norm2_w, v_ffn_w_up,
                            v_ffn_conv_w, v_ffn_conv_b, v_ffn_w_down, v_final_norm_w)))
    depth, D = norm1_w.shape
    dff = ffn_conv_b.shape[1]
    ctx_rows = ctx.shape[1]

    assert depth == 2, "the lower-bound softmax is written for two layers"
    core = lax.axis_index("c")
    chip = 2 * lax.axis_index("x") + lax.axis_index("y")
    ids = jnp.stack([core, chip]).astype(jnp.int32)

    first, rest = _LAYER_KEYS[:2], _LAYER_KEYS[2:]
    shard = lambda l, k: w[_SHARDED[_LAYER_KEYS.index(k)]][l].astype(BF16)
    started, conv_full = {}, []

    def landing(s):
        return lax.dynamic_update_slice(lax.empty((N_CHIPS,) + s.shape, s.dtype), s[None], (chip,) + (0,) * s.ndim)

    def start_gather(l, keys, tag):
        lands = [landing(shard(l, k)) for k in keys]
        started[tag] = _split_start([], lands, lambda ins, lds, x, y, c: _plan_gather_far(lds, x, y, c),
                                    (N_CHIPS - 1) * _n_half_pieces(lands), f"gather_start_{tag}")
        return started[tag]["token"]

    def finish_gather(keys, tag, after):
        _, lands = _split_wait(started[tag], after, f"gather_wait_{tag}")
        return dict(zip(keys, _pair_forward(lands, f"gather_forward_{tag}")))

    def layer_weights(l, after):
        if l == 0:
            got = _gather_weights([landing(shard(0, k)) for k in first] + [landing(ffn_conv_w)], "gather_weights_first")
            conv_full.append(jnp.transpose(got[-1], (1, 2, 3, 0, 4)).reshape(depth, 9, dff))
            out = dict(zip(first, got), token=start_gather(0, rest, "rest_0"))
        else:
            out = dict(finish_gather(first, f"first_{l}", after), token=0.0)

        def late(after_late):
            more = finish_gather(rest, f"rest_{l}", after_late)
            more["late_token"] = 0.0
            if l + 1 < depth:
                more["late_token"] = start_gather(l + 1, first, f"first_{l + 1}") + start_gather(l + 1, rest, f"rest_{l + 1}")
            return more

        return dict(out, conv_w=conv_full[0][l], late=late)

    groups, order = {}, []

    def as_parts(gs):
        return [g.reshape(N_CHIPS, N_CORES, g.size // (N_CHIPS * N_CORES * g.shape[-1]), g.shape[-1]) for g in gs]

    def pair_start(tag, l, keys, gs):
        parts = as_parts(gs)
        lands = [lax.empty((N_CHIPS,) + p.shape[2:], p.dtype) for p in parts]
        groups[tag] = dict(l=l, keys=keys, pair=_split_start(parts, lands, _plan_pair, _n_pair_copies(parts),
                                                             f"reduce_pair_start_{tag}"))
        order.append(tag)
        return groups[tag]["pair"]["token"]

    def chips_start(tag, after):
        parts, other = _split_wait(groups[tag]["pair"], after, f"reduce_pair_wait_{tag}")
        sums = [_sum_pair(a, o, ids, f"sum_pair_{tag}_{i}") for i, (a, o) in enumerate(zip(parts, other))]
        lands = [lax.empty(s.shape, s.dtype) for s in sums]
        groups[tag]["chips"] = _split_start(sums, lands, _plan_chips, _n_chips_copies(sums), f"reduce_chips_start_{tag}")
        return groups[tag]["chips"]["token"]

    def chips_finish(tag, after):
        sums, recv = _split_wait(groups[tag]["chips"], after, f"reduce_chips_wait_{tag}")
        return {(groups[tag]["l"], k): _sum_chips(sums[i], recv[i], ids, f"sum_chips_{tag}_{i}")
                for i, k in enumerate(groups[tag]["keys"])}

    def on_layer_grads(l, stage, gs):
        if stage == "early":
            return pair_start(f"early_{l}", l, list(gs), list(gs.values()))
        if stage == "late":
            return pair_start(f"late_{l}", l, ["w_in"], [gs["w_in"]]) + chips_start(f"early_{l}", gs["w_in"])
        return chips_start(f"late_{l}", gs["after"])

    W = dict(ada_b=ada_b, norm1_w=norm1_w, sgu_ln_w=sgu_ln_w, sgu_ln_b=sgu_ln_b, sgu_w=sgu_w.astype(BF16),
             sgu_bt=jnp.swapaxes(sgu_b, 1, 2), hlb=hgrn_lower_bounds, hnw=hgrn_norm_w, norm2_w=norm2_w, conv_b=ffn_conv_b,
             final_norm_w=final_norm_w)
    xs = jnp.concatenate([ctx[0], x[0]], axis=0)
    cv = jnp.concatenate([c_ctx[None, :], c, jnp.zeros((14, D), F32)], axis=0)
    loss_local, dxs, G, sa = _local_step(xs, cv, loss_target[0], W, layer_weights, on_layer_grads, ctx_rows)
    loss = lax.psum(loss_local, ("x", "y", "c"))
    grad_x = dxs[ctx_rows:][None]

    pad8 = lambda a: jnp.pad(a, ((0, 8 - a.shape[0]), (0, 0)))
    fact = jnp.concatenate([pad8(sa[1:2].astype(F32))] + [pad8(G["dmod"][l][1].reshape(N_MOD, D)) for l in range(depth)]
                           + [pad8(G["dmod"][l][0].reshape(N_MOD, D)) for l in range(depth)], axis=0)
    facts = _gather_all(fact, "gather_mod_factors")
    lhs = jnp.concatenate([facts[:, 0].astype(BF16), jnp.broadcast_to(sa[0:1], (8, D))], axis=0)
    ada_cols = N_MOD * D // N_CHIPS
    g_ada = []
    for l in range(depth):
        lo_x, lo_c = 8 * (1 + l), 8 * (1 + depth + l)
        rhs = jnp.concatenate([facts[:, lo_x:lo_x + N_MOD].reshape(8, N_MOD * D),
                               facts[:, lo_c:lo_c + N_MOD].reshape(8, N_MOD * D)], axis=0)
        rhs = lax.dynamic_slice_in_dim(rhs, chip * ada_cols, ada_cols, axis=1).astype(BF16)
        g_ada.append(_mm_tn(lhs, rhs, F32, f"dw_ada_{l}"))

    dh = G["hlb1"][depth - 1]
    small_like = [w[k] for k in _SMALL] + [jnp.zeros((depth, 9, dff), F32)]
    small = [G["c_ctx"], jnp.stack(G["ada_b"]), jnp.stack(G["norm1_w"]), jnp.stack(G["sgu_ln_w"]), jnp.stack(G["sgu_ln_b"]),
             jnp.stack(G["sgu_w"]), jnp.stack(G["sgu_b"]), jnp.stack([-dh, dh]), jnp.stack(G["hnw"]), jnp.stack(G["norm2_w"]),
             jnp.stack(G["conv_b"]), G["final_norm_w"], jnp.stack(G["conv_w"])]
    n_small = sum(a.size for a in small)
    n_small_pad = _round_up(n_small, N_CORES * 16 * PACK_COLS)
    small_rows = n_small_pad // (N_CORES * PACK_COLS)
    small_rep = jnp.broadcast_to(_pack(small, n_small_pad).reshape(1, N_CORES, small_rows, PACK_COLS),
                                 (N_CHIPS, N_CORES, small_rows, PACK_COLS))
    small_parts = as_parts([small_rep])
    small_sums = [_sum_pair(small_parts[0], _reduce_pair(small_parts, "reduce_pair_small")[0], ids, "sum_pair_small")]
    groups["small"] = dict(l=None, keys=["small"], chips=_split_start(
        small_sums, [lax.empty(small_sums[0].shape, small_sums[0].dtype)], _plan_chips, _n_chips_copies(small_sums),
        "reduce_chips_start_small"))

    def gather_halves(halves, name):
        return dict(zip(halves, _gather_pair(list(halves.values()), name)))

    last = order[-1]
    halves = {}
    for tag in order[:-1]:
        halves.update(chips_finish(tag, groups["small"]["chips"]["token_array"]))
    reduced = gather_halves(halves, "gather_pair")
    grads, delta, new_m, new_v = {}, {}, {}, {}

    def adamw_sharded(i):
        k = _SHARDED[i]
        gs = g_ada if i == 0 else [reduced[(l, _LAYER_KEYS[i])].reshape(w[k].shape[1:]) for l in range(depth)]
        grads[k], delta[k], new_m[k], new_v[k] = _adamw(w[k], gs, mom[k], var[k], f"adamw_{k}")

    last_keys = groups[last]["keys"]
    for i in range(len(_SHARDED)):
        if _LAYER_KEYS[i] not in last_keys:
            adamw_sharded(i)
    halves = chips_finish(last, new_v[_SHARDED[-1]])
    halves.update(chips_finish("small", new_v[_SHARDED[-1]]))
    reduced.update(gather_halves(halves, "gather_pair_last"))
    for i in range(len(_SHARDED)):
        if _LAYER_KEYS[i] in last_keys:
            adamw_sharded(i)

    g_small = _unpack(reduced[(None, "small")].reshape(-1), small_like)
    grads.update(zip(_SMALL, g_small[:-1]))
    grads["ffn_conv_w"] = lax.dynamic_slice_in_dim(g_small[-1].reshape(depth, 3, 3, dff), chip * (dff // N_CHIPS),
                                                   dff // N_CHIPS, axis=3)
    packed = _SMALL + ("ffn_conv_w",)
    n_pad = _round_up(sum(w[k].size for k in packed), 16 * PACK_COLS)
    pack = lambda t: _pack([t[k] for k in packed], n_pad).reshape(1, -1, PACK_COLS)
    _, d, nm, nv = _adamw(pack(w), [pack(grads)[0]], pack(mom), pack(var), "adamw_packed")
    like = [w[k] for k in packed]
    for src, dst in ((d, delta), (nm, new_m), (nv, new_v)):
        dst.update(zip(packed, _unpack(src.reshape(-1), like)))

    return (loss, grad_x, *[grads[k] for k in _ORDER], *[delta[k] for k in _ORDER], *[new_m[k] for k in _ORDER],
            *[new_v[k] for k in _ORDER])
```

```python
import functools

import jax
import jax.numpy as jnp
from jax import lax
from jax.experimental import pallas as pl
from jax.experimental.pallas import tpu as pltpu

F32 = jnp.float32
BF16 = jnp.bfloat16

GRID_W = 64
HG_CHUNK = 64
SGU_CHUNK = 128
HEAD = 128
TB = 256
N_MOD = 6
RMS_EPS = 1e-6
LN_EPS = 1e-5
VMEM_LIMIT = 48 * 1024 * 1024
VMEM_LIMIT_PAIR = 58 * 1024 * 1024
VMEM_WHOLE_K = 50 * 1024 * 1024
N_CHIPS = 4
N_CORES = 2

ADAM_LR = 0.001
ADAM_B1 = 0.9
ADAM_B2 = 0.999
ADAM_EPS = 1e-08
ADAM_WD = 0.01
ADAM_STEP = 10

_GELU_C = 0.7978845608028654
_GELU_A = 0.044715


def _sigmoid(x):
    return 0.5 * jnp.tanh(0.5 * x) + 0.5


def _silu(x):
    return x * _sigmoid(x)


def _silu_both(x):
    s = _sigmoid(x)
    return x * s, s * (1.0 + x * (1.0 - s))


def _dsilu(x):
    return _silu_both(x)[1]


def _gelu_both(x):
    x2 = x * x
    t = jnp.tanh(_GELU_C * (x + _GELU_A * x2 * x))
    h = 0.5 * (1.0 + t)
    return x * h, h + 0.5 * x * (1.0 - t * t) * (_GELU_C + 3.0 * _GELU_C * _GELU_A * x2)


def _gelu(x):
    return 0.5 * x * (1.0 + jnp.tanh(_GELU_C * (x + _GELU_A * x * x * x)))


def _dot(a, b, ca, cb):
    return lax.dot_general(a, b, (((ca,), (cb,)), ((), ())), preferred_element_type=F32)


def _nn(a, b):
    return _dot(a, b, 1, 0)


def _nt(a, b):
    return _dot(a, b, 1, 1)


def _tn(a, b):
    return _dot(a, b, 0, 0)


def _params(*sem, vmem=VMEM_LIMIT):
    return pltpu.CompilerParams(dimension_semantics=sem if sem else None, vmem_limit_bytes=vmem)


def _stream_of(i, ctx_blocks):
    return (i >= ctx_blocks).astype(jnp.int32)


def _mm(a, b, mode, tm, tn, tk, out_dtype, name, b_chips=False, out_chips=False, vmem=VMEM_LIMIT):
    a_pair, b_pair = isinstance(a, tuple), isinstance(b, tuple)
    assert (not a_pair or mode == "nt") and (not b_pair or (mode == "tn" and not b_chips))
    ashape = (a[0].shape[0], 2 * a[0].shape[1]) if a_pair else a.shape
    if b_pair:
        bshape = (b[0].shape[0], 2 * b[0].shape[1])
    elif not b_chips:
        bshape = b.shape
    else:
        bshape = (b.shape[1], N_CHIPS * b.shape[2])
    if mode == "nn":
        (M, K), (K2, N) = ashape, bshape
    elif mode == "nt":
        (M, K), (N, K2) = ashape, bshape
    else:
        (K, M), (K2, N) = ashape, bshape
    assert K == K2 and M % tm == 0 and N % tn == 0 and K % tk == 0, (name, ashape, bshape, tm, tn, tk)
    nk = K // tk
    if a_pair:
        n1 = a[0].shape[1] // tk
        assert a[0].shape[1] % tk == 0
        a_specs = [pl.BlockSpec((tm, tk), lambda j, i, k: (i, jnp.minimum(k, n1 - 1))),
                   pl.BlockSpec((tm, tk), lambda j, i, k: (i, jnp.maximum(k - n1, 0)))]
    elif mode == "tn":
        a_specs = [pl.BlockSpec((tk, tm), lambda j, i, k: (k, i))]
    else:
        a_specs = [pl.BlockSpec((tm, tk), lambda j, i, k: (i, k))]
    if b_pair:
        n1 = b[0].shape[1] // tn
        assert b[0].shape[1] % tn == 0
        b_specs = [pl.BlockSpec((tk, tn), lambda j, i, k: (k, jnp.minimum(j, n1 - 1))),
                   pl.BlockSpec((tk, tn), lambda j, i, k: (k, jnp.maximum(j - n1, 0)))]
    elif not b_chips:
        if mode == "nt":
            b_spec = pl.BlockSpec((tn, tk), lambda j, i, k: (j, k))
        else:
            b_spec = pl.BlockSpec((tk, tn), lambda j, i, k: (k, j))
    else:
        cols = b.shape[2]
        if mode == "nn":
            per = cols // tn
            assert cols % tn == 0
            b_spec = pl.BlockSpec((None, tk, tn), lambda j, i, k: (j // per, k, j % per))
        else:
            per = cols // tk
            assert mode == "nt" and cols % tk == 0
            b_spec = pl.BlockSpec((None, tn, tk), lambda j, i, k: (k // per, j, k % per))
    if not b_pair:
        b_specs = [b_spec]
    if out_chips:
        per_o = (N // N_CHIPS) // tn
        assert (N // N_CHIPS) % tn == 0
        o_spec = pl.BlockSpec((None, tm, tn), lambda j, i, k: (j // per_o, i, j % per_o))
        o_shape = (N_CHIPS, M, N // N_CHIPS)
    else:
        o_spec = pl.BlockSpec((tm, tn), lambda j, i, k: (i, j))
        o_shape = (M, N)
    ca, cb = {"nn": (1, 0), "nt": (1, 1), "tn": (0, 0)}[mode]

    in_place = nk == 1
    na, nb = len(a_specs), len(b_specs)

    def body(*refs):
        a_refs, b_refs, rest = refs[:na], refs[na:na + nb], refs[na + nb:]
        if in_place:
            (o_ref,) = rest
        else:
            o_ref, acc = rest
        k = pl.program_id(2)

        if not in_place:
            @pl.when(k == 0)
            def _():
                acc[...] = jnp.zeros_like(acc)

        def multiply(which):
            part = _dot(a_refs[which if a_pair else 0][...], b_refs[which if b_pair else 0][...], ca, cb)
            if in_place:
                o_ref[...] = part.astype(out_dtype)
            else:
                acc[...] += part

        if a_pair or b_pair:
            first = (k < n1) if a_pair else (pl.program_id(0) < n1)
            pl.when(first)(functools.partial(multiply, 0))
            pl.when(jnp.logical_not(first))(functools.partial(multiply, 1))
        else:
            multiply(0)

        if not in_place:
            @pl.when(k == nk - 1)
            def _():
                o_ref[...] = acc[...].astype(out_dtype)

    ins = (list(a) if a_pair else [a]) + (list(b) if b_pair else [b])
    return pl.pallas_call(
        body, name=name, grid=(N // tn, M // tm, nk), in_specs=a_specs + b_specs, out_specs=o_spec,
        out_shape=jax.ShapeDtypeStruct(o_shape, out_dtype),
        scratch_shapes=[] if in_place else [pltpu.VMEM((tm, tn), F32)],
        compiler_params=_params("parallel", "parallel", "arbitrary", vmem=vmem),
    )(*ins)


def _tile(n, pref):
    if n <= pref:
        return n
    best = None
    for t in range(128, pref + 1, 128):
        if n % t == 0:
            best = t
    assert best is not None, (n, pref)
    return best


def _rows_tile(n, pref):
    if n <= pref:
        return n
    best = None
    for t in range(16, pref + 1, 16):
        if n % t == 0:
            best = t
    assert best is not None, (n, pref)
    return best


def _mm_nn_w(a, wg, out_dtype, name):
    M, K = a.shape
    return _mm(a, wg, "nn", _rows_tile(M, 2176), _tile(wg.shape[2], 1536), _tile(K, 1536), out_dtype, name, b_chips=True)


def _mm_nt_w(a, wg, out_dtype, name):
    M = a[0].shape[0] if isinstance(a, tuple) else a.shape[0]
    return _mm(a, wg, "nt", _rows_tile(M, 1088), _tile(wg.shape[1], 1024), _tile(wg.shape[2], 2304), out_dtype, name,
               b_chips=True)


def _mm_tn(a, b, out_dtype, name, out_chips=False):
    K, M = a.shape
    N = 2 * b[0].shape[1] if isinstance(b, tuple) else b.shape[1]
    ncol = N // N_CHIPS if out_chips else N
    tm, tn = _tile(M, 1408), _tile(ncol, 1408)
    if tm * tn > 1408 * 1152:
        tn = _tile(ncol, 1152)
    if isinstance(b, tuple):
        return _mm(a, b, "tn", tm, tn, _rows_tile(K, 2176), out_dtype, name, out_chips=out_chips, vmem=VMEM_LIMIT_PAIR)
    whole = 2 * (K * tm * a.dtype.itemsize + K * tn * b.dtype.itemsize + tm * tn * jnp.dtype(out_dtype).itemsize)
    if whole <= VMEM_WHOLE_K:
        return _mm(a, b, "tn", tm, tn, K, out_dtype, name, out_chips=out_chips, vmem=VMEM_LIMIT_PAIR)
    return _mm(a, b, "tn", tm, tn, _rows_tile(K, 2176), out_dtype, name, out_chips=out_chips)


def _mod_fwd(cv, wg, b, name):
    R, D = cv.shape
    tn = wg.shape[2]
    N = N_CHIPS * tn

    def body(cv_ref, w_ref, b_ref, mod_ref, sa_ref):
        sa = _silu(cv_ref[...]).astype(BF16)
        sa_ref[...] = sa
        mod_ref[...] = _nn(sa, w_ref[...]) + b_ref[...]

    return pl.pallas_call(
        body, name=name, grid=(N_CHIPS,),
        in_specs=[pl.BlockSpec((R, D), lambda j: (0, 0)), pl.BlockSpec((None, D, tn), lambda j: (j, 0, 0)),
                  pl.BlockSpec((1, tn), lambda j: (0, j))],
        out_specs=[pl.BlockSpec((R, tn), lambda j: (0, j)), pl.BlockSpec((R, D), lambda j: (0, 0))],
        out_shape=[jax.ShapeDtypeStruct((R, N), F32), jax.ShapeDtypeStruct((R, D), BF16)],
        compiler_params=_params("arbitrary"),
    )(cv, wg, b)


def _cvec_bwd(dmod, wg, cv, name):
    R, N = dmod.shape
    D = wg.shape[1]
    tk = wg.shape[2]
    nk = N_CHIPS

    def body(dm_ref, w_ref, cv_ref, o_ref):
        k = pl.program_id(0)

        @pl.when(k == 0)
        def _():
            o_ref[...] = jnp.zeros_like(o_ref)

        o_ref[...] += _nt(dm_ref[...].astype(BF16), w_ref[...])

        @pl.when(k == nk - 1)
        def _():
            o_ref[...] = o_ref[...] * _dsilu(cv_ref[...])

    return pl.pallas_call(
        body, name=name, grid=(nk,),
        in_specs=[pl.BlockSpec((R, tk), lambda k: (0, k)), pl.BlockSpec((None, D, tk), lambda k: (k, 0, 0)),
                  pl.BlockSpec((R, D), lambda k: (0, 0))],
        out_specs=pl.BlockSpec((R, D), lambda k: (0, 0)),
        out_shape=jax.ShapeDtypeStruct((R, D), F32),
        compiler_params=_params("arbitrary"),
    )(dmod, wg, cv)


def _norm_mod(x, nw, mod, which, ctx_rows, name):
    T, D = x.shape
    cb = ctx_rows // TB

    def body(x_ref, nw_ref, mod_ref, h_ref):
        xv = x_ref[...]
        r = lax.rsqrt(jnp.mean(xv * xv, axis=-1, keepdims=True) + RMS_EPS)
        y = xv * r * nw_ref[...]
        sh = mod_ref[which:which + 1, :]
        sc = mod_ref[which + 1:which + 2, :]
        h_ref[...] = (y * (1.0 + sc) + sh).astype(BF16)

    return pl.pallas_call(
        body, name=name, grid=(T // TB,),
        in_specs=[pl.BlockSpec((TB, D), lambda i: (i, 0)), pl.BlockSpec((1, D), lambda i: (0, 0)),
                  pl.BlockSpec((None, N_MOD, D), lambda i: (_stream_of(i, cb), 0, 0))],
        out_specs=pl.BlockSpec((TB, D), lambda i: (i, 0)),
        out_shape=jax.ShapeDtypeStruct((T, D), BF16),
        compiler_params=_params("parallel"),
    )(x, nw, mod)


def _norm_mod_bwd(dh, x, dres, nw, mod, which, ctx_rows, name):
    T, D = x.shape
    cb = ctx_rows // TB

    def body(dh_ref, x_ref, dres_ref, nw_ref, mod_ref, dx_ref, dm_ref, dnw_ref):
        i = pl.program_id(0)

        @pl.when(i == 0)
        def _():
            dnw_ref[...] = jnp.zeros_like(dnw_ref)

        @pl.when((i == 0) | (i == cb))
        def _():
            dm_ref[...] = jnp.zeros_like(dm_ref)

        xv = x_ref[...]
        dh = dh_ref[...]
        r = lax.rsqrt(jnp.mean(xv * xv, axis=-1, keepdims=True) + RMS_EPS)
        xh = xv * r
        nwv = nw_ref[...]
        sc = mod_ref[which + 1:which + 2, :]
        y = xh * nwv
        dm_ref[0:1, :] += jnp.sum(dh, axis=0, keepdims=True)
        dm_ref[1:2, :] += jnp.sum(dh * y, axis=0, keepdims=True)
        dy = dh * (1.0 + sc)
        dnw_ref[...] += jnp.sum(dy * xh, axis=0, keepdims=True)
        dxh = dy * nwv
        dx_ref[...] = dres_ref[...] + r * (dxh - xh * jnp.mean(dxh * xh, axis=-1, keepdims=True))

    return pl.pallas_call(
        body, name=name, grid=(T // TB,),
        in_specs=[pl.BlockSpec((TB, D), lambda i: (i, 0)), pl.BlockSpec((TB, D), lambda i: (i, 0)),
                  pl.BlockSpec((TB, D), lambda i: (i, 0)), pl.BlockSpec((1, D), lambda i: (0, 0)),
                  pl.BlockSpec((None, N_MOD, D), lambda i: (_stream_of(i, cb), 0, 0))],
        out_specs=[pl.BlockSpec((TB, D), lambda i: (i, 0)),
                   pl.BlockSpec((None, 2, D), lambda i: (_stream_of(i, cb), 0, 0)),
                   pl.BlockSpec((1, D), lambda i: (0, 0))],
        out_shape=[jax.ShapeDtypeStruct((T, D), F32), jax.ShapeDtypeStruct((2, 2, D), F32),
                   jax.ShapeDtypeStruct((1, D), F32)],
        compiler_params=_params("arbitrary"),
    )(dh, x, dres, nw, mod)


def _scan_chunk(n, rev, n_ctx, n_all):
    if not rev:
        return n
    return jnp.where(n < n_ctx, n_ctx - 1 - n, n_all - 1 + n_ctx - n)


def _cumsum_rows(x, rev):
    rows = x.shape[0]
    row = lax.broadcasted_iota(jnp.int32, (rows, 1), 0)
    s = 1
    while s < rows:
        if not rev:
            x = x + jnp.where(row >= s, pltpu.roll(x, s, 0), 0.0)
        else:
            x = x + jnp.where(row < rows - s, pltpu.roll(x, rows - s, 0), 0.0)
        s *= 2
    return x


def _lower_bound(hlb_ref, layer):
    h = hlb_ref[...]
    if layer == 0:
        return jnp.zeros_like(h[0:1, :])
    return _sigmoid(h[1:2, :] - h[0:1, :])


HG_STEP = 4


def _step_rows(j, rev, backward):
    sub = j if rev == backward else HG_STEP - 1 - j
    return slice(sub * HG_CHUNK, (sub + 1) * HG_CHUNK)


def _hgrn_gates(q_ref, f_ref, hlb_ref, layer, rev, rows):
    lb = _lower_bound(hlb_ref, layer)
    z = f_ref[rows, :]
    sig = 1.0 / (1.0 + jnp.exp(-z))
    fg = lb + (1.0 - lb) * sig
    kk = (1.0 - lb) * (1.0 - sig)
    g = jnp.log(fg)
    b = _cumsum_rows(g, rev)
    bt = jnp.sum(g, axis=0, keepdims=True)
    mid = HG_CHUNK // 2
    r = b[mid:mid + 1, :] if rev else b[mid - 1:mid, :]
    qh = _silu(q_ref[rows, :])
    return lb, sig, fg, kk, b, bt, r, qh


def _tri_mask(rev):
    t = lax.broadcasted_iota(jnp.int32, (HG_CHUNK, HG_CHUNK), 0)
    s = lax.broadcasted_iota(jnp.int32, (HG_CHUNK, HG_CHUNK), 1)
    return (s >= t) if rev else (s <= t)


def _hgrn_fwd(parts, hlb, layer, rev, ctx_rows, name, o_add=None):
    T = parts.shape[0]
    D = hlb.shape[1] // 2
    nh = D // HEAD
    n_all, n_ctx = T // HG_CHUNK, ctx_rows // HG_CHUNK
    assert n_all % HG_STEP == 0 and n_ctx % HG_STEP == 0
    n_steps = n_all // HG_STEP
    block = functools.partial(_scan_chunk, rev=rev, n_ctx=n_ctx // HG_STEP, n_all=n_steps)
    fcol = 2 if rev else 1

    def body(q_ref, f_ref, i_ref, hlb_ref, *rest):
        if o_add is None:
            o_ref, st_ref, s_scr = rest
        else:
            oa_ref, o_ref, st_ref, s_scr = rest
        n = pl.program_id(0)

        @pl.when(n == 0)
        def _():
            s_scr[...] = jnp.zeros_like(s_scr)

        mask = _tri_mask(rev)
        hs = [slice(h * HEAD, (h + 1) * HEAD) for h in range(nh)]
        for j in range(HG_STEP):
            rows = _step_rows(j, rev, False)
            lb, sig, fg, kk, b, bt, r, qh = _hgrn_gates(q_ref, f_ref, hlb_ref, layer, rev, rows)
            qr = (qh * jnp.exp(b - r)).astype(BF16)
            kr = (kk * jnp.exp(r - b)).astype(BF16)
            qe = (qh * jnp.exp(b)).astype(BF16)
            ke = (kk * jnp.exp(bt - b)).astype(BF16)
            dec = jnp.exp(bt)
            v = i_ref[rows, :].astype(BF16)
            st = [s_scr[h] for h in range(nh)]
            a_raw = [_nt(qr[:, sl], kr[:, sl]) for sl in hs]
            o_int = [_nt(qe[:, sl], st[h].astype(BF16)) for h, sl in enumerate(hs)]
            kv = [_tn(v[:, sl], ke[:, sl]) for sl in hs]
            for h, sl in enumerate(hs):
                st_ref[j, h] = st[h]
                o = _nn(jnp.where(mask, a_raw[h], 0.0).astype(BF16), v[:, sl]) + o_int[h]
                if o_add is not None:
                    o = o + oa_ref[rows, sl]
                o_ref[rows, sl] = o
                s_scr[h] = st[h] * dec[:, sl] + kv[h]

    cspec = lambda col: pl.BlockSpec((HG_STEP * HG_CHUNK, D), lambda n: (block(n), col))
    ins = [parts, parts, parts, hlb]
    specs = [cspec(0), cspec(fcol), cspec(3), pl.BlockSpec((2, D), lambda n: (0, 1 if rev else 0))]
    if o_add is not None:
        ins.append(o_add)
        specs.append(cspec(0))
    return pl.pallas_call(
        body, name=name, grid=(n_steps,), in_specs=specs,
        out_specs=[cspec(0), pl.BlockSpec((HG_STEP, nh, HEAD, HEAD), lambda n: (n, 0, 0, 0))],
        out_shape=[jax.ShapeDtypeStruct((T, D), F32), jax.ShapeDtypeStruct((n_all, nh, HEAD, HEAD), F32)],
        scratch_shapes=[pltpu.VMEM((nh, HEAD, HEAD), F32)],
        compiler_params=_params("arbitrary"),
    )(*ins)


def _hgrn_bwd(parts, hlb, do, states, layer, rev, ctx_rows, name, other=None, dparts=None):
    T = parts.shape[0]
    D = hlb.shape[1] // 2
    nh = D // HEAD
    n_all, n_ctx = T // HG_CHUNK, ctx_rows // HG_CHUNK
    assert n_all % HG_STEP == 0 and n_ctx % HG_STEP == 0
    n_steps = n_all // HG_STEP
    step = lambda m: n_steps - 1 - m
    block = lambda m: _scan_chunk(step(m), rev, n_ctx // HG_STEP, n_steps)
    fcol = 2 if rev else 1
    has_add = other is not None
    assert not has_add or rev

    def body(q_ref, f_ref, i_ref, hlb_ref, do_ref, st_ref, *rest):
        if has_add:
            dqa_ref, dza_ref, dia_ref, _, out_ref, dlb_ref, ds_scr = rest
            dq_ref, dz_ref, di_ref = out_ref.at[:, 0:D], out_ref.at[:, 2 * D:3 * D], out_ref.at[:, 3 * D:4 * D]
            out_ref[:, D:2 * D] = dza_ref[...]
        else:
            dq_ref, dz_ref, di_ref, dlb_ref, ds_scr = rest
        m = pl.program_id(0)

        @pl.when(m == 0)
        def _():
            ds_scr[...] = jnp.zeros_like(ds_scr)
            dlb_ref[...] = jnp.zeros_like(dlb_ref)

        mask = _tri_mask(rev)
        hs = [slice(h * HEAD, (h + 1) * HEAD) for h in range(nh)]
        for j in range(HG_STEP):
            rows = _step_rows(j, rev, True)
            slot = HG_STEP - 1 - j
            lb, sig, fg, kk, b, bt, r, qh = _hgrn_gates(q_ref, f_ref, hlb_ref, layer, rev, rows)
            e_qr = jnp.exp(b - r)
            e_kr = jnp.exp(r - b)
            e_b = jnp.exp(b)
            e_ke = jnp.exp(bt - b)
            dec = jnp.exp(bt)
            qr = (qh * e_qr).astype(BF16)
            kr = (kk * e_kr).astype(BF16)
            qe = (qh * e_b).astype(BF16)
            ke = (kk * e_ke).astype(BF16)
            v = i_ref[rows, :].astype(BF16)
            dov = do_ref[rows, :].astype(BF16)
            st = [st_ref[slot, h] for h in range(nh)]
            dst = [ds_scr[h] for h in range(nh)]
            stb = [t.astype(BF16) for t in st]
            dstb = [t.astype(BF16) for t in dst]
            a_raw = [_nt(qr[:, sl], kr[:, sl]) for sl in hs]
            da_raw = [_nt(dov[:, sl], v[:, sl]) for sl in hs]
            dq_int = [_nn(dov[:, sl], stb[h]) for h, sl in enumerate(hs)]
            dk_int = [_nn(v[:, sl], dstb[h]) for h, sl in enumerate(hs)]
            dv_int = [_nt(ke[:, sl], dstb[h]) for h, sl in enumerate(hs)]
            ds_new = [_tn(dov[:, sl], qe[:, sl]) for sl in hs]
            a = [jnp.where(mask, t, 0.0).astype(BF16) for t in a_raw]
            da = [jnp.where(mask, t, 0.0).astype(BF16) for t in da_raw]
            dv_parts = [_tn(a[h], dov[:, sl]) + dv_int[h] for h, sl in enumerate(hs)]
            dq_parts = [_nn(da[h], kr[:, sl]) * e_qr[:, sl] + dq_int[h] * e_b[:, sl] for h, sl in enumerate(hs)]
            dki_parts = [dk_int[h] * e_ke[:, sl] for h, sl in enumerate(hs)]
            dk_parts = [_tn(da[h], qr[:, sl]) * e_kr[:, sl] + dki_parts[h] for h, sl in enumerate(hs)]
            dbt_parts = [dec[:, sl] * jnp.sum(st[h] * dst[h], axis=0, keepdims=True) for h, sl in enumerate(hs)]
            for h, sl in enumerate(hs):
                ds_scr[h] = dst[h] * dec[:, sl] + ds_new[h]
            dq = jnp.concatenate(dq_parts, axis=1)
            dk = jnp.concatenate(dk_parts, axis=1)
            dki = jnp.concatenate(dki_parts, axis=1)
            dv = jnp.concatenate(dv_parts, axis=1)
            dbt = jnp.concatenate(dbt_parts, axis=1) + jnp.sum(kk * dki, axis=0, keepdims=True)
            db = qh * dq - kk * dk
            dg = _cumsum_rows(db, not rev) + dbt
            df = dg / fg - dk
            dz_ref[rows, :] = (df * (1.0 - lb) * sig * (1.0 - sig)).astype(BF16)
            dlb_ref[...] += jnp.sum(df * (1.0 - sig), axis=0, keepdims=True)
            dqr = dq * _dsilu(q_ref[rows, :])
            if has_add:
                dqr = dqr + dqa_ref[rows, :]
                dv = dv + dia_ref[rows, :]
            dq_ref[rows, :] = dqr.astype(dq_ref.dtype)
            di_ref[rows, :] = dv.astype(di_ref.dtype)

        @pl.when(m == n_steps - 1)
        def _():
            lb = _lower_bound(hlb_ref, layer)
            if layer == 0:
                dlb_ref[...] = jnp.zeros_like(dlb_ref)
            else:
                dlb_ref[...] = dlb_ref[...] * lb * (1.0 - lb)

    cspec = lambda col: pl.BlockSpec((HG_STEP * HG_CHUNK, D), lambda m: (block(m), col))
    ins = [parts, parts, parts, hlb, do, states]
    specs = [cspec(0), cspec(fcol), cspec(3), pl.BlockSpec((2, D), lambda m: (0, 1 if rev else 0)), cspec(0),
             pl.BlockSpec((HG_STEP, nh, HEAD, HEAD), lambda m: (step(m), 0, 0, 0))]
    dlb_spec = pl.BlockSpec((1, D), lambda m: (0, 0))
    dlb_shape = jax.ShapeDtypeStruct((1, D), F32)
    if has_add:
        return pl.pallas_call(
            body, name=name, grid=(n_steps,),
            in_specs=specs + [cspec(0), cspec(0), cspec(0), pl.BlockSpec(memory_space=pl.ANY)],
            out_specs=[pl.BlockSpec((HG_STEP * HG_CHUNK, 4 * D), lambda m: (block(m), 0)), dlb_spec],
            out_shape=[jax.ShapeDtypeStruct(dparts.shape, dparts.dtype), dlb_shape],
            scratch_shapes=[pltpu.VMEM((nh, HEAD, HEAD), F32)], input_output_aliases={len(ins) + 3: 0},
            compiler_params=_params("arbitrary"),
        )(*ins, *other, dparts)
    return pl.pallas_call(
        body, name=name, grid=(n_steps,), in_specs=specs,
        out_specs=[cspec(0), cspec(0), cspec(0), dlb_spec],
        out_shape=[jax.ShapeDtypeStruct((T, D), F32), jax.ShapeDtypeStruct((T, D), BF16),
                   jax.ShapeDtypeStruct((T, D), F32), dlb_shape],
        scratch_shapes=[pltpu.VMEM((nh, HEAD, HEAD), F32)],
        compiler_params=_params("arbitrary"),
    )(*ins)


def _sgu_ln(gv, lnw_ref, lnb_ref):
    mu = jnp.mean(gv, axis=-1, keepdims=True)
    xc = gv - mu
    rstd = lax.rsqrt(jnp.mean(xc * xc, axis=-1, keepdims=True) + LN_EPS)
    xh = xc * rstd
    return xh, rstd, xh * lnw_ref[...] + lnb_ref[...]


def _sgu_fwd(parts, lnw, lnb, w, bt, name):
    T = parts.shape[0]
    D = lnw.shape[1]
    G = D // HEAD

    def body(u_ref, v_ref, lnw_ref, lnb_ref, w_ref, bt_ref, ya_ref):
        gu = _gelu(u_ref[...])
        _, _, vn = _sgu_ln(_gelu(v_ref[...]), lnw_ref, lnb_ref)
        vnb = vn.astype(BF16)
        for g in range(G):
            sl = slice(g * HEAD, (g + 1) * HEAD)
            mixed = _nn(w_ref[g], vnb[:, sl]) + bt_ref[:, g:g + 1]
            ya_ref[:, sl] = (gu[:, sl] * mixed).astype(BF16)

    return pl.pallas_call(
        body, name=name, grid=(T // SGU_CHUNK,),
        in_specs=[pl.BlockSpec((SGU_CHUNK, D), lambda n: (n, 4)), pl.BlockSpec((SGU_CHUNK, D), lambda n: (n, 5)),
                  pl.BlockSpec((1, D), lambda n: (0, 0)), pl.BlockSpec((1, D), lambda n: (0, 0)),
                  pl.BlockSpec((G, SGU_CHUNK, SGU_CHUNK), lambda n: (0, 0, 0)),
                  pl.BlockSpec((SGU_CHUNK, G), lambda n: (0, 0))],
        out_specs=pl.BlockSpec((SGU_CHUNK, D), lambda n: (n, 0)),
        out_shape=jax.ShapeDtypeStruct((T, D), BF16),
        compiler_params=_params("parallel"),
    )(parts, parts, lnw, lnb, w, bt)


def _sgu_bwd(parts, dya, lnw, lnb, w, bt, dparts, name):
    T = parts.shape[0]
    D = lnw.shape[1]
    G = D // HEAD

    def body(u_ref, v_ref, dya_ref, lnw_ref, lnb_ref, w_ref, bt_ref, dparts_in,
             duv_ref, dw_ref, dbt_ref, dlnw_ref, dlnb_ref, dvn_scr):
        du_ref = duv_ref.at[:, 0:D]
        dv_ref = duv_ref.at[:, D:2 * D]
        n = pl.program_id(0)

        @pl.when(n == 0)
        def _():
            dw_ref[...] = jnp.zeros_like(dw_ref)
            dbt_ref[...] = jnp.zeros_like(dbt_ref)
            dlnw_ref[...] = jnp.zeros_like(dlnw_ref)
            dlnb_ref[...] = jnp.zeros_like(dlnb_ref)

        gu, dgu = _gelu_both(u_ref[...])
        gv, dgv_dv = _gelu_both(v_ref[...])
        xh, rstd, vn = _sgu_ln(gv, lnw_ref, lnb_ref)
        vnb = vn.astype(BF16)
        dya = dya_ref[...]
        lane = lax.broadcasted_iota(jnp.int32, (SGU_CHUNK, G), 1)
        dbt = jnp.zeros((SGU_CHUNK, G), F32)
        for g in range(G):
            sl = slice(g * HEAD, (g + 1) * HEAD)
            wg = w_ref[g]
            mixed = _nn(wg, vnb[:, sl]) + bt_ref[:, g:g + 1]
            dmix = dya[:, sl] * gu[:, sl]
            du_ref[:, sl] = (dya[:, sl] * mixed * dgu[:, sl]).astype(BF16)
            dmb = dmix.astype(BF16)
            dvn_scr[:, sl] = _tn(wg, dmb)
            dw_ref[g] += _nt(dmb, vnb[:, sl])
            dbt = dbt + jnp.where(lane == g, jnp.sum(dmix, axis=1, keepdims=True), 0.0)
        dbt_ref[...] += dbt
        dvn = dvn_scr[...]
        dlnw_ref[...] += jnp.sum(dvn * xh, axis=0, keepdims=True)
        dlnb_ref[...] += jnp.sum(dvn, axis=0, keepdims=True)
        dxh = dvn * lnw_ref[...]
        dgv = rstd * (dxh - jnp.mean(dxh, axis=-1, keepdims=True) - xh * jnp.mean(dxh * xh, axis=-1, keepdims=True))
        dv_ref[...] = (dgv * dgv_dv).astype(BF16)

    row = lambda col: pl.BlockSpec((SGU_CHUNK, D), lambda n: (n, col))
    vec = pl.BlockSpec((1, D), lambda n: (0, 0))
    wsp = pl.BlockSpec((G, SGU_CHUNK, SGU_CHUNK), lambda n: (0, 0, 0))
    bsp = pl.BlockSpec((SGU_CHUNK, G), lambda n: (0, 0))
    return pl.pallas_call(
        body, name=name, grid=(T // SGU_CHUNK,),
        in_specs=[row(4), row(5), row(0), vec, vec, wsp, bsp, pl.BlockSpec(memory_space=pl.ANY)],
        out_specs=[pl.BlockSpec((SGU_CHUNK, 2 * D), lambda n: (n, 2)), wsp, bsp, vec, vec],
        out_shape=[jax.ShapeDtypeStruct(dparts.shape, dparts.dtype),
                   jax.ShapeDtypeStruct((G, SGU_CHUNK, SGU_CHUNK), F32), jax.ShapeDtypeStruct((SGU_CHUNK, G), F32),
                   jax.ShapeDtypeStruct((1, D), F32), jax.ShapeDtypeStruct((1, D), F32)],
        scratch_shapes=[pltpu.VMEM((SGU_CHUNK, D), F32)], input_output_aliases={7: 0},
        compiler_params=_params("arbitrary"),
    )(parts, parts, dya, lnw, lnb, w, bt, dparts)


TBT = 256
VMEM_LIMIT_TOKEN_OUT = 58 * 1024 * 1024


def _rows_weight_spec(wg):
    return pl.BlockSpec(wg.shape, lambda i: (0, 0, 0))


def _full(w_ref):
    return w_ref[...].reshape(w_ref.shape[0] * w_ref.shape[1], w_ref.shape[2])


def _token_out_fwd(o, parts, ya, x, mod, hnw, nw2, wa, wb, wo, ctx_rows, name):
    T, D = x.shape
    nh = D // HEAD
    cb = ctx_rows // TBT

    def body(o_ref, og_ref, ga_ref, gb_ref, ya_ref, x_ref, mod_ref, hnw_ref, nw2_ref, wa_ref, wb_ref, wo_ref,
             yb_ref, pa_ref, pb_ref, mg_ref, tmo_ref, xm_ref, h2_ref):
        ov = o_ref[...]
        so = _silu(og_ref[...])
        nw = hnw_ref[...]
        for h in range(nh):
            sl = slice(h * HEAD, (h + 1) * HEAD)
            seg = ov[:, sl]
            r = lax.rsqrt(jnp.mean(seg * seg, axis=-1, keepdims=True) + RMS_EPS)
            yb_ref[:, sl] = (seg * r * nw * so[:, sl]).astype(BF16)
        pa = _nn(ya_ref[...], _full(wa_ref))
        pb = _nn(yb_ref[...], _full(wb_ref))
        pa_ref[...] = pa.astype(BF16)
        pb_ref[...] = pb.astype(BF16)
        mg = (_sigmoid(ga_ref[...]) * pa + _sigmoid(gb_ref[...]) * pb).astype(BF16)
        mg_ref[...] = mg
        out = _nn(mg, _full(wo_ref))
        tmo_ref[...] = out.astype(BF16)
        xm = x_ref[...] + mod_ref[2:3, :] * out
        xm_ref[...] = xm
        r = lax.rsqrt(jnp.mean(xm * xm, axis=-1, keepdims=True) + RMS_EPS)
        h2_ref[...] = (xm * r * nw2_ref[...] * (1.0 + mod_ref[4:5, :]) + mod_ref[3:4, :]).astype(BF16)

    row = lambda col: pl.BlockSpec((TBT, D), lambda i: (i, col))
    wsp = _rows_weight_spec(wa)
    sd = lambda dt: jax.ShapeDtypeStruct((T, D), dt)
    return pl.pallas_call(
        body, name=name, grid=(T // TBT,),
        in_specs=[row(0), row(6), row(7), row(8), row(0), row(0),
                  pl.BlockSpec((None, N_MOD, D), lambda i: (_stream_of(i, cb), 0, 0)),
                  pl.BlockSpec((1, HEAD), lambda i: (0, 0)), pl.BlockSpec((1, D), lambda i: (0, 0)), wsp, wsp, wsp],
        out_specs=[row(0)] * 7,
        out_shape=[sd(BF16), sd(BF16), sd(BF16), sd(BF16), sd(BF16), sd(F32), sd(BF16)],
        compiler_params=_params("parallel", vmem=VMEM_LIMIT_TOKEN_OUT),
    )(o, parts, parts, parts, ya, x, mod, hnw, nw2, wa, wb, wo)


def _token_out_bwd(dx, tmo, pa, pb, o, parts, mod, hnw, wa, wb, wo, ctx_rows, name):
    T, D = dx.shape
    nh = D // HEAD
    cb = ctx_rows // TBT

    def body(dx_ref, tmo_ref, pa_ref, pb_ref, o_ref, og_ref, ga_ref, gb_ref, mod_ref, hnw_ref, wa_ref, wb_ref, wo_ref,
             dout_ref, dpa_ref, dpb_ref, dgate_ref, dya_ref, do_ref, dg1_ref, dhnw_ref):
        i = pl.program_id(0)

        @pl.when(i == 0)
        def _():
            dhnw_ref[...] = jnp.zeros_like(dhnw_ref)

        @pl.when((i == 0) | (i == cb))
        def _():
            dg1_ref[...] = jnp.zeros_like(dg1_ref)

        dxv = dx_ref[...]
        dg1_ref[...] += jnp.sum(dxv * tmo_ref[...], axis=0, keepdims=True)
        dout = (dxv * mod_ref[2:3, :]).astype(BF16)
        dout_ref[...] = dout
        dmg = _nt(dout, _full(wo_ref))
        sa = _sigmoid(ga_ref[...])
        sb = _sigmoid(gb_ref[...])
        dpa = (dmg * sa).astype(BF16)
        dpb = (dmg * sb).astype(BF16)
        dpa_ref[...] = dpa
        dpb_ref[...] = dpb
        dgate_ref[:, D:2 * D] = (dmg * pa_ref[...] * sa * (1.0 - sa)).astype(BF16)
        dgate_ref[:, 2 * D:3 * D] = (dmg * pb_ref[...] * sb * (1.0 - sb)).astype(BF16)
        dya_ref[...] = _nt(dpa, _full(wa_ref))
        dyb = _nt(dpb, _full(wb_ref))
        so, dso = _silu_both(og_ref[...])
        ov = o_ref[...]
        nw = hnw_ref[...]
        dnw = jnp.zeros((1, HEAD), F32)
        for h in range(nh):
            sl = slice(h * HEAD, (h + 1) * HEAD)
            seg = ov[:, sl]
            r = lax.rsqrt(jnp.mean(seg * seg, axis=-1, keepdims=True) + RMS_EPS)
            oh = seg * r
            dn = dyb[:, sl] * so[:, sl]
            dgate_ref[:, sl] = (dyb[:, sl] * oh * nw * dso[:, sl]).astype(BF16)
            dnw = dnw + jnp.sum(dn * oh, axis=0, keepdims=True)
            doh = dn * nw
            do_ref[:, sl] = r * (doh - oh * jnp.mean(doh * oh, axis=-1, keepdims=True))
        dhnw_ref[...] += dnw

    row = lambda col: pl.BlockSpec((TBT, D), lambda i: (i, col))
    wsp = _rows_weight_spec(wa)
    sd = lambda dt: jax.ShapeDtypeStruct((T, D), dt)
    return pl.pallas_call(
        body, name=name, grid=(T // TBT,),
        in_specs=[row(0), row(0), row(0), row(0), row(0), row(6), row(7), row(8),
                  pl.BlockSpec((None, N_MOD, D), lambda i: (_stream_of(i, cb), 0, 0)),
                  pl.BlockSpec((1, HEAD), lambda i: (0, 0)), wsp, wsp, wsp],
        out_specs=[row(0)] * 3 + [pl.BlockSpec((TBT, 3 * D), lambda i: (i, 2)), row(0), row(0),
                                  pl.BlockSpec((None, 1, D), lambda i: (_stream_of(i, cb), 0, 0)),
                                  pl.BlockSpec((1, HEAD), lambda i: (0, 0))],
        out_shape=[sd(BF16)] * 3 + [jax.ShapeDtypeStruct((T, 9 * D), BF16), sd(F32), sd(F32),
                                    jax.ShapeDtypeStruct((2, 1, D), F32), jax.ShapeDtypeStruct((1, HEAD), F32)],
        compiler_params=_params("arbitrary", vmem=VMEM_LIMIT_TOKEN_OUT),
    )(dx, tmo, pa, pb, o, parts, parts, parts, mod, hnw, wa, wb, wo)


def _conv_geometry(i, nb, cb):
    is_ctx = i < cb
    first = (i == 0) | (i == cb)
    last = (i == cb - 1) | (i == nb - 1)
    row = lax.broadcasted_iota(jnp.int32, (TB + 2 * GRID_W, 1), 0)
    w = row & (GRID_W - 1)
    left_ok = (w != 0) | is_ctx
    right_ok = (w != GRID_W - 1) | is_ctx
    return is_ctx, first, last, left_ok, right_ok


def _ext(p_ref, m_ref, n_ref, first, last):
    return jnp.concatenate([jnp.where(first, 0.0, p_ref[...]), m_ref[...], jnp.where(last, 0.0, n_ref[...])], axis=0)


def _shift_prev(e, ok):
    return jnp.where(ok, pltpu.roll(e, 1, 0), 0.0)


def _shift_next(e, ok):
    return jnp.where(ok, pltpu.roll(e, e.shape[0] - 1, 0), 0.0)


def _halo_specs(cbk, n64, coff=0):
    r = TB // GRID_W
    prev = pl.BlockSpec((GRID_W, cbk), lambda j, i: (jnp.maximum(r * i - 1, 0), j + coff))
    main = pl.BlockSpec((TB, cbk), lambda j, i: (i, j + coff))
    nxt = pl.BlockSpec((GRID_W, cbk), lambda j, i: (jnp.minimum(r * i + r, n64 - 1), j + coff))
    return [prev, main, nxt]


def _conv_cblock(dff):
    return _tile(dff, 1408)


def _conv_fwd(up, cw, cbias, ctx_rows, name):
    T, dff = up.shape[0], up.shape[1] // 2
    cbk = _conv_cblock(dff)
    nb, cb = T // TB, ctx_rows // TB
    nvb = dff // cbk

    def body(ap_ref, a_ref, an_ref, v_ref, cw_ref, cb_ref, ac_ref, act_ref):
        i = pl.program_id(1)
        is_ctx, first, last, lok, rok = _conv_geometry(i, nb, cb)
        e = _ext(ap_ref, a_ref, an_ref, first, last)
        el = _shift_prev(e, lok)
        er = _shift_next(e, rok)
        cwv = cw_ref[...]

        def comb(dr, lo):
            sl = slice(lo, lo + TB)
            return cwv[3 * dr:3 * dr + 1] * el[sl] + cwv[3 * dr + 1:3 * dr + 2] * e[sl] + cwv[3 * dr + 2:3 * dr + 3] * er[sl]

        out = comb(1, GRID_W) + jnp.where(is_ctx, 0.0, comb(0, 0) + comb(2, 2 * GRID_W))
        a_c = out + cb_ref[...]
        ac_ref[...] = a_c
        act_ref[...] = (_gelu(a_c) * v_ref[...]).astype(BF16)

    main = pl.BlockSpec((TB, cbk), lambda j, i: (i, j))
    return pl.pallas_call(
        body, name=name, grid=(dff // cbk, nb),
        in_specs=_halo_specs(cbk, T // GRID_W) + [pl.BlockSpec((TB, cbk), lambda j, i: (i, j + nvb)),
                                                 pl.BlockSpec((9, cbk), lambda j, i: (0, j)),
                                                 pl.BlockSpec((1, cbk), lambda j, i: (0, j))],
        out_specs=[main, main],
        out_shape=[jax.ShapeDtypeStruct((T, dff), F32), jax.ShapeDtypeStruct((T, dff), BF16)],
        compiler_params=_params("parallel", "parallel"),
    )(up, up, up, up, cw, cbias)


def _conv_bwd(up, ac, dact, cw, ctx_rows, name):
    T, dff = up.shape[0], up.shape[1] // 2
    cbk = _conv_cblock(dff)
    nb, cb = T // TB, ctx_rows // TB
    nvb = dff // cbk

    def body(ap_ref, a_ref, an_ref, vp_ref, v_ref, vn_ref, cp_ref, c_ref, cn_ref, dp_ref, d_ref, dn_ref, cw_ref,
             da_ref, dv_ref, dcw_ref, dcb_ref):
        i = pl.program_id(1)

        @pl.when(i == 0)
        def _():
            dcw_ref[...] = jnp.zeros_like(dcw_ref)
            dcb_ref[...] = jnp.zeros_like(dcb_ref)

        is_ctx, first, last, lok, rok = _conv_geometry(i, nb, cb)
        gl, dgl = _gelu_both(_ext(cp_ref, c_ref, cn_ref, first, last))
        g = _ext(dp_ref, d_ref, dn_ref, first, last) * _ext(vp_ref, v_ref, vn_ref, first, last) * dgl
        dv_ref[...] = (d_ref[...] * gl[GRID_W:GRID_W + TB]).astype(BF16)
        gm = _shift_prev(g, lok)
        gp = _shift_next(g, rok)
        cwv = cw_ref[...]

        def comb(dr, lo):
            sl = slice(lo, lo + TB)
            return cwv[3 * dr:3 * dr + 1] * gp[sl] + cwv[3 * dr + 1:3 * dr + 2] * g[sl] + cwv[3 * dr + 2:3 * dr + 3] * gm[sl]

        da = comb(1, GRID_W) + jnp.where(is_ctx, 0.0, comb(0, 2 * GRID_W) + comb(2, 0))
        da_ref[...] = da.astype(BF16)
        e = _ext(ap_ref, a_ref, an_ref, first, last)
        taps = [_shift_prev(e, lok), e, _shift_next(e, rok)]
        gmain = g[GRID_W:GRID_W + TB]
        dcb_ref[...] += jnp.sum(gmain, axis=0, keepdims=True)
        vert = jnp.where(is_ctx, 0.0, 1.0)
        for dr in range(3):
            sl = slice(dr * GRID_W, dr * GRID_W + TB)
            for dw in range(3):
                s = jnp.sum(gmain * taps[dw][sl], axis=0, keepdims=True)
                if dr != 1:
                    s = s * vert
                k = 3 * dr + dw
                dcw_ref[k:k + 1, :] += s

    main = pl.BlockSpec((TB, cbk), lambda j, i: (i, j))
    halo = _halo_specs(cbk, T // GRID_W)
    acc9 = pl.BlockSpec((9, cbk), lambda j, i: (0, j))
    acc1 = pl.BlockSpec((1, cbk), lambda j, i: (0, j))
    return pl.pallas_call(
        body, name=name, grid=(dff // cbk, nb),
        in_specs=halo + _halo_specs(cbk, T // GRID_W, nvb) + halo + halo + [acc9],
        out_specs=[main, main, acc9, acc1],
        out_shape=[jax.ShapeDtypeStruct((T, dff), BF16), jax.ShapeDtypeStruct((T, dff), BF16),
                   jax.ShapeDtypeStruct((9, dff), F32), jax.ShapeDtypeStruct((1, dff), F32)],
        compiler_params=_params("parallel", "arbitrary"),
    )(up, up, up, up, up, up, ac, ac, ac, dact, dact, dact, cw)


def _ffn_out_fwd(act, xm, mod, wd, ctx_rows, name):
    T, D = xm.shape
    dff = act.shape[1]
    cb = ctx_rows // TB

    def body(act_ref, x_ref, mod_ref, w_ref, xo_ref, fo_ref):
        out = _nn(act_ref[...], _full(w_ref))
        fo_ref[...] = out
        xo_ref[...] = x_ref[...] + mod_ref[5:6, :] * out

    row = pl.BlockSpec((TB, D), lambda i: (i, 0))
    return pl.pallas_call(
        body, name=name, grid=(T // TB,),
        in_specs=[pl.BlockSpec((TB, dff), lambda i: (i, 0)), row,
                  pl.BlockSpec((None, N_MOD, D), lambda i: (_stream_of(i, cb), 0, 0)),
                  _rows_weight_spec(wd)],
        out_specs=[row, row],
        out_shape=[jax.ShapeDtypeStruct((T, D), F32), jax.ShapeDtypeStruct((T, D), F32)],
        compiler_params=_params("parallel"),
    )(act, xm, mod, wd)


def _ffn_out_bwd(dx, fo, mod, wd, ctx_rows, name):
    T, D = dx.shape
    dff = N_CHIPS * wd.shape[1]
    cb = ctx_rows // TB

    def body(dx_ref, fo_ref, mod_ref, w_ref, dout_ref, dact_ref, dg2_ref):
        i = pl.program_id(0)

        @pl.when((i == 0) | (i == cb))
        def _():
            dg2_ref[...] = jnp.zeros_like(dg2_ref)

        dxv = dx_ref[...]
        dg2_ref[...] += jnp.sum(dxv * fo_ref[...], axis=0, keepdims=True)
        dout = (dxv * mod_ref[5:6, :]).astype(BF16)
        dout_ref[...] = dout
        dact_ref[...] = _nt(dout, _full(w_ref))

    row = pl.BlockSpec((TB, D), lambda i: (i, 0))
    return pl.pallas_call(
        body, name=name, grid=(T // TB,),
        in_specs=[row, row, pl.BlockSpec((None, N_MOD, D), lambda i: (_stream_of(i, cb), 0, 0)),
                  _rows_weight_spec(wd)],
        out_specs=[row, pl.BlockSpec((TB, dff), lambda i: (i, 0)),
                   pl.BlockSpec((None, 1, D), lambda i: (_stream_of(i, cb), 0, 0))],
        out_shape=[jax.ShapeDtypeStruct((T, D), BF16), jax.ShapeDtypeStruct((T, dff), F32),
                   jax.ShapeDtypeStruct((2, 1, D), F32)],
        compiler_params=_params("arbitrary"),
    )(dx, fo, mod, wd)


def _loss_bwd(x, target, fw, ctx_rows, name):
    T, D = x.shape
    cb = ctx_rows // TB

    def body(x_ref, t_ref, fw_ref, dx_ref, loss_ref, dfw_ref):
        i = pl.program_id(0)

        @pl.when(i == 0)
        def _():
            loss_ref[...] = jnp.zeros_like(loss_ref)
            dfw_ref[...] = jnp.zeros_like(dfw_ref)

        @pl.when(i < cb)
        def _():
            dx_ref[...] = jnp.zeros_like(dx_ref)

        @pl.when(i >= cb)
        def _():
            xv = x_ref[...]
            r = lax.rsqrt(jnp.mean(xv * xv, axis=-1, keepdims=True) + RMS_EPS)
            xh = xv * r
            fwv = fw_ref[...]
            err = xh * fwv - t_ref[...]
            loss_ref[...] += (0.5 / D) * jnp.sum(err * err)
            dy = err * (1.0 / D)
            dfw_ref[...] += jnp.sum(dy * xh, axis=0, keepdims=True)
            dxh = dy * fwv
            dx_ref[...] = r * (dxh - xh * jnp.mean(dxh * xh, axis=-1, keepdims=True))

    row = pl.BlockSpec((TB, D), lambda i: (i, 0))
    return pl.pallas_call(
        body, name=name, grid=(T // TB,),
        in_specs=[row, pl.BlockSpec((TB, D), lambda i: (jnp.maximum(i - cb, 0), 0)), pl.BlockSpec((1, D), lambda i: (0, 0))],
        out_specs=[row, pl.BlockSpec((1, 128), lambda i: (0, 0)), pl.BlockSpec((1, D), lambda i: (0, 0))],
        out_shape=[jax.ShapeDtypeStruct((T, D), F32), jax.ShapeDtypeStruct((1, 128), F32),
                   jax.ShapeDtypeStruct((1, D), F32)],
        compiler_params=_params("arbitrary"),
    )(x, target, fw)


def _adamw(w, gs, m, v, name):
    L, R, C = w.shape
    assert len(gs) == L
    rb = _rows_tile(R, max(16, (1 << 18) // C // 16 * 16))
    bc1 = 1.0 - ADAM_B1 ** ADAM_STEP
    bc2 = 1.0 - ADAM_B2 ** ADAM_STEP

    def body(w_ref, m_ref, v_ref, *rest):
        g_refs, (g_ref, d_ref, nm_ref, nv_ref) = rest[:L], rest[L:]
        layer = pl.program_id(0)
        for li in range(L):
            @pl.when(layer == li)
            def _():
                gv = g_refs[li][...]
                g_ref[...] = gv
                nm = ADAM_B1 * m_ref[...] + (1.0 - ADAM_B1) * gv
                nv = ADAM_B2 * v_ref[...] + (1.0 - ADAM_B2) * (gv * gv)
                nm_ref[...] = nm
                nv_ref[...] = nv
                d_ref[...] = -ADAM_LR * ((nm / bc1) / (jnp.sqrt(nv / bc2) + ADAM_EPS) + ADAM_WD * w_ref[...])

    blk = pl.BlockSpec((None, rb, C), lambda l, i: (l, i, 0))
    gblk = pl.BlockSpec((rb, C), lambda l, i: (i, 0))
    sd = jax.ShapeDtypeStruct((L, R, C), F32)
    return pl.pallas_call(
        body, name=name, grid=(L, R // rb), in_specs=[blk] * 3 + [gblk] * L, out_specs=[blk] * 4, out_shape=[sd] * 4,
        compiler_params=_params("parallel", "parallel"),
    )(w, m, v, *gs)


def _local_step(xs, cv, target, W, layer_weights, on_layer_grads, ctx_rows):
    T, D = xs.shape
    depth = W["norm1_w"].shape[0]
    saved = []
    X = xs
    for l in range(depth):
        s = {}
        Wl = layer_weights(l, X)
        mod_all, sa = _mod_fwd(cv, Wl["ada_w"], W["ada_b"][l][None, :] + Wl["token"], f"mod_fwd_{l}")
        mod = mod_all[:2].reshape(2, N_MOD, D)
        h1 = _norm_mod(X, W["norm1_w"][l][None, :], mod, 0, ctx_rows, f"norm1_{l}")
        parts = _mm_nn_w(h1, Wl["w_in"], F32, f"in_proj_{l}")
        o_f, st_f = _hgrn_fwd(parts, W["hlb"], l, False, ctx_rows, f"hgrn_fwd_f_{l}")
        o, st_b = _hgrn_fwd(parts, W["hlb"], l, True, ctx_rows, f"hgrn_fwd_b_{l}", o_add=o_f)
        ya = _sgu_fwd(parts, W["sgu_ln_w"][l][None, :], W["sgu_ln_b"][l][None, :], W["sgu_w"][l], W["sgu_bt"][l],
                      f"sgu_fwd_{l}")
        Wl.update(Wl.pop("late")(ya))
        yb, pa, pb, mg, tmo, xm, h2 = _token_out_fwd(o, parts, ya, X, mod, W["hnw"][l][None, :] + Wl["late_token"],
                                                     W["norm2_w"][l][None, :], Wl["w_a"], Wl["w_b"], Wl["w_o"], ctx_rows,
                                                     f"token_out_fwd_{l}")
        up = _mm_nn_w(h2, Wl["w_up"], F32, f"up_proj_{l}")
        ac, act = _conv_fwd(up, Wl["conv_w"], W["conv_b"][l][None, :], ctx_rows, f"conv_fwd_{l}")
        xo, fo = _ffn_out_fwd(act, xm, mod, Wl["w_down"], ctx_rows, f"ffn_out_fwd_{l}")
        s.update(X=X, Wl=Wl, mod=mod, mod_all=mod_all, sa=sa, h1=h1, parts=parts, o=o, st_f=st_f, st_b=st_b, ya=ya, yb=yb,
                 pa=pa, pb=pb, mg=mg, tmo=tmo, xm=xm, h2=h2, up=up, ac=ac, act=act, fo=fo)
        saved.append(s)
        X = xo

    dX, loss_row, dfw = _loss_bwd(X, target, W["final_norm_w"][None, :], ctx_rows, "loss_bwd")
    G = {k: [None] * depth for k in ("ada_b", "norm1_w", "sgu_ln_w", "sgu_ln_b", "sgu_w", "sgu_b", "hlb1", "hnw", "norm2_w",
                                     "conv_w", "conv_b", "dmod")}
    dcv = jnp.zeros_like(cv)
    for l in reversed(range(depth)):
        s = saved[l]
        mod, Wl = s["mod"], s["Wl"]
        big = {}
        dout2, dact, dg2 = _ffn_out_bwd(dX, s["fo"], mod, Wl["w_down"], ctx_rows, f"ffn_out_bwd_{l}")
        big["w_down"] = _mm_tn(s["act"], dout2, F32, f"dw_down_{l}")
        da, dv, dcw, dcb = _conv_bwd(s["up"], s["ac"], dact, Wl["conv_w"], ctx_rows, f"conv_bwd_{l}")
        G["conv_w"][l], G["conv_b"][l] = dcw, dcb[0]
        big["w_up"] = _mm_tn(s["h2"], (da, dv), F32, f"dw_up_{l}", out_chips=True)
        dh2 = _mm_nt_w((da, dv), Wl["w_up"], F32, f"dh2_{l}")
        dxm, dm2, dnw2 = _norm_mod_bwd(dh2, s["xm"], dX, W["norm2_w"][l][None, :], mod, 3, ctx_rows, f"norm2_bwd_{l}")
        G["norm2_w"][l] = dnw2[0]
        (dout1, dpa, dpb, dparts, dya, do, dg1, dhnw) = _token_out_bwd(
            dxm, s["tmo"], s["pa"], s["pb"], s["o"], s["parts"], mod, W["hnw"][l][None, :], Wl["w_a"], Wl["w_b"], Wl["w_o"],
            ctx_rows, f"token_out_bwd_{l}")
        G["hnw"][l] = dhnw[0]
        big["w_o"] = _mm_tn(s["mg"], dout1, F32, f"dw_o_{l}")
        big["w_a"] = _mm_tn(s["ya"], dpa, F32, f"dw_a_{l}")
        big["w_b"] = _mm_tn(s["yb"], dpb, F32, f"dw_b_{l}")
        tok = on_layer_grads(l, "early", big)
        dparts, dsw, dsbt, dlnw, dlnb = _sgu_bwd(s["parts"], dya, W["sgu_ln_w"][l][None, :], W["sgu_ln_b"][l][None, :] + tok,
                                                 W["sgu_w"][l], W["sgu_bt"][l], dparts, f"sgu_bwd_{l}")
        G["sgu_w"][l], G["sgu_b"][l], G["sgu_ln_w"][l], G["sgu_ln_b"][l] = dsw, dsbt.T, dlnw[0], dlnb[0]
        dq_f, dz_f, di_f, dlb_f = _hgrn_bwd(s["parts"], W["hlb"], do, s["st_f"], l, False, ctx_rows, f"hgrn_bwd_f_{l}")
        dparts, dlb_b = _hgrn_bwd(s["parts"], W["hlb"], do, s["st_b"], l, True, ctx_rows, f"hgrn_bwd_b_{l}",
                                  other=(dq_f, dz_f, di_f), dparts=dparts)
        G["hlb1"][l] = jnp.concatenate([dlb_f[0], dlb_b[0]])
        tok = on_layer_grads(l, "late", {"w_in": _mm_tn(s["h1"], dparts, F32, f"dw_in_{l}", out_chips=True)})
        dh1 = _mm_nt_w(dparts, Wl["w_in"], F32, f"dh1_{l}")
        tok = tok + on_layer_grads(l, "end", {"after": dh1})
        dX, dm1, dnw1 = _norm_mod_bwd(dh1, s["X"], dxm, W["norm1_w"][l][None, :] + tok, mod, 0, ctx_rows, f"norm1_bwd_{l}")
        G["norm1_w"][l] = dnw1[0]
        dmod = jnp.concatenate([dm1, dg1, dm2, dg2], axis=1).reshape(2, N_MOD * D)
        dmod16 = jnp.concatenate([dmod, jnp.zeros((cv.shape[0] - 2, N_MOD * D), F32)], axis=0)
        G["ada_b"][l] = dmod[0] + dmod[1]
        G["dmod"][l] = dmod
        dcv = dcv + _cvec_bwd(dmod16, Wl["ada_w"], cv, f"dcvec_{l}")
    G["c_ctx"] = dcv[0]
    G["final_norm_w"] = dfw[0]
    return loss_row[0, 0], dX, G, saved[0]["sa"]


def _chip_peers(x, y, c):
    return [((1 - x, y, c), 2 * (1 - x) + y), ((x, 1 - y, c), 2 * x + 1 - y), ((1 - x, 1 - y, c), 2 * (1 - x) + 1 - y)]


def _rdma_call(ins, out_shapes, plan, n_remote, n_local, name, aliases=None):
    n_in, n_out = len(ins), len(out_shapes)

    def body(*refs):
        in_refs, out_refs = refs[:n_in], refs[n_in:n_in + n_out]
        send_sems, recv_sems, local_sems = refs[n_in + n_out:]
        x, y, c = lax.axis_index("x"), lax.axis_index("y"), lax.axis_index("c")
        remote, local = plan(in_refs, out_refs, x, y, c)
        assert len(remote) == n_remote and len(local) == n_local, (name, len(remote), len(local))
        copies = [pltpu.make_async_copy(s, d, local_sems.at[i]) for i, (s, d) in enumerate(local)]
        copies += [pltpu.make_async_remote_copy(src_ref=s, dst_ref=d, send_sem=send_sems.at[k], recv_sem=recv_sems.at[k],
                                                device_id=dev, device_id_type=pl.DeviceIdType.MESH)
                   for k, (s, d, dev) in enumerate(remote)]
        for cp in copies:
            cp.start()
        for cp in copies:
            cp.wait()

    hbm = pl.BlockSpec(memory_space=pltpu.HBM)
    return pl.pallas_call(
        body, name=name, in_specs=[hbm] * n_in, out_specs=[hbm] * n_out, out_shape=out_shapes,
        scratch_shapes=[pltpu.SemaphoreType.DMA((n_remote,)), pltpu.SemaphoreType.DMA((n_remote,)),
                        pltpu.SemaphoreType.DMA((max(n_local, 1),))],
        input_output_aliases=aliases or {},
    )(*ins)


DMA_PIECE_BYTES = 1 << 18
DMA_MAX_PIECES = 8


def _row_pieces(shape, dtype):
    rows = shape[0]
    row_bytes = jnp.dtype(dtype).itemsize
    for d in shape[1:]:
        row_bytes *= d
    n = 1
    while n < DMA_MAX_PIECES and rows % (2 * n * 16) == 0 and rows * row_bytes // (2 * n) >= DMA_PIECE_BYTES:
        n *= 2
    return [(i * (rows // n), rows // n) for i in range(n)]


def _half_pieces(o, c):
    r2 = o.shape[1] // 2
    return [pl.ds(c * r2 + st, sz) for st, sz in _row_pieces((r2,) + o.shape[2:], o.dtype)]


def _n_half_pieces(arrays):
    return sum(len(_row_pieces((a.shape[1] // 2,) + a.shape[2:], a.dtype)) for a in arrays)


def _plan_gather_far(lands, x, y, c):
    me = 2 * x + y
    return [(o.at[me, rows], o.at[me, rows], dev) for dev, _ in _chip_peers(x, y, c) for o in lands
            for rows in _half_pieces(o, c)]


def _plan_gather_near(lands, x, y, c):
    return [(o.at[idx, rows], o.at[idx, rows], (x, y, 1 - c)) for _, idx in _chip_peers(x, y, c) for o in lands
            for rows in _half_pieces(o, c)]


def _gather_weights(lands, name):
    n = len(lands)
    n_far = (N_CHIPS - 1) * _n_half_pieces(lands)

    def body(*refs):
        outs = refs[n:2 * n]
        far_send, far_recv, near_send, near_recv = refs[2 * n:]
        x, y, c = lax.axis_index("x"), lax.axis_index("y"), lax.axis_index("c")
        mk = lambda plan, send, recv: [
            pltpu.make_async_remote_copy(src_ref=s, dst_ref=d, send_sem=send.at[k], recv_sem=recv.at[k], device_id=dev,
                                         device_id_type=pl.DeviceIdType.MESH)
            for k, (s, d, dev) in enumerate(plan(outs, x, y, c))]
        far, near = mk(_plan_gather_far, far_send, far_recv), mk(_plan_gather_near, near_send, near_recv)
        assert len(far) == n_far and len(near) == n_far
        for cp in far:
            cp.start()
        for k in range(n_far):
            far[k].wait_recv()
            near[k].start()
        for k in range(n_far):
            near[k].wait_recv()
        for cp in far + near:
            cp.wait_send()

    hbm = pl.BlockSpec(memory_space=pltpu.HBM)
    sems = pltpu.SemaphoreType.DMA((n_far,))
    return pl.pallas_call(
        body, name=name, in_specs=[hbm] * n, out_specs=[hbm] * n,
        out_shape=[jax.ShapeDtypeStruct(a.shape, a.dtype) for a in lands],
        scratch_shapes=[sems, sems, sems, sems], input_output_aliases={i: i for i in range(n)},
    )(*lands)


def _gather_all(v, name):
    def plan(ins, outs, x, y, c):
        (s,), (o,) = ins, outs
        me = 4 * x + 2 * y + c
        flip = lambda a, f: 1 - a if f else a
        remote = [(s, o.at[me], (flip(x, m & 4), flip(y, m & 2), flip(c, m & 1))) for m in range(1, 8)]
        return remote, [(s, o.at[me])]

    return _rdma_call([v], [jax.ShapeDtypeStruct((8,) + v.shape, v.dtype)], plan, 7, 1, name)[0]


def _plan_pair(ins, lands, x, y, c):
    return [(a.at[j, 1 - c, pl.ds(st, sz)], o.at[j, pl.ds(st, sz)], (x, y, 1 - c)) for a, o in zip(ins, lands)
            for j in range(N_CHIPS) for st, sz in _row_pieces(a.shape[2:], a.dtype)]


def _n_pair_copies(parts):
    return N_CHIPS * sum(len(_row_pieces(a.shape[2:], a.dtype)) for a in parts)


def _reduce_pair(parts, name):
    shapes = [jax.ShapeDtypeStruct((N_CHIPS,) + a.shape[2:], a.dtype) for a in parts]
    return _rdma_call(parts, shapes, lambda ins, outs, x, y, c: (_plan_pair(ins, outs, x, y, c), []),
                      _n_pair_copies(parts), 0, name)


def _plan_chips(ins, lands, x, y, c):
    me = 2 * x + y
    return [(a.at[idx, pl.ds(st, sz)], o.at[me, pl.ds(st, sz)], dev) for dev, idx in _chip_peers(x, y, c)
            for a, o in zip(ins, lands) for st, sz in _row_pieces(a.shape[1:], a.dtype)]


def _n_chips_copies(parts):
    return (N_CHIPS - 1) * sum(len(_row_pieces(a.shape[1:], a.dtype)) for a in parts)


def _gather_pair(halves, name):
    def plan(ins, outs, x, y, c):
        return [(o.at[c, pl.ds(st, sz)], o.at[c, pl.ds(st, sz)], (x, y, 1 - c)) for o in outs
                for st, sz in _row_pieces(o.shape[1:], o.dtype)], []

    shapes = [jax.ShapeDtypeStruct(a.shape, a.dtype) for a in halves]
    n_remote = sum(len(_row_pieces(a.shape[1:], a.dtype)) for a in halves)
    return _rdma_call(halves, shapes, plan, n_remote, 0, name, aliases={i: i for i in range(len(halves))})


def _split_start(ins, lands, plan, n_remote, name):
    n_buf = len(ins) + len(lands)

    def body(*refs):
        in_refs, land_refs = refs[:len(ins)], refs[len(ins):n_buf]
        send_sems, recv_sems, token = refs[n_buf], refs[n_buf + 1], refs[-1]
        x, y, c = lax.axis_index("x"), lax.axis_index("y"), lax.axis_index("c")
        remote = plan(in_refs, land_refs, x, y, c)
        assert len(remote) == n_remote, (name, len(remote))
        for k, (s, d, dev) in enumerate(remote):
            pltpu.make_async_remote_copy(src_ref=s, dst_ref=d, send_sem=send_sems.at[k], recv_sem=recv_sems.at[k],
                                         device_id=dev, device_id_type=pl.DeviceIdType.MESH).start()
        token[...] = jnp.zeros_like(token)

    hbm = pl.BlockSpec(memory_space=pltpu.HBM)
    sem = pl.BlockSpec(memory_space=pltpu.SEMAPHORE)
    bufs = list(ins) + list(lands)
    out = pl.pallas_call(
        body, name=name, in_specs=[hbm] * n_buf,
        out_specs=(sem, sem) + (hbm,) * n_buf + (pl.BlockSpec(memory_space=pltpu.VMEM),),
        out_shape=(pltpu.SemaphoreType.DMA((n_remote,)), pltpu.SemaphoreType.DMA((n_remote,)))
        + tuple(pltpu.HBM(a.shape, a.dtype) for a in bufs) + (jax.ShapeDtypeStruct((8, 128), F32),),
        input_output_aliases={i: 2 + i for i in range(n_buf)},
        compiler_params=pltpu.CompilerParams(has_side_effects=pltpu.SideEffectType.DATAFLOW_SIDE_EFFECTING),
    )(*[pltpu.with_memory_space_constraint(a, pltpu.HBM) for a in bufs])
    return dict(send=out[0], recv=out[1], ins=list(out[2:2 + len(ins)]), lands=list(out[2 + len(ins):2 + n_buf]),
                token=out[-1][0, 0], token_array=out[-1], plan=plan, n_remote=n_remote)


def _split_wait(st, after, name):
    n_in, n_buf = len(st["ins"]), len(st["ins"]) + len(st["lands"])
    plan, n_remote = st["plan"], st["n_remote"]

    def body(*refs):
        in_refs, land_refs = refs[:n_in], refs[n_in:n_buf]
        send_sems, recv_sems = refs[n_buf], refs[n_buf + 1]
        x, y, c = lax.axis_index("x"), lax.axis_index("y"), lax.axis_index("c")
        for k, (s, d, dev) in enumerate(plan(in_refs, land_refs, x, y, c)):
            cp = pltpu.make_async_remote_copy(src_ref=s, dst_ref=d, send_sem=send_sems.at[k], recv_sem=recv_sems.at[k],
                                              device_id=dev, device_id_type=pl.DeviceIdType.MESH)
            cp.wait_send()
            cp.wait_recv()

    hbm = pl.BlockSpec(memory_space=pltpu.HBM)
    sem = pl.BlockSpec(memory_space=pltpu.SEMAPHORE)
    bufs = st["ins"] + st["lands"]
    out = pl.pallas_call(
        body, name=name, in_specs=[hbm] * n_buf + [sem, sem, pl.BlockSpec(memory_space=pl.ANY)],
        out_specs=[hbm] * n_buf, out_shape=[pltpu.HBM(a.shape, a.dtype) for a in bufs],
        input_output_aliases={i: i for i in range(n_buf)},
        compiler_params=pltpu.CompilerParams(has_side_effects=pltpu.SideEffectType.DATAFLOW_SIDE_EFFECTING),
    )(*bufs, st["send"], st["recv"], after)
    return list(out[:n_in]), list(out[n_in:])


def _pair_forward(lands, name):
    shapes = [jax.ShapeDtypeStruct(a.shape, a.dtype) for a in lands]
    return _rdma_call(lands, shapes, lambda ins, outs, x, y, c: (_plan_gather_near(outs, x, y, c), []),
                      (N_CHIPS - 1) * _n_half_pieces(lands), 0, name, aliases={i: i for i in range(len(lands))})


def _sum_block_rows(r, C):
    return _rows_tile(r, max(16, (1 << 18) // C // 16 * 16))


def _sum_pair(a, recv, cidx, name):
    nch, _, r, C = a.shape
    rb = _sum_block_rows(r, C)

    def body(c_ref, a_ref, r_ref, o_ref):
        o_ref[...] = (a_ref[...] + r_ref[...]).astype(BF16)

    blk = pl.BlockSpec((None, rb, C), lambda j, i, c: (j, i, 0))
    return pl.pallas_call(
        body, name=name,
        grid_spec=pltpu.PrefetchScalarGridSpec(
            num_scalar_prefetch=1, grid=(nch, r // rb),
            in_specs=[pl.BlockSpec((None, None, rb, C), lambda j, i, c: (j, c[0], i, 0)), blk], out_specs=blk),
        out_shape=jax.ShapeDtypeStruct((nch, r, C), BF16),
        compiler_params=_params("parallel", "parallel"),
    )(cidx, a, recv)


def _sum_chips(mine, recv, ids, name):
    nch, r, C = recv.shape
    rb = _sum_block_rows(r, C)

    def body(ids_ref, m_ref, *rest):
        r_refs, o_ref = rest[:nch], rest[nch]
        chip = ids_ref[1]
        own = m_ref[...].astype(F32)
        acc = jnp.where(chip == 0, own, r_refs[0][...].astype(F32))
        for q in range(1, nch):
            acc = acc + jnp.where(chip == q, own, r_refs[q][...].astype(F32))
        o_ref[...] = acc

    def slot(q):
        return pl.BlockSpec((None, rb, C), lambda i, ids: (jnp.where(ids[1] == q, (q + 1) % nch, q), i, 0))

    return pl.pallas_call(
        body, name=name,
        grid_spec=pltpu.PrefetchScalarGridSpec(
            num_scalar_prefetch=1, grid=(r // rb,),
            in_specs=[pl.BlockSpec((None, rb, C), lambda i, ids: (ids[1], i, 0))] + [slot(q) for q in range(nch)],
            out_specs=pl.BlockSpec((None, rb, C), lambda i, ids: (ids[0], i, 0))),
        out_shape=jax.ShapeDtypeStruct((N_CORES, r, C), F32),
        compiler_params=_params("parallel"),
    )(ids, mine, *([recv] * nch))


PACK_COLS = 1024
_SHARDED = ("ada_w", "w_in", "w_branch_a", "w_branch_b", "w_out", "ffn_w_up", "ffn_w_down")
_LAYER_KEYS = ("ada_w", "w_in", "w_a", "w_b", "w_o", "w_up", "w_down")
_SMALL = ("c_ctx", "ada_b", "norm1_w", "sgu_ln_w", "sgu_ln_b", "sgu_w", "sgu_b", "hgrn_lower_bounds", "hgrn_norm_w",
          "norm2_w", "ffn_conv_b", "final_norm_w")
_ORDER = ("c_ctx", "ada_w", "ada_b", "norm1_w", "w_in", "sgu_ln_w", "sgu_ln_b", "sgu_w", "sgu_b", "hgrn_lower_bounds",
          "hgrn_norm_w", "w_branch_a", "w_branch_b", "w_out", "norm2_w", "ffn_w_up", "ffn_conv_w", "ffn_conv_b",
          "ffn_w_down", "final_norm_w")


def _pad_to(v, n):
    return jnp.concatenate([v, jnp.zeros((n - v.shape[0],), v.dtype)]) if v.shape[0] < n else v


def _round_up(n, m):
    return (n + m - 1) // m * m


def _pack(arrays, n_pad):
    flat = jnp.concatenate([a.reshape(-1) for a in arrays])
    return _pad_to(flat, n_pad)


def _unpack(flat, like):
    out, off = [], 0
    for a in like:
        out.append(flat[off:off + a.size].reshape(a.shape))
        off += a.size
    return out


def kernel(x, c, ctx, c_ctx, ada_w, ada_b, norm1_w, w_in, sgu_ln_w, sgu_ln_b, sgu_w, sgu_b, hgrn_lower_bounds, hgrn_norm_w, w_branch_a, w_branch_b, w_out, norm2_w, ffn_w_up, ffn_conv_w, ffn_conv_b, ffn_w_down, final_norm_w, loss_target, m_c_ctx, m_ada_w, m_ada_b, m_norm1_w, m_w_in, m_sgu_ln_w, m_sgu_ln_b, m_sgu_w, m_sgu_b, m_hgrn_lower_bounds, m_hgrn_norm_w, m_w_branch_a, m_w_branch_b, m_w_out, m_norm2_w, m_ffn_w_up, m_ffn_conv_w, m_ffn_conv_b, m_ffn_w_down, m_final_norm_w, v_c_ctx, v_ada_w, v_ada_b, v_norm1_w, v_w_in, v_sgu_ln_w, v_sgu_ln_b, v_sgu_w, v_sgu_b, v_hgrn_lower_bounds, v_hgrn_norm_w, v_w_branch_a, v_w_branch_b, v_w_out, v_norm2_w, v_ffn_w_up, v_ffn_conv_w, v_ffn_conv_b, v_ffn_w_down, v_final_norm_w):
    w = dict(c_ctx=c_ctx, ada_w=ada_w, ada_b=ada_b, norm1_w=norm1_w, w_in=w_in, sgu_ln_w=sgu_ln_w, sgu_ln_b=sgu_ln_b,
             sgu_w=sgu_w, sgu_b=sgu_b, hgrn_lower_bounds=hgrn_lower_bounds, hgrn_norm_w=hgrn_norm_w, w_branch_a=w_branch_a,
             w_branch_b=w_branch_b, w_out=w_out, norm2_w=norm2_w, ffn_w_up=ffn_w_up, ffn_conv_w=ffn_conv_w,
             ffn_conv_b=ffn_conv_b, ffn_w_down=ffn_w_down, final_norm_w=final_norm_w)
    mom = dict(zip(_ORDER, (m_c_ctx, m_ada_w, m_ada_b, m_norm1_w, m_w_in, m_sgu_ln_w, m_sgu_ln_b, m_sgu_w, m_sgu_b,
                            m_hgrn_lower_bounds, m_hgrn_norm_w, m_w_branch_a, m_w_branch_b, m_w_out, m_norm2_w, m_ffn_w_up,
                            m_ffn_conv_w, m_ffn_conv_b, m_ffn_w_down, m_final_norm_w)))
    var = dict(zip(_ORDER, (v_c_ctx, v_ada_w, v_ada_b, v_norm1_w, v_w_in, v_sgu_ln_w, v_sgu_ln_b, v_sgu_w, v_sgu_b,
                            v_hgrn_lower_bounds, v_hgrn_norm_w, v_w_branch_a, v_w_branch_b, v_w_out, v_norm2_w, v_ffn_w_up,
                            v_ffn_conv_w, v_ffn_conv_b, v_ffn_w_down, v_final_norm_w)))
    depth, D = norm1_w.shape
    dff = ffn_conv_b.shape[1]
    ctx_rows = ctx.shape[1]

    assert depth == 2, "the lower-bound softmax is written for two layers"
    core = lax.axis_index("c")
    chip = 2 * lax.axis_index("x") + lax.axis_index("y")
    ids = jnp.stack([core, chip]).astype(jnp.int32)

    first, rest = _LAYER_KEYS[:2], _LAYER_KEYS[2:]
    shard = lambda l, k: w[_SHARDED[_LAYER_KEYS.index(k)]][l].astype(BF16)
    started, conv_full = {}, []

    def landing(s):
        return lax.dynamic_update_slice(lax.empty((N_CHIPS,) + s.shape, s.dtype), s[None], (chip,) + (0,) * s.ndim)

    def start_gather(l, keys, tag):
        lands = [landing(shard(l, k)) for k in keys]
        started[tag] = _split_start([], lands, lambda ins, lds, x, y, c: _plan_gather_far(lds, x, y, c),
                                    (N_CHIPS - 1) * _n_half_pieces(lands), f"gather_start_{tag}")
        return started[tag]["token"]

    def finish_gather(keys, tag, after):
        _, lands = _split_wait(started[tag], after, f"gather_wait_{tag}")
        return dict(zip(keys, _pair_forward(lands, f"gather_forward_{tag}")))

    def layer_weights(l, after):
        if l == 0:
            got = _gather_weights([landing(shard(0, k)) for k in first] + [landing(ffn_conv_w)], "gather_weights_first")
            conv_full.append(jnp.transpose(got[-1], (1, 2, 3, 0, 4)).reshape(depth, 9, dff))
            out = dict(zip(first, got), token=start_gather(0, rest, "rest_0"))
        else:
            out = dict(finish_gather(first, f"first_{l}", after), token=0.0)

        def late(after_late):
            more = finish_gather(rest, f"rest_{l}", after_late)
            more["late_token"] = 0.0
            if l + 1 < depth:
                more["late_token"] = start_gather(l + 1, first, f"first_{l + 1}") + start_gather(l + 1, rest, f"rest_{l + 1}")
            return more

        return dict(out, conv_w=conv_full[0][l], late=late)

    groups, order = {}, []

    def as_parts(gs):
        return [g.reshape(N_CHIPS, N_CORES, g.size // (N_CHIPS * N_CORES * g.shape[-1]), g.shape[-1]) for g in gs]

    def pair_start(tag, l, keys, gs):
        parts = as_parts(gs)
        lands = [lax.empty((N_CHIPS,) + p.shape[2:], p.dtype) for p in parts]
        groups[tag] = dict(l=l, keys=keys, pair=_split_start(parts, lands, _plan_pair, _n_pair_copies(parts),
                                                             f"reduce_pair_start_{tag}"))
        order.append(tag)
        return groups[tag]["pair"]["token"]

    def chips_start(tag, after):
        parts, other = _split_wait(groups[tag]["pair"], after, f"reduce_pair_wait_{tag}")
        sums = [_sum_pair(a, o, ids, f"sum_pair_{tag}_{i}") for i, (a, o) in enumerate(zip(parts, other))]
        lands = [lax.empty(s.shape, s.dtype) for s in sums]
        groups[tag]["chips"] = _split_start(sums, lands, _plan_chips, _n_chips_copies(sums), f"reduce_chips_start_{tag}")
        return groups[tag]["chips"]["token"]

    def chips_finish(tag, after):
        sums, recv = _split_wait(groups[tag]["chips"], after, f"reduce_chips_wait_{tag}")
        return {(groups[tag]["l"], k): _sum_chips(sums[i], recv[i], ids, f"sum_chips_{tag}_{i}")
                for i, k in enumerate(groups[tag]["keys"])}

    def on_layer_grads(l, stage, gs):
        if stage == "early":
            return pair_start(f"early_{l}", l, list(gs), list(gs.values()))
        if stage == "late":
            return pair_start(f"late_{l}", l, ["w_in"], [gs["w_in"]]) + chips_start(f"early_{l}", gs["w_in"])
        return chips_start(f"late_{l}", gs["after"])

    W = dict(ada_b=ada_b, norm1_w=norm1_w, sgu_ln_w=sgu_ln_w, sgu_ln_b=sgu_ln_b, sgu_w=sgu_w.astype(BF16),
             sgu_bt=jnp.swapaxes(sgu_b, 1, 2), hlb=hgrn_lower_bounds, hnw=hgrn_norm_w, norm2_w=norm2_w, conv_b=ffn_conv_b,
             final_norm_w=final_norm_w)
    xs = jnp.concatenate([ctx[0], x[0]], axis=0)
    cv = jnp.concatenate([c_ctx[None, :], c, jnp.zeros((14, D), F32)], axis=0)
    loss_local, dxs, G, sa = _local_step(xs, cv, loss_target[0], W, layer_weights, on_layer_grads, ctx_rows)
    loss = lax.psum(loss_local, ("x", "y", "c"))
    grad_x = dxs[ctx_rows:][None]

    pad8 = lambda a: jnp.pad(a, ((0, 8 - a.shape[0]), (0, 0)))
    fact = jnp.concatenate([pad8(sa[1:2].astype(F32))] + [pad8(G["dmod"][l][1].reshape(N_MOD, D)) for l in range(depth)]
                           + [pad8(G["dmod"][l][0].reshape(N_MOD, D)) for l in range(depth)], axis=0)
    facts = _gather_all(fact, "gather_mod_factors")
    lhs = jnp.concatenate([facts[:, 0].astype(BF16), jnp.broadcast_to(sa[0:1], (8, D))], axis=0)
    ada_cols = N_MOD * D // N_CHIPS
    g_ada = []
    for l in range(depth):
        lo_x, lo_c = 8 * (1 + l), 8 * (1 + depth + l)
        rhs = jnp.concatenate([facts[:, lo_x:lo_x + N_MOD].reshape(8, N_MOD * D),
                               facts[:, lo_c:lo_c + N_MOD].reshape(8, N_MOD * D)], axis=0)
        rhs = lax.dynamic_slice_in_dim(rhs, chip * ada_cols, ada_cols, axis=1).astype(BF16)
        g_ada.append(_mm_tn(lhs, rhs, F32, f"dw_ada_{l}"))

    dh = G["hlb1"][depth - 1]
    small_like = [w[k] for k in _SMALL] + [jnp.zeros((depth, 9, dff), F32)]
    small = [G["c_ctx"], jnp.stack(G["ada_b"]), jnp.stack(G["norm1_w"]), jnp.stack(G["sgu_ln_w"]), jnp.stack(G["sgu_ln_b"]),
             jnp.stack(G["sgu_w"]), jnp.stack(G["sgu_b"]), jnp.stack([-dh, dh]), jnp.stack(G["hnw"]), jnp.stack(G["norm2_w"]),
             jnp.stack(G["conv_b"]), G["final_norm_w"], jnp.stack(G["conv_w"])]
    n_small = sum(a.size for a in small)
    n_small_pad = _round_up(n_small, N_CORES * 16 * PACK_COLS)
    small_rows = n_small_pad // (N_CORES * PACK_COLS)
    small_rep = jnp.broadcast_to(_pack(small, n_small_pad).reshape(1, N_CORES, small_rows, PACK_COLS),
                                 (N_CHIPS, N_CORES, small_rows, PACK_COLS))
    small_parts = as_parts([small_rep])
    small_sums = [_sum_pair(small_parts[0], _reduce_pair(small_parts, "reduce_pair_small")[0], ids, "sum_pair_small")]
    groups["small"] = dict(l=None, keys=["small"], chips=_split_start(
        small_sums, [lax.empty(small_sums[0].shape, small_sums[0].dtype)], _plan_chips, _n_chips_copies(small_sums),
        "reduce_chips_start_small"))

    def gather_halves(halves, name):
        return dict(zip(halves, _gather_pair(list(halves.values()), name)))

    last = order[-1]
    halves = {}
    for tag in order[:-1]:
        halves.update(chips_finish(tag, groups["small"]["chips"]["token_array"]))
    reduced = gather_halves(halves, "gather_pair")
    grads, delta, new_m, new_v = {}, {}, {}, {}

    def adamw_sharded(i):
        k = _SHARDED[i]
        gs = g_ada if i == 0 else [reduced[(l, _LAYER_KEYS[i])].reshape(w[k].shape[1:]) for l in range(depth)]
        grads[k], delta[k], new_m[k], new_v[k] = _adamw(w[k], gs, mom[k], var[k], f"adamw_{k}")

    last_keys = groups[last]["keys"]
    for i in range(len(_SHARDED)):
        if _LAYER_KEYS[i] not in last_keys:
            adamw_sharded(i)
    halves = chips_finish(last, new_v[_SHARDED[-1]])
    halves.update(chips_finish("small", new_v[_SHARDED[-1]]))
    reduced.update(gather_halves(halves, "gather_pair_last"))
    for i in range(len(_SHARDED)):
        if _LAYER_KEYS[i] in last_keys:
            adamw_sharded(i)

    g_small = _unpack(reduced[(None, "small")].reshape(-1), small_like)
    grads.update(zip(_SMALL, g_small[:-1]))
    grads["ffn_conv_w"] = lax.dynamic_slice_in_dim(g_small[-1].reshape(depth, 3, 3, dff), chip * (dff // N_CHIPS),
                                                   dff // N_CHIPS, axis=3)
    packed = _SMALL + ("ffn_conv_w",)
    n_pad = _round_up(sum(w[k].size for k in packed), 16 * PACK_COLS)
    pack = lambda t: _pack([t[k] for k in packed], n_pad).reshape(1, -1, PACK_COLS)
    _, d, nm, nv = _adamw(pack(w), [pack(grads)[0]], pack(mom), pack(var), "adamw_packed")
    like = [w[k] for k in packed]
    for src, dst in ((d, delta), (nm, new_m), (nv, new_v)):
        dst.update(zip(packed, _unpack(src.reshape(-1), like)))

    return (loss, grad_x, *[grads[k] for k in _ORDER], *[delta[k] for k in _ORDER], *[new_m[k] for k in _ORDER],
            *[new_v[k] for k in _ORDER])
```

```python
import functools

import jax
import jax.numpy as jnp
from jax import lax
from jax.experimental import pallas as pl
from jax.experimental.pallas import tpu as pltpu

F32 = jnp.float32
BF16 = jnp.bfloat16

GRID_W = 64
HG_CHUNK = 64
SGU_CHUNK = 128
HEAD = 128
TB = 256
N_MOD = 6
RMS_EPS = 1e-6
LN_EPS = 1e-5
VMEM_LIMIT = 48 * 1024 * 1024
VMEM_LIMIT_PAIR = 58 * 1024 * 1024
VMEM_WHOLE_K = 50 * 1024 * 1024
N_CHIPS = 4
N_CORES = 2

ADAM_LR = 0.001
ADAM_B1 = 0.9
ADAM_B2 = 0.999
ADAM_EPS = 1e-08
ADAM_WD = 0.01
ADAM_STEP = 10

_GELU_C = 0.7978845608028654
_GELU_A = 0.044715


def _sigmoid(x):
    return 0.5 * jnp.tanh(0.5 * x) + 0.5


def _silu(x):
    return x * _sigmoid(x)


def _silu_both(x):
    s = _sigmoid(x)
    return x * s, s * (1.0 + x * (1.0 - s))


def _dsilu(x):
    return _silu_both(x)[1]


def _gelu_both(x):
    x2 = x * x
    t = jnp.tanh(_GELU_C * (x + _GELU_A * x2 * x))
    h = 0.5 * (1.0 + t)
    return x * h, h + 0.5 * x * (1.0 - t * t) * (_GELU_C + 3.0 * _GELU_C * _GELU_A * x2)


def _gelu(x):
    return 0.5 * x * (1.0 + jnp.tanh(_GELU_C * (x + _GELU_A * x * x * x)))


def _dot(a, b, ca, cb):
    return lax.dot_general(a, b, (((ca,), (cb,)), ((), ())), preferred_element_type=F32)


def _nn(a, b):
    return _dot(a, b, 1, 0)


def _nt(a, b):
    return _dot(a, b, 1, 1)


def _tn(a, b):
    return _dot(a, b, 0, 0)


def _params(*sem, vmem=VMEM_LIMIT):
    return pltpu.CompilerParams(dimension_semantics=sem if sem else None, vmem_limit_bytes=vmem)


def _stream_of(i, ctx_blocks):
    return (i >= ctx_blocks).astype(jnp.int32)


def _mm(a, b, mode, tm, tn, tk, out_dtype, name, b_chips=False, out_chips=False, vmem=VMEM_LIMIT):
    a_pair, b_pair = isinstance(a, tuple), isinstance(b, tuple)
    assert (not a_pair or mode == "nt") and (not b_pair or (mode == "tn" and not b_chips))
    ashape = (a[0].shape[0], 2 * a[0].shape[1]) if a_pair else a.shape
    if b_pair:
        bshape = (b[0].shape[0], 2 * b[0].shape[1])
    elif not b_chips:
        bshape = b.shape
    else:
        bshape = (b.shape[1], N_CHIPS * b.shape[2])
    if mode == "nn":
        (M, K), (K2, N) = ashape, bshape
    elif mode == "nt":
        (M, K), (N, K2) = ashape, bshape
    else:
        (K, M), (K2, N) = ashape, bshape
    assert K == K2 and M % tm == 0 and N % tn == 0 and K % tk == 0, (name, ashape, bshape, tm, tn, tk)
    nk = K // tk
    if a_pair:
        n1 = a[0].shape[1] // tk
        assert a[0].shape[1] % tk == 0
        a_specs = [pl.BlockSpec((tm, tk), lambda j, i, k: (i, jnp.minimum(k, n1 - 1))),
                   pl.BlockSpec((tm, tk), lambda j, i, k: (i, jnp.maximum(k - n1, 0)))]
    elif mode == "tn":
        a_specs = [pl.BlockSpec((tk, tm), lambda j, i, k: (k, i))]
    else:
        a_specs = [pl.BlockSpec((tm, tk), lambda j, i, k: (i, k))]
    if b_pair:
        n1 = b[0].shape[1] // tn
        assert b[0].shape[1] % tn == 0
        b_specs = [pl.BlockSpec((tk, tn), lambda j, i, k: (k, jnp.minimum(j, n1 - 1))),
                   pl.BlockSpec((tk, tn), lambda j, i, k: (k, jnp.maximum(j - n1, 0)))]
    elif not b_chips:
        if mode == "nt":
            b_spec = pl.BlockSpec((tn, tk), lambda j, i, k: (j, k))
        else:
            b_spec = pl.BlockSpec((tk, tn), lambda j, i, k: (k, j))
    else:
        cols = b.shape[2]
        if mode == "nn":
            per = cols // tn
            assert cols % tn == 0
            b_spec = pl.BlockSpec((None, tk, tn), lambda j, i, k: (j // per, k, j % per))
        else:
            per = cols // tk
            assert mode == "nt" and cols % tk == 0
            b_spec = pl.BlockSpec((None, tn, tk), lambda j, i, k: (k // per, j, k % per))
    if not b_pair:
        b_specs = [b_spec]
    if out_chips:
        per_o = (N // N_CHIPS) // tn
        assert (N // N_CHIPS) % tn == 0
        o_spec = pl.BlockSpec((None, tm, tn), lambda j, i, k: (j // per_o, i, j % per_o))
        o_shape = (N_CHIPS, M, N // N_CHIPS)
    else:
        o_spec = pl.BlockSpec((tm, tn), lambda j, i, k: (i, j))
        o_shape = (M, N)
    ca, cb = {"nn": (1, 0), "nt": (1, 1), "tn": (0, 0)}[mode]

    in_place = nk == 1
    na, nb = len(a_specs), len(b_specs)

    def body(*refs):
        a_refs, b_refs, rest = refs[:na], refs[na:na + nb], refs[na + nb:]
        if in_place:
            (o_ref,) = rest
        else:
            o_ref, acc = rest
        k = pl.program_id(2)

        if not in_place:
            @pl.when(k == 0)
            def _():
                acc[...] = jnp.zeros_like(acc)

        def multiply(which):
            part = _dot(a_refs[which if a_pair else 0][...], b_refs[which if b_pair else 0][...], ca, cb)
            if in_place:
                o_ref[...] = part.astype(out_dtype)
            else:
                acc[...] += part

        if a_pair or b_pair:
            first = (k < n1) if a_pair else (pl.program_id(0) < n1)
            pl.when(first)(functools.partial(multiply, 0))
            pl.when(jnp.logical_not(first))(functools.partial(multiply, 1))
        else:
            multiply(0)

        if not in_place:
            @pl.when(k == nk - 1)
            def _():
                o_ref[...] = acc[...].astype(out_dtype)

    ins = (list(a) if a_pair else [a]) + (list(b) if b_pair else [b])
    return pl.pallas_call(
        body, name=name, grid=(N // tn, M // tm, nk), in_specs=a_specs + b_specs, out_specs=o_spec,
        out_shape=jax.ShapeDtypeStruct(o_shape, out_dtype),
        scratch_shapes=[] if in_place else [pltpu.VMEM((tm, tn), F32)],
        compiler_params=_params("parallel", "parallel", "arbitrary", vmem=vmem),
    )(*ins)


def _tile(n, pref):
    if n <= pref:
        return n
    best = None
    for t in range(128, pref + 1, 128):
        if n % t == 0:
            best = t
    assert best is not None, (n, pref)
    return best


def _rows_tile(n, pref):
    if n <= pref:
        return n
    best = None
    for t in range(16, pref + 1, 16):
        if n % t == 0:
            best = t
    assert best is not None, (n, pref)
    return best


def _mm_nn_w(a, wg, out_dtype, name):
    M, K = a.shape
    return _mm(a, wg, "nn", _rows_tile(M, 2176), _tile(wg.shape[2], 1536), _tile(K, 1536), out_dtype, name, b_chips=True)


def _mm_nt_w(a, wg, out_dtype, name):
    M = a[0].shape[0] if isinstance(a, tuple) else a.shape[0]
    return _mm(a, wg, "nt", _rows_tile(M, 1088), _tile(wg.shape[1], 1024), _tile(wg.shape[2], 2304), out_dtype, name,
               b_chips=True)


def _mm_tn(a, b, out_dtype, name, out_chips=False):
    K, M = a.shape
    N = 2 * b[0].shape[1] if isinstance(b, tuple) else b.shape[1]
    ncol = N // N_CHIPS if out_chips else N
    tm, tn = _tile(M, 1408), _tile(ncol, 1408)
    if tm * tn > 1408 * 1152:
        tn = _tile(ncol, 1152)
    if isinstance(b, tuple):
        return _mm(a, b, "tn", tm, tn, _rows_tile(K, 2176), out_dtype, name, out_chips=out_chips, vmem=VMEM_LIMIT_PAIR)
    whole = 2 * (K * tm * a.dtype.itemsize + K * tn * b.dtype.itemsize + tm * tn * jnp.dtype(out_dtype).itemsize)
    if whole <= VMEM_WHOLE_K and (M // tm) * (N // tn) >= 4:
        return _mm(a, b, "tn", tm, tn, K, out_dtype, name, out_chips=out_chips, vmem=VMEM_LIMIT_PAIR)
    return _mm(a, b, "tn", tm, tn, _rows_tile(K, 2176), out_dtype, name, out_chips=out_chips)


def _mod_fwd(cv, wg, b, name):
    R, D = cv.shape
    tn = wg.shape[2]
    N = N_CHIPS * tn

    def body(cv_ref, w_ref, b_ref, mod_ref, sa_ref):
        sa = _silu(cv_ref[...]).astype(BF16)
        sa_ref[...] = sa
        mod_ref[...] = _nn(sa, w_ref[...]) + b_ref[...]

    return pl.pallas_call(
        body, name=name, grid=(N_CHIPS,),
        in_specs=[pl.BlockSpec((R, D), lambda j: (0, 0)), pl.BlockSpec((None, D, tn), lambda j: (j, 0, 0)),
                  pl.BlockSpec((1, tn), lambda j: (0, j))],
        out_specs=[pl.BlockSpec((R, tn), lambda j: (0, j)), pl.BlockSpec((R, D), lambda j: (0, 0))],
        out_shape=[jax.ShapeDtypeStruct((R, N), F32), jax.ShapeDtypeStruct((R, D), BF16)],
        compiler_params=_params("arbitrary"),
    )(cv, wg, b)


def _cvec_bwd(dmod, wg, cv, name):
    R, N = dmod.shape
    D = wg.shape[1]
    tk = wg.shape[2]
    nk = N_CHIPS

    def body(dm_ref, w_ref, cv_ref, o_ref):
        k = pl.program_id(0)

        @pl.when(k == 0)
        def _():
            o_ref[...] = jnp.zeros_like(o_ref)

        o_ref[...] += _nt(dm_ref[...].astype(BF16), w_ref[...])

        @pl.when(k == nk - 1)
        def _():
            o_ref[...] = o_ref[...] * _dsilu(cv_ref[...])

    return pl.pallas_call(
        body, name=name, grid=(nk,),
        in_specs=[pl.BlockSpec((R, tk), lambda k: (0, k)), pl.BlockSpec((None, D, tk), lambda k: (k, 0, 0)),
                  pl.BlockSpec((R, D), lambda k: (0, 0))],
        out_specs=pl.BlockSpec((R, D), lambda k: (0, 0)),
        out_shape=jax.ShapeDtypeStruct((R, D), F32),
        compiler_params=_params("arbitrary"),
    )(dmod, wg, cv)


def _norm_mod(x, nw, mod, which, ctx_rows, name):
    T, D = x.shape
    cb = ctx_rows // TB

    def body(x_ref, nw_ref, mod_ref, h_ref):
        xv = x_ref[...]
        r = lax.rsqrt(jnp.mean(xv * xv, axis=-1, keepdims=True) + RMS_EPS)
        y = xv * r * nw_ref[...]
        sh = mod_ref[which:which + 1, :]
        sc = mod_ref[which + 1:which + 2, :]
        h_ref[...] = (y * (1.0 + sc) + sh).astype(BF16)

    return pl.pallas_call(
        body, name=name, grid=(T // TB,),
        in_specs=[pl.BlockSpec((TB, D), lambda i: (i, 0)), pl.BlockSpec((1, D), lambda i: (0, 0)),
                  pl.BlockSpec((None, N_MOD, D), lambda i: (_stream_of(i, cb), 0, 0))],
        out_specs=pl.BlockSpec((TB, D), lambda i: (i, 0)),
        out_shape=jax.ShapeDtypeStruct((T, D), BF16),
        compiler_params=_params("parallel"),
    )(x, nw, mod)


def _norm_mod_bwd(dh, x, dres, nw, mod, which, ctx_rows, name):
    T, D = x.shape
    cb = ctx_rows // TB

    def body(dh_ref, x_ref, dres_ref, nw_ref, mod_ref, dx_ref, dm_ref, dnw_ref):
        i = pl.program_id(0)

        @pl.when(i == 0)
        def _():
            dnw_ref[...] = jnp.zeros_like(dnw_ref)

        @pl.when((i == 0) | (i == cb))
        def _():
            dm_ref[...] = jnp.zeros_like(dm_ref)

        xv = x_ref[...]
        dh = dh_ref[...]
        r = lax.rsqrt(jnp.mean(xv * xv, axis=-1, keepdims=True) + RMS_EPS)
        xh = xv * r
        nwv = nw_ref[...]
        sc = mod_ref[which + 1:which + 2, :]
        y = xh * nwv
        dm_ref[0:1, :] += jnp.sum(dh, axis=0, keepdims=True)
        dm_ref[1:2, :] += jnp.sum(dh * y, axis=0, keepdims=True)
        dy = dh * (1.0 + sc)
        dnw_ref[...] += jnp.sum(dy * xh, axis=0, keepdims=True)
        dxh = dy * nwv
        dx_ref[...] = dres_ref[...] + r * (dxh - xh * jnp.mean(dxh * xh, axis=-1, keepdims=True))

    return pl.pallas_call(
        body, name=name, grid=(T // TB,),
        in_specs=[pl.BlockSpec((TB, D), lambda i: (i, 0)), pl.BlockSpec((TB, D), lambda i: (i, 0)),
                  pl.BlockSpec((TB, D), lambda i: (i, 0)), pl.BlockSpec((1, D), lambda i: (0, 0)),
                  pl.BlockSpec((None, N_MOD, D), lambda i: (_stream_of(i, cb), 0, 0))],
        out_specs=[pl.BlockSpec((TB, D), lambda i: (i, 0)),
                   pl.BlockSpec((None, 2, D), lambda i: (_stream_of(i, cb), 0, 0)),
                   pl.BlockSpec((1, D), lambda i: (0, 0))],
        out_shape=[jax.ShapeDtypeStruct((T, D), F32), jax.ShapeDtypeStruct((2, 2, D), F32),
                   jax.ShapeDtypeStruct((1, D), F32)],
        compiler_params=_params("arbitrary"),
    )(dh, x, dres, nw, mod)


def _scan_chunk(n, rev, n_ctx, n_all):
    if not rev:
        return n
    return jnp.where(n < n_ctx, n_ctx - 1 - n, n_all - 1 + n_ctx - n)


def _cumsum_rows(x, rev):
    rows = x.shape[0]
    row = lax.broadcasted_iota(jnp.int32, (rows, 1), 0)
    s = 1
    while s < rows:
        if not rev:
            x = x + jnp.where(row >= s, pltpu.roll(x, s, 0), 0.0)
        else:
            x = x + jnp.where(row < rows - s, pltpu.roll(x, rows - s, 0), 0.0)
        s *= 2
    return x


def _lower_bound(hlb_ref, layer):
    h = hlb_ref[...]
    if layer == 0:
        return jnp.zeros_like(h[0:1, :])
    return _sigmoid(h[1:2, :] - h[0:1, :])


HG_STEP = 4


def _step_rows(j, rev, backward):
    sub = j if rev == backward else HG_STEP - 1 - j
    return slice(sub * HG_CHUNK, (sub + 1) * HG_CHUNK)


def _hgrn_gates(q_ref, f_ref, hlb_ref, layer, rev, rows):
    lb = _lower_bound(hlb_ref, layer)
    z = f_ref[rows, :]
    sig = 1.0 / (1.0 + jnp.exp(-z))
    fg = lb + (1.0 - lb) * sig
    kk = (1.0 - lb) * (1.0 - sig)
    g = jnp.log(fg)
    b = _cumsum_rows(g, rev)
    bt = jnp.sum(g, axis=0, keepdims=True)
    mid = HG_CHUNK // 2
    r = b[mid:mid + 1, :] if rev else b[mid - 1:mid, :]
    qh = _silu(q_ref[rows, :])
    return lb, sig, fg, kk, b, bt, r, qh


def _tri_mask(rev):
    t = lax.broadcasted_iota(jnp.int32, (HG_CHUNK, HG_CHUNK), 0)
    s = lax.broadcasted_iota(jnp.int32, (HG_CHUNK, HG_CHUNK), 1)
    return (s >= t) if rev else (s <= t)


def _hgrn_fwd(parts, hlb, layer, rev, ctx_rows, name, o_add=None):
    T = parts.shape[0]
    D = hlb.shape[1] // 2
    nh = D // HEAD
    n_all, n_ctx = T // HG_CHUNK, ctx_rows // HG_CHUNK
    assert n_all % HG_STEP == 0 and n_ctx % HG_STEP == 0
    n_steps = n_all // HG_STEP
    block = functools.partial(_scan_chunk, rev=rev, n_ctx=n_ctx // HG_STEP, n_all=n_steps)
    fcol = 2 if rev else 1

    def body(q_ref, f_ref, i_ref, hlb_ref, *rest):
        if o_add is None:
            o_ref, st_ref, s_scr = rest
        else:
            oa_ref, o_ref, st_ref, s_scr = rest
        n = pl.program_id(0)

        @pl.when(n == 0)
        def _():
            s_scr[...] = jnp.zeros_like(s_scr)

        mask = _tri_mask(rev)
        hs = [slice(h * HEAD, (h + 1) * HEAD) for h in range(nh)]
        for j in range(HG_STEP):
            rows = _step_rows(j, rev, False)
            lb, sig, fg, kk, b, bt, r, qh = _hgrn_gates(q_ref, f_ref, hlb_ref, layer, rev, rows)
            qr = (qh * jnp.exp(b - r)).astype(BF16)
            kr = (kk * jnp.exp(r - b)).astype(BF16)
            qe = (qh * jnp.exp(b)).astype(BF16)
            ke = (kk * jnp.exp(bt - b)).astype(BF16)
            dec = jnp.exp(bt)
            v = i_ref[rows, :].astype(BF16)
            st = [s_scr[h] for h in range(nh)]
            a_raw = [_nt(qr[:, sl], kr[:, sl]) for sl in hs]
            o_int = [_nt(qe[:, sl], st[h].astype(BF16)) for h, sl in enumerate(hs)]
            kv = [_tn(v[:, sl], ke[:, sl]) for sl in hs]
            for h, sl in enumerate(hs):
                st_ref[j, h] = st[h]
                o = _nn(jnp.where(mask, a_raw[h], 0.0).astype(BF16), v[:, sl]) + o_int[h]
                if o_add is not None:
                    o = o + oa_ref[rows, sl]
                o_ref[rows, sl] = o
                s_scr[h] = st[h] * dec[:, sl] + kv[h]

    cspec = lambda col: pl.BlockSpec((HG_STEP * HG_CHUNK, D), lambda n: (block(n), col))
    ins = [parts, parts, parts, hlb]
    specs = [cspec(0), cspec(fcol), cspec(3), pl.BlockSpec((2, D), lambda n: (0, 1 if rev else 0))]
    if o_add is not None:
        ins.append(o_add)
        specs.append(cspec(0))
    return pl.pallas_call(
        body, name=name, grid=(n_steps,), in_specs=specs,
        out_specs=[cspec(0), pl.BlockSpec((HG_STEP, nh, HEAD, HEAD), lambda n: (n, 0, 0, 0))],
        out_shape=[jax.ShapeDtypeStruct((T, D), F32), jax.ShapeDtypeStruct((n_all, nh, HEAD, HEAD), F32)],
        scratch_shapes=[pltpu.VMEM((nh, HEAD, HEAD), F32)],
        compiler_params=_params("arbitrary"),
    )(*ins)


def _hgrn_bwd(parts, hlb, do, states, layer, rev, ctx_rows, name, other=None, dparts=None):
    T = parts.shape[0]
    D = hlb.shape[1] // 2
    nh = D // HEAD
    n_all, n_ctx = T // HG_CHUNK, ctx_rows // HG_CHUNK
    assert n_all % HG_STEP == 0 and n_ctx % HG_STEP == 0
    n_steps = n_all // HG_STEP
    step = lambda m: n_steps - 1 - m
    block = lambda m: _scan_chunk(step(m), rev, n_ctx // HG_STEP, n_steps)
    fcol = 2 if rev else 1
    has_add = other is not None
    assert not has_add or rev

    def body(q_ref, f_ref, i_ref, hlb_ref, do_ref, st_ref, *rest):
        if has_add:
            dqa_ref, dza_ref, dia_ref, _, out_ref, dlb_ref, ds_scr = rest
            dq_ref, dz_ref, di_ref = out_ref.at[:, 0:D], out_ref.at[:, 2 * D:3 * D], out_ref.at[:, 3 * D:4 * D]
            out_ref[:, D:2 * D] = dza_ref[...]
        else:
            dq_ref, dz_ref, di_ref, dlb_ref, ds_scr = rest
        m = pl.program_id(0)

        @pl.when(m == 0)
        def _():
            ds_scr[...] = jnp.zeros_like(ds_scr)
            dlb_ref[...] = jnp.zeros_like(dlb_ref)

        mask = _tri_mask(rev)
        hs = [slice(h * HEAD, (h + 1) * HEAD) for h in range(nh)]
        for j in range(HG_STEP):
            rows = _step_rows(j, rev, True)
            slot = HG_STEP - 1 - j
            lb, sig, fg, kk, b, bt, r, qh = _hgrn_gates(q_ref, f_ref, hlb_ref, layer, rev, rows)
            e_qr = jnp.exp(b - r)
            e_kr = jnp.exp(r - b)
            e_b = jnp.exp(b)
            e_ke = jnp.exp(bt - b)
            dec = jnp.exp(bt)
            qr = (qh * e_qr).astype(BF16)
            kr = (kk * e_kr).astype(BF16)
            qe = (qh * e_b).astype(BF16)
            ke = (kk * e_ke).astype(BF16)
            v = i_ref[rows, :].astype(BF16)
            dov = do_ref[rows, :].astype(BF16)
            st = [st_ref[slot, h] for h in range(nh)]
            dst = [ds_scr[h] for h in range(nh)]
            stb = [t.astype(BF16) for t in st]
            dstb = [t.astype(BF16) for t in dst]
            a_raw = [_nt(qr[:, sl], kr[:, sl]) for sl in hs]
            da_raw = [_nt(dov[:, sl], v[:, sl]) for sl in hs]
            dq_int = [_nn(dov[:, sl], stb[h]) for h, sl in enumerate(hs)]
            dk_int = [_nn(v[:, sl], dstb[h]) for h, sl in enumerate(hs)]
            dv_int = [_nt(ke[:, sl], dstb[h]) for h, sl in enumerate(hs)]
            ds_new = [_tn(dov[:, sl], qe[:, sl]) for sl in hs]
            a = [jnp.where(mask, t, 0.0).astype(BF16) for t in a_raw]
            da = [jnp.where(mask, t, 0.0).astype(BF16) for t in da_raw]
            dv_parts = [_tn(a[h], dov[:, sl]) + dv_int[h] for h, sl in enumerate(hs)]
            dq_parts = [_nn(da[h], kr[:, sl]) * e_qr[:, sl] + dq_int[h] * e_b[:, sl] for h, sl in enumerate(hs)]
            dki_parts = [dk_int[h] * e_ke[:, sl] for h, sl in enumerate(hs)]
            dk_parts = [_tn(da[h], qr[:, sl]) * e_kr[:, sl] + dki_parts[h] for h, sl in enumerate(hs)]
            dbt_parts = [dec[:, sl] * jnp.sum(st[h] * dst[h], axis=0, keepdims=True) for h, sl in enumerate(hs)]
            for h, sl in enumerate(hs):
                ds_scr[h] = dst[h] * dec[:, sl] + ds_new[h]
            dq = jnp.concatenate(dq_parts, axis=1)
            dk = jnp.concatenate(dk_parts, axis=1)
            dki = jnp.concatenate(dki_parts, axis=1)
            dv = jnp.concatenate(dv_parts, axis=1)
            dbt = jnp.concatenate(dbt_parts, axis=1) + jnp.sum(kk * dki, axis=0, keepdims=True)
            db = qh * dq - kk * dk
            dg = _cumsum_rows(db, not rev) + dbt
            df = dg / fg - dk
            dz_ref[rows, :] = (df * (1.0 - lb) * sig * (1.0 - sig)).astype(BF16)
            dlb_ref[...] += jnp.sum(df * (1.0 - sig), axis=0, keepdims=True)
            dqr = dq * _dsilu(q_ref[rows, :])
            if has_add:
                dqr = dqr + dqa_ref[rows, :]
                dv = dv + dia_ref[rows, :]
            dq_ref[rows, :] = dqr.astype(dq_ref.dtype)
            di_ref[rows, :] = dv.astype(di_ref.dtype)

        @pl.when(m == n_steps - 1)
        def _():
            lb = _lower_bound(hlb_ref, layer)
            if layer == 0:
                dlb_ref[...] = jnp.zeros_like(dlb_ref)
            else:
                dlb_ref[...] = dlb_ref[...] * lb * (1.0 - lb)

    cspec = lambda col: pl.BlockSpec((HG_STEP * HG_CHUNK, D), lambda m: (block(m), col))
    ins = [parts, parts, parts, hlb, do, states]
    specs = [cspec(0), cspec(fcol), cspec(3), pl.BlockSpec((2, D), lambda m: (0, 1 if rev else 0)), cspec(0),
             pl.BlockSpec((HG_STEP, nh, HEAD, HEAD), lambda m: (step(m), 0, 0, 0))]
    dlb_spec = pl.BlockSpec((1, D), lambda m: (0, 0))
    dlb_shape = jax.ShapeDtypeStruct((1, D), F32)
    if has_add:
        return pl.pallas_call(
            body, name=name, grid=(n_steps,),
            in_specs=specs + [cspec(0), cspec(0), cspec(0), pl.BlockSpec(memory_space=pl.ANY)],
            out_specs=[pl.BlockSpec((HG_STEP * HG_CHUNK, 4 * D), lambda m: (block(m), 0)), dlb_spec],
            out_shape=[jax.ShapeDtypeStruct(dparts.shape, dparts.dtype), dlb_shape],
            scratch_shapes=[pltpu.VMEM((nh, HEAD, HEAD), F32)], input_output_aliases={len(ins) + 3: 0},
            compiler_params=_params("arbitrary"),
        )(*ins, *other, dparts)
    return pl.pallas_call(
        body, name=name, grid=(n_steps,), in_specs=specs,
        out_specs=[cspec(0), cspec(0), cspec(0), dlb_spec],
        out_shape=[jax.ShapeDtypeStruct((T, D), F32), jax.ShapeDtypeStruct((T, D), BF16),
                   jax.ShapeDtypeStruct((T, D), F32), dlb_shape],
        scratch_shapes=[pltpu.VMEM((nh, HEAD, HEAD), F32)],
        compiler_params=_params("arbitrary"),
    )(*ins)


def _sgu_ln(gv, lnw_ref, lnb_ref):
    mu = jnp.mean(gv, axis=-1, keepdims=True)
    xc = gv - mu
    rstd = lax.rsqrt(jnp.mean(xc * xc, axis=-1, keepdims=True) + LN_EPS)
    xh = xc * rstd
    return xh, rstd, xh * lnw_ref[...] + lnb_ref[...]


def _sgu_fwd(parts, lnw, lnb, w, bt, name):
    T = parts.shape[0]
    D = lnw.shape[1]
    G = D // HEAD

    def body(u_ref, v_ref, lnw_ref, lnb_ref, w_ref, bt_ref, ya_ref):
        gu = _gelu(u_ref[...])
        _, _, vn = _sgu_ln(_gelu(v_ref[...]), lnw_ref, lnb_ref)
        vnb = vn.astype(BF16)
        for g in range(G):
            sl = slice(g * HEAD, (g + 1) * HEAD)
            mixed = _nn(w_ref[g], vnb[:, sl]) + bt_ref[:, g:g + 1]
            ya_ref[:, sl] = (gu[:, sl] * mixed).astype(BF16)

    return pl.pallas_call(
        body, name=name, grid=(T // SGU_CHUNK,),
        in_specs=[pl.BlockSpec((SGU_CHUNK, D), lambda n: (n, 4)), pl.BlockSpec((SGU_CHUNK, D), lambda n: (n, 5)),
                  pl.BlockSpec((1, D), lambda n: (0, 0)), pl.BlockSpec((1, D), lambda n: (0, 0)),
                  pl.BlockSpec((G, SGU_CHUNK, SGU_CHUNK), lambda n: (0, 0, 0)),
                  pl.BlockSpec((SGU_CHUNK, G), lambda n: (0, 0))],
        out_specs=pl.BlockSpec((SGU_CHUNK, D), lambda n: (n, 0)),
        out_shape=jax.ShapeDtypeStruct((T, D), BF16),
        compiler_params=_params("parallel"),
    )(parts, parts, lnw, lnb, w, bt)


def _sgu_bwd(parts, dya, lnw, lnb, w, bt, dparts, name):
    T = parts.shape[0]
    D = lnw.shape[1]
    G = D // HEAD

    def body(u_ref, v_ref, dya_ref, lnw_ref, lnb_ref, w_ref, bt_ref, dparts_in,
             duv_ref, dw_ref, dbt_ref, dlnw_ref, dlnb_ref, dvn_scr):
        du_ref = duv_ref.at[:, 0:D]
        dv_ref = duv_ref.at[:, D:2 * D]
        n = pl.program_id(0)

        @pl.when(n == 0)
        def _():
            dw_ref[...] = jnp.zeros_like(dw_ref)
            dbt_ref[...] = jnp.zeros_like(dbt_ref)
            dlnw_ref[...] = jnp.zeros_like(dlnw_ref)
            dlnb_ref[...] = jnp.zeros_like(dlnb_ref)

        gu, dgu = _gelu_both(u_ref[...])
        gv, dgv_dv = _gelu_both(v_ref[...])
        xh, rstd, vn = _sgu_ln(gv, lnw_ref, lnb_ref)
        vnb = vn.astype(BF16)
        dya = dya_ref[...]
        lane = lax.broadcasted_iota(jnp.int32, (SGU_CHUNK, G), 1)
        dbt = jnp.zeros((SGU_CHUNK, G), F32)
        for g in range(G):
            sl = slice(g * HEAD, (g + 1) * HEAD)
            wg = w_ref[g]
            mixed = _nn(wg, vnb[:, sl]) + bt_ref[:, g:g + 1]
            dmix = dya[:, sl] * gu[:, sl]
            du_ref[:, sl] = (dya[:, sl] * mixed * dgu[:, sl]).astype(BF16)
            dmb = dmix.astype(BF16)
            dvn_scr[:, sl] = _tn(wg, dmb)
            dw_ref[g] += _nt(dmb, vnb[:, sl])
            dbt = dbt + jnp.where(lane == g, jnp.sum(dmix, axis=1, keepdims=True), 0.0)
        dbt_ref[...] += dbt
        dvn = dvn_scr[...]
        dlnw_ref[...] += jnp.sum(dvn * xh, axis=0, keepdims=True)
        dlnb_ref[...] += jnp.sum(dvn, axis=0, keepdims=True)
        dxh = dvn * lnw_ref[...]
        dgv = rstd * (dxh - jnp.mean(dxh, axis=-1, keepdims=True) - xh * jnp.mean(dxh * xh, axis=-1, keepdims=True))
        dv_ref[...] = (dgv * dgv_dv).astype(BF16)

    row = lambda col: pl.BlockSpec((SGU_CHUNK, D), lambda n: (n, col))
    vec = pl.BlockSpec((1, D), lambda n: (0, 0))
    wsp = pl.BlockSpec((G, SGU_CHUNK, SGU_CHUNK), lambda n: (0, 0, 0))
    bsp = pl.BlockSpec((SGU_CHUNK, G), lambda n: (0, 0))
    return pl.pallas_call(
        body, name=name, grid=(T // SGU_CHUNK,),
        in_specs=[row(4), row(5), row(0), vec, vec, wsp, bsp, pl.BlockSpec(memory_space=pl.ANY)],
        out_specs=[pl.BlockSpec((SGU_CHUNK, 2 * D), lambda n: (n, 2)), wsp, bsp, vec, vec],
        out_shape=[jax.ShapeDtypeStruct(dparts.shape, dparts.dtype),
                   jax.ShapeDtypeStruct((G, SGU_CHUNK, SGU_CHUNK), F32), jax.ShapeDtypeStruct((SGU_CHUNK, G), F32),
                   jax.ShapeDtypeStruct((1, D), F32), jax.ShapeDtypeStruct((1, D), F32)],
        scratch_shapes=[pltpu.VMEM((SGU_CHUNK, D), F32)], input_output_aliases={7: 0},
        compiler_params=_params("arbitrary"),
    )(parts, parts, dya, lnw, lnb, w, bt, dparts)


TBT = 256
VMEM_LIMIT_TOKEN_OUT = 58 * 1024 * 1024


def _rows_weight_spec(wg):
    return pl.BlockSpec(wg.shape, lambda i: (0, 0, 0))


def _full(w_ref):
    return w_ref[...].reshape(w_ref.shape[0] * w_ref.shape[1], w_ref.shape[2])


def _token_out_fwd(o, parts, ya, x, mod, hnw, nw2, wa, wb, wo, ctx_rows, name):
    T, D = x.shape
    nh = D // HEAD
    cb = ctx_rows // TBT

    def body(o_ref, og_ref, ga_ref, gb_ref, ya_ref, x_ref, mod_ref, hnw_ref, nw2_ref, wa_ref, wb_ref, wo_ref,
             yb_ref, pa_ref, pb_ref, mg_ref, tmo_ref, xm_ref, h2_ref):
        ov = o_ref[...]
        so = _silu(og_ref[...])
        nw = hnw_ref[...]
        for h in range(nh):
            sl = slice(h * HEAD, (h + 1) * HEAD)
            seg = ov[:, sl]
            r = lax.rsqrt(jnp.mean(seg * seg, axis=-1, keepdims=True) + RMS_EPS)
            yb_ref[:, sl] = (seg * r * nw * so[:, sl]).astype(BF16)
        pa = _nn(ya_ref[...], _full(wa_ref))
        pb = _nn(yb_ref[...], _full(wb_ref))
        pa_ref[...] = pa.astype(BF16)
        pb_ref[...] = pb.astype(BF16)
        mg = (_sigmoid(ga_ref[...]) * pa + _sigmoid(gb_ref[...]) * pb).astype(BF16)
        mg_ref[...] = mg
        out = _nn(mg, _full(wo_ref))
        tmo_ref[...] = out.astype(BF16)
        xm = x_ref[...] + mod_ref[2:3, :] * out
        xm_ref[...] = xm
        r = lax.rsqrt(jnp.mean(xm * xm, axis=-1, keepdims=True) + RMS_EPS)
        h2_ref[...] = (xm * r * nw2_ref[...] * (1.0 + mod_ref[4:5, :]) + mod_ref[3:4, :]).astype(BF16)

    row = lambda col: pl.BlockSpec((TBT, D), lambda i: (i, col))
    wsp = _rows_weight_spec(wa)
    sd = lambda dt: jax.ShapeDtypeStruct((T, D), dt)
    return pl.pallas_call(
        body, name=name, grid=(T // TBT,),
        in_specs=[row(0), row(6), row(7), row(8), row(0), row(0),
                  pl.BlockSpec((None, N_MOD, D), lambda i: (_stream_of(i, cb), 0, 0)),
                  pl.BlockSpec((1, HEAD), lambda i: (0, 0)), pl.BlockSpec((1, D), lambda i: (0, 0)), wsp, wsp, wsp],
        out_specs=[row(0)] * 7,
        out_shape=[sd(BF16), sd(BF16), sd(BF16), sd(BF16), sd(BF16), sd(F32), sd(BF16)],
        compiler_params=_params("parallel", vmem=VMEM_LIMIT_TOKEN_OUT),
    )(o, parts, parts, parts, ya, x, mod, hnw, nw2, wa, wb, wo)


def _token_out_bwd(dx, tmo, pa, pb, o, parts, mod, hnw, wa, wb, wo, ctx_rows, name):
    T, D = dx.shape
    nh = D // HEAD
    cb = ctx_rows // TBT

    def body(dx_ref, tmo_ref, pa_ref, pb_ref, o_ref, og_ref, ga_ref, gb_ref, mod_ref, hnw_ref, wa_ref, wb_ref, wo_ref,
             dout_ref, dpa_ref, dpb_ref, dgate_ref, dya_ref, do_ref, dg1_ref, dhnw_ref):
        i = pl.program_id(0)

        @pl.when(i == 0)
        def _():
            dhnw_ref[...] = jnp.zeros_like(dhnw_ref)

        @pl.when((i == 0) | (i == cb))
        def _():
            dg1_ref[...] = jnp.zeros_like(dg1_ref)

        dxv = dx_ref[...]
        dg1_ref[...] += jnp.sum(dxv * tmo_ref[...], axis=0, keepdims=True)
        dout = (dxv * mod_ref[2:3, :]).astype(BF16)
        dout_ref[...] = dout
        dmg = _nt(dout, _full(wo_ref))
        sa = _sigmoid(ga_ref[...])
        sb = _sigmoid(gb_ref[...])
        dpa = (dmg * sa).astype(BF16)
        dpb = (dmg * sb).astype(BF16)
        dpa_ref[...] = dpa
        dpb_ref[...] = dpb
        dgate_ref[:, D:2 * D] = (dmg * pa_ref[...] * sa * (1.0 - sa)).astype(BF16)
        dgate_ref[:, 2 * D:3 * D] = (dmg * pb_ref[...] * sb * (1.0 - sb)).astype(BF16)
        dya_ref[...] = _nt(dpa, _full(wa_ref))
        dyb = _nt(dpb, _full(wb_ref))
        so, dso = _silu_both(og_ref[...])
        ov = o_ref[...]
        nw = hnw_ref[...]
        dnw = jnp.zeros((1, HEAD), F32)
        for h in range(nh):
            sl = slice(h * HEAD, (h + 1) * HEAD)
            seg = ov[:, sl]
            r = lax.rsqrt(jnp.mean(seg * seg, axis=-1, keepdims=True) + RMS_EPS)
            oh = seg * r
            dn = dyb[:, sl] * so[:, sl]
            dgate_ref[:, sl] = (dyb[:, sl] * oh * nw * dso[:, sl]).astype(BF16)
            dnw = dnw + jnp.sum(dn * oh, axis=0, keepdims=True)
            doh = dn * nw
            do_ref[:, sl] = (r * (doh - oh * jnp.mean(doh * oh, axis=-1, keepdims=True))).astype(BF16)
        dhnw_ref[...] += dnw

    row = lambda col: pl.BlockSpec((TBT, D), lambda i: (i, col))
    wsp = _rows_weight_spec(wa)
    sd = lambda dt: jax.ShapeDtypeStruct((T, D), dt)
    return pl.pallas_call(
        body, name=name, grid=(T // TBT,),
        in_specs=[row(0), row(0), row(0), row(0), row(0), row(6), row(7), row(8),
                  pl.BlockSpec((None, N_MOD, D), lambda i: (_stream_of(i, cb), 0, 0)),
                  pl.BlockSpec((1, HEAD), lambda i: (0, 0)), wsp, wsp, wsp],
        out_specs=[row(0)] * 3 + [pl.BlockSpec((TBT, 3 * D), lambda i: (i, 2)), row(0), row(0),
                                  pl.BlockSpec((None, 1, D), lambda i: (_stream_of(i, cb), 0, 0)),
                                  pl.BlockSpec((1, HEAD), lambda i: (0, 0))],
        out_shape=[sd(BF16)] * 3 + [jax.ShapeDtypeStruct((T, 9 * D), BF16), sd(F32), sd(BF16),
                                    jax.ShapeDtypeStruct((2, 1, D), F32), jax.ShapeDtypeStruct((1, HEAD), F32)],
        compiler_params=_params("arbitrary", vmem=VMEM_LIMIT_TOKEN_OUT),
    )(dx, tmo, pa, pb, o, parts, parts, parts, mod, hnw, wa, wb, wo)


def _conv_geometry(i, nb, cb):
    is_ctx = i < cb
    first = (i == 0) | (i == cb)
    last = (i == cb - 1) | (i == nb - 1)
    row = lax.broadcasted_iota(jnp.int32, (TB + 2 * GRID_W, 1), 0)
    w = row & (GRID_W - 1)
    left_ok = (w != 0) | is_ctx
    right_ok = (w != GRID_W - 1) | is_ctx
    return is_ctx, first, last, left_ok, right_ok


def _ext(p_ref, m_ref, n_ref, first, last):
    return jnp.concatenate([jnp.where(first, 0.0, p_ref[...]), m_ref[...], jnp.where(last, 0.0, n_ref[...])], axis=0)


def _shift_prev(e, ok):
    return jnp.where(ok, pltpu.roll(e, 1, 0), 0.0)


def _shift_next(e, ok):
    return jnp.where(ok, pltpu.roll(e, e.shape[0] - 1, 0), 0.0)


def _halo_specs(cbk, n64, coff=0):
    r = TB // GRID_W
    prev = pl.BlockSpec((GRID_W, cbk), lambda j, i: (jnp.maximum(r * i - 1, 0), j + coff))
    main = pl.BlockSpec((TB, cbk), lambda j, i: (i, j + coff))
    nxt = pl.BlockSpec((GRID_W, cbk), lambda j, i: (jnp.minimum(r * i + r, n64 - 1), j + coff))
    return [prev, main, nxt]


def _conv_cblock(dff):
    return _tile(dff, 1408)


def _conv_fwd(up, cw, cbias, ctx_rows, name):
    T, dff = up.shape[0], up.shape[1] // 2
    cbk = _conv_cblock(dff)
    nb, cb = T // TB, ctx_rows // TB
    nvb = dff // cbk

    def body(ap_ref, a_ref, an_ref, v_ref, cw_ref, cb_ref, ac_ref, act_ref):
        i = pl.program_id(1)
        is_ctx, first, last, lok, rok = _conv_geometry(i, nb, cb)
        e = _ext(ap_ref, a_ref, an_ref, first, last)
        el = _shift_prev(e, lok)
        er = _shift_next(e, rok)
        cwv = cw_ref[...]

        def comb(dr, lo):
            sl = slice(lo, lo + TB)
            return cwv[3 * dr:3 * dr + 1] * el[sl] + cwv[3 * dr + 1:3 * dr + 2] * e[sl] + cwv[3 * dr + 2:3 * dr + 3] * er[sl]

        out = comb(1, GRID_W) + jnp.where(is_ctx, 0.0, comb(0, 0) + comb(2, 2 * GRID_W))
        a_c = out + cb_ref[...]
        ac_ref[...] = a_c
        act_ref[...] = (_gelu(a_c) * v_ref[...]).astype(BF16)

    main = pl.BlockSpec((TB, cbk), lambda j, i: (i, j))
    return pl.pallas_call(
        body, name=name, grid=(dff // cbk, nb),
        in_specs=_halo_specs(cbk, T // GRID_W) + [pl.BlockSpec((TB, cbk), lambda j, i: (i, j + nvb)),
                                                 pl.BlockSpec((9, cbk), lambda j, i: (0, j)),
                                                 pl.BlockSpec((1, cbk), lambda j, i: (0, j))],
        out_specs=[main, main],
        out_shape=[jax.ShapeDtypeStruct((T, dff), F32), jax.ShapeDtypeStruct((T, dff), BF16)],
        compiler_params=_params("parallel", "parallel"),
    )(up, up, up, up, cw, cbias)


def _conv_bwd(up, ac, dact, cw, ctx_rows, name):
    T, dff = up.shape[0], up.shape[1] // 2
    cbk = _conv_cblock(dff)
    nb, cb = T // TB, ctx_rows // TB
    nvb = dff // cbk

    def body(ap_ref, a_ref, an_ref, vp_ref, v_ref, vn_ref, cp_ref, c_ref, cn_ref, dp_ref, d_ref, dn_ref, cw_ref,
             da_ref, dv_ref, dcw_ref, dcb_ref):
        i = pl.program_id(1)

        @pl.when(i == 0)
        def _():
            dcw_ref[...] = jnp.zeros_like(dcw_ref)
            dcb_ref[...] = jnp.zeros_like(dcb_ref)

        is_ctx, first, last, lok, rok = _conv_geometry(i, nb, cb)
        gl, dgl = _gelu_both(_ext(cp_ref, c_ref, cn_ref, first, last))
        g = _ext(dp_ref, d_ref, dn_ref, first, last) * _ext(vp_ref, v_ref, vn_ref, first, last) * dgl
        dv_ref[...] = (d_ref[...] * gl[GRID_W:GRID_W + TB]).astype(BF16)
        gm = _shift_prev(g, lok)
        gp = _shift_next(g, rok)
        cwv = cw_ref[...]

        def comb(dr, lo):
            sl = slice(lo, lo + TB)
            return cwv[3 * dr:3 * dr + 1] * gp[sl] + cwv[3 * dr + 1:3 * dr + 2] * g[sl] + cwv[3 * dr + 2:3 * dr + 3] * gm[sl]

        da = comb(1, GRID_W) + jnp.where(is_ctx, 0.0, comb(0, 2 * GRID_W) + comb(2, 0))
        da_ref[...] = da.astype(BF16)
        e = _ext(ap_ref, a_ref, an_ref, first, last)
        taps = [_shift_prev(e, lok), e, _shift_next(e, rok)]
        gmain = g[GRID_W:GRID_W + TB]
        dcb_ref[...] += jnp.sum(gmain, axis=0, keepdims=True)
        vert = jnp.where(is_ctx, 0.0, 1.0)
        for dr in range(3):
            sl = slice(dr * GRID_W, dr * GRID_W + TB)
            for dw in range(3):
                s = jnp.sum(gmain * taps[dw][sl], axis=0, keepdims=True)
                if dr != 1:
                    s = s * vert
                k = 3 * dr + dw
                dcw_ref[k:k + 1, :] += s

    main = pl.BlockSpec((TB, cbk), lambda j, i: (i, j))
    halo = _halo_specs(cbk, T // GRID_W)
    acc9 = pl.BlockSpec((9, cbk), lambda j, i: (0, j))
    acc1 = pl.BlockSpec((1, cbk), lambda j, i: (0, j))
    return pl.pallas_call(
        body, name=name, grid=(dff // cbk, nb),
        in_specs=halo + _halo_specs(cbk, T // GRID_W, nvb) + halo + halo + [acc9],
        out_specs=[main, main, acc9, acc1],
        out_shape=[jax.ShapeDtypeStruct((T, dff), BF16), jax.ShapeDtypeStruct((T, dff), BF16),
                   jax.ShapeDtypeStruct((9, dff), F32), jax.ShapeDtypeStruct((1, dff), F32)],
        compiler_params=_params("parallel", "arbitrary"),
    )(up, up, up, up, up, up, ac, ac, ac, dact, dact, dact, cw)


def _ffn_out_fwd(act, xm, mod, wd, ctx_rows, name):
    T, D = xm.shape
    dff = act.shape[1]
    cb = ctx_rows // TB

    def body(act_ref, x_ref, mod_ref, w_ref, xo_ref, fo_ref):
        out = _nn(act_ref[...], _full(w_ref))
        fo_ref[...] = out.astype(BF16)
        xo_ref[...] = x_ref[...] + mod_ref[5:6, :] * out

    row = pl.BlockSpec((TB, D), lambda i: (i, 0))
    return pl.pallas_call(
        body, name=name, grid=(T // TB,),
        in_specs=[pl.BlockSpec((TB, dff), lambda i: (i, 0)), row,
                  pl.BlockSpec((None, N_MOD, D), lambda i: (_stream_of(i, cb), 0, 0)),
                  _rows_weight_spec(wd)],
        out_specs=[row, row],
        out_shape=[jax.ShapeDtypeStruct((T, D), F32), jax.ShapeDtypeStruct((T, D), BF16)],
        compiler_params=_params("parallel"),
    )(act, xm, mod, wd)


def _ffn_out_bwd(dx, fo, mod, wd, ctx_rows, name):
    T, D = dx.shape
    dff = N_CHIPS * wd.shape[1]
    cb = ctx_rows // TB

    def body(dx_ref, fo_ref, mod_ref, w_ref, dout_ref, dact_ref, dg2_ref):
        i = pl.program_id(0)

        @pl.when((i == 0) | (i == cb))
        def _():
            dg2_ref[...] = jnp.zeros_like(dg2_ref)

        dxv = dx_ref[...]
        dg2_ref[...] += jnp.sum(dxv * fo_ref[...], axis=0, keepdims=True)
        dout = (dxv * mod_ref[5:6, :]).astype(BF16)
        dout_ref[...] = dout
        dact_ref[...] = _nt(dout, _full(w_ref))

    row = pl.BlockSpec((TB, D), lambda i: (i, 0))
    return pl.pallas_call(
        body, name=name, grid=(T // TB,),
        in_specs=[row, row, pl.BlockSpec((None, N_MOD, D), lambda i: (_stream_of(i, cb), 0, 0)),
                  _rows_weight_spec(wd)],
        out_specs=[row, pl.BlockSpec((TB, dff), lambda i: (i, 0)),
                   pl.BlockSpec((None, 1, D), lambda i: (_stream_of(i, cb), 0, 0))],
        out_shape=[jax.ShapeDtypeStruct((T, D), BF16), jax.ShapeDtypeStruct((T, dff), F32),
                   jax.ShapeDtypeStruct((2, 1, D), F32)],
        compiler_params=_params("arbitrary"),
    )(dx, fo, mod, wd)


def _loss_bwd(x, target, fw, ctx_rows, name):
    T, D = x.shape
    cb = ctx_rows // TB

    def body(x_ref, t_ref, fw_ref, dx_ref, loss_ref, dfw_ref):
        i = pl.program_id(0)

        @pl.when(i == 0)
        def _():
            loss_ref[...] = jnp.zeros_like(loss_ref)
            dfw_ref[...] = jnp.zeros_like(dfw_ref)

        @pl.when(i < cb)
        def _():
            dx_ref[...] = jnp.zeros_like(dx_ref)

        @pl.when(i >= cb)
        def _():
            xv = x_ref[...]
            r = lax.rsqrt(jnp.mean(xv * xv, axis=-1, keepdims=True) + RMS_EPS)
            xh = xv * r
            fwv = fw_ref[...]
            err = xh * fwv - t_ref[...]
            loss_ref[...] += (0.5 / D) * jnp.sum(err * err)
            dy = err * (1.0 / D)
            dfw_ref[...] += jnp.sum(dy * xh, axis=0, keepdims=True)
            dxh = dy * fwv
            dx_ref[...] = r * (dxh - xh * jnp.mean(dxh * xh, axis=-1, keepdims=True))

    row = pl.BlockSpec((TB, D), lambda i: (i, 0))
    return pl.pallas_call(
        body, name=name, grid=(T // TB,),
        in_specs=[row, pl.BlockSpec((TB, D), lambda i: (jnp.maximum(i - cb, 0), 0)), pl.BlockSpec((1, D), lambda i: (0, 0))],
        out_specs=[row, pl.BlockSpec((1, 128), lambda i: (0, 0)), pl.BlockSpec((1, D), lambda i: (0, 0))],
        out_shape=[jax.ShapeDtypeStruct((T, D), F32), jax.ShapeDtypeStruct((1, 128), F32),
                   jax.ShapeDtypeStruct((1, D), F32)],
        compiler_params=_params("arbitrary"),
    )(x, target, fw)


def _adamw(w, gs, m, v, name):
    L, R, C = w.shape
    assert len(gs) == L
    rb = _rows_tile(R, max(16, (1 << 18) // C // 16 * 16))
    bc1 = 1.0 - ADAM_B1 ** ADAM_STEP
    bc2 = 1.0 - ADAM_B2 ** ADAM_STEP

    def body(w_ref, m_ref, v_ref, *rest):
        g_refs, (g_ref, d_ref, nm_ref, nv_ref) = rest[:L], rest[L:]
        layer = pl.program_id(0)
        for li in range(L):
            @pl.when(layer == li)
            def _():
                gv = g_refs[li][...]
                g_ref[...] = gv
                nm = ADAM_B1 * m_ref[...] + (1.0 - ADAM_B1) * gv
                nv = ADAM_B2 * v_ref[...] + (1.0 - ADAM_B2) * (gv * gv)
                nm_ref[...] = nm
                nv_ref[...] = nv
                d_ref[...] = -ADAM_LR * ((nm / bc1) / (jnp.sqrt(nv / bc2) + ADAM_EPS) + ADAM_WD * w_ref[...])

    blk = pl.BlockSpec((None, rb, C), lambda l, i: (l, i, 0))
    gblk = pl.BlockSpec((rb, C), lambda l, i: (i, 0))
    sd = jax.ShapeDtypeStruct((L, R, C), F32)
    return pl.pallas_call(
        body, name=name, grid=(L, R // rb), in_specs=[blk] * 3 + [gblk] * L, out_specs=[blk] * 4, out_shape=[sd] * 4,
        compiler_params=_params("parallel", "parallel"),
    )(w, m, v, *gs)


def _local_step(xs, cv, target, W, layer_weights, on_layer_grads, ctx_rows):
    T, D = xs.shape
    depth = W["norm1_w"].shape[0]
    saved = []
    X = xs
    for l in range(depth):
        s = {}
        Wl = layer_weights(l, X)
        mod_all, sa = _mod_fwd(cv, Wl["ada_w"], W["ada_b"][l][None, :] + Wl["token"], f"mod_fwd_{l}")
        mod = mod_all[:2].reshape(2, N_MOD, D)
        h1 = _norm_mod(X, W["norm1_w"][l][None, :], mod, 0, ctx_rows, f"norm1_{l}")
        parts = _mm_nn_w(h1, Wl["w_in"], F32, f"in_proj_{l}")
        o_f, st_f = _hgrn_fwd(parts, W["hlb"], l, False, ctx_rows, f"hgrn_fwd_f_{l}")
        o, st_b = _hgrn_fwd(parts, W["hlb"], l, True, ctx_rows, f"hgrn_fwd_b_{l}", o_add=o_f)
        ya = _sgu_fwd(parts, W["sgu_ln_w"][l][None, :], W["sgu_ln_b"][l][None, :], W["sgu_w"][l], W["sgu_bt"][l],
                      f"sgu_fwd_{l}")
        Wl.update(Wl.pop("late")(ya))
        yb, pa, pb, mg, tmo, xm, h2 = _token_out_fwd(o, parts, ya, X, mod, W["hnw"][l][None, :] + Wl["late_token"],
                                                     W["norm2_w"][l][None, :], Wl["w_a"], Wl["w_b"], Wl["w_o"], ctx_rows,
                                                     f"token_out_fwd_{l}")
        up = _mm_nn_w(h2, Wl["w_up"], F32, f"up_proj_{l}")
        ac, act = _conv_fwd(up, Wl["conv_w"], W["conv_b"][l][None, :], ctx_rows, f"conv_fwd_{l}")
        xo, fo = _ffn_out_fwd(act, xm, mod, Wl["w_down"], ctx_rows, f"ffn_out_fwd_{l}")
        s.update(X=X, Wl=Wl, mod=mod, mod_all=mod_all, sa=sa, h1=h1, parts=parts, o=o, st_f=st_f, st_b=st_b, ya=ya, yb=yb,
                 pa=pa, pb=pb, mg=mg, tmo=tmo, xm=xm, h2=h2, up=up, ac=ac, act=act, fo=fo)
        saved.append(s)
        X = xo

    dX, loss_row, dfw = _loss_bwd(X, target, W["final_norm_w"][None, :], ctx_rows, "loss_bwd")
    G = {k: [None] * depth for k in ("ada_b", "norm1_w", "sgu_ln_w", "sgu_ln_b", "sgu_w", "sgu_b", "hlb1", "hnw", "norm2_w",
                                     "conv_w", "conv_b", "dmod")}
    dcv = jnp.zeros_like(cv)
    for l in reversed(range(depth)):
        s = saved[l]
        mod, Wl = s["mod"], s["Wl"]
        big = {}
        dout2, dact, dg2 = _ffn_out_bwd(dX, s["fo"], mod, Wl["w_down"], ctx_rows, f"ffn_out_bwd_{l}")
        big["w_down"] = _mm_tn(s["act"], dout2, F32, f"dw_down_{l}")
        da, dv, dcw, dcb = _conv_bwd(s["up"], s["ac"], dact, Wl["conv_w"], ctx_rows, f"conv_bwd_{l}")
        G["conv_w"][l], G["conv_b"][l] = dcw, dcb[0]
        big["w_up"] = _mm_tn(s["h2"], (da, dv), F32, f"dw_up_{l}", out_chips=True)
        dh2 = _mm_nt_w((da, dv), Wl["w_up"], F32, f"dh2_{l}")
        dxm, dm2, dnw2 = _norm_mod_bwd(dh2, s["xm"], dX, W["norm2_w"][l][None, :], mod, 3, ctx_rows, f"norm2_bwd_{l}")
        G["norm2_w"][l] = dnw2[0]
        (dout1, dpa, dpb, dparts, dya, do, dg1, dhnw) = _token_out_bwd(
            dxm, s["tmo"], s["pa"], s["pb"], s["o"], s["parts"], mod, W["hnw"][l][None, :], Wl["w_a"], Wl["w_b"], Wl["w_o"],
            ctx_rows, f"token_out_bwd_{l}")
        G["hnw"][l] = dhnw[0]
        big["w_o"] = _mm_tn(s["mg"], dout1, F32, f"dw_o_{l}")
        big["w_a"] = _mm_tn(s["ya"], dpa, F32, f"dw_a_{l}")
        big["w_b"] = _mm_tn(s["yb"], dpb, F32, f"dw_b_{l}")
        tok = on_layer_grads(l, "early", big)
        dparts, dsw, dsbt, dlnw, dlnb = _sgu_bwd(s["parts"], dya, W["sgu_ln_w"][l][None, :], W["sgu_ln_b"][l][None, :] + tok,
                                                 W["sgu_w"][l], W["sgu_bt"][l], dparts, f"sgu_bwd_{l}")
        G["sgu_w"][l], G["sgu_b"][l], G["sgu_ln_w"][l], G["sgu_ln_b"][l] = dsw, dsbt.T, dlnw[0], dlnb[0]
        dq_f, dz_f, di_f, dlb_f = _hgrn_bwd(s["parts"], W["hlb"], do, s["st_f"], l, False, ctx_rows, f"hgrn_bwd_f_{l}")
        dparts, dlb_b = _hgrn_bwd(s["parts"], W["hlb"], do, s["st_b"], l, True, ctx_rows, f"hgrn_bwd_b_{l}",
                                  other=(dq_f, dz_f, di_f), dparts=dparts)
        G["hlb1"][l] = jnp.concatenate([dlb_f[0], dlb_b[0]])
        tok = on_layer_grads(l, "late", {"w_in": _mm_tn(s["h1"], dparts, F32, f"dw_in_{l}", out_chips=True)})
        dh1 = _mm_nt_w(dparts, Wl["w_in"], F32, f"dh1_{l}")
        tok = tok + on_layer_grads(l, "end", {"after": dh1})
        dX, dm1, dnw1 = _norm_mod_bwd(dh1, s["X"], dxm, W["norm1_w"][l][None, :] + tok, mod, 0, ctx_rows, f"norm1_bwd_{l}")
        G["norm1_w"][l] = dnw1[0]
        dmod = jnp.concatenate([dm1, dg1, dm2, dg2], axis=1).reshape(2, N_MOD * D)
        dmod16 = jnp.concatenate([dmod, jnp.zeros((cv.shape[0] - 2, N_MOD * D), F32)], axis=0)
        G["ada_b"][l] = dmod[0] + dmod[1]
        G["dmod"][l] = dmod
        dcv = dcv + _cvec_bwd(dmod16, Wl["ada_w"], cv, f"dcvec_{l}")
    G["c_ctx"] = dcv[0]
    G["final_norm_w"] = dfw[0]
    return loss_row[0, 0], dX, G, saved[0]["sa"]


def _chip_peers(x, y, c):
    return [((1 - x, y, c), 2 * (1 - x) + y), ((x, 1 - y, c), 2 * x + 1 - y), ((1 - x, 1 - y, c), 2 * (1 - x) + 1 - y)]


def _rdma_call(ins, out_shapes, plan, n_remote, n_local, name, aliases=None):
    n_in, n_out = len(ins), len(out_shapes)

    def body(*refs):
        in_refs, out_refs = refs[:n_in], refs[n_in:n_in + n_out]
        send_sems, recv_sems, local_sems = refs[n_in + n_out:]
        x, y, c = lax.axis_index("x"), lax.axis_index("y"), lax.axis_index("c")
        remote, local = plan(in_refs, out_refs, x, y, c)
        assert len(remote) == n_remote and len(local) == n_local, (name, len(remote), len(local))
        copies = [pltpu.make_async_copy(s, d, local_sems.at[i]) for i, (s, d) in enumerate(local)]
        copies += [pltpu.make_async_remote_copy(src_ref=s, dst_ref=d, send_sem=send_sems.at[k], recv_sem=recv_sems.at[k],
                                                device_id=dev, device_id_type=pl.DeviceIdType.MESH)
                   for k, (s, d, dev) in enumerate(remote)]
        for cp in copies:
            cp.start()
        for cp in copies:
            cp.wait()

    hbm = pl.BlockSpec(memory_space=pltpu.HBM)
    return pl.pallas_call(
        body, name=name, in_specs=[hbm] * n_in, out_specs=[hbm] * n_out, out_shape=out_shapes,
        scratch_shapes=[pltpu.SemaphoreType.DMA((n_remote,)), pltpu.SemaphoreType.DMA((n_remote,)),
                        pltpu.SemaphoreType.DMA((max(n_local, 1),))],
        input_output_aliases=aliases or {},
    )(*ins)


DMA_PIECE_BYTES = 1 << 18
DMA_MAX_PIECES = 8


def _row_pieces(shape, dtype):
    rows = shape[0]
    row_bytes = jnp.dtype(dtype).itemsize
    for d in shape[1:]:
        row_bytes *= d
    n = 1
    while n < DMA_MAX_PIECES and rows % (2 * n * 16) == 0 and rows * row_bytes // (2 * n) >= DMA_PIECE_BYTES:
        n *= 2
    return [(i * (rows // n), rows // n) for i in range(n)]


def _half_pieces(o, c):
    r2 = o.shape[1] // 2
    return [pl.ds(c * r2 + st, sz) for st, sz in _row_pieces((r2,) + o.shape[2:], o.dtype)]


def _n_half_pieces(arrays):
    return sum(len(_row_pieces((a.shape[1] // 2,) + a.shape[2:], a.dtype)) for a in arrays)


def _plan_gather_far(lands, x, y, c):
    me = 2 * x + y
    return [(o.at[me, rows], o.at[me, rows], dev) for dev, _ in _chip_peers(x, y, c) for o in lands
            for rows in _half_pieces(o, c)]


def _plan_gather_near(lands, x, y, c):
    return [(o.at[idx, rows], o.at[idx, rows], (x, y, 1 - c)) for _, idx in _chip_peers(x, y, c) for o in lands
            for rows in _half_pieces(o, c)]


def _gather_weights(lands, name):
    n = len(lands)
    n_far = (N_CHIPS - 1) * _n_half_pieces(lands)

    def body(*refs):
        outs = refs[n:2 * n]
        far_send, far_recv, near_send, near_recv = refs[2 * n:]
        x, y, c = lax.axis_index("x"), lax.axis_index("y"), lax.axis_index("c")
        mk = lambda plan, send, recv: [
            pltpu.make_async_remote_copy(src_ref=s, dst_ref=d, send_sem=send.at[k], recv_sem=recv.at[k], device_id=dev,
                                         device_id_type=pl.DeviceIdType.MESH)
            for k, (s, d, dev) in enumerate(plan(outs, x, y, c))]
        far, near = mk(_plan_gather_far, far_send, far_recv), mk(_plan_gather_near, near_send, near_recv)
        assert len(far) == n_far and len(near) == n_far
        for cp in far:
            cp.start()
        for k in range(n_far):
            far[k].wait_recv()
            near[k].start()
        for k in range(n_far):
            near[k].wait_recv()
        for cp in far + near:
            cp.wait_send()

    hbm = pl.BlockSpec(memory_space=pltpu.HBM)
    sems = pltpu.SemaphoreType.DMA((n_far,))
    return pl.pallas_call(
        body, name=name, in_specs=[hbm] * n, out_specs=[hbm] * n,
        out_shape=[jax.ShapeDtypeStruct(a.shape, a.dtype) for a in lands],
        scratch_shapes=[sems, sems, sems, sems], input_output_aliases={i: i for i in range(n)},
    )(*lands)


def _gather_all(v, name):
    def plan(ins, outs, x, y, c):
        (s,), (o,) = ins, outs
        me = 4 * x + 2 * y + c
        flip = lambda a, f: 1 - a if f else a
        remote = [(s, o.at[me], (flip(x, m & 4), flip(y, m & 2), flip(c, m & 1))) for m in range(1, 8)]
        return remote, [(s, o.at[me])]

    return _rdma_call([v], [jax.ShapeDtypeStruct((8,) + v.shape, v.dtype)], plan, 7, 1, name)[0]


def _plan_pair(ins, lands, x, y, c):
    return [(a.at[j, 1 - c, pl.ds(st, sz)], o.at[j, pl.ds(st, sz)], (x, y, 1 - c)) for a, o in zip(ins, lands)
            for j in range(N_CHIPS) for st, sz in _row_pieces(a.shape[2:], a.dtype)]


def _n_pair_copies(parts):
    return N_CHIPS * sum(len(_row_pieces(a.shape[2:], a.dtype)) for a in parts)


def _reduce_pair(parts, name):
    shapes = [jax.ShapeDtypeStruct((N_CHIPS,) + a.shape[2:], a.dtype) for a in parts]
    return _rdma_call(parts, shapes, lambda ins, outs, x, y, c: (_plan_pair(ins, outs, x, y, c), []),
                      _n_pair_copies(parts), 0, name)


def _plan_chips(ins, lands, x, y, c):
    me = 2 * x + y
    return [(a.at[idx, pl.ds(st, sz)], o.at[me, pl.ds(st, sz)], dev) for dev, idx in _chip_peers(x, y, c)
            for a, o in zip(ins, lands) for st, sz in _row_pieces(a.shape[1:], a.dtype)]


def _n_chips_copies(parts):
    return (N_CHIPS - 1) * sum(len(_row_pieces(a.shape[1:], a.dtype)) for a in parts)


def _gather_pair(halves, name):
    def plan(ins, outs, x, y, c):
        return [(o.at[c, pl.ds(st, sz)], o.at[c, pl.ds(st, sz)], (x, y, 1 - c)) for o in outs
                for st, sz in _row_pieces(o.shape[1:], o.dtype)], []

    shapes = [jax.ShapeDtypeStruct(a.shape, a.dtype) for a in halves]
    n_remote = sum(len(_row_pieces(a.shape[1:], a.dtype)) for a in halves)
    return _rdma_call(halves, shapes, plan, n_remote, 0, name, aliases={i: i for i in range(len(halves))})


def _split_start(ins, lands, plan, n_remote, name):
    n_buf = len(ins) + len(lands)

    def body(*refs):
        in_refs, land_refs = refs[:len(ins)], refs[len(ins):n_buf]
        send_sems, recv_sems, token = refs[n_buf], refs[n_buf + 1], refs[-1]
        x, y, c = lax.axis_index("x"), lax.axis_index("y"), lax.axis_index("c")
        remote = plan(in_refs, land_refs, x, y, c)
        assert len(remote) == n_remote, (name, len(remote))
        for k, (s, d, dev) in enumerate(remote):
            pltpu.make_async_remote_copy(src_ref=s, dst_ref=d, send_sem=send_sems.at[k], recv_sem=recv_sems.at[k],
                                         device_id=dev, device_id_type=pl.DeviceIdType.MESH).start()
        token[...] = jnp.zeros_like(token)

    hbm = pl.BlockSpec(memory_space=pltpu.HBM)
    sem = pl.BlockSpec(memory_space=pltpu.SEMAPHORE)
    bufs = list(ins) + list(lands)
    out = pl.pallas_call(
        body, name=name, in_specs=[hbm] * n_buf,
        out_specs=(sem, sem) + (hbm,) * n_buf + (pl.BlockSpec(memory_space=pltpu.VMEM),),
        out_shape=(pltpu.SemaphoreType.DMA((n_remote,)), pltpu.SemaphoreType.DMA((n_remote,)))
        + tuple(pltpu.HBM(a.shape, a.dtype) for a in bufs) + (jax.ShapeDtypeStruct((8, 128), F32),),
        input_output_aliases={i: 2 + i for i in range(n_buf)},
        compiler_params=pltpu.CompilerParams(has_side_effects=pltpu.SideEffectType.DATAFLOW_SIDE_EFFECTING),
    )(*[pltpu.with_memory_space_constraint(a, pltpu.HBM) for a in bufs])
    return dict(send=out[0], recv=out[1], ins=list(out[2:2 + len(ins)]), lands=list(out[2 + len(ins):2 + n_buf]),
                token=out[-1][0, 0], token_array=out[-1], plan=plan, n_remote=n_remote)


def _split_wait(st, after, name):
    n_in, n_buf = len(st["ins"]), len(st["ins"]) + len(st["lands"])
    plan, n_remote = st["plan"], st["n_remote"]

    def body(*refs):
        in_refs, land_refs = refs[:n_in], refs[n_in:n_buf]
        send_sems, recv_sems = refs[n_buf], refs[n_buf + 1]
        x, y, c = lax.axis_index("x"), lax.axis_index("y"), lax.axis_index("c")
        for k, (s, d, dev) in enumerate(plan(in_refs, land_refs, x, y, c)):
            cp = pltpu.make_async_remote_copy(src_ref=s, dst_ref=d, send_sem=send_sems.at[k], recv_sem=recv_sems.at[k],
                                              device_id=dev, device_id_type=pl.DeviceIdType.MESH)
            cp.wait_send()
            cp.wait_recv()

    hbm = pl.BlockSpec(memory_space=pltpu.HBM)
    sem = pl.BlockSpec(memory_space=pltpu.SEMAPHORE)
    bufs = st["ins"] + st["lands"]
    out = pl.pallas_call(
        body, name=name, in_specs=[hbm] * n_buf + [sem, sem, pl.BlockSpec(memory_space=pl.ANY)],
        out_specs=[hbm] * n_buf, out_shape=[pltpu.HBM(a.shape, a.dtype) for a in bufs],
        input_output_aliases={i: i for i in range(n_buf)},
        compiler_params=pltpu.CompilerParams(has_side_effects=pltpu.SideEffectType.DATAFLOW_SIDE_EFFECTING),
    )(*bufs, st["send"], st["recv"], after)
    return list(out[:n_in]), list(out[n_in:])


def _pair_forward(lands, name):
    shapes = [jax.ShapeDtypeStruct(a.shape, a.dtype) for a in lands]
    return _rdma_call(lands, shapes, lambda ins, outs, x, y, c: (_plan_gather_near(outs, x, y, c), []),
                      (N_CHIPS - 1) * _n_half_pieces(lands), 0, name, aliases={i: i for i in range(len(lands))})


def _sum_block_rows(r, C):
    return _rows_tile(r, max(16, (1 << 18) // C // 16 * 16))


def _sum_pair(a, recv, cidx, name):
    nch, _, r, C = a.shape
    rb = _sum_block_rows(r, C)

    def body(c_ref, a_ref, r_ref, o_ref):
        o_ref[...] = (a_ref[...] + r_ref[...]).astype(BF16)

    blk = pl.BlockSpec((None, rb, C), lambda j, i, c: (j, i, 0))
    return pl.pallas_call(
        body, name=name,
        grid_spec=pltpu.PrefetchScalarGridSpec(
            num_scalar_prefetch=1, grid=(nch, r // rb),
            in_specs=[pl.BlockSpec((None, None, rb, C), lambda j, i, c: (j, c[0], i, 0)), blk], out_specs=blk),
        out_shape=jax.ShapeDtypeStruct((nch, r, C), BF16),
        compiler_params=_params("parallel", "parallel"),
    )(cidx, a, recv)


def _sum_chips(mine, recv, ids, name):
    nch, r, C = recv.shape
    rb = _sum_block_rows(r, C)

    def body(ids_ref, m_ref, *rest):
        r_refs, o_ref = rest[:nch], rest[nch]
        chip = ids_ref[1]
        own = m_ref[...].astype(F32)
        acc = jnp.where(chip == 0, own, r_refs[0][...].astype(F32))
        for q in range(1, nch):
            acc = acc + jnp.where(chip == q, own, r_refs[q][...].astype(F32))
        o_ref[...] = acc

    def slot(q):
        return pl.BlockSpec((None, rb, C), lambda i, ids: (jnp.where(ids[1] == q, (q + 1) % nch, q), i, 0))

    return pl.pallas_call(
        body, name=name,
        grid_spec=pltpu.PrefetchScalarGridSpec(
            num_scalar_prefetch=1, grid=(r // rb,),
            in_specs=[pl.BlockSpec((None, rb, C), lambda i, ids: (ids[1], i, 0))] + [slot(q) for q in range(nch)],
            out_specs=pl.BlockSpec((None, rb, C), lambda i, ids: (ids[0], i, 0))),
        out_shape=jax.ShapeDtypeStruct((N_CORES, r, C), F32),
        compiler_params=_params("parallel"),
    )(ids, mine, *([recv] * nch))


PACK_COLS = 1024
_SHARDED = ("ada_w", "w_in", "w_branch_a", "w_branch_b", "w_out", "ffn_w_up", "ffn_w_down")
_LAYER_KEYS = ("ada_w", "w_in", "w_a", "w_b", "w_o", "w_up", "w_down")
_SMALL = ("c_ctx", "ada_b", "norm1_w", "sgu_ln_w", "sgu_ln_b", "sgu_w", "sgu_b", "hgrn_lower_bounds", "hgrn_norm_w",
          "norm2_w", "ffn_conv_b", "final_norm_w")
_ORDER = ("c_ctx", "ada_w", "ada_b", "norm1_w", "w_in", "sgu_ln_w", "sgu_ln_b", "sgu_w", "sgu_b", "hgrn_lower_bounds",
          "hgrn_norm_w", "w_branch_a", "w_branch_b", "w_out", "norm2_w", "ffn_w_up", "ffn_conv_w", "ffn_conv_b",
          "ffn_w_down", "final_norm_w")


def _pad_to(v, n):
    return jnp.concatenate([v, jnp.zeros((n - v.shape[0],), v.dtype)]) if v.shape[0] < n else v


def _round_up(n, m):
    return (n + m - 1) // m * m


def _pack(arrays, n_pad):
    flat = jnp.concatenate([a.reshape(-1) for a in arrays])
    return _pad_to(flat, n_pad)


def _unpack(flat, like):
    out, off = [], 0
    for a in like:
        out.append(flat[off:off + a.size].reshape(a.shape))
        off += a.size
    return out


def kernel(x, c, ctx, c_ctx, ada_w, ada_b, norm1_w, w_in, sgu_ln_w, sgu_ln_b, sgu_w, sgu_b, hgrn_lower_bounds, hgrn_norm_w, w_branch_a, w_branch_b, w_out, norm2_w, ffn_w_up, ffn_conv_w, ffn_conv_b, ffn_w_down, final_norm_w, loss_target, m_c_ctx, m_ada_w, m_ada_b, m_norm1_w, m_w_in, m_sgu_ln_w, m_sgu_ln_b, m_sgu_w, m_sgu_b, m_hgrn_lower_bounds, m_hgrn_norm_w, m_w_branch_a, m_w_branch_b, m_w_out, m_norm2_w, m_ffn_w_up, m_ffn_conv_w, m_ffn_conv_b, m_ffn_w_down, m_final_norm_w, v_c_ctx, v_ada_w, v_ada_b, v_norm1_w, v_w_in, v_sgu_ln_w, v_sgu_ln_b, v_sgu_w, v_sgu_b, v_hgrn_lower_bounds, v_hgrn_norm_w, v_w_branch_a, v_w_branch_b, v_w_out, v_norm2_w, v_ffn_w_up, v_ffn_conv_w, v_ffn_conv_b, v_ffn_w_down, v_final_norm_w):
    w = dict(c_ctx=c_ctx, ada_w=ada_w, ada_b=ada_b, norm1_w=norm1_w, w_in=w_in, sgu_ln_w=sgu_ln_w, sgu_ln_b=sgu_ln_b,
             sgu_w=sgu_w, sgu_b=sgu_b, hgrn_lower_bounds=hgrn_lower_bounds, hgrn_norm_w=hgrn_norm_w, w_branch_a=w_branch_a,
             w_branch_b=w_branch_b, w_out=w_out, norm2_w=norm2_w, ffn_w_up=ffn_w_up, ffn_conv_w=ffn_conv_w,
             ffn_conv_b=ffn_conv_b, ffn_w_down=ffn_w_down, final_norm_w=final_norm_w)
    mom = dict(zip(_ORDER, (m_c_ctx, m_ada_w, m_ada_b, m_norm1_w, m_w_in, m_sgu_ln_w, m_sgu_ln_b, m_sgu_w, m_sgu_b,
                            m_hgrn_lower_bounds, m_hgrn_norm_w, m_w_branch_a, m_w_branch_b, m_w_out, m_norm2_w, m_ffn_w_up,
                            m_ffn_conv_w, m_ffn_conv_b, m_ffn_w_down, m_final_norm_w)))
    var = dict(zip(_ORDER, (v_c_ctx, v_ada_w, v_ada_b, v_norm1_w, v_w_in, v_sgu_ln_w, v_sgu_ln_b, v_sgu_w, v_sgu_b,
                            v_hgrn_lower_bounds, v_hgrn_norm_w, v_w_branch_a, v_w_branch_b, v_w_out, v_norm2_w, v_ffn_w_up,
                            v_ffn_conv_w, v_ffn_conv_b, v_ffn_w_down, v_final_norm_w)))
    depth, D = norm1_w.shape
    dff = ffn_conv_b.shape[1]
    ctx_rows = ctx.shape[1]

    assert depth == 2, "the lower-bound softmax is written for two layers"
    core = lax.axis_index("c")
    chip = 2 * lax.axis_index("x") + lax.axis_index("y")
    ids = jnp.stack([core, chip]).astype(jnp.int32)

    first, rest = _LAYER_KEYS[:2], _LAYER_KEYS[2:]
    shard = lambda l, k: w[_SHARDED[_LAYER_KEYS.index(k)]][l].astype(BF16)
    started, conv_full = {}, []

    def landing(s):
        return lax.dynamic_update_slice(lax.empty((N_CHIPS,) + s.shape, s.dtype), s[None], (chip,) + (0,) * s.ndim)

    def start_gather(l, keys, tag):
        lands = [landing(shard(l, k)) for k in keys]
        started[tag] = _split_start([], lands, lambda ins, lds, x, y, c: _plan_gather_far(lds, x, y, c),
                                    (N_CHIPS - 1) * _n_half_pieces(lands), f"gather_start_{tag}")
        return started[tag]["token"]

    def finish_gather(keys, tag, after):
        _, lands = _split_wait(started[tag], after, f"gather_wait_{tag}")
        return dict(zip(keys, _pair_forward(lands, f"gather_forward_{tag}")))

    def layer_weights(l, after):
        if l == 0:
            got = _gather_weights([landing(shard(0, k)) for k in first] + [landing(ffn_conv_w)], "gather_weights_first")
            conv_full.append(jnp.transpose(got[-1], (1, 2, 3, 0, 4)).reshape(depth, 9, dff))
            out = dict(zip(first, got), token=start_gather(0, rest, "rest_0"))
        else:
            out = dict(finish_gather(first, f"first_{l}", after), token=0.0)

        def late(after_late):
            more = finish_gather(rest, f"rest_{l}", after_late)
            more["late_token"] = 0.0
            if l + 1 < depth:
                more["late_token"] = start_gather(l + 1, first, f"first_{l + 1}") + start_gather(l + 1, rest, f"rest_{l + 1}")
            return more

        return dict(out, conv_w=conv_full[0][l], late=late)

    groups, order = {}, []

    def as_parts(gs):
        return [g.reshape(N_CHIPS, N_CORES, g.size // (N_CHIPS * N_CORES * g.shape[-1]), g.shape[-1]) for g in gs]

    def pair_start(tag, l, keys, gs):
        parts = as_parts(gs)
        lands = [lax.empty((N_CHIPS,) + p.shape[2:], p.dtype) for p in parts]
        groups[tag] = dict(l=l, keys=keys, pair=_split_start(parts, lands, _plan_pair, _n_pair_copies(parts),
                                                             f"reduce_pair_start_{tag}"))
        order.append(tag)
        return groups[tag]["pair"]["token"]

    def chips_start(tag, after):
        parts, other = _split_wait(groups[tag]["pair"], after, f"reduce_pair_wait_{tag}")
        sums = [_sum_pair(a, o, ids, f"sum_pair_{tag}_{i}") for i, (a, o) in enumerate(zip(parts, other))]
        lands = [lax.empty(s.shape, s.dtype) for s in sums]
        groups[tag]["chips"] = _split_start(sums, lands, _plan_chips, _n_chips_copies(sums), f"reduce_chips_start_{tag}")
        return groups[tag]["chips"]["token"]

    def chips_finish(tag, after):
        sums, recv = _split_wait(groups[tag]["chips"], after, f"reduce_chips_wait_{tag}")
        return {(groups[tag]["l"], k): _sum_chips(sums[i], recv[i], ids, f"sum_chips_{tag}_{i}")
                for i, k in enumerate(groups[tag]["keys"])}

    def on_layer_grads(l, stage, gs):
        if stage == "early":
            return pair_start(f"early_{l}", l, list(gs), list(gs.values()))
        if stage == "late":
            return pair_start(f"late_{l}", l, ["w_in"], [gs["w_in"]]) + chips_start(f"early_{l}", gs["w_in"])
        return chips_start(f"late_{l}", gs["after"])

    W = dict(ada_b=ada_b, norm1_w=norm1_w, sgu_ln_w=sgu_ln_w, sgu_ln_b=sgu_ln_b, sgu_w=sgu_w.astype(BF16),
             sgu_bt=jnp.swapaxes(sgu_b, 1, 2), hlb=hgrn_lower_bounds, hnw=hgrn_norm_w, norm2_w=norm2_w, conv_b=ffn_conv_b,
             final_norm_w=final_norm_w)
    xs = jnp.concatenate([ctx[0], x[0]], axis=0)
    cv = jnp.concatenate([c_ctx[None, :], c, jnp.zeros((14, D), F32)], axis=0)
    loss_local, dxs, G, sa = _local_step(xs, cv, loss_target[0], W, layer_weights, on_layer_grads, ctx_rows)
    loss = lax.psum(loss_local, ("x", "y", "c"))
    grad_x = dxs[ctx_rows:][None]

    pad8 = lambda a: jnp.pad(a, ((0, 8 - a.shape[0]), (0, 0)))
    fact = jnp.concatenate([pad8(sa[1:2].astype(F32))] + [pad8(G["dmod"][l][1].reshape(N_MOD, D)) for l in range(depth)]
                           + [pad8(G["dmod"][l][0].reshape(N_MOD, D)) for l in range(depth)], axis=0)
    facts = _gather_all(fact, "gather_mod_factors")
    lhs = jnp.concatenate([facts[:, 0].astype(BF16), jnp.broadcast_to(sa[0:1], (8, D))], axis=0)
    ada_cols = N_MOD * D // N_CHIPS
    g_ada = []
    for l in range(depth):
        lo_x, lo_c = 8 * (1 + l), 8 * (1 + depth + l)
        rhs = jnp.concatenate([facts[:, lo_x:lo_x + N_MOD].reshape(8, N_MOD * D),
                               facts[:, lo_c:lo_c + N_MOD].reshape(8, N_MOD * D)], axis=0)
        rhs = lax.dynamic_slice_in_dim(rhs, chip * ada_cols, ada_cols, axis=1).astype(BF16)
        g_ada.append(_mm_tn(lhs, rhs, F32, f"dw_ada_{l}"))

    dh = G["hlb1"][depth - 1]
    small_like = [w[k] for k in _SMALL] + [jnp.zeros((depth, 9, dff), F32)]
    small = [G["c_ctx"], jnp.stack(G["ada_b"]), jnp.stack(G["norm1_w"]), jnp.stack(G["sgu_ln_w"]), jnp.stack(G["sgu_ln_b"]),
             jnp.stack(G["sgu_w"]), jnp.stack(G["sgu_b"]), jnp.stack([-dh, dh]), jnp.stack(G["hnw"]), jnp.stack(G["norm2_w"]),
             jnp.stack(G["conv_b"]), G["final_norm_w"], jnp.stack(G["conv_w"])]
    n_small = sum(a.size for a in small)
    n_small_pad = _round_up(n_small, N_CORES * 16 * PACK_COLS)
    small_rows = n_small_pad // (N_CORES * PACK_COLS)
    small_rep = jnp.broadcast_to(_pack(small, n_small_pad).reshape(1, N_CORES, small_rows, PACK_COLS),
                                 (N_CHIPS, N_CORES, small_rows, PACK_COLS))
    small_parts = as_parts([small_rep])
    small_sums = [_sum_pair(small_parts[0], _reduce_pair(small_parts, "reduce_pair_small")[0], ids, "sum_pair_small")]
    groups["small"] = dict(l=None, keys=["small"], chips=_split_start(
        small_sums, [lax.empty(small_sums[0].shape, small_sums[0].dtype)], _plan_chips, _n_chips_copies(small_sums),
        "reduce_chips_start_small"))

    def gather_halves(halves, name):
        return dict(zip(halves, _gather_pair(list(halves.values()), name)))

    last = order[-1]
    halves = {}
    for tag in order[:-1]:
        halves.update(chips_finish(tag, groups["small"]["chips"]["token_array"]))
    reduced = gather_halves(halves, "gather_pair")
    grads, delta, new_m, new_v = {}, {}, {}, {}

    def adamw_sharded(i):
        k = _SHARDED[i]
        gs = g_ada if i == 0 else [reduced[(l, _LAYER_KEYS[i])].reshape(w[k].shape[1:]) for l in range(depth)]
        grads[k], delta[k], new_m[k], new_v[k] = _adamw(w[k], gs, mom[k], var[k], f"adamw_{k}")

    last_keys = groups[last]["keys"]
    for i in range(len(_SHARDED)):
        if _LAYER_KEYS[i] not in last_keys:
            adamw_sharded(i)
    halves = chips_finish(last, new_v[_SHARDED[-1]])
    halves.update(chips_finish("small", new_v[_SHARDED[-1]]))
    reduced.update(gather_halves(halves, "gather_pair_last"))
    for i in range(len(_SHARDED)):
        if _LAYER_KEYS[i] in last_keys:
            adamw_sharded(i)

    g_small = _unpack(reduced[(None, "small")].reshape(-1), small_like)
    grads.update(zip(_SMALL, g_small[:-1]))
    grads["ffn_conv_w"] = lax.dynamic_slice_in_dim(g_small[-1].reshape(depth, 3, 3, dff), chip * (dff // N_CHIPS),
                                                   dff // N_CHIPS, axis=3)
    packed = _SMALL + ("ffn_conv_w",)
    n_pad = _round_up(sum(w[k].size for k in packed), 16 * PACK_COLS)
    pack = lambda t: _pack([t[k] for k in packed], n_pad).reshape(1, -1, PACK_COLS)
    _, d, nm, nv = _adamw(pack(w), [pack(grads)[0]], pack(mom), pack(var), "adamw_packed")
    like = [w[k] for k in packed]
    for src, dst in ((d, delta), (nm, new_m), (nv, new_v)):
        dst.update(zip(packed, _unpack(src.reshape(-1), like)))

    return (loss, grad_x, *[grads[k] for k in _ORDER], *[delta[k] for k in _ORDER], *[new_m[k] for k in _ORDER],
            *[new_v[k] for k in _ORDER])
```

```python
import functools

import jax
import jax.numpy as jnp
from jax import lax
from jax.experimental import pallas as pl
from jax.experimental.pallas import tpu as pltpu

F32 = jnp.float32
BF16 = jnp.bfloat16

GRID_W = 64
HG_CHUNK = 64
SGU_CHUNK = 128
HEAD = 128
TB = 256
N_MOD = 6
RMS_EPS = 1e-6
LN_EPS = 1e-5
VMEM_LIMIT = 48 * 1024 * 1024
VMEM_LIMIT_PAIR = 58 * 1024 * 1024
VMEM_WHOLE_K = 50 * 1024 * 1024
N_CHIPS = 4
N_CORES = 2

ADAM_LR = 0.001
ADAM_B1 = 0.9
ADAM_B2 = 0.999
ADAM_EPS = 1e-08
ADAM_WD = 0.01
ADAM_STEP = 10

_GELU_C = 0.7978845608028654
_GELU_A = 0.044715


def _sigmoid(x):
    return 0.5 * jnp.tanh(0.5 * x) + 0.5


def _silu(x):
    return x * _sigmoid(x)


def _silu_both(x):
    s = _sigmoid(x)
    return x * s, s * (1.0 + x * (1.0 - s))


def _dsilu(x):
    return _silu_both(x)[1]


def _gelu_both(x):
    x2 = x * x
    t = jnp.tanh(_GELU_C * (x + _GELU_A * x2 * x))
    h = 0.5 * (1.0 + t)
    return x * h, h + 0.5 * x * (1.0 - t * t) * (_GELU_C + 3.0 * _GELU_C * _GELU_A * x2)


def _gelu(x):
    return 0.5 * x * (1.0 + jnp.tanh(_GELU_C * (x + _GELU_A * x * x * x)))


def _dot(a, b, ca, cb):
    return lax.dot_general(a, b, (((ca,), (cb,)), ((), ())), preferred_element_type=F32)


def _nn(a, b):
    return _dot(a, b, 1, 0)


def _nt(a, b):
    return _dot(a, b, 1, 1)


def _tn(a, b):
    return _dot(a, b, 0, 0)


def _params(*sem, vmem=VMEM_LIMIT):
    return pltpu.CompilerParams(dimension_semantics=sem if sem else None, vmem_limit_bytes=vmem)


def _stream_of(i, ctx_blocks):
    return (i >= ctx_blocks).astype(jnp.int32)


def _mm(a, b, mode, tm, tn, tk, out_dtype, name, b_chips=False, out_chips=False, vmem=VMEM_LIMIT):
    a_pair, b_pair = isinstance(a, tuple), isinstance(b, tuple)
    assert (not a_pair or mode == "nt") and (not b_pair or (mode == "tn" and not b_chips))
    ashape = (a[0].shape[0], 2 * a[0].shape[1]) if a_pair else a.shape
    if b_pair:
        bshape = (b[0].shape[0], 2 * b[0].shape[1])
    elif not b_chips:
        bshape = b.shape
    else:
        bshape = (b.shape[1], N_CHIPS * b.shape[2])
    if mode == "nn":
        (M, K), (K2, N) = ashape, bshape
    elif mode == "nt":
        (M, K), (N, K2) = ashape, bshape
    else:
        (K, M), (K2, N) = ashape, bshape
    assert K == K2 and M % tm == 0 and N % tn == 0 and K % tk == 0, (name, ashape, bshape, tm, tn, tk)
    nk = K // tk
    if a_pair:
        n1 = a[0].shape[1] // tk
        assert a[0].shape[1] % tk == 0
        a_specs = [pl.BlockSpec((tm, tk), lambda j, i, k: (i, jnp.minimum(k, n1 - 1))),
                   pl.BlockSpec((tm, tk), lambda j, i, k: (i, jnp.maximum(k - n1, 0)))]
    elif mode == "tn":
        a_specs = [pl.BlockSpec((tk, tm), lambda j, i, k: (k, i))]
    else:
        a_specs = [pl.BlockSpec((tm, tk), lambda j, i, k: (i, k))]
    if b_pair:
        n1 = b[0].shape[1] // tn
        assert b[0].shape[1] % tn == 0
        b_specs = [pl.BlockSpec((tk, tn), lambda j, i, k: (k, jnp.minimum(j, n1 - 1))),
                   pl.BlockSpec((tk, tn), lambda j, i, k: (k, jnp.maximum(j - n1, 0)))]
    elif not b_chips:
        if mode == "nt":
            b_spec = pl.BlockSpec((tn, tk), lambda j, i, k: (j, k))
        else:
            b_spec = pl.BlockSpec((tk, tn), lambda j, i, k: (k, j))
    else:
        cols = b.shape[2]
        if mode == "nn":
            per = cols // tn
            assert cols % tn == 0
            b_spec = pl.BlockSpec((None, tk, tn), lambda j, i, k: (j // per, k, j % per))
        else:
            per = cols // tk
            assert mode == "nt" and cols % tk == 0
            b_spec = pl.BlockSpec((None, tn, tk), lambda j, i, k: (k // per, j, k % per))
    if not b_pair:
        b_specs = [b_spec]
    if out_chips:
        per_o = (N // N_CHIPS) // tn
        assert (N // N_CHIPS) % tn == 0
        o_spec = pl.BlockSpec((None, tm, tn), lambda j, i, k: (j // per_o, i, j % per_o))
        o_shape = (N_CHIPS, M, N // N_CHIPS)
    else:
        o_spec = pl.BlockSpec((tm, tn), lambda j, i, k: (i, j))
        o_shape = (M, N)
    ca, cb = {"nn": (1, 0), "nt": (1, 1), "tn": (0, 0)}[mode]

    in_place = nk == 1
    na, nb = len(a_specs), len(b_specs)

    def body(*refs):
        a_refs, b_refs, rest = refs[:na], refs[na:na + nb], refs[na + nb:]
        if in_place:
            (o_ref,) = rest
        else:
            o_ref, acc = rest
        k = pl.program_id(2)

        if not in_place:
            @pl.when(k == 0)
            def _():
                acc[...] = jnp.zeros_like(acc)

        def multiply(which):
            part = _dot(a_refs[which if a_pair else 0][...], b_refs[which if b_pair else 0][...], ca, cb)
            if in_place:
                o_ref[...] = part.astype(out_dtype)
            else:
                acc[...] += part

        if a_pair or b_pair:
            first = (k < n1) if a_pair else (pl.program_id(0) < n1)
            pl.when(first)(functools.partial(multiply, 0))
            pl.when(jnp.logical_not(first))(functools.partial(multiply, 1))
        else:
            multiply(0)

        if not in_place:
            @pl.when(k == nk - 1)
            def _():
                o_ref[...] = acc[...].astype(out_dtype)

    ins = (list(a) if a_pair else [a]) + (list(b) if b_pair else [b])
    return pl.pallas_call(
        body, name=name, grid=(N // tn, M // tm, nk), in_specs=a_specs + b_specs, out_specs=o_spec,
        out_shape=jax.ShapeDtypeStruct(o_shape, out_dtype),
        scratch_shapes=[] if in_place else [pltpu.VMEM((tm, tn), F32)],
        compiler_params=_params("parallel", "parallel", "arbitrary", vmem=vmem),
    )(*ins)


def _tile(n, pref):
    if n <= pref:
        return n
    best = None
    for t in range(128, pref + 1, 128):
        if n % t == 0:
            best = t
    assert best is not None, (n, pref)
    return best


def _rows_tile(n, pref):
    if n <= pref:
        return n
    best = None
    for t in range(16, pref + 1, 16):
        if n % t == 0:
            best = t
    assert best is not None, (n, pref)
    return best


def _mm_nn_w(a, wg, out_dtype, name):
    M, K = a.shape
    return _mm(a, wg, "nn", _rows_tile(M, 2176), _tile(wg.shape[2], 1536), _tile(K, 1536), out_dtype, name, b_chips=True)


def _mm_nt_w(a, wg, out_dtype, name):
    M = a[0].shape[0] if isinstance(a, tuple) else a.shape[0]
    return _mm(a, wg, "nt", _rows_tile(M, 1088), _tile(wg.shape[1], 1024), _tile(wg.shape[2], 2304), out_dtype, name,
               b_chips=True)


def _mm_tn(a, b, out_dtype, name, out_chips=False):
    K, M = a.shape
    N = 2 * b[0].shape[1] if isinstance(b, tuple) else b.shape[1]
    ncol = N // N_CHIPS if out_chips else N
    tm, tn = _tile(M, 1408), _tile(ncol, 1408)
    if tm * tn > 1408 * 1152:
        tn = _tile(ncol, 1152)
    if isinstance(b, tuple):
        return _mm(a, b, "tn", tm, tn, _rows_tile(K, 2176), out_dtype, name, out_chips=out_chips, vmem=VMEM_LIMIT_PAIR)
    whole = 2 * (K * tm * a.dtype.itemsize + K * tn * b.dtype.itemsize + tm * tn * jnp.dtype(out_dtype).itemsize)
    if whole <= VMEM_WHOLE_K and (M // tm) * (N // tn) >= 4:
        return _mm(a, b, "tn", tm, tn, K, out_dtype, name, out_chips=out_chips, vmem=VMEM_LIMIT_PAIR)
    return _mm(a, b, "tn", tm, tn, _rows_tile(K, 2176), out_dtype, name, out_chips=out_chips)


def _mod_fwd(cv, wg, b, name):
    R, D = cv.shape
    tn = wg.shape[2]
    N = N_CHIPS * tn

    def body(cv_ref, w_ref, b_ref, mod_ref, sa_ref):
        sa = _silu(cv_ref[...]).astype(BF16)
        sa_ref[...] = sa
        mod_ref[...] = _nn(sa, w_ref[...]) + b_ref[...]

    return pl.pallas_call(
        body, name=name, grid=(N_CHIPS,),
        in_specs=[pl.BlockSpec((R, D), lambda j: (0, 0)), pl.BlockSpec((None, D, tn), lambda j: (j, 0, 0)),
                  pl.BlockSpec((1, tn), lambda j: (0, j))],
        out_specs=[pl.BlockSpec((R, tn), lambda j: (0, j)), pl.BlockSpec((R, D), lambda j: (0, 0))],
        out_shape=[jax.ShapeDtypeStruct((R, N), F32), jax.ShapeDtypeStruct((R, D), BF16)],
        compiler_params=_params("arbitrary"),
    )(cv, wg, b)


def _cvec_bwd(dmod, wg, cv, name):
    R, N = dmod.shape
    D = wg.shape[1]
    tk = wg.shape[2]
    nk = N_CHIPS

    def body(dm_ref, w_ref, cv_ref, o_ref):
        k = pl.program_id(0)

        @pl.when(k == 0)
        def _():
            o_ref[...] = jnp.zeros_like(o_ref)

        o_ref[...] += _nt(dm_ref[...].astype(BF16), w_ref[...])

        @pl.when(k == nk - 1)
        def _():
            o_ref[...] = o_ref[...] * _dsilu(cv_ref[...])

    return pl.pallas_call(
        body, name=name, grid=(nk,),
        in_specs=[pl.BlockSpec((R, tk), lambda k: (0, k)), pl.BlockSpec((None, D, tk), lambda k: (k, 0, 0)),
                  pl.BlockSpec((R, D), lambda k: (0, 0))],
        out_specs=pl.BlockSpec((R, D), lambda k: (0, 0)),
        out_shape=jax.ShapeDtypeStruct((R, D), F32),
        compiler_params=_params("arbitrary"),
    )(dmod, wg, cv)


def _norm_mod(x, nw, mod, which, ctx_rows, name):
    T, D = x.shape
    cb = ctx_rows // TB

    def body(x_ref, nw_ref, mod_ref, h_ref):
        xv = x_ref[...]
        r = lax.rsqrt(jnp.mean(xv * xv, axis=-1, keepdims=True) + RMS_EPS)
        y = xv * r * nw_ref[...]
        sh = mod_ref[which:which + 1, :]
        sc = mod_ref[which + 1:which + 2, :]
        h_ref[...] = (y * (1.0 + sc) + sh).astype(BF16)

    return pl.pallas_call(
        body, name=name, grid=(T // TB,),
        in_specs=[pl.BlockSpec((TB, D), lambda i: (i, 0)), pl.BlockSpec((1, D), lambda i: (0, 0)),
                  pl.BlockSpec((None, N_MOD, D), lambda i: (_stream_of(i, cb), 0, 0))],
        out_specs=pl.BlockSpec((TB, D), lambda i: (i, 0)),
        out_shape=jax.ShapeDtypeStruct((T, D), BF16),
        compiler_params=_params("parallel"),
    )(x, nw, mod)


def _norm_mod_bwd(dh, x, dres, nw, mod, which, ctx_rows, name):
    T, D = x.shape
    cb = ctx_rows // TB

    def body(dh_ref, x_ref, dres_ref, nw_ref, mod_ref, dx_ref, dm_ref, dnw_ref):
        i = pl.program_id(0)

        @pl.when(i == 0)
        def _():
            dnw_ref[...] = jnp.zeros_like(dnw_ref)

        @pl.when((i == 0) | (i == cb))
        def _():
            dm_ref[...] = jnp.zeros_like(dm_ref)

        xv = x_ref[...]
        dh = dh_ref[...]
        r = lax.rsqrt(jnp.mean(xv * xv, axis=-1, keepdims=True) + RMS_EPS)
        xh = xv * r
        nwv = nw_ref[...]
        sc = mod_ref[which + 1:which + 2, :]
        y = xh * nwv
        dm_ref[0:1, :] += jnp.sum(dh, axis=0, keepdims=True)
        dm_ref[1:2, :] += jnp.sum(dh * y, axis=0, keepdims=True)
        dy = dh * (1.0 + sc)
        dnw_ref[...] += jnp.sum(dy * xh, axis=0, keepdims=True)
        dxh = dy * nwv
        dx_ref[...] = dres_ref[...] + r * (dxh - xh * jnp.mean(dxh * xh, axis=-1, keepdims=True))

    return pl.pallas_call(
        body, name=name, grid=(T // TB,),
        in_specs=[pl.BlockSpec((TB, D), lambda i: (i, 0)), pl.BlockSpec((TB, D), lambda i: (i, 0)),
                  pl.BlockSpec((TB, D), lambda i: (i, 0)), pl.BlockSpec((1, D), lambda i: (0, 0)),
                  pl.BlockSpec((None, N_MOD, D), lambda i: (_stream_of(i, cb), 0, 0))],
        out_specs=[pl.BlockSpec((TB, D), lambda i: (i, 0)),
                   pl.BlockSpec((None, 2, D), lambda i: (_stream_of(i, cb), 0, 0)),
                   pl.BlockSpec((1, D), lambda i: (0, 0))],
        out_shape=[jax.ShapeDtypeStruct((T, D), F32), jax.ShapeDtypeStruct((2, 2, D), F32),
                   jax.ShapeDtypeStruct((1, D), F32)],
        compiler_params=_params("arbitrary"),
    )(dh, x, dres, nw, mod)


def _scan_chunk(n, rev, n_ctx, n_all):
    if not rev:
        return n
    return jnp.where(n < n_ctx, n_ctx - 1 - n, n_all - 1 + n_ctx - n)


def _cumsum_rows(x, rev):
    rows = x.shape[0]
    row = lax.broadcasted_iota(jnp.int32, (rows, 1), 0)
    s = 1
    while s < rows:
        if not rev:
            x = x + jnp.where(row >= s, pltpu.roll(x, s, 0), 0.0)
        else:
            x = x + jnp.where(row < rows - s, pltpu.roll(x, rows - s, 0), 0.0)
        s *= 2
    return x


def _lower_bound(hlb_ref, layer):
    h = hlb_ref[...]
    if layer == 0:
        return jnp.zeros_like(h[0:1, :])
    return _sigmoid(h[1:2, :] - h[0:1, :])


HG_STEP = 4


def _step_rows(j, rev, backward):
    sub = j if rev == backward else HG_STEP - 1 - j
    return slice(sub * HG_CHUNK, (sub + 1) * HG_CHUNK)


def _hgrn_gates(q_ref, f_ref, hlb_ref, layer, rev, rows):
    lb = _lower_bound(hlb_ref, layer)
    z = f_ref[rows, :]
    sig = 1.0 / (1.0 + jnp.exp(-z))
    fg = lb + (1.0 - lb) * sig
    kk = (1.0 - lb) * (1.0 - sig)
    g = jnp.log(fg)
    b = _cumsum_rows(g, rev)
    bt = jnp.sum(g, axis=0, keepdims=True)
    mid = HG_CHUNK // 2
    r = b[mid:mid + 1, :] if rev else b[mid - 1:mid, :]
    qh = _silu(q_ref[rows, :])
    return lb, sig, fg, kk, b, bt, r, qh


def _tri_mask(rev):
    t = lax.broadcasted_iota(jnp.int32, (HG_CHUNK, HG_CHUNK), 0)
    s = lax.broadcasted_iota(jnp.int32, (HG_CHUNK, HG_CHUNK), 1)
    return (s >= t) if rev else (s <= t)


def _hgrn_fwd(parts, hlb, layer, rev, ctx_rows, name, o_add=None):
    T = parts.shape[0]
    D = hlb.shape[1] // 2
    nh = D // HEAD
    n_all, n_ctx = T // HG_CHUNK, ctx_rows // HG_CHUNK
    assert n_all % HG_STEP == 0 and n_ctx % HG_STEP == 0
    n_steps = n_all // HG_STEP
    block = functools.partial(_scan_chunk, rev=rev, n_ctx=n_ctx // HG_STEP, n_all=n_steps)
    fcol = 2 if rev else 1

    def body(q_ref, f_ref, i_ref, hlb_ref, *rest):
        if o_add is None:
            o_ref, st_ref, s_scr = rest
        else:
            oa_ref, o_ref, st_ref, s_scr = rest
        n = pl.program_id(0)

        @pl.when(n == 0)
        def _():
            s_scr[...] = jnp.zeros_like(s_scr)

        mask = _tri_mask(rev)
        hs = [slice(h * HEAD, (h + 1) * HEAD) for h in range(nh)]
        for j in range(HG_STEP):
            rows = _step_rows(j, rev, False)
            lb, sig, fg, kk, b, bt, r, qh = _hgrn_gates(q_ref, f_ref, hlb_ref, layer, rev, rows)
            qr = (qh * jnp.exp(b - r)).astype(BF16)
            kr = (kk * jnp.exp(r - b)).astype(BF16)
            qe = (qh * jnp.exp(b)).astype(BF16)
            ke = (kk * jnp.exp(bt - b)).astype(BF16)
            dec = jnp.exp(bt)
            v = i_ref[rows, :].astype(BF16)
            st = [s_scr[h] for h in range(nh)]
            a_raw = [_nt(qr[:, sl], kr[:, sl]) for sl in hs]
            o_int = [_nt(qe[:, sl], st[h].astype(BF16)) for h, sl in enumerate(hs)]
            kv = [_tn(v[:, sl], ke[:, sl]) for sl in hs]
            for h, sl in enumerate(hs):
                st_ref[j, h] = st[h]
                o = _nn(jnp.where(mask, a_raw[h], 0.0).astype(BF16), v[:, sl]) + o_int[h]
                if o_add is not None:
                    o = o + oa_ref[rows, sl]
                o_ref[rows, sl] = o
                s_scr[h] = st[h] * dec[:, sl] + kv[h]

    cspec = lambda col: pl.BlockSpec((HG_STEP * HG_CHUNK, D), lambda n: (block(n), col))
    ins = [parts, parts, parts, hlb]
    specs = [cspec(0), cspec(fcol), cspec(3), pl.BlockSpec((2, D), lambda n: (0, 1 if rev else 0))]
    if o_add is not None:
        ins.append(o_add)
        specs.append(cspec(0))
    return pl.pallas_call(
        body, name=name, grid=(n_steps,), in_specs=specs,
        out_specs=[cspec(0), pl.BlockSpec((HG_STEP, nh, HEAD, HEAD), lambda n: (n, 0, 0, 0))],
        out_shape=[jax.ShapeDtypeStruct((T, D), F32), jax.ShapeDtypeStruct((n_all, nh, HEAD, HEAD), F32)],
        scratch_shapes=[pltpu.VMEM((nh, HEAD, HEAD), F32)],
        compiler_params=_params("arbitrary"),
    )(*ins)


def _hgrn_bwd(parts, hlb, do, states, layer, rev, ctx_rows, name, other=None, dparts=None):
    T = parts.shape[0]
    D = hlb.shape[1] // 2
    nh = D // HEAD
    n_all, n_ctx = T // HG_CHUNK, ctx_rows // HG_CHUNK
    assert n_all % HG_STEP == 0 and n_ctx % HG_STEP == 0
    n_steps = n_all // HG_STEP
    step = lambda m: n_steps - 1 - m
    block = lambda m: _scan_chunk(step(m), rev, n_ctx // HG_STEP, n_steps)
    fcol = 2 if rev else 1
    has_add = other is not None
    assert not has_add or rev

    def body(q_ref, f_ref, i_ref, hlb_ref, do_ref, st_ref, *rest):
        if has_add:
            dqa_ref, dza_ref, dia_ref, _, out_ref, dlb_ref, ds_scr = rest
            dq_ref, dz_ref, di_ref = out_ref.at[:, 0:D], out_ref.at[:, 2 * D:3 * D], out_ref.at[:, 3 * D:4 * D]
            out_ref[:, D:2 * D] = dza_ref[...]
        else:
            dq_ref, dz_ref, di_ref, dlb_ref, ds_scr = rest
        m = pl.program_id(0)

        @pl.when(m == 0)
        def _():
            ds_scr[...] = jnp.zeros_like(ds_scr)
            dlb_ref[...] = jnp.zeros_like(dlb_ref)

        mask = _tri_mask(rev)
        hs = [slice(h * HEAD, (h + 1) * HEAD) for h in range(nh)]
        for j in range(HG_STEP):
            rows = _step_rows(j, rev, True)
            slot = HG_STEP - 1 - j
            lb, sig, fg, kk, b, bt, r, qh = _hgrn_gates(q_ref, f_ref, hlb_ref, layer, rev, rows)
            e_qr = jnp.exp(b - r)
            e_kr = jnp.exp(r - b)
            e_b = jnp.exp(b)
            e_ke = jnp.exp(bt - b)
            dec = jnp.exp(bt)
            qr = (qh * e_qr).astype(BF16)
            kr = (kk * e_kr).astype(BF16)
            qe = (qh * e_b).astype(BF16)
            ke = (kk * e_ke).astype(BF16)
            v = i_ref[rows, :].astype(BF16)
            dov = do_ref[rows, :].astype(BF16)
            st = [st_ref[slot, h] for h in range(nh)]
            dst = [ds_scr[h] for h in range(nh)]
            stb = [t.astype(BF16) for t in st]
            dstb = [t.astype(BF16) for t in dst]
            a_raw = [_nt(qr[:, sl], kr[:, sl]) for sl in hs]
            da_raw = [_nt(dov[:, sl], v[:, sl]) for sl in hs]
            dq_int = [_nn(dov[:, sl], stb[h]) for h, sl in enumerate(hs)]
            dk_int = [_nn(v[:, sl], dstb[h]) for h, sl in enumerate(hs)]
            dv_int = [_nt(ke[:, sl], dstb[h]) for h, sl in enumerate(hs)]
            ds_new = [_tn(dov[:, sl], qe[:, sl]) for sl in hs]
            a = [jnp.where(mask, t, 0.0).astype(BF16) for t in a_raw]
            da = [jnp.where(mask, t, 0.0).astype(BF16) for t in da_raw]
            dv_parts = [_tn(a[h], dov[:, sl]) + dv_int[h] for h, sl in enumerate(hs)]
            dq_parts = [_nn(da[h], kr[:, sl]) * e_qr[:, sl] + dq_int[h] * e_b[:, sl] for h, sl in enumerate(hs)]
            dki_parts = [dk_int[h] * e_ke[:, sl] for h, sl in enumerate(hs)]
            dk_parts = [_tn(da[h], qr[:, sl]) * e_kr[:, sl] + dki_parts[h] for h, sl in enumerate(hs)]
            dbt_parts = [dec[:, sl] * jnp.sum(st[h] * dst[h], axis=0, keepdims=True) for h, sl in enumerate(hs)]
            for h, sl in enumerate(hs):
                ds_scr[h] = dst[h] * dec[:, sl] + ds_new[h]
            dq = jnp.concatenate(dq_parts, axis=1)
            dk = jnp.concatenate(dk_parts, axis=1)
            dki = jnp.concatenate(dki_parts, axis=1)
            dv = jnp.concatenate(dv_parts, axis=1)
            dbt = jnp.concatenate(dbt_parts, axis=1) + jnp.sum(kk * dki, axis=0, keepdims=True)
            db = qh * dq - kk * dk
            dg = _cumsum_rows(db, not rev) + dbt
            df = dg / fg - dk
            dz_ref[rows, :] = (df * (1.0 - lb) * sig * (1.0 - sig)).astype(BF16)
            dlb_ref[...] += jnp.sum(df * (1.0 - sig), axis=0, keepdims=True)
            dqr = dq * _dsilu(q_ref[rows, :])
            if has_add:
                dqr = dqr + dqa_ref[rows, :]
                dv = dv + dia_ref[rows, :]
            dq_ref[rows, :] = dqr.astype(dq_ref.dtype)
            di_ref[rows, :] = dv.astype(di_ref.dtype)

        @pl.when(m == n_steps - 1)
        def _():
            lb = _lower_bound(hlb_ref, layer)
            if layer == 0:
                dlb_ref[...] = jnp.zeros_like(dlb_ref)
            else:
                dlb_ref[...] = dlb_ref[...] * lb * (1.0 - lb)

    cspec = lambda col: pl.BlockSpec((HG_STEP * HG_CHUNK, D), lambda m: (block(m), col))
    ins = [parts, parts, parts, hlb, do, states]
    specs = [cspec(0), cspec(fcol), cspec(3), pl.BlockSpec((2, D), lambda m: (0, 1 if rev else 0)), cspec(0),
             pl.BlockSpec((HG_STEP, nh, HEAD, HEAD), lambda m: (step(m), 0, 0, 0))]
    dlb_spec = pl.BlockSpec((1, D), lambda m: (0, 0))
    dlb_shape = jax.ShapeDtypeStruct((1, D), F32)
    if has_add:
        return pl.pallas_call(
            body, name=name, grid=(n_steps,),
            in_specs=specs + [cspec(0), cspec(0), cspec(0), pl.BlockSpec(memory_space=pl.ANY)],
            out_specs=[pl.BlockSpec((HG_STEP * HG_CHUNK, 4 * D), lambda m: (block(m), 0)), dlb_spec],
            out_shape=[jax.ShapeDtypeStruct(dparts.shape, dparts.dtype), dlb_shape],
            scratch_shapes=[pltpu.VMEM((nh, HEAD, HEAD), F32)], input_output_aliases={len(ins) + 3: 0},
            compiler_params=_params("arbitrary"),
        )(*ins, *other, dparts)
    return pl.pallas_call(
        body, name=name, grid=(n_steps,), in_specs=specs,
        out_specs=[cspec(0), cspec(0), cspec(0), dlb_spec],
        out_shape=[jax.ShapeDtypeStruct((T, D), F32), jax.ShapeDtypeStruct((T, D), BF16),
                   jax.ShapeDtypeStruct((T, D), F32), dlb_shape],
        scratch_shapes=[pltpu.VMEM((nh, HEAD, HEAD), F32)],
        compiler_params=_params("arbitrary"),
    )(*ins)


def _sgu_ln(gv, lnw_ref, lnb_ref):
    mu = jnp.mean(gv, axis=-1, keepdims=True)
    xc = gv - mu
    rstd = lax.rsqrt(jnp.mean(xc * xc, axis=-1, keepdims=True) + LN_EPS)
    xh = xc * rstd
    return xh, rstd, xh * lnw_ref[...] + lnb_ref[...]


def _sgu_fwd(parts, lnw, lnb, w, bt, name):
    T = parts.shape[0]
    D = lnw.shape[1]
    G = D // HEAD

    def body(u_ref, v_ref, lnw_ref, lnb_ref, w_ref, bt_ref, ya_ref):
        gu = _gelu(u_ref[...])
        _, _, vn = _sgu_ln(_gelu(v_ref[...]), lnw_ref, lnb_ref)
        vnb = vn.astype(BF16)
        for g in range(G):
            sl = slice(g * HEAD, (g + 1) * HEAD)
            mixed = _nn(w_ref[g], vnb[:, sl]) + bt_ref[:, g:g + 1]
            ya_ref[:, sl] = (gu[:, sl] * mixed).astype(BF16)

    return pl.pallas_call(
        body, name=name, grid=(T // SGU_CHUNK,),
        in_specs=[pl.BlockSpec((SGU_CHUNK, D), lambda n: (n, 4)), pl.BlockSpec((SGU_CHUNK, D), lambda n: (n, 5)),
                  pl.BlockSpec((1, D), lambda n: (0, 0)), pl.BlockSpec((1, D), lambda n: (0, 0)),
                  pl.BlockSpec((G, SGU_CHUNK, SGU_CHUNK), lambda n: (0, 0, 0)),
                  pl.BlockSpec((SGU_CHUNK, G), lambda n: (0, 0))],
        out_specs=pl.BlockSpec((SGU_CHUNK, D), lambda n: (n, 0)),
        out_shape=jax.ShapeDtypeStruct((T, D), BF16),
        compiler_params=_params("parallel"),
    )(parts, parts, lnw, lnb, w, bt)


def _sgu_bwd(parts, dya, lnw, lnb, w, bt, dparts, name):
    T = parts.shape[0]
    D = lnw.shape[1]
    G = D // HEAD

    def body(u_ref, v_ref, dya_ref, lnw_ref, lnb_ref, w_ref, bt_ref, dparts_in,
             duv_ref, dw_ref, dbt_ref, dlnw_ref, dlnb_ref, dvn_scr):
        du_ref = duv_ref.at[:, 0:D]
        dv_ref = duv_ref.at[:, D:2 * D]
        n = pl.program_id(0)

        @pl.when(n == 0)
        def _():
            dw_ref[...] = jnp.zeros_like(dw_ref)
            dbt_ref[...] = jnp.zeros_like(dbt_ref)
            dlnw_ref[...] = jnp.zeros_like(dlnw_ref)
            dlnb_ref[...] = jnp.zeros_like(dlnb_ref)

        gu, dgu = _gelu_both(u_ref[...])
        gv, dgv_dv = _gelu_both(v_ref[...])
        xh, rstd, vn = _sgu_ln(gv, lnw_ref, lnb_ref)
        vnb = vn.astype(BF16)
        dya = dya_ref[...]
        lane = lax.broadcasted_iota(jnp.int32, (SGU_CHUNK, G), 1)
        dbt = jnp.zeros((SGU_CHUNK, G), F32)
        for g in range(G):
            sl = slice(g * HEAD, (g + 1) * HEAD)
            wg = w_ref[g]
            mixed = _nn(wg, vnb[:, sl]) + bt_ref[:, g:g + 1]
            dmix = dya[:, sl] * gu[:, sl]
            du_ref[:, sl] = (dya[:, sl] * mixed * dgu[:, sl]).astype(BF16)
            dmb = dmix.astype(BF16)
            dvn_scr[:, sl] = _tn(wg, dmb)
            dw_ref[g] += _nt(dmb, vnb[:, sl])
            dbt = dbt + jnp.where(lane == g, jnp.sum(dmix, axis=1, keepdims=True), 0.0)
        dbt_ref[...] += dbt
        dvn = dvn_scr[...]
        dlnw_ref[...] += jnp.sum(dvn * xh, axis=0, keepdims=True)
        dlnb_ref[...] += jnp.sum(dvn, axis=0, keepdims=True)
        dxh = dvn * lnw_ref[...]
        dgv = rstd * (dxh - jnp.mean(dxh, axis=-1, keepdims=True) - xh * jnp.mean(dxh * xh, axis=-1, keepdims=True))
        dv_ref[...] = (dgv * dgv_dv).astype(BF16)

    row = lambda col: pl.BlockSpec((SGU_CHUNK, D), lambda n: (n, col))
    vec = pl.BlockSpec((1, D), lambda n: (0, 0))
    wsp = pl.BlockSpec((G, SGU_CHUNK, SGU_CHUNK), lambda n: (0, 0, 0))
    bsp = pl.BlockSpec((SGU_CHUNK, G), lambda n: (0, 0))
    return pl.pallas_call(
        body, name=name, grid=(T // SGU_CHUNK,),
        in_specs=[row(4), row(5), row(0), vec, vec, wsp, bsp, pl.BlockSpec(memory_space=pl.ANY)],
        out_specs=[pl.BlockSpec((SGU_CHUNK, 2 * D), lambda n: (n, 2)), wsp, bsp, vec, vec],
        out_shape=[jax.ShapeDtypeStruct(dparts.shape, dparts.dtype),
                   jax.ShapeDtypeStruct((G, SGU_CHUNK, SGU_CHUNK), F32), jax.ShapeDtypeStruct((SGU_CHUNK, G), F32),
                   jax.ShapeDtypeStruct((1, D), F32), jax.ShapeDtypeStruct((1, D), F32)],
        scratch_shapes=[pltpu.VMEM((SGU_CHUNK, D), F32)], input_output_aliases={7: 0},
        compiler_params=_params("arbitrary"),
    )(parts, parts, dya, lnw, lnb, w, bt, dparts)


TBT = 256
VMEM_LIMIT_TOKEN_OUT = 58 * 1024 * 1024


def _rows_weight_spec(wg):
    return pl.BlockSpec(wg.shape, lambda i: (0, 0, 0))


def _full(w_ref):
    return w_ref[...].reshape(w_ref.shape[0] * w_ref.shape[1], w_ref.shape[2])


def _token_out_fwd(o, parts, ya, x, mod, hnw, nw2, wa, wb, wo, ctx_rows, name):
    T, D = x.shape
    nh = D // HEAD
    cb = ctx_rows // TBT

    def body(o_ref, og_ref, ga_ref, gb_ref, ya_ref, x_ref, mod_ref, hnw_ref, nw2_ref, wa_ref, wb_ref, wo_ref,
             yb_ref, pa_ref, pb_ref, mg_ref, tmo_ref, xm_ref, h2_ref):
        ov = o_ref[...]
        so = _silu(og_ref[...])
        nw = hnw_ref[...]
        for h in range(nh):
            sl = slice(h * HEAD, (h + 1) * HEAD)
            seg = ov[:, sl]
            r = lax.rsqrt(jnp.mean(seg * seg, axis=-1, keepdims=True) + RMS_EPS)
            yb_ref[:, sl] = (seg * r * nw * so[:, sl]).astype(BF16)
        pa = _nn(ya_ref[...], _full(wa_ref))
        pb = _nn(yb_ref[...], _full(wb_ref))
        pa_ref[...] = pa.astype(BF16)
        pb_ref[...] = pb.astype(BF16)
        mg = (_sigmoid(ga_ref[...]) * pa + _sigmoid(gb_ref[...]) * pb).astype(BF16)
        mg_ref[...] = mg
        out = _nn(mg, _full(wo_ref))
        tmo_ref[...] = out.astype(BF16)
        xm = x_ref[...] + mod_ref[2:3, :] * out
        xm_ref[...] = xm
        r = lax.rsqrt(jnp.mean(xm * xm, axis=-1, keepdims=True) + RMS_EPS)
        h2_ref[...] = (xm * r * nw2_ref[...] * (1.0 + mod_ref[4:5, :]) + mod_ref[3:4, :]).astype(BF16)

    row = lambda col: pl.BlockSpec((TBT, D), lambda i: (i, col))
    wsp = _rows_weight_spec(wa)
    sd = lambda dt: jax.ShapeDtypeStruct((T, D), dt)
    return pl.pallas_call(
        body, name=name, grid=(T // TBT,),
        in_specs=[row(0), row(6), row(7), row(8), row(0), row(0),
                  pl.BlockSpec((None, N_MOD, D), lambda i: (_stream_of(i, cb), 0, 0)),
                  pl.BlockSpec((1, HEAD), lambda i: (0, 0)), pl.BlockSpec((1, D), lambda i: (0, 0)), wsp, wsp, wsp],
        out_specs=[row(0)] * 7,
        out_shape=[sd(BF16), sd(BF16), sd(BF16), sd(BF16), sd(BF16), sd(F32), sd(BF16)],
        compiler_params=_params("parallel", vmem=VMEM_LIMIT_TOKEN_OUT),
    )(o, parts, parts, parts, ya, x, mod, hnw, nw2, wa, wb, wo)


def _token_out_bwd(dx, tmo, pa, pb, o, parts, mod, hnw, wa, wb, wo, ctx_rows, name):
    T, D = dx.shape
    nh = D // HEAD
    cb = ctx_rows // TBT

    def body(dx_ref, tmo_ref, pa_ref, pb_ref, o_ref, og_ref, ga_ref, gb_ref, mod_ref, hnw_ref, wa_ref, wb_ref, wo_ref,
             dout_ref, dpa_ref, dpb_ref, dgate_ref, dya_ref, do_ref, dg1_ref, dhnw_ref):
        i = pl.program_id(0)

        @pl.when(i == 0)
        def _():
            dhnw_ref[...] = jnp.zeros_like(dhnw_ref)

        @pl.when((i == 0) | (i == cb))
        def _():
            dg1_ref[...] = jnp.zeros_like(dg1_ref)

        dxv = dx_ref[...]
        dg1_ref[...] += jnp.sum(dxv * tmo_ref[...], axis=0, keepdims=True)
        dout = (dxv * mod_ref[2:3, :]).astype(BF16)
        dout_ref[...] = dout
        dmg = _nt(dout, _full(wo_ref))
        sa = _sigmoid(ga_ref[...])
        sb = _sigmoid(gb_ref[...])
        dpa = (dmg * sa).astype(BF16)
        dpb = (dmg * sb).astype(BF16)
        dpa_ref[...] = dpa
        dpb_ref[...] = dpb
        dgate_ref[:, D:2 * D] = (dmg * pa_ref[...] * sa * (1.0 - sa)).astype(BF16)
        dgate_ref[:, 2 * D:3 * D] = (dmg * pb_ref[...] * sb * (1.0 - sb)).astype(BF16)
        dya_ref[...] = _nt(dpa, _full(wa_ref))
        dyb = _nt(dpb, _full(wb_ref))
        so, dso = _silu_both(og_ref[...])
        ov = o_ref[...]
        nw = hnw_ref[...]
        dnw = jnp.zeros((1, HEAD), F32)
        for h in range(nh):
            sl = slice(h * HEAD, (h + 1) * HEAD)
            seg = ov[:, sl]
            r = lax.rsqrt(jnp.mean(seg * seg, axis=-1, keepdims=True) + RMS_EPS)
            oh = seg * r
            dn = dyb[:, sl] * so[:, sl]
            dgate_ref[:, sl] = (dyb[:, sl] * oh * nw * dso[:, sl]).astype(BF16)
            dnw = dnw + jnp.sum(dn * oh, axis=0, keepdims=True)
            doh = dn * nw
            do_ref[:, sl] = (r * (doh - oh * jnp.mean(doh * oh, axis=-1, keepdims=True))).astype(BF16)
        dhnw_ref[...] += dnw

    row = lambda col: pl.BlockSpec((TBT, D), lambda i: (i, col))
    wsp = _rows_weight_spec(wa)
    sd = lambda dt: jax.ShapeDtypeStruct((T, D), dt)
    return pl.pallas_call(
        body, name=name, grid=(T // TBT,),
        in_specs=[row(0), row(0), row(0), row(0), row(0), row(6), row(7), row(8),
                  pl.BlockSpec((None, N_MOD, D), lambda i: (_stream_of(i, cb), 0, 0)),
                  pl.BlockSpec((1, HEAD), lambda i: (0, 0)), wsp, wsp, wsp],
        out_specs=[row(0)] * 3 + [pl.BlockSpec((TBT, 3 * D), lambda i: (i, 2)), row(0), row(0),
                                  pl.BlockSpec((None, 1, D), lambda i: (_stream_of(i, cb), 0, 0)),
                                  pl.BlockSpec((1, HEAD), lambda i: (0, 0))],
        out_shape=[sd(BF16)] * 3 + [jax.ShapeDtypeStruct((T, 9 * D), BF16), sd(F32), sd(BF16),
                                    jax.ShapeDtypeStruct((2, 1, D), F32), jax.ShapeDtypeStruct((1, HEAD), F32)],
        compiler_params=_params("arbitrary", vmem=VMEM_LIMIT_TOKEN_OUT),
    )(dx, tmo, pa, pb, o, parts, parts, parts, mod, hnw, wa, wb, wo)


def _conv_geometry(i, nb, cb):
    is_ctx = i < cb
    first = (i == 0) | (i == cb)
    last = (i == cb - 1) | (i == nb - 1)
    row = lax.broadcasted_iota(jnp.int32, (TB + 2 * GRID_W, 1), 0)
    w = row & (GRID_W - 1)
    left_ok = (w != 0) | is_ctx
    right_ok = (w != GRID_W - 1) | is_ctx
    return is_ctx, first, last, left_ok, right_ok


def _ext(p_ref, m_ref, n_ref, first, last):
    return jnp.concatenate([jnp.where(first, 0.0, p_ref[...]), m_ref[...], jnp.where(last, 0.0, n_ref[...])], axis=0)


def _shift_prev(e, ok):
    return jnp.where(ok, pltpu.roll(e, 1, 0), 0.0)


def _shift_next(e, ok):
    return jnp.where(ok, pltpu.roll(e, e.shape[0] - 1, 0), 0.0)


def _halo_specs(cbk, n64, coff=0):
    r = TB // GRID_W
    prev = pl.BlockSpec((GRID_W, cbk), lambda j, i: (jnp.maximum(r * i - 1, 0), j + coff))
    main = pl.BlockSpec((TB, cbk), lambda j, i: (i, j + coff))
    nxt = pl.BlockSpec((GRID_W, cbk), lambda j, i: (jnp.minimum(r * i + r, n64 - 1), j + coff))
    return [prev, main, nxt]


def _conv_cblock(dff):
    return _tile(dff, 1408)


def _conv_fwd(up, cw, cbias, ctx_rows, name):
    T, dff = up.shape[0], up.shape[1] // 2
    cbk = _conv_cblock(dff)
    nb, cb = T // TB, ctx_rows // TB
    nvb = dff // cbk

    def body(ap_ref, a_ref, an_ref, v_ref, cw_ref, cb_ref, ac_ref, act_ref):
        i = pl.program_id(1)
        is_ctx, first, last, lok, rok = _conv_geometry(i, nb, cb)
        e = _ext(ap_ref, a_ref, an_ref, first, last)
        el = _shift_prev(e, lok)
        er = _shift_next(e, rok)
        cwv = cw_ref[...]

        def comb(dr, lo):
            sl = slice(lo, lo + TB)
            return cwv[3 * dr:3 * dr + 1] * el[sl] + cwv[3 * dr + 1:3 * dr + 2] * e[sl] + cwv[3 * dr + 2:3 * dr + 3] * er[sl]

        out = comb(1, GRID_W) + jnp.where(is_ctx, 0.0, comb(0, 0) + comb(2, 2 * GRID_W))
        a_c = out + cb_ref[...]
        ac_ref[...] = a_c
        act_ref[...] = (_gelu(a_c) * v_ref[...]).astype(BF16)

    main = pl.BlockSpec((TB, cbk), lambda j, i: (i, j))
    return pl.pallas_call(
        body, name=name, grid=(dff // cbk, nb),
        in_specs=_halo_specs(cbk, T // GRID_W) + [pl.BlockSpec((TB, cbk), lambda j, i: (i, j + nvb)),
                                                 pl.BlockSpec((9, cbk), lambda j, i: (0, j)),
                                                 pl.BlockSpec((1, cbk), lambda j, i: (0, j))],
        out_specs=[main, main],
        out_shape=[jax.ShapeDtypeStruct((T, dff), F32), jax.ShapeDtypeStruct((T, dff), BF16)],
        compiler_params=_params("parallel", "parallel"),
    )(up, up, up, up, cw, cbias)


def _conv_bwd(up, ac, dact, cw, ctx_rows, name):
    T, dff = up.shape[0], up.shape[1] // 2
    cbk = _conv_cblock(dff)
    nb, cb = T // TB, ctx_rows // TB
    nvb = dff // cbk

    def body(ap_ref, a_ref, an_ref, vp_ref, v_ref, vn_ref, cp_ref, c_ref, cn_ref, dp_ref, d_ref, dn_ref, cw_ref,
             da_ref, dv_ref, dcw_ref, dcb_ref):
        i = pl.program_id(1)

        @pl.when(i == 0)
        def _():
            dcw_ref[...] = jnp.zeros_like(dcw_ref)
            dcb_ref[...] = jnp.zeros_like(dcb_ref)

        is_ctx, first, last, lok, rok = _conv_geometry(i, nb, cb)
        gl, dgl = _gelu_both(_ext(cp_ref, c_ref, cn_ref, first, last))
        g = _ext(dp_ref, d_ref, dn_ref, first, last) * _ext(vp_ref, v_ref, vn_ref, first, last) * dgl
        dv_ref[...] = (d_ref[...] * gl[GRID_W:GRID_W + TB]).astype(BF16)
        gm = _shift_prev(g, lok)
        gp = _shift_next(g, rok)
        cwv = cw_ref[...]

        def comb(dr, lo):
            sl = slice(lo, lo + TB)
            return cwv[3 * dr:3 * dr + 1] * gp[sl] + cwv[3 * dr + 1:3 * dr + 2] * g[sl] + cwv[3 * dr + 2:3 * dr + 3] * gm[sl]

        da = comb(1, GRID_W) + jnp.where(is_ctx, 0.0, comb(0, 2 * GRID_W) + comb(2, 0))
        da_ref[...] = da.astype(BF16)
        e = _ext(ap_ref, a_ref, an_ref, first, last)
        taps = [_shift_prev(e, lok), e, _shift_next(e, rok)]
        gmain = g[GRID_W:GRID_W + TB]
        dcb_ref[...] += jnp.sum(gmain, axis=0, keepdims=True)
        vert = jnp.where(is_ctx, 0.0, 1.0)
        for dr in range(3):
            sl = slice(dr * GRID_W, dr * GRID_W + TB)
            for dw in range(3):
                s = jnp.sum(gmain * taps[dw][sl], axis=0, keepdims=True)
                if dr != 1:
                    s = s * vert
                k = 3 * dr + dw
                dcw_ref[k:k + 1, :] += s

    main = pl.BlockSpec((TB, cbk), lambda j, i: (i, j))
    halo = _halo_specs(cbk, T // GRID_W)
    acc9 = pl.BlockSpec((9, cbk), lambda j, i: (0, j))
    acc1 = pl.BlockSpec((1, cbk), lambda j, i: (0, j))
    return pl.pallas_call(
        body, name=name, grid=(dff // cbk, nb),
        in_specs=halo + _halo_specs(cbk, T // GRID_W, nvb) + halo + halo + [acc9],
        out_specs=[main, main, acc9, acc1],
        out_shape=[jax.ShapeDtypeStruct((T, dff), BF16), jax.ShapeDtypeStruct((T, dff), BF16),
                   jax.ShapeDtypeStruct((9, dff), F32), jax.ShapeDtypeStruct((1, dff), F32)],
        compiler_params=_params("parallel", "arbitrary"),
    )(up, up, up, up, up, up, ac, ac, ac, dact, dact, dact, cw)


def _ffn_out_fwd(act, xm, mod, wd, ctx_rows, name):
    T, D = xm.shape
    dff = act.shape[1]
    cb = ctx_rows // TB

    def body(act_ref, x_ref, mod_ref, w_ref, xo_ref, fo_ref):
        out = _nn(act_ref[...], _full(w_ref))
        fo_ref[...] = out.astype(BF16)
        xo_ref[...] = x_ref[...] + mod_ref[5:6, :] * out

    row = pl.BlockSpec((TB, D), lambda i: (i, 0))
    return pl.pallas_call(
        body, name=name, grid=(T // TB,),
        in_specs=[pl.BlockSpec((TB, dff), lambda i: (i, 0)), row,
                  pl.BlockSpec((None, N_MOD, D), lambda i: (_stream_of(i, cb), 0, 0)),
                  _rows_weight_spec(wd)],
        out_specs=[row, row],
        out_shape=[jax.ShapeDtypeStruct((T, D), F32), jax.ShapeDtypeStruct((T, D), BF16)],
        compiler_params=_params("parallel"),
    )(act, xm, mod, wd)


def _ffn_out_bwd(dx, fo, mod, wd, ctx_rows, name):
    T, D = dx.shape
    dff = N_CHIPS * wd.shape[1]
    cb = ctx_rows // TB

    def body(dx_ref, fo_ref, mod_ref, w_ref, dout_ref, dact_ref, dg2_ref):
        i = pl.program_id(0)

        @pl.when((i == 0) | (i == cb))
        def _():
            dg2_ref[...] = jnp.zeros_like(dg2_ref)

        dxv = dx_ref[...]
        dg2_ref[...] += jnp.sum(dxv * fo_ref[...], axis=0, keepdims=True)
        dout = (dxv * mod_ref[5:6, :]).astype(BF16)
        dout_ref[...] = dout
        dact_ref[...] = _nt(dout, _full(w_ref))

    row = pl.BlockSpec((TB, D), lambda i: (i, 0))
    return pl.pallas_call(
        body, name=name, grid=(T // TB,),
        in_specs=[row, row, pl.BlockSpec((None, N_MOD, D), lambda i: (_stream_of(i, cb), 0, 0)),
                  _rows_weight_spec(wd)],
        out_specs=[row, pl.BlockSpec((TB, dff), lambda i: (i, 0)),
                   pl.BlockSpec((None, 1, D), lambda i: (_stream_of(i, cb), 0, 0))],
        out_shape=[jax.ShapeDtypeStruct((T, D), BF16), jax.ShapeDtypeStruct((T, dff), F32),
                   jax.ShapeDtypeStruct((2, 1, D), F32)],
        compiler_params=_params("arbitrary"),
    )(dx, fo, mod, wd)


def _loss_bwd(x, target, fw, ctx_rows, name):
    T, D = x.shape
    cb = ctx_rows // TB

    def body(x_ref, t_ref, fw_ref, dx_ref, loss_ref, dfw_ref):
        i = pl.program_id(0)

        @pl.when(i == 0)
        def _():
            loss_ref[...] = jnp.zeros_like(loss_ref)
            dfw_ref[...] = jnp.zeros_like(dfw_ref)

        @pl.when(i < cb)
        def _():
            dx_ref[...] = jnp.zeros_like(dx_ref)

        @pl.when(i >= cb)
        def _():
            xv = x_ref[...]
            r = lax.rsqrt(jnp.mean(xv * xv, axis=-1, keepdims=True) + RMS_EPS)
            xh = xv * r
            fwv = fw_ref[...]
            err = xh * fwv - t_ref[...]
            loss_ref[...] += (0.5 / D) * jnp.sum(err * err)
            dy = err * (1.0 / D)
            dfw_ref[...] += jnp.sum(dy * xh, axis=0, keepdims=True)
            dxh = dy * fwv
            dx_ref[...] = r * (dxh - xh * jnp.mean(dxh * xh, axis=-1, keepdims=True))

    row = pl.BlockSpec((TB, D), lambda i: (i, 0))
    return pl.pallas_call(
        body, name=name, grid=(T // TB,),
        in_specs=[row, pl.BlockSpec((TB, D), lambda i: (jnp.maximum(i - cb, 0), 0)), pl.BlockSpec((1, D), lambda i: (0, 0))],
        out_specs=[row, pl.BlockSpec((1, 128), lambda i: (0, 0)), pl.BlockSpec((1, D), lambda i: (0, 0))],
        out_shape=[jax.ShapeDtypeStruct((T, D), F32), jax.ShapeDtypeStruct((1, 128), F32),
                   jax.ShapeDtypeStruct((1, D), F32)],
        compiler_params=_params("arbitrary"),
    )(x, target, fw)


def _adamw(w, gs, m, v, name):
    L, R, C = w.shape
    assert len(gs) == L
    rb = _rows_tile(R, max(16, (1 << 19) // C // 16 * 16))
    bc1 = 1.0 - ADAM_B1 ** ADAM_STEP
    bc2 = 1.0 - ADAM_B2 ** ADAM_STEP

    def body(w_ref, m_ref, v_ref, *rest):
        g_refs, (g_ref, d_ref, nm_ref, nv_ref) = rest[:L], rest[L:]
        layer = pl.program_id(0)
        for li in range(L):
            @pl.when(layer == li)
            def _():
                gv = g_refs[li][...]
                g_ref[...] = gv
                nm = ADAM_B1 * m_ref[...] + (1.0 - ADAM_B1) * gv
                nv = ADAM_B2 * v_ref[...] + (1.0 - ADAM_B2) * (gv * gv)
                nm_ref[...] = nm
                nv_ref[...] = nv
                d_ref[...] = -ADAM_LR * ((nm / bc1) / (jnp.sqrt(nv / bc2) + ADAM_EPS) + ADAM_WD * w_ref[...])

    blk = pl.BlockSpec((None, rb, C), lambda l, i: (l, i, 0))
    gblk = pl.BlockSpec((rb, C), lambda l, i: (i, 0))
    sd = jax.ShapeDtypeStruct((L, R, C), F32)
    return pl.pallas_call(
        body, name=name, grid=(L, R // rb), in_specs=[blk] * 3 + [gblk] * L, out_specs=[blk] * 4, out_shape=[sd] * 4,
        compiler_params=_params("parallel", "parallel"),
    )(w, m, v, *gs)


def _local_step(xs, cv, target, W, layer_weights, on_layer_grads, ctx_rows):
    T, D = xs.shape
    depth = W["norm1_w"].shape[0]
    saved = []
    X = xs
    for l in range(depth):
        s = {}
        Wl = layer_weights(l, X)
        mod_all, sa = _mod_fwd(cv, Wl["ada_w"], W["ada_b"][l][None, :] + Wl["token"], f"mod_fwd_{l}")
        mod = mod_all[:2].reshape(2, N_MOD, D)
        h1 = _norm_mod(X, W["norm1_w"][l][None, :], mod, 0, ctx_rows, f"norm1_{l}")
        parts = _mm_nn_w(h1, Wl["w_in"], F32, f"in_proj_{l}")
        o_f, st_f = _hgrn_fwd(parts, W["hlb"], l, False, ctx_rows, f"hgrn_fwd_f_{l}")
        o, st_b = _hgrn_fwd(parts, W["hlb"], l, True, ctx_rows, f"hgrn_fwd_b_{l}", o_add=o_f)
        ya = _sgu_fwd(parts, W["sgu_ln_w"][l][None, :], W["sgu_ln_b"][l][None, :], W["sgu_w"][l], W["sgu_bt"][l],
                      f"sgu_fwd_{l}")
        Wl.update(Wl.pop("late")(ya))
        yb, pa, pb, mg, tmo, xm, h2 = _token_out_fwd(o, parts, ya, X, mod, W["hnw"][l][None, :] + Wl["late_token"],
                                                     W["norm2_w"][l][None, :], Wl["w_a"], Wl["w_b"], Wl["w_o"], ctx_rows,
                                                     f"token_out_fwd_{l}")
        up = _mm_nn_w(h2, Wl["w_up"], F32, f"up_proj_{l}")
        ac, act = _conv_fwd(up, Wl["conv_w"], W["conv_b"][l][None, :], ctx_rows, f"conv_fwd_{l}")
        xo, fo = _ffn_out_fwd(act, xm, mod, Wl["w_down"], ctx_rows, f"ffn_out_fwd_{l}")
        s.update(X=X, Wl=Wl, mod=mod, mod_all=mod_all, sa=sa, h1=h1, parts=parts, o=o, st_f=st_f, st_b=st_b, ya=ya, yb=yb,
                 pa=pa, pb=pb, mg=mg, tmo=tmo, xm=xm, h2=h2, up=up, ac=ac, act=act, fo=fo)
        saved.append(s)
        X = xo

    dX, loss_row, dfw = _loss_bwd(X, target, W["final_norm_w"][None, :], ctx_rows, "loss_bwd")
    G = {k: [None] * depth for k in ("ada_b", "norm1_w", "sgu_ln_w", "sgu_ln_b", "sgu_w", "sgu_b", "hlb1", "hnw", "norm2_w",
                                     "conv_w", "conv_b", "dmod")}
    dcv = jnp.zeros_like(cv)
    for l in reversed(range(depth)):
        s = saved[l]
        mod, Wl = s["mod"], s["Wl"]
        big = {}
        dout2, dact, dg2 = _ffn_out_bwd(dX, s["fo"], mod, Wl["w_down"], ctx_rows, f"ffn_out_bwd_{l}")
        big["w_down"] = _mm_tn(s["act"], dout2, F32, f"dw_down_{l}")
        da, dv, dcw, dcb = _conv_bwd(s["up"], s["ac"], dact, Wl["conv_w"], ctx_rows, f"conv_bwd_{l}")
        G["conv_w"][l], G["conv_b"][l] = dcw, dcb[0]
        big["w_up"] = _mm_tn(s["h2"], (da, dv), F32, f"dw_up_{l}", out_chips=True)
        dh2 = _mm_nt_w((da, dv), Wl["w_up"], F32, f"dh2_{l}")
        dxm, dm2, dnw2 = _norm_mod_bwd(dh2, s["xm"], dX, W["norm2_w"][l][None, :], mod, 3, ctx_rows, f"norm2_bwd_{l}")
        G["norm2_w"][l] = dnw2[0]
        (dout1, dpa, dpb, dparts, dya, do, dg1, dhnw) = _token_out_bwd(
            dxm, s["tmo"], s["pa"], s["pb"], s["o"], s["parts"], mod, W["hnw"][l][None, :], Wl["w_a"], Wl["w_b"], Wl["w_o"],
            ctx_rows, f"token_out_bwd_{l}")
        G["hnw"][l] = dhnw[0]
        big["w_o"] = _mm_tn(s["mg"], dout1, F32, f"dw_o_{l}")
        big["w_a"] = _mm_tn(s["ya"], dpa, F32, f"dw_a_{l}")
        big["w_b"] = _mm_tn(s["yb"], dpb, F32, f"dw_b_{l}")
        tok = on_layer_grads(l, "early", big)
        dparts, dsw, dsbt, dlnw, dlnb = _sgu_bwd(s["parts"], dya, W["sgu_ln_w"][l][None, :], W["sgu_ln_b"][l][None, :] + tok,
                                                 W["sgu_w"][l], W["sgu_bt"][l], dparts, f"sgu_bwd_{l}")
        G["sgu_w"][l], G["sgu_b"][l], G["sgu_ln_w"][l], G["sgu_ln_b"][l] = dsw, dsbt.T, dlnw[0], dlnb[0]
        dq_f, dz_f, di_f, dlb_f = _hgrn_bwd(s["parts"], W["hlb"], do, s["st_f"], l, False, ctx_rows, f"hgrn_bwd_f_{l}")
        dparts, dlb_b = _hgrn_bwd(s["parts"], W["hlb"], do, s["st_b"], l, True, ctx_rows, f"hgrn_bwd_b_{l}",
                                  other=(dq_f, dz_f, di_f), dparts=dparts)
        G["hlb1"][l] = jnp.concatenate([dlb_f[0], dlb_b[0]])
        tok = on_layer_grads(l, "late", {"w_in": _mm_tn(s["h1"], dparts, F32, f"dw_in_{l}", out_chips=True)})
        dh1 = _mm_nt_w(dparts, Wl["w_in"], F32, f"dh1_{l}")
        tok = tok + on_layer_grads(l, "end", {"after": dh1})
        dX, dm1, dnw1 = _norm_mod_bwd(dh1, s["X"], dxm, W["norm1_w"][l][None, :] + tok, mod, 0, ctx_rows, f"norm1_bwd_{l}")
        G["norm1_w"][l] = dnw1[0]
        dmod = jnp.concatenate([dm1, dg1, dm2, dg2], axis=1).reshape(2, N_MOD * D)
        dmod16 = jnp.concatenate([dmod, jnp.zeros((cv.shape[0] - 2, N_MOD * D), F32)], axis=0)
        G["ada_b"][l] = dmod[0] + dmod[1]
        G["dmod"][l] = dmod
        dcv = dcv + _cvec_bwd(dmod16, Wl["ada_w"], cv, f"dcvec_{l}")
    G["c_ctx"] = dcv[0]
    G["final_norm_w"] = dfw[0]
    return loss_row[0, 0], dX, G, saved[0]["sa"]


def _chip_peers(x, y, c):
    return [((1 - x, y, c), 2 * (1 - x) + y), ((x, 1 - y, c), 2 * x + 1 - y), ((1 - x, 1 - y, c), 2 * (1 - x) + 1 - y)]


def _rdma_call(ins, out_shapes, plan, n_remote, n_local, name, aliases=None):
    n_in, n_out = len(ins), len(out_shapes)

    def body(*refs):
        in_refs, out_refs = refs[:n_in], refs[n_in:n_in + n_out]
        send_sems, recv_sems, local_sems = refs[n_in + n_out:]
        x, y, c = lax.axis_index("x"), lax.axis_index("y"), lax.axis_index("c")
        remote, local = plan(in_refs, out_refs, x, y, c)
        assert len(remote) == n_remote and len(local) == n_local, (name, len(remote), len(local))
        copies = [pltpu.make_async_copy(s, d, local_sems.at[i]) for i, (s, d) in enumerate(local)]
        copies += [pltpu.make_async_remote_copy(src_ref=s, dst_ref=d, send_sem=send_sems.at[k], recv_sem=recv_sems.at[k],
                                                device_id=dev, device_id_type=pl.DeviceIdType.MESH)
                   for k, (s, d, dev) in enumerate(remote)]
        for cp in copies:
            cp.start()
        for cp in copies:
            cp.wait()

    hbm = pl.BlockSpec(memory_space=pltpu.HBM)
    return pl.pallas_call(
        body, name=name, in_specs=[hbm] * n_in, out_specs=[hbm] * n_out, out_shape=out_shapes,
        scratch_shapes=[pltpu.SemaphoreType.DMA((n_remote,)), pltpu.SemaphoreType.DMA((n_remote,)),
                        pltpu.SemaphoreType.DMA((max(n_local, 1),))],
        input_output_aliases=aliases or {},
    )(*ins)


DMA_PIECE_BYTES = 1 << 18
DMA_MAX_PIECES = 8


def _row_pieces(shape, dtype):
    rows = shape[0]
    row_bytes = jnp.dtype(dtype).itemsize
    for d in shape[1:]:
        row_bytes *= d
    n = 1
    while n < DMA_MAX_PIECES and rows % (2 * n * 16) == 0 and rows * row_bytes // (2 * n) >= DMA_PIECE_BYTES:
        n *= 2
    return [(i * (rows // n), rows // n) for i in range(n)]


def _half_pieces(o, c):
    r2 = o.shape[1] // 2
    return [pl.ds(c * r2 + st, sz) for st, sz in _row_pieces((r2,) + o.shape[2:], o.dtype)]


def _n_half_pieces(arrays):
    return sum(len(_row_pieces((a.shape[1] // 2,) + a.shape[2:], a.dtype)) for a in arrays)


def _plan_gather_far(lands, x, y, c):
    me = 2 * x + y
    return [(o.at[me, rows], o.at[me, rows], dev) for dev, _ in _chip_peers(x, y, c) for o in lands
            for rows in _half_pieces(o, c)]


def _plan_gather_near(lands, x, y, c):
    return [(o.at[idx, rows], o.at[idx, rows], (x, y, 1 - c)) for _, idx in _chip_peers(x, y, c) for o in lands
            for rows in _half_pieces(o, c)]


def _gather_weights(lands, name):
    n = len(lands)
    n_far = (N_CHIPS - 1) * _n_half_pieces(lands)

    def body(*refs):
        outs = refs[n:2 * n]
        far_send, far_recv, near_send, near_recv = refs[2 * n:]
        x, y, c = lax.axis_index("x"), lax.axis_index("y"), lax.axis_index("c")
        mk = lambda plan, send, recv: [
            pltpu.make_async_remote_copy(src_ref=s, dst_ref=d, send_sem=send.at[k], recv_sem=recv.at[k], device_id=dev,
                                         device_id_type=pl.DeviceIdType.MESH)
            for k, (s, d, dev) in enumerate(plan(outs, x, y, c))]
        far, near = mk(_plan_gather_far, far_send, far_recv), mk(_plan_gather_near, near_send, near_recv)
        assert len(far) == n_far and len(near) == n_far
        for cp in far:
            cp.start()
        for k in range(n_far):
            far[k].wait_recv()
            near[k].start()
        for k in range(n_far):
            near[k].wait_recv()
        for cp in far + near:
            cp.wait_send()

    hbm = pl.BlockSpec(memory_space=pltpu.HBM)
    sems = pltpu.SemaphoreType.DMA((n_far,))
    return pl.pallas_call(
        body, name=name, in_specs=[hbm] * n, out_specs=[hbm] * n,
        out_shape=[jax.ShapeDtypeStruct(a.shape, a.dtype) for a in lands],
        scratch_shapes=[sems, sems, sems, sems], input_output_aliases={i: i for i in range(n)},
    )(*lands)


def _gather_all(v, name):
    def plan(ins, outs, x, y, c):
        (s,), (o,) = ins, outs
        me = 4 * x + 2 * y + c
        flip = lambda a, f: 1 - a if f else a
        remote = [(s, o.at[me], (flip(x, m & 4), flip(y, m & 2), flip(c, m & 1))) for m in range(1, 8)]
        return remote, [(s, o.at[me])]

    return _rdma_call([v], [jax.ShapeDtypeStruct((8,) + v.shape, v.dtype)], plan, 7, 1, name)[0]


def _plan_pair(ins, lands, x, y, c):
    return [(a.at[j, 1 - c, pl.ds(st, sz)], o.at[j, pl.ds(st, sz)], (x, y, 1 - c)) for a, o in zip(ins, lands)
            for j in range(N_CHIPS) for st, sz in _row_pieces(a.shape[2:], a.dtype)]


def _n_pair_copies(parts):
    return N_CHIPS * sum(len(_row_pieces(a.shape[2:], a.dtype)) for a in parts)


def _reduce_pair(parts, name):
    shapes = [jax.ShapeDtypeStruct((N_CHIPS,) + a.shape[2:], a.dtype) for a in parts]
    return _rdma_call(parts, shapes, lambda ins, outs, x, y, c: (_plan_pair(ins, outs, x, y, c), []),
                      _n_pair_copies(parts), 0, name)


def _plan_chips(ins, lands, x, y, c):
    me = 2 * x + y
    return [(a.at[idx, pl.ds(st, sz)], o.at[me, pl.ds(st, sz)], dev) for dev, idx in _chip_peers(x, y, c)
            for a, o in zip(ins, lands) for st, sz in _row_pieces(a.shape[1:], a.dtype)]


def _n_chips_copies(parts):
    return (N_CHIPS - 1) * sum(len(_row_pieces(a.shape[1:], a.dtype)) for a in parts)


def _gather_pair(halves, name):
    def plan(ins, outs, x, y, c):
        return [(o.at[c, pl.ds(st, sz)], o.at[c, pl.ds(st, sz)], (x, y, 1 - c)) for o in outs
                for st, sz in _row_pieces(o.shape[1:], o.dtype)], []

    shapes = [jax.ShapeDtypeStruct(a.shape, a.dtype) for a in halves]
    n_remote = sum(len(_row_pieces(a.shape[1:], a.dtype)) for a in halves)
    return _rdma_call(halves, shapes, plan, n_remote, 0, name, aliases={i: i for i in range(len(halves))})


def _split_start(ins, lands, plan, n_remote, name):
    n_buf = len(ins) + len(lands)

    def body(*refs):
        in_refs, land_refs = refs[:len(ins)], refs[len(ins):n_buf]
        send_sems, recv_sems, token = refs[n_buf], refs[n_buf + 1], refs[-1]
        x, y, c = lax.axis_index("x"), lax.axis_index("y"), lax.axis_index("c")
        remote = plan(in_refs, land_refs, x, y, c)
        assert len(remote) == n_remote, (name, len(remote))
        for k, (s, d, dev) in enumerate(remote):
            pltpu.make_async_remote_copy(src_ref=s, dst_ref=d, send_sem=send_sems.at[k], recv_sem=recv_sems.at[k],
                                         device_id=dev, device_id_type=pl.DeviceIdType.MESH).start()
        token[...] = jnp.zeros_like(token)

    hbm = pl.BlockSpec(memory_space=pltpu.HBM)
    sem = pl.BlockSpec(memory_space=pltpu.SEMAPHORE)
    bufs = list(ins) + list(lands)
    out = pl.pallas_call(
        body, name=name, in_specs=[hbm] * n_buf,
        out_specs=(sem, sem) + (hbm,) * n_buf + (pl.BlockSpec(memory_space=pltpu.VMEM),),
        out_shape=(pltpu.SemaphoreType.DMA((n_remote,)), pltpu.SemaphoreType.DMA((n_remote,)))
        + tuple(pltpu.HBM(a.shape, a.dtype) for a in bufs) + (jax.ShapeDtypeStruct((8, 128), F32),),
        input_output_aliases={i: 2 + i for i in range(n_buf)},
        compiler_params=pltpu.CompilerParams(has_side_effects=pltpu.SideEffectType.DATAFLOW_SIDE_EFFECTING),
    )(*[pltpu.with_memory_space_constraint(a, pltpu.HBM) for a in bufs])
    return dict(send=out[0], recv=out[1], ins=list(out[2:2 + len(ins)]), lands=list(out[2 + len(ins):2 + n_buf]),
                token=out[-1][0, 0], token_array=out[-1], plan=plan, n_remote=n_remote)


def _split_wait(st, after, name):
    n_in, n_buf = len(st["ins"]), len(st["ins"]) + len(st["lands"])
    plan, n_remote = st["plan"], st["n_remote"]

    def body(*refs):
        in_refs, land_refs = refs[:n_in], refs[n_in:n_buf]
        send_sems, recv_sems = refs[n_buf], refs[n_buf + 1]
        x, y, c = lax.axis_index("x"), lax.axis_index("y"), lax.axis_index("c")
        for k, (s, d, dev) in enumerate(plan(in_refs, land_refs, x, y, c)):
            cp = pltpu.make_async_remote_copy(src_ref=s, dst_ref=d, send_sem=send_sems.at[k], recv_sem=recv_sems.at[k],
                                              device_id=dev, device_id_type=pl.DeviceIdType.MESH)
            cp.wait_send()
            cp.wait_recv()

    hbm = pl.BlockSpec(memory_space=pltpu.HBM)
    sem = pl.BlockSpec(memory_space=pltpu.SEMAPHORE)
    bufs = st["ins"] + st["lands"]
    out = pl.pallas_call(
        body, name=name, in_specs=[hbm] * n_buf + [sem, sem, pl.BlockSpec(memory_space=pl.ANY)],
        out_specs=[hbm] * n_buf, out_shape=[pltpu.HBM(a.shape, a.dtype) for a in bufs],
        input_output_aliases={i: i for i in range(n_buf)},
        compiler_params=pltpu.CompilerParams(has_side_effects=pltpu.SideEffectType.DATAFLOW_SIDE_EFFECTING),
    )(*bufs, st["send"], st["recv"], after)
    return list(out[:n_in]), list(out[n_in:])


def _pair_forward(lands, name):
    shapes = [jax.ShapeDtypeStruct(a.shape, a.dtype) for a in lands]
    return _rdma_call(lands, shapes, lambda ins, outs, x, y, c: (_plan_gather_near(outs, x, y, c), []),
                      (N_CHIPS - 1) * _n_half_pieces(lands), 0, name, aliases={i: i for i in range(len(lands))})


def _sum_block_rows(r, C):
    return _rows_tile(r, max(16, (1 << 19) // C // 16 * 16))


def _sum_pair(a, recv, cidx, name):
    nch, _, r, C = a.shape
    rb = _sum_block_rows(r, C)

    def body(c_ref, a_ref, r_ref, o_ref):
        o_ref[...] = (a_ref[...] + r_ref[...]).astype(BF16)

    blk = pl.BlockSpec((None, rb, C), lambda j, i, c: (j, i, 0))
    return pl.pallas_call(
        body, name=name,
        grid_spec=pltpu.PrefetchScalarGridSpec(
            num_scalar_prefetch=1, grid=(nch, r // rb),
            in_specs=[pl.BlockSpec((None, None, rb, C), lambda j, i, c: (j, c[0], i, 0)), blk], out_specs=blk),
        out_shape=jax.ShapeDtypeStruct((nch, r, C), BF16),
        compiler_params=_params("parallel", "parallel"),
    )(cidx, a, recv)


def _sum_chips(mine, recv, ids, name):
    nch, r, C = recv.shape
    rb = _sum_block_rows(r, C)

    def body(ids_ref, m_ref, *rest):
        r_refs, o_ref = rest[:nch], rest[nch]
        chip = ids_ref[1]
        own = m_ref[...].astype(F32)
        acc = jnp.where(chip == 0, own, r_refs[0][...].astype(F32))
        for q in range(1, nch):
            acc = acc + jnp.where(chip == q, own, r_refs[q][...].astype(F32))
        o_ref[...] = acc

    def slot(q):
        return pl.BlockSpec((None, rb, C), lambda i, ids: (jnp.where(ids[1] == q, (q + 1) % nch, q), i, 0))

    return pl.pallas_call(
        body, name=name,
        grid_spec=pltpu.PrefetchScalarGridSpec(
            num_scalar_prefetch=1, grid=(r // rb,),
            in_specs=[pl.BlockSpec((None, rb, C), lambda i, ids: (ids[1], i, 0))] + [slot(q) for q in range(nch)],
            out_specs=pl.BlockSpec((None, rb, C), lambda i, ids: (ids[0], i, 0))),
        out_shape=jax.ShapeDtypeStruct((N_CORES, r, C), F32),
        compiler_params=_params("parallel"),
    )(ids, mine, *([recv] * nch))


PACK_COLS = 1024
_SHARDED = ("ada_w", "w_in", "w_branch_a", "w_branch_b", "w_out", "ffn_w_up", "ffn_w_down")
_LAYER_KEYS = ("ada_w", "w_in", "w_a", "w_b", "w_o", "w_up", "w_down")
_SMALL = ("c_ctx", "ada_b", "norm1_w", "sgu_ln_w", "sgu_ln_b", "sgu_w", "sgu_b", "hgrn_lower_bounds", "hgrn_norm_w",
          "norm2_w", "ffn_conv_b", "final_norm_w")
_ORDER = ("c_ctx", "ada_w", "ada_b", "norm1_w", "w_in", "sgu_ln_w", "sgu_ln_b", "sgu_w", "sgu_b", "hgrn_lower_bounds",
          "hgrn_norm_w", "w_branch_a", "w_branch_b", "w_out", "norm2_w", "ffn_w_up", "ffn_conv_w", "ffn_conv_b",
          "ffn_w_down", "final_norm_w")


def _pad_to(v, n):
    return jnp.concatenate([v, jnp.zeros((n - v.shape[0],), v.dtype)]) if v.shape[0] < n else v


def _round_up(n, m):
    return (n + m - 1) // m * m


def _pack(arrays, n_pad):
    flat = jnp.concatenate([a.reshape(-1) for a in arrays])
    return _pad_to(flat, n_pad)


def _unpack(flat, like):
    out, off = [], 0
    for a in like:
        out.append(flat[off:off + a.size].reshape(a.shape))
        off += a.size
    return out


def kernel(x, c, ctx, c_ctx, ada_w, ada_b, norm1_w, w_in, sgu_ln_w, sgu_ln_b, sgu_w, sgu_b, hgrn_lower_bounds, hgrn_norm_w, w_branch_a, w_branch_b, w_out, norm2_w, ffn_w_up, ffn_conv_w, ffn_conv_b, ffn_w_down, final_norm_w, loss_target, m_c_ctx, m_ada_w, m_ada_b, m_norm1_w, m_w_in, m_sgu_ln_w, m_sgu_ln_b, m_sgu_w, m_sgu_b, m_hgrn_lower_bounds, m_hgrn_norm_w, m_w_branch_a, m_w_branch_b, m_w_out, m_norm2_w, m_ffn_w_up, m_ffn_conv_w, m_ffn_conv_b, m_ffn_w_down, m_final_norm_w, v_c_ctx, v_ada_w, v_ada_b, v_norm1_w, v_w_in, v_sgu_ln_w, v_sgu_ln_b, v_sgu_w, v_sgu_b, v_hgrn_lower_bounds, v_hgrn_norm_w, v_w_branch_a, v_w_branch_b, v_w_out, v_norm2_w, v_ffn_w_up, v_ffn_conv_w, v_ffn_conv_b, v_ffn_w_down, v_final_norm_w):
    w = dict(c_ctx=c_ctx, ada_w=ada_w, ada_b=ada_b, norm1_w=norm1_w, w_in=w_in, sgu_ln_w=sgu_ln_w, sgu_ln_b=sgu_ln_b,
             sgu_w=sgu_w, sgu_b=sgu_b, hgrn_lower_bounds=hgrn_lower_bounds, hgrn_norm_w=hgrn_norm_w, w_branch_a=w_branch_a,
             w_branch_b=w_branch_b, w_out=w_out, norm2_w=norm2_w, ffn_w_up=ffn_w_up, ffn_conv_w=ffn_conv_w,
             ffn_conv_b=ffn_conv_b, ffn_w_down=ffn_w_down, final_norm_w=final_norm_w)
    mom = dict(zip(_ORDER, (m_c_ctx, m_ada_w, m_ada_b, m_norm1_w, m_w_in, m_sgu_ln_w, m_sgu_ln_b, m_sgu_w, m_sgu_b,
                            m_hgrn_lower_bounds, m_hgrn_norm_w, m_w_branch_a, m_w_branch_b, m_w_out, m_norm2_w, m_ffn_w_up,
                            m_ffn_conv_w, m_ffn_conv_b, m_ffn_w_down, m_final_norm_w)))
    var = dict(zip(_ORDER, (v_c_ctx, v_ada_w, v_ada_b, v_norm1_w, v_w_in, v_sgu_ln_w, v_sgu_ln_b, v_sgu_w, v_sgu_b,
                            v_hgrn_lower_bounds, v_hgrn_norm_w, v_w_branch_a, v_w_branch_b, v_w_out, v_norm2_w, v_ffn_w_up,
                            v_ffn_conv_w, v_ffn_conv_b, v_ffn_w_down, v_final_norm_w)))
    depth, D = norm1_w.shape
    dff = ffn_conv_b.shape[1]
    ctx_rows = ctx.shape[1]

    assert depth == 2, "the lower-bound softmax is written for two layers"
    core = lax.axis_index("c")
    chip = 2 * lax.axis_index("x") + lax.axis_index("y")
    ids = jnp.stack([core, chip]).astype(jnp.int32)

    first, rest = _LAYER_KEYS[:2], _LAYER_KEYS[2:]
    shard = lambda l, k: w[_SHARDED[_LAYER_KEYS.index(k)]][l].astype(BF16)
    started, conv_full = {}, []

    def landing(s):
        return lax.dynamic_update_slice(lax.empty((N_CHIPS,) + s.shape, s.dtype), s[None], (chip,) + (0,) * s.ndim)

    def start_gather(l, keys, tag):
        lands = [landing(shard(l, k)) for k in keys]
        started[tag] = _split_start([], lands, lambda ins, lds, x, y, c: _plan_gather_far(lds, x, y, c),
                                    (N_CHIPS - 1) * _n_half_pieces(lands), f"gather_start_{tag}")
        return started[tag]["token"]

    def finish_gather(keys, tag, after):
        _, lands = _split_wait(started[tag], after, f"gather_wait_{tag}")
        return dict(zip(keys, _pair_forward(lands, f"gather_forward_{tag}")))

    def layer_weights(l, after):
        if l == 0:
            got = _gather_weights([landing(shard(0, k)) for k in first] + [landing(ffn_conv_w)], "gather_weights_first")
            conv_full.append(jnp.transpose(got[-1], (1, 2, 3, 0, 4)).reshape(depth, 9, dff))
            out = dict(zip(first, got), token=start_gather(0, rest, "rest_0"))
        else:
            out = dict(finish_gather(first, f"first_{l}", after), token=0.0)

        def late(after_late):
            more = finish_gather(rest, f"rest_{l}", after_late)
            more["late_token"] = 0.0
            if l + 1 < depth:
                more["late_token"] = start_gather(l + 1, first, f"first_{l + 1}") + start_gather(l + 1, rest, f"rest_{l + 1}")
            return more

        return dict(out, conv_w=conv_full[0][l], late=late)

    groups, order = {}, []

    def as_parts(gs):
        return [g.reshape(N_CHIPS, N_CORES, g.size // (N_CHIPS * N_CORES * g.shape[-1]), g.shape[-1]) for g in gs]

    def pair_start(tag, l, keys, gs):
        parts = as_parts(gs)
        lands = [lax.empty((N_CHIPS,) + p.shape[2:], p.dtype) for p in parts]
        groups[tag] = dict(l=l, keys=keys, pair=_split_start(parts, lands, _plan_pair, _n_pair_copies(parts),
                                                             f"reduce_pair_start_{tag}"))
        order.append(tag)
        return groups[tag]["pair"]["token"]

    def chips_start(tag, after):
        parts, other = _split_wait(groups[tag]["pair"], after, f"reduce_pair_wait_{tag}")
        sums = [_sum_pair(a, o, ids, f"sum_pair_{tag}_{i}") for i, (a, o) in enumerate(zip(parts, other))]
        lands = [lax.empty(s.shape, s.dtype) for s in sums]
        groups[tag]["chips"] = _split_start(sums, lands, _plan_chips, _n_chips_copies(sums), f"reduce_chips_start_{tag}")
        return groups[tag]["chips"]["token"]

    def chips_finish(tag, after):
        sums, recv = _split_wait(groups[tag]["chips"], after, f"reduce_chips_wait_{tag}")
        return {(groups[tag]["l"], k): _sum_chips(sums[i], recv[i], ids, f"sum_chips_{tag}_{i}")
                for i, k in enumerate(groups[tag]["keys"])}

    def on_layer_grads(l, stage, gs):
        if stage == "early":
            return pair_start(f"early_{l}", l, list(gs), list(gs.values()))
        if stage == "late":
            return pair_start(f"late_{l}", l, ["w_in"], [gs["w_in"]]) + chips_start(f"early_{l}", gs["w_in"])
        return chips_start(f"late_{l}", gs["after"])

    W = dict(ada_b=ada_b, norm1_w=norm1_w, sgu_ln_w=sgu_ln_w, sgu_ln_b=sgu_ln_b, sgu_w=sgu_w.astype(BF16),
             sgu_bt=jnp.swapaxes(sgu_b, 1, 2), hlb=hgrn_lower_bounds, hnw=hgrn_norm_w, norm2_w=norm2_w, conv_b=ffn_conv_b,
             final_norm_w=final_norm_w)
    xs = jnp.concatenate([ctx[0], x[0]], axis=0)
    cv = jnp.concatenate([c_ctx[None, :], c, jnp.zeros((14, D), F32)], axis=0)
    loss_local, dxs, G, sa = _local_step(xs, cv, loss_target[0], W, layer_weights, on_layer_grads, ctx_rows)
    loss = lax.psum(loss_local, ("x", "y", "c"))
    grad_x = dxs[ctx_rows:][None]

    pad8 = lambda a: jnp.pad(a, ((0, 8 - a.shape[0]), (0, 0)))
    fact = jnp.concatenate([pad8(sa[1:2].astype(F32))] + [pad8(G["dmod"][l][1].reshape(N_MOD, D)) for l in range(depth)]
                           + [pad8(G["dmod"][l][0].reshape(N_MOD, D)) for l in range(depth)], axis=0)
    facts = _gather_all(fact, "gather_mod_factors")
    lhs = jnp.concatenate([facts[:, 0].astype(BF16), jnp.broadcast_to(sa[0:1], (8, D))], axis=0)
    ada_cols = N_MOD * D // N_CHIPS
    g_ada = []
    for l in range(depth):
        lo_x, lo_c = 8 * (1 + l), 8 * (1 + depth + l)
        rhs = jnp.concatenate([facts[:, lo_x:lo_x + N_MOD].reshape(8, N_MOD * D),
                               facts[:, lo_c:lo_c + N_MOD].reshape(8, N_MOD * D)], axis=0)
        rhs = lax.dynamic_slice_in_dim(rhs, chip * ada_cols, ada_cols, axis=1).astype(BF16)
        g_ada.append(_mm_tn(lhs, rhs, F32, f"dw_ada_{l}"))

    dh = G["hlb1"][depth - 1]
    small_like = [w[k] for k in _SMALL] + [jnp.zeros((depth, 9, dff), F32)]
    small = [G["c_ctx"], jnp.stack(G["ada_b"]), jnp.stack(G["norm1_w"]), jnp.stack(G["sgu_ln_w"]), jnp.stack(G["sgu_ln_b"]),
             jnp.stack(G["sgu_w"]), jnp.stack(G["sgu_b"]), jnp.stack([-dh, dh]), jnp.stack(G["hnw"]), jnp.stack(G["norm2_w"]),
             jnp.stack(G["conv_b"]), G["final_norm_w"], jnp.stack(G["conv_w"])]
    n_small = sum(a.size for a in small)
    n_small_pad = _round_up(n_small, N_CORES * 16 * PACK_COLS)
    small_rows = n_small_pad // (N_CORES * PACK_COLS)
    small_rep = jnp.broadcast_to(_pack(small, n_small_pad).reshape(1, N_CORES, small_rows, PACK_COLS),
                                 (N_CHIPS, N_CORES, small_rows, PACK_COLS))
    small_parts = as_parts([small_rep])
    small_sums = [_sum_pair(small_parts[0], _reduce_pair(small_parts, "reduce_pair_small")[0], ids, "sum_pair_small")]
    groups["small"] = dict(l=None, keys=["small"], chips=_split_start(
        small_sums, [lax.empty(small_sums[0].shape, small_sums[0].dtype)], _plan_chips, _n_chips_copies(small_sums),
        "reduce_chips_start_small"))

    def gather_halves(halves, name):
        return dict(zip(halves, _gather_pair(list(halves.values()), name)))

    last = order[-1]
    halves = {}
    for tag in order[:-1]:
        halves.update(chips_finish(tag, groups["small"]["chips"]["token_array"]))
    reduced = gather_halves(halves, "gather_pair")
    grads, delta, new_m, new_v = {}, {}, {}, {}

    def adamw_sharded(i):
        k = _SHARDED[i]
        gs = g_ada if i == 0 else [reduced[(l, _LAYER_KEYS[i])].reshape(w[k].shape[1:]) for l in range(depth)]
        grads[k], delta[k], new_m[k], new_v[k] = _adamw(w[k], gs, mom[k], var[k], f"adamw_{k}")

    last_keys = groups[last]["keys"]
    for i in range(len(_SHARDED)):
        if _LAYER_KEYS[i] not in last_keys:
            adamw_sharded(i)
    halves = chips_finish(last, new_v[_SHARDED[-1]])
    halves.update(chips_finish("small", new_v[_SHARDED[-1]]))
    reduced.update(gather_halves(halves, "gather_pair_last"))
    for i in range(len(_SHARDED)):
        if _LAYER_KEYS[i] in last_keys:
            adamw_sharded(i)

    g_small = _unpack(reduced[(None, "small")].reshape(-1), small_like)
    grads.update(zip(_SMALL, g_small[:-1]))
    grads["ffn_conv_w"] = lax.dynamic_slice_in_dim(g_small[-1].reshape(depth, 3, 3, dff), chip * (dff // N_CHIPS),
                                                   dff // N_CHIPS, axis=3)
    packed = _SMALL + ("ffn_conv_w",)
    n_pad = _round_up(sum(w[k].size for k in packed), 16 * PACK_COLS)
    pack = lambda t: _pack([t[k] for k in packed], n_pad).reshape(1, -1, PACK_COLS)
    _, d, nm, nv = _adamw(pack(w), [pack(grads)[0]], pack(mom), pack(var), "adamw_packed")
    like = [w[k] for k in packed]
    for src, dst in ((d, delta), (nm, new_m), (nv, new_v)):
        dst.update(zip(packed, _unpack(src.reshape(-1), like)))

    return (loss, grad_x, *[grads[k] for k in _ORDER], *[delta[k] for k in _ORDER], *[new_m[k] for k in _ORDER],
            *[new_v[k] for k in _ORDER])
```

```python
import functools

import jax
import jax.numpy as jnp
from jax import lax
from jax.experimental import pallas as pl
from jax.experimental.pallas import tpu as pltpu

F32 = jnp.float32
BF16 = jnp.bfloat16

GRID_W = 64
HG_CHUNK = 64
SGU_CHUNK = 128
HEAD = 128
TB = 256
N_MOD = 6
RMS_EPS = 1e-6
LN_EPS = 1e-5
VMEM_LIMIT = 48 * 1024 * 1024
VMEM_LIMIT_PAIR = 58 * 1024 * 1024
VMEM_WHOLE_K = 50 * 1024 * 1024
N_CHIPS = 4
N_CORES = 2

ADAM_LR = 0.001
ADAM_B1 = 0.9
ADAM_B2 = 0.999
ADAM_EPS = 1e-08
ADAM_WD = 0.01
ADAM_STEP = 10

_GELU_C = 0.7978845608028654
_GELU_A = 0.044715


def _sigmoid(x):
    return 0.5 * jnp.tanh(0.5 * x) + 0.5


def _silu(x):
    return x * _sigmoid(x)


def _silu_both(x):
    s = _sigmoid(x)
    return x * s, s * (1.0 + x * (1.0 - s))


def _dsilu(x):
    return _silu_both(x)[1]


def _gelu_both(x):
    x2 = x * x
    t = jnp.tanh(_GELU_C * (x + _GELU_A * x2 * x))
    h = 0.5 * (1.0 + t)
    return x * h, h + 0.5 * x * (1.0 - t * t) * (_GELU_C + 3.0 * _GELU_C * _GELU_A * x2)


def _gelu(x):
    return 0.5 * x * (1.0 + jnp.tanh(_GELU_C * (x + _GELU_A * x * x * x)))


def _dot(a, b, ca, cb):
    return lax.dot_general(a, b, (((ca,), (cb,)), ((), ())), preferred_element_type=F32)


def _nn(a, b):
    return _dot(a, b, 1, 0)


def _nt(a, b):
    return _dot(a, b, 1, 1)


def _tn(a, b):
    return _dot(a, b, 0, 0)


def _params(*sem, vmem=VMEM_LIMIT):
    return pltpu.CompilerParams(dimension_semantics=sem if sem else None, vmem_limit_bytes=vmem)


def _stream_of(i, ctx_blocks):
    return (i >= ctx_blocks).astype(jnp.int32)


def _mm(a, b, mode, tm, tn, tk, out_dtype, name, b_chips=False, out_chips=False, vmem=VMEM_LIMIT):
    a_pair, b_pair = isinstance(a, tuple), isinstance(b, tuple)
    assert (not a_pair or mode == "nt") and (not b_pair or (mode == "tn" and not b_chips))
    ashape = (a[0].shape[0], 2 * a[0].shape[1]) if a_pair else a.shape
    if b_pair:
        bshape = (b[0].shape[0], 2 * b[0].shape[1])
    elif not b_chips:
        bshape = b.shape
    else:
        bshape = (b.shape[1], N_CHIPS * b.shape[2])
    if mode == "nn":
        (M, K), (K2, N) = ashape, bshape
    elif mode == "nt":
        (M, K), (N, K2) = ashape, bshape
    else:
        (K, M), (K2, N) = ashape, bshape
    assert K == K2 and M % tm == 0 and N % tn == 0 and K % tk == 0, (name, ashape, bshape, tm, tn, tk)
    nk = K // tk
    if a_pair:
        n1 = a[0].shape[1] // tk
        assert a[0].shape[1] % tk == 0
        a_specs = [pl.BlockSpec((tm, tk), lambda j, i, k: (i, jnp.minimum(k, n1 - 1))),
                   pl.BlockSpec((tm, tk), lambda j, i, k: (i, jnp.maximum(k - n1, 0)))]
    elif mode == "tn":
        a_specs = [pl.BlockSpec((tk, tm), lambda j, i, k: (k, i))]
    else:
        a_specs = [pl.BlockSpec((tm, tk), lambda j, i, k: (i, k))]
    if b_pair:
        n1 = b[0].shape[1] // tn
        assert b[0].shape[1] % tn == 0
        b_specs = [pl.BlockSpec((tk, tn), lambda j, i, k: (k, jnp.minimum(j, n1 - 1))),
                   pl.BlockSpec((tk, tn), lambda j, i, k: (k, jnp.maximum(j - n1, 0)))]
    elif not b_chips:
        if mode == "nt":
            b_spec = pl.BlockSpec((tn, tk), lambda j, i, k: (j, k))
        else:
            b_spec = pl.BlockSpec((tk, tn), lambda j, i, k: (k, j))
    else:
        cols = b.shape[2]
        if mode == "nn":
            per = cols // tn
            assert cols % tn == 0
            b_spec = pl.BlockSpec((None, tk, tn), lambda j, i, k: (j // per, k, j % per))
        else:
            per = cols // tk
            assert mode == "nt" and cols % tk == 0
            b_spec = pl.BlockSpec((None, tn, tk), lambda j, i, k: (k // per, j, k % per))
    if not b_pair:
        b_specs = [b_spec]
    if out_chips:
        per_o = (N // N_CHIPS) // tn
        assert (N // N_CHIPS) % tn == 0
        o_spec = pl.BlockSpec((None, tm, tn), lambda j, i, k: (j // per_o, i, j % per_o))
        o_shape = (N_CHIPS, M, N // N_CHIPS)
    else:
        o_spec = pl.BlockSpec((tm, tn), lambda j, i, k: (i, j))
        o_shape = (M, N)
    ca, cb = {"nn": (1, 0), "nt": (1, 1), "tn": (0, 0)}[mode]

    in_place = nk == 1
    na, nb = len(a_specs), len(b_specs)

    def body(*refs):
        a_refs, b_refs, rest = refs[:na], refs[na:na + nb], refs[na + nb:]
        if in_place:
            (o_ref,) = rest
        else:
            o_ref, acc = rest
        k = pl.program_id(2)

        if not in_place:
            @pl.when(k == 0)
            def _():
                acc[...] = jnp.zeros_like(acc)

        def multiply(which):
            part = _dot(a_refs[which if a_pair else 0][...], b_refs[which if b_pair else 0][...], ca, cb)
            if in_place:
                o_ref[...] = part.astype(out_dtype)
            else:
                acc[...] += part

        if a_pair or b_pair:
            first = (k < n1) if a_pair else (pl.program_id(0) < n1)
            pl.when(first)(functools.partial(multiply, 0))
            pl.when(jnp.logical_not(first))(functools.partial(multiply, 1))
        else:
            multiply(0)

        if not in_place:
            @pl.when(k == nk - 1)
            def _():
                o_ref[...] = acc[...].astype(out_dtype)

    ins = (list(a) if a_pair else [a]) + (list(b) if b_pair else [b])
    return pl.pallas_call(
        body, name=name, grid=(N // tn, M // tm, nk), in_specs=a_specs + b_specs, out_specs=o_spec,
        out_shape=jax.ShapeDtypeStruct(o_shape, out_dtype),
        scratch_shapes=[] if in_place else [pltpu.VMEM((tm, tn), F32)],
        compiler_params=_params("parallel", "parallel", "arbitrary", vmem=vmem),
    )(*ins)


def _tile(n, pref):
    if n <= pref:
        return n
    best = None
    for t in range(128, pref + 1, 128):
        if n % t == 0:
            best = t
    assert best is not None, (n, pref)
    return best


def _rows_tile(n, pref):
    if n <= pref:
        return n
    best = None
    for t in range(16, pref + 1, 16):
        if n % t == 0:
            best = t
    assert best is not None, (n, pref)
    return best


def _mm_nn_w(a, wg, out_dtype, name):
    M, K = a.shape
    return _mm(a, wg, "nn", _rows_tile(M, 2176), _tile(wg.shape[2], 1536), _tile(K, 1536), out_dtype, name, b_chips=True)


def _mm_nt_w(a, wg, out_dtype, name):
    M = a[0].shape[0] if isinstance(a, tuple) else a.shape[0]
    return _mm(a, wg, "nt", _rows_tile(M, 1088), _tile(wg.shape[1], 1024), _tile(wg.shape[2], 2304), out_dtype, name,
               b_chips=True)


def _mm_tn(a, b, out_dtype, name, out_chips=False):
    K, M = a.shape
    N = 2 * b[0].shape[1] if isinstance(b, tuple) else b.shape[1]
    ncol = N // N_CHIPS if out_chips else N
    tm, tn = _tile(M, 1408), _tile(ncol, 1408)
    if tm * tn > 1408 * 1152:
        tn = _tile(ncol, 1152)
    if isinstance(b, tuple):
        return _mm(a, b, "tn", tm, tn, _rows_tile(K, 2176), out_dtype, name, out_chips=out_chips, vmem=VMEM_LIMIT_PAIR)
    whole = 2 * (K * tm * a.dtype.itemsize + K * tn * b.dtype.itemsize + tm * tn * jnp.dtype(out_dtype).itemsize)
    if whole <= VMEM_WHOLE_K and (M // tm) * (N // tn) >= 4:
        return _mm(a, b, "tn", tm, tn, K, out_dtype, name, out_chips=out_chips, vmem=VMEM_LIMIT_PAIR)
    return _mm(a, b, "tn", tm, tn, _rows_tile(K, 2176), out_dtype, name, out_chips=out_chips)


def _mod_fwd(cv, wg, b, name):
    R, D = cv.shape
    tn = wg.shape[2]
    N = N_CHIPS * tn

    def body(cv_ref, w_ref, b_ref, mod_ref, sa_ref):
        sa = _silu(cv_ref[...]).astype(BF16)
        sa_ref[...] = sa
        mod_ref[...] = _nn(sa, w_ref[...]) + b_ref[...]

    return pl.pallas_call(
        body, name=name, grid=(N_CHIPS,),
        in_specs=[pl.BlockSpec((R, D), lambda j: (0, 0)), pl.BlockSpec((None, D, tn), lambda j: (j, 0, 0)),
                  pl.BlockSpec((1, tn), lambda j: (0, j))],
        out_specs=[pl.BlockSpec((R, tn), lambda j: (0, j)), pl.BlockSpec((R, D), lambda j: (0, 0))],
        out_shape=[jax.ShapeDtypeStruct((R, N), F32), jax.ShapeDtypeStruct((R, D), BF16)],
        compiler_params=_params("arbitrary"),
    )(cv, wg, b)


def _cvec_bwd(dmod, wg, cv, name):
    R, N = dmod.shape
    D = wg.shape[1]
    tk = wg.shape[2]
    nk = N_CHIPS

    def body(dm_ref, w_ref, cv_ref, o_ref):
        k = pl.program_id(0)

        @pl.when(k == 0)
        def _():
            o_ref[...] = jnp.zeros_like(o_ref)

        o_ref[...] += _nt(dm_ref[...].astype(BF16), w_ref[...])

        @pl.when(k == nk - 1)
        def _():
            o_ref[...] = o_ref[...] * _dsilu(cv_ref[...])

    return pl.pallas_call(
        body, name=name, grid=(nk,),
        in_specs=[pl.BlockSpec((R, tk), lambda k: (0, k)), pl.BlockSpec((None, D, tk), lambda k: (k, 0, 0)),
                  pl.BlockSpec((R, D), lambda k: (0, 0))],
        out_specs=pl.BlockSpec((R, D), lambda k: (0, 0)),
        out_shape=jax.ShapeDtypeStruct((R, D), F32),
        compiler_params=_params("arbitrary"),
    )(dmod, wg, cv)


def _norm_mod(x, nw, mod, which, ctx_rows, name):
    T, D = x.shape
    cb = ctx_rows // TB

    def body(x_ref, nw_ref, mod_ref, h_ref):
        xv = x_ref[...]
        r = lax.rsqrt(jnp.mean(xv * xv, axis=-1, keepdims=True) + RMS_EPS)
        y = xv * r * nw_ref[...]
        sh = mod_ref[which:which + 1, :]
        sc = mod_ref[which + 1:which + 2, :]
        h_ref[...] = (y * (1.0 + sc) + sh).astype(BF16)

    return pl.pallas_call(
        body, name=name, grid=(T // TB,),
        in_specs=[pl.BlockSpec((TB, D), lambda i: (i, 0)), pl.BlockSpec((1, D), lambda i: (0, 0)),
                  pl.BlockSpec((None, N_MOD, D), lambda i: (_stream_of(i, cb), 0, 0))],
        out_specs=pl.BlockSpec((TB, D), lambda i: (i, 0)),
        out_shape=jax.ShapeDtypeStruct((T, D), BF16),
        compiler_params=_params("parallel"),
    )(x, nw, mod)


def _norm_mod_bwd(dh, x, dres, nw, mod, which, ctx_rows, name):
    T, D = x.shape
    cb = ctx_rows // TB

    def body(dh_ref, x_ref, dres_ref, nw_ref, mod_ref, dx_ref, dm_ref, dnw_ref):
        i = pl.program_id(0)

        @pl.when(i == 0)
        def _():
            dnw_ref[...] = jnp.zeros_like(dnw_ref)

        @pl.when((i == 0) | (i == cb))
        def _():
            dm_ref[...] = jnp.zeros_like(dm_ref)

        xv = x_ref[...]
        dh = dh_ref[...]
        r = lax.rsqrt(jnp.mean(xv * xv, axis=-1, keepdims=True) + RMS_EPS)
        xh = xv * r
        nwv = nw_ref[...]
        sc = mod_ref[which + 1:which + 2, :]
        y = xh * nwv
        dm_ref[0:1, :] += jnp.sum(dh, axis=0, keepdims=True)
        dm_ref[1:2, :] += jnp.sum(dh * y, axis=0, keepdims=True)
        dy = dh * (1.0 + sc)
        dnw_ref[...] += jnp.sum(dy * xh, axis=0, keepdims=True)
        dxh = dy * nwv
        dx_ref[...] = dres_ref[...] + r * (dxh - xh * jnp.mean(dxh * xh, axis=-1, keepdims=True))

    return pl.pallas_call(
        body, name=name, grid=(T // TB,),
        in_specs=[pl.BlockSpec((TB, D), lambda i: (i, 0)), pl.BlockSpec((TB, D), lambda i: (i, 0)),
                  pl.BlockSpec((TB, D), lambda i: (i, 0)), pl.BlockSpec((1, D), lambda i: (0, 0)),
                  pl.BlockSpec((None, N_MOD, D), lambda i: (_stream_of(i, cb), 0, 0))],
        out_specs=[pl.BlockSpec((TB, D), lambda i: (i, 0)),
                   pl.BlockSpec((None, 2, D), lambda i: (_stream_of(i, cb), 0, 0)),
                   pl.BlockSpec((1, D), lambda i: (0, 0))],
        out_shape=[jax.ShapeDtypeStruct((T, D), F32), jax.ShapeDtypeStruct((2, 2, D), F32),
                   jax.ShapeDtypeStruct((1, D), F32)],
        compiler_params=_params("arbitrary"),
    )(dh, x, dres, nw, mod)


def _scan_chunk(n, rev, n_ctx, n_all):
    if not rev:
        return n
    return jnp.where(n < n_ctx, n_ctx - 1 - n, n_all - 1 + n_ctx - n)


def _cumsum_rows(x, rev):
    rows = x.shape[0]
    row = lax.broadcasted_iota(jnp.int32, (rows, 1), 0)
    s = 1
    while s < rows:
        if not rev:
            x = x + jnp.where(row >= s, pltpu.roll(x, s, 0), 0.0)
        else:
            x = x + jnp.where(row < rows - s, pltpu.roll(x, rows - s, 0), 0.0)
        s *= 2
    return x


def _lower_bound(hlb_ref, layer):
    h = hlb_ref[...]
    if layer == 0:
        return jnp.zeros_like(h[0:1, :])
    return _sigmoid(h[1:2, :] - h[0:1, :])


HG_STEP = 4
HG_HEADS = 4


def _step_rows(j, rev, backward):
    sub = j if rev == backward else HG_STEP - 1 - j
    return slice(sub * HG_CHUNK, (sub + 1) * HG_CHUNK)


def _hgrn_gates(q_ref, f_ref, hlb_ref, layer, rev, rows, cols=slice(None)):
    lb = _lower_bound(hlb_ref, layer)[:, cols]
    z = f_ref[rows, cols]
    sig = 1.0 / (1.0 + jnp.exp(-z))
    fg = lb + (1.0 - lb) * sig
    kk = (1.0 - lb) * (1.0 - sig)
    g = jnp.log(fg)
    b = _cumsum_rows(g, rev)
    bt = jnp.sum(g, axis=0, keepdims=True)
    mid = HG_CHUNK // 2
    r = b[mid:mid + 1, :] if rev else b[mid - 1:mid, :]
    qh = _silu(q_ref[rows, cols])
    return lb, sig, fg, kk, b, bt, r, qh


def _tri_mask(rev):
    t = lax.broadcasted_iota(jnp.int32, (HG_CHUNK, HG_CHUNK), 0)
    s = lax.broadcasted_iota(jnp.int32, (HG_CHUNK, HG_CHUNK), 1)
    return (s >= t) if rev else (s <= t)


def _hgrn_fwd_both(parts, hlb, layer, ctx_rows, name):
    T = parts.shape[0]
    D = hlb.shape[1] // 2
    nh = D // HEAD
    n_all, n_ctx = T // HG_CHUNK, ctx_rows // HG_CHUNK
    assert n_all % HG_STEP == 0 and n_ctx % HG_STEP == 0
    n_steps = n_all // HG_STEP
    block = lambda rev: functools.partial(_scan_chunk, rev=rev, n_ctx=n_ctx // HG_STEP, n_all=n_steps)

    def body(qf_ref, ff_ref, if_ref, hf_ref, qb_ref, fb_ref, ib_ref, hb_ref, of_ref, ob_ref, stf_ref, stb_ref, sf_scr, sb_scr):
        n = pl.program_id(0)

        @pl.when(n == 0)
        def _():
            sf_scr[...] = jnp.zeros_like(sf_scr)
            sb_scr[...] = jnp.zeros_like(sb_scr)

        hs = [slice(h * HEAD, (h + 1) * HEAD) for h in range(nh)]
        dirs = ((False, qf_ref, ff_ref, if_ref, hf_ref, of_ref, stf_ref, sf_scr),
                (True, qb_ref, fb_ref, ib_ref, hb_ref, ob_ref, stb_ref, sb_scr))
        for j in range(HG_STEP):
            for rev, q_ref, f_ref, i_ref, hlb_ref, o_ref, st_ref, s_scr in dirs:
                mask = _tri_mask(rev)
                rows = _step_rows(j, rev, False)
                lb, sig, fg, kk, b, bt, r, qh = _hgrn_gates(q_ref, f_ref, hlb_ref, layer, rev, rows)
                qr = (qh * jnp.exp(b - r)).astype(BF16)
                kr = (kk * jnp.exp(r - b)).astype(BF16)
                qe = (qh * jnp.exp(b)).astype(BF16)
                ke = (kk * jnp.exp(bt - b)).astype(BF16)
                dec = jnp.exp(bt)
                v = i_ref[rows, :].astype(BF16)
                st = [s_scr[h] for h in range(nh)]
                a_raw = [_nt(qr[:, sl], kr[:, sl]) for sl in hs]
                o_int = [_nt(qe[:, sl], st[h].astype(BF16)) for h, sl in enumerate(hs)]
                kv = [_tn(v[:, sl], ke[:, sl]) for sl in hs]
                for h, sl in enumerate(hs):
                    st_ref[j, h] = st[h]
                    o_ref[rows, sl] = _nn(jnp.where(mask, a_raw[h], 0.0).astype(BF16), v[:, sl]) + o_int[h]
                    s_scr[h] = st[h] * dec[:, sl] + kv[h]

    cspec = lambda rev, col: pl.BlockSpec((HG_STEP * HG_CHUNK, D), lambda n: (block(rev)(n), col))
    hspec = lambda rev: pl.BlockSpec((2, D), lambda n: (0, 1 if rev else 0))
    stspec = pl.BlockSpec((HG_STEP, nh, HEAD, HEAD), lambda n: (n, 0, 0, 0))
    o_shape = jax.ShapeDtypeStruct((T, D), F32)
    st_shape = jax.ShapeDtypeStruct((n_all, nh, HEAD, HEAD), F32)
    return pl.pallas_call(
        body, name=name, grid=(n_steps,),
        in_specs=[cspec(False, 0), cspec(False, 1), cspec(False, 3), hspec(False),
                  cspec(True, 0), cspec(True, 2), cspec(True, 3), hspec(True)],
        out_specs=[cspec(False, 0), cspec(True, 0), stspec, stspec],
        out_shape=[o_shape, o_shape, st_shape, st_shape],
        scratch_shapes=[pltpu.VMEM((nh, HEAD, HEAD), F32), pltpu.VMEM((nh, HEAD, HEAD), F32)],
        compiler_params=_params("arbitrary"),
    )(parts, parts, parts, hlb, parts, parts, parts, hlb)


def _hgrn_bwd(parts, hlb, do, states, layer, rev, ctx_rows, name, other=None, dparts=None):
    T = parts.shape[0]
    D = hlb.shape[1] // 2
    nh = D // HEAD
    n_all, n_ctx = T // HG_CHUNK, ctx_rows // HG_CHUNK
    assert n_all % HG_STEP == 0 and n_ctx % HG_STEP == 0
    n_steps = n_all // HG_STEP
    step = lambda m: n_steps - 1 - m
    block = lambda m: _scan_chunk(step(m), rev, n_ctx // HG_STEP, n_steps)
    fcol = 2 if rev else 1
    has_add = other is not None
    assert not has_add or rev

    def body(q_ref, f_ref, i_ref, hlb_ref, do_ref, st_ref, *rest):
        if has_add:
            dqa_ref, dza_ref, dia_ref, _, out_ref, dlb_ref, ds_scr = rest
            dq_ref, dz_ref, di_ref = out_ref.at[:, 0:D], out_ref.at[:, 2 * D:3 * D], out_ref.at[:, 3 * D:4 * D]
            out_ref[:, D:2 * D] = dza_ref[...]
        else:
            dq_ref, dz_ref, di_ref, dlb_ref, ds_scr = rest
        m = pl.program_id(0)

        @pl.when(m == 0)
        def _():
            ds_scr[...] = jnp.zeros_like(ds_scr)
            dlb_ref[...] = jnp.zeros_like(dlb_ref)

        mask = _tri_mask(rev)
        hpg = min(nh, HG_HEADS)
        hs = [slice(h * HEAD, (h + 1) * HEAD) for h in range(hpg)]
        for j, g0 in [(j, g0) for j in range(HG_STEP) for g0 in range(0, nh, hpg)]:
            rows = _step_rows(j, rev, True)
            cols = slice(g0 * HEAD, (g0 + hpg) * HEAD)
            slot = HG_STEP - 1 - j
            lb, sig, fg, kk, b, bt, r, qh = _hgrn_gates(q_ref, f_ref, hlb_ref, layer, rev, rows, cols)
            e_qr = jnp.exp(b - r)
            e_kr = jnp.exp(r - b)
            e_b = jnp.exp(b)
            e_ke = jnp.exp(bt - b)
            dec = jnp.exp(bt)
            qr = (qh * e_qr).astype(BF16)
            kr = (kk * e_kr).astype(BF16)
            qe = (qh * e_b).astype(BF16)
            ke = (kk * e_ke).astype(BF16)
            v = i_ref[rows, cols].astype(BF16)
            dov = do_ref[rows, cols].astype(BF16)
            st = [st_ref[slot, g0 + h] for h in range(hpg)]
            dst = [ds_scr[g0 + h] for h in range(hpg)]
            stb = [t.astype(BF16) for t in st]
            dstb = [t.astype(BF16) for t in dst]
            a_raw = [_nt(qr[:, sl], kr[:, sl]) for sl in hs]
            da_raw = [_nt(dov[:, sl], v[:, sl]) for sl in hs]
            dq_int = [_nn(dov[:, sl], stb[h]) for h, sl in enumerate(hs)]
            dk_int = [_nn(v[:, sl], dstb[h]) for h, sl in enumerate(hs)]
            dv_int = [_nt(ke[:, sl], dstb[h]) for h, sl in enumerate(hs)]
            ds_new = [_tn(dov[:, sl], qe[:, sl]) for sl in hs]
            a = [jnp.where(mask, t, 0.0).astype(BF16) for t in a_raw]
            da = [jnp.where(mask, t, 0.0).astype(BF16) for t in da_raw]
            dv_parts = [_tn(a[h], dov[:, sl]) + dv_int[h] for h, sl in enumerate(hs)]
            dq_parts = [_nn(da[h], kr[:, sl]) * e_qr[:, sl] + dq_int[h] * e_b[:, sl] for h, sl in enumerate(hs)]
            dki_parts = [dk_int[h] * e_ke[:, sl] for h, sl in enumerate(hs)]
            dk_parts = [_tn(da[h], qr[:, sl]) * e_kr[:, sl] + dki_parts[h] for h, sl in enumerate(hs)]
            dbt_parts = [dec[:, sl] * jnp.sum(st[h] * dst[h], axis=0, keepdims=True) for h, sl in enumerate(hs)]
            for h, sl in enumerate(hs):
                ds_scr[g0 + h] = dst[h] * dec[:, sl] + ds_new[h]
            dq = jnp.concatenate(dq_parts, axis=1)
            dk = jnp.concatenate(dk_parts, axis=1)
            dki = jnp.concatenate(dki_parts, axis=1)
            dv = jnp.concatenate(dv_parts, axis=1)
            dbt = jnp.concatenate(dbt_parts, axis=1) + jnp.sum(kk * dki, axis=0, keepdims=True)
            db = qh * dq - kk * dk
            dg = _cumsum_rows(db, not rev) + dbt
            df = dg / fg - dk
            dz_ref[rows, cols] = (df * (1.0 - lb) * sig * (1.0 - sig)).astype(BF16)
            dlb_ref[:, cols] += jnp.sum(df * (1.0 - sig), axis=0, keepdims=True)
            dqr = dq * _dsilu(q_ref[rows, cols])
            if has_add:
                dqr = dqr + dqa_ref[rows, cols]
                dv = dv + dia_ref[rows, cols]
            dq_ref[rows, cols] = dqr.astype(dq_ref.dtype)
            di_ref[rows, cols] = dv.astype(di_ref.dtype)

        @pl.when(m == n_steps - 1)
        def _():
            lb = _lower_bound(hlb_ref, layer)
            if layer == 0:
                dlb_ref[...] = jnp.zeros_like(dlb_ref)
            else:
                dlb_ref[...] = dlb_ref[...] * lb * (1.0 - lb)

    cspec = lambda col: pl.BlockSpec((HG_STEP * HG_CHUNK, D), lambda m: (block(m), col))
    ins = [parts, parts, parts, hlb, do, states]
    specs = [cspec(0), cspec(fcol), cspec(3), pl.BlockSpec((2, D), lambda m: (0, 1 if rev else 0)), cspec(0),
             pl.BlockSpec((HG_STEP, nh, HEAD, HEAD), lambda m: (step(m), 0, 0, 0))]
    dlb_spec = pl.BlockSpec((1, D), lambda m: (0, 0))
    dlb_shape = jax.ShapeDtypeStruct((1, D), F32)
    if has_add:
        return pl.pallas_call(
            body, name=name, grid=(n_steps,),
            in_specs=specs + [cspec(0), cspec(0), cspec(0), pl.BlockSpec(memory_space=pl.ANY)],
            out_specs=[pl.BlockSpec((HG_STEP * HG_CHUNK, 4 * D), lambda m: (block(m), 0)), dlb_spec],
            out_shape=[jax.ShapeDtypeStruct(dparts.shape, dparts.dtype), dlb_shape],
            scratch_shapes=[pltpu.VMEM((nh, HEAD, HEAD), F32)], input_output_aliases={len(ins) + 3: 0},
            compiler_params=_params("arbitrary"),
        )(*ins, *other, dparts)
    return pl.pallas_call(
        body, name=name, grid=(n_steps,), in_specs=specs,
        out_specs=[cspec(0), cspec(0), cspec(0), dlb_spec],
        out_shape=[jax.ShapeDtypeStruct((T, D), F32), jax.ShapeDtypeStruct((T, D), BF16),
                   jax.ShapeDtypeStruct((T, D), F32), dlb_shape],
        scratch_shapes=[pltpu.VMEM((nh, HEAD, HEAD), F32)],
        compiler_params=_params("arbitrary"),
    )(*ins)


def _sgu_ln(gv, lnw_ref, lnb_ref):
    mu = jnp.mean(gv, axis=-1, keepdims=True)
    xc = gv - mu
    rstd = lax.rsqrt(jnp.mean(xc * xc, axis=-1, keepdims=True) + LN_EPS)
    xh = xc * rstd
    return xh, rstd, xh * lnw_ref[...] + lnb_ref[...]


def _sgu_fwd(parts, lnw, lnb, w, bt, name):
    T = parts.shape[0]
    D = lnw.shape[1]
    G = D // HEAD

    def body(u_ref, v_ref, lnw_ref, lnb_ref, w_ref, bt_ref, ya_ref):
        gu = _gelu(u_ref[...])
        _, _, vn = _sgu_ln(_gelu(v_ref[...]), lnw_ref, lnb_ref)
        vnb = vn.astype(BF16)
        for g in range(G):
            sl = slice(g * HEAD, (g + 1) * HEAD)
            mixed = _nn(w_ref[g], vnb[:, sl]) + bt_ref[:, g:g + 1]
            ya_ref[:, sl] = (gu[:, sl] * mixed).astype(BF16)

    return pl.pallas_call(
        body, name=name, grid=(T // SGU_CHUNK,),
        in_specs=[pl.BlockSpec((SGU_CHUNK, D), lambda n: (n, 4)), pl.BlockSpec((SGU_CHUNK, D), lambda n: (n, 5)),
                  pl.BlockSpec((1, D), lambda n: (0, 0)), pl.BlockSpec((1, D), lambda n: (0, 0)),
                  pl.BlockSpec((G, SGU_CHUNK, SGU_CHUNK), lambda n: (0, 0, 0)),
                  pl.BlockSpec((SGU_CHUNK, G), lambda n: (0, 0))],
        out_specs=pl.BlockSpec((SGU_CHUNK, D), lambda n: (n, 0)),
        out_shape=jax.ShapeDtypeStruct((T, D), BF16),
        compiler_params=_params("parallel"),
    )(parts, parts, lnw, lnb, w, bt)


def _sgu_bwd(parts, dya, lnw, lnb, w, bt, dparts, name):
    T = parts.shape[0]
    D = lnw.shape[1]
    G = D // HEAD

    def body(u_ref, v_ref, dya_ref, lnw_ref, lnb_ref, w_ref, bt_ref, dparts_in,
             duv_ref, dw_ref, dbt_ref, dlnw_ref, dlnb_ref, dvn_scr):
        du_ref = duv_ref.at[:, 0:D]
        dv_ref = duv_ref.at[:, D:2 * D]
        n = pl.program_id(0)

        @pl.when(n == 0)
        def _():
            dw_ref[...] = jnp.zeros_like(dw_ref)
            dbt_ref[...] = jnp.zeros_like(dbt_ref)
            dlnw_ref[...] = jnp.zeros_like(dlnw_ref)
            dlnb_ref[...] = jnp.zeros_like(dlnb_ref)

        gu, dgu = _gelu_both(u_ref[...])
        gv, dgv_dv = _gelu_both(v_ref[...])
        xh, rstd, vn = _sgu_ln(gv, lnw_ref, lnb_ref)
        vnb = vn.astype(BF16)
        dya = dya_ref[...]
        lane = lax.broadcasted_iota(jnp.int32, (SGU_CHUNK, G), 1)
        dbt = jnp.zeros((SGU_CHUNK, G), F32)
        for g in range(G):
            sl = slice(g * HEAD, (g + 1) * HEAD)
            wg = w_ref[g]
            mixed = _nn(wg, vnb[:, sl]) + bt_ref[:, g:g + 1]
            dmix = dya[:, sl] * gu[:, sl]
            du_ref[:, sl] = (dya[:, sl] * mixed * dgu[:, sl]).astype(BF16)
            dmb = dmix.astype(BF16)
            dvn_scr[:, sl] = _tn(wg, dmb)
            dw_ref[g] += _nt(dmb, vnb[:, sl])
            dbt = dbt + jnp.where(lane == g, jnp.sum(dmix, axis=1, keepdims=True), 0.0)
        dbt_ref[...] += dbt
        dvn = dvn_scr[...]
        dlnw_ref[...] += jnp.sum(dvn * xh, axis=0, keepdims=True)
        dlnb_ref[...] += jnp.sum(dvn, axis=0, keepdims=True)
        dxh = dvn * lnw_ref[...]
        dgv = rstd * (dxh - jnp.mean(dxh, axis=-1, keepdims=True) - xh * jnp.mean(dxh * xh, axis=-1, keepdims=True))
        dv_ref[...] = (dgv * dgv_dv).astype(BF16)

    row = lambda col: pl.BlockSpec((SGU_CHUNK, D), lambda n: (n, col))
    vec = pl.BlockSpec((1, D), lambda n: (0, 0))
    wsp = pl.BlockSpec((G, SGU_CHUNK, SGU_CHUNK), lambda n: (0, 0, 0))
    bsp = pl.BlockSpec((SGU_CHUNK, G), lambda n: (0, 0))
    return pl.pallas_call(
        body, name=name, grid=(T // SGU_CHUNK,),
        in_specs=[row(4), row(5), row(0), vec, vec, wsp, bsp, pl.BlockSpec(memory_space=pl.ANY)],
        out_specs=[pl.BlockSpec((SGU_CHUNK, 2 * D), lambda n: (n, 2)), wsp, bsp, vec, vec],
        out_shape=[jax.ShapeDtypeStruct(dparts.shape, dparts.dtype),
                   jax.ShapeDtypeStruct((G, SGU_CHUNK, SGU_CHUNK), F32), jax.ShapeDtypeStruct((SGU_CHUNK, G), F32),
                   jax.ShapeDtypeStruct((1, D), F32), jax.ShapeDtypeStruct((1, D), F32)],
        scratch_shapes=[pltpu.VMEM((SGU_CHUNK, D), F32)], input_output_aliases={7: 0},
        compiler_params=_params("arbitrary"),
    )(parts, parts, dya, lnw, lnb, w, bt, dparts)


TBT = 256
VMEM_LIMIT_TOKEN_OUT = 58 * 1024 * 1024


def _rows_weight_spec(wg):
    return pl.BlockSpec(wg.shape, lambda i: (0, 0, 0))


def _full(w_ref):
    return w_ref[...].reshape(w_ref.shape[0] * w_ref.shape[1], w_ref.shape[2])


def _token_out_fwd(o, parts, ya, x, mod, hnw, nw2, wa, wb, wo, ctx_rows, name):
    T, D = x.shape
    nh = D // HEAD
    cb = ctx_rows // TBT

    def body(of_ref, ob_ref, og_ref, ga_ref, gb_ref, ya_ref, x_ref, mod_ref, hnw_ref, nw2_ref, wa_ref, wb_ref, wo_ref,
             yb_ref, pa_ref, pb_ref, mg_ref, tmo_ref, xm_ref, h2_ref):
        ov = of_ref[...] + ob_ref[...]
        so = _silu(og_ref[...])
        nw = hnw_ref[...]
        for h in range(nh):
            sl = slice(h * HEAD, (h + 1) * HEAD)
            seg = ov[:, sl]
            r = lax.rsqrt(jnp.mean(seg * seg, axis=-1, keepdims=True) + RMS_EPS)
            yb_ref[:, sl] = (seg * r * nw * so[:, sl]).astype(BF16)
        pa = _nn(ya_ref[...], _full(wa_ref))
        pb = _nn(yb_ref[...], _full(wb_ref))
        pa_ref[...] = pa.astype(BF16)
        pb_ref[...] = pb.astype(BF16)
        mg = (_sigmoid(ga_ref[...]) * pa + _sigmoid(gb_ref[...]) * pb).astype(BF16)
        mg_ref[...] = mg
        out = _nn(mg, _full(wo_ref))
        tmo_ref[...] = out.astype(BF16)
        xm = x_ref[...] + mod_ref[2:3, :] * out
        xm_ref[...] = xm
        r = lax.rsqrt(jnp.mean(xm * xm, axis=-1, keepdims=True) + RMS_EPS)
        h2_ref[...] = (xm * r * nw2_ref[...] * (1.0 + mod_ref[4:5, :]) + mod_ref[3:4, :]).astype(BF16)

    row = lambda col: pl.BlockSpec((TBT, D), lambda i: (i, col))
    wsp = _rows_weight_spec(wa)
    sd = lambda dt: jax.ShapeDtypeStruct((T, D), dt)
    return pl.pallas_call(
        body, name=name, grid=(T // TBT,),
        in_specs=[row(0), row(0), row(6), row(7), row(8), row(0), row(0),
                  pl.BlockSpec((None, N_MOD, D), lambda i: (_stream_of(i, cb), 0, 0)),
                  pl.BlockSpec((1, HEAD), lambda i: (0, 0)), pl.BlockSpec((1, D), lambda i: (0, 0)), wsp, wsp, wsp],
        out_specs=[row(0)] * 7,
        out_shape=[sd(BF16), sd(BF16), sd(BF16), sd(BF16), sd(BF16), sd(F32), sd(BF16)],
        compiler_params=_params("parallel", vmem=VMEM_LIMIT_TOKEN_OUT),
    )(o[0], o[1], parts, parts, parts, ya, x, mod, hnw, nw2, wa, wb, wo)


def _token_out_bwd(dx, tmo, pa, pb, o, parts, mod, hnw, wa, wb, wo, ctx_rows, name):
    T, D = dx.shape
    nh = D // HEAD
    cb = ctx_rows // TBT

    def body(dx_ref, tmo_ref, pa_ref, pb_ref, of_ref, ob_ref, og_ref, ga_ref, gb_ref, mod_ref, hnw_ref, wa_ref, wb_ref,
             wo_ref, dout_ref, dpa_ref, dpb_ref, dgate_ref, dya_ref, do_ref, dg1_ref, dhnw_ref):
        i = pl.program_id(0)

        @pl.when(i == 0)
        def _():
            dhnw_ref[...] = jnp.zeros_like(dhnw_ref)

        @pl.when((i == 0) | (i == cb))
        def _():
            dg1_ref[...] = jnp.zeros_like(dg1_ref)

        dxv = dx_ref[...]
        dg1_ref[...] += jnp.sum(dxv * tmo_ref[...], axis=0, keepdims=True)
        dout = (dxv * mod_ref[2:3, :]).astype(BF16)
        dout_ref[...] = dout
        dmg = _nt(dout, _full(wo_ref))
        sa = _sigmoid(ga_ref[...])
        sb = _sigmoid(gb_ref[...])
        dpa = (dmg * sa).astype(BF16)
        dpb = (dmg * sb).astype(BF16)
        dpa_ref[...] = dpa
        dpb_ref[...] = dpb
        dgate_ref[:, D:2 * D] = (dmg * pa_ref[...] * sa * (1.0 - sa)).astype(BF16)
        dgate_ref[:, 2 * D:3 * D] = (dmg * pb_ref[...] * sb * (1.0 - sb)).astype(BF16)
        dya_ref[...] = _nt(dpa, _full(wa_ref))
        dyb = _nt(dpb, _full(wb_ref))
        so, dso = _silu_both(og_ref[...])
        ov = of_ref[...] + ob_ref[...]
        nw = hnw_ref[...]
        dnw = jnp.zeros((1, HEAD), F32)
        for h in range(nh):
            sl = slice(h * HEAD, (h + 1) * HEAD)
            seg = ov[:, sl]
            r = lax.rsqrt(jnp.mean(seg * seg, axis=-1, keepdims=True) + RMS_EPS)
            oh = seg * r
            dn = dyb[:, sl] * so[:, sl]
            dgate_ref[:, sl] = (dyb[:, sl] * oh * nw * dso[:, sl]).astype(BF16)
            dnw = dnw + jnp.sum(dn * oh, axis=0, keepdims=True)
            doh = dn * nw
            do_ref[:, sl] = (r * (doh - oh * jnp.mean(doh * oh, axis=-1, keepdims=True))).astype(BF16)
        dhnw_ref[...] += dnw

    row = lambda col: pl.BlockSpec((TBT, D), lambda i: (i, col))
    wsp = _rows_weight_spec(wa)
    sd = lambda dt: jax.ShapeDtypeStruct((T, D), dt)
    return pl.pallas_call(
        body, name=name, grid=(T // TBT,),
        in_specs=[row(0), row(0), row(0), row(0), row(0), row(0), row(6), row(7), row(8),
                  pl.BlockSpec((None, N_MOD, D), lambda i: (_stream_of(i, cb), 0, 0)),
                  pl.BlockSpec((1, HEAD), lambda i: (0, 0)), wsp, wsp, wsp],
        out_specs=[row(0)] * 3 + [pl.BlockSpec((TBT, 3 * D), lambda i: (i, 2)), row(0), row(0),
                                  pl.BlockSpec((None, 1, D), lambda i: (_stream_of(i, cb), 0, 0)),
                                  pl.BlockSpec((1, HEAD), lambda i: (0, 0))],
        out_shape=[sd(BF16)] * 3 + [jax.ShapeDtypeStruct((T, 9 * D), BF16), sd(F32), sd(BF16),
                                    jax.ShapeDtypeStruct((2, 1, D), F32), jax.ShapeDtypeStruct((1, HEAD), F32)],
        compiler_params=_params("arbitrary", vmem=VMEM_LIMIT_TOKEN_OUT),
    )(dx, tmo, pa, pb, o[0], o[1], parts, parts, parts, mod, hnw, wa, wb, wo)


def _conv_geometry(i, nb, cb):
    is_ctx = i < cb
    first = (i == 0) | (i == cb)
    last = (i == cb - 1) | (i == nb - 1)
    row = lax.broadcasted_iota(jnp.int32, (TB + 2 * GRID_W, 1), 0)
    w = row & (GRID_W - 1)
    left_ok = (w != 0) | is_ctx
    right_ok = (w != GRID_W - 1) | is_ctx
    return is_ctx, first, last, left_ok, right_ok


def _ext(p_ref, m_ref, n_ref, first, last):
    return jnp.concatenate([jnp.where(first, 0.0, p_ref[...]), m_ref[...], jnp.where(last, 0.0, n_ref[...])], axis=0)


def _shift_prev(e, ok):
    return jnp.where(ok, pltpu.roll(e, 1, 0), 0.0)


def _shift_next(e, ok):
    return jnp.where(ok, pltpu.roll(e, e.shape[0] - 1, 0), 0.0)


def _halo_specs(cbk, n64, coff=0):
    r = TB // GRID_W
    prev = pl.BlockSpec((GRID_W, cbk), lambda j, i: (jnp.maximum(r * i - 1, 0), j + coff))
    main = pl.BlockSpec((TB, cbk), lambda j, i: (i, j + coff))
    nxt = pl.BlockSpec((GRID_W, cbk), lambda j, i: (jnp.minimum(r * i + r, n64 - 1), j + coff))
    return [prev, main, nxt]


def _conv_cblock(dff):
    return _tile(dff, 1408)


def _conv_fwd(up, cw, cbias, ctx_rows, name):
    T, dff = up.shape[0], up.shape[1] // 2
    cbk = _conv_cblock(dff)
    nb, cb = T // TB, ctx_rows // TB
    nvb = dff // cbk

    def body(ap_ref, a_ref, an_ref, v_ref, cw_ref, cb_ref, ac_ref, act_ref):
        i = pl.program_id(1)
        is_ctx, first, last, lok, rok = _conv_geometry(i, nb, cb)
        e = _ext(ap_ref, a_ref, an_ref, first, last)
        el = _shift_prev(e, lok)
        er = _shift_next(e, rok)
        cwv = cw_ref[...]

        def comb(dr, lo):
            sl = slice(lo, lo + TB)
            return cwv[3 * dr:3 * dr + 1] * el[sl] + cwv[3 * dr + 1:3 * dr + 2] * e[sl] + cwv[3 * dr + 2:3 * dr + 3] * er[sl]

        out = comb(1, GRID_W) + jnp.where(is_ctx, 0.0, comb(0, 0) + comb(2, 2 * GRID_W))
        a_c = out + cb_ref[...]
        ac_ref[...] = a_c
        act_ref[...] = (_gelu(a_c) * v_ref[...]).astype(BF16)

    main = pl.BlockSpec((TB, cbk), lambda j, i: (i, j))
    return pl.pallas_call(
        body, name=name, grid=(dff // cbk, nb),
        in_specs=_halo_specs(cbk, T // GRID_W) + [pl.BlockSpec((TB, cbk), lambda j, i: (i, j + nvb)),
                                                 pl.BlockSpec((9, cbk), lambda j, i: (0, j)),
                                                 pl.BlockSpec((1, cbk), lambda j, i: (0, j))],
        out_specs=[main, main],
        out_shape=[jax.ShapeDtypeStruct((T, dff), F32), jax.ShapeDtypeStruct((T, dff), BF16)],
        compiler_params=_params("parallel", "parallel"),
    )(up, up, up, up, cw, cbias)


def _conv_bwd(up, ac, dact, cw, ctx_rows, name):
    T, dff = up.shape[0], up.shape[1] // 2
    cbk = _conv_cblock(dff)
    nb, cb = T // TB, ctx_rows // TB
    nvb = dff // cbk

    def body(ap_ref, a_ref, an_ref, vp_ref, v_ref, vn_ref, cp_ref, c_ref, cn_ref, dp_ref, d_ref, dn_ref, cw_ref,
             da_ref, dv_ref, dcw_ref, dcb_ref):
        i = pl.program_id(1)

        @pl.when(i == 0)
        def _():
            dcw_ref[...] = jnp.zeros_like(dcw_ref)
            dcb_ref[...] = jnp.zeros_like(dcb_ref)

        is_ctx, first, last, lok, rok = _conv_geometry(i, nb, cb)
        gl, dgl = _gelu_both(_ext(cp_ref, c_ref, cn_ref, first, last))
        g = _ext(dp_ref, d_ref, dn_ref, first, last) * _ext(vp_ref, v_ref, vn_ref, first, last) * dgl
        dv_ref[...] = (d_ref[...] * gl[GRID_W:GRID_W + TB]).astype(BF16)
        gm = _shift_prev(g, lok)
        gp = _shift_next(g, rok)
        cwv = cw_ref[...]

        def comb(dr, lo):
            sl = slice(lo, lo + TB)
            return cwv[3 * dr:3 * dr + 1] * gp[sl] + cwv[3 * dr + 1:3 * dr + 2] * g[sl] + cwv[3 * dr + 2:3 * dr + 3] * gm[sl]

        da = comb(1, GRID_W) + jnp.where(is_ctx, 0.0, comb(0, 2 * GRID_W) + comb(2, 0))
        da_ref[...] = da.astype(BF16)
        e = _ext(ap_ref, a_ref, an_ref, first, last)
        taps = [_shift_prev(e, lok), e, _shift_next(e, rok)]
        gmain = g[GRID_W:GRID_W + TB]
        dcb_ref[...] += jnp.sum(gmain, axis=0, keepdims=True)
        vert = jnp.where(is_ctx, 0.0, 1.0)
        for dr in range(3):
            sl = slice(dr * GRID_W, dr * GRID_W + TB)
            for dw in range(3):
                s = jnp.sum(gmain * taps[dw][sl], axis=0, keepdims=True)
                if dr != 1:
                    s = s * vert
                k = 3 * dr + dw
                dcw_ref[k:k + 1, :] += s

    main = pl.BlockSpec((TB, cbk), lambda j, i: (i, j))
    halo = _halo_specs(cbk, T // GRID_W)
    acc9 = pl.BlockSpec((9, cbk), lambda j, i: (0, j))
    acc1 = pl.BlockSpec((1, cbk), lambda j, i: (0, j))
    return pl.pallas_call(
        body, name=name, grid=(dff // cbk, nb),
        in_specs=halo + _halo_specs(cbk, T // GRID_W, nvb) + halo + halo + [acc9],
        out_specs=[main, main, acc9, acc1],
        out_shape=[jax.ShapeDtypeStruct((T, dff), BF16), jax.ShapeDtypeStruct((T, dff), BF16),
                   jax.ShapeDtypeStruct((9, dff), F32), jax.ShapeDtypeStruct((1, dff), F32)],
        compiler_params=_params("parallel", "arbitrary"),
    )(up, up, up, up, up, up, ac, ac, ac, dact, dact, dact, cw)


def _ffn_out_fwd(act, xm, mod, wd, ctx_rows, name):
    T, D = xm.shape
    dff = act.shape[1]
    cb = ctx_rows // TB

    def body(act_ref, x_ref, mod_ref, w_ref, xo_ref, fo_ref):
        out = _nn(act_ref[...], _full(w_ref))
        fo_ref[...] = out.astype(BF16)
        xo_ref[...] = x_ref[...] + mod_ref[5:6, :] * out

    row = pl.BlockSpec((TB, D), lambda i: (i, 0))
    return pl.pallas_call(
        body, name=name, grid=(T // TB,),
        in_specs=[pl.BlockSpec((TB, dff), lambda i: (i, 0)), row,
                  pl.BlockSpec((None, N_MOD, D), lambda i: (_stream_of(i, cb), 0, 0)),
                  _rows_weight_spec(wd)],
        out_specs=[row, row],
        out_shape=[jax.ShapeDtypeStruct((T, D), F32), jax.ShapeDtypeStruct((T, D), BF16)],
        compiler_params=_params("parallel"),
    )(act, xm, mod, wd)


def _ffn_out_bwd(dx, fo, mod, wd, ctx_rows, name):
    T, D = dx.shape
    dff = N_CHIPS * wd.shape[1]
    cb = ctx_rows // TB

    def body(dx_ref, fo_ref, mod_ref, w_ref, dout_ref, dact_ref, dg2_ref):
        i = pl.program_id(0)

        @pl.when((i == 0) | (i == cb))
        def _():
            dg2_ref[...] = jnp.zeros_like(dg2_ref)

        dxv = dx_ref[...]
        dg2_ref[...] += jnp.sum(dxv * fo_ref[...], axis=0, keepdims=True)
        dout = (dxv * mod_ref[5:6, :]).astype(BF16)
        dout_ref[...] = dout
        dact_ref[...] = _nt(dout, _full(w_ref))

    row = pl.BlockSpec((TB, D), lambda i: (i, 0))
    return pl.pallas_call(
        body, name=name, grid=(T // TB,),
        in_specs=[row, row, pl.BlockSpec((None, N_MOD, D), lambda i: (_stream_of(i, cb), 0, 0)),
                  _rows_weight_spec(wd)],
        out_specs=[row, pl.BlockSpec((TB, dff), lambda i: (i, 0)),
                   pl.BlockSpec((None, 1, D), lambda i: (_stream_of(i, cb), 0, 0))],
        out_shape=[jax.ShapeDtypeStruct((T, D), BF16), jax.ShapeDtypeStruct((T, dff), F32),
                   jax.ShapeDtypeStruct((2, 1, D), F32)],
        compiler_params=_params("arbitrary"),
    )(dx, fo, mod, wd)


def _loss_bwd(x, target, fw, ctx_rows, name):
    T, D = x.shape
    cb = ctx_rows // TB

    def body(x_ref, t_ref, fw_ref, dx_ref, loss_ref, dfw_ref):
        i = pl.program_id(0)

        @pl.when(i == 0)
        def _():
            loss_ref[...] = jnp.zeros_like(loss_ref)
            dfw_ref[...] = jnp.zeros_like(dfw_ref)

        @pl.when(i < cb)
        def _():
            dx_ref[...] = jnp.zeros_like(dx_ref)

        @pl.when(i >= cb)
        def _():
            xv = x_ref[...]
            r = lax.rsqrt(jnp.mean(xv * xv, axis=-1, keepdims=True) + RMS_EPS)
            xh = xv * r
            fwv = fw_ref[...]
            err = xh * fwv - t_ref[...]
            loss_ref[...] += (0.5 / D) * jnp.sum(err * err)
            dy = err * (1.0 / D)
            dfw_ref[...] += jnp.sum(dy * xh, axis=0, keepdims=True)
            dxh = dy * fwv
            dx_ref[...] = r * (dxh - xh * jnp.mean(dxh * xh, axis=-1, keepdims=True))

    row = pl.BlockSpec((TB, D), lambda i: (i, 0))
    return pl.pallas_call(
        body, name=name, grid=(T // TB,),
        in_specs=[row, pl.BlockSpec((TB, D), lambda i: (jnp.maximum(i - cb, 0), 0)), pl.BlockSpec((1, D), lambda i: (0, 0))],
        out_specs=[row, pl.BlockSpec((1, 128), lambda i: (0, 0)), pl.BlockSpec((1, D), lambda i: (0, 0))],
        out_shape=[jax.ShapeDtypeStruct((T, D), F32), jax.ShapeDtypeStruct((1, 128), F32),
                   jax.ShapeDtypeStruct((1, D), F32)],
        compiler_params=_params("arbitrary"),
    )(x, target, fw)


def _adamw(w, gs, m, v, name):
    L, R, C = w.shape
    assert len(gs) == L
    rb = _rows_tile(R, max(16, (1 << 19) // C // 16 * 16))
    bc1 = 1.0 - ADAM_B1 ** ADAM_STEP
    bc2 = 1.0 - ADAM_B2 ** ADAM_STEP

    def body(w_ref, m_ref, v_ref, *rest):
        g_refs, (g_ref, d_ref, nm_ref, nv_ref) = rest[:L], rest[L:]
        layer = pl.program_id(0)
        for li in range(L):
            @pl.when(layer == li)
            def _():
                gv = g_refs[li][...]
                g_ref[...] = gv
                nm = ADAM_B1 * m_ref[...] + (1.0 - ADAM_B1) * gv
                nv = ADAM_B2 * v_ref[...] + (1.0 - ADAM_B2) * (gv * gv)
                nm_ref[...] = nm
                nv_ref[...] = nv
                d_ref[...] = -ADAM_LR * ((nm / bc1) / (jnp.sqrt(nv / bc2) + ADAM_EPS) + ADAM_WD * w_ref[...])

    blk = pl.BlockSpec((None, rb, C), lambda l, i: (l, i, 0))
    gblk = pl.BlockSpec((rb, C), lambda l, i: (i, 0))
    sd = jax.ShapeDtypeStruct((L, R, C), F32)
    return pl.pallas_call(
        body, name=name, grid=(L, R // rb), in_specs=[blk] * 3 + [gblk] * L, out_specs=[blk] * 4, out_shape=[sd] * 4,
        compiler_params=_params("parallel", "parallel"),
    )(w, m, v, *gs)


def _local_step(xs, cv, target, W, layer_weights, on_layer_grads, ctx_rows):
    T, D = xs.shape
    depth = W["norm1_w"].shape[0]
    saved = []
    X = xs
    for l in range(depth):
        s = {}
        Wl = layer_weights(l, X)
        mod_all, sa = _mod_fwd(cv, Wl["ada_w"], W["ada_b"][l][None, :] + Wl["token"], f"mod_fwd_{l}")
        mod = mod_all[:2].reshape(2, N_MOD, D)
        h1 = _norm_mod(X, W["norm1_w"][l][None, :], mod, 0, ctx_rows, f"norm1_{l}")
        parts = _mm_nn_w(h1, Wl["w_in"], F32, f"in_proj_{l}")
        o_f, o_b, st_f, st_b = _hgrn_fwd_both(parts, W["hlb"], l, ctx_rows, f"hgrn_fwd_{l}")
        o = (o_f, o_b)
        ya = _sgu_fwd(parts, W["sgu_ln_w"][l][None, :], W["sgu_ln_b"][l][None, :], W["sgu_w"][l], W["sgu_bt"][l],
                      f"sgu_fwd_{l}")
        Wl.update(Wl.pop("late")(ya))
        yb, pa, pb, mg, tmo, xm, h2 = _token_out_fwd(o, parts, ya, X, mod, W["hnw"][l][None, :] + Wl["late_token"],
                                                     W["norm2_w"][l][None, :], Wl["w_a"], Wl["w_b"], Wl["w_o"], ctx_rows,
                                                     f"token_out_fwd_{l}")
        up = _mm_nn_w(h2, Wl["w_up"], F32, f"up_proj_{l}")
        ac, act = _conv_fwd(up, Wl["conv_w"], W["conv_b"][l][None, :], ctx_rows, f"conv_fwd_{l}")
        xo, fo = _ffn_out_fwd(act, xm, mod, Wl["w_down"], ctx_rows, f"ffn_out_fwd_{l}")
        s.update(X=X, Wl=Wl, mod=mod, mod_all=mod_all, sa=sa, h1=h1, parts=parts, o=o, st_f=st_f, st_b=st_b, ya=ya, yb=yb,
                 pa=pa, pb=pb, mg=mg, tmo=tmo, xm=xm, h2=h2, up=up, ac=ac, act=act, fo=fo)
        saved.append(s)
        X = xo

    dX, loss_row, dfw = _loss_bwd(X, target, W["final_norm_w"][None, :], ctx_rows, "loss_bwd")
    G = {k: [None] * depth for k in ("ada_b", "norm1_w", "sgu_ln_w", "sgu_ln_b", "sgu_w", "sgu_b", "hlb1", "hnw", "norm2_w",
                                     "conv_w", "conv_b", "dmod")}
    dcv = jnp.zeros_like(cv)
    for l in reversed(range(depth)):
        s = saved[l]
        mod, Wl = s["mod"], s["Wl"]
        big = {}
        dout2, dact, dg2 = _ffn_out_bwd(dX, s["fo"], mod, Wl["w_down"], ctx_rows, f"ffn_out_bwd_{l}")
        big["w_down"] = _mm_tn(s["act"], dout2, F32, f"dw_down_{l}")
        da, dv, dcw, dcb = _conv_bwd(s["up"], s["ac"], dact, Wl["conv_w"], ctx_rows, f"conv_bwd_{l}")
        G["conv_w"][l], G["conv_b"][l] = dcw, dcb[0]
        big["w_up"] = _mm_tn(s["h2"], (da, dv), F32, f"dw_up_{l}", out_chips=True)
        dh2 = _mm_nt_w((da, dv), Wl["w_up"], F32, f"dh2_{l}")
        dxm, dm2, dnw2 = _norm_mod_bwd(dh2, s["xm"], dX, W["norm2_w"][l][None, :], mod, 3, ctx_rows, f"norm2_bwd_{l}")
        G["norm2_w"][l] = dnw2[0]
        (dout1, dpa, dpb, dparts, dya, do, dg1, dhnw) = _token_out_bwd(
            dxm, s["tmo"], s["pa"], s["pb"], s["o"], s["parts"], mod, W["hnw"][l][None, :], Wl["w_a"], Wl["w_b"], Wl["w_o"],
            ctx_rows, f"token_out_bwd_{l}")
        G["hnw"][l] = dhnw[0]
        big["w_o"] = _mm_tn(s["mg"], dout1, F32, f"dw_o_{l}")
        big["w_a"] = _mm_tn(s["ya"], dpa, F32, f"dw_a_{l}")
        big["w_b"] = _mm_tn(s["yb"], dpb, F32, f"dw_b_{l}")
        tok = on_layer_grads(l, "early", big)
        dparts, dsw, dsbt, dlnw, dlnb = _sgu_bwd(s["parts"], dya, W["sgu_ln_w"][l][None, :], W["sgu_ln_b"][l][None, :] + tok,
                                                 W["sgu_w"][l], W["sgu_bt"][l], dparts, f"sgu_bwd_{l}")
        G["sgu_w"][l], G["sgu_b"][l], G["sgu_ln_w"][l], G["sgu_ln_b"][l] = dsw, dsbt.T, dlnw[0], dlnb[0]
        dq_f, dz_f, di_f, dlb_f = _hgrn_bwd(s["parts"], W["hlb"], do, s["st_f"], l, False, ctx_rows, f"hgrn_bwd_f_{l}")
        dparts, dlb_b = _hgrn_bwd(s["parts"], W["hlb"], do, s["st_b"], l, True, ctx_rows, f"hgrn_bwd_b_{l}",
                                  other=(dq_f, dz_f, di_f), dparts=dparts)
        G["hlb1"][l] = jnp.concatenate([dlb_f[0], dlb_b[0]])
        tok = on_layer_grads(l, "late", {"w_in": _mm_tn(s["h1"], dparts, F32, f"dw_in_{l}", out_chips=True)})
        dh1 = _mm_nt_w(dparts, Wl["w_in"], F32, f"dh1_{l}")
        tok = tok + on_layer_grads(l, "end", {"after": dh1})
        dX, dm1, dnw1 = _norm_mod_bwd(dh1, s["X"], dxm, W["norm1_w"][l][None, :] + tok, mod, 0, ctx_rows, f"norm1_bwd_{l}")
        G["norm1_w"][l] = dnw1[0]
        dmod = jnp.concatenate([dm1, dg1, dm2, dg2], axis=1).reshape(2, N_MOD * D)
        dmod16 = jnp.concatenate([dmod, jnp.zeros((cv.shape[0] - 2, N_MOD * D), F32)], axis=0)
        G["ada_b"][l] = dmod[0] + dmod[1]
        G["dmod"][l] = dmod
        dcv = dcv + _cvec_bwd(dmod16, Wl["ada_w"], cv, f"dcvec_{l}")
    G["c_ctx"] = dcv[0]
    G["final_norm_w"] = dfw[0]
    return loss_row[0, 0], dX, G, saved[0]["sa"]


def _chip_peers(x, y, c):
    return [((1 - x, y, c), 2 * (1 - x) + y), ((x, 1 - y, c), 2 * x + 1 - y), ((1 - x, 1 - y, c), 2 * (1 - x) + 1 - y)]


def _rdma_call(ins, out_shapes, plan, n_remote, n_local, name, aliases=None):
    n_in, n_out = len(ins), len(out_shapes)

    def body(*refs):
        in_refs, out_refs = refs[:n_in], refs[n_in:n_in + n_out]
        send_sems, recv_sems, local_sems = refs[n_in + n_out:]
        x, y, c = lax.axis_index("x"), lax.axis_index("y"), lax.axis_index("c")
        remote, local = plan(in_refs, out_refs, x, y, c)
        assert len(remote) == n_remote and len(local) == n_local, (name, len(remote), len(local))
        copies = [pltpu.make_async_copy(s, d, local_sems.at[i]) for i, (s, d) in enumerate(local)]
        copies += [pltpu.make_async_remote_copy(src_ref=s, dst_ref=d, send_sem=send_sems.at[k], recv_sem=recv_sems.at[k],
                                                device_id=dev, device_id_type=pl.DeviceIdType.MESH)
                   for k, (s, d, dev) in enumerate(remote)]
        for cp in copies:
            cp.start()
        for cp in copies:
            cp.wait()

    hbm = pl.BlockSpec(memory_space=pltpu.HBM)
    return pl.pallas_call(
        body, name=name, in_specs=[hbm] * n_in, out_specs=[hbm] * n_out, out_shape=out_shapes,
        scratch_shapes=[pltpu.SemaphoreType.DMA((n_remote,)), pltpu.SemaphoreType.DMA((n_remote,)),
                        pltpu.SemaphoreType.DMA((max(n_local, 1),))],
        input_output_aliases=aliases or {},
    )(*ins)


DMA_PIECE_BYTES = 1 << 18
DMA_MAX_PIECES = 8


def _row_pieces(shape, dtype):
    rows = shape[0]
    row_bytes = jnp.dtype(dtype).itemsize
    for d in shape[1:]:
        row_bytes *= d
    n = 1
    while n < DMA_MAX_PIECES and rows % (2 * n * 16) == 0 and rows * row_bytes // (2 * n) >= DMA_PIECE_BYTES:
        n *= 2
    return [(i * (rows // n), rows // n) for i in range(n)]


def _half_pieces(o, c):
    r2 = o.shape[1] // 2
    return [pl.ds(c * r2 + st, sz) for st, sz in _row_pieces((r2,) + o.shape[2:], o.dtype)]


def _n_half_pieces(arrays):
    return sum(len(_row_pieces((a.shape[1] // 2,) + a.shape[2:], a.dtype)) for a in arrays)


def _plan_gather_far(lands, x, y, c):
    me = 2 * x + y
    return [(o.at[me, rows], o.at[me, rows], dev) for dev, _ in _chip_peers(x, y, c) for o in lands
            for rows in _half_pieces(o, c)]


def _plan_gather_near(lands, x, y, c):
    return [(o.at[idx, rows], o.at[idx, rows], (x, y, 1 - c)) for _, idx in _chip_peers(x, y, c) for o in lands
            for rows in _half_pieces(o, c)]


def _gather_weights(lands, name):
    n = len(lands)
    n_far = (N_CHIPS - 1) * _n_half_pieces(lands)

    def body(*refs):
        outs = refs[n:2 * n]
        far_send, far_recv, near_send, near_recv = refs[2 * n:]
        x, y, c = lax.axis_index("x"), lax.axis_index("y"), lax.axis_index("c")
        mk = lambda plan, send, recv: [
            pltpu.make_async_remote_copy(src_ref=s, dst_ref=d, send_sem=send.at[k], recv_sem=recv.at[k], device_id=dev,
                                         device_id_type=pl.DeviceIdType.MESH)
            for k, (s, d, dev) in enumerate(plan(outs, x, y, c))]
        far, near = mk(_plan_gather_far, far_send, far_recv), mk(_plan_gather_near, near_send, near_recv)
        assert len(far) == n_far and len(near) == n_far
        for cp in far:
            cp.start()
        for k in range(n_far):
            far[k].wait_recv()
            near[k].start()
        for k in range(n_far):
            near[k].wait_recv()
        for cp in far + near:
            cp.wait_send()

    hbm = pl.BlockSpec(memory_space=pltpu.HBM)
    sems = pltpu.SemaphoreType.DMA((n_far,))
    return pl.pallas_call(
        body, name=name, in_specs=[hbm] * n, out_specs=[hbm] * n,
        out_shape=[jax.ShapeDtypeStruct(a.shape, a.dtype) for a in lands],
        scratch_shapes=[sems, sems, sems, sems], input_output_aliases={i: i for i in range(n)},
    )(*lands)


def _gather_all(v, name):
    def plan(ins, outs, x, y, c):
        (s,), (o,) = ins, outs
        me = 4 * x + 2 * y + c
        flip = lambda a, f: 1 - a if f else a
        remote = [(s, o.at[me], (flip(x, m & 4), flip(y, m & 2), flip(c, m & 1))) for m in range(1, 8)]
        return remote, [(s, o.at[me])]

    return _rdma_call([v], [jax.ShapeDtypeStruct((8,) + v.shape, v.dtype)], plan, 7, 1, name)[0]


def _plan_pair(ins, lands, x, y, c):
    return [(a.at[j, 1 - c, pl.ds(st, sz)], o.at[j, pl.ds(st, sz)], (x, y, 1 - c)) for a, o in zip(ins, lands)
            for j in range(N_CHIPS) for st, sz in _row_pieces(a.shape[2:], a.dtype)]


def _n_pair_copies(parts):
    return N_CHIPS * sum(len(_row_pieces(a.shape[2:], a.dtype)) for a in parts)


def _reduce_pair(parts, name):
    shapes = [jax.ShapeDtypeStruct((N_CHIPS,) + a.shape[2:], a.dtype) for a in parts]
    return _rdma_call(parts, shapes, lambda ins, outs, x, y, c: (_plan_pair(ins, outs, x, y, c), []),
                      _n_pair_copies(parts), 0, name)


def _plan_chips(ins, lands, x, y, c):
    me = 2 * x + y
    return [(a.at[idx, pl.ds(st, sz)], o.at[me, pl.ds(st, sz)], dev) for dev, idx in _chip_peers(x, y, c)
            for a, o in zip(ins, lands) for st, sz in _row_pieces(a.shape[1:], a.dtype)]


def _n_chips_copies(parts):
    return (N_CHIPS - 1) * sum(len(_row_pieces(a.shape[1:], a.dtype)) for a in parts)


def _gather_pair(halves, name):
    def plan(ins, outs, x, y, c):
        return [(o.at[c, pl.ds(st, sz)], o.at[c, pl.ds(st, sz)], (x, y, 1 - c)) for o in outs
                for st, sz in _row_pieces(o.shape[1:], o.dtype)], []

    shapes = [jax.ShapeDtypeStruct(a.shape, a.dtype) for a in halves]
    n_remote = sum(len(_row_pieces(a.shape[1:], a.dtype)) for a in halves)
    return _rdma_call(halves, shapes, plan, n_remote, 0, name, aliases={i: i for i in range(len(halves))})


def _split_start(ins, lands, plan, n_remote, name):
    n_buf = len(ins) + len(lands)

    def body(*refs):
        in_refs, land_refs = refs[:len(ins)], refs[len(ins):n_buf]
        send_sems, recv_sems, token = refs[n_buf], refs[n_buf + 1], refs[-1]
        x, y, c = lax.axis_index("x"), lax.axis_index("y"), lax.axis_index("c")
        remote = plan(in_refs, land_refs, x, y, c)
        assert len(remote) == n_remote, (name, len(remote))
        for k, (s, d, dev) in enumerate(remote):
            pltpu.make_async_remote_copy(src_ref=s, dst_ref=d, send_sem=send_sems.at[k], recv_sem=recv_sems.at[k],
                                         device_id=dev, device_id_type=pl.DeviceIdType.MESH).start()
        token[...] = jnp.zeros_like(token)

    hbm = pl.BlockSpec(memory_space=pltpu.HBM)
    sem = pl.BlockSpec(memory_space=pltpu.SEMAPHORE)
    bufs = list(ins) + list(lands)
    out = pl.pallas_call(
        body, name=name, in_specs=[hbm] * n_buf,
        out_specs=(sem, sem) + (hbm,) * n_buf + (pl.BlockSpec(memory_space=pltpu.VMEM),),
        out_shape=(pltpu.SemaphoreType.DMA((n_remote,)), pltpu.SemaphoreType.DMA((n_remote,)))
        + tuple(pltpu.HBM(a.shape, a.dtype) for a in bufs) + (jax.ShapeDtypeStruct((8, 128), F32),),
        input_output_aliases={i: 2 + i for i in range(n_buf)},
        compiler_params=pltpu.CompilerParams(has_side_effects=pltpu.SideEffectType.DATAFLOW_SIDE_EFFECTING),
    )(*[pltpu.with_memory_space_constraint(a, pltpu.HBM) for a in bufs])
    return dict(send=out[0], recv=out[1], ins=list(out[2:2 + len(ins)]), lands=list(out[2 + len(ins):2 + n_buf]),
                token=out[-1][0, 0], token_array=out[-1], plan=plan, n_remote=n_remote)


def _split_wait(st, after, name):
    n_in, n_buf = len(st["ins"]), len(st["ins"]) + len(st["lands"])
    plan, n_remote = st["plan"], st["n_remote"]

    def body(*refs):
        in_refs, land_refs = refs[:n_in], refs[n_in:n_buf]
        send_sems, recv_sems = refs[n_buf], refs[n_buf + 1]
        x, y, c = lax.axis_index("x"), lax.axis_index("y"), lax.axis_index("c")
        for k, (s, d, dev) in enumerate(plan(in_refs, land_refs, x, y, c)):
            cp = pltpu.make_async_remote_copy(src_ref=s, dst_ref=d, send_sem=send_sems.at[k], recv_sem=recv_sems.at[k],
                                              device_id=dev, device_id_type=pl.DeviceIdType.MESH)
            cp.wait_send()
            cp.wait_recv()

    hbm = pl.BlockSpec(memory_space=pltpu.HBM)
    sem = pl.BlockSpec(memory_space=pltpu.SEMAPHORE)
    bufs = st["ins"] + st["lands"]
    out = pl.pallas_call(
        body, name=name, in_specs=[hbm] * n_buf + [sem, sem, pl.BlockSpec(memory_space=pl.ANY)],
        out_specs=[hbm] * n_buf, out_shape=[pltpu.HBM(a.shape, a.dtype) for a in bufs],
        input_output_aliases={i: i for i in range(n_buf)},
        compiler_params=pltpu.CompilerParams(has_side_effects=pltpu.SideEffectType.DATAFLOW_SIDE_EFFECTING),
    )(*bufs, st["send"], st["recv"], after)
    return list(out[:n_in]), list(out[n_in:])


def _pair_forward(lands, name):
    shapes = [jax.ShapeDtypeStruct(a.shape, a.dtype) for a in lands]
    return _rdma_call(lands, shapes, lambda ins, outs, x, y, c: (_plan_gather_near(outs, x, y, c), []),
                      (N_CHIPS - 1) * _n_half_pieces(lands), 0, name, aliases={i: i for i in range(len(lands))})


def _sum_block_rows(r, C):
    return _rows_tile(r, max(16, (1 << 19) // C // 16 * 16))


def _sum_pair(a, recv, cidx, name):
    nch, _, r, C = a.shape
    rb = _sum_block_rows(r, C)

    def body(c_ref, a_ref, r_ref, o_ref):
        o_ref[...] = (a_ref[...] + r_ref[...]).astype(BF16)

    blk = pl.BlockSpec((None, rb, C), lambda j, i, c: (j, i, 0))
    return pl.pallas_call(
        body, name=name,
        grid_spec=pltpu.PrefetchScalarGridSpec(
            num_scalar_prefetch=1, grid=(nch, r // rb),
            in_specs=[pl.BlockSpec((None, None, rb, C), lambda j, i, c: (j, c[0], i, 0)), blk], out_specs=blk),
        out_shape=jax.ShapeDtypeStruct((nch, r, C), BF16),
        compiler_params=_params("parallel", "parallel"),
    )(cidx, a, recv)


def _sum_chips(mine, recv, ids, name):
    nch, r, C = recv.shape
    rb = _sum_block_rows(r, C)

    def body(ids_ref, m_ref, *rest):
        r_refs, o_ref = rest[:nch], rest[nch]
        chip = ids_ref[1]
        own = m_ref[...].astype(F32)
        acc = jnp.where(chip == 0, own, r_refs[0][...].astype(F32))
        for q in range(1, nch):
            acc = acc + jnp.where(chip == q, own, r_refs[q][...].astype(F32))
        o_ref[...] = acc

    def slot(q):
        return pl.BlockSpec((None, rb, C), lambda i, ids: (jnp.where(ids[1] == q, (q + 1) % nch, q), i, 0))

    return pl.pallas_call(
        body, name=name,
        grid_spec=pltpu.PrefetchScalarGridSpec(
            num_scalar_prefetch=1, grid=(r // rb,),
            in_specs=[pl.BlockSpec((None, rb, C), lambda i, ids: (ids[1], i, 0))] + [slot(q) for q in range(nch)],
            out_specs=pl.BlockSpec((None, rb, C), lambda i, ids: (ids[0], i, 0))),
        out_shape=jax.ShapeDtypeStruct((N_CORES, r, C), F32),
        compiler_params=_params("parallel"),
    )(ids, mine, *([recv] * nch))


PACK_COLS = 1024
_SHARDED = ("ada_w", "w_in", "w_branch_a", "w_branch_b", "w_out", "ffn_w_up", "ffn_w_down")
_LAYER_KEYS = ("ada_w", "w_in", "w_a", "w_b", "w_o", "w_up", "w_down")
_SMALL = ("c_ctx", "ada_b", "norm1_w", "sgu_ln_w", "sgu_ln_b", "sgu_w", "sgu_b", "hgrn_lower_bounds", "hgrn_norm_w",
          "norm2_w", "ffn_conv_b", "final_norm_w")
_ORDER = ("c_ctx", "ada_w", "ada_b", "norm1_w", "w_in", "sgu_ln_w", "sgu_ln_b", "sgu_w", "sgu_b", "hgrn_lower_bounds",
          "hgrn_norm_w", "w_branch_a", "w_branch_b", "w_out", "norm2_w", "ffn_w_up", "ffn_conv_w", "ffn_conv_b",
          "ffn_w_down", "final_norm_w")


def _pad_to(v, n):
    return jnp.concatenate([v, jnp.zeros((n - v.shape[0],), v.dtype)]) if v.shape[0] < n else v


def _round_up(n, m):
    return (n + m - 1) // m * m


def _pack(arrays, n_pad):
    flat = jnp.concatenate([a.reshape(-1) for a in arrays])
    return _pad_to(flat, n_pad)


def _unpack(flat, like):
    out, off = [], 0
    for a in like:
        out.append(flat[off:off + a.size].reshape(a.shape))
        off += a.size
    return out


def kernel(x, c, ctx, c_ctx, ada_w, ada_b, norm1_w, w_in, sgu_ln_w, sgu_ln_b, sgu_w, sgu_b, hgrn_lower_bounds, hgrn_norm_w, w_branch_a, w_branch_b, w_out, norm2_w, ffn_w_up, ffn_conv_w, ffn_conv_b, ffn_w_down, final_norm_w, loss_target, m_c_ctx, m_ada_w, m_ada_b, m_norm1_w, m_w_in, m_sgu_ln_w, m_sgu_ln_b, m_sgu_w, m_sgu_b, m_hgrn_lower_bounds, m_hgrn_norm_w, m_w_branch_a, m_w_branch_b, m_w_out, m_norm2_w, m_ffn_w_up, m_ffn_conv_w, m_ffn_conv_b, m_ffn_w_down, m_final_norm_w, v_c_ctx, v_ada_w, v_ada_b, v_norm1_w, v_w_in, v_sgu_ln_w, v_sgu_ln_b, v_sgu_w, v_sgu_b, v_hgrn_lower_bounds, v_hgrn_norm_w, v_w_branch_a, v_w_branch_b, v_w_out, v_norm2_w, v_ffn_w_up, v_ffn_conv_w, v_ffn_conv_b, v_ffn_w_down, v_final_norm_w):
    w = dict(c_ctx=c_ctx, ada_w=ada_w, ada_b=ada_b, norm1_w=norm1_w, w_in=w_in, sgu_ln_w=sgu_ln_w, sgu_ln_b=sgu_ln_b,
             sgu_w=sgu_w, sgu_b=sgu_b, hgrn_lower_bounds=hgrn_lower_bounds, hgrn_norm_w=hgrn_norm_w, w_branch_a=w_branch_a,
             w_branch_b=w_branch_b, w_out=w_out, norm2_w=norm2_w, ffn_w_up=ffn_w_up, ffn_conv_w=ffn_conv_w,
             ffn_conv_b=ffn_conv_b, ffn_w_down=ffn_w_down, final_norm_w=final_norm_w)
    mom = dict(zip(_ORDER, (m_c_ctx, m_ada_w, m_ada_b, m_norm1_w, m_w_in, m_sgu_ln_w, m_sgu_ln_b, m_sgu_w, m_sgu_b,
                            m_hgrn_lower_bounds, m_hgrn_norm_w, m_w_branch_a, m_w_branch_b, m_w_out, m_norm2_w, m_ffn_w_up,
                            m_ffn_conv_w, m_ffn_conv_b, m_ffn_w_down, m_final_norm_w)))
    var = dict(zip(_ORDER, (v_c_ctx, v_ada_w, v_ada_b, v_norm1_w, v_w_in, v_sgu_ln_w, v_sgu_ln_b, v_sgu_w, v_sgu_b,
                            v_hgrn_lower_bounds, v_hgrn_norm_w, v_w_branch_a, v_w_branch_b, v_w_out, v_norm2_w, v_ffn_w_up,
                            v_ffn_conv_w, v_ffn_conv_b, v_ffn_w_down, v_final_norm_w)))
    depth, D = norm1_w.shape
    dff = ffn_conv_b.shape[1]
    ctx_rows = ctx.shape[1]

    assert depth == 2, "the lower-bound softmax is written for two layers"
    core = lax.axis_index("c")
    chip = 2 * lax.axis_index("x") + lax.axis_index("y")
    ids = jnp.stack([core, chip]).astype(jnp.int32)

    first, rest = _LAYER_KEYS[:2], _LAYER_KEYS[2:]
    shard = lambda l, k: w[_SHARDED[_LAYER_KEYS.index(k)]][l].astype(BF16)
    started, conv_full = {}, []

    def landing(s):
        return lax.dynamic_update_slice(lax.empty((N_CHIPS,) + s.shape, s.dtype), s[None], (chip,) + (0,) * s.ndim)

    def start_gather(l, keys, tag):
        lands = [landing(shard(l, k)) for k in keys]
        started[tag] = _split_start([], lands, lambda ins, lds, x, y, c: _plan_gather_far(lds, x, y, c),
                                    (N_CHIPS - 1) * _n_half_pieces(lands), f"gather_start_{tag}")
        return started[tag]["token"]

    def finish_gather(keys, tag, after):
        _, lands = _split_wait(started[tag], after, f"gather_wait_{tag}")
        return dict(zip(keys, _pair_forward(lands, f"gather_forward_{tag}")))

    def layer_weights(l, after):
        if l == 0:
            got = _gather_weights([landing(shard(0, k)) for k in first] + [landing(ffn_conv_w)], "gather_weights_first")
            conv_full.append(jnp.transpose(got[-1], (1, 2, 3, 0, 4)).reshape(depth, 9, dff))
            out = dict(zip(first, got), token=start_gather(0, rest, "rest_0"))
        else:
            out = dict(finish_gather(first, f"first_{l}", after), token=0.0)

        def late(after_late):
            more = finish_gather(rest, f"rest_{l}", after_late)
            more["late_token"] = 0.0
            if l + 1 < depth:
                more["late_token"] = start_gather(l + 1, first, f"first_{l + 1}") + start_gather(l + 1, rest, f"rest_{l + 1}")
            return more

        return dict(out, conv_w=conv_full[0][l], late=late)

    groups, order = {}, []

    def as_parts(gs):
        return [g.reshape(N_CHIPS, N_CORES, g.size // (N_CHIPS * N_CORES * g.shape[-1]), g.shape[-1]) for g in gs]

    def pair_start(tag, l, keys, gs):
        parts = as_parts(gs)
        lands = [lax.empty((N_CHIPS,) + p.shape[2:], p.dtype) for p in parts]
        groups[tag] = dict(l=l, keys=keys, pair=_split_start(parts, lands, _plan_pair, _n_pair_copies(parts),
                                                             f"reduce_pair_start_{tag}"))
        order.append(tag)
        return groups[tag]["pair"]["token"]

    def chips_start(tag, after):
        parts, other = _split_wait(groups[tag]["pair"], after, f"reduce_pair_wait_{tag}")
        sums = [_sum_pair(a, o, ids, f"sum_pair_{tag}_{i}") for i, (a, o) in enumerate(zip(parts, other))]
        lands = [lax.empty(s.shape, s.dtype) for s in sums]
        groups[tag]["chips"] = _split_start(sums, lands, _plan_chips, _n_chips_copies(sums), f"reduce_chips_start_{tag}")
        return groups[tag]["chips"]["token"]

    def chips_finish(tag, after):
        sums, recv = _split_wait(groups[tag]["chips"], after, f"reduce_chips_wait_{tag}")
        return {(groups[tag]["l"], k): _sum_chips(sums[i], recv[i], ids, f"sum_chips_{tag}_{i}")
                for i, k in enumerate(groups[tag]["keys"])}

    def on_layer_grads(l, stage, gs):
        if stage == "early":
            return pair_start(f"early_{l}", l, list(gs), list(gs.values()))
        if stage == "late":
            return pair_start(f"late_{l}", l, ["w_in"], [gs["w_in"]]) + chips_start(f"early_{l}", gs["w_in"])
        return chips_start(f"late_{l}", gs["after"])

    W = dict(ada_b=ada_b, norm1_w=norm1_w, sgu_ln_w=sgu_ln_w, sgu_ln_b=sgu_ln_b, sgu_w=sgu_w.astype(BF16),
             sgu_bt=jnp.swapaxes(sgu_b, 1, 2), hlb=hgrn_lower_bounds, hnw=hgrn_norm_w, norm2_w=norm2_w, conv_b=ffn_conv_b,
             final_norm_w=final_norm_w)
    xs = jnp.concatenate([ctx[0], x[0]], axis=0)
    cv = jnp.concatenate([c_ctx[None, :], c, jnp.zeros((14, D), F32)], axis=0)
    loss_local, dxs, G, sa = _local_step(xs, cv, loss_target[0], W, layer_weights, on_layer_grads, ctx_rows)
    loss = lax.psum(loss_local, ("x", "y", "c"))
    grad_x = dxs[ctx_rows:][None]

    pad8 = lambda a: jnp.pad(a, ((0, 8 - a.shape[0]), (0, 0)))
    fact = jnp.concatenate([pad8(sa[1:2].astype(F32))] + [pad8(G["dmod"][l][1].reshape(N_MOD, D)) for l in range(depth)]
                           + [pad8(G["dmod"][l][0].reshape(N_MOD, D)) for l in range(depth)], axis=0)
    facts = _gather_all(fact, "gather_mod_factors")
    lhs = jnp.concatenate([facts[:, 0].astype(BF16), jnp.broadcast_to(sa[0:1], (8, D))], axis=0)
    ada_cols = N_MOD * D // N_CHIPS
    g_ada = []
    for l in range(depth):
        lo_x, lo_c = 8 * (1 + l), 8 * (1 + depth + l)
        rhs = jnp.concatenate([facts[:, lo_x:lo_x + N_MOD].reshape(8, N_MOD * D),
                               facts[:, lo_c:lo_c + N_MOD].reshape(8, N_MOD * D)], axis=0)
        rhs = lax.dynamic_slice_in_dim(rhs, chip * ada_cols, ada_cols, axis=1).astype(BF16)
        g_ada.append(_mm_tn(lhs, rhs, F32, f"dw_ada_{l}"))

    dh = G["hlb1"][depth - 1]
    small_like = [w[k] for k in _SMALL] + [jnp.zeros((depth, 9, dff), F32)]
    small = [G["c_ctx"], jnp.stack(G["ada_b"]), jnp.stack(G["norm1_w"]), jnp.stack(G["sgu_ln_w"]), jnp.stack(G["sgu_ln_b"]),
             jnp.stack(G["sgu_w"]), jnp.stack(G["sgu_b"]), jnp.stack([-dh, dh]), jnp.stack(G["hnw"]), jnp.stack(G["norm2_w"]),
             jnp.stack(G["conv_b"]), G["final_norm_w"], jnp.stack(G["conv_w"])]
    n_small = sum(a.size for a in small)
    n_small_pad = _round_up(n_small, N_CORES * 16 * PACK_COLS)
    small_rows = n_small_pad // (N_CORES * PACK_COLS)
    small_rep = jnp.broadcast_to(_pack(small, n_small_pad).reshape(1, N_CORES, small_rows, PACK_COLS),
                                 (N_CHIPS, N_CORES, small_rows, PACK_COLS))
    small_parts = as_parts([small_rep])
    small_sums = [_sum_pair(small_parts[0], _reduce_pair(small_parts, "reduce_pair_small")[0], ids, "sum_pair_small")]
    groups["small"] = dict(l=None, keys=["small"], chips=_split_start(
        small_sums, [lax.empty(small_sums[0].shape, small_sums[0].dtype)], _plan_chips, _n_chips_copies(small_sums),
        "reduce_chips_start_small"))

    def gather_halves(halves, name):
        return dict(zip(halves, _gather_pair(list(halves.values()), name)))

    last = order[-1]
    halves = {}
    for tag in order[:-1]:
        halves.update(chips_finish(tag, groups["small"]["chips"]["token_array"]))
    reduced = gather_halves(halves, "gather_pair")
    grads, delta, new_m, new_v = {}, {}, {}, {}

    def adamw_sharded(i):
        k = _SHARDED[i]
        gs = g_ada if i == 0 else [reduced[(l, _LAYER_KEYS[i])].reshape(w[k].shape[1:]) for l in range(depth)]
        grads[k], delta[k], new_m[k], new_v[k] = _adamw(w[k], gs, mom[k], var[k], f"adamw_{k}")

    last_keys = groups[last]["keys"]
    for i in range(len(_SHARDED)):
        if _LAYER_KEYS[i] not in last_keys:
            adamw_sharded(i)
    halves = chips_finish(last, new_v[_SHARDED[-1]])
    halves.update(chips_finish("small", new_v[_SHARDED[-1]]))
    reduced.update(gather_halves(halves, "gather_pair_last"))
    for i in range(len(_SHARDED)):
        if _LAYER_KEYS[i] in last_keys:
            adamw_sharded(i)

    g_small = _unpack(reduced[(None, "small")].reshape(-1), small_like)
    grads.update(zip(_SMALL, g_small[:-1]))
    grads["ffn_conv_w"] = lax.dynamic_slice_in_dim(g_small[-1].reshape(depth, 3, 3, dff), chip * (dff // N_CHIPS),
                                                   dff // N_CHIPS, axis=3)
    packed = _SMALL + ("ffn_conv_w",)
    n_pad = _round_up(sum(w[k].size for k in packed), 16 * PACK_COLS)
    pack = lambda t: _pack([t[k] for k in packed], n_pad).reshape(1, -1, PACK_COLS)
    _, d, nm, nv = _adamw(pack(w), [pack(grads)[0]], pack(mom), pack(var), "adamw_packed")
    like = [w[k] for k in packed]
    for src, dst in ((d, delta), (nm, new_m), (nv, new_v)):
        dst.update(zip(packed, _unpack(src.reshape(-1), like)))

    return (loss, grad_x, *[grads[k] for k in _ORDER], *[delta[k] for k in _ORDER], *[new_m[k] for k in _ORDER],
            *[new_v[k] for k in _ORDER])
```

```python
import functools

import jax
import jax.numpy as jnp
from jax import lax
from jax.experimental import pallas as pl
from jax.experimental.pallas import tpu as pltpu

F32 = jnp.float32
BF16 = jnp.bfloat16

GRID_W = 64
HG_CHUNK = 64
SGU_CHUNK = 128
HEAD = 128
TB = 256
N_MOD = 6
RMS_EPS = 1e-6
LN_EPS = 1e-5
VMEM_LIMIT = 48 * 1024 * 1024
VMEM_LIMIT_PAIR = 58 * 1024 * 1024
VMEM_WHOLE_K = 50 * 1024 * 1024
N_CHIPS = 4
N_CORES = 2

ADAM_LR = 0.001
ADAM_B1 = 0.9
ADAM_B2 = 0.999
ADAM_EPS = 1e-08
ADAM_WD = 0.01
ADAM_STEP = 10

_GELU_C = 0.7978845608028654
_GELU_A = 0.044715


def _sigmoid(x):
    return 0.5 * jnp.tanh(0.5 * x) + 0.5


def _silu(x):
    return x * _sigmoid(x)


def _silu_both(x):
    s = _sigmoid(x)
    return x * s, s * (1.0 + x * (1.0 - s))


def _dsilu(x):
    return _silu_both(x)[1]


def _gelu_both(x):
    x2 = x * x
    t = jnp.tanh(_GELU_C * (x + _GELU_A * x2 * x))
    h = 0.5 * (1.0 + t)
    return x * h, h + 0.5 * x * (1.0 - t * t) * (_GELU_C + 3.0 * _GELU_C * _GELU_A * x2)


def _gelu(x):
    return 0.5 * x * (1.0 + jnp.tanh(_GELU_C * (x + _GELU_A * x * x * x)))


def _dot(a, b, ca, cb):
    return lax.dot_general(a, b, (((ca,), (cb,)), ((), ())), preferred_element_type=F32)


def _nn(a, b):
    return _dot(a, b, 1, 0)


def _nt(a, b):
    return _dot(a, b, 1, 1)


def _tn(a, b):
    return _dot(a, b, 0, 0)


def _params(*sem, vmem=VMEM_LIMIT):
    return pltpu.CompilerParams(dimension_semantics=sem if sem else None, vmem_limit_bytes=vmem)


def _stream_of(i, ctx_blocks):
    return (i >= ctx_blocks).astype(jnp.int32)


def _mm(a, b, mode, tm, tn, tk, out_dtype, name, b_chips=False, out_chips=False, vmem=VMEM_LIMIT):
    a_pair, b_pair = isinstance(a, tuple), isinstance(b, tuple)
    assert (not a_pair or mode == "nt") and (not b_pair or (mode == "tn" and not b_chips))
    ashape = (a[0].shape[0], 2 * a[0].shape[1]) if a_pair else a.shape
    if b_pair:
        bshape = (b[0].shape[0], 2 * b[0].shape[1])
    elif not b_chips:
        bshape = b.shape
    else:
        bshape = (b.shape[1], N_CHIPS * b.shape[2])
    if mode == "nn":
        (M, K), (K2, N) = ashape, bshape
    elif mode == "nt":
        (M, K), (N, K2) = ashape, bshape
    else:
        (K, M), (K2, N) = ashape, bshape
    assert K == K2 and M % tm == 0 and N % tn == 0 and K % tk == 0, (name, ashape, bshape, tm, tn, tk)
    nk = K // tk
    if a_pair:
        n1 = a[0].shape[1] // tk
        assert a[0].shape[1] % tk == 0
        a_specs = [pl.BlockSpec((tm, tk), lambda j, i, k: (i, jnp.minimum(k, n1 - 1))),
                   pl.BlockSpec((tm, tk), lambda j, i, k: (i, jnp.maximum(k - n1, 0)))]
    elif mode == "tn":
        a_specs = [pl.BlockSpec((tk, tm), lambda j, i, k: (k, i))]
    else:
        a_specs = [pl.BlockSpec((tm, tk), lambda j, i, k: (i, k))]
    if b_pair:
        n1 = b[0].shape[1] // tn
        assert b[0].shape[1] % tn == 0
        b_specs = [pl.BlockSpec((tk, tn), lambda j, i, k: (k, jnp.minimum(j, n1 - 1))),
                   pl.BlockSpec((tk, tn), lambda j, i, k: (k, jnp.maximum(j - n1, 0)))]
    elif not b_chips:
        if mode == "nt":
            b_spec = pl.BlockSpec((tn, tk), lambda j, i, k: (j, k))
        else:
            b_spec = pl.BlockSpec((tk, tn), lambda j, i, k: (k, j))
    else:
        cols = b.shape[2]
        if mode == "nn":
            per = cols // tn
            assert cols % tn == 0
            b_spec = pl.BlockSpec((None, tk, tn), lambda j, i, k: (j // per, k, j % per))
        else:
            per = cols // tk
            assert mode == "nt" and cols % tk == 0
            b_spec = pl.BlockSpec((None, tn, tk), lambda j, i, k: (k // per, j, k % per))
    if not b_pair:
        b_specs = [b_spec]
    if out_chips:
        per_o = (N // N_CHIPS) // tn
        assert (N // N_CHIPS) % tn == 0
        o_spec = pl.BlockSpec((None, tm, tn), lambda j, i, k: (j // per_o, i, j % per_o))
        o_shape = (N_CHIPS, M, N // N_CHIPS)
    else:
        o_spec = pl.BlockSpec((tm, tn), lambda j, i, k: (i, j))
        o_shape = (M, N)
    ca, cb = {"nn": (1, 0), "nt": (1, 1), "tn": (0, 0)}[mode]

    in_place = nk == 1
    na, nb = len(a_specs), len(b_specs)

    def body(*refs):
        a_refs, b_refs, rest = refs[:na], refs[na:na + nb], refs[na + nb:]
        if in_place:
            (o_ref,) = rest
        else:
            o_ref, acc = rest
        k = pl.program_id(2)

        if not in_place:
            @pl.when(k == 0)
            def _():
                acc[...] = jnp.zeros_like(acc)

        def multiply(which):
            part = _dot(a_refs[which if a_pair else 0][...], b_refs[which if b_pair else 0][...], ca, cb)
            if in_place:
                o_ref[...] = part.astype(out_dtype)
            else:
                acc[...] += part

        if a_pair or b_pair:
            first = (k < n1) if a_pair else (pl.program_id(0) < n1)
            pl.when(first)(functools.partial(multiply, 0))
            pl.when(jnp.logical_not(first))(functools.partial(multiply, 1))
        else:
            multiply(0)

        if not in_place:
            @pl.when(k == nk - 1)
            def _():
                o_ref[...] = acc[...].astype(out_dtype)

    ins = (list(a) if a_pair else [a]) + (list(b) if b_pair else [b])
    return pl.pallas_call(
        body, name=name, grid=(N // tn, M // tm, nk), in_specs=a_specs + b_specs, out_specs=o_spec,
        out_shape=jax.ShapeDtypeStruct(o_shape, out_dtype),
        scratch_shapes=[] if in_place else [pltpu.VMEM((tm, tn), F32)],
        compiler_params=_params("parallel", "parallel", "arbitrary", vmem=vmem),
    )(*ins)


def _tile(n, pref):
    if n <= pref:
        return n
    best = None
    for t in range(128, pref + 1, 128):
        if n % t == 0:
            best = t
    assert best is not None, (n, pref)
    return best


def _rows_tile(n, pref):
    if n <= pref:
        return n
    best = None
    for t in range(16, pref + 1, 16):
        if n % t == 0:
            best = t
    assert best is not None, (n, pref)
    return best


def _mm_nn_w(a, wg, out_dtype, name):
    M, K = a.shape
    return _mm(a, wg, "nn", _rows_tile(M, 2176), _tile(wg.shape[2], 1536), _tile(K, 1536), out_dtype, name, b_chips=True)


def _mm_nt_w(a, wg, out_dtype, name):
    M = a[0].shape[0] if isinstance(a, tuple) else a.shape[0]
    return _mm(a, wg, "nt", _rows_tile(M, 1088), _tile(wg.shape[1], 1024), _tile(wg.shape[2], 2304), out_dtype, name,
               b_chips=True)


def _mm_tn(a, b, out_dtype, name, out_chips=False):
    K, M = a.shape
    N = 2 * b[0].shape[1] if isinstance(b, tuple) else b.shape[1]
    ncol = N // N_CHIPS if out_chips else N
    tm, tn = _tile(M, 1408), _tile(ncol, 1408)
    if tm * tn > 1408 * 1152:
        tn = _tile(ncol, 1152)
    if isinstance(b, tuple):
        return _mm(a, b, "tn", tm, tn, _rows_tile(K, 2176), out_dtype, name, out_chips=out_chips, vmem=VMEM_LIMIT_PAIR)
    whole = 2 * (K * tm * a.dtype.itemsize + K * tn * b.dtype.itemsize + tm * tn * jnp.dtype(out_dtype).itemsize)
    if whole <= VMEM_WHOLE_K and (M // tm) * (N // tn) >= 4:
        return _mm(a, b, "tn", tm, tn, K, out_dtype, name, out_chips=out_chips, vmem=VMEM_LIMIT_PAIR)
    return _mm(a, b, "tn", tm, tn, _rows_tile(K, 2176), out_dtype, name, out_chips=out_chips)


def _mod_fwd(cv, wg, b, name):
    R, D = cv.shape
    tn = wg.shape[2]
    N = N_CHIPS * tn

    def body(cv_ref, w_ref, b_ref, mod_ref, sa_ref):
        sa = _silu(cv_ref[...]).astype(BF16)
        sa_ref[...] = sa
        mod_ref[...] = _nn(sa, w_ref[...]) + b_ref[...]

    return pl.pallas_call(
        body, name=name, grid=(N_CHIPS,),
        in_specs=[pl.BlockSpec((R, D), lambda j: (0, 0)), pl.BlockSpec((None, D, tn), lambda j: (j, 0, 0)),
                  pl.BlockSpec((1, tn), lambda j: (0, j))],
        out_specs=[pl.BlockSpec((R, tn), lambda j: (0, j)), pl.BlockSpec((R, D), lambda j: (0, 0))],
        out_shape=[jax.ShapeDtypeStruct((R, N), F32), jax.ShapeDtypeStruct((R, D), BF16)],
        compiler_params=_params("arbitrary"),
    )(cv, wg, b)


def _cvec_bwd(dmod, wg, cv, name):
    R, N = dmod.shape
    D = wg.shape[1]
    tk = wg.shape[2]
    nk = N_CHIPS

    def body(dm_ref, w_ref, cv_ref, o_ref):
        k = pl.program_id(0)

        @pl.when(k == 0)
        def _():
            o_ref[...] = jnp.zeros_like(o_ref)

        o_ref[...] += _nt(dm_ref[...].astype(BF16), w_ref[...])

        @pl.when(k == nk - 1)
        def _():
            o_ref[...] = o_ref[...] * _dsilu(cv_ref[...])

    return pl.pallas_call(
        body, name=name, grid=(nk,),
        in_specs=[pl.BlockSpec((R, tk), lambda k: (0, k)), pl.BlockSpec((None, D, tk), lambda k: (k, 0, 0)),
                  pl.BlockSpec((R, D), lambda k: (0, 0))],
        out_specs=pl.BlockSpec((R, D), lambda k: (0, 0)),
        out_shape=jax.ShapeDtypeStruct((R, D), F32),
        compiler_params=_params("arbitrary"),
    )(dmod, wg, cv)


def _norm_mod(x, nw, mod, which, ctx_rows, name):
    T, D = x.shape
    cb = ctx_rows // TB

    def body(x_ref, nw_ref, mod_ref, h_ref):
        xv = x_ref[...]
        r = lax.rsqrt(jnp.mean(xv * xv, axis=-1, keepdims=True) + RMS_EPS)
        y = xv * r * nw_ref[...]
        sh = mod_ref[which:which + 1, :]
        sc = mod_ref[which + 1:which + 2, :]
        h_ref[...] = (y * (1.0 + sc) + sh).astype(BF16)

    return pl.pallas_call(
        body, name=name, grid=(T // TB,),
        in_specs=[pl.BlockSpec((TB, D), lambda i: (i, 0)), pl.BlockSpec((1, D), lambda i: (0, 0)),
                  pl.BlockSpec((None, N_MOD, D), lambda i: (_stream_of(i, cb), 0, 0))],
        out_specs=pl.BlockSpec((TB, D), lambda i: (i, 0)),
        out_shape=jax.ShapeDtypeStruct((T, D), BF16),
        compiler_params=_params("parallel"),
    )(x, nw, mod)


def _norm_mod_bwd(dh, x, dres, nw, mod, which, ctx_rows, name):
    T, D = x.shape
    cb = ctx_rows // TB

    def body(dh_ref, x_ref, dres_ref, nw_ref, mod_ref, dx_ref, dm_ref, dnw_ref):
        i = pl.program_id(0)

        @pl.when(i == 0)
        def _():
            dnw_ref[...] = jnp.zeros_like(dnw_ref)

        @pl.when((i == 0) | (i == cb))
        def _():
            dm_ref[...] = jnp.zeros_like(dm_ref)

        xv = x_ref[...]
        dh = dh_ref[...]
        r = lax.rsqrt(jnp.mean(xv * xv, axis=-1, keepdims=True) + RMS_EPS)
        xh = xv * r
        nwv = nw_ref[...]
        sc = mod_ref[which + 1:which + 2, :]
        y = xh * nwv
        dm_ref[0:1, :] += jnp.sum(dh, axis=0, keepdims=True)
        dm_ref[1:2, :] += jnp.sum(dh * y, axis=0, keepdims=True)
        dy = dh * (1.0 + sc)
        dnw_ref[...] += jnp.sum(dy * xh, axis=0, keepdims=True)
        dxh = dy * nwv
        dx_ref[...] = dres_ref[...] + r * (dxh - xh * jnp.mean(dxh * xh, axis=-1, keepdims=True))

    return pl.pallas_call(
        body, name=name, grid=(T // TB,),
        in_specs=[pl.BlockSpec((TB, D), lambda i: (i, 0)), pl.BlockSpec((TB, D), lambda i: (i, 0)),
                  pl.BlockSpec((TB, D), lambda i: (i, 0)), pl.BlockSpec((1, D), lambda i: (0, 0)),
                  pl.BlockSpec((None, N_MOD, D), lambda i: (_stream_of(i, cb), 0, 0))],
        out_specs=[pl.BlockSpec((TB, D), lambda i: (i, 0)),
                   pl.BlockSpec((None, 2, D), lambda i: (_stream_of(i, cb), 0, 0)),
                   pl.BlockSpec((1, D), lambda i: (0, 0))],
        out_shape=[jax.ShapeDtypeStruct((T, D), F32), jax.ShapeDtypeStruct((2, 2, D), F32),
                   jax.ShapeDtypeStruct((1, D), F32)],
        compiler_params=_params("arbitrary"),
    )(dh, x, dres, nw, mod)


def _scan_chunk(n, rev, n_ctx, n_all):
    if not rev:
        return n
    return jnp.where(n < n_ctx, n_ctx - 1 - n, n_all - 1 + n_ctx - n)


def _cumsum_rows(x, rev):
    rows = x.shape[0]
    row = lax.broadcasted_iota(jnp.int32, (rows, 1), 0)
    s = 1
    while s < rows:
        if not rev:
            x = x + jnp.where(row >= s, pltpu.roll(x, s, 0), 0.0)
        else:
            x = x + jnp.where(row < rows - s, pltpu.roll(x, rows - s, 0), 0.0)
        s *= 2
    return x


def _lower_bound(hlb_ref, layer):
    h = hlb_ref[...]
    if layer == 0:
        return jnp.zeros_like(h[0:1, :])
    return _sigmoid(h[1:2, :] - h[0:1, :])


HG_STEP = 4
HG_HEADS = 4


def _step_rows(j, rev, backward):
    sub = j if rev == backward else HG_STEP - 1 - j
    return slice(sub * HG_CHUNK, (sub + 1) * HG_CHUNK)


def _hgrn_gates(q_ref, f_ref, hlb_ref, layer, rev, rows, cols=slice(None)):
    lb = _lower_bound(hlb_ref, layer)[:, cols]
    z = f_ref[rows, cols]
    sig = 1.0 / (1.0 + jnp.exp(-z))
    fg = lb + (1.0 - lb) * sig
    kk = (1.0 - lb) * (1.0 - sig)
    g = jnp.log(fg)
    b = _cumsum_rows(g, rev)
    bt = jnp.sum(g, axis=0, keepdims=True)
    mid = HG_CHUNK // 2
    r = b[mid:mid + 1, :] if rev else b[mid - 1:mid, :]
    qh = _silu(q_ref[rows, cols])
    return lb, sig, fg, kk, b, bt, r, qh


def _tri_mask(rev):
    t = lax.broadcasted_iota(jnp.int32, (HG_CHUNK, HG_CHUNK), 0)
    s = lax.broadcasted_iota(jnp.int32, (HG_CHUNK, HG_CHUNK), 1)
    return (s >= t) if rev else (s <= t)


def _hgrn_fwd_both(parts, hlb, layer, ctx_rows, name):
    T = parts.shape[0]
    D = hlb.shape[1] // 2
    nh = D // HEAD
    n_all, n_ctx = T // HG_CHUNK, ctx_rows // HG_CHUNK
    assert n_all % HG_STEP == 0 and n_ctx % HG_STEP == 0
    n_steps = n_all // HG_STEP
    block = lambda rev: functools.partial(_scan_chunk, rev=rev, n_ctx=n_ctx // HG_STEP, n_all=n_steps)

    def body(qf_ref, ff_ref, if_ref, hf_ref, qb_ref, fb_ref, ib_ref, hb_ref, of_ref, ob_ref, stf_ref, stb_ref, sf_scr, sb_scr):
        n = pl.program_id(0)

        @pl.when(n == 0)
        def _():
            sf_scr[...] = jnp.zeros_like(sf_scr)
            sb_scr[...] = jnp.zeros_like(sb_scr)

        hpg = min(nh, HG_HEADS)
        hs = [slice(h * HEAD, (h + 1) * HEAD) for h in range(hpg)]
        dirs = ((False, qf_ref, ff_ref, if_ref, hf_ref, of_ref, stf_ref, sf_scr),
                (True, qb_ref, fb_ref, ib_ref, hb_ref, ob_ref, stb_ref, sb_scr))
        for j, g0 in [(j, g0) for j in range(HG_STEP) for g0 in range(0, nh, hpg)]:
            cols = slice(g0 * HEAD, (g0 + hpg) * HEAD)
            for rev, q_ref, f_ref, i_ref, hlb_ref, o_ref, st_ref, s_scr in dirs:
                mask = _tri_mask(rev)
                rows = _step_rows(j, rev, False)
                lb, sig, fg, kk, b, bt, r, qh = _hgrn_gates(q_ref, f_ref, hlb_ref, layer, rev, rows, cols)
                qr = (qh * jnp.exp(b - r)).astype(BF16)
                kr = (kk * jnp.exp(r - b)).astype(BF16)
                qe = (qh * jnp.exp(b)).astype(BF16)
                ke = (kk * jnp.exp(bt - b)).astype(BF16)
                dec = jnp.exp(bt)
                v = i_ref[rows, cols].astype(BF16)
                st = [s_scr[g0 + h] for h in range(hpg)]
                a_raw = [_nt(qr[:, sl], kr[:, sl]) for sl in hs]
                o_int = [_nt(qe[:, sl], st[h].astype(BF16)) for h, sl in enumerate(hs)]
                kv = [_tn(v[:, sl], ke[:, sl]) for sl in hs]
                for h, sl in enumerate(hs):
                    out_cols = slice((g0 + h) * HEAD, (g0 + h + 1) * HEAD)
                    st_ref[j, g0 + h] = st[h]
                    o_ref[rows, out_cols] = _nn(jnp.where(mask, a_raw[h], 0.0).astype(BF16), v[:, sl]) + o_int[h]
                    s_scr[g0 + h] = st[h] * dec[:, sl] + kv[h]

    cspec = lambda rev, col: pl.BlockSpec((HG_STEP * HG_CHUNK, D), lambda n: (block(rev)(n), col))
    hspec = lambda rev: pl.BlockSpec((2, D), lambda n: (0, 1 if rev else 0))
    stspec = pl.BlockSpec((HG_STEP, nh, HEAD, HEAD), lambda n: (n, 0, 0, 0))
    o_shape = jax.ShapeDtypeStruct((T, D), F32)
    st_shape = jax.ShapeDtypeStruct((n_all, nh, HEAD, HEAD), F32)
    return pl.pallas_call(
        body, name=name, grid=(n_steps,),
        in_specs=[cspec(False, 0), cspec(False, 1), cspec(False, 3), hspec(False),
                  cspec(True, 0), cspec(True, 2), cspec(True, 3), hspec(True)],
        out_specs=[cspec(False, 0), cspec(True, 0), stspec, stspec],
        out_shape=[o_shape, o_shape, st_shape, st_shape],
        scratch_shapes=[pltpu.VMEM((nh, HEAD, HEAD), F32), pltpu.VMEM((nh, HEAD, HEAD), F32)],
        compiler_params=_params("arbitrary"),
    )(parts, parts, parts, hlb, parts, parts, parts, hlb)


def _hgrn_bwd(parts, hlb, do, states, layer, rev, ctx_rows, name, other=None, dparts=None):
    T = parts.shape[0]
    D = hlb.shape[1] // 2
    nh = D // HEAD
    n_all, n_ctx = T // HG_CHUNK, ctx_rows // HG_CHUNK
    assert n_all % HG_STEP == 0 and n_ctx % HG_STEP == 0
    n_steps = n_all // HG_STEP
    step = lambda m: n_steps - 1 - m
    block = lambda m: _scan_chunk(step(m), rev, n_ctx // HG_STEP, n_steps)
    fcol = 2 if rev else 1
    has_add = other is not None
    assert not has_add or rev

    def body(q_ref, f_ref, i_ref, hlb_ref, do_ref, st_ref, *rest):
        if has_add:
            dqa_ref, dza_ref, dia_ref, _, out_ref, dlb_ref, ds_scr = rest
            dq_ref, dz_ref, di_ref = out_ref.at[:, 0:D], out_ref.at[:, 2 * D:3 * D], out_ref.at[:, 3 * D:4 * D]
            out_ref[:, D:2 * D] = dza_ref[...]
        else:
            dq_ref, dz_ref, di_ref, dlb_ref, ds_scr = rest
        m = pl.program_id(0)

        @pl.when(m == 0)
        def _():
            ds_scr[...] = jnp.zeros_like(ds_scr)
            dlb_ref[...] = jnp.zeros_like(dlb_ref)

        mask = _tri_mask(rev)
        hpg = min(nh, HG_HEADS)
        hs = [slice(h * HEAD, (h + 1) * HEAD) for h in range(hpg)]
        for j, g0 in [(j, g0) for j in range(HG_STEP) for g0 in range(0, nh, hpg)]:
            rows = _step_rows(j, rev, True)
            cols = slice(g0 * HEAD, (g0 + hpg) * HEAD)
            slot = HG_STEP - 1 - j
            lb, sig, fg, kk, b, bt, r, qh = _hgrn_gates(q_ref, f_ref, hlb_ref, layer, rev, rows, cols)
            e_qr = jnp.exp(b - r)
            e_kr = jnp.exp(r - b)
            e_b = jnp.exp(b)
            e_ke = jnp.exp(bt - b)
            dec = jnp.exp(bt)
            qr = (qh * e_qr).astype(BF16)
            kr = (kk * e_kr).astype(BF16)
            qe = (qh * e_b).astype(BF16)
            ke = (kk * e_ke).astype(BF16)
            v = i_ref[rows, cols].astype(BF16)
            dov = do_ref[rows, cols].astype(BF16)
            st = [st_ref[slot, g0 + h] for h in range(hpg)]
            dst = [ds_scr[g0 + h] for h in range(hpg)]
            stb = [t.astype(BF16) for t in st]
            dstb = [t.astype(BF16) for t in dst]
            a_raw = [_nt(qr[:, sl], kr[:, sl]) for sl in hs]
            da_raw = [_nt(dov[:, sl], v[:, sl]) for sl in hs]
            dq_int = [_nn(dov[:, sl], stb[h]) for h, sl in enumerate(hs)]
            dk_int = [_nn(v[:, sl], dstb[h]) for h, sl in enumerate(hs)]
            dv_int = [_nt(ke[:, sl], dstb[h]) for h, sl in enumerate(hs)]
            ds_new = [_tn(dov[:, sl], qe[:, sl]) for sl in hs]
            a = [jnp.where(mask, t, 0.0).astype(BF16) for t in a_raw]
            da = [jnp.where(mask, t, 0.0).astype(BF16) for t in da_raw]
            dv_parts = [_tn(a[h], dov[:, sl]) + dv_int[h] for h, sl in enumerate(hs)]
            dq_parts = [_nn(da[h], kr[:, sl]) * e_qr[:, sl] + dq_int[h] * e_b[:, sl] for h, sl in enumerate(hs)]
            dki_parts = [dk_int[h] * e_ke[:, sl] for h, sl in enumerate(hs)]
            dk_parts = [_tn(da[h], qr[:, sl]) * e_kr[:, sl] + dki_parts[h] for h, sl in enumerate(hs)]
            dbt_parts = [dec[:, sl] * jnp.sum(st[h] * dst[h], axis=0, keepdims=True) for h, sl in enumerate(hs)]
            for h, sl in enumerate(hs):
                ds_scr[g0 + h] = dst[h] * dec[:, sl] + ds_new[h]
            dq = jnp.concatenate(dq_parts, axis=1)
            dk = jnp.concatenate(dk_parts, axis=1)
            dki = jnp.concatenate(dki_parts, axis=1)
            dv = jnp.concatenate(dv_parts, axis=1)
            dbt = jnp.concatenate(dbt_parts, axis=1) + jnp.sum(kk * dki, axis=0, keepdims=True)
            db = qh * dq - kk * dk
            dg = _cumsum_rows(db, not rev) + dbt
            df = dg / fg - dk
            dz_ref[rows, cols] = (df * (1.0 - lb) * sig * (1.0 - sig)).astype(BF16)
            dlb_ref[:, cols] += jnp.sum(df * (1.0 - sig), axis=0, keepdims=True)
            dqr = dq * _dsilu(q_ref[rows, cols])
            if has_add:
                dqr = dqr + dqa_ref[rows, cols]
                dv = dv + dia_ref[rows, cols]
            dq_ref[rows, cols] = dqr.astype(dq_ref.dtype)
            di_ref[rows, cols] = dv.astype(di_ref.dtype)

        @pl.when(m == n_steps - 1)
        def _():
            lb = _lower_bound(hlb_ref, layer)
            if layer == 0:
                dlb_ref[...] = jnp.zeros_like(dlb_ref)
            else:
                dlb_ref[...] = dlb_ref[...] * lb * (1.0 - lb)

    cspec = lambda col: pl.BlockSpec((HG_STEP * HG_CHUNK, D), lambda m: (block(m), col))
    ins = [parts, parts, parts, hlb, do, states]
    specs = [cspec(0), cspec(fcol), cspec(3), pl.BlockSpec((2, D), lambda m: (0, 1 if rev else 0)), cspec(0),
             pl.BlockSpec((HG_STEP, nh, HEAD, HEAD), lambda m: (step(m), 0, 0, 0))]
    dlb_spec = pl.BlockSpec((1, D), lambda m: (0, 0))
    dlb_shape = jax.ShapeDtypeStruct((1, D), F32)
    if has_add:
        return pl.pallas_call(
            body, name=name, grid=(n_steps,),
            in_specs=specs + [cspec(0), cspec(0), cspec(0), pl.BlockSpec(memory_space=pl.ANY)],
            out_specs=[pl.BlockSpec((HG_STEP * HG_CHUNK, 4 * D), lambda m: (block(m), 0)), dlb_spec],
            out_shape=[jax.ShapeDtypeStruct(dparts.shape, dparts.dtype), dlb_shape],
            scratch_shapes=[pltpu.VMEM((nh, HEAD, HEAD), F32)], input_output_aliases={len(ins) + 3: 0},
            compiler_params=_params("arbitrary"),
        )(*ins, *other, dparts)
    return pl.pallas_call(
        body, name=name, grid=(n_steps,), in_specs=specs,
        out_specs=[cspec(0), cspec(0), cspec(0), dlb_spec],
        out_shape=[jax.ShapeDtypeStruct((T, D), F32), jax.ShapeDtypeStruct((T, D), BF16),
                   jax.ShapeDtypeStruct((T, D), F32), dlb_shape],
        scratch_shapes=[pltpu.VMEM((nh, HEAD, HEAD), F32)],
        compiler_params=_params("arbitrary"),
    )(*ins)


def _sgu_ln(gv, lnw_ref, lnb_ref):
    mu = jnp.mean(gv, axis=-1, keepdims=True)
    xc = gv - mu
    rstd = lax.rsqrt(jnp.mean(xc * xc, axis=-1, keepdims=True) + LN_EPS)
    xh = xc * rstd
    return xh, rstd, xh * lnw_ref[...] + lnb_ref[...]


def _sgu_fwd(parts, lnw, lnb, w, bt, name):
    T = parts.shape[0]
    D = lnw.shape[1]
    G = D // HEAD

    def body(u_ref, v_ref, lnw_ref, lnb_ref, w_ref, bt_ref, ya_ref):
        gu = _gelu(u_ref[...])
        _, _, vn = _sgu_ln(_gelu(v_ref[...]), lnw_ref, lnb_ref)
        vnb = vn.astype(BF16)
        for g in range(G):
            sl = slice(g * HEAD, (g + 1) * HEAD)
            mixed = _nn(w_ref[g], vnb[:, sl]) + bt_ref[:, g:g + 1]
            ya_ref[:, sl] = (gu[:, sl] * mixed).astype(BF16)

    return pl.pallas_call(
        body, name=name, grid=(T // SGU_CHUNK,),
        in_specs=[pl.BlockSpec((SGU_CHUNK, D), lambda n: (n, 4)), pl.BlockSpec((SGU_CHUNK, D), lambda n: (n, 5)),
                  pl.BlockSpec((1, D), lambda n: (0, 0)), pl.BlockSpec((1, D), lambda n: (0, 0)),
                  pl.BlockSpec((G, SGU_CHUNK, SGU_CHUNK), lambda n: (0, 0, 0)),
                  pl.BlockSpec((SGU_CHUNK, G), lambda n: (0, 0))],
        out_specs=pl.BlockSpec((SGU_CHUNK, D), lambda n: (n, 0)),
        out_shape=jax.ShapeDtypeStruct((T, D), BF16),
        compiler_params=_params("parallel"),
    )(parts, parts, lnw, lnb, w, bt)


def _sgu_bwd(parts, dya, lnw, lnb, w, bt, dparts, name):
    T = parts.shape[0]
    D = lnw.shape[1]
    G = D // HEAD

    def body(u_ref, v_ref, dya_ref, lnw_ref, lnb_ref, w_ref, bt_ref, dparts_in,
             duv_ref, dw_ref, dbt_ref, dlnw_ref, dlnb_ref, dvn_scr):
        du_ref = duv_ref.at[:, 0:D]
        dv_ref = duv_ref.at[:, D:2 * D]
        n = pl.program_id(0)

        @pl.when(n == 0)
        def _():
            dw_ref[...] = jnp.zeros_like(dw_ref)
            dbt_ref[...] = jnp.zeros_like(dbt_ref)
            dlnw_ref[...] = jnp.zeros_like(dlnw_ref)
            dlnb_ref[...] = jnp.zeros_like(dlnb_ref)

        gu, dgu = _gelu_both(u_ref[...])
        gv, dgv_dv = _gelu_both(v_ref[...])
        xh, rstd, vn = _sgu_ln(gv, lnw_ref, lnb_ref)
        vnb = vn.astype(BF16)
        dya = dya_ref[...]
        lane = lax.broadcasted_iota(jnp.int32, (SGU_CHUNK, G), 1)
        dbt = jnp.zeros((SGU_CHUNK, G), F32)
        for g in range(G):
            sl = slice(g * HEAD, (g + 1) * HEAD)
            wg = w_ref[g]
            mixed = _nn(wg, vnb[:, sl]) + bt_ref[:, g:g + 1]
            dmix = dya[:, sl] * gu[:, sl]
            du_ref[:, sl] = (dya[:, sl] * mixed * dgu[:, sl]).astype(BF16)
            dmb = dmix.astype(BF16)
            dvn_scr[:, sl] = _tn(wg, dmb)
            dw_ref[g] += _nt(dmb, vnb[:, sl])
            dbt = dbt + jnp.where(lane == g, jnp.sum(dmix, axis=1, keepdims=True), 0.0)
        dbt_ref[...] += dbt
        dvn = dvn_scr[...]
        dlnw_ref[...] += jnp.sum(dvn * xh, axis=0, keepdims=True)
        dlnb_ref[...] += jnp.sum(dvn, axis=0, keepdims=True)
        dxh = dvn * lnw_ref[...]
        dgv = rstd * (dxh - jnp.mean(dxh, axis=-1, keepdims=True) - xh * jnp.mean(dxh * xh, axis=-1, keepdims=True))
        dv_ref[...] = (dgv * dgv_dv).astype(BF16)

    row = lambda col: pl.BlockSpec((SGU_CHUNK, D), lambda n: (n, col))
    vec = pl.BlockSpec((1, D), lambda n: (0, 0))
    wsp = pl.BlockSpec((G, SGU_CHUNK, SGU_CHUNK), lambda n: (0, 0, 0))
    bsp = pl.BlockSpec((SGU_CHUNK, G), lambda n: (0, 0))
    return pl.pallas_call(
        body, name=name, grid=(T // SGU_CHUNK,),
        in_specs=[row(4), row(5), row(0), vec, vec, wsp, bsp, pl.BlockSpec(memory_space=pl.ANY)],
        out_specs=[pl.BlockSpec((SGU_CHUNK, 2 * D), lambda n: (n, 2)), wsp, bsp, vec, vec],
        out_shape=[jax.ShapeDtypeStruct(dparts.shape, dparts.dtype),
                   jax.ShapeDtypeStruct((G, SGU_CHUNK, SGU_CHUNK), F32), jax.ShapeDtypeStruct((SGU_CHUNK, G), F32),
                   jax.ShapeDtypeStruct((1, D), F32), jax.ShapeDtypeStruct((1, D), F32)],
        scratch_shapes=[pltpu.VMEM((SGU_CHUNK, D), F32)], input_output_aliases={7: 0},
        compiler_params=_params("arbitrary"),
    )(parts, parts, dya, lnw, lnb, w, bt, dparts)


TBT = 256
VMEM_LIMIT_TOKEN_OUT = 58 * 1024 * 1024


def _rows_weight_spec(wg):
    return pl.BlockSpec(wg.shape, lambda i: (0, 0, 0))


def _full(w_ref):
    return w_ref[...].reshape(w_ref.shape[0] * w_ref.shape[1], w_ref.shape[2])


def _token_out_fwd(o, parts, ya, x, mod, hnw, nw2, wa, wb, wo, ctx_rows, name):
    T, D = x.shape
    nh = D // HEAD
    cb = ctx_rows // TBT

    def body(of_ref, ob_ref, og_ref, ga_ref, gb_ref, ya_ref, x_ref, mod_ref, hnw_ref, nw2_ref, wa_ref, wb_ref, wo_ref,
             yb_ref, pa_ref, pb_ref, mg_ref, tmo_ref, xm_ref, h2_ref):
        ov = of_ref[...] + ob_ref[...]
        so = _silu(og_ref[...])
        nw = hnw_ref[...]
        for h in range(nh):
            sl = slice(h * HEAD, (h + 1) * HEAD)
            seg = ov[:, sl]
            r = lax.rsqrt(jnp.mean(seg * seg, axis=-1, keepdims=True) + RMS_EPS)
            yb_ref[:, sl] = (seg * r * nw * so[:, sl]).astype(BF16)
        pa = _nn(ya_ref[...], _full(wa_ref))
        pb = _nn(yb_ref[...], _full(wb_ref))
        pa_ref[...] = pa.astype(BF16)
        pb_ref[...] = pb.astype(BF16)
        mg = (_sigmoid(ga_ref[...]) * pa + _sigmoid(gb_ref[...]) * pb).astype(BF16)
        mg_ref[...] = mg
        out = _nn(mg, _full(wo_ref))
        tmo_ref[...] = out.astype(BF16)
        xm = x_ref[...] + mod_ref[2:3, :] * out
        xm_ref[...] = xm
        r = lax.rsqrt(jnp.mean(xm * xm, axis=-1, keepdims=True) + RMS_EPS)
        h2_ref[...] = (xm * r * nw2_ref[...] * (1.0 + mod_ref[4:5, :]) + mod_ref[3:4, :]).astype(BF16)

    row = lambda col: pl.BlockSpec((TBT, D), lambda i: (i, col))
    wsp = _rows_weight_spec(wa)
    sd = lambda dt: jax.ShapeDtypeStruct((T, D), dt)
    return pl.pallas_call(
        body, name=name, grid=(T // TBT,),
        in_specs=[row(0), row(0), row(6), row(7), row(8), row(0), row(0),
                  pl.BlockSpec((None, N_MOD, D), lambda i: (_stream_of(i, cb), 0, 0)),
                  pl.BlockSpec((1, HEAD), lambda i: (0, 0)), pl.BlockSpec((1, D), lambda i: (0, 0)), wsp, wsp, wsp],
        out_specs=[row(0)] * 7,
        out_shape=[sd(BF16), sd(BF16), sd(BF16), sd(BF16), sd(BF16), sd(F32), sd(BF16)],
        compiler_params=_params("parallel", vmem=VMEM_LIMIT_TOKEN_OUT),
    )(o[0], o[1], parts, parts, parts, ya, x, mod, hnw, nw2, wa, wb, wo)


def _token_out_bwd(dx, tmo, pa, pb, o, parts, mod, hnw, wa, wb, wo, ctx_rows, name):
    T, D = dx.shape
    nh = D // HEAD
    cb = ctx_rows // TBT

    def body(dx_ref, tmo_ref, pa_ref, pb_ref, of_ref, ob_ref, og_ref, ga_ref, gb_ref, mod_ref, hnw_ref, wa_ref, wb_ref,
             wo_ref, dout_ref, dpa_ref, dpb_ref, dgate_ref, dya_ref, do_ref, dg1_ref, dhnw_ref):
        i = pl.program_id(0)

        @pl.when(i == 0)
        def _():
            dhnw_ref[...] = jnp.zeros_like(dhnw_ref)

        @pl.when((i == 0) | (i == cb))
        def _():
            dg1_ref[...] = jnp.zeros_like(dg1_ref)

        dxv = dx_ref[...]
        dg1_ref[...] += jnp.sum(dxv * tmo_ref[...], axis=0, keepdims=True)
        dout = (dxv * mod_ref[2:3, :]).astype(BF16)
        dout_ref[...] = dout
        dmg = _nt(dout, _full(wo_ref))
        sa = _sigmoid(ga_ref[...])
        sb = _sigmoid(gb_ref[...])
        dpa = (dmg * sa).astype(BF16)
        dpb = (dmg * sb).astype(BF16)
        dpa_ref[...] = dpa
        dpb_ref[...] = dpb
        dgate_ref[:, D:2 * D] = (dmg * pa_ref[...] * sa * (1.0 - sa)).astype(BF16)
        dgate_ref[:, 2 * D:3 * D] = (dmg * pb_ref[...] * sb * (1.0 - sb)).astype(BF16)
        dya_ref[...] = _nt(dpa, _full(wa_ref))
        dyb = _nt(dpb, _full(wb_ref))
        so, dso = _silu_both(og_ref[...])
        ov = of_ref[...] + ob_ref[...]
        nw = hnw_ref[...]
        dnw = jnp.zeros((1, HEAD), F32)
        for h in range(nh):
            sl = slice(h * HEAD, (h + 1) * HEAD)
            seg = ov[:, sl]
            r = lax.rsqrt(jnp.mean(seg * seg, axis=-1, keepdims=True) + RMS_EPS)
            oh = seg * r
            dn = dyb[:, sl] * so[:, sl]
            dgate_ref[:, sl] = (dyb[:, sl] * oh * nw * dso[:, sl]).astype(BF16)
            dnw = dnw + jnp.sum(dn * oh, axis=0, keepdims=True)
            doh = dn * nw
            do_ref[:, sl] = (r * (doh - oh * jnp.mean(doh * oh, axis=-1, keepdims=True))).astype(BF16)
        dhnw_ref[...] += dnw

    row = lambda col: pl.BlockSpec((TBT, D), lambda i: (i, col))
    wsp = _rows_weight_spec(wa)
    sd = lambda dt: jax.ShapeDtypeStruct((T, D), dt)
    return pl.pallas_call(
        body, name=name, grid=(T // TBT,),
        in_specs=[row(0), row(0), row(0), row(0), row(0), row(0), row(6), row(7), row(8),
                  pl.BlockSpec((None, N_MOD, D), lambda i: (_stream_of(i, cb), 0, 0)),
                  pl.BlockSpec((1, HEAD), lambda i: (0, 0)), wsp, wsp, wsp],
        out_specs=[row(0)] * 3 + [pl.BlockSpec((TBT, 3 * D), lambda i: (i, 2)), row(0), row(0),
                                  pl.BlockSpec((None, 1, D), lambda i: (_stream_of(i, cb), 0, 0)),
                                  pl.BlockSpec((1, HEAD), lambda i: (0, 0))],
        out_shape=[sd(BF16)] * 3 + [jax.ShapeDtypeStruct((T, 9 * D), BF16), sd(F32), sd(BF16),
                                    jax.ShapeDtypeStruct((2, 1, D), F32), jax.ShapeDtypeStruct((1, HEAD), F32)],
        compiler_params=_params("arbitrary", vmem=VMEM_LIMIT_TOKEN_OUT),
    )(dx, tmo, pa, pb, o[0], o[1], parts, parts, parts, mod, hnw, wa, wb, wo)


def _conv_geometry(i, nb, cb):
    is_ctx = i < cb
    first = (i == 0) | (i == cb)
    last = (i == cb - 1) | (i == nb - 1)
    row = lax.broadcasted_iota(jnp.int32, (TB + 2 * GRID_W, 1), 0)
    w = row & (GRID_W - 1)
    left_ok = (w != 0) | is_ctx
    right_ok = (w != GRID_W - 1) | is_ctx
    return is_ctx, first, last, left_ok, right_ok


def _ext(p_ref, m_ref, n_ref, first, last):
    return jnp.concatenate([jnp.where(first, 0.0, p_ref[...]), m_ref[...], jnp.where(last, 0.0, n_ref[...])], axis=0)


def _shift_prev(e, ok):
    return jnp.where(ok, pltpu.roll(e, 1, 0), 0.0)


def _shift_next(e, ok):
    return jnp.where(ok, pltpu.roll(e, e.shape[0] - 1, 0), 0.0)


def _halo_specs(cbk, n64, coff=0):
    r = TB // GRID_W
    prev = pl.BlockSpec((GRID_W, cbk), lambda j, i: (jnp.maximum(r * i - 1, 0), j + coff))
    main = pl.BlockSpec((TB, cbk), lambda j, i: (i, j + coff))
    nxt = pl.BlockSpec((GRID_W, cbk), lambda j, i: (jnp.minimum(r * i + r, n64 - 1), j + coff))
    return [prev, main, nxt]


def _conv_cblock(dff):
    return _tile(dff, 1408)


def _conv_fwd(up, cw, cbias, ctx_rows, name):
    T, dff = up.shape[0], up.shape[1] // 2
    cbk = _conv_cblock(dff)
    nb, cb = T // TB, ctx_rows // TB
    nvb = dff // cbk

    def body(ap_ref, a_ref, an_ref, v_ref, cw_ref, cb_ref, ac_ref, act_ref):
        i = pl.program_id(1)
        is_ctx, first, last, lok, rok = _conv_geometry(i, nb, cb)
        e = _ext(ap_ref, a_ref, an_ref, first, last)
        el = _shift_prev(e, lok)
        er = _shift_next(e, rok)
        cwv = cw_ref[...]

        def comb(dr, lo):
            sl = slice(lo, lo + TB)
            return cwv[3 * dr:3 * dr + 1] * el[sl] + cwv[3 * dr + 1:3 * dr + 2] * e[sl] + cwv[3 * dr + 2:3 * dr + 3] * er[sl]

        out = comb(1, GRID_W) + jnp.where(is_ctx, 0.0, comb(0, 0) + comb(2, 2 * GRID_W))
        a_c = out + cb_ref[...]
        ac_ref[...] = a_c
        act_ref[...] = (_gelu(a_c) * v_ref[...]).astype(BF16)

    main = pl.BlockSpec((TB, cbk), lambda j, i: (i, j))
    return pl.pallas_call(
        body, name=name, grid=(dff // cbk, nb),
        in_specs=_halo_specs(cbk, T // GRID_W) + [pl.BlockSpec((TB, cbk), lambda j, i: (i, j + nvb)),
                                                 pl.BlockSpec((9, cbk), lambda j, i: (0, j)),
                                                 pl.BlockSpec((1, cbk), lambda j, i: (0, j))],
        out_specs=[main, main],
        out_shape=[jax.ShapeDtypeStruct((T, dff), F32), jax.ShapeDtypeStruct((T, dff), BF16)],
        compiler_params=_params("parallel", "parallel"),
    )(up, up, up, up, cw, cbias)


def _conv_bwd(up, ac, dact, cw, ctx_rows, name):
    T, dff = up.shape[0], up.shape[1] // 2
    cbk = _conv_cblock(dff)
    nb, cb = T // TB, ctx_rows // TB
    nvb = dff // cbk

    def body(ap_ref, a_ref, an_ref, vp_ref, v_ref, vn_ref, cp_ref, c_ref, cn_ref, dp_ref, d_ref, dn_ref, cw_ref,
             da_ref, dv_ref, dcw_ref, dcb_ref):
        i = pl.program_id(1)

        @pl.when(i == 0)
        def _():
            dcw_ref[...] = jnp.zeros_like(dcw_ref)
            dcb_ref[...] = jnp.zeros_like(dcb_ref)

        is_ctx, first, last, lok, rok = _conv_geometry(i, nb, cb)
        gl, dgl = _gelu_both(_ext(cp_ref, c_ref, cn_ref, first, last))
        g = _ext(dp_ref, d_ref, dn_ref, first, last) * _ext(vp_ref, v_ref, vn_ref, first, last) * dgl
        dv_ref[...] = (d_ref[...] * gl[GRID_W:GRID_W + TB]).astype(BF16)
        gm = _shift_prev(g, lok)
        gp = _shift_next(g, rok)
        cwv = cw_ref[...]

        def comb(dr, lo):
            sl = slice(lo, lo + TB)
            return cwv[3 * dr:3 * dr + 1] * gp[sl] + cwv[3 * dr + 1:3 * dr + 2] * g[sl] + cwv[3 * dr + 2:3 * dr + 3] * gm[sl]

        da = comb(1, GRID_W) + jnp.where(is_ctx, 0.0, comb(0, 2 * GRID_W) + comb(2, 0))
        da_ref[...] = da.astype(BF16)
        e = _ext(ap_ref, a_ref, an_ref, first, last)
        taps = [_shift_prev(e, lok), e, _shift_next(e, rok)]
        gmain = g[GRID_W:GRID_W + TB]
        dcb_ref[...] += jnp.sum(gmain, axis=0, keepdims=True)
        vert = jnp.where(is_ctx, 0.0, 1.0)
        for dr in range(3):
            sl = slice(dr * GRID_W, dr * GRID_W + TB)
            for dw in range(3):
                s = jnp.sum(gmain * taps[dw][sl], axis=0, keepdims=True)
                if dr != 1:
                    s = s * vert
                k = 3 * dr + dw
                dcw_ref[k:k + 1, :] += s

    main = pl.BlockSpec((TB, cbk), lambda j, i: (i, j))
    halo = _halo_specs(cbk, T // GRID_W)
    acc9 = pl.BlockSpec((9, cbk), lambda j, i: (0, j))
    acc1 = pl.BlockSpec((1, cbk), lambda j, i: (0, j))
    return pl.pallas_call(
        body, name=name, grid=(dff // cbk, nb),
        in_specs=halo + _halo_specs(cbk, T // GRID_W, nvb) + halo + halo + [acc9],
        out_specs=[main, main, acc9, acc1],
        out_shape=[jax.ShapeDtypeStruct((T, dff), BF16), jax.ShapeDtypeStruct((T, dff), BF16),
                   jax.ShapeDtypeStruct((9, dff), F32), jax.ShapeDtypeStruct((1, dff), F32)],
        compiler_params=_params("parallel", "arbitrary"),
    )(up, up, up, up, up, up, ac, ac, ac, dact, dact, dact, cw)


def _ffn_out_fwd(act, xm, mod, wd, ctx_rows, name):
    T, D = xm.shape
    dff = act.shape[1]
    cb = ctx_rows // TB

    def body(act_ref, x_ref, mod_ref, w_ref, xo_ref, fo_ref):
        out = _nn(act_ref[...], _full(w_ref))
        fo_ref[...] = out.astype(BF16)
        xo_ref[...] = x_ref[...] + mod_ref[5:6, :] * out

    row = pl.BlockSpec((TB, D), lambda i: (i, 0))
    return pl.pallas_call(
        body, name=name, grid=(T // TB,),
        in_specs=[pl.BlockSpec((TB, dff), lambda i: (i, 0)), row,
                  pl.BlockSpec((None, N_MOD, D), lambda i: (_stream_of(i, cb), 0, 0)),
                  _rows_weight_spec(wd)],
        out_specs=[row, row],
        out_shape=[jax.ShapeDtypeStruct((T, D), F32), jax.ShapeDtypeStruct((T, D), BF16)],
        compiler_params=_params("parallel"),
    )(act, xm, mod, wd)


def _ffn_out_bwd(dx, fo, mod, wd, ctx_rows, name):
    T, D = dx.shape
    dff = N_CHIPS * wd.shape[1]
    cb = ctx_rows // TB

    def body(dx_ref, fo_ref, mod_ref, w_ref, dout_ref, dact_ref, dg2_ref):
        i = pl.program_id(0)

        @pl.when((i == 0) | (i == cb))
        def _():
            dg2_ref[...] = jnp.zeros_like(dg2_ref)

        dxv = dx_ref[...]
        dg2_ref[...] += jnp.sum(dxv * fo_ref[...], axis=0, keepdims=True)
        dout = (dxv * mod_ref[5:6, :]).astype(BF16)
        dout_ref[...] = dout
        dact_ref[...] = _nt(dout, _full(w_ref))

    row = pl.BlockSpec((TB, D), lambda i: (i, 0))
    return pl.pallas_call(
        body, name=name, grid=(T // TB,),
        in_specs=[row, row, pl.BlockSpec((None, N_MOD, D), lambda i: (_stream_of(i, cb), 0, 0)),
                  _rows_weight_spec(wd)],
        out_specs=[row, pl.BlockSpec((TB, dff), lambda i: (i, 0)),
                   pl.BlockSpec((None, 1, D), lambda i: (_stream_of(i, cb), 0, 0))],
        out_shape=[jax.ShapeDtypeStruct((T, D), BF16), jax.ShapeDtypeStruct((T, dff), F32),
                   jax.ShapeDtypeStruct((2, 1, D), F32)],
        compiler_params=_params("arbitrary"),
    )(dx, fo, mod, wd)


def _loss_bwd(x, target, fw, ctx_rows, name):
    T, D = x.shape
    cb = ctx_rows // TB

    def body(x_ref, t_ref, fw_ref, dx_ref, loss_ref, dfw_ref):
        i = pl.program_id(0)

        @pl.when(i == 0)
        def _():
            loss_ref[...] = jnp.zeros_like(loss_ref)
            dfw_ref[...] = jnp.zeros_like(dfw_ref)

        @pl.when(i < cb)
        def _():
            dx_ref[...] = jnp.zeros_like(dx_ref)

        @pl.when(i >= cb)
        def _():
            xv = x_ref[...]
            r = lax.rsqrt(jnp.mean(xv * xv, axis=-1, keepdims=True) + RMS_EPS)
            xh = xv * r
            fwv = fw_ref[...]
            err = xh * fwv - t_ref[...]
            loss_ref[...] += (0.5 / D) * jnp.sum(err * err)
            dy = err * (1.0 / D)
            dfw_ref[...] += jnp.sum(dy * xh, axis=0, keepdims=True)
            dxh = dy * fwv
            dx_ref[...] = r * (dxh - xh * jnp.mean(dxh * xh, axis=-1, keepdims=True))

    row = pl.BlockSpec((TB, D), lambda i: (i, 0))
    return pl.pallas_call(
        body, name=name, grid=(T // TB,),
        in_specs=[row, pl.BlockSpec((TB, D), lambda i: (jnp.maximum(i - cb, 0), 0)), pl.BlockSpec((1, D), lambda i: (0, 0))],
        out_specs=[row, pl.BlockSpec((1, 128), lambda i: (0, 0)), pl.BlockSpec((1, D), lambda i: (0, 0))],
        out_shape=[jax.ShapeDtypeStruct((T, D), F32), jax.ShapeDtypeStruct((1, 128), F32),
                   jax.ShapeDtypeStruct((1, D), F32)],
        compiler_params=_params("arbitrary"),
    )(x, target, fw)


def _adamw(w, gs, m, v, name):
    L, R, C = w.shape
    assert len(gs) == L
    rb = _rows_tile(R, max(16, (1 << 19) // C // 16 * 16))
    bc1 = 1.0 - ADAM_B1 ** ADAM_STEP
    bc2 = 1.0 - ADAM_B2 ** ADAM_STEP

    def body(w_ref, m_ref, v_ref, *rest):
        g_refs, (g_ref, d_ref, nm_ref, nv_ref) = rest[:L], rest[L:]
        layer = pl.program_id(0)
        for li in range(L):
            @pl.when(layer == li)
            def _():
                gv = g_refs[li][...]
                g_ref[...] = gv
                nm = ADAM_B1 * m_ref[...] + (1.0 - ADAM_B1) * gv
                nv = ADAM_B2 * v_ref[...] + (1.0 - ADAM_B2) * (gv * gv)
                nm_ref[...] = nm
                nv_ref[...] = nv
                d_ref[...] = -ADAM_LR * ((nm / bc1) / (jnp.sqrt(nv / bc2) + ADAM_EPS) + ADAM_WD * w_ref[...])

    blk = pl.BlockSpec((None, rb, C), lambda l, i: (l, i, 0))
    gblk = pl.BlockSpec((rb, C), lambda l, i: (i, 0))
    sd = jax.ShapeDtypeStruct((L, R, C), F32)
    return pl.pallas_call(
        body, name=name, grid=(L, R // rb), in_specs=[blk] * 3 + [gblk] * L, out_specs=[blk] * 4, out_shape=[sd] * 4,
        compiler_params=_params("parallel", "parallel"),
    )(w, m, v, *gs)


def _local_step(xs, cv, target, W, layer_weights, on_layer_grads, ctx_rows):
    T, D = xs.shape
    depth = W["norm1_w"].shape[0]
    saved = []
    X = xs
    for l in range(depth):
        s = {}
        Wl = layer_weights(l, X)
        mod_all, sa = _mod_fwd(cv, Wl["ada_w"], W["ada_b"][l][None, :] + Wl["token"], f"mod_fwd_{l}")
        mod = mod_all[:2].reshape(2, N_MOD, D)
        h1 = _norm_mod(X, W["norm1_w"][l][None, :], mod, 0, ctx_rows, f"norm1_{l}")
        parts = _mm_nn_w(h1, Wl["w_in"], F32, f"in_proj_{l}")
        o_f, o_b, st_f, st_b = _hgrn_fwd_both(parts, W["hlb"], l, ctx_rows, f"hgrn_fwd_{l}")
        o = (o_f, o_b)
        ya = _sgu_fwd(parts, W["sgu_ln_w"][l][None, :], W["sgu_ln_b"][l][None, :], W["sgu_w"][l], W["sgu_bt"][l],
                      f"sgu_fwd_{l}")
        Wl.update(Wl.pop("late")(ya))
        yb, pa, pb, mg, tmo, xm, h2 = _token_out_fwd(o, parts, ya, X, mod, W["hnw"][l][None, :] + Wl["late_token"],
                                                     W["norm2_w"][l][None, :], Wl["w_a"], Wl["w_b"], Wl["w_o"], ctx_rows,
                                                     f"token_out_fwd_{l}")
        up = _mm_nn_w(h2, Wl["w_up"], F32, f"up_proj_{l}")
        ac, act = _conv_fwd(up, Wl["conv_w"], W["conv_b"][l][None, :], ctx_rows, f"conv_fwd_{l}")
        xo, fo = _ffn_out_fwd(act, xm, mod, Wl["w_down"], ctx_rows, f"ffn_out_fwd_{l}")
        s.update(X=X, Wl=Wl, mod=mod, mod_all=mod_all, sa=sa, h1=h1, parts=parts, o=o, st_f=st_f, st_b=st_b, ya=ya, yb=yb,
                 pa=pa, pb=pb, mg=mg, tmo=tmo, xm=xm, h2=h2, up=up, ac=ac, act=act, fo=fo)
        saved.append(s)
        X = xo

    dX, loss_row, dfw = _loss_bwd(X, target, W["final_norm_w"][None, :], ctx_rows, "loss_bwd")
    G = {k: [None] * depth for k in ("ada_b", "norm1_w", "sgu_ln_w", "sgu_ln_b", "sgu_w", "sgu_b", "hlb1", "hnw", "norm2_w",
                                     "conv_w", "conv_b", "dmod")}
    dcv = jnp.zeros_like(cv)
    for l in reversed(range(depth)):
        s = saved[l]
        mod, Wl = s["mod"], s["Wl"]
        big = {}
        dout2, dact, dg2 = _ffn_out_bwd(dX, s["fo"], mod, Wl["w_down"], ctx_rows, f"ffn_out_bwd_{l}")
        big["w_down"] = _mm_tn(s["act"], dout2, F32, f"dw_down_{l}")
        da, dv, dcw, dcb = _conv_bwd(s["up"], s["ac"], dact, Wl["conv_w"], ctx_rows, f"conv_bwd_{l}")
        G["conv_w"][l], G["conv_b"][l] = dcw, dcb[0]
        big["w_up"] = _mm_tn(s["h2"], (da, dv), F32, f"dw_up_{l}", out_chips=True)
        dh2 = _mm_nt_w((da, dv), Wl["w_up"], F32, f"dh2_{l}")
        dxm, dm2, dnw2 = _norm_mod_bwd(dh2, s["xm"], dX, W["norm2_w"][l][None, :], mod, 3, ctx_rows, f"norm2_bwd_{l}")
        G["norm2_w"][l] = dnw2[0]
        (dout1, dpa, dpb, dparts, dya, do, dg1, dhnw) = _token_out_bwd(
            dxm, s["tmo"], s["pa"], s["pb"], s["o"], s["parts"], mod, W["hnw"][l][None, :], Wl["w_a"], Wl["w_b"], Wl["w_o"],
            ctx_rows, f"token_out_bwd_{l}")
        G["hnw"][l] = dhnw[0]
        big["w_o"] = _mm_tn(s["mg"], dout1, F32, f"dw_o_{l}")
        big["w_a"] = _mm_tn(s["ya"], dpa, F32, f"dw_a_{l}")
        big["w_b"] = _mm_tn(s["yb"], dpb, F32, f"dw_b_{l}")
        tok = on_layer_grads(l, "early", big)
        dparts, dsw, dsbt, dlnw, dlnb = _sgu_bwd(s["parts"], dya, W["sgu_ln_w"][l][None, :], W["sgu_ln_b"][l][None, :] + tok,
                                                 W["sgu_w"][l], W["sgu_bt"][l], dparts, f"sgu_bwd_{l}")
        G["sgu_w"][l], G["sgu_b"][l], G["sgu_ln_w"][l], G["sgu_ln_b"][l] = dsw, dsbt.T, dlnw[0], dlnb[0]
        dq_f, dz_f, di_f, dlb_f = _hgrn_bwd(s["parts"], W["hlb"], do, s["st_f"], l, False, ctx_rows, f"hgrn_bwd_f_{l}")
        dparts, dlb_b = _hgrn_bwd(s["parts"], W["hlb"], do, s["st_b"], l, True, ctx_rows, f"hgrn_bwd_b_{l}",
                                  other=(dq_f, dz_f, di_f), dparts=dparts)
        G["hlb1"][l] = jnp.concatenate([dlb_f[0], dlb_b[0]])
        tok = on_layer_grads(l, "late", {"w_in": _mm_tn(s["h1"], dparts, F32, f"dw_in_{l}", out_chips=True)})
        dh1 = _mm_nt_w(dparts, Wl["w_in"], F32, f"dh1_{l}")
        tok = tok + on_layer_grads(l, "end", {"after": dh1})
        dX, dm1, dnw1 = _norm_mod_bwd(dh1, s["X"], dxm, W["norm1_w"][l][None, :] + tok, mod, 0, ctx_rows, f"norm1_bwd_{l}")
        G["norm1_w"][l] = dnw1[0]
        dmod = jnp.concatenate([dm1, dg1, dm2, dg2], axis=1).reshape(2, N_MOD * D)
        dmod16 = jnp.concatenate([dmod, jnp.zeros((cv.shape[0] - 2, N_MOD * D), F32)], axis=0)
        G["ada_b"][l] = dmod[0] + dmod[1]
        G["dmod"][l] = dmod
        dcv = dcv + _cvec_bwd(dmod16, Wl["ada_w"], cv, f"dcvec_{l}")
    G["c_ctx"] = dcv[0]
    G["final_norm_w"] = dfw[0]
    return loss_row[0, 0], dX, G, saved[0]["sa"]


def _chip_peers(x, y, c):
    return [((1 - x, y, c), 2 * (1 - x) + y), ((x, 1 - y, c), 2 * x + 1 - y), ((1 - x, 1 - y, c), 2 * (1 - x) + 1 - y)]


def _rdma_call(ins, out_shapes, plan, n_remote, n_local, name, aliases=None):
    n_in, n_out = len(ins), len(out_shapes)

    def body(*refs):
        in_refs, out_refs = refs[:n_in], refs[n_in:n_in + n_out]
        send_sems, recv_sems, local_sems = refs[n_in + n_out:]
        x, y, c = lax.axis_index("x"), lax.axis_index("y"), lax.axis_index("c")
        remote, local = plan(in_refs, out_refs, x, y, c)
        assert len(remote) == n_remote and len(local) == n_local, (name, len(remote), len(local))
        copies = [pltpu.make_async_copy(s, d, local_sems.at[i]) for i, (s, d) in enumerate(local)]
        copies += [pltpu.make_async_remote_copy(src_ref=s, dst_ref=d, send_sem=send_sems.at[k], recv_sem=recv_sems.at[k],
                                                device_id=dev, device_id_type=pl.DeviceIdType.MESH)
                   for k, (s, d, dev) in enumerate(remote)]
        for cp in copies:
            cp.start()
        for cp in copies:
            cp.wait()

    hbm = pl.BlockSpec(memory_space=pltpu.HBM)
    return pl.pallas_call(
        body, name=name, in_specs=[hbm] * n_in, out_specs=[hbm] * n_out, out_shape=out_shapes,
        scratch_shapes=[pltpu.SemaphoreType.DMA((n_remote,)), pltpu.SemaphoreType.DMA((n_remote,)),
                        pltpu.SemaphoreType.DMA((max(n_local, 1),))],
        input_output_aliases=aliases or {},
    )(*ins)


DMA_PIECE_BYTES = 1 << 18
DMA_MAX_PIECES = 8


def _row_pieces(shape, dtype):
    rows = shape[0]
    row_bytes = jnp.dtype(dtype).itemsize
    for d in shape[1:]:
        row_bytes *= d
    n = 1
    while n < DMA_MAX_PIECES and rows % (2 * n * 16) == 0 and rows * row_bytes // (2 * n) >= DMA_PIECE_BYTES:
        n *= 2
    return [(i * (rows // n), rows // n) for i in range(n)]


def _half_pieces(o, c):
    r2 = o.shape[1] // 2
    return [pl.ds(c * r2 + st, sz) for st, sz in _row_pieces((r2,) + o.shape[2:], o.dtype)]


def _n_half_pieces(arrays):
    return sum(len(_row_pieces((a.shape[1] // 2,) + a.shape[2:], a.dtype)) for a in arrays)


def _plan_gather_far(lands, x, y, c):
    me = 2 * x + y
    return [(o.at[me, rows], o.at[me, rows], dev) for dev, _ in _chip_peers(x, y, c) for o in lands
            for rows in _half_pieces(o, c)]


def _plan_gather_near(lands, x, y, c):
    return [(o.at[idx, rows], o.at[idx, rows], (x, y, 1 - c)) for _, idx in _chip_peers(x, y, c) for o in lands
            for rows in _half_pieces(o, c)]


def _gather_weights(lands, name):
    n = len(lands)
    n_far = (N_CHIPS - 1) * _n_half_pieces(lands)

    def body(*refs):
        outs = refs[n:2 * n]
        far_send, far_recv, near_send, near_recv = refs[2 * n:]
        x, y, c = lax.axis_index("x"), lax.axis_index("y"), lax.axis_index("c")
        mk = lambda plan, send, recv: [
            pltpu.make_async_remote_copy(src_ref=s, dst_ref=d, send_sem=send.at[k], recv_sem=recv.at[k], device_id=dev,
                                         device_id_type=pl.DeviceIdType.MESH)
            for k, (s, d, dev) in enumerate(plan(outs, x, y, c))]
        far, near = mk(_plan_gather_far, far_send, far_recv), mk(_plan_gather_near, near_send, near_recv)
        assert len(far) == n_far and len(near) == n_far
        for cp in far:
            cp.start()
        for k in range(n_far):
            far[k].wait_recv()
            near[k].start()
        for k in range(n_far):
            near[k].wait_recv()
        for cp in far + near:
            cp.wait_send()

    hbm = pl.BlockSpec(memory_space=pltpu.HBM)
    sems = pltpu.SemaphoreType.DMA((n_far,))
    return pl.pallas_call(
        body, name=name, in_specs=[hbm] * n, out_specs=[hbm] * n,
        out_shape=[jax.ShapeDtypeStruct(a.shape, a.dtype) for a in lands],
        scratch_shapes=[sems, sems, sems, sems], input_output_aliases={i: i for i in range(n)},
    )(*lands)


def _gather_all(v, name):
    def plan(ins, outs, x, y, c):
        (s,), (o,) = ins, outs
        me = 4 * x + 2 * y + c
        flip = lambda a, f: 1 - a if f else a
        remote = [(s, o.at[me], (flip(x, m & 4), flip(y, m & 2), flip(c, m & 1))) for m in range(1, 8)]
        return remote, [(s, o.at[me])]

    return _rdma_call([v], [jax.ShapeDtypeStruct((8,) + v.shape, v.dtype)], plan, 7, 1, name)[0]


def _plan_pair(ins, lands, x, y, c):
    return [(a.at[j, 1 - c, pl.ds(st, sz)], o.at[j, pl.ds(st, sz)], (x, y, 1 - c)) for a, o in zip(ins, lands)
            for j in range(N_CHIPS) for st, sz in _row_pieces(a.shape[2:], a.dtype)]


def _n_pair_copies(parts):
    return N_CHIPS * sum(len(_row_pieces(a.shape[2:], a.dtype)) for a in parts)


def _reduce_pair(parts, name):
    shapes = [jax.ShapeDtypeStruct((N_CHIPS,) + a.shape[2:], a.dtype) for a in parts]
    return _rdma_call(parts, shapes, lambda ins, outs, x, y, c: (_plan_pair(ins, outs, x, y, c), []),
                      _n_pair_copies(parts), 0, name)


def _plan_chips(ins, lands, x, y, c):
    me = 2 * x + y
    return [(a.at[idx, pl.ds(st, sz)], o.at[me, pl.ds(st, sz)], dev) for dev, idx in _chip_peers(x, y, c)
            for a, o in zip(ins, lands) for st, sz in _row_pieces(a.shape[1:], a.dtype)]


def _n_chips_copies(parts):
    return (N_CHIPS - 1) * sum(len(_row_pieces(a.shape[1:], a.dtype)) for a in parts)


def _gather_pair(halves, name):
    def plan(ins, outs, x, y, c):
        return [(o.at[c, pl.ds(st, sz)], o.at[c, pl.ds(st, sz)], (x, y, 1 - c)) for o in outs
                for st, sz in _row_pieces(o.shape[1:], o.dtype)], []

    shapes = [jax.ShapeDtypeStruct(a.shape, a.dtype) for a in halves]
    n_remote = sum(len(_row_pieces(a.shape[1:], a.dtype)) for a in halves)
    return _rdma_call(halves, shapes, plan, n_remote, 0, name, aliases={i: i for i in range(len(halves))})


def _split_start(ins, lands, plan, n_remote, name):
    n_buf = len(ins) + len(lands)

    def body(*refs):
        in_refs, land_refs = refs[:len(ins)], refs[len(ins):n_buf]
        send_sems, recv_sems, token = refs[n_buf], refs[n_buf + 1], refs[-1]
        x, y, c = lax.axis_index("x"), lax.axis_index("y"), lax.axis_index("c")
        remote = plan(in_refs, land_refs, x, y, c)
        assert len(remote) == n_remote, (name, len(remote))
        for k, (s, d, dev) in enumerate(remote):
            pltpu.make_async_remote_copy(src_ref=s, dst_ref=d, send_sem=send_sems.at[k], recv_sem=recv_sems.at[k],
                                         device_id=dev, device_id_type=pl.DeviceIdType.MESH).start()
        token[...] = jnp.zeros_like(token)

    hbm = pl.BlockSpec(memory_space=pltpu.HBM)
    sem = pl.BlockSpec(memory_space=pltpu.SEMAPHORE)
    bufs = list(ins) + list(lands)
    out = pl.pallas_call(
        body, name=name, in_specs=[hbm] * n_buf,
        out_specs=(sem, sem) + (hbm,) * n_buf + (pl.BlockSpec(memory_space=pltpu.VMEM),),
        out_shape=(pltpu.SemaphoreType.DMA((n_remote,)), pltpu.SemaphoreType.DMA((n_remote,)))
        + tuple(pltpu.HBM(a.shape, a.dtype) for a in bufs) + (jax.ShapeDtypeStruct((8, 128), F32),),
        input_output_aliases={i: 2 + i for i in range(n_buf)},
        compiler_params=pltpu.CompilerParams(has_side_effects=pltpu.SideEffectType.DATAFLOW_SIDE_EFFECTING),
    )(*[pltpu.with_memory_space_constraint(a, pltpu.HBM) for a in bufs])
    return dict(send=out[0], recv=out[1], ins=list(out[2:2 + len(ins)]), lands=list(out[2 + len(ins):2 + n_buf]),
                token=out[-1][0, 0], token_array=out[-1], plan=plan, n_remote=n_remote)


def _split_wait(st, after, name):
    n_in, n_buf = len(st["ins"]), len(st["ins"]) + len(st["lands"])
    plan, n_remote = st["plan"], st["n_remote"]

    def body(*refs):
        in_refs, land_refs = refs[:n_in], refs[n_in:n_buf]
        send_sems, recv_sems = refs[n_buf], refs[n_buf + 1]
        x, y, c = lax.axis_index("x"), lax.axis_index("y"), lax.axis_index("c")
        for k, (s, d, dev) in enumerate(plan(in_refs, land_refs, x, y, c)):
            cp = pltpu.make_async_remote_copy(src_ref=s, dst_ref=d, send_sem=send_sems.at[k], recv_sem=recv_sems.at[k],
                                              device_id=dev, device_id_type=pl.DeviceIdType.MESH)
            cp.wait_send()
            cp.wait_recv()

    hbm = pl.BlockSpec(memory_space=pltpu.HBM)
    sem = pl.BlockSpec(memory_space=pltpu.SEMAPHORE)
    bufs = st["ins"] + st["lands"]
    out = pl.pallas_call(
        body, name=name, in_specs=[hbm] * n_buf + [sem, sem, pl.BlockSpec(memory_space=pl.ANY)],
        out_specs=[hbm] * n_buf, out_shape=[pltpu.HBM(a.shape, a.dtype) for a in bufs],
        input_output_aliases={i: i for i in range(n_buf)},
        compiler_params=pltpu.CompilerParams(has_side_effects=pltpu.SideEffectType.DATAFLOW_SIDE_EFFECTING),
    )(*bufs, st["send"], st["recv"], after)
    return list(out[:n_in]), list(out[n_in:])


def _pair_forward(lands, name):
    shapes = [jax.ShapeDtypeStruct(a.shape, a.dtype) for a in lands]
    return _rdma_call(lands, shapes, lambda ins, outs, x, y, c: (_plan_gather_near(outs, x, y, c), []),
                      (N_CHIPS - 1) * _n_half_pieces(lands), 0, name, aliases={i: i for i in range(len(lands))})


def _sum_block_rows(r, C):
    return _rows_tile(r, max(16, (1 << 19) // C // 16 * 16))


def _sum_pair(a, recv, cidx, name):
    nch, _, r, C = a.shape
    rb = _sum_block_rows(r, C)

    def body(c_ref, a_ref, r_ref, o_ref):
        o_ref[...] = (a_ref[...] + r_ref[...]).astype(BF16)

    blk = pl.BlockSpec((None, rb, C), lambda j, i, c: (j, i, 0))
    return pl.pallas_call(
        body, name=name,
        grid_spec=pltpu.PrefetchScalarGridSpec(
            num_scalar_prefetch=1, grid=(nch, r // rb),
            in_specs=[pl.BlockSpec((None, None, rb, C), lambda j, i, c: (j, c[0], i, 0)), blk], out_specs=blk),
        out_shape=jax.ShapeDtypeStruct((nch, r, C), BF16),
        compiler_params=_params("parallel", "parallel"),
    )(cidx, a, recv)


def _sum_chips(mine, recv, ids, name):
    nch, r, C = recv.shape
    rb = _sum_block_rows(r, C)

    def body(ids_ref, m_ref, *rest):
        r_refs, o_ref = rest[:nch], rest[nch]
        chip = ids_ref[1]
        own = m_ref[...].astype(F32)
        acc = jnp.where(chip == 0, own, r_refs[0][...].astype(F32))
        for q in range(1, nch):
            acc = acc + jnp.where(chip == q, own, r_refs[q][...].astype(F32))
        o_ref[...] = acc

    def slot(q):
        return pl.BlockSpec((None, rb, C), lambda i, ids: (jnp.where(ids[1] == q, (q + 1) % nch, q), i, 0))

    return pl.pallas_call(
        body, name=name,
        grid_spec=pltpu.PrefetchScalarGridSpec(
            num_scalar_prefetch=1, grid=(r // rb,),
            in_specs=[pl.BlockSpec((None, rb, C), lambda i, ids: (ids[1], i, 0))] + [slot(q) for q in range(nch)],
            out_specs=pl.BlockSpec((None, rb, C), lambda i, ids: (ids[0], i, 0))),
        out_shape=jax.ShapeDtypeStruct((N_CORES, r, C), F32),
        compiler_params=_params("parallel"),
    )(ids, mine, *([recv] * nch))


PACK_COLS = 1024
_SHARDED = ("ada_w", "w_in", "w_branch_a", "w_branch_b", "w_out", "ffn_w_up", "ffn_w_down")
_LAYER_KEYS = ("ada_w", "w_in", "w_a", "w_b", "w_o", "w_up", "w_down")
_SMALL = ("c_ctx", "ada_b", "norm1_w", "sgu_ln_w", "sgu_ln_b", "sgu_w", "sgu_b", "hgrn_lower_bounds", "hgrn_norm_w",
          "norm2_w", "ffn_conv_b", "final_norm_w")
_ORDER = ("c_ctx", "ada_w", "ada_b", "norm1_w", "w_in", "sgu_ln_w", "sgu_ln_b", "sgu_w", "sgu_b", "hgrn_lower_bounds",
          "hgrn_norm_w", "w_branch_a", "w_branch_b", "w_out", "norm2_w", "ffn_w_up", "ffn_conv_w", "ffn_conv_b",
          "ffn_w_down", "final_norm_w")


def _pad_to(v, n):
    return jnp.concatenate([v, jnp.zeros((n - v.shape[0],), v.dtype)]) if v.shape[0] < n else v


def _round_up(n, m):
    return (n + m - 1) // m * m


def _pack(arrays, n_pad):
    flat = jnp.concatenate([a.reshape(-1) for a in arrays])
    return _pad_to(flat, n_pad)


def _unpack(flat, like):
    out, off = [], 0
    for a in like:
        out.append(flat[off:off + a.size].reshape(a.shape))
        off += a.size
    return out


def kernel(x, c, ctx, c_ctx, ada_w, ada_b, norm1_w, w_in, sgu_ln_w, sgu_ln_b, sgu_w, sgu_b, hgrn_lower_bounds, hgrn_norm_w, w_branch_a, w_branch_b, w_out, norm2_w, ffn_w_up, ffn_conv_w, ffn_conv_b, ffn_w_down, final_norm_w, loss_target, m_c_ctx, m_ada_w, m_ada_b, m_norm1_w, m_w_in, m_sgu_ln_w, m_sgu_ln_b, m_sgu_w, m_sgu_b, m_hgrn_lower_bounds, m_hgrn_norm_w, m_w_branch_a, m_w_branch_b, m_w_out, m_norm2_w, m_ffn_w_up, m_ffn_conv_w, m_ffn_conv_b, m_ffn_w_down, m_final_norm_w, v_c_ctx, v_ada_w, v_ada_b, v_norm1_w, v_w_in, v_sgu_ln_w, v_sgu_ln_b, v_sgu_w, v_sgu_b, v_hgrn_lower_bounds, v_hgrn_norm_w, v_w_branch_a, v_w_branch_b, v_w_out, v_norm2_w, v_ffn_w_up, v_ffn_conv_w, v_ffn_conv_b, v_ffn_w_down, v_final_norm_w):
    w = dict(c_ctx=c_ctx, ada_w=ada_w, ada_b=ada_b, norm1_w=norm1_w, w_in=w_in, sgu_ln_w=sgu_ln_w, sgu_ln_b=sgu_ln_b,
             sgu_w=sgu_w, sgu_b=sgu_b, hgrn_lower_bounds=hgrn_lower_bounds, hgrn_norm_w=hgrn_norm_w, w_branch_a=w_branch_a,
             w_branch_b=w_branch_b, w_out=w_out, norm2_w=norm2_w, ffn_w_up=ffn_w_up, ffn_conv_w=ffn_conv_w,
             ffn_conv_b=ffn_conv_b, ffn_w_down=ffn_w_down, final_norm_w=final_norm_w)
    mom = dict(zip(_ORDER, (m_c_ctx, m_ada_w, m_ada_b, m_norm1_w, m_w_in, m_sgu_ln_w, m_sgu_ln_b, m_sgu_w, m_sgu_b,
                            m_hgrn_lower_bounds, m_hgrn_norm_w, m_w_branch_a, m_w_branch_b, m_w_out, m_norm2_w, m_ffn_w_up,
                            m_ffn_conv_w, m_ffn_conv_b, m_ffn_w_down, m_final_norm_w)))
    var = dict(zip(_ORDER, (v_c_ctx, v_ada_w, v_ada_b, v_norm1_w, v_w_in, v_sgu_ln_w, v_sgu_ln_b, v_sgu_w, v_sgu_b,
                            v_hgrn_lower_bounds, v_hgrn_norm_w, v_w_branch_a, v_w_branch_b, v_w_out, v_norm2_w, v_ffn_w_up,
                            v_ffn_conv_w, v_ffn_conv_b, v_ffn_w_down, v_final_norm_w)))
    depth, D = norm1_w.shape
    dff = ffn_conv_b.shape[1]
    ctx_rows = ctx.shape[1]

    assert depth == 2, "the lower-bound softmax is written for two layers"
    core = lax.axis_index("c")
    chip = 2 * lax.axis_index("x") + lax.axis_index("y")
    ids = jnp.stack([core, chip]).astype(jnp.int32)

    first, rest = _LAYER_KEYS[:2], _LAYER_KEYS[2:]
    shard = lambda l, k: w[_SHARDED[_LAYER_KEYS.index(k)]][l].astype(BF16)
    started, conv_full = {}, []

    def landing(s):
        return lax.dynamic_update_slice(lax.empty((N_CHIPS,) + s.shape, s.dtype), s[None], (chip,) + (0,) * s.ndim)

    def start_gather(l, keys, tag):
        lands = [landing(shard(l, k)) for k in keys]
        started[tag] = _split_start([], lands, lambda ins, lds, x, y, c: _plan_gather_far(lds, x, y, c),
                                    (N_CHIPS - 1) * _n_half_pieces(lands), f"gather_start_{tag}")
        return started[tag]["token"]

    def finish_gather(keys, tag, after):
        _, lands = _split_wait(started[tag], after, f"gather_wait_{tag}")
        return dict(zip(keys, _pair_forward(lands, f"gather_forward_{tag}")))

    def layer_weights(l, after):
        if l == 0:
            got = _gather_weights([landing(shard(0, k)) for k in first] + [landing(ffn_conv_w)], "gather_weights_first")
            conv_full.append(jnp.transpose(got[-1], (1, 2, 3, 0, 4)).reshape(depth, 9, dff))
            out = dict(zip(first, got), token=start_gather(0, rest, "rest_0"))
        else:
            out = dict(finish_gather(first, f"first_{l}", after), token=0.0)

        def late(after_late):
            more = finish_gather(rest, f"rest_{l}", after_late)
            more["late_token"] = 0.0
            if l + 1 < depth:
                more["late_token"] = start_gather(l + 1, first, f"first_{l + 1}") + start_gather(l + 1, rest, f"rest_{l + 1}")
            return more

        return dict(out, conv_w=conv_full[0][l], late=late)

    groups, order = {}, []

    def as_parts(gs):
        return [g.reshape(N_CHIPS, N_CORES, g.size // (N_CHIPS * N_CORES * g.shape[-1]), g.shape[-1]) for g in gs]

    def pair_start(tag, l, keys, gs):
        parts = as_parts(gs)
        lands = [lax.empty((N_CHIPS,) + p.shape[2:], p.dtype) for p in parts]
        groups[tag] = dict(l=l, keys=keys, pair=_split_start(parts, lands, _plan_pair, _n_pair_copies(parts),
                                                             f"reduce_pair_start_{tag}"))
        order.append(tag)
        return groups[tag]["pair"]["token"]

    def chips_start(tag, after):
        parts, other = _split_wait(groups[tag]["pair"], after, f"reduce_pair_wait_{tag}")
        sums = [_sum_pair(a, o, ids, f"sum_pair_{tag}_{i}") for i, (a, o) in enumerate(zip(parts, other))]
        lands = [lax.empty(s.shape, s.dtype) for s in sums]
        groups[tag]["chips"] = _split_start(sums, lands, _plan_chips, _n_chips_copies(sums), f"reduce_chips_start_{tag}")
        return groups[tag]["chips"]["token"]

    def chips_finish(tag, after):
        sums, recv = _split_wait(groups[tag]["chips"], after, f"reduce_chips_wait_{tag}")
        return {(groups[tag]["l"], k): _sum_chips(sums[i], recv[i], ids, f"sum_chips_{tag}_{i}")
                for i, k in enumerate(groups[tag]["keys"])}

    def on_layer_grads(l, stage, gs):
        if stage == "early":
            return pair_start(f"early_{l}", l, list(gs), list(gs.values()))
        if stage == "late":
            return pair_start(f"late_{l}", l, ["w_in"], [gs["w_in"]]) + chips_start(f"early_{l}", gs["w_in"])
        return chips_start(f"late_{l}", gs["after"])

    W = dict(ada_b=ada_b, norm1_w=norm1_w, sgu_ln_w=sgu_ln_w, sgu_ln_b=sgu_ln_b, sgu_w=sgu_w.astype(BF16),
             sgu_bt=jnp.swapaxes(sgu_b, 1, 2), hlb=hgrn_lower_bounds, hnw=hgrn_norm_w, norm2_w=norm2_w, conv_b=ffn_conv_b,
             final_norm_w=final_norm_w)
    xs = jnp.concatenate([ctx[0], x[0]], axis=0)
    cv = jnp.concatenate([c_ctx[None, :], c, jnp.zeros((14, D), F32)], axis=0)
    loss_local, dxs, G, sa = _local_step(xs, cv, loss_target[0], W, layer_weights, on_layer_grads, ctx_rows)
    loss = lax.psum(loss_local, ("x", "y", "c"))
    grad_x = dxs[ctx_rows:][None]

    pad8 = lambda a: jnp.pad(a, ((0, 8 - a.shape[0]), (0, 0)))
    fact = jnp.concatenate([pad8(sa[1:2].astype(F32))] + [pad8(G["dmod"][l][1].reshape(N_MOD, D)) for l in range(depth)]
                           + [pad8(G["dmod"][l][0].reshape(N_MOD, D)) for l in range(depth)], axis=0)
    facts = _gather_all(fact, "gather_mod_factors")
    lhs = jnp.concatenate([facts[:, 0].astype(BF16), jnp.broadcast_to(sa[0:1], (8, D))], axis=0)
    ada_cols = N_MOD * D // N_CHIPS
    g_ada = []
    for l in range(depth):
        lo_x, lo_c = 8 * (1 + l), 8 * (1 + depth + l)
        rhs = jnp.concatenate([facts[:, lo_x:lo_x + N_MOD].reshape(8, N_MOD * D),
                               facts[:, lo_c:lo_c + N_MOD].reshape(8, N_MOD * D)], axis=0)
        rhs = lax.dynamic_slice_in_dim(rhs, chip * ada_cols, ada_cols, axis=1).astype(BF16)
        g_ada.append(_mm_tn(lhs, rhs, F32, f"dw_ada_{l}"))

    dh = G["hlb1"][depth - 1]
    small_like = [w[k] for k in _SMALL] + [jnp.zeros((depth, 9, dff), F32)]
    small = [G["c_ctx"], jnp.stack(G["ada_b"]), jnp.stack(G["norm1_w"]), jnp.stack(G["sgu_ln_w"]), jnp.stack(G["sgu_ln_b"]),
             jnp.stack(G["sgu_w"]), jnp.stack(G["sgu_b"]), jnp.stack([-dh, dh]), jnp.stack(G["hnw"]), jnp.stack(G["norm2_w"]),
             jnp.stack(G["conv_b"]), G["final_norm_w"], jnp.stack(G["conv_w"])]
    n_small = sum(a.size for a in small)
    n_small_pad = _round_up(n_small, N_CORES * 16 * PACK_COLS)
    small_rows = n_small_pad // (N_CORES * PACK_COLS)
    small_rep = jnp.broadcast_to(_pack(small, n_small_pad).reshape(1, N_CORES, small_rows, PACK_COLS),
                                 (N_CHIPS, N_CORES, small_rows, PACK_COLS))
    small_parts = as_parts([small_rep])
    small_sums = [_sum_pair(small_parts[0], _reduce_pair(small_parts, "reduce_pair_small")[0], ids, "sum_pair_small")]
    groups["small"] = dict(l=None, keys=["small"], chips=_split_start(
        small_sums, [lax.empty(small_sums[0].shape, small_sums[0].dtype)], _plan_chips, _n_chips_copies(small_sums),
        "reduce_chips_start_small"))

    def gather_halves(halves, name):
        return dict(zip(halves, _gather_pair(list(halves.values()), name)))

    last = order[-1]
    halves = {}
    for tag in order[:-1]:
        halves.update(chips_finish(tag, groups["small"]["chips"]["token_array"]))
    reduced = gather_halves(halves, "gather_pair")
    grads, delta, new_m, new_v = {}, {}, {}, {}

    def adamw_sharded(i):
        k = _SHARDED[i]
        gs = g_ada if i == 0 else [reduced[(l, _LAYER_KEYS[i])].reshape(w[k].shape[1:]) for l in range(depth)]
        grads[k], delta[k], new_m[k], new_v[k] = _adamw(w[k], gs, mom[k], var[k], f"adamw_{k}")

    last_keys = groups[last]["keys"]
    for i in range(len(_SHARDED)):
        if _LAYER_KEYS[i] not in last_keys:
            adamw_sharded(i)
    halves = chips_finish(last, new_v[_SHARDED[-1]])
    halves.update(chips_finish("small", new_v[_SHARDED[-1]]))
    reduced.update(gather_halves(halves, "gather_pair_last"))
    for i in range(len(_SHARDED)):
        if _LAYER_KEYS[i] in last_keys:
            adamw_sharded(i)

    g_small = _unpack(reduced[(None, "small")].reshape(-1), small_like)
    grads.update(zip(_SMALL, g_small[:-1]))
    grads["ffn_conv_w"] = lax.dynamic_slice_in_dim(g_small[-1].reshape(depth, 3, 3, dff), chip * (dff // N_CHIPS),
                                                   dff // N_CHIPS, axis=3)
    packed = _SMALL + ("ffn_conv_w",)
    n_pad = _round_up(sum(w[k].size for k in packed), 16 * PACK_COLS)
    pack = lambda t: _pack([t[k] for k in packed], n_pad).reshape(1, -1, PACK_COLS)
    _, d, nm, nv = _adamw(pack(w), [pack(grads)[0]], pack(mom), pack(var), "adamw_packed")
    like = [w[k] for k in packed]
    for src, dst in ((d, delta), (nm, new_m), (nv, new_v)):
        dst.update(zip(packed, _unpack(src.reshape(-1), like)))

    return (loss, grad_x, *[grads[k] for k in _ORDER], *[delta[k] for k in _ORDER], *[new_m[k] for k in _ORDER],
            *[new_v[k] for k in _ORDER])
```
